```python
import math
import jax, jax.numpy as jnp
from jax import lax
import numpy as np

D_MODEL = 1024
BATCH = 8
SEQ = 8192
DEPTH = 2

N_META = 16
CHUNK = 64
PAD = CHUNK - N_META
N_MIXERS = 2
RMS_EPS = 1e-6
RET_HEADS = 4
RET_DK = 256
RET_DV = 512
RET_QK = RET_HEADS * RET_DK
RET_V = RET_HEADS * RET_DV
RET_IN = 2 * RET_QK + 2 * RET_V
ROPE_BASE = 10000.0
DN_HEADS = 8
DN_DK = 128
DN_DV = 256
DN_QK = DN_HEADS * DN_DK
DN_V = DN_HEADS * DN_DV
DN_CONV_CH = 2 * DN_QK + DN_V
DN_IN = DN_CONV_CH + DN_V + 2 * DN_HEADS
CONV_K = 4
FFN_HIDDEN = ((8 * D_MODEL + 3 * 256 - 1) // (3 * 256)) * 256
N_RET_LAYERS = (DEPTH + 1) // 2
N_DN_LAYERS = DEPTH // 2

kernel_name = "hybrid_retention_gated_deltanet_meta"


def rmsnorm(x, w):
    xf = x.astype(jnp.float32)
    y = xf * lax.rsqrt(jnp.mean(xf * xf, axis=-1, keepdims=True) + RMS_EPS)
    return (y * w.astype(jnp.float32)).astype(x.dtype)


def l2norm(x):
    return x * lax.rsqrt(jnp.sum(x * x, axis=-1, keepdims=True) + RMS_EPS)


def rope(t, pos):
    half = t.shape[-1] // 2
    inv_freq = ROPE_BASE ** (-jnp.arange(half, dtype=jnp.float32) / half)
    ang = pos[:, None] * inv_freq[None, :]
    cos = jnp.cos(ang)[None, :, None, :]
    sin = jnp.sin(ang)[None, :, None, :]
    t1, t2 = t[..., :half], t[..., half:]
    return jnp.concatenate([t1 * cos - t2 * sin, t1 * sin + t2 * cos], axis=-1)


def to_chunks(t):
    b, l, h, d = t.shape
    return t.reshape(b, l // CHUNK, CHUNK, h, d).transpose(1, 0, 3, 2, 4)


def from_chunks(t):
    n, b, h, c, d = t.shape
    return t.transpose(1, 0, 3, 2, 4).reshape(b, n * c, h, d)


def causal_conv(x, w):
    ch = x.shape[-1]
    return lax.conv_general_dilated(
        x, w.astype(x.dtype)[:, None, :], window_strides=(1,), padding=[(CONV_K - 1, 0)],
        dimension_numbers=("NWC", "WIO", "NWC"), feature_group_count=ch)


def gated_head_norm(o, norm_w, gate):
    o = o * lax.rsqrt(jnp.mean(o * o, axis=-1, keepdims=True) + RMS_EPS) * norm_w.astype(jnp.float32)
    return o * jax.nn.silu(gate.astype(jnp.float32))


def retention(h, w_in, gn_w, w_out, valid, pos):
    b, l, _ = h.shape
    proj = h @ w_in
    q, k, v, g = jnp.split(proj, [RET_QK, 2 * RET_QK, 2 * RET_QK + RET_V], axis=-1)
    q = rope(q.reshape(b, l, RET_HEADS, RET_DK).astype(jnp.float32), pos)
    k = rope(k.reshape(b, l, RET_HEADS, RET_DK).astype(jnp.float32), pos)
    k = k * (RET_DK ** -0.5) * valid[None, :, None, None]
    v = v.reshape(b, l, RET_HEADS, RET_DV).astype(jnp.float32)

    log_gamma = jnp.log1p(-jnp.exp2(-5.0 - jnp.arange(RET_HEADS, dtype=jnp.float32)))
    idx = jnp.arange(CHUNK, dtype=jnp.float32)
    rel = idx[:, None] - idx[None, :]
    dmask = jnp.where((rel >= 0)[None], jnp.exp(log_gamma[:, None, None] * jnp.maximum(rel, 0.0)), 0.0)
    xi = jnp.exp(log_gamma[:, None] * (idx[None, :] + 1.0))[:, :, None]
    zeta = jnp.exp(log_gamma[:, None] * (CHUNK - 1.0 - idx[None, :]))[:, :, None]
    gamma_c = jnp.exp(log_gamma * CHUNK)[:, None, None]

    def step(state, inp):
        qc, kc, vc = inp
        scores = jnp.einsum("bhid,bhjd->bhij", qc, kc) * dmask
        o = jnp.einsum("bhij,bhjv->bhiv", scores, vc) + jnp.einsum("bhid,bhdv->bhiv", qc * xi, state)
        state = gamma_c * state + jnp.einsum("bhjd,bhjv->bhdv", kc * zeta, vc)
        return state, o

    s0 = jnp.zeros((b, RET_HEADS, RET_DK, RET_DV), jnp.float32)
    _, o = lax.scan(step, s0, (to_chunks(q), to_chunks(k), to_chunks(v)))
    o = from_chunks(o)
    o = gated_head_norm(o, gn_w, g.reshape(b, l, RET_HEADS, RET_DV))
    return o.reshape(b, l, RET_V).astype(h.dtype) @ w_out


def gated_deltanet(h, w_in, conv_w, a_log, dt_bias, norm_w, w_out, valid):
    b, l, _ = h.shape
    proj = h @ w_in
    qkv, gate, beta_in, a_in = jnp.split(
        proj, [DN_CONV_CH, DN_CONV_CH + DN_V, DN_CONV_CH + DN_V + DN_HEADS], axis=-1)
    qkv = qkv * valid[None, :, None].astype(qkv.dtype)
    qkv = jax.nn.silu(causal_conv(qkv, conv_w))
    q, k, v = jnp.split(qkv, [DN_QK, 2 * DN_QK], axis=-1)
    q = l2norm(q.reshape(b, l, DN_HEADS, DN_DK).astype(jnp.float32)) * (DN_DK ** -0.5)
    k = l2norm(k.reshape(b, l, DN_HEADS, DN_DK).astype(jnp.float32))
    v = v.reshape(b, l, DN_HEADS, DN_DV).astype(jnp.float32)
    vmask = valid[None, :, None]
    beta = (jax.nn.sigmoid(beta_in.astype(jnp.float32)) * vmask)[..., None]
    g = (-jnp.exp(a_log.astype(jnp.float32))
         * jax.nn.softplus(a_in.astype(jnp.float32) + dt_bias.astype(jnp.float32)) * vmask)[..., None]

    incl = jnp.tril(jnp.ones((CHUNK, CHUNK), dtype=bool))
    strict = jnp.tril(jnp.ones((CHUNK, CHUNK), dtype=bool), -1)
    eye = jnp.eye(CHUNK, dtype=jnp.float32)

    def step(state, inp):
        qc, kc, vc, bc, gc = inp
        gam = jnp.cumsum(gc, axis=-2)
        diff = gam - jnp.swapaxes(gam, -1, -2)
        decay = jnp.exp(jnp.where(incl, diff, -jnp.inf))
        kk = jnp.einsum("bhid,bhjd->bhij", kc, kc)
        a_mat = jnp.where(strict, bc * kk * decay, 0.0)
        rhs = jnp.concatenate([vc * bc, kc * bc * jnp.exp(gam)], axis=-1)
        sol = lax.linalg.triangular_solve(eye + a_mat, rhs, left_side=True, lower=True,
                                          unit_diagonal=True)
        u, w = sol[..., :DN_DV], sol[..., DN_DV:]
        v_new = u - jnp.einsum("bhik,bhkv->bhiv", w, state)
        qk = jnp.einsum("bhid,bhjd->bhij", qc, kc) * decay
        o = (jnp.einsum("bhid,bhdv->bhiv", qc * jnp.exp(gam), state)
             + jnp.einsum("bhij,bhjv->bhiv", qk, v_new))
        g_last = gam[..., -1:, :]
        state = state * jnp.exp(g_last) + jnp.einsum(
            "bhjd,bhjv->bhdv", kc * jnp.exp(g_last - gam), v_new)
        return state, o

    s0 = jnp.zeros((b, DN_HEADS, DN_DK, DN_DV), jnp.float32)
    _, o = lax.scan(step, s0, (to_chunks(q), to_chunks(k), to_chunks(v),
                               to_chunks(beta), to_chunks(g)))
    o = from_chunks(o)
    o = gated_head_norm(o, norm_w, gate.reshape(b, l, DN_HEADS, DN_DV))
    return o.reshape(b, l, DN_V).astype(h.dtype) @ w_out


def swiglu(h, w_gate, w_up, w_down):
    return (jax.nn.silu(h @ w_gate) * (h @ w_up)) @ w_down


def _fwd_setup_inputs(seed: int = 0) -> dict:
    key = jax.random.key(seed)
    ks = jax.random.split(key, 20)
    f32 = jnp.float32
    nrm = lambda k, shape, fan_in: jax.random.normal(k, shape, f32) * (fan_in ** -0.5)
    gain = lambda k, shape: 1.0 + 0.02 * jax.random.normal(k, shape, f32)
    dt = jnp.exp(jax.random.uniform(ks[9], (N_DN_LAYERS, DN_HEADS), f32)
                 * (math.log(0.1) - math.log(0.001)) + math.log(0.001))
    return {
        "x": jax.random.normal(ks[0], (BATCH, SEQ, D_MODEL), f32),
        "meta_tokens": jax.random.normal(ks[1], (N_META, D_MODEL), f32),
        "mix_norm_w": gain(ks[2], (DEPTH, D_MODEL)),
        "ffn_norm_w": gain(ks[3], (DEPTH, D_MODEL)),
        "ret_w_in": nrm(ks[4], (N_RET_LAYERS, D_MODEL, RET_IN), D_MODEL),
        "ret_gn_w": gain(ks[5], (N_RET_LAYERS, RET_DV)),
        "ret_w_out": nrm(ks[6], (N_RET_LAYERS, RET_V, D_MODEL), RET_V),
        "dn_w_in": nrm(ks[7], (N_DN_LAYERS, D_MODEL, DN_IN), D_MODEL),
        "dn_conv_w": nrm(ks[8], (N_DN_LAYERS, CONV_K, DN_CONV_CH), CONV_K),
        "dn_a_log": jnp.log(jax.random.uniform(ks[10], (N_DN_LAYERS, DN_HEADS), f32, 1.0, 16.0)),
        "dn_dt_bias": dt + jnp.log(-jnp.expm1(-dt)),
        "dn_norm_w": gain(ks[11], (N_DN_LAYERS, DN_DV)),
        "dn_w_out": nrm(ks[12], (N_DN_LAYERS, DN_V, D_MODEL), DN_V),
        "ffn_w_gate": nrm(ks[13], (DEPTH, D_MODEL, FFN_HIDDEN), D_MODEL),
        "ffn_w_up": nrm(ks[14], (DEPTH, D_MODEL, FFN_HIDDEN), D_MODEL),
        "ffn_w_down": nrm(ks[15], (DEPTH, FFN_HIDDEN, D_MODEL), FFN_HIDDEN),
        "final_norm_w": gain(ks[16], (D_MODEL,)),
    }


def _fwd_reference(x, meta_tokens, mix_norm_w, ffn_norm_w, ret_w_in, ret_gn_w, ret_w_out,
              dn_w_in, dn_conv_w, dn_a_log, dn_dt_bias, dn_norm_w, dn_w_out,
              ffn_w_gate, ffn_w_up, ffn_w_down, final_norm_w):
    b = x.shape[0]
    h = jnp.concatenate([
        jnp.zeros((b, PAD, D_MODEL), x.dtype),
        jnp.broadcast_to(meta_tokens.astype(x.dtype)[None], (b, N_META, D_MODEL)),
        x], axis=1)
    l = h.shape[1]
    pos_i = jnp.arange(l) - PAD
    valid = (pos_i >= 0).astype(jnp.float32)
    pos = pos_i.astype(jnp.float32)
    for i in range(DEPTH):
        hn = rmsnorm(h, mix_norm_w[i])
        if i % N_MIXERS == 0:
            j = i // N_MIXERS
            mix = retention(hn, ret_w_in[j], ret_gn_w[j], ret_w_out[j], valid, pos)
        else:
            j = i // N_MIXERS
            mix = gated_deltanet(hn, dn_w_in[j], dn_conv_w[j], dn_a_log[j], dn_dt_bias[j],
                                 dn_norm_w[j], dn_w_out[j], valid)
        h = h + mix
        h = h + swiglu(rmsnorm(h, ffn_norm_w[i]), ffn_w_gate[i], ffn_w_up[i], ffn_w_down[i])
    return rmsnorm(h, final_norm_w)[:, CHUNK:, :]


import jax as _jax
import jax.numpy as _jnp

TWIN_FORMAT = 'train_step'
FWD_PARAMS = ['x', 'meta_tokens', 'mix_norm_w', 'ffn_norm_w', 'ret_w_in', 'ret_gn_w', 'ret_w_out', 'dn_w_in', 'dn_conv_w', 'dn_a_log', 'dn_dt_bias', 'dn_norm_w', 'dn_w_out', 'ffn_w_gate', 'ffn_w_up', 'ffn_w_down', 'final_norm_w']
TWIN_WEIGHTS = ['meta_tokens', 'mix_norm_w', 'ffn_norm_w', 'ret_w_in', 'ret_gn_w', 'ret_w_out', 'dn_w_in', 'dn_conv_w', 'dn_a_log', 'dn_dt_bias', 'dn_norm_w', 'dn_w_out', 'ffn_w_gate', 'ffn_w_up', 'ffn_w_down', 'final_norm_w']
TWIN_DIFF_INPUT = 'x'
TWIN_INPUTS = ['x', 'meta_tokens', 'mix_norm_w', 'ffn_norm_w', 'ret_w_in', 'ret_gn_w', 'ret_w_out', 'dn_w_in', 'dn_conv_w', 'dn_a_log', 'dn_dt_bias', 'dn_norm_w', 'dn_w_out', 'ffn_w_gate', 'ffn_w_up', 'ffn_w_down', 'final_norm_w', 'loss_target', 'm_meta_tokens', 'm_mix_norm_w', 'm_ffn_norm_w', 'm_ret_w_in', 'm_ret_gn_w', 'm_ret_w_out', 'm_dn_w_in', 'm_dn_conv_w', 'm_dn_a_log', 'm_dn_dt_bias', 'm_dn_norm_w', 'm_dn_w_out', 'm_ffn_w_gate', 'm_ffn_w_up', 'm_ffn_w_down', 'm_final_norm_w', 'v_meta_tokens', 'v_mix_norm_w', 'v_ffn_norm_w', 'v_ret_w_in', 'v_ret_gn_w', 'v_ret_w_out', 'v_dn_w_in', 'v_dn_conv_w', 'v_dn_a_log', 'v_dn_dt_bias', 'v_dn_norm_w', 'v_dn_w_out', 'v_ffn_w_gate', 'v_ffn_w_up', 'v_ffn_w_down', 'v_final_norm_w']
TWIN_OUTPUTS = ['loss', 'grad_x', 'grad_meta_tokens', 'grad_mix_norm_w', 'grad_ffn_norm_w', 'grad_ret_w_in', 'grad_ret_gn_w', 'grad_ret_w_out', 'grad_dn_w_in', 'grad_dn_conv_w', 'grad_dn_a_log', 'grad_dn_dt_bias', 'grad_dn_norm_w', 'grad_dn_w_out', 'grad_ffn_w_gate', 'grad_ffn_w_up', 'grad_ffn_w_down', 'grad_final_norm_w', 'delta_meta_tokens', 'delta_mix_norm_w', 'delta_ffn_norm_w', 'delta_ret_w_in', 'delta_ret_gn_w', 'delta_ret_w_out', 'delta_dn_w_in', 'delta_dn_conv_w', 'delta_dn_a_log', 'delta_dn_dt_bias', 'delta_dn_norm_w', 'delta_dn_w_out', 'delta_ffn_w_gate', 'delta_ffn_w_up', 'delta_ffn_w_down', 'delta_final_norm_w', 'new_m_meta_tokens', 'new_m_mix_norm_w', 'new_m_ffn_norm_w', 'new_m_ret_w_in', 'new_m_ret_gn_w', 'new_m_ret_w_out', 'new_m_dn_w_in', 'new_m_dn_conv_w', 'new_m_dn_a_log', 'new_m_dn_dt_bias', 'new_m_dn_norm_w', 'new_m_dn_w_out', 'new_m_ffn_w_gate', 'new_m_ffn_w_up', 'new_m_ffn_w_down', 'new_m_final_norm_w', 'new_v_meta_tokens', 'new_v_mix_norm_w', 'new_v_ffn_norm_w', 'new_v_ret_w_in', 'new_v_ret_gn_w', 'new_v_ret_w_out', 'new_v_dn_w_in', 'new_v_dn_conv_w', 'new_v_dn_a_log', 'new_v_dn_dt_bias', 'new_v_dn_norm_w', 'new_v_dn_w_out', 'new_v_ffn_w_gate', 'new_v_ffn_w_up', 'new_v_ffn_w_down', 'new_v_final_norm_w']
TWIN_LEAF_KINDS = {'loss': 'loss', 'grad_x': 'grad_x', 'grad_meta_tokens': 'grad_w', 'grad_mix_norm_w': 'grad_w', 'grad_ffn_norm_w': 'grad_w', 'grad_ret_w_in': 'grad_w', 'grad_ret_gn_w': 'grad_w', 'grad_ret_w_out': 'grad_w', 'grad_dn_w_in': 'grad_w', 'grad_dn_conv_w': 'grad_w', 'grad_dn_a_log': 'grad_w', 'grad_dn_dt_bias': 'grad_w', 'grad_dn_norm_w': 'grad_w', 'grad_dn_w_out': 'grad_w', 'grad_ffn_w_gate': 'grad_w', 'grad_ffn_w_up': 'grad_w', 'grad_ffn_w_down': 'grad_w', 'grad_final_norm_w': 'grad_w', 'delta_meta_tokens': 'delta_w', 'delta_mix_norm_w': 'delta_w', 'delta_ffn_norm_w': 'delta_w', 'delta_ret_w_in': 'delta_w', 'delta_ret_gn_w': 'delta_w', 'delta_ret_w_out': 'delta_w', 'delta_dn_w_in': 'delta_w', 'delta_dn_conv_w': 'delta_w', 'delta_dn_a_log': 'delta_w', 'delta_dn_dt_bias': 'delta_w', 'delta_dn_norm_w': 'delta_w', 'delta_dn_w_out': 'delta_w', 'delta_ffn_w_gate': 'delta_w', 'delta_ffn_w_up': 'delta_w', 'delta_ffn_w_down': 'delta_w', 'delta_final_norm_w': 'delta_w', 'new_m_meta_tokens': 'new_m', 'new_m_mix_norm_w': 'new_m', 'new_m_ffn_norm_w': 'new_m', 'new_m_ret_w_in': 'new_m', 'new_m_ret_gn_w': 'new_m', 'new_m_ret_w_out': 'new_m', 'new_m_dn_w_in': 'new_m', 'new_m_dn_conv_w': 'new_m', 'new_m_dn_a_log': 'new_m', 'new_m_dn_dt_bias': 'new_m', 'new_m_dn_norm_w': 'new_m', 'new_m_dn_w_out': 'new_m', 'new_m_ffn_w_gate': 'new_m', 'new_m_ffn_w_up': 'new_m', 'new_m_ffn_w_down': 'new_m', 'new_m_final_norm_w': 'new_m', 'new_v_meta_tokens': 'new_v', 'new_v_mix_norm_w': 'new_v', 'new_v_ffn_norm_w': 'new_v', 'new_v_ret_w_in': 'new_v', 'new_v_ret_gn_w': 'new_v', 'new_v_ret_w_out': 'new_v', 'new_v_dn_w_in': 'new_v', 'new_v_dn_conv_w': 'new_v', 'new_v_dn_a_log': 'new_v', 'new_v_dn_dt_bias': 'new_v', 'new_v_dn_norm_w': 'new_v', 'new_v_dn_w_out': 'new_v', 'new_v_ffn_w_gate': 'new_v', 'new_v_ffn_w_up': 'new_v', 'new_v_ffn_w_down': 'new_v', 'new_v_final_norm_w': 'new_v'}


def _forward(args):
    return _fwd_reference(*[args[k] for k in FWD_PARAMS])


def _output_shape():
    out = _jax.eval_shape(lambda: _forward(_fwd_setup_inputs(0)))
    return out.shape, out.dtype

N_MICROBATCH = 1
ADAM_LR = 0.001
ADAM_B1 = 0.9
ADAM_B2 = 0.999
ADAM_EPS = 1e-08
ADAM_WD = 0.01
ADAM_STEP = 10
PER_EXAMPLE_BATCH_AXIS = {'x': 0, 'loss_target': 0}
SHARED_INPUTS = []
_WEIGHT_DTYPES = {'meta_tokens': _jnp.float32, 'mix_norm_w': _jnp.float32, 'ffn_norm_w': _jnp.float32, 'ret_w_in': _jnp.float32, 'ret_gn_w': _jnp.float32, 'ret_w_out': _jnp.float32, 'dn_w_in': _jnp.float32, 'dn_conv_w': _jnp.float32, 'dn_a_log': _jnp.float32, 'dn_dt_bias': _jnp.float32, 'dn_norm_w': _jnp.float32, 'dn_w_out': _jnp.float32, 'ffn_w_gate': _jnp.float32, 'ffn_w_up': _jnp.float32, 'ffn_w_down': _jnp.float32, 'final_norm_w': _jnp.float32}
MOMENT_SCALE = {'meta_tokens': 1.963064e-02, 'mix_norm_w': 2.993739e-01, 'ffn_norm_w': 1.778010e-01, 'ret_w_in': 1.444430e-01, 'ret_gn_w': 2.578950e-01, 'ret_w_out': 1.757160e-01, 'dn_w_in': 7.598295e-02, 'dn_conv_w': 7.702500e-02, 'dn_a_log': 7.015925e-01, 'dn_dt_bias': 6.418210e-01, 'dn_norm_w': 2.155978e-01, 'dn_w_out': 1.095928e-01, 'ffn_w_gate': 7.542607e-02, 'ffn_w_up': 7.288109e-02, 'ffn_w_down': 1.208259e-01, 'final_norm_w': 6.407944e+01}


def _to_microbatches(a, axis):
    t = _jnp.moveaxis(a, axis, 0)
    t = t.reshape((N_MICROBATCH, t.shape[0] // N_MICROBATCH) + t.shape[1:])
    return _jnp.moveaxis(t, 1, axis + 1)


def setup_inputs(seed: int = 0) -> dict:
    inp = _fwd_setup_inputs(seed)
    key = _jax.random.fold_in(_jax.random.key(seed), 7919)
    shape, _ = _output_shape()
    out = dict(inp)
    out["loss_target"] = _jax.random.normal(_jax.random.fold_in(key, 0), shape, _jnp.float32)
    for i, name in enumerate(TWIN_WEIGHTS):
        w = inp[name].astype(_jnp.float32)
        if MOMENT_SCALE is None:
            s = _jnp.sqrt(_jnp.mean(_jnp.square(w)) + 1e-30)
        else:
            s = MOMENT_SCALE[name]
        km, kv = _jax.random.split(_jax.random.fold_in(key, i + 1))
        out[name] = w
        out["m_" + name] = s * _jax.random.normal(km, w.shape, _jnp.float32)
        out["v_" + name] = (s * s) * _jax.random.uniform(kv, w.shape, _jnp.float32, 0.5, 1.5)
    if N_MICROBATCH > 1:
        for name, axis in PER_EXAMPLE_BATCH_AXIS.items():
            out[name] = _to_microbatches(out[name], axis)
    return {'x': out['x'], 'meta_tokens': out['meta_tokens'], 'mix_norm_w': out['mix_norm_w'], 'ffn_norm_w': out['ffn_norm_w'], 'ret_w_in': out['ret_w_in'], 'ret_gn_w': out['ret_gn_w'], 'ret_w_out': out['ret_w_out'], 'dn_w_in': out['dn_w_in'], 'dn_conv_w': out['dn_conv_w'], 'dn_a_log': out['dn_a_log'], 'dn_dt_bias': out['dn_dt_bias'], 'dn_norm_w': out['dn_norm_w'], 'dn_w_out': out['dn_w_out'], 'ffn_w_gate': out['ffn_w_gate'], 'ffn_w_up': out['ffn_w_up'], 'ffn_w_down': out['ffn_w_down'], 'final_norm_w': out['final_norm_w'], 'loss_target': out['loss_target'], 'm_meta_tokens': out['m_meta_tokens'], 'm_mix_norm_w': out['m_mix_norm_w'], 'm_ffn_norm_w': out['m_ffn_norm_w'], 'm_ret_w_in': out['m_ret_w_in'], 'm_ret_gn_w': out['m_ret_gn_w'], 'm_ret_w_out': out['m_ret_w_out'], 'm_dn_w_in': out['m_dn_w_in'], 'm_dn_conv_w': out['m_dn_conv_w'], 'm_dn_a_log': out['m_dn_a_log'], 'm_dn_dt_bias': out['m_dn_dt_bias'], 'm_dn_norm_w': out['m_dn_norm_w'], 'm_dn_w_out': out['m_dn_w_out'], 'm_ffn_w_gate': out['m_ffn_w_gate'], 'm_ffn_w_up': out['m_ffn_w_up'], 'm_ffn_w_down': out['m_ffn_w_down'], 'm_final_norm_w': out['m_final_norm_w'], 'v_meta_tokens': out['v_meta_tokens'], 'v_mix_norm_w': out['v_mix_norm_w'], 'v_ffn_norm_w': out['v_ffn_norm_w'], 'v_ret_w_in': out['v_ret_w_in'], 'v_ret_gn_w': out['v_ret_gn_w'], 'v_ret_w_out': out['v_ret_w_out'], 'v_dn_w_in': out['v_dn_w_in'], 'v_dn_conv_w': out['v_dn_conv_w'], 'v_dn_a_log': out['v_dn_a_log'], 'v_dn_dt_bias': out['v_dn_dt_bias'], 'v_dn_norm_w': out['v_dn_norm_w'], 'v_dn_w_out': out['v_dn_w_out'], 'v_ffn_w_gate': out['v_ffn_w_gate'], 'v_ffn_w_up': out['v_ffn_w_up'], 'v_ffn_w_down': out['v_ffn_w_down'], 'v_final_norm_w': out['v_final_norm_w']}


def _loss(weights, diff, rest, loss_target):
    with _jax.named_scope("forward"):
        args = {**rest, TWIN_DIFF_INPUT: diff, **{k: w.astype(_WEIGHT_DTYPES[k]) for k, w in weights.items()}}
        y = _forward(args)
    with _jax.named_scope("loss_head"):
        err = _jnp.square(y.astype(_jnp.float32) - loss_target)
        return 0.5 * _jnp.sum(_jnp.mean(err, axis=-1)) if err.ndim else 0.5 * err


def _adamw(w, g, m, v):
    m = ADAM_B1 * m + (1.0 - ADAM_B1) * g
    v = ADAM_B2 * v + (1.0 - ADAM_B2) * _jnp.square(g)
    m_hat = m / (1.0 - ADAM_B1 ** ADAM_STEP)
    v_hat = v / (1.0 - ADAM_B2 ** ADAM_STEP)
    delta = -ADAM_LR * (m_hat / (_jnp.sqrt(v_hat) + ADAM_EPS) + ADAM_WD * w)
    return delta, m, v


def reference(x, meta_tokens, mix_norm_w, ffn_norm_w, ret_w_in, ret_gn_w, ret_w_out, dn_w_in, dn_conv_w, dn_a_log, dn_dt_bias, dn_norm_w, dn_w_out, ffn_w_gate, ffn_w_up, ffn_w_down, final_norm_w, loss_target, m_meta_tokens, m_mix_norm_w, m_ffn_norm_w, m_ret_w_in, m_ret_gn_w, m_ret_w_out, m_dn_w_in, m_dn_conv_w, m_dn_a_log, m_dn_dt_bias, m_dn_norm_w, m_dn_w_out, m_ffn_w_gate, m_ffn_w_up, m_ffn_w_down, m_final_norm_w, v_meta_tokens, v_mix_norm_w, v_ffn_norm_w, v_ret_w_in, v_ret_gn_w, v_ret_w_out, v_dn_w_in, v_dn_conv_w, v_dn_a_log, v_dn_dt_bias, v_dn_norm_w, v_dn_w_out, v_ffn_w_gate, v_ffn_w_up, v_ffn_w_down, v_final_norm_w):
    given = dict(x=x, meta_tokens=meta_tokens, mix_norm_w=mix_norm_w, ffn_norm_w=ffn_norm_w, ret_w_in=ret_w_in, ret_gn_w=ret_gn_w, ret_w_out=ret_w_out, dn_w_in=dn_w_in, dn_conv_w=dn_conv_w, dn_a_log=dn_a_log, dn_dt_bias=dn_dt_bias, dn_norm_w=dn_norm_w, dn_w_out=dn_w_out, ffn_w_gate=ffn_w_gate, ffn_w_up=ffn_w_up, ffn_w_down=ffn_w_down, final_norm_w=final_norm_w, loss_target=loss_target, m_meta_tokens=m_meta_tokens, m_mix_norm_w=m_mix_norm_w, m_ffn_norm_w=m_ffn_norm_w, m_ret_w_in=m_ret_w_in, m_ret_gn_w=m_ret_gn_w, m_ret_w_out=m_ret_w_out, m_dn_w_in=m_dn_w_in, m_dn_conv_w=m_dn_conv_w, m_dn_a_log=m_dn_a_log, m_dn_dt_bias=m_dn_dt_bias, m_dn_norm_w=m_dn_norm_w, m_dn_w_out=m_dn_w_out, m_ffn_w_gate=m_ffn_w_gate, m_ffn_w_up=m_ffn_w_up, m_ffn_w_down=m_ffn_w_down, m_final_norm_w=m_final_norm_w, v_meta_tokens=v_meta_tokens, v_mix_norm_w=v_mix_norm_w, v_ffn_norm_w=v_ffn_norm_w, v_ret_w_in=v_ret_w_in, v_ret_gn_w=v_ret_gn_w, v_ret_w_out=v_ret_w_out, v_dn_w_in=v_dn_w_in, v_dn_conv_w=v_dn_conv_w, v_dn_a_log=v_dn_a_log, v_dn_dt_bias=v_dn_dt_bias, v_dn_norm_w=v_dn_norm_w, v_dn_w_out=v_dn_w_out, v_ffn_w_gate=v_ffn_w_gate, v_ffn_w_up=v_ffn_w_up, v_ffn_w_down=v_ffn_w_down, v_final_norm_w=v_final_norm_w)
    weights = {n: given[n] for n in TWIN_WEIGHTS}
    shared = {n: given[n] for n in SHARED_INPUTS}
    per_example = {n: given[n] for n in ['x']}
    grad_fn = _jax.value_and_grad(_loss, argnums=(0, 1))

    def one_microbatch(ex, loss_target):
        ex = dict(ex)
        diff = ex.pop(TWIN_DIFF_INPUT)
        return grad_fn(weights, diff, {**shared, **ex}, loss_target)

    if N_MICROBATCH == 1:
        loss, (grad_w, grad_x) = one_microbatch(per_example, given["loss_target"])
    else:
        def body(carry, xs):
            loss_sum, grad_sum = carry
            l_k, (gw_k, gx_k) = one_microbatch(xs[0], xs[1])
            with _jax.named_scope("update"):
                return (loss_sum + l_k, _jax.tree.map(_jnp.add, grad_sum, gw_k)), gx_k

        init = (_jnp.zeros((), _jnp.float32), _jax.tree.map(_jnp.zeros_like, weights))
        (loss, grad_w), grad_x = _jax.lax.scan(body, init, (per_example, given["loss_target"]))
    with _jax.named_scope("update"):
        delta_w, new_m, new_v = {}, {}, {}
        for n in TWIN_WEIGHTS:
            delta_w[n], new_m[n], new_v[n] = _adamw(weights[n], grad_w[n], given["m_" + n], given["v_" + n])
    return (loss, grad_x, *[grad_w[n] for n in TWIN_WEIGHTS], *[delta_w[n] for n in TWIN_WEIGHTS],
            *[new_m[n] for n in TWIN_WEIGHTS], *[new_v[n] for n in TWIN_WEIGHTS])
```

```python
import functools
import math

import jax
import jax.numpy as jnp
from jax import lax
from jax.experimental import pallas as pl
from jax.experimental.pallas import tpu as pltpu

F32 = jnp.float32
BF16 = jnp.bfloat16
MXU_DTYPE = BF16

D_MODEL = 1024
N_META = 16
CHUNK = 64
PAD = CHUNK - N_META
RMS_EPS = 1e-6
RET_HEADS, RET_DK, RET_DV = 4, 256, 512
RET_QK, RET_V = RET_HEADS * RET_DK, RET_HEADS * RET_DV
RET_IN = 2 * RET_QK + 2 * RET_V
ROPE_BASE = 10000.0
DN_HEADS, DN_DK, DN_DV = 8, 128, 256
DN_QK, DN_V = DN_HEADS * DN_DK, DN_HEADS * DN_DV
DN_CONV_CH = 2 * DN_QK + DN_V
DN_IN = DN_CONV_CH + DN_V + 2 * DN_HEADS
LANES = 128
DN_IN_PAD = DN_CONV_CH + DN_V + LANES
CONV_K = 4
FFN_HIDDEN = 2816
ADAM_LR, ADAM_B1, ADAM_B2, ADAM_EPS, ADAM_WD, ADAM_STEP = 0.001, 0.9, 0.999, 1e-08, 0.01, 10

ROW_ALIGN = 256
VMEM_LIMIT = 56 * 1024 * 1024
MESH = pl.DeviceIdType.MESH
ANY = pl.BlockSpec(memory_space=pl.ANY)
VMEM_SPEC = pl.BlockSpec(memory_space=pltpu.VMEM)
_HI = lax.Precision.HIGHEST


def _params(*sem):
    return pltpu.CompilerParams(dimension_semantics=sem, vmem_limit_bytes=VMEM_LIMIT)


def _dg(a, b, ca, cb, hi):
    dims = (((ca,), (cb,)), ((), ()))
    if hi:
        return lax.dot_general(a.astype(F32), b.astype(F32), dims, precision=_HI,
                               preferred_element_type=F32)
    return lax.dot_general(a.astype(MXU_DTYPE), b.astype(MXU_DTYPE), dims,
                           preferred_element_type=F32)


def _nn(a, b, hi=False):
    return _dg(a, b, 1, 0, hi)


def _nt(a, b, hi=False):
    return _dg(a, b, 1, 1, hi)


def _tn(a, b, hi=False):
    return _dg(a, b, 0, 0, hi)


def _iota(shape, dim):
    return lax.broadcasted_iota(jnp.int32, shape, dim)


def _valid_rows(first_row, rows, seq):
    r = first_row + _iota((rows, 1), 0)
    return ((r >= PAD) & (r < CHUNK + seq)).astype(F32)


def _rope(t, cs, sn):
    half = t.shape[-1] // 2
    t1, t2 = t[:, :half], t[:, half:]
    return jnp.concatenate([t1 * cs - t2 * sn, t1 * sn + t2 * cs], axis=1)


def _rope_bwd(d, cs, sn):
    half = d.shape[-1] // 2
    d1, d2 = d[:, :half], d[:, half:]
    return jnp.concatenate([d1 * cs + d2 * sn, d2 * cs - d1 * sn], axis=1)


def _col(x, idx):
    oh = (_iota((1, x.shape[1]), 1) == idx).astype(F32)
    return jnp.sum(x * oh, axis=1, keepdims=True)


def _row(x, idx):
    oh = (_iota((x.shape[0], 1), 0) == idx).astype(F32)
    return jnp.sum(x * oh, axis=0, keepdims=True)


def _shift_down(x, halo8, k):
    xr = pltpu.roll(x, k, 0)
    hr = pltpu.roll(halo8, k, 0)
    first = jnp.where(_iota((8, 1), 0) < k, hr, xr[0:8])
    return jnp.concatenate([first, xr[8:]], axis=0)


def _shift_up(x, next8, j):
    rows = x.shape[0]
    xr = pltpu.roll(x, rows - j, 0)
    nr = pltpu.roll(next8, 8 - j, 0)
    last = jnp.where(_iota((8, 1), 0) >= 8 - j, nr, xr[rows - 8:])
    return jnp.concatenate([xr[:rows - 8], last], axis=0)


def _gated_norm(o, gate, w):
    r = lax.rsqrt(jnp.mean(o * o, axis=-1, keepdims=True) + RMS_EPS)
    return o * r * w * (gate * jax.nn.sigmoid(gate))


def _gated_norm_bwd(dy, o, gate, w):
    r = lax.rsqrt(jnp.mean(o * o, axis=-1, keepdims=True) + RMS_EPS)
    nrm = o * r
    sg = jax.nn.sigmoid(gate)
    sl = gate * sg
    dgate = dy * nrm * w * (sg * (1.0 + gate * (1.0 - sg)))
    dn = dy * w * sl
    dw = jnp.sum(dy * nrm * sl, axis=0, keepdims=True)
    do = r * (dn - nrm * jnp.mean(dn * nrm, axis=-1, keepdims=True))
    return do, dgate, dw


def _softplus(z):
    return jnp.maximum(z, 0.0) + jnp.log(1.0 + jnp.exp(-jnp.abs(z)))


def _row_tile(rows, cap=768):
    for t in (768, 512, 256, 128, 64, 32, 16, 8):
        if t <= cap and rows % t == 0:
            return t
    return rows


def _div_tile(n, cap, mult):
    best = None
    for t in range(mult, min(cap, n) + 1, mult):
        if n % t == 0:
            best = t
    return best or n


def _col_tile(cols, cap=1536):
    best = None
    for t in range(LANES, min(cap, cols) + 1, LANES):
        if cols % t == 0:
            best = t
    return best or cols


def _rms_fwd(h, w, name):
    rows, d = h.shape
    tm = _row_tile(rows)

    def body(h_ref, w_ref, o_ref):
        x = h_ref[...]
        r = lax.rsqrt(jnp.mean(x * x, axis=-1, keepdims=True) + RMS_EPS)
        o_ref[...] = (x * r * w_ref[...]).astype(o_ref.dtype)

    return pl.pallas_call(
        body, grid=(rows // tm,),
        in_specs=[pl.BlockSpec((tm, d), lambda i: (i, 0)), pl.BlockSpec((1, d), lambda i: (0, 0))],
        out_specs=pl.BlockSpec((tm, d), lambda i: (i, 0)),
        out_shape=jax.ShapeDtypeStruct((rows, d), BF16), name=name,
        compiler_params=_params("parallel"))(h, w.reshape(1, d))


def _rms_bwd(dy, h, w, resid, name):
    rows, d = h.shape
    tm = _row_tile(rows)

    def body(dy_ref, h_ref, w_ref, r_ref, dh_ref, dw_ref):
        i = pl.program_id(0)
        x = h_ref[...]
        r = lax.rsqrt(jnp.mean(x * x, axis=-1, keepdims=True) + RMS_EPS)
        xh = x * r
        dyv = dy_ref[...]
        dxh = dyv * w_ref[...]
        dh_ref[...] = r_ref[...] + r * (dxh - xh * jnp.mean(dxh * xh, axis=-1, keepdims=True))
        part = jnp.sum(dyv * xh, axis=0, keepdims=True)

        @pl.when(i == 0)
        def _():
            dw_ref[...] = part

        @pl.when(i > 0)
        def _():
            dw_ref[...] += part

    blk = pl.BlockSpec((tm, d), lambda i: (i, 0))
    vec = pl.BlockSpec((1, d), lambda i: (0, 0))
    return pl.pallas_call(
        body, grid=(rows // tm,), in_specs=[blk, blk, vec, blk], out_specs=[blk, vec],
        out_shape=[jax.ShapeDtypeStruct((rows, d), F32), jax.ShapeDtypeStruct((1, d), F32)],
        name=name, compiler_params=_params("arbitrary"))(dy, h, w.reshape(1, d), resid)


def _final_loss(h, w, tgt, seq, name):
    rows, d = h.shape
    tm = _row_tile(rows)

    def body(h_ref, w_ref, t_ref, dh_ref, dw_ref, loss_ref):
        i = pl.program_id(0)
        r_idx = i * tm + _iota((tm, 1), 0)
        m = ((r_idx >= CHUNK) & (r_idx < CHUNK + seq)).astype(F32)
        x = h_ref[...]
        wv = w_ref[...]
        r = lax.rsqrt(jnp.mean(x * x, axis=-1, keepdims=True) + RMS_EPS)
        xh = x * r
        err = (xh * wv - t_ref[...]) * m
        lpart = 0.5 * jnp.sum(jnp.mean(err * err, axis=-1, keepdims=True), axis=0, keepdims=True)
        dyv = err * (1.0 / d)
        dxh = dyv * wv
        dh_ref[...] = r * (dxh - xh * jnp.mean(dxh * xh, axis=-1, keepdims=True))
        part = jnp.sum(dyv * xh, axis=0, keepdims=True)

        @pl.when(i == 0)
        def _():
            dw_ref[...] = part
            loss_ref[...] = jnp.broadcast_to(lpart, loss_ref.shape)

        @pl.when(i > 0)
        def _():
            dw_ref[...] += part
            loss_ref[...] += jnp.broadcast_to(lpart, loss_ref.shape)

    blk = pl.BlockSpec((tm, d), lambda i: (i, 0))
    vec = pl.BlockSpec((1, d), lambda i: (0, 0))
    return pl.pallas_call(
        body, grid=(rows // tm,), in_specs=[blk, vec, blk],
        out_specs=[blk, vec, pl.BlockSpec((1, LANES), lambda i: (0, 0))],
        out_shape=[jax.ShapeDtypeStruct((rows, d), F32), jax.ShapeDtypeStruct((1, d), F32),
                   jax.ShapeDtypeStruct((1, LANES), F32)],
        name=name, compiler_params=_params("arbitrary"))(h, w.reshape(1, d), tgt)


def _mm(a, b, *, mode, name, out_dtype=F32, resid=None, row_cap=768, col_cap=1536):
    if mode == "tn":
        m, k = a.shape
        n = b.shape[1]
        tm, tn = _row_tile(m, row_cap), _col_tile(n, col_cap)

        def body_tn(a_ref, b_ref, o_ref):
            i = pl.program_id(1)
            part = _tn(a_ref[...], b_ref[...])

            @pl.when(i == 0)
            def _():
                o_ref[...] = part

            @pl.when(i > 0)
            def _():
                o_ref[...] += part

        return pl.pallas_call(
            body_tn, grid=(n // tn, m // tm),
            in_specs=[pl.BlockSpec((tm, k), lambda j, i: (i, 0)),
                      pl.BlockSpec((tm, tn), lambda j, i: (i, j))],
            out_specs=pl.BlockSpec((k, tn), lambda j, i: (0, j)),
            out_shape=jax.ShapeDtypeStruct((k, n), F32), name=name,
            compiler_params=_params("parallel", "arbitrary"))(a, b)

    m, ka = a.shape
    n = b.shape[1] if mode == "nn" else b.shape[0]
    tm, tn = _row_tile(m, row_cap), _col_tile(n, col_cap)
    has_resid = resid is not None

    def body(*refs):
        if has_resid:
            a_ref, b_ref, r_ref, o_ref = refs
        else:
            a_ref, b_ref, o_ref = refs
        acc = _nn(a_ref[...], b_ref[...]) if mode == "nn" else _nt(a_ref[...], b_ref[...])
        if has_resid:
            acc = acc + r_ref[...]
        o_ref[...] = acc.astype(o_ref.dtype)

    b_spec = (pl.BlockSpec((b.shape[0], tn), lambda j, i: (0, j)) if mode == "nn"
              else pl.BlockSpec((tn, b.shape[1]), lambda j, i: (j, 0)))
    o_spec = pl.BlockSpec((tm, tn), lambda j, i: (i, j))
    in_specs = [pl.BlockSpec((tm, ka), lambda j, i: (i, 0)), b_spec]
    args = [a, b]
    if has_resid:
        in_specs.append(o_spec)
        args.append(resid)
    return pl.pallas_call(
        body, grid=(n // tn, m // tm), in_specs=in_specs, out_specs=o_spec,
        out_shape=jax.ShapeDtypeStruct((m, n), out_dtype), name=name,
        compiler_params=_params("parallel", "parallel"))(*args)


def _ffn_up(hn, wg, wu, name):
    m, k = hn.shape
    n = wg.shape[1]
    tm, tn = _row_tile(m), _col_tile(n)

    def body(a_ref, wg_ref, wu_ref, g_ref, u_ref, act_ref):
        a = a_ref[...]
        g = _nn(a, wg_ref[...])
        u = _nn(a, wu_ref[...])
        g_ref[...] = g.astype(g_ref.dtype)
        u_ref[...] = u.astype(u_ref.dtype)
        act_ref[...] = (g * jax.nn.sigmoid(g) * u).astype(act_ref.dtype)

    w_spec = pl.BlockSpec((k, tn), lambda j, i: (0, j))
    o_spec = pl.BlockSpec((tm, tn), lambda j, i: (i, j))
    out = jax.ShapeDtypeStruct((m, n), BF16)
    return pl.pallas_call(
        body, grid=(n // tn, m // tm),
        in_specs=[pl.BlockSpec((tm, k), lambda j, i: (i, 0)), w_spec, w_spec],
        out_specs=[o_spec, o_spec, o_spec], out_shape=[out, out, out], name=name,
        compiler_params=_params("parallel", "parallel"))(hn, wg, wu)


def _ffn_down_bwd(dh, wd, g, u, name):
    m, k = dh.shape
    n = wd.shape[0]
    tm, tn = _row_tile(m), _col_tile(n)

    def body(dh_ref, wd_ref, g_ref, u_ref, dg_ref, du_ref):
        dact = _nt(dh_ref[...], wd_ref[...])
        gv = g_ref[...].astype(F32)
        uv = u_ref[...].astype(F32)
        sg = jax.nn.sigmoid(gv)
        dg_ref[...] = (dact * uv * (sg * (1.0 + gv * (1.0 - sg)))).astype(dg_ref.dtype)
        du_ref[...] = (dact * gv * sg).astype(du_ref.dtype)

    o_spec = pl.BlockSpec((tm, tn), lambda j, i: (i, j))
    out = jax.ShapeDtypeStruct((m, n), BF16)
    return pl.pallas_call(
        body, grid=(n // tn, m // tm),
        in_specs=[pl.BlockSpec((tm, k), lambda j, i: (i, 0)),
                  pl.BlockSpec((tn, k), lambda j, i: (j, 0)), o_spec, o_spec],
        out_specs=[o_spec, o_spec], out_shape=[out, out], name=name,
        compiler_params=_params("parallel", "parallel"))(dh, wd, g, u)


def _ret_consts():
    log_gamma = jnp.log1p(-jnp.exp2(-5.0 - jnp.arange(RET_HEADS, dtype=F32)))
    idx = jnp.arange(CHUNK, dtype=F32)
    rel = idx[:, None] - idx[None, :]
    dmask = jnp.where((rel >= 0)[None], jnp.exp(log_gamma[:, None, None] * jnp.maximum(rel, 0.0)), 0.0)
    xi = jnp.exp(log_gamma[:, None] * (idx[None, :] + 1.0))[:, :, None]
    zeta = jnp.exp(log_gamma[:, None] * (CHUNK - 1.0 - idx[None, :]))[:, :, None]
    gamma_c = jnp.exp(log_gamma * CHUNK)
    wide = (RET_HEADS, CHUNK, RET_DK)
    return dmask, jnp.broadcast_to(xi, wide), jnp.broadcast_to(zeta, wide), gamma_c


def _rope_tables(rows):
    half = RET_DK // 2
    inv_freq = ROPE_BASE ** (-jnp.arange(half, dtype=F32) / half)
    pos = (jnp.arange(rows) - PAD).astype(F32)
    ang = pos[:, None] * inv_freq[None, :]
    return jnp.cos(ang), jnp.sin(ang)


def _ret_specs(order):
    return [pl.BlockSpec((CHUNK, RET_QK), lambda n: (order(n), 0)),
            pl.BlockSpec((CHUNK, RET_QK), lambda n: (order(n), 1)),
            pl.BlockSpec((CHUNK, RET_V), lambda n: (order(n), 1)),
            pl.BlockSpec((CHUNK, RET_V), lambda n: (order(n), 2))]


def _ret_const_specs():
    return [pl.BlockSpec((RET_HEADS, CHUNK, CHUNK), lambda n: (0, 0, 0)),
            pl.BlockSpec((RET_HEADS, CHUNK, RET_DK), lambda n: (0, 0, 0)),
            pl.BlockSpec((RET_HEADS, CHUNK, RET_DK), lambda n: (0, 0, 0)),
            pl.BlockSpec((1, RET_DV), lambda n: (0, 0))]


def _ret_fwd(proj, cos, sin, consts, gn_w, seq):
    rows = proj.shape[0]
    nc = rows // CHUNK
    dmask, xi, zeta, gamma_c = consts

    def body(gam_ref, q_ref, k_ref, v_ref, g_ref, cos_ref, sin_ref, dm_ref, xi_ref, ze_ref, gn_ref,
             o_ref, y_ref, ss_ref, s_ref):
        n = pl.program_id(0)

        @pl.when(n == 0)
        def _():
            s_ref[...] = jnp.zeros_like(s_ref)

        cs, sn = cos_ref[...], sin_ref[...]
        kscale = _valid_rows(n * CHUNK, CHUNK, seq) * (RET_DK ** -0.5)
        gn = gn_ref[...]
        for h in range(RET_HEADS):
            qk_cols = slice(h * RET_DK, (h + 1) * RET_DK)
            v_cols = slice(h * RET_DV, (h + 1) * RET_DV)
            qr = _rope(q_ref[:, qk_cols], cs, sn)
            kr = _rope(k_ref[:, qk_cols], cs, sn) * kscale
            v = v_ref[:, v_cols]
            s = s_ref[h]
            ss_ref[0, h] = s.astype(ss_ref.dtype)
            scores = _nt(qr, kr) * dm_ref[h]
            o = _nn(scores, v) + _nn(qr * xi_ref[h], s)
            s_ref[h] = gam_ref[h] * s + _tn(kr * ze_ref[h], v)
            o_ref[:, v_cols] = o
            y_ref[:, v_cols] = _gated_norm(o, g_ref[:, v_cols], gn).astype(y_ref.dtype)

    fwd = lambda n: n
    row128 = pl.BlockSpec((CHUNK, RET_DK // 2), lambda n: (n, 0))
    row_v = pl.BlockSpec((CHUNK, RET_V), lambda n: (n, 0))
    return pl.pallas_call(
        body, grid=(nc,),
        in_specs=[pl.BlockSpec(memory_space=pltpu.SMEM)] + _ret_specs(fwd) + [row128, row128]
        + _ret_const_specs(),
        out_specs=[row_v, row_v,
                   pl.BlockSpec((1, RET_HEADS, RET_DK, RET_DV), lambda n: (n, 0, 0, 0))],
        out_shape=[jax.ShapeDtypeStruct((rows, RET_V), F32), jax.ShapeDtypeStruct((rows, RET_V), BF16),
                   jax.ShapeDtypeStruct((nc, RET_HEADS, RET_DK, RET_DV), BF16)],
        scratch_shapes=[pltpu.VMEM((RET_HEADS, RET_DK, RET_DV), F32)],
        name="ret_fwd", compiler_params=_params("arbitrary"))(
            gamma_c, proj, proj, proj, proj, cos, sin, dmask, xi, zeta, gn_w.reshape(1, RET_DV))


def _ret_bwd(proj, o, dy, states, cos, sin, consts, gn_w, seq):
    rows = proj.shape[0]
    nc = rows // CHUNK
    dmask, xi, zeta, gamma_c = consts

    def body(gam_ref, q_ref, k_ref, v_ref, g_ref, o_ref, dy_ref, ss_ref, cos_ref, sin_ref,
             dm_ref, xi_ref, ze_ref, gn_ref, dp_ref, dgn_ref, ds_ref):
        n = pl.program_id(0)

        @pl.when(n == 0)
        def _():
            ds_ref[...] = jnp.zeros_like(ds_ref)
            dgn_ref[...] = jnp.zeros_like(dgn_ref)

        cs, sn = cos_ref[...], sin_ref[...]
        kscale = _valid_rows((nc - 1 - n) * CHUNK, CHUNK, seq) * (RET_DK ** -0.5)
        gn = gn_ref[...]
        dgn = jnp.zeros((1, RET_DV), F32)
        for h in range(RET_HEADS):
            qk_cols = slice(h * RET_DK, (h + 1) * RET_DK)
            v_cols = slice(h * RET_DV, (h + 1) * RET_DV)
            qr = _rope(q_ref[:, qk_cols], cs, sn)
            kr = _rope(k_ref[:, qk_cols], cs, sn) * kscale
            v = v_ref[:, v_cols]
            s = ss_ref[0, h]
            do, dgate, dw = _gated_norm_bwd(dy_ref[:, v_cols], o_ref[:, v_cols], g_ref[:, v_cols], gn)
            dgn = dgn + dw
            ds = ds_ref[h]
            dm = dm_ref[h]
            scores = _nt(qr, kr) * dm
            dscores = _nt(do, v) * dm
            dv = _tn(scores, do) + _nn(kr * ze_ref[h], ds)
            dqr = _nn(dscores, kr) + _nt(do, s) * xi_ref[h]
            dkr = _tn(dscores, qr) + _nt(v, ds) * ze_ref[h]
            ds_ref[h] = gam_ref[h] * ds + _tn(qr * xi_ref[h], do)
            dp_ref[:, qk_cols] = _rope_bwd(dqr, cs, sn).astype(dp_ref.dtype)
            dp_ref[:, RET_QK + h * RET_DK:RET_QK + (h + 1) * RET_DK] = (
                _rope_bwd(dkr * kscale, cs, sn).astype(dp_ref.dtype))
            dp_ref[:, 2 * RET_QK + h * RET_DV:2 * RET_QK + (h + 1) * RET_DV] = dv.astype(dp_ref.dtype)
            dp_ref[:, 2 * RET_QK + RET_V + h * RET_DV:2 * RET_QK + RET_V + (h + 1) * RET_DV] = (
                dgate.astype(dp_ref.dtype))
        dgn_ref[...] += dgn

    rev = lambda n: nc - 1 - n
    row128 = pl.BlockSpec((CHUNK, RET_DK // 2), lambda n: (rev(n), 0))
    row_v = pl.BlockSpec((CHUNK, RET_V), lambda n: (rev(n), 0))
    return pl.pallas_call(
        body, grid=(nc,),
        in_specs=[pl.BlockSpec(memory_space=pltpu.SMEM)] + _ret_specs(rev) + [
            row_v, row_v, pl.BlockSpec((1, RET_HEADS, RET_DK, RET_DV), lambda n: (rev(n), 0, 0, 0)),
            row128, row128] + _ret_const_specs(),
        out_specs=[pl.BlockSpec((CHUNK, RET_IN), lambda n: (rev(n), 0)),
                   pl.BlockSpec((1, RET_DV), lambda n: (0, 0))],
        out_shape=[jax.ShapeDtypeStruct((rows, RET_IN), BF16), jax.ShapeDtypeStruct((1, RET_DV), F32)],
        scratch_shapes=[pltpu.VMEM((RET_HEADS, RET_DK, RET_DV), F32)],
        name="ret_bwd", compiler_params=_params("arbitrary"))(
            gamma_c, proj, proj, proj, proj, o, dy, states, cos, sin, dmask, xi, zeta,
            gn_w.reshape(1, RET_DV))


GATE_COL = DN_CONV_CH // DN_V
BA_COL = (DN_CONV_CH + DN_V) // LANES
BETA_LANE, DECAY_LANE = 0, DN_HEADS


def _dn_in_specs(order):
    return [pl.BlockSpec((CHUNK, DN_CONV_CH), lambda n: (order(n), 0)),
            pl.BlockSpec((8, DN_CONV_CH), lambda n: (jnp.maximum(order(n) * (CHUNK // 8) - 1, 0), 0)),
            pl.BlockSpec((CHUNK, DN_V), lambda n: (order(n), GATE_COL)),
            pl.BlockSpec((CHUNK, LANES), lambda n: (order(n), BA_COL)),
            pl.BlockSpec((CONV_K, 1, DN_CONV_CH), lambda n: (0, 0, 0)),
            pl.BlockSpec((1, LANES), lambda n: (0, 0)),
            pl.BlockSpec((1, LANES), lambda n: (0, 0)),
            pl.BlockSpec((1, DN_DV), lambda n: (0, 0))]


def _dn_front(c, seq, x_ref, halo_ref, ba_ref, cw_ref, al_ref, dt_ref):
    valid = _valid_rows(c * CHUNK, CHUNK, seq)
    xin = x_ref[...] * valid
    halo = halo_ref[...] * _valid_rows(c * CHUNK - 8, 8, seq)
    x_sh = [xin] + [_shift_down(xin, halo, k) for k in range(1, CONV_K)]
    yc = x_sh[0] * cw_ref[CONV_K - 1]
    for k in range(1, CONV_K):
        yc = yc + x_sh[k] * cw_ref[CONV_K - 1 - k]
    sgc = jax.nn.sigmoid(yc)
    ba = ba_ref[...]
    sig = jax.nn.sigmoid(ba)
    beta = sig * valid
    z = ba + dt_ref[...]
    eal = jnp.exp(al_ref[...])
    g = -eal * _softplus(z) * valid
    ri, ci = _iota((CHUNK, CHUNK), 0), _iota((CHUNK, CHUNK), 1)
    lower = (ri >= ci).astype(F32)
    upper = (ri <= ci).astype(F32)
    eye = (ri == ci).astype(F32)
    gam = _nn(lower, g, hi=True)
    gam_t = _tn(g, upper, hi=True)
    return dict(valid=valid, x_sh=x_sh, yc=yc, sgc=sgc, act=yc * sgc, sig=sig, beta=beta, z=z,
                eal=eal, g=g, gam=gam, gam_t=gam_t, ri=ri, ci=ci, upper=upper, eye=eye)


def _dn_head(f, h):
    act = f["act"]
    q_raw = act[:, h * DN_DK:(h + 1) * DN_DK]
    k_raw = act[:, DN_QK + h * DN_DK:DN_QK + (h + 1) * DN_DK]
    v = act[:, 2 * DN_QK + h * DN_DV:2 * DN_QK + (h + 1) * DN_DV]
    rq = lax.rsqrt(jnp.sum(q_raw * q_raw, axis=-1, keepdims=True) + RMS_EPS)
    rk = lax.rsqrt(jnp.sum(k_raw * k_raw, axis=-1, keepdims=True) + RMS_EPS)
    qh = q_raw * rq
    kn = k_raw * rk
    gam_c = _col(f["gam"], DECAY_LANE + h)
    gam_r = _row(f["gam_t"], DECAY_LANE + h)
    bc = _col(f["beta"], BETA_LANE + h)
    diff = gam_c - gam_r
    decay = jnp.where(f["ri"] >= f["ci"], jnp.exp(jnp.minimum(diff, 0.0)), 0.0)
    glast = jnp.sum(gam_r * (_iota((1, CHUNK), 1) == CHUNK - 1).astype(F32), axis=1, keepdims=True)
    return dict(rq=rq, rk=rk, qh=qh, qn=qh * (DN_DK ** -0.5), kn=kn, v=v, gam_c=gam_c, gam_r=gam_r,
                bc=bc, diff=diff, decay=decay, egam=jnp.exp(gam_c), glast=glast,
                eglast=jnp.exp(glast), ekd=jnp.exp(glast - gam_c))


def _dn_fwd(proj, conv_w, alog, dtb, norm_w, seq):
    rows = proj.shape[0]
    nc = rows // CHUNK

    def body(x_ref, halo_ref, gate_ref, ba_ref, cw_ref, al_ref, dt_ref, nw_ref,
             o_ref, y_ref, ss_ref, t_ref, s_ref, at_ref, tt_ref):
        n = pl.program_id(0)

        @pl.when(n == 0)
        def _():
            s_ref[...] = jnp.zeros_like(s_ref)

        f = _dn_front(n, seq, x_ref, halo_ref, ba_ref, cw_ref, al_ref, dt_ref)
        beta_t = _tn(f["beta"], f["eye"], hi=True)
        heads = []
        for h in range(DN_HEADS):
            hd = _dn_head(f, h)
            heads.append(hd)
            bc_r = _row(beta_t, BETA_LANE + h)
            decay_t = jnp.where(f["ri"] <= f["ci"], jnp.exp(jnp.minimum(-hd["diff"], 0.0)), 0.0)
            kk = _nt(hd["kn"], hd["kn"])
            at_ref[h] = jnp.where(f["ri"] < f["ci"], bc_r * kk * decay_t, 0.0)
        tt_ref[...] = jnp.zeros_like(tt_ref)
        lane = _iota((1, CHUNK), 1)
        ri = f["ri"]

        def solve_row(i, carry):
            oh = (lane == i).astype(F32)
            for h in range(DN_HEADS):
                t = tt_ref[h]
                a_i = jnp.sum(at_ref[h] * oh, axis=1, keepdims=True)
                new = oh - jnp.sum(a_i * t, axis=0, keepdims=True)
                tt_ref[h] = jnp.where(ri == i, new, t)
            return carry

        lax.fori_loop(0, CHUNK, solve_row, 0)
        nw = nw_ref[...]
        for h in range(DN_HEADS):
            hd = heads[h]
            v_cols = slice(h * DN_DV, (h + 1) * DN_DV)
            t = tt_ref[h]
            t_ref[0, h] = t
            s = s_ref[h]
            ss_ref[0, h] = s
            u = _nn(t, hd["v"] * hd["bc"], hi=True)
            w = _nn(t, hd["kn"] * (hd["bc"] * hd["egam"]), hi=True)
            v_new = u - _nn(w, s)
            qk = _nt(hd["qn"], hd["kn"]) * hd["decay"]
            o = _nn(hd["qn"] * hd["egam"], s) + _nn(qk, v_new)
            s_ref[h] = s * hd["eglast"] + _tn(hd["kn"] * hd["ekd"], v_new)
            o_ref[:, v_cols] = o
            y_ref[:, v_cols] = _gated_norm(o, gate_ref[:, v_cols], nw).astype(y_ref.dtype)

    fwd = lambda n: n
    row_v = pl.BlockSpec((CHUNK, DN_V), lambda n: (n, 0))
    return pl.pallas_call(
        body, grid=(nc,), in_specs=_dn_in_specs(fwd),
        out_specs=[row_v, row_v,
                   pl.BlockSpec((1, DN_HEADS, DN_DK, DN_DV), lambda n: (n, 0, 0, 0)),
                   pl.BlockSpec((1, DN_HEADS, CHUNK, CHUNK), lambda n: (n, 0, 0, 0))],
        out_shape=[jax.ShapeDtypeStruct((rows, DN_V), F32), jax.ShapeDtypeStruct((rows, DN_V), BF16),
                   jax.ShapeDtypeStruct((nc, DN_HEADS, DN_DK, DN_DV), F32),
                   jax.ShapeDtypeStruct((nc, DN_HEADS, CHUNK, CHUNK), F32)],
        scratch_shapes=[pltpu.VMEM((DN_HEADS, DN_DK, DN_DV), F32),
                        pltpu.VMEM((DN_HEADS, CHUNK, CHUNK), F32),
                        pltpu.VMEM((DN_HEADS, CHUNK, CHUNK), F32)],
        name="dn_fwd", compiler_params=_params("arbitrary"))(
            proj, proj, proj, proj, conv_w, alog, dtb, norm_w.reshape(1, DN_DV))


def _dn_bwd(proj, o, dy, states, tinv, conv_w, alog, dtb, norm_w, seq):
    rows = proj.shape[0]
    nc = rows // CHUNK

    def body(x_ref, halo_ref, gate_ref, ba_ref, cw_ref, al_ref, dt_ref, nw_ref,
             o_ref, dy_ref, ss_ref, t_ref,
             dp_ref, dcw_ref, dal_ref, ddt_ref, dnw_ref, ds_ref, nxt_ref):
        n = pl.program_id(0)

        @pl.when(n == 0)
        def _():
            ds_ref[...] = jnp.zeros_like(ds_ref)
            nxt_ref[...] = jnp.zeros_like(nxt_ref)
            dcw_ref[...] = jnp.zeros_like(dcw_ref)
            dal_ref[...] = jnp.zeros_like(dal_ref)
            ddt_ref[...] = jnp.zeros_like(ddt_ref)
            dnw_ref[...] = jnp.zeros_like(dnw_ref)

        f = _dn_front(nc - 1 - n, seq, x_ref, halo_ref, ba_ref, cw_ref, al_ref, dt_ref)
        ri, ci = f["ri"], f["ci"]
        strict = (ri > ci).astype(F32)
        nw = nw_ref[...]
        lane128 = _iota((1, LANES), 1)
        row128 = _iota((LANES, 1), 0)
        dgam_col = jnp.zeros((CHUNK, LANES), F32)
        dgam_row = jnp.zeros((LANES, CHUNK), F32)
        dbeta = jnp.zeros((CHUNK, LANES), F32)
        dnw = jnp.zeros((1, DN_DV), F32)
        dq_parts, dk_parts, dv_parts = [], [], []
        for h in range(DN_HEADS):
            hd = _dn_head(f, h)
            qn, kn, v, bc, egam, decay = hd["qn"], hd["kn"], hd["v"], hd["bc"], hd["egam"], hd["decay"]
            v_cols = slice(h * DN_DV, (h + 1) * DN_DV)
            t = t_ref[0, h]
            s = ss_ref[0, h]
            kk = _nt(kn, kn)
            p = _nt(qn, kn)
            qk = p * decay
            rhs_w = kn * (bc * egam)
            u = _nn(t, v * bc, hi=True)
            w = _nn(t, rhs_w, hi=True)
            v_new = u - _nn(w, s)
            qg = qn * egam
            kd = kn * hd["ekd"]
            do, dgate, dw_n = _gated_norm_bwd(dy_ref[:, v_cols], o_ref[:, v_cols], gate_ref[:, v_cols], nw)
            dnw = dnw + dw_n
            ds = ds_ref[h]
            dv_new = _tn(qk, do) + _nn(kd, ds)
            m = _nt(do, v_new)
            dp = m * decay
            dqg = _nt(do, s)
            dqn = dqg * egam + _nn(dp, kn)
            dgc = jnp.sum(dqg * qg, axis=1, keepdims=True)
            dkn = _tn(dp, qn)
            dkd = _nt(v_new, ds)
            dkn = dkn + dkd * hd["ekd"]
            t1 = jnp.sum(dkd * kd, axis=1, keepdims=True)
            dgc = dgc - t1
            dglast = (jnp.sum(t1, axis=0, keepdims=True)
                      + jnp.sum(jnp.sum(ds * s, axis=1, keepdims=True), axis=0, keepdims=True) * hd["eglast"])
            ds_ref[h] = ds * hd["eglast"] + _tn(qg, do) - _tn(w, dv_new)
            dw_ = -_nt(dv_new, s)
            dru = _tn(t, dv_new, hi=True)
            drw = _tn(t, dw_, hi=True)
            da = -(_nt(dru, u) + _nt(drw, w)) * strict
            dv_parts.append(dru * bc)
            dbc = (jnp.sum(dru * v, axis=1, keepdims=True)
                   + jnp.sum(drw * kn, axis=1, keepdims=True) * egam
                   + jnp.sum(da * kk * decay, axis=1, keepdims=True))
            dkn = dkn + drw * (bc * egam)
            dgc = dgc + jnp.sum(drw * rhs_w, axis=1, keepdims=True)
            dkk = da * (bc * decay)
            e = (m * p + da * (bc * kk)) * decay
            dkn = dkn + _nn(dkk, kn) + _tn(dkk, kn)
            dgc = dgc + jnp.sum(e, axis=1, keepdims=True)
            dgr = -jnp.sum(e, axis=0, keepdims=True)
            dgc = dgc + jnp.where(_iota((CHUNK, 1), 0) == CHUNK - 1, dglast, 0.0)
            qh = hd["qh"]
            dq_parts.append(((DN_DK ** -0.5) * hd["rq"])
                            * (dqn - qh * jnp.sum(dqn * qh, axis=1, keepdims=True)))
            dk_parts.append(hd["rk"] * (dkn - kn * jnp.sum(dkn * kn, axis=1, keepdims=True)))
            dgam_col = dgam_col + dgc * (lane128 == DECAY_LANE + h).astype(F32)
            dbeta = dbeta + dbc * (lane128 == BETA_LANE + h).astype(F32)
            dgam_row = dgam_row + (row128 == DECAY_LANE + h).astype(F32) * dgr
            dp_ref[:, DN_CONV_CH + h * DN_DV:DN_CONV_CH + (h + 1) * DN_DV] = dgate.astype(dp_ref.dtype)
        dnw_ref[...] += dnw
        dgam = dgam_col + _nt(f["eye"], dgam_row, hi=True)
        dg = _nn(f["upper"], dgam, hi=True)
        d_a = dg * (-f["eal"]) * jax.nn.sigmoid(f["z"]) * f["valid"]
        dal_ref[...] += jnp.sum(dg * f["g"], axis=0, keepdims=True)
        ddt_ref[...] += jnp.sum(d_a, axis=0, keepdims=True)
        d_b = dbeta * f["valid"] * f["sig"] * (1.0 - f["sig"])
        dp_ref[:, DN_CONV_CH + DN_V:] = (d_a + d_b).astype(dp_ref.dtype)
        dact = jnp.concatenate(dq_parts + dk_parts + dv_parts, axis=1)
        yc, sgc = f["yc"], f["sgc"]
        dyc = dact * (sgc * (1.0 + yc * (1.0 - sgc)))
        for k in range(CONV_K):
            dcw_ref[k] += jnp.sum(dyc * f["x_sh"][CONV_K - 1 - k], axis=0, keepdims=True)
        nxt = nxt_ref[...]
        dx = dyc * cw_ref[CONV_K - 1]
        for j in range(1, CONV_K):
            dx = dx + _shift_up(dyc, nxt, j) * cw_ref[CONV_K - 1 - j]
        nxt_ref[...] = dyc[0:8]
        dp_ref[:, :DN_CONV_CH] = (dx * f["valid"]).astype(dp_ref.dtype)

    rev = lambda n: nc - 1 - n
    row_v = pl.BlockSpec((CHUNK, DN_V), lambda n: (rev(n), 0))
    vec = pl.BlockSpec((1, LANES), lambda n: (0, 0))
    return pl.pallas_call(
        body, grid=(nc,),
        in_specs=_dn_in_specs(rev) + [
            row_v, row_v,
            pl.BlockSpec((1, DN_HEADS, DN_DK, DN_DV), lambda n: (rev(n), 0, 0, 0)),
            pl.BlockSpec((1, DN_HEADS, CHUNK, CHUNK), lambda n: (rev(n), 0, 0, 0))],
        out_specs=[pl.BlockSpec((CHUNK, DN_IN_PAD), lambda n: (rev(n), 0)),
                   pl.BlockSpec((CONV_K, 1, DN_CONV_CH), lambda n: (0, 0, 0)), vec, vec,
                   pl.BlockSpec((1, DN_DV), lambda n: (0, 0))],
        out_shape=[jax.ShapeDtypeStruct((rows, DN_IN_PAD), BF16),
                   jax.ShapeDtypeStruct((CONV_K, 1, DN_CONV_CH), F32),
                   jax.ShapeDtypeStruct((1, LANES), F32), jax.ShapeDtypeStruct((1, LANES), F32),
                   jax.ShapeDtypeStruct((1, DN_DV), F32)],
        scratch_shapes=[pltpu.VMEM((DN_HEADS, DN_DK, DN_DV), F32), pltpu.VMEM((8, DN_CONV_CH), F32)],
        name="dn_bwd", compiler_params=_params("arbitrary"))(
            proj, proj, proj, proj, conv_w, alog, dtb, norm_w.reshape(1, DN_DV), o, dy, states, tinv)


def _local_step(x, tgt, wts):
    seq = x.shape[0]
    rows = -(-(seq + CHUNK) // ROW_ALIGN) * ROW_ALIGN
    tail = rows - seq - CHUNK
    h0 = jnp.concatenate([jnp.zeros((PAD, D_MODEL), F32), wts["meta_tokens"].astype(F32), x,
                          jnp.zeros((tail, D_MODEL), F32)], axis=0)
    tgt_p = jnp.concatenate([jnp.zeros((CHUNK, D_MODEL), F32), tgt, jnp.zeros((tail, D_MODEL), F32)],
                            axis=0)
    cos, sin = _rope_tables(rows)
    consts = _ret_consts()
    conv_w = wts["dn_conv_w"].reshape(CONV_K, 1, DN_CONV_CH)
    lane_pad = LANES - 2 * DN_HEADS
    alog = jnp.pad(wts["dn_a_log"].reshape(1, DN_HEADS), ((0, 0), (DECAY_LANE, lane_pad)))
    dtb = jnp.pad(wts["dn_dt_bias"].reshape(1, DN_HEADS), ((0, 0), (DECAY_LANE, lane_pad)))
    g = {}

    hn0 = _rms_fwd(h0, wts["mix_norm_w"][0], "rms_mix0")
    proj0 = _mm(hn0, wts["ret_w_in"], mode="nn", name="ret_in")
    o0, y0, st0 = _ret_fwd(proj0, cos, sin, consts, wts["ret_gn_w"], seq)
    h1 = _mm(y0, wts["ret_w_out"], mode="nn", name="ret_out", resid=h0)
    hn1 = _rms_fwd(h1, wts["ffn_norm_w"][0], "rms_ffn0")
    g0, u0, act0 = _ffn_up(hn1, wts["ffn_w_gate"][0], wts["ffn_w_up"][0], "ffn_up0")
    h2 = _mm(act0, wts["ffn_w_down"][0], mode="nn", name="ffn_down0", resid=h1)
    hn2 = _rms_fwd(h2, wts["mix_norm_w"][1], "rms_mix1")
    proj1 = _mm(hn2, wts["dn_w_in"], mode="nn", name="dn_in")
    o1, y1, st1, tinv = _dn_fwd(proj1, conv_w, alog, dtb, wts["dn_norm_w"], seq)
    h3 = _mm(y1, wts["dn_w_out"], mode="nn", name="dn_out", resid=h2)
    hn3 = _rms_fwd(h3, wts["ffn_norm_w"][1], "rms_ffn1")
    g1, u1, act1 = _ffn_up(hn3, wts["ffn_w_gate"][1], wts["ffn_w_up"][1], "ffn_up1")
    h4 = _mm(act1, wts["ffn_w_down"][1], mode="nn", name="ffn_down1", resid=h3)

    dh4, g["final_norm_w"], loss = _final_loss(h4, wts["final_norm_w"], tgt_p, seq, "final_loss")

    def ffn_bwd(dh_out, h_mid, hn, gg, uu, act, layer, tag):
        dg, du = _ffn_down_bwd(dh_out, wts["ffn_w_down"][layer], gg, uu, "ffn_down_bwd" + tag)
        d_down = _mm(act, dh_out, mode="tn", name="ffn_dwd" + tag)
        d_gate = _mm(hn, dg, mode="tn", name="ffn_dwg" + tag)
        d_up = _mm(hn, du, mode="tn", name="ffn_dwu" + tag)
        dhn = _mm(dg, wts["ffn_w_gate"][layer], mode="nt", name="ffn_dhg" + tag)
        dhn = _mm(du, wts["ffn_w_up"][layer], mode="nt", name="ffn_dhu" + tag, resid=dhn)
        dh_mid, d_norm = _rms_bwd(dhn, h_mid, wts["ffn_norm_w"][layer], dh_out, "rms_ffn_bwd" + tag)
        return dh_mid, d_down, d_gate, d_up, d_norm

    dh3, dwd1, dwg1, dwu1, dfn1 = ffn_bwd(dh4, h3, hn3, g1, u1, act1, 1, "1")
    dy1 = _mm(dh3, wts["dn_w_out"], mode="nt", name="dn_out_bwd")
    g["dn_w_out"] = _mm(y1, dh3, mode="tn", name="dn_dwo")
    dproj1, dcw, dal, ddt, g["dn_norm_w"] = _dn_bwd(proj1, o1, dy1, st1, tinv, conv_w, alog, dtb,
                                                    wts["dn_norm_w"], seq)
    g["dn_w_in"] = _mm(hn2, dproj1, mode="tn", name="dn_dwi")[:, :DN_IN]
    dhn2 = _mm(dproj1, wts["dn_w_in"], mode="nt", name="dn_in_bwd", row_cap=256)
    dh2, dmn1 = _rms_bwd(dhn2, h2, wts["mix_norm_w"][1], dh3, "rms_mix_bwd1")
    g["dn_conv_w"] = dcw.reshape(CONV_K, DN_CONV_CH)
    g["dn_a_log"] = dal[0, DECAY_LANE:DECAY_LANE + DN_HEADS]
    g["dn_dt_bias"] = ddt[0, DECAY_LANE:DECAY_LANE + DN_HEADS]

    dh1, dwd0, dwg0, dwu0, dfn0 = ffn_bwd(dh2, h1, hn1, g0, u0, act0, 0, "0")
    dy0 = _mm(dh1, wts["ret_w_out"], mode="nt", name="ret_out_bwd")
    g["ret_w_out"] = _mm(y0, dh1, mode="tn", name="ret_dwo")
    dproj0, g["ret_gn_w"] = _ret_bwd(proj0, o0, dy0, st0, cos, sin, consts, wts["ret_gn_w"], seq)
    g["ret_w_in"] = _mm(hn0, dproj0, mode="tn", name="ret_dwi")
    dhn0 = _mm(dproj0, wts["ret_w_in"], mode="nt", name="ret_in_bwd", row_cap=256)
    dh0, dmn0 = _rms_bwd(dhn0, h0, wts["mix_norm_w"][0], dh1, "rms_mix_bwd0")

    g["ffn_w_down"] = jnp.stack([dwd0, dwd1])
    g["ffn_w_gate"] = jnp.stack([dwg0, dwg1])
    g["ffn_w_up"] = jnp.stack([dwu0, dwu1])
    g["ffn_norm_w"] = jnp.concatenate([dfn0, dfn1], axis=0)
    g["mix_norm_w"] = jnp.concatenate([dmn0, dmn1], axis=0)
    g["meta_tokens"] = dh0[PAD:CHUNK]
    g["final_norm_w"] = g["final_norm_w"].reshape(D_MODEL)
    g["ret_gn_w"] = g["ret_gn_w"].reshape(RET_DV)
    g["dn_norm_w"] = g["dn_norm_w"].reshape(DN_DV)
    return loss, dh0, g


def _mesh_pos():
    return lax.axis_index("x"), lax.axis_index("y"), lax.axis_index("c")


def _other_chips(x, y):
    return [(1 - x, y), (x, 1 - y), (1 - x, 1 - y)]


def _remote(src, dst, send_sem, recv_sem, to):
    return pltpu.make_async_remote_copy(src_ref=src, dst_ref=dst, send_sem=send_sem, recv_sem=recv_sem,
                                        device_id=to, device_id_type=MESH)


def _gather_big(wpack):
    r, wd = wpack.shape
    half = r // 2

    def body(w_ref, out_ref, send_sems, recv_sems, local_sem):
        x, y, c = _mesh_pos()
        mine = pl.ds(c * half, half)
        other = pl.ds((1 - c) * half, half)
        chips = _other_chips(x, y)
        sibling = (x, y, 1 - c)

        def blk(px, py, rows):
            return out_ref.at[2 * px + py, rows]

        local = pltpu.make_async_copy(w_ref, out_ref.at[2 * x + y], local_sem)
        local.start()
        first = [_remote(w_ref.at[mine], blk(x, y, mine), send_sems.at[k], recv_sems.at[k], (px, py, c))
                 for k, (px, py) in enumerate(chips)]
        for cp in first:
            cp.start()
        passed = []
        for k, (px, py) in enumerate(chips):
            _remote(w_ref.at[mine], blk(px, py, mine), send_sems.at[k], recv_sems.at[k], (px, py, c)).wait_recv()
            cp = _remote(blk(px, py, mine), blk(px, py, mine), send_sems.at[3 + k], recv_sems.at[3 + k], sibling)
            cp.start()
            passed.append(cp)
        for k, (px, py) in enumerate(chips):
            _remote(blk(px, py, other), blk(px, py, other), send_sems.at[3 + k], recv_sems.at[3 + k],
                    sibling).wait_recv()
        for cp in first + passed:
            cp.wait_send()
        local.wait()

    return pl.pallas_call(
        body, out_shape=jax.ShapeDtypeStruct((4, r, wd), wpack.dtype), in_specs=[ANY], out_specs=ANY,
        scratch_shapes=[pltpu.SemaphoreType.DMA((6,)), pltpu.SemaphoreType.DMA((6,)), pltpu.SemaphoreType.DMA(())],
        name="gather_big")(wpack)


def _gather_small(blk):
    r, wd = blk.shape

    def body(b_ref, out_ref, send_sems, recv_sems):
        x, y, c = _mesh_pos()
        chips = _other_chips(x, y)
        out_ref[2 * x + y] = b_ref[...]
        sends = [_remote(b_ref, out_ref.at[2 * x + y], send_sems.at[k], recv_sems.at[k], (px, py, c))
                 for k, (px, py) in enumerate(chips)]
        for cp in sends:
            cp.start()
        for k, (px, py) in enumerate(chips):
            _remote(b_ref, out_ref.at[2 * px + py], send_sems.at[k], recv_sems.at[k], (px, py, c)).wait_recv()
        for cp in sends:
            cp.wait_send()

    return pl.pallas_call(
        body, out_shape=jax.ShapeDtypeStruct((4, r, wd), blk.dtype), in_specs=[VMEM_SPEC], out_specs=VMEM_SPEC,
        scratch_shapes=[pltpu.SemaphoreType.DMA((3,)), pltpu.SemaphoreType.DMA((3,))],
        name="gather_small")(blk)


def _allreduce_small(blk):
    r, wd = blk.shape
    rels = [(dx, dy, dc) for dx in (0, 1) for dy in (0, 1) for dc in (0, 1) if dx or dy or dc]

    def body(b_ref, out_ref, buf_ref, send_sems, recv_sems):
        x, y, c = _mesh_pos()

        def peer(rel):
            dx, dy, dc = rel
            return (1 - x if dx else x, 1 - y if dy else y, 1 - c if dc else c)

        me = 4 * x + 2 * y + c
        buf_ref[me] = b_ref[...]
        sends = [_remote(b_ref, buf_ref.at[me], send_sems.at[k], recv_sems.at[k], peer(rel))
                 for k, rel in enumerate(rels)]
        for cp in sends:
            cp.start()
        for k, rel in enumerate(rels):
            px, py, pc = peer(rel)
            _remote(b_ref, buf_ref.at[4 * px + 2 * py + pc], send_sems.at[k], recv_sems.at[k],
                    (px, py, pc)).wait_recv()
        for cp in sends:
            cp.wait_send()
        acc = buf_ref[0]
        for d in range(1, 8):
            acc = acc + buf_ref[d]
        out_ref[...] = acc

    return pl.pallas_call(
        body, out_shape=jax.ShapeDtypeStruct((r, wd), blk.dtype), in_specs=[VMEM_SPEC], out_specs=VMEM_SPEC,
        scratch_shapes=[pltpu.VMEM((8, r, wd), blk.dtype), pltpu.SemaphoreType.DMA((7,)),
                        pltpu.SemaphoreType.DMA((7,))],
        name="allreduce_small")(blk)


def _rs_pair(g):
    _, r, wd = g.shape
    half = r // 2

    def body(g_ref, a_ref, send_sem, recv_sem):
        x, y, c = _mesh_pos()
        cp = _remote(g_ref.at[:, pl.ds((1 - c) * half, half)], a_ref, send_sem, recv_sem, (x, y, 1 - c))
        cp.start()
        cp.wait()

    return pl.pallas_call(
        body, out_shape=jax.ShapeDtypeStruct((4, half, wd), g.dtype), in_specs=[ANY], out_specs=ANY,
        scratch_shapes=[pltpu.SemaphoreType.DMA(()), pltpu.SemaphoreType.DMA(())], name="rs_pair")(g)


def _rs_pair_add(g, a, c_idx):
    _, r, wd = g.shape
    half = r // 2
    tr = _div_tile(half, 1536, 16)
    nb = half // tr

    def body(s_ref, g_ref, a_ref, p_ref):
        p_ref[...] = (g_ref[...] + a_ref[...]).astype(p_ref.dtype)

    spec = pltpu.PrefetchScalarGridSpec(
        num_scalar_prefetch=1, grid=(4, nb),
        in_specs=[pl.BlockSpec((1, tr, wd), lambda j, i, s: (j, s[0] * nb + i, 0)),
                  pl.BlockSpec((1, tr, wd), lambda j, i, s: (j, i, 0))],
        out_specs=pl.BlockSpec((1, tr, wd), lambda j, i, s: (j, i, 0)))
    return pl.pallas_call(
        body, grid_spec=spec, out_shape=jax.ShapeDtypeStruct((4, half, wd), BF16), name="rs_pair_add",
        compiler_params=_params("parallel", "parallel"))(c_idx, g, a)


def _rs_chips(p):
    _, half, wd = p.shape

    def body(p_ref, b_ref, send_sems, recv_sems):
        x, y, c = _mesh_pos()
        cps = [_remote(p_ref.at[2 * px + py], b_ref.at[k], send_sems.at[k], recv_sems.at[k], (px, py, c))
               for k, (px, py) in enumerate(_other_chips(x, y))]
        for cp in cps:
            cp.start()
        for cp in cps:
            cp.wait()

    return pl.pallas_call(
        body, out_shape=jax.ShapeDtypeStruct((3, half, wd), p.dtype), in_specs=[ANY], out_specs=ANY,
        scratch_shapes=[pltpu.SemaphoreType.DMA((3,)), pltpu.SemaphoreType.DMA((3,))], name="rs_chips")(p)


def _rs_final_add(g, a, b, idx):
    _, r, wd = g.shape
    half = r // 2
    tr = _div_tile(half, 768, 16)
    nb = half // tr

    def body(s_ref, g_ref, a_ref, b0_ref, b1_ref, b2_ref, f_ref):
        own = g_ref[0] + a_ref[0]
        f_ref[...] = ((own + b0_ref[0].astype(F32)) + b1_ref[0].astype(F32)) + b2_ref[0].astype(F32)

    def b_spec(k):
        return pl.BlockSpec((1, tr, wd), lambda i, s: (k, i, 0))

    spec = pltpu.PrefetchScalarGridSpec(
        num_scalar_prefetch=1, grid=(nb,),
        in_specs=[pl.BlockSpec((1, tr, wd), lambda i, s: (s[1], s[0] * nb + i, 0)),
                  pl.BlockSpec((1, tr, wd), lambda i, s: (s[1], i, 0)), b_spec(0), b_spec(1), b_spec(2)],
        out_specs=pl.BlockSpec((tr, wd), lambda i, s: (i, 0)))
    return pl.pallas_call(
        body, grid_spec=spec, out_shape=jax.ShapeDtypeStruct((half, wd), F32), name="rs_final_add",
        compiler_params=_params("parallel"))(idx, g, a, b, b, b)


def _rs_share(f):
    half, wd = f.shape

    def body(f_ref, out_ref, send_sem, recv_sem, local_sem):
        x, y, c = _mesh_pos()
        mine = pl.ds(c * half, half)
        local = pltpu.make_async_copy(f_ref, out_ref.at[mine], local_sem)
        local.start()
        cp = _remote(f_ref, out_ref.at[mine], send_sem, recv_sem, (x, y, 1 - c))
        cp.start()
        cp.wait()
        local.wait()

    return pl.pallas_call(
        body, out_shape=jax.ShapeDtypeStruct((2 * half, wd), f.dtype), in_specs=[ANY], out_specs=ANY,
        scratch_shapes=[pltpu.SemaphoreType.DMA(()), pltpu.SemaphoreType.DMA(()), pltpu.SemaphoreType.DMA(())],
        name="rs_share")(f)


def _adamw(w, g, m, v, name):
    rows, cols = w.shape
    tr = rows // 4 if rows % 32 == 0 else rows

    def body(w_ref, g_ref, m_ref, v_ref, d_ref, mo_ref, vo_ref):
        gv = g_ref[...]
        mn = ADAM_B1 * m_ref[...] + (1.0 - ADAM_B1) * gv
        vn = ADAM_B2 * v_ref[...] + (1.0 - ADAM_B2) * (gv * gv)
        m_hat = mn / (1.0 - ADAM_B1 ** ADAM_STEP)
        v_hat = vn / (1.0 - ADAM_B2 ** ADAM_STEP)
        d_ref[...] = -ADAM_LR * (m_hat / (jnp.sqrt(v_hat) + ADAM_EPS) + ADAM_WD * w_ref[...])
        mo_ref[...] = mn
        vo_ref[...] = vn

    blk = pl.BlockSpec((tr, cols), lambda i: (i, 0))
    out = jax.ShapeDtypeStruct((rows, cols), F32)
    return pl.pallas_call(
        body, grid=(rows // tr,), in_specs=[blk] * 4, out_specs=[blk] * 3, out_shape=[out] * 3, name=name,
        compiler_params=_params("parallel"))(w, g, m, v)


BIG = ["ret_w_in", "ret_w_out", "dn_w_in", "dn_w_out", "ffn_w_gate", "ffn_w_up", "ffn_w_down"]
COL_SHARDED = {"ret_w_in", "dn_w_in", "ffn_w_gate", "ffn_w_up"}
SMALL = ["meta_tokens", "mix_norm_w", "ffn_norm_w", "ret_gn_w", "dn_conv_w", "dn_a_log", "dn_dt_bias",
         "dn_norm_w", "final_norm_w"]
SMALL_SHARDED = {"meta_tokens", "dn_conv_w", "dn_norm_w"}
ORDER = ["meta_tokens", "mix_norm_w", "ffn_norm_w", "ret_w_in", "ret_gn_w", "ret_w_out", "dn_w_in",
         "dn_conv_w", "dn_a_log", "dn_dt_bias", "dn_norm_w", "dn_w_out", "ffn_w_gate", "ffn_w_up",
         "ffn_w_down", "final_norm_w"]
PACK_W = 1024
PACK_ROW_ALIGN = 256


def _pack_rows(parts, width, align):
    flat = jnp.concatenate([p.reshape(-1, width) for p in parts], axis=0)
    return jnp.pad(flat, ((0, -flat.shape[0] % align), (0, 0)))


def _pack_lanes(parts, align=8):
    flat = jnp.concatenate([p.reshape(-1) for p in parts])
    flat = jnp.pad(flat, (0, -flat.shape[0] % (align * LANES)))
    return flat.reshape(-1, LANES)


def _unpack(buf, shapes):
    lead = buf.shape[:-2]
    flat = buf.reshape(lead + (-1,))
    out, off = [], 0
    for shp in shapes:
        size = math.prod(shp)
        out.append(flat[..., off:off + size].reshape(lead + tuple(shp)))
        off += size
    return out


def _join(shards, col_sharded):
    axis = shards.ndim - 1 if col_sharded else shards.ndim - 2
    return jnp.concatenate([shards[j] for j in range(4)], axis=axis - 1)


def _split(full, col_sharded):
    axis = full.ndim - 1 if col_sharded else full.ndim - 2
    return jnp.stack(jnp.split(full, 4, axis=axis))


def kernel(x, meta_tokens, mix_norm_w, ffn_norm_w, ret_w_in, ret_gn_w, ret_w_out, dn_w_in, dn_conv_w, dn_a_log, dn_dt_bias, dn_norm_w, dn_w_out, ffn_w_gate, ffn_w_up, ffn_w_down, final_norm_w, loss_target, m_meta_tokens, m_mix_norm_w, m_ffn_norm_w, m_ret_w_in, m_ret_gn_w, m_ret_w_out, m_dn_w_in, m_dn_conv_w, m_dn_a_log, m_dn_dt_bias, m_dn_norm_w, m_dn_w_out, m_ffn_w_gate, m_ffn_w_up, m_ffn_w_down, m_final_norm_w, v_meta_tokens, v_mix_norm_w, v_ffn_norm_w, v_ret_w_in, v_ret_gn_w, v_ret_w_out, v_dn_w_in, v_dn_conv_w, v_dn_a_log, v_dn_dt_bias, v_dn_norm_w, v_dn_w_out, v_ffn_w_gate, v_ffn_w_up, v_ffn_w_down, v_final_norm_w):
    w = dict(meta_tokens=meta_tokens, mix_norm_w=mix_norm_w, ffn_norm_w=ffn_norm_w, ret_w_in=ret_w_in,
             ret_gn_w=ret_gn_w, ret_w_out=ret_w_out, dn_w_in=dn_w_in, dn_conv_w=dn_conv_w, dn_a_log=dn_a_log,
             dn_dt_bias=dn_dt_bias, dn_norm_w=dn_norm_w, dn_w_out=dn_w_out, ffn_w_gate=ffn_w_gate,
             ffn_w_up=ffn_w_up, ffn_w_down=ffn_w_down, final_norm_w=final_norm_w)
    m = dict(meta_tokens=m_meta_tokens, mix_norm_w=m_mix_norm_w, ffn_norm_w=m_ffn_norm_w, ret_w_in=m_ret_w_in,
             ret_gn_w=m_ret_gn_w, ret_w_out=m_ret_w_out, dn_w_in=m_dn_w_in, dn_conv_w=m_dn_conv_w,
             dn_a_log=m_dn_a_log, dn_dt_bias=m_dn_dt_bias, dn_norm_w=m_dn_norm_w, dn_w_out=m_dn_w_out,
             ffn_w_gate=m_ffn_w_gate, ffn_w_up=m_ffn_w_up, ffn_w_down=m_ffn_w_down, final_norm_w=m_final_norm_w)
    v = dict(meta_tokens=v_meta_tokens, mix_norm_w=v_mix_norm_w, ffn_norm_w=v_ffn_norm_w, ret_w_in=v_ret_w_in,
             ret_gn_w=v_ret_gn_w, ret_w_out=v_ret_w_out, dn_w_in=v_dn_w_in, dn_conv_w=v_dn_conv_w,
             dn_a_log=v_dn_a_log, dn_dt_bias=v_dn_dt_bias, dn_norm_w=v_dn_norm_w, dn_w_out=v_dn_w_out,
             ffn_w_gate=v_ffn_w_gate, ffn_w_up=v_ffn_w_up, ffn_w_down=v_ffn_w_down, final_norm_w=v_final_norm_w)
    mx, my, mc = _mesh_pos()
    chip = 2 * mx + my

    big_shapes = [w[n].shape for n in BIG]
    wpack = _pack_rows([w[n].astype(MXU_DTYPE) for n in BIG], PACK_W, PACK_ROW_ALIGN)
    gathered = _unpack(_gather_big(wpack), big_shapes)
    full = {n: _join(gathered[i], n in COL_SHARDED) for i, n in enumerate(BIG)}
    sm_shapes = [w[n].shape for n in SMALL if n in SMALL_SHARDED]
    sm_gathered = _unpack(_gather_small(_pack_lanes([w[n] for n in SMALL if n in SMALL_SHARDED])), sm_shapes)
    for i, n in enumerate([n for n in SMALL if n in SMALL_SHARDED]):
        full[n] = _join(sm_gathered[i], True)
    wts = {
        "meta_tokens": full["meta_tokens"], "mix_norm_w": mix_norm_w, "ffn_norm_w": ffn_norm_w,
        "ret_gn_w": ret_gn_w[0], "final_norm_w": final_norm_w, "dn_conv_w": full["dn_conv_w"][0],
        "dn_a_log": dn_a_log[0], "dn_dt_bias": dn_dt_bias[0], "dn_norm_w": full["dn_norm_w"][0],
        "ret_w_in": full["ret_w_in"][0], "ret_w_out": full["ret_w_out"][0],
        "dn_w_in": jnp.pad(full["dn_w_in"][0], ((0, 0), (0, DN_IN_PAD - DN_IN))),
        "dn_w_out": full["dn_w_out"][0], "ffn_w_gate": full["ffn_w_gate"], "ffn_w_up": full["ffn_w_up"],
        "ffn_w_down": full["ffn_w_down"],
    }

    loss_part, dh0, g = _local_step(x[0], loss_target[0], wts)
    seq = x.shape[1]
    grad_x = dh0[CHUNK:CHUNK + seq].reshape(x.shape)

    gpack = jnp.concatenate(
        [_split(g[n].reshape((-1,) + full[n].shape[1:]) if g[n].ndim < full[n].ndim else g[n],
                n in COL_SHARDED).reshape(4, -1, PACK_W) for n in BIG], axis=1)
    gpack = jnp.pad(gpack, ((0, 0), (0, -gpack.shape[1] % PACK_ROW_ALIGN), (0, 0)))
    sib = _rs_pair(gpack)
    c_idx = jnp.stack([mc, chip]).astype(jnp.int32)
    part = _rs_pair_add(gpack, sib, c_idx[:1])
    others = _rs_chips(part)
    mine = _rs_final_add(gpack, sib, others, c_idx)
    gsh = dict(zip(BIG, _unpack(_rs_share(mine), big_shapes)))

    small_full_shapes = [g[n].shape for n in SMALL] + [(1,)]
    red = _unpack(_allreduce_small(_pack_lanes([g[n] for n in SMALL] + [loss_part[0, :1]])), small_full_shapes)
    loss = red[-1][0]
    for i, n in enumerate(SMALL):
        gn = red[i]
        if n in SMALL_SHARDED:
            width = w[n].shape[-1]
            gn = lax.dynamic_slice_in_dim(gn, chip * width, width, axis=gn.ndim - 1)
        gsh[n] = gn.reshape(w[n].shape)

    delta, new_m, new_v = {}, {}, {}
    for n in BIG:
        shp = w[n].shape
        two_d = (-1, shp[-1])
        d_, m_, v_ = _adamw(w[n].reshape(two_d), gsh[n].reshape(two_d), m[n].reshape(two_d),
                            v[n].reshape(two_d), "adamw_" + n)
        delta[n], new_m[n], new_v[n] = d_.reshape(shp), m_.reshape(shp), v_.reshape(shp)
    sm_local_shapes = [w[n].shape for n in SMALL]
    d_, m_, v_ = _adamw(_pack_lanes([w[n] for n in SMALL]), _pack_lanes([gsh[n] for n in SMALL]),
                        _pack_lanes([m[n] for n in SMALL]), _pack_lanes([v[n] for n in SMALL]), "adamw_small")
    for n, dd, mm, vv in zip(SMALL, _unpack(d_, sm_local_shapes), _unpack(m_, sm_local_shapes),
                             _unpack(v_, sm_local_shapes)):
        delta[n], new_m[n], new_v[n] = dd, mm, vv

    return (loss, grad_x, *[gsh[n] for n in ORDER], *[delta[n] for n in ORDER],
            *[new_m[n] for n in ORDER], *[new_v[n] for n in ORDER])
```

```python
import functools
import math

import jax
import jax.numpy as jnp
from jax import lax
from jax.experimental import pallas as pl
from jax.experimental.pallas import tpu as pltpu

F32 = jnp.float32
BF16 = jnp.bfloat16
MXU_DTYPE = BF16

D_MODEL = 1024
N_META = 16
CHUNK = 64
PAD = CHUNK - N_META
RMS_EPS = 1e-6
RET_HEADS, RET_DK, RET_DV = 4, 256, 512
RET_QK, RET_V = RET_HEADS * RET_DK, RET_HEADS * RET_DV
RET_IN = 2 * RET_QK + 2 * RET_V
ROPE_BASE = 10000.0
DN_HEADS, DN_DK, DN_DV = 8, 128, 256
DN_QK, DN_V = DN_HEADS * DN_DK, DN_HEADS * DN_DV
DN_CONV_CH = 2 * DN_QK + DN_V
DN_IN = DN_CONV_CH + DN_V + 2 * DN_HEADS
LANES = 128
DN_IN_PAD = DN_CONV_CH + DN_V + LANES
CONV_K = 4
FFN_HIDDEN = 2816
ADAM_LR, ADAM_B1, ADAM_B2, ADAM_EPS, ADAM_WD, ADAM_STEP = 0.001, 0.9, 0.999, 1e-08, 0.01, 10

ROW_ALIGN = 256
VMEM_LIMIT = 56 * 1024 * 1024
MESH = pl.DeviceIdType.MESH
ANY = pl.BlockSpec(memory_space=pl.ANY)
VMEM_SPEC = pl.BlockSpec(memory_space=pltpu.VMEM)
_HI = lax.Precision.HIGHEST


def _params(*sem):
    return pltpu.CompilerParams(dimension_semantics=sem, vmem_limit_bytes=VMEM_LIMIT)


def _dg(a, b, ca, cb, hi):
    dims = (((ca,), (cb,)), ((), ()))
    if hi:
        return lax.dot_general(a.astype(F32), b.astype(F32), dims, precision=_HI,
                               preferred_element_type=F32)
    return lax.dot_general(a.astype(MXU_DTYPE), b.astype(MXU_DTYPE), dims,
                           preferred_element_type=F32)


def _nn(a, b, hi=False):
    return _dg(a, b, 1, 0, hi)


def _nt(a, b, hi=False):
    return _dg(a, b, 1, 1, hi)


def _tn(a, b, hi=False):
    return _dg(a, b, 0, 0, hi)


def _iota(shape, dim):
    return lax.broadcasted_iota(jnp.int32, shape, dim)


def _valid_rows(first_row, rows, seq):
    r = first_row + _iota((rows, 1), 0)
    return ((r >= PAD) & (r < CHUNK + seq)).astype(F32)


def _rope(t, cs, sn):
    half = t.shape[-1] // 2
    t1, t2 = t[:, :half], t[:, half:]
    return jnp.concatenate([t1 * cs - t2 * sn, t1 * sn + t2 * cs], axis=1)


def _rope_bwd(d, cs, sn):
    half = d.shape[-1] // 2
    d1, d2 = d[:, :half], d[:, half:]
    return jnp.concatenate([d1 * cs + d2 * sn, d2 * cs - d1 * sn], axis=1)


def _col(x, idx):
    oh = (_iota((1, x.shape[1]), 1) == idx).astype(F32)
    return jnp.sum(x * oh, axis=1, keepdims=True)


def _row(x, idx):
    oh = (_iota((x.shape[0], 1), 0) == idx).astype(F32)
    return jnp.sum(x * oh, axis=0, keepdims=True)


def _shift_down(x, halo8, k):
    xr = pltpu.roll(x, k, 0)
    hr = pltpu.roll(halo8, k, 0)
    first = jnp.where(_iota((8, 1), 0) < k, hr, xr[0:8])
    return jnp.concatenate([first, xr[8:]], axis=0)


def _shift_up(x, next8, j):
    rows = x.shape[0]
    xr = pltpu.roll(x, rows - j, 0)
    nr = pltpu.roll(next8, 8 - j, 0)
    last = jnp.where(_iota((8, 1), 0) >= 8 - j, nr, xr[rows - 8:])
    return jnp.concatenate([xr[:rows - 8], last], axis=0)


def _gated_norm(o, gate, w):
    r = lax.rsqrt(jnp.mean(o * o, axis=-1, keepdims=True) + RMS_EPS)
    return o * r * w * (gate * jax.nn.sigmoid(gate))


def _gated_norm_bwd(dy, o, gate, w):
    r = lax.rsqrt(jnp.mean(o * o, axis=-1, keepdims=True) + RMS_EPS)
    nrm = o * r
    sg = jax.nn.sigmoid(gate)
    sl = gate * sg
    dgate = dy * nrm * w * (sg * (1.0 + gate * (1.0 - sg)))
    dn = dy * w * sl
    dw = jnp.sum(dy * nrm * sl, axis=0, keepdims=True)
    do = r * (dn - nrm * jnp.mean(dn * nrm, axis=-1, keepdims=True))
    return do, dgate, dw


def _softplus(z):
    return jnp.maximum(z, 0.0) + jnp.log(1.0 + jnp.exp(-jnp.abs(z)))


def _row_tile(rows, cap=768):
    for t in (768, 512, 256, 128, 64, 32, 16, 8):
        if t <= cap and rows % t == 0:
            return t
    return rows


def _div_tile(n, cap, mult):
    best = None
    for t in range(mult, min(cap, n) + 1, mult):
        if n % t == 0:
            best = t
    return best or n


def _col_tile(cols, cap=1536):
    best = None
    for t in range(LANES, min(cap, cols) + 1, LANES):
        if cols % t == 0:
            best = t
    return best or cols


def _rms_fwd(h, w, name):
    rows, d = h.shape
    tm = _row_tile(rows)

    def body(h_ref, w_ref, o_ref):
        x = h_ref[...]
        r = lax.rsqrt(jnp.mean(x * x, axis=-1, keepdims=True) + RMS_EPS)
        o_ref[...] = (x * r * w_ref[...]).astype(o_ref.dtype)

    return pl.pallas_call(
        body, grid=(rows // tm,),
        in_specs=[pl.BlockSpec((tm, d), lambda i: (i, 0)), pl.BlockSpec((1, d), lambda i: (0, 0))],
        out_specs=pl.BlockSpec((tm, d), lambda i: (i, 0)),
        out_shape=jax.ShapeDtypeStruct((rows, d), BF16), name=name,
        compiler_params=_params("parallel"))(h, w.reshape(1, d))


def _rms_bwd(dy, h, w, resid, name):
    rows, d = h.shape
    tm = _row_tile(rows)

    def body(dy_ref, h_ref, w_ref, r_ref, dh_ref, dw_ref):
        i = pl.program_id(0)
        x = h_ref[...]
        r = lax.rsqrt(jnp.mean(x * x, axis=-1, keepdims=True) + RMS_EPS)
        xh = x * r
        dyv = dy_ref[...]
        dxh = dyv * w_ref[...]
        dh_ref[...] = r_ref[...] + r * (dxh - xh * jnp.mean(dxh * xh, axis=-1, keepdims=True))
        part = jnp.sum(dyv * xh, axis=0, keepdims=True)

        @pl.when(i == 0)
        def _():
            dw_ref[...] = part

        @pl.when(i > 0)
        def _():
            dw_ref[...] += part

    blk = pl.BlockSpec((tm, d), lambda i: (i, 0))
    vec = pl.BlockSpec((1, d), lambda i: (0, 0))
    return pl.pallas_call(
        body, grid=(rows // tm,), in_specs=[blk, blk, vec, blk], out_specs=[blk, vec],
        out_shape=[jax.ShapeDtypeStruct((rows, d), F32), jax.ShapeDtypeStruct((1, d), F32)],
        name=name, compiler_params=_params("arbitrary"))(dy, h, w.reshape(1, d), resid)


def _final_loss(h, w, tgt, seq, name):
    rows, d = h.shape
    tm = _row_tile(rows)

    def body(h_ref, w_ref, t_ref, dh_ref, dw_ref, loss_ref):
        i = pl.program_id(0)
        r_idx = i * tm + _iota((tm, 1), 0)
        m = ((r_idx >= CHUNK) & (r_idx < CHUNK + seq)).astype(F32)
        x = h_ref[...]
        wv = w_ref[...]
        r = lax.rsqrt(jnp.mean(x * x, axis=-1, keepdims=True) + RMS_EPS)
        xh = x * r
        err = (xh * wv - t_ref[...]) * m
        lpart = 0.5 * jnp.sum(jnp.mean(err * err, axis=-1, keepdims=True), axis=0, keepdims=True)
        dyv = err * (1.0 / d)
        dxh = dyv * wv
        dh_ref[...] = r * (dxh - xh * jnp.mean(dxh * xh, axis=-1, keepdims=True))
        part = jnp.sum(dyv * xh, axis=0, keepdims=True)

        @pl.when(i == 0)
        def _():
            dw_ref[...] = part
            loss_ref[...] = jnp.broadcast_to(lpart, loss_ref.shape)

        @pl.when(i > 0)
        def _():
            dw_ref[...] += part
            loss_ref[...] += jnp.broadcast_to(lpart, loss_ref.shape)

    blk = pl.BlockSpec((tm, d), lambda i: (i, 0))
    vec = pl.BlockSpec((1, d), lambda i: (0, 0))
    return pl.pallas_call(
        body, grid=(rows // tm,), in_specs=[blk, vec, blk],
        out_specs=[blk, vec, pl.BlockSpec((1, LANES), lambda i: (0, 0))],
        out_shape=[jax.ShapeDtypeStruct((rows, d), F32), jax.ShapeDtypeStruct((1, d), F32),
                   jax.ShapeDtypeStruct((1, LANES), F32)],
        name=name, compiler_params=_params("arbitrary"))(h, w.reshape(1, d), tgt)


def _mm(a, b, *, mode, name, out_dtype=F32, resid=None, row_cap=768, col_cap=1536):
    if mode == "tn":
        m, k = a.shape
        n = b.shape[1]
        tm, tn = _row_tile(m, row_cap), _col_tile(n, col_cap)

        def body_tn(a_ref, b_ref, o_ref):
            i = pl.program_id(1)
            part = _tn(a_ref[...], b_ref[...])

            @pl.when(i == 0)
            def _():
                o_ref[...] = part

            @pl.when(i > 0)
            def _():
                o_ref[...] += part

        return pl.pallas_call(
            body_tn, grid=(n // tn, m // tm),
            in_specs=[pl.BlockSpec((tm, k), lambda j, i: (i, 0)),
                      pl.BlockSpec((tm, tn), lambda j, i: (i, j))],
            out_specs=pl.BlockSpec((k, tn), lambda j, i: (0, j)),
            out_shape=jax.ShapeDtypeStruct((k, n), F32), name=name,
            compiler_params=_params("parallel", "arbitrary"))(a, b)

    m, ka = a.shape
    n = b.shape[1] if mode == "nn" else b.shape[0]
    tm, tn = _row_tile(m, row_cap), _col_tile(n, col_cap)
    has_resid = resid is not None

    def body(*refs):
        if has_resid:
            a_ref, b_ref, r_ref, o_ref = refs
        else:
            a_ref, b_ref, o_ref = refs
        acc = _nn(a_ref[...], b_ref[...]) if mode == "nn" else _nt(a_ref[...], b_ref[...])
        if has_resid:
            acc = acc + r_ref[...]
        o_ref[...] = acc.astype(o_ref.dtype)

    b_spec = (pl.BlockSpec((b.shape[0], tn), lambda j, i: (0, j)) if mode == "nn"
              else pl.BlockSpec((tn, b.shape[1]), lambda j, i: (j, 0)))
    o_spec = pl.BlockSpec((tm, tn), lambda j, i: (i, j))
    in_specs = [pl.BlockSpec((tm, ka), lambda j, i: (i, 0)), b_spec]
    args = [a, b]
    if has_resid:
        in_specs.append(o_spec)
        args.append(resid)
    return pl.pallas_call(
        body, grid=(n // tn, m // tm), in_specs=in_specs, out_specs=o_spec,
        out_shape=jax.ShapeDtypeStruct((m, n), out_dtype), name=name,
        compiler_params=_params("parallel", "parallel"))(*args)


N_SHARD = 4


def _gmm(name, grid, args, in_specs, out_specs, out_shape, fn, red_axis=None, init_arg=None, aliases=None):
    n_in = len(args)

    def body(*refs):
        ins, outs = refs[:n_in], refs[n_in:]
        parts = fn(*ins)
        if red_axis is None:
            for o_ref, p in zip(outs, parts):
                o_ref[...] = p.astype(o_ref.dtype)
            return
        k = pl.program_id(red_axis)

        @pl.when(k == 0)
        def _():
            for idx, (o_ref, p) in enumerate(zip(outs, parts)):
                o_ref[...] = p + ins[init_arg][...] if (idx == 0 and init_arg is not None) else p

        @pl.when(k > 0)
        def _():
            for o_ref, p in zip(outs, parts):
                o_ref[...] += p

    sem = tuple("arbitrary" if ax == red_axis else "parallel" for ax in range(len(grid)))
    return pl.pallas_call(body, grid=grid, in_specs=in_specs, out_specs=out_specs, out_shape=out_shape,
                          name=name, input_output_aliases=aliases or {}, compiler_params=_params(*sem))(*args)


def _mm_cols(a, ws, name):
    m, k = a.shape
    n = ws.shape[2]
    tm = _row_tile(m)
    return _gmm(name, (N_SHARD, m // tm), [a, ws],
                [pl.BlockSpec((tm, k), lambda j, i: (i, 0)), pl.BlockSpec((None, k, n), lambda j, i: (j, 0, 0))],
                pl.BlockSpec((tm, n), lambda j, i: (i, j)), jax.ShapeDtypeStruct((m, N_SHARD * n), F32),
                lambda a_ref, w_ref: (_nn(a_ref[...], w_ref[...]),))


def _mm_cols_t(d, ws, name):
    m = d.shape[0]
    _, k, n = ws.shape
    tm = _row_tile(m)
    return _gmm(name, (m // tm, N_SHARD), [d, ws],
                [pl.BlockSpec((tm, n), lambda i, j: (i, j)), pl.BlockSpec((None, k, n), lambda i, j: (j, 0, 0))],
                pl.BlockSpec((tm, k), lambda i, j: (i, 0)), jax.ShapeDtypeStruct((m, k), F32),
                lambda d_ref, w_ref: (_nt(d_ref[...], w_ref[...]),), red_axis=1)


def _mm_cols_grad(a, d, name):
    m, k = a.shape
    n = d.shape[1] // N_SHARD
    tm = _row_tile(m)
    return _gmm(name, (N_SHARD, m // tm), [a, d],
                [pl.BlockSpec((tm, k), lambda j, i: (i, 0)), pl.BlockSpec((tm, n), lambda j, i: (i, j))],
                pl.BlockSpec((None, k, n), lambda j, i: (j, 0, 0)), jax.ShapeDtypeStruct((N_SHARD, k, n), F32),
                lambda a_ref, d_ref: (_tn(a_ref[...], d_ref[...]),), red_axis=1)


def _ffn_up(hn, wg, wu, layer, name):
    m, k = hn.shape
    n = wg.shape[3]
    tm = _row_tile(m)

    def fn(a_ref, wg_ref, wu_ref):
        a = a_ref[...]
        g = _nn(a, wg_ref[...])
        u = _nn(a, wu_ref[...])
        return g, u, g * jax.nn.sigmoid(g) * u

    w_spec = pl.BlockSpec((None, None, k, n), lambda j, i: (j, layer, 0, 0))
    o_spec = pl.BlockSpec((None, tm, n), lambda j, i: (j, i, 0))
    out = jax.ShapeDtypeStruct((N_SHARD, m, n), BF16)
    return _gmm(name, (N_SHARD, m // tm), [hn, wg, wu],
                [pl.BlockSpec((tm, k), lambda j, i: (i, 0)), w_spec, w_spec],
                [o_spec, o_spec, o_spec], [out, out, out], fn)


def _ffn_down(act, wd, resid, layer, name):
    _, m, n = act.shape
    d = wd.shape[3]
    tm = _row_tile(m)
    row = pl.BlockSpec((tm, d), lambda i, j: (i, 0))
    return _gmm(name, (m // tm, N_SHARD), [act, wd, resid],
                [pl.BlockSpec((None, tm, n), lambda i, j: (j, i, 0)),
                 pl.BlockSpec((None, None, n, d), lambda i, j: (j, layer, 0, 0)), row],
                row, jax.ShapeDtypeStruct((m, d), F32),
                lambda a_ref, w_ref, r_ref: (_nn(a_ref[...], w_ref[...]),), red_axis=1, init_arg=2)


def _ffn_down_bwd(dh, wd, g, u, layer, name):
    m, d = dh.shape
    n = wd.shape[2]
    tm = _row_tile(m)

    def fn(dh_ref, wd_ref, g_ref, u_ref):
        dact = _nt(dh_ref[...], wd_ref[...])
        gv = g_ref[...].astype(F32)
        uv = u_ref[...].astype(F32)
        sg = jax.nn.sigmoid(gv)
        return dact * uv * (sg * (1.0 + gv * (1.0 - sg))), dact * gv * sg

    o_spec = pl.BlockSpec((None, tm, n), lambda j, i: (j, i, 0))
    out = jax.ShapeDtypeStruct((N_SHARD, m, n), BF16)
    return _gmm(name, (N_SHARD, m // tm), [dh, wd, g, u],
                [pl.BlockSpec((tm, d), lambda j, i: (i, 0)),
                 pl.BlockSpec((None, None, n, d), lambda j, i: (j, layer, 0, 0)), o_spec, o_spec],
                [o_spec, o_spec], [out, out], fn)


def _ffn_up_bwd(dg, du, wg, wu, layer, name):
    _, m, n = dg.shape
    k = wg.shape[2]
    tm = _row_tile(m)
    d_spec = pl.BlockSpec((None, tm, n), lambda i, j: (j, i, 0))
    w_spec = pl.BlockSpec((None, None, k, n), lambda i, j: (j, layer, 0, 0))
    return _gmm(name, (m // tm, N_SHARD), [dg, du, wg, wu], [d_spec, d_spec, w_spec, w_spec],
                pl.BlockSpec((tm, k), lambda i, j: (i, 0)), jax.ShapeDtypeStruct((m, k), F32),
                lambda dg_ref, du_ref, wg_ref, wu_ref: (
                    _nt(dg_ref[...], wg_ref[...]) + _nt(du_ref[...], wu_ref[...]),), red_axis=1)


def _ffn_wgrad(lhs, rhs_list, layer, layers, prev, lhs_sharded, name):
    if lhs_sharded:
        _, m, k = lhs.shape
        n = rhs_list[0].shape[1]
    else:
        m, k = lhs.shape
        n = rhs_list[0].shape[2]
    tm = _row_tile(m)
    sh = pl.BlockSpec((None, tm, k if lhs_sharded else n), lambda j, i: (j, i, 0))
    fl = pl.BlockSpec((tm, n if lhs_sharded else k), lambda j, i: (i, 0))
    n_out = len(rhs_list)
    args = [lhs] + list(rhs_list)
    in_specs = [sh if lhs_sharded else fl] + [fl if lhs_sharded else sh] * n_out
    aliases = None
    if prev is not None:
        aliases = {len(args) + t: t for t in range(n_out)}
        args = args + list(prev)
        in_specs = in_specs + [ANY] * n_out

    def fn(l_ref, *rest):
        lv = l_ref[...]
        return tuple(_tn(lv, r_ref[...]) for r_ref in rest[:n_out])

    o_spec = pl.BlockSpec((None, None, k, n), lambda j, i: (j, layer, 0, 0))
    out = jax.ShapeDtypeStruct((N_SHARD, layers, k, n), F32)
    return _gmm(name, (N_SHARD, m // tm), args, in_specs, [o_spec] * n_out, [out] * n_out, fn,
                red_axis=1, aliases=aliases)


def _ret_consts():
    log_gamma = jnp.log1p(-jnp.exp2(-5.0 - jnp.arange(RET_HEADS, dtype=F32)))
    idx = jnp.arange(CHUNK, dtype=F32)
    rel = idx[:, None] - idx[None, :]
    dmask = jnp.where((rel >= 0)[None], jnp.exp(log_gamma[:, None, None] * jnp.maximum(rel, 0.0)), 0.0)
    xi = jnp.exp(log_gamma[:, None] * (idx[None, :] + 1.0))[:, :, None]
    zeta = jnp.exp(log_gamma[:, None] * (CHUNK - 1.0 - idx[None, :]))[:, :, None]
    gamma_c = jnp.exp(log_gamma * CHUNK)
    wide = (RET_HEADS, CHUNK, RET_DK)
    return dmask, jnp.broadcast_to(xi, wide), jnp.broadcast_to(zeta, wide), gamma_c


def _rope_tables(rows):
    half = RET_DK // 2
    inv_freq = ROPE_BASE ** (-jnp.arange(half, dtype=F32) / half)
    pos = (jnp.arange(rows) - PAD).astype(F32)
    ang = pos[:, None] * inv_freq[None, :]
    return jnp.cos(ang), jnp.sin(ang)


def _ret_specs(order):
    return [pl.BlockSpec((CHUNK, RET_QK), lambda n: (order(n), 0)),
            pl.BlockSpec((CHUNK, RET_QK), lambda n: (order(n), 1)),
            pl.BlockSpec((CHUNK, RET_V), lambda n: (order(n), 1)),
            pl.BlockSpec((CHUNK, RET_V), lambda n: (order(n), 2))]


def _ret_const_specs():
    return [pl.BlockSpec((RET_HEADS, CHUNK, CHUNK), lambda n: (0, 0, 0)),
            pl.BlockSpec((RET_HEADS, CHUNK, RET_DK), lambda n: (0, 0, 0)),
            pl.BlockSpec((RET_HEADS, CHUNK, RET_DK), lambda n: (0, 0, 0)),
            pl.BlockSpec((1, RET_DV), lambda n: (0, 0))]


def _ret_fwd(proj, cos, sin, consts, gn_w, seq):
    rows = proj.shape[0]
    nc = rows // CHUNK
    dmask, xi, zeta, gamma_c = consts

    def body(gam_ref, q_ref, k_ref, v_ref, g_ref, cos_ref, sin_ref, dm_ref, xi_ref, ze_ref, gn_ref,
             o_ref, y_ref, ss_ref, s_ref):
        n = pl.program_id(0)

        @pl.when(n == 0)
        def _():
            s_ref[...] = jnp.zeros_like(s_ref)

        cs, sn = cos_ref[...], sin_ref[...]
        kscale = _valid_rows(n * CHUNK, CHUNK, seq) * (RET_DK ** -0.5)
        gn = gn_ref[...]
        for h in range(RET_HEADS):
            qk_cols = slice(h * RET_DK, (h + 1) * RET_DK)
            v_cols = slice(h * RET_DV, (h + 1) * RET_DV)
            qr = _rope(q_ref[:, qk_cols], cs, sn)
            kr = _rope(k_ref[:, qk_cols], cs, sn) * kscale
            v = v_ref[:, v_cols]
            s = s_ref[h]
            ss_ref[0, h] = s.astype(ss_ref.dtype)
            scores = _nt(qr, kr) * dm_ref[h]
            o = _nn(scores, v) + _nn(qr * xi_ref[h], s)
            s_ref[h] = gam_ref[h] * s + _tn(kr * ze_ref[h], v)
            o_ref[:, v_cols] = o
            y_ref[:, v_cols] = _gated_norm(o, g_ref[:, v_cols], gn).astype(y_ref.dtype)

    fwd = lambda n: n
    row128 = pl.BlockSpec((CHUNK, RET_DK // 2), lambda n: (n, 0))
    row_v = pl.BlockSpec((CHUNK, RET_V), lambda n: (n, 0))
    return pl.pallas_call(
        body, grid=(nc,),
        in_specs=[pl.BlockSpec(memory_space=pltpu.SMEM)] + _ret_specs(fwd) + [row128, row128]
        + _ret_const_specs(),
        out_specs=[row_v, row_v,
                   pl.BlockSpec((1, RET_HEADS, RET_DK, RET_DV), lambda n: (n, 0, 0, 0))],
        out_shape=[jax.ShapeDtypeStruct((rows, RET_V), F32), jax.ShapeDtypeStruct((rows, RET_V), BF16),
                   jax.ShapeDtypeStruct((nc, RET_HEADS, RET_DK, RET_DV), BF16)],
        scratch_shapes=[pltpu.VMEM((RET_HEADS, RET_DK, RET_DV), F32)],
        name="ret_fwd", compiler_params=_params("arbitrary"))(
            gamma_c, proj, proj, proj, proj, cos, sin, dmask, xi, zeta, gn_w.reshape(1, RET_DV))


def _ret_bwd(proj, o, dy, states, cos, sin, consts, gn_w, seq):
    rows = proj.shape[0]
    nc = rows // CHUNK
    dmask, xi, zeta, gamma_c = consts

    def body(gam_ref, q_ref, k_ref, v_ref, g_ref, o_ref, dy_ref, ss_ref, cos_ref, sin_ref,
             dm_ref, xi_ref, ze_ref, gn_ref, dp_ref, dgn_ref, ds_ref):
        n = pl.program_id(0)

        @pl.when(n == 0)
        def _():
            ds_ref[...] = jnp.zeros_like(ds_ref)
            dgn_ref[...] = jnp.zeros_like(dgn_ref)

        cs, sn = cos_ref[...], sin_ref[...]
        kscale = _valid_rows((nc - 1 - n) * CHUNK, CHUNK, seq) * (RET_DK ** -0.5)
        gn = gn_ref[...]
        dgn = jnp.zeros((1, RET_DV), F32)
        for h in range(RET_HEADS):
            qk_cols = slice(h * RET_DK, (h + 1) * RET_DK)
            v_cols = slice(h * RET_DV, (h + 1) * RET_DV)
            qr = _rope(q_ref[:, qk_cols], cs, sn)
            kr = _rope(k_ref[:, qk_cols], cs, sn) * kscale
            v = v_ref[:, v_cols]
            s = ss_ref[0, h]
            do, dgate, dw = _gated_norm_bwd(dy_ref[:, v_cols], o_ref[:, v_cols], g_ref[:, v_cols], gn)
            dgn = dgn + dw
            ds = ds_ref[h]
            dm = dm_ref[h]
            scores = _nt(qr, kr) * dm
            dscores = _nt(do, v) * dm
            dv = _tn(scores, do) + _nn(kr * ze_ref[h], ds)
            dqr = _nn(dscores, kr) + _nt(do, s) * xi_ref[h]
            dkr = _tn(dscores, qr) + _nt(v, ds) * ze_ref[h]
            ds_ref[h] = gam_ref[h] * ds + _tn(qr * xi_ref[h], do)
            dp_ref[:, qk_cols] = _rope_bwd(dqr, cs, sn).astype(dp_ref.dtype)
            dp_ref[:, RET_QK + h * RET_DK:RET_QK + (h + 1) * RET_DK] = (
                _rope_bwd(dkr * kscale, cs, sn).astype(dp_ref.dtype))
            dp_ref[:, 2 * RET_QK + h * RET_DV:2 * RET_QK + (h + 1) * RET_DV] = dv.astype(dp_ref.dtype)
            dp_ref[:, 2 * RET_QK + RET_V + h * RET_DV:2 * RET_QK + RET_V + (h + 1) * RET_DV] = (
                dgate.astype(dp_ref.dtype))
        dgn_ref[...] += dgn

    rev = lambda n: nc - 1 - n
    row128 = pl.BlockSpec((CHUNK, RET_DK // 2), lambda n: (rev(n), 0))
    row_v = pl.BlockSpec((CHUNK, RET_V), lambda n: (rev(n), 0))
    return pl.pallas_call(
        body, grid=(nc,),
        in_specs=[pl.BlockSpec(memory_space=pltpu.SMEM)] + _ret_specs(rev) + [
            row_v, row_v, pl.BlockSpec((1, RET_HEADS, RET_DK, RET_DV), lambda n: (rev(n), 0, 0, 0)),
            row128, row128] + _ret_const_specs(),
        out_specs=[pl.BlockSpec((CHUNK, RET_IN), lambda n: (rev(n), 0)),
                   pl.BlockSpec((1, RET_DV), lambda n: (0, 0))],
        out_shape=[jax.ShapeDtypeStruct((rows, RET_IN), BF16), jax.ShapeDtypeStruct((1, RET_DV), F32)],
        scratch_shapes=[pltpu.VMEM((RET_HEADS, RET_DK, RET_DV), F32)],
        name="ret_bwd", compiler_params=_params("arbitrary"))(
            gamma_c, proj, proj, proj, proj, o, dy, states, cos, sin, dmask, xi, zeta,
            gn_w.reshape(1, RET_DV))


GATE_COL = DN_CONV_CH // DN_V
BA_COL = (DN_CONV_CH + DN_V) // LANES
BETA_LANE, DECAY_LANE = 0, DN_HEADS
INV_SHIFT = 4
INV_SQUARINGS = INV_SHIFT - 1
assert CHUNK == 4 << INV_SHIFT


def _dn_in_specs(order):
    return [pl.BlockSpec((CHUNK, DN_CONV_CH), lambda n: (order(n), 0)),
            pl.BlockSpec((8, DN_CONV_CH), lambda n: (jnp.maximum(order(n) * (CHUNK // 8) - 1, 0), 0)),
            pl.BlockSpec((CHUNK, DN_V), lambda n: (order(n), GATE_COL)),
            pl.BlockSpec((CHUNK, LANES), lambda n: (order(n), BA_COL)),
            pl.BlockSpec((CONV_K, 1, DN_CONV_CH), lambda n: (0, 0, 0)),
            pl.BlockSpec((1, LANES), lambda n: (0, 0)),
            pl.BlockSpec((1, LANES), lambda n: (0, 0)),
            pl.BlockSpec((1, DN_DV), lambda n: (0, 0))]


def _dn_front(c, seq, x_ref, halo_ref, ba_ref, cw_ref, al_ref, dt_ref):
    valid = _valid_rows(c * CHUNK, CHUNK, seq)
    xin = x_ref[...] * valid
    halo = halo_ref[...] * _valid_rows(c * CHUNK - 8, 8, seq)
    x_sh = [xin] + [_shift_down(xin, halo, k) for k in range(1, CONV_K)]
    yc = x_sh[0] * cw_ref[CONV_K - 1]
    for k in range(1, CONV_K):
        yc = yc + x_sh[k] * cw_ref[CONV_K - 1 - k]
    sgc = jax.nn.sigmoid(yc)
    ba = ba_ref[...]
    sig = jax.nn.sigmoid(ba)
    beta = sig * valid
    z = ba + dt_ref[...]
    eal = jnp.exp(al_ref[...])
    g = -eal * _softplus(z) * valid
    ri, ci = _iota((CHUNK, CHUNK), 0), _iota((CHUNK, CHUNK), 1)
    lower = (ri >= ci).astype(F32)
    upper = (ri <= ci).astype(F32)
    eye = (ri == ci).astype(F32)
    gam = _nn(lower, g, hi=True)
    gam_t = _tn(g, upper, hi=True)
    return dict(valid=valid, x_sh=x_sh, yc=yc, sgc=sgc, act=yc * sgc, sig=sig, beta=beta, z=z,
                eal=eal, g=g, gam=gam, gam_t=gam_t, ri=ri, ci=ci, upper=upper, eye=eye)


def _dn_head(f, h):
    act = f["act"]
    q_raw = act[:, h * DN_DK:(h + 1) * DN_DK]
    k_raw = act[:, DN_QK + h * DN_DK:DN_QK + (h + 1) * DN_DK]
    v = act[:, 2 * DN_QK + h * DN_DV:2 * DN_QK + (h + 1) * DN_DV]
    rq = lax.rsqrt(jnp.sum(q_raw * q_raw, axis=-1, keepdims=True) + RMS_EPS)
    rk = lax.rsqrt(jnp.sum(k_raw * k_raw, axis=-1, keepdims=True) + RMS_EPS)
    qh = q_raw * rq
    kn = k_raw * rk
    gam_c = _col(f["gam"], DECAY_LANE + h)
    gam_r = _row(f["gam_t"], DECAY_LANE + h)
    bc = _col(f["beta"], BETA_LANE + h)
    diff = gam_c - gam_r
    decay = jnp.where(f["ri"] >= f["ci"], jnp.exp(jnp.minimum(diff, 0.0)), 0.0)
    glast = jnp.sum(gam_r * (_iota((1, CHUNK), 1) == CHUNK - 1).astype(F32), axis=1, keepdims=True)
    return dict(rq=rq, rk=rk, qh=qh, qn=qh * (DN_DK ** -0.5), kn=kn, v=v, gam_c=gam_c, gam_r=gam_r,
                bc=bc, diff=diff, decay=decay, egam=jnp.exp(gam_c), glast=glast,
                eglast=jnp.exp(glast), ekd=jnp.exp(glast - gam_c))


def _dn_fwd(proj, conv_w, alog, dtb, norm_w, seq):
    rows = proj.shape[0]
    nc = rows // CHUNK

    def body(x_ref, halo_ref, gate_ref, ba_ref, cw_ref, al_ref, dt_ref, nw_ref,
             o_ref, y_ref, ss_ref, t_ref, s_ref):
        n = pl.program_id(0)

        @pl.when(n == 0)
        def _():
            s_ref[...] = jnp.zeros_like(s_ref)

        f = _dn_front(n, seq, x_ref, halo_ref, ba_ref, cw_ref, al_ref, dt_ref)
        ri, ci = f["ri"], f["ci"]
        eye = f["eye"]
        diag_m = (jnp.right_shift(ri, INV_SHIFT) == jnp.right_shift(ci, INV_SHIFT)).astype(F32)
        half_m = (jnp.right_shift(ri, INV_SHIFT + 1) == jnp.right_shift(ci, INV_SHIFT + 1)).astype(F32)
        nw = nw_ref[...]
        for h in range(DN_HEADS):
            hd = _dn_head(f, h)
            v_cols = slice(h * DN_DV, (h + 1) * DN_DV)
            kk = _nt(hd["kn"], hd["kn"])
            a = jnp.where(ri > ci, hd["bc"] * kk * hd["decay"], 0.0)
            ad = a * diag_m
            t = eye - ad
            b = ad
            for _ in range(INV_SQUARINGS):
                b = _nn(b, b, hi=True)
                t = t + _nn(t, b, hi=True)
            for off in (a * (half_m - diag_m), a * (1.0 - half_m)):
                t = t - _nn(t, _nn(off, t, hi=True), hi=True)
            t_ref[0, h] = t
            s = s_ref[h]
            ss_ref[0, h] = s
            u = _nn(t, hd["v"] * hd["bc"], hi=True)
            w = _nn(t, hd["kn"] * (hd["bc"] * hd["egam"]), hi=True)
            v_new = u - _nn(w, s)
            qk = _nt(hd["qn"], hd["kn"]) * hd["decay"]
            o = _nn(hd["qn"] * hd["egam"], s) + _nn(qk, v_new)
            s_ref[h] = s * hd["eglast"] + _tn(hd["kn"] * hd["ekd"], v_new)
            o_ref[:, v_cols] = o
            y_ref[:, v_cols] = _gated_norm(o, gate_ref[:, v_cols], nw).astype(y_ref.dtype)

    fwd = lambda n: n
    row_v = pl.BlockSpec((CHUNK, DN_V), lambda n: (n, 0))
    return pl.pallas_call(
        body, grid=(nc,), in_specs=_dn_in_specs(fwd),
        out_specs=[row_v, row_v,
                   pl.BlockSpec((1, DN_HEADS, DN_DK, DN_DV), lambda n: (n, 0, 0, 0)),
                   pl.BlockSpec((1, DN_HEADS, CHUNK, CHUNK), lambda n: (n, 0, 0, 0))],
        out_shape=[jax.ShapeDtypeStruct((rows, DN_V), F32), jax.ShapeDtypeStruct((rows, DN_V), BF16),
                   jax.ShapeDtypeStruct((nc, DN_HEADS, DN_DK, DN_DV), F32),
                   jax.ShapeDtypeStruct((nc, DN_HEADS, CHUNK, CHUNK), F32)],
        scratch_shapes=[pltpu.VMEM((DN_HEADS, DN_DK, DN_DV), F32)],
        name="dn_fwd", compiler_params=_params("arbitrary"))(
            proj, proj, proj, proj, conv_w, alog, dtb, norm_w.reshape(1, DN_DV))


def _dn_bwd(proj, o, dy, states, tinv, conv_w, alog, dtb, norm_w, seq):
    rows = proj.shape[0]
    nc = rows // CHUNK

    def body(x_ref, halo_ref, gate_ref, ba_ref, cw_ref, al_ref, dt_ref, nw_ref,
             o_ref, dy_ref, ss_ref, t_ref,
             dp_ref, dcw_ref, dal_ref, ddt_ref, dnw_ref, ds_ref, nxt_ref):
        n = pl.program_id(0)

        @pl.when(n == 0)
        def _():
            ds_ref[...] = jnp.zeros_like(ds_ref)
            nxt_ref[...] = jnp.zeros_like(nxt_ref)
            dcw_ref[...] = jnp.zeros_like(dcw_ref)
            dal_ref[...] = jnp.zeros_like(dal_ref)
            ddt_ref[...] = jnp.zeros_like(ddt_ref)
            dnw_ref[...] = jnp.zeros_like(dnw_ref)

        f = _dn_front(nc - 1 - n, seq, x_ref, halo_ref, ba_ref, cw_ref, al_ref, dt_ref)
        ri, ci = f["ri"], f["ci"]
        strict = (ri > ci).astype(F32)
        nw = nw_ref[...]
        lane128 = _iota((1, LANES), 1)
        row128 = _iota((LANES, 1), 0)
        dgam_col = jnp.zeros((CHUNK, LANES), F32)
        dgam_row = jnp.zeros((LANES, CHUNK), F32)
        dbeta = jnp.zeros((CHUNK, LANES), F32)
        dnw = jnp.zeros((1, DN_DV), F32)
        dq_parts, dk_parts, dv_parts = [], [], []
        for h in range(DN_HEADS):
            hd = _dn_head(f, h)
            qn, kn, v, bc, egam, decay = hd["qn"], hd["kn"], hd["v"], hd["bc"], hd["egam"], hd["decay"]
            v_cols = slice(h * DN_DV, (h + 1) * DN_DV)
            t = t_ref[0, h]
            s = ss_ref[0, h]
            kk = _nt(kn, kn)
            p = _nt(qn, kn)
            qk = p * decay
            rhs_w = kn * (bc * egam)
            u = _nn(t, v * bc, hi=True)
            w = _nn(t, rhs_w, hi=True)
            v_new = u - _nn(w, s)
            qg = qn * egam
            kd = kn * hd["ekd"]
            do, dgate, dw_n = _gated_norm_bwd(dy_ref[:, v_cols], o_ref[:, v_cols], gate_ref[:, v_cols], nw)
            dnw = dnw + dw_n
            ds = ds_ref[h]
            dv_new = _tn(qk, do) + _nn(kd, ds)
            m = _nt(do, v_new)
            dp = m * decay
            dqg = _nt(do, s)
            dqn = dqg * egam + _nn(dp, kn)
            dgc = jnp.sum(dqg * qg, axis=1, keepdims=True)
            dkn = _tn(dp, qn)
            dkd = _nt(v_new, ds)
            dkn = dkn + dkd * hd["ekd"]
            t1 = jnp.sum(dkd * kd, axis=1, keepdims=True)
            dgc = dgc - t1
            dglast = (jnp.sum(t1, axis=0, keepdims=True)
                      + jnp.sum(jnp.sum(ds * s, axis=1, keepdims=True), axis=0, keepdims=True) * hd["eglast"])
            ds_ref[h] = ds * hd["eglast"] + _tn(qg, do) - _tn(w, dv_new)
            dw_ = -_nt(dv_new, s)
            dru = _tn(t, dv_new, hi=True)
            drw = _tn(t, dw_, hi=True)
            da = -(_nt(dru, u) + _nt(drw, w)) * strict
            dv_parts.append(dru * bc)
            dbc = (jnp.sum(dru * v, axis=1, keepdims=True)
                   + jnp.sum(drw * kn, axis=1, keepdims=True) * egam
                   + jnp.sum(da * kk * decay, axis=1, keepdims=True))
            dkn = dkn + drw * (bc * egam)
            dgc = dgc + jnp.sum(drw * rhs_w, axis=1, keepdims=True)
            dkk = da * (bc * decay)
            e = (m * p + da * (bc * kk)) * decay
            dkn = dkn + _nn(dkk, kn) + _tn(dkk, kn)
            dgc = dgc + jnp.sum(e, axis=1, keepdims=True)
            dgr = -jnp.sum(e, axis=0, keepdims=True)
            dgc = dgc + jnp.where(_iota((CHUNK, 1), 0) == CHUNK - 1, dglast, 0.0)
            qh = hd["qh"]
            dq_parts.append(((DN_DK ** -0.5) * hd["rq"])
                            * (dqn - qh * jnp.sum(dqn * qh, axis=1, keepdims=True)))
            dk_parts.append(hd["rk"] * (dkn - kn * jnp.sum(dkn * kn, axis=1, keepdims=True)))
            dgam_col = dgam_col + dgc * (lane128 == DECAY_LANE + h).astype(F32)
            dbeta = dbeta + dbc * (lane128 == BETA_LANE + h).astype(F32)
            dgam_row = dgam_row + (row128 == DECAY_LANE + h).astype(F32) * dgr
            dp_ref[:, DN_CONV_CH + h * DN_DV:DN_CONV_CH + (h + 1) * DN_DV] = dgate.astype(dp_ref.dtype)
        dnw_ref[...] += dnw
        dgam = dgam_col + _nt(f["eye"], dgam_row, hi=True)
        dg = _nn(f["upper"], dgam, hi=True)
        d_a = dg * (-f["eal"]) * jax.nn.sigmoid(f["z"]) * f["valid"]
        dal_ref[...] += jnp.sum(dg * f["g"], axis=0, keepdims=True)
        ddt_ref[...] += jnp.sum(d_a, axis=0, keepdims=True)
        d_b = dbeta * f["valid"] * f["sig"] * (1.0 - f["sig"])
        dp_ref[:, DN_CONV_CH + DN_V:] = (d_a + d_b).astype(dp_ref.dtype)
        dact = jnp.concatenate(dq_parts + dk_parts + dv_parts, axis=1)
        yc, sgc = f["yc"], f["sgc"]
        dyc = dact * (sgc * (1.0 + yc * (1.0 - sgc)))
        for k in range(CONV_K):
            dcw_ref[k] += jnp.sum(dyc * f["x_sh"][CONV_K - 1 - k], axis=0, keepdims=True)
        nxt = nxt_ref[...]
        dx = dyc * cw_ref[CONV_K - 1]
        for j in range(1, CONV_K):
            dx = dx + _shift_up(dyc, nxt, j) * cw_ref[CONV_K - 1 - j]
        nxt_ref[...] = dyc[0:8]
        dp_ref[:, :DN_CONV_CH] = (dx * f["valid"]).astype(dp_ref.dtype)

    rev = lambda n: nc - 1 - n
    row_v = pl.BlockSpec((CHUNK, DN_V), lambda n: (rev(n), 0))
    vec = pl.BlockSpec((1, LANES), lambda n: (0, 0))
    return pl.pallas_call(
        body, grid=(nc,),
        in_specs=_dn_in_specs(rev) + [
            row_v, row_v,
            pl.BlockSpec((1, DN_HEADS, DN_DK, DN_DV), lambda n: (rev(n), 0, 0, 0)),
            pl.BlockSpec((1, DN_HEADS, CHUNK, CHUNK), lambda n: (rev(n), 0, 0, 0))],
        out_specs=[pl.BlockSpec((CHUNK, DN_IN_PAD), lambda n: (rev(n), 0)),
                   pl.BlockSpec((CONV_K, 1, DN_CONV_CH), lambda n: (0, 0, 0)), vec, vec,
                   pl.BlockSpec((1, DN_DV), lambda n: (0, 0))],
        out_shape=[jax.ShapeDtypeStruct((rows, DN_IN_PAD), BF16),
                   jax.ShapeDtypeStruct((CONV_K, 1, DN_CONV_CH), F32),
                   jax.ShapeDtypeStruct((1, LANES), F32), jax.ShapeDtypeStruct((1, LANES), F32),
                   jax.ShapeDtypeStruct((1, DN_DV), F32)],
        scratch_shapes=[pltpu.VMEM((DN_HEADS, DN_DK, DN_DV), F32), pltpu.VMEM((8, DN_CONV_CH), F32)],
        name="dn_bwd", compiler_params=_params("arbitrary"))(
            proj, proj, proj, proj, conv_w, alog, dtb, norm_w.reshape(1, DN_DV), o, dy, states, tinv)


def _local_step(x, tgt, wts):
    seq = x.shape[0]
    rows = -(-(seq + CHUNK) // ROW_ALIGN) * ROW_ALIGN
    tail = rows - seq - CHUNK
    h0 = jnp.concatenate([jnp.zeros((PAD, D_MODEL), F32), wts["meta_tokens"].astype(F32), x,
                          jnp.zeros((tail, D_MODEL), F32)], axis=0)
    tgt_p = jnp.concatenate([jnp.zeros((CHUNK, D_MODEL), F32), tgt, jnp.zeros((tail, D_MODEL), F32)],
                            axis=0)
    cos, sin = _rope_tables(rows)
    consts = _ret_consts()
    conv_w = wts["dn_conv_w"].reshape(CONV_K, 1, DN_CONV_CH)
    lane_pad = LANES - 2 * DN_HEADS
    alog = jnp.pad(wts["dn_a_log"].reshape(1, DN_HEADS), ((0, 0), (DECAY_LANE, lane_pad)))
    dtb = jnp.pad(wts["dn_dt_bias"].reshape(1, DN_HEADS), ((0, 0), (DECAY_LANE, lane_pad)))
    g = {}

    hn0 = _rms_fwd(h0, wts["mix_norm_w"][0], "rms_mix0")
    proj0 = _mm_cols(hn0, wts["ret_w_in"], "ret_in")
    o0, y0, st0 = _ret_fwd(proj0, cos, sin, consts, wts["ret_gn_w"], seq)
    h1 = _mm(y0, wts["ret_w_out"], mode="nn", name="ret_out", resid=h0)
    hn1 = _rms_fwd(h1, wts["ffn_norm_w"][0], "rms_ffn0")
    g0, u0, act0 = _ffn_up(hn1, wts["ffn_w_gate"], wts["ffn_w_up"], 0, "ffn_up0")
    h2 = _ffn_down(act0, wts["ffn_w_down"], h1, 0, "ffn_down0")
    hn2 = _rms_fwd(h2, wts["mix_norm_w"][1], "rms_mix1")
    proj1 = _mm(hn2, wts["dn_w_in"], mode="nn", name="dn_in")
    o1, y1, st1, tinv = _dn_fwd(proj1, conv_w, alog, dtb, wts["dn_norm_w"], seq)
    h3 = _mm(y1, wts["dn_w_out"], mode="nn", name="dn_out", resid=h2)
    hn3 = _rms_fwd(h3, wts["ffn_norm_w"][1], "rms_ffn1")
    g1, u1, act1 = _ffn_up(hn3, wts["ffn_w_gate"], wts["ffn_w_up"], 1, "ffn_up1")
    h4 = _ffn_down(act1, wts["ffn_w_down"], h3, 1, "ffn_down1")

    dh4, g["final_norm_w"], loss = _final_loss(h4, wts["final_norm_w"], tgt_p, seq, "final_loss")

    layers = wts["ffn_w_gate"].shape[1]

    def ffn_bwd(dh_out, h_mid, hn, gg, uu, act, layer, prev):
        tag = str(layer)
        dg, du = _ffn_down_bwd(dh_out, wts["ffn_w_down"], gg, uu, layer, "ffn_down_bwd" + tag)
        d_down = _ffn_wgrad(act, [dh_out], layer, layers, prev and prev[:1], True, "ffn_dwd" + tag)
        d_gu = _ffn_wgrad(hn, [dg, du], layer, layers, prev and prev[1:], False, "ffn_dwgu" + tag)
        dhn = _ffn_up_bwd(dg, du, wts["ffn_w_gate"], wts["ffn_w_up"], layer, "ffn_up_bwd" + tag)
        dh_mid, d_norm = _rms_bwd(dhn, h_mid, wts["ffn_norm_w"][layer], dh_out, "rms_ffn_bwd" + tag)
        return dh_mid, list(d_down) + list(d_gu), d_norm

    dh3, ffn_grads, dfn1 = ffn_bwd(dh4, h3, hn3, g1, u1, act1, 1, None)
    dy1 = _mm(dh3, wts["dn_w_out"], mode="nt", name="dn_out_bwd")
    g["dn_w_out"] = _mm(y1, dh3, mode="tn", name="dn_dwo")
    dproj1, dcw, dal, ddt, g["dn_norm_w"] = _dn_bwd(proj1, o1, dy1, st1, tinv, conv_w, alog, dtb,
                                                    wts["dn_norm_w"], seq)
    g["dn_w_in"] = _mm(hn2, dproj1, mode="tn", name="dn_dwi")
    dhn2 = _mm(dproj1, wts["dn_w_in"], mode="nt", name="dn_in_bwd", row_cap=256)
    dh2, dmn1 = _rms_bwd(dhn2, h2, wts["mix_norm_w"][1], dh3, "rms_mix_bwd1")
    g["dn_conv_w"] = dcw.reshape(CONV_K, DN_CONV_CH)
    g["dn_a_log"] = dal[0, DECAY_LANE:DECAY_LANE + DN_HEADS]
    g["dn_dt_bias"] = ddt[0, DECAY_LANE:DECAY_LANE + DN_HEADS]

    dh1, ffn_grads, dfn0 = ffn_bwd(dh2, h1, hn1, g0, u0, act0, 0, ffn_grads)
    dy0 = _mm(dh1, wts["ret_w_out"], mode="nt", name="ret_out_bwd")
    g["ret_w_out"] = _mm(y0, dh1, mode="tn", name="ret_dwo")
    dproj0, g["ret_gn_w"] = _ret_bwd(proj0, o0, dy0, st0, cos, sin, consts, wts["ret_gn_w"], seq)
    g["ret_w_in"] = _mm_cols_grad(hn0, dproj0, "ret_dwi")
    dhn0 = _mm_cols_t(dproj0, wts["ret_w_in"], "ret_in_bwd")
    dh0, dmn0 = _rms_bwd(dhn0, h0, wts["mix_norm_w"][0], dh1, "rms_mix_bwd0")

    g["ffn_w_down"], g["ffn_w_gate"], g["ffn_w_up"] = ffn_grads
    g["ffn_norm_w"] = jnp.concatenate([dfn0, dfn1], axis=0)
    g["mix_norm_w"] = jnp.concatenate([dmn0, dmn1], axis=0)
    g["meta_tokens"] = dh0[PAD:CHUNK]
    g["final_norm_w"] = g["final_norm_w"].reshape(D_MODEL)
    g["ret_gn_w"] = g["ret_gn_w"].reshape(RET_DV)
    g["dn_norm_w"] = g["dn_norm_w"].reshape(DN_DV)
    return loss, dh0, g


def _mesh_pos():
    return lax.axis_index("x"), lax.axis_index("y"), lax.axis_index("c")


def _other_chips(x, y):
    return [(1 - x, y), (x, 1 - y), (1 - x, 1 - y)]


def _remote(src, dst, send_sem, recv_sem, to):
    return pltpu.make_async_remote_copy(src_ref=src, dst_ref=dst, send_sem=send_sem, recv_sem=recv_sem,
                                        device_id=to, device_id_type=MESH)


GATHER_COPIES = 7


def _gather_weights(shards):
    nt = len(shards)

    def body(*refs):
        ins, outs = refs[:nt], refs[nt:2 * nt]
        send_sems, recv_sems = refs[2 * nt:]
        x, y, c = _mesh_pos()
        me = 2 * x + y
        chips = _other_chips(x, y)
        sibling = (x, y, 1 - c)

        def cp(t, k, src, dst, to):
            i = GATHER_COPIES * t + k
            return _remote(src, dst, send_sems.at[i], recv_sems.at[i], to)

        started = []
        for t in range(nt):
            started.append(cp(t, 0, ins[t], outs[t].at[me], sibling))
            for k, (px, py) in enumerate(chips):
                started.append(cp(t, 1 + k, ins[t].at[c], outs[t].at[me, c], (px, py, c)))
        for s in started:
            s.start()
        for t in range(nt):
            for k, (px, py) in enumerate(chips):
                landed = outs[t].at[2 * px + py, c]
                cp(t, 1 + k, ins[t].at[c], landed, (px, py, c)).wait_recv()
                fwd = cp(t, 4 + k, landed, landed, sibling)
                fwd.start()
                started.append(fwd)
        for t in range(nt):
            cp(t, 0, ins[t], outs[t].at[me], sibling).wait_recv()
            for k, (px, py) in enumerate(chips):
                theirs = outs[t].at[2 * px + py, 1 - c]
                cp(t, 4 + k, theirs, theirs, sibling).wait_recv()
        for s in started:
            s.wait_send()

    return pl.pallas_call(
        body, out_shape=[jax.ShapeDtypeStruct((N_SHARD,) + s.shape, s.dtype) for s in shards],
        in_specs=[ANY] * nt, out_specs=[ANY] * nt,
        scratch_shapes=[pltpu.SemaphoreType.DMA((GATHER_COPIES * nt,)),
                        pltpu.SemaphoreType.DMA((GATHER_COPIES * nt,))],
        name="gather_weights")(*shards)


def _gather_small(blk):
    r, wd = blk.shape

    def body(b_ref, out_ref, send_sems, recv_sems):
        x, y, c = _mesh_pos()
        chips = _other_chips(x, y)
        out_ref[2 * x + y] = b_ref[...]
        sends = [_remote(b_ref, out_ref.at[2 * x + y], send_sems.at[k], recv_sems.at[k], (px, py, c))
                 for k, (px, py) in enumerate(chips)]
        for cp in sends:
            cp.start()
        for k, (px, py) in enumerate(chips):
            _remote(b_ref, out_ref.at[2 * px + py], send_sems.at[k], recv_sems.at[k], (px, py, c)).wait_recv()
        for cp in sends:
            cp.wait_send()

    return pl.pallas_call(
        body, out_shape=jax.ShapeDtypeStruct((4, r, wd), blk.dtype), in_specs=[VMEM_SPEC], out_specs=VMEM_SPEC,
        scratch_shapes=[pltpu.SemaphoreType.DMA((3,)), pltpu.SemaphoreType.DMA((3,))],
        name="gather_small")(blk)


def _allreduce_small(blk):
    r, wd = blk.shape
    rels = [(dx, dy, dc) for dx in (0, 1) for dy in (0, 1) for dc in (0, 1) if dx or dy or dc]

    def body(b_ref, out_ref, buf_ref, send_sems, recv_sems):
        x, y, c = _mesh_pos()

        def peer(rel):
            dx, dy, dc = rel
            return (1 - x if dx else x, 1 - y if dy else y, 1 - c if dc else c)

        me = 4 * x + 2 * y + c
        buf_ref[me] = b_ref[...]
        sends = [_remote(b_ref, buf_ref.at[me], send_sems.at[k], recv_sems.at[k], peer(rel))
                 for k, rel in enumerate(rels)]
        for cp in sends:
            cp.start()
        for k, rel in enumerate(rels):
            px, py, pc = peer(rel)
            _remote(b_ref, buf_ref.at[4 * px + 2 * py + pc], send_sems.at[k], recv_sems.at[k],
                    (px, py, pc)).wait_recv()
        for cp in sends:
            cp.wait_send()
        acc = buf_ref[0]
        for d in range(1, 8):
            acc = acc + buf_ref[d]
        out_ref[...] = acc

    return pl.pallas_call(
        body, out_shape=jax.ShapeDtypeStruct((r, wd), blk.dtype), in_specs=[VMEM_SPEC], out_specs=VMEM_SPEC,
        scratch_shapes=[pltpu.VMEM((8, r, wd), blk.dtype), pltpu.SemaphoreType.DMA((7,)),
                        pltpu.SemaphoreType.DMA((7,))],
        name="allreduce_small")(blk)


def _rs_pair(gs):
    nt = len(gs)

    def body(*refs):
        ins, outs = refs[:nt], refs[nt:2 * nt]
        send_sems, recv_sems = refs[2 * nt:]
        x, y, c = _mesh_pos()
        cps = [_remote(ins[t].at[:, 1 - c], outs[t], send_sems.at[t], recv_sems.at[t], (x, y, 1 - c))
               for t in range(nt)]
        for cp in cps:
            cp.start()
        for cp in cps:
            cp.wait()

    return pl.pallas_call(
        body, out_shape=[jax.ShapeDtypeStruct(g.shape[:1] + g.shape[2:], g.dtype) for g in gs],
        in_specs=[ANY] * nt, out_specs=[ANY] * nt,
        scratch_shapes=[pltpu.SemaphoreType.DMA((nt,)), pltpu.SemaphoreType.DMA((nt,))], name="rs_pair")(*gs)


def _rs_tile(a, b):
    return _div_tile(a, 512 if b <= 1024 else 256, 16)


def _rs_pair_add(g, a, idx, name):
    _, _, rows, cols = g.shape
    tr = _rs_tile(rows, cols)

    def body(s_ref, g_ref, a_ref, p_ref):
        p_ref[...] = (g_ref[...] + a_ref[...]).astype(p_ref.dtype)

    blk = pl.BlockSpec((None, tr, cols), lambda j, i, s: (j, i, 0))
    spec = pltpu.PrefetchScalarGridSpec(
        num_scalar_prefetch=1, grid=(N_SHARD, rows // tr),
        in_specs=[pl.BlockSpec((None, None, tr, cols), lambda j, i, s: (j, s[0], i, 0)), blk], out_specs=blk)
    return pl.pallas_call(
        body, grid_spec=spec, out_shape=jax.ShapeDtypeStruct((N_SHARD, rows, cols), BF16), name=name,
        compiler_params=_params("parallel", "parallel"))(idx, g, a)


def _rs_chips(ps):
    nt = len(ps)

    def body(*refs):
        ins, outs = refs[:nt], refs[nt:2 * nt]
        send_sems, recv_sems = refs[2 * nt:]
        x, y, c = _mesh_pos()
        cps = [_remote(ins[t].at[2 * px + py], outs[t].at[k], send_sems.at[3 * t + k], recv_sems.at[3 * t + k],
                       (px, py, c))
               for t in range(nt) for k, (px, py) in enumerate(_other_chips(x, y))]
        for cp in cps:
            cp.start()
        for cp in cps:
            cp.wait()

    return pl.pallas_call(
        body, out_shape=[jax.ShapeDtypeStruct((3,) + p.shape[1:], p.dtype) for p in ps],
        in_specs=[ANY] * nt, out_specs=[ANY] * nt,
        scratch_shapes=[pltpu.SemaphoreType.DMA((3 * nt,)), pltpu.SemaphoreType.DMA((3 * nt,))],
        name="rs_chips")(*ps)


def _rs_final_add(g, a, b, idx, name):
    _, _, rows, cols = g.shape
    tr = _rs_tile(rows, cols)

    def body(s_ref, g_ref, a_ref, b0_ref, b1_ref, b2_ref, f_ref):
        own = g_ref[...] + a_ref[...]
        f_ref[...] = ((own + b0_ref[...].astype(F32)) + b1_ref[...].astype(F32)) + b2_ref[...].astype(F32)

    def b_spec(k):
        return pl.BlockSpec((None, tr, cols), lambda i, s: (k, i, 0))

    spec = pltpu.PrefetchScalarGridSpec(
        num_scalar_prefetch=1, grid=(rows // tr,),
        in_specs=[pl.BlockSpec((None, None, tr, cols), lambda i, s: (s[1], s[0], i, 0)),
                  pl.BlockSpec((None, tr, cols), lambda i, s: (s[1], i, 0)), b_spec(0), b_spec(1), b_spec(2)],
        out_specs=pl.BlockSpec((None, tr, cols), lambda i, s: (s[0], i, 0)))
    return pl.pallas_call(
        body, grid_spec=spec, out_shape=jax.ShapeDtypeStruct((2, rows, cols), F32), name=name,
        compiler_params=_params("parallel"))(idx, g, a, b, b, b)


def _rs_share(fs):
    nt = len(fs)

    def body(*refs):
        outs = refs[nt:2 * nt]
        send_sems, recv_sems = refs[2 * nt:]
        x, y, c = _mesh_pos()
        cps = [_remote(outs[t].at[c], outs[t].at[c], send_sems.at[t], recv_sems.at[t], (x, y, 1 - c))
               for t in range(nt)]
        for cp in cps:
            cp.start()
        for cp in cps:
            cp.wait()

    return pl.pallas_call(
        body, out_shape=[jax.ShapeDtypeStruct(f.shape, f.dtype) for f in fs],
        in_specs=[ANY] * nt, out_specs=[ANY] * nt, input_output_aliases={t: t for t in range(nt)},
        scratch_shapes=[pltpu.SemaphoreType.DMA((nt,)), pltpu.SemaphoreType.DMA((nt,))], name="rs_share")(*fs)


def _adamw(w, g, m, v, name):
    rows, cols = w.shape
    tr = rows // 4 if rows % 32 == 0 else rows

    def body(w_ref, g_ref, m_ref, v_ref, d_ref, mo_ref, vo_ref):
        gv = g_ref[...]
        mn = ADAM_B1 * m_ref[...] + (1.0 - ADAM_B1) * gv
        vn = ADAM_B2 * v_ref[...] + (1.0 - ADAM_B2) * (gv * gv)
        m_hat = mn / (1.0 - ADAM_B1 ** ADAM_STEP)
        v_hat = vn / (1.0 - ADAM_B2 ** ADAM_STEP)
        d_ref[...] = -ADAM_LR * (m_hat / (jnp.sqrt(v_hat) + ADAM_EPS) + ADAM_WD * w_ref[...])
        mo_ref[...] = mn
        vo_ref[...] = vn

    blk = pl.BlockSpec((tr, cols), lambda i: (i, 0))
    out = jax.ShapeDtypeStruct((rows, cols), F32)
    return pl.pallas_call(
        body, grid=(rows // tr,), in_specs=[blk] * 4, out_specs=[blk] * 3, out_shape=[out] * 3, name=name,
        compiler_params=_params("parallel"))(w, g, m, v)


BIG = ["ret_w_in", "ret_w_out", "dn_w_in", "dn_w_out", "ffn_w_gate", "ffn_w_up", "ffn_w_down"]
SMALL =["meta_tokens", "mix_norm_w", "ffn_norm_w", "ret_gn_w", "dn_conv_w", "dn_a_log", "dn_dt_bias",
         "dn_norm_w", "final_norm_w"]
SMALL_SHARDED = {"meta_tokens", "dn_conv_w", "dn_norm_w"}
ORDER = ["meta_tokens", "mix_norm_w", "ffn_norm_w", "ret_w_in", "ret_gn_w", "ret_w_out", "dn_w_in",
         "dn_conv_w", "dn_a_log", "dn_dt_bias", "dn_norm_w", "dn_w_out", "ffn_w_gate", "ffn_w_up",
         "ffn_w_down", "final_norm_w"]


def _halves(a):
    return a.reshape(2, -1, a.shape[-1])


def _pack_lanes(parts, align=8):
    flat = jnp.concatenate([p.reshape(-1) for p in parts])
    flat = jnp.pad(flat, (0, -flat.shape[0] % (align * LANES)))
    return flat.reshape(-1, LANES)


def _unpack(buf, shapes):
    lead = buf.shape[:-2]
    flat = buf.reshape(lead + (-1,))
    out, off = [], 0
    for shp in shapes:
        size = math.prod(shp)
        out.append(flat[..., off:off + size].reshape(lead + tuple(shp)))
        off += size
    return out


def _join_cols(shards):
    return jnp.concatenate([shards[j] for j in range(N_SHARD)], axis=-1)


def kernel(x, meta_tokens, mix_norm_w, ffn_norm_w, ret_w_in, ret_gn_w, ret_w_out, dn_w_in, dn_conv_w, dn_a_log, dn_dt_bias, dn_norm_w, dn_w_out, ffn_w_gate, ffn_w_up, ffn_w_down, final_norm_w, loss_target, m_meta_tokens, m_mix_norm_w, m_ffn_norm_w, m_ret_w_in, m_ret_gn_w, m_ret_w_out, m_dn_w_in, m_dn_conv_w, m_dn_a_log, m_dn_dt_bias, m_dn_norm_w, m_dn_w_out, m_ffn_w_gate, m_ffn_w_up, m_ffn_w_down, m_final_norm_w, v_meta_tokens, v_mix_norm_w, v_ffn_norm_w, v_ret_w_in, v_ret_gn_w, v_ret_w_out, v_dn_w_in, v_dn_conv_w, v_dn_a_log, v_dn_dt_bias, v_dn_norm_w, v_dn_w_out, v_ffn_w_gate, v_ffn_w_up, v_ffn_w_down, v_final_norm_w):
    w = dict(meta_tokens=meta_tokens, mix_norm_w=mix_norm_w, ffn_norm_w=ffn_norm_w, ret_w_in=ret_w_in,
             ret_gn_w=ret_gn_w, ret_w_out=ret_w_out, dn_w_in=dn_w_in, dn_conv_w=dn_conv_w, dn_a_log=dn_a_log,
             dn_dt_bias=dn_dt_bias, dn_norm_w=dn_norm_w, dn_w_out=dn_w_out, ffn_w_gate=ffn_w_gate,
             ffn_w_up=ffn_w_up, ffn_w_down=ffn_w_down, final_norm_w=final_norm_w)
    m = dict(meta_tokens=m_meta_tokens, mix_norm_w=m_mix_norm_w, ffn_norm_w=m_ffn_norm_w, ret_w_in=m_ret_w_in,
             ret_gn_w=m_ret_gn_w, ret_w_out=m_ret_w_out, dn_w_in=m_dn_w_in, dn_conv_w=m_dn_conv_w,
             dn_a_log=m_dn_a_log, dn_dt_bias=m_dn_dt_bias, dn_norm_w=m_dn_norm_w, dn_w_out=m_dn_w_out,
             ffn_w_gate=m_ffn_w_gate, ffn_w_up=m_ffn_w_up, ffn_w_down=m_ffn_w_down, final_norm_w=m_final_norm_w)
    v = dict(meta_tokens=v_meta_tokens, mix_norm_w=v_mix_norm_w, ffn_norm_w=v_ffn_norm_w, ret_w_in=v_ret_w_in,
             ret_gn_w=v_ret_gn_w, ret_w_out=v_ret_w_out, dn_w_in=v_dn_w_in, dn_conv_w=v_dn_conv_w,
             dn_a_log=v_dn_a_log, dn_dt_bias=v_dn_dt_bias, dn_norm_w=v_dn_norm_w, dn_w_out=v_dn_w_out,
             ffn_w_gate=v_ffn_w_gate, ffn_w_up=v_ffn_w_up, ffn_w_down=v_ffn_w_down, final_norm_w=v_final_norm_w)
    mx, my, mc = _mesh_pos()
    chip = 2 * mx + my

    gathered = _gather_weights([_halves(w[n].astype(MXU_DTYPE)) for n in BIG])
    full = {n: gathered[i].reshape((N_SHARD,) + w[n].shape) for i, n in enumerate(BIG)}
    sm_names = [n for n in SMALL if n in SMALL_SHARDED]
    sm_gathered = _unpack(_gather_small(_pack_lanes([w[n] for n in sm_names])), [w[n].shape for n in sm_names])
    for i, n in enumerate(sm_names):
        full[n] = _join_cols(sm_gathered[i])
    wts = {
        "meta_tokens": full["meta_tokens"], "mix_norm_w": mix_norm_w, "ffn_norm_w": ffn_norm_w,
        "ret_gn_w": ret_gn_w[0], "final_norm_w": final_norm_w, "dn_conv_w": full["dn_conv_w"][0],
        "dn_a_log": dn_a_log[0], "dn_dt_bias": dn_dt_bias[0], "dn_norm_w": full["dn_norm_w"][0],
        "ret_w_in": full["ret_w_in"][:, 0], "ret_w_out": full["ret_w_out"].reshape(-1, D_MODEL),
        "dn_w_in": jnp.pad(_join_cols(full["dn_w_in"][:, 0]), ((0, 0), (0, DN_IN_PAD - DN_IN))),
        "dn_w_out": full["dn_w_out"].reshape(-1, D_MODEL), "ffn_w_gate": full["ffn_w_gate"],
        "ffn_w_up": full["ffn_w_up"], "ffn_w_down": full["ffn_w_down"],
    }

    loss_part, dh0, g = _local_step(x[0], loss_target[0], wts)
    seq = x.shape[1]
    grad_x = dh0[CHUNK:CHUNK + seq].reshape(x.shape)

    n_dn = dn_w_in.shape[-1]
    g["dn_w_in"] = jnp.stack([g["dn_w_in"][:, j * n_dn:(j + 1) * n_dn] for j in range(N_SHARD)])
    gs = [g[n].reshape((N_SHARD,) + _halves(w[n]).shape) for n in BIG]
    idx = jnp.stack([mc, chip]).astype(jnp.int32)
    sib = _rs_pair(gs)
    parts = [_rs_pair_add(gs[t], sib[t], idx, "rs_pair_add_" + n) for t, n in enumerate(BIG)]
    others = _rs_chips(parts)
    mine = [_rs_final_add(gs[t], sib[t], others[t], idx, "rs_final_add_" + n) for t, n in enumerate(BIG)]
    gsh = {n: f.reshape(w[n].shape) for n, f in zip(BIG, _rs_share(mine))}

    small_full_shapes = [g[n].shape for n in SMALL] + [(1,)]
    red = _unpack(_allreduce_small(_pack_lanes([g[n] for n in SMALL] + [loss_part[0, :1]])), small_full_shapes)
    loss = red[-1][0]
    for i, n in enumerate(SMALL):
        gn = red[i]
        if n in SMALL_SHARDED:
            width = w[n].shape[-1]
            gn = lax.dynamic_slice_in_dim(gn, chip * width, width, axis=gn.ndim - 1)
        gsh[n] = gn.reshape(w[n].shape)

    delta, new_m, new_v = {}, {}, {}
    for n in BIG:
        shp = w[n].shape
        two_d = (-1, shp[-1])
        d_, m_, v_ = _adamw(w[n].reshape(two_d), gsh[n].reshape(two_d), m[n].reshape(two_d),
                            v[n].reshape(two_d), "adamw_" + n)
        delta[n], new_m[n], new_v[n] = d_.reshape(shp), m_.reshape(shp), v_.reshape(shp)
    sm_local_shapes = [w[n].shape for n in SMALL]
    d_, m_, v_ = _adamw(_pack_lanes([w[n] for n in SMALL]), _pack_lanes([gsh[n] for n in SMALL]),
                        _pack_lanes([m[n] for n in SMALL]), _pack_lanes([v[n] for n in SMALL]), "adamw_small")
    for n, dd, mm, vv in zip(SMALL, _unpack(d_, sm_local_shapes), _unpack(m_, sm_local_shapes),
                             _unpack(v_, sm_local_shapes)):
        delta[n], new_m[n], new_v[n] = dd, mm, vv

    return (loss, grad_x, *[gsh[n] for n in ORDER], *[delta[n] for n in ORDER],
            *[new_m[n] for n in ORDER], *[new_v[n] for n in ORDER])
```

```python
import functools
import math

import jax
import jax.numpy as jnp
from jax import lax
from jax.experimental import pallas as pl
from jax.experimental.pallas import tpu as pltpu

F32 = jnp.float32
BF16 = jnp.bfloat16
MXU_DTYPE = BF16

D_MODEL = 1024
N_META = 16
CHUNK = 64
PAD = CHUNK - N_META
RMS_EPS = 1e-6
RET_HEADS, RET_DK, RET_DV = 4, 256, 512
RET_QK, RET_V = RET_HEADS * RET_DK, RET_HEADS * RET_DV
RET_IN = 2 * RET_QK + 2 * RET_V
ROPE_BASE = 10000.0
DN_HEADS, DN_DK, DN_DV = 8, 128, 256
DN_QK, DN_V = DN_HEADS * DN_DK, DN_HEADS * DN_DV
DN_CONV_CH = 2 * DN_QK + DN_V
DN_IN = DN_CONV_CH + DN_V + 2 * DN_HEADS
LANES = 128
DN_IN_PAD = DN_CONV_CH + DN_V + LANES
CONV_K = 4
FFN_HIDDEN = 2816
ADAM_LR, ADAM_B1, ADAM_B2, ADAM_EPS, ADAM_WD, ADAM_STEP = 0.001, 0.9, 0.999, 1e-08, 0.01, 10

ROW_ALIGN = 256
VMEM_LIMIT = 56 * 1024 * 1024
MESH = pl.DeviceIdType.MESH
ANY = pl.BlockSpec(memory_space=pl.ANY)
VMEM_SPEC = pl.BlockSpec(memory_space=pltpu.VMEM)
_HI = lax.Precision.HIGHEST


def _params(*sem):
    return pltpu.CompilerParams(dimension_semantics=sem, vmem_limit_bytes=VMEM_LIMIT)


def _dg(a, b, ca, cb, hi):
    dims = (((ca,), (cb,)), ((), ()))

    def dot(p, q):
        return lax.dot_general(p, q, dims, preferred_element_type=F32)

    if not hi:
        return dot(a.astype(MXU_DTYPE), b.astype(MXU_DTYPE))
    if MXU_DTYPE == F32:
        return lax.dot_general(a, b, dims, precision=_HI, preferred_element_type=F32)
    a_hi, b_hi = a.astype(MXU_DTYPE), b.astype(MXU_DTYPE)
    a_lo = (a - a_hi.astype(F32)).astype(MXU_DTYPE)
    b_lo = (b - b_hi.astype(F32)).astype(MXU_DTYPE)
    return dot(a_hi, b_hi) + (dot(a_hi, b_lo) + dot(a_lo, b_hi))


def _nn(a, b, hi=False):
    return _dg(a, b, 1, 0, hi)


def _nt(a, b, hi=False):
    return _dg(a, b, 1, 1, hi)


def _tn(a, b, hi=False):
    return _dg(a, b, 0, 0, hi)


def _iota(shape, dim):
    return lax.broadcasted_iota(jnp.int32, shape, dim)


def _valid_rows(first_row, rows, seq):
    r = first_row + _iota((rows, 1), 0)
    return ((r >= PAD) & (r < CHUNK + seq)).astype(F32)


def _rope(t, cs, sn):
    half = t.shape[-1] // 2
    t1, t2 = t[:, :half], t[:, half:]
    return jnp.concatenate([t1 * cs - t2 * sn, t1 * sn + t2 * cs], axis=1)


def _rope_bwd(d, cs, sn):
    half = d.shape[-1] // 2
    d1, d2 = d[:, :half], d[:, half:]
    return jnp.concatenate([d1 * cs + d2 * sn, d2 * cs - d1 * sn], axis=1)


def _col(x, idx):
    oh = (_iota((1, x.shape[1]), 1) == idx).astype(F32)
    return jnp.sum(x * oh, axis=1, keepdims=True)


def _row(x, idx):
    oh = (_iota((x.shape[0], 1), 0) == idx).astype(F32)
    return jnp.sum(x * oh, axis=0, keepdims=True)


def _shift_down(x, halo8, k):
    xr = pltpu.roll(x, k, 0)
    hr = pltpu.roll(halo8, k, 0)
    first = jnp.where(_iota((8, 1), 0) < k, hr, xr[0:8])
    return jnp.concatenate([first, xr[8:]], axis=0)


def _shift_up(x, next8, j):
    rows = x.shape[0]
    xr = pltpu.roll(x, rows - j, 0)
    nr = pltpu.roll(next8, 8 - j, 0)
    last = jnp.where(_iota((8, 1), 0) >= 8 - j, nr, xr[rows - 8:])
    return jnp.concatenate([xr[:rows - 8], last], axis=0)


def _gated_norm(o, gate, w):
    r = lax.rsqrt(jnp.mean(o * o, axis=-1, keepdims=True) + RMS_EPS)
    return o * r * w * (gate * jax.nn.sigmoid(gate))


def _gated_norm_bwd(dy, o, gate, w):
    r = lax.rsqrt(jnp.mean(o * o, axis=-1, keepdims=True) + RMS_EPS)
    nrm = o * r
    sg = jax.nn.sigmoid(gate)
    sl = gate * sg
    dgate = dy * nrm * w * (sg * (1.0 + gate * (1.0 - sg)))
    dn = dy * w * sl
    dw = jnp.sum(dy * nrm * sl, axis=0, keepdims=True)
    do = r * (dn - nrm * jnp.mean(dn * nrm, axis=-1, keepdims=True))
    return do, dgate, dw


def _softplus(z):
    return jnp.maximum(z, 0.0) + jnp.log(1.0 + jnp.exp(-jnp.abs(z)))


def _row_tile(rows, cap=768):
    for t in (768, 512, 256, 128, 64, 32, 16, 8):
        if t <= cap and rows % t == 0:
            return t
    return rows


def _div_tile(n, cap, mult):
    best = None
    for t in range(mult, min(cap, n) + 1, mult):
        if n % t == 0:
            best = t
    return best or n


def _col_tile(cols, cap=1536):
    best = None
    for t in range(LANES, min(cap, cols) + 1, LANES):
        if cols % t == 0:
            best = t
    return best or cols


def _rms_fwd(h, w, name):
    rows, d = h.shape
    tm = _row_tile(rows)

    def body(h_ref, w_ref, o_ref):
        x = h_ref[...]
        r = lax.rsqrt(jnp.mean(x * x, axis=-1, keepdims=True) + RMS_EPS)
        o_ref[...] = (x * r * w_ref[...]).astype(o_ref.dtype)

    return pl.pallas_call(
        body, grid=(rows // tm,),
        in_specs=[pl.BlockSpec((tm, d), lambda i: (i, 0)), pl.BlockSpec((1, d), lambda i: (0, 0))],
        out_specs=pl.BlockSpec((tm, d), lambda i: (i, 0)),
        out_shape=jax.ShapeDtypeStruct((rows, d), BF16), name=name,
        compiler_params=_params("parallel"))(h, w.reshape(1, d))


def _rms_bwd(dy, h, w, resid, name):
    rows, d = h.shape
    tm = _row_tile(rows)

    def body(dy_ref, h_ref, w_ref, r_ref, dh_ref, dw_ref):
        i = pl.program_id(0)
        x = h_ref[...]
        r = lax.rsqrt(jnp.mean(x * x, axis=-1, keepdims=True) + RMS_EPS)
        xh = x * r
        dyv = dy_ref[...]
        dxh = dyv * w_ref[...]
        dh_ref[...] = r_ref[...] + r * (dxh - xh * jnp.mean(dxh * xh, axis=-1, keepdims=True))
        part = jnp.sum(dyv * xh, axis=0, keepdims=True)

        @pl.when(i == 0)
        def _():
            dw_ref[...] = part

        @pl.when(i > 0)
        def _():
            dw_ref[...] += part

    blk = pl.BlockSpec((tm, d), lambda i: (i, 0))
    vec = pl.BlockSpec((1, d), lambda i: (0, 0))
    return pl.pallas_call(
        body, grid=(rows // tm,), in_specs=[blk, blk, vec, blk], out_specs=[blk, vec],
        out_shape=[jax.ShapeDtypeStruct((rows, d), F32), jax.ShapeDtypeStruct((1, d), F32)],
        name=name, compiler_params=_params("arbitrary"))(dy, h, w.reshape(1, d), resid)


def _final_loss(h, w, tgt, seq, name):
    rows, d = h.shape
    tm = _row_tile(rows)

    def body(h_ref, w_ref, t_ref, dh_ref, dw_ref, loss_ref):
        i = pl.program_id(0)
        r_idx = i * tm + _iota((tm, 1), 0)
        m = ((r_idx >= CHUNK) & (r_idx < CHUNK + seq)).astype(F32)
        x = h_ref[...]
        wv = w_ref[...]
        r = lax.rsqrt(jnp.mean(x * x, axis=-1, keepdims=True) + RMS_EPS)
        xh = x * r
        err = (xh * wv - t_ref[...]) * m
        lpart = 0.5 * jnp.sum(jnp.mean(err * err, axis=-1, keepdims=True), axis=0, keepdims=True)
        dyv = err * (1.0 / d)
        dxh = dyv * wv
        dh_ref[...] = r * (dxh - xh * jnp.mean(dxh * xh, axis=-1, keepdims=True))
        part = jnp.sum(dyv * xh, axis=0, keepdims=True)

        @pl.when(i == 0)
        def _():
            dw_ref[...] = part
            loss_ref[...] = jnp.broadcast_to(lpart, loss_ref.shape)

        @pl.when(i > 0)
        def _():
            dw_ref[...] += part
            loss_ref[...] += jnp.broadcast_to(lpart, loss_ref.shape)

    blk = pl.BlockSpec((tm, d), lambda i: (i, 0))
    vec = pl.BlockSpec((1, d), lambda i: (0, 0))
    return pl.pallas_call(
        body, grid=(rows // tm,), in_specs=[blk, vec, blk],
        out_specs=[blk, vec, pl.BlockSpec((1, LANES), lambda i: (0, 0))],
        out_shape=[jax.ShapeDtypeStruct((rows, d), F32), jax.ShapeDtypeStruct((1, d), F32),
                   jax.ShapeDtypeStruct((1, LANES), F32)],
        name=name, compiler_params=_params("arbitrary"))(h, w.reshape(1, d), tgt)


def _mm(a, b, *, mode, name, out_dtype=F32, resid=None, row_cap=768, col_cap=1536):
    if mode == "tn":
        m, k = a.shape
        n = b.shape[1]
        tm, tn = _row_tile(m, row_cap), _col_tile(n, col_cap)

        def body_tn(a_ref, b_ref, o_ref):
            i = pl.program_id(1)
            part = _tn(a_ref[...], b_ref[...])

            @pl.when(i == 0)
            def _():
                o_ref[...] = part

            @pl.when(i > 0)
            def _():
                o_ref[...] += part

        return pl.pallas_call(
            body_tn, grid=(n // tn, m // tm),
            in_specs=[pl.BlockSpec((tm, k), lambda j, i: (i, 0)),
                      pl.BlockSpec((tm, tn), lambda j, i: (i, j))],
            out_specs=pl.BlockSpec((k, tn), lambda j, i: (0, j)),
            out_shape=jax.ShapeDtypeStruct((k, n), F32), name=name,
            compiler_params=_params("parallel", "arbitrary"))(a, b)

    m, ka = a.shape
    n = b.shape[1] if mode == "nn" else b.shape[0]
    tm, tn = _row_tile(m, row_cap), _col_tile(n, col_cap)
    has_resid = resid is not None

    def body(*refs):
        if has_resid:
            a_ref, b_ref, r_ref, o_ref = refs
        else:
            a_ref, b_ref, o_ref = refs
        acc = _nn(a_ref[...], b_ref[...]) if mode == "nn" else _nt(a_ref[...], b_ref[...])
        if has_resid:
            acc = acc + r_ref[...]
        o_ref[...] = acc.astype(o_ref.dtype)

    b_spec = (pl.BlockSpec((b.shape[0], tn), lambda j, i: (0, j)) if mode == "nn"
              else pl.BlockSpec((tn, b.shape[1]), lambda j, i: (j, 0)))
    o_spec = pl.BlockSpec((tm, tn), lambda j, i: (i, j))
    in_specs = [pl.BlockSpec((tm, ka), lambda j, i: (i, 0)), b_spec]
    args = [a, b]
    if has_resid:
        in_specs.append(o_spec)
        args.append(resid)
    return pl.pallas_call(
        body, grid=(n // tn, m // tm), in_specs=in_specs, out_specs=o_spec,
        out_shape=jax.ShapeDtypeStruct((m, n), out_dtype), name=name,
        compiler_params=_params("parallel", "parallel"))(*args)


N_SHARD = 4


def _gmm(name, grid, args, in_specs, out_specs, out_shape, fn, red_axis=None, init_arg=None, aliases=None):
    n_in = len(args)

    def body(*refs):
        ins, outs = refs[:n_in], refs[n_in:]
        parts = fn(*ins)
        if red_axis is None:
            for o_ref, p in zip(outs, parts):
                o_ref[...] = p.astype(o_ref.dtype)
            return
        k = pl.program_id(red_axis)

        @pl.when(k == 0)
        def _():
            for idx, (o_ref, p) in enumerate(zip(outs, parts)):
                o_ref[...] = p + ins[init_arg][...] if (idx == 0 and init_arg is not None) else p

        @pl.when(k > 0)
        def _():
            for o_ref, p in zip(outs, parts):
                o_ref[...] += p

    sem = tuple("arbitrary" if ax == red_axis else "parallel" for ax in range(len(grid)))
    return pl.pallas_call(body, grid=grid, in_specs=in_specs, out_specs=out_specs, out_shape=out_shape,
                          name=name, input_output_aliases=aliases or {}, compiler_params=_params(*sem))(*args)


def _mm_cols(a, ws, name):
    m, k = a.shape
    n = ws.shape[2]
    tm = _row_tile(m)
    return _gmm(name, (N_SHARD, m // tm), [a, ws],
                [pl.BlockSpec((tm, k), lambda j, i: (i, 0)), pl.BlockSpec((None, k, n), lambda j, i: (j, 0, 0))],
                pl.BlockSpec((tm, n), lambda j, i: (i, j)), jax.ShapeDtypeStruct((m, N_SHARD * n), F32),
                lambda a_ref, w_ref: (_nn(a_ref[...], w_ref[...]),))


def _mm_cols_t(d, ws, name):
    m = d.shape[0]
    _, k, n = ws.shape
    tm = _row_tile(m)
    return _gmm(name, (m // tm, N_SHARD), [d, ws],
                [pl.BlockSpec((tm, n), lambda i, j: (i, j)), pl.BlockSpec((None, k, n), lambda i, j: (j, 0, 0))],
                pl.BlockSpec((tm, k), lambda i, j: (i, 0)), jax.ShapeDtypeStruct((m, k), F32),
                lambda d_ref, w_ref: (_nt(d_ref[...], w_ref[...]),), red_axis=1)


def _mm_cols_grad(a, d, name):
    m, k = a.shape
    n = d.shape[1] // N_SHARD
    tm = _row_tile(m)
    return _gmm(name, (N_SHARD, m // tm), [a, d],
                [pl.BlockSpec((tm, k), lambda j, i: (i, 0)), pl.BlockSpec((tm, n), lambda j, i: (i, j))],
                pl.BlockSpec((None, k, n), lambda j, i: (j, 0, 0)), jax.ShapeDtypeStruct((N_SHARD, k, n), F32),
                lambda a_ref, d_ref: (_tn(a_ref[...], d_ref[...]),), red_axis=1)


def _ffn_up(hn, wg, wu, layer, name):
    m, k = hn.shape
    n = wg.shape[3]
    tm = _row_tile(m)

    def fn(a_ref, wg_ref, wu_ref):
        a = a_ref[...]
        g = _nn(a, wg_ref[...])
        u = _nn(a, wu_ref[...])
        return g, u, g * jax.nn.sigmoid(g) * u

    w_spec = pl.BlockSpec((None, None, k, n), lambda j, i: (j, layer, 0, 0))
    o_spec = pl.BlockSpec((None, tm, n), lambda j, i: (j, i, 0))
    out = jax.ShapeDtypeStruct((N_SHARD, m, n), BF16)
    return _gmm(name, (N_SHARD, m // tm), [hn, wg, wu],
                [pl.BlockSpec((tm, k), lambda j, i: (i, 0)), w_spec, w_spec],
                [o_spec, o_spec, o_spec], [out, out, out], fn)


def _ffn_down(act, wd, resid, layer, name):
    _, m, n = act.shape
    d = wd.shape[3]
    tm = _row_tile(m)
    row = pl.BlockSpec((tm, d), lambda i, j: (i, 0))
    return _gmm(name, (m // tm, N_SHARD), [act, wd, resid],
                [pl.BlockSpec((None, tm, n), lambda i, j: (j, i, 0)),
                 pl.BlockSpec((None, None, n, d), lambda i, j: (j, layer, 0, 0)), row],
                row, jax.ShapeDtypeStruct((m, d), F32),
                lambda a_ref, w_ref, r_ref: (_nn(a_ref[...], w_ref[...]),), red_axis=1, init_arg=2)


def _ffn_down_bwd(dh, wd, g, u, layer, name):
    m, d = dh.shape
    n = wd.shape[2]
    tm = _row_tile(m)

    def fn(dh_ref, wd_ref, g_ref, u_ref):
        dact = _nt(dh_ref[...], wd_ref[...])
        gv = g_ref[...].astype(F32)
        uv = u_ref[...].astype(F32)
        sg = jax.nn.sigmoid(gv)
        return dact * uv * (sg * (1.0 + gv * (1.0 - sg))), dact * gv * sg

    o_spec = pl.BlockSpec((None, tm, n), lambda j, i: (j, i, 0))
    out = jax.ShapeDtypeStruct((N_SHARD, m, n), BF16)
    return _gmm(name, (N_SHARD, m // tm), [dh, wd, g, u],
                [pl.BlockSpec((tm, d), lambda j, i: (i, 0)),
                 pl.BlockSpec((None, None, n, d), lambda j, i: (j, layer, 0, 0)), o_spec, o_spec],
                [o_spec, o_spec], [out, out], fn)


def _ffn_up_bwd(dg, du, wg, wu, layer, name):
    _, m, n = dg.shape
    k = wg.shape[2]
    tm = _row_tile(m)
    d_spec = pl.BlockSpec((None, tm, n), lambda i, j: (j, i, 0))
    w_spec = pl.BlockSpec((None, None, k, n), lambda i, j: (j, layer, 0, 0))
    return _gmm(name, (m // tm, N_SHARD), [dg, du, wg, wu], [d_spec, d_spec, w_spec, w_spec],
                pl.BlockSpec((tm, k), lambda i, j: (i, 0)), jax.ShapeDtypeStruct((m, k), F32),
                lambda dg_ref, du_ref, wg_ref, wu_ref: (
                    _nt(dg_ref[...], wg_ref[...]) + _nt(du_ref[...], wu_ref[...]),), red_axis=1)


def _ffn_wgrad(lhs, rhs_list, layer, layers, prev, lhs_sharded, name):
    if lhs_sharded:
        _, m, k = lhs.shape
        n = rhs_list[0].shape[1]
    else:
        m, k = lhs.shape
        n = rhs_list[0].shape[2]
    tm = _row_tile(m)
    sh = pl.BlockSpec((None, tm, k if lhs_sharded else n), lambda j, i: (j, i, 0))
    fl = pl.BlockSpec((tm, n if lhs_sharded else k), lambda j, i: (i, 0))
    n_out = len(rhs_list)
    args = [lhs] + list(rhs_list)
    in_specs = [sh if lhs_sharded else fl] + [fl if lhs_sharded else sh] * n_out
    aliases = None
    if prev is not None:
        aliases = {len(args) + t: t for t in range(n_out)}
        args = args + list(prev)
        in_specs = in_specs + [ANY] * n_out

    def fn(l_ref, *rest):
        lv = l_ref[...]
        return tuple(_tn(lv, r_ref[...]) for r_ref in rest[:n_out])

    o_spec = pl.BlockSpec((None, None, k, n), lambda j, i: (j, layer, 0, 0))
    out = jax.ShapeDtypeStruct((N_SHARD, layers, k, n), F32)
    return _gmm(name, (N_SHARD, m // tm), args, in_specs, [o_spec] * n_out, [out] * n_out, fn,
                red_axis=1, aliases=aliases)


def _ret_consts():
    log_gamma = jnp.log1p(-jnp.exp2(-5.0 - jnp.arange(RET_HEADS, dtype=F32)))
    idx = jnp.arange(CHUNK, dtype=F32)
    rel = idx[:, None] - idx[None, :]
    dmask = jnp.where((rel >= 0)[None], jnp.exp(log_gamma[:, None, None] * jnp.maximum(rel, 0.0)), 0.0)
    xi = jnp.exp(log_gamma[:, None] * (idx[None, :] + 1.0))[:, :, None]
    zeta = jnp.exp(log_gamma[:, None] * (CHUNK - 1.0 - idx[None, :]))[:, :, None]
    gamma_c = jnp.exp(log_gamma * CHUNK)
    wide = (RET_HEADS, CHUNK, RET_DK)
    return dmask, jnp.broadcast_to(xi, wide), jnp.broadcast_to(zeta, wide), gamma_c


def _rope_tables(rows):
    half = RET_DK // 2
    inv_freq = ROPE_BASE ** (-jnp.arange(half, dtype=F32) / half)
    pos = (jnp.arange(rows) - PAD).astype(F32)
    ang = pos[:, None] * inv_freq[None, :]
    return jnp.cos(ang), jnp.sin(ang)


def _ret_specs(order):
    return [pl.BlockSpec((CHUNK, RET_QK), lambda n: (order(n), 0)),
            pl.BlockSpec((CHUNK, RET_QK), lambda n: (order(n), 1)),
            pl.BlockSpec((CHUNK, RET_V), lambda n: (order(n), 1)),
            pl.BlockSpec((CHUNK, RET_V), lambda n: (order(n), 2))]


def _ret_const_specs():
    return [pl.BlockSpec((RET_HEADS, CHUNK, CHUNK), lambda n: (0, 0, 0)),
            pl.BlockSpec((RET_HEADS, CHUNK, RET_DK), lambda n: (0, 0, 0)),
            pl.BlockSpec((RET_HEADS, CHUNK, RET_DK), lambda n: (0, 0, 0)),
            pl.BlockSpec((1, RET_DV), lambda n: (0, 0))]


def _ret_fwd(proj, cos, sin, consts, gn_w, seq):
    rows = proj.shape[0]
    nc = rows // CHUNK
    dmask, xi, zeta, gamma_c = consts

    def body(gam_ref, q_ref, k_ref, v_ref, g_ref, cos_ref, sin_ref, dm_ref, xi_ref, ze_ref, gn_ref,
             o_ref, y_ref, ss_ref, s_ref):
        n = pl.program_id(0)

        @pl.when(n == 0)
        def _():
            s_ref[...] = jnp.zeros_like(s_ref)

        cs, sn = cos_ref[...], sin_ref[...]
        kscale = _valid_rows(n * CHUNK, CHUNK, seq) * (RET_DK ** -0.5)
        gn = gn_ref[...]
        hs = range(RET_HEADS)
        qk_cols = [slice(h * RET_DK, (h + 1) * RET_DK) for h in hs]
        v_cols = [slice(h * RET_DV, (h + 1) * RET_DV) for h in hs]
        qr_l = [_rope(q_ref[:, c], cs, sn) for c in qk_cols]
        kr_l = [_rope(k_ref[:, c], cs, sn) * kscale for c in qk_cols]
        v_l = [v_ref[:, c] for c in v_cols]
        s_l = [s_ref[h] for h in hs]
        sc_l = [_nt(qr, kr) * dm_ref[h] for h, (qr, kr) in enumerate(zip(qr_l, kr_l))]
        o_l = [_nn(sc_l[h], v_l[h]) + _nn(qr_l[h] * xi_ref[h], s_l[h]) for h in hs]
        for h in hs:
            ss_ref[0, h] = s_l[h].astype(ss_ref.dtype)
            s_ref[h] = gam_ref[h] * s_l[h] + _tn(kr_l[h] * ze_ref[h], v_l[h])
            o_ref[:, v_cols[h]] = o_l[h]
            y_ref[:, v_cols[h]] = _gated_norm(o_l[h], g_ref[:, v_cols[h]], gn).astype(y_ref.dtype)

    fwd = lambda n: n
    row128 = pl.BlockSpec((CHUNK, RET_DK // 2), lambda n: (n, 0))
    row_v = pl.BlockSpec((CHUNK, RET_V), lambda n: (n, 0))
    return pl.pallas_call(
        body, grid=(nc,),
        in_specs=[pl.BlockSpec(memory_space=pltpu.SMEM)] + _ret_specs(fwd) + [row128, row128]
        + _ret_const_specs(),
        out_specs=[row_v, row_v,
                   pl.BlockSpec((1, RET_HEADS, RET_DK, RET_DV), lambda n: (n, 0, 0, 0))],
        out_shape=[jax.ShapeDtypeStruct((rows, RET_V), F32), jax.ShapeDtypeStruct((rows, RET_V), BF16),
                   jax.ShapeDtypeStruct((nc, RET_HEADS, RET_DK, RET_DV), BF16)],
        scratch_shapes=[pltpu.VMEM((RET_HEADS, RET_DK, RET_DV), F32)],
        name="ret_fwd", compiler_params=_params("arbitrary"))(
            gamma_c, proj, proj, proj, proj, cos, sin, dmask, xi, zeta, gn_w.reshape(1, RET_DV))


def _ret_bwd(proj, o, dy, states, cos, sin, consts, gn_w, seq):
    rows = proj.shape[0]
    nc = rows // CHUNK
    dmask, xi, zeta, gamma_c = consts

    def body(gam_ref, q_ref, k_ref, v_ref, g_ref, o_ref, dy_ref, ss_ref, cos_ref, sin_ref,
             dm_ref, xi_ref, ze_ref, gn_ref, dp_ref, dgn_ref, ds_ref):
        n = pl.program_id(0)

        @pl.when(n == 0)
        def _():
            ds_ref[...] = jnp.zeros_like(ds_ref)
            dgn_ref[...] = jnp.zeros_like(dgn_ref)

        cs, sn = cos_ref[...], sin_ref[...]
        kscale = _valid_rows((nc - 1 - n) * CHUNK, CHUNK, seq) * (RET_DK ** -0.5)
        gn = gn_ref[...]
        dgn = jnp.zeros((1, RET_DV), F32)
        hs = range(RET_HEADS)
        qk_cols = [slice(h * RET_DK, (h + 1) * RET_DK) for h in hs]
        v_cols = [slice(h * RET_DV, (h + 1) * RET_DV) for h in hs]
        qr_l = [_rope(q_ref[:, c], cs, sn) for c in qk_cols]
        kr_l = [_rope(k_ref[:, c], cs, sn) * kscale for c in qk_cols]
        v_l = [v_ref[:, c] for c in v_cols]
        s_l = [ss_ref[0, h] for h in hs]
        ds_l = [ds_ref[h] for h in hs]
        gnb = [_gated_norm_bwd(dy_ref[:, c], o_ref[:, c], g_ref[:, c], gn) for c in v_cols]
        do_l = [x[0] for x in gnb]
        sc_l = [_nt(qr_l[h], kr_l[h]) * dm_ref[h] for h in hs]
        dsc_l = [_nt(do_l[h], v_l[h]) * dm_ref[h] for h in hs]
        dv_l = [_tn(sc_l[h], do_l[h]) + _nn(kr_l[h] * ze_ref[h], ds_l[h]) for h in hs]
        dqr_l = [_nn(dsc_l[h], kr_l[h]) + _nt(do_l[h], s_l[h]) * xi_ref[h] for h in hs]
        dkr_l = [_tn(dsc_l[h], qr_l[h]) + _nt(v_l[h], ds_l[h]) * ze_ref[h] for h in hs]
        for h in hs:
            dgn = dgn + gnb[h][2]
            ds_ref[h] = gam_ref[h] * ds_l[h] + _tn(qr_l[h] * xi_ref[h], do_l[h])
            dp_ref[:, qk_cols[h]] = _rope_bwd(dqr_l[h], cs, sn).astype(dp_ref.dtype)
            dp_ref[:, RET_QK + h * RET_DK:RET_QK + (h + 1) * RET_DK] = (
                _rope_bwd(dkr_l[h] * kscale, cs, sn).astype(dp_ref.dtype))
            dp_ref[:, 2 * RET_QK + h * RET_DV:2 * RET_QK + (h + 1) * RET_DV] = dv_l[h].astype(dp_ref.dtype)
            dp_ref[:, 2 * RET_QK + RET_V + h * RET_DV:2 * RET_QK + RET_V + (h + 1) * RET_DV] = (
                gnb[h][1].astype(dp_ref.dtype))
        dgn_ref[...] += dgn

    rev = lambda n: nc - 1 - n
    row128 = pl.BlockSpec((CHUNK, RET_DK // 2), lambda n: (rev(n), 0))
    row_v = pl.BlockSpec((CHUNK, RET_V), lambda n: (rev(n), 0))
    return pl.pallas_call(
        body, grid=(nc,),
        in_specs=[pl.BlockSpec(memory_space=pltpu.SMEM)] + _ret_specs(rev) + [
            row_v, row_v, pl.BlockSpec((1, RET_HEADS, RET_DK, RET_DV), lambda n: (rev(n), 0, 0, 0)),
            row128, row128] + _ret_const_specs(),
        out_specs=[pl.BlockSpec((CHUNK, RET_IN), lambda n: (rev(n), 0)),
                   pl.BlockSpec((1, RET_DV), lambda n: (0, 0))],
        out_shape=[jax.ShapeDtypeStruct((rows, RET_IN), BF16), jax.ShapeDtypeStruct((1, RET_DV), F32)],
        scratch_shapes=[pltpu.VMEM((RET_HEADS, RET_DK, RET_DV), F32)],
        name="ret_bwd", compiler_params=_params("arbitrary"))(
            gamma_c, proj, proj, proj, proj, o, dy, states, cos, sin, dmask, xi, zeta,
            gn_w.reshape(1, RET_DV))


GATE_COL = DN_CONV_CH // DN_V
BA_COL = (DN_CONV_CH + DN_V) // LANES
BETA_LANE, DECAY_LANE = 0, DN_HEADS
INV_SHIFT = 4
INV_SQUARINGS = INV_SHIFT - 1
assert CHUNK == 4 << INV_SHIFT


def _dn_in_specs(order):
    return [pl.BlockSpec((CHUNK, DN_CONV_CH), lambda n: (order(n), 0)),
            pl.BlockSpec((8, DN_CONV_CH), lambda n: (jnp.maximum(order(n) * (CHUNK // 8) - 1, 0), 0)),
            pl.BlockSpec((CHUNK, DN_V), lambda n: (order(n), GATE_COL)),
            pl.BlockSpec((CHUNK, LANES), lambda n: (order(n), BA_COL)),
            pl.BlockSpec((CONV_K, 1, DN_CONV_CH), lambda n: (0, 0, 0)),
            pl.BlockSpec((1, LANES), lambda n: (0, 0)),
            pl.BlockSpec((1, LANES), lambda n: (0, 0)),
            pl.BlockSpec((1, DN_DV), lambda n: (0, 0))]


def _dn_front(c, seq, x_ref, halo_ref, ba_ref, cw_ref, al_ref, dt_ref):
    valid = _valid_rows(c * CHUNK, CHUNK, seq)
    xin = x_ref[...] * valid
    halo = halo_ref[...] * _valid_rows(c * CHUNK - 8, 8, seq)
    x_sh = [xin] + [_shift_down(xin, halo, k) for k in range(1, CONV_K)]
    yc = x_sh[0] * cw_ref[CONV_K - 1]
    for k in range(1, CONV_K):
        yc = yc + x_sh[k] * cw_ref[CONV_K - 1 - k]
    sgc = jax.nn.sigmoid(yc)
    ba = ba_ref[...]
    sig = jax.nn.sigmoid(ba)
    beta = sig * valid
    z = ba + dt_ref[...]
    eal = jnp.exp(al_ref[...])
    g = -eal * _softplus(z) * valid
    ri, ci = _iota((CHUNK, CHUNK), 0), _iota((CHUNK, CHUNK), 1)
    lower = (ri >= ci).astype(F32)
    upper = (ri <= ci).astype(F32)
    eye = (ri == ci).astype(F32)
    gam = _nn(lower, g, hi=True)
    gam_t = _tn(g, upper, hi=True)
    return dict(valid=valid, x_sh=x_sh, yc=yc, sgc=sgc, act=yc * sgc, sig=sig, beta=beta, z=z,
                eal=eal, g=g, gam=gam, gam_t=gam_t, ri=ri, ci=ci, upper=upper, eye=eye)


def _dn_head(f, h):
    act = f["act"]
    q_raw = act[:, h * DN_DK:(h + 1) * DN_DK]
    k_raw = act[:, DN_QK + h * DN_DK:DN_QK + (h + 1) * DN_DK]
    v = act[:, 2 * DN_QK + h * DN_DV:2 * DN_QK + (h + 1) * DN_DV]
    rq = lax.rsqrt(jnp.sum(q_raw * q_raw, axis=-1, keepdims=True) + RMS_EPS)
    rk = lax.rsqrt(jnp.sum(k_raw * k_raw, axis=-1, keepdims=True) + RMS_EPS)
    qh = q_raw * rq
    kn = k_raw * rk
    gam_c = _col(f["gam"], DECAY_LANE + h)
    gam_r = _row(f["gam_t"], DECAY_LANE + h)
    bc = _col(f["beta"], BETA_LANE + h)
    diff = gam_c - gam_r
    decay = jnp.where(f["ri"] >= f["ci"], jnp.exp(jnp.minimum(diff, 0.0)), 0.0)
    glast = jnp.sum(gam_r * (_iota((1, CHUNK), 1) == CHUNK - 1).astype(F32), axis=1, keepdims=True)
    return dict(rq=rq, rk=rk, qh=qh, qn=qh * (DN_DK ** -0.5), kn=kn, v=v, gam_c=gam_c, gam_r=gam_r,
                bc=bc, diff=diff, decay=decay, egam=jnp.exp(gam_c), glast=glast,
                eglast=jnp.exp(glast), ekd=jnp.exp(glast - gam_c))


def _dn_fwd(proj, conv_w, alog, dtb, norm_w, seq):
    rows = proj.shape[0]
    nc = rows // CHUNK

    def body(x_ref, halo_ref, gate_ref, ba_ref, cw_ref, al_ref, dt_ref, nw_ref,
             o_ref, y_ref, ss_ref, t_ref, s_ref):
        n = pl.program_id(0)

        @pl.when(n == 0)
        def _():
            s_ref[...] = jnp.zeros_like(s_ref)

        f = _dn_front(n, seq, x_ref, halo_ref, ba_ref, cw_ref, al_ref, dt_ref)
        ri, ci = f["ri"], f["ci"]
        eye = f["eye"]
        diag_m = (jnp.right_shift(ri, INV_SHIFT) == jnp.right_shift(ci, INV_SHIFT)).astype(F32)
        half_m = (jnp.right_shift(ri, INV_SHIFT + 1) == jnp.right_shift(ci, INV_SHIFT + 1)).astype(F32)
        nw = nw_ref[...]
        heads = [_dn_head(f, h) for h in range(DN_HEADS)]
        a_all = [jnp.where(ri > ci, hd["bc"] * _nt(hd["kn"], hd["kn"]) * hd["decay"], 0.0) for hd in heads]
        b_all = [a * diag_m for a in a_all]
        t_all = [eye - b for b in b_all]
        for _ in range(INV_SQUARINGS):
            b_all = [_nn(b, b, hi=True) for b in b_all]
            t_all = [t + _nn(t, b, hi=True) for t, b in zip(t_all, b_all)]
        for off_m in (half_m - diag_m, 1.0 - half_m):
            x_all = [_nn(a * off_m, t, hi=True) for a, t in zip(a_all, t_all)]
            t_all = [t - _nn(t, x, hi=True) for t, x in zip(t_all, x_all)]
        u_all = [_nn(t, hd["v"] * hd["bc"], hi=True) for t, hd in zip(t_all, heads)]
        w_all = [_nn(t, hd["kn"] * (hd["bc"] * hd["egam"]), hi=True) for t, hd in zip(t_all, heads)]
        for h in range(DN_HEADS):
            hd = heads[h]
            v_cols = slice(h * DN_DV, (h + 1) * DN_DV)
            t_ref[0, h] = t_all[h]
            s = s_ref[h]
            ss_ref[0, h] = s
            u, w = u_all[h], w_all[h]
            v_new = u - _nn(w, s)
            qk = _nt(hd["qn"], hd["kn"]) * hd["decay"]
            o = _nn(hd["qn"] * hd["egam"], s) + _nn(qk, v_new)
            s_ref[h] = s * hd["eglast"] + _tn(hd["kn"] * hd["ekd"], v_new)
            o_ref[:, v_cols] = o
            y_ref[:, v_cols] = _gated_norm(o, gate_ref[:, v_cols], nw).astype(y_ref.dtype)

    fwd = lambda n: n
    row_v = pl.BlockSpec((CHUNK, DN_V), lambda n: (n, 0))
    return pl.pallas_call(
        body, grid=(nc,), in_specs=_dn_in_specs(fwd),
        out_specs=[row_v, row_v,
                   pl.BlockSpec((1, DN_HEADS, DN_DK, DN_DV), lambda n: (n, 0, 0, 0)),
                   pl.BlockSpec((1, DN_HEADS, CHUNK, CHUNK), lambda n: (n, 0, 0, 0))],
        out_shape=[jax.ShapeDtypeStruct((rows, DN_V), F32), jax.ShapeDtypeStruct((rows, DN_V), BF16),
                   jax.ShapeDtypeStruct((nc, DN_HEADS, DN_DK, DN_DV), F32),
                   jax.ShapeDtypeStruct((nc, DN_HEADS, CHUNK, CHUNK), F32)],
        scratch_shapes=[pltpu.VMEM((DN_HEADS, DN_DK, DN_DV), F32)],
        name="dn_fwd", compiler_params=_params("arbitrary"))(
            proj, proj, proj, proj, conv_w, alog, dtb, norm_w.reshape(1, DN_DV))


def _dn_bwd(proj, o, dy, states, tinv, conv_w, alog, dtb, norm_w, seq):
    rows = proj.shape[0]
    nc = rows // CHUNK

    def body(x_ref, halo_ref, gate_ref, ba_ref, cw_ref, al_ref, dt_ref, nw_ref,
             o_ref, dy_ref, ss_ref, t_ref,
             dp_ref, dcw_ref, dal_ref, ddt_ref, dnw_ref, ds_ref, nxt_ref):
        n = pl.program_id(0)

        @pl.when(n == 0)
        def _():
            ds_ref[...] = jnp.zeros_like(ds_ref)
            nxt_ref[...] = jnp.zeros_like(nxt_ref)
            dcw_ref[...] = jnp.zeros_like(dcw_ref)
            dal_ref[...] = jnp.zeros_like(dal_ref)
            ddt_ref[...] = jnp.zeros_like(ddt_ref)
            dnw_ref[...] = jnp.zeros_like(dnw_ref)

        f = _dn_front(nc - 1 - n, seq, x_ref, halo_ref, ba_ref, cw_ref, al_ref, dt_ref)
        ri, ci = f["ri"], f["ci"]
        strict = (ri > ci).astype(F32)
        nw = nw_ref[...]
        lane128 = _iota((1, LANES), 1)
        row128 = _iota((LANES, 1), 0)
        dgam_col = jnp.zeros((CHUNK, LANES), F32)
        dgam_row = jnp.zeros((LANES, CHUNK), F32)
        dbeta = jnp.zeros((CHUNK, LANES), F32)
        dnw = jnp.zeros((1, DN_DV), F32)
        hs = range(DN_HEADS)
        heads = [_dn_head(f, h) for h in hs]
        cols = [slice(h * DN_DV, (h + 1) * DN_DV) for h in hs]
        t_l = [t_ref[0, h] for h in hs]
        s_l = [ss_ref[0, h] for h in hs]
        ds_l = [ds_ref[h] for h in hs]
        kk_l = [_nt(hd["kn"], hd["kn"]) for hd in heads]
        p_l = [_nt(hd["qn"], hd["kn"]) for hd in heads]
        rhsw_l = [hd["kn"] * (hd["bc"] * hd["egam"]) for hd in heads]
        u_l = [_nn(t, hd["v"] * hd["bc"], hi=True) for t, hd in zip(t_l, heads)]
        w_l = [_nn(t, r, hi=True) for t, r in zip(t_l, rhsw_l)]
        vnew_l = [u - _nn(w, s) for u, w, s in zip(u_l, w_l, s_l)]
        gnb = [_gated_norm_bwd(dy_ref[:, c], o_ref[:, c], gate_ref[:, c], nw) for c in cols]
        do_l = [x[0] for x in gnb]
        for h in hs:
            dp_ref[:, DN_CONV_CH + h * DN_DV:DN_CONV_CH + (h + 1) * DN_DV] = gnb[h][1].astype(dp_ref.dtype)
            dnw = dnw + gnb[h][2]
        qg_l = [hd["qn"] * hd["egam"] for hd in heads]
        kd_l = [hd["kn"] * hd["ekd"] for hd in heads]
        dvnew_l = [_tn(p * hd["decay"], do) + _nn(kd, ds)
                   for p, hd, do, kd, ds in zip(p_l, heads, do_l, kd_l, ds_l)]
        m_l = [_nt(do, vn) for do, vn in zip(do_l, vnew_l)]
        dqg_l = [_nt(do, s) for do, s in zip(do_l, s_l)]
        dkd_l = [_nt(vn, ds) for vn, ds in zip(vnew_l, ds_l)]
        for h in hs:
            ds_ref[h] = (ds_l[h] * heads[h]["eglast"] + _tn(qg_l[h], do_l[h]) - _tn(w_l[h], dvnew_l[h]))
        dw_l = [-_nt(dvn, s) for dvn, s in zip(dvnew_l, s_l)]
        dru_l = [_tn(t, dvn, hi=True) for t, dvn in zip(t_l, dvnew_l)]
        drw_l = [_tn(t, dw_, hi=True) for t, dw_ in zip(t_l, dw_l)]
        da_l = [-(_nt(dru, u) + _nt(drw, w)) * strict for dru, u, drw, w in zip(dru_l, u_l, drw_l, w_l)]
        dp_l = [m * hd["decay"] for m, hd in zip(m_l, heads)]
        dkk_l = [da * (hd["bc"] * hd["decay"]) for da, hd in zip(da_l, heads)]
        dqn_l = [dqg * hd["egam"] + _nn(dp, hd["kn"]) for dqg, hd, dp in zip(dqg_l, heads, dp_l)]
        dkn_l = [_tn(dp, hd["qn"]) + dkd * hd["ekd"] + drw * (hd["bc"] * hd["egam"])
                 + _nn(dkk, hd["kn"]) + _tn(dkk, hd["kn"])
                 for dp, hd, dkd, drw, dkk in zip(dp_l, heads, dkd_l, drw_l, dkk_l)]
        dq_parts, dk_parts, dv_parts = [], [], []
        for h in hs:
            hd = heads[h]
            kn, v, bc, egam, decay = hd["kn"], hd["v"], hd["bc"], hd["egam"], hd["decay"]
            t1 = jnp.sum(dkd_l[h] * kd_l[h], axis=1, keepdims=True)
            dglast = (jnp.sum(t1, axis=0, keepdims=True)
                      + jnp.sum(jnp.sum(ds_l[h] * s_l[h], axis=1, keepdims=True), axis=0, keepdims=True)
                      * hd["eglast"])
            e = (m_l[h] * p_l[h] + da_l[h] * (bc * kk_l[h])) * decay
            dgc = (jnp.sum(dqg_l[h] * qg_l[h], axis=1, keepdims=True) - t1
                   + jnp.sum(drw_l[h] * rhsw_l[h], axis=1, keepdims=True)
                   + jnp.sum(e, axis=1, keepdims=True)
                   + jnp.where(_iota((CHUNK, 1), 0) == CHUNK - 1, dglast, 0.0))
            dgr = -jnp.sum(e, axis=0, keepdims=True)
            dbc = (jnp.sum(dru_l[h] * v, axis=1, keepdims=True)
                   + jnp.sum(drw_l[h] * kn, axis=1, keepdims=True) * egam
                   + jnp.sum(da_l[h] * kk_l[h] * decay, axis=1, keepdims=True))
            dv_parts.append(dru_l[h] * bc)
            qh, dqn, dkn = hd["qh"], dqn_l[h], dkn_l[h]
            dq_parts.append(((DN_DK ** -0.5) * hd["rq"])
                            * (dqn - qh * jnp.sum(dqn * qh, axis=1, keepdims=True)))
            dk_parts.append(hd["rk"] * (dkn - kn * jnp.sum(dkn * kn, axis=1, keepdims=True)))
            dgam_col = dgam_col + dgc * (lane128 == DECAY_LANE + h).astype(F32)
            dbeta = dbeta + dbc * (lane128 == BETA_LANE + h).astype(F32)
            dgam_row = dgam_row + (row128 == DECAY_LANE + h).astype(F32) * dgr
        dnw_ref[...] += dnw
        dgam = dgam_col + _nt(f["eye"], dgam_row, hi=True)
        dg = _nn(f["upper"], dgam, hi=True)
        d_a = dg * (-f["eal"]) * jax.nn.sigmoid(f["z"]) * f["valid"]
        dal_ref[...] += jnp.sum(dg * f["g"], axis=0, keepdims=True)
        ddt_ref[...] += jnp.sum(d_a, axis=0, keepdims=True)
        d_b = dbeta * f["valid"] * f["sig"] * (1.0 - f["sig"])
        dp_ref[:, DN_CONV_CH + DN_V:] = (d_a + d_b).astype(dp_ref.dtype)
        dact = jnp.concatenate(dq_parts + dk_parts + dv_parts, axis=1)
        yc, sgc = f["yc"], f["sgc"]
        dyc = dact * (sgc * (1.0 + yc * (1.0 - sgc)))
        for k in range(CONV_K):
            dcw_ref[k] += jnp.sum(dyc * f["x_sh"][CONV_K - 1 - k], axis=0, keepdims=True)
        nxt = nxt_ref[...]
        dx = dyc * cw_ref[CONV_K - 1]
        for j in range(1, CONV_K):
            dx = dx + _shift_up(dyc, nxt, j) * cw_ref[CONV_K - 1 - j]
        nxt_ref[...] = dyc[0:8]
        dp_ref[:, :DN_CONV_CH] = (dx * f["valid"]).astype(dp_ref.dtype)

    rev = lambda n: nc - 1 - n
    row_v = pl.BlockSpec((CHUNK, DN_V), lambda n: (rev(n), 0))
    vec = pl.BlockSpec((1, LANES), lambda n: (0, 0))
    return pl.pallas_call(
        body, grid=(nc,),
        in_specs=_dn_in_specs(rev) + [
            row_v, row_v,
            pl.BlockSpec((1, DN_HEADS, DN_DK, DN_DV), lambda n: (rev(n), 0, 0, 0)),
            pl.BlockSpec((1, DN_HEADS, CHUNK, CHUNK), lambda n: (rev(n), 0, 0, 0))],
        out_specs=[pl.BlockSpec((CHUNK, DN_IN_PAD), lambda n: (rev(n), 0)),
                   pl.BlockSpec((CONV_K, 1, DN_CONV_CH), lambda n: (0, 0, 0)), vec, vec,
                   pl.BlockSpec((1, DN_DV), lambda n: (0, 0))],
        out_shape=[jax.ShapeDtypeStruct((rows, DN_IN_PAD), BF16),
                   jax.ShapeDtypeStruct((CONV_K, 1, DN_CONV_CH), F32),
                   jax.ShapeDtypeStruct((1, LANES), F32), jax.ShapeDtypeStruct((1, LANES), F32),
                   jax.ShapeDtypeStruct((1, DN_DV), F32)],
        scratch_shapes=[pltpu.VMEM((DN_HEADS, DN_DK, DN_DV), F32), pltpu.VMEM((8, DN_CONV_CH), F32)],
        name="dn_bwd", compiler_params=_params("arbitrary"))(
            proj, proj, proj, proj, conv_w, alog, dtb, norm_w.reshape(1, DN_DV), o, dy, states, tinv)


def _local_step(x, tgt, wts):
    seq = x.shape[0]
    rows = -(-(seq + CHUNK) // ROW_ALIGN) * ROW_ALIGN
    tail = rows - seq - CHUNK
    h0 = jnp.concatenate([jnp.zeros((PAD, D_MODEL), F32), wts["meta_tokens"].astype(F32), x,
                          jnp.zeros((tail, D_MODEL), F32)], axis=0)
    tgt_p = jnp.concatenate([jnp.zeros((CHUNK, D_MODEL), F32), tgt, jnp.zeros((tail, D_MODEL), F32)],
                            axis=0)
    cos, sin = _rope_tables(rows)
    consts = _ret_consts()
    conv_w = wts["dn_conv_w"].reshape(CONV_K, 1, DN_CONV_CH)
    lane_pad = LANES - 2 * DN_HEADS
    alog = jnp.pad(wts["dn_a_log"].reshape(1, DN_HEADS), ((0, 0), (DECAY_LANE, lane_pad)))
    dtb = jnp.pad(wts["dn_dt_bias"].reshape(1, DN_HEADS), ((0, 0), (DECAY_LANE, lane_pad)))
    g = {}

    hn0 = _rms_fwd(h0, wts["mix_norm_w"][0], "rms_mix0")
    proj0 = _mm_cols(hn0, wts["ret_w_in"], "ret_in")
    o0, y0, st0 = _ret_fwd(proj0, cos, sin, consts, wts["ret_gn_w"], seq)
    h1 = _mm(y0, wts["ret_w_out"], mode="nn", name="ret_out", resid=h0)
    hn1 = _rms_fwd(h1, wts["ffn_norm_w"][0], "rms_ffn0")
    g0, u0, act0 = _ffn_up(hn1, wts["ffn_w_gate"], wts["ffn_w_up"], 0, "ffn_up0")
    h2 = _ffn_down(act0, wts["ffn_w_down"], h1, 0, "ffn_down0")
    hn2 = _rms_fwd(h2, wts["mix_norm_w"][1], "rms_mix1")
    proj1 = _mm(hn2, wts["dn_w_in"], mode="nn", name="dn_in")
    o1, y1, st1, tinv = _dn_fwd(proj1, conv_w, alog, dtb, wts["dn_norm_w"], seq)
    h3 = _mm(y1, wts["dn_w_out"], mode="nn", name="dn_out", resid=h2)
    hn3 = _rms_fwd(h3, wts["ffn_norm_w"][1], "rms_ffn1")
    g1, u1, act1 = _ffn_up(hn3, wts["ffn_w_gate"], wts["ffn_w_up"], 1, "ffn_up1")
    h4 = _ffn_down(act1, wts["ffn_w_down"], h3, 1, "ffn_down1")

    dh4, g["final_norm_w"], loss = _final_loss(h4, wts["final_norm_w"], tgt_p, seq, "final_loss")

    layers = wts["ffn_w_gate"].shape[1]

    def ffn_bwd(dh_out, h_mid, hn, gg, uu, act, layer, prev):
        tag = str(layer)
        dg, du = _ffn_down_bwd(dh_out, wts["ffn_w_down"], gg, uu, layer, "ffn_down_bwd" + tag)
        d_down = _ffn_wgrad(act, [dh_out], layer, layers, prev and prev[:1], True, "ffn_dwd" + tag)
        d_gu = _ffn_wgrad(hn, [dg, du], layer, layers, prev and prev[1:], False, "ffn_dwgu" + tag)
        dhn = _ffn_up_bwd(dg, du, wts["ffn_w_gate"], wts["ffn_w_up"], layer, "ffn_up_bwd" + tag)
        dh_mid, d_norm = _rms_bwd(dhn, h_mid, wts["ffn_norm_w"][layer], dh_out, "rms_ffn_bwd" + tag)
        return dh_mid, list(d_down) + list(d_gu), d_norm

    dh3, ffn_grads, dfn1 = ffn_bwd(dh4, h3, hn3, g1, u1, act1, 1, None)
    dy1 = _mm(dh3, wts["dn_w_out"], mode="nt", name="dn_out_bwd")
    g["dn_w_out"] = _mm(y1, dh3, mode="tn", name="dn_dwo")
    dproj1, dcw, dal, ddt, g["dn_norm_w"] = _dn_bwd(proj1, o1, dy1, st1, tinv, conv_w, alog, dtb,
                                                    wts["dn_norm_w"], seq)
    g["dn_w_in"] = _mm(hn2, dproj1, mode="tn", name="dn_dwi")
    dhn2 = _mm(dproj1, wts["dn_w_in"], mode="nt", name="dn_in_bwd", row_cap=256)
    dh2, dmn1 = _rms_bwd(dhn2, h2, wts["mix_norm_w"][1], dh3, "rms_mix_bwd1")
    g["dn_conv_w"] = dcw.reshape(CONV_K, DN_CONV_CH)
    g["dn_a_log"] = dal[0, DECAY_LANE:DECAY_LANE + DN_HEADS]
    g["dn_dt_bias"] = ddt[0, DECAY_LANE:DECAY_LANE + DN_HEADS]

    dh1, ffn_grads, dfn0 = ffn_bwd(dh2, h1, hn1, g0, u0, act0, 0, ffn_grads)
    dy0 = _mm(dh1, wts["ret_w_out"], mode="nt", name="ret_out_bwd")
    g["ret_w_out"] = _mm(y0, dh1, mode="tn", name="ret_dwo")
    dproj0, g["ret_gn_w"] = _ret_bwd(proj0, o0, dy0, st0, cos, sin, consts, wts["ret_gn_w"], seq)
    g["ret_w_in"] = _mm_cols_grad(hn0, dproj0, "ret_dwi")
    dhn0 = _mm_cols_t(dproj0, wts["ret_w_in"], "ret_in_bwd")
    dh0, dmn0 = _rms_bwd(dhn0, h0, wts["mix_norm_w"][0], dh1, "rms_mix_bwd0")

    g["ffn_w_down"], g["ffn_w_gate"], g["ffn_w_up"] = ffn_grads
    g["ffn_norm_w"] = jnp.concatenate([dfn0, dfn1], axis=0)
    g["mix_norm_w"] = jnp.concatenate([dmn0, dmn1], axis=0)
    g["meta_tokens"] = dh0[PAD:CHUNK]
    g["final_norm_w"] = g["final_norm_w"].reshape(D_MODEL)
    g["ret_gn_w"] = g["ret_gn_w"].reshape(RET_DV)
    g["dn_norm_w"] = g["dn_norm_w"].reshape(DN_DV)
    return loss, dh0, g


def _mesh_pos():
    return lax.axis_index("x"), lax.axis_index("y"), lax.axis_index("c")


def _other_chips(x, y):
    return [(1 - x, y), (x, 1 - y), (1 - x, 1 - y)]


def _remote(src, dst, send_sem, recv_sem, to):
    return pltpu.make_async_remote_copy(src_ref=src, dst_ref=dst, send_sem=send_sem, recv_sem=recv_sem,
                                        device_id=to, device_id_type=MESH)


GATHER_COPIES = 7


def _gather_weights(shards):
    nt = len(shards)

    def body(*refs):
        ins, outs = refs[:nt], refs[nt:2 * nt]
        send_sems, recv_sems = refs[2 * nt:]
        x, y, c = _mesh_pos()
        me = 2 * x + y
        chips = _other_chips(x, y)
        sibling = (x, y, 1 - c)

        def cp(t, k, src, dst, to):
            i = GATHER_COPIES * t + k
            return _remote(src, dst, send_sems.at[i], recv_sems.at[i], to)

        started = []
        for t in range(nt):
            started.append(cp(t, 0, ins[t], outs[t].at[me], sibling))
            for k, (px, py) in enumerate(chips):
                started.append(cp(t, 1 + k, ins[t].at[c], outs[t].at[me, c], (px, py, c)))
        for s in started:
            s.start()
        for t in range(nt):
            for k, (px, py) in enumerate(chips):
                landed = outs[t].at[2 * px + py, c]
                cp(t, 1 + k, ins[t].at[c], landed, (px, py, c)).wait_recv()
                fwd = cp(t, 4 + k, landed, landed, sibling)
                fwd.start()
                started.append(fwd)
        for t in range(nt):
            cp(t, 0, ins[t], outs[t].at[me], sibling).wait_recv()
            for k, (px, py) in enumerate(chips):
                theirs = outs[t].at[2 * px + py, 1 - c]
                cp(t, 4 + k, theirs, theirs, sibling).wait_recv()
        for s in started:
            s.wait_send()

    return pl.pallas_call(
        body, out_shape=[jax.ShapeDtypeStruct((N_SHARD,) + s.shape, s.dtype) for s in shards],
        in_specs=[ANY] * nt, out_specs=[ANY] * nt,
        scratch_shapes=[pltpu.SemaphoreType.DMA((GATHER_COPIES * nt,)),
                        pltpu.SemaphoreType.DMA((GATHER_COPIES * nt,))],
        name="gather_weights")(*shards)


def _gather_small(blk):
    r, wd = blk.shape

    def body(b_ref, out_ref, send_sems, recv_sems):
        x, y, c = _mesh_pos()
        chips = _other_chips(x, y)
        out_ref[2 * x + y] = b_ref[...]
        sends = [_remote(b_ref, out_ref.at[2 * x + y], send_sems.at[k], recv_sems.at[k], (px, py, c))
                 for k, (px, py) in enumerate(chips)]
        for cp in sends:
            cp.start()
        for k, (px, py) in enumerate(chips):
            _remote(b_ref, out_ref.at[2 * px + py], send_sems.at[k], recv_sems.at[k], (px, py, c)).wait_recv()
        for cp in sends:
            cp.wait_send()

    return pl.pallas_call(
        body, out_shape=jax.ShapeDtypeStruct((4, r, wd), blk.dtype), in_specs=[VMEM_SPEC], out_specs=VMEM_SPEC,
        scratch_shapes=[pltpu.SemaphoreType.DMA((3,)), pltpu.SemaphoreType.DMA((3,))],
        name="gather_small")(blk)


def _allreduce_small(blk):
    r, wd = blk.shape
    rels = [(dx, dy, dc) for dx in (0, 1) for dy in (0, 1) for dc in (0, 1) if dx or dy or dc]

    def body(b_ref, out_ref, buf_ref, send_sems, recv_sems):
        x, y, c = _mesh_pos()

        def peer(rel):
            dx, dy, dc = rel
            return (1 - x if dx else x, 1 - y if dy else y, 1 - c if dc else c)

        me = 4 * x + 2 * y + c
        buf_ref[me] = b_ref[...]
        sends = [_remote(b_ref, buf_ref.at[me], send_sems.at[k], recv_sems.at[k], peer(rel))
                 for k, rel in enumerate(rels)]
        for cp in sends:
            cp.start()
        for k, rel in enumerate(rels):
            px, py, pc = peer(rel)
            _remote(b_ref, buf_ref.at[4 * px + 2 * py + pc], send_sems.at[k], recv_sems.at[k],
                    (px, py, pc)).wait_recv()
        for cp in sends:
            cp.wait_send()
        acc = buf_ref[0]
        for d in range(1, 8):
            acc = acc + buf_ref[d]
        out_ref[...] = acc

    return pl.pallas_call(
        body, out_shape=jax.ShapeDtypeStruct((r, wd), blk.dtype), in_specs=[VMEM_SPEC], out_specs=VMEM_SPEC,
        scratch_shapes=[pltpu.VMEM((8, r, wd), blk.dtype), pltpu.SemaphoreType.DMA((7,)),
                        pltpu.SemaphoreType.DMA((7,))],
        name="allreduce_small")(blk)


def _rs_pair(gs):
    nt = len(gs)

    def body(*refs):
        ins, outs = refs[:nt], refs[nt:2 * nt]
        send_sems, recv_sems = refs[2 * nt:]
        x, y, c = _mesh_pos()
        cps = [_remote(ins[t].at[:, 1 - c], outs[t], send_sems.at[t], recv_sems.at[t], (x, y, 1 - c))
               for t in range(nt)]
        for cp in cps:
            cp.start()
        for cp in cps:
            cp.wait()

    return pl.pallas_call(
        body, out_shape=[jax.ShapeDtypeStruct(g.shape[:1] + g.shape[2:], g.dtype) for g in gs],
        in_specs=[ANY] * nt, out_specs=[ANY] * nt,
        scratch_shapes=[pltpu.SemaphoreType.DMA((nt,)), pltpu.SemaphoreType.DMA((nt,))], name="rs_pair")(*gs)


def _rs_tile(a, b):
    return _div_tile(a, 512 if b <= 1024 else 256, 16)


def _rs_pair_add(g, a, idx, name):
    _, _, rows, cols = g.shape
    tr = _rs_tile(rows, cols)

    def body(s_ref, g_ref, a_ref, p_ref):
        p_ref[...] = (g_ref[...] + a_ref[...]).astype(p_ref.dtype)

    blk = pl.BlockSpec((None, tr, cols), lambda j, i, s: (j, i, 0))
    spec = pltpu.PrefetchScalarGridSpec(
        num_scalar_prefetch=1, grid=(N_SHARD, rows // tr),
        in_specs=[pl.BlockSpec((None, None, tr, cols), lambda j, i, s: (j, s[0], i, 0)), blk], out_specs=blk)
    return pl.pallas_call(
        body, grid_spec=spec, out_shape=jax.ShapeDtypeStruct((N_SHARD, rows, cols), BF16), name=name,
        compiler_params=_params("parallel", "parallel"))(idx, g, a)


def _rs_chips(ps):
    nt = len(ps)

    def body(*refs):
        ins, outs = refs[:nt], refs[nt:2 * nt]
        send_sems, recv_sems = refs[2 * nt:]
        x, y, c = _mesh_pos()
        cps = [_remote(ins[t].at[2 * px + py], outs[t].at[k], send_sems.at[3 * t + k], recv_sems.at[3 * t + k],
                       (px, py, c))
               for t in range(nt) for k, (px, py) in enumerate(_other_chips(x, y))]
        for cp in cps:
            cp.start()
        for cp in cps:
            cp.wait()

    return pl.pallas_call(
        body, out_shape=[jax.ShapeDtypeStruct((3,) + p.shape[1:], p.dtype) for p in ps],
        in_specs=[ANY] * nt, out_specs=[ANY] * nt,
        scratch_shapes=[pltpu.SemaphoreType.DMA((3 * nt,)), pltpu.SemaphoreType.DMA((3 * nt,))],
        name="rs_chips")(*ps)


def _rs_final_add(g, a, b, idx, name):
    _, _, rows, cols = g.shape
    tr = _rs_tile(rows, cols)

    def body(s_ref, g_ref, a_ref, b0_ref, b1_ref, b2_ref, f_ref):
        own = g_ref[...] + a_ref[...]
        f_ref[...] = ((own + b0_ref[...].astype(F32)) + b1_ref[...].astype(F32)) + b2_ref[...].astype(F32)

    def b_spec(k):
        return pl.BlockSpec((None, tr, cols), lambda i, s: (k, i, 0))

    spec = pltpu.PrefetchScalarGridSpec(
        num_scalar_prefetch=1, grid=(rows // tr,),
        in_specs=[pl.BlockSpec((None, None, tr, cols), lambda i, s: (s[1], s[0], i, 0)),
                  pl.BlockSpec((None, tr, cols), lambda i, s: (s[1], i, 0)), b_spec(0), b_spec(1), b_spec(2)],
        out_specs=pl.BlockSpec((None, tr, cols), lambda i, s: (s[0], i, 0)))
    return pl.pallas_call(
        body, grid_spec=spec, out_shape=jax.ShapeDtypeStruct((2, rows, cols), F32), name=name,
        compiler_params=_params("parallel"))(idx, g, a, b, b, b)


def _rs_share(fs):
    nt = len(fs)

    def body(*refs):
        outs = refs[nt:2 * nt]
        send_sems, recv_sems = refs[2 * nt:]
        x, y, c = _mesh_pos()
        cps = [_remote(outs[t].at[c], outs[t].at[c], send_sems.at[t], recv_sems.at[t], (x, y, 1 - c))
               for t in range(nt)]
        for cp in cps:
            cp.start()
        for cp in cps:
            cp.wait()

    return pl.pallas_call(
        body, out_shape=[jax.ShapeDtypeStruct(f.shape, f.dtype) for f in fs],
        in_specs=[ANY] * nt, out_specs=[ANY] * nt, input_output_aliases={t: t for t in range(nt)},
        scratch_shapes=[pltpu.SemaphoreType.DMA((nt,)), pltpu.SemaphoreType.DMA((nt,))], name="rs_share")(*fs)


def _adamw(w, g, m, v, name):
    rows, cols = w.shape
    tr = rows // 4 if rows % 32 == 0 else rows

    def body(w_ref, g_ref, m_ref, v_ref, d_ref, mo_ref, vo_ref):
        gv = g_ref[...]
        mn = ADAM_B1 * m_ref[...] + (1.0 - ADAM_B1) * gv
        vn = ADAM_B2 * v_ref[...] + (1.0 - ADAM_B2) * (gv * gv)
        m_hat = mn / (1.0 - ADAM_B1 ** ADAM_STEP)
        v_hat = vn / (1.0 - ADAM_B2 ** ADAM_STEP)
        d_ref[...] = -ADAM_LR * (m_hat / (jnp.sqrt(v_hat) + ADAM_EPS) + ADAM_WD * w_ref[...])
        mo_ref[...] = mn
        vo_ref[...] = vn

    blk = pl.BlockSpec((tr, cols), lambda i: (i, 0))
    out = jax.ShapeDtypeStruct((rows, cols), F32)
    return pl.pallas_call(
        body, grid=(rows // tr,), in_specs=[blk] * 4, out_specs=[blk] * 3, out_shape=[out] * 3, name=name,
        compiler_params=_params("parallel"))(w, g, m, v)


BIG = ["ret_w_in", "ret_w_out", "dn_w_in", "dn_w_out", "ffn_w_gate", "ffn_w_up", "ffn_w_down"]
SMALL =["meta_tokens", "mix_norm_w", "ffn_norm_w", "ret_gn_w", "dn_conv_w", "dn_a_log", "dn_dt_bias",
         "dn_norm_w", "final_norm_w"]
SMALL_SHARDED = {"meta_tokens", "dn_conv_w", "dn_norm_w"}
ORDER = ["meta_tokens", "mix_norm_w", "ffn_norm_w", "ret_w_in", "ret_gn_w", "ret_w_out", "dn_w_in",
         "dn_conv_w", "dn_a_log", "dn_dt_bias", "dn_norm_w", "dn_w_out", "ffn_w_gate", "ffn_w_up",
         "ffn_w_down", "final_norm_w"]


def _halves(a):
    return a.reshape(2, -1, a.shape[-1])


def _pack_lanes(parts, align=8):
    flat = jnp.concatenate([p.reshape(-1) for p in parts])
    flat = jnp.pad(flat, (0, -flat.shape[0] % (align * LANES)))
    return flat.reshape(-1, LANES)


def _unpack(buf, shapes):
    lead = buf.shape[:-2]
    flat = buf.reshape(lead + (-1,))
    out, off = [], 0
    for shp in shapes:
        size = math.prod(shp)
        out.append(flat[..., off:off + size].reshape(lead + tuple(shp)))
        off += size
    return out


def _join_cols(shards):
    return jnp.concatenate([shards[j] for j in range(N_SHARD)], axis=-1)


def kernel(x, meta_tokens, mix_norm_w, ffn_norm_w, ret_w_in, ret_gn_w, ret_w_out, dn_w_in, dn_conv_w, dn_a_log, dn_dt_bias, dn_norm_w, dn_w_out, ffn_w_gate, ffn_w_up, ffn_w_down, final_norm_w, loss_target, m_meta_tokens, m_mix_norm_w, m_ffn_norm_w, m_ret_w_in, m_ret_gn_w, m_ret_w_out, m_dn_w_in, m_dn_conv_w, m_dn_a_log, m_dn_dt_bias, m_dn_norm_w, m_dn_w_out, m_ffn_w_gate, m_ffn_w_up, m_ffn_w_down, m_final_norm_w, v_meta_tokens, v_mix_norm_w, v_ffn_norm_w, v_ret_w_in, v_ret_gn_w, v_ret_w_out, v_dn_w_in, v_dn_conv_w, v_dn_a_log, v_dn_dt_bias, v_dn_norm_w, v_dn_w_out, v_ffn_w_gate, v_ffn_w_up, v_ffn_w_down, v_final_norm_w):
    w = dict(meta_tokens=meta_tokens, mix_norm_w=mix_norm_w, ffn_norm_w=ffn_norm_w, ret_w_in=ret_w_in,
             ret_gn_w=ret_gn_w, ret_w_out=ret_w_out, dn_w_in=dn_w_in, dn_conv_w=dn_conv_w, dn_a_log=dn_a_log,
             dn_dt_bias=dn_dt_bias, dn_norm_w=dn_norm_w, dn_w_out=dn_w_out, ffn_w_gate=ffn_w_gate,
             ffn_w_up=ffn_w_up, ffn_w_down=ffn_w_down, final_norm_w=final_norm_w)
    m = dict(meta_tokens=m_meta_tokens, mix_norm_w=m_mix_norm_w, ffn_norm_w=m_ffn_norm_w, ret_w_in=m_ret_w_in,
             ret_gn_w=m_ret_gn_w, ret_w_out=m_ret_w_out, dn_w_in=m_dn_w_in, dn_conv_w=m_dn_conv_w,
             dn_a_log=m_dn_a_log, dn_dt_bias=m_dn_dt_bias, dn_norm_w=m_dn_norm_w, dn_w_out=m_dn_w_out,
             ffn_w_gate=m_ffn_w_gate, ffn_w_up=m_ffn_w_up, ffn_w_down=m_ffn_w_down, final_norm_w=m_final_norm_w)
    v = dict(meta_tokens=v_meta_tokens, mix_norm_w=v_mix_norm_w, ffn_norm_w=v_ffn_norm_w, ret_w_in=v_ret_w_in,
             ret_gn_w=v_ret_gn_w, ret_w_out=v_ret_w_out, dn_w_in=v_dn_w_in, dn_conv_w=v_dn_conv_w,
             dn_a_log=v_dn_a_log, dn_dt_bias=v_dn_dt_bias, dn_norm_w=v_dn_norm_w, dn_w_out=v_dn_w_out,
             ffn_w_gate=v_ffn_w_gate, ffn_w_up=v_ffn_w_up, ffn_w_down=v_ffn_w_down, final_norm_w=v_final_norm_w)
    mx, my, mc = _mesh_pos()
    chip = 2 * mx + my

    gathered = _gather_weights([_halves(w[n].astype(MXU_DTYPE)) for n in BIG])
    full = {n: gathered[i].reshape((N_SHARD,) + w[n].shape) for i, n in enumerate(BIG)}
    sm_names = [n for n in SMALL if n in SMALL_SHARDED]
    sm_gathered = _unpack(_gather_small(_pack_lanes([w[n] for n in sm_names])), [w[n].shape for n in sm_names])
    for i, n in enumerate(sm_names):
        full[n] = _join_cols(sm_gathered[i])
    wts = {
        "meta_tokens": full["meta_tokens"], "mix_norm_w": mix_norm_w, "ffn_norm_w": ffn_norm_w,
        "ret_gn_w": ret_gn_w[0], "final_norm_w": final_norm_w, "dn_conv_w": full["dn_conv_w"][0],
        "dn_a_log": dn_a_log[0], "dn_dt_bias": dn_dt_bias[0], "dn_norm_w": full["dn_norm_w"][0],
        "ret_w_in": full["ret_w_in"][:, 0], "ret_w_out": full["ret_w_out"].reshape(-1, D_MODEL),
        "dn_w_in": jnp.pad(_join_cols(full["dn_w_in"][:, 0]), ((0, 0), (0, DN_IN_PAD - DN_IN))),
        "dn_w_out": full["dn_w_out"].reshape(-1, D_MODEL), "ffn_w_gate": full["ffn_w_gate"],
        "ffn_w_up": full["ffn_w_up"], "ffn_w_down": full["ffn_w_down"],
    }

    loss_part, dh0, g = _local_step(x[0], loss_target[0], wts)
    seq = x.shape[1]
    grad_x = dh0[CHUNK:CHUNK + seq].reshape(x.shape)

    n_dn = dn_w_in.shape[-1]
    g["dn_w_in"] = jnp.stack([g["dn_w_in"][:, j * n_dn:(j + 1) * n_dn] for j in range(N_SHARD)])
    gs = [g[n].reshape((N_SHARD,) + _halves(w[n]).shape) for n in BIG]
    idx = jnp.stack([mc, chip]).astype(jnp.int32)
    sib = _rs_pair(gs)
    parts = [_rs_pair_add(gs[t], sib[t], idx, "rs_pair_add_" + n) for t, n in enumerate(BIG)]
    others = _rs_chips(parts)
    mine = [_rs_final_add(gs[t], sib[t], others[t], idx, "rs_final_add_" + n) for t, n in enumerate(BIG)]
    gsh = {n: f.reshape(w[n].shape) for n, f in zip(BIG, _rs_share(mine))}

    small_full_shapes = [g[n].shape for n in SMALL] + [(1,)]
    red = _unpack(_allreduce_small(_pack_lanes([g[n] for n in SMALL] + [loss_part[0, :1]])), small_full_shapes)
    loss = red[-1][0]
    for i, n in enumerate(SMALL):
        gn = red[i]
        if n in SMALL_SHARDED:
            width = w[n].shape[-1]
            gn = lax.dynamic_slice_in_dim(gn, chip * width, width, axis=gn.ndim - 1)
        gsh[n] = gn.reshape(w[n].shape)

    delta, new_m, new_v = {}, {}, {}
    for n in BIG:
        shp = w[n].shape
        two_d = (-1, shp[-1])
        d_, m_, v_ = _adamw(w[n].reshape(two_d), gsh[n].reshape(two_d), m[n].reshape(two_d),
                            v[n].reshape(two_d), "adamw_" + n)
        delta[n], new_m[n], new_v[n] = d_.reshape(shp), m_.reshape(shp), v_.reshape(shp)
    sm_local_shapes = [w[n].shape for n in SMALL]
    d_, m_, v_ = _adamw(_pack_lanes([w[n] for n in SMALL]), _pack_lanes([gsh[n] for n in SMALL]),
                        _pack_lanes([m[n] for n in SMALL]), _pack_lanes([v[n] for n in SMALL]), "adamw_small")
    for n, dd, mm, vv in zip(SMALL, _unpack(d_, sm_local_shapes), _unpack(m_, sm_local_shapes),
                             _unpack(v_, sm_local_shapes)):
        delta[n], new_m[n], new_v[n] = dd, mm, vv

    return (loss, grad_x, *[gsh[n] for n in ORDER], *[delta[n] for n in ORDER],
            *[new_m[n] for n in ORDER], *[new_v[n] for n in ORDER])
```

```python
import functools
import math

import jax
import jax.numpy as jnp
from jax import lax
from jax.experimental import pallas as pl
from jax.experimental.pallas import tpu as pltpu

F32 = jnp.float32
BF16 = jnp.bfloat16
MXU_DTYPE = BF16

D_MODEL = 1024
N_META = 16
CHUNK = 64
PAD = CHUNK - N_META
RMS_EPS = 1e-6
RET_HEADS, RET_DK, RET_DV = 4, 256, 512
RET_QK, RET_V = RET_HEADS * RET_DK, RET_HEADS * RET_DV
RET_IN = 2 * RET_QK + 2 * RET_V
ROPE_BASE = 10000.0
DN_HEADS, DN_DK, DN_DV = 8, 128, 256
DN_QK, DN_V = DN_HEADS * DN_DK, DN_HEADS * DN_DV
DN_CONV_CH = 2 * DN_QK + DN_V
DN_IN = DN_CONV_CH + DN_V + 2 * DN_HEADS
LANES = 128
DN_IN_PAD = DN_CONV_CH + DN_V + LANES
CONV_K = 4
FFN_HIDDEN = 2816
ADAM_LR, ADAM_B1, ADAM_B2, ADAM_EPS, ADAM_WD, ADAM_STEP = 0.001, 0.9, 0.999, 1e-08, 0.01, 10

ROW_ALIGN = 256
VMEM_LIMIT = 56 * 1024 * 1024
MESH = pl.DeviceIdType.MESH
ANY = pl.BlockSpec(memory_space=pl.ANY)
VMEM_SPEC = pl.BlockSpec(memory_space=pltpu.VMEM)
_HI = lax.Precision.HIGHEST


def _params(*sem):
    return pltpu.CompilerParams(dimension_semantics=sem, vmem_limit_bytes=VMEM_LIMIT)


def _dg(a, b, ca, cb, hi):
    dims = (((ca,), (cb,)), ((), ()))

    def dot(p, q):
        return lax.dot_general(p, q, dims, preferred_element_type=F32)

    if not hi:
        return dot(a.astype(MXU_DTYPE), b.astype(MXU_DTYPE))
    if MXU_DTYPE == F32:
        return lax.dot_general(a, b, dims, precision=_HI, preferred_element_type=F32)
    a_hi, b_hi = a.astype(MXU_DTYPE), b.astype(MXU_DTYPE)
    a_lo = (a - a_hi.astype(F32)).astype(MXU_DTYPE)
    b_lo = (b - b_hi.astype(F32)).astype(MXU_DTYPE)
    return dot(a_hi, b_hi) + (dot(a_hi, b_lo) + dot(a_lo, b_hi))


def _nn(a, b, hi=False):
    return _dg(a, b, 1, 0, hi)


def _nt(a, b, hi=False):
    return _dg(a, b, 1, 1, hi)


def _tn(a, b, hi=False):
    return _dg(a, b, 0, 0, hi)


def _iota(shape, dim):
    return lax.broadcasted_iota(jnp.int32, shape, dim)


def _valid_rows(first_row, rows, seq):
    r = first_row + _iota((rows, 1), 0)
    return ((r >= PAD) & (r < CHUNK + seq)).astype(F32)


def _rope(t, cs, sn):
    half = t.shape[-1] // 2
    t1, t2 = t[:, :half], t[:, half:]
    return jnp.concatenate([t1 * cs - t2 * sn, t1 * sn + t2 * cs], axis=1)


def _rope_bwd(d, cs, sn):
    half = d.shape[-1] // 2
    d1, d2 = d[:, :half], d[:, half:]
    return jnp.concatenate([d1 * cs + d2 * sn, d2 * cs - d1 * sn], axis=1)


def _col(x, idx):
    oh = (_iota((1, x.shape[1]), 1) == idx).astype(F32)
    return jnp.sum(x * oh, axis=1, keepdims=True)


def _row(x, idx):
    oh = (_iota((x.shape[0], 1), 0) == idx).astype(F32)
    return jnp.sum(x * oh, axis=0, keepdims=True)


def _shift_down(x, halo8, k):
    xr = pltpu.roll(x, k, 0)
    hr = pltpu.roll(halo8, k, 0)
    first = jnp.where(_iota((8, 1), 0) < k, hr, xr[0:8])
    return jnp.concatenate([first, xr[8:]], axis=0)


def _shift_up(x, next8, j):
    rows = x.shape[0]
    xr = pltpu.roll(x, rows - j, 0)
    nr = pltpu.roll(next8, 8 - j, 0)
    last = jnp.where(_iota((8, 1), 0) >= 8 - j, nr, xr[rows - 8:])
    return jnp.concatenate([xr[:rows - 8], last], axis=0)


def _gated_norm(o, gate, w):
    r = lax.rsqrt(jnp.mean(o * o, axis=-1, keepdims=True) + RMS_EPS)
    return o * r * w * (gate * jax.nn.sigmoid(gate))


def _gated_norm_bwd(dy, o, gate, w):
    r = lax.rsqrt(jnp.mean(o * o, axis=-1, keepdims=True) + RMS_EPS)
    nrm = o * r
    sg = jax.nn.sigmoid(gate)
    sl = gate * sg
    dgate = dy * nrm * w * (sg * (1.0 + gate * (1.0 - sg)))
    dn = dy * w * sl
    dw = jnp.sum(dy * nrm * sl, axis=0, keepdims=True)
    do = r * (dn - nrm * jnp.mean(dn * nrm, axis=-1, keepdims=True))
    return do, dgate, dw


def _softplus(z):
    return jnp.maximum(z, 0.0) + jnp.log(1.0 + jnp.exp(-jnp.abs(z)))


def _row_tile(rows, cap=768):
    for t in (768, 512, 256, 128, 64, 32, 16, 8):
        if t <= cap and rows % t == 0:
            return t
    return rows


def _div_tile(n, cap, mult):
    best = None
    for t in range(mult, min(cap, n) + 1, mult):
        if n % t == 0:
            best = t
    return best or n


def _col_tile(cols, cap=1536):
    best = None
    for t in range(LANES, min(cap, cols) + 1, LANES):
        if cols % t == 0:
            best = t
    return best or cols


def _rms_fwd(h, w, name):
    rows, d = h.shape
    tm = _row_tile(rows)

    def body(h_ref, w_ref, o_ref):
        x = h_ref[...]
        r = lax.rsqrt(jnp.mean(x * x, axis=-1, keepdims=True) + RMS_EPS)
        o_ref[...] = (x * r * w_ref[...]).astype(o_ref.dtype)

    return pl.pallas_call(
        body, grid=(rows // tm,),
        in_specs=[pl.BlockSpec((tm, d), lambda i: (i, 0)), pl.BlockSpec((1, d), lambda i: (0, 0))],
        out_specs=pl.BlockSpec((tm, d), lambda i: (i, 0)),
        out_shape=jax.ShapeDtypeStruct((rows, d), BF16), name=name,
        compiler_params=_params("parallel"))(h, w.reshape(1, d))


def _rms_bwd(dy, h, w, resid, name):
    rows, d = h.shape
    tm = _row_tile(rows)

    def body(dy_ref, h_ref, w_ref, r_ref, dh_ref, dw_ref):
        i = pl.program_id(0)
        x = h_ref[...]
        r = lax.rsqrt(jnp.mean(x * x, axis=-1, keepdims=True) + RMS_EPS)
        xh = x * r
        dyv = dy_ref[...]
        dxh = dyv * w_ref[...]
        dh_ref[...] = r_ref[...] + r * (dxh - xh * jnp.mean(dxh * xh, axis=-1, keepdims=True))
        part = jnp.sum(dyv * xh, axis=0, keepdims=True)

        @pl.when(i == 0)
        def _():
            dw_ref[...] = part

        @pl.when(i > 0)
        def _():
            dw_ref[...] += part

    blk = pl.BlockSpec((tm, d), lambda i: (i, 0))
    vec = pl.BlockSpec((1, d), lambda i: (0, 0))
    return pl.pallas_call(
        body, grid=(rows // tm,), in_specs=[blk, blk, vec, blk], out_specs=[blk, vec],
        out_shape=[jax.ShapeDtypeStruct((rows, d), F32), jax.ShapeDtypeStruct((1, d), F32)],
        name=name, compiler_params=_params("arbitrary"))(dy, h, w.reshape(1, d), resid)


def _final_loss(h, w, tgt, seq, name):
    rows, d = h.shape
    tm = _row_tile(rows)

    def body(h_ref, w_ref, t_ref, dh_ref, dw_ref, loss_ref):
        i = pl.program_id(0)
        r_idx = i * tm + _iota((tm, 1), 0)
        m = ((r_idx >= CHUNK) & (r_idx < CHUNK + seq)).astype(F32)
        x = h_ref[...]
        wv = w_ref[...]
        r = lax.rsqrt(jnp.mean(x * x, axis=-1, keepdims=True) + RMS_EPS)
        xh = x * r
        err = (xh * wv - t_ref[...]) * m
        lpart = 0.5 * jnp.sum(jnp.mean(err * err, axis=-1, keepdims=True), axis=0, keepdims=True)
        dyv = err * (1.0 / d)
        dxh = dyv * wv
        dh_ref[...] = r * (dxh - xh * jnp.mean(dxh * xh, axis=-1, keepdims=True))
        part = jnp.sum(dyv * xh, axis=0, keepdims=True)

        @pl.when(i == 0)
        def _():
            dw_ref[...] = part
            loss_ref[...] = jnp.broadcast_to(lpart, loss_ref.shape)

        @pl.when(i > 0)
        def _():
            dw_ref[...] += part
            loss_ref[...] += jnp.broadcast_to(lpart, loss_ref.shape)

    blk = pl.BlockSpec((tm, d), lambda i: (i, 0))
    vec = pl.BlockSpec((1, d), lambda i: (0, 0))
    return pl.pallas_call(
        body, grid=(rows // tm,), in_specs=[blk, vec, blk],
        out_specs=[blk, vec, pl.BlockSpec((1, LANES), lambda i: (0, 0))],
        out_shape=[jax.ShapeDtypeStruct((rows, d), F32), jax.ShapeDtypeStruct((1, d), F32),
                   jax.ShapeDtypeStruct((1, LANES), F32)],
        name=name, compiler_params=_params("arbitrary"))(h, w.reshape(1, d), tgt)


def _mm(a, b, *, mode, name, out_dtype=F32, resid=None, row_cap=768, col_cap=1536, ride=None):
    if mode == "tn":
        m, k = a.shape
        n = b.shape[1]
        tm, tn = _row_tile(m, row_cap), _col_tile(n, col_cap)

        def body_tn(a_ref, b_ref, o_ref):
            i = pl.program_id(1)
            part = _tn(a_ref[...], b_ref[...])

            @pl.when(i == 0)
            def _():
                o_ref[...] = part

            @pl.when(i > 0)
            def _():
                o_ref[...] += part

        return pl.pallas_call(
            body_tn, grid=(n // tn, m // tm),
            in_specs=[pl.BlockSpec((tm, k), lambda j, i: (i, 0)),
                      pl.BlockSpec((tm, tn), lambda j, i: (i, j))],
            out_specs=pl.BlockSpec((k, tn), lambda j, i: (0, j)),
            out_shape=jax.ShapeDtypeStruct((k, n), F32), name=name,
            compiler_params=_params("parallel", "arbitrary"))(a, b)

    m, ka = a.shape
    n = b.shape[1] if mode == "nn" else b.shape[0]
    tm, tn = _row_tile(m, row_cap), _col_tile(n, col_cap)
    has_resid = resid is not None

    def body(*refs):
        if has_resid:
            a_ref, b_ref, r_ref, o_ref = refs
        else:
            a_ref, b_ref, o_ref = refs
        acc = _nn(a_ref[...], b_ref[...]) if mode == "nn" else _nt(a_ref[...], b_ref[...])
        if has_resid:
            acc = acc + r_ref[...]
        o_ref[...] = acc.astype(o_ref.dtype)

    b_spec = (pl.BlockSpec((b.shape[0], tn), lambda j, i: (0, j)) if mode == "nn"
              else pl.BlockSpec((tn, b.shape[1]), lambda j, i: (j, 0)))
    o_spec = pl.BlockSpec((tm, tn), lambda j, i: (i, j))
    in_specs = [pl.BlockSpec((tm, ka), lambda j, i: (i, 0)), b_spec]
    args = [a, b]
    if has_resid:
        in_specs.append(o_spec)
        args.append(resid)
    res, rode = _pcall(body, args, grid=(n // tn, m // tm), in_specs=in_specs, out_specs=[o_spec],
                       out_shape=[jax.ShapeDtypeStruct((m, n), out_dtype)], name=name,
                       sem=("parallel", "parallel"), ride=ride)
    return res[0] if ride is None else (res[0], rode)


N_SHARD = 4


def _gmm(name, grid, args, in_specs, out_specs, out_shape, fn, red_axis=None, init_arg=None, aliases=None,
         ride=None):
    n_in = len(args)
    single = not isinstance(out_shape, (list, tuple))
    out_specs = [out_specs] if single else list(out_specs)
    out_shape = [out_shape] if single else list(out_shape)

    def body(*refs):
        _gmm_step(fn, refs[:n_in], refs[n_in:], red_axis, init_arg)

    sem = tuple("arbitrary" if ax == red_axis else "parallel" for ax in range(len(grid)))
    res, rode = _pcall(body, args, grid=grid, in_specs=in_specs, out_specs=out_specs, out_shape=out_shape,
                       name=name, sem=sem, aliases=aliases, ride=ride)
    ours = res[0] if single else res
    return ours if ride is None else (ours, rode)


def _gmm_step(fn, ins, outs, red_axis, init_arg):
    parts = fn(*ins)
    if red_axis is None:
        for o_ref, p in zip(outs, parts):
            o_ref[...] = p.astype(o_ref.dtype)
        return
    k = pl.program_id(red_axis)

    @pl.when(k == 0)
    def _():
        for idx, (o_ref, p) in enumerate(zip(outs, parts)):
            o_ref[...] = p + ins[init_arg][...] if (idx == 0 and init_arg is not None) else p

    @pl.when(k > 0)
    def _():
        for o_ref, p in zip(outs, parts):
            o_ref[...] += p


def _ride_body(ride, grid, n_in, n_out, n_scratch, body):
    n_rin, n_rout = len(ride.arrays), len(ride.out_shape)
    nsteps = math.prod(grid)

    def wrapped(*refs):
        ins = refs[:n_in]
        r_ins = refs[n_in:n_in + n_rin]
        o0 = n_in + n_rin
        outs = refs[o0:o0 + n_out]
        r_outs = refs[o0 + n_out:o0 + n_out + n_rout]
        s0 = o0 + n_out + n_rout
        scratch = refs[s0:s0 + n_scratch]
        send_sems, recv_sems = refs[-2:]
        step = pl.program_id(0)
        for ax in range(1, len(grid)):
            step = step * grid[ax] + pl.program_id(ax)
        ride.emit(step, nsteps, r_ins, r_outs, send_sems, recv_sems, before=True)
        body(*ins, *outs, *scratch)
        ride.emit(step, nsteps, r_ins, r_outs, send_sems, recv_sems, before=False)

    return wrapped


def _pcall(body, args, *, grid, in_specs, out_specs, out_shape, name, sem, scratch=(), aliases=None, ride=None):
    if ride is None:
        res = pl.pallas_call(body, grid=grid, in_specs=list(in_specs), out_specs=list(out_specs),
                             out_shape=list(out_shape), scratch_shapes=list(scratch), name=name,
                             input_output_aliases=aliases or {}, compiler_params=_params(*sem))(*args)
        return res, None
    n_in, n_out = len(args), len(out_shape)
    res = pl.pallas_call(
        _ride_body(ride, grid, n_in, n_out, len(scratch), body), grid=grid,
        in_specs=list(in_specs) + ride.in_specs, out_specs=list(out_specs) + ride.out_specs,
        out_shape=list(out_shape) + ride.out_shape, scratch_shapes=list(scratch) + ride.scratch, name=name,
        input_output_aliases=aliases or {},
        compiler_params=_params(*(("arbitrary",) * len(grid))))(*args, *ride.arrays)
    return res[:n_out], res[n_out:]


def _mm_cols(a, ws, name, ride=None):
    m, k = a.shape
    n = ws.shape[2]
    tm = _row_tile(m)
    return _gmm(name, (N_SHARD, m // tm), [a, ws],
                [pl.BlockSpec((tm, k), lambda j, i: (i, 0)), pl.BlockSpec((None, k, n), lambda j, i: (j, 0, 0))],
                pl.BlockSpec((tm, n), lambda j, i: (i, j)), jax.ShapeDtypeStruct((m, N_SHARD * n), F32),
                lambda a_ref, w_ref: (_nn(a_ref[...], w_ref[...]),), ride=ride)


def _mm_cols_t(d, ws, name, ride=None):
    m = d.shape[0]
    _, k, n = ws.shape
    tm = _row_tile(m)
    return _gmm(name, (m // tm, N_SHARD), [d, ws],
                [pl.BlockSpec((tm, n), lambda i, j: (i, j)), pl.BlockSpec((None, k, n), lambda i, j: (j, 0, 0))],
                pl.BlockSpec((tm, k), lambda i, j: (i, 0)), jax.ShapeDtypeStruct((m, k), F32),
                lambda d_ref, w_ref: (_nt(d_ref[...], w_ref[...]),), red_axis=1, ride=ride)


def _mm_cols_grad(a, d, name):
    m, k = a.shape
    n = d.shape[1] // N_SHARD
    tm = _row_tile(m)
    return _gmm(name, (N_SHARD, m // tm), [a, d],
                [pl.BlockSpec((tm, k), lambda j, i: (i, 0)), pl.BlockSpec((tm, n), lambda j, i: (i, j))],
                pl.BlockSpec((None, k, n), lambda j, i: (j, 0, 0)), jax.ShapeDtypeStruct((N_SHARD, k, n), F32),
                lambda a_ref, d_ref: (_tn(a_ref[...], d_ref[...]),), red_axis=1)


def _ffn_up(hn, wg, wu, layer, name):
    m, k = hn.shape
    n = wg.shape[3]
    tm = _row_tile(m)

    def fn(a_ref, wg_ref, wu_ref):
        a = a_ref[...]
        g = _nn(a, wg_ref[...])
        u = _nn(a, wu_ref[...])
        return g, u, g * jax.nn.sigmoid(g) * u

    w_spec = pl.BlockSpec((None, None, k, n), lambda j, i: (j, layer, 0, 0))
    o_spec = pl.BlockSpec((None, tm, n), lambda j, i: (j, i, 0))
    out = jax.ShapeDtypeStruct((N_SHARD, m, n), BF16)
    return _gmm(name, (N_SHARD, m // tm), [hn, wg, wu],
                [pl.BlockSpec((tm, k), lambda j, i: (i, 0)), w_spec, w_spec],
                [o_spec, o_spec, o_spec], [out, out, out], fn)


def _ffn_down(act, wd, resid, layer, name):
    _, m, n = act.shape
    d = wd.shape[3]
    tm = _row_tile(m)
    row = pl.BlockSpec((tm, d), lambda i, j: (i, 0))
    return _gmm(name, (m // tm, N_SHARD), [act, wd, resid],
                [pl.BlockSpec((None, tm, n), lambda i, j: (j, i, 0)),
                 pl.BlockSpec((None, None, n, d), lambda i, j: (j, layer, 0, 0)), row],
                row, jax.ShapeDtypeStruct((m, d), F32),
                lambda a_ref, w_ref, r_ref: (_nn(a_ref[...], w_ref[...]),), red_axis=1, init_arg=2)


def _ffn_down_bwd(dh, wd, g, u, layer, name):
    m, d = dh.shape
    n = wd.shape[2]
    tm = _row_tile(m)

    def fn(dh_ref, wd_ref, g_ref, u_ref):
        dact = _nt(dh_ref[...], wd_ref[...])
        gv = g_ref[...].astype(F32)
        uv = u_ref[...].astype(F32)
        sg = jax.nn.sigmoid(gv)
        return dact * uv * (sg * (1.0 + gv * (1.0 - sg))), dact * gv * sg

    o_spec = pl.BlockSpec((None, tm, n), lambda j, i: (j, i, 0))
    out = jax.ShapeDtypeStruct((N_SHARD, m, n), BF16)
    return _gmm(name, (N_SHARD, m // tm), [dh, wd, g, u],
                [pl.BlockSpec((tm, d), lambda j, i: (i, 0)),
                 pl.BlockSpec((None, None, n, d), lambda j, i: (j, layer, 0, 0)), o_spec, o_spec],
                [o_spec, o_spec], [out, out], fn)


def _ffn_up_bwd(dg, du, wg, wu, layer, name):
    _, m, n = dg.shape
    k = wg.shape[2]
    tm = _row_tile(m)
    d_spec = pl.BlockSpec((None, tm, n), lambda i, j: (j, i, 0))
    w_spec = pl.BlockSpec((None, None, k, n), lambda i, j: (j, layer, 0, 0))
    return _gmm(name, (m // tm, N_SHARD), [dg, du, wg, wu], [d_spec, d_spec, w_spec, w_spec],
                pl.BlockSpec((tm, k), lambda i, j: (i, 0)), jax.ShapeDtypeStruct((m, k), F32),
                lambda dg_ref, du_ref, wg_ref, wu_ref: (
                    _nt(dg_ref[...], wg_ref[...]) + _nt(du_ref[...], wu_ref[...]),), red_axis=1)


def _ffn_wgrad(lhs, rhs_list, layer, layers, prev, lhs_sharded, name):
    if lhs_sharded:
        _, m, k = lhs.shape
        n = rhs_list[0].shape[1]
    else:
        m, k = lhs.shape
        n = rhs_list[0].shape[2]
    tm = _row_tile(m)
    sh = pl.BlockSpec((None, tm, k if lhs_sharded else n), lambda j, i: (j, i, 0))
    fl = pl.BlockSpec((tm, n if lhs_sharded else k), lambda j, i: (i, 0))
    n_out = len(rhs_list)
    args = [lhs] + list(rhs_list)
    in_specs = [sh if lhs_sharded else fl] + [fl if lhs_sharded else sh] * n_out
    aliases = None
    if prev is not None:
        aliases = {len(args) + t: t for t in range(n_out)}
        args = args + list(prev)
        in_specs = in_specs + [ANY] * n_out

    def fn(l_ref, *rest):
        lv = l_ref[...]
        return tuple(_tn(lv, r_ref[...]) for r_ref in rest[:n_out])

    o_spec = pl.BlockSpec((None, None, k, n), lambda j, i: (j, layer, 0, 0))
    out = jax.ShapeDtypeStruct((N_SHARD, layers, k, n), F32)
    return _gmm(name, (N_SHARD, m // tm), args, in_specs, [o_spec] * n_out, [out] * n_out, fn,
                red_axis=1, aliases=aliases)


def _ret_consts():
    log_gamma = jnp.log1p(-jnp.exp2(-5.0 - jnp.arange(RET_HEADS, dtype=F32)))
    idx = jnp.arange(CHUNK, dtype=F32)
    rel = idx[:, None] - idx[None, :]
    dmask = jnp.where((rel >= 0)[None], jnp.exp(log_gamma[:, None, None] * jnp.maximum(rel, 0.0)), 0.0)
    xi = jnp.exp(log_gamma[:, None] * (idx[None, :] + 1.0))[:, :, None]
    zeta = jnp.exp(log_gamma[:, None] * (CHUNK - 1.0 - idx[None, :]))[:, :, None]
    gamma_c = jnp.exp(log_gamma * CHUNK)
    wide = (RET_HEADS, CHUNK, RET_DK)
    return dmask, jnp.broadcast_to(xi, wide), jnp.broadcast_to(zeta, wide), gamma_c


def _rope_tables(rows):
    half = RET_DK // 2
    inv_freq = ROPE_BASE ** (-jnp.arange(half, dtype=F32) / half)
    pos = (jnp.arange(rows) - PAD).astype(F32)
    ang = pos[:, None] * inv_freq[None, :]
    return jnp.cos(ang), jnp.sin(ang)


def _ret_specs(order):
    return [pl.BlockSpec((CHUNK, RET_QK), lambda n: (order(n), 0)),
            pl.BlockSpec((CHUNK, RET_QK), lambda n: (order(n), 1)),
            pl.BlockSpec((CHUNK, RET_V), lambda n: (order(n), 1)),
            pl.BlockSpec((CHUNK, RET_V), lambda n: (order(n), 2))]


def _ret_const_specs():
    return [pl.BlockSpec((RET_HEADS, CHUNK, CHUNK), lambda n: (0, 0, 0)),
            pl.BlockSpec((RET_HEADS, CHUNK, RET_DK), lambda n: (0, 0, 0)),
            pl.BlockSpec((RET_HEADS, CHUNK, RET_DK), lambda n: (0, 0, 0)),
            pl.BlockSpec((1, RET_DV), lambda n: (0, 0))]


def _ret_fwd(proj, cos, sin, consts, gn_w, seq, ride=None):
    rows = proj.shape[0]
    nc = rows // CHUNK
    dmask, xi, zeta, gamma_c = consts

    def body(gam_ref, q_ref, k_ref, v_ref, g_ref, cos_ref, sin_ref, dm_ref, xi_ref, ze_ref, gn_ref,
             o_ref, y_ref, ss_ref, s_ref):
        n = pl.program_id(0)

        @pl.when(n == 0)
        def _():
            s_ref[...] = jnp.zeros_like(s_ref)

        cs, sn = cos_ref[...], sin_ref[...]
        kscale = _valid_rows(n * CHUNK, CHUNK, seq) * (RET_DK ** -0.5)
        gn = gn_ref[...]
        hs = range(RET_HEADS)
        qk_cols = [slice(h * RET_DK, (h + 1) * RET_DK) for h in hs]
        v_cols = [slice(h * RET_DV, (h + 1) * RET_DV) for h in hs]
        qr_l = [_rope(q_ref[:, c], cs, sn) for c in qk_cols]
        kr_l = [_rope(k_ref[:, c], cs, sn) * kscale for c in qk_cols]
        v_l = [v_ref[:, c] for c in v_cols]
        s_l = [s_ref[h] for h in hs]
        sc_l = [_nt(qr, kr) * dm_ref[h] for h, (qr, kr) in enumerate(zip(qr_l, kr_l))]
        o_l = [_nn(sc_l[h], v_l[h]) + _nn(qr_l[h] * xi_ref[h], s_l[h]) for h in hs]
        for h in hs:
            ss_ref[0, h] = s_l[h].astype(ss_ref.dtype)
            s_ref[h] = gam_ref[h] * s_l[h] + _tn(kr_l[h] * ze_ref[h], v_l[h])
            o_ref[:, v_cols[h]] = o_l[h]
            y_ref[:, v_cols[h]] = _gated_norm(o_l[h], g_ref[:, v_cols[h]], gn).astype(y_ref.dtype)

    fwd = lambda n: n
    row128 = pl.BlockSpec((CHUNK, RET_DK // 2), lambda n: (n, 0))
    row_v = pl.BlockSpec((CHUNK, RET_V), lambda n: (n, 0))
    res, rode = _pcall(
        body, [gamma_c, proj, proj, proj, proj, cos, sin, dmask, xi, zeta, gn_w.reshape(1, RET_DV)],
        grid=(nc,),
        in_specs=[pl.BlockSpec(memory_space=pltpu.SMEM)] + _ret_specs(fwd) + [row128, row128]
        + _ret_const_specs(),
        out_specs=[row_v, row_v,
                   pl.BlockSpec((1, RET_HEADS, RET_DK, RET_DV), lambda n: (n, 0, 0, 0))],
        out_shape=[jax.ShapeDtypeStruct((rows, RET_V), F32), jax.ShapeDtypeStruct((rows, RET_V), BF16),
                   jax.ShapeDtypeStruct((nc, RET_HEADS, RET_DK, RET_DV), BF16)],
        scratch=[pltpu.VMEM((RET_HEADS, RET_DK, RET_DV), F32)], name="ret_fwd", sem=("arbitrary",), ride=ride)
    return res if ride is None else (res, rode)


def _ret_bwd(proj, o, dy, states, cos, sin, consts, gn_w, seq, ride=None):
    rows = proj.shape[0]
    nc = rows // CHUNK
    dmask, xi, zeta, gamma_c = consts

    def body(gam_ref, q_ref, k_ref, v_ref, g_ref, o_ref, dy_ref, ss_ref, cos_ref, sin_ref,
             dm_ref, xi_ref, ze_ref, gn_ref, dp_ref, dgn_ref, ds_ref):
        n = pl.program_id(0)

        @pl.when(n == 0)
        def _():
            ds_ref[...] = jnp.zeros_like(ds_ref)
            dgn_ref[...] = jnp.zeros_like(dgn_ref)

        cs, sn = cos_ref[...], sin_ref[...]
        kscale = _valid_rows((nc - 1 - n) * CHUNK, CHUNK, seq) * (RET_DK ** -0.5)
        gn = gn_ref[...]
        dgn = jnp.zeros((1, RET_DV), F32)
        hs = range(RET_HEADS)
        qk_cols = [slice(h * RET_DK, (h + 1) * RET_DK) for h in hs]
        v_cols = [slice(h * RET_DV, (h + 1) * RET_DV) for h in hs]
        qr_l = [_rope(q_ref[:, c], cs, sn) for c in qk_cols]
        kr_l = [_rope(k_ref[:, c], cs, sn) * kscale for c in qk_cols]
        v_l = [v_ref[:, c] for c in v_cols]
        s_l = [ss_ref[0, h] for h in hs]
        ds_l = [ds_ref[h] for h in hs]
        gnb = [_gated_norm_bwd(dy_ref[:, c], o_ref[:, c], g_ref[:, c], gn) for c in v_cols]
        do_l = [x[0] for x in gnb]
        sc_l = [_nt(qr_l[h], kr_l[h]) * dm_ref[h] for h in hs]
        dsc_l = [_nt(do_l[h], v_l[h]) * dm_ref[h] for h in hs]
        dv_l = [_tn(sc_l[h], do_l[h]) + _nn(kr_l[h] * ze_ref[h], ds_l[h]) for h in hs]
        dqr_l = [_nn(dsc_l[h], kr_l[h]) + _nt(do_l[h], s_l[h]) * xi_ref[h] for h in hs]
        dkr_l = [_tn(dsc_l[h], qr_l[h]) + _nt(v_l[h], ds_l[h]) * ze_ref[h] for h in hs]
        for h in hs:
            dgn = dgn + gnb[h][2]
            ds_ref[h] = gam_ref[h] * ds_l[h] + _tn(qr_l[h] * xi_ref[h], do_l[h])
            dp_ref[:, qk_cols[h]] = _rope_bwd(dqr_l[h], cs, sn).astype(dp_ref.dtype)
            dp_ref[:, RET_QK + h * RET_DK:RET_QK + (h + 1) * RET_DK] = (
                _rope_bwd(dkr_l[h] * kscale, cs, sn).astype(dp_ref.dtype))
            dp_ref[:, 2 * RET_QK + h * RET_DV:2 * RET_QK + (h + 1) * RET_DV] = dv_l[h].astype(dp_ref.dtype)
            dp_ref[:, 2 * RET_QK + RET_V + h * RET_DV:2 * RET_QK + RET_V + (h + 1) * RET_DV] = (
                gnb[h][1].astype(dp_ref.dtype))
        dgn_ref[...] += dgn

    rev = lambda n: nc - 1 - n
    row128 = pl.BlockSpec((CHUNK, RET_DK // 2), lambda n: (rev(n), 0))
    row_v = pl.BlockSpec((CHUNK, RET_V), lambda n: (rev(n), 0))
    res, rode = _pcall(
        body, [gamma_c, proj, proj, proj, proj, o, dy, states, cos, sin, dmask, xi, zeta,
               gn_w.reshape(1, RET_DV)],
        grid=(nc,),
        in_specs=[pl.BlockSpec(memory_space=pltpu.SMEM)] + _ret_specs(rev) + [
            row_v, row_v, pl.BlockSpec((1, RET_HEADS, RET_DK, RET_DV), lambda n: (rev(n), 0, 0, 0)),
            row128, row128] + _ret_const_specs(),
        out_specs=[pl.BlockSpec((CHUNK, RET_IN), lambda n: (rev(n), 0)),
                   pl.BlockSpec((1, RET_DV), lambda n: (0, 0))],
        out_shape=[jax.ShapeDtypeStruct((rows, RET_IN), BF16), jax.ShapeDtypeStruct((1, RET_DV), F32)],
        scratch=[pltpu.VMEM((RET_HEADS, RET_DK, RET_DV), F32)], name="ret_bwd", sem=("arbitrary",), ride=ride)
    return res if ride is None else (res, rode)


GATE_COL = DN_CONV_CH // DN_V
BA_COL = (DN_CONV_CH + DN_V) // LANES
BETA_LANE, DECAY_LANE = 0, DN_HEADS
INV_SHIFT = 4
INV_SQUARINGS = INV_SHIFT - 1
assert CHUNK == 4 << INV_SHIFT


def _dn_in_specs(order):
    return [pl.BlockSpec((CHUNK, DN_CONV_CH), lambda n: (order(n), 0)),
            pl.BlockSpec((8, DN_CONV_CH), lambda n: (jnp.maximum(order(n) * (CHUNK // 8) - 1, 0), 0)),
            pl.BlockSpec((CHUNK, DN_V), lambda n: (order(n), GATE_COL)),
            pl.BlockSpec((CHUNK, LANES), lambda n: (order(n), BA_COL)),
            pl.BlockSpec((CONV_K, 1, DN_CONV_CH), lambda n: (0, 0, 0)),
            pl.BlockSpec((1, LANES), lambda n: (0, 0)),
            pl.BlockSpec((1, LANES), lambda n: (0, 0)),
            pl.BlockSpec((1, DN_DV), lambda n: (0, 0))]


def _dn_front(c, seq, x_ref, halo_ref, ba_ref, cw_ref, al_ref, dt_ref):
    valid = _valid_rows(c * CHUNK, CHUNK, seq)
    xin = x_ref[...] * valid
    halo = halo_ref[...] * _valid_rows(c * CHUNK - 8, 8, seq)
    x_sh = [xin] + [_shift_down(xin, halo, k) for k in range(1, CONV_K)]
    yc = x_sh[0] * cw_ref[CONV_K - 1]
    for k in range(1, CONV_K):
        yc = yc + x_sh[k] * cw_ref[CONV_K - 1 - k]
    sgc = jax.nn.sigmoid(yc)
    ba = ba_ref[...]
    sig = jax.nn.sigmoid(ba)
    beta = sig * valid
    z = ba + dt_ref[...]
    eal = jnp.exp(al_ref[...])
    g = -eal * _softplus(z) * valid
    ri, ci = _iota((CHUNK, CHUNK), 0), _iota((CHUNK, CHUNK), 1)
    lower = (ri >= ci).astype(F32)
    upper = (ri <= ci).astype(F32)
    eye = (ri == ci).astype(F32)
    gam = _nn(lower, g, hi=True)
    gam_t = _tn(g, upper, hi=True)
    return dict(valid=valid, x_sh=x_sh, yc=yc, sgc=sgc, act=yc * sgc, sig=sig, beta=beta, z=z,
                eal=eal, g=g, gam=gam, gam_t=gam_t, ri=ri, ci=ci, upper=upper, eye=eye)


def _dn_head(f, h):
    act = f["act"]
    q_raw = act[:, h * DN_DK:(h + 1) * DN_DK]
    k_raw = act[:, DN_QK + h * DN_DK:DN_QK + (h + 1) * DN_DK]
    v = act[:, 2 * DN_QK + h * DN_DV:2 * DN_QK + (h + 1) * DN_DV]
    rq = lax.rsqrt(jnp.sum(q_raw * q_raw, axis=-1, keepdims=True) + RMS_EPS)
    rk = lax.rsqrt(jnp.sum(k_raw * k_raw, axis=-1, keepdims=True) + RMS_EPS)
    qh = q_raw * rq
    kn = k_raw * rk
    gam_c = _col(f["gam"], DECAY_LANE + h)
    gam_r = _row(f["gam_t"], DECAY_LANE + h)
    bc = _col(f["beta"], BETA_LANE + h)
    diff = gam_c - gam_r
    decay = jnp.where(f["ri"] >= f["ci"], jnp.exp(jnp.minimum(diff, 0.0)), 0.0)
    glast = jnp.sum(gam_r * (_iota((1, CHUNK), 1) == CHUNK - 1).astype(F32), axis=1, keepdims=True)
    return dict(rq=rq, rk=rk, qh=qh, qn=qh * (DN_DK ** -0.5), kn=kn, v=v, gam_c=gam_c, gam_r=gam_r,
                bc=bc, diff=diff, decay=decay, egam=jnp.exp(gam_c), glast=glast,
                eglast=jnp.exp(glast), ekd=jnp.exp(glast - gam_c))


def _dn_fwd(proj, conv_w, alog, dtb, norm_w, seq):
    rows = proj.shape[0]
    nc = rows // CHUNK

    def body(x_ref, halo_ref, gate_ref, ba_ref, cw_ref, al_ref, dt_ref, nw_ref,
             o_ref, y_ref, ss_ref, t_ref, s_ref):
        n = pl.program_id(0)

        @pl.when(n == 0)
        def _():
            s_ref[...] = jnp.zeros_like(s_ref)

        f = _dn_front(n, seq, x_ref, halo_ref, ba_ref, cw_ref, al_ref, dt_ref)
        ri, ci = f["ri"], f["ci"]
        eye = f["eye"]
        diag_m = (jnp.right_shift(ri, INV_SHIFT) == jnp.right_shift(ci, INV_SHIFT)).astype(F32)
        half_m = (jnp.right_shift(ri, INV_SHIFT + 1) == jnp.right_shift(ci, INV_SHIFT + 1)).astype(F32)
        nw = nw_ref[...]
        heads = [_dn_head(f, h) for h in range(DN_HEADS)]
        a_all = [jnp.where(ri > ci, hd["bc"] * _nt(hd["kn"], hd["kn"]) * hd["decay"], 0.0) for hd in heads]
        b_all = [a * diag_m for a in a_all]
        t_all = [eye - b for b in b_all]
        for _ in range(INV_SQUARINGS):
            b_all = [_nn(b, b, hi=True) for b in b_all]
            t_all = [t + _nn(t, b, hi=True) for t, b in zip(t_all, b_all)]
        for off_m in (half_m - diag_m, 1.0 - half_m):
            x_all = [_nn(a * off_m, t, hi=True) for a, t in zip(a_all, t_all)]
            t_all = [t - _nn(t, x, hi=True) for t, x in zip(t_all, x_all)]
        u_all = [_nn(t, hd["v"] * hd["bc"], hi=True) for t, hd in zip(t_all, heads)]
        w_all = [_nn(t, hd["kn"] * (hd["bc"] * hd["egam"]), hi=True) for t, hd in zip(t_all, heads)]
        for h in range(DN_HEADS):
            hd = heads[h]
            v_cols = slice(h * DN_DV, (h + 1) * DN_DV)
            t_ref[0, h] = t_all[h]
            s = s_ref[h]
            ss_ref[0, h] = s
            u, w = u_all[h], w_all[h]
            v_new = u - _nn(w, s)
            qk = _nt(hd["qn"], hd["kn"]) * hd["decay"]
            o = _nn(hd["qn"] * hd["egam"], s) + _nn(qk, v_new)
            s_ref[h] = s * hd["eglast"] + _tn(hd["kn"] * hd["ekd"], v_new)
            o_ref[:, v_cols] = o
            y_ref[:, v_cols] = _gated_norm(o, gate_ref[:, v_cols], nw).astype(y_ref.dtype)

    fwd = lambda n: n
    row_v = pl.BlockSpec((CHUNK, DN_V), lambda n: (n, 0))
    return pl.pallas_call(
        body, grid=(nc,), in_specs=_dn_in_specs(fwd),
        out_specs=[row_v, row_v,
                   pl.BlockSpec((1, DN_HEADS, DN_DK, DN_DV), lambda n: (n, 0, 0, 0)),
                   pl.BlockSpec((1, DN_HEADS, CHUNK, CHUNK), lambda n: (n, 0, 0, 0))],
        out_shape=[jax.ShapeDtypeStruct((rows, DN_V), F32), jax.ShapeDtypeStruct((rows, DN_V), BF16),
                   jax.ShapeDtypeStruct((nc, DN_HEADS, DN_DK, DN_DV), F32),
                   jax.ShapeDtypeStruct((nc, DN_HEADS, CHUNK, CHUNK), F32)],
        scratch_shapes=[pltpu.VMEM((DN_HEADS, DN_DK, DN_DV), F32)],
        name="dn_fwd", compiler_params=_params("arbitrary"))(
            proj, proj, proj, proj, conv_w, alog, dtb, norm_w.reshape(1, DN_DV))


def _dn_bwd(proj, o, dy, states, tinv, conv_w, alog, dtb, norm_w, seq):
    rows = proj.shape[0]
    nc = rows // CHUNK

    def body(x_ref, halo_ref, gate_ref, ba_ref, cw_ref, al_ref, dt_ref, nw_ref,
             o_ref, dy_ref, ss_ref, t_ref,
             dp_ref, dcw_ref, dal_ref, ddt_ref, dnw_ref, ds_ref, nxt_ref):
        n = pl.program_id(0)

        @pl.when(n == 0)
        def _():
            ds_ref[...] = jnp.zeros_like(ds_ref)
            nxt_ref[...] = jnp.zeros_like(nxt_ref)
            dcw_ref[...] = jnp.zeros_like(dcw_ref)
            dal_ref[...] = jnp.zeros_like(dal_ref)
            ddt_ref[...] = jnp.zeros_like(ddt_ref)
            dnw_ref[...] = jnp.zeros_like(dnw_ref)

        f = _dn_front(nc - 1 - n, seq, x_ref, halo_ref, ba_ref, cw_ref, al_ref, dt_ref)
        ri, ci = f["ri"], f["ci"]
        strict = (ri > ci).astype(F32)
        nw = nw_ref[...]
        lane128 = _iota((1, LANES), 1)
        row128 = _iota((LANES, 1), 0)
        dgam_col = jnp.zeros((CHUNK, LANES), F32)
        dgam_row = jnp.zeros((LANES, CHUNK), F32)
        dbeta = jnp.zeros((CHUNK, LANES), F32)
        dnw = jnp.zeros((1, DN_DV), F32)
        hs = range(DN_HEADS)
        heads = [_dn_head(f, h) for h in hs]
        cols = [slice(h * DN_DV, (h + 1) * DN_DV) for h in hs]
        t_l = [t_ref[0, h] for h in hs]
        s_l = [ss_ref[0, h] for h in hs]
        ds_l = [ds_ref[h] for h in hs]
        kk_l = [_nt(hd["kn"], hd["kn"]) for hd in heads]
        p_l = [_nt(hd["qn"], hd["kn"]) for hd in heads]
        rhsw_l = [hd["kn"] * (hd["bc"] * hd["egam"]) for hd in heads]
        u_l = [_nn(t, hd["v"] * hd["bc"], hi=True) for t, hd in zip(t_l, heads)]
        w_l = [_nn(t, r, hi=True) for t, r in zip(t_l, rhsw_l)]
        vnew_l = [u - _nn(w, s) for u, w, s in zip(u_l, w_l, s_l)]
        gnb = [_gated_norm_bwd(dy_ref[:, c], o_ref[:, c], gate_ref[:, c], nw) for c in cols]
        do_l = [x[0] for x in gnb]
        for h in hs:
            dp_ref[:, DN_CONV_CH + h * DN_DV:DN_CONV_CH + (h + 1) * DN_DV] = gnb[h][1].astype(dp_ref.dtype)
            dnw = dnw + gnb[h][2]
        qg_l = [hd["qn"] * hd["egam"] for hd in heads]
        kd_l = [hd["kn"] * hd["ekd"] for hd in heads]
        dvnew_l = [_tn(p * hd["decay"], do) + _nn(kd, ds)
                   for p, hd, do, kd, ds in zip(p_l, heads, do_l, kd_l, ds_l)]
        m_l = [_nt(do, vn) for do, vn in zip(do_l, vnew_l)]
        dqg_l = [_nt(do, s) for do, s in zip(do_l, s_l)]
        dkd_l = [_nt(vn, ds) for vn, ds in zip(vnew_l, ds_l)]
        for h in hs:
            ds_ref[h] = (ds_l[h] * heads[h]["eglast"] + _tn(qg_l[h], do_l[h]) - _tn(w_l[h], dvnew_l[h]))
        dw_l = [-_nt(dvn, s) for dvn, s in zip(dvnew_l, s_l)]
        dru_l = [_tn(t, dvn, hi=True) for t, dvn in zip(t_l, dvnew_l)]
        drw_l = [_tn(t, dw_, hi=True) for t, dw_ in zip(t_l, dw_l)]
        da_l = [-(_nt(dru, u) + _nt(drw, w)) * strict for dru, u, drw, w in zip(dru_l, u_l, drw_l, w_l)]
        dp_l = [m * hd["decay"] for m, hd in zip(m_l, heads)]
        dkk_l = [da * (hd["bc"] * hd["decay"]) for da, hd in zip(da_l, heads)]
        dqn_l = [dqg * hd["egam"] + _nn(dp, hd["kn"]) for dqg, hd, dp in zip(dqg_l, heads, dp_l)]
        dkn_l = [_tn(dp, hd["qn"]) + dkd * hd["ekd"] + drw * (hd["bc"] * hd["egam"])
                 + _nn(dkk, hd["kn"]) + _tn(dkk, hd["kn"])
                 for dp, hd, dkd, drw, dkk in zip(dp_l, heads, dkd_l, drw_l, dkk_l)]
        dq_parts, dk_parts, dv_parts = [], [], []
        for h in hs:
            hd = heads[h]
            kn, v, bc, egam, decay = hd["kn"], hd["v"], hd["bc"], hd["egam"], hd["decay"]
            t1 = jnp.sum(dkd_l[h] * kd_l[h], axis=1, keepdims=True)
            dglast = (jnp.sum(t1, axis=0, keepdims=True)
                      + jnp.sum(jnp.sum(ds_l[h] * s_l[h], axis=1, keepdims=True), axis=0, keepdims=True)
                      * hd["eglast"])
            e = (m_l[h] * p_l[h] + da_l[h] * (bc * kk_l[h])) * decay
            dgc = (jnp.sum(dqg_l[h] * qg_l[h], axis=1, keepdims=True) - t1
                   + jnp.sum(drw_l[h] * rhsw_l[h], axis=1, keepdims=True)
                   + jnp.sum(e, axis=1, keepdims=True)
                   + jnp.where(_iota((CHUNK, 1), 0) == CHUNK - 1, dglast, 0.0))
            dgr = -jnp.sum(e, axis=0, keepdims=True)
            dbc = (jnp.sum(dru_l[h] * v, axis=1, keepdims=True)
                   + jnp.sum(drw_l[h] * kn, axis=1, keepdims=True) * egam
                   + jnp.sum(da_l[h] * kk_l[h] * decay, axis=1, keepdims=True))
            dv_parts.append(dru_l[h] * bc)
            qh, dqn, dkn = hd["qh"], dqn_l[h], dkn_l[h]
            dq_parts.append(((DN_DK ** -0.5) * hd["rq"])
                            * (dqn - qh * jnp.sum(dqn * qh, axis=1, keepdims=True)))
            dk_parts.append(hd["rk"] * (dkn - kn * jnp.sum(dkn * kn, axis=1, keepdims=True)))
            dgam_col = dgam_col + dgc * (lane128 == DECAY_LANE + h).astype(F32)
            dbeta = dbeta + dbc * (lane128 == BETA_LANE + h).astype(F32)
            dgam_row = dgam_row + (row128 == DECAY_LANE + h).astype(F32) * dgr
        dnw_ref[...] += dnw
        dgam = dgam_col + _nt(f["eye"], dgam_row, hi=True)
        dg = _nn(f["upper"], dgam, hi=True)
        d_a = dg * (-f["eal"]) * jax.nn.sigmoid(f["z"]) * f["valid"]
        dal_ref[...] += jnp.sum(dg * f["g"], axis=0, keepdims=True)
        ddt_ref[...] += jnp.sum(d_a, axis=0, keepdims=True)
        d_b = dbeta * f["valid"] * f["sig"] * (1.0 - f["sig"])
        dp_ref[:, DN_CONV_CH + DN_V:] = (d_a + d_b).astype(dp_ref.dtype)
        dact = jnp.concatenate(dq_parts + dk_parts + dv_parts, axis=1)
        yc, sgc = f["yc"], f["sgc"]
        dyc = dact * (sgc * (1.0 + yc * (1.0 - sgc)))
        for k in range(CONV_K):
            dcw_ref[k] += jnp.sum(dyc * f["x_sh"][CONV_K - 1 - k], axis=0, keepdims=True)
        nxt = nxt_ref[...]
        dx = dyc * cw_ref[CONV_K - 1]
        for j in range(1, CONV_K):
            dx = dx + _shift_up(dyc, nxt, j) * cw_ref[CONV_K - 1 - j]
        nxt_ref[...] = dyc[0:8]
        dp_ref[:, :DN_CONV_CH] = (dx * f["valid"]).astype(dp_ref.dtype)

    rev = lambda n: nc - 1 - n
    row_v = pl.BlockSpec((CHUNK, DN_V), lambda n: (rev(n), 0))
    vec = pl.BlockSpec((1, LANES), lambda n: (0, 0))
    return pl.pallas_call(
        body, grid=(nc,),
        in_specs=_dn_in_specs(rev) + [
            row_v, row_v,
            pl.BlockSpec((1, DN_HEADS, DN_DK, DN_DV), lambda n: (rev(n), 0, 0, 0)),
            pl.BlockSpec((1, DN_HEADS, CHUNK, CHUNK), lambda n: (rev(n), 0, 0, 0))],
        out_specs=[pl.BlockSpec((CHUNK, DN_IN_PAD), lambda n: (rev(n), 0)),
                   pl.BlockSpec((CONV_K, 1, DN_CONV_CH), lambda n: (0, 0, 0)), vec, vec,
                   pl.BlockSpec((1, DN_DV), lambda n: (0, 0))],
        out_shape=[jax.ShapeDtypeStruct((rows, DN_IN_PAD), BF16),
                   jax.ShapeDtypeStruct((CONV_K, 1, DN_CONV_CH), F32),
                   jax.ShapeDtypeStruct((1, LANES), F32), jax.ShapeDtypeStruct((1, LANES), F32),
                   jax.ShapeDtypeStruct((1, DN_DV), F32)],
        scratch_shapes=[pltpu.VMEM((DN_HEADS, DN_DK, DN_DV), F32), pltpu.VMEM((8, DN_CONV_CH), F32)],
        name="dn_bwd", compiler_params=_params("arbitrary"))(
            proj, proj, proj, proj, conv_w, alog, dtb, norm_w.reshape(1, DN_DV), o, dy, states, tinv)


def _train_step(x, tgt, wts, sh, idx):
    seq = x.shape[0]
    rows = -(-(seq + CHUNK) // ROW_ALIGN) * ROW_ALIGN
    tail = rows - seq - CHUNK
    h0 = jnp.concatenate([jnp.zeros((PAD, D_MODEL), F32), wts["meta_tokens"].astype(F32), x,
                          jnp.zeros((tail, D_MODEL), F32)], axis=0)
    tgt_p = jnp.concatenate([jnp.zeros((CHUNK, D_MODEL), F32), tgt, jnp.zeros((tail, D_MODEL), F32)],
                            axis=0)
    cos, sin = _rope_tables(rows)
    consts = _ret_consts()
    conv_w = wts["dn_conv_w"].reshape(CONV_K, 1, DN_CONV_CH)
    lane_pad = LANES - 2 * DN_HEADS
    alog = jnp.pad(wts["dn_a_log"].reshape(1, DN_HEADS), ((0, 0), (DECAY_LANE, lane_pad)))
    dtb = jnp.pad(wts["dn_dt_bias"].reshape(1, DN_HEADS), ((0, 0), (DECAY_LANE, lane_pad)))
    g = {}

    wts = dict(wts)
    (got,) = _gather_weights([sh["ret_w_in"]])
    wts["ret_w_in"] = got.reshape(N_SHARD, D_MODEL, -1)
    hn0 = _rms_fwd(h0, wts["mix_norm_w"][0], "rms_mix0")
    proj0, got = _mm_cols(hn0, wts["ret_w_in"], "ret_in",
                          ride=_Ride("gather", [sh["ret_w_out"], sh["ffn_w_gate"], sh["ffn_w_up"]]))
    wts["ret_w_out"] = got[0].reshape(-1, D_MODEL)
    wts["ffn_w_gate"], wts["ffn_w_up"] = got[1], got[2]
    (o0, y0, st0), got = _ret_fwd(proj0, cos, sin, consts, wts["ret_gn_w"], seq,
                                  ride=_Ride("gather", [sh["ffn_w_down"], sh["dn_w_in"], sh["dn_w_out"]]))
    wts["ffn_w_down"] = got[0]
    n_dn = sh["dn_w_in"].shape[-1]
    wts["dn_w_in"] = jnp.pad(_join_cols(got[1].reshape(N_SHARD, D_MODEL, n_dn)),
                             ((0, 0), (0, DN_IN_PAD - N_SHARD * n_dn)))
    wts["dn_w_out"] = got[2].reshape(-1, D_MODEL)
    h1 = _mm(y0, wts["ret_w_out"], mode="nn", name="ret_out", resid=h0)
    hn1 = _rms_fwd(h1, wts["ffn_norm_w"][0], "rms_ffn0")
    g0, u0, act0 = _ffn_up(hn1, wts["ffn_w_gate"], wts["ffn_w_up"], 0, "ffn_up0")
    h2 = _ffn_down(act0, wts["ffn_w_down"], h1, 0, "ffn_down0")
    hn2 = _rms_fwd(h2, wts["mix_norm_w"][1], "rms_mix1")
    proj1 = _mm(hn2, wts["dn_w_in"], mode="nn", name="dn_in")
    o1, y1, st1, tinv = _dn_fwd(proj1, conv_w, alog, dtb, wts["dn_norm_w"], seq)
    h3 = _mm(y1, wts["dn_w_out"], mode="nn", name="dn_out", resid=h2)
    hn3 = _rms_fwd(h3, wts["ffn_norm_w"][1], "rms_ffn1")
    g1, u1, act1 = _ffn_up(hn3, wts["ffn_w_gate"], wts["ffn_w_up"], 1, "ffn_up1")
    h4 = _ffn_down(act1, wts["ffn_w_down"], h3, 1, "ffn_down1")

    dh4, g["final_norm_w"], loss = _final_loss(h4, wts["final_norm_w"], tgt_p, seq, "final_loss")

    layers = wts["ffn_w_gate"].shape[1]

    def ffn_bwd(dh_out, h_mid, hn, gg, uu, act, layer, prev):
        tag = str(layer)
        dg, du = _ffn_down_bwd(dh_out, wts["ffn_w_down"], gg, uu, layer, "ffn_down_bwd" + tag)
        d_down = _ffn_wgrad(act, [dh_out], layer, layers, prev and prev[:1], True, "ffn_dwd" + tag)
        d_gu = _ffn_wgrad(hn, [dg, du], layer, layers, prev and prev[1:], False, "ffn_dwgu" + tag)
        dhn = _ffn_up_bwd(dg, du, wts["ffn_w_gate"], wts["ffn_w_up"], layer, "ffn_up_bwd" + tag)
        dh_mid, d_norm = _rms_bwd(dhn, h_mid, wts["ffn_norm_w"][layer], dh_out, "rms_ffn_bwd" + tag)
        return dh_mid, list(d_down) + list(d_gu), d_norm

    red = {}

    def rs_begin(names, grads, tag):
        gs = [gr.reshape((N_SHARD,) + sh[n].shape) for n, gr in zip(names, grads)]
        sib = _rs_pair(gs, "rs_pair" + tag)
        parts = [_rs_pair_add(gs[t], sib[t], idx, "rs_pair_add_" + n) for t, n in enumerate(names)]
        return gs, sib, parts

    def rs_end(names, begun, others, tag):
        gs, sib, _ = begun
        mine = [_rs_final_add(gs[t], sib[t], others[t], idx, "rs_final_add_" + n) for t, n in enumerate(names)]
        red.update(zip(names, _rs_share(mine, "rs_share" + tag)))

    dh3, ffn_grads, dfn1 = ffn_bwd(dh4, h3, hn3, g1, u1, act1, 1, None)
    dy1 = _mm(dh3, wts["dn_w_out"], mode="nt", name="dn_out_bwd")
    d_dn_out = _mm(y1, dh3, mode="tn", name="dn_dwo")
    dproj1, dcw, dal, ddt, g["dn_norm_w"] = _dn_bwd(proj1, o1, dy1, st1, tinv, conv_w, alog, dtb,
                                                    wts["dn_norm_w"], seq)
    d_dn_in = _mm(hn2, dproj1, mode="tn", name="dn_dwi")
    d_dn_in = jnp.stack([d_dn_in[:, j * n_dn:(j + 1) * n_dn] for j in range(N_SHARD)])
    group = ["dn_w_out", "dn_w_in"]
    begun = rs_begin(group, [d_dn_out, d_dn_in], "1")
    dhn2, others = _mm(dproj1, wts["dn_w_in"], mode="nt", name="dn_in_bwd", row_cap=256,
                       ride=_Ride("chips", begun[2]))
    rs_end(group, begun, others, "1")
    dh2, dmn1 = _rms_bwd(dhn2, h2, wts["mix_norm_w"][1], dh3, "rms_mix_bwd1")
    g["dn_conv_w"] = dcw.reshape(CONV_K, DN_CONV_CH)
    g["dn_a_log"] = dal[0, DECAY_LANE:DECAY_LANE + DN_HEADS]
    g["dn_dt_bias"] = ddt[0, DECAY_LANE:DECAY_LANE + DN_HEADS]

    dh1, ffn_grads, dfn0 = ffn_bwd(dh2, h1, hn1, g0, u0, act0, 0, ffn_grads)
    dy0 = _mm(dh1, wts["ret_w_out"], mode="nt", name="ret_out_bwd")
    d_ret_out = _mm(y0, dh1, mode="tn", name="ret_dwo")
    group = ["ffn_w_down", "ffn_w_gate", "ffn_w_up", "ret_w_out"]
    begun = rs_begin(group, list(ffn_grads) + [d_ret_out], "2")
    (dproj0, g["ret_gn_w"]), others = _ret_bwd(proj0, o0, dy0, st0, cos, sin, consts, wts["ret_gn_w"], seq,
                                               ride=_Ride("chips", begun[2]))
    rs_end(group, begun, others, "2")
    d_ret_in = _mm_cols_grad(hn0, dproj0, "ret_dwi")
    begun = rs_begin(["ret_w_in"], [d_ret_in], "3")
    dhn0, others = _mm_cols_t(dproj0, wts["ret_w_in"], "ret_in_bwd", ride=_Ride("chips", begun[2]))
    rs_end(["ret_w_in"], begun, others, "3")
    dh0, dmn0 = _rms_bwd(dhn0, h0, wts["mix_norm_w"][0], dh1, "rms_mix_bwd0")

    g["ffn_norm_w"] = jnp.concatenate([dfn0, dfn1], axis=0)
    g["mix_norm_w"] = jnp.concatenate([dmn0, dmn1], axis=0)
    g["meta_tokens"] = dh0[PAD:CHUNK]
    g["final_norm_w"] = g["final_norm_w"].reshape(D_MODEL)
    g["ret_gn_w"] = g["ret_gn_w"].reshape(RET_DV)
    g["dn_norm_w"] = g["dn_norm_w"].reshape(DN_DV)
    return loss, dh0, g, red


def _mesh_pos():
    return lax.axis_index("x"), lax.axis_index("y"), lax.axis_index("c")


def _other_chips(x, y):
    return [(1 - x, y), (x, 1 - y), (1 - x, 1 - y)]


def _remote(src, dst, send_sem, recv_sem, to):
    return pltpu.make_async_remote_copy(src_ref=src, dst_ref=dst, send_sem=send_sem, recv_sem=recv_sem,
                                        device_id=to, device_id_type=MESH)


GATHER_COPIES = 7


def _gather_weights(shards):
    ride = _Ride("gather", shards)

    def body(*refs):
        nt = len(shards)
        for phase in range(3):
            _gather_phase(phase, refs[:nt], refs[nt:2 * nt], *refs[2 * nt:])

    return pl.pallas_call(body, out_shape=ride.out_shape, in_specs=ride.in_specs, out_specs=ride.out_specs,
                          scratch_shapes=ride.scratch, name="gather_weights")(*shards)


def _gather_phase(phase, ins, outs, send_sems, recv_sems):
    x, y, c = _mesh_pos()
    me = 2 * x + y
    chips = _other_chips(x, y)
    sibling = (x, y, 1 - c)

    def cp(t, k, src, dst, to):
        i = GATHER_COPIES * t + k
        return _remote(src, dst, send_sems.at[i], recv_sems.at[i], to)

    for t in range(len(ins)):
        own = cp(t, 0, ins[t], outs[t].at[me], sibling)
        if phase == 0:
            own.start()
        if phase == 2:
            own.wait()
        for k, (px, py) in enumerate(chips):
            landed = outs[t].at[2 * px + py, c]
            theirs = outs[t].at[2 * px + py, 1 - c]
            to_chip = cp(t, 1 + k, ins[t].at[c], outs[t].at[me, c], (px, py, c))
            if phase == 0:
                to_chip.start()
            if phase == 1:
                cp(t, 1 + k, ins[t].at[c], landed, (px, py, c)).wait_recv()
                cp(t, 4 + k, landed, landed, sibling).start()
            if phase == 2:
                to_chip.wait_send()
                cp(t, 4 + k, landed, landed, sibling).wait_send()
                cp(t, 4 + k, theirs, theirs, sibling).wait_recv()


def _chips_phase(phase, ins, outs, send_sems, recv_sems):
    x, y, c = _mesh_pos()
    for t in range(len(ins)):
        for k, (px, py) in enumerate(_other_chips(x, y)):
            cp = _remote(ins[t].at[2 * px + py], outs[t].at[k], send_sems.at[3 * t + k], recv_sems.at[3 * t + k],
                         (px, py, c))
            if phase == 0:
                cp.start()
            if phase == 2:
                cp.wait()


class _Ride:
    def __init__(self, kind, arrays):
        self.kind, self.arrays = kind, list(arrays)
        nt = len(self.arrays)
        if kind == "gather":
            self.phase_fn, n_sem = _gather_phase, GATHER_COPIES * nt
            self.out_shape = [jax.ShapeDtypeStruct((N_SHARD,) + a.shape, a.dtype) for a in self.arrays]
        else:
            self.phase_fn, n_sem = _chips_phase, 3 * nt
            self.out_shape = [jax.ShapeDtypeStruct((3,) + a.shape[1:], a.dtype) for a in self.arrays]
        self.in_specs, self.out_specs = [ANY] * nt, [ANY] * nt
        self.scratch = [pltpu.SemaphoreType.DMA((n_sem,)), pltpu.SemaphoreType.DMA((n_sem,))]

    def emit(self, step, nsteps, ins, outs, send_sems, recv_sems, before):
        mid = max(0, min((3 * nsteps) // 4, nsteps - 2))
        todo = [(0, 0), (1, mid)] if before else [(2, nsteps - 1)]
        for phase, at in todo:
            if phase == 1 and self.kind != "gather":
                continue

            @pl.when(step == at)
            def _(phase=phase):
                self.phase_fn(phase, ins, outs, send_sems, recv_sems)


def _gather_small(blk):
    r, wd = blk.shape

    def body(b_ref, out_ref, send_sems, recv_sems):
        x, y, c = _mesh_pos()
        chips = _other_chips(x, y)
        out_ref[2 * x + y] = b_ref[...]
        sends = [_remote(b_ref, out_ref.at[2 * x + y], send_sems.at[k], recv_sems.at[k], (px, py, c))
                 for k, (px, py) in enumerate(chips)]
        for cp in sends:
            cp.start()
        for k, (px, py) in enumerate(chips):
            _remote(b_ref, out_ref.at[2 * px + py], send_sems.at[k], recv_sems.at[k], (px, py, c)).wait_recv()
        for cp in sends:
            cp.wait_send()

    return pl.pallas_call(
        body, out_shape=jax.ShapeDtypeStruct((4, r, wd), blk.dtype), in_specs=[VMEM_SPEC], out_specs=VMEM_SPEC,
        scratch_shapes=[pltpu.SemaphoreType.DMA((3,)), pltpu.SemaphoreType.DMA((3,))],
        name="gather_small")(blk)


def _allreduce_small(blk):
    r, wd = blk.shape
    rels = [(dx, dy, dc) for dx in (0, 1) for dy in (0, 1) for dc in (0, 1) if dx or dy or dc]

    def body(b_ref, out_ref, buf_ref, send_sems, recv_sems):
        x, y, c = _mesh_pos()

        def peer(rel):
            dx, dy, dc = rel
            return (1 - x if dx else x, 1 - y if dy else y, 1 - c if dc else c)

        me = 4 * x + 2 * y + c
        buf_ref[me] = b_ref[...]
        sends = [_remote(b_ref, buf_ref.at[me], send_sems.at[k], recv_sems.at[k], peer(rel))
                 for k, rel in enumerate(rels)]
        for cp in sends:
            cp.start()
        for k, rel in enumerate(rels):
            px, py, pc = peer(rel)
            _remote(b_ref, buf_ref.at[4 * px + 2 * py + pc], send_sems.at[k], recv_sems.at[k],
                    (px, py, pc)).wait_recv()
        for cp in sends:
            cp.wait_send()
        acc = buf_ref[0]
        for d in range(1, 8):
            acc = acc + buf_ref[d]
        out_ref[...] = acc

    return pl.pallas_call(
        body, out_shape=jax.ShapeDtypeStruct((r, wd), blk.dtype), in_specs=[VMEM_SPEC], out_specs=VMEM_SPEC,
        scratch_shapes=[pltpu.VMEM((8, r, wd), blk.dtype), pltpu.SemaphoreType.DMA((7,)),
                        pltpu.SemaphoreType.DMA((7,))],
        name="allreduce_small")(blk)


def _rs_pair(gs, name):
    nt = len(gs)

    def body(*refs):
        ins, outs = refs[:nt], refs[nt:2 * nt]
        send_sems, recv_sems = refs[2 * nt:]
        x, y, c = _mesh_pos()
        cps = [_remote(ins[t].at[:, 1 - c], outs[t], send_sems.at[t], recv_sems.at[t], (x, y, 1 - c))
               for t in range(nt)]
        for cp in cps:
            cp.start()
        for cp in cps:
            cp.wait()

    return pl.pallas_call(
        body, out_shape=[jax.ShapeDtypeStruct(g.shape[:1] + g.shape[2:], g.dtype) for g in gs],
        in_specs=[ANY] * nt, out_specs=[ANY] * nt,
        scratch_shapes=[pltpu.SemaphoreType.DMA((nt,)), pltpu.SemaphoreType.DMA((nt,))], name=name)(*gs)


def _rs_tile(a, b):
    return _div_tile(a, 512 if b <= 1024 else 256, 16)


def _rs_pair_add(g, a, idx, name):
    _, _, rows, cols = g.shape
    tr = _rs_tile(rows, cols)

    def body(s_ref, g_ref, a_ref, p_ref):
        p_ref[...] = (g_ref[...] + a_ref[...]).astype(p_ref.dtype)

    blk = pl.BlockSpec((None, tr, cols), lambda j, i, s: (j, i, 0))
    spec = pltpu.PrefetchScalarGridSpec(
        num_scalar_prefetch=1, grid=(N_SHARD, rows // tr),
        in_specs=[pl.BlockSpec((None, None, tr, cols), lambda j, i, s: (j, s[0], i, 0)), blk], out_specs=blk)
    return pl.pallas_call(
        body, grid_spec=spec, out_shape=jax.ShapeDtypeStruct((N_SHARD, rows, cols), BF16), name=name,
        compiler_params=_params("parallel", "parallel"))(idx, g, a)


def _rs_final_add(g, a, b, idx, name):
    _, _, rows, cols = g.shape
    tr = _rs_tile(rows, cols)

    def body(s_ref, g_ref, a_ref, b0_ref, b1_ref, b2_ref, f_ref):
        own = g_ref[...] + a_ref[...]
        f_ref[...] = ((own + b0_ref[...].astype(F32)) + b1_ref[...].astype(F32)) + b2_ref[...].astype(F32)

    def b_spec(k):
        return pl.BlockSpec((None, tr, cols), lambda i, s: (k, i, 0))

    spec = pltpu.PrefetchScalarGridSpec(
        num_scalar_prefetch=1, grid=(rows // tr,),
        in_specs=[pl.BlockSpec((None, None, tr, cols), lambda i, s: (s[1], s[0], i, 0)),
                  pl.BlockSpec((None, tr, cols), lambda i, s: (s[1], i, 0)), b_spec(0), b_spec(1), b_spec(2)],
        out_specs=pl.BlockSpec((None, tr, cols), lambda i, s: (s[0], i, 0)))
    return pl.pallas_call(
        body, grid_spec=spec, out_shape=jax.ShapeDtypeStruct((2, rows, cols), F32), name=name,
        compiler_params=_params("parallel"))(idx, g, a, b, b, b)


def _rs_share(fs, name):
    nt = len(fs)

    def body(*refs):
        outs = refs[nt:2 * nt]
        send_sems, recv_sems = refs[2 * nt:]
        x, y, c = _mesh_pos()
        cps = [_remote(outs[t].at[c], outs[t].at[c], send_sems.at[t], recv_sems.at[t], (x, y, 1 - c))
               for t in range(nt)]
        for cp in cps:
            cp.start()
        for cp in cps:
            cp.wait()

    return pl.pallas_call(
        body, out_shape=[jax.ShapeDtypeStruct(f.shape, f.dtype) for f in fs],
        in_specs=[ANY] * nt, out_specs=[ANY] * nt, input_output_aliases={t: t for t in range(nt)},
        scratch_shapes=[pltpu.SemaphoreType.DMA((nt,)), pltpu.SemaphoreType.DMA((nt,))], name=name)(*fs)


def _adamw(w, g, m, v, name):
    rows, cols = w.shape
    tr = rows // 4 if rows % 32 == 0 else rows

    def body(w_ref, g_ref, m_ref, v_ref, go_ref, d_ref, mo_ref, vo_ref):
        gv = g_ref[...]
        go_ref[...] = gv
        mn = ADAM_B1 * m_ref[...] + (1.0 - ADAM_B1) * gv
        vn = ADAM_B2 * v_ref[...] + (1.0 - ADAM_B2) * (gv * gv)
        m_hat = mn / (1.0 - ADAM_B1 ** ADAM_STEP)
        v_hat = vn / (1.0 - ADAM_B2 ** ADAM_STEP)
        d_ref[...] = -ADAM_LR * (m_hat / (jnp.sqrt(v_hat) + ADAM_EPS) + ADAM_WD * w_ref[...])
        mo_ref[...] = mn
        vo_ref[...] = vn

    blk = pl.BlockSpec((tr, cols), lambda i: (i, 0))
    out = jax.ShapeDtypeStruct((rows, cols), F32)
    return pl.pallas_call(
        body, grid=(rows // tr,), in_specs=[blk] * 4, out_specs=[blk] * 4, out_shape=[out] * 4, name=name,
        compiler_params=_params("parallel"))(w, g, m, v)


BIG = ["ret_w_in", "ret_w_out", "dn_w_in", "dn_w_out", "ffn_w_gate", "ffn_w_up", "ffn_w_down"]
SMALL =["meta_tokens", "mix_norm_w", "ffn_norm_w", "ret_gn_w", "dn_conv_w", "dn_a_log", "dn_dt_bias",
         "dn_norm_w", "final_norm_w"]
SMALL_SHARDED = {"meta_tokens", "dn_conv_w", "dn_norm_w"}
ORDER = ["meta_tokens", "mix_norm_w", "ffn_norm_w", "ret_w_in", "ret_gn_w", "ret_w_out", "dn_w_in",
         "dn_conv_w", "dn_a_log", "dn_dt_bias", "dn_norm_w", "dn_w_out", "ffn_w_gate", "ffn_w_up",
         "ffn_w_down", "final_norm_w"]


def _halves(a):
    return a.reshape(2, -1, a.shape[-1])


def _pack_lanes(parts, align=8):
    flat = jnp.concatenate([p.reshape(-1) for p in parts])
    flat = jnp.pad(flat, (0, -flat.shape[0] % (align * LANES)))
    return flat.reshape(-1, LANES)


def _unpack(buf, shapes):
    lead = buf.shape[:-2]
    flat = buf.reshape(lead + (-1,))
    out, off = [], 0
    for shp in shapes:
        size = math.prod(shp)
        out.append(flat[..., off:off + size].reshape(lead + tuple(shp)))
        off += size
    return out


def _join_cols(shards):
    return jnp.concatenate([shards[j] for j in range(N_SHARD)], axis=-1)


def kernel(x, meta_tokens, mix_norm_w, ffn_norm_w, ret_w_in, ret_gn_w, ret_w_out, dn_w_in, dn_conv_w, dn_a_log, dn_dt_bias, dn_norm_w, dn_w_out, ffn_w_gate, ffn_w_up, ffn_w_down, final_norm_w, loss_target, m_meta_tokens, m_mix_norm_w, m_ffn_norm_w, m_ret_w_in, m_ret_gn_w, m_ret_w_out, m_dn_w_in, m_dn_conv_w, m_dn_a_log, m_dn_dt_bias, m_dn_norm_w, m_dn_w_out, m_ffn_w_gate, m_ffn_w_up, m_ffn_w_down, m_final_norm_w, v_meta_tokens, v_mix_norm_w, v_ffn_norm_w, v_ret_w_in, v_ret_gn_w, v_ret_w_out, v_dn_w_in, v_dn_conv_w, v_dn_a_log, v_dn_dt_bias, v_dn_norm_w, v_dn_w_out, v_ffn_w_gate, v_ffn_w_up, v_ffn_w_down, v_final_norm_w):
    w = dict(meta_tokens=meta_tokens, mix_norm_w=mix_norm_w, ffn_norm_w=ffn_norm_w, ret_w_in=ret_w_in,
             ret_gn_w=ret_gn_w, ret_w_out=ret_w_out, dn_w_in=dn_w_in, dn_conv_w=dn_conv_w, dn_a_log=dn_a_log,
             dn_dt_bias=dn_dt_bias, dn_norm_w=dn_norm_w, dn_w_out=dn_w_out, ffn_w_gate=ffn_w_gate,
             ffn_w_up=ffn_w_up, ffn_w_down=ffn_w_down, final_norm_w=final_norm_w)
    m = dict(meta_tokens=m_meta_tokens, mix_norm_w=m_mix_norm_w, ffn_norm_w=m_ffn_norm_w, ret_w_in=m_ret_w_in,
             ret_gn_w=m_ret_gn_w, ret_w_out=m_ret_w_out, dn_w_in=m_dn_w_in, dn_conv_w=m_dn_conv_w,
             dn_a_log=m_dn_a_log, dn_dt_bias=m_dn_dt_bias, dn_norm_w=m_dn_norm_w, dn_w_out=m_dn_w_out,
             ffn_w_gate=m_ffn_w_gate, ffn_w_up=m_ffn_w_up, ffn_w_down=m_ffn_w_down, final_norm_w=m_final_norm_w)
    v = dict(meta_tokens=v_meta_tokens, mix_norm_w=v_mix_norm_w, ffn_norm_w=v_ffn_norm_w, ret_w_in=v_ret_w_in,
             ret_gn_w=v_ret_gn_w, ret_w_out=v_ret_w_out, dn_w_in=v_dn_w_in, dn_conv_w=v_dn_conv_w,
             dn_a_log=v_dn_a_log, dn_dt_bias=v_dn_dt_bias, dn_norm_w=v_dn_norm_w, dn_w_out=v_dn_w_out,
             ffn_w_gate=v_ffn_w_gate, ffn_w_up=v_ffn_w_up, ffn_w_down=v_ffn_w_down, final_norm_w=v_final_norm_w)
    mx, my, mc = _mesh_pos()
    chip = 2 * mx + my

    sm_names = [n for n in SMALL if n in SMALL_SHARDED]
    sm_gathered = _unpack(_gather_small(_pack_lanes([w[n] for n in sm_names])), [w[n].shape for n in sm_names])
    full = {n: _join_cols(sm_gathered[i]) for i, n in enumerate(sm_names)}
    wts = {
        "meta_tokens": full["meta_tokens"], "mix_norm_w": mix_norm_w, "ffn_norm_w": ffn_norm_w,
        "ret_gn_w": ret_gn_w[0], "final_norm_w": final_norm_w, "dn_conv_w": full["dn_conv_w"][0],
        "dn_a_log": dn_a_log[0], "dn_dt_bias": dn_dt_bias[0], "dn_norm_w": full["dn_norm_w"][0],
    }
    idx = jnp.stack([mc, chip]).astype(jnp.int32)
    shards = {n: _halves(w[n].astype(MXU_DTYPE)) for n in BIG}
    loss_part, dh0, g, reduced = _train_step(x[0], loss_target[0], wts, shards, idx)
    seq = x.shape[1]
    grad_x = dh0[CHUNK:CHUNK + seq].reshape(x.shape)
    gsh = {}

    small_full_shapes = [g[n].shape for n in SMALL] + [(1,)]
    red = _unpack(_allreduce_small(_pack_lanes([g[n] for n in SMALL] + [loss_part[0, :1]])), small_full_shapes)
    loss = red[-1][0]
    for i, n in enumerate(SMALL):
        gn = red[i]
        if n in SMALL_SHARDED:
            width = w[n].shape[-1]
            gn = lax.dynamic_slice_in_dim(gn, chip * width, width, axis=gn.ndim - 1)
        gsh[n] = gn.reshape(w[n].shape)

    delta, new_m, new_v = {}, {}, {}
    for n in BIG:
        shp = w[n].shape
        two_d = (-1, shp[-1])
        g_, d_, m_, v_ = _adamw(w[n].reshape(two_d), reduced[n].reshape(two_d), m[n].reshape(two_d),
                                v[n].reshape(two_d), "adamw_" + n)
        gsh[n], delta[n], new_m[n], new_v[n] = g_.reshape(shp), d_.reshape(shp), m_.reshape(shp), v_.reshape(shp)
    sm_local_shapes = [w[n].shape for n in SMALL]
    _, d_, m_, v_ = _adamw(_pack_lanes([w[n] for n in SMALL]), _pack_lanes([gsh[n] for n in SMALL]),
                           _pack_lanes([m[n] for n in SMALL]), _pack_lanes([v[n] for n in SMALL]), "adamw_small")
    for n, dd, mm, vv in zip(SMALL, _unpack(d_, sm_local_shapes), _unpack(m_, sm_local_shapes),
                             _unpack(v_, sm_local_shapes)):
        delta[n], new_m[n], new_v[n] = dd, mm, vv

    return (loss, grad_x, *[gsh[n] for n in ORDER], *[delta[n] for n in ORDER],
            *[new_m[n] for n in ORDER], *[new_v[n] for n in ORDER])
```

```python
import functools
import math

import jax
import jax.numpy as jnp
from jax import lax
from jax.experimental import pallas as pl
from jax.experimental.pallas import tpu as pltpu

F32 = jnp.float32
BF16 = jnp.bfloat16
MXU_DTYPE = BF16

D_MODEL = 1024
N_META = 16
CHUNK = 64
PAD = CHUNK - N_META
RMS_EPS = 1e-6
RET_HEADS, RET_DK, RET_DV = 4, 256, 512
RET_QK, RET_V = RET_HEADS * RET_DK, RET_HEADS * RET_DV
RET_IN = 2 * RET_QK + 2 * RET_V
ROPE_BASE = 10000.0
DN_HEADS, DN_DK, DN_DV = 8, 128, 256
DN_QK, DN_V = DN_HEADS * DN_DK, DN_HEADS * DN_DV
DN_CONV_CH = 2 * DN_QK + DN_V
DN_IN = DN_CONV_CH + DN_V + 2 * DN_HEADS
LANES = 128
DN_IN_PAD = DN_CONV_CH + DN_V + LANES
CONV_K = 4
FFN_HIDDEN = 2816
ADAM_LR, ADAM_B1, ADAM_B2, ADAM_EPS, ADAM_WD, ADAM_STEP = 0.001, 0.9, 0.999, 1e-08, 0.01, 10

ROW_ALIGN = 256
VMEM_LIMIT = 56 * 1024 * 1024
MESH = pl.DeviceIdType.MESH
ANY = pl.BlockSpec(memory_space=pl.ANY)
VMEM_SPEC = pl.BlockSpec(memory_space=pltpu.VMEM)
_HI = lax.Precision.HIGHEST


def _params(*sem):
    return pltpu.CompilerParams(dimension_semantics=sem, vmem_limit_bytes=VMEM_LIMIT)


def _dg(a, b, ca, cb, hi):
    dims = (((ca,), (cb,)), ((), ()))

    def dot(p, q):
        return lax.dot_general(p, q, dims, preferred_element_type=F32)

    if not hi:
        return dot(a.astype(MXU_DTYPE), b.astype(MXU_DTYPE))
    if MXU_DTYPE == F32:
        return lax.dot_general(a, b, dims, precision=_HI, preferred_element_type=F32)
    a_hi, b_hi = a.astype(MXU_DTYPE), b.astype(MXU_DTYPE)
    a_lo = (a - a_hi.astype(F32)).astype(MXU_DTYPE)
    b_lo = (b - b_hi.astype(F32)).astype(MXU_DTYPE)
    return dot(a_hi, b_hi) + (dot(a_hi, b_lo) + dot(a_lo, b_hi))


def _nn(a, b, hi=False):
    return _dg(a, b, 1, 0, hi)


def _nt(a, b, hi=False):
    return _dg(a, b, 1, 1, hi)


def _tn(a, b, hi=False):
    return _dg(a, b, 0, 0, hi)


def _iota(shape, dim):
    return lax.broadcasted_iota(jnp.int32, shape, dim)


def _valid_rows(first_row, rows, seq):
    r = first_row + _iota((rows, 1), 0)
    return ((r >= PAD) & (r < CHUNK + seq)).astype(F32)


def _rope(t, cs, sn):
    half = t.shape[-1] // 2
    t1, t2 = t[:, :half], t[:, half:]
    return jnp.concatenate([t1 * cs - t2 * sn, t1 * sn + t2 * cs], axis=1)


def _rope_bwd(d, cs, sn):
    half = d.shape[-1] // 2
    d1, d2 = d[:, :half], d[:, half:]
    return jnp.concatenate([d1 * cs + d2 * sn, d2 * cs - d1 * sn], axis=1)


def _col(x, idx):
    oh = (_iota((1, x.shape[1]), 1) == idx).astype(F32)
    return jnp.sum(x * oh, axis=1, keepdims=True)


def _row(x, idx):
    oh = (_iota((x.shape[0], 1), 0) == idx).astype(F32)
    return jnp.sum(x * oh, axis=0, keepdims=True)


def _shift_down(x, halo8, k):
    xr = pltpu.roll(x, k, 0)
    hr = pltpu.roll(halo8, k, 0)
    first = jnp.where(_iota((8, 1), 0) < k, hr, xr[0:8])
    return jnp.concatenate([first, xr[8:]], axis=0)


def _shift_up(x, next8, j):
    rows = x.shape[0]
    xr = pltpu.roll(x, rows - j, 0)
    nr = pltpu.roll(next8, 8 - j, 0)
    last = jnp.where(_iota((8, 1), 0) >= 8 - j, nr, xr[rows - 8:])
    return jnp.concatenate([xr[:rows - 8], last], axis=0)


def _gated_norm(o, gate, w):
    r = lax.rsqrt(jnp.mean(o * o, axis=-1, keepdims=True) + RMS_EPS)
    return o * r * w * (gate * jax.nn.sigmoid(gate))


def _gated_norm_bwd(dy, o, gate, w):
    r = lax.rsqrt(jnp.mean(o * o, axis=-1, keepdims=True) + RMS_EPS)
    nrm = o * r
    sg = jax.nn.sigmoid(gate)
    sl = gate * sg
    dgate = dy * nrm * w * (sg * (1.0 + gate * (1.0 - sg)))
    dn = dy * w * sl
    dw = jnp.sum(dy * nrm * sl, axis=0, keepdims=True)
    do = r * (dn - nrm * jnp.mean(dn * nrm, axis=-1, keepdims=True))
    return do, dgate, dw


def _softplus(z):
    return jnp.maximum(z, 0.0) + jnp.log(1.0 + jnp.exp(-jnp.abs(z)))


def _row_tile(rows, cap=768):
    for t in (768, 512, 256, 128, 64, 32, 16, 8):
        if t <= cap and rows % t == 0:
            return t
    return rows


TILE_BUDGET = 44 * 1024 * 1024


def _fit_rows(rows, row_bytes, fixed_bytes, value_row_bytes):
    best = None
    for t in range(LANES, rows + 1, LANES):
        if rows % t == 0 and 2 * (row_bytes * t + fixed_bytes) + value_row_bytes * t <= TILE_BUDGET:
            best = t
    return best or _row_tile(rows, 256)


def _div_tile(n, cap, mult):
    best = None
    for t in range(mult, min(cap, n) + 1, mult):
        if n % t == 0:
            best = t
    return best or n


def _col_tile(cols, cap=1536):
    best = None
    for t in range(LANES, min(cap, cols) + 1, LANES):
        if cols % t == 0:
            best = t
    return best or cols


def _rms_fwd(h, w, name):
    rows, d = h.shape
    tm = _row_tile(rows)

    def body(h_ref, w_ref, o_ref):
        x = h_ref[...]
        r = lax.rsqrt(jnp.mean(x * x, axis=-1, keepdims=True) + RMS_EPS)
        o_ref[...] = (x * r * w_ref[...]).astype(o_ref.dtype)

    return pl.pallas_call(
        body, grid=(rows // tm,),
        in_specs=[pl.BlockSpec((tm, d), lambda i: (i, 0)), pl.BlockSpec((1, d), lambda i: (0, 0))],
        out_specs=pl.BlockSpec((tm, d), lambda i: (i, 0)),
        out_shape=jax.ShapeDtypeStruct((rows, d), BF16), name=name,
        compiler_params=_params("parallel"))(h, w.reshape(1, d))


def _rms_bwd(dy, h, w, resid, name):
    rows, d = h.shape
    tm = _row_tile(rows)

    def body(dy_ref, h_ref, w_ref, r_ref, dh_ref, dw_ref):
        i = pl.program_id(0)
        x = h_ref[...]
        r = lax.rsqrt(jnp.mean(x * x, axis=-1, keepdims=True) + RMS_EPS)
        xh = x * r
        dyv = dy_ref[...]
        dxh = dyv * w_ref[...]
        dh_ref[...] = r_ref[...] + r * (dxh - xh * jnp.mean(dxh * xh, axis=-1, keepdims=True))
        part = jnp.sum(dyv * xh, axis=0, keepdims=True)

        @pl.when(i == 0)
        def _():
            dw_ref[...] = part

        @pl.when(i > 0)
        def _():
            dw_ref[...] += part

    blk = pl.BlockSpec((tm, d), lambda i: (i, 0))
    vec = pl.BlockSpec((1, d), lambda i: (0, 0))
    return pl.pallas_call(
        body, grid=(rows // tm,), in_specs=[blk, blk, vec, blk], out_specs=[blk, vec],
        out_shape=[jax.ShapeDtypeStruct((rows, d), F32), jax.ShapeDtypeStruct((1, d), F32)],
        name=name, compiler_params=_params("arbitrary"))(dy, h, w.reshape(1, d), resid)


def _final_loss(h, w, tgt, seq, name):
    rows, d = h.shape
    tm = _row_tile(rows)

    def body(h_ref, w_ref, t_ref, dh_ref, dw_ref, loss_ref):
        i = pl.program_id(0)
        r_idx = i * tm + _iota((tm, 1), 0)
        m = ((r_idx >= CHUNK) & (r_idx < CHUNK + seq)).astype(F32)
        x = h_ref[...]
        wv = w_ref[...]
        r = lax.rsqrt(jnp.mean(x * x, axis=-1, keepdims=True) + RMS_EPS)
        xh = x * r
        err = (xh * wv - t_ref[...]) * m
        lpart = 0.5 * jnp.sum(jnp.mean(err * err, axis=-1, keepdims=True), axis=0, keepdims=True)
        dyv = err * (1.0 / d)
        dxh = dyv * wv
        dh_ref[...] = r * (dxh - xh * jnp.mean(dxh * xh, axis=-1, keepdims=True))
        part = jnp.sum(dyv * xh, axis=0, keepdims=True)

        @pl.when(i == 0)
        def _():
            dw_ref[...] = part
            loss_ref[...] = jnp.broadcast_to(lpart, loss_ref.shape)

        @pl.when(i > 0)
        def _():
            dw_ref[...] += part
            loss_ref[...] += jnp.broadcast_to(lpart, loss_ref.shape)

    blk = pl.BlockSpec((tm, d), lambda i: (i, 0))
    vec = pl.BlockSpec((1, d), lambda i: (0, 0))
    return pl.pallas_call(
        body, grid=(rows // tm,), in_specs=[blk, vec, blk],
        out_specs=[blk, vec, pl.BlockSpec((1, LANES), lambda i: (0, 0))],
        out_shape=[jax.ShapeDtypeStruct((rows, d), F32), jax.ShapeDtypeStruct((1, d), F32),
                   jax.ShapeDtypeStruct((1, LANES), F32)],
        name=name, compiler_params=_params("arbitrary"))(h, w.reshape(1, d), tgt)


def _isz(x):
    return jnp.dtype(x.dtype).itemsize


def _mm(a, b, *, mode, name, out_dtype=F32, resid=None, col_cap=1536, ride=None):
    if mode == "tn":
        m, k = a.shape
        n = b.shape[1]
        tn = _col_tile(n, col_cap)
        tm = _fit_rows(m, k * _isz(a) + tn * _isz(b), (3 * k * tn * 4) // 2, 2 * (k + tn))

        def body_tn(a_ref, b_ref, o_ref):
            i = pl.program_id(1)
            part = _tn(a_ref[...], b_ref[...])

            @pl.when(i == 0)
            def _():
                o_ref[...] = part

            @pl.when(i > 0)
            def _():
                o_ref[...] += part

        return pl.pallas_call(
            body_tn, grid=(n // tn, m // tm),
            in_specs=[pl.BlockSpec((tm, k), lambda j, i: (i, 0)),
                      pl.BlockSpec((tm, tn), lambda j, i: (i, j))],
            out_specs=pl.BlockSpec((k, tn), lambda j, i: (0, j)),
            out_shape=jax.ShapeDtypeStruct((k, n), F32), name=name,
            compiler_params=_params("parallel", "arbitrary"))(a, b)

    m, ka = a.shape
    n = b.shape[1] if mode == "nn" else b.shape[0]
    has_resid = resid is not None
    tn = _col_tile(n, col_cap)
    tm = _fit_rows(m, ka * _isz(a) + tn * (jnp.dtype(out_dtype).itemsize + (4 if has_resid else 0)),
                   ka * tn * _isz(b), 2 * ka + 8 * tn)

    def body(*refs):
        if has_resid:
            a_ref, b_ref, r_ref, o_ref = refs
        else:
            a_ref, b_ref, o_ref = refs
        acc = _nn(a_ref[...], b_ref[...]) if mode == "nn" else _nt(a_ref[...], b_ref[...])
        if has_resid:
            acc = acc + r_ref[...]
        o_ref[...] = acc.astype(o_ref.dtype)

    b_spec = (pl.BlockSpec((b.shape[0], tn), lambda j, i: (0, j)) if mode == "nn"
              else pl.BlockSpec((tn, b.shape[1]), lambda j, i: (j, 0)))
    o_spec = pl.BlockSpec((tm, tn), lambda j, i: (i, j))
    in_specs = [pl.BlockSpec((tm, ka), lambda j, i: (i, 0)), b_spec]
    args = [a, b]
    if has_resid:
        in_specs.append(o_spec)
        args.append(resid)
    res, rode = _pcall(body, args, grid=(n // tn, m // tm), in_specs=in_specs, out_specs=[o_spec],
                       out_shape=[jax.ShapeDtypeStruct((m, n), out_dtype)], name=name,
                       sem=("parallel", "parallel"), ride=ride)
    return res[0] if ride is None else (res[0], rode)


N_SHARD = 4


def _gmm(name, grid, args, in_specs, out_specs, out_shape, fn, red_axis=None, init_arg=None, aliases=None,
         ride=None):
    n_in = len(args)
    single = not isinstance(out_shape, (list, tuple))
    out_specs = [out_specs] if single else list(out_specs)
    out_shape = [out_shape] if single else list(out_shape)

    def body(*refs):
        _gmm_step(fn, refs[:n_in], refs[n_in:], red_axis, init_arg)

    sem = tuple("arbitrary" if ax == red_axis else "parallel" for ax in range(len(grid)))
    res, rode = _pcall(body, args, grid=grid, in_specs=in_specs, out_specs=out_specs, out_shape=out_shape,
                       name=name, sem=sem, aliases=aliases, ride=ride)
    ours = res[0] if single else res
    return ours if ride is None else (ours, rode)


def _gmm_step(fn, ins, outs, red_axis, init_arg):
    parts = fn(*ins)
    if red_axis is None:
        for o_ref, p in zip(outs, parts):
            o_ref[...] = p.astype(o_ref.dtype)
        return
    k = pl.program_id(red_axis)

    @pl.when(k == 0)
    def _():
        for idx, (o_ref, p) in enumerate(zip(outs, parts)):
            o_ref[...] = p + ins[init_arg][...] if (idx == 0 and init_arg is not None) else p

    @pl.when(k > 0)
    def _():
        for o_ref, p in zip(outs, parts):
            o_ref[...] += p


def _ride_body(ride, grid, n_in, n_out, n_scratch, body):
    n_rin, n_rout = len(ride.arrays), len(ride.out_shape)
    nsteps = math.prod(grid)

    def wrapped(*refs):
        ins = refs[:n_in]
        r_ins = refs[n_in:n_in + n_rin]
        o0 = n_in + n_rin
        outs = refs[o0:o0 + n_out]
        r_outs = refs[o0 + n_out:o0 + n_out + n_rout]
        s0 = o0 + n_out + n_rout
        scratch = refs[s0:s0 + n_scratch]
        send_sems, recv_sems = refs[-2:]
        step = pl.program_id(0)
        for ax in range(1, len(grid)):
            step = step * grid[ax] + pl.program_id(ax)
        ride.emit(step, nsteps, r_ins, r_outs, send_sems, recv_sems, before=True)
        body(*ins, *outs, *scratch)
        ride.emit(step, nsteps, r_ins, r_outs, send_sems, recv_sems, before=False)

    return wrapped


def _pcall(body, args, *, grid, in_specs, out_specs, out_shape, name, sem, scratch=(), aliases=None, ride=None):
    if ride is None:
        res = pl.pallas_call(body, grid=grid, in_specs=list(in_specs), out_specs=list(out_specs),
                             out_shape=list(out_shape), scratch_shapes=list(scratch), name=name,
                             input_output_aliases=aliases or {}, compiler_params=_params(*sem))(*args)
        return res, None
    n_in, n_out = len(args), len(out_shape)
    res = pl.pallas_call(
        _ride_body(ride, grid, n_in, n_out, len(scratch), body), grid=grid,
        in_specs=list(in_specs) + ride.in_specs, out_specs=list(out_specs) + ride.out_specs,
        out_shape=list(out_shape) + ride.out_shape, scratch_shapes=list(scratch) + ride.scratch, name=name,
        input_output_aliases=aliases or {},
        compiler_params=_params(*(("arbitrary",) * len(grid))))(*args, *ride.arrays)
    return res[:n_out], res[n_out:]


def _mm_cols(a, ws, name, ride=None):
    m, k = a.shape
    n = ws.shape[2]
    tm = _fit_rows(m, k * _isz(a) + n * 4, k * n * _isz(ws), 4 * n)
    return _gmm(name, (N_SHARD, m // tm), [a, ws],
                [pl.BlockSpec((tm, k), lambda j, i: (i, 0)), pl.BlockSpec((None, k, n), lambda j, i: (j, 0, 0))],
                pl.BlockSpec((tm, n), lambda j, i: (i, j)), jax.ShapeDtypeStruct((m, N_SHARD * n), F32),
                lambda a_ref, w_ref: (_nn(a_ref[...], w_ref[...]),), ride=ride)


def _mm_cols_t(d, ws, name, ride=None):
    m = d.shape[0]
    _, k, n = ws.shape
    tm = _fit_rows(m, n * _isz(d) + k * 4, k * n * _isz(ws), 8 * k)
    return _gmm(name, (m // tm, N_SHARD), [d, ws],
                [pl.BlockSpec((tm, n), lambda i, j: (i, j)), pl.BlockSpec((None, k, n), lambda i, j: (j, 0, 0))],
                pl.BlockSpec((tm, k), lambda i, j: (i, 0)), jax.ShapeDtypeStruct((m, k), F32),
                lambda d_ref, w_ref: (_nt(d_ref[...], w_ref[...]),), red_axis=1, ride=ride)


def _mm_cols_grad(a, d, name):
    m, k = a.shape
    n = d.shape[1] // N_SHARD
    tm = _fit_rows(m, k * _isz(a) + n * _isz(d), (3 * k * n * 4) // 2, 2 * (k + n))
    return _gmm(name, (N_SHARD, m // tm), [a, d],
                [pl.BlockSpec((tm, k), lambda j, i: (i, 0)), pl.BlockSpec((tm, n), lambda j, i: (i, j))],
                pl.BlockSpec((None, k, n), lambda j, i: (j, 0, 0)), jax.ShapeDtypeStruct((N_SHARD, k, n), F32),
                lambda a_ref, d_ref: (_tn(a_ref[...], d_ref[...]),), red_axis=1)


def _ffn_up(hn, wg, wu, layer, name):
    m, k = hn.shape
    n = wg.shape[3]
    tm = _fit_rows(m, k * _isz(hn) + 3 * n * jnp.dtype(BF16).itemsize, 2 * k * n * _isz(wg), 16 * n)

    def fn(a_ref, wg_ref, wu_ref):
        a = a_ref[...]
        g = _nn(a, wg_ref[...])
        u = _nn(a, wu_ref[...])
        return g, u, g * jax.nn.sigmoid(g) * u

    w_spec = pl.BlockSpec((None, None, k, n), lambda j, i: (j, layer, 0, 0))
    o_spec = pl.BlockSpec((None, tm, n), lambda j, i: (j, i, 0))
    out = jax.ShapeDtypeStruct((N_SHARD, m, n), BF16)
    return _gmm(name, (N_SHARD, m // tm), [hn, wg, wu],
                [pl.BlockSpec((tm, k), lambda j, i: (i, 0)), w_spec, w_spec],
                [o_spec, o_spec, o_spec], [out, out, out], fn)


def _ffn_down(act, wd, resid, layer, name):
    _, m, n = act.shape
    d = wd.shape[3]
    tm = _fit_rows(m, N_SHARD * n * _isz(act) + 2 * d * 4, N_SHARD * n * d * _isz(wd), 8 * d)

    def fn(a_ref, w_ref, r_ref):
        acc = r_ref[...]
        for j in range(N_SHARD):
            acc = acc + _nn(a_ref[j], w_ref[j])
        return (acc,)

    row = pl.BlockSpec((tm, d), lambda i: (i, 0))
    return _gmm(name, (m // tm,), [act, wd, resid],
                [pl.BlockSpec((N_SHARD, tm, n), lambda i: (0, i, 0)),
                 pl.BlockSpec((N_SHARD, None, n, d), lambda i: (0, layer, 0, 0)), row],
                row, jax.ShapeDtypeStruct((m, d), F32), fn)


def _ffn_down_bwd(dh, wd, g, u, layer, name):
    m, d = dh.shape
    n = wd.shape[2]
    tm = _fit_rows(m, d * _isz(dh) + 4 * n * jnp.dtype(BF16).itemsize, n * d * _isz(wd), 2 * d + 24 * n)

    def fn(dh_ref, wd_ref, g_ref, u_ref):
        dact = _nt(dh_ref[...], wd_ref[...])
        gv = g_ref[...].astype(F32)
        uv = u_ref[...].astype(F32)
        sg = jax.nn.sigmoid(gv)
        return dact * uv * (sg * (1.0 + gv * (1.0 - sg))), dact * gv * sg

    o_spec = pl.BlockSpec((None, tm, n), lambda j, i: (j, i, 0))
    out = jax.ShapeDtypeStruct((N_SHARD, m, n), BF16)
    return _gmm(name, (N_SHARD, m // tm), [dh, wd, g, u],
                [pl.BlockSpec((tm, d), lambda j, i: (i, 0)),
                 pl.BlockSpec((None, None, n, d), lambda j, i: (j, layer, 0, 0)), o_spec, o_spec],
                [o_spec, o_spec], [out, out], fn)


def _ffn_up_bwd(dg, du, wg, wu, layer, name):
    _, m, n = dg.shape
    k = wg.shape[2]
    tm = _fit_rows(m, 2 * N_SHARD * n * _isz(dg) + k * 4, 2 * N_SHARD * k * n * _isz(wg), 8 * k)

    def fn(dg_ref, du_ref, wg_ref, wu_ref):
        acc = _nt(dg_ref[0], wg_ref[0]) + _nt(du_ref[0], wu_ref[0])
        for j in range(1, N_SHARD):
            acc = acc + _nt(dg_ref[j], wg_ref[j]) + _nt(du_ref[j], wu_ref[j])
        return (acc,)

    d_spec = pl.BlockSpec((N_SHARD, tm, n), lambda i: (0, i, 0))
    w_spec = pl.BlockSpec((N_SHARD, None, k, n), lambda i: (0, layer, 0, 0))
    return _gmm(name, (m // tm,), [dg, du, wg, wu], [d_spec, d_spec, w_spec, w_spec],
                pl.BlockSpec((tm, k), lambda i: (i, 0)), jax.ShapeDtypeStruct((m, k), F32), fn)


def _ffn_wgrad(lhs, rhs_list, layer, layers, prev, lhs_sharded, name):
    if lhs_sharded:
        _, m, k = lhs.shape
        n = rhs_list[0].shape[1]
    else:
        m, k = lhs.shape
        n = rhs_list[0].shape[2]
    n_out = len(rhs_list)
    tm = _fit_rows(m, k * _isz(lhs) + n_out * n * _isz(rhs_list[0]), (3 * n_out * k * n * 4) // 2,
                   2 * (k + n_out * n))
    sh = pl.BlockSpec((None, tm, k if lhs_sharded else n), lambda j, i: (j, i, 0))
    fl = pl.BlockSpec((tm, n if lhs_sharded else k), lambda j, i: (i, 0))
    n_out = len(rhs_list)
    args = [lhs] + list(rhs_list)
    in_specs = [sh if lhs_sharded else fl] + [fl if lhs_sharded else sh] * n_out
    aliases = None
    if prev is not None:
        aliases = {len(args) + t: t for t in range(n_out)}
        args = args + list(prev)
        in_specs = in_specs + [ANY] * n_out

    def fn(l_ref, *rest):
        lv = l_ref[...]
        return tuple(_tn(lv, r_ref[...]) for r_ref in rest[:n_out])

    o_spec = pl.BlockSpec((None, None, k, n), lambda j, i: (j, layer, 0, 0))
    out = jax.ShapeDtypeStruct((N_SHARD, layers, k, n), F32)
    return _gmm(name, (N_SHARD, m // tm), args, in_specs, [o_spec] * n_out, [out] * n_out, fn,
                red_axis=1, aliases=aliases)


def _ret_consts():
    log_gamma = jnp.log1p(-jnp.exp2(-5.0 - jnp.arange(RET_HEADS, dtype=F32)))
    idx = jnp.arange(CHUNK, dtype=F32)
    rel = idx[:, None] - idx[None, :]
    dmask = jnp.where((rel >= 0)[None], jnp.exp(log_gamma[:, None, None] * jnp.maximum(rel, 0.0)), 0.0)
    xi = jnp.exp(log_gamma[:, None] * (idx[None, :] + 1.0))[:, :, None]
    zeta = jnp.exp(log_gamma[:, None] * (CHUNK - 1.0 - idx[None, :]))[:, :, None]
    gamma_c = jnp.exp(log_gamma * CHUNK)
    wide = (RET_HEADS, CHUNK, RET_DK)
    return dmask, jnp.broadcast_to(xi, wide), jnp.broadcast_to(zeta, wide), gamma_c


def _rope_tables(rows):
    half = RET_DK // 2
    inv_freq = ROPE_BASE ** (-jnp.arange(half, dtype=F32) / half)
    pos = (jnp.arange(rows) - PAD).astype(F32)
    ang = pos[:, None] * inv_freq[None, :]
    return jnp.cos(ang), jnp.sin(ang)


def _ret_specs(order):
    return [pl.BlockSpec((CHUNK, RET_QK), lambda n: (order(n), 0)),
            pl.BlockSpec((CHUNK, RET_QK), lambda n: (order(n), 1)),
            pl.BlockSpec((CHUNK, RET_V), lambda n: (order(n), 1)),
            pl.BlockSpec((CHUNK, RET_V), lambda n: (order(n), 2))]


def _ret_const_specs():
    return [pl.BlockSpec((RET_HEADS, CHUNK, CHUNK), lambda n: (0, 0, 0)),
            pl.BlockSpec((RET_HEADS, CHUNK, RET_DK), lambda n: (0, 0, 0)),
            pl.BlockSpec((RET_HEADS, CHUNK, RET_DK), lambda n: (0, 0, 0)),
            pl.BlockSpec((1, RET_DV), lambda n: (0, 0))]


def _ret_fwd(proj, cos, sin, consts, gn_w, seq, ride=None):
    rows = proj.shape[0]
    nc = rows // CHUNK
    dmask, xi, zeta, gamma_c = consts

    def body(gam_ref, q_ref, k_ref, v_ref, g_ref, cos_ref, sin_ref, dm_ref, xi_ref, ze_ref, gn_ref,
             o_ref, y_ref, ss_ref, s_ref):
        n = pl.program_id(0)

        @pl.when(n == 0)
        def _():
            s_ref[...] = jnp.zeros_like(s_ref)

        cs, sn = cos_ref[...], sin_ref[...]
        kscale = _valid_rows(n * CHUNK, CHUNK, seq) * (RET_DK ** -0.5)
        gn = gn_ref[...]
        hs = range(RET_HEADS)
        qk_cols = [slice(h * RET_DK, (h + 1) * RET_DK) for h in hs]
        v_cols = [slice(h * RET_DV, (h + 1) * RET_DV) for h in hs]
        qr_l = [_rope(q_ref[:, c], cs, sn) for c in qk_cols]
        kr_l = [_rope(k_ref[:, c], cs, sn) * kscale for c in qk_cols]
        v_l = [v_ref[:, c] for c in v_cols]
        s_l = [s_ref[h] for h in hs]
        sc_l = [_nt(qr, kr) * dm_ref[h] for h, (qr, kr) in enumerate(zip(qr_l, kr_l))]
        o_l = [_nn(sc_l[h], v_l[h]) + _nn(qr_l[h] * xi_ref[h], s_l[h]) for h in hs]
        for h in hs:
            ss_ref[0, h] = s_l[h].astype(ss_ref.dtype)
            s_ref[h] = gam_ref[h] * s_l[h] + _tn(kr_l[h] * ze_ref[h], v_l[h])
            o_ref[:, v_cols[h]] = o_l[h]
            y_ref[:, v_cols[h]] = _gated_norm(o_l[h], g_ref[:, v_cols[h]], gn).astype(y_ref.dtype)

    fwd = lambda n: n
    row128 = pl.BlockSpec((CHUNK, RET_DK // 2), lambda n: (n, 0))
    row_v = pl.BlockSpec((CHUNK, RET_V), lambda n: (n, 0))
    res, rode = _pcall(
        body, [gamma_c, proj, proj, proj, proj, cos, sin, dmask, xi, zeta, gn_w.reshape(1, RET_DV)],
        grid=(nc,),
        in_specs=[pl.BlockSpec(memory_space=pltpu.SMEM)] + _ret_specs(fwd) + [row128, row128]
        + _ret_const_specs(),
        out_specs=[row_v, row_v,
                   pl.BlockSpec((1, RET_HEADS, RET_DK, RET_DV), lambda n: (n, 0, 0, 0))],
        out_shape=[jax.ShapeDtypeStruct((rows, RET_V), F32), jax.ShapeDtypeStruct((rows, RET_V), BF16),
                   jax.ShapeDtypeStruct((nc, RET_HEADS, RET_DK, RET_DV), BF16)],
        scratch=[pltpu.VMEM((RET_HEADS, RET_DK, RET_DV), F32)], name="ret_fwd", sem=("arbitrary",), ride=ride)
    return res if ride is None else (res, rode)


def _ret_bwd(proj, o, dy, states, cos, sin, consts, gn_w, seq, ride=None):
    rows = proj.shape[0]
    nc = rows // CHUNK
    dmask, xi, zeta, gamma_c = consts

    def body(gam_ref, q_ref, k_ref, v_ref, g_ref, o_ref, dy_ref, ss_ref, cos_ref, sin_ref,
             dm_ref, xi_ref, ze_ref, gn_ref, dp_ref, dgn_ref, ds_ref):
        n = pl.program_id(0)

        @pl.when(n == 0)
        def _():
            ds_ref[...] = jnp.zeros_like(ds_ref)
            dgn_ref[...] = jnp.zeros_like(dgn_ref)

        cs, sn = cos_ref[...], sin_ref[...]
        kscale = _valid_rows((nc - 1 - n) * CHUNK, CHUNK, seq) * (RET_DK ** -0.5)
        gn = gn_ref[...]
        dgn = jnp.zeros((1, RET_DV), F32)
        hs = range(RET_HEADS)
        qk_cols = [slice(h * RET_DK, (h + 1) * RET_DK) for h in hs]
        v_cols = [slice(h * RET_DV, (h + 1) * RET_DV) for h in hs]
        qr_l = [_rope(q_ref[:, c], cs, sn) for c in qk_cols]
        kr_l = [_rope(k_ref[:, c], cs, sn) * kscale for c in qk_cols]
        v_l = [v_ref[:, c] for c in v_cols]
        s_l = [ss_ref[0, h] for h in hs]
        ds_l = [ds_ref[h] for h in hs]
        gnb = [_gated_norm_bwd(dy_ref[:, c], o_ref[:, c], g_ref[:, c], gn) for c in v_cols]
        do_l = [x[0] for x in gnb]
        sc_l = [_nt(qr_l[h], kr_l[h]) * dm_ref[h] for h in hs]
        dsc_l = [_nt(do_l[h], v_l[h]) * dm_ref[h] for h in hs]
        dv_l = [_tn(sc_l[h], do_l[h]) + _nn(kr_l[h] * ze_ref[h], ds_l[h]) for h in hs]
        dqr_l = [_nn(dsc_l[h], kr_l[h]) + _nt(do_l[h], s_l[h]) * xi_ref[h] for h in hs]
        dkr_l = [_tn(dsc_l[h], qr_l[h]) + _nt(v_l[h], ds_l[h]) * ze_ref[h] for h in hs]
        for h in hs:
            dgn = dgn + gnb[h][2]
            ds_ref[h] = gam_ref[h] * ds_l[h] + _tn(qr_l[h] * xi_ref[h], do_l[h])
            dp_ref[:, qk_cols[h]] = _rope_bwd(dqr_l[h], cs, sn).astype(dp_ref.dtype)
            dp_ref[:, RET_QK + h * RET_DK:RET_QK + (h + 1) * RET_DK] = (
                _rope_bwd(dkr_l[h] * kscale, cs, sn).astype(dp_ref.dtype))
            dp_ref[:, 2 * RET_QK + h * RET_DV:2 * RET_QK + (h + 1) * RET_DV] = dv_l[h].astype(dp_ref.dtype)
            dp_ref[:, 2 * RET_QK + RET_V + h * RET_DV:2 * RET_QK + RET_V + (h + 1) * RET_DV] = (
                gnb[h][1].astype(dp_ref.dtype))
        dgn_ref[...] += dgn

    rev = lambda n: nc - 1 - n
    row128 = pl.BlockSpec((CHUNK, RET_DK // 2), lambda n: (rev(n), 0))
    row_v = pl.BlockSpec((CHUNK, RET_V), lambda n: (rev(n), 0))
    res, rode = _pcall(
        body, [gamma_c, proj, proj, proj, proj, o, dy, states, cos, sin, dmask, xi, zeta,
               gn_w.reshape(1, RET_DV)],
        grid=(nc,),
        in_specs=[pl.BlockSpec(memory_space=pltpu.SMEM)] + _ret_specs(rev) + [
            row_v, row_v, pl.BlockSpec((1, RET_HEADS, RET_DK, RET_DV), lambda n: (rev(n), 0, 0, 0)),
            row128, row128] + _ret_const_specs(),
        out_specs=[pl.BlockSpec((CHUNK, RET_IN), lambda n: (rev(n), 0)),
                   pl.BlockSpec((1, RET_DV), lambda n: (0, 0))],
        out_shape=[jax.ShapeDtypeStruct((rows, RET_IN), BF16), jax.ShapeDtypeStruct((1, RET_DV), F32)],
        scratch=[pltpu.VMEM((RET_HEADS, RET_DK, RET_DV), F32)], name="ret_bwd", sem=("arbitrary",), ride=ride)
    return res if ride is None else (res, rode)


GATE_COL = DN_CONV_CH // DN_V
BA_COL = (DN_CONV_CH + DN_V) // LANES
BETA_LANE, DECAY_LANE = 0, DN_HEADS
INV_SHIFT = 4
INV_SQUARINGS = INV_SHIFT - 1
assert CHUNK == 4 << INV_SHIFT


def _dn_in_specs(order):
    return [pl.BlockSpec((CHUNK, DN_CONV_CH), lambda n: (order(n), 0)),
            pl.BlockSpec((8, DN_CONV_CH), lambda n: (jnp.maximum(order(n) * (CHUNK // 8) - 1, 0), 0)),
            pl.BlockSpec((CHUNK, DN_V), lambda n: (order(n), GATE_COL)),
            pl.BlockSpec((CHUNK, LANES), lambda n: (order(n), BA_COL)),
            pl.BlockSpec((CONV_K, 1, DN_CONV_CH), lambda n: (0, 0, 0)),
            pl.BlockSpec((1, LANES), lambda n: (0, 0)),
            pl.BlockSpec((1, LANES), lambda n: (0, 0)),
            pl.BlockSpec((1, DN_DV), lambda n: (0, 0))]


def _dn_front(c, seq, x_ref, halo_ref, ba_ref, cw_ref, al_ref, dt_ref):
    valid = _valid_rows(c * CHUNK, CHUNK, seq)
    xin = x_ref[...] * valid
    halo = halo_ref[...] * _valid_rows(c * CHUNK - 8, 8, seq)
    x_sh = [xin] + [_shift_down(xin, halo, k) for k in range(1, CONV_K)]
    yc = x_sh[0] * cw_ref[CONV_K - 1]
    for k in range(1, CONV_K):
        yc = yc + x_sh[k] * cw_ref[CONV_K - 1 - k]
    sgc = jax.nn.sigmoid(yc)
    ba = ba_ref[...]
    sig = jax.nn.sigmoid(ba)
    beta = sig * valid
    z = ba + dt_ref[...]
    eal = jnp.exp(al_ref[...])
    g = -eal * _softplus(z) * valid
    ri, ci = _iota((CHUNK, CHUNK), 0), _iota((CHUNK, CHUNK), 1)
    lower = (ri >= ci).astype(F32)
    upper = (ri <= ci).astype(F32)
    eye = (ri == ci).astype(F32)
    gam = _nn(lower, g, hi=True)
    gam_t = _tn(g, upper, hi=True)
    return dict(valid=valid, x_sh=x_sh, yc=yc, sgc=sgc, act=yc * sgc, sig=sig, beta=beta, z=z,
                eal=eal, g=g, gam=gam, gam_t=gam_t, ri=ri, ci=ci, upper=upper, eye=eye)


def _dn_head(f, h):
    act = f["act"]
    q_raw = act[:, h * DN_DK:(h + 1) * DN_DK]
    k_raw = act[:, DN_QK + h * DN_DK:DN_QK + (h + 1) * DN_DK]
    v = act[:, 2 * DN_QK + h * DN_DV:2 * DN_QK + (h + 1) * DN_DV]
    rq = lax.rsqrt(jnp.sum(q_raw * q_raw, axis=-1, keepdims=True) + RMS_EPS)
    rk = lax.rsqrt(jnp.sum(k_raw * k_raw, axis=-1, keepdims=True) + RMS_EPS)
    qh = q_raw * rq
    kn = k_raw * rk
    gam_c = _col(f["gam"], DECAY_LANE + h)
    gam_r = _row(f["gam_t"], DECAY_LANE + h)
    bc = _col(f["beta"], BETA_LANE + h)
    diff = gam_c - gam_r
    decay = jnp.where(f["ri"] >= f["ci"], jnp.exp(jnp.minimum(diff, 0.0)), 0.0)
    glast = jnp.sum(gam_r * (_iota((1, CHUNK), 1) == CHUNK - 1).astype(F32), axis=1, keepdims=True)
    return dict(rq=rq, rk=rk, qh=qh, qn=qh * (DN_DK ** -0.5), kn=kn, v=v, gam_c=gam_c, gam_r=gam_r,
                bc=bc, diff=diff, decay=decay, egam=jnp.exp(gam_c), glast=glast,
                eglast=jnp.exp(glast), ekd=jnp.exp(glast - gam_c))


def _dn_fwd(proj, conv_w, alog, dtb, norm_w, seq):
    rows = proj.shape[0]
    nc = rows // CHUNK

    def body(x_ref, halo_ref, gate_ref, ba_ref, cw_ref, al_ref, dt_ref, nw_ref,
             o_ref, y_ref, ss_ref, t_ref, s_ref):
        n = pl.program_id(0)

        @pl.when(n == 0)
        def _():
            s_ref[...] = jnp.zeros_like(s_ref)

        f = _dn_front(n, seq, x_ref, halo_ref, ba_ref, cw_ref, al_ref, dt_ref)
        ri, ci = f["ri"], f["ci"]
        eye = f["eye"]
        diag_m = (jnp.right_shift(ri, INV_SHIFT) == jnp.right_shift(ci, INV_SHIFT)).astype(F32)
        half_m = (jnp.right_shift(ri, INV_SHIFT + 1) == jnp.right_shift(ci, INV_SHIFT + 1)).astype(F32)
        nw = nw_ref[...]
        heads = [_dn_head(f, h) for h in range(DN_HEADS)]
        a_all = [jnp.where(ri > ci, hd["bc"] * _nt(hd["kn"], hd["kn"]) * hd["decay"], 0.0) for hd in heads]
        b_all = [a * diag_m for a in a_all]
        t_all = [eye - b for b in b_all]
        for _ in range(INV_SQUARINGS):
            b_all = [_nn(b, b, hi=True) for b in b_all]
            t_all = [t + _nn(t, b, hi=True) for t, b in zip(t_all, b_all)]
        for off_m in (half_m - diag_m, 1.0 - half_m):
            x_all = [_nn(a * off_m, t, hi=True) for a, t in zip(a_all, t_all)]
            t_all = [t - _nn(t, x, hi=True) for t, x in zip(t_all, x_all)]
        u_all = [_nn(t, hd["v"] * hd["bc"], hi=True) for t, hd in zip(t_all, heads)]
        w_all = [_nn(t, hd["kn"] * (hd["bc"] * hd["egam"]), hi=True) for t, hd in zip(t_all, heads)]
        for h in range(DN_HEADS):
            hd = heads[h]
            v_cols = slice(h * DN_DV, (h + 1) * DN_DV)
            t_ref[0, h] = t_all[h]
            s = s_ref[h]
            ss_ref[0, h] = s
            u, w = u_all[h], w_all[h]
            v_new = u - _nn(w, s)
            qk = _nt(hd["qn"], hd["kn"]) * hd["decay"]
            o = _nn(hd["qn"] * hd["egam"], s) + _nn(qk, v_new)
            s_ref[h] = s * hd["eglast"] + _tn(hd["kn"] * hd["ekd"], v_new)
            o_ref[:, v_cols] = o
            y_ref[:, v_cols] = _gated_norm(o, gate_ref[:, v_cols], nw).astype(y_ref.dtype)

    fwd = lambda n: n
    row_v = pl.BlockSpec((CHUNK, DN_V), lambda n: (n, 0))
    return pl.pallas_call(
        body, grid=(nc,), in_specs=_dn_in_specs(fwd),
        out_specs=[row_v, row_v,
                   pl.BlockSpec((1, DN_HEADS, DN_DK, DN_DV), lambda n: (n, 0, 0, 0)),
                   pl.BlockSpec((1, DN_HEADS, CHUNK, CHUNK), lambda n: (n, 0, 0, 0))],
        out_shape=[jax.ShapeDtypeStruct((rows, DN_V), F32), jax.ShapeDtypeStruct((rows, DN_V), BF16),
                   jax.ShapeDtypeStruct((nc, DN_HEADS, DN_DK, DN_DV), F32),
                   jax.ShapeDtypeStruct((nc, DN_HEADS, CHUNK, CHUNK), F32)],
        scratch_shapes=[pltpu.VMEM((DN_HEADS, DN_DK, DN_DV), F32)],
        name="dn_fwd", compiler_params=_params("arbitrary"))(
            proj, proj, proj, proj, conv_w, alog, dtb, norm_w.reshape(1, DN_DV))


def _dn_bwd(proj, o, dy, states, tinv, conv_w, alog, dtb, norm_w, seq):
    rows = proj.shape[0]
    nc = rows // CHUNK

    def body(x_ref, halo_ref, gate_ref, ba_ref, cw_ref, al_ref, dt_ref, nw_ref,
             o_ref, dy_ref, ss_ref, t_ref,
             dp_ref, dcw_ref, dal_ref, ddt_ref, dnw_ref, ds_ref, nxt_ref):
        n = pl.program_id(0)

        @pl.when(n == 0)
        def _():
            ds_ref[...] = jnp.zeros_like(ds_ref)
            nxt_ref[...] = jnp.zeros_like(nxt_ref)
            dcw_ref[...] = jnp.zeros_like(dcw_ref)
            dal_ref[...] = jnp.zeros_like(dal_ref)
            ddt_ref[...] = jnp.zeros_like(ddt_ref)
            dnw_ref[...] = jnp.zeros_like(dnw_ref)

        f = _dn_front(nc - 1 - n, seq, x_ref, halo_ref, ba_ref, cw_ref, al_ref, dt_ref)
        ri, ci = f["ri"], f["ci"]
        strict = (ri > ci).astype(F32)
        nw = nw_ref[...]
        lane128 = _iota((1, LANES), 1)
        row128 = _iota((LANES, 1), 0)
        dgam_col = jnp.zeros((CHUNK, LANES), F32)
        dgam_row = jnp.zeros((LANES, CHUNK), F32)
        dbeta = jnp.zeros((CHUNK, LANES), F32)
        dnw = jnp.zeros((1, DN_DV), F32)
        hs = range(DN_HEADS)
        heads = [_dn_head(f, h) for h in hs]
        cols = [slice(h * DN_DV, (h + 1) * DN_DV) for h in hs]
        t_l = [t_ref[0, h] for h in hs]
        s_l = [ss_ref[0, h] for h in hs]
        ds_l = [ds_ref[h] for h in hs]
        kk_l = [_nt(hd["kn"], hd["kn"]) for hd in heads]
        p_l = [_nt(hd["qn"], hd["kn"]) for hd in heads]
        rhsw_l = [hd["kn"] * (hd["bc"] * hd["egam"]) for hd in heads]
        u_l = [_nn(t, hd["v"] * hd["bc"], hi=True) for t, hd in zip(t_l, heads)]
        w_l = [_nn(t, r, hi=True) for t, r in zip(t_l, rhsw_l)]
        vnew_l = [u - _nn(w, s) for u, w, s in zip(u_l, w_l, s_l)]
        gnb = [_gated_norm_bwd(dy_ref[:, c], o_ref[:, c], gate_ref[:, c], nw) for c in cols]
        do_l = [x[0] for x in gnb]
        for h in hs:
            dp_ref[:, DN_CONV_CH + h * DN_DV:DN_CONV_CH + (h + 1) * DN_DV] = gnb[h][1].astype(dp_ref.dtype)
            dnw = dnw + gnb[h][2]
        qg_l = [hd["qn"] * hd["egam"] for hd in heads]
        kd_l = [hd["kn"] * hd["ekd"] for hd in heads]
        dvnew_l = [_tn(p * hd["decay"], do) + _nn(kd, ds)
                   for p, hd, do, kd, ds in zip(p_l, heads, do_l, kd_l, ds_l)]
        m_l = [_nt(do, vn) for do, vn in zip(do_l, vnew_l)]
        dqg_l = [_nt(do, s) for do, s in zip(do_l, s_l)]
        dkd_l = [_nt(vn, ds) for vn, ds in zip(vnew_l, ds_l)]
        for h in hs:
            ds_ref[h] = (ds_l[h] * heads[h]["eglast"] + _tn(qg_l[h], do_l[h]) - _tn(w_l[h], dvnew_l[h]))
        dw_l = [-_nt(dvn, s) for dvn, s in zip(dvnew_l, s_l)]
        dru_l = [_tn(t, dvn, hi=True) for t, dvn in zip(t_l, dvnew_l)]
        drw_l = [_tn(t, dw_, hi=True) for t, dw_ in zip(t_l, dw_l)]
        da_l = [-(_nt(dru, u) + _nt(drw, w)) * strict for dru, u, drw, w in zip(dru_l, u_l, drw_l, w_l)]
        dp_l = [m * hd["decay"] for m, hd in zip(m_l, heads)]
        dkk_l = [da * (hd["bc"] * hd["decay"]) for da, hd in zip(da_l, heads)]
        dqn_l = [dqg * hd["egam"] + _nn(dp, hd["kn"]) for dqg, hd, dp in zip(dqg_l, heads, dp_l)]
        dkn_l = [_tn(dp, hd["qn"]) + dkd * hd["ekd"] + drw * (hd["bc"] * hd["egam"])
                 + _nn(dkk, hd["kn"]) + _tn(dkk, hd["kn"])
                 for dp, hd, dkd, drw, dkk in zip(dp_l, heads, dkd_l, drw_l, dkk_l)]
        dq_parts, dk_parts, dv_parts = [], [], []
        for h in hs:
            hd = heads[h]
            kn, v, bc, egam, decay = hd["kn"], hd["v"], hd["bc"], hd["egam"], hd["decay"]
            t1 = jnp.sum(dkd_l[h] * kd_l[h], axis=1, keepdims=True)
            dglast = (jnp.sum(t1, axis=0, keepdims=True)
                      + jnp.sum(jnp.sum(ds_l[h] * s_l[h], axis=1, keepdims=True), axis=0, keepdims=True)
                      * hd["eglast"])
            e = (m_l[h] * p_l[h] + da_l[h] * (bc * kk_l[h])) * decay
            dgc = (jnp.sum(dqg_l[h] * qg_l[h], axis=1, keepdims=True) - t1
                   + jnp.sum(drw_l[h] * rhsw_l[h], axis=1, keepdims=True)
                   + jnp.sum(e, axis=1, keepdims=True)
                   + jnp.where(_iota((CHUNK, 1), 0) == CHUNK - 1, dglast, 0.0))
            dgr = -jnp.sum(e, axis=0, keepdims=True)
            dbc = (jnp.sum(dru_l[h] * v, axis=1, keepdims=True)
                   + jnp.sum(drw_l[h] * kn, axis=1, keepdims=True) * egam
                   + jnp.sum(da_l[h] * kk_l[h] * decay, axis=1, keepdims=True))
            dv_parts.append(dru_l[h] * bc)
            qh, dqn, dkn = hd["qh"], dqn_l[h], dkn_l[h]
            dq_parts.append(((DN_DK ** -0.5) * hd["rq"])
                            * (dqn - qh * jnp.sum(dqn * qh, axis=1, keepdims=True)))
            dk_parts.append(hd["rk"] * (dkn - kn * jnp.sum(dkn * kn, axis=1, keepdims=True)))
            dgam_col = dgam_col + dgc * (lane128 == DECAY_LANE + h).astype(F32)
            dbeta = dbeta + dbc * (lane128 == BETA_LANE + h).astype(F32)
            dgam_row = dgam_row + (row128 == DECAY_LANE + h).astype(F32) * dgr
        dnw_ref[...] += dnw
        dgam = dgam_col + _nt(f["eye"], dgam_row, hi=True)
        dg = _nn(f["upper"], dgam, hi=True)
        d_a = dg * (-f["eal"]) * jax.nn.sigmoid(f["z"]) * f["valid"]
        dal_ref[...] += jnp.sum(dg * f["g"], axis=0, keepdims=True)
        ddt_ref[...] += jnp.sum(d_a, axis=0, keepdims=True)
        d_b = dbeta * f["valid"] * f["sig"] * (1.0 - f["sig"])
        dp_ref[:, DN_CONV_CH + DN_V:] = (d_a + d_b).astype(dp_ref.dtype)
        dact = jnp.concatenate(dq_parts + dk_parts + dv_parts, axis=1)
        yc, sgc = f["yc"], f["sgc"]
        dyc = dact * (sgc * (1.0 + yc * (1.0 - sgc)))
        for k in range(CONV_K):
            dcw_ref[k] += jnp.sum(dyc * f["x_sh"][CONV_K - 1 - k], axis=0, keepdims=True)
        nxt = nxt_ref[...]
        dx = dyc * cw_ref[CONV_K - 1]
        for j in range(1, CONV_K):
            dx = dx + _shift_up(dyc, nxt, j) * cw_ref[CONV_K - 1 - j]
        nxt_ref[...] = dyc[0:8]
        dp_ref[:, :DN_CONV_CH] = (dx * f["valid"]).astype(dp_ref.dtype)

    rev = lambda n: nc - 1 - n
    row_v = pl.BlockSpec((CHUNK, DN_V), lambda n: (rev(n), 0))
    vec = pl.BlockSpec((1, LANES), lambda n: (0, 0))
    return pl.pallas_call(
        body, grid=(nc,),
        in_specs=_dn_in_specs(rev) + [
            row_v, row_v,
            pl.BlockSpec((1, DN_HEADS, DN_DK, DN_DV), lambda n: (rev(n), 0, 0, 0)),
            pl.BlockSpec((1, DN_HEADS, CHUNK, CHUNK), lambda n: (rev(n), 0, 0, 0))],
        out_specs=[pl.BlockSpec((CHUNK, DN_IN_PAD), lambda n: (rev(n), 0)),
                   pl.BlockSpec((CONV_K, 1, DN_CONV_CH), lambda n: (0, 0, 0)), vec, vec,
                   pl.BlockSpec((1, DN_DV), lambda n: (0, 0))],
        out_shape=[jax.ShapeDtypeStruct((rows, DN_IN_PAD), BF16),
                   jax.ShapeDtypeStruct((CONV_K, 1, DN_CONV_CH), F32),
                   jax.ShapeDtypeStruct((1, LANES), F32), jax.ShapeDtypeStruct((1, LANES), F32),
                   jax.ShapeDtypeStruct((1, DN_DV), F32)],
        scratch_shapes=[pltpu.VMEM((DN_HEADS, DN_DK, DN_DV), F32), pltpu.VMEM((8, DN_CONV_CH), F32)],
        name="dn_bwd", compiler_params=_params("arbitrary"))(
            proj, proj, proj, proj, conv_w, alog, dtb, norm_w.reshape(1, DN_DV), o, dy, states, tinv)


def _train_step(x, tgt, wts, sh, idx):
    seq = x.shape[0]
    rows = -(-(seq + CHUNK) // ROW_ALIGN) * ROW_ALIGN
    tail = rows - seq - CHUNK
    h0 = jnp.concatenate([jnp.zeros((PAD, D_MODEL), F32), wts["meta_tokens"].astype(F32), x,
                          jnp.zeros((tail, D_MODEL), F32)], axis=0)
    tgt_p = jnp.concatenate([jnp.zeros((CHUNK, D_MODEL), F32), tgt, jnp.zeros((tail, D_MODEL), F32)],
                            axis=0)
    cos, sin = _rope_tables(rows)
    consts = _ret_consts()
    conv_w = wts["dn_conv_w"].reshape(CONV_K, 1, DN_CONV_CH)
    lane_pad = LANES - 2 * DN_HEADS
    alog = jnp.pad(wts["dn_a_log"].reshape(1, DN_HEADS), ((0, 0), (DECAY_LANE, lane_pad)))
    dtb = jnp.pad(wts["dn_dt_bias"].reshape(1, DN_HEADS), ((0, 0), (DECAY_LANE, lane_pad)))
    g = {}

    wts = dict(wts)
    (got,) = _gather_weights([sh["ret_w_in"]])
    wts["ret_w_in"] = got.reshape(N_SHARD, D_MODEL, -1)
    hn0 = _rms_fwd(h0, wts["mix_norm_w"][0], "rms_mix0")
    proj0, got = _mm_cols(hn0, wts["ret_w_in"], "ret_in",
                          ride=_Ride("gather", [sh["ret_w_out"], sh["ffn_w_gate"]]))
    wts["ret_w_out"] = got[0].reshape(-1, D_MODEL)
    wts["ffn_w_gate"] = got[1]
    (o0, y0, st0), got = _ret_fwd(
        proj0, cos, sin, consts, wts["ret_gn_w"], seq,
        ride=_Ride("gather", [sh["ffn_w_up"], sh["ffn_w_down"], sh["dn_w_in"], sh["dn_w_out"]]))
    wts["ffn_w_up"], wts["ffn_w_down"] = got[0], got[1]
    n_dn = sh["dn_w_in"].shape[-1]
    wts["dn_w_in"] = jnp.pad(_join_cols(got[2].reshape(N_SHARD, D_MODEL, n_dn)),
                             ((0, 0), (0, DN_IN_PAD - N_SHARD * n_dn)))
    wts["dn_w_out"] = got[3].reshape(-1, D_MODEL)
    h1 = _mm(y0, wts["ret_w_out"], mode="nn", name="ret_out", resid=h0)
    hn1 = _rms_fwd(h1, wts["ffn_norm_w"][0], "rms_ffn0")
    g0, u0, act0 = _ffn_up(hn1, wts["ffn_w_gate"], wts["ffn_w_up"], 0, "ffn_up0")
    h2 = _ffn_down(act0, wts["ffn_w_down"], h1, 0, "ffn_down0")
    hn2 = _rms_fwd(h2, wts["mix_norm_w"][1], "rms_mix1")
    proj1 = _mm(hn2, wts["dn_w_in"], mode="nn", name="dn_in")
    o1, y1, st1, tinv = _dn_fwd(proj1, conv_w, alog, dtb, wts["dn_norm_w"], seq)
    h3 = _mm(y1, wts["dn_w_out"], mode="nn", name="dn_out", resid=h2)
    hn3 = _rms_fwd(h3, wts["ffn_norm_w"][1], "rms_ffn1")
    g1, u1, act1 = _ffn_up(hn3, wts["ffn_w_gate"], wts["ffn_w_up"], 1, "ffn_up1")
    h4 = _ffn_down(act1, wts["ffn_w_down"], h3, 1, "ffn_down1")

    dh4, g["final_norm_w"], loss = _final_loss(h4, wts["final_norm_w"], tgt_p, seq, "final_loss")

    layers = wts["ffn_w_gate"].shape[1]

    def ffn_bwd(dh_out, h_mid, hn, gg, uu, act, layer, prev):
        tag = str(layer)
        dg, du = _ffn_down_bwd(dh_out, wts["ffn_w_down"], gg, uu, layer, "ffn_down_bwd" + tag)
        d_down = _ffn_wgrad(act, [dh_out], layer, layers, prev and prev[:1], True, "ffn_dwd" + tag)
        d_gu = _ffn_wgrad(hn, [dg, du], layer, layers, prev and prev[1:], False, "ffn_dwgu" + tag)
        dhn = _ffn_up_bwd(dg, du, wts["ffn_w_gate"], wts["ffn_w_up"], layer, "ffn_up_bwd" + tag)
        dh_mid, d_norm = _rms_bwd(dhn, h_mid, wts["ffn_norm_w"][layer], dh_out, "rms_ffn_bwd" + tag)
        return dh_mid, list(d_down) + list(d_gu), d_norm

    red = {}

    def rs_begin(names, grads, tag):
        gs = [gr.reshape((N_SHARD,) + sh[n].shape) for n, gr in zip(names, grads)]
        sib = _rs_pair(gs, "rs_pair" + tag)
        parts = [_rs_pair_add(gs[t], sib[t], idx, "rs_pair_add_" + n) for t, n in enumerate(names)]
        return gs, sib, parts

    def rs_end(names, begun, others, tag):
        gs, sib, _ = begun
        mine = [_rs_final_add(gs[t], sib[t], others[t], idx, "rs_final_add_" + n) for t, n in enumerate(names)]
        red.update(zip(names, _rs_share(mine, "rs_share" + tag)))

    dh3, ffn_grads, dfn1 = ffn_bwd(dh4, h3, hn3, g1, u1, act1, 1, None)
    dy1 = _mm(dh3, wts["dn_w_out"], mode="nt", name="dn_out_bwd")
    d_dn_out = _mm(y1, dh3, mode="tn", name="dn_dwo")
    dproj1, dcw, dal, ddt, g["dn_norm_w"] = _dn_bwd(proj1, o1, dy1, st1, tinv, conv_w, alog, dtb,
                                                    wts["dn_norm_w"], seq)
    d_dn_in = _mm(hn2, dproj1, mode="tn", name="dn_dwi")
    d_dn_in = jnp.stack([d_dn_in[:, j * n_dn:(j + 1) * n_dn] for j in range(N_SHARD)])
    group = ["dn_w_out", "dn_w_in"]
    begun = rs_begin(group, [d_dn_out, d_dn_in], "1")
    dhn2, others = _mm(dproj1, wts["dn_w_in"], mode="nt", name="dn_in_bwd", ride=_Ride("chips", begun[2]))
    rs_end(group, begun, others, "1")
    dh2, dmn1 = _rms_bwd(dhn2, h2, wts["mix_norm_w"][1], dh3, "rms_mix_bwd1")
    g["dn_conv_w"] = dcw.reshape(CONV_K, DN_CONV_CH)
    g["dn_a_log"] = dal[0, DECAY_LANE:DECAY_LANE + DN_HEADS]
    g["dn_dt_bias"] = ddt[0, DECAY_LANE:DECAY_LANE + DN_HEADS]

    dh1, ffn_grads, dfn0 = ffn_bwd(dh2, h1, hn1, g0, u0, act0, 0, ffn_grads)
    dy0 = _mm(dh1, wts["ret_w_out"], mode="nt", name="ret_out_bwd")
    d_ret_out = _mm(y0, dh1, mode="tn", name="ret_dwo")
    group = ["ffn_w_down", "ffn_w_gate", "ffn_w_up", "ret_w_out"]
    begun = rs_begin(group, list(ffn_grads) + [d_ret_out], "2")
    (dproj0, g["ret_gn_w"]), others = _ret_bwd(proj0, o0, dy0, st0, cos, sin, consts, wts["ret_gn_w"], seq,
                                               ride=_Ride("chips", begun[2]))
    rs_end(group, begun, others, "2")
    d_ret_in = _mm_cols_grad(hn0, dproj0, "ret_dwi")
    begun = rs_begin(["ret_w_in"], [d_ret_in], "3")
    dhn0, others = _mm_cols_t(dproj0, wts["ret_w_in"], "ret_in_bwd", ride=_Ride("chips", begun[2]))
    rs_end(["ret_w_in"], begun, others, "3")
    dh0, dmn0 = _rms_bwd(dhn0, h0, wts["mix_norm_w"][0], dh1, "rms_mix_bwd0")

    g["ffn_norm_w"] = jnp.concatenate([dfn0, dfn1], axis=0)
    g["mix_norm_w"] = jnp.concatenate([dmn0, dmn1], axis=0)
    g["meta_tokens"] = dh0[PAD:CHUNK]
    g["final_norm_w"] = g["final_norm_w"].reshape(D_MODEL)
    g["ret_gn_w"] = g["ret_gn_w"].reshape(RET_DV)
    g["dn_norm_w"] = g["dn_norm_w"].reshape(DN_DV)
    return loss, dh0, g, red


def _mesh_pos():
    return lax.axis_index("x"), lax.axis_index("y"), lax.axis_index("c")


def _other_chips(x, y):
    return [(1 - x, y), (x, 1 - y), (1 - x, 1 - y)]


def _remote(src, dst, send_sem, recv_sem, to):
    return pltpu.make_async_remote_copy(src_ref=src, dst_ref=dst, send_sem=send_sem, recv_sem=recv_sem,
                                        device_id=to, device_id_type=MESH)


GATHER_COPIES = 7


def _gather_weights(shards):
    ride = _Ride("gather", shards)

    def body(*refs):
        nt = len(shards)
        for phase in range(3):
            _gather_phase(phase, refs[:nt], refs[nt:2 * nt], *refs[2 * nt:])

    return pl.pallas_call(body, out_shape=ride.out_shape, in_specs=ride.in_specs, out_specs=ride.out_specs,
                          scratch_shapes=ride.scratch, name="gather_weights")(*shards)


def _gather_phase(phase, ins, outs, send_sems, recv_sems):
    x, y, c = _mesh_pos()
    me = 2 * x + y
    chips = _other_chips(x, y)
    sibling = (x, y, 1 - c)

    def cp(t, k, src, dst, to):
        i = GATHER_COPIES * t + k
        return _remote(src, dst, send_sems.at[i], recv_sems.at[i], to)

    for t in range(len(ins)):
        own = cp(t, 0, ins[t], outs[t].at[me], sibling)
        if phase == 0:
            own.start()
        if phase == 2:
            own.wait()
        for k, (px, py) in enumerate(chips):
            landed = outs[t].at[2 * px + py, c]
            theirs = outs[t].at[2 * px + py, 1 - c]
            to_chip = cp(t, 1 + k, ins[t].at[c], outs[t].at[me, c], (px, py, c))
            if phase == 0:
                to_chip.start()
            if phase == 1:
                cp(t, 1 + k, ins[t].at[c], landed, (px, py, c)).wait_recv()
                cp(t, 4 + k, landed, landed, sibling).start()
            if phase == 2:
                to_chip.wait_send()
                cp(t, 4 + k, landed, landed, sibling).wait_send()
                cp(t, 4 + k, theirs, theirs, sibling).wait_recv()


def _chips_phase(phase, ins, outs, send_sems, recv_sems):
    x, y, c = _mesh_pos()
    for t in range(len(ins)):
        for k, (px, py) in enumerate(_other_chips(x, y)):
            cp = _remote(ins[t].at[2 * px + py], outs[t].at[k], send_sems.at[3 * t + k], recv_sems.at[3 * t + k],
                         (px, py, c))
            if phase == 0:
                cp.start()
            if phase == 2:
                cp.wait()


class _Ride:
    def __init__(self, kind, arrays):
        self.kind, self.arrays = kind, list(arrays)
        nt = len(self.arrays)
        if kind == "gather":
            self.phase_fn, n_sem = _gather_phase, GATHER_COPIES * nt
            self.out_shape = [jax.ShapeDtypeStruct((N_SHARD,) + a.shape, a.dtype) for a in self.arrays]
        else:
            self.phase_fn, n_sem = _chips_phase, 3 * nt
            self.out_shape = [jax.ShapeDtypeStruct((3,) + a.shape[1:], a.dtype) for a in self.arrays]
        self.in_specs, self.out_specs = [ANY] * nt, [ANY] * nt
        self.scratch = [pltpu.SemaphoreType.DMA((n_sem,)), pltpu.SemaphoreType.DMA((n_sem,))]

    def emit(self, step, nsteps, ins, outs, send_sems, recv_sems, before):
        mid = max(0, min((3 * nsteps) // 4, nsteps - 2))
        todo = [(0, 0), (1, mid)] if before else [(2, nsteps - 1)]
        for phase, at in todo:
            if phase == 1 and self.kind != "gather":
                continue

            @pl.when(step == at)
            def _(phase=phase):
                self.phase_fn(phase, ins, outs, send_sems, recv_sems)


def _gather_small(blk):
    r, wd = blk.shape

    def body(b_ref, out_ref, send_sems, recv_sems):
        x, y, c = _mesh_pos()
        chips = _other_chips(x, y)
        out_ref[2 * x + y] = b_ref[...]
        sends = [_remote(b_ref, out_ref.at[2 * x + y], send_sems.at[k], recv_sems.at[k], (px, py, c))
                 for k, (px, py) in enumerate(chips)]
        for cp in sends:
            cp.start()
        for k, (px, py) in enumerate(chips):
            _remote(b_ref, out_ref.at[2 * px + py], send_sems.at[k], recv_sems.at[k], (px, py, c)).wait_recv()
        for cp in sends:
            cp.wait_send()

    return pl.pallas_call(
        body, out_shape=jax.ShapeDtypeStruct((4, r, wd), blk.dtype), in_specs=[VMEM_SPEC], out_specs=VMEM_SPEC,
        scratch_shapes=[pltpu.SemaphoreType.DMA((3,)), pltpu.SemaphoreType.DMA((3,))],
        name="gather_small")(blk)


def _allreduce_small(blk):
    r, wd = blk.shape
    rels = [(dx, dy, dc) for dx in (0, 1) for dy in (0, 1) for dc in (0, 1) if dx or dy or dc]

    def body(b_ref, out_ref, buf_ref, send_sems, recv_sems):
        x, y, c = _mesh_pos()

        def peer(rel):
            dx, dy, dc = rel
            return (1 - x if dx else x, 1 - y if dy else y, 1 - c if dc else c)

        me = 4 * x + 2 * y + c
        buf_ref[me] = b_ref[...]
        sends = [_remote(b_ref, buf_ref.at[me], send_sems.at[k], recv_sems.at[k], peer(rel))
                 for k, rel in enumerate(rels)]
        for cp in sends:
            cp.start()
        for k, rel in enumerate(rels):
            px, py, pc = peer(rel)
            _remote(b_ref, buf_ref.at[4 * px + 2 * py + pc], send_sems.at[k], recv_sems.at[k],
                    (px, py, pc)).wait_recv()
        for cp in sends:
            cp.wait_send()
        acc = buf_ref[0]
        for d in range(1, 8):
            acc = acc + buf_ref[d]
        out_ref[...] = acc

    return pl.pallas_call(
        body, out_shape=jax.ShapeDtypeStruct((r, wd), blk.dtype), in_specs=[VMEM_SPEC], out_specs=VMEM_SPEC,
        scratch_shapes=[pltpu.VMEM((8, r, wd), blk.dtype), pltpu.SemaphoreType.DMA((7,)),
                        pltpu.SemaphoreType.DMA((7,))],
        name="allreduce_small")(blk)


def _rs_pair(gs, name):
    nt = len(gs)

    def body(*refs):
        ins, outs = refs[:nt], refs[nt:2 * nt]
        send_sems, recv_sems = refs[2 * nt:]
        x, y, c = _mesh_pos()
        cps = [_remote(ins[t].at[:, 1 - c], outs[t], send_sems.at[t], recv_sems.at[t], (x, y, 1 - c))
               for t in range(nt)]
        for cp in cps:
            cp.start()
        for cp in cps:
            cp.wait()

    return pl.pallas_call(
        body, out_shape=[jax.ShapeDtypeStruct(g.shape[:1] + g.shape[2:], g.dtype) for g in gs],
        in_specs=[ANY] * nt, out_specs=[ANY] * nt,
        scratch_shapes=[pltpu.SemaphoreType.DMA((nt,)), pltpu.SemaphoreType.DMA((nt,))], name=name)(*gs)


def _rs_tile(a, b):
    return _div_tile(a, 512 if b <= 1024 else 256, 16)


def _rs_pair_add(g, a, idx, name):
    _, _, rows, cols = g.shape
    tr = _rs_tile(rows, cols)

    def body(s_ref, g_ref, a_ref, p_ref):
        p_ref[...] = (g_ref[...] + a_ref[...]).astype(p_ref.dtype)

    blk = pl.BlockSpec((None, tr, cols), lambda j, i, s: (j, i, 0))
    spec = pltpu.PrefetchScalarGridSpec(
        num_scalar_prefetch=1, grid=(N_SHARD, rows // tr),
        in_specs=[pl.BlockSpec((None, None, tr, cols), lambda j, i, s: (j, s[0], i, 0)), blk], out_specs=blk)
    return pl.pallas_call(
        body, grid_spec=spec, out_shape=jax.ShapeDtypeStruct((N_SHARD, rows, cols), BF16), name=name,
        compiler_params=_params("parallel", "parallel"))(idx, g, a)


def _rs_final_add(g, a, b, idx, name):
    _, _, rows, cols = g.shape
    tr = _rs_tile(rows, cols)

    def body(s_ref, g_ref, a_ref, b0_ref, b1_ref, b2_ref, f_ref):
        own = g_ref[...] + a_ref[...]
        f_ref[...] = ((own + b0_ref[...].astype(F32)) + b1_ref[...].astype(F32)) + b2_ref[...].astype(F32)

    def b_spec(k):
        return pl.BlockSpec((None, tr, cols), lambda i, s: (k, i, 0))

    spec = pltpu.PrefetchScalarGridSpec(
        num_scalar_prefetch=1, grid=(rows // tr,),
        in_specs=[pl.BlockSpec((None, None, tr, cols), lambda i, s: (s[1], s[0], i, 0)),
                  pl.BlockSpec((None, tr, cols), lambda i, s: (s[1], i, 0)), b_spec(0), b_spec(1), b_spec(2)],
        out_specs=pl.BlockSpec((None, tr, cols), lambda i, s: (s[0], i, 0)))
    return pl.pallas_call(
        body, grid_spec=spec, out_shape=jax.ShapeDtypeStruct((2, rows, cols), F32), name=name,
        compiler_params=_params("parallel"))(idx, g, a, b, b, b)


def _rs_share(fs, name):
    nt = len(fs)

    def body(*refs):
        outs = refs[nt:2 * nt]
        send_sems, recv_sems = refs[2 * nt:]
        x, y, c = _mesh_pos()
        cps = [_remote(outs[t].at[c], outs[t].at[c], send_sems.at[t], recv_sems.at[t], (x, y, 1 - c))
               for t in range(nt)]
        for cp in cps:
            cp.start()
        for cp in cps:
            cp.wait()

    return pl.pallas_call(
        body, out_shape=[jax.ShapeDtypeStruct(f.shape, f.dtype) for f in fs],
        in_specs=[ANY] * nt, out_specs=[ANY] * nt, input_output_aliases={t: t for t in range(nt)},
        scratch_shapes=[pltpu.SemaphoreType.DMA((nt,)), pltpu.SemaphoreType.DMA((nt,))], name=name)(*fs)


def _adamw(w, g, m, v, name):
    lead, rows, cols = w.shape
    tr = rows // 4 if rows % 32 == 0 else rows

    def body(w_ref, g_ref, m_ref, v_ref, go_ref, d_ref, mo_ref, vo_ref):
        gv = g_ref[...]
        go_ref[...] = gv
        mn = ADAM_B1 * m_ref[...] + (1.0 - ADAM_B1) * gv
        vn = ADAM_B2 * v_ref[...] + (1.0 - ADAM_B2) * (gv * gv)
        m_hat = mn / (1.0 - ADAM_B1 ** ADAM_STEP)
        v_hat = vn / (1.0 - ADAM_B2 ** ADAM_STEP)
        d_ref[...] = -ADAM_LR * (m_hat / (jnp.sqrt(v_hat) + ADAM_EPS) + ADAM_WD * w_ref[...])
        mo_ref[...] = mn
        vo_ref[...] = vn

    blk = pl.BlockSpec((None, tr, cols), lambda l, i: (l, i, 0))
    out = jax.ShapeDtypeStruct((lead, rows, cols), F32)
    return pl.pallas_call(
        body, grid=(lead, rows // tr), in_specs=[blk] * 4, out_specs=[blk] * 4, out_shape=[out] * 4, name=name,
        compiler_params=_params("parallel", "parallel"))(w, g, m, v)


BIG = ["ret_w_in", "ret_w_out", "dn_w_in", "dn_w_out", "ffn_w_gate", "ffn_w_up", "ffn_w_down"]
SMALL =["meta_tokens", "mix_norm_w", "ffn_norm_w", "ret_gn_w", "dn_conv_w", "dn_a_log", "dn_dt_bias",
         "dn_norm_w", "final_norm_w"]
SMALL_SHARDED = {"meta_tokens", "dn_conv_w", "dn_norm_w"}
ORDER = ["meta_tokens", "mix_norm_w", "ffn_norm_w", "ret_w_in", "ret_gn_w", "ret_w_out", "dn_w_in",
         "dn_conv_w", "dn_a_log", "dn_dt_bias", "dn_norm_w", "dn_w_out", "ffn_w_gate", "ffn_w_up",
         "ffn_w_down", "final_norm_w"]


def _halves(a):
    return a.reshape(2, -1, a.shape[-1])


def _pack_lanes(parts, align=8):
    flat = jnp.concatenate([p.reshape(-1) for p in parts])
    flat = jnp.pad(flat, (0, -flat.shape[0] % (align * LANES)))
    return flat.reshape(-1, LANES)


def _unpack(buf, shapes):
    lead = buf.shape[:-2]
    flat = buf.reshape(lead + (-1,))
    out, off = [], 0
    for shp in shapes:
        size = math.prod(shp)
        out.append(flat[..., off:off + size].reshape(lead + tuple(shp)))
        off += size
    return out


def _join_cols(shards):
    return jnp.concatenate([shards[j] for j in range(N_SHARD)], axis=-1)


def kernel(x, meta_tokens, mix_norm_w, ffn_norm_w, ret_w_in, ret_gn_w, ret_w_out, dn_w_in, dn_conv_w, dn_a_log, dn_dt_bias, dn_norm_w, dn_w_out, ffn_w_gate, ffn_w_up, ffn_w_down, final_norm_w, loss_target, m_meta_tokens, m_mix_norm_w, m_ffn_norm_w, m_ret_w_in, m_ret_gn_w, m_ret_w_out, m_dn_w_in, m_dn_conv_w, m_dn_a_log, m_dn_dt_bias, m_dn_norm_w, m_dn_w_out, m_ffn_w_gate, m_ffn_w_up, m_ffn_w_down, m_final_norm_w, v_meta_tokens, v_mix_norm_w, v_ffn_norm_w, v_ret_w_in, v_ret_gn_w, v_ret_w_out, v_dn_w_in, v_dn_conv_w, v_dn_a_log, v_dn_dt_bias, v_dn_norm_w, v_dn_w_out, v_ffn_w_gate, v_ffn_w_up, v_ffn_w_down, v_final_norm_w):
    w = dict(meta_tokens=meta_tokens, mix_norm_w=mix_norm_w, ffn_norm_w=ffn_norm_w, ret_w_in=ret_w_in,
             ret_gn_w=ret_gn_w, ret_w_out=ret_w_out, dn_w_in=dn_w_in, dn_conv_w=dn_conv_w, dn_a_log=dn_a_log,
             dn_dt_bias=dn_dt_bias, dn_norm_w=dn_norm_w, dn_w_out=dn_w_out, ffn_w_gate=ffn_w_gate,
             ffn_w_up=ffn_w_up, ffn_w_down=ffn_w_down, final_norm_w=final_norm_w)
    m = dict(meta_tokens=m_meta_tokens, mix_norm_w=m_mix_norm_w, ffn_norm_w=m_ffn_norm_w, ret_w_in=m_ret_w_in,
             ret_gn_w=m_ret_gn_w, ret_w_out=m_ret_w_out, dn_w_in=m_dn_w_in, dn_conv_w=m_dn_conv_w,
             dn_a_log=m_dn_a_log, dn_dt_bias=m_dn_dt_bias, dn_norm_w=m_dn_norm_w, dn_w_out=m_dn_w_out,
             ffn_w_gate=m_ffn_w_gate, ffn_w_up=m_ffn_w_up, ffn_w_down=m_ffn_w_down, final_norm_w=m_final_norm_w)
    v = dict(meta_tokens=v_meta_tokens, mix_norm_w=v_mix_norm_w, ffn_norm_w=v_ffn_norm_w, ret_w_in=v_ret_w_in,
             ret_gn_w=v_ret_gn_w, ret_w_out=v_ret_w_out, dn_w_in=v_dn_w_in, dn_conv_w=v_dn_conv_w,
             dn_a_log=v_dn_a_log, dn_dt_bias=v_dn_dt_bias, dn_norm_w=v_dn_norm_w, dn_w_out=v_dn_w_out,
             ffn_w_gate=v_ffn_w_gate, ffn_w_up=v_ffn_w_up, ffn_w_down=v_ffn_w_down, final_norm_w=v_final_norm_w)
    mx, my, mc = _mesh_pos()
    chip = 2 * mx + my

    sm_names = [n for n in SMALL if n in SMALL_SHARDED]
    sm_gathered = _unpack(_gather_small(_pack_lanes([w[n] for n in sm_names])), [w[n].shape for n in sm_names])
    full = {n: _join_cols(sm_gathered[i]) for i, n in enumerate(sm_names)}
    wts = {
        "meta_tokens": full["meta_tokens"], "mix_norm_w": mix_norm_w, "ffn_norm_w": ffn_norm_w,
        "ret_gn_w": ret_gn_w[0], "final_norm_w": final_norm_w, "dn_conv_w": full["dn_conv_w"][0],
        "dn_a_log": dn_a_log[0], "dn_dt_bias": dn_dt_bias[0], "dn_norm_w": full["dn_norm_w"][0],
    }
    idx = jnp.stack([mc, chip]).astype(jnp.int32)
    shards = {n: _halves(w[n].astype(MXU_DTYPE)) for n in BIG}
    loss_part, dh0, g, reduced = _train_step(x[0], loss_target[0], wts, shards, idx)
    seq = x.shape[1]
    grad_x = dh0[CHUNK:CHUNK + seq].reshape(x.shape)
    gsh = {}

    small_full_shapes = [g[n].shape for n in SMALL] + [(1,)]
    red = _unpack(_allreduce_small(_pack_lanes([g[n] for n in SMALL] + [loss_part[0, :1]])), small_full_shapes)
    loss = red[-1][0]
    for i, n in enumerate(SMALL):
        gn = red[i]
        if n in SMALL_SHARDED:
            width = w[n].shape[-1]
            gn = lax.dynamic_slice_in_dim(gn, chip * width, width, axis=gn.ndim - 1)
        gsh[n] = gn.reshape(w[n].shape)

    delta, new_m, new_v = {}, {}, {}
    for n in BIG:
        gsh[n], delta[n], new_m[n], new_v[n] = _adamw(w[n], reduced[n].reshape(w[n].shape), m[n], v[n],
                                                      "adamw_" + n)
    sm_local_shapes = [w[n].shape for n in SMALL]
    _, d_, m_, v_ = _adamw(*[_pack_lanes([t[n] for n in SMALL])[None] for t in (w, gsh, m, v)], "adamw_small")
    d_, m_, v_ = d_[0], m_[0], v_[0]
    for n, dd, mm, vv in zip(SMALL, _unpack(d_, sm_local_shapes), _unpack(m_, sm_local_shapes),
                             _unpack(v_, sm_local_shapes)):
        delta[n], new_m[n], new_v[n] = dd, mm, vv

    return (loss, grad_x, *[gsh[n] for n in ORDER], *[delta[n] for n in ORDER],
            *[new_m[n] for n in ORDER], *[new_v[n] for n in ORDER])
```

```python
import functools
import math

import jax
import jax.numpy as jnp
from jax import lax
from jax.experimental import pallas as pl
from jax.experimental.pallas import tpu as pltpu

F32 = jnp.float32
BF16 = jnp.bfloat16
MXU_DTYPE = BF16

D_MODEL = 1024
N_META = 16
CHUNK = 64
PAD = CHUNK - N_META
RMS_EPS = 1e-6
RET_HEADS, RET_DK, RET_DV = 4, 256, 512
RET_QK, RET_V = RET_HEADS * RET_DK, RET_HEADS * RET_DV
RET_IN = 2 * RET_QK + 2 * RET_V
ROPE_BASE = 10000.0
DN_HEADS, DN_DK, DN_DV = 8, 128, 256
DN_QK, DN_V = DN_HEADS * DN_DK, DN_HEADS * DN_DV
DN_CONV_CH = 2 * DN_QK + DN_V
DN_IN = DN_CONV_CH + DN_V + 2 * DN_HEADS
LANES = 128
DN_IN_PAD = DN_CONV_CH + DN_V + LANES
CONV_K = 4
FFN_HIDDEN = 2816
ADAM_LR, ADAM_B1, ADAM_B2, ADAM_EPS, ADAM_WD, ADAM_STEP = 0.001, 0.9, 0.999, 1e-08, 0.01, 10

ROW_ALIGN = 256
VMEM_LIMIT = 56 * 1024 * 1024
MESH = pl.DeviceIdType.MESH
ANY = pl.BlockSpec(memory_space=pl.ANY)
VMEM_SPEC = pl.BlockSpec(memory_space=pltpu.VMEM)
_HI = lax.Precision.HIGHEST


def _params(*sem):
    return pltpu.CompilerParams(dimension_semantics=sem, vmem_limit_bytes=VMEM_LIMIT)


def _dg(a, b, ca, cb, hi):
    dims = (((ca,), (cb,)), ((), ()))

    def dot(p, q):
        return lax.dot_general(p, q, dims, preferred_element_type=F32)

    if not hi:
        return dot(a.astype(MXU_DTYPE), b.astype(MXU_DTYPE))
    if MXU_DTYPE == F32:
        return lax.dot_general(a, b, dims, precision=_HI, preferred_element_type=F32)
    a_hi, b_hi = a.astype(MXU_DTYPE), b.astype(MXU_DTYPE)
    a_lo = (a - a_hi.astype(F32)).astype(MXU_DTYPE)
    b_lo = (b - b_hi.astype(F32)).astype(MXU_DTYPE)
    return dot(a_hi, b_hi) + (dot(a_hi, b_lo) + dot(a_lo, b_hi))


def _nn(a, b, hi=False):
    return _dg(a, b, 1, 0, hi)


def _nt(a, b, hi=False):
    return _dg(a, b, 1, 1, hi)


def _tn(a, b, hi=False):
    return _dg(a, b, 0, 0, hi)


def _iota(shape, dim):
    return lax.broadcasted_iota(jnp.int32, shape, dim)


def _valid_rows(first_row, rows, seq):
    r = first_row + _iota((rows, 1), 0)
    return ((r >= PAD) & (r < CHUNK + seq)).astype(F32)


def _rope(t, cs, sn):
    half = t.shape[-1] // 2
    t1, t2 = t[:, :half], t[:, half:]
    return jnp.concatenate([t1 * cs - t2 * sn, t1 * sn + t2 * cs], axis=1)


def _rope_bwd(d, cs, sn):
    half = d.shape[-1] // 2
    d1, d2 = d[:, :half], d[:, half:]
    return jnp.concatenate([d1 * cs + d2 * sn, d2 * cs - d1 * sn], axis=1)


def _col(x, idx):
    oh = (_iota((1, x.shape[1]), 1) == idx).astype(F32)
    return jnp.sum(x * oh, axis=1, keepdims=True)


def _row(x, idx):
    oh = (_iota((x.shape[0], 1), 0) == idx).astype(F32)
    return jnp.sum(x * oh, axis=0, keepdims=True)


def _shift_down(x, halo8, k):
    xr = pltpu.roll(x, k, 0)
    hr = pltpu.roll(halo8, k, 0)
    first = jnp.where(_iota((8, 1), 0) < k, hr, xr[0:8])
    return jnp.concatenate([first, xr[8:]], axis=0)


def _shift_up(x, next8, j):
    rows = x.shape[0]
    xr = pltpu.roll(x, rows - j, 0)
    nr = pltpu.roll(next8, 8 - j, 0)
    last = jnp.where(_iota((8, 1), 0) >= 8 - j, nr, xr[rows - 8:])
    return jnp.concatenate([xr[:rows - 8], last], axis=0)


def _gated_norm(o, gate, w):
    r = lax.rsqrt(jnp.mean(o * o, axis=-1, keepdims=True) + RMS_EPS)
    return o * r * w * (gate * jax.nn.sigmoid(gate))


def _gated_norm_bwd(dy, o, gate, w):
    r = lax.rsqrt(jnp.mean(o * o, axis=-1, keepdims=True) + RMS_EPS)
    nrm = o * r
    sg = jax.nn.sigmoid(gate)
    sl = gate * sg
    dgate = dy * nrm * w * (sg * (1.0 + gate * (1.0 - sg)))
    dn = dy * w * sl
    dw = jnp.sum(dy * nrm * sl, axis=0, keepdims=True)
    do = r * (dn - nrm * jnp.mean(dn * nrm, axis=-1, keepdims=True))
    return do, dgate, dw


def _softplus(z):
    return jnp.maximum(z, 0.0) + jnp.log(1.0 + jnp.exp(-jnp.abs(z)))


def _row_tile(rows, cap=768):
    for t in (768, 512, 256, 128, 64, 32, 16, 8):
        if t <= cap and rows % t == 0:
            return t
    return rows


TILE_BUDGET = 44 * 1024 * 1024


def _fit_rows(rows, row_bytes, fixed_bytes, value_row_bytes):
    best = None
    for t in range(LANES, rows + 1, LANES):
        if rows % t == 0 and 2 * (row_bytes * t + fixed_bytes) + value_row_bytes * t <= TILE_BUDGET:
            best = t
    return best or _row_tile(rows, 256)


def _div_tile(n, cap, mult):
    best = None
    for t in range(mult, min(cap, n) + 1, mult):
        if n % t == 0:
            best = t
    return best or n


def _col_tile(cols, cap=1536):
    best = None
    for t in range(LANES, min(cap, cols) + 1, LANES):
        if cols % t == 0:
            best = t
    return best or cols


def _rms_fwd(h, w, name):
    rows, d = h.shape
    tm = _row_tile(rows)

    def body(h_ref, w_ref, o_ref):
        x = h_ref[...]
        r = lax.rsqrt(jnp.mean(x * x, axis=-1, keepdims=True) + RMS_EPS)
        o_ref[...] = (x * r * w_ref[...]).astype(o_ref.dtype)

    return pl.pallas_call(
        body, grid=(rows // tm,),
        in_specs=[pl.BlockSpec((tm, d), lambda i: (i, 0)), pl.BlockSpec((1, d), lambda i: (0, 0))],
        out_specs=pl.BlockSpec((tm, d), lambda i: (i, 0)),
        out_shape=jax.ShapeDtypeStruct((rows, d), BF16), name=name,
        compiler_params=_params("parallel"))(h, w.reshape(1, d))


def _gmm_rms(name, grid, args, in_specs, row_spec, fn, h, w, resid, row_axis, red_axis=None, ride=None):
    m, d = h.shape
    n_in = len(args)
    vec = pl.BlockSpec((1, d), lambda *g: (0, 0))

    def body(*refs):
        ins = refs[:n_in]
        h_ref, w_ref, r_ref, dh_ref, dw_ref = refs[n_in:]
        part = fn(*ins)
        row = pl.program_id(row_axis)

        def finish(dy):
            x = h_ref[...]
            r = lax.rsqrt(jnp.mean(x * x, axis=-1, keepdims=True) + RMS_EPS)
            xh = x * r
            dxh = dy * w_ref[...]
            dh_ref[...] = r_ref[...] + r * (dxh - xh * jnp.mean(dxh * xh, axis=-1, keepdims=True))
            dwp = jnp.sum(dy * xh, axis=0, keepdims=True)

            @pl.when(row == 0)
            def _():
                dw_ref[...] = dwp

            @pl.when(row > 0)
            def _():
                dw_ref[...] += dwp

        if red_axis is None:
            finish(part)
            return
        k = pl.program_id(red_axis)

        @pl.when(k == 0)
        def _():
            dh_ref[...] = part

        @pl.when(k > 0)
        def _():
            dh_ref[...] += part

        @pl.when(k == grid[red_axis] - 1)
        def _():
            finish(dh_ref[...])

    res, rode = _pcall(body, list(args) + [h, w.reshape(1, d), resid], grid=grid,
                       in_specs=list(in_specs) + [row_spec, vec, row_spec], out_specs=[row_spec, vec],
                       out_shape=[jax.ShapeDtypeStruct((m, d), F32), jax.ShapeDtypeStruct((1, d), F32)],
                       name=name, sem=("arbitrary",) * len(grid), ride=ride)
    return res if ride is None else (res, rode)


def _final_loss(h, w, tgt, seq, name):
    rows, d = h.shape
    tm = _row_tile(rows)

    def body(h_ref, w_ref, t_ref, dh_ref, dw_ref, loss_ref):
        i = pl.program_id(0)
        r_idx = i * tm + _iota((tm, 1), 0)
        m = ((r_idx >= CHUNK) & (r_idx < CHUNK + seq)).astype(F32)
        x = h_ref[...]
        wv = w_ref[...]
        r = lax.rsqrt(jnp.mean(x * x, axis=-1, keepdims=True) + RMS_EPS)
        xh = x * r
        err = (xh * wv - t_ref[...]) * m
        lpart = 0.5 * jnp.sum(jnp.mean(err * err, axis=-1, keepdims=True), axis=0, keepdims=True)
        dyv = err * (1.0 / d)
        dxh = dyv * wv
        dh_ref[...] = r * (dxh - xh * jnp.mean(dxh * xh, axis=-1, keepdims=True))
        part = jnp.sum(dyv * xh, axis=0, keepdims=True)

        @pl.when(i == 0)
        def _():
            dw_ref[...] = part
            loss_ref[...] = jnp.broadcast_to(lpart, loss_ref.shape)

        @pl.when(i > 0)
        def _():
            dw_ref[...] += part
            loss_ref[...] += jnp.broadcast_to(lpart, loss_ref.shape)

    blk = pl.BlockSpec((tm, d), lambda i: (i, 0))
    vec = pl.BlockSpec((1, d), lambda i: (0, 0))
    return pl.pallas_call(
        body, grid=(rows // tm,), in_specs=[blk, vec, blk],
        out_specs=[blk, vec, pl.BlockSpec((1, LANES), lambda i: (0, 0))],
        out_shape=[jax.ShapeDtypeStruct((rows, d), F32), jax.ShapeDtypeStruct((1, d), F32),
                   jax.ShapeDtypeStruct((1, LANES), F32)],
        name=name, compiler_params=_params("arbitrary"))(h, w.reshape(1, d), tgt)


def _isz(x):
    return jnp.dtype(x.dtype).itemsize


def _mm(a, b, *, mode, name, out_dtype=F32, resid=None, col_cap=1536, ride=None):
    if mode == "tn":
        m, k = a.shape
        n = b.shape[1]
        tn = _col_tile(n, col_cap)
        tm = _fit_rows(m, k * _isz(a) + tn * _isz(b), (3 * k * tn * 4) // 2, 2 * (k + tn))

        def body_tn(a_ref, b_ref, o_ref):
            i = pl.program_id(1)
            part = _tn(a_ref[...], b_ref[...])

            @pl.when(i == 0)
            def _():
                o_ref[...] = part

            @pl.when(i > 0)
            def _():
                o_ref[...] += part

        return pl.pallas_call(
            body_tn, grid=(n // tn, m // tm),
            in_specs=[pl.BlockSpec((tm, k), lambda j, i: (i, 0)),
                      pl.BlockSpec((tm, tn), lambda j, i: (i, j))],
            out_specs=pl.BlockSpec((k, tn), lambda j, i: (0, j)),
            out_shape=jax.ShapeDtypeStruct((k, n), F32), name=name,
            compiler_params=_params("parallel", "arbitrary"))(a, b)

    m, ka = a.shape
    n = b.shape[1] if mode == "nn" else b.shape[0]
    has_resid = resid is not None
    tn = _col_tile(n, col_cap)
    tm = _fit_rows(m, ka * _isz(a) + tn * (jnp.dtype(out_dtype).itemsize + (4 if has_resid else 0)),
                   ka * tn * _isz(b), 2 * ka + 8 * tn)

    def body(*refs):
        if has_resid:
            a_ref, b_ref, r_ref, o_ref = refs
        else:
            a_ref, b_ref, o_ref = refs
        acc = _nn(a_ref[...], b_ref[...]) if mode == "nn" else _nt(a_ref[...], b_ref[...])
        if has_resid:
            acc = acc + r_ref[...]
        o_ref[...] = acc.astype(o_ref.dtype)

    b_spec = (pl.BlockSpec((b.shape[0], tn), lambda j, i: (0, j)) if mode == "nn"
              else pl.BlockSpec((tn, b.shape[1]), lambda j, i: (j, 0)))
    o_spec = pl.BlockSpec((tm, tn), lambda j, i: (i, j))
    in_specs = [pl.BlockSpec((tm, ka), lambda j, i: (i, 0)), b_spec]
    args = [a, b]
    if has_resid:
        in_specs.append(o_spec)
        args.append(resid)
    res, rode = _pcall(body, args, grid=(n // tn, m // tm), in_specs=in_specs, out_specs=[o_spec],
                       out_shape=[jax.ShapeDtypeStruct((m, n), out_dtype)], name=name,
                       sem=("parallel", "parallel"), ride=ride)
    return res[0] if ride is None else (res[0], rode)


N_SHARD = 4


def _gmm(name, grid, args, in_specs, out_specs, out_shape, fn, red_axis=None, init_arg=None, aliases=None,
         ride=None):
    n_in = len(args)
    single = not isinstance(out_shape, (list, tuple))
    out_specs = [out_specs] if single else list(out_specs)
    out_shape = [out_shape] if single else list(out_shape)

    def body(*refs):
        _gmm_step(fn, refs[:n_in], refs[n_in:], red_axis, init_arg)

    sem = tuple("arbitrary" if ax == red_axis else "parallel" for ax in range(len(grid)))
    res, rode = _pcall(body, args, grid=grid, in_specs=in_specs, out_specs=out_specs, out_shape=out_shape,
                       name=name, sem=sem, aliases=aliases, ride=ride)
    ours = res[0] if single else res
    return ours if ride is None else (ours, rode)


def _gmm_step(fn, ins, outs, red_axis, init_arg):
    parts = fn(*ins)
    if red_axis is None:
        for o_ref, p in zip(outs, parts):
            o_ref[...] = p.astype(o_ref.dtype)
        return
    k = pl.program_id(red_axis)

    @pl.when(k == 0)
    def _():
        for idx, (o_ref, p) in enumerate(zip(outs, parts)):
            o_ref[...] = p + ins[init_arg][...] if (idx == 0 and init_arg is not None) else p

    @pl.when(k > 0)
    def _():
        for o_ref, p in zip(outs, parts):
            o_ref[...] += p


def _ride_body(ride, grid, n_in, n_out, n_scratch, body):
    n_rin, n_rout = len(ride.arrays), len(ride.out_shape)
    nsteps = math.prod(grid)

    def wrapped(*refs):
        ins = refs[:n_in]
        r_ins = refs[n_in:n_in + n_rin]
        o0 = n_in + n_rin
        outs = refs[o0:o0 + n_out]
        r_outs = refs[o0 + n_out:o0 + n_out + n_rout]
        s0 = o0 + n_out + n_rout
        scratch = refs[s0:s0 + n_scratch]
        send_sems, recv_sems = refs[-2:]
        step = pl.program_id(0)
        for ax in range(1, len(grid)):
            step = step * grid[ax] + pl.program_id(ax)
        ride.emit(step, nsteps, r_ins, r_outs, send_sems, recv_sems, before=True)
        body(*ins, *outs, *scratch)
        ride.emit(step, nsteps, r_ins, r_outs, send_sems, recv_sems, before=False)

    return wrapped


def _pcall(body, args, *, grid, in_specs, out_specs, out_shape, name, sem, scratch=(), aliases=None, ride=None):
    if ride is None:
        res = pl.pallas_call(body, grid=grid, in_specs=list(in_specs), out_specs=list(out_specs),
                             out_shape=list(out_shape), scratch_shapes=list(scratch), name=name,
                             input_output_aliases=aliases or {}, compiler_params=_params(*sem))(*args)
        return res, None
    n_in, n_out = len(args), len(out_shape)
    res = pl.pallas_call(
        _ride_body(ride, grid, n_in, n_out, len(scratch), body), grid=grid,
        in_specs=list(in_specs) + ride.in_specs, out_specs=list(out_specs) + ride.out_specs,
        out_shape=list(out_shape) + ride.out_shape, scratch_shapes=list(scratch) + ride.scratch, name=name,
        input_output_aliases=aliases or {},
        compiler_params=_params(*(("arbitrary",) * len(grid))))(*args, *ride.arrays)
    return res[:n_out], res[n_out:]


def _mm_cols(a, ws, name, ride=None):
    m, k = a.shape
    n = ws.shape[2]
    tm = _fit_rows(m, k * _isz(a) + n * 4, k * n * _isz(ws), 4 * n)
    return _gmm(name, (N_SHARD, m // tm), [a, ws],
                [pl.BlockSpec((tm, k), lambda j, i: (i, 0)), pl.BlockSpec((None, k, n), lambda j, i: (j, 0, 0))],
                pl.BlockSpec((tm, n), lambda j, i: (i, j)), jax.ShapeDtypeStruct((m, N_SHARD * n), F32),
                lambda a_ref, w_ref: (_nn(a_ref[...], w_ref[...]),), ride=ride)


def _mm_cols_t_rms(d, ws, h, w, resid, name, ride=None):
    m = d.shape[0]
    _, k, n = ws.shape
    tm = _fit_rows(m, n * _isz(d) + 3 * k * 4, k * n * _isz(ws), 16 * k)
    return _gmm_rms(name, (m // tm, N_SHARD), [d, ws],
                    [pl.BlockSpec((tm, n), lambda i, j: (i, j)), pl.BlockSpec((None, k, n), lambda i, j: (j, 0, 0))],
                    pl.BlockSpec((tm, k), lambda i, j: (i, 0)),
                    lambda d_ref, w_ref: _nt(d_ref[...], w_ref[...]), h, w, resid, 0, red_axis=1, ride=ride)


def _mm_nt_rms(a, b, h, w, resid, name, ride=None):
    m, n = a.shape
    k = b.shape[0]
    tm = _fit_rows(m, n * _isz(a) + 3 * k * 4, k * n * _isz(b), 16 * k)
    return _gmm_rms(name, (m // tm,), [a, b],
                    [pl.BlockSpec((tm, n), lambda i: (i, 0)), pl.BlockSpec((k, n), lambda i: (0, 0))],
                    pl.BlockSpec((tm, k), lambda i: (i, 0)),
                    lambda a_ref, b_ref: _nt(a_ref[...], b_ref[...]), h, w, resid, 0, ride=ride)


def _mm_cols_grad(a, d, name):
    m, k = a.shape
    n = d.shape[1] // N_SHARD
    tm = _fit_rows(m, k * _isz(a) + n * _isz(d), (3 * k * n * 4) // 2, 2 * (k + n))
    return _gmm(name, (N_SHARD, m // tm), [a, d],
                [pl.BlockSpec((tm, k), lambda j, i: (i, 0)), pl.BlockSpec((tm, n), lambda j, i: (i, j))],
                pl.BlockSpec((None, k, n), lambda j, i: (j, 0, 0)), jax.ShapeDtypeStruct((N_SHARD, k, n), F32),
                lambda a_ref, d_ref: (_tn(a_ref[...], d_ref[...]),), red_axis=1)


def _ffn_up(hn, wg, wu, layer, name):
    m, k = hn.shape
    n = wg.shape[3]
    tm = _fit_rows(m, k * _isz(hn) + 3 * n * jnp.dtype(BF16).itemsize, 2 * k * n * _isz(wg), 16 * n)

    def fn(a_ref, wg_ref, wu_ref):
        a = a_ref[...]
        g = _nn(a, wg_ref[...])
        u = _nn(a, wu_ref[...])
        return g, u, g * jax.nn.sigmoid(g) * u

    w_spec = pl.BlockSpec((None, None, k, n), lambda j, i: (j, layer, 0, 0))
    o_spec = pl.BlockSpec((None, tm, n), lambda j, i: (j, i, 0))
    out = jax.ShapeDtypeStruct((N_SHARD, m, n), BF16)
    return _gmm(name, (N_SHARD, m // tm), [hn, wg, wu],
                [pl.BlockSpec((tm, k), lambda j, i: (i, 0)), w_spec, w_spec],
                [o_spec, o_spec, o_spec], [out, out, out], fn)


def _ffn_down(act, wd, resid, layer, name):
    _, m, n = act.shape
    d = wd.shape[3]
    tm = _fit_rows(m, N_SHARD * n * _isz(act) + 2 * d * 4, N_SHARD * n * d * _isz(wd), 8 * d)

    def fn(a_ref, w_ref, r_ref):
        acc = r_ref[...]
        for j in range(N_SHARD):
            acc = acc + _nn(a_ref[j], w_ref[j])
        return (acc,)

    row = pl.BlockSpec((tm, d), lambda i: (i, 0))
    return _gmm(name, (m // tm,), [act, wd, resid],
                [pl.BlockSpec((N_SHARD, tm, n), lambda i: (0, i, 0)),
                 pl.BlockSpec((N_SHARD, None, n, d), lambda i: (0, layer, 0, 0)), row],
                row, jax.ShapeDtypeStruct((m, d), F32), fn)


def _ffn_down_bwd(dh, wd, g, u, layer, name):
    m, d = dh.shape
    n = wd.shape[2]
    tm = _fit_rows(m, d * _isz(dh) + 4 * n * jnp.dtype(BF16).itemsize, n * d * _isz(wd), 2 * d + 24 * n)

    def fn(dh_ref, wd_ref, g_ref, u_ref):
        dact = _nt(dh_ref[...], wd_ref[...])
        gv = g_ref[...].astype(F32)
        uv = u_ref[...].astype(F32)
        sg = jax.nn.sigmoid(gv)
        return dact * uv * (sg * (1.0 + gv * (1.0 - sg))), dact * gv * sg

    o_spec = pl.BlockSpec((None, tm, n), lambda j, i: (j, i, 0))
    out = jax.ShapeDtypeStruct((N_SHARD, m, n), BF16)
    return _gmm(name, (N_SHARD, m // tm), [dh, wd, g, u],
                [pl.BlockSpec((tm, d), lambda j, i: (i, 0)),
                 pl.BlockSpec((None, None, n, d), lambda j, i: (j, layer, 0, 0)), o_spec, o_spec],
                [o_spec, o_spec], [out, out], fn)


def _ffn_up_bwd(dg, du, wg, wu, layer, h, w, resid, name):
    _, m, n = dg.shape
    k = wg.shape[2]
    tm = _fit_rows(m, 2 * N_SHARD * n * _isz(dg) + 3 * k * 4, 2 * N_SHARD * k * n * _isz(wg), 16 * k)

    def fn(dg_ref, du_ref, wg_ref, wu_ref):
        acc = _nt(dg_ref[0], wg_ref[0]) + _nt(du_ref[0], wu_ref[0])
        for j in range(1, N_SHARD):
            acc = acc + _nt(dg_ref[j], wg_ref[j]) + _nt(du_ref[j], wu_ref[j])
        return acc

    d_spec = pl.BlockSpec((N_SHARD, tm, n), lambda i: (0, i, 0))
    w_spec = pl.BlockSpec((N_SHARD, None, k, n), lambda i: (0, layer, 0, 0))
    return _gmm_rms(name, (m // tm,), [dg, du, wg, wu], [d_spec, d_spec, w_spec, w_spec],
                    pl.BlockSpec((tm, k), lambda i: (i, 0)), fn, h, w, resid, 0)


def _ffn_wgrad(lhs, rhs_list, layer, layers, prev, lhs_sharded, name):
    if lhs_sharded:
        _, m, k = lhs.shape
        n = rhs_list[0].shape[1]
    else:
        m, k = lhs.shape
        n = rhs_list[0].shape[2]
    n_out = len(rhs_list)
    tm = _fit_rows(m, k * _isz(lhs) + n_out * n * _isz(rhs_list[0]), (3 * n_out * k * n * 4) // 2,
                   2 * (k + n_out * n))
    sh = pl.BlockSpec((None, tm, k if lhs_sharded else n), lambda j, i: (j, i, 0))
    fl = pl.BlockSpec((tm, n if lhs_sharded else k), lambda j, i: (i, 0))
    n_out = len(rhs_list)
    args = [lhs] + list(rhs_list)
    in_specs = [sh if lhs_sharded else fl] + [fl if lhs_sharded else sh] * n_out
    aliases = None
    if prev is not None:
        aliases = {len(args) + t: t for t in range(n_out)}
        args = args + list(prev)
        in_specs = in_specs + [ANY] * n_out

    def fn(l_ref, *rest):
        lv = l_ref[...]
        return tuple(_tn(lv, r_ref[...]) for r_ref in rest[:n_out])

    o_spec = pl.BlockSpec((None, None, k, n), lambda j, i: (j, layer, 0, 0))
    out = jax.ShapeDtypeStruct((N_SHARD, layers, k, n), F32)
    return _gmm(name, (N_SHARD, m // tm), args, in_specs, [o_spec] * n_out, [out] * n_out, fn,
                red_axis=1, aliases=aliases)


def _ret_consts():
    log_gamma = jnp.log1p(-jnp.exp2(-5.0 - jnp.arange(RET_HEADS, dtype=F32)))
    idx = jnp.arange(CHUNK, dtype=F32)
    rel = idx[:, None] - idx[None, :]
    dmask = jnp.where((rel >= 0)[None], jnp.exp(log_gamma[:, None, None] * jnp.maximum(rel, 0.0)), 0.0)
    xi = jnp.exp(log_gamma[:, None] * (idx[None, :] + 1.0))[:, :, None]
    zeta = jnp.exp(log_gamma[:, None] * (CHUNK - 1.0 - idx[None, :]))[:, :, None]
    gamma_c = jnp.exp(log_gamma * CHUNK)
    wide = (RET_HEADS, CHUNK, RET_DK)
    return dmask, jnp.broadcast_to(xi, wide), jnp.broadcast_to(zeta, wide), gamma_c


def _rope_tables(rows):
    half = RET_DK // 2
    inv_freq = ROPE_BASE ** (-jnp.arange(half, dtype=F32) / half)
    pos = (jnp.arange(rows) - PAD).astype(F32)
    ang = pos[:, None] * inv_freq[None, :]
    return jnp.cos(ang), jnp.sin(ang)


def _ret_specs(order):
    return [pl.BlockSpec((CHUNK, RET_QK), lambda n: (order(n), 0)),
            pl.BlockSpec((CHUNK, RET_QK), lambda n: (order(n), 1)),
            pl.BlockSpec((CHUNK, RET_V), lambda n: (order(n), 1)),
            pl.BlockSpec((CHUNK, RET_V), lambda n: (order(n), 2))]


def _ret_const_specs():
    return [pl.BlockSpec((RET_HEADS, CHUNK, CHUNK), lambda n: (0, 0, 0)),
            pl.BlockSpec((RET_HEADS, CHUNK, RET_DK), lambda n: (0, 0, 0)),
            pl.BlockSpec((RET_HEADS, CHUNK, RET_DK), lambda n: (0, 0, 0)),
            pl.BlockSpec((1, RET_DV), lambda n: (0, 0))]


def _ret_fwd(proj, cos, sin, consts, gn_w, seq, ride=None):
    rows = proj.shape[0]
    nc = rows // CHUNK
    dmask, xi, zeta, gamma_c = consts

    def body(gam_ref, q_ref, k_ref, v_ref, g_ref, cos_ref, sin_ref, dm_ref, xi_ref, ze_ref, gn_ref,
             o_ref, y_ref, ss_ref, s_ref):
        n = pl.program_id(0)

        @pl.when(n == 0)
        def _():
            s_ref[...] = jnp.zeros_like(s_ref)

        cs, sn = cos_ref[...], sin_ref[...]
        kscale = _valid_rows(n * CHUNK, CHUNK, seq) * (RET_DK ** -0.5)
        gn = gn_ref[...]
        hs = range(RET_HEADS)
        qk_cols = [slice(h * RET_DK, (h + 1) * RET_DK) for h in hs]
        v_cols = [slice(h * RET_DV, (h + 1) * RET_DV) for h in hs]
        qr_l = [_rope(q_ref[:, c], cs, sn) for c in qk_cols]
        kr_l = [_rope(k_ref[:, c], cs, sn) * kscale for c in qk_cols]
        v_l = [v_ref[:, c] for c in v_cols]
        s_l = [s_ref[h] for h in hs]
        sc_l = [_nt(qr, kr) * dm_ref[h] for h, (qr, kr) in enumerate(zip(qr_l, kr_l))]
        o_l = [_nn(sc_l[h], v_l[h]) + _nn(qr_l[h] * xi_ref[h], s_l[h]) for h in hs]
        for h in hs:
            ss_ref[0, h] = s_l[h].astype(ss_ref.dtype)
            s_ref[h] = gam_ref[h] * s_l[h] + _tn(kr_l[h] * ze_ref[h], v_l[h])
            o_ref[:, v_cols[h]] = o_l[h]
            y_ref[:, v_cols[h]] = _gated_norm(o_l[h], g_ref[:, v_cols[h]], gn).astype(y_ref.dtype)

    fwd = lambda n: n
    row128 = pl.BlockSpec((CHUNK, RET_DK // 2), lambda n: (n, 0))
    row_v = pl.BlockSpec((CHUNK, RET_V), lambda n: (n, 0))
    res, rode = _pcall(
        body, [gamma_c, proj, proj, proj, proj, cos, sin, dmask, xi, zeta, gn_w.reshape(1, RET_DV)],
        grid=(nc,),
        in_specs=[pl.BlockSpec(memory_space=pltpu.SMEM)] + _ret_specs(fwd) + [row128, row128]
        + _ret_const_specs(),
        out_specs=[row_v, row_v,
                   pl.BlockSpec((1, RET_HEADS, RET_DK, RET_DV), lambda n: (n, 0, 0, 0))],
        out_shape=[jax.ShapeDtypeStruct((rows, RET_V), F32), jax.ShapeDtypeStruct((rows, RET_V), BF16),
                   jax.ShapeDtypeStruct((nc, RET_HEADS, RET_DK, RET_DV), BF16)],
        scratch=[pltpu.VMEM((RET_HEADS, RET_DK, RET_DV), F32)], name="ret_fwd", sem=("arbitrary",), ride=ride)
    return res if ride is None else (res, rode)


def _ret_bwd(proj, o, dy, states, cos, sin, consts, gn_w, seq, ride=None):
    rows = proj.shape[0]
    nc = rows // CHUNK
    dmask, xi, zeta, gamma_c = consts

    def body(gam_ref, q_ref, k_ref, v_ref, g_ref, o_ref, dy_ref, ss_ref, cos_ref, sin_ref,
             dm_ref, xi_ref, ze_ref, gn_ref, dp_ref, dgn_ref, ds_ref):
        n = pl.program_id(0)

        @pl.when(n == 0)
        def _():
            ds_ref[...] = jnp.zeros_like(ds_ref)
            dgn_ref[...] = jnp.zeros_like(dgn_ref)

        cs, sn = cos_ref[...], sin_ref[...]
        kscale = _valid_rows((nc - 1 - n) * CHUNK, CHUNK, seq) * (RET_DK ** -0.5)
        gn = gn_ref[...]
        dgn = jnp.zeros((1, RET_DV), F32)
        hs = range(RET_HEADS)
        qk_cols = [slice(h * RET_DK, (h + 1) * RET_DK) for h in hs]
        v_cols = [slice(h * RET_DV, (h + 1) * RET_DV) for h in hs]
        qr_l = [_rope(q_ref[:, c], cs, sn) for c in qk_cols]
        kr_l = [_rope(k_ref[:, c], cs, sn) * kscale for c in qk_cols]
        v_l = [v_ref[:, c] for c in v_cols]
        s_l = [ss_ref[0, h] for h in hs]
        ds_l = [ds_ref[h] for h in hs]
        gnb = [_gated_norm_bwd(dy_ref[:, c], o_ref[:, c], g_ref[:, c], gn) for c in v_cols]
        do_l = [x[0] for x in gnb]
        sc_l = [_nt(qr_l[h], kr_l[h]) * dm_ref[h] for h in hs]
        dsc_l = [_nt(do_l[h], v_l[h]) * dm_ref[h] for h in hs]
        dv_l = [_tn(sc_l[h], do_l[h]) + _nn(kr_l[h] * ze_ref[h], ds_l[h]) for h in hs]
        dqr_l = [_nn(dsc_l[h], kr_l[h]) + _nt(do_l[h], s_l[h]) * xi_ref[h] for h in hs]
        dkr_l = [_tn(dsc_l[h], qr_l[h]) + _nt(v_l[h], ds_l[h]) * ze_ref[h] for h in hs]
        for h in hs:
            dgn = dgn + gnb[h][2]
            ds_ref[h] = gam_ref[h] * ds_l[h] + _tn(qr_l[h] * xi_ref[h], do_l[h])
            dp_ref[:, qk_cols[h]] = _rope_bwd(dqr_l[h], cs, sn).astype(dp_ref.dtype)
            dp_ref[:, RET_QK + h * RET_DK:RET_QK + (h + 1) * RET_DK] = (
                _rope_bwd(dkr_l[h] * kscale, cs, sn).astype(dp_ref.dtype))
            dp_ref[:, 2 * RET_QK + h * RET_DV:2 * RET_QK + (h + 1) * RET_DV] = dv_l[h].astype(dp_ref.dtype)
            dp_ref[:, 2 * RET_QK + RET_V + h * RET_DV:2 * RET_QK + RET_V + (h + 1) * RET_DV] = (
                gnb[h][1].astype(dp_ref.dtype))
        dgn_ref[...] += dgn

    rev = lambda n: nc - 1 - n
    row128 = pl.BlockSpec((CHUNK, RET_DK // 2), lambda n: (rev(n), 0))
    row_v = pl.BlockSpec((CHUNK, RET_V), lambda n: (rev(n), 0))
    res, rode = _pcall(
        body, [gamma_c, proj, proj, proj, proj, o, dy, states, cos, sin, dmask, xi, zeta,
               gn_w.reshape(1, RET_DV)],
        grid=(nc,),
        in_specs=[pl.BlockSpec(memory_space=pltpu.SMEM)] + _ret_specs(rev) + [
            row_v, row_v, pl.BlockSpec((1, RET_HEADS, RET_DK, RET_DV), lambda n: (rev(n), 0, 0, 0)),
            row128, row128] + _ret_const_specs(),
        out_specs=[pl.BlockSpec((CHUNK, RET_IN), lambda n: (rev(n), 0)),
                   pl.BlockSpec((1, RET_DV), lambda n: (0, 0))],
        out_shape=[jax.ShapeDtypeStruct((rows, RET_IN), BF16), jax.ShapeDtypeStruct((1, RET_DV), F32)],
        scratch=[pltpu.VMEM((RET_HEADS, RET_DK, RET_DV), F32)], name="ret_bwd", sem=("arbitrary",), ride=ride)
    return res if ride is None else (res, rode)


GATE_COL = DN_CONV_CH // DN_V
BA_COL = (DN_CONV_CH + DN_V) // LANES
BETA_LANE, DECAY_LANE = 0, DN_HEADS
INV_SHIFT = 4
INV_SQUARINGS = INV_SHIFT - 1
assert CHUNK == 4 << INV_SHIFT


def _dn_in_specs(order):
    return [pl.BlockSpec((CHUNK, DN_CONV_CH), lambda n: (order(n), 0)),
            pl.BlockSpec((8, DN_CONV_CH), lambda n: (jnp.maximum(order(n) * (CHUNK // 8) - 1, 0), 0)),
            pl.BlockSpec((CHUNK, DN_V), lambda n: (order(n), GATE_COL)),
            pl.BlockSpec((CHUNK, LANES), lambda n: (order(n), BA_COL)),
            pl.BlockSpec((CONV_K, 1, DN_CONV_CH), lambda n: (0, 0, 0)),
            pl.BlockSpec((1, LANES), lambda n: (0, 0)),
            pl.BlockSpec((1, LANES), lambda n: (0, 0)),
            pl.BlockSpec((1, DN_DV), lambda n: (0, 0))]


def _dn_front(c, seq, x_ref, halo_ref, ba_ref, cw_ref, al_ref, dt_ref):
    valid = _valid_rows(c * CHUNK, CHUNK, seq)
    xin = x_ref[...] * valid
    halo = halo_ref[...] * _valid_rows(c * CHUNK - 8, 8, seq)
    x_sh = [xin] + [_shift_down(xin, halo, k) for k in range(1, CONV_K)]
    yc = x_sh[0] * cw_ref[CONV_K - 1]
    for k in range(1, CONV_K):
        yc = yc + x_sh[k] * cw_ref[CONV_K - 1 - k]
    sgc = jax.nn.sigmoid(yc)
    ba = ba_ref[...]
    sig = jax.nn.sigmoid(ba)
    beta = sig * valid
    z = ba + dt_ref[...]
    eal = jnp.exp(al_ref[...])
    g = -eal * _softplus(z) * valid
    ri, ci = _iota((CHUNK, CHUNK), 0), _iota((CHUNK, CHUNK), 1)
    lower = (ri >= ci).astype(F32)
    upper = (ri <= ci).astype(F32)
    eye = (ri == ci).astype(F32)
    gam = _nn(lower, g, hi=True)
    gam_t = _tn(g, upper, hi=True)
    return dict(valid=valid, x_sh=x_sh, yc=yc, sgc=sgc, act=yc * sgc, sig=sig, beta=beta, z=z,
                eal=eal, g=g, gam=gam, gam_t=gam_t, ri=ri, ci=ci, upper=upper, eye=eye)


def _dn_head(f, h):
    act = f["act"]
    q_raw = act[:, h * DN_DK:(h + 1) * DN_DK]
    k_raw = act[:, DN_QK + h * DN_DK:DN_QK + (h + 1) * DN_DK]
    v = act[:, 2 * DN_QK + h * DN_DV:2 * DN_QK + (h + 1) * DN_DV]
    rq = lax.rsqrt(jnp.sum(q_raw * q_raw, axis=-1, keepdims=True) + RMS_EPS)
    rk = lax.rsqrt(jnp.sum(k_raw * k_raw, axis=-1, keepdims=True) + RMS_EPS)
    qh = q_raw * rq
    kn = k_raw * rk
    gam_c = _col(f["gam"], DECAY_LANE + h)
    gam_r = _row(f["gam_t"], DECAY_LANE + h)
    bc = _col(f["beta"], BETA_LANE + h)
    diff = gam_c - gam_r
    decay = jnp.where(f["ri"] >= f["ci"], jnp.exp(jnp.minimum(diff, 0.0)), 0.0)
    glast = jnp.sum(gam_r * (_iota((1, CHUNK), 1) == CHUNK - 1).astype(F32), axis=1, keepdims=True)
    return dict(rq=rq, rk=rk, qh=qh, qn=qh * (DN_DK ** -0.5), kn=kn, v=v, gam_c=gam_c, gam_r=gam_r,
                bc=bc, diff=diff, decay=decay, egam=jnp.exp(gam_c), glast=glast,
                eglast=jnp.exp(glast), ekd=jnp.exp(glast - gam_c))


def _dn_fwd(proj, conv_w, alog, dtb, norm_w, seq):
    rows = proj.shape[0]
    nc = rows // CHUNK

    def body(x_ref, halo_ref, gate_ref, ba_ref, cw_ref, al_ref, dt_ref, nw_ref,
             o_ref, y_ref, ss_ref, t_ref, s_ref):
        n = pl.program_id(0)

        @pl.when(n == 0)
        def _():
            s_ref[...] = jnp.zeros_like(s_ref)

        f = _dn_front(n, seq, x_ref, halo_ref, ba_ref, cw_ref, al_ref, dt_ref)
        ri, ci = f["ri"], f["ci"]
        eye = f["eye"]
        diag_m = (jnp.right_shift(ri, INV_SHIFT) == jnp.right_shift(ci, INV_SHIFT)).astype(F32)
        half_m = (jnp.right_shift(ri, INV_SHIFT + 1) == jnp.right_shift(ci, INV_SHIFT + 1)).astype(F32)
        nw = nw_ref[...]
        heads = [_dn_head(f, h) for h in range(DN_HEADS)]
        a_all = [jnp.where(ri > ci, hd["bc"] * _nt(hd["kn"], hd["kn"]) * hd["decay"], 0.0) for hd in heads]
        b_all = [a * diag_m for a in a_all]
        t_all = [eye - b for b in b_all]
        for _ in range(INV_SQUARINGS):
            b_all = [_nn(b, b, hi=True) for b in b_all]
            t_all = [t + _nn(t, b, hi=True) for t, b in zip(t_all, b_all)]
        for off_m in (half_m - diag_m, 1.0 - half_m):
            x_all = [_nn(a * off_m, t, hi=True) for a, t in zip(a_all, t_all)]
            t_all = [t - _nn(t, x, hi=True) for t, x in zip(t_all, x_all)]
        u_all = [_nn(t, hd["v"] * hd["bc"], hi=True) for t, hd in zip(t_all, heads)]
        w_all = [_nn(t, hd["kn"] * (hd["bc"] * hd["egam"]), hi=True) for t, hd in zip(t_all, heads)]
        for h in range(DN_HEADS):
            hd = heads[h]
            v_cols = slice(h * DN_DV, (h + 1) * DN_DV)
            t_ref[0, h] = t_all[h]
            s = s_ref[h]
            ss_ref[0, h] = s
            u, w = u_all[h], w_all[h]
            v_new = u - _nn(w, s)
            qk = _nt(hd["qn"], hd["kn"]) * hd["decay"]
            o = _nn(hd["qn"] * hd["egam"], s) + _nn(qk, v_new)
            s_ref[h] = s * hd["eglast"] + _tn(hd["kn"] * hd["ekd"], v_new)
            o_ref[:, v_cols] = o
            y_ref[:, v_cols] = _gated_norm(o, gate_ref[:, v_cols], nw).astype(y_ref.dtype)

    fwd = lambda n: n
    row_v = pl.BlockSpec((CHUNK, DN_V), lambda n: (n, 0))
    return pl.pallas_call(
        body, grid=(nc,), in_specs=_dn_in_specs(fwd),
        out_specs=[row_v, row_v,
                   pl.BlockSpec((1, DN_HEADS, DN_DK, DN_DV), lambda n: (n, 0, 0, 0)),
                   pl.BlockSpec((1, DN_HEADS, CHUNK, CHUNK), lambda n: (n, 0, 0, 0))],
        out_shape=[jax.ShapeDtypeStruct((rows, DN_V), F32), jax.ShapeDtypeStruct((rows, DN_V), BF16),
                   jax.ShapeDtypeStruct((nc, DN_HEADS, DN_DK, DN_DV), F32),
                   jax.ShapeDtypeStruct((nc, DN_HEADS, CHUNK, CHUNK), F32)],
        scratch_shapes=[pltpu.VMEM((DN_HEADS, DN_DK, DN_DV), F32)],
        name="dn_fwd", compiler_params=_params("arbitrary"))(
            proj, proj, proj, proj, conv_w, alog, dtb, norm_w.reshape(1, DN_DV))


def _dn_bwd(proj, o, dy, states, tinv, conv_w, alog, dtb, norm_w, seq):
    rows = proj.shape[0]
    nc = rows // CHUNK

    def body(x_ref, halo_ref, gate_ref, ba_ref, cw_ref, al_ref, dt_ref, nw_ref,
             o_ref, dy_ref, ss_ref, t_ref,
             dp_ref, dcw_ref, dal_ref, ddt_ref, dnw_ref, ds_ref, nxt_ref):
        n = pl.program_id(0)

        @pl.when(n == 0)
        def _():
            ds_ref[...] = jnp.zeros_like(ds_ref)
            nxt_ref[...] = jnp.zeros_like(nxt_ref)
            dcw_ref[...] = jnp.zeros_like(dcw_ref)
            dal_ref[...] = jnp.zeros_like(dal_ref)
            ddt_ref[...] = jnp.zeros_like(ddt_ref)
            dnw_ref[...] = jnp.zeros_like(dnw_ref)

        f = _dn_front(nc - 1 - n, seq, x_ref, halo_ref, ba_ref, cw_ref, al_ref, dt_ref)
        ri, ci = f["ri"], f["ci"]
        strict = (ri > ci).astype(F32)
        nw = nw_ref[...]
        lane128 = _iota((1, LANES), 1)
        row128 = _iota((LANES, 1), 0)
        dgam_col = jnp.zeros((CHUNK, LANES), F32)
        dgam_row = jnp.zeros((LANES, CHUNK), F32)
        dbeta = jnp.zeros((CHUNK, LANES), F32)
        dnw = jnp.zeros((1, DN_DV), F32)
        hs = range(DN_HEADS)
        heads = [_dn_head(f, h) for h in hs]
        cols = [slice(h * DN_DV, (h + 1) * DN_DV) for h in hs]
        t_l = [t_ref[0, h] for h in hs]
        s_l = [ss_ref[0, h] for h in hs]
        ds_l = [ds_ref[h] for h in hs]
        kk_l = [_nt(hd["kn"], hd["kn"]) for hd in heads]
        p_l = [_nt(hd["qn"], hd["kn"]) for hd in heads]
        rhsw_l = [hd["kn"] * (hd["bc"] * hd["egam"]) for hd in heads]
        u_l = [_nn(t, hd["v"] * hd["bc"], hi=True) for t, hd in zip(t_l, heads)]
        w_l = [_nn(t, r, hi=True) for t, r in zip(t_l, rhsw_l)]
        vnew_l = [u - _nn(w, s) for u, w, s in zip(u_l, w_l, s_l)]
        gnb = [_gated_norm_bwd(dy_ref[:, c], o_ref[:, c], gate_ref[:, c], nw) for c in cols]
        do_l = [x[0] for x in gnb]
        for h in hs:
            dp_ref[:, DN_CONV_CH + h * DN_DV:DN_CONV_CH + (h + 1) * DN_DV] = gnb[h][1].astype(dp_ref.dtype)
            dnw = dnw + gnb[h][2]
        qg_l = [hd["qn"] * hd["egam"] for hd in heads]
        kd_l = [hd["kn"] * hd["ekd"] for hd in heads]
        dvnew_l = [_tn(p * hd["decay"], do) + _nn(kd, ds)
                   for p, hd, do, kd, ds in zip(p_l, heads, do_l, kd_l, ds_l)]
        m_l = [_nt(do, vn) for do, vn in zip(do_l, vnew_l)]
        dqg_l = [_nt(do, s) for do, s in zip(do_l, s_l)]
        dkd_l = [_nt(vn, ds) for vn, ds in zip(vnew_l, ds_l)]
        for h in hs:
            ds_ref[h] = (ds_l[h] * heads[h]["eglast"] + _tn(qg_l[h], do_l[h]) - _tn(w_l[h], dvnew_l[h]))
        dw_l = [-_nt(dvn, s) for dvn, s in zip(dvnew_l, s_l)]
        dru_l = [_tn(t, dvn, hi=True) for t, dvn in zip(t_l, dvnew_l)]
        drw_l = [_tn(t, dw_, hi=True) for t, dw_ in zip(t_l, dw_l)]
        da_l = [-(_nt(dru, u) + _nt(drw, w)) * strict for dru, u, drw, w in zip(dru_l, u_l, drw_l, w_l)]
        dp_l = [m * hd["decay"] for m, hd in zip(m_l, heads)]
        dkk_l = [da * (hd["bc"] * hd["decay"]) for da, hd in zip(da_l, heads)]
        dqn_l = [dqg * hd["egam"] + _nn(dp, hd["kn"]) for dqg, hd, dp in zip(dqg_l, heads, dp_l)]
        dkn_l = [_tn(dp, hd["qn"]) + dkd * hd["ekd"] + drw * (hd["bc"] * hd["egam"])
                 + _nn(dkk, hd["kn"]) + _tn(dkk, hd["kn"])
                 for dp, hd, dkd, drw, dkk in zip(dp_l, heads, dkd_l, drw_l, dkk_l)]
        dq_parts, dk_parts, dv_parts = [], [], []
        for h in hs:
            hd = heads[h]
            kn, v, bc, egam, decay = hd["kn"], hd["v"], hd["bc"], hd["egam"], hd["decay"]
            t1 = jnp.sum(dkd_l[h] * kd_l[h], axis=1, keepdims=True)
            dglast = (jnp.sum(t1, axis=0, keepdims=True)
                      + jnp.sum(jnp.sum(ds_l[h] * s_l[h], axis=1, keepdims=True), axis=0, keepdims=True)
                      * hd["eglast"])
            e = (m_l[h] * p_l[h] + da_l[h] * (bc * kk_l[h])) * decay
            dgc = (jnp.sum(dqg_l[h] * qg_l[h], axis=1, keepdims=True) - t1
                   + jnp.sum(drw_l[h] * rhsw_l[h], axis=1, keepdims=True)
                   + jnp.sum(e, axis=1, keepdims=True)
                   + jnp.where(_iota((CHUNK, 1), 0) == CHUNK - 1, dglast, 0.0))
            dgr = -jnp.sum(e, axis=0, keepdims=True)
            dbc = (jnp.sum(dru_l[h] * v, axis=1, keepdims=True)
                   + jnp.sum(drw_l[h] * kn, axis=1, keepdims=True) * egam
                   + jnp.sum(da_l[h] * kk_l[h] * decay, axis=1, keepdims=True))
            dv_parts.append(dru_l[h] * bc)
            qh, dqn, dkn = hd["qh"], dqn_l[h], dkn_l[h]
            dq_parts.append(((DN_DK ** -0.5) * hd["rq"])
                            * (dqn - qh * jnp.sum(dqn * qh, axis=1, keepdims=True)))
            dk_parts.append(hd["rk"] * (dkn - kn * jnp.sum(dkn * kn, axis=1, keepdims=True)))
            dgam_col = dgam_col + dgc * (lane128 == DECAY_LANE + h).astype(F32)
            dbeta = dbeta + dbc * (lane128 == BETA_LANE + h).astype(F32)
            dgam_row = dgam_row + (row128 == DECAY_LANE + h).astype(F32) * dgr
        dnw_ref[...] += dnw
        dgam = dgam_col + _nt(f["eye"], dgam_row, hi=True)
        dg = _nn(f["upper"], dgam, hi=True)
        d_a = dg * (-f["eal"]) * jax.nn.sigmoid(f["z"]) * f["valid"]
        dal_ref[...] += jnp.sum(dg * f["g"], axis=0, keepdims=True)
        ddt_ref[...] += jnp.sum(d_a, axis=0, keepdims=True)
        d_b = dbeta * f["valid"] * f["sig"] * (1.0 - f["sig"])
        dp_ref[:, DN_CONV_CH + DN_V:] = (d_a + d_b).astype(dp_ref.dtype)
        dact = jnp.concatenate(dq_parts + dk_parts + dv_parts, axis=1)
        yc, sgc = f["yc"], f["sgc"]
        dyc = dact * (sgc * (1.0 + yc * (1.0 - sgc)))
        for k in range(CONV_K):
            dcw_ref[k] += jnp.sum(dyc * f["x_sh"][CONV_K - 1 - k], axis=0, keepdims=True)
        nxt = nxt_ref[...]
        dx = dyc * cw_ref[CONV_K - 1]
        for j in range(1, CONV_K):
            dx = dx + _shift_up(dyc, nxt, j) * cw_ref[CONV_K - 1 - j]
        nxt_ref[...] = dyc[0:8]
        dp_ref[:, :DN_CONV_CH] = (dx * f["valid"]).astype(dp_ref.dtype)

    rev = lambda n: nc - 1 - n
    row_v = pl.BlockSpec((CHUNK, DN_V), lambda n: (rev(n), 0))
    vec = pl.BlockSpec((1, LANES), lambda n: (0, 0))
    return pl.pallas_call(
        body, grid=(nc,),
        in_specs=_dn_in_specs(rev) + [
            row_v, row_v,
            pl.BlockSpec((1, DN_HEADS, DN_DK, DN_DV), lambda n: (rev(n), 0, 0, 0)),
            pl.BlockSpec((1, DN_HEADS, CHUNK, CHUNK), lambda n: (rev(n), 0, 0, 0))],
        out_specs=[pl.BlockSpec((CHUNK, DN_IN_PAD), lambda n: (rev(n), 0)),
                   pl.BlockSpec((CONV_K, 1, DN_CONV_CH), lambda n: (0, 0, 0)), vec, vec,
                   pl.BlockSpec((1, DN_DV), lambda n: (0, 0))],
        out_shape=[jax.ShapeDtypeStruct((rows, DN_IN_PAD), BF16),
                   jax.ShapeDtypeStruct((CONV_K, 1, DN_CONV_CH), F32),
                   jax.ShapeDtypeStruct((1, LANES), F32), jax.ShapeDtypeStruct((1, LANES), F32),
                   jax.ShapeDtypeStruct((1, DN_DV), F32)],
        scratch_shapes=[pltpu.VMEM((DN_HEADS, DN_DK, DN_DV), F32), pltpu.VMEM((8, DN_CONV_CH), F32)],
        name="dn_bwd", compiler_params=_params("arbitrary"))(
            proj, proj, proj, proj, conv_w, alog, dtb, norm_w.reshape(1, DN_DV), o, dy, states, tinv)


def _train_step(x, tgt, wts, sh, idx):
    seq = x.shape[0]
    rows = -(-(seq + CHUNK) // ROW_ALIGN) * ROW_ALIGN
    tail = rows - seq - CHUNK
    h0 = jnp.concatenate([jnp.zeros((PAD, D_MODEL), F32), wts["meta_tokens"].astype(F32), x,
                          jnp.zeros((tail, D_MODEL), F32)], axis=0)
    tgt_p = jnp.concatenate([jnp.zeros((CHUNK, D_MODEL), F32), tgt, jnp.zeros((tail, D_MODEL), F32)],
                            axis=0)
    cos, sin = _rope_tables(rows)
    consts = _ret_consts()
    conv_w = wts["dn_conv_w"].reshape(CONV_K, 1, DN_CONV_CH)
    lane_pad = LANES - 2 * DN_HEADS
    alog = jnp.pad(wts["dn_a_log"].reshape(1, DN_HEADS), ((0, 0), (DECAY_LANE, lane_pad)))
    dtb = jnp.pad(wts["dn_dt_bias"].reshape(1, DN_HEADS), ((0, 0), (DECAY_LANE, lane_pad)))
    g = {}

    wts = dict(wts)
    (got,) = _gather_weights([sh["ret_w_in"]])
    wts["ret_w_in"] = got.reshape(N_SHARD, D_MODEL, -1)
    hn0 = _rms_fwd(h0, wts["mix_norm_w"][0], "rms_mix0")
    proj0, got = _mm_cols(hn0, wts["ret_w_in"], "ret_in",
                          ride=_Ride("gather", [sh["ret_w_out"], sh["ffn_w_gate"], sh["dn_w_out"]]))
    wts["ret_w_out"] = got[0].reshape(-1, D_MODEL)
    wts["ffn_w_gate"] = got[1]
    wts["dn_w_out"] = got[2].reshape(-1, D_MODEL)
    (o0, y0, st0), got = _ret_fwd(proj0, cos, sin, consts, wts["ret_gn_w"], seq,
                                  ride=_Ride("gather", [sh["ffn_w_up"], sh["ffn_w_down"], sh["dn_w_in"]]))
    wts["ffn_w_up"], wts["ffn_w_down"] = got[0], got[1]
    n_dn = sh["dn_w_in"].shape[-1]
    wts["dn_w_in"] = jnp.pad(_join_cols(got[2].reshape(N_SHARD, D_MODEL, n_dn)),
                             ((0, 0), (0, DN_IN_PAD - N_SHARD * n_dn)))
    h1 = _mm(y0, wts["ret_w_out"], mode="nn", name="ret_out", resid=h0)
    hn1 = _rms_fwd(h1, wts["ffn_norm_w"][0], "rms_ffn0")
    g0, u0, act0 = _ffn_up(hn1, wts["ffn_w_gate"], wts["ffn_w_up"], 0, "ffn_up0")
    h2 = _ffn_down(act0, wts["ffn_w_down"], h1, 0, "ffn_down0")
    hn2 = _rms_fwd(h2, wts["mix_norm_w"][1], "rms_mix1")
    proj1 = _mm(hn2, wts["dn_w_in"], mode="nn", name="dn_in")
    o1, y1, st1, tinv = _dn_fwd(proj1, conv_w, alog, dtb, wts["dn_norm_w"], seq)
    h3 = _mm(y1, wts["dn_w_out"], mode="nn", name="dn_out", resid=h2)
    hn3 = _rms_fwd(h3, wts["ffn_norm_w"][1], "rms_ffn1")
    g1, u1, act1 = _ffn_up(hn3, wts["ffn_w_gate"], wts["ffn_w_up"], 1, "ffn_up1")
    h4 = _ffn_down(act1, wts["ffn_w_down"], h3, 1, "ffn_down1")

    dh4, g["final_norm_w"], loss = _final_loss(h4, wts["final_norm_w"], tgt_p, seq, "final_loss")

    layers = wts["ffn_w_gate"].shape[1]

    def ffn_bwd(dh_out, h_mid, hn, gg, uu, act, layer, prev):
        tag = str(layer)
        dg, du = _ffn_down_bwd(dh_out, wts["ffn_w_down"], gg, uu, layer, "ffn_down_bwd" + tag)
        d_down = _ffn_wgrad(act, [dh_out], layer, layers, prev and prev[:1], True, "ffn_dwd" + tag)
        d_gu = _ffn_wgrad(hn, [dg, du], layer, layers, prev and prev[1:], False, "ffn_dwgu" + tag)
        dh_mid, d_norm = _ffn_up_bwd(dg, du, wts["ffn_w_gate"], wts["ffn_w_up"], layer, h_mid,
                                     wts["ffn_norm_w"][layer], dh_out, "ffn_up_bwd" + tag)
        return dh_mid, list(d_down) + list(d_gu), d_norm

    red = {}

    def rs_begin(names, grads, tag):
        gs = [gr.reshape((N_SHARD,) + sh[n].shape) for n, gr in zip(names, grads)]
        sib = _rs_pair(gs, "rs_pair" + tag)
        parts = [_rs_pair_add(gs[t], sib[t], idx, "rs_pair_add_" + n) for t, n in enumerate(names)]
        return gs, sib, parts

    def rs_end(names, begun, others, tag):
        gs, sib, _ = begun
        mine = [_rs_final_add(gs[t], sib[t], others[t], idx, "rs_final_add_" + n) for t, n in enumerate(names)]
        red.update(zip(names, _rs_share(mine, "rs_share" + tag)))

    dh3, ffn_grads, dfn1 = ffn_bwd(dh4, h3, hn3, g1, u1, act1, 1, None)
    dy1 = _mm(dh3, wts["dn_w_out"], mode="nt", name="dn_out_bwd")
    d_dn_out = _mm(y1, dh3, mode="tn", name="dn_dwo")
    dproj1, dcw, dal, ddt, g["dn_norm_w"] = _dn_bwd(proj1, o1, dy1, st1, tinv, conv_w, alog, dtb,
                                                    wts["dn_norm_w"], seq)
    d_dn_in = _mm(hn2, dproj1, mode="tn", name="dn_dwi")
    d_dn_in = jnp.stack([d_dn_in[:, j * n_dn:(j + 1) * n_dn] for j in range(N_SHARD)])
    group = ["dn_w_out", "dn_w_in"]
    begun = rs_begin(group, [d_dn_out, d_dn_in], "1")
    (dh2, dmn1), others = _mm_nt_rms(dproj1, wts["dn_w_in"], h2, wts["mix_norm_w"][1], dh3, "dn_in_bwd",
                                     ride=_Ride("chips", begun[2]))
    rs_end(group, begun, others, "1")
    g["dn_conv_w"] = dcw.reshape(CONV_K, DN_CONV_CH)
    g["dn_a_log"] = dal[0, DECAY_LANE:DECAY_LANE + DN_HEADS]
    g["dn_dt_bias"] = ddt[0, DECAY_LANE:DECAY_LANE + DN_HEADS]

    dh1, ffn_grads, dfn0 = ffn_bwd(dh2, h1, hn1, g0, u0, act0, 0, ffn_grads)
    dy0 = _mm(dh1, wts["ret_w_out"], mode="nt", name="ret_out_bwd")
    d_ret_out = _mm(y0, dh1, mode="tn", name="ret_dwo")
    group = ["ffn_w_down", "ffn_w_gate", "ffn_w_up", "ret_w_out"]
    begun = rs_begin(group, list(ffn_grads) + [d_ret_out], "2")
    (dproj0, g["ret_gn_w"]), others = _ret_bwd(proj0, o0, dy0, st0, cos, sin, consts, wts["ret_gn_w"], seq,
                                               ride=_Ride("chips", begun[2]))
    rs_end(group, begun, others, "2")
    d_ret_in = _mm_cols_grad(hn0, dproj0, "ret_dwi")
    begun = rs_begin(["ret_w_in"], [d_ret_in], "3")
    (dh0, dmn0), others = _mm_cols_t_rms(dproj0, wts["ret_w_in"], h0, wts["mix_norm_w"][0], dh1, "ret_in_bwd",
                                         ride=_Ride("chips", begun[2]))
    rs_end(["ret_w_in"], begun, others, "3")

    g["ffn_norm_w"] = jnp.concatenate([dfn0, dfn1], axis=0)
    g["mix_norm_w"] = jnp.concatenate([dmn0, dmn1], axis=0)
    g["meta_tokens"] = dh0[PAD:CHUNK]
    g["final_norm_w"] = g["final_norm_w"].reshape(D_MODEL)
    g["ret_gn_w"] = g["ret_gn_w"].reshape(RET_DV)
    g["dn_norm_w"] = g["dn_norm_w"].reshape(DN_DV)
    return loss, dh0, g, red


def _mesh_pos():
    return lax.axis_index("x"), lax.axis_index("y"), lax.axis_index("c")


def _other_chips(x, y):
    return [(1 - x, y), (x, 1 - y), (1 - x, 1 - y)]


def _remote(src, dst, send_sem, recv_sem, to):
    return pltpu.make_async_remote_copy(src_ref=src, dst_ref=dst, send_sem=send_sem, recv_sem=recv_sem,
                                        device_id=to, device_id_type=MESH)


GATHER_COPIES = 7


def _gather_weights(shards):
    ride = _Ride("gather", shards)

    def body(*refs):
        nt = len(shards)
        for phase in range(3):
            _gather_phase(phase, refs[:nt], refs[nt:2 * nt], *refs[2 * nt:])

    return pl.pallas_call(body, out_shape=ride.out_shape, in_specs=ride.in_specs, out_specs=ride.out_specs,
                          scratch_shapes=ride.scratch, name="gather_weights")(*shards)


def _gather_phase(phase, ins, outs, send_sems, recv_sems):
    x, y, c = _mesh_pos()
    me = 2 * x + y
    chips = _other_chips(x, y)
    sibling = (x, y, 1 - c)

    def cp(t, k, src, dst, to):
        i = GATHER_COPIES * t + k
        return _remote(src, dst, send_sems.at[i], recv_sems.at[i], to)

    for t in range(len(ins)):
        own = cp(t, 0, ins[t], outs[t].at[me], sibling)
        if phase == 0:
            own.start()
        if phase == 2:
            own.wait()
        for k, (px, py) in enumerate(chips):
            landed = outs[t].at[2 * px + py, c]
            theirs = outs[t].at[2 * px + py, 1 - c]
            to_chip = cp(t, 1 + k, ins[t].at[c], outs[t].at[me, c], (px, py, c))
            if phase == 0:
                to_chip.start()
            if phase == 1:
                cp(t, 1 + k, ins[t].at[c], landed, (px, py, c)).wait_recv()
                cp(t, 4 + k, landed, landed, sibling).start()
            if phase == 2:
                to_chip.wait_send()
                cp(t, 4 + k, landed, landed, sibling).wait_send()
                cp(t, 4 + k, theirs, theirs, sibling).wait_recv()


def _chips_phase(phase, ins, outs, send_sems, recv_sems):
    x, y, c = _mesh_pos()
    for t in range(len(ins)):
        for k, (px, py) in enumerate(_other_chips(x, y)):
            cp = _remote(ins[t].at[2 * px + py], outs[t].at[k], send_sems.at[3 * t + k], recv_sems.at[3 * t + k],
                         (px, py, c))
            if phase == 0:
                cp.start()
            if phase == 2:
                cp.wait()


class _Ride:
    def __init__(self, kind, arrays):
        self.kind, self.arrays = kind, list(arrays)
        nt = len(self.arrays)
        if kind == "gather":
            self.phase_fn, n_sem = _gather_phase, GATHER_COPIES * nt
            self.out_shape = [jax.ShapeDtypeStruct((N_SHARD,) + a.shape, a.dtype) for a in self.arrays]
        else:
            self.phase_fn, n_sem = _chips_phase, 3 * nt
            self.out_shape = [jax.ShapeDtypeStruct((3,) + a.shape[1:], a.dtype) for a in self.arrays]
        self.in_specs, self.out_specs = [ANY] * nt, [ANY] * nt
        self.scratch = [pltpu.SemaphoreType.DMA((n_sem,)), pltpu.SemaphoreType.DMA((n_sem,))]

    def emit(self, step, nsteps, ins, outs, send_sems, recv_sems, before):
        mid = max(0, min((7 * nsteps) // 8, nsteps - 2))
        todo = [(0, 0), (1, mid)] if before else [(2, nsteps - 1)]
        for phase, at in todo:
            if phase == 1 and self.kind != "gather":
                continue

            @pl.when(step == at)
            def _(phase=phase):
                self.phase_fn(phase, ins, outs, send_sems, recv_sems)


def _gather_small(blk):
    r, wd = blk.shape

    def body(b_ref, out_ref, send_sems, recv_sems):
        x, y, c = _mesh_pos()
        chips = _other_chips(x, y)
        out_ref[2 * x + y] = b_ref[...]
        sends = [_remote(b_ref, out_ref.at[2 * x + y], send_sems.at[k], recv_sems.at[k], (px, py, c))
                 for k, (px, py) in enumerate(chips)]
        for cp in sends:
            cp.start()
        for k, (px, py) in enumerate(chips):
            _remote(b_ref, out_ref.at[2 * px + py], send_sems.at[k], recv_sems.at[k], (px, py, c)).wait_recv()
        for cp in sends:
            cp.wait_send()

    return pl.pallas_call(
        body, out_shape=jax.ShapeDtypeStruct((4, r, wd), blk.dtype), in_specs=[VMEM_SPEC], out_specs=VMEM_SPEC,
        scratch_shapes=[pltpu.SemaphoreType.DMA((3,)), pltpu.SemaphoreType.DMA((3,))],
        name="gather_small")(blk)


def _allreduce_small(blk):
    r, wd = blk.shape
    rels = [(dx, dy, dc) for dx in (0, 1) for dy in (0, 1) for dc in (0, 1) if dx or dy or dc]

    def body(b_ref, out_ref, buf_ref, send_sems, recv_sems):
        x, y, c = _mesh_pos()

        def peer(rel):
            dx, dy, dc = rel
            return (1 - x if dx else x, 1 - y if dy else y, 1 - c if dc else c)

        me = 4 * x + 2 * y + c
        buf_ref[me] = b_ref[...]
        sends = [_remote(b_ref, buf_ref.at[me], send_sems.at[k], recv_sems.at[k], peer(rel))
                 for k, rel in enumerate(rels)]
        for cp in sends:
            cp.start()
        for k, rel in enumerate(rels):
            px, py, pc = peer(rel)
            _remote(b_ref, buf_ref.at[4 * px + 2 * py + pc], send_sems.at[k], recv_sems.at[k],
                    (px, py, pc)).wait_recv()
        for cp in sends:
            cp.wait_send()
        acc = buf_ref[0]
        for d in range(1, 8):
            acc = acc + buf_ref[d]
        out_ref[...] = acc

    return pl.pallas_call(
        body, out_shape=jax.ShapeDtypeStruct((r, wd), blk.dtype), in_specs=[VMEM_SPEC], out_specs=VMEM_SPEC,
        scratch_shapes=[pltpu.VMEM((8, r, wd), blk.dtype), pltpu.SemaphoreType.DMA((7,)),
                        pltpu.SemaphoreType.DMA((7,))],
        name="allreduce_small")(blk)


def _rs_pair(gs, name):
    nt = len(gs)

    def body(*refs):
        ins, outs = refs[:nt], refs[nt:2 * nt]
        send_sems, recv_sems = refs[2 * nt:]
        x, y, c = _mesh_pos()
        cps = [_remote(ins[t].at[:, 1 - c], outs[t], send_sems.at[t], recv_sems.at[t], (x, y, 1 - c))
               for t in range(nt)]
        for cp in cps:
            cp.start()
        for cp in cps:
            cp.wait()

    return pl.pallas_call(
        body, out_shape=[jax.ShapeDtypeStruct(g.shape[:1] + g.shape[2:], g.dtype) for g in gs],
        in_specs=[ANY] * nt, out_specs=[ANY] * nt,
        scratch_shapes=[pltpu.SemaphoreType.DMA((nt,)), pltpu.SemaphoreType.DMA((nt,))], name=name)(*gs)


def _rs_tile(a, b):
    return _div_tile(a, 512 if b <= 1024 else 256, 16)


def _rs_pair_add(g, a, idx, name):
    _, _, rows, cols = g.shape
    tr = _rs_tile(rows, cols)

    def body(s_ref, g_ref, a_ref, p_ref):
        p_ref[...] = (g_ref[...] + a_ref[...]).astype(p_ref.dtype)

    blk = pl.BlockSpec((None, tr, cols), lambda j, i, s: (j, i, 0))
    spec = pltpu.PrefetchScalarGridSpec(
        num_scalar_prefetch=1, grid=(N_SHARD, rows // tr),
        in_specs=[pl.BlockSpec((None, None, tr, cols), lambda j, i, s: (j, s[0], i, 0)), blk], out_specs=blk)
    return pl.pallas_call(
        body, grid_spec=spec, out_shape=jax.ShapeDtypeStruct((N_SHARD, rows, cols), BF16), name=name,
        compiler_params=_params("parallel", "parallel"))(idx, g, a)


def _rs_final_add(g, a, b, idx, name):
    _, _, rows, cols = g.shape
    tr = _rs_tile(rows, cols)

    def body(s_ref, g_ref, a_ref, b0_ref, b1_ref, b2_ref, f_ref):
        own = g_ref[...] + a_ref[...]
        f_ref[...] = ((own + b0_ref[...].astype(F32)) + b1_ref[...].astype(F32)) + b2_ref[...].astype(F32)

    def b_spec(k):
        return pl.BlockSpec((None, tr, cols), lambda i, s: (k, i, 0))

    spec = pltpu.PrefetchScalarGridSpec(
        num_scalar_prefetch=1, grid=(rows // tr,),
        in_specs=[pl.BlockSpec((None, None, tr, cols), lambda i, s: (s[1], s[0], i, 0)),
                  pl.BlockSpec((None, tr, cols), lambda i, s: (s[1], i, 0)), b_spec(0), b_spec(1), b_spec(2)],
        out_specs=pl.BlockSpec((None, tr, cols), lambda i, s: (s[0], i, 0)))
    return pl.pallas_call(
        body, grid_spec=spec, out_shape=jax.ShapeDtypeStruct((2, rows, cols), F32), name=name,
        compiler_params=_params("parallel"))(idx, g, a, b, b, b)


def _rs_share(fs, name):
    nt = len(fs)

    def body(*refs):
        outs = refs[nt:2 * nt]
        send_sems, recv_sems = refs[2 * nt:]
        x, y, c = _mesh_pos()
        cps = [_remote(outs[t].at[c], outs[t].at[c], send_sems.at[t], recv_sems.at[t], (x, y, 1 - c))
               for t in range(nt)]
        for cp in cps:
            cp.start()
        for cp in cps:
            cp.wait()

    return pl.pallas_call(
        body, out_shape=[jax.ShapeDtypeStruct(f.shape, f.dtype) for f in fs],
        in_specs=[ANY] * nt, out_specs=[ANY] * nt, input_output_aliases={t: t for t in range(nt)},
        scratch_shapes=[pltpu.SemaphoreType.DMA((nt,)), pltpu.SemaphoreType.DMA((nt,))], name=name)(*fs)


def _adamw(w, g, m, v, name):
    lead, rows, cols = w.shape
    tr = rows // 4 if rows % 32 == 0 else rows

    def body(w_ref, g_ref, m_ref, v_ref, go_ref, d_ref, mo_ref, vo_ref):
        gv = g_ref[...]
        go_ref[...] = gv
        mn = ADAM_B1 * m_ref[...] + (1.0 - ADAM_B1) * gv
        vn = ADAM_B2 * v_ref[...] + (1.0 - ADAM_B2) * (gv * gv)
        m_hat = mn / (1.0 - ADAM_B1 ** ADAM_STEP)
        v_hat = vn / (1.0 - ADAM_B2 ** ADAM_STEP)
        d_ref[...] = -ADAM_LR * (m_hat / (jnp.sqrt(v_hat) + ADAM_EPS) + ADAM_WD * w_ref[...])
        mo_ref[...] = mn
        vo_ref[...] = vn

    blk = pl.BlockSpec((None, tr, cols), lambda l, i: (l, i, 0))
    out = jax.ShapeDtypeStruct((lead, rows, cols), F32)
    return pl.pallas_call(
        body, grid=(lead, rows // tr), in_specs=[blk] * 4, out_specs=[blk] * 4, out_shape=[out] * 4, name=name,
        compiler_params=_params("parallel", "parallel"))(w, g, m, v)


BIG = ["ret_w_in", "ret_w_out", "dn_w_in", "dn_w_out", "ffn_w_gate", "ffn_w_up", "ffn_w_down"]
SMALL =["meta_tokens", "mix_norm_w", "ffn_norm_w", "ret_gn_w", "dn_conv_w", "dn_a_log", "dn_dt_bias",
         "dn_norm_w", "final_norm_w"]
SMALL_SHARDED = {"meta_tokens", "dn_conv_w", "dn_norm_w"}
ORDER = ["meta_tokens", "mix_norm_w", "ffn_norm_w", "ret_w_in", "ret_gn_w", "ret_w_out", "dn_w_in",
         "dn_conv_w", "dn_a_log", "dn_dt_bias", "dn_norm_w", "dn_w_out", "ffn_w_gate", "ffn_w_up",
         "ffn_w_down", "final_norm_w"]


def _halves(a):
    return a.reshape(2, -1, a.shape[-1])


def _pack_lanes(parts, align=8):
    flat = jnp.concatenate([p.reshape(-1) for p in parts])
    flat = jnp.pad(flat, (0, -flat.shape[0] % (align * LANES)))
    return flat.reshape(-1, LANES)


def _unpack(buf, shapes):
    lead = buf.shape[:-2]
    flat = buf.reshape(lead + (-1,))
    out, off = [], 0
    for shp in shapes:
        size = math.prod(shp)
        out.append(flat[..., off:off + size].reshape(lead + tuple(shp)))
        off += size
    return out


def _join_cols(shards):
    return jnp.concatenate([shards[j] for j in range(N_SHARD)], axis=-1)


def kernel(x, meta_tokens, mix_norm_w, ffn_norm_w, ret_w_in, ret_gn_w, ret_w_out, dn_w_in, dn_conv_w, dn_a_log, dn_dt_bias, dn_norm_w, dn_w_out, ffn_w_gate, ffn_w_up, ffn_w_down, final_norm_w, loss_target, m_meta_tokens, m_mix_norm_w, m_ffn_norm_w, m_ret_w_in, m_ret_gn_w, m_ret_w_out, m_dn_w_in, m_dn_conv_w, m_dn_a_log, m_dn_dt_bias, m_dn_norm_w, m_dn_w_out, m_ffn_w_gate, m_ffn_w_up, m_ffn_w_down, m_final_norm_w, v_meta_tokens, v_mix_norm_w, v_ffn_norm_w, v_ret_w_in, v_ret_gn_w, v_ret_w_out, v_dn_w_in, v_dn_conv_w, v_dn_a_log, v_dn_dt_bias, v_dn_norm_w, v_dn_w_out, v_ffn_w_gate, v_ffn_w_up, v_ffn_w_down, v_final_norm_w):
    w = dict(meta_tokens=meta_tokens, mix_norm_w=mix_norm_w, ffn_norm_w=ffn_norm_w, ret_w_in=ret_w_in,
             ret_gn_w=ret_gn_w, ret_w_out=ret_w_out, dn_w_in=dn_w_in, dn_conv_w=dn_conv_w, dn_a_log=dn_a_log,
             dn_dt_bias=dn_dt_bias, dn_norm_w=dn_norm_w, dn_w_out=dn_w_out, ffn_w_gate=ffn_w_gate,
             ffn_w_up=ffn_w_up, ffn_w_down=ffn_w_down, final_norm_w=final_norm_w)
    m = dict(meta_tokens=m_meta_tokens, mix_norm_w=m_mix_norm_w, ffn_norm_w=m_ffn_norm_w, ret_w_in=m_ret_w_in,
             ret_gn_w=m_ret_gn_w, ret_w_out=m_ret_w_out, dn_w_in=m_dn_w_in, dn_conv_w=m_dn_conv_w,
             dn_a_log=m_dn_a_log, dn_dt_bias=m_dn_dt_bias, dn_norm_w=m_dn_norm_w, dn_w_out=m_dn_w_out,
             ffn_w_gate=m_ffn_w_gate, ffn_w_up=m_ffn_w_up, ffn_w_down=m_ffn_w_down, final_norm_w=m_final_norm_w)
    v = dict(meta_tokens=v_meta_tokens, mix_norm_w=v_mix_norm_w, ffn_norm_w=v_ffn_norm_w, ret_w_in=v_ret_w_in,
             ret_gn_w=v_ret_gn_w, ret_w_out=v_ret_w_out, dn_w_in=v_dn_w_in, dn_conv_w=v_dn_conv_w,
             dn_a_log=v_dn_a_log, dn_dt_bias=v_dn_dt_bias, dn_norm_w=v_dn_norm_w, dn_w_out=v_dn_w_out,
             ffn_w_gate=v_ffn_w_gate, ffn_w_up=v_ffn_w_up, ffn_w_down=v_ffn_w_down, final_norm_w=v_final_norm_w)
    mx, my, mc = _mesh_pos()
    chip = 2 * mx + my

    sm_names = [n for n in SMALL if n in SMALL_SHARDED]
    sm_gathered = _unpack(_gather_small(_pack_lanes([w[n] for n in sm_names])), [w[n].shape for n in sm_names])
    full = {n: _join_cols(sm_gathered[i]) for i, n in enumerate(sm_names)}
    wts = {
        "meta_tokens": full["meta_tokens"], "mix_norm_w": mix_norm_w, "ffn_norm_w": ffn_norm_w,
        "ret_gn_w": ret_gn_w[0], "final_norm_w": final_norm_w, "dn_conv_w": full["dn_conv_w"][0],
        "dn_a_log": dn_a_log[0], "dn_dt_bias": dn_dt_bias[0], "dn_norm_w": full["dn_norm_w"][0],
    }
    idx = jnp.stack([mc, chip]).astype(jnp.int32)
    shards = {n: _halves(w[n].astype(MXU_DTYPE)) for n in BIG}
    loss_part, dh0, g, reduced = _train_step(x[0], loss_target[0], wts, shards, idx)
    seq = x.shape[1]
    grad_x = dh0[CHUNK:CHUNK + seq].reshape(x.shape)
    gsh = {}

    small_full_shapes = [g[n].shape for n in SMALL] + [(1,)]
    red = _unpack(_allreduce_small(_pack_lanes([g[n] for n in SMALL] + [loss_part[0, :1]])), small_full_shapes)
    loss = red[-1][0]
    for i, n in enumerate(SMALL):
        gn = red[i]
        if n in SMALL_SHARDED:
            width = w[n].shape[-1]
            gn = lax.dynamic_slice_in_dim(gn, chip * width, width, axis=gn.ndim - 1)
        gsh[n] = gn.reshape(w[n].shape)

    delta, new_m, new_v = {}, {}, {}
    for n in BIG:
        gsh[n], delta[n], new_m[n], new_v[n] = _adamw(w[n], reduced[n].reshape(w[n].shape), m[n], v[n],
                                                      "adamw_" + n)
    sm_local_shapes = [w[n].shape for n in SMALL]
    _, d_, m_, v_ = _adamw(*[_pack_lanes([t[n] for n in SMALL])[None] for t in (w, gsh, m, v)], "adamw_small")
    d_, m_, v_ = d_[0], m_[0], v_[0]
    for n, dd, mm, vv in zip(SMALL, _unpack(d_, sm_local_shapes), _unpack(m_, sm_local_shapes),
                             _unpack(v_, sm_local_shapes)):
        delta[n], new_m[n], new_v[n] = dd, mm, vv

    return (loss, grad_x, *[gsh[n] for n in ORDER], *[delta[n] for n in ORDER],
            *[new_m[n] for n in ORDER], *[new_v[n] for n in ORDER])
```

```python
import functools
import math

import jax
import jax.numpy as jnp
from jax import lax
from jax.experimental import pallas as pl
from jax.experimental.pallas import tpu as pltpu

F32 = jnp.float32
BF16 = jnp.bfloat16
MXU_DTYPE = BF16

D_MODEL = 1024
N_META = 16
CHUNK = 64
PAD = CHUNK - N_META
RMS_EPS = 1e-6
RET_HEADS, RET_DK, RET_DV = 4, 256, 512
RET_QK, RET_V = RET_HEADS * RET_DK, RET_HEADS * RET_DV
RET_IN = 2 * RET_QK + 2 * RET_V
ROPE_BASE = 10000.0
DN_HEADS, DN_DK, DN_DV = 8, 128, 256
DN_QK, DN_V = DN_HEADS * DN_DK, DN_HEADS * DN_DV
DN_CONV_CH = 2 * DN_QK + DN_V
DN_IN = DN_CONV_CH + DN_V + 2 * DN_HEADS
LANES = 128
DN_IN_USED = DN_CONV_CH + DN_V + LANES
DN_IN_PAD = DN_IN_USED + LANES
CONV_K = 4
FFN_HIDDEN = 2816
ADAM_LR, ADAM_B1, ADAM_B2, ADAM_EPS, ADAM_WD, ADAM_STEP = 0.001, 0.9, 0.999, 1e-08, 0.01, 10

ROW_ALIGN = 256
VMEM_LIMIT = 56 * 1024 * 1024
MESH = pl.DeviceIdType.MESH
ANY = pl.BlockSpec(memory_space=pl.ANY)
VMEM_SPEC = pl.BlockSpec(memory_space=pltpu.VMEM)
_HI = lax.Precision.HIGHEST


def _params(*sem):
    return pltpu.CompilerParams(dimension_semantics=sem, vmem_limit_bytes=VMEM_LIMIT)


def _dg(a, b, ca, cb, hi):
    dims = (((ca,), (cb,)), ((), ()))

    def dot(p, q):
        return lax.dot_general(p, q, dims, preferred_element_type=F32)

    if not hi:
        return dot(a.astype(MXU_DTYPE), b.astype(MXU_DTYPE))
    if MXU_DTYPE == F32:
        return lax.dot_general(a, b, dims, precision=_HI, preferred_element_type=F32)
    a_hi, b_hi = a.astype(MXU_DTYPE), b.astype(MXU_DTYPE)
    a_lo = (a - a_hi.astype(F32)).astype(MXU_DTYPE)
    b_lo = (b - b_hi.astype(F32)).astype(MXU_DTYPE)
    return dot(a_hi, b_hi) + (dot(a_hi, b_lo) + dot(a_lo, b_hi))


def _nn(a, b, hi=False):
    return _dg(a, b, 1, 0, hi)


def _nt(a, b, hi=False):
    return _dg(a, b, 1, 1, hi)


def _tn(a, b, hi=False):
    return _dg(a, b, 0, 0, hi)


def _iota(shape, dim):
    return lax.broadcasted_iota(jnp.int32, shape, dim)


def _valid_rows(first_row, rows, seq):
    r = first_row + _iota((rows, 1), 0)
    return ((r >= PAD) & (r < CHUNK + seq)).astype(F32)


def _rope(t, cs, sn):
    half = t.shape[-1] // 2
    t1, t2 = t[:, :half], t[:, half:]
    return jnp.concatenate([t1 * cs - t2 * sn, t1 * sn + t2 * cs], axis=1)


def _rope_bwd(d, cs, sn):
    half = d.shape[-1] // 2
    d1, d2 = d[:, :half], d[:, half:]
    return jnp.concatenate([d1 * cs + d2 * sn, d2 * cs - d1 * sn], axis=1)


def _col(x, idx):
    oh = (_iota((1, x.shape[1]), 1) == idx).astype(F32)
    return jnp.sum(x * oh, axis=1, keepdims=True)


def _row(x, idx):
    oh = (_iota((x.shape[0], 1), 0) == idx).astype(F32)
    return jnp.sum(x * oh, axis=0, keepdims=True)


def _shift_down(x, halo8, k):
    xr = pltpu.roll(x, k, 0)
    hr = pltpu.roll(halo8, k, 0)
    first = jnp.where(_iota((8, 1), 0) < k, hr, xr[0:8])
    return jnp.concatenate([first, xr[8:]], axis=0)


def _shift_up(x, next8, j):
    rows = x.shape[0]
    xr = pltpu.roll(x, rows - j, 0)
    nr = pltpu.roll(next8, 8 - j, 0)
    last = jnp.where(_iota((8, 1), 0) >= 8 - j, nr, xr[rows - 8:])
    return jnp.concatenate([xr[:rows - 8], last], axis=0)


def _gated_norm(o, gate, w):
    r = lax.rsqrt(jnp.mean(o * o, axis=-1, keepdims=True) + RMS_EPS)
    return o * r * w * (gate * jax.nn.sigmoid(gate))


def _gated_norm_bwd(dy, o, gate, w):
    r = lax.rsqrt(jnp.mean(o * o, axis=-1, keepdims=True) + RMS_EPS)
    nrm = o * r
    sg = jax.nn.sigmoid(gate)
    sl = gate * sg
    dgate = dy * nrm * w * (sg * (1.0 + gate * (1.0 - sg)))
    dn = dy * w * sl
    dw = jnp.sum(dy * nrm * sl, axis=0, keepdims=True)
    do = r * (dn - nrm * jnp.mean(dn * nrm, axis=-1, keepdims=True))
    return do, dgate, dw


def _softplus(z):
    return jnp.maximum(z, 0.0) + jnp.log(1.0 + jnp.exp(-jnp.abs(z)))


def _row_tile(rows, cap=768):
    for t in (768, 512, 256, 128, 64, 32, 16, 8):
        if t <= cap and rows % t == 0:
            return t
    return rows


TILE_BUDGET = 44 * 1024 * 1024


def _fit_rows(rows, row_bytes, fixed_bytes, value_row_bytes):
    best = None
    for t in range(LANES, rows + 1, LANES):
        if rows % t == 0 and 2 * (row_bytes * t + fixed_bytes) + value_row_bytes * t <= TILE_BUDGET:
            best = t
    return best or _row_tile(rows, 256)


def _div_tile(n, cap, mult):
    best = None
    for t in range(mult, min(cap, n) + 1, mult):
        if n % t == 0:
            best = t
    return best or n


def _col_tile(cols, cap=1536):
    best = None
    for t in range(LANES, min(cap, cols) + 1, LANES):
        if cols % t == 0:
            best = t
    return best or cols


def _rms_fwd(h, w, name):
    rows, d = h.shape
    tm = _row_tile(rows)

    def body(h_ref, w_ref, o_ref):
        x = h_ref[...]
        r = lax.rsqrt(jnp.mean(x * x, axis=-1, keepdims=True) + RMS_EPS)
        o_ref[...] = (x * r * w_ref[...]).astype(o_ref.dtype)

    return pl.pallas_call(
        body, grid=(rows // tm,),
        in_specs=[pl.BlockSpec((tm, d), lambda i: (i, 0)), pl.BlockSpec((1, d), lambda i: (0, 0))],
        out_specs=pl.BlockSpec((tm, d), lambda i: (i, 0)),
        out_shape=jax.ShapeDtypeStruct((rows, d), BF16), name=name,
        compiler_params=_params("parallel"))(h, w.reshape(1, d))


def _gmm_rms(name, grid, args, in_specs, row_spec, fn, h, w, resid, row_axis, red_axis=None, ride=None):
    m, d = h.shape
    n_in = len(args)
    vec = pl.BlockSpec((1, d), lambda *g: (0, 0))

    def body(*refs):
        ins = refs[:n_in]
        h_ref, w_ref, r_ref, dh_ref, dw_ref = refs[n_in:]
        part = fn(*ins)
        row = pl.program_id(row_axis)

        def finish(dy):
            x = h_ref[...]
            r = lax.rsqrt(jnp.mean(x * x, axis=-1, keepdims=True) + RMS_EPS)
            xh = x * r
            dxh = dy * w_ref[...]
            dh_ref[...] = r_ref[...] + r * (dxh - xh * jnp.mean(dxh * xh, axis=-1, keepdims=True))
            dwp = jnp.sum(dy * xh, axis=0, keepdims=True)

            @pl.when(row == 0)
            def _():
                dw_ref[...] = dwp

            @pl.when(row > 0)
            def _():
                dw_ref[...] += dwp

        if red_axis is None:
            finish(part)
            return
        k = pl.program_id(red_axis)

        @pl.when(k == 0)
        def _():
            dh_ref[...] = part

        @pl.when(k > 0)
        def _():
            dh_ref[...] += part

        @pl.when(k == grid[red_axis] - 1)
        def _():
            finish(dh_ref[...])

    res, rode = _pcall(body, list(args) + [h, w.reshape(1, d), resid], grid=grid,
                       in_specs=list(in_specs) + [row_spec, vec, row_spec], out_specs=[row_spec, vec],
                       out_shape=[jax.ShapeDtypeStruct((m, d), F32), jax.ShapeDtypeStruct((1, d), F32)],
                       name=name, sem=("arbitrary",) * len(grid), ride=ride)
    return res if ride is None else (res, rode)


def _final_loss(h, w, tgt, seq, name):
    rows, d = h.shape
    tm = _row_tile(rows)

    def body(h_ref, w_ref, t_ref, dh_ref, dw_ref, loss_ref):
        i = pl.program_id(0)
        r_idx = i * tm + _iota((tm, 1), 0)
        m = ((r_idx >= CHUNK) & (r_idx < CHUNK + seq)).astype(F32)
        x = h_ref[...]
        wv = w_ref[...]
        r = lax.rsqrt(jnp.mean(x * x, axis=-1, keepdims=True) + RMS_EPS)
        xh = x * r
        err = (xh * wv - t_ref[...]) * m
        lpart = 0.5 * jnp.sum(jnp.mean(err * err, axis=-1, keepdims=True), axis=0, keepdims=True)
        dyv = err * (1.0 / d)
        dxh = dyv * wv
        dh_ref[...] = r * (dxh - xh * jnp.mean(dxh * xh, axis=-1, keepdims=True))
        part = jnp.sum(dyv * xh, axis=0, keepdims=True)

        @pl.when(i == 0)
        def _():
            dw_ref[...] = part
            loss_ref[...] = jnp.broadcast_to(lpart, loss_ref.shape)

        @pl.when(i > 0)
        def _():
            dw_ref[...] += part
            loss_ref[...] += jnp.broadcast_to(lpart, loss_ref.shape)

    blk = pl.BlockSpec((tm, d), lambda i: (i, 0))
    vec = pl.BlockSpec((1, d), lambda i: (0, 0))
    return pl.pallas_call(
        body, grid=(rows // tm,), in_specs=[blk, vec, blk],
        out_specs=[blk, vec, pl.BlockSpec((1, LANES), lambda i: (0, 0))],
        out_shape=[jax.ShapeDtypeStruct((rows, d), F32), jax.ShapeDtypeStruct((1, d), F32),
                   jax.ShapeDtypeStruct((1, LANES), F32)],
        name=name, compiler_params=_params("arbitrary"))(h, w.reshape(1, d), tgt)


def _isz(x):
    return jnp.dtype(x.dtype).itemsize


def _mm(a, b, *, mode, name, out_dtype=F32, resid=None, col_cap=1536, ride=None):
    if mode == "tn":
        m, k = a.shape
        n = b.shape[1]
        tn = _col_tile(n, col_cap)
        tm = _fit_rows(m, k * _isz(a) + tn * _isz(b), (3 * k * tn * 4) // 2, 2 * (k + tn))

        def body_tn(a_ref, b_ref, o_ref):
            i = pl.program_id(1)
            part = _tn(a_ref[...], b_ref[...])

            @pl.when(i == 0)
            def _():
                o_ref[...] = part

            @pl.when(i > 0)
            def _():
                o_ref[...] += part

        return pl.pallas_call(
            body_tn, grid=(n // tn, m // tm),
            in_specs=[pl.BlockSpec((tm, k), lambda j, i: (i, 0)),
                      pl.BlockSpec((tm, tn), lambda j, i: (i, j))],
            out_specs=pl.BlockSpec((k, tn), lambda j, i: (0, j)),
            out_shape=jax.ShapeDtypeStruct((k, n), F32), name=name,
            compiler_params=_params("parallel", "arbitrary"))(a, b)

    m, ka = a.shape
    n = b.shape[1] if mode == "nn" else b.shape[0]
    has_resid = resid is not None
    tn = _col_tile(n, col_cap)
    tm = _fit_rows(m, ka * _isz(a) + tn * (jnp.dtype(out_dtype).itemsize + (4 if has_resid else 0)),
                   ka * tn * _isz(b), 2 * ka + 8 * tn)

    def body(*refs):
        if has_resid:
            a_ref, b_ref, r_ref, o_ref = refs
        else:
            a_ref, b_ref, o_ref = refs
        acc = _nn(a_ref[...], b_ref[...]) if mode == "nn" else _nt(a_ref[...], b_ref[...])
        if has_resid:
            acc = acc + r_ref[...]
        o_ref[...] = acc.astype(o_ref.dtype)

    b_spec = (pl.BlockSpec((b.shape[0], tn), lambda j, i: (0, j)) if mode == "nn"
              else pl.BlockSpec((tn, b.shape[1]), lambda j, i: (j, 0)))
    o_spec = pl.BlockSpec((tm, tn), lambda j, i: (i, j))
    in_specs = [pl.BlockSpec((tm, ka), lambda j, i: (i, 0)), b_spec]
    args = [a, b]
    if has_resid:
        in_specs.append(o_spec)
        args.append(resid)
    res, rode = _pcall(body, args, grid=(n // tn, m // tm), in_specs=in_specs, out_specs=[o_spec],
                       out_shape=[jax.ShapeDtypeStruct((m, n), out_dtype)], name=name,
                       sem=("parallel", "parallel"), ride=ride)
    return res[0] if ride is None else (res[0], rode)


N_SHARD = 4


def _gmm(name, grid, args, in_specs, out_specs, out_shape, fn, red_axis=None, init_arg=None, aliases=None,
         ride=None):
    n_in = len(args)
    single = not isinstance(out_shape, (list, tuple))
    out_specs = [out_specs] if single else list(out_specs)
    out_shape = [out_shape] if single else list(out_shape)

    def body(*refs):
        _gmm_step(fn, refs[:n_in], refs[n_in:], red_axis, init_arg)

    sem = tuple("arbitrary" if ax == red_axis else "parallel" for ax in range(len(grid)))
    res, rode = _pcall(body, args, grid=grid, in_specs=in_specs, out_specs=out_specs, out_shape=out_shape,
                       name=name, sem=sem, aliases=aliases, ride=ride)
    ours = res[0] if single else res
    return ours if ride is None else (ours, rode)


def _gmm_step(fn, ins, outs, red_axis, init_arg):
    parts = fn(*ins)
    if red_axis is None:
        for o_ref, p in zip(outs, parts):
            o_ref[...] = p.astype(o_ref.dtype)
        return
    k = pl.program_id(red_axis)

    @pl.when(k == 0)
    def _():
        for idx, (o_ref, p) in enumerate(zip(outs, parts)):
            o_ref[...] = p + ins[init_arg][...] if (idx == 0 and init_arg is not None) else p

    @pl.when(k > 0)
    def _():
        for o_ref, p in zip(outs, parts):
            o_ref[...] += p


def _ride_body(ride, grid, n_in, n_out, n_scratch, body):
    n_rin, n_rout = len(ride.arrays), len(ride.out_shape)
    nsteps = math.prod(grid)

    def wrapped(*refs):
        ins = refs[:n_in]
        r_ins = refs[n_in:n_in + n_rin]
        o0 = n_in + n_rin
        outs = refs[o0:o0 + n_out]
        r_outs = refs[o0 + n_out:o0 + n_out + n_rout]
        s0 = o0 + n_out + n_rout
        scratch = refs[s0:s0 + n_scratch]
        send_sems, recv_sems = refs[-2:]
        step = pl.program_id(0)
        for ax in range(1, len(grid)):
            step = step * grid[ax] + pl.program_id(ax)
        ride.emit(step, nsteps, r_ins, r_outs, send_sems, recv_sems, before=True)
        body(*ins, *outs, *scratch)
        ride.emit(step, nsteps, r_ins, r_outs, send_sems, recv_sems, before=False)

    return wrapped


def _pcall(body, args, *, grid, in_specs, out_specs, out_shape, name, sem, scratch=(), aliases=None, ride=None):
    if ride is None:
        res = pl.pallas_call(body, grid=grid, in_specs=list(in_specs), out_specs=list(out_specs),
                             out_shape=list(out_shape), scratch_shapes=list(scratch), name=name,
                             input_output_aliases=aliases or {}, compiler_params=_params(*sem))(*args)
        return res, None
    n_in, n_out = len(args), len(out_shape)
    res = pl.pallas_call(
        _ride_body(ride, grid, n_in, n_out, len(scratch), body), grid=grid,
        in_specs=list(in_specs) + ride.in_specs, out_specs=list(out_specs) + ride.out_specs,
        out_shape=list(out_shape) + ride.out_shape, scratch_shapes=list(scratch) + ride.scratch, name=name,
        input_output_aliases=aliases or {},
        compiler_params=_params(*(("arbitrary",) * len(grid))))(*args, *ride.arrays)
    return res[:n_out], res[n_out:]


def _mm_cols(a, ws, name, ride=None):
    m, k = a.shape
    n = ws.shape[2]
    tm = _fit_rows(m, k * _isz(a) + n * 4, k * n * _isz(ws), 4 * n)
    return _gmm(name, (N_SHARD, m // tm), [a, ws],
                [pl.BlockSpec((tm, k), lambda j, i: (i, 0)), pl.BlockSpec((None, k, n), lambda j, i: (j, 0, 0))],
                pl.BlockSpec((tm, n), lambda j, i: (i, j)), jax.ShapeDtypeStruct((m, N_SHARD * n), F32),
                lambda a_ref, w_ref: (_nn(a_ref[...], w_ref[...]),), ride=ride)


def _mm_cols_t_rms(d, ws, h, w, resid, name, ride=None):
    m = d.shape[0]
    _, k, n = ws.shape
    tm = _fit_rows(m, n * _isz(d) + 3 * k * 4, k * n * _isz(ws), 16 * k)
    return _gmm_rms(name, (m // tm, N_SHARD), [d, ws],
                    [pl.BlockSpec((tm, n), lambda i, j: (i, j)), pl.BlockSpec((None, k, n), lambda i, j: (j, 0, 0))],
                    pl.BlockSpec((tm, k), lambda i, j: (i, 0)),
                    lambda d_ref, w_ref: _nt(d_ref[...], w_ref[...]), h, w, resid, 0, red_axis=1, ride=ride)


def _mm_nt_rms(a, b, h, w, resid, name, ride=None):
    m, n = a.shape
    k = b.shape[0]
    tm = _fit_rows(m, n * _isz(a) + 3 * k * 4, k * n * _isz(b), 16 * k)
    return _gmm_rms(name, (m // tm,), [a, b],
                    [pl.BlockSpec((tm, n), lambda i: (i, 0)), pl.BlockSpec((k, n), lambda i: (0, 0))],
                    pl.BlockSpec((tm, k), lambda i: (i, 0)),
                    lambda a_ref, b_ref: _nt(a_ref[...], b_ref[...]), h, w, resid, 0, ride=ride)


def _mm_cols_grad(a, d, name):
    m, k = a.shape
    n = d.shape[1] // N_SHARD
    tm = _fit_rows(m, k * _isz(a) + n * _isz(d), (3 * k * n * 4) // 2, 2 * (k + n))
    return _gmm(name, (N_SHARD, m // tm), [a, d],
                [pl.BlockSpec((tm, k), lambda j, i: (i, 0)), pl.BlockSpec((tm, n), lambda j, i: (i, j))],
                pl.BlockSpec((None, k, n), lambda j, i: (j, 0, 0)), jax.ShapeDtypeStruct((N_SHARD, k, n), F32),
                lambda a_ref, d_ref: (_tn(a_ref[...], d_ref[...]),), red_axis=1)


def _ffn_up(hn, wg, wu, layer, name):
    m, k = hn.shape
    n = wg.shape[3]
    tm = _fit_rows(m, k * _isz(hn) + 3 * n * jnp.dtype(BF16).itemsize, 2 * k * n * _isz(wg), 16 * n)

    def fn(a_ref, wg_ref, wu_ref):
        a = a_ref[...]
        g = _nn(a, wg_ref[...])
        u = _nn(a, wu_ref[...])
        return g, u, g * jax.nn.sigmoid(g) * u

    w_spec = pl.BlockSpec((None, None, k, n), lambda j, i: (j, layer, 0, 0))
    o_spec = pl.BlockSpec((None, tm, n), lambda j, i: (j, i, 0))
    out = jax.ShapeDtypeStruct((N_SHARD, m, n), BF16)
    return _gmm(name, (N_SHARD, m // tm), [hn, wg, wu],
                [pl.BlockSpec((tm, k), lambda j, i: (i, 0)), w_spec, w_spec],
                [o_spec, o_spec, o_spec], [out, out, out], fn)


def _ffn_down(act, wd, resid, layer, name):
    _, m, n = act.shape
    d = wd.shape[3]
    tm = _fit_rows(m, N_SHARD * n * _isz(act) + 2 * d * 4, N_SHARD * n * d * _isz(wd), 8 * d)

    def fn(a_ref, w_ref, r_ref):
        acc = r_ref[...]
        for j in range(N_SHARD):
            acc = acc + _nn(a_ref[j], w_ref[j])
        return (acc,)

    row = pl.BlockSpec((tm, d), lambda i: (i, 0))
    return _gmm(name, (m // tm,), [act, wd, resid],
                [pl.BlockSpec((N_SHARD, tm, n), lambda i: (0, i, 0)),
                 pl.BlockSpec((N_SHARD, None, n, d), lambda i: (0, layer, 0, 0)), row],
                row, jax.ShapeDtypeStruct((m, d), F32), fn)


def _ffn_down_bwd(dh, wd, g, u, layer, name, ride=None):
    m, d = dh.shape
    n = wd.shape[2]
    tm = _fit_rows(m, d * _isz(dh) + 4 * n * jnp.dtype(BF16).itemsize, n * d * _isz(wd), 2 * d + 24 * n)

    def fn(dh_ref, wd_ref, g_ref, u_ref):
        dact = _nt(dh_ref[...], wd_ref[...])
        gv = g_ref[...].astype(F32)
        uv = u_ref[...].astype(F32)
        sg = jax.nn.sigmoid(gv)
        return dact * uv * (sg * (1.0 + gv * (1.0 - sg))), dact * gv * sg

    o_spec = pl.BlockSpec((None, tm, n), lambda j, i: (j, i, 0))
    out = jax.ShapeDtypeStruct((N_SHARD, m, n), BF16)
    return _gmm(name, (N_SHARD, m // tm), [dh, wd, g, u],
                [pl.BlockSpec((tm, d), lambda j, i: (i, 0)),
                 pl.BlockSpec((None, None, n, d), lambda j, i: (j, layer, 0, 0)), o_spec, o_spec],
                [o_spec, o_spec], [out, out], fn, ride=ride)


def _ffn_up_bwd(dg, du, wg, wu, layer, h, w, resid, name):
    _, m, n = dg.shape
    k = wg.shape[2]
    tm = _fit_rows(m, 2 * N_SHARD * n * _isz(dg) + 3 * k * 4, 2 * N_SHARD * k * n * _isz(wg), 16 * k)

    def fn(dg_ref, du_ref, wg_ref, wu_ref):
        acc = _nt(dg_ref[0], wg_ref[0]) + _nt(du_ref[0], wu_ref[0])
        for j in range(1, N_SHARD):
            acc = acc + _nt(dg_ref[j], wg_ref[j]) + _nt(du_ref[j], wu_ref[j])
        return acc

    d_spec = pl.BlockSpec((N_SHARD, tm, n), lambda i: (0, i, 0))
    w_spec = pl.BlockSpec((N_SHARD, None, k, n), lambda i: (0, layer, 0, 0))
    return _gmm_rms(name, (m // tm,), [dg, du, wg, wu], [d_spec, d_spec, w_spec, w_spec],
                    pl.BlockSpec((tm, k), lambda i: (i, 0)), fn, h, w, resid, 0)


def _ffn_wgrad(lhs, rhs_list, layer, layers, prev, lhs_sharded, name):
    if lhs_sharded:
        _, m, k = lhs.shape
        n = rhs_list[0].shape[1]
    else:
        m, k = lhs.shape
        n = rhs_list[0].shape[2]
    n_out = len(rhs_list)
    tm = _fit_rows(m, k * _isz(lhs) + n_out * n * _isz(rhs_list[0]), (3 * n_out * k * n * 4) // 2,
                   2 * (k + n_out * n))
    sh = pl.BlockSpec((None, tm, k if lhs_sharded else n), lambda j, i: (j, i, 0))
    fl = pl.BlockSpec((tm, n if lhs_sharded else k), lambda j, i: (i, 0))
    n_out = len(rhs_list)
    args = [lhs] + list(rhs_list)
    in_specs = [sh if lhs_sharded else fl] + [fl if lhs_sharded else sh] * n_out
    aliases = None
    if prev is not None:
        aliases = {len(args) + t: t for t in range(n_out)}
        args = args + list(prev)
        in_specs = in_specs + [ANY] * n_out

    def fn(l_ref, *rest):
        lv = l_ref[...]
        return tuple(_tn(lv, r_ref[...]) for r_ref in rest[:n_out])

    o_spec = pl.BlockSpec((None, None, k, n), lambda j, i: (j, layer, 0, 0))
    out = jax.ShapeDtypeStruct((N_SHARD, layers, k, n), F32)
    return _gmm(name, (N_SHARD, m // tm), args, in_specs, [o_spec] * n_out, [out] * n_out, fn,
                red_axis=1, aliases=aliases)


def _ret_consts():
    log_gamma = jnp.log1p(-jnp.exp2(-5.0 - jnp.arange(RET_HEADS, dtype=F32)))
    idx = jnp.arange(CHUNK, dtype=F32)
    rel = idx[:, None] - idx[None, :]
    dmask = jnp.where((rel >= 0)[None], jnp.exp(log_gamma[:, None, None] * jnp.maximum(rel, 0.0)), 0.0)
    xi = jnp.exp(log_gamma[:, None] * (idx[None, :] + 1.0))[:, :, None]
    zeta = jnp.exp(log_gamma[:, None] * (CHUNK - 1.0 - idx[None, :]))[:, :, None]
    gamma_c = jnp.exp(log_gamma * CHUNK)
    wide = (RET_HEADS, CHUNK, RET_DK)
    return dmask, jnp.broadcast_to(xi, wide), jnp.broadcast_to(zeta, wide), gamma_c


def _rope_tables(rows):
    half = RET_DK // 2
    inv_freq = ROPE_BASE ** (-jnp.arange(half, dtype=F32) / half)
    pos = (jnp.arange(rows) - PAD).astype(F32)
    ang = pos[:, None] * inv_freq[None, :]
    return jnp.cos(ang), jnp.sin(ang)


def _ret_specs(order):
    return [pl.BlockSpec((CHUNK, RET_QK), lambda n: (order(n), 0)),
            pl.BlockSpec((CHUNK, RET_QK), lambda n: (order(n), 1)),
            pl.BlockSpec((CHUNK, RET_V), lambda n: (order(n), 1)),
            pl.BlockSpec((CHUNK, RET_V), lambda n: (order(n), 2))]


def _ret_const_specs():
    return [pl.BlockSpec((RET_HEADS, CHUNK, CHUNK), lambda n: (0, 0, 0)),
            pl.BlockSpec((RET_HEADS, CHUNK, RET_DK), lambda n: (0, 0, 0)),
            pl.BlockSpec((RET_HEADS, CHUNK, RET_DK), lambda n: (0, 0, 0)),
            pl.BlockSpec((1, RET_DV), lambda n: (0, 0))]


def _ret_fwd(proj, cos, sin, consts, gn_w, seq, ride=None):
    rows = proj.shape[0]
    nc = rows // CHUNK
    dmask, xi, zeta, gamma_c = consts

    def body(gam_ref, q_ref, k_ref, v_ref, g_ref, cos_ref, sin_ref, dm_ref, xi_ref, ze_ref, gn_ref,
             o_ref, y_ref, ss_ref, s_ref):
        n = pl.program_id(0)

        @pl.when(n == 0)
        def _():
            s_ref[...] = jnp.zeros_like(s_ref)

        cs, sn = cos_ref[...], sin_ref[...]
        kscale = _valid_rows(n * CHUNK, CHUNK, seq) * (RET_DK ** -0.5)
        gn = gn_ref[...]
        hs = range(RET_HEADS)
        qk_cols = [slice(h * RET_DK, (h + 1) * RET_DK) for h in hs]
        v_cols = [slice(h * RET_DV, (h + 1) * RET_DV) for h in hs]
        qr_l = [_rope(q_ref[:, c], cs, sn) for c in qk_cols]
        kr_l = [_rope(k_ref[:, c], cs, sn) * kscale for c in qk_cols]
        v_l = [v_ref[:, c] for c in v_cols]
        s_l = [s_ref[h] for h in hs]
        sc_l = [_nt(qr, kr) * dm_ref[h] for h, (qr, kr) in enumerate(zip(qr_l, kr_l))]
        o_l = [_nn(sc_l[h], v_l[h]) + _nn(qr_l[h] * xi_ref[h], s_l[h]) for h in hs]
        for h in hs:
            ss_ref[0, h] = s_l[h].astype(ss_ref.dtype)
            s_ref[h] = gam_ref[h] * s_l[h] + _tn(kr_l[h] * ze_ref[h], v_l[h])
            o_ref[:, v_cols[h]] = o_l[h]
            y_ref[:, v_cols[h]] = _gated_norm(o_l[h], g_ref[:, v_cols[h]], gn).astype(y_ref.dtype)

    fwd = lambda n: n
    row128 = pl.BlockSpec((CHUNK, RET_DK // 2), lambda n: (n, 0))
    row_v = pl.BlockSpec((CHUNK, RET_V), lambda n: (n, 0))
    res, rode = _pcall(
        body, [gamma_c, proj, proj, proj, proj, cos, sin, dmask, xi, zeta, gn_w.reshape(1, RET_DV)],
        grid=(nc,),
        in_specs=[pl.BlockSpec(memory_space=pltpu.SMEM)] + _ret_specs(fwd) + [row128, row128]
        + _ret_const_specs(),
        out_specs=[row_v, row_v,
                   pl.BlockSpec((1, RET_HEADS, RET_DK, RET_DV), lambda n: (n, 0, 0, 0))],
        out_shape=[jax.ShapeDtypeStruct((rows, RET_V), F32), jax.ShapeDtypeStruct((rows, RET_V), BF16),
                   jax.ShapeDtypeStruct((nc, RET_HEADS, RET_DK, RET_DV), BF16)],
        scratch=[pltpu.VMEM((RET_HEADS, RET_DK, RET_DV), F32)], name="ret_fwd", sem=("arbitrary",), ride=ride)
    return res if ride is None else (res, rode)


def _ret_bwd(proj, o, dy, states, cos, sin, consts, gn_w, seq, ride=None):
    rows = proj.shape[0]
    nc = rows // CHUNK
    dmask, xi, zeta, gamma_c = consts

    def body(gam_ref, q_ref, k_ref, v_ref, g_ref, o_ref, dy_ref, ss_ref, cos_ref, sin_ref,
             dm_ref, xi_ref, ze_ref, gn_ref, dp_ref, dgn_ref, ds_ref):
        n = pl.program_id(0)

        @pl.when(n == 0)
        def _():
            ds_ref[...] = jnp.zeros_like(ds_ref)
            dgn_ref[...] = jnp.zeros_like(dgn_ref)

        cs, sn = cos_ref[...], sin_ref[...]
        kscale = _valid_rows((nc - 1 - n) * CHUNK, CHUNK, seq) * (RET_DK ** -0.5)
        gn = gn_ref[...]
        dgn = jnp.zeros((1, RET_DV), F32)
        hs = range(RET_HEADS)
        qk_cols = [slice(h * RET_DK, (h + 1) * RET_DK) for h in hs]
        v_cols = [slice(h * RET_DV, (h + 1) * RET_DV) for h in hs]
        qr_l = [_rope(q_ref[:, c], cs, sn) for c in qk_cols]
        kr_l = [_rope(k_ref[:, c], cs, sn) * kscale for c in qk_cols]
        v_l = [v_ref[:, c] for c in v_cols]
        s_l = [ss_ref[0, h] for h in hs]
        ds_l = [ds_ref[h] for h in hs]
        gnb = [_gated_norm_bwd(dy_ref[:, c], o_ref[:, c], g_ref[:, c], gn) for c in v_cols]
        do_l = [x[0] for x in gnb]
        sc_l = [_nt(qr_l[h], kr_l[h]) * dm_ref[h] for h in hs]
        dsc_l = [_nt(do_l[h], v_l[h]) * dm_ref[h] for h in hs]
        dv_l = [_tn(sc_l[h], do_l[h]) + _nn(kr_l[h] * ze_ref[h], ds_l[h]) for h in hs]
        dqr_l = [_nn(dsc_l[h], kr_l[h]) + _nt(do_l[h], s_l[h]) * xi_ref[h] for h in hs]
        dkr_l = [_tn(dsc_l[h], qr_l[h]) + _nt(v_l[h], ds_l[h]) * ze_ref[h] for h in hs]
        for h in hs:
            dgn = dgn + gnb[h][2]
            ds_ref[h] = gam_ref[h] * ds_l[h] + _tn(qr_l[h] * xi_ref[h], do_l[h])
            dp_ref[:, qk_cols[h]] = _rope_bwd(dqr_l[h], cs, sn).astype(dp_ref.dtype)
            dp_ref[:, RET_QK + h * RET_DK:RET_QK + (h + 1) * RET_DK] = (
                _rope_bwd(dkr_l[h] * kscale, cs, sn).astype(dp_ref.dtype))
            dp_ref[:, 2 * RET_QK + h * RET_DV:2 * RET_QK + (h + 1) * RET_DV] = dv_l[h].astype(dp_ref.dtype)
            dp_ref[:, 2 * RET_QK + RET_V + h * RET_DV:2 * RET_QK + RET_V + (h + 1) * RET_DV] = (
                gnb[h][1].astype(dp_ref.dtype))
        dgn_ref[...] += dgn

    rev = lambda n: nc - 1 - n
    row128 = pl.BlockSpec((CHUNK, RET_DK // 2), lambda n: (rev(n), 0))
    row_v = pl.BlockSpec((CHUNK, RET_V), lambda n: (rev(n), 0))
    res, rode = _pcall(
        body, [gamma_c, proj, proj, proj, proj, o, dy, states, cos, sin, dmask, xi, zeta,
               gn_w.reshape(1, RET_DV)],
        grid=(nc,),
        in_specs=[pl.BlockSpec(memory_space=pltpu.SMEM)] + _ret_specs(rev) + [
            row_v, row_v, pl.BlockSpec((1, RET_HEADS, RET_DK, RET_DV), lambda n: (rev(n), 0, 0, 0)),
            row128, row128] + _ret_const_specs(),
        out_specs=[pl.BlockSpec((CHUNK, RET_IN), lambda n: (rev(n), 0)),
                   pl.BlockSpec((1, RET_DV), lambda n: (0, 0))],
        out_shape=[jax.ShapeDtypeStruct((rows, RET_IN), BF16), jax.ShapeDtypeStruct((1, RET_DV), F32)],
        scratch=[pltpu.VMEM((RET_HEADS, RET_DK, RET_DV), F32)], name="ret_bwd", sem=("arbitrary",), ride=ride)
    return res if ride is None else (res, rode)


GATE_COL = DN_CONV_CH // DN_V
BA_COL = (DN_CONV_CH + DN_V) // LANES
BETA_LANE, DECAY_LANE = 0, DN_HEADS
INV_SHIFT = 4
INV_SQUARINGS = INV_SHIFT - 1
assert CHUNK == 4 << INV_SHIFT


def _dn_in_specs(order):
    return [pl.BlockSpec((CHUNK, DN_CONV_CH), lambda n: (order(n), 0)),
            pl.BlockSpec((8, DN_CONV_CH), lambda n: (jnp.maximum(order(n) * (CHUNK // 8) - 1, 0), 0)),
            pl.BlockSpec((CHUNK, DN_V), lambda n: (order(n), GATE_COL)),
            pl.BlockSpec((CHUNK, LANES), lambda n: (order(n), BA_COL)),
            pl.BlockSpec((CONV_K, 1, DN_CONV_CH), lambda n: (0, 0, 0)),
            pl.BlockSpec((1, LANES), lambda n: (0, 0)),
            pl.BlockSpec((1, LANES), lambda n: (0, 0)),
            pl.BlockSpec((1, DN_DV), lambda n: (0, 0))]


def _dn_front(c, seq, x_ref, halo_ref, ba_ref, cw_ref, al_ref, dt_ref):
    valid = _valid_rows(c * CHUNK, CHUNK, seq)
    xin = x_ref[...] * valid
    halo = halo_ref[...] * _valid_rows(c * CHUNK - 8, 8, seq)
    x_sh = [xin] + [_shift_down(xin, halo, k) for k in range(1, CONV_K)]
    yc = x_sh[0] * cw_ref[CONV_K - 1]
    for k in range(1, CONV_K):
        yc = yc + x_sh[k] * cw_ref[CONV_K - 1 - k]
    sgc = jax.nn.sigmoid(yc)
    ba = ba_ref[...]
    sig = jax.nn.sigmoid(ba)
    beta = sig * valid
    z = ba + dt_ref[...]
    eal = jnp.exp(al_ref[...])
    g = -eal * _softplus(z) * valid
    ri, ci = _iota((CHUNK, CHUNK), 0), _iota((CHUNK, CHUNK), 1)
    lower = (ri >= ci).astype(F32)
    upper = (ri <= ci).astype(F32)
    eye = (ri == ci).astype(F32)
    gam = _nn(lower, g, hi=True)
    gam_t = _tn(g, upper, hi=True)
    return dict(valid=valid, x_sh=x_sh, yc=yc, sgc=sgc, act=yc * sgc, sig=sig, beta=beta, z=z,
                eal=eal, g=g, gam=gam, gam_t=gam_t, ri=ri, ci=ci, upper=upper, eye=eye)


def _dn_head(f, h):
    act = f["act"]
    q_raw = act[:, h * DN_DK:(h + 1) * DN_DK]
    k_raw = act[:, DN_QK + h * DN_DK:DN_QK + (h + 1) * DN_DK]
    v = act[:, 2 * DN_QK + h * DN_DV:2 * DN_QK + (h + 1) * DN_DV]
    rq = lax.rsqrt(jnp.sum(q_raw * q_raw, axis=-1, keepdims=True) + RMS_EPS)
    rk = lax.rsqrt(jnp.sum(k_raw * k_raw, axis=-1, keepdims=True) + RMS_EPS)
    qh = q_raw * rq
    kn = k_raw * rk
    gam_c = _col(f["gam"], DECAY_LANE + h)
    gam_r = _row(f["gam_t"], DECAY_LANE + h)
    bc = _col(f["beta"], BETA_LANE + h)
    diff = gam_c - gam_r
    decay = jnp.where(f["ri"] >= f["ci"], jnp.exp(jnp.minimum(diff, 0.0)), 0.0)
    glast = jnp.sum(gam_r * (_iota((1, CHUNK), 1) == CHUNK - 1).astype(F32), axis=1, keepdims=True)
    return dict(rq=rq, rk=rk, qh=qh, qn=qh * (DN_DK ** -0.5), kn=kn, v=v, gam_c=gam_c, gam_r=gam_r,
                bc=bc, diff=diff, decay=decay, egam=jnp.exp(gam_c), glast=glast,
                eglast=jnp.exp(glast), ekd=jnp.exp(glast - gam_c))


def _dn_fwd(proj, conv_w, alog, dtb, norm_w, seq):
    rows = proj.shape[0]
    nc = rows // CHUNK

    def body(x_ref, halo_ref, gate_ref, ba_ref, cw_ref, al_ref, dt_ref, nw_ref,
             o_ref, y_ref, ss_ref, t_ref, s_ref):
        n = pl.program_id(0)

        @pl.when(n == 0)
        def _():
            s_ref[...] = jnp.zeros_like(s_ref)

        f = _dn_front(n, seq, x_ref, halo_ref, ba_ref, cw_ref, al_ref, dt_ref)
        ri, ci = f["ri"], f["ci"]
        eye = f["eye"]
        diag_m = (jnp.right_shift(ri, INV_SHIFT) == jnp.right_shift(ci, INV_SHIFT)).astype(F32)
        half_m = (jnp.right_shift(ri, INV_SHIFT + 1) == jnp.right_shift(ci, INV_SHIFT + 1)).astype(F32)
        nw = nw_ref[...]
        heads = [_dn_head(f, h) for h in range(DN_HEADS)]
        a_all = [jnp.where(ri > ci, hd["bc"] * _nt(hd["kn"], hd["kn"]) * hd["decay"], 0.0) for hd in heads]
        b_all = [a * diag_m for a in a_all]
        t_all = [eye - b for b in b_all]
        for _ in range(INV_SQUARINGS):
            b_all = [_nn(b, b, hi=True) for b in b_all]
            t_all = [t + _nn(t, b, hi=True) for t, b in zip(t_all, b_all)]
        for off_m in (half_m - diag_m, 1.0 - half_m):
            x_all = [_nn(a * off_m, t, hi=True) for a, t in zip(a_all, t_all)]
            t_all = [t - _nn(t, x, hi=True) for t, x in zip(t_all, x_all)]
        u_all = [_nn(t, hd["v"] * hd["bc"], hi=True) for t, hd in zip(t_all, heads)]
        w_all = [_nn(t, hd["kn"] * (hd["bc"] * hd["egam"]), hi=True) for t, hd in zip(t_all, heads)]
        for h in range(DN_HEADS):
            hd = heads[h]
            v_cols = slice(h * DN_DV, (h + 1) * DN_DV)
            t_ref[0, h] = t_all[h]
            s = s_ref[h]
            ss_ref[0, h] = s
            u, w = u_all[h], w_all[h]
            v_new = u - _nn(w, s)
            qk = _nt(hd["qn"], hd["kn"]) * hd["decay"]
            o = _nn(hd["qn"] * hd["egam"], s) + _nn(qk, v_new)
            s_ref[h] = s * hd["eglast"] + _tn(hd["kn"] * hd["ekd"], v_new)
            o_ref[:, v_cols] = o
            y_ref[:, v_cols] = _gated_norm(o, gate_ref[:, v_cols], nw).astype(y_ref.dtype)

    fwd = lambda n: n
    row_v = pl.BlockSpec((CHUNK, DN_V), lambda n: (n, 0))
    return pl.pallas_call(
        body, grid=(nc,), in_specs=_dn_in_specs(fwd),
        out_specs=[row_v, row_v,
                   pl.BlockSpec((1, DN_HEADS, DN_DK, DN_DV), lambda n: (n, 0, 0, 0)),
                   pl.BlockSpec((1, DN_HEADS, CHUNK, CHUNK), lambda n: (n, 0, 0, 0))],
        out_shape=[jax.ShapeDtypeStruct((rows, DN_V), F32), jax.ShapeDtypeStruct((rows, DN_V), BF16),
                   jax.ShapeDtypeStruct((nc, DN_HEADS, DN_DK, DN_DV), F32),
                   jax.ShapeDtypeStruct((nc, DN_HEADS, CHUNK, CHUNK), F32)],
        scratch_shapes=[pltpu.VMEM((DN_HEADS, DN_DK, DN_DV), F32)],
        name="dn_fwd", compiler_params=_params("arbitrary"))(
            proj, proj, proj, proj, conv_w, alog, dtb, norm_w.reshape(1, DN_DV))


def _dn_bwd(proj, o, dy, states, tinv, conv_w, alog, dtb, norm_w, seq):
    rows = proj.shape[0]
    nc = rows // CHUNK

    def body(x_ref, halo_ref, gate_ref, ba_ref, cw_ref, al_ref, dt_ref, nw_ref,
             o_ref, dy_ref, ss_ref, t_ref,
             dp_ref, dcw_ref, dal_ref, ddt_ref, dnw_ref, ds_ref, nxt_ref):
        n = pl.program_id(0)

        @pl.when(n == 0)
        def _():
            ds_ref[...] = jnp.zeros_like(ds_ref)
            nxt_ref[...] = jnp.zeros_like(nxt_ref)
            dcw_ref[...] = jnp.zeros_like(dcw_ref)
            dal_ref[...] = jnp.zeros_like(dal_ref)
            ddt_ref[...] = jnp.zeros_like(ddt_ref)
            dnw_ref[...] = jnp.zeros_like(dnw_ref)

        f = _dn_front(nc - 1 - n, seq, x_ref, halo_ref, ba_ref, cw_ref, al_ref, dt_ref)
        ri, ci = f["ri"], f["ci"]
        strict = (ri > ci).astype(F32)
        nw = nw_ref[...]
        lane128 = _iota((1, LANES), 1)
        row128 = _iota((LANES, 1), 0)
        dgam_col = jnp.zeros((CHUNK, LANES), F32)
        dgam_row = jnp.zeros((LANES, CHUNK), F32)
        dbeta = jnp.zeros((CHUNK, LANES), F32)
        dnw = jnp.zeros((1, DN_DV), F32)
        hs = range(DN_HEADS)
        heads = [_dn_head(f, h) for h in hs]
        cols = [slice(h * DN_DV, (h + 1) * DN_DV) for h in hs]
        t_l = [t_ref[0, h] for h in hs]
        s_l = [ss_ref[0, h] for h in hs]
        ds_l = [ds_ref[h] for h in hs]
        kk_l = [_nt(hd["kn"], hd["kn"]) for hd in heads]
        p_l = [_nt(hd["qn"], hd["kn"]) for hd in heads]
        rhsw_l = [hd["kn"] * (hd["bc"] * hd["egam"]) for hd in heads]
        u_l = [_nn(t, hd["v"] * hd["bc"], hi=True) for t, hd in zip(t_l, heads)]
        w_l = [_nn(t, r, hi=True) for t, r in zip(t_l, rhsw_l)]
        vnew_l = [u - _nn(w, s) for u, w, s in zip(u_l, w_l, s_l)]
        gnb = [_gated_norm_bwd(dy_ref[:, c], o_ref[:, c], gate_ref[:, c], nw) for c in cols]
        do_l = [x[0] for x in gnb]
        for h in hs:
            dp_ref[:, DN_CONV_CH + h * DN_DV:DN_CONV_CH + (h + 1) * DN_DV] = gnb[h][1].astype(dp_ref.dtype)
            dnw = dnw + gnb[h][2]
        qg_l = [hd["qn"] * hd["egam"] for hd in heads]
        kd_l = [hd["kn"] * hd["ekd"] for hd in heads]
        dvnew_l = [_tn(p * hd["decay"], do) + _nn(kd, ds)
                   for p, hd, do, kd, ds in zip(p_l, heads, do_l, kd_l, ds_l)]
        m_l = [_nt(do, vn) for do, vn in zip(do_l, vnew_l)]
        dqg_l = [_nt(do, s) for do, s in zip(do_l, s_l)]
        dkd_l = [_nt(vn, ds) for vn, ds in zip(vnew_l, ds_l)]
        for h in hs:
            ds_ref[h] = (ds_l[h] * heads[h]["eglast"] + _tn(qg_l[h], do_l[h]) - _tn(w_l[h], dvnew_l[h]))
        dw_l = [-_nt(dvn, s) for dvn, s in zip(dvnew_l, s_l)]
        dru_l = [_tn(t, dvn, hi=True) for t, dvn in zip(t_l, dvnew_l)]
        drw_l = [_tn(t, dw_, hi=True) for t, dw_ in zip(t_l, dw_l)]
        da_l = [-(_nt(dru, u) + _nt(drw, w)) * strict for dru, u, drw, w in zip(dru_l, u_l, drw_l, w_l)]
        dp_l = [m * hd["decay"] for m, hd in zip(m_l, heads)]
        dkk_l = [da * (hd["bc"] * hd["decay"]) for da, hd in zip(da_l, heads)]
        dqn_l = [dqg * hd["egam"] + _nn(dp, hd["kn"]) for dqg, hd, dp in zip(dqg_l, heads, dp_l)]
        dkn_l = [_tn(dp, hd["qn"]) + dkd * hd["ekd"] + drw * (hd["bc"] * hd["egam"])
                 + _nn(dkk, hd["kn"]) + _tn(dkk, hd["kn"])
                 for dp, hd, dkd, drw, dkk in zip(dp_l, heads, dkd_l, drw_l, dkk_l)]
        dq_parts, dk_parts, dv_parts = [], [], []
        for h in hs:
            hd = heads[h]
            kn, v, bc, egam, decay = hd["kn"], hd["v"], hd["bc"], hd["egam"], hd["decay"]
            t1 = jnp.sum(dkd_l[h] * kd_l[h], axis=1, keepdims=True)
            dglast = (jnp.sum(t1, axis=0, keepdims=True)
                      + jnp.sum(jnp.sum(ds_l[h] * s_l[h], axis=1, keepdims=True), axis=0, keepdims=True)
                      * hd["eglast"])
            e = (m_l[h] * p_l[h] + da_l[h] * (bc * kk_l[h])) * decay
            dgc = (jnp.sum(dqg_l[h] * qg_l[h], axis=1, keepdims=True) - t1
                   + jnp.sum(drw_l[h] * rhsw_l[h], axis=1, keepdims=True)
                   + jnp.sum(e, axis=1, keepdims=True)
                   + jnp.where(_iota((CHUNK, 1), 0) == CHUNK - 1, dglast, 0.0))
            dgr = -jnp.sum(e, axis=0, keepdims=True)
            dbc = (jnp.sum(dru_l[h] * v, axis=1, keepdims=True)
                   + jnp.sum(drw_l[h] * kn, axis=1, keepdims=True) * egam
                   + jnp.sum(da_l[h] * kk_l[h] * decay, axis=1, keepdims=True))
            dv_parts.append(dru_l[h] * bc)
            qh, dqn, dkn = hd["qh"], dqn_l[h], dkn_l[h]
            dq_parts.append(((DN_DK ** -0.5) * hd["rq"])
                            * (dqn - qh * jnp.sum(dqn * qh, axis=1, keepdims=True)))
            dk_parts.append(hd["rk"] * (dkn - kn * jnp.sum(dkn * kn, axis=1, keepdims=True)))
            dgam_col = dgam_col + dgc * (lane128 == DECAY_LANE + h).astype(F32)
            dbeta = dbeta + dbc * (lane128 == BETA_LANE + h).astype(F32)
            dgam_row = dgam_row + (row128 == DECAY_LANE + h).astype(F32) * dgr
        dnw_ref[...] += dnw
        dgam = dgam_col + _nt(f["eye"], dgam_row, hi=True)
        dg = _nn(f["upper"], dgam, hi=True)
        d_a = dg * (-f["eal"]) * jax.nn.sigmoid(f["z"]) * f["valid"]
        dal_ref[...] += jnp.sum(dg * f["g"], axis=0, keepdims=True)
        ddt_ref[...] += jnp.sum(d_a, axis=0, keepdims=True)
        d_b = dbeta * f["valid"] * f["sig"] * (1.0 - f["sig"])
        dp_ref[:, DN_CONV_CH + DN_V:DN_CONV_CH + DN_V + LANES] = (d_a + d_b).astype(dp_ref.dtype)
        dp_ref[:, DN_CONV_CH + DN_V + LANES:] = jnp.zeros((CHUNK, DN_IN_PAD - DN_IN_USED), dp_ref.dtype)
        dact = jnp.concatenate(dq_parts + dk_parts + dv_parts, axis=1)
        yc, sgc = f["yc"], f["sgc"]
        dyc = dact * (sgc * (1.0 + yc * (1.0 - sgc)))
        for k in range(CONV_K):
            dcw_ref[k] += jnp.sum(dyc * f["x_sh"][CONV_K - 1 - k], axis=0, keepdims=True)
        nxt = nxt_ref[...]
        dx = dyc * cw_ref[CONV_K - 1]
        for j in range(1, CONV_K):
            dx = dx + _shift_up(dyc, nxt, j) * cw_ref[CONV_K - 1 - j]
        nxt_ref[...] = dyc[0:8]
        dp_ref[:, :DN_CONV_CH] = (dx * f["valid"]).astype(dp_ref.dtype)

    rev = lambda n: nc - 1 - n
    row_v = pl.BlockSpec((CHUNK, DN_V), lambda n: (rev(n), 0))
    vec = pl.BlockSpec((1, LANES), lambda n: (0, 0))
    return pl.pallas_call(
        body, grid=(nc,),
        in_specs=_dn_in_specs(rev) + [
            row_v, row_v,
            pl.BlockSpec((1, DN_HEADS, DN_DK, DN_DV), lambda n: (rev(n), 0, 0, 0)),
            pl.BlockSpec((1, DN_HEADS, CHUNK, CHUNK), lambda n: (rev(n), 0, 0, 0))],
        out_specs=[pl.BlockSpec((CHUNK, DN_IN_PAD), lambda n: (rev(n), 0)),
                   pl.BlockSpec((CONV_K, 1, DN_CONV_CH), lambda n: (0, 0, 0)), vec, vec,
                   pl.BlockSpec((1, DN_DV), lambda n: (0, 0))],
        out_shape=[jax.ShapeDtypeStruct((rows, DN_IN_PAD), BF16),
                   jax.ShapeDtypeStruct((CONV_K, 1, DN_CONV_CH), F32),
                   jax.ShapeDtypeStruct((1, LANES), F32), jax.ShapeDtypeStruct((1, LANES), F32),
                   jax.ShapeDtypeStruct((1, DN_DV), F32)],
        scratch_shapes=[pltpu.VMEM((DN_HEADS, DN_DK, DN_DV), F32), pltpu.VMEM((8, DN_CONV_CH), F32)],
        name="dn_bwd", compiler_params=_params("arbitrary"))(
            proj, proj, proj, proj, conv_w, alog, dtb, norm_w.reshape(1, DN_DV), o, dy, states, tinv)


def _train_step(x, tgt, wts, sh, idx):
    seq = x.shape[0]
    rows = -(-(seq + CHUNK) // ROW_ALIGN) * ROW_ALIGN
    tail = rows - seq - CHUNK
    h0 = jnp.concatenate([jnp.zeros((PAD, D_MODEL), F32), wts["meta_tokens"].astype(F32), x,
                          jnp.zeros((tail, D_MODEL), F32)], axis=0)
    tgt_p = jnp.concatenate([jnp.zeros((CHUNK, D_MODEL), F32), tgt, jnp.zeros((tail, D_MODEL), F32)],
                            axis=0)
    cos, sin = _rope_tables(rows)
    consts = _ret_consts()
    conv_w = wts["dn_conv_w"].reshape(CONV_K, 1, DN_CONV_CH)
    lane_pad = LANES - 2 * DN_HEADS
    alog = jnp.pad(wts["dn_a_log"].reshape(1, DN_HEADS), ((0, 0), (DECAY_LANE, lane_pad)))
    dtb = jnp.pad(wts["dn_dt_bias"].reshape(1, DN_HEADS), ((0, 0), (DECAY_LANE, lane_pad)))
    g = {}

    wts = dict(wts)
    (got,) = _gather_weights([sh["ret_w_in"]])
    wts["ret_w_in"] = got.reshape(N_SHARD, D_MODEL, -1)
    hn0 = _rms_fwd(h0, wts["mix_norm_w"][0], "rms_mix0")
    proj0, got = _mm_cols(hn0, wts["ret_w_in"], "ret_in",
                          ride=_Ride("gather", [sh["ret_w_out"], sh["ffn_w_gate"], sh["dn_w_out"]]))
    wts["ret_w_out"] = got[0].reshape(-1, D_MODEL)
    wts["ffn_w_gate"] = got[1]
    wts["dn_w_out"] = got[2].reshape(-1, D_MODEL)
    (o0, y0, st0), got = _ret_fwd(proj0, cos, sin, consts, wts["ret_gn_w"], seq,
                                  ride=_Ride("gather", [sh["ffn_w_up"], sh["ffn_w_down"], sh["dn_w_in"]]))
    wts["ffn_w_up"], wts["ffn_w_down"] = got[0], got[1]
    n_dn = sh["dn_w_in"].shape[-1]
    wts["dn_w_in"] = jnp.pad(_join_cols(got[2].reshape(N_SHARD, D_MODEL, n_dn)),
                             ((0, 0), (0, DN_IN_PAD - N_SHARD * n_dn)))
    h1 = _mm(y0, wts["ret_w_out"], mode="nn", name="ret_out", resid=h0)
    hn1 = _rms_fwd(h1, wts["ffn_norm_w"][0], "rms_ffn0")
    g0, u0, act0 = _ffn_up(hn1, wts["ffn_w_gate"], wts["ffn_w_up"], 0, "ffn_up0")
    h2 = _ffn_down(act0, wts["ffn_w_down"], h1, 0, "ffn_down0")
    hn2 = _rms_fwd(h2, wts["mix_norm_w"][1], "rms_mix1")
    proj1 = _mm(hn2, wts["dn_w_in"], mode="nn", name="dn_in")
    o1, y1, st1, tinv = _dn_fwd(proj1, conv_w, alog, dtb, wts["dn_norm_w"], seq)
    h3 = _mm(y1, wts["dn_w_out"], mode="nn", name="dn_out", resid=h2)
    hn3 = _rms_fwd(h3, wts["ffn_norm_w"][1], "rms_ffn1")
    g1, u1, act1 = _ffn_up(hn3, wts["ffn_w_gate"], wts["ffn_w_up"], 1, "ffn_up1")
    h4 = _ffn_down(act1, wts["ffn_w_down"], h3, 1, "ffn_down1")

    dh4, g["final_norm_w"], loss = _final_loss(h4, wts["final_norm_w"], tgt_p, seq, "final_loss")

    layers = wts["ffn_w_gate"].shape[1]

    def ffn_bwd(dh_out, h_mid, hn, gg, uu, act, layer, prev, ride=None):
        tag = str(layer)
        res = _ffn_down_bwd(dh_out, wts["ffn_w_down"], gg, uu, layer, "ffn_down_bwd" + tag, ride=ride)
        (dg, du), rode = res if ride is not None else (res, None)
        d_down = _ffn_wgrad(act, [dh_out], layer, layers, prev and prev[:1], True, "ffn_dwd" + tag)
        d_gu = _ffn_wgrad(hn, [dg, du], layer, layers, prev and prev[1:], False, "ffn_dwgu" + tag)
        dh_mid, d_norm = _ffn_up_bwd(dg, du, wts["ffn_w_gate"], wts["ffn_w_up"], layer, h_mid,
                                     wts["ffn_norm_w"][layer], dh_out, "ffn_up_bwd" + tag)
        return dh_mid, list(d_down) + list(d_gu), d_norm, rode

    red = {}

    def rs_grads(names, grads):
        return [gr.reshape((N_SHARD,) + sh[n].shape) for n, gr in zip(names, grads)]

    def rs_partials(names, gs, sib):
        return [_rs_pair_add(gs[t], sib[t], idx, "rs_pair_add_" + n) for t, n in enumerate(names)]

    def rs_end(names, gs, sib, others, tag):
        mine = [_rs_final_add(gs[t], sib[t], others[t], idx, "rs_final_add_" + n) for t, n in enumerate(names)]
        red.update(zip(names, _rs_share(mine, "rs_share" + tag)))

    dh3, ffn_grads, dfn1, _ = ffn_bwd(dh4, h3, hn3, g1, u1, act1, 1, None)
    dy1 = _mm(dh3, wts["dn_w_out"], mode="nt", name="dn_out_bwd")
    d_dn_out = _mm(y1, dh3, mode="tn", name="dn_dwo")
    dproj1, dcw, dal, ddt, g["dn_norm_w"] = _dn_bwd(proj1, o1, dy1, st1, tinv, conv_w, alog, dtb,
                                                    wts["dn_norm_w"], seq)
    d_dn_in = _mm(hn2, dproj1, mode="tn", name="dn_dwi")
    d_dn_in = jnp.stack([d_dn_in[:, j * n_dn:(j + 1) * n_dn] for j in range(N_SHARD)])
    group1 = ["dn_w_out", "dn_w_in"]
    gs1 = rs_grads(group1, [d_dn_out, d_dn_in])
    (dh2, dmn1), sib1 = _mm_nt_rms(dproj1, wts["dn_w_in"], h2, wts["mix_norm_w"][1], dh3, "dn_in_bwd",
                                   ride=_Ride("pair", gs1))
    g["dn_conv_w"] = dcw.reshape(CONV_K, DN_CONV_CH)
    g["dn_a_log"] = dal[0, DECAY_LANE:DECAY_LANE + DN_HEADS]
    g["dn_dt_bias"] = ddt[0, DECAY_LANE:DECAY_LANE + DN_HEADS]

    dh1, ffn_grads, dfn0, others1 = ffn_bwd(dh2, h1, hn1, g0, u0, act0, 0, ffn_grads,
                                            ride=_Ride("chips", rs_partials(group1, gs1, sib1)))
    rs_end(group1, gs1, sib1, others1, "1")
    d_ret_out = _mm(y0, dh1, mode="tn", name="ret_dwo")
    group2 = ["ffn_w_down", "ffn_w_gate", "ffn_w_up", "ret_w_out"]
    gs2 = rs_grads(group2, list(ffn_grads) + [d_ret_out])
    dy0, sib2 = _mm(dh1, wts["ret_w_out"], mode="nt", name="ret_out_bwd", ride=_Ride("pair", gs2))
    (dproj0, g["ret_gn_w"]), others2 = _ret_bwd(proj0, o0, dy0, st0, cos, sin, consts, wts["ret_gn_w"], seq,
                                                ride=_Ride("chips", rs_partials(group2, gs2, sib2)))
    rs_end(group2, gs2, sib2, others2, "2")
    d_ret_in = _mm_cols_grad(hn0, dproj0, "ret_dwi")
    gs3 = rs_grads(["ret_w_in"], [d_ret_in])
    sib3 = _rs_pair(gs3, "rs_pair3")
    (dh0, dmn0), others3 = _mm_cols_t_rms(dproj0, wts["ret_w_in"], h0, wts["mix_norm_w"][0], dh1, "ret_in_bwd",
                                          ride=_Ride("chips", rs_partials(["ret_w_in"], gs3, sib3)))
    rs_end(["ret_w_in"], gs3, sib3, others3, "3")

    g["ffn_norm_w"] = jnp.concatenate([dfn0, dfn1], axis=0)
    g["mix_norm_w"] = jnp.concatenate([dmn0, dmn1], axis=0)
    g["meta_tokens"] = dh0[PAD:CHUNK]
    g["final_norm_w"] = g["final_norm_w"].reshape(D_MODEL)
    g["ret_gn_w"] = g["ret_gn_w"].reshape(RET_DV)
    g["dn_norm_w"] = g["dn_norm_w"].reshape(DN_DV)
    return loss, dh0, g, red


def _mesh_pos():
    return lax.axis_index("x"), lax.axis_index("y"), lax.axis_index("c")


def _other_chips(x, y):
    return [(1 - x, y), (x, 1 - y), (1 - x, 1 - y)]


def _remote(src, dst, send_sem, recv_sem, to):
    return pltpu.make_async_remote_copy(src_ref=src, dst_ref=dst, send_sem=send_sem, recv_sem=recv_sem,
                                        device_id=to, device_id_type=MESH)


GATHER_COPIES = 7


def _gather_weights(shards):
    ride = _Ride("gather", shards)

    def body(*refs):
        nt = len(shards)
        for phase in range(3):
            _gather_phase(phase, refs[:nt], refs[nt:2 * nt], *refs[2 * nt:])

    return pl.pallas_call(body, out_shape=ride.out_shape, in_specs=ride.in_specs, out_specs=ride.out_specs,
                          scratch_shapes=ride.scratch, name="gather_weights")(*shards)


def _gather_phase(phase, ins, outs, send_sems, recv_sems):
    x, y, c = _mesh_pos()
    me = 2 * x + y
    chips = _other_chips(x, y)
    sibling = (x, y, 1 - c)

    def cp(t, k, src, dst, to):
        i = GATHER_COPIES * t + k
        return _remote(src, dst, send_sems.at[i], recv_sems.at[i], to)

    for t in range(len(ins)):
        own = cp(t, 0, ins[t], outs[t].at[me], sibling)
        if phase == 0:
            own.start()
        if phase == 2:
            own.wait()
        for k, (px, py) in enumerate(chips):
            landed = outs[t].at[2 * px + py, c]
            theirs = outs[t].at[2 * px + py, 1 - c]
            to_chip = cp(t, 1 + k, ins[t].at[c], outs[t].at[me, c], (px, py, c))
            if phase == 0:
                to_chip.start()
            if phase == 1:
                cp(t, 1 + k, ins[t].at[c], landed, (px, py, c)).wait_recv()
                cp(t, 4 + k, landed, landed, sibling).start()
            if phase == 2:
                to_chip.wait_send()
                cp(t, 4 + k, landed, landed, sibling).wait_send()
                cp(t, 4 + k, theirs, theirs, sibling).wait_recv()


def _chips_phase(phase, ins, outs, send_sems, recv_sems):
    x, y, c = _mesh_pos()
    for t in range(len(ins)):
        for k, (px, py) in enumerate(_other_chips(x, y)):
            cp = _remote(ins[t].at[2 * px + py], outs[t].at[k], send_sems.at[3 * t + k], recv_sems.at[3 * t + k],
                         (px, py, c))
            if phase == 0:
                cp.start()
            if phase == 2:
                cp.wait()


class _Ride:
    def __init__(self, kind, arrays):
        self.kind, self.arrays = kind, list(arrays)
        nt = len(self.arrays)
        if kind == "gather":
            self.phase_fn, n_sem = _gather_phase, GATHER_COPIES * nt
            self.out_shape = [jax.ShapeDtypeStruct((N_SHARD,) + a.shape, a.dtype) for a in self.arrays]
        elif kind == "pair":
            self.phase_fn, n_sem = _pair_phase, nt
            self.out_shape = [jax.ShapeDtypeStruct(a.shape[:1] + a.shape[2:], a.dtype) for a in self.arrays]
        else:
            self.phase_fn, n_sem = _chips_phase, 3 * nt
            self.out_shape = [jax.ShapeDtypeStruct((3,) + a.shape[1:], a.dtype) for a in self.arrays]
        self.in_specs, self.out_specs = [ANY] * nt, [ANY] * nt
        self.scratch = [pltpu.SemaphoreType.DMA((n_sem,)), pltpu.SemaphoreType.DMA((n_sem,))]

    def emit(self, step, nsteps, ins, outs, send_sems, recv_sems, before):
        mid = max(0, min((7 * nsteps) // 8, nsteps - 2))
        todo = [(0, 0), (1, mid)] if before else [(2, nsteps - 1)]
        for phase, at in todo:
            if phase == 1 and self.kind != "gather":
                continue

            @pl.when(step == at)
            def _(phase=phase):
                self.phase_fn(phase, ins, outs, send_sems, recv_sems)


def _gather_small(blk):
    r, wd = blk.shape

    def body(b_ref, out_ref, send_sems, recv_sems):
        x, y, c = _mesh_pos()
        chips = _other_chips(x, y)
        out_ref[2 * x + y] = b_ref[...]
        sends = [_remote(b_ref, out_ref.at[2 * x + y], send_sems.at[k], recv_sems.at[k], (px, py, c))
                 for k, (px, py) in enumerate(chips)]
        for cp in sends:
            cp.start()
        for k, (px, py) in enumerate(chips):
            _remote(b_ref, out_ref.at[2 * px + py], send_sems.at[k], recv_sems.at[k], (px, py, c)).wait_recv()
        for cp in sends:
            cp.wait_send()

    return pl.pallas_call(
        body, out_shape=jax.ShapeDtypeStruct((4, r, wd), blk.dtype), in_specs=[VMEM_SPEC], out_specs=VMEM_SPEC,
        scratch_shapes=[pltpu.SemaphoreType.DMA((3,)), pltpu.SemaphoreType.DMA((3,))],
        name="gather_small")(blk)


def _allreduce_small(blk):
    r, wd = blk.shape
    rels = [(dx, dy, dc) for dx in (0, 1) for dy in (0, 1) for dc in (0, 1) if dx or dy or dc]

    def body(b_ref, out_ref, buf_ref, send_sems, recv_sems):
        x, y, c = _mesh_pos()

        def peer(rel):
            dx, dy, dc = rel
            return (1 - x if dx else x, 1 - y if dy else y, 1 - c if dc else c)

        me = 4 * x + 2 * y + c
        buf_ref[me] = b_ref[...]
        sends = [_remote(b_ref, buf_ref.at[me], send_sems.at[k], recv_sems.at[k], peer(rel))
                 for k, rel in enumerate(rels)]
        for cp in sends:
            cp.start()
        for k, rel in enumerate(rels):
            px, py, pc = peer(rel)
            _remote(b_ref, buf_ref.at[4 * px + 2 * py + pc], send_sems.at[k], recv_sems.at[k],
                    (px, py, pc)).wait_recv()
        for cp in sends:
            cp.wait_send()
        acc = buf_ref[0]
        for d in range(1, 8):
            acc = acc + buf_ref[d]
        out_ref[...] = acc

    return pl.pallas_call(
        body, out_shape=jax.ShapeDtypeStruct((r, wd), blk.dtype), in_specs=[VMEM_SPEC], out_specs=VMEM_SPEC,
        scratch_shapes=[pltpu.VMEM((8, r, wd), blk.dtype), pltpu.SemaphoreType.DMA((7,)),
                        pltpu.SemaphoreType.DMA((7,))],
        name="allreduce_small")(blk)


def _rs_pair(gs, name):
    ride = _Ride("pair", gs)

    def body(*refs):
        nt = len(gs)
        for phase in (0, 2):
            _pair_phase(phase, refs[:nt], refs[nt:2 * nt], *refs[2 * nt:])

    return pl.pallas_call(body, out_shape=ride.out_shape, in_specs=ride.in_specs, out_specs=ride.out_specs,
                          scratch_shapes=ride.scratch, name=name)(*gs)


def _pair_phase(phase, ins, outs, send_sems, recv_sems):
    x, y, c = _mesh_pos()
    for t in range(len(ins)):
        cp = _remote(ins[t].at[:, 1 - c], outs[t], send_sems.at[t], recv_sems.at[t], (x, y, 1 - c))
        if phase == 0:
            cp.start()
        if phase == 2:
            cp.wait()


def _rs_tile(a, b):
    return _div_tile(a, 512 if b <= 1024 else 256, 16)


def _rs_pair_add(g, a, idx, name):
    _, _, rows, cols = g.shape
    tr = _rs_tile(rows, cols)

    def body(s_ref, g_ref, a_ref, p_ref):
        p_ref[...] = (g_ref[...] + a_ref[...]).astype(p_ref.dtype)

    blk = pl.BlockSpec((None, tr, cols), lambda j, i, s: (j, i, 0))
    spec = pltpu.PrefetchScalarGridSpec(
        num_scalar_prefetch=1, grid=(N_SHARD, rows // tr),
        in_specs=[pl.BlockSpec((None, None, tr, cols), lambda j, i, s: (j, s[0], i, 0)), blk], out_specs=blk)
    return pl.pallas_call(
        body, grid_spec=spec, out_shape=jax.ShapeDtypeStruct((N_SHARD, rows, cols), BF16), name=name,
        compiler_params=_params("parallel", "parallel"))(idx, g, a)


def _rs_final_add(g, a, b, idx, name):
    _, _, rows, cols = g.shape
    tr = _rs_tile(rows, cols)

    def body(s_ref, g_ref, a_ref, b0_ref, b1_ref, b2_ref, f_ref):
        own = g_ref[...] + a_ref[...]
        f_ref[...] = ((own + b0_ref[...].astype(F32)) + b1_ref[...].astype(F32)) + b2_ref[...].astype(F32)

    def b_spec(k):
        return pl.BlockSpec((None, tr, cols), lambda i, s: (k, i, 0))

    spec = pltpu.PrefetchScalarGridSpec(
        num_scalar_prefetch=1, grid=(rows // tr,),
        in_specs=[pl.BlockSpec((None, None, tr, cols), lambda i, s: (s[1], s[0], i, 0)),
                  pl.BlockSpec((None, tr, cols), lambda i, s: (s[1], i, 0)), b_spec(0), b_spec(1), b_spec(2)],
        out_specs=pl.BlockSpec((None, tr, cols), lambda i, s: (s[0], i, 0)))
    return pl.pallas_call(
        body, grid_spec=spec, out_shape=jax.ShapeDtypeStruct((2, rows, cols), F32), name=name,
        compiler_params=_params("parallel"))(idx, g, a, b, b, b)


def _rs_share(fs, name):
    nt = len(fs)

    def body(*refs):
        outs = refs[nt:2 * nt]
        send_sems, recv_sems = refs[2 * nt:]
        x, y, c = _mesh_pos()
        cps = [_remote(outs[t].at[c], outs[t].at[c], send_sems.at[t], recv_sems.at[t], (x, y, 1 - c))
               for t in range(nt)]
        for cp in cps:
            cp.start()
        for cp in cps:
            cp.wait()

    return pl.pallas_call(
        body, out_shape=[jax.ShapeDtypeStruct(f.shape, f.dtype) for f in fs],
        in_specs=[ANY] * nt, out_specs=[ANY] * nt, input_output_aliases={t: t for t in range(nt)},
        scratch_shapes=[pltpu.SemaphoreType.DMA((nt,)), pltpu.SemaphoreType.DMA((nt,))], name=name)(*fs)


def _adamw(w, g, m, v, name):
    lead, rows, cols = w.shape
    tr = rows // 4 if rows % 32 == 0 else rows

    def body(w_ref, g_ref, m_ref, v_ref, go_ref, d_ref, mo_ref, vo_ref):
        gv = g_ref[...]
        go_ref[...] = gv
        mn = ADAM_B1 * m_ref[...] + (1.0 - ADAM_B1) * gv
        vn = ADAM_B2 * v_ref[...] + (1.0 - ADAM_B2) * (gv * gv)
        m_hat = mn / (1.0 - ADAM_B1 ** ADAM_STEP)
        v_hat = vn / (1.0 - ADAM_B2 ** ADAM_STEP)
        d_ref[...] = -ADAM_LR * (m_hat / (jnp.sqrt(v_hat) + ADAM_EPS) + ADAM_WD * w_ref[...])
        mo_ref[...] = mn
        vo_ref[...] = vn

    blk = pl.BlockSpec((None, tr, cols), lambda l, i: (l, i, 0))
    out = jax.ShapeDtypeStruct((lead, rows, cols), F32)
    return pl.pallas_call(
        body, grid=(lead, rows // tr), in_specs=[blk] * 4, out_specs=[blk] * 4, out_shape=[out] * 4, name=name,
        compiler_params=_params("parallel", "parallel"))(w, g, m, v)


BIG = ["ret_w_in", "ret_w_out", "dn_w_in", "dn_w_out", "ffn_w_gate", "ffn_w_up", "ffn_w_down"]
TRANSPOSED_AT_BOUNDARY = {"dn_w_in"}
SMALL =["meta_tokens", "mix_norm_w", "ffn_norm_w", "ret_gn_w", "dn_conv_w", "dn_a_log", "dn_dt_bias",
         "dn_norm_w", "final_norm_w"]
SMALL_SHARDED = {"meta_tokens", "dn_conv_w", "dn_norm_w"}
ORDER = ["meta_tokens", "mix_norm_w", "ffn_norm_w", "ret_w_in", "ret_gn_w", "ret_w_out", "dn_w_in",
         "dn_conv_w", "dn_a_log", "dn_dt_bias", "dn_norm_w", "dn_w_out", "ffn_w_gate", "ffn_w_up",
         "ffn_w_down", "final_norm_w"]


def _halves(a):
    return a.reshape(2, -1, a.shape[-1])


def _pack_lanes(parts, align=8):
    flat = jnp.concatenate([p.reshape(-1) for p in parts])
    flat = jnp.pad(flat, (0, -flat.shape[0] % (align * LANES)))
    return flat.reshape(-1, LANES)


def _unpack(buf, shapes):
    lead = buf.shape[:-2]
    flat = buf.reshape(lead + (-1,))
    out, off = [], 0
    for shp in shapes:
        size = math.prod(shp)
        out.append(flat[..., off:off + size].reshape(lead + tuple(shp)))
        off += size
    return out


def _join_cols(shards):
    return jnp.concatenate([shards[j] for j in range(N_SHARD)], axis=-1)


def kernel(x, meta_tokens, mix_norm_w, ffn_norm_w, ret_w_in, ret_gn_w, ret_w_out, dn_w_in, dn_conv_w, dn_a_log, dn_dt_bias, dn_norm_w, dn_w_out, ffn_w_gate, ffn_w_up, ffn_w_down, final_norm_w, loss_target, m_meta_tokens, m_mix_norm_w, m_ffn_norm_w, m_ret_w_in, m_ret_gn_w, m_ret_w_out, m_dn_w_in, m_dn_conv_w, m_dn_a_log, m_dn_dt_bias, m_dn_norm_w, m_dn_w_out, m_ffn_w_gate, m_ffn_w_up, m_ffn_w_down, m_final_norm_w, v_meta_tokens, v_mix_norm_w, v_ffn_norm_w, v_ret_w_in, v_ret_gn_w, v_ret_w_out, v_dn_w_in, v_dn_conv_w, v_dn_a_log, v_dn_dt_bias, v_dn_norm_w, v_dn_w_out, v_ffn_w_gate, v_ffn_w_up, v_ffn_w_down, v_final_norm_w):
    w = dict(meta_tokens=meta_tokens, mix_norm_w=mix_norm_w, ffn_norm_w=ffn_norm_w, ret_w_in=ret_w_in,
             ret_gn_w=ret_gn_w, ret_w_out=ret_w_out, dn_w_in=dn_w_in, dn_conv_w=dn_conv_w, dn_a_log=dn_a_log,
             dn_dt_bias=dn_dt_bias, dn_norm_w=dn_norm_w, dn_w_out=dn_w_out, ffn_w_gate=ffn_w_gate,
             ffn_w_up=ffn_w_up, ffn_w_down=ffn_w_down, final_norm_w=final_norm_w)
    m = dict(meta_tokens=m_meta_tokens, mix_norm_w=m_mix_norm_w, ffn_norm_w=m_ffn_norm_w, ret_w_in=m_ret_w_in,
             ret_gn_w=m_ret_gn_w, ret_w_out=m_ret_w_out, dn_w_in=m_dn_w_in, dn_conv_w=m_dn_conv_w,
             dn_a_log=m_dn_a_log, dn_dt_bias=m_dn_dt_bias, dn_norm_w=m_dn_norm_w, dn_w_out=m_dn_w_out,
             ffn_w_gate=m_ffn_w_gate, ffn_w_up=m_ffn_w_up, ffn_w_down=m_ffn_w_down, final_norm_w=m_final_norm_w)
    v = dict(meta_tokens=v_meta_tokens, mix_norm_w=v_mix_norm_w, ffn_norm_w=v_ffn_norm_w, ret_w_in=v_ret_w_in,
             ret_gn_w=v_ret_gn_w, ret_w_out=v_ret_w_out, dn_w_in=v_dn_w_in, dn_conv_w=v_dn_conv_w,
             dn_a_log=v_dn_a_log, dn_dt_bias=v_dn_dt_bias, dn_norm_w=v_dn_norm_w, dn_w_out=v_dn_w_out,
             ffn_w_gate=v_ffn_w_gate, ffn_w_up=v_ffn_w_up, ffn_w_down=v_ffn_w_down, final_norm_w=v_final_norm_w)
    mx, my, mc = _mesh_pos()
    chip = 2 * mx + my

    sm_names = [n for n in SMALL if n in SMALL_SHARDED]
    sm_gathered = _unpack(_gather_small(_pack_lanes([w[n] for n in sm_names])), [w[n].shape for n in sm_names])
    full = {n: _join_cols(sm_gathered[i]) for i, n in enumerate(sm_names)}
    wts = {
        "meta_tokens": full["meta_tokens"], "mix_norm_w": mix_norm_w, "ffn_norm_w": ffn_norm_w,
        "ret_gn_w": ret_gn_w[0], "final_norm_w": final_norm_w, "dn_conv_w": full["dn_conv_w"][0],
        "dn_a_log": dn_a_log[0], "dn_dt_bias": dn_dt_bias[0], "dn_norm_w": full["dn_norm_w"][0],
    }
    idx = jnp.stack([mc, chip]).astype(jnp.int32)
    shards = {n: _halves(w[n].astype(MXU_DTYPE)) for n in BIG}
    loss_part, dh0, g, reduced = _train_step(x[0], loss_target[0], wts, shards, idx)
    seq = x.shape[1]
    grad_x = dh0[CHUNK:CHUNK + seq].reshape(x.shape)
    gsh = {}

    small_full_shapes = [g[n].shape for n in SMALL] + [(1,)]
    red = _unpack(_allreduce_small(_pack_lanes([g[n] for n in SMALL] + [loss_part[0, :1]])), small_full_shapes)
    loss = red[-1][0]
    for i, n in enumerate(SMALL):
        gn = red[i]
        if n in SMALL_SHARDED:
            width = w[n].shape[-1]
            gn = lax.dynamic_slice_in_dim(gn, chip * width, width, axis=gn.ndim - 1)
        gsh[n] = gn.reshape(w[n].shape)

    delta, new_m, new_v = {}, {}, {}
    for n in BIG:
        shp = w[n].shape
        if n in TRANSPOSED_AT_BOUNDARY:
            view = lambda a: jnp.swapaxes(a, 1, 2).reshape(1, -1, LANES)
            back = lambda a: jnp.swapaxes(a.reshape(shp[0], shp[2], shp[1]), 1, 2)
        else:
            view = back = lambda a: a
        res = _adamw(view(w[n]), view(reduced[n].reshape(shp)), view(m[n]), view(v[n]), "adamw_" + n)
        gsh[n], delta[n], new_m[n], new_v[n] = [back(r) for r in res]
    sm_local_shapes = [w[n].shape for n in SMALL]
    _, d_, m_, v_ = _adamw(*[_pack_lanes([t[n] for n in SMALL])[None] for t in (w, gsh, m, v)], "adamw_small")
    d_, m_, v_ = d_[0], m_[0], v_[0]
    for n, dd, mm, vv in zip(SMALL, _unpack(d_, sm_local_shapes), _unpack(m_, sm_local_shapes),
                             _unpack(v_, sm_local_shapes)):
        delta[n], new_m[n], new_v[n] = dd, mm, vv

    return (loss, grad_x, *[gsh[n] for n in ORDER], *[delta[n] for n in ORDER],
            *[new_m[n] for n in ORDER], *[new_v[n] for n in ORDER])
```

```python
import functools
import math

import jax
import jax.numpy as jnp
from jax import lax
from jax.experimental import pallas as pl
from jax.experimental.pallas import tpu as pltpu

F32 = jnp.float32
BF16 = jnp.bfloat16
MXU_DTYPE = BF16

D_MODEL = 1024
N_META = 16
CHUNK = 64
PAD = CHUNK - N_META
RMS_EPS = 1e-6
RET_HEADS, RET_DK, RET_DV = 4, 256, 512
RET_QK, RET_V = RET_HEADS * RET_DK, RET_HEADS * RET_DV
RET_IN = 2 * RET_QK + 2 * RET_V
ROPE_BASE = 10000.0
DN_HEADS, DN_DK, DN_DV = 8, 128, 256
DN_QK, DN_V = DN_HEADS * DN_DK, DN_HEADS * DN_DV
DN_CONV_CH = 2 * DN_QK + DN_V
DN_IN = DN_CONV_CH + DN_V + 2 * DN_HEADS
LANES = 128
DN_IN_USED = DN_CONV_CH + DN_V + LANES
DN_IN_PAD = DN_IN_USED + LANES
CONV_K = 4
FFN_HIDDEN = 2816
ADAM_LR, ADAM_B1, ADAM_B2, ADAM_EPS, ADAM_WD, ADAM_STEP = 0.001, 0.9, 0.999, 1e-08, 0.01, 10

ROW_ALIGN = 256
VMEM_LIMIT = 56 * 1024 * 1024
MESH = pl.DeviceIdType.MESH
ANY = pl.BlockSpec(memory_space=pl.ANY)
VMEM_SPEC = pl.BlockSpec(memory_space=pltpu.VMEM)
_HI = lax.Precision.HIGHEST


def _params(*sem):
    return pltpu.CompilerParams(dimension_semantics=sem, vmem_limit_bytes=VMEM_LIMIT)


def _dg(a, b, ca, cb, hi):
    dims = (((ca,), (cb,)), ((), ()))

    def dot(p, q):
        return lax.dot_general(p, q, dims, preferred_element_type=F32)

    if not hi:
        return dot(a.astype(MXU_DTYPE), b.astype(MXU_DTYPE))
    if MXU_DTYPE == F32:
        return lax.dot_general(a, b, dims, precision=_HI, preferred_element_type=F32)
    a_hi, b_hi = a.astype(MXU_DTYPE), b.astype(MXU_DTYPE)
    a_lo = (a - a_hi.astype(F32)).astype(MXU_DTYPE)
    b_lo = (b - b_hi.astype(F32)).astype(MXU_DTYPE)
    return dot(a_hi, b_hi) + (dot(a_hi, b_lo) + dot(a_lo, b_hi))


def _nn(a, b, hi=False):
    return _dg(a, b, 1, 0, hi)


def _nt(a, b, hi=False):
    return _dg(a, b, 1, 1, hi)


def _tn(a, b, hi=False):
    return _dg(a, b, 0, 0, hi)


def _iota(shape, dim):
    return lax.broadcasted_iota(jnp.int32, shape, dim)


def _valid_rows(first_row, rows, seq):
    r = first_row + _iota((rows, 1), 0)
    return ((r >= PAD) & (r < CHUNK + seq)).astype(F32)


def _rope(t, cs, sn):
    half = t.shape[-1] // 2
    t1, t2 = t[:, :half], t[:, half:]
    return jnp.concatenate([t1 * cs - t2 * sn, t1 * sn + t2 * cs], axis=1)


def _rope_bwd(d, cs, sn):
    half = d.shape[-1] // 2
    d1, d2 = d[:, :half], d[:, half:]
    return jnp.concatenate([d1 * cs + d2 * sn, d2 * cs - d1 * sn], axis=1)


def _col(x, idx):
    oh = (_iota((1, x.shape[1]), 1) == idx).astype(F32)
    return jnp.sum(x * oh, axis=1, keepdims=True)


def _row(x, idx):
    oh = (_iota((x.shape[0], 1), 0) == idx).astype(F32)
    return jnp.sum(x * oh, axis=0, keepdims=True)


def _shift_down(x, halo8, k):
    xr = pltpu.roll(x, k, 0)
    hr = pltpu.roll(halo8, k, 0)
    first = jnp.where(_iota((8, 1), 0) < k, hr, xr[0:8])
    return jnp.concatenate([first, xr[8:]], axis=0)


def _shift_up(x, next8, j):
    rows = x.shape[0]
    xr = pltpu.roll(x, rows - j, 0)
    nr = pltpu.roll(next8, 8 - j, 0)
    last = jnp.where(_iota((8, 1), 0) >= 8 - j, nr, xr[rows - 8:])
    return jnp.concatenate([xr[:rows - 8], last], axis=0)


def _gated_norm(o, gate, w):
    r = lax.rsqrt(jnp.mean(o * o, axis=-1, keepdims=True) + RMS_EPS)
    return o * r * w * (gate * jax.nn.sigmoid(gate))


def _gated_norm_bwd(dy, o, gate, w):
    r = lax.rsqrt(jnp.mean(o * o, axis=-1, keepdims=True) + RMS_EPS)
    nrm = o * r
    sg = jax.nn.sigmoid(gate)
    sl = gate * sg
    dgate = dy * nrm * w * (sg * (1.0 + gate * (1.0 - sg)))
    dn = dy * w * sl
    dw = jnp.sum(dy * nrm * sl, axis=0, keepdims=True)
    do = r * (dn - nrm * jnp.mean(dn * nrm, axis=-1, keepdims=True))
    return do, dgate, dw


def _softplus(z):
    return jnp.maximum(z, 0.0) + jnp.log(1.0 + jnp.exp(-jnp.abs(z)))


def _row_tile(rows, cap=768):
    for t in (768, 512, 256, 128, 64, 32, 16, 8):
        if t <= cap and rows % t == 0:
            return t
    return rows


TILE_BUDGET = 44 * 1024 * 1024


def _fit_rows(rows, row_bytes, fixed_bytes, value_row_bytes):
    best = None
    for t in range(LANES, rows + 1, LANES):
        if rows % t == 0 and 2 * (row_bytes * t + fixed_bytes) + value_row_bytes * t <= TILE_BUDGET:
            best = t
    return best or _row_tile(rows, 256)


def _div_tile(n, cap, mult):
    best = None
    for t in range(mult, min(cap, n) + 1, mult):
        if n % t == 0:
            best = t
    return best or n


def _col_tile(cols, cap=1536):
    best = None
    for t in range(LANES, min(cap, cols) + 1, LANES):
        if cols % t == 0:
            best = t
    return best or cols


def _gmm_norm(name, h, w, tm, n_col, args, in_specs, out_specs, out_shape, fn, ride=None):
    m, d = h.shape
    n_in = len(args)

    def body(*refs):
        h_ref, w_ref = refs[:2]
        ins = refs[2:2 + n_in]
        hn_out = refs[2 + n_in]
        outs = refs[3 + n_in:-1]
        hn_ref = refs[-1]

        @pl.when(pl.program_id(1) == 0)
        def _():
            x = h_ref[...]
            r = lax.rsqrt(jnp.mean(x * x, axis=-1, keepdims=True) + RMS_EPS)
            hn = (x * r * w_ref[...]).astype(hn_ref.dtype)
            hn_ref[...] = hn
            hn_out[...] = hn

        for o_ref, p in zip(outs, fn(hn_ref, *ins)):
            o_ref[...] = p.astype(o_ref.dtype)

    row = pl.BlockSpec((tm, d), lambda i, j: (i, 0))
    res, rode = _pcall(body, [h, w.reshape(1, d)] + list(args), grid=(m // tm, n_col),
                       in_specs=[row, pl.BlockSpec((1, d), lambda i, j: (0, 0))] + list(in_specs),
                       out_specs=[row] + list(out_specs),
                       out_shape=[jax.ShapeDtypeStruct((m, d), BF16)] + list(out_shape),
                       scratch=[pltpu.VMEM((tm, d), BF16)], name=name, sem=("parallel", "arbitrary"), ride=ride)
    return res if ride is None else (res, rode)


def _gmm_rms(name, grid, args, in_specs, row_spec, fn, h, w, resid, row_axis, red_axis=None, ride=None):
    m, d = h.shape
    n_in = len(args)
    vec = pl.BlockSpec((1, d), lambda *g: (0, 0))

    def body(*refs):
        ins = refs[:n_in]
        h_ref, w_ref, r_ref, dh_ref, dw_ref = refs[n_in:]
        part = fn(*ins)
        row = pl.program_id(row_axis)

        def finish(dy):
            x = h_ref[...]
            r = lax.rsqrt(jnp.mean(x * x, axis=-1, keepdims=True) + RMS_EPS)
            xh = x * r
            dxh = dy * w_ref[...]
            dh_ref[...] = r_ref[...] + r * (dxh - xh * jnp.mean(dxh * xh, axis=-1, keepdims=True))
            dwp = jnp.sum(dy * xh, axis=0, keepdims=True)

            @pl.when(row == 0)
            def _():
                dw_ref[...] = dwp

            @pl.when(row > 0)
            def _():
                dw_ref[...] += dwp

        if red_axis is None:
            finish(part)
            return
        k = pl.program_id(red_axis)

        @pl.when(k == 0)
        def _():
            dh_ref[...] = part

        @pl.when(k > 0)
        def _():
            dh_ref[...] += part

        @pl.when(k == grid[red_axis] - 1)
        def _():
            finish(dh_ref[...])

    res, rode = _pcall(body, list(args) + [h, w.reshape(1, d), resid], grid=grid,
                       in_specs=list(in_specs) + [row_spec, vec, row_spec], out_specs=[row_spec, vec],
                       out_shape=[jax.ShapeDtypeStruct((m, d), F32), jax.ShapeDtypeStruct((1, d), F32)],
                       name=name, sem=("arbitrary",) * len(grid), ride=ride)
    return res if ride is None else (res, rode)


def _final_loss(h, w, tgt, seq, name):
    rows, d = h.shape
    tm = _row_tile(rows)

    def body(h_ref, w_ref, t_ref, dh_ref, dw_ref, loss_ref):
        i = pl.program_id(0)
        r_idx = i * tm + _iota((tm, 1), 0)
        m = ((r_idx >= CHUNK) & (r_idx < CHUNK + seq)).astype(F32)
        x = h_ref[...]
        wv = w_ref[...]
        r = lax.rsqrt(jnp.mean(x * x, axis=-1, keepdims=True) + RMS_EPS)
        xh = x * r
        err = (xh * wv - t_ref[...]) * m
        lpart = 0.5 * jnp.sum(jnp.mean(err * err, axis=-1, keepdims=True), axis=0, keepdims=True)
        dyv = err * (1.0 / d)
        dxh = dyv * wv
        dh_ref[...] = r * (dxh - xh * jnp.mean(dxh * xh, axis=-1, keepdims=True))
        part = jnp.sum(dyv * xh, axis=0, keepdims=True)

        @pl.when(i == 0)
        def _():
            dw_ref[...] = part
            loss_ref[...] = jnp.broadcast_to(lpart, loss_ref.shape)

        @pl.when(i > 0)
        def _():
            dw_ref[...] += part
            loss_ref[...] += jnp.broadcast_to(lpart, loss_ref.shape)

    blk = pl.BlockSpec((tm, d), lambda i: (i, 0))
    vec = pl.BlockSpec((1, d), lambda i: (0, 0))
    return pl.pallas_call(
        body, grid=(rows // tm,), in_specs=[blk, vec, blk],
        out_specs=[blk, vec, pl.BlockSpec((1, LANES), lambda i: (0, 0))],
        out_shape=[jax.ShapeDtypeStruct((rows, d), F32), jax.ShapeDtypeStruct((1, d), F32),
                   jax.ShapeDtypeStruct((1, LANES), F32)],
        name=name, compiler_params=_params("arbitrary"))(h, w.reshape(1, d), tgt)


def _isz(x):
    return jnp.dtype(x.dtype).itemsize


def _mm(a, b, *, mode, name, out_dtype=F32, resid=None, col_cap=1536, ride=None):
    if mode == "tn":
        m, k = a.shape
        n = b.shape[1]
        tn = _col_tile(n, col_cap)
        tm = _fit_rows(m, k * _isz(a) + tn * _isz(b), (3 * k * tn * 4) // 2, 2 * (k + tn))

        def body_tn(a_ref, b_ref, o_ref):
            i = pl.program_id(1)
            part = _tn(a_ref[...], b_ref[...])

            @pl.when(i == 0)
            def _():
                o_ref[...] = part

            @pl.when(i > 0)
            def _():
                o_ref[...] += part

        return pl.pallas_call(
            body_tn, grid=(n // tn, m // tm),
            in_specs=[pl.BlockSpec((tm, k), lambda j, i: (i, 0)),
                      pl.BlockSpec((tm, tn), lambda j, i: (i, j))],
            out_specs=pl.BlockSpec((k, tn), lambda j, i: (0, j)),
            out_shape=jax.ShapeDtypeStruct((k, n), F32), name=name,
            compiler_params=_params("parallel", "arbitrary"))(a, b)

    m, ka = a.shape
    n = b.shape[1] if mode == "nn" else b.shape[0]
    has_resid = resid is not None
    tn = _col_tile(n, col_cap)
    tm = _fit_rows(m, ka * _isz(a) + tn * (jnp.dtype(out_dtype).itemsize + (4 if has_resid else 0)),
                   ka * tn * _isz(b), 2 * ka + 8 * tn)

    def body(*refs):
        if has_resid:
            a_ref, b_ref, r_ref, o_ref = refs
        else:
            a_ref, b_ref, o_ref = refs
        acc = _nn(a_ref[...], b_ref[...]) if mode == "nn" else _nt(a_ref[...], b_ref[...])
        if has_resid:
            acc = acc + r_ref[...]
        o_ref[...] = acc.astype(o_ref.dtype)

    b_spec = (pl.BlockSpec((b.shape[0], tn), lambda j, i: (0, j)) if mode == "nn"
              else pl.BlockSpec((tn, b.shape[1]), lambda j, i: (j, 0)))
    o_spec = pl.BlockSpec((tm, tn), lambda j, i: (i, j))
    in_specs = [pl.BlockSpec((tm, ka), lambda j, i: (i, 0)), b_spec]
    args = [a, b]
    if has_resid:
        in_specs.append(o_spec)
        args.append(resid)
    res, rode = _pcall(body, args, grid=(n // tn, m // tm), in_specs=in_specs, out_specs=[o_spec],
                       out_shape=[jax.ShapeDtypeStruct((m, n), out_dtype)], name=name,
                       sem=("parallel", "parallel"), ride=ride)
    return res[0] if ride is None else (res[0], rode)


N_SHARD = 4


def _gmm(name, grid, args, in_specs, out_specs, out_shape, fn, red_axis=None, init_arg=None, aliases=None,
         ride=None):
    n_in = len(args)
    single = not isinstance(out_shape, (list, tuple))
    out_specs = [out_specs] if single else list(out_specs)
    out_shape = [out_shape] if single else list(out_shape)

    def body(*refs):
        _gmm_step(fn, refs[:n_in], refs[n_in:], red_axis, init_arg)

    sem = tuple("arbitrary" if ax == red_axis else "parallel" for ax in range(len(grid)))
    res, rode = _pcall(body, args, grid=grid, in_specs=in_specs, out_specs=out_specs, out_shape=out_shape,
                       name=name, sem=sem, aliases=aliases, ride=ride)
    ours = res[0] if single else res
    return ours if ride is None else (ours, rode)


def _gmm_step(fn, ins, outs, red_axis, init_arg):
    parts = fn(*ins)
    if red_axis is None:
        for o_ref, p in zip(outs, parts):
            o_ref[...] = p.astype(o_ref.dtype)
        return
    k = pl.program_id(red_axis)

    @pl.when(k == 0)
    def _():
        for idx, (o_ref, p) in enumerate(zip(outs, parts)):
            o_ref[...] = p + ins[init_arg][...] if (idx == 0 and init_arg is not None) else p

    @pl.when(k > 0)
    def _():
        for o_ref, p in zip(outs, parts):
            o_ref[...] += p


def _ride_body(ride, grid, n_in, n_out, n_scratch, body):
    n_rin, n_rout = len(ride.arrays), len(ride.out_shape)
    nsteps = math.prod(grid)

    def wrapped(*refs):
        ins = refs[:n_in]
        r_ins = refs[n_in:n_in + n_rin]
        o0 = n_in + n_rin
        outs = refs[o0:o0 + n_out]
        r_outs = refs[o0 + n_out:o0 + n_out + n_rout]
        s0 = o0 + n_out + n_rout
        scratch = refs[s0:s0 + n_scratch]
        send_sems, recv_sems = refs[-2:]
        step = pl.program_id(0)
        for ax in range(1, len(grid)):
            step = step * grid[ax] + pl.program_id(ax)
        ride.emit(step, nsteps, r_ins, r_outs, send_sems, recv_sems, before=True)
        body(*ins, *outs, *scratch)
        ride.emit(step, nsteps, r_ins, r_outs, send_sems, recv_sems, before=False)

    return wrapped


def _pcall(body, args, *, grid, in_specs, out_specs, out_shape, name, sem, scratch=(), aliases=None, ride=None):
    if ride is None:
        res = pl.pallas_call(body, grid=grid, in_specs=list(in_specs), out_specs=list(out_specs),
                             out_shape=list(out_shape), scratch_shapes=list(scratch), name=name,
                             input_output_aliases=aliases or {}, compiler_params=_params(*sem))(*args)
        return res, None
    n_in, n_out = len(args), len(out_shape)
    res = pl.pallas_call(
        _ride_body(ride, grid, n_in, n_out, len(scratch), body), grid=grid,
        in_specs=list(in_specs) + ride.in_specs, out_specs=list(out_specs) + ride.out_specs,
        out_shape=list(out_shape) + ride.out_shape, scratch_shapes=list(scratch) + ride.scratch, name=name,
        input_output_aliases=aliases or {},
        compiler_params=_params(*(("arbitrary",) * len(grid))))(*args, *ride.arrays)
    return res[:n_out], res[n_out:]


def _norm_mm_cols(h, wn, ws, name, ride=None):
    m, k = h.shape
    n = ws.shape[2]
    tm = _fit_rows(m, k * 6 + n * 4, k * n * _isz(ws), 6 * k + 4 * n)
    return _gmm_norm(name, h, wn, tm, N_SHARD, [ws], [pl.BlockSpec((None, k, n), lambda i, j: (j, 0, 0))],
                     [pl.BlockSpec((tm, n), lambda i, j: (i, j))], [jax.ShapeDtypeStruct((m, N_SHARD * n), F32)],
                     lambda hn_ref, w_ref: (_nn(hn_ref[...], w_ref[...]),), ride=ride)


def _norm_mm(h, wn, b, name):
    m, k = h.shape
    n = b.shape[1]
    tn = _col_tile(n)
    tm = _fit_rows(m, k * 6 + tn * 4, k * tn * _isz(b), 6 * k + 4 * tn)
    return _gmm_norm(name, h, wn, tm, n // tn, [b], [pl.BlockSpec((k, tn), lambda i, j: (0, j))],
                     [pl.BlockSpec((tm, tn), lambda i, j: (i, j))], [jax.ShapeDtypeStruct((m, n), F32)],
                     lambda hn_ref, b_ref: (_nn(hn_ref[...], b_ref[...]),))


def _mm_cols_t_rms(d, ws, h, w, resid, name, ride=None):
    m = d.shape[0]
    _, k, n = ws.shape
    tm = _fit_rows(m, n * _isz(d) + 3 * k * 4, k * n * _isz(ws), 16 * k)
    return _gmm_rms(name, (m // tm, N_SHARD), [d, ws],
                    [pl.BlockSpec((tm, n), lambda i, j: (i, j)), pl.BlockSpec((None, k, n), lambda i, j: (j, 0, 0))],
                    pl.BlockSpec((tm, k), lambda i, j: (i, 0)),
                    lambda d_ref, w_ref: _nt(d_ref[...], w_ref[...]), h, w, resid, 0, red_axis=1, ride=ride)


def _mm_nt_rms(a, b, h, w, resid, name, ride=None):
    m, n = a.shape
    k = b.shape[0]
    tm = _fit_rows(m, n * _isz(a) + 3 * k * 4, k * n * _isz(b), 16 * k)
    return _gmm_rms(name, (m // tm,), [a, b],
                    [pl.BlockSpec((tm, n), lambda i: (i, 0)), pl.BlockSpec((k, n), lambda i: (0, 0))],
                    pl.BlockSpec((tm, k), lambda i: (i, 0)),
                    lambda a_ref, b_ref: _nt(a_ref[...], b_ref[...]), h, w, resid, 0, ride=ride)


def _mm_cols_grad(a, d, name):
    m, k = a.shape
    n = d.shape[1] // N_SHARD
    tm = _fit_rows(m, k * _isz(a) + n * _isz(d), (3 * k * n * 4) // 2, 2 * (k + n))
    return _gmm(name, (N_SHARD, m // tm), [a, d],
                [pl.BlockSpec((tm, k), lambda j, i: (i, 0)), pl.BlockSpec((tm, n), lambda j, i: (i, j))],
                pl.BlockSpec((None, k, n), lambda j, i: (j, 0, 0)), jax.ShapeDtypeStruct((N_SHARD, k, n), F32),
                lambda a_ref, d_ref: (_tn(a_ref[...], d_ref[...]),), red_axis=1)


def _ffn_up(h, wn, wg, wu, layer, name):
    m, k = h.shape
    n = wg.shape[3]
    tm = _fit_rows(m, k * 6 + 3 * n * jnp.dtype(BF16).itemsize, 2 * k * n * _isz(wg), 6 * k + 16 * n)

    def fn(a_ref, wg_ref, wu_ref):
        a = a_ref[...]
        g = _nn(a, wg_ref[...])
        u = _nn(a, wu_ref[...])
        return g, u, g * jax.nn.sigmoid(g) * u

    w_spec = pl.BlockSpec((None, None, k, n), lambda i, j: (j, layer, 0, 0))
    o_spec = pl.BlockSpec((None, tm, n), lambda i, j: (j, i, 0))
    out = jax.ShapeDtypeStruct((N_SHARD, m, n), BF16)
    return _gmm_norm(name, h, wn, tm, N_SHARD, [wg, wu], [w_spec, w_spec], [o_spec, o_spec, o_spec],
                     [out, out, out], fn)


def _ffn_down(act, wd, resid, layer, name):
    _, m, n = act.shape
    d = wd.shape[3]
    tm = _fit_rows(m, N_SHARD * n * _isz(act) + 2 * d * 4, N_SHARD * n * d * _isz(wd), 8 * d)

    def fn(a_ref, w_ref, r_ref):
        acc = r_ref[...]
        for j in range(N_SHARD):
            acc = acc + _nn(a_ref[j], w_ref[j])
        return (acc,)

    row = pl.BlockSpec((tm, d), lambda i: (i, 0))
    return _gmm(name, (m // tm,), [act, wd, resid],
                [pl.BlockSpec((N_SHARD, tm, n), lambda i: (0, i, 0)),
                 pl.BlockSpec((N_SHARD, None, n, d), lambda i: (0, layer, 0, 0)), row],
                row, jax.ShapeDtypeStruct((m, d), F32), fn)


def _ffn_down_bwd(dh, wd, g, u, layer, name, ride=None):
    m, d = dh.shape
    n = wd.shape[2]
    tm = _fit_rows(m, d * _isz(dh) + 4 * n * jnp.dtype(BF16).itemsize, n * d * _isz(wd), 2 * d + 24 * n)

    def fn(dh_ref, wd_ref, g_ref, u_ref):
        dact = _nt(dh_ref[...], wd_ref[...])
        gv = g_ref[...].astype(F32)
        uv = u_ref[...].astype(F32)
        sg = jax.nn.sigmoid(gv)
        return dact * uv * (sg * (1.0 + gv * (1.0 - sg))), dact * gv * sg

    o_spec = pl.BlockSpec((None, tm, n), lambda j, i: (j, i, 0))
    out = jax.ShapeDtypeStruct((N_SHARD, m, n), BF16)
    return _gmm(name, (N_SHARD, m // tm), [dh, wd, g, u],
                [pl.BlockSpec((tm, d), lambda j, i: (i, 0)),
                 pl.BlockSpec((None, None, n, d), lambda j, i: (j, layer, 0, 0)), o_spec, o_spec],
                [o_spec, o_spec], [out, out], fn, ride=ride)


def _ffn_up_bwd(dg, du, wg, wu, layer, h, w, resid, name):
    _, m, n = dg.shape
    k = wg.shape[2]
    tm = _fit_rows(m, 2 * N_SHARD * n * _isz(dg) + 3 * k * 4, 2 * N_SHARD * k * n * _isz(wg), 16 * k)

    def fn(dg_ref, du_ref, wg_ref, wu_ref):
        acc = _nt(dg_ref[0], wg_ref[0]) + _nt(du_ref[0], wu_ref[0])
        for j in range(1, N_SHARD):
            acc = acc + _nt(dg_ref[j], wg_ref[j]) + _nt(du_ref[j], wu_ref[j])
        return acc

    d_spec = pl.BlockSpec((N_SHARD, tm, n), lambda i: (0, i, 0))
    w_spec = pl.BlockSpec((N_SHARD, None, k, n), lambda i: (0, layer, 0, 0))
    return _gmm_rms(name, (m // tm,), [dg, du, wg, wu], [d_spec, d_spec, w_spec, w_spec],
                    pl.BlockSpec((tm, k), lambda i: (i, 0)), fn, h, w, resid, 0)


def _ffn_wgrad(lhs, rhs_list, layer, layers, prev, lhs_sharded, name):
    if lhs_sharded:
        _, m, k = lhs.shape
        n = rhs_list[0].shape[1]
    else:
        m, k = lhs.shape
        n = rhs_list[0].shape[2]
    n_out = len(rhs_list)
    tm = _fit_rows(m, k * _isz(lhs) + n_out * n * _isz(rhs_list[0]), (3 * n_out * k * n * 4) // 2,
                   2 * (k + n_out * n))
    sh = pl.BlockSpec((None, tm, k if lhs_sharded else n), lambda j, i: (j, i, 0))
    fl = pl.BlockSpec((tm, n if lhs_sharded else k), lambda j, i: (i, 0))
    n_out = len(rhs_list)
    args = [lhs] + list(rhs_list)
    in_specs = [sh if lhs_sharded else fl] + [fl if lhs_sharded else sh] * n_out
    aliases = None
    if prev is not None:
        aliases = {len(args) + t: t for t in range(n_out)}
        args = args + list(prev)
        in_specs = in_specs + [ANY] * n_out

    def fn(l_ref, *rest):
        lv = l_ref[...]
        return tuple(_tn(lv, r_ref[...]) for r_ref in rest[:n_out])

    o_spec = pl.BlockSpec((None, None, k, n), lambda j, i: (j, layer, 0, 0))
    out = jax.ShapeDtypeStruct((N_SHARD, layers, k, n), F32)
    return _gmm(name, (N_SHARD, m // tm), args, in_specs, [o_spec] * n_out, [out] * n_out, fn,
                red_axis=1, aliases=aliases)


def _ret_consts():
    log_gamma = jnp.log1p(-jnp.exp2(-5.0 - jnp.arange(RET_HEADS, dtype=F32)))
    idx = jnp.arange(CHUNK, dtype=F32)
    rel = idx[:, None] - idx[None, :]
    dmask = jnp.where((rel >= 0)[None], jnp.exp(log_gamma[:, None, None] * jnp.maximum(rel, 0.0)), 0.0)
    xi = jnp.exp(log_gamma[:, None] * (idx[None, :] + 1.0))[:, :, None]
    zeta = jnp.exp(log_gamma[:, None] * (CHUNK - 1.0 - idx[None, :]))[:, :, None]
    gamma_c = jnp.exp(log_gamma * CHUNK)
    wide = (RET_HEADS, CHUNK, RET_DK)
    return dmask, jnp.broadcast_to(xi, wide), jnp.broadcast_to(zeta, wide), gamma_c


def _rope_tables(rows):
    half = RET_DK // 2
    inv_freq = ROPE_BASE ** (-jnp.arange(half, dtype=F32) / half)
    pos = (jnp.arange(rows) - PAD).astype(F32)
    ang = pos[:, None] * inv_freq[None, :]
    return jnp.cos(ang), jnp.sin(ang)


def _ret_specs(order):
    return [pl.BlockSpec((CHUNK, RET_QK), lambda n: (order(n), 0)),
            pl.BlockSpec((CHUNK, RET_QK), lambda n: (order(n), 1)),
            pl.BlockSpec((CHUNK, RET_V), lambda n: (order(n), 1)),
            pl.BlockSpec((CHUNK, RET_V), lambda n: (order(n), 2))]


def _ret_const_specs():
    return [pl.BlockSpec((RET_HEADS, CHUNK, CHUNK), lambda n: (0, 0, 0)),
            pl.BlockSpec((RET_HEADS, CHUNK, RET_DK), lambda n: (0, 0, 0)),
            pl.BlockSpec((RET_HEADS, CHUNK, RET_DK), lambda n: (0, 0, 0)),
            pl.BlockSpec((1, RET_DV), lambda n: (0, 0))]


def _ret_fwd(proj, cos, sin, consts, gn_w, seq, ride=None):
    rows = proj.shape[0]
    nc = rows // CHUNK
    dmask, xi, zeta, gamma_c = consts

    def body(gam_ref, q_ref, k_ref, v_ref, g_ref, cos_ref, sin_ref, dm_ref, xi_ref, ze_ref, gn_ref,
             o_ref, y_ref, ss_ref, s_ref):
        n = pl.program_id(0)

        @pl.when(n == 0)
        def _():
            s_ref[...] = jnp.zeros_like(s_ref)

        cs, sn = cos_ref[...], sin_ref[...]
        kscale = _valid_rows(n * CHUNK, CHUNK, seq) * (RET_DK ** -0.5)
        gn = gn_ref[...]
        hs = range(RET_HEADS)
        qk_cols = [slice(h * RET_DK, (h + 1) * RET_DK) for h in hs]
        v_cols = [slice(h * RET_DV, (h + 1) * RET_DV) for h in hs]
        qr_l = [_rope(q_ref[:, c], cs, sn) for c in qk_cols]
        kr_l = [_rope(k_ref[:, c], cs, sn) * kscale for c in qk_cols]
        v_l = [v_ref[:, c] for c in v_cols]
        s_l = [s_ref[h] for h in hs]
        sc_l = [_nt(qr, kr) * dm_ref[h] for h, (qr, kr) in enumerate(zip(qr_l, kr_l))]
        o_l = [_nn(sc_l[h], v_l[h]) + _nn(qr_l[h] * xi_ref[h], s_l[h]) for h in hs]
        for h in hs:
            ss_ref[0, h] = s_l[h].astype(ss_ref.dtype)
            s_ref[h] = gam_ref[h] * s_l[h] + _tn(kr_l[h] * ze_ref[h], v_l[h])
            o_ref[:, v_cols[h]] = o_l[h]
            y_ref[:, v_cols[h]] = _gated_norm(o_l[h], g_ref[:, v_cols[h]], gn).astype(y_ref.dtype)

    fwd = lambda n: n
    row128 = pl.BlockSpec((CHUNK, RET_DK // 2), lambda n: (n, 0))
    row_v = pl.BlockSpec((CHUNK, RET_V), lambda n: (n, 0))
    res, rode = _pcall(
        body, [gamma_c, proj, proj, proj, proj, cos, sin, dmask, xi, zeta, gn_w.reshape(1, RET_DV)],
        grid=(nc,),
        in_specs=[pl.BlockSpec(memory_space=pltpu.SMEM)] + _ret_specs(fwd) + [row128, row128]
        + _ret_const_specs(),
        out_specs=[row_v, row_v,
                   pl.BlockSpec((1, RET_HEADS, RET_DK, RET_DV), lambda n: (n, 0, 0, 0))],
        out_shape=[jax.ShapeDtypeStruct((rows, RET_V), F32), jax.ShapeDtypeStruct((rows, RET_V), BF16),
                   jax.ShapeDtypeStruct((nc, RET_HEADS, RET_DK, RET_DV), BF16)],
        scratch=[pltpu.VMEM((RET_HEADS, RET_DK, RET_DV), F32)], name="ret_fwd", sem=("arbitrary",), ride=ride)
    return res if ride is None else (res, rode)


def _ret_bwd(proj, o, dy, states, cos, sin, consts, gn_w, seq, ride=None):
    rows = proj.shape[0]
    nc = rows // CHUNK
    dmask, xi, zeta, gamma_c = consts

    def body(gam_ref, q_ref, k_ref, v_ref, g_ref, o_ref, dy_ref, ss_ref, cos_ref, sin_ref,
             dm_ref, xi_ref, ze_ref, gn_ref, dp_ref, dgn_ref, ds_ref):
        n = pl.program_id(0)

        @pl.when(n == 0)
        def _():
            ds_ref[...] = jnp.zeros_like(ds_ref)
            dgn_ref[...] = jnp.zeros_like(dgn_ref)

        cs, sn = cos_ref[...], sin_ref[...]
        kscale = _valid_rows((nc - 1 - n) * CHUNK, CHUNK, seq) * (RET_DK ** -0.5)
        gn = gn_ref[...]
        dgn = jnp.zeros((1, RET_DV), F32)
        hs = range(RET_HEADS)
        qk_cols = [slice(h * RET_DK, (h + 1) * RET_DK) for h in hs]
        v_cols = [slice(h * RET_DV, (h + 1) * RET_DV) for h in hs]
        qr_l = [_rope(q_ref[:, c], cs, sn) for c in qk_cols]
        kr_l = [_rope(k_ref[:, c], cs, sn) * kscale for c in qk_cols]
        v_l = [v_ref[:, c] for c in v_cols]
        s_l = [ss_ref[0, h] for h in hs]
        ds_l = [ds_ref[h] for h in hs]
        gnb = [_gated_norm_bwd(dy_ref[:, c], o_ref[:, c], g_ref[:, c], gn) for c in v_cols]
        do_l = [x[0] for x in gnb]
        sc_l = [_nt(qr_l[h], kr_l[h]) * dm_ref[h] for h in hs]
        dsc_l = [_nt(do_l[h], v_l[h]) * dm_ref[h] for h in hs]
        dv_l = [_tn(sc_l[h], do_l[h]) + _nn(kr_l[h] * ze_ref[h], ds_l[h]) for h in hs]
        dqr_l = [_nn(dsc_l[h], kr_l[h]) + _nt(do_l[h], s_l[h]) * xi_ref[h] for h in hs]
        dkr_l = [_tn(dsc_l[h], qr_l[h]) + _nt(v_l[h], ds_l[h]) * ze_ref[h] for h in hs]
        for h in hs:
            dgn = dgn + gnb[h][2]
            ds_ref[h] = gam_ref[h] * ds_l[h] + _tn(qr_l[h] * xi_ref[h], do_l[h])
            dp_ref[:, qk_cols[h]] = _rope_bwd(dqr_l[h], cs, sn).astype(dp_ref.dtype)
            dp_ref[:, RET_QK + h * RET_DK:RET_QK + (h + 1) * RET_DK] = (
                _rope_bwd(dkr_l[h] * kscale, cs, sn).astype(dp_ref.dtype))
            dp_ref[:, 2 * RET_QK + h * RET_DV:2 * RET_QK + (h + 1) * RET_DV] = dv_l[h].astype(dp_ref.dtype)
            dp_ref[:, 2 * RET_QK + RET_V + h * RET_DV:2 * RET_QK + RET_V + (h + 1) * RET_DV] = (
                gnb[h][1].astype(dp_ref.dtype))
        dgn_ref[...] += dgn

    rev = lambda n: nc - 1 - n
    row128 = pl.BlockSpec((CHUNK, RET_DK // 2), lambda n: (rev(n), 0))
    row_v = pl.BlockSpec((CHUNK, RET_V), lambda n: (rev(n), 0))
    res, rode = _pcall(
        body, [gamma_c, proj, proj, proj, proj, o, dy, states, cos, sin, dmask, xi, zeta,
               gn_w.reshape(1, RET_DV)],
        grid=(nc,),
        in_specs=[pl.BlockSpec(memory_space=pltpu.SMEM)] + _ret_specs(rev) + [
            row_v, row_v, pl.BlockSpec((1, RET_HEADS, RET_DK, RET_DV), lambda n: (rev(n), 0, 0, 0)),
            row128, row128] + _ret_const_specs(),
        out_specs=[pl.BlockSpec((CHUNK, RET_IN), lambda n: (rev(n), 0)),
                   pl.BlockSpec((1, RET_DV), lambda n: (0, 0))],
        out_shape=[jax.ShapeDtypeStruct((rows, RET_IN), BF16), jax.ShapeDtypeStruct((1, RET_DV), F32)],
        scratch=[pltpu.VMEM((RET_HEADS, RET_DK, RET_DV), F32)], name="ret_bwd", sem=("arbitrary",), ride=ride)
    return res if ride is None else (res, rode)


GATE_COL = DN_CONV_CH // DN_V
BA_COL = (DN_CONV_CH + DN_V) // LANES
BETA_LANE, DECAY_LANE = 0, DN_HEADS
INV_SHIFT = 4
INV_SQUARINGS = INV_SHIFT - 1
assert CHUNK == 4 << INV_SHIFT


def _dn_in_specs(order):
    return [pl.BlockSpec((CHUNK, DN_CONV_CH), lambda n: (order(n), 0)),
            pl.BlockSpec((8, DN_CONV_CH), lambda n: (jnp.maximum(order(n) * (CHUNK // 8) - 1, 0), 0)),
            pl.BlockSpec((CHUNK, DN_V), lambda n: (order(n), GATE_COL)),
            pl.BlockSpec((CHUNK, LANES), lambda n: (order(n), BA_COL)),
            pl.BlockSpec((CONV_K, 1, DN_CONV_CH), lambda n: (0, 0, 0)),
            pl.BlockSpec((1, LANES), lambda n: (0, 0)),
            pl.BlockSpec((1, LANES), lambda n: (0, 0)),
            pl.BlockSpec((1, DN_DV), lambda n: (0, 0))]


def _dn_front(c, seq, x_ref, halo_ref, ba_ref, cw_ref, al_ref, dt_ref):
    valid = _valid_rows(c * CHUNK, CHUNK, seq)
    xin = x_ref[...] * valid
    halo = halo_ref[...] * _valid_rows(c * CHUNK - 8, 8, seq)
    x_sh = [xin] + [_shift_down(xin, halo, k) for k in range(1, CONV_K)]
    yc = x_sh[0] * cw_ref[CONV_K - 1]
    for k in range(1, CONV_K):
        yc = yc + x_sh[k] * cw_ref[CONV_K - 1 - k]
    sgc = jax.nn.sigmoid(yc)
    ba = ba_ref[...]
    sig = jax.nn.sigmoid(ba)
    beta = sig * valid
    z = ba + dt_ref[...]
    eal = jnp.exp(al_ref[...])
    g = -eal * _softplus(z) * valid
    ri, ci = _iota((CHUNK, CHUNK), 0), _iota((CHUNK, CHUNK), 1)
    lower = (ri >= ci).astype(F32)
    upper = (ri <= ci).astype(F32)
    eye = (ri == ci).astype(F32)
    gam = _nn(lower, g, hi=True)
    gam_t = _tn(g, upper, hi=True)
    return dict(valid=valid, x_sh=x_sh, yc=yc, sgc=sgc, act=yc * sgc, sig=sig, beta=beta, z=z,
                eal=eal, g=g, gam=gam, gam_t=gam_t, ri=ri, ci=ci, upper=upper, eye=eye)


def _dn_head(f, h):
    act = f["act"]
    q_raw = act[:, h * DN_DK:(h + 1) * DN_DK]
    k_raw = act[:, DN_QK + h * DN_DK:DN_QK + (h + 1) * DN_DK]
    v = act[:, 2 * DN_QK + h * DN_DV:2 * DN_QK + (h + 1) * DN_DV]
    rq = lax.rsqrt(jnp.sum(q_raw * q_raw, axis=-1, keepdims=True) + RMS_EPS)
    rk = lax.rsqrt(jnp.sum(k_raw * k_raw, axis=-1, keepdims=True) + RMS_EPS)
    qh = q_raw * rq
    kn = k_raw * rk
    gam_c = _col(f["gam"], DECAY_LANE + h)
    gam_r = _row(f["gam_t"], DECAY_LANE + h)
    bc = _col(f["beta"], BETA_LANE + h)
    diff = gam_c - gam_r
    decay = jnp.where(f["ri"] >= f["ci"], jnp.exp(jnp.minimum(diff, 0.0)), 0.0)
    glast = jnp.sum(gam_r * (_iota((1, CHUNK), 1) == CHUNK - 1).astype(F32), axis=1, keepdims=True)
    return dict(rq=rq, rk=rk, qh=qh, qn=qh * (DN_DK ** -0.5), kn=kn, v=v, gam_c=gam_c, gam_r=gam_r,
                bc=bc, diff=diff, decay=decay, egam=jnp.exp(gam_c), glast=glast,
                eglast=jnp.exp(glast), ekd=jnp.exp(glast - gam_c))


def _dn_fwd(proj, conv_w, alog, dtb, norm_w, seq):
    rows = proj.shape[0]
    nc = rows // CHUNK

    def body(x_ref, halo_ref, gate_ref, ba_ref, cw_ref, al_ref, dt_ref, nw_ref,
             o_ref, y_ref, ss_ref, t_ref, s_ref):
        n = pl.program_id(0)

        @pl.when(n == 0)
        def _():
            s_ref[...] = jnp.zeros_like(s_ref)

        f = _dn_front(n, seq, x_ref, halo_ref, ba_ref, cw_ref, al_ref, dt_ref)
        ri, ci = f["ri"], f["ci"]
        eye = f["eye"]
        diag_m = (jnp.right_shift(ri, INV_SHIFT) == jnp.right_shift(ci, INV_SHIFT)).astype(F32)
        half_m = (jnp.right_shift(ri, INV_SHIFT + 1) == jnp.right_shift(ci, INV_SHIFT + 1)).astype(F32)
        nw = nw_ref[...]
        heads = [_dn_head(f, h) for h in range(DN_HEADS)]
        a_all = [jnp.where(ri > ci, hd["bc"] * _nt(hd["kn"], hd["kn"]) * hd["decay"], 0.0) for hd in heads]
        b_all = [a * diag_m for a in a_all]
        t_all = [eye - b for b in b_all]
        for _ in range(INV_SQUARINGS):
            b_all = [_nn(b, b, hi=True) for b in b_all]
            t_all = [t + _nn(t, b, hi=True) for t, b in zip(t_all, b_all)]
        for off_m in (half_m - diag_m, 1.0 - half_m):
            x_all = [_nn(a * off_m, t, hi=True) for a, t in zip(a_all, t_all)]
            t_all = [t - _nn(t, x, hi=True) for t, x in zip(t_all, x_all)]
        u_all = [_nn(t, hd["v"] * hd["bc"], hi=True) for t, hd in zip(t_all, heads)]
        w_all = [_nn(t, hd["kn"] * (hd["bc"] * hd["egam"]), hi=True) for t, hd in zip(t_all, heads)]
        for h in range(DN_HEADS):
            hd = heads[h]
            v_cols = slice(h * DN_DV, (h + 1) * DN_DV)
            t_ref[0, h] = t_all[h]
            s = s_ref[h]
            ss_ref[0, h] = s
            u, w = u_all[h], w_all[h]
            v_new = u - _nn(w, s)
            qk = _nt(hd["qn"], hd["kn"]) * hd["decay"]
            o = _nn(hd["qn"] * hd["egam"], s) + _nn(qk, v_new)
            s_ref[h] = s * hd["eglast"] + _tn(hd["kn"] * hd["ekd"], v_new)
            o_ref[:, v_cols] = o
            y_ref[:, v_cols] = _gated_norm(o, gate_ref[:, v_cols], nw).astype(y_ref.dtype)

    fwd = lambda n: n
    row_v = pl.BlockSpec((CHUNK, DN_V), lambda n: (n, 0))
    return pl.pallas_call(
        body, grid=(nc,), in_specs=_dn_in_specs(fwd),
        out_specs=[row_v, row_v,
                   pl.BlockSpec((1, DN_HEADS, DN_DK, DN_DV), lambda n: (n, 0, 0, 0)),
                   pl.BlockSpec((1, DN_HEADS, CHUNK, CHUNK), lambda n: (n, 0, 0, 0))],
        out_shape=[jax.ShapeDtypeStruct((rows, DN_V), F32), jax.ShapeDtypeStruct((rows, DN_V), BF16),
                   jax.ShapeDtypeStruct((nc, DN_HEADS, DN_DK, DN_DV), F32),
                   jax.ShapeDtypeStruct((nc, DN_HEADS, CHUNK, CHUNK), F32)],
        scratch_shapes=[pltpu.VMEM((DN_HEADS, DN_DK, DN_DV), F32)],
        name="dn_fwd", compiler_params=_params("arbitrary"))(
            proj, proj, proj, proj, conv_w, alog, dtb, norm_w.reshape(1, DN_DV))


def _dn_bwd(proj, o, dy, states, tinv, conv_w, alog, dtb, norm_w, seq):
    rows = proj.shape[0]
    nc = rows // CHUNK

    def body(x_ref, halo_ref, gate_ref, ba_ref, cw_ref, al_ref, dt_ref, nw_ref,
             o_ref, dy_ref, ss_ref, t_ref,
             dp_ref, dcw_ref, dal_ref, ddt_ref, dnw_ref, ds_ref, nxt_ref):
        n = pl.program_id(0)

        @pl.when(n == 0)
        def _():
            ds_ref[...] = jnp.zeros_like(ds_ref)
            nxt_ref[...] = jnp.zeros_like(nxt_ref)
            dcw_ref[...] = jnp.zeros_like(dcw_ref)
            dal_ref[...] = jnp.zeros_like(dal_ref)
            ddt_ref[...] = jnp.zeros_like(ddt_ref)
            dnw_ref[...] = jnp.zeros_like(dnw_ref)

        f = _dn_front(nc - 1 - n, seq, x_ref, halo_ref, ba_ref, cw_ref, al_ref, dt_ref)
        ri, ci = f["ri"], f["ci"]
        strict = (ri > ci).astype(F32)
        nw = nw_ref[...]
        lane128 = _iota((1, LANES), 1)
        row128 = _iota((LANES, 1), 0)
        dgam_col = jnp.zeros((CHUNK, LANES), F32)
        dgam_row = jnp.zeros((LANES, CHUNK), F32)
        dbeta = jnp.zeros((CHUNK, LANES), F32)
        dnw = jnp.zeros((1, DN_DV), F32)
        hs = range(DN_HEADS)
        heads = [_dn_head(f, h) for h in hs]
        cols = [slice(h * DN_DV, (h + 1) * DN_DV) for h in hs]
        t_l = [t_ref[0, h] for h in hs]
        s_l = [ss_ref[0, h] for h in hs]
        ds_l = [ds_ref[h] for h in hs]
        kk_l = [_nt(hd["kn"], hd["kn"]) for hd in heads]
        p_l = [_nt(hd["qn"], hd["kn"]) for hd in heads]
        rhsw_l = [hd["kn"] * (hd["bc"] * hd["egam"]) for hd in heads]
        u_l = [_nn(t, hd["v"] * hd["bc"], hi=True) for t, hd in zip(t_l, heads)]
        w_l = [_nn(t, r, hi=True) for t, r in zip(t_l, rhsw_l)]
        vnew_l = [u - _nn(w, s) for u, w, s in zip(u_l, w_l, s_l)]
        gnb = [_gated_norm_bwd(dy_ref[:, c], o_ref[:, c], gate_ref[:, c], nw) for c in cols]
        do_l = [x[0] for x in gnb]
        for h in hs:
            dp_ref[:, DN_CONV_CH + h * DN_DV:DN_CONV_CH + (h + 1) * DN_DV] = gnb[h][1].astype(dp_ref.dtype)
            dnw = dnw + gnb[h][2]
        qg_l = [hd["qn"] * hd["egam"] for hd in heads]
        kd_l = [hd["kn"] * hd["ekd"] for hd in heads]
        dvnew_l = [_tn(p * hd["decay"], do) + _nn(kd, ds)
                   for p, hd, do, kd, ds in zip(p_l, heads, do_l, kd_l, ds_l)]
        m_l = [_nt(do, vn) for do, vn in zip(do_l, vnew_l)]
        dqg_l = [_nt(do, s) for do, s in zip(do_l, s_l)]
        dkd_l = [_nt(vn, ds) for vn, ds in zip(vnew_l, ds_l)]
        for h in hs:
            ds_ref[h] = (ds_l[h] * heads[h]["eglast"] + _tn(qg_l[h], do_l[h]) - _tn(w_l[h], dvnew_l[h]))
        dw_l = [-_nt(dvn, s) for dvn, s in zip(dvnew_l, s_l)]
        dru_l = [_tn(t, dvn, hi=True) for t, dvn in zip(t_l, dvnew_l)]
        drw_l = [_tn(t, dw_, hi=True) for t, dw_ in zip(t_l, dw_l)]
        da_l = [-(_nt(dru, u) + _nt(drw, w)) * strict for dru, u, drw, w in zip(dru_l, u_l, drw_l, w_l)]
        dp_l = [m * hd["decay"] for m, hd in zip(m_l, heads)]
        dkk_l = [da * (hd["bc"] * hd["decay"]) for da, hd in zip(da_l, heads)]
        dqn_l = [dqg * hd["egam"] + _nn(dp, hd["kn"]) for dqg, hd, dp in zip(dqg_l, heads, dp_l)]
        dkn_l = [_tn(dp, hd["qn"]) + dkd * hd["ekd"] + drw * (hd["bc"] * hd["egam"])
                 + _nn(dkk, hd["kn"]) + _tn(dkk, hd["kn"])
                 for dp, hd, dkd, drw, dkk in zip(dp_l, heads, dkd_l, drw_l, dkk_l)]
        dq_parts, dk_parts, dv_parts = [], [], []
        for h in hs:
            hd = heads[h]
            kn, v, bc, egam, decay = hd["kn"], hd["v"], hd["bc"], hd["egam"], hd["decay"]
            t1 = jnp.sum(dkd_l[h] * kd_l[h], axis=1, keepdims=True)
            dglast = (jnp.sum(t1, axis=0, keepdims=True)
                      + jnp.sum(jnp.sum(ds_l[h] * s_l[h], axis=1, keepdims=True), axis=0, keepdims=True)
                      * hd["eglast"])
            e = (m_l[h] * p_l[h] + da_l[h] * (bc * kk_l[h])) * decay
            dgc = (jnp.sum(dqg_l[h] * qg_l[h], axis=1, keepdims=True) - t1
                   + jnp.sum(drw_l[h] * rhsw_l[h], axis=1, keepdims=True)
                   + jnp.sum(e, axis=1, keepdims=True)
                   + jnp.where(_iota((CHUNK, 1), 0) == CHUNK - 1, dglast, 0.0))
            dgr = -jnp.sum(e, axis=0, keepdims=True)
            dbc = (jnp.sum(dru_l[h] * v, axis=1, keepdims=True)
                   + jnp.sum(drw_l[h] * kn, axis=1, keepdims=True) * egam
                   + jnp.sum(da_l[h] * kk_l[h] * decay, axis=1, keepdims=True))
            dv_parts.append(dru_l[h] * bc)
            qh, dqn, dkn = hd["qh"], dqn_l[h], dkn_l[h]
            dq_parts.append(((DN_DK ** -0.5) * hd["rq"])
                            * (dqn - qh * jnp.sum(dqn * qh, axis=1, keepdims=True)))
            dk_parts.append(hd["rk"] * (dkn - kn * jnp.sum(dkn * kn, axis=1, keepdims=True)))
            dgam_col = dgam_col + dgc * (lane128 == DECAY_LANE + h).astype(F32)
            dbeta = dbeta + dbc * (lane128 == BETA_LANE + h).astype(F32)
            dgam_row = dgam_row + (row128 == DECAY_LANE + h).astype(F32) * dgr
        dnw_ref[...] += dnw
        dgam = dgam_col + _nt(f["eye"], dgam_row, hi=True)
        dg = _nn(f["upper"], dgam, hi=True)
        d_a = dg * (-f["eal"]) * jax.nn.sigmoid(f["z"]) * f["valid"]
        dal_ref[...] += jnp.sum(dg * f["g"], axis=0, keepdims=True)
        ddt_ref[...] += jnp.sum(d_a, axis=0, keepdims=True)
        d_b = dbeta * f["valid"] * f["sig"] * (1.0 - f["sig"])
        dp_ref[:, DN_CONV_CH + DN_V:DN_CONV_CH + DN_V + LANES] = (d_a + d_b).astype(dp_ref.dtype)
        dp_ref[:, DN_CONV_CH + DN_V + LANES:] = jnp.zeros((CHUNK, DN_IN_PAD - DN_IN_USED), dp_ref.dtype)
        dact = jnp.concatenate(dq_parts + dk_parts + dv_parts, axis=1)
        yc, sgc = f["yc"], f["sgc"]
        dyc = dact * (sgc * (1.0 + yc * (1.0 - sgc)))
        for k in range(CONV_K):
            dcw_ref[k] += jnp.sum(dyc * f["x_sh"][CONV_K - 1 - k], axis=0, keepdims=True)
        nxt = nxt_ref[...]
        dx = dyc * cw_ref[CONV_K - 1]
        for j in range(1, CONV_K):
            dx = dx + _shift_up(dyc, nxt, j) * cw_ref[CONV_K - 1 - j]
        nxt_ref[...] = dyc[0:8]
        dp_ref[:, :DN_CONV_CH] = (dx * f["valid"]).astype(dp_ref.dtype)

    rev = lambda n: nc - 1 - n
    row_v = pl.BlockSpec((CHUNK, DN_V), lambda n: (rev(n), 0))
    vec = pl.BlockSpec((1, LANES), lambda n: (0, 0))
    return pl.pallas_call(
        body, grid=(nc,),
        in_specs=_dn_in_specs(rev) + [
            row_v, row_v,
            pl.BlockSpec((1, DN_HEADS, DN_DK, DN_DV), lambda n: (rev(n), 0, 0, 0)),
            pl.BlockSpec((1, DN_HEADS, CHUNK, CHUNK), lambda n: (rev(n), 0, 0, 0))],
        out_specs=[pl.BlockSpec((CHUNK, DN_IN_PAD), lambda n: (rev(n), 0)),
                   pl.BlockSpec((CONV_K, 1, DN_CONV_CH), lambda n: (0, 0, 0)), vec, vec,
                   pl.BlockSpec((1, DN_DV), lambda n: (0, 0))],
        out_shape=[jax.ShapeDtypeStruct((rows, DN_IN_PAD), BF16),
                   jax.ShapeDtypeStruct((CONV_K, 1, DN_CONV_CH), F32),
                   jax.ShapeDtypeStruct((1, LANES), F32), jax.ShapeDtypeStruct((1, LANES), F32),
                   jax.ShapeDtypeStruct((1, DN_DV), F32)],
        scratch_shapes=[pltpu.VMEM((DN_HEADS, DN_DK, DN_DV), F32), pltpu.VMEM((8, DN_CONV_CH), F32)],
        name="dn_bwd", compiler_params=_params("arbitrary"))(
            proj, proj, proj, proj, conv_w, alog, dtb, norm_w.reshape(1, DN_DV), o, dy, states, tinv)


def _train_step(x, tgt, wts, sh, idx):
    seq = x.shape[0]
    rows = -(-(seq + CHUNK) // ROW_ALIGN) * ROW_ALIGN
    tail = rows - seq - CHUNK
    h0 = jnp.concatenate([jnp.zeros((PAD, D_MODEL), F32), wts["meta_tokens"].astype(F32), x,
                          jnp.zeros((tail, D_MODEL), F32)], axis=0)
    tgt_p = jnp.concatenate([jnp.zeros((CHUNK, D_MODEL), F32), tgt, jnp.zeros((tail, D_MODEL), F32)],
                            axis=0)
    cos, sin = _rope_tables(rows)
    consts = _ret_consts()
    conv_w = wts["dn_conv_w"].reshape(CONV_K, 1, DN_CONV_CH)
    lane_pad = LANES - 2 * DN_HEADS
    alog = jnp.pad(wts["dn_a_log"].reshape(1, DN_HEADS), ((0, 0), (DECAY_LANE, lane_pad)))
    dtb = jnp.pad(wts["dn_dt_bias"].reshape(1, DN_HEADS), ((0, 0), (DECAY_LANE, lane_pad)))
    g = {}

    wts = dict(wts)
    (got,) = _gather_weights([sh["ret_w_in"]])
    wts["ret_w_in"] = got.reshape(N_SHARD, D_MODEL, -1)
    (hn0, proj0), got = _norm_mm_cols(h0, wts["mix_norm_w"][0], wts["ret_w_in"], "ret_in",
                                      ride=_Ride("gather", [sh["ret_w_out"], sh["ffn_w_gate"], sh["dn_w_out"]]))
    wts["ret_w_out"] = got[0].reshape(-1, D_MODEL)
    wts["ffn_w_gate"] = got[1]
    wts["dn_w_out"] = got[2].reshape(-1, D_MODEL)
    (o0, y0, st0), got = _ret_fwd(proj0, cos, sin, consts, wts["ret_gn_w"], seq,
                                  ride=_Ride("gather", [sh["ffn_w_up"], sh["ffn_w_down"], sh["dn_w_in"]]))
    wts["ffn_w_up"], wts["ffn_w_down"] = got[0], got[1]
    n_dn = sh["dn_w_in"].shape[-1]
    wts["dn_w_in"] = jnp.pad(_join_cols(got[2].reshape(N_SHARD, D_MODEL, n_dn)),
                             ((0, 0), (0, DN_IN_PAD - N_SHARD * n_dn)))
    h1 = _mm(y0, wts["ret_w_out"], mode="nn", name="ret_out", resid=h0)
    hn1, g0, u0, act0 = _ffn_up(h1, wts["ffn_norm_w"][0], wts["ffn_w_gate"], wts["ffn_w_up"], 0, "ffn_up0")
    h2 = _ffn_down(act0, wts["ffn_w_down"], h1, 0, "ffn_down0")
    hn2, proj1 = _norm_mm(h2, wts["mix_norm_w"][1], wts["dn_w_in"], "dn_in")
    o1, y1, st1, tinv = _dn_fwd(proj1, conv_w, alog, dtb, wts["dn_norm_w"], seq)
    h3 = _mm(y1, wts["dn_w_out"], mode="nn", name="dn_out", resid=h2)
    hn3, g1, u1, act1 = _ffn_up(h3, wts["ffn_norm_w"][1], wts["ffn_w_gate"], wts["ffn_w_up"], 1, "ffn_up1")
    h4 = _ffn_down(act1, wts["ffn_w_down"], h3, 1, "ffn_down1")

    dh4, g["final_norm_w"], loss = _final_loss(h4, wts["final_norm_w"], tgt_p, seq, "final_loss")

    layers = wts["ffn_w_gate"].shape[1]

    def ffn_bwd(dh_out, h_mid, hn, gg, uu, act, layer, prev, ride=None):
        tag = str(layer)
        res = _ffn_down_bwd(dh_out, wts["ffn_w_down"], gg, uu, layer, "ffn_down_bwd" + tag, ride=ride)
        (dg, du), rode = res if ride is not None else (res, None)
        d_down = _ffn_wgrad(act, [dh_out], layer, layers, prev and prev[:1], True, "ffn_dwd" + tag)
        d_gu = _ffn_wgrad(hn, [dg, du], layer, layers, prev and prev[1:], False, "ffn_dwgu" + tag)
        dh_mid, d_norm = _ffn_up_bwd(dg, du, wts["ffn_w_gate"], wts["ffn_w_up"], layer, h_mid,
                                     wts["ffn_norm_w"][layer], dh_out, "ffn_up_bwd" + tag)
        return dh_mid, list(d_down) + list(d_gu), d_norm, rode

    red = {}

    def rs_grads(names, grads):
        return [gr.reshape((N_SHARD,) + sh[n].shape) for n, gr in zip(names, grads)]

    def rs_partials(names, gs, sib):
        return [_rs_pair_add(gs[t], sib[t], idx, "rs_pair_add_" + n) for t, n in enumerate(names)]

    def rs_end(names, gs, sib, others, tag):
        mine = [_rs_final_add(gs[t], sib[t], others[t], idx, "rs_final_add_" + n) for t, n in enumerate(names)]
        red.update(zip(names, _rs_share(mine, "rs_share" + tag)))

    dh3, ffn_grads, dfn1, _ = ffn_bwd(dh4, h3, hn3, g1, u1, act1, 1, None)
    dy1 = _mm(dh3, wts["dn_w_out"], mode="nt", name="dn_out_bwd")
    d_dn_out = _mm(y1, dh3, mode="tn", name="dn_dwo")
    dproj1, dcw, dal, ddt, g["dn_norm_w"] = _dn_bwd(proj1, o1, dy1, st1, tinv, conv_w, alog, dtb,
                                                    wts["dn_norm_w"], seq)
    d_dn_in = _mm(hn2, dproj1, mode="tn", name="dn_dwi")
    d_dn_in = jnp.stack([d_dn_in[:, j * n_dn:(j + 1) * n_dn] for j in range(N_SHARD)])
    group1 = ["dn_w_out", "dn_w_in"]
    gs1 = rs_grads(group1, [d_dn_out, d_dn_in])
    (dh2, dmn1), sib1 = _mm_nt_rms(dproj1, wts["dn_w_in"], h2, wts["mix_norm_w"][1], dh3, "dn_in_bwd",
                                   ride=_Ride("pair", gs1))
    g["dn_conv_w"] = dcw.reshape(CONV_K, DN_CONV_CH)
    g["dn_a_log"] = dal[0, DECAY_LANE:DECAY_LANE + DN_HEADS]
    g["dn_dt_bias"] = ddt[0, DECAY_LANE:DECAY_LANE + DN_HEADS]

    dh1, ffn_grads, dfn0, others1 = ffn_bwd(dh2, h1, hn1, g0, u0, act0, 0, ffn_grads,
                                            ride=_Ride("chips", rs_partials(group1, gs1, sib1)))
    rs_end(group1, gs1, sib1, others1, "1")
    d_ret_out = _mm(y0, dh1, mode="tn", name="ret_dwo")
    group2 = ["ffn_w_down", "ffn_w_gate", "ffn_w_up", "ret_w_out"]
    gs2 = rs_grads(group2, list(ffn_grads) + [d_ret_out])
    dy0, sib2 = _mm(dh1, wts["ret_w_out"], mode="nt", name="ret_out_bwd", ride=_Ride("pair", gs2))
    (dproj0, g["ret_gn_w"]), others2 = _ret_bwd(proj0, o0, dy0, st0, cos, sin, consts, wts["ret_gn_w"], seq,
                                                ride=_Ride("chips", rs_partials(group2, gs2, sib2)))
    rs_end(group2, gs2, sib2, others2, "2")
    d_ret_in = _mm_cols_grad(hn0, dproj0, "ret_dwi")
    gs3 = rs_grads(["ret_w_in"], [d_ret_in])
    sib3 = _rs_pair(gs3, "rs_pair3")
    (dh0, dmn0), others3 = _mm_cols_t_rms(dproj0, wts["ret_w_in"], h0, wts["mix_norm_w"][0], dh1, "ret_in_bwd",
                                          ride=_Ride("chips", rs_partials(["ret_w_in"], gs3, sib3)))
    rs_end(["ret_w_in"], gs3, sib3, others3, "3")

    g["ffn_norm_w"] = jnp.concatenate([dfn0, dfn1], axis=0)
    g["mix_norm_w"] = jnp.concatenate([dmn0, dmn1], axis=0)
    g["meta_tokens"] = dh0[PAD:CHUNK]
    g["final_norm_w"] = g["final_norm_w"].reshape(D_MODEL)
    g["ret_gn_w"] = g["ret_gn_w"].reshape(RET_DV)
    g["dn_norm_w"] = g["dn_norm_w"].reshape(DN_DV)
    return loss, dh0, g, red


def _mesh_pos():
    return lax.axis_index("x"), lax.axis_index("y"), lax.axis_index("c")


def _other_chips(x, y):
    return [(1 - x, y), (x, 1 - y), (1 - x, 1 - y)]


def _remote(src, dst, send_sem, recv_sem, to):
    return pltpu.make_async_remote_copy(src_ref=src, dst_ref=dst, send_sem=send_sem, recv_sem=recv_sem,
                                        device_id=to, device_id_type=MESH)


GATHER_COPIES = 7


def _gather_weights(shards):
    ride = _Ride("gather", shards)

    def body(*refs):
        nt = len(shards)
        for phase in range(3):
            _gather_phase(phase, refs[:nt], refs[nt:2 * nt], *refs[2 * nt:])

    return pl.pallas_call(body, out_shape=ride.out_shape, in_specs=ride.in_specs, out_specs=ride.out_specs,
                          scratch_shapes=ride.scratch, name="gather_weights")(*shards)


def _gather_phase(phase, ins, outs, send_sems, recv_sems):
    x, y, c = _mesh_pos()
    me = 2 * x + y
    chips = _other_chips(x, y)
    sibling = (x, y, 1 - c)

    def cp(t, k, src, dst, to):
        i = GATHER_COPIES * t + k
        return _remote(src, dst, send_sems.at[i], recv_sems.at[i], to)

    for t in range(len(ins)):
        own = cp(t, 0, ins[t], outs[t].at[me], sibling)
        if phase == 0:
            own.start()
        if phase == 2:
            own.wait()
        for k, (px, py) in enumerate(chips):
            landed = outs[t].at[2 * px + py, c]
            theirs = outs[t].at[2 * px + py, 1 - c]
            to_chip = cp(t, 1 + k, ins[t].at[c], outs[t].at[me, c], (px, py, c))
            if phase == 0:
                to_chip.start()
            if phase == 1:
                cp(t, 1 + k, ins[t].at[c], landed, (px, py, c)).wait_recv()
                cp(t, 4 + k, landed, landed, sibling).start()
            if phase == 2:
                to_chip.wait_send()
                cp(t, 4 + k, landed, landed, sibling).wait_send()
                cp(t, 4 + k, theirs, theirs, sibling).wait_recv()


def _chips_phase(phase, ins, outs, send_sems, recv_sems):
    x, y, c = _mesh_pos()
    for t in range(len(ins)):
        for k, (px, py) in enumerate(_other_chips(x, y)):
            cp = _remote(ins[t].at[2 * px + py], outs[t].at[k], send_sems.at[3 * t + k], recv_sems.at[3 * t + k],
                         (px, py, c))
            if phase == 0:
                cp.start()
            if phase == 2:
                cp.wait()


class _Ride:
    def __init__(self, kind, arrays):
        self.kind, self.arrays = kind, list(arrays)
        nt = len(self.arrays)
        if kind == "gather":
            self.phase_fn, n_sem = _gather_phase, GATHER_COPIES * nt
            self.out_shape = [jax.ShapeDtypeStruct((N_SHARD,) + a.shape, a.dtype) for a in self.arrays]
        elif kind == "pair":
            self.phase_fn, n_sem = _pair_phase, nt
            self.out_shape = [jax.ShapeDtypeStruct(a.shape[:1] + a.shape[2:], a.dtype) for a in self.arrays]
        else:
            self.phase_fn, n_sem = _chips_phase, 3 * nt
            self.out_shape = [jax.ShapeDtypeStruct((3,) + a.shape[1:], a.dtype) for a in self.arrays]
        self.in_specs, self.out_specs = [ANY] * nt, [ANY] * nt
        self.scratch = [pltpu.SemaphoreType.DMA((n_sem,)), pltpu.SemaphoreType.DMA((n_sem,))]

    def emit(self, step, nsteps, ins, outs, send_sems, recv_sems, before):
        mid = max(0, min((7 * nsteps) // 8, nsteps - 2))
        todo = [(0, 0), (1, mid)] if before else [(2, nsteps - 1)]
        for phase, at in todo:
            if phase == 1 and self.kind != "gather":
                continue

            @pl.when(step == at)
            def _(phase=phase):
                self.phase_fn(phase, ins, outs, send_sems, recv_sems)


def _gather_small(blk):
    r, wd = blk.shape

    def body(b_ref, out_ref, send_sems, recv_sems):
        x, y, c = _mesh_pos()
        chips = _other_chips(x, y)
        out_ref[2 * x + y] = b_ref[...]
        sends = [_remote(b_ref, out_ref.at[2 * x + y], send_sems.at[k], recv_sems.at[k], (px, py, c))
                 for k, (px, py) in enumerate(chips)]
        for cp in sends:
            cp.start()
        for k, (px, py) in enumerate(chips):
            _remote(b_ref, out_ref.at[2 * px + py], send_sems.at[k], recv_sems.at[k], (px, py, c)).wait_recv()
        for cp in sends:
            cp.wait_send()

    return pl.pallas_call(
        body, out_shape=jax.ShapeDtypeStruct((4, r, wd), blk.dtype), in_specs=[VMEM_SPEC], out_specs=VMEM_SPEC,
        scratch_shapes=[pltpu.SemaphoreType.DMA((3,)), pltpu.SemaphoreType.DMA((3,))],
        name="gather_small")(blk)


def _allreduce_small(blk):
    r, wd = blk.shape
    rels = [(dx, dy, dc) for dx in (0, 1) for dy in (0, 1) for dc in (0, 1) if dx or dy or dc]

    def body(b_ref, out_ref, buf_ref, send_sems, recv_sems):
        x, y, c = _mesh_pos()

        def peer(rel):
            dx, dy, dc = rel
            return (1 - x if dx else x, 1 - y if dy else y, 1 - c if dc else c)

        me = 4 * x + 2 * y + c
        buf_ref[me] = b_ref[...]
        sends = [_remote(b_ref, buf_ref.at[me], send_sems.at[k], recv_sems.at[k], peer(rel))
                 for k, rel in enumerate(rels)]
        for cp in sends:
            cp.start()
        for k, rel in enumerate(rels):
            px, py, pc = peer(rel)
            _remote(b_ref, buf_ref.at[4 * px + 2 * py + pc], send_sems.at[k], recv_sems.at[k],
                    (px, py, pc)).wait_recv()
        for cp in sends:
            cp.wait_send()
        acc = buf_ref[0]
        for d in range(1, 8):
            acc = acc + buf_ref[d]
        out_ref[...] = acc

    return pl.pallas_call(
        body, out_shape=jax.ShapeDtypeStruct((r, wd), blk.dtype), in_specs=[VMEM_SPEC], out_specs=VMEM_SPEC,
        scratch_shapes=[pltpu.VMEM((8, r, wd), blk.dtype), pltpu.SemaphoreType.DMA((7,)),
                        pltpu.SemaphoreType.DMA((7,))],
        name="allreduce_small")(blk)


def _rs_pair(gs, name):
    ride = _Ride("pair", gs)

    def body(*refs):
        nt = len(gs)
        for phase in (0, 2):
            _pair_phase(phase, refs[:nt], refs[nt:2 * nt], *refs[2 * nt:])

    return pl.pallas_call(body, out_shape=ride.out_shape, in_specs=ride.in_specs, out_specs=ride.out_specs,
                          scratch_shapes=ride.scratch, name=name)(*gs)


def _pair_phase(phase, ins, outs, send_sems, recv_sems):
    x, y, c = _mesh_pos()
    for t in range(len(ins)):
        cp = _remote(ins[t].at[:, 1 - c], outs[t], send_sems.at[t], recv_sems.at[t], (x, y, 1 - c))
        if phase == 0:
            cp.start()
        if phase == 2:
            cp.wait()


def _rs_tile(a, b):
    return _div_tile(a, 512 if b <= 1024 else 256, 16)


def _rs_pair_add(g, a, idx, name):
    _, _, rows, cols = g.shape
    tr = _rs_tile(rows, cols)

    def body(s_ref, g_ref, a_ref, p_ref):
        p_ref[...] = (g_ref[...] + a_ref[...]).astype(p_ref.dtype)

    blk = pl.BlockSpec((None, tr, cols), lambda j, i, s: (j, i, 0))
    spec = pltpu.PrefetchScalarGridSpec(
        num_scalar_prefetch=1, grid=(N_SHARD, rows // tr),
        in_specs=[pl.BlockSpec((None, None, tr, cols), lambda j, i, s: (j, s[0], i, 0)), blk], out_specs=blk)
    return pl.pallas_call(
        body, grid_spec=spec, out_shape=jax.ShapeDtypeStruct((N_SHARD, rows, cols), BF16), name=name,
        compiler_params=_params("parallel", "parallel"))(idx, g, a)


def _rs_final_add(g, a, b, idx, name):
    _, _, rows, cols = g.shape
    tr = _rs_tile(rows, cols)

    def body(s_ref, g_ref, a_ref, b0_ref, b1_ref, b2_ref, f_ref):
        own = g_ref[...] + a_ref[...]
        f_ref[...] = ((own + b0_ref[...].astype(F32)) + b1_ref[...].astype(F32)) + b2_ref[...].astype(F32)

    def b_spec(k):
        return pl.BlockSpec((None, tr, cols), lambda i, s: (k, i, 0))

    spec = pltpu.PrefetchScalarGridSpec(
        num_scalar_prefetch=1, grid=(rows // tr,),
        in_specs=[pl.BlockSpec((None, None, tr, cols), lambda i, s: (s[1], s[0], i, 0)),
                  pl.BlockSpec((None, tr, cols), lambda i, s: (s[1], i, 0)), b_spec(0), b_spec(1), b_spec(2)],
        out_specs=pl.BlockSpec((None, tr, cols), lambda i, s: (s[0], i, 0)))
    return pl.pallas_call(
        body, grid_spec=spec, out_shape=jax.ShapeDtypeStruct((2, rows, cols), F32), name=name,
        compiler_params=_params("parallel"))(idx, g, a, b, b, b)


def _rs_share(fs, name):
    nt = len(fs)

    def body(*refs):
        outs = refs[nt:2 * nt]
        send_sems, recv_sems = refs[2 * nt:]
        x, y, c = _mesh_pos()
        cps = [_remote(outs[t].at[c], outs[t].at[c], send_sems.at[t], recv_sems.at[t], (x, y, 1 - c))
               for t in range(nt)]
        for cp in cps:
            cp.start()
        for cp in cps:
            cp.wait()

    return pl.pallas_call(
        body, out_shape=[jax.ShapeDtypeStruct(f.shape, f.dtype) for f in fs],
        in_specs=[ANY] * nt, out_specs=[ANY] * nt, input_output_aliases={t: t for t in range(nt)},
        scratch_shapes=[pltpu.SemaphoreType.DMA((nt,)), pltpu.SemaphoreType.DMA((nt,))], name=name)(*fs)


def _adamw(w, g, m, v, name):
    lead, rows, cols = w.shape
    tr = rows // 4 if rows % 32 == 0 else rows

    def body(w_ref, g_ref, m_ref, v_ref, go_ref, d_ref, mo_ref, vo_ref):
        gv = g_ref[...]
        go_ref[...] = gv
        mn = ADAM_B1 * m_ref[...] + (1.0 - ADAM_B1) * gv
        vn = ADAM_B2 * v_ref[...] + (1.0 - ADAM_B2) * (gv * gv)
        m_hat = mn / (1.0 - ADAM_B1 ** ADAM_STEP)
        v_hat = vn / (1.0 - ADAM_B2 ** ADAM_STEP)
        d_ref[...] = -ADAM_LR * (m_hat / (jnp.sqrt(v_hat) + ADAM_EPS) + ADAM_WD * w_ref[...])
        mo_ref[...] = mn
        vo_ref[...] = vn

    blk = pl.BlockSpec((None, tr, cols), lambda l, i: (l, i, 0))
    out = jax.ShapeDtypeStruct((lead, rows, cols), F32)
    return pl.pallas_call(
        body, grid=(lead, rows // tr), in_specs=[blk] * 4, out_specs=[blk] * 4, out_shape=[out] * 4, name=name,
        compiler_params=_params("parallel", "parallel"))(w, g, m, v)


BIG = ["ret_w_in", "ret_w_out", "dn_w_in", "dn_w_out", "ffn_w_gate", "ffn_w_up", "ffn_w_down"]
TRANSPOSED_AT_BOUNDARY = {"dn_w_in": True, "ffn_w_gate": False, "ffn_w_up": False}
SMALL =["meta_tokens", "mix_norm_w", "ffn_norm_w", "ret_gn_w", "dn_conv_w", "dn_a_log", "dn_dt_bias",
         "dn_norm_w", "final_norm_w"]
SMALL_SHARDED = {"meta_tokens", "dn_conv_w", "dn_norm_w"}
ORDER = ["meta_tokens", "mix_norm_w", "ffn_norm_w", "ret_w_in", "ret_gn_w", "ret_w_out", "dn_w_in",
         "dn_conv_w", "dn_a_log", "dn_dt_bias", "dn_norm_w", "dn_w_out", "ffn_w_gate", "ffn_w_up",
         "ffn_w_down", "final_norm_w"]


def _halves(a):
    return a.reshape(2, -1, a.shape[-1])


def _pack_lanes(parts, align=8):
    flat = jnp.concatenate([p.reshape(-1) for p in parts])
    flat = jnp.pad(flat, (0, -flat.shape[0] % (align * LANES)))
    return flat.reshape(-1, LANES)


def _unpack(buf, shapes):
    lead = buf.shape[:-2]
    flat = buf.reshape(lead + (-1,))
    out, off = [], 0
    for shp in shapes:
        size = math.prod(shp)
        out.append(flat[..., off:off + size].reshape(lead + tuple(shp)))
        off += size
    return out


def _join_cols(shards):
    return jnp.concatenate([shards[j] for j in range(N_SHARD)], axis=-1)


def kernel(x, meta_tokens, mix_norm_w, ffn_norm_w, ret_w_in, ret_gn_w, ret_w_out, dn_w_in, dn_conv_w, dn_a_log, dn_dt_bias, dn_norm_w, dn_w_out, ffn_w_gate, ffn_w_up, ffn_w_down, final_norm_w, loss_target, m_meta_tokens, m_mix_norm_w, m_ffn_norm_w, m_ret_w_in, m_ret_gn_w, m_ret_w_out, m_dn_w_in, m_dn_conv_w, m_dn_a_log, m_dn_dt_bias, m_dn_norm_w, m_dn_w_out, m_ffn_w_gate, m_ffn_w_up, m_ffn_w_down, m_final_norm_w, v_meta_tokens, v_mix_norm_w, v_ffn_norm_w, v_ret_w_in, v_ret_gn_w, v_ret_w_out, v_dn_w_in, v_dn_conv_w, v_dn_a_log, v_dn_dt_bias, v_dn_norm_w, v_dn_w_out, v_ffn_w_gate, v_ffn_w_up, v_ffn_w_down, v_final_norm_w):
    w = dict(meta_tokens=meta_tokens, mix_norm_w=mix_norm_w, ffn_norm_w=ffn_norm_w, ret_w_in=ret_w_in,
             ret_gn_w=ret_gn_w, ret_w_out=ret_w_out, dn_w_in=dn_w_in, dn_conv_w=dn_conv_w, dn_a_log=dn_a_log,
             dn_dt_bias=dn_dt_bias, dn_norm_w=dn_norm_w, dn_w_out=dn_w_out, ffn_w_gate=ffn_w_gate,
             ffn_w_up=ffn_w_up, ffn_w_down=ffn_w_down, final_norm_w=final_norm_w)
    m = dict(meta_tokens=m_meta_tokens, mix_norm_w=m_mix_norm_w, ffn_norm_w=m_ffn_norm_w, ret_w_in=m_ret_w_in,
             ret_gn_w=m_ret_gn_w, ret_w_out=m_ret_w_out, dn_w_in=m_dn_w_in, dn_conv_w=m_dn_conv_w,
             dn_a_log=m_dn_a_log, dn_dt_bias=m_dn_dt_bias, dn_norm_w=m_dn_norm_w, dn_w_out=m_dn_w_out,
             ffn_w_gate=m_ffn_w_gate, ffn_w_up=m_ffn_w_up, ffn_w_down=m_ffn_w_down, final_norm_w=m_final_norm_w)
    v = dict(meta_tokens=v_meta_tokens, mix_norm_w=v_mix_norm_w, ffn_norm_w=v_ffn_norm_w, ret_w_in=v_ret_w_in,
             ret_gn_w=v_ret_gn_w, ret_w_out=v_ret_w_out, dn_w_in=v_dn_w_in, dn_conv_w=v_dn_conv_w,
             dn_a_log=v_dn_a_log, dn_dt_bias=v_dn_dt_bias, dn_norm_w=v_dn_norm_w, dn_w_out=v_dn_w_out,
             ffn_w_gate=v_ffn_w_gate, ffn_w_up=v_ffn_w_up, ffn_w_down=v_ffn_w_down, final_norm_w=v_final_norm_w)
    mx, my, mc = _mesh_pos()
    chip = 2 * mx + my

    sm_names = [n for n in SMALL if n in SMALL_SHARDED]
    sm_gathered = _unpack(_gather_small(_pack_lanes([w[n] for n in sm_names])), [w[n].shape for n in sm_names])
    full = {n: _join_cols(sm_gathered[i]) for i, n in enumerate(sm_names)}
    wts = {
        "meta_tokens": full["meta_tokens"], "mix_norm_w": mix_norm_w, "ffn_norm_w": ffn_norm_w,
        "ret_gn_w": ret_gn_w[0], "final_norm_w": final_norm_w, "dn_conv_w": full["dn_conv_w"][0],
        "dn_a_log": dn_a_log[0], "dn_dt_bias": dn_dt_bias[0], "dn_norm_w": full["dn_norm_w"][0],
    }
    idx = jnp.stack([mc, chip]).astype(jnp.int32)
    shards = {n: _halves(w[n].astype(MXU_DTYPE)) for n in BIG}
    loss_part, dh0, g, reduced = _train_step(x[0], loss_target[0], wts, shards, idx)
    seq = x.shape[1]
    grad_x = dh0[CHUNK:CHUNK + seq].reshape(x.shape)
    gsh = {}

    small_full_shapes = [g[n].shape for n in SMALL] + [(1,)]
    red = _unpack(_allreduce_small(_pack_lanes([g[n] for n in SMALL] + [loss_part[0, :1]])), small_full_shapes)
    loss = red[-1][0]
    for i, n in enumerate(SMALL):
        gn = red[i]
        if n in SMALL_SHARDED:
            width = w[n].shape[-1]
            gn = lax.dynamic_slice_in_dim(gn, chip * width, width, axis=gn.ndim - 1)
        gsh[n] = gn.reshape(w[n].shape)

    delta, new_m, new_v = {}, {}, {}
    for n in BIG:
        shp = w[n].shape
        if n in TRANSPOSED_AT_BOUNDARY and TRANSPOSED_AT_BOUNDARY[n]:
            view = lambda a: jnp.swapaxes(a, 1, 2).reshape(1, -1, LANES)
            back = lambda a: jnp.swapaxes(a.reshape(shp[0], shp[2], shp[1]), 1, 2)
        elif n in TRANSPOSED_AT_BOUNDARY:
            view = back = lambda a: jnp.swapaxes(a, 1, 2)
        else:
            view = back = lambda a: a
        res = _adamw(view(w[n]), view(reduced[n].reshape(shp)), view(m[n]), view(v[n]), "adamw_" + n)
        gsh[n], delta[n], new_m[n], new_v[n] = [back(r) for r in res]
    sm_local_shapes = [w[n].shape for n in SMALL]
    _, d_, m_, v_ = _adamw(*[_pack_lanes([t[n] for n in SMALL])[None] for t in (w, gsh, m, v)], "adamw_small")
    d_, m_, v_ = d_[0], m_[0], v_[0]
    for n, dd, mm, vv in zip(SMALL, _unpack(d_, sm_local_shapes), _unpack(m_, sm_local_shapes),
                             _unpack(v_, sm_local_shapes)):
        delta[n], new_m[n], new_v[n] = dd, mm, vv

    return (loss, grad_x, *[gsh[n] for n in ORDER], *[delta[n] for n in ORDER],
            *[new_m[n] for n in ORDER], *[new_v[n] for n in ORDER])
```

```python
import functools
import math

import jax
import jax.numpy as jnp
from jax import lax
from jax.experimental import pallas as pl
from jax.experimental.pallas import tpu as pltpu

F32 = jnp.float32
BF16 = jnp.bfloat16
MXU_DTYPE = BF16

D_MODEL = 1024
N_META = 16
CHUNK = 64
PAD = CHUNK - N_META
RMS_EPS = 1e-6
RET_HEADS, RET_DK, RET_DV = 4, 256, 512
RET_QK, RET_V = RET_HEADS * RET_DK, RET_HEADS * RET_DV
RET_IN = 2 * RET_QK + 2 * RET_V
ROPE_BASE = 10000.0
DN_HEADS, DN_DK, DN_DV = 8, 128, 256
DN_QK, DN_V = DN_HEADS * DN_DK, DN_HEADS * DN_DV
DN_CONV_CH = 2 * DN_QK + DN_V
DN_IN = DN_CONV_CH + DN_V + 2 * DN_HEADS
LANES = 128
DN_IN_USED = DN_CONV_CH + DN_V + LANES
DN_IN_PAD = DN_IN_USED + LANES
CONV_K = 4
FFN_HIDDEN = 2816
ADAM_LR, ADAM_B1, ADAM_B2, ADAM_EPS, ADAM_WD, ADAM_STEP = 0.001, 0.9, 0.999, 1e-08, 0.01, 10

ROW_ALIGN = 256
VMEM_LIMIT = 56 * 1024 * 1024
MESH = pl.DeviceIdType.MESH
ANY = pl.BlockSpec(memory_space=pl.ANY)
VMEM_SPEC = pl.BlockSpec(memory_space=pltpu.VMEM)
_HI = lax.Precision.HIGHEST


def _params(*sem):
    return pltpu.CompilerParams(dimension_semantics=sem, vmem_limit_bytes=VMEM_LIMIT)


def _dg(a, b, ca, cb, hi):
    dims = (((ca,), (cb,)), ((), ()))

    def dot(p, q):
        return lax.dot_general(p, q, dims, preferred_element_type=F32)

    if not hi:
        return dot(a.astype(MXU_DTYPE), b.astype(MXU_DTYPE))
    if MXU_DTYPE == F32:
        return lax.dot_general(a, b, dims, precision=_HI, preferred_element_type=F32)
    a_hi, b_hi = a.astype(MXU_DTYPE), b.astype(MXU_DTYPE)
    a_lo = (a - a_hi.astype(F32)).astype(MXU_DTYPE)
    b_lo = (b - b_hi.astype(F32)).astype(MXU_DTYPE)
    return dot(a_hi, b_hi) + (dot(a_hi, b_lo) + dot(a_lo, b_hi))


def _nn(a, b, hi=False):
    return _dg(a, b, 1, 0, hi)


def _nt(a, b, hi=False):
    return _dg(a, b, 1, 1, hi)


def _tn(a, b, hi=False):
    return _dg(a, b, 0, 0, hi)


def _iota(shape, dim):
    return lax.broadcasted_iota(jnp.int32, shape, dim)


def _valid_rows(first_row, rows, seq):
    r = first_row + _iota((rows, 1), 0)
    return ((r >= PAD) & (r < CHUNK + seq)).astype(F32)


def _rope(t, cs, sn):
    half = t.shape[-1] // 2
    t1, t2 = t[:, :half], t[:, half:]
    return jnp.concatenate([t1 * cs - t2 * sn, t1 * sn + t2 * cs], axis=1)


def _rope_bwd(d, cs, sn):
    half = d.shape[-1] // 2
    d1, d2 = d[:, :half], d[:, half:]
    return jnp.concatenate([d1 * cs + d2 * sn, d2 * cs - d1 * sn], axis=1)


def _col(x, idx):
    oh = (_iota((1, x.shape[1]), 1) == idx).astype(F32)
    return jnp.sum(x * oh, axis=1, keepdims=True)


def _row(x, idx):
    oh = (_iota((x.shape[0], 1), 0) == idx).astype(F32)
    return jnp.sum(x * oh, axis=0, keepdims=True)


def _shift_down(x, halo8, k):
    xr = pltpu.roll(x, k, 0)
    hr = pltpu.roll(halo8, k, 0)
    first = jnp.where(_iota((8, 1), 0) < k, hr, xr[0:8])
    return jnp.concatenate([first, xr[8:]], axis=0)


def _shift_up(x, next8, j):
    rows = x.shape[0]
    xr = pltpu.roll(x, rows - j, 0)
    nr = pltpu.roll(next8, 8 - j, 0)
    last = jnp.where(_iota((8, 1), 0) >= 8 - j, nr, xr[rows - 8:])
    return jnp.concatenate([xr[:rows - 8], last], axis=0)


def _gated_norm(o, gate, w):
    r = lax.rsqrt(jnp.mean(o * o, axis=-1, keepdims=True) + RMS_EPS)
    return o * r * w * (gate * jax.nn.sigmoid(gate))


def _gated_norm_bwd(dy, o, gate, w):
    r = lax.rsqrt(jnp.mean(o * o, axis=-1, keepdims=True) + RMS_EPS)
    nrm = o * r
    sg = jax.nn.sigmoid(gate)
    sl = gate * sg
    dgate = dy * nrm * w * (sg * (1.0 + gate * (1.0 - sg)))
    dn = dy * w * sl
    dw = jnp.sum(dy * nrm * sl, axis=0, keepdims=True)
    do = r * (dn - nrm * jnp.mean(dn * nrm, axis=-1, keepdims=True))
    return do, dgate, dw


def _softplus(z):
    return jnp.maximum(z, 0.0) + jnp.log(1.0 + jnp.exp(-jnp.abs(z)))


def _row_tile(rows, cap=768):
    for t in (768, 512, 256, 128, 64, 32, 16, 8):
        if t <= cap and rows % t == 0:
            return t
    return rows


TILE_BUDGET = 44 * 1024 * 1024


def _fit_rows(rows, row_bytes, fixed_bytes, value_row_bytes):
    best = None
    for t in range(LANES, rows + 1, LANES):
        if rows % t == 0 and 2 * (row_bytes * t + fixed_bytes) + value_row_bytes * t <= TILE_BUDGET:
            best = t
    return best or _row_tile(rows, 256)


def _div_tile(n, cap, mult):
    best = None
    for t in range(mult, min(cap, n) + 1, mult):
        if n % t == 0:
            best = t
    return best or n


def _col_tile(cols, cap=1536):
    best = None
    for t in range(LANES, min(cap, cols) + 1, LANES):
        if cols % t == 0:
            best = t
    return best or cols


def _rms_fwd(h, w, name):
    rows, d = h.shape
    tm = _row_tile(rows)

    def body(h_ref, w_ref, o_ref):
        x = h_ref[...]
        r = lax.rsqrt(jnp.mean(x * x, axis=-1, keepdims=True) + RMS_EPS)
        o_ref[...] = (x * r * w_ref[...]).astype(o_ref.dtype)

    return pl.pallas_call(
        body, grid=(rows // tm,),
        in_specs=[pl.BlockSpec((tm, d), lambda i: (i, 0)), pl.BlockSpec((1, d), lambda i: (0, 0))],
        out_specs=pl.BlockSpec((tm, d), lambda i: (i, 0)),
        out_shape=jax.ShapeDtypeStruct((rows, d), BF16), name=name,
        compiler_params=_params("parallel"))(h, w.reshape(1, d))


def _gmm_rms(name, grid, args, in_specs, row_spec, fn, h, w, resid, row_axis, red_axis=None, ride=None):
    m, d = h.shape
    n_in = len(args)
    vec = pl.BlockSpec((1, d), lambda *g: (0, 0))

    def body(*refs):
        ins = refs[:n_in]
        h_ref, w_ref, r_ref, dh_ref, dw_ref = refs[n_in:]
        part = fn(*ins)
        row = pl.program_id(row_axis)

        def finish(dy):
            x = h_ref[...]
            r = lax.rsqrt(jnp.mean(x * x, axis=-1, keepdims=True) + RMS_EPS)
            xh = x * r
            dxh = dy * w_ref[...]
            dh_ref[...] = r_ref[...] + r * (dxh - xh * jnp.mean(dxh * xh, axis=-1, keepdims=True))
            dwp = jnp.sum(dy * xh, axis=0, keepdims=True)

            @pl.when(row == 0)
            def _():
                dw_ref[...] = dwp

            @pl.when(row > 0)
            def _():
                dw_ref[...] += dwp

        if red_axis is None:
            finish(part)
            return
        k = pl.program_id(red_axis)

        @pl.when(k == 0)
        def _():
            dh_ref[...] = part

        @pl.when(k > 0)
        def _():
            dh_ref[...] += part

        @pl.when(k == grid[red_axis] - 1)
        def _():
            finish(dh_ref[...])

    res, rode = _pcall(body, list(args) + [h, w.reshape(1, d), resid], grid=grid,
                       in_specs=list(in_specs) + [row_spec, vec, row_spec], out_specs=[row_spec, vec],
                       out_shape=[jax.ShapeDtypeStruct((m, d), F32), jax.ShapeDtypeStruct((1, d), F32)],
                       name=name, sem=("arbitrary",) * len(grid), ride=ride)
    return res if ride is None else (res, rode)


def _final_loss(h, w, tgt, seq, name):
    rows, d = h.shape
    tm = _row_tile(rows)

    def body(h_ref, w_ref, t_ref, dh_ref, dw_ref, loss_ref):
        i = pl.program_id(0)
        r_idx = i * tm + _iota((tm, 1), 0)
        m = ((r_idx >= CHUNK) & (r_idx < CHUNK + seq)).astype(F32)
        x = h_ref[...]
        wv = w_ref[...]
        r = lax.rsqrt(jnp.mean(x * x, axis=-1, keepdims=True) + RMS_EPS)
        xh = x * r
        err = (xh * wv - t_ref[...]) * m
        lpart = 0.5 * jnp.sum(jnp.mean(err * err, axis=-1, keepdims=True), axis=0, keepdims=True)
        dyv = err * (1.0 / d)
        dxh = dyv * wv
        dh_ref[...] = r * (dxh - xh * jnp.mean(dxh * xh, axis=-1, keepdims=True))
        part = jnp.sum(dyv * xh, axis=0, keepdims=True)

        @pl.when(i == 0)
        def _():
            dw_ref[...] = part
            loss_ref[...] = jnp.broadcast_to(lpart, loss_ref.shape)

        @pl.when(i > 0)
        def _():
            dw_ref[...] += part
            loss_ref[...] += jnp.broadcast_to(lpart, loss_ref.shape)

    blk = pl.BlockSpec((tm, d), lambda i: (i, 0))
    vec = pl.BlockSpec((1, d), lambda i: (0, 0))
    return pl.pallas_call(
        body, grid=(rows // tm,), in_specs=[blk, vec, blk],
        out_specs=[blk, vec, pl.BlockSpec((1, LANES), lambda i: (0, 0))],
        out_shape=[jax.ShapeDtypeStruct((rows, d), F32), jax.ShapeDtypeStruct((1, d), F32),
                   jax.ShapeDtypeStruct((1, LANES), F32)],
        name=name, compiler_params=_params("arbitrary"))(h, w.reshape(1, d), tgt)


def _isz(x):
    return jnp.dtype(x.dtype).itemsize


def _mm(a, b, *, mode, name, out_dtype=F32, resid=None, col_cap=1536, ride=None):
    if mode == "tn":
        m, k = a.shape
        n = b.shape[1]
        tn = _col_tile(n, col_cap)
        tm = _fit_rows(m, k * _isz(a) + tn * _isz(b), (3 * k * tn * 4) // 2, 2 * (k + tn))

        def body_tn(a_ref, b_ref, o_ref):
            i = pl.program_id(1)
            part = _tn(a_ref[...], b_ref[...])

            @pl.when(i == 0)
            def _():
                o_ref[...] = part

            @pl.when(i > 0)
            def _():
                o_ref[...] += part

        return pl.pallas_call(
            body_tn, grid=(n // tn, m // tm),
            in_specs=[pl.BlockSpec((tm, k), lambda j, i: (i, 0)),
                      pl.BlockSpec((tm, tn), lambda j, i: (i, j))],
            out_specs=pl.BlockSpec((k, tn), lambda j, i: (0, j)),
            out_shape=jax.ShapeDtypeStruct((k, n), F32), name=name,
            compiler_params=_params("parallel", "arbitrary"))(a, b)

    m, ka = a.shape
    n = b.shape[1] if mode == "nn" else b.shape[0]
    has_resid = resid is not None
    tn = _col_tile(n, col_cap)
    tm = _fit_rows(m, ka * _isz(a) + tn * (jnp.dtype(out_dtype).itemsize + (4 if has_resid else 0)),
                   ka * tn * _isz(b), 2 * ka + 8 * tn)

    def body(*refs):
        if has_resid:
            a_ref, b_ref, r_ref, o_ref = refs
        else:
            a_ref, b_ref, o_ref = refs
        acc = _nn(a_ref[...], b_ref[...]) if mode == "nn" else _nt(a_ref[...], b_ref[...])
        if has_resid:
            acc = acc + r_ref[...]
        o_ref[...] = acc.astype(o_ref.dtype)

    b_spec = (pl.BlockSpec((b.shape[0], tn), lambda j, i: (0, j)) if mode == "nn"
              else pl.BlockSpec((tn, b.shape[1]), lambda j, i: (j, 0)))
    o_spec = pl.BlockSpec((tm, tn), lambda j, i: (i, j))
    in_specs = [pl.BlockSpec((tm, ka), lambda j, i: (i, 0)), b_spec]
    args = [a, b]
    if has_resid:
        in_specs.append(o_spec)
        args.append(resid)
    res, rode = _pcall(body, args, grid=(n // tn, m // tm), in_specs=in_specs, out_specs=[o_spec],
                       out_shape=[jax.ShapeDtypeStruct((m, n), out_dtype)], name=name,
                       sem=("parallel", "parallel"), ride=ride)
    return res[0] if ride is None else (res[0], rode)


N_SHARD = 4


def _gmm(name, grid, args, in_specs, out_specs, out_shape, fn, red_axis=None, init_arg=None, aliases=None,
         ride=None):
    n_in = len(args)
    single = not isinstance(out_shape, (list, tuple))
    out_specs = [out_specs] if single else list(out_specs)
    out_shape = [out_shape] if single else list(out_shape)

    def body(*refs):
        _gmm_step(fn, refs[:n_in], refs[n_in:], red_axis, init_arg)

    sem = tuple("arbitrary" if ax == red_axis else "parallel" for ax in range(len(grid)))
    res, rode = _pcall(body, args, grid=grid, in_specs=in_specs, out_specs=out_specs, out_shape=out_shape,
                       name=name, sem=sem, aliases=aliases, ride=ride)
    ours = res[0] if single else res
    return ours if ride is None else (ours, rode)


def _gmm_step(fn, ins, outs, red_axis, init_arg):
    parts = fn(*ins)
    if red_axis is None:
        for o_ref, p in zip(outs, parts):
            o_ref[...] = p.astype(o_ref.dtype)
        return
    k = pl.program_id(red_axis)

    @pl.when(k == 0)
    def _():
        for idx, (o_ref, p) in enumerate(zip(outs, parts)):
            o_ref[...] = p + ins[init_arg][...] if (idx == 0 and init_arg is not None) else p

    @pl.when(k > 0)
    def _():
        for o_ref, p in zip(outs, parts):
            o_ref[...] += p


def _ride_body(ride, grid, n_in, n_out, n_scratch, body):
    n_rin, n_rout = len(ride.arrays), len(ride.out_shape)
    nsteps = math.prod(grid)

    def wrapped(*refs):
        ins = refs[:n_in]
        r_ins = refs[n_in:n_in + n_rin]
        o0 = n_in + n_rin
        outs = refs[o0:o0 + n_out]
        r_outs = refs[o0 + n_out:o0 + n_out + n_rout]
        s0 = o0 + n_out + n_rout
        scratch = refs[s0:s0 + n_scratch]
        send_sems, recv_sems = refs[-2:]
        step = pl.program_id(0)
        for ax in range(1, len(grid)):
            step = step * grid[ax] + pl.program_id(ax)
        ride.emit(step, nsteps, r_ins, r_outs, send_sems, recv_sems, before=True)
        body(*ins, *outs, *scratch)
        ride.emit(step, nsteps, r_ins, r_outs, send_sems, recv_sems, before=False)

    return wrapped


def _pcall(body, args, *, grid, in_specs, out_specs, out_shape, name, sem, scratch=(), aliases=None, ride=None):
    if ride is None:
        res = pl.pallas_call(body, grid=grid, in_specs=list(in_specs), out_specs=list(out_specs),
                             out_shape=list(out_shape), scratch_shapes=list(scratch), name=name,
                             input_output_aliases=aliases or {}, compiler_params=_params(*sem))(*args)
        return res, None
    n_in, n_out = len(args), len(out_shape)
    res = pl.pallas_call(
        _ride_body(ride, grid, n_in, n_out, len(scratch), body), grid=grid,
        in_specs=list(in_specs) + ride.in_specs, out_specs=list(out_specs) + ride.out_specs,
        out_shape=list(out_shape) + ride.out_shape, scratch_shapes=list(scratch) + ride.scratch, name=name,
        input_output_aliases=aliases or {},
        compiler_params=_params(*(("arbitrary",) * len(grid))))(*args, *ride.arrays)
    return res[:n_out], res[n_out:]


def _mm_cols(a, ws, name, ride=None):
    m, k = a.shape
    n = ws.shape[2]
    tm = _fit_rows(m, k * _isz(a) + n * 4, k * n * _isz(ws), 4 * n)
    return _gmm(name, (N_SHARD, m // tm), [a, ws],
                [pl.BlockSpec((tm, k), lambda j, i: (i, 0)), pl.BlockSpec((None, k, n), lambda j, i: (j, 0, 0))],
                pl.BlockSpec((tm, n), lambda j, i: (i, j)), jax.ShapeDtypeStruct((m, N_SHARD * n), F32),
                lambda a_ref, w_ref: (_nn(a_ref[...], w_ref[...]),), ride=ride)


def _mm_cols_t_rms(d, ws, h, w, resid, name, ride=None):
    m = d.shape[0]
    _, k, n = ws.shape
    tm = _fit_rows(m, n * _isz(d) + 3 * k * 4, k * n * _isz(ws), 16 * k)
    return _gmm_rms(name, (m // tm, N_SHARD), [d, ws],
                    [pl.BlockSpec((tm, n), lambda i, j: (i, j)), pl.BlockSpec((None, k, n), lambda i, j: (j, 0, 0))],
                    pl.BlockSpec((tm, k), lambda i, j: (i, 0)),
                    lambda d_ref, w_ref: _nt(d_ref[...], w_ref[...]), h, w, resid, 0, red_axis=1, ride=ride)


def _mm_nt_rms(a, b, h, w, resid, name, ride=None):
    m, n = a.shape
    k = b.shape[0]
    tm = _fit_rows(m, n * _isz(a) + 3 * k * 4, k * n * _isz(b), 16 * k)
    return _gmm_rms(name, (m // tm,), [a, b],
                    [pl.BlockSpec((tm, n), lambda i: (i, 0)), pl.BlockSpec((k, n), lambda i: (0, 0))],
                    pl.BlockSpec((tm, k), lambda i: (i, 0)),
                    lambda a_ref, b_ref: _nt(a_ref[...], b_ref[...]), h, w, resid, 0, ride=ride)


def _mm_cols_grad(a, d, name):
    m, k = a.shape
    n = d.shape[1] // N_SHARD
    tm = _fit_rows(m, k * _isz(a) + n * _isz(d), (3 * k * n * 4) // 2, 2 * (k + n))
    return _gmm(name, (N_SHARD, m // tm), [a, d],
                [pl.BlockSpec((tm, k), lambda j, i: (i, 0)), pl.BlockSpec((tm, n), lambda j, i: (i, j))],
                pl.BlockSpec((None, k, n), lambda j, i: (j, 0, 0)), jax.ShapeDtypeStruct((N_SHARD, k, n), F32),
                lambda a_ref, d_ref: (_tn(a_ref[...], d_ref[...]),), red_axis=1)


def _ffn_up(hn, wg, wu, layer, name):
    m, k = hn.shape
    n = wg.shape[3]
    tm = _fit_rows(m, k * _isz(hn) + 3 * n * jnp.dtype(BF16).itemsize, 2 * k * n * _isz(wg), 16 * n)

    def fn(a_ref, wg_ref, wu_ref):
        a = a_ref[...]
        g = _nn(a, wg_ref[...])
        u = _nn(a, wu_ref[...])
        return g, u, g * jax.nn.sigmoid(g) * u

    w_spec = pl.BlockSpec((None, None, k, n), lambda j, i: (j, layer, 0, 0))
    o_spec = pl.BlockSpec((None, tm, n), lambda j, i: (j, i, 0))
    out = jax.ShapeDtypeStruct((N_SHARD, m, n), BF16)
    return _gmm(name, (N_SHARD, m // tm), [hn, wg, wu],
                [pl.BlockSpec((tm, k), lambda j, i: (i, 0)), w_spec, w_spec],
                [o_spec, o_spec, o_spec], [out, out, out], fn)


def _ffn_down(act, wd, resid, layer, name):
    _, m, n = act.shape
    d = wd.shape[3]
    tm = _fit_rows(m, N_SHARD * n * _isz(act) + 2 * d * 4, N_SHARD * n * d * _isz(wd), 8 * d)

    def fn(a_ref, w_ref, r_ref):
        acc = r_ref[...]
        for j in range(N_SHARD):
            acc = acc + _nn(a_ref[j], w_ref[j])
        return (acc,)

    row = pl.BlockSpec((tm, d), lambda i: (i, 0))
    return _gmm(name, (m // tm,), [act, wd, resid],
                [pl.BlockSpec((N_SHARD, tm, n), lambda i: (0, i, 0)),
                 pl.BlockSpec((N_SHARD, None, n, d), lambda i: (0, layer, 0, 0)), row],
                row, jax.ShapeDtypeStruct((m, d), F32), fn)


def _ffn_down_bwd(dh, wd, g, u, layer, name, ride=None):
    m, d = dh.shape
    n = wd.shape[2]
    tm = _fit_rows(m, d * _isz(dh) + 4 * n * jnp.dtype(BF16).itemsize, n * d * _isz(wd), 2 * d + 24 * n)

    def fn(dh_ref, wd_ref, g_ref, u_ref):
        dact = _nt(dh_ref[...], wd_ref[...])
        gv = g_ref[...].astype(F32)
        uv = u_ref[...].astype(F32)
        sg = jax.nn.sigmoid(gv)
        return dact * uv * (sg * (1.0 + gv * (1.0 - sg))), dact * gv * sg

    o_spec = pl.BlockSpec((None, tm, n), lambda j, i: (j, i, 0))
    out = jax.ShapeDtypeStruct((N_SHARD, m, n), BF16)
    return _gmm(name, (N_SHARD, m // tm), [dh, wd, g, u],
                [pl.BlockSpec((tm, d), lambda j, i: (i, 0)),
                 pl.BlockSpec((None, None, n, d), lambda j, i: (j, layer, 0, 0)), o_spec, o_spec],
                [o_spec, o_spec], [out, out], fn, ride=ride)


def _ffn_up_bwd(dg, du, wg, wu, layer, h, w, resid, name, ride=None):
    _, m, n = dg.shape
    k = wg.shape[2]
    tm = _fit_rows(m, 2 * N_SHARD * n * _isz(dg) + 3 * k * 4, 2 * N_SHARD * k * n * _isz(wg), 16 * k)

    def fn(dg_ref, du_ref, wg_ref, wu_ref):
        acc = _nt(dg_ref[0], wg_ref[0]) + _nt(du_ref[0], wu_ref[0])
        for j in range(1, N_SHARD):
            acc = acc + _nt(dg_ref[j], wg_ref[j]) + _nt(du_ref[j], wu_ref[j])
        return acc

    d_spec = pl.BlockSpec((N_SHARD, tm, n), lambda i: (0, i, 0))
    w_spec = pl.BlockSpec((N_SHARD, None, k, n), lambda i: (0, layer, 0, 0))
    return _gmm_rms(name, (m // tm,), [dg, du, wg, wu], [d_spec, d_spec, w_spec, w_spec],
                    pl.BlockSpec((tm, k), lambda i: (i, 0)), fn, h, w, resid, 0, ride=ride)


def _ffn_wgrad(lhs, rhs_list, layer, layers, prev, lhs_sharded, name):
    if lhs_sharded:
        _, m, k = lhs.shape
        n = rhs_list[0].shape[1]
    else:
        m, k = lhs.shape
        n = rhs_list[0].shape[2]
    n_out = len(rhs_list)
    tm = _fit_rows(m, k * _isz(lhs) + n_out * n * _isz(rhs_list[0]), (3 * n_out * k * n * 4) // 2,
                   2 * (k + n_out * n))
    sh = pl.BlockSpec((None, tm, k if lhs_sharded else n), lambda j, i: (j, i, 0))
    fl = pl.BlockSpec((tm, n if lhs_sharded else k), lambda j, i: (i, 0))
    n_out = len(rhs_list)
    args = [lhs] + list(rhs_list)
    in_specs = [sh if lhs_sharded else fl] + [fl if lhs_sharded else sh] * n_out
    aliases = None
    if prev is not None:
        aliases = {len(args) + t: t for t in range(n_out)}
        args = args + list(prev)
        in_specs = in_specs + [ANY] * n_out

    def fn(l_ref, *rest):
        lv = l_ref[...]
        return tuple(_tn(lv, r_ref[...]) for r_ref in rest[:n_out])

    o_spec = pl.BlockSpec((None, None, k, n), lambda j, i: (j, layer, 0, 0))
    out = jax.ShapeDtypeStruct((N_SHARD, layers, k, n), F32)
    return _gmm(name, (N_SHARD, m // tm), args, in_specs, [o_spec] * n_out, [out] * n_out, fn,
                red_axis=1, aliases=aliases)


def _ret_consts():
    log_gamma = jnp.log1p(-jnp.exp2(-5.0 - jnp.arange(RET_HEADS, dtype=F32)))
    idx = jnp.arange(CHUNK, dtype=F32)
    rel = idx[:, None] - idx[None, :]
    dmask = jnp.where((rel >= 0)[None], jnp.exp(log_gamma[:, None, None] * jnp.maximum(rel, 0.0)), 0.0)
    xi = jnp.exp(log_gamma[:, None] * (idx[None, :] + 1.0))[:, :, None]
    zeta = jnp.exp(log_gamma[:, None] * (CHUNK - 1.0 - idx[None, :]))[:, :, None]
    gamma_c = jnp.exp(log_gamma * CHUNK)
    wide = (RET_HEADS, CHUNK, RET_DK)
    return dmask, jnp.broadcast_to(xi, wide), jnp.broadcast_to(zeta, wide), gamma_c


def _rope_tables(rows):
    half = RET_DK // 2
    inv_freq = ROPE_BASE ** (-jnp.arange(half, dtype=F32) / half)
    pos = (jnp.arange(rows) - PAD).astype(F32)
    ang = pos[:, None] * inv_freq[None, :]
    return jnp.cos(ang), jnp.sin(ang)


def _ret_specs(order):
    return [pl.BlockSpec((CHUNK, RET_QK), lambda n: (order(n), 0)),
            pl.BlockSpec((CHUNK, RET_QK), lambda n: (order(n), 1)),
            pl.BlockSpec((CHUNK, RET_V), lambda n: (order(n), 1)),
            pl.BlockSpec((CHUNK, RET_V), lambda n: (order(n), 2))]


def _ret_const_specs():
    return [pl.BlockSpec((RET_HEADS, CHUNK, CHUNK), lambda n: (0, 0, 0)),
            pl.BlockSpec((RET_HEADS, CHUNK, RET_DK), lambda n: (0, 0, 0)),
            pl.BlockSpec((RET_HEADS, CHUNK, RET_DK), lambda n: (0, 0, 0)),
            pl.BlockSpec((1, RET_DV), lambda n: (0, 0))]


def _ret_fwd(proj, cos, sin, consts, gn_w, seq, ride=None):
    rows = proj.shape[0]
    nc = rows // CHUNK
    dmask, xi, zeta, gamma_c = consts

    def body(gam_ref, q_ref, k_ref, v_ref, g_ref, cos_ref, sin_ref, dm_ref, xi_ref, ze_ref, gn_ref,
             o_ref, y_ref, ss_ref, s_ref):
        n = pl.program_id(0)

        @pl.when(n == 0)
        def _():
            s_ref[...] = jnp.zeros_like(s_ref)

        cs, sn = cos_ref[...], sin_ref[...]
        kscale = _valid_rows(n * CHUNK, CHUNK, seq) * (RET_DK ** -0.5)
        gn = gn_ref[...]
        hs = range(RET_HEADS)
        qk_cols = [slice(h * RET_DK, (h + 1) * RET_DK) for h in hs]
        v_cols = [slice(h * RET_DV, (h + 1) * RET_DV) for h in hs]
        qr_l = [_rope(q_ref[:, c], cs, sn) for c in qk_cols]
        kr_l = [_rope(k_ref[:, c], cs, sn) * kscale for c in qk_cols]
        v_l = [v_ref[:, c] for c in v_cols]
        s_l = [s_ref[h] for h in hs]
        sc_l = [_nt(qr, kr) * dm_ref[h] for h, (qr, kr) in enumerate(zip(qr_l, kr_l))]
        o_l = [_nn(sc_l[h], v_l[h]) + _nn(qr_l[h] * xi_ref[h], s_l[h]) for h in hs]
        for h in hs:
            ss_ref[0, h] = s_l[h].astype(ss_ref.dtype)
            s_ref[h] = gam_ref[h] * s_l[h] + _tn(kr_l[h] * ze_ref[h], v_l[h])
            o_ref[:, v_cols[h]] = o_l[h]
            y_ref[:, v_cols[h]] = _gated_norm(o_l[h], g_ref[:, v_cols[h]], gn).astype(y_ref.dtype)

    fwd = lambda n: n
    row128 = pl.BlockSpec((CHUNK, RET_DK // 2), lambda n: (n, 0))
    row_v = pl.BlockSpec((CHUNK, RET_V), lambda n: (n, 0))
    res, rode = _pcall(
        body, [gamma_c, proj, proj, proj, proj, cos, sin, dmask, xi, zeta, gn_w.reshape(1, RET_DV)],
        grid=(nc,),
        in_specs=[pl.BlockSpec(memory_space=pltpu.SMEM)] + _ret_specs(fwd) + [row128, row128]
        + _ret_const_specs(),
        out_specs=[row_v, row_v,
                   pl.BlockSpec((1, RET_HEADS, RET_DK, RET_DV), lambda n: (n, 0, 0, 0))],
        out_shape=[jax.ShapeDtypeStruct((rows, RET_V), F32), jax.ShapeDtypeStruct((rows, RET_V), BF16),
                   jax.ShapeDtypeStruct((nc, RET_HEADS, RET_DK, RET_DV), BF16)],
        scratch=[pltpu.VMEM((RET_HEADS, RET_DK, RET_DV), F32)], name="ret_fwd", sem=("arbitrary",), ride=ride)
    return res if ride is None else (res, rode)


def _ret_bwd(proj, o, dy, states, cos, sin, consts, gn_w, seq, ride=None):
    rows = proj.shape[0]
    nc = rows // CHUNK
    dmask, xi, zeta, gamma_c = consts

    def body(gam_ref, q_ref, k_ref, v_ref, g_ref, o_ref, dy_ref, ss_ref, cos_ref, sin_ref,
             dm_ref, xi_ref, ze_ref, gn_ref, dp_ref, dgn_ref, ds_ref):
        n = pl.program_id(0)

        @pl.when(n == 0)
        def _():
            ds_ref[...] = jnp.zeros_like(ds_ref)
            dgn_ref[...] = jnp.zeros_like(dgn_ref)

        cs, sn = cos_ref[...], sin_ref[...]
        kscale = _valid_rows((nc - 1 - n) * CHUNK, CHUNK, seq) * (RET_DK ** -0.5)
        gn = gn_ref[...]
        dgn = jnp.zeros((1, RET_DV), F32)
        hs = range(RET_HEADS)
        qk_cols = [slice(h * RET_DK, (h + 1) * RET_DK) for h in hs]
        v_cols = [slice(h * RET_DV, (h + 1) * RET_DV) for h in hs]
        qr_l = [_rope(q_ref[:, c], cs, sn) for c in qk_cols]
        kr_l = [_rope(k_ref[:, c], cs, sn) * kscale for c in qk_cols]
        v_l = [v_ref[:, c] for c in v_cols]
        s_l = [ss_ref[0, h] for h in hs]
        ds_l = [ds_ref[h] for h in hs]
        gnb = [_gated_norm_bwd(dy_ref[:, c], o_ref[:, c], g_ref[:, c], gn) for c in v_cols]
        do_l = [x[0] for x in gnb]
        sc_l = [_nt(qr_l[h], kr_l[h]) * dm_ref[h] for h in hs]
        dsc_l = [_nt(do_l[h], v_l[h]) * dm_ref[h] for h in hs]
        dv_l = [_tn(sc_l[h], do_l[h]) + _nn(kr_l[h] * ze_ref[h], ds_l[h]) for h in hs]
        dqr_l = [_nn(dsc_l[h], kr_l[h]) + _nt(do_l[h], s_l[h]) * xi_ref[h] for h in hs]
        dkr_l = [_tn(dsc_l[h], qr_l[h]) + _nt(v_l[h], ds_l[h]) * ze_ref[h] for h in hs]
        for h in hs:
            dgn = dgn + gnb[h][2]
            ds_ref[h] = gam_ref[h] * ds_l[h] + _tn(qr_l[h] * xi_ref[h], do_l[h])
            dp_ref[:, qk_cols[h]] = _rope_bwd(dqr_l[h], cs, sn).astype(dp_ref.dtype)
            dp_ref[:, RET_QK + h * RET_DK:RET_QK + (h + 1) * RET_DK] = (
                _rope_bwd(dkr_l[h] * kscale, cs, sn).astype(dp_ref.dtype))
            dp_ref[:, 2 * RET_QK + h * RET_DV:2 * RET_QK + (h + 1) * RET_DV] = dv_l[h].astype(dp_ref.dtype)
            dp_ref[:, 2 * RET_QK + RET_V + h * RET_DV:2 * RET_QK + RET_V + (h + 1) * RET_DV] = (
                gnb[h][1].astype(dp_ref.dtype))
        dgn_ref[...] += dgn

    rev = lambda n: nc - 1 - n
    row128 = pl.BlockSpec((CHUNK, RET_DK // 2), lambda n: (rev(n), 0))
    row_v = pl.BlockSpec((CHUNK, RET_V), lambda n: (rev(n), 0))
    res, rode = _pcall(
        body, [gamma_c, proj, proj, proj, proj, o, dy, states, cos, sin, dmask, xi, zeta,
               gn_w.reshape(1, RET_DV)],
        grid=(nc,),
        in_specs=[pl.BlockSpec(memory_space=pltpu.SMEM)] + _ret_specs(rev) + [
            row_v, row_v, pl.BlockSpec((1, RET_HEADS, RET_DK, RET_DV), lambda n: (rev(n), 0, 0, 0)),
            row128, row128] + _ret_const_specs(),
        out_specs=[pl.BlockSpec((CHUNK, RET_IN), lambda n: (rev(n), 0)),
                   pl.BlockSpec((1, RET_DV), lambda n: (0, 0))],
        out_shape=[jax.ShapeDtypeStruct((rows, RET_IN), BF16), jax.ShapeDtypeStruct((1, RET_DV), F32)],
        scratch=[pltpu.VMEM((RET_HEADS, RET_DK, RET_DV), F32)], name="ret_bwd", sem=("arbitrary",), ride=ride)
    return res if ride is None else (res, rode)


GATE_COL = DN_CONV_CH // DN_V
BA_COL = (DN_CONV_CH + DN_V) // LANES
BETA_LANE, DECAY_LANE = 0, DN_HEADS
INV_SHIFT = 4
INV_SQUARINGS = INV_SHIFT - 1
assert CHUNK == 4 << INV_SHIFT


def _dn_in_specs(order, conv_saved=False):
    return [pl.BlockSpec((CHUNK, DN_CONV_CH), lambda n: (order(n), 0)),
            pl.BlockSpec((CHUNK, DN_CONV_CH), lambda n: (order(n), 0)) if conv_saved else
            pl.BlockSpec((8, DN_CONV_CH), lambda n: (jnp.maximum(order(n) * (CHUNK // 8) - 1, 0), 0)),
            pl.BlockSpec((CHUNK, DN_V), lambda n: (order(n), GATE_COL)),
            pl.BlockSpec((CHUNK, LANES), lambda n: (order(n), BA_COL)),
            pl.BlockSpec((CONV_K, 1, DN_CONV_CH), lambda n: (0, 0, 0)),
            pl.BlockSpec((1, LANES), lambda n: (0, 0)),
            pl.BlockSpec((1, LANES), lambda n: (0, 0)),
            pl.BlockSpec((1, DN_DV), lambda n: (0, 0))]


def _dn_front(c, seq, x_ref, halo_ref, ba_ref, cw_ref, al_ref, dt_ref, yc_ref=None):
    valid = _valid_rows(c * CHUNK, CHUNK, seq)
    xin = x_ref[...] * valid
    if yc_ref is None:
        halo = halo_ref[...] * _valid_rows(c * CHUNK - 8, 8, seq)
        yc = xin * cw_ref[CONV_K - 1]
        for k in range(1, CONV_K):
            yc = yc + _shift_down(xin, halo, k) * cw_ref[CONV_K - 1 - k]
    else:
        yc = yc_ref[...]
    sgc = jax.nn.sigmoid(yc)
    ba = ba_ref[...]
    sig = jax.nn.sigmoid(ba)
    beta = sig * valid
    z = ba + dt_ref[...]
    eal = jnp.exp(al_ref[...])
    g = -eal * _softplus(z) * valid
    ri, ci = _iota((CHUNK, CHUNK), 0), _iota((CHUNK, CHUNK), 1)
    lower = (ri >= ci).astype(F32)
    upper = (ri <= ci).astype(F32)
    eye = (ri == ci).astype(F32)
    gam = _nn(lower, g, hi=True)
    gam_t = _tn(g, upper, hi=True)
    return dict(valid=valid, xin=xin, yc=yc, sgc=sgc, act=yc * sgc, sig=sig, beta=beta, z=z,
                eal=eal, g=g, gam=gam, gam_t=gam_t, ri=ri, ci=ci, upper=upper, eye=eye)


def _dn_head(f, h):
    act = f["act"]
    q_raw = act[:, h * DN_DK:(h + 1) * DN_DK]
    k_raw = act[:, DN_QK + h * DN_DK:DN_QK + (h + 1) * DN_DK]
    v = act[:, 2 * DN_QK + h * DN_DV:2 * DN_QK + (h + 1) * DN_DV]
    rq = lax.rsqrt(jnp.sum(q_raw * q_raw, axis=-1, keepdims=True) + RMS_EPS)
    rk = lax.rsqrt(jnp.sum(k_raw * k_raw, axis=-1, keepdims=True) + RMS_EPS)
    qh = q_raw * rq
    kn = k_raw * rk
    gam_c = _col(f["gam"], DECAY_LANE + h)
    gam_r = _row(f["gam_t"], DECAY_LANE + h)
    bc = _col(f["beta"], BETA_LANE + h)
    diff = gam_c - gam_r
    decay = jnp.where(f["ri"] >= f["ci"], jnp.exp(jnp.minimum(diff, 0.0)), 0.0)
    glast = jnp.sum(gam_r * (_iota((1, CHUNK), 1) == CHUNK - 1).astype(F32), axis=1, keepdims=True)
    return dict(rq=rq, rk=rk, qh=qh, qn=qh * (DN_DK ** -0.5), kn=kn, v=v, gam_c=gam_c, gam_r=gam_r,
                bc=bc, diff=diff, decay=decay, egam=jnp.exp(gam_c), glast=glast,
                eglast=jnp.exp(glast), ekd=jnp.exp(glast - gam_c))


def _dn_fwd(proj, conv_w, alog, dtb, norm_w, seq):
    rows = proj.shape[0]
    nc = rows // CHUNK

    def body(x_ref, halo_ref, gate_ref, ba_ref, cw_ref, al_ref, dt_ref, nw_ref,
             o_ref, y_ref, ss_ref, t_ref, yc_ref, s_ref):
        n = pl.program_id(0)

        @pl.when(n == 0)
        def _():
            s_ref[...] = jnp.zeros_like(s_ref)

        f = _dn_front(n, seq, x_ref, halo_ref, ba_ref, cw_ref, al_ref, dt_ref)
        yc_ref[...] = f["yc"]
        ri, ci = f["ri"], f["ci"]
        eye = f["eye"]
        diag_m = (jnp.right_shift(ri, INV_SHIFT) == jnp.right_shift(ci, INV_SHIFT)).astype(F32)
        half_m = (jnp.right_shift(ri, INV_SHIFT + 1) == jnp.right_shift(ci, INV_SHIFT + 1)).astype(F32)
        nw = nw_ref[...]
        heads = [_dn_head(f, h) for h in range(DN_HEADS)]
        a_all = [jnp.where(ri > ci, hd["bc"] * _nt(hd["kn"], hd["kn"]) * hd["decay"], 0.0) for hd in heads]
        b_all = [a * diag_m for a in a_all]
        t_all = [eye - b for b in b_all]
        for _ in range(INV_SQUARINGS):
            b_all = [_nn(b, b, hi=True) for b in b_all]
            t_all = [t + _nn(t, b, hi=True) for t, b in zip(t_all, b_all)]
        for off_m in (half_m - diag_m, 1.0 - half_m):
            x_all = [_nn(a * off_m, t, hi=True) for a, t in zip(a_all, t_all)]
            t_all = [t - _nn(t, x, hi=True) for t, x in zip(t_all, x_all)]
        u_all = [_nn(t, hd["v"] * hd["bc"], hi=True) for t, hd in zip(t_all, heads)]
        w_all = [_nn(t, hd["kn"] * (hd["bc"] * hd["egam"]), hi=True) for t, hd in zip(t_all, heads)]
        for h in range(DN_HEADS):
            hd = heads[h]
            v_cols = slice(h * DN_DV, (h + 1) * DN_DV)
            t_ref[0, h] = t_all[h]
            s = s_ref[h]
            ss_ref[0, h] = s
            u, w = u_all[h], w_all[h]
            v_new = u - _nn(w, s)
            qk = _nt(hd["qn"], hd["kn"]) * hd["decay"]
            o = _nn(hd["qn"] * hd["egam"], s) + _nn(qk, v_new)
            s_ref[h] = s * hd["eglast"] + _tn(hd["kn"] * hd["ekd"], v_new)
            o_ref[:, v_cols] = o
            y_ref[:, v_cols] = _gated_norm(o, gate_ref[:, v_cols], nw).astype(y_ref.dtype)

    fwd = lambda n: n
    row_v = pl.BlockSpec((CHUNK, DN_V), lambda n: (n, 0))
    return pl.pallas_call(
        body, grid=(nc,), in_specs=_dn_in_specs(fwd),
        out_specs=[row_v, row_v,
                   pl.BlockSpec((1, DN_HEADS, DN_DK, DN_DV), lambda n: (n, 0, 0, 0)),
                   pl.BlockSpec((1, DN_HEADS, CHUNK, CHUNK), lambda n: (n, 0, 0, 0)),
                   pl.BlockSpec((CHUNK, DN_CONV_CH), lambda n: (n, 0))],
        out_shape=[jax.ShapeDtypeStruct((rows, DN_V), F32), jax.ShapeDtypeStruct((rows, DN_V), BF16),
                   jax.ShapeDtypeStruct((nc, DN_HEADS, DN_DK, DN_DV), F32),
                   jax.ShapeDtypeStruct((nc, DN_HEADS, CHUNK, CHUNK), F32),
                   jax.ShapeDtypeStruct((rows, DN_CONV_CH), F32)],
        scratch_shapes=[pltpu.VMEM((DN_HEADS, DN_DK, DN_DV), F32)],
        name="dn_fwd", compiler_params=_params("arbitrary"))(
            proj, proj, proj, proj, conv_w, alog, dtb, norm_w.reshape(1, DN_DV))


def _dn_bwd(proj, conv_out, o, dy, states, tinv, conv_w, alog, dtb, norm_w, seq):
    rows = proj.shape[0]
    nc = rows // CHUNK

    def body(x_ref, yc_ref, gate_ref, ba_ref, cw_ref, al_ref, dt_ref, nw_ref,
             o_ref, dy_ref, ss_ref, t_ref,
             dp_ref, dcw_ref, dal_ref, ddt_ref, dnw_ref, ds_ref, nxt_ref):
        n = pl.program_id(0)

        @pl.when(n == 0)
        def _():
            ds_ref[...] = jnp.zeros_like(ds_ref)
            nxt_ref[...] = jnp.zeros_like(nxt_ref)
            dcw_ref[...] = jnp.zeros_like(dcw_ref)
            dal_ref[...] = jnp.zeros_like(dal_ref)
            ddt_ref[...] = jnp.zeros_like(ddt_ref)
            dnw_ref[...] = jnp.zeros_like(dnw_ref)

        f = _dn_front(nc - 1 - n, seq, x_ref, None, ba_ref, cw_ref, al_ref, dt_ref, yc_ref)
        ri, ci = f["ri"], f["ci"]
        strict = (ri > ci).astype(F32)
        nw = nw_ref[...]
        lane128 = _iota((1, LANES), 1)
        row128 = _iota((LANES, 1), 0)
        dgam_col = jnp.zeros((CHUNK, LANES), F32)
        dgam_row = jnp.zeros((LANES, CHUNK), F32)
        dbeta = jnp.zeros((CHUNK, LANES), F32)
        dnw = jnp.zeros((1, DN_DV), F32)
        hs = range(DN_HEADS)
        heads = [_dn_head(f, h) for h in hs]
        cols = [slice(h * DN_DV, (h + 1) * DN_DV) for h in hs]
        t_l = [t_ref[0, h] for h in hs]
        s_l = [ss_ref[0, h] for h in hs]
        ds_l = [ds_ref[h] for h in hs]
        kk_l = [_nt(hd["kn"], hd["kn"]) for hd in heads]
        p_l = [_nt(hd["qn"], hd["kn"]) for hd in heads]
        rhsw_l = [hd["kn"] * (hd["bc"] * hd["egam"]) for hd in heads]
        u_l = [_nn(t, hd["v"] * hd["bc"], hi=True) for t, hd in zip(t_l, heads)]
        w_l = [_nn(t, r, hi=True) for t, r in zip(t_l, rhsw_l)]
        vnew_l = [u - _nn(w, s) for u, w, s in zip(u_l, w_l, s_l)]
        gnb = [_gated_norm_bwd(dy_ref[:, c], o_ref[:, c], gate_ref[:, c], nw) for c in cols]
        do_l = [x[0] for x in gnb]
        for h in hs:
            dp_ref[:, DN_CONV_CH + h * DN_DV:DN_CONV_CH + (h + 1) * DN_DV] = gnb[h][1].astype(dp_ref.dtype)
            dnw = dnw + gnb[h][2]
        qg_l = [hd["qn"] * hd["egam"] for hd in heads]
        kd_l = [hd["kn"] * hd["ekd"] for hd in heads]
        dvnew_l = [_tn(p * hd["decay"], do) + _nn(kd, ds)
                   for p, hd, do, kd, ds in zip(p_l, heads, do_l, kd_l, ds_l)]
        m_l = [_nt(do, vn) for do, vn in zip(do_l, vnew_l)]
        dqg_l = [_nt(do, s) for do, s in zip(do_l, s_l)]
        dkd_l = [_nt(vn, ds) for vn, ds in zip(vnew_l, ds_l)]
        for h in hs:
            ds_ref[h] = (ds_l[h] * heads[h]["eglast"] + _tn(qg_l[h], do_l[h]) - _tn(w_l[h], dvnew_l[h]))
        dw_l = [-_nt(dvn, s) for dvn, s in zip(dvnew_l, s_l)]
        dru_l = [_tn(t, dvn, hi=True) for t, dvn in zip(t_l, dvnew_l)]
        drw_l = [_tn(t, dw_, hi=True) for t, dw_ in zip(t_l, dw_l)]
        da_l = [-(_nt(dru, u) + _nt(drw, w)) * strict for dru, u, drw, w in zip(dru_l, u_l, drw_l, w_l)]
        dp_l = [m * hd["decay"] for m, hd in zip(m_l, heads)]
        dkk_l = [da * (hd["bc"] * hd["decay"]) for da, hd in zip(da_l, heads)]
        dqn_l = [dqg * hd["egam"] + _nn(dp, hd["kn"]) for dqg, hd, dp in zip(dqg_l, heads, dp_l)]
        dkn_l = [_tn(dp, hd["qn"]) + dkd * hd["ekd"] + drw * (hd["bc"] * hd["egam"])
                 + _nn(dkk, hd["kn"]) + _tn(dkk, hd["kn"])
                 for dp, hd, dkd, drw, dkk in zip(dp_l, heads, dkd_l, drw_l, dkk_l)]
        dq_parts, dk_parts, dv_parts = [], [], []
        for h in hs:
            hd = heads[h]
            kn, v, bc, egam, decay = hd["kn"], hd["v"], hd["bc"], hd["egam"], hd["decay"]
            t1 = jnp.sum(dkd_l[h] * kd_l[h], axis=1, keepdims=True)
            dglast = (jnp.sum(t1, axis=0, keepdims=True)
                      + jnp.sum(jnp.sum(ds_l[h] * s_l[h], axis=1, keepdims=True), axis=0, keepdims=True)
                      * hd["eglast"])
            e = (m_l[h] * p_l[h] + da_l[h] * (bc * kk_l[h])) * decay
            dgc = (jnp.sum(dqg_l[h] * qg_l[h], axis=1, keepdims=True) - t1
                   + jnp.sum(drw_l[h] * rhsw_l[h], axis=1, keepdims=True)
                   + jnp.sum(e, axis=1, keepdims=True)
                   + jnp.where(_iota((CHUNK, 1), 0) == CHUNK - 1, dglast, 0.0))
            dgr = -jnp.sum(e, axis=0, keepdims=True)
            dbc = (jnp.sum(dru_l[h] * v, axis=1, keepdims=True)
                   + jnp.sum(drw_l[h] * kn, axis=1, keepdims=True) * egam
                   + jnp.sum(da_l[h] * kk_l[h] * decay, axis=1, keepdims=True))
            dv_parts.append(dru_l[h] * bc)
            qh, dqn, dkn = hd["qh"], dqn_l[h], dkn_l[h]
            dq_parts.append(((DN_DK ** -0.5) * hd["rq"])
                            * (dqn - qh * jnp.sum(dqn * qh, axis=1, keepdims=True)))
            dk_parts.append(hd["rk"] * (dkn - kn * jnp.sum(dkn * kn, axis=1, keepdims=True)))
            dgam_col = dgam_col + dgc * (lane128 == DECAY_LANE + h).astype(F32)
            dbeta = dbeta + dbc * (lane128 == BETA_LANE + h).astype(F32)
            dgam_row = dgam_row + (row128 == DECAY_LANE + h).astype(F32) * dgr
        dnw_ref[...] += dnw
        dgam = dgam_col + _nt(f["eye"], dgam_row, hi=True)
        dg = _nn(f["upper"], dgam, hi=True)
        d_a = dg * (-f["eal"]) * jax.nn.sigmoid(f["z"]) * f["valid"]
        dal_ref[...] += jnp.sum(dg * f["g"], axis=0, keepdims=True)
        ddt_ref[...] += jnp.sum(d_a, axis=0, keepdims=True)
        d_b = dbeta * f["valid"] * f["sig"] * (1.0 - f["sig"])
        dp_ref[:, DN_CONV_CH + DN_V:DN_CONV_CH + DN_V + LANES] = (d_a + d_b).astype(dp_ref.dtype)
        dp_ref[:, DN_CONV_CH + DN_V + LANES:] = jnp.zeros((CHUNK, DN_IN_PAD - DN_IN_USED), dp_ref.dtype)
        dact = jnp.concatenate(dq_parts + dk_parts + dv_parts, axis=1)
        yc, sgc = f["yc"], f["sgc"]
        dyc = dact * (sgc * (1.0 + yc * (1.0 - sgc)))
        nxt = nxt_ref[...]
        ups = [dyc] + [_shift_up(dyc, nxt, j) for j in range(1, CONV_K)]
        dx = ups[0] * cw_ref[CONV_K - 1]
        for j in range(1, CONV_K):
            dx = dx + ups[j] * cw_ref[CONV_K - 1 - j]
        for j in range(CONV_K):
            dcw_ref[CONV_K - 1 - j] += jnp.sum(f["xin"] * ups[j], axis=0, keepdims=True)
        nxt_ref[...] = dyc[0:8]
        dp_ref[:, :DN_CONV_CH] = (dx * f["valid"]).astype(dp_ref.dtype)

    rev = lambda n: nc - 1 - n
    row_v = pl.BlockSpec((CHUNK, DN_V), lambda n: (rev(n), 0))
    vec = pl.BlockSpec((1, LANES), lambda n: (0, 0))
    return pl.pallas_call(
        body, grid=(nc,),
        in_specs=_dn_in_specs(rev, conv_saved=True) + [
            row_v, row_v,
            pl.BlockSpec((1, DN_HEADS, DN_DK, DN_DV), lambda n: (rev(n), 0, 0, 0)),
            pl.BlockSpec((1, DN_HEADS, CHUNK, CHUNK), lambda n: (rev(n), 0, 0, 0))],
        out_specs=[pl.BlockSpec((CHUNK, DN_IN_PAD), lambda n: (rev(n), 0)),
                   pl.BlockSpec((CONV_K, 1, DN_CONV_CH), lambda n: (0, 0, 0)), vec, vec,
                   pl.BlockSpec((1, DN_DV), lambda n: (0, 0))],
        out_shape=[jax.ShapeDtypeStruct((rows, DN_IN_PAD), BF16),
                   jax.ShapeDtypeStruct((CONV_K, 1, DN_CONV_CH), F32),
                   jax.ShapeDtypeStruct((1, LANES), F32), jax.ShapeDtypeStruct((1, LANES), F32),
                   jax.ShapeDtypeStruct((1, DN_DV), F32)],
        scratch_shapes=[pltpu.VMEM((DN_HEADS, DN_DK, DN_DV), F32), pltpu.VMEM((8, DN_CONV_CH), F32)],
        name="dn_bwd", compiler_params=_params("arbitrary"))(
            proj, conv_out, proj, proj, conv_w, alog, dtb, norm_w.reshape(1, DN_DV), o, dy, states, tinv)


def _train_step(x, tgt, wts, sh, idx):
    seq = x.shape[0]
    rows = -(-(seq + CHUNK) // ROW_ALIGN) * ROW_ALIGN
    tail = rows - seq - CHUNK
    h0 = jnp.concatenate([jnp.zeros((PAD, D_MODEL), F32), wts["meta_tokens"].astype(F32), x,
                          jnp.zeros((tail, D_MODEL), F32)], axis=0)
    tgt_p = jnp.concatenate([jnp.zeros((CHUNK, D_MODEL), F32), tgt, jnp.zeros((tail, D_MODEL), F32)],
                            axis=0)
    cos, sin = _rope_tables(rows)
    consts = _ret_consts()
    conv_w = wts["dn_conv_w"].reshape(CONV_K, 1, DN_CONV_CH)
    lane_pad = LANES - 2 * DN_HEADS
    alog = jnp.pad(wts["dn_a_log"].reshape(1, DN_HEADS), ((0, 0), (DECAY_LANE, lane_pad)))
    dtb = jnp.pad(wts["dn_dt_bias"].reshape(1, DN_HEADS), ((0, 0), (DECAY_LANE, lane_pad)))
    g = {}

    wts = dict(wts)
    (got,) = _gather_weights([sh["ret_w_in"]])
    wts["ret_w_in"] = got.reshape(N_SHARD, D_MODEL, -1)
    hn0 = _rms_fwd(h0, wts["mix_norm_w"][0], "rms_mix0")
    proj0, got = _mm_cols(hn0, wts["ret_w_in"], "ret_in",
                          ride=_Ride("gather", [sh["ret_w_out"], sh["ffn_w_gate"], sh["dn_w_out"]]))
    wts["ret_w_out"] = got[0].reshape(-1, D_MODEL)
    wts["ffn_w_gate"] = got[1]
    wts["dn_w_out"] = got[2].reshape(-1, D_MODEL)
    (o0, y0, st0), got = _ret_fwd(proj0, cos, sin, consts, wts["ret_gn_w"], seq,
                                  ride=_Ride("gather", [sh["ffn_w_up"], sh["ffn_w_down"], sh["dn_w_in"]]))
    wts["ffn_w_up"], wts["ffn_w_down"] = got[0], got[1]
    n_dn = sh["dn_w_in"].shape[-1]
    wts["dn_w_in"] = jnp.pad(_join_cols(got[2].reshape(N_SHARD, D_MODEL, n_dn)),
                             ((0, 0), (0, DN_IN_PAD - N_SHARD * n_dn)))
    h1 = _mm(y0, wts["ret_w_out"], mode="nn", name="ret_out", resid=h0)
    hn1 = _rms_fwd(h1, wts["ffn_norm_w"][0], "rms_ffn0")
    g0, u0, act0 = _ffn_up(hn1, wts["ffn_w_gate"], wts["ffn_w_up"], 0, "ffn_up0")
    h2 = _ffn_down(act0, wts["ffn_w_down"], h1, 0, "ffn_down0")
    hn2 = _rms_fwd(h2, wts["mix_norm_w"][1], "rms_mix1")
    proj1 = _mm(hn2, wts["dn_w_in"], mode="nn", name="dn_in")
    o1, y1, st1, tinv, conv1 = _dn_fwd(proj1, conv_w, alog, dtb, wts["dn_norm_w"], seq)
    h3 = _mm(y1, wts["dn_w_out"], mode="nn", name="dn_out", resid=h2)
    hn3 = _rms_fwd(h3, wts["ffn_norm_w"][1], "rms_ffn1")
    g1, u1, act1 = _ffn_up(hn3, wts["ffn_w_gate"], wts["ffn_w_up"], 1, "ffn_up1")
    h4 = _ffn_down(act1, wts["ffn_w_down"], h3, 1, "ffn_down1")

    dh4, g["final_norm_w"], loss = _final_loss(h4, wts["final_norm_w"], tgt_p, seq, "final_loss")

    layers = wts["ffn_w_gate"].shape[1]

    ffn_names = ["ffn_w_down", "ffn_w_gate", "ffn_w_up"]

    def ffn_bwd(dh_out, h_mid, hn, gg, uu, act, layer, prev, ride=None, last=False):
        tag = str(layer)
        res = _ffn_down_bwd(dh_out, wts["ffn_w_down"], gg, uu, layer, "ffn_down_bwd" + tag, ride=ride)
        (dg, du), rode = res if ride is not None else (res, None)
        d_down = _ffn_wgrad(act, [dh_out], layer, layers, prev and prev[:1], True, "ffn_dwd" + tag)
        d_gu = _ffn_wgrad(hn, [dg, du], layer, layers, prev and prev[1:], False, "ffn_dwgu" + tag)
        grads = list(d_down) + list(d_gu)
        gs = rs_grads(ffn_names, grads) if last else None
        res = _ffn_up_bwd(dg, du, wts["ffn_w_gate"], wts["ffn_w_up"], layer, h_mid, wts["ffn_norm_w"][layer],
                          dh_out, "ffn_up_bwd" + tag, ride=_Ride("pair", gs) if last else None)
        (dh_mid, d_norm), sib = res if last else (res, None)
        return dh_mid, grads, d_norm, rode, gs, sib

    red = {}

    def rs_grads(names, grads):
        return [gr.reshape((N_SHARD,) + sh[n].shape) for n, gr in zip(names, grads)]

    def rs_partials(names, gs, sib):
        return [_rs_pair_add(gs[t], sib[t], idx, "rs_pair_add_" + n) for t, n in enumerate(names)]

    def rs_end(names, gs, sib, others, tag):
        mine = [_rs_final_add(gs[t], sib[t], others[t], idx, "rs_final_add_" + n) for t, n in enumerate(names)]
        red.update(zip(names, _rs_share(mine, "rs_share" + tag)))

    dh3, ffn_grads, dfn1 = ffn_bwd(dh4, h3, hn3, g1, u1, act1, 1, None)[:3]
    dy1 = _mm(dh3, wts["dn_w_out"], mode="nt", name="dn_out_bwd")
    d_dn_out = _mm(y1, dh3, mode="tn", name="dn_dwo")
    dproj1, dcw, dal, ddt, g["dn_norm_w"] = _dn_bwd(proj1, conv1, o1, dy1, st1, tinv, conv_w, alog, dtb,
                                                    wts["dn_norm_w"], seq)
    d_dn_in = _mm(hn2, dproj1, mode="tn", name="dn_dwi")
    d_dn_in = jnp.stack([d_dn_in[:, j * n_dn:(j + 1) * n_dn] for j in range(N_SHARD)])
    group1 = ["dn_w_out", "dn_w_in"]
    gs1 = rs_grads(group1, [d_dn_out, d_dn_in])
    (dh2, dmn1), sib1 = _mm_nt_rms(dproj1, wts["dn_w_in"], h2, wts["mix_norm_w"][1], dh3, "dn_in_bwd",
                                   ride=_Ride("pair", gs1))
    g["dn_conv_w"] = dcw.reshape(CONV_K, DN_CONV_CH)
    g["dn_a_log"] = dal[0, DECAY_LANE:DECAY_LANE + DN_HEADS]
    g["dn_dt_bias"] = ddt[0, DECAY_LANE:DECAY_LANE + DN_HEADS]

    dh1, _, dfn0, others1, gs2, sib2 = ffn_bwd(dh2, h1, hn1, g0, u0, act0, 0, ffn_grads,
                                               ride=_Ride("chips", rs_partials(group1, gs1, sib1)), last=True)
    rs_end(group1, gs1, sib1, others1, "1")
    d_ret_out = _mm(y0, dh1, mode="tn", name="ret_dwo")
    gs2b = rs_grads(["ret_w_out"], [d_ret_out])
    dy0, sib2b = _mm(dh1, wts["ret_w_out"], mode="nt", name="ret_out_bwd", ride=_Ride("pair", gs2b))
    group2 = ffn_names + ["ret_w_out"]
    gs2, sib2 = gs2 + gs2b, list(sib2) + list(sib2b)
    (dproj0, g["ret_gn_w"]), others2 = _ret_bwd(proj0, o0, dy0, st0, cos, sin, consts, wts["ret_gn_w"], seq,
                                                ride=_Ride("chips", rs_partials(group2, gs2, sib2)))
    rs_end(group2, gs2, sib2, others2, "2")
    d_ret_in = _mm_cols_grad(hn0, dproj0, "ret_dwi")
    gs3 = rs_grads(["ret_w_in"], [d_ret_in])
    sib3 = _rs_pair(gs3, "rs_pair3")
    (dh0, dmn0), others3 = _mm_cols_t_rms(dproj0, wts["ret_w_in"], h0, wts["mix_norm_w"][0], dh1, "ret_in_bwd",
                                          ride=_Ride("chips", rs_partials(["ret_w_in"], gs3, sib3)))
    rs_end(["ret_w_in"], gs3, sib3, others3, "3")

    g["ffn_norm_w"] = jnp.concatenate([dfn0, dfn1], axis=0)
    g["mix_norm_w"] = jnp.concatenate([dmn0, dmn1], axis=0)
    g["meta_tokens"] = dh0[PAD:CHUNK]
    g["final_norm_w"] = g["final_norm_w"].reshape(D_MODEL)
    g["ret_gn_w"] = g["ret_gn_w"].reshape(RET_DV)
    g["dn_norm_w"] = g["dn_norm_w"].reshape(DN_DV)
    return loss, dh0, g, red


def _mesh_pos():
    return lax.axis_index("x"), lax.axis_index("y"), lax.axis_index("c")


def _other_chips(x, y):
    return [(1 - x, y), (x, 1 - y), (1 - x, 1 - y)]


def _remote(src, dst, send_sem, recv_sem, to):
    return pltpu.make_async_remote_copy(src_ref=src, dst_ref=dst, send_sem=send_sem, recv_sem=recv_sem,
                                        device_id=to, device_id_type=MESH)


GATHER_COPIES = 7


def _gather_weights(shards):
    ride = _Ride("gather", shards)

    def body(*refs):
        nt = len(shards)
        for phase in range(3):
            _gather_phase(phase, refs[:nt], refs[nt:2 * nt], *refs[2 * nt:])

    return pl.pallas_call(body, out_shape=ride.out_shape, in_specs=ride.in_specs, out_specs=ride.out_specs,
                          scratch_shapes=ride.scratch, name="gather_weights")(*shards)


def _gather_phase(phase, ins, outs, send_sems, recv_sems):
    x, y, c = _mesh_pos()
    me = 2 * x + y
    chips = _other_chips(x, y)
    sibling = (x, y, 1 - c)

    def cp(t, k, src, dst, to):
        i = GATHER_COPIES * t + k
        return _remote(src, dst, send_sems.at[i], recv_sems.at[i], to)

    for t in range(len(ins)):
        own = cp(t, 0, ins[t], outs[t].at[me], sibling)
        if phase == 0:
            own.start()
        if phase == 2:
            own.wait()
        for k, (px, py) in enumerate(chips):
            landed = outs[t].at[2 * px + py, c]
            theirs = outs[t].at[2 * px + py, 1 - c]
            to_chip = cp(t, 1 + k, ins[t].at[c], outs[t].at[me, c], (px, py, c))
            if phase == 0:
                to_chip.start()
            if phase == 1:
                cp(t, 1 + k, ins[t].at[c], landed, (px, py, c)).wait_recv()
                cp(t, 4 + k, landed, landed, sibling).start()
            if phase == 2:
                to_chip.wait_send()
                cp(t, 4 + k, landed, landed, sibling).wait_send()
                cp(t, 4 + k, theirs, theirs, sibling).wait_recv()


def _chips_phase(phase, ins, outs, send_sems, recv_sems):
    x, y, c = _mesh_pos()
    for t in range(len(ins)):
        for k, (px, py) in enumerate(_other_chips(x, y)):
            cp = _remote(ins[t].at[2 * px + py], outs[t].at[k], send_sems.at[3 * t + k], recv_sems.at[3 * t + k],
                         (px, py, c))
            if phase == 0:
                cp.start()
            if phase == 2:
                cp.wait()


class _Ride:
    def __init__(self, kind, arrays):
        self.kind, self.arrays = kind, list(arrays)
        nt = len(self.arrays)
        if kind == "gather":
            self.phase_fn, n_sem = _gather_phase, GATHER_COPIES * nt
            self.out_shape = [jax.ShapeDtypeStruct((N_SHARD,) + a.shape, a.dtype) for a in self.arrays]
        elif kind == "pair":
            self.phase_fn, n_sem = _pair_phase, nt
            self.out_shape = [jax.ShapeDtypeStruct(a.shape[:1] + a.shape[2:], a.dtype) for a in self.arrays]
        else:
            self.phase_fn, n_sem = _chips_phase, 3 * nt
            self.out_shape = [jax.ShapeDtypeStruct((3,) + a.shape[1:], a.dtype) for a in self.arrays]
        self.in_specs, self.out_specs = [ANY] * nt, [ANY] * nt
        self.scratch = [pltpu.SemaphoreType.DMA((n_sem,)), pltpu.SemaphoreType.DMA((n_sem,))]

    def emit(self, step, nsteps, ins, outs, send_sems, recv_sems, before):
        mid = max(0, min((7 * nsteps) // 8, nsteps - 2))
        todo = [(0, 0), (1, mid)] if before else [(2, nsteps - 1)]
        for phase, at in todo:
            if phase == 1 and self.kind != "gather":
                continue

            @pl.when(step == at)
            def _(phase=phase):
                self.phase_fn(phase, ins, outs, send_sems, recv_sems)


def _gather_small(blk):
    r, wd = blk.shape

    def body(b_ref, out_ref, send_sems, recv_sems):
        x, y, c = _mesh_pos()
        chips = _other_chips(x, y)
        out_ref[2 * x + y] = b_ref[...]
        sends = [_remote(b_ref, out_ref.at[2 * x + y], send_sems.at[k], recv_sems.at[k], (px, py, c))
                 for k, (px, py) in enumerate(chips)]
        for cp in sends:
            cp.start()
        for k, (px, py) in enumerate(chips):
            _remote(b_ref, out_ref.at[2 * px + py], send_sems.at[k], recv_sems.at[k], (px, py, c)).wait_recv()
        for cp in sends:
            cp.wait_send()

    return pl.pallas_call(
        body, out_shape=jax.ShapeDtypeStruct((4, r, wd), blk.dtype), in_specs=[VMEM_SPEC], out_specs=VMEM_SPEC,
        scratch_shapes=[pltpu.SemaphoreType.DMA((3,)), pltpu.SemaphoreType.DMA((3,))],
        name="gather_small")(blk)


def _allreduce_small(blk):
    r, wd = blk.shape
    rels = [(dx, dy, dc) for dx in (0, 1) for dy in (0, 1) for dc in (0, 1) if dx or dy or dc]

    def body(b_ref, out_ref, buf_ref, send_sems, recv_sems):
        x, y, c = _mesh_pos()

        def peer(rel):
            dx, dy, dc = rel
            return (1 - x if dx else x, 1 - y if dy else y, 1 - c if dc else c)

        me = 4 * x + 2 * y + c
        buf_ref[me] = b_ref[...]
        sends = [_remote(b_ref, buf_ref.at[me], send_sems.at[k], recv_sems.at[k], peer(rel))
                 for k, rel in enumerate(rels)]
        for cp in sends:
            cp.start()
        for k, rel in enumerate(rels):
            px, py, pc = peer(rel)
            _remote(b_ref, buf_ref.at[4 * px + 2 * py + pc], send_sems.at[k], recv_sems.at[k],
                    (px, py, pc)).wait_recv()
        for cp in sends:
            cp.wait_send()
        acc = buf_ref[0]
        for d in range(1, 8):
            acc = acc + buf_ref[d]
        out_ref[...] = acc

    return pl.pallas_call(
        body, out_shape=jax.ShapeDtypeStruct((r, wd), blk.dtype), in_specs=[VMEM_SPEC], out_specs=VMEM_SPEC,
        scratch_shapes=[pltpu.VMEM((8, r, wd), blk.dtype), pltpu.SemaphoreType.DMA((7,)),
                        pltpu.SemaphoreType.DMA((7,))],
        name="allreduce_small")(blk)


def _rs_pair(gs, name):
    ride = _Ride("pair", gs)

    def body(*refs):
        nt = len(gs)
        for phase in (0, 2):
            _pair_phase(phase, refs[:nt], refs[nt:2 * nt], *refs[2 * nt:])

    return pl.pallas_call(body, out_shape=ride.out_shape, in_specs=ride.in_specs, out_specs=ride.out_specs,
                          scratch_shapes=ride.scratch, name=name)(*gs)


def _pair_phase(phase, ins, outs, send_sems, recv_sems):
    x, y, c = _mesh_pos()
    for t in range(len(ins)):
        cp = _remote(ins[t].at[:, 1 - c], outs[t], send_sems.at[t], recv_sems.at[t], (x, y, 1 - c))
        if phase == 0:
            cp.start()
        if phase == 2:
            cp.wait()


def _rs_tile(a, b):
    return _div_tile(a, 512 if b <= 1024 else 256, 16)


def _rs_pair_add(g, a, idx, name):
    _, _, rows, cols = g.shape
    tr = _rs_tile(rows, cols)

    def body(s_ref, g_ref, a_ref, p_ref):
        p_ref[...] = (g_ref[...] + a_ref[...]).astype(p_ref.dtype)

    blk = pl.BlockSpec((None, tr, cols), lambda j, i, s: (j, i, 0))
    spec = pltpu.PrefetchScalarGridSpec(
        num_scalar_prefetch=1, grid=(N_SHARD, rows // tr),
        in_specs=[pl.BlockSpec((None, None, tr, cols), lambda j, i, s: (j, s[0], i, 0)), blk], out_specs=blk)
    return pl.pallas_call(
        body, grid_spec=spec, out_shape=jax.ShapeDtypeStruct((N_SHARD, rows, cols), BF16), name=name,
        compiler_params=_params("parallel", "parallel"))(idx, g, a)


def _rs_final_add(g, a, b, idx, name):
    _, _, rows, cols = g.shape
    tr = _rs_tile(rows, cols)

    def body(s_ref, g_ref, a_ref, b0_ref, b1_ref, b2_ref, f_ref):
        own = g_ref[...] + a_ref[...]
        f_ref[...] = ((own + b0_ref[...].astype(F32)) + b1_ref[...].astype(F32)) + b2_ref[...].astype(F32)

    def b_spec(k):
        return pl.BlockSpec((None, tr, cols), lambda i, s: (k, i, 0))

    spec = pltpu.PrefetchScalarGridSpec(
        num_scalar_prefetch=1, grid=(rows // tr,),
        in_specs=[pl.BlockSpec((None, None, tr, cols), lambda i, s: (s[1], s[0], i, 0)),
                  pl.BlockSpec((None, tr, cols), lambda i, s: (s[1], i, 0)), b_spec(0), b_spec(1), b_spec(2)],
        out_specs=pl.BlockSpec((None, tr, cols), lambda i, s: (s[0], i, 0)))
    return pl.pallas_call(
        body, grid_spec=spec, out_shape=jax.ShapeDtypeStruct((2, rows, cols), F32), name=name,
        compiler_params=_params("parallel"))(idx, g, a, b, b, b)


def _rs_share(fs, name):
    nt = len(fs)

    def body(*refs):
        outs = refs[nt:2 * nt]
        send_sems, recv_sems = refs[2 * nt:]
        x, y, c = _mesh_pos()
        cps = [_remote(outs[t].at[c], outs[t].at[c], send_sems.at[t], recv_sems.at[t], (x, y, 1 - c))
               for t in range(nt)]
        for cp in cps:
            cp.start()
        for cp in cps:
            cp.wait()

    return pl.pallas_call(
        body, out_shape=[jax.ShapeDtypeStruct(f.shape, f.dtype) for f in fs],
        in_specs=[ANY] * nt, out_specs=[ANY] * nt, input_output_aliases={t: t for t in range(nt)},
        scratch_shapes=[pltpu.SemaphoreType.DMA((nt,)), pltpu.SemaphoreType.DMA((nt,))], name=name)(*fs)


def _adamw(w, g, m, v, name):
    lead, rows, cols = w.shape
    tr = rows // 4 if rows % 32 == 0 else rows

    def body(w_ref, g_ref, m_ref, v_ref, go_ref, d_ref, mo_ref, vo_ref):
        gv = g_ref[...]
        go_ref[...] = gv
        mn = ADAM_B1 * m_ref[...] + (1.0 - ADAM_B1) * gv
        vn = ADAM_B2 * v_ref[...] + (1.0 - ADAM_B2) * (gv * gv)
        m_hat = mn / (1.0 - ADAM_B1 ** ADAM_STEP)
        v_hat = vn / (1.0 - ADAM_B2 ** ADAM_STEP)
        d_ref[...] = -ADAM_LR * (m_hat / (jnp.sqrt(v_hat) + ADAM_EPS) + ADAM_WD * w_ref[...])
        mo_ref[...] = mn
        vo_ref[...] = vn

    blk = pl.BlockSpec((None, tr, cols), lambda l, i: (l, i, 0))
    out = jax.ShapeDtypeStruct((lead, rows, cols), F32)
    return pl.pallas_call(
        body, grid=(lead, rows // tr), in_specs=[blk] * 4, out_specs=[blk] * 4, out_shape=[out] * 4, name=name,
        compiler_params=_params("parallel", "parallel"))(w, g, m, v)


BIG = ["ret_w_in", "ret_w_out", "dn_w_in", "dn_w_out", "ffn_w_gate", "ffn_w_up", "ffn_w_down"]
TRANSPOSED_AT_BOUNDARY = {"dn_w_in": True, "ffn_w_gate": False, "ffn_w_up": False}
SMALL =["meta_tokens", "mix_norm_w", "ffn_norm_w", "ret_gn_w", "dn_conv_w", "dn_a_log", "dn_dt_bias",
         "dn_norm_w", "final_norm_w"]
SMALL_SHARDED = {"meta_tokens", "dn_conv_w", "dn_norm_w"}
ORDER = ["meta_tokens", "mix_norm_w", "ffn_norm_w", "ret_w_in", "ret_gn_w", "ret_w_out", "dn_w_in",
         "dn_conv_w", "dn_a_log", "dn_dt_bias", "dn_norm_w", "dn_w_out", "ffn_w_gate", "ffn_w_up",
         "ffn_w_down", "final_norm_w"]


def _halves(a):
    return a.reshape(2, -1, a.shape[-1])


def _pack_lanes(parts, align=8):
    flat = jnp.concatenate([p.reshape(-1) for p in parts])
    flat = jnp.pad(flat, (0, -flat.shape[0] % (align * LANES)))
    return flat.reshape(-1, LANES)


def _unpack(buf, shapes):
    lead = buf.shape[:-2]
    flat = buf.reshape(lead + (-1,))
    out, off = [], 0
    for shp in shapes:
        size = math.prod(shp)
        out.append(flat[..., off:off + size].reshape(lead + tuple(shp)))
        off += size
    return out


def _join_cols(shards):
    return jnp.concatenate([shards[j] for j in range(N_SHARD)], axis=-1)


def kernel(x, meta_tokens, mix_norm_w, ffn_norm_w, ret_w_in, ret_gn_w, ret_w_out, dn_w_in, dn_conv_w, dn_a_log, dn_dt_bias, dn_norm_w, dn_w_out, ffn_w_gate, ffn_w_up, ffn_w_down, final_norm_w, loss_target, m_meta_tokens, m_mix_norm_w, m_ffn_norm_w, m_ret_w_in, m_ret_gn_w, m_ret_w_out, m_dn_w_in, m_dn_conv_w, m_dn_a_log, m_dn_dt_bias, m_dn_norm_w, m_dn_w_out, m_ffn_w_gate, m_ffn_w_up, m_ffn_w_down, m_final_norm_w, v_meta_tokens, v_mix_norm_w, v_ffn_norm_w, v_ret_w_in, v_ret_gn_w, v_ret_w_out, v_dn_w_in, v_dn_conv_w, v_dn_a_log, v_dn_dt_bias, v_dn_norm_w, v_dn_w_out, v_ffn_w_gate, v_ffn_w_up, v_ffn_w_down, v_final_norm_w):
    w = dict(meta_tokens=meta_tokens, mix_norm_w=mix_norm_w, ffn_norm_w=ffn_norm_w, ret_w_in=ret_w_in,
             ret_gn_w=ret_gn_w, ret_w_out=ret_w_out, dn_w_in=dn_w_in, dn_conv_w=dn_conv_w, dn_a_log=dn_a_log,
             dn_dt_bias=dn_dt_bias, dn_norm_w=dn_norm_w, dn_w_out=dn_w_out, ffn_w_gate=ffn_w_gate,
             ffn_w_up=ffn_w_up, ffn_w_down=ffn_w_down, final_norm_w=final_norm_w)
    m = dict(meta_tokens=m_meta_tokens, mix_norm_w=m_mix_norm_w, ffn_norm_w=m_ffn_norm_w, ret_w_in=m_ret_w_in,
             ret_gn_w=m_ret_gn_w, ret_w_out=m_ret_w_out, dn_w_in=m_dn_w_in, dn_conv_w=m_dn_conv_w,
             dn_a_log=m_dn_a_log, dn_dt_bias=m_dn_dt_bias, dn_norm_w=m_dn_norm_w, dn_w_out=m_dn_w_out,
             ffn_w_gate=m_ffn_w_gate, ffn_w_up=m_ffn_w_up, ffn_w_down=m_ffn_w_down, final_norm_w=m_final_norm_w)
    v = dict(meta_tokens=v_meta_tokens, mix_norm_w=v_mix_norm_w, ffn_norm_w=v_ffn_norm_w, ret_w_in=v_ret_w_in,
             ret_gn_w=v_ret_gn_w, ret_w_out=v_ret_w_out, dn_w_in=v_dn_w_in, dn_conv_w=v_dn_conv_w,
             dn_a_log=v_dn_a_log, dn_dt_bias=v_dn_dt_bias, dn_norm_w=v_dn_norm_w, dn_w_out=v_dn_w_out,
             ffn_w_gate=v_ffn_w_gate, ffn_w_up=v_ffn_w_up, ffn_w_down=v_ffn_w_down, final_norm_w=v_final_norm_w)
    mx, my, mc = _mesh_pos()
    chip = 2 * mx + my

    sm_names = [n for n in SMALL if n in SMALL_SHARDED]
    sm_gathered = _unpack(_gather_small(_pack_lanes([w[n] for n in sm_names])), [w[n].shape for n in sm_names])
    full = {n: _join_cols(sm_gathered[i]) for i, n in enumerate(sm_names)}
    wts = {
        "meta_tokens": full["meta_tokens"], "mix_norm_w": mix_norm_w, "ffn_norm_w": ffn_norm_w,
        "ret_gn_w": ret_gn_w[0], "final_norm_w": final_norm_w, "dn_conv_w": full["dn_conv_w"][0],
        "dn_a_log": dn_a_log[0], "dn_dt_bias": dn_dt_bias[0], "dn_norm_w": full["dn_norm_w"][0],
    }
    idx = jnp.stack([mc, chip]).astype(jnp.int32)
    shards = {n: _halves(w[n].astype(MXU_DTYPE)) for n in BIG}
    loss_part, dh0, g, reduced = _train_step(x[0], loss_target[0], wts, shards, idx)
    seq = x.shape[1]
    grad_x = dh0[CHUNK:CHUNK + seq].reshape(x.shape)
    gsh = {}

    small_full_shapes = [g[n].shape for n in SMALL] + [(1,)]
    red = _unpack(_allreduce_small(_pack_lanes([g[n] for n in SMALL] + [loss_part[0, :1]])), small_full_shapes)
    loss = red[-1][0]
    for i, n in enumerate(SMALL):
        gn = red[i]
        if n in SMALL_SHARDED:
            width = w[n].shape[-1]
            gn = lax.dynamic_slice_in_dim(gn, chip * width, width, axis=gn.ndim - 1)
        gsh[n] = gn.reshape(w[n].shape)

    delta, new_m, new_v = {}, {}, {}
    for n in BIG:
        shp = w[n].shape
        if n in TRANSPOSED_AT_BOUNDARY and TRANSPOSED_AT_BOUNDARY[n]:
            view = lambda a: jnp.swapaxes(a, 1, 2).reshape(1, -1, LANES)
            back = lambda a: jnp.swapaxes(a.reshape(shp[0], shp[2], shp[1]), 1, 2)
        elif n in TRANSPOSED_AT_BOUNDARY:
            view = back = lambda a: jnp.swapaxes(a, 1, 2)
        else:
            view = back = lambda a: a
        res = _adamw(view(w[n]), view(reduced[n].reshape(shp)), view(m[n]), view(v[n]), "adamw_" + n)
        gsh[n], delta[n], new_m[n], new_v[n] = [back(r) for r in res]
    sm_local_shapes = [w[n].shape for n in SMALL]
    _, d_, m_, v_ = _adamw(*[_pack_lanes([t[n] for n in SMALL])[None] for t in (w, gsh, m, v)], "adamw_small")
    d_, m_, v_ = d_[0], m_[0], v_[0]
    for n, dd, mm, vv in zip(SMALL, _unpack(d_, sm_local_shapes), _unpack(m_, sm_local_shapes),
                             _unpack(v_, sm_local_shapes)):
        delta[n], new_m[n], new_v[n] = dd, mm, vv

    return (loss, grad_x, *[gsh[n] for n in ORDER], *[delta[n] for n in ORDER],
            *[new_m[n] for n in ORDER], *[new_v[n] for n in ORDER])
```

```python
import functools
import math

import jax
import jax.numpy as jnp
from jax import lax
from jax.experimental import pallas as pl
from jax.experimental.pallas import tpu as pltpu

F32 = jnp.float32
BF16 = jnp.bfloat16
MXU_DTYPE = BF16

D_MODEL = 1024
N_META = 16
CHUNK = 64
PAD = CHUNK - N_META
RMS_EPS = 1e-6
RET_HEADS, RET_DK, RET_DV = 4, 256, 512
RET_QK, RET_V = RET_HEADS * RET_DK, RET_HEADS * RET_DV
RET_IN = 2 * RET_QK + 2 * RET_V
ROPE_BASE = 10000.0
DN_HEADS, DN_DK, DN_DV = 8, 128, 256
DN_QK, DN_V = DN_HEADS * DN_DK, DN_HEADS * DN_DV
DN_CONV_CH = 2 * DN_QK + DN_V
DN_IN = DN_CONV_CH + DN_V + 2 * DN_HEADS
LANES = 128
DN_IN_USED = DN_CONV_CH + DN_V + LANES
DN_IN_PAD = DN_IN_USED + LANES
CONV_K = 4
FFN_HIDDEN = 2816
ADAM_LR, ADAM_B1, ADAM_B2, ADAM_EPS, ADAM_WD, ADAM_STEP = 0.001, 0.9, 0.999, 1e-08, 0.01, 10

ROW_ALIGN = 256
VMEM_LIMIT = 56 * 1024 * 1024
MESH = pl.DeviceIdType.MESH
ANY = pl.BlockSpec(memory_space=pl.ANY)
VMEM_SPEC = pl.BlockSpec(memory_space=pltpu.VMEM)
_HI = lax.Precision.HIGHEST


def _params(*sem):
    return pltpu.CompilerParams(dimension_semantics=sem, vmem_limit_bytes=VMEM_LIMIT)


def _dg(a, b, ca, cb, hi):
    dims = (((ca,), (cb,)), ((), ()))

    def dot(p, q):
        return lax.dot_general(p, q, dims, preferred_element_type=F32)

    if not hi:
        return dot(a.astype(MXU_DTYPE), b.astype(MXU_DTYPE))
    if MXU_DTYPE == F32:
        return lax.dot_general(a, b, dims, precision=_HI, preferred_element_type=F32)
    a_hi, b_hi = a.astype(MXU_DTYPE), b.astype(MXU_DTYPE)
    a_lo = (a - a_hi.astype(F32)).astype(MXU_DTYPE)
    b_lo = (b - b_hi.astype(F32)).astype(MXU_DTYPE)
    return dot(a_hi, b_hi) + (dot(a_hi, b_lo) + dot(a_lo, b_hi))


def _nn(a, b, hi=False):
    return _dg(a, b, 1, 0, hi)


def _nt(a, b, hi=False):
    return _dg(a, b, 1, 1, hi)


def _tn(a, b, hi=False):
    return _dg(a, b, 0, 0, hi)


def _iota(shape, dim):
    return lax.broadcasted_iota(jnp.int32, shape, dim)


def _valid_rows(first_row, rows, seq):
    r = first_row + _iota((rows, 1), 0)
    return ((r >= PAD) & (r < CHUNK + seq)).astype(F32)


def _rope(t, cs, sn):
    half = t.shape[-1] // 2
    t1, t2 = t[:, :half], t[:, half:]
    return jnp.concatenate([t1 * cs - t2 * sn, t1 * sn + t2 * cs], axis=1)


def _rope_bwd(d, cs, sn):
    half = d.shape[-1] // 2
    d1, d2 = d[:, :half], d[:, half:]
    return jnp.concatenate([d1 * cs + d2 * sn, d2 * cs - d1 * sn], axis=1)


def _col(x, idx):
    oh = (_iota((1, x.shape[1]), 1) == idx).astype(F32)
    return jnp.sum(x * oh, axis=1, keepdims=True)


def _row(x, idx):
    oh = (_iota((x.shape[0], 1), 0) == idx).astype(F32)
    return jnp.sum(x * oh, axis=0, keepdims=True)


def _shift_down(x, halo8, k):
    xr = pltpu.roll(x, k, 0)
    hr = pltpu.roll(halo8, k, 0)
    first = jnp.where(_iota((8, 1), 0) < k, hr, xr[0:8])
    return jnp.concatenate([first, xr[8:]], axis=0)


def _shift_up(x, next8, j):
    rows = x.shape[0]
    xr = pltpu.roll(x, rows - j, 0)
    nr = pltpu.roll(next8, 8 - j, 0)
    last = jnp.where(_iota((8, 1), 0) >= 8 - j, nr, xr[rows - 8:])
    return jnp.concatenate([xr[:rows - 8], last], axis=0)


def _gated_norm(o, gate, w):
    r = lax.rsqrt(jnp.mean(o * o, axis=-1, keepdims=True) + RMS_EPS)
    return o * r * w * (gate * jax.nn.sigmoid(gate))


def _gated_norm_bwd(dy, o, gate, w):
    r = lax.rsqrt(jnp.mean(o * o, axis=-1, keepdims=True) + RMS_EPS)
    nrm = o * r
    sg = jax.nn.sigmoid(gate)
    sl = gate * sg
    dgate = dy * nrm * w * (sg * (1.0 + gate * (1.0 - sg)))
    dn = dy * w * sl
    dw = jnp.sum(dy * nrm * sl, axis=0, keepdims=True)
    do = r * (dn - nrm * jnp.mean(dn * nrm, axis=-1, keepdims=True))
    return do, dgate, dw


def _softplus(z):
    return jnp.maximum(z, 0.0) + jnp.log(1.0 + jnp.exp(-jnp.abs(z)))


def _row_tile(rows, cap=768):
    for t in (768, 512, 256, 128, 64, 32, 16, 8):
        if t <= cap and rows % t == 0:
            return t
    return rows


TILE_BUDGET = 44 * 1024 * 1024


def _fit_rows(rows, row_bytes, fixed_bytes, value_row_bytes):
    best = None
    for t in range(LANES, rows + 1, LANES):
        if rows % t == 0 and 2 * (row_bytes * t + fixed_bytes) + value_row_bytes * t <= TILE_BUDGET:
            best = t
    return best or _row_tile(rows, 256)


def _div_tile(n, cap, mult):
    best = None
    for t in range(mult, min(cap, n) + 1, mult):
        if n % t == 0:
            best = t
    return best or n


def _col_tile(cols, cap=1536):
    best = None
    for t in range(LANES, min(cap, cols) + 1, LANES):
        if cols % t == 0:
            best = t
    return best or cols


def _rms_fwd(h, w, name, ride=None):
    rows, d = h.shape
    tm = _row_tile(rows)

    def body(h_ref, w_ref, o_ref):
        x = h_ref[...]
        r = lax.rsqrt(jnp.mean(x * x, axis=-1, keepdims=True) + RMS_EPS)
        o_ref[...] = (x * r * w_ref[...]).astype(o_ref.dtype)

    res, rode = _pcall(body, [h, w.reshape(1, d)], grid=(rows // tm,),
                       in_specs=[pl.BlockSpec((tm, d), lambda i: (i, 0)), pl.BlockSpec((1, d), lambda i: (0, 0))],
                       out_specs=[pl.BlockSpec((tm, d), lambda i: (i, 0))],
                       out_shape=[jax.ShapeDtypeStruct((rows, d), BF16)], name=name, sem=("parallel",), ride=ride)
    return res[0] if ride is None else (res[0], rode)


def _gmm_rms(name, grid, args, in_specs, row_spec, fn, h, w, resid, row_axis, red_axis=None, ride=None):
    m, d = h.shape
    n_in = len(args)
    vec = pl.BlockSpec((1, d), lambda *g: (0, 0))

    def body(*refs):
        ins = refs[:n_in]
        h_ref, w_ref, r_ref, dh_ref, dw_ref = refs[n_in:]
        part = fn(*ins)
        row = pl.program_id(row_axis)

        def finish(dy):
            x = h_ref[...]
            r = lax.rsqrt(jnp.mean(x * x, axis=-1, keepdims=True) + RMS_EPS)
            xh = x * r
            dxh = dy * w_ref[...]
            dh_ref[...] = r_ref[...] + r * (dxh - xh * jnp.mean(dxh * xh, axis=-1, keepdims=True))
            dwp = jnp.sum(dy * xh, axis=0, keepdims=True)

            @pl.when(row == 0)
            def _():
                dw_ref[...] = dwp

            @pl.when(row > 0)
            def _():
                dw_ref[...] += dwp

        if red_axis is None:
            finish(part)
            return
        k = pl.program_id(red_axis)

        @pl.when(k == 0)
        def _():
            dh_ref[...] = part

        @pl.when(k > 0)
        def _():
            dh_ref[...] += part

        @pl.when(k == grid[red_axis] - 1)
        def _():
            finish(dh_ref[...])

    res, rode = _pcall(body, list(args) + [h, w.reshape(1, d), resid], grid=grid,
                       in_specs=list(in_specs) + [row_spec, vec, row_spec], out_specs=[row_spec, vec],
                       out_shape=[jax.ShapeDtypeStruct((m, d), F32), jax.ShapeDtypeStruct((1, d), F32)],
                       name=name, sem=("arbitrary",) * len(grid), ride=ride)
    return res if ride is None else (res, rode)


def _final_loss(h, w, tgt, seq, name):
    rows, d = h.shape
    tm = _row_tile(rows)

    def body(h_ref, w_ref, t_ref, dh_ref, dw_ref, loss_ref):
        i = pl.program_id(0)
        r_idx = i * tm + _iota((tm, 1), 0)
        m = ((r_idx >= CHUNK) & (r_idx < CHUNK + seq)).astype(F32)
        x = h_ref[...]
        wv = w_ref[...]
        r = lax.rsqrt(jnp.mean(x * x, axis=-1, keepdims=True) + RMS_EPS)
        xh = x * r
        err = (xh * wv - t_ref[...]) * m
        lpart = 0.5 * jnp.sum(jnp.mean(err * err, axis=-1, keepdims=True), axis=0, keepdims=True)
        dyv = err * (1.0 / d)
        dxh = dyv * wv
        dh_ref[...] = r * (dxh - xh * jnp.mean(dxh * xh, axis=-1, keepdims=True))
        part = jnp.sum(dyv * xh, axis=0, keepdims=True)

        @pl.when(i == 0)
        def _():
            dw_ref[...] = part
            loss_ref[...] = jnp.broadcast_to(lpart, loss_ref.shape)

        @pl.when(i > 0)
        def _():
            dw_ref[...] += part
            loss_ref[...] += jnp.broadcast_to(lpart, loss_ref.shape)

    blk = pl.BlockSpec((tm, d), lambda i: (i, 0))
    vec = pl.BlockSpec((1, d), lambda i: (0, 0))
    return pl.pallas_call(
        body, grid=(rows // tm,), in_specs=[blk, vec, blk],
        out_specs=[blk, vec, pl.BlockSpec((1, LANES), lambda i: (0, 0))],
        out_shape=[jax.ShapeDtypeStruct((rows, d), F32), jax.ShapeDtypeStruct((1, d), F32),
                   jax.ShapeDtypeStruct((1, LANES), F32)],
        name=name, compiler_params=_params("arbitrary"))(h, w.reshape(1, d), tgt)


def _isz(x):
    return jnp.dtype(x.dtype).itemsize


def _mm(a, b, *, mode, name, out_dtype=F32, resid=None, col_cap=1536, ride=None):
    if mode == "tn":
        m, k = a.shape
        n = b.shape[1]
        tn = _col_tile(n, col_cap)
        tm = _fit_rows(m, k * _isz(a) + tn * _isz(b), (3 * k * tn * 4) // 2, 2 * (k + tn))

        def body_tn(a_ref, b_ref, o_ref):
            i = pl.program_id(1)
            part = _tn(a_ref[...], b_ref[...])

            @pl.when(i == 0)
            def _():
                o_ref[...] = part

            @pl.when(i > 0)
            def _():
                o_ref[...] += part

        return pl.pallas_call(
            body_tn, grid=(n // tn, m // tm),
            in_specs=[pl.BlockSpec((tm, k), lambda j, i: (i, 0)),
                      pl.BlockSpec((tm, tn), lambda j, i: (i, j))],
            out_specs=pl.BlockSpec((k, tn), lambda j, i: (0, j)),
            out_shape=jax.ShapeDtypeStruct((k, n), F32), name=name,
            compiler_params=_params("parallel", "arbitrary"))(a, b)

    m, ka = a.shape
    n = b.shape[1] if mode == "nn" else b.shape[0]
    has_resid = resid is not None
    tn = _col_tile(n, col_cap)
    tm = _fit_rows(m, ka * _isz(a) + tn * (jnp.dtype(out_dtype).itemsize + (4 if has_resid else 0)),
                   ka * tn * _isz(b), 2 * ka + 8 * tn)

    def body(*refs):
        if has_resid:
            a_ref, b_ref, r_ref, o_ref = refs
        else:
            a_ref, b_ref, o_ref = refs
        acc = _nn(a_ref[...], b_ref[...]) if mode == "nn" else _nt(a_ref[...], b_ref[...])
        if has_resid:
            acc = acc + r_ref[...]
        o_ref[...] = acc.astype(o_ref.dtype)

    b_spec = (pl.BlockSpec((b.shape[0], tn), lambda j, i: (0, j)) if mode == "nn"
              else pl.BlockSpec((tn, b.shape[1]), lambda j, i: (j, 0)))
    o_spec = pl.BlockSpec((tm, tn), lambda j, i: (i, j))
    in_specs = [pl.BlockSpec((tm, ka), lambda j, i: (i, 0)), b_spec]
    args = [a, b]
    if has_resid:
        in_specs.append(o_spec)
        args.append(resid)
    res, rode = _pcall(body, args, grid=(n // tn, m // tm), in_specs=in_specs, out_specs=[o_spec],
                       out_shape=[jax.ShapeDtypeStruct((m, n), out_dtype)], name=name,
                       sem=("parallel", "parallel"), ride=ride)
    return res[0] if ride is None else (res[0], rode)


N_SHARD = 4


def _gmm(name, grid, args, in_specs, out_specs, out_shape, fn, red_axis=None, init_arg=None, aliases=None,
         ride=None):
    n_in = len(args)
    single = not isinstance(out_shape, (list, tuple))
    out_specs = [out_specs] if single else list(out_specs)
    out_shape = [out_shape] if single else list(out_shape)

    def body(*refs):
        _gmm_step(fn, refs[:n_in], refs[n_in:], red_axis, init_arg)

    sem = tuple("arbitrary" if ax == red_axis else "parallel" for ax in range(len(grid)))
    res, rode = _pcall(body, args, grid=grid, in_specs=in_specs, out_specs=out_specs, out_shape=out_shape,
                       name=name, sem=sem, aliases=aliases, ride=ride)
    ours = res[0] if single else res
    return ours if ride is None else (ours, rode)


def _gmm_step(fn, ins, outs, red_axis, init_arg):
    parts = fn(*ins)
    if red_axis is None:
        for o_ref, p in zip(outs, parts):
            o_ref[...] = p.astype(o_ref.dtype)
        return
    k = pl.program_id(red_axis)

    @pl.when(k == 0)
    def _():
        for idx, (o_ref, p) in enumerate(zip(outs, parts)):
            o_ref[...] = p + ins[init_arg][...] if (idx == 0 and init_arg is not None) else p

    @pl.when(k > 0)
    def _():
        for o_ref, p in zip(outs, parts):
            o_ref[...] += p


def _ride_body(ride, grid, n_in, n_out, n_scratch, body):
    n_rin, n_rout = len(ride.arrays), len(ride.out_shape)
    nsteps = math.prod(grid)

    def wrapped(*refs):
        ins = refs[:n_in]
        r_ins = refs[n_in:n_in + n_rin]
        o0 = n_in + n_rin
        outs = refs[o0:o0 + n_out]
        r_outs = refs[o0 + n_out:o0 + n_out + n_rout]
        s0 = o0 + n_out + n_rout
        scratch = refs[s0:s0 + n_scratch]
        send_sems, recv_sems = refs[-2:]
        step = pl.program_id(0)
        for ax in range(1, len(grid)):
            step = step * grid[ax] + pl.program_id(ax)
        ride.emit(step, nsteps, r_ins, r_outs, send_sems, recv_sems, before=True)
        body(*ins, *outs, *scratch)
        ride.emit(step, nsteps, r_ins, r_outs, send_sems, recv_sems, before=False)

    return wrapped


def _pcall(body, args, *, grid, in_specs, out_specs, out_shape, name, sem, scratch=(), aliases=None, ride=None):
    if ride is None:
        res = pl.pallas_call(body, grid=grid, in_specs=list(in_specs), out_specs=list(out_specs),
                             out_shape=list(out_shape), scratch_shapes=list(scratch), name=name,
                             input_output_aliases=aliases or {}, compiler_params=_params(*sem))(*args)
        return res, None
    n_in, n_out = len(args), len(out_shape)
    res = pl.pallas_call(
        _ride_body(ride, grid, n_in, n_out, len(scratch), body), grid=grid,
        in_specs=list(in_specs) + ride.in_specs, out_specs=list(out_specs) + ride.out_specs,
        out_shape=list(out_shape) + ride.out_shape, scratch_shapes=list(scratch) + ride.scratch, name=name,
        input_output_aliases=aliases or {},
        compiler_params=_params(*(("arbitrary",) * len(grid))))(*args, *ride.arrays)
    return res[:n_out], res[n_out:]


def _mm_cols(a, ws, name, ride=None):
    m, k = a.shape
    n = ws.shape[2]
    tm = _fit_rows(m, k * _isz(a) + n * 4, k * n * _isz(ws), 4 * n)
    return _gmm(name, (N_SHARD, m // tm), [a, ws],
                [pl.BlockSpec((tm, k), lambda j, i: (i, 0)), pl.BlockSpec((None, k, n), lambda j, i: (j, 0, 0))],
                pl.BlockSpec((tm, n), lambda j, i: (i, j)), jax.ShapeDtypeStruct((m, N_SHARD * n), F32),
                lambda a_ref, w_ref: (_nn(a_ref[...], w_ref[...]),), ride=ride)


def _mm_cols_t_rms(d, ws, h, w, resid, name, ride=None):
    m = d.shape[0]
    _, k, n = ws.shape
    tm = _fit_rows(m, n * _isz(d) + 3 * k * 4, k * n * _isz(ws), 16 * k)
    return _gmm_rms(name, (m // tm, N_SHARD), [d, ws],
                    [pl.BlockSpec((tm, n), lambda i, j: (i, j)), pl.BlockSpec((None, k, n), lambda i, j: (j, 0, 0))],
                    pl.BlockSpec((tm, k), lambda i, j: (i, 0)),
                    lambda d_ref, w_ref: _nt(d_ref[...], w_ref[...]), h, w, resid, 0, red_axis=1, ride=ride)


def _mm_nt_rms(a, b, h, w, resid, name, ride=None):
    m, n = a.shape
    k = b.shape[0]
    tm = _fit_rows(m, n * _isz(a) + 3 * k * 4, k * n * _isz(b), 16 * k)
    return _gmm_rms(name, (m // tm,), [a, b],
                    [pl.BlockSpec((tm, n), lambda i: (i, 0)), pl.BlockSpec((k, n), lambda i: (0, 0))],
                    pl.BlockSpec((tm, k), lambda i: (i, 0)),
                    lambda a_ref, b_ref: _nt(a_ref[...], b_ref[...]), h, w, resid, 0, ride=ride)


def _mm_cols_grad(a, d, name):
    m, k = a.shape
    n = d.shape[1] // N_SHARD
    tm = _fit_rows(m, k * _isz(a) + n * _isz(d), (3 * k * n * 4) // 2, 2 * (k + n))
    return _gmm(name, (N_SHARD, m // tm), [a, d],
                [pl.BlockSpec((tm, k), lambda j, i: (i, 0)), pl.BlockSpec((tm, n), lambda j, i: (i, j))],
                pl.BlockSpec((None, k, n), lambda j, i: (j, 0, 0)), jax.ShapeDtypeStruct((N_SHARD, k, n), F32),
                lambda a_ref, d_ref: (_tn(a_ref[...], d_ref[...]),), red_axis=1)


def _ffn_up(hn, wg, wu, layer, name):
    m, k = hn.shape
    n = wg.shape[3]
    tm = _fit_rows(m, k * _isz(hn) + 3 * n * jnp.dtype(BF16).itemsize, 2 * k * n * _isz(wg), 16 * n)

    def fn(a_ref, wg_ref, wu_ref):
        a = a_ref[...]
        g = _nn(a, wg_ref[...])
        u = _nn(a, wu_ref[...])
        return g, u, g * jax.nn.sigmoid(g) * u

    w_spec = pl.BlockSpec((None, None, k, n), lambda j, i: (j, layer, 0, 0))
    o_spec = pl.BlockSpec((None, tm, n), lambda j, i: (j, i, 0))
    out = jax.ShapeDtypeStruct((N_SHARD, m, n), BF16)
    return _gmm(name, (N_SHARD, m // tm), [hn, wg, wu],
                [pl.BlockSpec((tm, k), lambda j, i: (i, 0)), w_spec, w_spec],
                [o_spec, o_spec, o_spec], [out, out, out], fn)


def _ffn_down(act, wd, resid, layer, name):
    _, m, n = act.shape
    d = wd.shape[3]
    tm = _fit_rows(m, N_SHARD * n * _isz(act) + 2 * d * 4, N_SHARD * n * d * _isz(wd), 8 * d)

    def fn(a_ref, w_ref, r_ref):
        acc = r_ref[...]
        for j in range(N_SHARD):
            acc = acc + _nn(a_ref[j], w_ref[j])
        return (acc,)

    row = pl.BlockSpec((tm, d), lambda i: (i, 0))
    return _gmm(name, (m // tm,), [act, wd, resid],
                [pl.BlockSpec((N_SHARD, tm, n), lambda i: (0, i, 0)),
                 pl.BlockSpec((N_SHARD, None, n, d), lambda i: (0, layer, 0, 0)), row],
                row, jax.ShapeDtypeStruct((m, d), F32), fn)


def _ffn_down_bwd(dh, wd, g, u, layer, name, ride=None):
    m, d = dh.shape
    n = wd.shape[2]
    tm = _fit_rows(m, d * _isz(dh) + 4 * N_SHARD * n * jnp.dtype(BF16).itemsize, N_SHARD * n * d * _isz(wd),
                   2 * d + 24 * n)

    def body(dh_ref, wd_ref, g_ref, u_ref, dg_ref, du_ref):
        dhv = dh_ref[...].astype(MXU_DTYPE)
        for j in range(N_SHARD):
            dact = _nt(dhv, wd_ref[j])
            gv = g_ref[j].astype(F32)
            sg = jax.nn.sigmoid(gv)
            gs = gv * sg
            dg_ref[j] = (dact * u_ref[j].astype(F32) * (sg + gs * (1.0 - sg))).astype(dg_ref.dtype)
            du_ref[j] = (dact * gs).astype(du_ref.dtype)

    sh_spec = pl.BlockSpec((N_SHARD, tm, n), lambda i: (0, i, 0))
    out = jax.ShapeDtypeStruct((N_SHARD, m, n), BF16)
    res, rode = _pcall(body, [dh, wd, g, u], grid=(m // tm,),
                       in_specs=[pl.BlockSpec((tm, d), lambda i: (i, 0)),
                                 pl.BlockSpec((N_SHARD, None, n, d), lambda i: (0, layer, 0, 0)), sh_spec, sh_spec],
                       out_specs=[sh_spec, sh_spec], out_shape=[out, out], name=name, sem=("parallel",), ride=ride)
    return res if ride is None else (res, rode)


def _ffn_up_bwd(dg, du, wg, wu, layer, h, w, resid, name, ride=None):
    _, m, n = dg.shape
    k = wg.shape[2]
    tm = _fit_rows(m, 2 * N_SHARD * n * _isz(dg) + 3 * k * 4, 2 * N_SHARD * k * n * _isz(wg), 16 * k)

    def fn(dg_ref, du_ref, wg_ref, wu_ref):
        acc = _nt(dg_ref[0], wg_ref[0]) + _nt(du_ref[0], wu_ref[0])
        for j in range(1, N_SHARD):
            acc = acc + _nt(dg_ref[j], wg_ref[j]) + _nt(du_ref[j], wu_ref[j])
        return acc

    d_spec = pl.BlockSpec((N_SHARD, tm, n), lambda i: (0, i, 0))
    w_spec = pl.BlockSpec((N_SHARD, None, k, n), lambda i: (0, layer, 0, 0))
    return _gmm_rms(name, (m // tm,), [dg, du, wg, wu], [d_spec, d_spec, w_spec, w_spec],
                    pl.BlockSpec((tm, k), lambda i: (i, 0)), fn, h, w, resid, 0, ride=ride)


def _ffn_wgrad(lhs, rhs_list, layer, layers, prev, lhs_sharded, name):
    if lhs_sharded:
        _, m, k = lhs.shape
        n = rhs_list[0].shape[1]
    else:
        m, k = lhs.shape
        n = rhs_list[0].shape[2]
    n_out = len(rhs_list)
    tm = _fit_rows(m, k * _isz(lhs) + n_out * n * _isz(rhs_list[0]), (3 * n_out * k * n * 4) // 2,
                   2 * (k + n_out * n))
    sh = pl.BlockSpec((None, tm, k if lhs_sharded else n), lambda j, i: (j, i, 0))
    fl = pl.BlockSpec((tm, n if lhs_sharded else k), lambda j, i: (i, 0))
    n_out = len(rhs_list)
    args = [lhs] + list(rhs_list)
    in_specs = [sh if lhs_sharded else fl] + [fl if lhs_sharded else sh] * n_out
    aliases = None
    if prev is not None:
        aliases = {len(args) + t: t for t in range(n_out)}
        args = args + list(prev)
        in_specs = in_specs + [ANY] * n_out

    def fn(l_ref, *rest):
        lv = l_ref[...]
        return tuple(_tn(lv, r_ref[...]) for r_ref in rest[:n_out])

    o_spec = pl.BlockSpec((None, None, k, n), lambda j, i: (j, layer, 0, 0))
    out = jax.ShapeDtypeStruct((N_SHARD, layers, k, n), F32)
    return _gmm(name, (N_SHARD, m // tm), args, in_specs, [o_spec] * n_out, [out] * n_out, fn,
                red_axis=1, aliases=aliases)


def _ret_consts():
    log_gamma = jnp.log1p(-jnp.exp2(-5.0 - jnp.arange(RET_HEADS, dtype=F32)))
    idx = jnp.arange(CHUNK, dtype=F32)
    rel = idx[:, None] - idx[None, :]
    dmask = jnp.where((rel >= 0)[None], jnp.exp(log_gamma[:, None, None] * jnp.maximum(rel, 0.0)), 0.0)
    xi = jnp.exp(log_gamma[:, None] * (idx[None, :] + 1.0))[:, :, None]
    zeta = jnp.exp(log_gamma[:, None] * (CHUNK - 1.0 - idx[None, :]))[:, :, None]
    gamma_c = jnp.exp(log_gamma * CHUNK)
    wide = (RET_HEADS, CHUNK, RET_DK)
    return dmask, jnp.broadcast_to(xi, wide), jnp.broadcast_to(zeta, wide), gamma_c


def _rope_tables(rows):
    half = RET_DK // 2
    inv_freq = ROPE_BASE ** (-jnp.arange(half, dtype=F32) / half)
    pos = (jnp.arange(rows) - PAD).astype(F32)
    ang = pos[:, None] * inv_freq[None, :]
    return jnp.cos(ang), jnp.sin(ang)


def _ret_specs(order):
    return [pl.BlockSpec((CHUNK, RET_QK), lambda n: (order(n), 0)),
            pl.BlockSpec((CHUNK, RET_QK), lambda n: (order(n), 1)),
            pl.BlockSpec((CHUNK, RET_V), lambda n: (order(n), 1)),
            pl.BlockSpec((CHUNK, RET_V), lambda n: (order(n), 2))]


def _ret_const_specs():
    return [pl.BlockSpec((RET_HEADS, CHUNK, CHUNK), lambda n: (0, 0, 0)),
            pl.BlockSpec((RET_HEADS, CHUNK, RET_DK), lambda n: (0, 0, 0)),
            pl.BlockSpec((RET_HEADS, CHUNK, RET_DK), lambda n: (0, 0, 0)),
            pl.BlockSpec((1, RET_DV), lambda n: (0, 0))]


def _ret_fwd(proj, cos, sin, consts, gn_w, seq, ride=None):
    rows = proj.shape[0]
    nc = rows // CHUNK
    dmask, xi, zeta, gamma_c = consts

    def body(gam_ref, q_ref, k_ref, v_ref, g_ref, cos_ref, sin_ref, dm_ref, xi_ref, ze_ref, gn_ref,
             o_ref, y_ref, ss_ref, s_ref):
        n = pl.program_id(0)

        @pl.when(n == 0)
        def _():
            s_ref[...] = jnp.zeros_like(s_ref)

        cs, sn = cos_ref[...], sin_ref[...]
        kscale = _valid_rows(n * CHUNK, CHUNK, seq) * (RET_DK ** -0.5)
        gn = gn_ref[...]
        hs = range(RET_HEADS)
        qk_cols = [slice(h * RET_DK, (h + 1) * RET_DK) for h in hs]
        v_cols = [slice(h * RET_DV, (h + 1) * RET_DV) for h in hs]
        qr_l = [_rope(q_ref[:, c], cs, sn) for c in qk_cols]
        kr_l = [_rope(k_ref[:, c], cs, sn) * kscale for c in qk_cols]
        v_l = [v_ref[:, c] for c in v_cols]
        s_l = [s_ref[h] for h in hs]
        sc_l = [_nt(qr, kr) * dm_ref[h] for h, (qr, kr) in enumerate(zip(qr_l, kr_l))]
        o_l = [_nn(sc_l[h], v_l[h]) + _nn(qr_l[h] * xi_ref[h], s_l[h]) for h in hs]
        for h in hs:
            ss_ref[0, h] = s_l[h].astype(ss_ref.dtype)
            s_ref[h] = gam_ref[h] * s_l[h] + _tn(kr_l[h] * ze_ref[h], v_l[h])
            o_ref[:, v_cols[h]] = o_l[h]
            y_ref[:, v_cols[h]] = _gated_norm(o_l[h], g_ref[:, v_cols[h]], gn).astype(y_ref.dtype)

    fwd = lambda n: n
    row128 = pl.BlockSpec((CHUNK, RET_DK // 2), lambda n: (n, 0))
    row_v = pl.BlockSpec((CHUNK, RET_V), lambda n: (n, 0))
    res, rode = _pcall(
        body, [gamma_c, proj, proj, proj, proj, cos, sin, dmask, xi, zeta, gn_w.reshape(1, RET_DV)],
        grid=(nc,),
        in_specs=[pl.BlockSpec(memory_space=pltpu.SMEM)] + _ret_specs(fwd) + [row128, row128]
        + _ret_const_specs(),
        out_specs=[row_v, row_v,
                   pl.BlockSpec((1, RET_HEADS, RET_DK, RET_DV), lambda n: (n, 0, 0, 0))],
        out_shape=[jax.ShapeDtypeStruct((rows, RET_V), F32), jax.ShapeDtypeStruct((rows, RET_V), BF16),
                   jax.ShapeDtypeStruct((nc, RET_HEADS, RET_DK, RET_DV), BF16)],
        scratch=[pltpu.VMEM((RET_HEADS, RET_DK, RET_DV), F32)], name="ret_fwd", sem=("arbitrary",), ride=ride)
    return res if ride is None else (res, rode)


def _ret_bwd(proj, o, dy, states, cos, sin, consts, gn_w, seq, ride=None):
    rows = proj.shape[0]
    nc = rows // CHUNK
    dmask, xi, zeta, gamma_c = consts

    def body(gam_ref, q_ref, k_ref, v_ref, g_ref, o_ref, dy_ref, ss_ref, cos_ref, sin_ref,
             dm_ref, xi_ref, ze_ref, gn_ref, dp_ref, dgn_ref, ds_ref):
        n = pl.program_id(0)

        @pl.when(n == 0)
        def _():
            ds_ref[...] = jnp.zeros_like(ds_ref)
            dgn_ref[...] = jnp.zeros_like(dgn_ref)

        cs, sn = cos_ref[...], sin_ref[...]
        kscale = _valid_rows((nc - 1 - n) * CHUNK, CHUNK, seq) * (RET_DK ** -0.5)
        gn = gn_ref[...]
        dgn = jnp.zeros((1, RET_DV), F32)
        hs = range(RET_HEADS)
        qk_cols = [slice(h * RET_DK, (h + 1) * RET_DK) for h in hs]
        v_cols = [slice(h * RET_DV, (h + 1) * RET_DV) for h in hs]
        qr_l = [_rope(q_ref[:, c], cs, sn) for c in qk_cols]
        kr_l = [_rope(k_ref[:, c], cs, sn) * kscale for c in qk_cols]
        v_l = [v_ref[:, c] for c in v_cols]
        s_l = [ss_ref[0, h] for h in hs]
        ds_l = [ds_ref[h] for h in hs]
        gnb = [_gated_norm_bwd(dy_ref[:, c], o_ref[:, c], g_ref[:, c], gn) for c in v_cols]
        do_l = [x[0] for x in gnb]
        sc_l = [_nt(qr_l[h], kr_l[h]) * dm_ref[h] for h in hs]
        dsc_l = [_nt(do_l[h], v_l[h]) * dm_ref[h] for h in hs]
        dv_l = [_tn(sc_l[h], do_l[h]) + _nn(kr_l[h] * ze_ref[h], ds_l[h]) for h in hs]
        dqr_l = [_nn(dsc_l[h], kr_l[h]) + _nt(do_l[h], s_l[h]) * xi_ref[h] for h in hs]
        dkr_l = [_tn(dsc_l[h], qr_l[h]) + _nt(v_l[h], ds_l[h]) * ze_ref[h] for h in hs]
        for h in hs:
            dgn = dgn + gnb[h][2]
            ds_ref[h] = gam_ref[h] * ds_l[h] + _tn(qr_l[h] * xi_ref[h], do_l[h])
            dp_ref[:, qk_cols[h]] = _rope_bwd(dqr_l[h], cs, sn).astype(dp_ref.dtype)
            dp_ref[:, RET_QK + h * RET_DK:RET_QK + (h + 1) * RET_DK] = (
                _rope_bwd(dkr_l[h] * kscale, cs, sn).astype(dp_ref.dtype))
            dp_ref[:, 2 * RET_QK + h * RET_DV:2 * RET_QK + (h + 1) * RET_DV] = dv_l[h].astype(dp_ref.dtype)
            dp_ref[:, 2 * RET_QK + RET_V + h * RET_DV:2 * RET_QK + RET_V + (h + 1) * RET_DV] = (
                gnb[h][1].astype(dp_ref.dtype))
        dgn_ref[...] += dgn

    rev = lambda n: nc - 1 - n
    row128 = pl.BlockSpec((CHUNK, RET_DK // 2), lambda n: (rev(n), 0))
    row_v = pl.BlockSpec((CHUNK, RET_V), lambda n: (rev(n), 0))
    res, rode = _pcall(
        body, [gamma_c, proj, proj, proj, proj, o, dy, states, cos, sin, dmask, xi, zeta,
               gn_w.reshape(1, RET_DV)],
        grid=(nc,),
        in_specs=[pl.BlockSpec(memory_space=pltpu.SMEM)] + _ret_specs(rev) + [
            row_v, row_v, pl.BlockSpec((1, RET_HEADS, RET_DK, RET_DV), lambda n: (rev(n), 0, 0, 0)),
            row128, row128] + _ret_const_specs(),
        out_specs=[pl.BlockSpec((CHUNK, RET_IN), lambda n: (rev(n), 0)),
                   pl.BlockSpec((1, RET_DV), lambda n: (0, 0))],
        out_shape=[jax.ShapeDtypeStruct((rows, RET_IN), BF16), jax.ShapeDtypeStruct((1, RET_DV), F32)],
        scratch=[pltpu.VMEM((RET_HEADS, RET_DK, RET_DV), F32)], name="ret_bwd", sem=("arbitrary",), ride=ride)
    return res if ride is None else (res, rode)


GATE_COL = DN_CONV_CH // DN_V
BA_COL = (DN_CONV_CH + DN_V) // LANES
BETA_LANE, DECAY_LANE = 0, DN_HEADS
INV_SHIFT = 4
INV_SQUARINGS = INV_SHIFT - 1
assert CHUNK == 4 << INV_SHIFT


def _dn_in_specs(order, conv_saved=False):
    return [pl.BlockSpec((CHUNK, DN_CONV_CH), lambda n: (order(n), 0)),
            pl.BlockSpec((CHUNK, DN_CONV_CH), lambda n: (order(n), 0)) if conv_saved else
            pl.BlockSpec((8, DN_CONV_CH), lambda n: (jnp.maximum(order(n) * (CHUNK // 8) - 1, 0), 0)),
            pl.BlockSpec((CHUNK, DN_V), lambda n: (order(n), GATE_COL)),
            pl.BlockSpec((CHUNK, LANES), lambda n: (order(n), BA_COL)),
            pl.BlockSpec((CONV_K, 1, DN_CONV_CH), lambda n: (0, 0, 0)),
            pl.BlockSpec((1, LANES), lambda n: (0, 0)),
            pl.BlockSpec((1, LANES), lambda n: (0, 0)),
            pl.BlockSpec((1, DN_DV), lambda n: (0, 0))]


def _dn_front(c, seq, x_ref, halo_ref, ba_ref, cw_ref, al_ref, dt_ref, yc_ref=None):
    valid = _valid_rows(c * CHUNK, CHUNK, seq)
    xin = x_ref[...] * valid
    if yc_ref is None:
        halo = halo_ref[...] * _valid_rows(c * CHUNK - 8, 8, seq)
        yc = xin * cw_ref[CONV_K - 1]
        for k in range(1, CONV_K):
            yc = yc + _shift_down(xin, halo, k) * cw_ref[CONV_K - 1 - k]
    else:
        yc = yc_ref[...]
    sgc = jax.nn.sigmoid(yc)
    ba = ba_ref[...]
    sig = jax.nn.sigmoid(ba)
    beta = sig * valid
    z = ba + dt_ref[...]
    eal = jnp.exp(al_ref[...])
    g = -eal * _softplus(z) * valid
    ri, ci = _iota((CHUNK, CHUNK), 0), _iota((CHUNK, CHUNK), 1)
    lower = (ri >= ci).astype(F32)
    upper = (ri <= ci).astype(F32)
    eye = (ri == ci).astype(F32)
    gam = _nn(lower, g, hi=True)
    gam_t = _tn(g, upper, hi=True)
    return dict(valid=valid, xin=xin, yc=yc, sgc=sgc, act=yc * sgc, sig=sig, beta=beta, z=z,
                eal=eal, g=g, gam=gam, gam_t=gam_t, ri=ri, ci=ci, upper=upper, eye=eye)


def _dn_head(f, h):
    act = f["act"]
    q_raw = act[:, h * DN_DK:(h + 1) * DN_DK]
    k_raw = act[:, DN_QK + h * DN_DK:DN_QK + (h + 1) * DN_DK]
    v = act[:, 2 * DN_QK + h * DN_DV:2 * DN_QK + (h + 1) * DN_DV]
    rq = lax.rsqrt(jnp.sum(q_raw * q_raw, axis=-1, keepdims=True) + RMS_EPS)
    rk = lax.rsqrt(jnp.sum(k_raw * k_raw, axis=-1, keepdims=True) + RMS_EPS)
    qh = q_raw * rq
    kn = k_raw * rk
    gam_c = _col(f["gam"], DECAY_LANE + h)
    gam_r = _row(f["gam_t"], DECAY_LANE + h)
    bc = _col(f["beta"], BETA_LANE + h)
    diff = gam_c - gam_r
    decay = jnp.where(f["ri"] >= f["ci"], jnp.exp(jnp.minimum(diff, 0.0)), 0.0)
    glast = jnp.sum(gam_r * (_iota((1, CHUNK), 1) == CHUNK - 1).astype(F32), axis=1, keepdims=True)
    return dict(rq=rq, rk=rk, qh=qh, qn=qh * (DN_DK ** -0.5), kn=kn, v=v, gam_c=gam_c, gam_r=gam_r,
                bc=bc, diff=diff, decay=decay, egam=jnp.exp(gam_c), glast=glast,
                eglast=jnp.exp(glast), ekd=jnp.exp(glast - gam_c))


def _dn_fwd(proj, conv_w, alog, dtb, norm_w, seq):
    rows = proj.shape[0]
    nc = rows // CHUNK

    def body(x_ref, halo_ref, gate_ref, ba_ref, cw_ref, al_ref, dt_ref, nw_ref,
             o_ref, y_ref, ss_ref, t_ref, yc_ref, s_ref):
        n = pl.program_id(0)

        @pl.when(n == 0)
        def _():
            s_ref[...] = jnp.zeros_like(s_ref)

        f = _dn_front(n, seq, x_ref, halo_ref, ba_ref, cw_ref, al_ref, dt_ref)
        yc_ref[...] = f["yc"]
        ri, ci = f["ri"], f["ci"]
        eye = f["eye"]
        diag_m = (jnp.right_shift(ri, INV_SHIFT) == jnp.right_shift(ci, INV_SHIFT)).astype(F32)
        half_m = (jnp.right_shift(ri, INV_SHIFT + 1) == jnp.right_shift(ci, INV_SHIFT + 1)).astype(F32)
        nw = nw_ref[...]
        heads = [_dn_head(f, h) for h in range(DN_HEADS)]
        a_all = [jnp.where(ri > ci, hd["bc"] * _nt(hd["kn"], hd["kn"]) * hd["decay"], 0.0) for hd in heads]
        b_all = [a * diag_m for a in a_all]
        t_all = [eye - b for b in b_all]
        for _ in range(INV_SQUARINGS):
            b_all = [_nn(b, b, hi=True) for b in b_all]
            t_all = [t + _nn(t, b, hi=True) for t, b in zip(t_all, b_all)]
        for off_m in (half_m - diag_m, 1.0 - half_m):
            x_all = [_nn(a * off_m, t, hi=True) for a, t in zip(a_all, t_all)]
            t_all = [t - _nn(t, x, hi=True) for t, x in zip(t_all, x_all)]
        u_all = [_nn(t, hd["v"] * hd["bc"], hi=True) for t, hd in zip(t_all, heads)]
        w_all = [_nn(t, hd["kn"] * (hd["bc"] * hd["egam"]), hi=True) for t, hd in zip(t_all, heads)]
        for h in range(DN_HEADS):
            hd = heads[h]
            v_cols = slice(h * DN_DV, (h + 1) * DN_DV)
            t_ref[0, h] = t_all[h]
            s = s_ref[h]
            ss_ref[0, h] = s
            u, w = u_all[h], w_all[h]
            v_new = u - _nn(w, s)
            qk = _nt(hd["qn"], hd["kn"]) * hd["decay"]
            o = _nn(hd["qn"] * hd["egam"], s) + _nn(qk, v_new)
            s_ref[h] = s * hd["eglast"] + _tn(hd["kn"] * hd["ekd"], v_new)
            o_ref[:, v_cols] = o
            y_ref[:, v_cols] = _gated_norm(o, gate_ref[:, v_cols], nw).astype(y_ref.dtype)

    fwd = lambda n: n
    row_v = pl.BlockSpec((CHUNK, DN_V), lambda n: (n, 0))
    return pl.pallas_call(
        body, grid=(nc,), in_specs=_dn_in_specs(fwd),
        out_specs=[row_v, row_v,
                   pl.BlockSpec((1, DN_HEADS, DN_DK, DN_DV), lambda n: (n, 0, 0, 0)),
                   pl.BlockSpec((1, DN_HEADS, CHUNK, CHUNK), lambda n: (n, 0, 0, 0)),
                   pl.BlockSpec((CHUNK, DN_CONV_CH), lambda n: (n, 0))],
        out_shape=[jax.ShapeDtypeStruct((rows, DN_V), F32), jax.ShapeDtypeStruct((rows, DN_V), BF16),
                   jax.ShapeDtypeStruct((nc, DN_HEADS, DN_DK, DN_DV), F32),
                   jax.ShapeDtypeStruct((nc, DN_HEADS, CHUNK, CHUNK), F32),
                   jax.ShapeDtypeStruct((rows, DN_CONV_CH), F32)],
        scratch_shapes=[pltpu.VMEM((DN_HEADS, DN_DK, DN_DV), F32)],
        name="dn_fwd", compiler_params=_params("arbitrary"))(
            proj, proj, proj, proj, conv_w, alog, dtb, norm_w.reshape(1, DN_DV))


def _dn_bwd(proj, conv_out, o, dy, states, tinv, conv_w, alog, dtb, norm_w, seq):
    rows = proj.shape[0]
    nc = rows // CHUNK

    def body(x_ref, yc_ref, gate_ref, ba_ref, cw_ref, al_ref, dt_ref, nw_ref,
             o_ref, dy_ref, ss_ref, t_ref,
             dp_ref, dcw_ref, dal_ref, ddt_ref, dnw_ref, ds_ref, nxt_ref):
        n = pl.program_id(0)

        @pl.when(n == 0)
        def _():
            ds_ref[...] = jnp.zeros_like(ds_ref)
            nxt_ref[...] = jnp.zeros_like(nxt_ref)
            dcw_ref[...] = jnp.zeros_like(dcw_ref)
            dal_ref[...] = jnp.zeros_like(dal_ref)
            ddt_ref[...] = jnp.zeros_like(ddt_ref)
            dnw_ref[...] = jnp.zeros_like(dnw_ref)

        f = _dn_front(nc - 1 - n, seq, x_ref, None, ba_ref, cw_ref, al_ref, dt_ref, yc_ref)
        ri, ci = f["ri"], f["ci"]
        strict = (ri > ci).astype(F32)
        nw = nw_ref[...]
        lane128 = _iota((1, LANES), 1)
        row128 = _iota((LANES, 1), 0)
        dgam_col = jnp.zeros((CHUNK, LANES), F32)
        dgam_row = jnp.zeros((LANES, CHUNK), F32)
        dbeta = jnp.zeros((CHUNK, LANES), F32)
        dnw = jnp.zeros((1, DN_DV), F32)
        hs = range(DN_HEADS)
        heads = [_dn_head(f, h) for h in hs]
        cols = [slice(h * DN_DV, (h + 1) * DN_DV) for h in hs]
        t_l = [t_ref[0, h] for h in hs]
        s_l = [ss_ref[0, h] for h in hs]
        ds_l = [ds_ref[h] for h in hs]
        kk_l = [_nt(hd["kn"], hd["kn"]) for hd in heads]
        p_l = [_nt(hd["qn"], hd["kn"]) for hd in heads]
        rhsw_l = [hd["kn"] * (hd["bc"] * hd["egam"]) for hd in heads]
        u_l = [_nn(t, hd["v"] * hd["bc"], hi=True) for t, hd in zip(t_l, heads)]
        w_l = [_nn(t, r, hi=True) for t, r in zip(t_l, rhsw_l)]
        vnew_l = [u - _nn(w, s) for u, w, s in zip(u_l, w_l, s_l)]
        gnb = [_gated_norm_bwd(dy_ref[:, c], o_ref[:, c], gate_ref[:, c], nw) for c in cols]
        do_l = [x[0] for x in gnb]
        for h in hs:
            dp_ref[:, DN_CONV_CH + h * DN_DV:DN_CONV_CH + (h + 1) * DN_DV] = gnb[h][1].astype(dp_ref.dtype)
            dnw = dnw + gnb[h][2]
        qg_l = [hd["qn"] * hd["egam"] for hd in heads]
        kd_l = [hd["kn"] * hd["ekd"] for hd in heads]
        dvnew_l = [_tn(p * hd["decay"], do) + _nn(kd, ds)
                   for p, hd, do, kd, ds in zip(p_l, heads, do_l, kd_l, ds_l)]
        m_l = [_nt(do, vn) for do, vn in zip(do_l, vnew_l)]
        dqg_l = [_nt(do, s) for do, s in zip(do_l, s_l)]
        dkd_l = [_nt(vn, ds) for vn, ds in zip(vnew_l, ds_l)]
        for h in hs:
            ds_ref[h] = (ds_l[h] * heads[h]["eglast"] + _tn(qg_l[h], do_l[h]) - _tn(w_l[h], dvnew_l[h]))
        dw_l = [-_nt(dvn, s) for dvn, s in zip(dvnew_l, s_l)]
        dru_l = [_tn(t, dvn, hi=True) for t, dvn in zip(t_l, dvnew_l)]
        drw_l = [_tn(t, dw_, hi=True) for t, dw_ in zip(t_l, dw_l)]
        da_l = [-(_nt(dru, u) + _nt(drw, w)) * strict for dru, u, drw, w in zip(dru_l, u_l, drw_l, w_l)]
        dp_l = [m * hd["decay"] for m, hd in zip(m_l, heads)]
        dkk_l = [da * (hd["bc"] * hd["decay"]) for da, hd in zip(da_l, heads)]
        dqn_l = [dqg * hd["egam"] + _nn(dp, hd["kn"]) for dqg, hd, dp in zip(dqg_l, heads, dp_l)]
        dkn_l = [_tn(dp, hd["qn"]) + dkd * hd["ekd"] + drw * (hd["bc"] * hd["egam"])
                 + _nn(dkk, hd["kn"]) + _tn(dkk, hd["kn"])
                 for dp, hd, dkd, drw, dkk in zip(dp_l, heads, dkd_l, drw_l, dkk_l)]
        dq_parts, dk_parts, dv_parts = [], [], []
        for h in hs:
            hd = heads[h]
            kn, v, bc, egam, decay = hd["kn"], hd["v"], hd["bc"], hd["egam"], hd["decay"]
            t1 = jnp.sum(dkd_l[h] * kd_l[h], axis=1, keepdims=True)
            dglast = (jnp.sum(t1, axis=0, keepdims=True)
                      + jnp.sum(jnp.sum(ds_l[h] * s_l[h], axis=1, keepdims=True), axis=0, keepdims=True)
                      * hd["eglast"])
            e = (m_l[h] * p_l[h] + da_l[h] * (bc * kk_l[h])) * decay
            dgc = (jnp.sum(dqg_l[h] * qg_l[h], axis=1, keepdims=True) - t1
                   + jnp.sum(drw_l[h] * rhsw_l[h], axis=1, keepdims=True)
                   + jnp.sum(e, axis=1, keepdims=True)
                   + jnp.where(_iota((CHUNK, 1), 0) == CHUNK - 1, dglast, 0.0))
            dgr = -jnp.sum(e, axis=0, keepdims=True)
            dbc = (jnp.sum(dru_l[h] * v, axis=1, keepdims=True)
                   + jnp.sum(drw_l[h] * kn, axis=1, keepdims=True) * egam
                   + jnp.sum(da_l[h] * kk_l[h] * decay, axis=1, keepdims=True))
            dv_parts.append(dru_l[h] * bc)
            qh, dqn, dkn = hd["qh"], dqn_l[h], dkn_l[h]
            dq_parts.append(((DN_DK ** -0.5) * hd["rq"])
                            * (dqn - qh * jnp.sum(dqn * qh, axis=1, keepdims=True)))
            dk_parts.append(hd["rk"] * (dkn - kn * jnp.sum(dkn * kn, axis=1, keepdims=True)))
            dgam_col = dgam_col + dgc * (lane128 == DECAY_LANE + h).astype(F32)
            dbeta = dbeta + dbc * (lane128 == BETA_LANE + h).astype(F32)
            dgam_row = dgam_row + (row128 == DECAY_LANE + h).astype(F32) * dgr
        dnw_ref[...] += dnw
        dgam = dgam_col + _nt(f["eye"], dgam_row, hi=True)
        dg = _nn(f["upper"], dgam, hi=True)
        d_a = dg * (-f["eal"]) * jax.nn.sigmoid(f["z"]) * f["valid"]
        dal_ref[...] += jnp.sum(dg * f["g"], axis=0, keepdims=True)
        ddt_ref[...] += jnp.sum(d_a, axis=0, keepdims=True)
        d_b = dbeta * f["valid"] * f["sig"] * (1.0 - f["sig"])
        dp_ref[:, DN_CONV_CH + DN_V:DN_CONV_CH + DN_V + LANES] = (d_a + d_b).astype(dp_ref.dtype)
        dp_ref[:, DN_CONV_CH + DN_V + LANES:] = jnp.zeros((CHUNK, DN_IN_PAD - DN_IN_USED), dp_ref.dtype)
        dact = jnp.concatenate(dq_parts + dk_parts + dv_parts, axis=1)
        yc, sgc = f["yc"], f["sgc"]
        dyc = dact * (sgc * (1.0 + yc * (1.0 - sgc)))
        nxt = nxt_ref[...]
        ups = [dyc] + [_shift_up(dyc, nxt, j) for j in range(1, CONV_K)]
        dx = ups[0] * cw_ref[CONV_K - 1]
        for j in range(1, CONV_K):
            dx = dx + ups[j] * cw_ref[CONV_K - 1 - j]
        for j in range(CONV_K):
            dcw_ref[CONV_K - 1 - j] += jnp.sum(f["xin"] * ups[j], axis=0, keepdims=True)
        nxt_ref[...] = dyc[0:8]
        dp_ref[:, :DN_CONV_CH] = (dx * f["valid"]).astype(dp_ref.dtype)

    rev = lambda n: nc - 1 - n
    row_v = pl.BlockSpec((CHUNK, DN_V), lambda n: (rev(n), 0))
    vec = pl.BlockSpec((1, LANES), lambda n: (0, 0))
    return pl.pallas_call(
        body, grid=(nc,),
        in_specs=_dn_in_specs(rev, conv_saved=True) + [
            row_v, row_v,
            pl.BlockSpec((1, DN_HEADS, DN_DK, DN_DV), lambda n: (rev(n), 0, 0, 0)),
            pl.BlockSpec((1, DN_HEADS, CHUNK, CHUNK), lambda n: (rev(n), 0, 0, 0))],
        out_specs=[pl.BlockSpec((CHUNK, DN_IN_PAD), lambda n: (rev(n), 0)),
                   pl.BlockSpec((CONV_K, 1, DN_CONV_CH), lambda n: (0, 0, 0)), vec, vec,
                   pl.BlockSpec((1, DN_DV), lambda n: (0, 0))],
        out_shape=[jax.ShapeDtypeStruct((rows, DN_IN_PAD), BF16),
                   jax.ShapeDtypeStruct((CONV_K, 1, DN_CONV_CH), F32),
                   jax.ShapeDtypeStruct((1, LANES), F32), jax.ShapeDtypeStruct((1, LANES), F32),
                   jax.ShapeDtypeStruct((1, DN_DV), F32)],
        scratch_shapes=[pltpu.VMEM((DN_HEADS, DN_DK, DN_DV), F32), pltpu.VMEM((8, DN_CONV_CH), F32)],
        name="dn_bwd", compiler_params=_params("arbitrary"))(
            proj, conv_out, proj, proj, conv_w, alog, dtb, norm_w.reshape(1, DN_DV), o, dy, states, tinv)


def _train_step(x, tgt, wts, sh, idx):
    seq = x.shape[0]
    rows = -(-(seq + CHUNK) // ROW_ALIGN) * ROW_ALIGN
    tail = rows - seq - CHUNK
    h0 = jnp.concatenate([jnp.zeros((PAD, D_MODEL), F32), wts["meta_tokens"].astype(F32), x,
                          jnp.zeros((tail, D_MODEL), F32)], axis=0)
    tgt_p = jnp.concatenate([jnp.zeros((CHUNK, D_MODEL), F32), tgt, jnp.zeros((tail, D_MODEL), F32)],
                            axis=0)
    cos, sin = _rope_tables(rows)
    consts = _ret_consts()
    conv_w = wts["dn_conv_w"].reshape(CONV_K, 1, DN_CONV_CH)
    lane_pad = LANES - 2 * DN_HEADS
    alog = jnp.pad(wts["dn_a_log"].reshape(1, DN_HEADS), ((0, 0), (DECAY_LANE, lane_pad)))
    dtb = jnp.pad(wts["dn_dt_bias"].reshape(1, DN_HEADS), ((0, 0), (DECAY_LANE, lane_pad)))
    g = {}

    wts = dict(wts)
    hn0, (got,) = _rms_fwd(h0, wts["mix_norm_w"][0], "rms_mix0", ride=_Ride("gather", [sh["ret_w_in"]]))
    wts["ret_w_in"] = got.reshape(N_SHARD, D_MODEL, -1)
    proj0, got = _mm_cols(hn0, wts["ret_w_in"], "ret_in",
                          ride=_Ride("gather", [sh["ret_w_out"], sh["ffn_w_gate"], sh["dn_w_out"]]))
    wts["ret_w_out"] = got[0].reshape(-1, D_MODEL)
    wts["ffn_w_gate"] = got[1]
    wts["dn_w_out"] = got[2].reshape(-1, D_MODEL)
    (o0, y0, st0), got = _ret_fwd(proj0, cos, sin, consts, wts["ret_gn_w"], seq,
                                  ride=_Ride("gather", [sh["ffn_w_up"], sh["ffn_w_down"], sh["dn_w_in"]]))
    wts["ffn_w_up"], wts["ffn_w_down"] = got[0], got[1]
    n_dn = sh["dn_w_in"].shape[-1]
    dn_shards = got[2].reshape(N_SHARD, D_MODEL, n_dn)
    wts["dn_w_in"] = jnp.concatenate(
        [dn_shards[j] for j in range(N_SHARD)]
        + [jnp.zeros((D_MODEL, DN_IN_PAD - N_SHARD * n_dn), dn_shards.dtype)], axis=-1)
    h1 = _mm(y0, wts["ret_w_out"], mode="nn", name="ret_out", resid=h0)
    hn1 = _rms_fwd(h1, wts["ffn_norm_w"][0], "rms_ffn0")
    g0, u0, act0 = _ffn_up(hn1, wts["ffn_w_gate"], wts["ffn_w_up"], 0, "ffn_up0")
    h2 = _ffn_down(act0, wts["ffn_w_down"], h1, 0, "ffn_down0")
    hn2 = _rms_fwd(h2, wts["mix_norm_w"][1], "rms_mix1")
    proj1 = _mm(hn2, wts["dn_w_in"], mode="nn", name="dn_in")
    o1, y1, st1, tinv, conv1 = _dn_fwd(proj1, conv_w, alog, dtb, wts["dn_norm_w"], seq)
    h3 = _mm(y1, wts["dn_w_out"], mode="nn", name="dn_out", resid=h2)
    hn3 = _rms_fwd(h3, wts["ffn_norm_w"][1], "rms_ffn1")
    g1, u1, act1 = _ffn_up(hn3, wts["ffn_w_gate"], wts["ffn_w_up"], 1, "ffn_up1")
    h4 = _ffn_down(act1, wts["ffn_w_down"], h3, 1, "ffn_down1")

    dh4, g["final_norm_w"], loss = _final_loss(h4, wts["final_norm_w"], tgt_p, seq, "final_loss")

    layers = wts["ffn_w_gate"].shape[1]

    ffn_names = ["ffn_w_down", "ffn_w_gate", "ffn_w_up"]

    def ffn_bwd(dh_out, h_mid, hn, gg, uu, act, layer, prev, ride=None, last=False):
        tag = str(layer)
        res = _ffn_down_bwd(dh_out, wts["ffn_w_down"], gg, uu, layer, "ffn_down_bwd" + tag, ride=ride)
        (dg, du), rode = res if ride is not None else (res, None)
        d_down = _ffn_wgrad(act, [dh_out], layer, layers, prev and prev[:1], True, "ffn_dwd" + tag)
        d_gu = _ffn_wgrad(hn, [dg, du], layer, layers, prev and prev[1:], False, "ffn_dwgu" + tag)
        grads = list(d_down) + list(d_gu)
        gs = rs_grads(ffn_names, grads) if last else None
        res = _ffn_up_bwd(dg, du, wts["ffn_w_gate"], wts["ffn_w_up"], layer, h_mid, wts["ffn_norm_w"][layer],
                          dh_out, "ffn_up_bwd" + tag, ride=_Ride("pair", gs) if last else None)
        (dh_mid, d_norm), sib = res if last else (res, None)
        return dh_mid, grads, d_norm, rode, gs, sib

    red = {}

    def rs_grads(names, grads):
        return [gr.reshape((N_SHARD,) + sh[n].shape) for n, gr in zip(names, grads)]

    def rs_partials(names, gs, sib):
        return [_rs_pair_add(gs[t], sib[t], idx, "rs_pair_add_" + n) for t, n in enumerate(names)]

    def rs_end(names, gs, sib, others, tag):
        mine = [_rs_final_add(gs[t], sib[t], others[t], idx, "rs_final_add_" + n) for t, n in enumerate(names)]
        red.update(zip(names, _rs_share(mine, "rs_share" + tag)))

    dh3, ffn_grads, dfn1 = ffn_bwd(dh4, h3, hn3, g1, u1, act1, 1, None)[:3]
    dy1 = _mm(dh3, wts["dn_w_out"], mode="nt", name="dn_out_bwd")
    d_dn_out = _mm(y1, dh3, mode="tn", name="dn_dwo")
    dproj1, dcw, dal, ddt, g["dn_norm_w"] = _dn_bwd(proj1, conv1, o1, dy1, st1, tinv, conv_w, alog, dtb,
                                                    wts["dn_norm_w"], seq)
    d_dn_in = _mm(hn2, dproj1, mode="tn", name="dn_dwi")
    d_dn_in = jnp.stack([d_dn_in[:, j * n_dn:(j + 1) * n_dn] for j in range(N_SHARD)])
    group1 = ["dn_w_out", "dn_w_in"]
    gs1 = rs_grads(group1, [d_dn_out, d_dn_in])
    (dh2, dmn1), sib1 = _mm_nt_rms(dproj1, wts["dn_w_in"], h2, wts["mix_norm_w"][1], dh3, "dn_in_bwd",
                                   ride=_Ride("pair", gs1))
    g["dn_conv_w"] = dcw.reshape(CONV_K, DN_CONV_CH)
    g["dn_a_log"] = dal[0, DECAY_LANE:DECAY_LANE + DN_HEADS]
    g["dn_dt_bias"] = ddt[0, DECAY_LANE:DECAY_LANE + DN_HEADS]

    dh1, _, dfn0, others1, gs2, sib2 = ffn_bwd(dh2, h1, hn1, g0, u0, act0, 0, ffn_grads,
                                               ride=_Ride("chips", rs_partials(group1, gs1, sib1)), last=True)
    rs_end(group1, gs1, sib1, others1, "1")
    d_ret_out = _mm(y0, dh1, mode="tn", name="ret_dwo")
    gs2b = rs_grads(["ret_w_out"], [d_ret_out])
    dy0, sib2b = _mm(dh1, wts["ret_w_out"], mode="nt", name="ret_out_bwd", ride=_Ride("pair", gs2b))
    group2 = ffn_names + ["ret_w_out"]
    gs2, sib2 = gs2 + gs2b, list(sib2) + list(sib2b)
    (dproj0, g["ret_gn_w"]), others2 = _ret_bwd(proj0, o0, dy0, st0, cos, sin, consts, wts["ret_gn_w"], seq,
                                                ride=_Ride("chips", rs_partials(group2, gs2, sib2)))
    rs_end(group2, gs2, sib2, others2, "2")
    d_ret_in = _mm_cols_grad(hn0, dproj0, "ret_dwi")
    gs3 = rs_grads(["ret_w_in"], [d_ret_in])
    sib3 = _rs_pair(gs3, "rs_pair3")
    (dh0, dmn0), others3 = _mm_cols_t_rms(dproj0, wts["ret_w_in"], h0, wts["mix_norm_w"][0], dh1, "ret_in_bwd",
                                          ride=_Ride("chips", rs_partials(["ret_w_in"], gs3, sib3)))
    rs_end(["ret_w_in"], gs3, sib3, others3, "3")

    g["ffn_norm_w"] = jnp.concatenate([dfn0, dfn1], axis=0)
    g["mix_norm_w"] = jnp.concatenate([dmn0, dmn1], axis=0)
    g["meta_tokens"] = dh0[PAD:CHUNK]
    g["final_norm_w"] = g["final_norm_w"].reshape(D_MODEL)
    g["ret_gn_w"] = g["ret_gn_w"].reshape(RET_DV)
    g["dn_norm_w"] = g["dn_norm_w"].reshape(DN_DV)
    return loss, dh0, g, red


def _mesh_pos():
    return lax.axis_index("x"), lax.axis_index("y"), lax.axis_index("c")


def _other_chips(x, y):
    return [(1 - x, y), (x, 1 - y), (1 - x, 1 - y)]


def _remote(src, dst, send_sem, recv_sem, to):
    return pltpu.make_async_remote_copy(src_ref=src, dst_ref=dst, send_sem=send_sem, recv_sem=recv_sem,
                                        device_id=to, device_id_type=MESH)


GATHER_COPIES = 7


def _gather_phase(phase, ins, outs, send_sems, recv_sems):
    x, y, c = _mesh_pos()
    me = 2 * x + y
    chips = _other_chips(x, y)
    sibling = (x, y, 1 - c)

    def cp(t, k, src, dst, to):
        i = GATHER_COPIES * t + k
        return _remote(src, dst, send_sems.at[i], recv_sems.at[i], to)

    for t in range(len(ins)):
        own = cp(t, 0, ins[t], outs[t].at[me], sibling)
        if phase == 0:
            own.start()
        if phase == 2:
            own.wait()
        for k, (px, py) in enumerate(chips):
            landed = outs[t].at[2 * px + py, c]
            theirs = outs[t].at[2 * px + py, 1 - c]
            to_chip = cp(t, 1 + k, ins[t].at[c], outs[t].at[me, c], (px, py, c))
            if phase == 0:
                to_chip.start()
            if phase == 1:
                cp(t, 1 + k, ins[t].at[c], landed, (px, py, c)).wait_recv()
                cp(t, 4 + k, landed, landed, sibling).start()
            if phase == 2:
                to_chip.wait_send()
                cp(t, 4 + k, landed, landed, sibling).wait_send()
                cp(t, 4 + k, theirs, theirs, sibling).wait_recv()


def _chips_phase(phase, ins, outs, send_sems, recv_sems):
    x, y, c = _mesh_pos()
    for t in range(len(ins)):
        for k, (px, py) in enumerate(_other_chips(x, y)):
            cp = _remote(ins[t].at[2 * px + py], outs[t].at[k], send_sems.at[3 * t + k], recv_sems.at[3 * t + k],
                         (px, py, c))
            if phase == 0:
                cp.start()
            if phase == 2:
                cp.wait()


class _Ride:
    def __init__(self, kind, arrays):
        self.kind, self.arrays = kind, list(arrays)
        nt = len(self.arrays)
        if kind == "gather":
            self.phase_fn, n_sem = _gather_phase, GATHER_COPIES * nt
            self.out_shape = [jax.ShapeDtypeStruct((N_SHARD,) + a.shape, a.dtype) for a in self.arrays]
        elif kind == "pair":
            self.phase_fn, n_sem = _pair_phase, nt
            self.out_shape = [jax.ShapeDtypeStruct(a.shape[:1] + a.shape[2:], a.dtype) for a in self.arrays]
        else:
            self.phase_fn, n_sem = _chips_phase, 3 * nt
            self.out_shape = [jax.ShapeDtypeStruct((3,) + a.shape[1:], a.dtype) for a in self.arrays]
        self.in_specs, self.out_specs = [ANY] * nt, [ANY] * nt
        self.scratch = [pltpu.SemaphoreType.DMA((n_sem,)), pltpu.SemaphoreType.DMA((n_sem,))]

    def emit(self, step, nsteps, ins, outs, send_sems, recv_sems, before):
        mid = max(0, min((7 * nsteps) // 8, nsteps - 2))
        todo = [(0, 0), (1, mid)] if before else [(2, nsteps - 1)]
        for phase, at in todo:
            if phase == 1 and self.kind != "gather":
                continue

            @pl.when(step == at)
            def _(phase=phase):
                self.phase_fn(phase, ins, outs, send_sems, recv_sems)


def _gather_small(blk):
    r, wd = blk.shape

    def body(b_ref, out_ref, send_sems, recv_sems):
        x, y, c = _mesh_pos()
        chips = _other_chips(x, y)
        out_ref[2 * x + y] = b_ref[...]
        sends = [_remote(b_ref, out_ref.at[2 * x + y], send_sems.at[k], recv_sems.at[k], (px, py, c))
                 for k, (px, py) in enumerate(chips)]
        for cp in sends:
            cp.start()
        for k, (px, py) in enumerate(chips):
            _remote(b_ref, out_ref.at[2 * px + py], send_sems.at[k], recv_sems.at[k], (px, py, c)).wait_recv()
        for cp in sends:
            cp.wait_send()

    return pl.pallas_call(
        body, out_shape=jax.ShapeDtypeStruct((4, r, wd), blk.dtype), in_specs=[VMEM_SPEC], out_specs=VMEM_SPEC,
        scratch_shapes=[pltpu.SemaphoreType.DMA((3,)), pltpu.SemaphoreType.DMA((3,))],
        name="gather_small")(blk)


def _allreduce_small(blk):
    r, wd = blk.shape
    rels = [(dx, dy, dc) for dx in (0, 1) for dy in (0, 1) for dc in (0, 1) if dx or dy or dc]

    def body(b_ref, out_ref, buf_ref, send_sems, recv_sems):
        x, y, c = _mesh_pos()

        def peer(rel):
            dx, dy, dc = rel
            return (1 - x if dx else x, 1 - y if dy else y, 1 - c if dc else c)

        me = 4 * x + 2 * y + c
        buf_ref[me] = b_ref[...]
        sends = [_remote(b_ref, buf_ref.at[me], send_sems.at[k], recv_sems.at[k], peer(rel))
                 for k, rel in enumerate(rels)]
        for cp in sends:
            cp.start()
        for k, rel in enumerate(rels):
            px, py, pc = peer(rel)
            _remote(b_ref, buf_ref.at[4 * px + 2 * py + pc], send_sems.at[k], recv_sems.at[k],
                    (px, py, pc)).wait_recv()
        for cp in sends:
            cp.wait_send()
        acc = buf_ref[0]
        for d in range(1, 8):
            acc = acc + buf_ref[d]
        out_ref[...] = acc

    return pl.pallas_call(
        body, out_shape=jax.ShapeDtypeStruct((r, wd), blk.dtype), in_specs=[VMEM_SPEC], out_specs=VMEM_SPEC,
        scratch_shapes=[pltpu.VMEM((8, r, wd), blk.dtype), pltpu.SemaphoreType.DMA((7,)),
                        pltpu.SemaphoreType.DMA((7,))],
        name="allreduce_small")(blk)


def _rs_pair(gs, name):
    ride = _Ride("pair", gs)

    def body(*refs):
        nt = len(gs)
        for phase in (0, 2):
            _pair_phase(phase, refs[:nt], refs[nt:2 * nt], *refs[2 * nt:])

    return pl.pallas_call(body, out_shape=ride.out_shape, in_specs=ride.in_specs, out_specs=ride.out_specs,
                          scratch_shapes=ride.scratch, name=name)(*gs)


def _pair_phase(phase, ins, outs, send_sems, recv_sems):
    x, y, c = _mesh_pos()
    for t in range(len(ins)):
        cp = _remote(ins[t].at[:, 1 - c], outs[t], send_sems.at[t], recv_sems.at[t], (x, y, 1 - c))
        if phase == 0:
            cp.start()
        if phase == 2:
            cp.wait()


def _rs_tile(a, b):
    return _div_tile(a, 512 if b <= 1024 else 256, 16)


def _rs_pair_add(g, a, idx, name):
    _, _, rows, cols = g.shape
    tr = _rs_tile(rows, cols)

    def body(s_ref, g_ref, a_ref, p_ref):
        p_ref[...] = (g_ref[...] + a_ref[...]).astype(p_ref.dtype)

    blk = pl.BlockSpec((None, tr, cols), lambda j, i, s: (j, i, 0))
    spec = pltpu.PrefetchScalarGridSpec(
        num_scalar_prefetch=1, grid=(N_SHARD, rows // tr),
        in_specs=[pl.BlockSpec((None, None, tr, cols), lambda j, i, s: (j, s[0], i, 0)), blk], out_specs=blk)
    return pl.pallas_call(
        body, grid_spec=spec, out_shape=jax.ShapeDtypeStruct((N_SHARD, rows, cols), BF16), name=name,
        compiler_params=_params("parallel", "parallel"))(idx, g, a)


def _rs_final_add(g, a, b, idx, name):
    _, _, rows, cols = g.shape
    tr = _rs_tile(rows, cols)

    def body(s_ref, g_ref, a_ref, b0_ref, b1_ref, b2_ref, f_ref):
        own = g_ref[...] + a_ref[...]
        f_ref[...] = ((own + b0_ref[...].astype(F32)) + b1_ref[...].astype(F32)) + b2_ref[...].astype(F32)

    def b_spec(k):
        return pl.BlockSpec((None, tr, cols), lambda i, s: (k, i, 0))

    spec = pltpu.PrefetchScalarGridSpec(
        num_scalar_prefetch=1, grid=(rows // tr,),
        in_specs=[pl.BlockSpec((None, None, tr, cols), lambda i, s: (s[1], s[0], i, 0)),
                  pl.BlockSpec((None, tr, cols), lambda i, s: (s[1], i, 0)), b_spec(0), b_spec(1), b_spec(2)],
        out_specs=pl.BlockSpec((None, tr, cols), lambda i, s: (s[0], i, 0)))
    return pl.pallas_call(
        body, grid_spec=spec, out_shape=jax.ShapeDtypeStruct((2, rows, cols), F32), name=name,
        compiler_params=_params("parallel"))(idx, g, a, b, b, b)


def _rs_share(fs, name):
    nt = len(fs)

    def body(*refs):
        outs = refs[nt:2 * nt]
        send_sems, recv_sems = refs[2 * nt:]
        x, y, c = _mesh_pos()
        cps = [_remote(outs[t].at[c], outs[t].at[c], send_sems.at[t], recv_sems.at[t], (x, y, 1 - c))
               for t in range(nt)]
        for cp in cps:
            cp.start()
        for cp in cps:
            cp.wait()

    return pl.pallas_call(
        body, out_shape=[jax.ShapeDtypeStruct(f.shape, f.dtype) for f in fs],
        in_specs=[ANY] * nt, out_specs=[ANY] * nt, input_output_aliases={t: t for t in range(nt)},
        scratch_shapes=[pltpu.SemaphoreType.DMA((nt,)), pltpu.SemaphoreType.DMA((nt,))], name=name)(*fs)


def _adamw(w, g, m, v, name):
    lead, rows, cols = w.shape
    tr = rows // 4 if rows % 32 == 0 else rows

    def body(w_ref, g_ref, m_ref, v_ref, go_ref, d_ref, mo_ref, vo_ref):
        gv = g_ref[...]
        go_ref[...] = gv
        mn = ADAM_B1 * m_ref[...] + (1.0 - ADAM_B1) * gv
        vn = ADAM_B2 * v_ref[...] + (1.0 - ADAM_B2) * (gv * gv)
        m_hat = mn / (1.0 - ADAM_B1 ** ADAM_STEP)
        v_hat = vn / (1.0 - ADAM_B2 ** ADAM_STEP)
        d_ref[...] = -ADAM_LR * (m_hat / (jnp.sqrt(v_hat) + ADAM_EPS) + ADAM_WD * w_ref[...])
        mo_ref[...] = mn
        vo_ref[...] = vn

    blk = pl.BlockSpec((None, tr, cols), lambda l, i: (l, i, 0))
    out = jax.ShapeDtypeStruct((lead, rows, cols), F32)
    return pl.pallas_call(
        body, grid=(lead, rows // tr), in_specs=[blk] * 4, out_specs=[blk] * 4, out_shape=[out] * 4, name=name,
        compiler_params=_params("parallel", "parallel"))(w, g, m, v)


BIG = ["ret_w_in", "ret_w_out", "dn_w_in", "dn_w_out", "ffn_w_gate", "ffn_w_up", "ffn_w_down"]
TRANSPOSED_AT_BOUNDARY = {"dn_w_in": True, "ffn_w_gate": False, "ffn_w_up": False}
SMALL =["meta_tokens", "mix_norm_w", "ffn_norm_w", "ret_gn_w", "dn_conv_w", "dn_a_log", "dn_dt_bias",
         "dn_norm_w", "final_norm_w"]
SMALL_SHARDED = {"meta_tokens", "dn_conv_w", "dn_norm_w"}
ORDER = ["meta_tokens", "mix_norm_w", "ffn_norm_w", "ret_w_in", "ret_gn_w", "ret_w_out", "dn_w_in",
         "dn_conv_w", "dn_a_log", "dn_dt_bias", "dn_norm_w", "dn_w_out", "ffn_w_gate", "ffn_w_up",
         "ffn_w_down", "final_norm_w"]


def _halves(a):
    return a.reshape(2, -1, a.shape[-1])


def _pack_lanes(parts, align=8):
    flat = jnp.concatenate([p.reshape(-1) for p in parts])
    flat = jnp.pad(flat, (0, -flat.shape[0] % (align * LANES)))
    return flat.reshape(-1, LANES)


def _unpack(buf, shapes):
    lead = buf.shape[:-2]
    flat = buf.reshape(lead + (-1,))
    out, off = [], 0
    for shp in shapes:
        size = math.prod(shp)
        out.append(flat[..., off:off + size].reshape(lead + tuple(shp)))
        off += size
    return out


def _join_cols(shards):
    return jnp.concatenate([shards[j] for j in range(N_SHARD)], axis=-1)


def kernel(x, meta_tokens, mix_norm_w, ffn_norm_w, ret_w_in, ret_gn_w, ret_w_out, dn_w_in, dn_conv_w, dn_a_log, dn_dt_bias, dn_norm_w, dn_w_out, ffn_w_gate, ffn_w_up, ffn_w_down, final_norm_w, loss_target, m_meta_tokens, m_mix_norm_w, m_ffn_norm_w, m_ret_w_in, m_ret_gn_w, m_ret_w_out, m_dn_w_in, m_dn_conv_w, m_dn_a_log, m_dn_dt_bias, m_dn_norm_w, m_dn_w_out, m_ffn_w_gate, m_ffn_w_up, m_ffn_w_down, m_final_norm_w, v_meta_tokens, v_mix_norm_w, v_ffn_norm_w, v_ret_w_in, v_ret_gn_w, v_ret_w_out, v_dn_w_in, v_dn_conv_w, v_dn_a_log, v_dn_dt_bias, v_dn_norm_w, v_dn_w_out, v_ffn_w_gate, v_ffn_w_up, v_ffn_w_down, v_final_norm_w):
    w = dict(meta_tokens=meta_tokens, mix_norm_w=mix_norm_w, ffn_norm_w=ffn_norm_w, ret_w_in=ret_w_in,
             ret_gn_w=ret_gn_w, ret_w_out=ret_w_out, dn_w_in=dn_w_in, dn_conv_w=dn_conv_w, dn_a_log=dn_a_log,
             dn_dt_bias=dn_dt_bias, dn_norm_w=dn_norm_w, dn_w_out=dn_w_out, ffn_w_gate=ffn_w_gate,
             ffn_w_up=ffn_w_up, ffn_w_down=ffn_w_down, final_norm_w=final_norm_w)
    m = dict(meta_tokens=m_meta_tokens, mix_norm_w=m_mix_norm_w, ffn_norm_w=m_ffn_norm_w, ret_w_in=m_ret_w_in,
             ret_gn_w=m_ret_gn_w, ret_w_out=m_ret_w_out, dn_w_in=m_dn_w_in, dn_conv_w=m_dn_conv_w,
             dn_a_log=m_dn_a_log, dn_dt_bias=m_dn_dt_bias, dn_norm_w=m_dn_norm_w, dn_w_out=m_dn_w_out,
             ffn_w_gate=m_ffn_w_gate, ffn_w_up=m_ffn_w_up, ffn_w_down=m_ffn_w_down, final_norm_w=m_final_norm_w)
    v = dict(meta_tokens=v_meta_tokens, mix_norm_w=v_mix_norm_w, ffn_norm_w=v_ffn_norm_w, ret_w_in=v_ret_w_in,
             ret_gn_w=v_ret_gn_w, ret_w_out=v_ret_w_out, dn_w_in=v_dn_w_in, dn_conv_w=v_dn_conv_w,
             dn_a_log=v_dn_a_log, dn_dt_bias=v_dn_dt_bias, dn_norm_w=v_dn_norm_w, dn_w_out=v_dn_w_out,
             ffn_w_gate=v_ffn_w_gate, ffn_w_up=v_ffn_w_up, ffn_w_down=v_ffn_w_down, final_norm_w=v_final_norm_w)
    mx, my, mc = _mesh_pos()
    chip = 2 * mx + my

    sm_names = [n for n in SMALL if n in SMALL_SHARDED]
    sm_gathered = _unpack(_gather_small(_pack_lanes([w[n] for n in sm_names])), [w[n].shape for n in sm_names])
    full = {n: _join_cols(sm_gathered[i]) for i, n in enumerate(sm_names)}
    wts = {
        "meta_tokens": full["meta_tokens"], "mix_norm_w": mix_norm_w, "ffn_norm_w": ffn_norm_w,
        "ret_gn_w": ret_gn_w[0], "final_norm_w": final_norm_w, "dn_conv_w": full["dn_conv_w"][0],
        "dn_a_log": dn_a_log[0], "dn_dt_bias": dn_dt_bias[0], "dn_norm_w": full["dn_norm_w"][0],
    }
    idx = jnp.stack([mc, chip]).astype(jnp.int32)
    shards = {n: _halves(w[n].astype(MXU_DTYPE)) for n in BIG}
    loss_part, dh0, g, reduced = _train_step(x[0], loss_target[0], wts, shards, idx)
    seq = x.shape[1]
    grad_x = dh0[CHUNK:CHUNK + seq].reshape(x.shape)
    gsh = {}

    small_full_shapes = [g[n].shape for n in SMALL] + [(1,)]
    red = _unpack(_allreduce_small(_pack_lanes([g[n] for n in SMALL] + [loss_part[0, :1]])), small_full_shapes)
    loss = red[-1][0]
    for i, n in enumerate(SMALL):
        gn = red[i]
        if n in SMALL_SHARDED:
            width = w[n].shape[-1]
            gn = lax.dynamic_slice_in_dim(gn, chip * width, width, axis=gn.ndim - 1)
        gsh[n] = gn.reshape(w[n].shape)

    delta, new_m, new_v = {}, {}, {}
    for n in BIG:
        shp = w[n].shape
        if n in TRANSPOSED_AT_BOUNDARY and TRANSPOSED_AT_BOUNDARY[n]:
            view = lambda a: jnp.swapaxes(a, 1, 2).reshape(1, -1, LANES)
            back = lambda a: jnp.swapaxes(a.reshape(shp[0], shp[2], shp[1]), 1, 2)
        elif n in TRANSPOSED_AT_BOUNDARY:
            view = back = lambda a: jnp.swapaxes(a, 1, 2)
        else:
            view = back = lambda a: a
        res = _adamw(view(w[n]), view(reduced[n].reshape(shp)), view(m[n]), view(v[n]), "adamw_" + n)
        gsh[n], delta[n], new_m[n], new_v[n] = [back(r) for r in res]
    sm_local_shapes = [w[n].shape for n in SMALL]
    _, d_, m_, v_ = _adamw(*[_pack_lanes([t[n] for n in SMALL])[None] for t in (w, gsh, m, v)], "adamw_small")
    d_, m_, v_ = d_[0], m_[0], v_[0]
    for n, dd, mm, vv in zip(SMALL, _unpack(d_, sm_local_shapes), _unpack(m_, sm_local_shapes),
                             _unpack(v_, sm_local_shapes)):
        delta[n], new_m[n], new_v[n] = dd, mm, vv

    return (loss, grad_x, *[gsh[n] for n in ORDER], *[delta[n] for n in ORDER],
            *[new_m[n] for n in ORDER], *[new_v[n] for n in ORDER])
```

```python
import functools
import math

import jax
import jax.numpy as jnp
from jax import lax
from jax.experimental import pallas as pl
from jax.experimental.pallas import tpu as pltpu

F32 = jnp.float32
BF16 = jnp.bfloat16
MXU_DTYPE = BF16

D_MODEL = 1024
N_META = 16
CHUNK = 64
PAD = CHUNK - N_META
RMS_EPS = 1e-6
RET_HEADS, RET_DK, RET_DV = 4, 256, 512
RET_QK, RET_V = RET_HEADS * RET_DK, RET_HEADS * RET_DV
RET_IN = 2 * RET_QK + 2 * RET_V
ROPE_BASE = 10000.0
DN_HEADS, DN_DK, DN_DV = 8, 128, 256
DN_QK, DN_V = DN_HEADS * DN_DK, DN_HEADS * DN_DV
DN_CONV_CH = 2 * DN_QK + DN_V
DN_IN = DN_CONV_CH + DN_V + 2 * DN_HEADS
LANES = 128
DN_IN_USED = DN_CONV_CH + DN_V + LANES
DN_IN_PAD = DN_IN_USED + LANES
CONV_K = 4
FFN_HIDDEN = 2816
ADAM_LR, ADAM_B1, ADAM_B2, ADAM_EPS, ADAM_WD, ADAM_STEP = 0.001, 0.9, 0.999, 1e-08, 0.01, 10

ROW_ALIGN = 256
VMEM_LIMIT = 56 * 1024 * 1024
MESH = pl.DeviceIdType.MESH
ANY = pl.BlockSpec(memory_space=pl.ANY)
VMEM_SPEC = pl.BlockSpec(memory_space=pltpu.VMEM)
_HI = lax.Precision.HIGHEST


def _params(*sem):
    return pltpu.CompilerParams(dimension_semantics=sem, vmem_limit_bytes=VMEM_LIMIT)


def _dg(a, b, ca, cb, hi):
    dims = (((ca,), (cb,)), ((), ()))

    def dot(p, q):
        return lax.dot_general(p, q, dims, preferred_element_type=F32)

    if not hi:
        return dot(a.astype(MXU_DTYPE), b.astype(MXU_DTYPE))
    if MXU_DTYPE == F32:
        return lax.dot_general(a, b, dims, precision=_HI, preferred_element_type=F32)
    a_hi, b_hi = a.astype(MXU_DTYPE), b.astype(MXU_DTYPE)
    a_lo = (a - a_hi.astype(F32)).astype(MXU_DTYPE)
    b_lo = (b - b_hi.astype(F32)).astype(MXU_DTYPE)
    return dot(a_hi, b_hi) + (dot(a_hi, b_lo) + dot(a_lo, b_hi))


def _nn(a, b, hi=False):
    return _dg(a, b, 1, 0, hi)


def _nt(a, b, hi=False):
    return _dg(a, b, 1, 1, hi)


def _tn(a, b, hi=False):
    return _dg(a, b, 0, 0, hi)


def _iota(shape, dim):
    return lax.broadcasted_iota(jnp.int32, shape, dim)


def _valid_rows(first_row, rows, seq):
    r = first_row + _iota((rows, 1), 0)
    return ((r >= PAD) & (r < CHUNK + seq)).astype(F32)


def _rope(t, cs, sn):
    half = t.shape[-1] // 2
    t1, t2 = t[:, :half], t[:, half:]
    return jnp.concatenate([t1 * cs - t2 * sn, t1 * sn + t2 * cs], axis=1)


def _rope_bwd(d, cs, sn):
    half = d.shape[-1] // 2
    d1, d2 = d[:, :half], d[:, half:]
    return jnp.concatenate([d1 * cs + d2 * sn, d2 * cs - d1 * sn], axis=1)


def _col(x, idx):
    oh = (_iota((1, x.shape[1]), 1) == idx).astype(F32)
    return jnp.sum(x * oh, axis=1, keepdims=True)


def _row(x, idx):
    oh = (_iota((x.shape[0], 1), 0) == idx).astype(F32)
    return jnp.sum(x * oh, axis=0, keepdims=True)


def _shift_down(x, halo8, k):
    xr = pltpu.roll(x, k, 0)
    hr = pltpu.roll(halo8, k, 0)
    first = jnp.where(_iota((8, 1), 0) < k, hr, xr[0:8])
    return jnp.concatenate([first, xr[8:]], axis=0)


def _shift_up(x, next8, j):
    rows = x.shape[0]
    xr = pltpu.roll(x, rows - j, 0)
    nr = pltpu.roll(next8, 8 - j, 0)
    last = jnp.where(_iota((8, 1), 0) >= 8 - j, nr, xr[rows - 8:])
    return jnp.concatenate([xr[:rows - 8], last], axis=0)


def _gated_norm(o, gate, w):
    r = lax.rsqrt(jnp.mean(o * o, axis=-1, keepdims=True) + RMS_EPS)
    return o * r * w * (gate * jax.nn.sigmoid(gate))


def _gated_norm_bwd(dy, o, gate, w):
    r = lax.rsqrt(jnp.mean(o * o, axis=-1, keepdims=True) + RMS_EPS)
    nrm = o * r
    sg = jax.nn.sigmoid(gate)
    sl = gate * sg
    dgate = dy * nrm * w * (sg * (1.0 + gate * (1.0 - sg)))
    dn = dy * w * sl
    dw = jnp.sum(dy * nrm * sl, axis=0, keepdims=True)
    do = r * (dn - nrm * jnp.mean(dn * nrm, axis=-1, keepdims=True))
    return do, dgate, dw


def _softplus(z):
    return jnp.maximum(z, 0.0) + jnp.log(1.0 + jnp.exp(-jnp.abs(z)))


def _row_tile(rows, cap=768):
    for t in (768, 512, 256, 128, 64, 32, 16, 8):
        if t <= cap and rows % t == 0:
            return t
    return rows


TILE_BUDGET = 44 * 1024 * 1024


def _fit_rows(rows, row_bytes, fixed_bytes, value_row_bytes):
    best = None
    for t in range(LANES, rows + 1, LANES):
        if rows % t == 0 and 2 * (row_bytes * t + fixed_bytes) + value_row_bytes * t <= TILE_BUDGET:
            best = t
    return best or _row_tile(rows, 256)


def _div_tile(n, cap, mult):
    best = None
    for t in range(mult, min(cap, n) + 1, mult):
        if n % t == 0:
            best = t
    return best or n


def _col_tile(cols, cap=1536):
    best = None
    for t in range(LANES, min(cap, cols) + 1, LANES):
        if cols % t == 0:
            best = t
    return best or cols


def _rms_fwd(h, w, name, ride=None):
    rows, d = h.shape
    tm = _row_tile(rows)

    def body(h_ref, w_ref, o_ref):
        x = h_ref[...]
        r = lax.rsqrt(jnp.mean(x * x, axis=-1, keepdims=True) + RMS_EPS)
        o_ref[...] = (x * r * w_ref[...]).astype(o_ref.dtype)

    res, rode = _pcall(body, [h, w.reshape(1, d)], grid=(rows // tm,),
                       in_specs=[pl.BlockSpec((tm, d), lambda i: (i, 0)), pl.BlockSpec((1, d), lambda i: (0, 0))],
                       out_specs=[pl.BlockSpec((tm, d), lambda i: (i, 0))],
                       out_shape=[jax.ShapeDtypeStruct((rows, d), BF16)], name=name, sem=("parallel",), ride=ride)
    return res[0] if ride is None else (res[0], rode)


def _gmm_rms(name, grid, args, in_specs, row_spec, fn, h, w, resid, row_axis, red_axis=None, ride=None):
    m, d = h.shape
    n_in = len(args)
    vec = pl.BlockSpec((1, d), lambda *g: (0, 0))

    def body(*refs):
        ins = refs[:n_in]
        h_ref, w_ref, r_ref, dh_ref, dw_ref, dh16_ref = refs[n_in:]
        part = fn(*ins)
        row = pl.program_id(row_axis)

        def finish(dy):
            x = h_ref[...]
            r = lax.rsqrt(jnp.mean(x * x, axis=-1, keepdims=True) + RMS_EPS)
            xh = x * r
            dxh = dy * w_ref[...]
            dh = r_ref[...] + r * (dxh - xh * jnp.mean(dxh * xh, axis=-1, keepdims=True))
            dh_ref[...] = dh
            dh16_ref[...] = dh.astype(dh16_ref.dtype)
            dwp = jnp.sum(dy * xh, axis=0, keepdims=True)

            @pl.when(row == 0)
            def _():
                dw_ref[...] = dwp

            @pl.when(row > 0)
            def _():
                dw_ref[...] += dwp

        if red_axis is None:
            finish(part)
            return
        k = pl.program_id(red_axis)

        @pl.when(k == 0)
        def _():
            dh_ref[...] = part

        @pl.when(k > 0)
        def _():
            dh_ref[...] += part

        @pl.when(k == grid[red_axis] - 1)
        def _():
            finish(dh_ref[...])

    res, rode = _pcall(body, list(args) + [h, w.reshape(1, d), resid], grid=grid,
                       in_specs=list(in_specs) + [row_spec, vec, row_spec], out_specs=[row_spec, vec, row_spec],
                       out_shape=[jax.ShapeDtypeStruct((m, d), F32), jax.ShapeDtypeStruct((1, d), F32),
                                  jax.ShapeDtypeStruct((m, d), BF16)],
                       name=name, sem=("arbitrary",) * len(grid), ride=ride)
    return res if ride is None else (res, rode)


def _final_loss(h, w, tgt, seq, name):
    rows, d = h.shape
    tm = _row_tile(rows)

    def body(h_ref, w_ref, t_ref, dh_ref, dw_ref, loss_ref, dh16_ref):
        i = pl.program_id(0)
        r_idx = i * tm + _iota((tm, 1), 0)
        m = ((r_idx >= CHUNK) & (r_idx < CHUNK + seq)).astype(F32)
        x = h_ref[...]
        wv = w_ref[...]
        r = lax.rsqrt(jnp.mean(x * x, axis=-1, keepdims=True) + RMS_EPS)
        xh = x * r
        err = (xh * wv - t_ref[...]) * m
        lpart = 0.5 * jnp.sum(jnp.mean(err * err, axis=-1, keepdims=True), axis=0, keepdims=True)
        dyv = err * (1.0 / d)
        dxh = dyv * wv
        dh = r * (dxh - xh * jnp.mean(dxh * xh, axis=-1, keepdims=True))
        dh_ref[...] = dh
        dh16_ref[...] = dh.astype(dh16_ref.dtype)
        part = jnp.sum(dyv * xh, axis=0, keepdims=True)

        @pl.when(i == 0)
        def _():
            dw_ref[...] = part
            loss_ref[...] = jnp.broadcast_to(lpart, loss_ref.shape)

        @pl.when(i > 0)
        def _():
            dw_ref[...] += part
            loss_ref[...] += jnp.broadcast_to(lpart, loss_ref.shape)

    blk = pl.BlockSpec((tm, d), lambda i: (i, 0))
    vec = pl.BlockSpec((1, d), lambda i: (0, 0))
    return pl.pallas_call(
        body, grid=(rows // tm,), in_specs=[blk, vec, blk],
        out_specs=[blk, vec, pl.BlockSpec((1, LANES), lambda i: (0, 0)), blk],
        out_shape=[jax.ShapeDtypeStruct((rows, d), F32), jax.ShapeDtypeStruct((1, d), F32),
                   jax.ShapeDtypeStruct((1, LANES), F32), jax.ShapeDtypeStruct((rows, d), BF16)],
        name=name, compiler_params=_params("arbitrary"))(h, w.reshape(1, d), tgt)


def _isz(x):
    return jnp.dtype(x.dtype).itemsize


def _mm(a, b, *, mode, name, out_dtype=F32, resid=None, col_cap=1536, ride=None):
    if mode == "tn":
        m, k = a.shape
        n = b.shape[1]
        tn = _col_tile(n, col_cap)
        tm = _fit_rows(m, k * _isz(a) + tn * _isz(b), (3 * k * tn * 4) // 2, 2 * (k + tn))

        def body_tn(a_ref, b_ref, o_ref):
            i = pl.program_id(1)
            part = _tn(a_ref[...], b_ref[...])

            @pl.when(i == 0)
            def _():
                o_ref[...] = part

            @pl.when(i > 0)
            def _():
                o_ref[...] += part

        return pl.pallas_call(
            body_tn, grid=(n // tn, m // tm),
            in_specs=[pl.BlockSpec((tm, k), lambda j, i: (i, 0)),
                      pl.BlockSpec((tm, tn), lambda j, i: (i, j))],
            out_specs=pl.BlockSpec((k, tn), lambda j, i: (0, j)),
            out_shape=jax.ShapeDtypeStruct((k, n), F32), name=name,
            compiler_params=_params("parallel", "arbitrary"))(a, b)

    m, ka = a.shape
    n = b.shape[1] if mode == "nn" else b.shape[0]
    has_resid = resid is not None
    tn = _col_tile(n, col_cap)
    tm = _fit_rows(m, ka * _isz(a) + tn * (jnp.dtype(out_dtype).itemsize + (4 if has_resid else 0)),
                   ka * tn * _isz(b), 2 * ka + 8 * tn)

    def body(*refs):
        if has_resid:
            a_ref, b_ref, r_ref, o_ref = refs
        else:
            a_ref, b_ref, o_ref = refs
        acc = _nn(a_ref[...], b_ref[...]) if mode == "nn" else _nt(a_ref[...], b_ref[...])
        if has_resid:
            acc = acc + r_ref[...]
        o_ref[...] = acc.astype(o_ref.dtype)

    b_spec = (pl.BlockSpec((b.shape[0], tn), lambda j, i: (0, j)) if mode == "nn"
              else pl.BlockSpec((tn, b.shape[1]), lambda j, i: (j, 0)))
    o_spec = pl.BlockSpec((tm, tn), lambda j, i: (i, j))
    in_specs = [pl.BlockSpec((tm, ka), lambda j, i: (i, 0)), b_spec]
    args = [a, b]
    if has_resid:
        in_specs.append(o_spec)
        args.append(resid)
    res, rode = _pcall(body, args, grid=(n // tn, m // tm), in_specs=in_specs, out_specs=[o_spec],
                       out_shape=[jax.ShapeDtypeStruct((m, n), out_dtype)], name=name,
                       sem=("parallel", "parallel"), ride=ride)
    return res[0] if ride is None else (res[0], rode)


N_SHARD = 4


def _gmm(name, grid, args, in_specs, out_specs, out_shape, fn, red_axis=None, init_arg=None, aliases=None,
         ride=None):
    n_in = len(args)
    single = not isinstance(out_shape, (list, tuple))
    out_specs = [out_specs] if single else list(out_specs)
    out_shape = [out_shape] if single else list(out_shape)

    def body(*refs):
        _gmm_step(fn, refs[:n_in], refs[n_in:], red_axis, init_arg)

    sem = tuple("arbitrary" if ax == red_axis else "parallel" for ax in range(len(grid)))
    res, rode = _pcall(body, args, grid=grid, in_specs=in_specs, out_specs=out_specs, out_shape=out_shape,
                       name=name, sem=sem, aliases=aliases, ride=ride)
    ours = res[0] if single else res
    return ours if ride is None else (ours, rode)


def _gmm_step(fn, ins, outs, red_axis, init_arg):
    parts = fn(*ins)
    if red_axis is None:
        for o_ref, p in zip(outs, parts):
            o_ref[...] = p.astype(o_ref.dtype)
        return
    k = pl.program_id(red_axis)

    @pl.when(k == 0)
    def _():
        for idx, (o_ref, p) in enumerate(zip(outs, parts)):
            o_ref[...] = p + ins[init_arg][...] if (idx == 0 and init_arg is not None) else p

    @pl.when(k > 0)
    def _():
        for o_ref, p in zip(outs, parts):
            o_ref[...] += p


def _ride_body(ride, grid, n_in, n_out, n_scratch, body):
    n_rin, n_rout = len(ride.arrays), len(ride.out_shape)
    nsteps = math.prod(grid)

    def wrapped(*refs):
        ins = refs[:n_in]
        r_ins = refs[n_in:n_in + n_rin]
        o0 = n_in + n_rin
        outs = refs[o0:o0 + n_out]
        r_outs = refs[o0 + n_out:o0 + n_out + n_rout]
        s0 = o0 + n_out + n_rout
        scratch = refs[s0:s0 + n_scratch]
        send_sems, recv_sems = refs[-2:]
        step = pl.program_id(0)
        for ax in range(1, len(grid)):
            step = step * grid[ax] + pl.program_id(ax)
        ride.emit(step, nsteps, r_ins, r_outs, send_sems, recv_sems, before=True)
        body(*ins, *outs, *scratch)
        ride.emit(step, nsteps, r_ins, r_outs, send_sems, recv_sems, before=False)

    return wrapped


def _pcall(body, args, *, grid, in_specs, out_specs, out_shape, name, sem, scratch=(), aliases=None, ride=None):
    if ride is None:
        res = pl.pallas_call(body, grid=grid, in_specs=list(in_specs), out_specs=list(out_specs),
                             out_shape=list(out_shape), scratch_shapes=list(scratch), name=name,
                             input_output_aliases=aliases or {}, compiler_params=_params(*sem))(*args)
        return res, None
    n_in, n_out = len(args), len(out_shape)
    res = pl.pallas_call(
        _ride_body(ride, grid, n_in, n_out, len(scratch), body), grid=grid,
        in_specs=list(in_specs) + ride.in_specs, out_specs=list(out_specs) + ride.out_specs,
        out_shape=list(out_shape) + ride.out_shape, scratch_shapes=list(scratch) + ride.scratch, name=name,
        input_output_aliases=aliases or {},
        compiler_params=_params(*(("arbitrary",) * len(grid))))(*args, *ride.arrays)
    return res[:n_out], res[n_out:]


def _mm_cols(a, ws, name, ride=None):
    m, k = a.shape
    n = ws.shape[2]
    tm = _fit_rows(m, k * _isz(a) + n * 4, k * n * _isz(ws), 4 * n)
    return _gmm(name, (N_SHARD, m // tm), [a, ws],
                [pl.BlockSpec((tm, k), lambda j, i: (i, 0)), pl.BlockSpec((None, k, n), lambda j, i: (j, 0, 0))],
                pl.BlockSpec((tm, n), lambda j, i: (i, j)), jax.ShapeDtypeStruct((m, N_SHARD * n), F32),
                lambda a_ref, w_ref: (_nn(a_ref[...], w_ref[...]),), ride=ride)


def _mm_cols_t_rms(d, ws, h, w, resid, name, ride=None):
    m = d.shape[0]
    _, k, n = ws.shape
    tm = _fit_rows(m, n * _isz(d) + 3 * k * 4, k * n * _isz(ws), 16 * k)
    return _gmm_rms(name, (m // tm, N_SHARD), [d, ws],
                    [pl.BlockSpec((tm, n), lambda i, j: (i, j)), pl.BlockSpec((None, k, n), lambda i, j: (j, 0, 0))],
                    pl.BlockSpec((tm, k), lambda i, j: (i, 0)),
                    lambda d_ref, w_ref: _nt(d_ref[...], w_ref[...]), h, w, resid, 0, red_axis=1, ride=ride)


def _mm_nt_rms(a, b, h, w, resid, name, ride=None):
    m, n = a.shape
    k = b.shape[0]
    tm = _fit_rows(m, n * _isz(a) + 3 * k * 4, k * n * _isz(b), 16 * k)
    return _gmm_rms(name, (m // tm,), [a, b],
                    [pl.BlockSpec((tm, n), lambda i: (i, 0)), pl.BlockSpec((k, n), lambda i: (0, 0))],
                    pl.BlockSpec((tm, k), lambda i: (i, 0)),
                    lambda a_ref, b_ref: _nt(a_ref[...], b_ref[...]), h, w, resid, 0, ride=ride)


def _mm_cols_grad(a, d, name):
    m, k = a.shape
    n = d.shape[1] // N_SHARD
    tm = _fit_rows(m, k * _isz(a) + n * _isz(d), (3 * k * n * 4) // 2, 2 * (k + n))
    return _gmm(name, (N_SHARD, m // tm), [a, d],
                [pl.BlockSpec((tm, k), lambda j, i: (i, 0)), pl.BlockSpec((tm, n), lambda j, i: (i, j))],
                pl.BlockSpec((None, k, n), lambda j, i: (j, 0, 0)), jax.ShapeDtypeStruct((N_SHARD, k, n), F32),
                lambda a_ref, d_ref: (_tn(a_ref[...], d_ref[...]),), red_axis=1)


def _ffn_up(hn, wg, wu, layer, name):
    m, k = hn.shape
    n = wg.shape[3]
    tm = _fit_rows(m, k * _isz(hn) + 3 * n * jnp.dtype(BF16).itemsize, 2 * k * n * _isz(wg), 16 * n)

    def fn(a_ref, wg_ref, wu_ref):
        a = a_ref[...]
        g = _nn(a, wg_ref[...])
        u = _nn(a, wu_ref[...])
        return g, u, g * jax.nn.sigmoid(g) * u

    w_spec = pl.BlockSpec((None, None, k, n), lambda j, i: (j, layer, 0, 0))
    o_spec = pl.BlockSpec((None, tm, n), lambda j, i: (j, i, 0))
    out = jax.ShapeDtypeStruct((N_SHARD, m, n), BF16)
    return _gmm(name, (N_SHARD, m // tm), [hn, wg, wu],
                [pl.BlockSpec((tm, k), lambda j, i: (i, 0)), w_spec, w_spec],
                [o_spec, o_spec, o_spec], [out, out, out], fn)


def _ffn_down(act, wd, resid, layer, name):
    _, m, n = act.shape
    d = wd.shape[3]
    tm = _fit_rows(m, N_SHARD * n * _isz(act) + 2 * d * 4, N_SHARD * n * d * _isz(wd), 8 * d)

    def fn(a_ref, w_ref, r_ref):
        acc = r_ref[...]
        for j in range(N_SHARD):
            acc = acc + _nn(a_ref[j], w_ref[j])
        return (acc,)

    row = pl.BlockSpec((tm, d), lambda i: (i, 0))
    return _gmm(name, (m // tm,), [act, wd, resid],
                [pl.BlockSpec((N_SHARD, tm, n), lambda i: (0, i, 0)),
                 pl.BlockSpec((N_SHARD, None, n, d), lambda i: (0, layer, 0, 0)), row],
                row, jax.ShapeDtypeStruct((m, d), F32), fn)


def _ffn_down_bwd(dh, wd, g, u, layer, name, ride=None):
    m, d = dh.shape
    n = wd.shape[2]
    tm = _fit_rows(m, d * _isz(dh) + 4 * N_SHARD * n * jnp.dtype(BF16).itemsize, N_SHARD * n * d * _isz(wd),
                   2 * d + 24 * n)

    def body(dh_ref, wd_ref, g_ref, u_ref, dg_ref, du_ref):
        dhv = dh_ref[...].astype(MXU_DTYPE)
        for j in range(N_SHARD):
            dact = _nt(dhv, wd_ref[j])
            gv = g_ref[j].astype(F32)
            sg = jax.nn.sigmoid(gv)
            gs = gv * sg
            dg_ref[j] = (dact * u_ref[j].astype(F32) * (sg + gs * (1.0 - sg))).astype(dg_ref.dtype)
            du_ref[j] = (dact * gs).astype(du_ref.dtype)

    sh_spec = pl.BlockSpec((N_SHARD, tm, n), lambda i: (0, i, 0))
    out = jax.ShapeDtypeStruct((N_SHARD, m, n), BF16)
    res, rode = _pcall(body, [dh, wd, g, u], grid=(m // tm,),
                       in_specs=[pl.BlockSpec((tm, d), lambda i: (i, 0)),
                                 pl.BlockSpec((N_SHARD, None, n, d), lambda i: (0, layer, 0, 0)), sh_spec, sh_spec],
                       out_specs=[sh_spec, sh_spec], out_shape=[out, out], name=name, sem=("parallel",), ride=ride)
    return res if ride is None else (res, rode)


def _ffn_up_bwd(dg, du, wg, wu, layer, h, w, resid, name, ride=None):
    _, m, n = dg.shape
    k = wg.shape[2]
    tm = _fit_rows(m, 2 * N_SHARD * n * _isz(dg) + 3 * k * 4, 2 * N_SHARD * k * n * _isz(wg), 16 * k)

    def fn(dg_ref, du_ref, wg_ref, wu_ref):
        acc = _nt(dg_ref[0], wg_ref[0]) + _nt(du_ref[0], wu_ref[0])
        for j in range(1, N_SHARD):
            acc = acc + _nt(dg_ref[j], wg_ref[j]) + _nt(du_ref[j], wu_ref[j])
        return acc

    d_spec = pl.BlockSpec((N_SHARD, tm, n), lambda i: (0, i, 0))
    w_spec = pl.BlockSpec((N_SHARD, None, k, n), lambda i: (0, layer, 0, 0))
    return _gmm_rms(name, (m // tm,), [dg, du, wg, wu], [d_spec, d_spec, w_spec, w_spec],
                    pl.BlockSpec((tm, k), lambda i: (i, 0)), fn, h, w, resid, 0, ride=ride)


def _ffn_wgrad(lhs, rhs_list, layer, layers, prev, lhs_sharded, name):
    if lhs_sharded:
        _, m, k = lhs.shape
        n = rhs_list[0].shape[1]
    else:
        m, k = lhs.shape
        n = rhs_list[0].shape[2]
    n_out = len(rhs_list)
    tm = _fit_rows(m, k * _isz(lhs) + n_out * n * _isz(rhs_list[0]), (3 * n_out * k * n * 4) // 2,
                   2 * (k + n_out * n))
    sh = pl.BlockSpec((None, tm, k if lhs_sharded else n), lambda j, i: (j, i, 0))
    fl = pl.BlockSpec((tm, n if lhs_sharded else k), lambda j, i: (i, 0))
    n_out = len(rhs_list)
    args = [lhs] + list(rhs_list)
    in_specs = [sh if lhs_sharded else fl] + [fl if lhs_sharded else sh] * n_out
    aliases = None
    if prev is not None:
        aliases = {len(args) + t: t for t in range(n_out)}
        args = args + list(prev)
        in_specs = in_specs + [ANY] * n_out

    def fn(l_ref, *rest):
        lv = l_ref[...]
        return tuple(_tn(lv, r_ref[...]) for r_ref in rest[:n_out])

    o_spec = pl.BlockSpec((None, None, k, n), lambda j, i: (j, layer, 0, 0))
    out = jax.ShapeDtypeStruct((N_SHARD, layers, k, n), F32)
    return _gmm(name, (N_SHARD, m // tm), args, in_specs, [o_spec] * n_out, [out] * n_out, fn,
                red_axis=1, aliases=aliases)


def _ret_consts():
    log_gamma = jnp.log1p(-jnp.exp2(-5.0 - jnp.arange(RET_HEADS, dtype=F32)))
    idx = jnp.arange(CHUNK, dtype=F32)
    rel = idx[:, None] - idx[None, :]
    dmask = jnp.where((rel >= 0)[None], jnp.exp(log_gamma[:, None, None] * jnp.maximum(rel, 0.0)), 0.0)
    xi = jnp.exp(log_gamma[:, None] * (idx[None, :] + 1.0))[:, :, None]
    zeta = jnp.exp(log_gamma[:, None] * (CHUNK - 1.0 - idx[None, :]))[:, :, None]
    gamma_c = jnp.exp(log_gamma * CHUNK)
    wide = (RET_HEADS, CHUNK, RET_DK)
    return dmask, jnp.broadcast_to(xi, wide), jnp.broadcast_to(zeta, wide), gamma_c


def _rope_tables(nc):
    half = RET_DK // 2
    inv_freq = ROPE_BASE ** (-jnp.arange(half, dtype=F32) / half)
    a_chunk = (jnp.arange(nc) * CHUNK - PAD).astype(F32)[:, None] * inv_freq[None, :]
    a_row = jnp.arange(CHUNK).astype(F32)[:, None] * inv_freq[None, :]
    return (jnp.stack([jnp.cos(a_chunk), jnp.sin(a_chunk)], axis=1),
            jnp.stack([jnp.cos(a_row), jnp.sin(a_row)], axis=0))


def _rope_chunk(rc_ref, rr_ref):
    cc, sc = rc_ref[0:1, :], rc_ref[1:2, :]
    cr, sr = rr_ref[0], rr_ref[1]
    return cc * cr - sc * sr, sc * cr + cc * sr


def _rope_specs(order):
    half = RET_DK // 2
    return [pl.BlockSpec((None, 2, half), lambda n: (order(n), 0, 0)),
            pl.BlockSpec((2, CHUNK, half), lambda n: (0, 0, 0))]


def _ret_specs(order):
    return [pl.BlockSpec((CHUNK, RET_QK), lambda n: (order(n), 0)),
            pl.BlockSpec((CHUNK, RET_QK), lambda n: (order(n), 1)),
            pl.BlockSpec((CHUNK, RET_V), lambda n: (order(n), 1)),
            pl.BlockSpec((CHUNK, RET_V), lambda n: (order(n), 2))]


def _ret_const_specs():
    return [pl.BlockSpec((RET_HEADS, CHUNK, CHUNK), lambda n: (0, 0, 0)),
            pl.BlockSpec((RET_HEADS, CHUNK, RET_DK), lambda n: (0, 0, 0)),
            pl.BlockSpec((RET_HEADS, CHUNK, RET_DK), lambda n: (0, 0, 0)),
            pl.BlockSpec((1, RET_DV), lambda n: (0, 0))]


def _ret_fwd(proj, cos, sin, consts, gn_w, seq, ride=None):
    rows = proj.shape[0]
    nc = rows // CHUNK
    dmask, xi, zeta, gamma_c = consts

    def body(gam_ref, q_ref, k_ref, v_ref, g_ref, cos_ref, sin_ref, dm_ref, xi_ref, ze_ref, gn_ref,
             o_ref, y_ref, ss_ref, s_ref):
        n = pl.program_id(0)

        @pl.when(n == 0)
        def _():
            s_ref[...] = jnp.zeros_like(s_ref)

        cs, sn = _rope_chunk(cos_ref, sin_ref)
        kscale = _valid_rows(n * CHUNK, CHUNK, seq) * (RET_DK ** -0.5)
        gn = gn_ref[...]
        hs = range(RET_HEADS)
        qk_cols = [slice(h * RET_DK, (h + 1) * RET_DK) for h in hs]
        v_cols = [slice(h * RET_DV, (h + 1) * RET_DV) for h in hs]
        qr_l = [_rope(q_ref[:, c], cs, sn) for c in qk_cols]
        kr_l = [_rope(k_ref[:, c], cs, sn) * kscale for c in qk_cols]
        v_l = [v_ref[:, c] for c in v_cols]
        s_l = [s_ref[h] for h in hs]
        sc_l = [_nt(qr, kr) * dm_ref[h] for h, (qr, kr) in enumerate(zip(qr_l, kr_l))]
        o_l = [_nn(sc_l[h], v_l[h]) + _nn(qr_l[h] * xi_ref[h], s_l[h]) for h in hs]
        for h in hs:
            ss_ref[0, h] = s_l[h].astype(ss_ref.dtype)
            s_ref[h] = gam_ref[h] * s_l[h] + _tn(kr_l[h] * ze_ref[h], v_l[h])
            o_ref[:, v_cols[h]] = o_l[h]
            y_ref[:, v_cols[h]] = _gated_norm(o_l[h], g_ref[:, v_cols[h]], gn).astype(y_ref.dtype)

    fwd = lambda n: n
    row_v = pl.BlockSpec((CHUNK, RET_V), lambda n: (n, 0))
    res, rode = _pcall(
        body, [gamma_c, proj, proj, proj, proj, cos, sin, dmask, xi, zeta, gn_w.reshape(1, RET_DV)],
        grid=(nc,),
        in_specs=[pl.BlockSpec(memory_space=pltpu.SMEM)] + _ret_specs(fwd) + _rope_specs(fwd)
        + _ret_const_specs(),
        out_specs=[row_v, row_v,
                   pl.BlockSpec((1, RET_HEADS, RET_DK, RET_DV), lambda n: (n, 0, 0, 0))],
        out_shape=[jax.ShapeDtypeStruct((rows, RET_V), F32), jax.ShapeDtypeStruct((rows, RET_V), BF16),
                   jax.ShapeDtypeStruct((nc, RET_HEADS, RET_DK, RET_DV), BF16)],
        scratch=[pltpu.VMEM((RET_HEADS, RET_DK, RET_DV), F32)], name="ret_fwd", sem=("arbitrary",), ride=ride)
    return res if ride is None else (res, rode)


def _ret_bwd(proj, o, dy, states, cos, sin, consts, gn_w, seq, ride=None):
    rows = proj.shape[0]
    nc = rows // CHUNK
    dmask, xi, zeta, gamma_c = consts

    def body(gam_ref, q_ref, k_ref, v_ref, g_ref, o_ref, dy_ref, ss_ref, cos_ref, sin_ref,
             dm_ref, xi_ref, ze_ref, gn_ref, dp_ref, dgn_ref, ds_ref):
        n = pl.program_id(0)

        @pl.when(n == 0)
        def _():
            ds_ref[...] = jnp.zeros_like(ds_ref)
            dgn_ref[...] = jnp.zeros_like(dgn_ref)

        cs, sn = _rope_chunk(cos_ref, sin_ref)
        kscale = _valid_rows((nc - 1 - n) * CHUNK, CHUNK, seq) * (RET_DK ** -0.5)
        gn = gn_ref[...]
        dgn = jnp.zeros((1, RET_DV), F32)
        hs = range(RET_HEADS)
        qk_cols = [slice(h * RET_DK, (h + 1) * RET_DK) for h in hs]
        v_cols = [slice(h * RET_DV, (h + 1) * RET_DV) for h in hs]
        qr_l = [_rope(q_ref[:, c], cs, sn) for c in qk_cols]
        kr_l = [_rope(k_ref[:, c], cs, sn) * kscale for c in qk_cols]
        v_l = [v_ref[:, c] for c in v_cols]
        s_l = [ss_ref[0, h] for h in hs]
        ds_l = [ds_ref[h] for h in hs]
        gnb = [_gated_norm_bwd(dy_ref[:, c], o_ref[:, c], g_ref[:, c], gn) for c in v_cols]
        do_l = [x[0] for x in gnb]
        sc_l = [_nt(qr_l[h], kr_l[h]) * dm_ref[h] for h in hs]
        dsc_l = [_nt(do_l[h], v_l[h]) * dm_ref[h] for h in hs]
        dv_l = [_tn(sc_l[h], do_l[h]) + _nn(kr_l[h] * ze_ref[h], ds_l[h]) for h in hs]
        dqr_l = [_nn(dsc_l[h], kr_l[h]) + _nt(do_l[h], s_l[h]) * xi_ref[h] for h in hs]
        dkr_l = [_tn(dsc_l[h], qr_l[h]) + _nt(v_l[h], ds_l[h]) * ze_ref[h] for h in hs]
        for h in hs:
            dgn = dgn + gnb[h][2]
            ds_ref[h] = gam_ref[h] * ds_l[h] + _tn(qr_l[h] * xi_ref[h], do_l[h])
            dp_ref[:, qk_cols[h]] = _rope_bwd(dqr_l[h], cs, sn).astype(dp_ref.dtype)
            dp_ref[:, RET_QK + h * RET_DK:RET_QK + (h + 1) * RET_DK] = (
                _rope_bwd(dkr_l[h] * kscale, cs, sn).astype(dp_ref.dtype))
            dp_ref[:, 2 * RET_QK + h * RET_DV:2 * RET_QK + (h + 1) * RET_DV] = dv_l[h].astype(dp_ref.dtype)
            dp_ref[:, 2 * RET_QK + RET_V + h * RET_DV:2 * RET_QK + RET_V + (h + 1) * RET_DV] = (
                gnb[h][1].astype(dp_ref.dtype))
        dgn_ref[...] += dgn

    rev = lambda n: nc - 1 - n
    row_v = pl.BlockSpec((CHUNK, RET_V), lambda n: (rev(n), 0))
    res, rode = _pcall(
        body, [gamma_c, proj, proj, proj, proj, o, dy, states, cos, sin, dmask, xi, zeta,
               gn_w.reshape(1, RET_DV)],
        grid=(nc,),
        in_specs=[pl.BlockSpec(memory_space=pltpu.SMEM)] + _ret_specs(rev) + [
            row_v, row_v, pl.BlockSpec((1, RET_HEADS, RET_DK, RET_DV), lambda n: (rev(n), 0, 0, 0))]
        + _rope_specs(rev) + _ret_const_specs(),
        out_specs=[pl.BlockSpec((CHUNK, RET_IN), lambda n: (rev(n), 0)),
                   pl.BlockSpec((1, RET_DV), lambda n: (0, 0))],
        out_shape=[jax.ShapeDtypeStruct((rows, RET_IN), BF16), jax.ShapeDtypeStruct((1, RET_DV), F32)],
        scratch=[pltpu.VMEM((RET_HEADS, RET_DK, RET_DV), F32)], name="ret_bwd", sem=("arbitrary",), ride=ride)
    return res if ride is None else (res, rode)


GATE_COL = DN_CONV_CH // DN_V
BA_COL = (DN_CONV_CH + DN_V) // LANES
BETA_LANE, DECAY_LANE = 0, DN_HEADS
INV_SHIFT = 4
INV_SQUARINGS = INV_SHIFT - 1
assert CHUNK == 4 << INV_SHIFT


def _dn_in_specs(order, conv_saved=False):
    return [pl.BlockSpec((CHUNK, DN_CONV_CH), lambda n: (order(n), 0)),
            pl.BlockSpec((CHUNK, DN_CONV_CH), lambda n: (order(n), 0)) if conv_saved else
            pl.BlockSpec((8, DN_CONV_CH), lambda n: (jnp.maximum(order(n) * (CHUNK // 8) - 1, 0), 0)),
            pl.BlockSpec((CHUNK, DN_V), lambda n: (order(n), GATE_COL)),
            pl.BlockSpec((CHUNK, LANES), lambda n: (order(n), BA_COL)),
            pl.BlockSpec((CONV_K, 1, DN_CONV_CH), lambda n: (0, 0, 0)),
            pl.BlockSpec((1, LANES), lambda n: (0, 0)),
            pl.BlockSpec((1, LANES), lambda n: (0, 0)),
            pl.BlockSpec((1, DN_DV), lambda n: (0, 0))]


def _dn_front(c, seq, x_ref, halo_ref, ba_ref, cw_ref, al_ref, dt_ref, yc_ref=None):
    valid = _valid_rows(c * CHUNK, CHUNK, seq)
    xin = x_ref[...] * valid
    if yc_ref is None:
        halo = halo_ref[...] * _valid_rows(c * CHUNK - 8, 8, seq)
        yc = xin * cw_ref[CONV_K - 1]
        for k in range(1, CONV_K):
            yc = yc + _shift_down(xin, halo, k) * cw_ref[CONV_K - 1 - k]
    else:
        yc = yc_ref[...]
    sgc = jax.nn.sigmoid(yc)
    ba = ba_ref[...]
    sig = jax.nn.sigmoid(ba)
    beta = sig * valid
    z = ba + dt_ref[...]
    eal = jnp.exp(al_ref[...])
    g = -eal * _softplus(z) * valid
    ri, ci = _iota((CHUNK, CHUNK), 0), _iota((CHUNK, CHUNK), 1)
    lower = (ri >= ci).astype(F32)
    upper = (ri <= ci).astype(F32)
    eye = (ri == ci).astype(F32)
    gam = _nn(lower, g, hi=True)
    gam_t = _tn(g, upper, hi=True)
    return dict(valid=valid, xin=xin, yc=yc, sgc=sgc, act=yc * sgc, sig=sig, beta=beta, z=z,
                eal=eal, g=g, gam=gam, gam_t=gam_t, ri=ri, ci=ci, upper=upper, eye=eye)


def _dn_head(f, h):
    act = f["act"]
    q_raw = act[:, h * DN_DK:(h + 1) * DN_DK]
    k_raw = act[:, DN_QK + h * DN_DK:DN_QK + (h + 1) * DN_DK]
    v = act[:, 2 * DN_QK + h * DN_DV:2 * DN_QK + (h + 1) * DN_DV]
    rq = lax.rsqrt(jnp.sum(q_raw * q_raw, axis=-1, keepdims=True) + RMS_EPS)
    rk = lax.rsqrt(jnp.sum(k_raw * k_raw, axis=-1, keepdims=True) + RMS_EPS)
    qh = q_raw * rq
    kn = k_raw * rk
    gam_c = _col(f["gam"], DECAY_LANE + h)
    gam_r = _row(f["gam_t"], DECAY_LANE + h)
    bc = _col(f["beta"], BETA_LANE + h)
    diff = gam_c - gam_r
    decay = jnp.where(f["ri"] >= f["ci"], jnp.exp(jnp.minimum(diff, 0.0)), 0.0)
    glast = jnp.sum(gam_r * (_iota((1, CHUNK), 1) == CHUNK - 1).astype(F32), axis=1, keepdims=True)
    return dict(rq=rq, rk=rk, qh=qh, qn=qh * (DN_DK ** -0.5), kn=kn, v=v, gam_c=gam_c, gam_r=gam_r,
                bc=bc, diff=diff, decay=decay, egam=jnp.exp(gam_c), glast=glast,
                eglast=jnp.exp(glast), ekd=jnp.exp(glast - gam_c))


def _dn_fwd(proj, conv_w, alog, dtb, norm_w, seq):
    rows = proj.shape[0]
    nc = rows // CHUNK

    def body(x_ref, halo_ref, gate_ref, ba_ref, cw_ref, al_ref, dt_ref, nw_ref,
             o_ref, y_ref, ss_ref, t_ref, yc_ref, s_ref):
        n = pl.program_id(0)

        @pl.when(n == 0)
        def _():
            s_ref[...] = jnp.zeros_like(s_ref)

        f = _dn_front(n, seq, x_ref, halo_ref, ba_ref, cw_ref, al_ref, dt_ref)
        yc_ref[...] = f["yc"]
        ri, ci = f["ri"], f["ci"]
        eye = f["eye"]
        diag_m = (jnp.right_shift(ri, INV_SHIFT) == jnp.right_shift(ci, INV_SHIFT)).astype(F32)
        half_m = (jnp.right_shift(ri, INV_SHIFT + 1) == jnp.right_shift(ci, INV_SHIFT + 1)).astype(F32)
        nw = nw_ref[...]
        heads = [_dn_head(f, h) for h in range(DN_HEADS)]
        a_all = [jnp.where(ri > ci, hd["bc"] * _nt(hd["kn"], hd["kn"]) * hd["decay"], 0.0) for hd in heads]
        b_all = [a * diag_m for a in a_all]
        t_all = [eye - b for b in b_all]
        for _ in range(INV_SQUARINGS):
            b_all = [_nn(b, b, hi=True) for b in b_all]
            t_all = [t + _nn(t, b, hi=True) for t, b in zip(t_all, b_all)]
        for off_m in (half_m - diag_m, 1.0 - half_m):
            x_all = [_nn(a * off_m, t, hi=True) for a, t in zip(a_all, t_all)]
            t_all = [t - _nn(t, x, hi=True) for t, x in zip(t_all, x_all)]
        u_all = [_nn(t, hd["v"] * hd["bc"], hi=True) for t, hd in zip(t_all, heads)]
        w_all = [_nn(t, hd["kn"] * (hd["bc"] * hd["egam"]), hi=True) for t, hd in zip(t_all, heads)]
        for h in range(DN_HEADS):
            hd = heads[h]
            v_cols = slice(h * DN_DV, (h + 1) * DN_DV)
            t_ref[0, h] = t_all[h]
            s = s_ref[h]
            ss_ref[0, h] = s
            u, w = u_all[h], w_all[h]
            v_new = u - _nn(w, s)
            qk = _nt(hd["qn"], hd["kn"]) * hd["decay"]
            o = _nn(hd["qn"] * hd["egam"], s) + _nn(qk, v_new)
            s_ref[h] = s * hd["eglast"] + _tn(hd["kn"] * hd["ekd"], v_new)
            o_ref[:, v_cols] = o
            y_ref[:, v_cols] = _gated_norm(o, gate_ref[:, v_cols], nw).astype(y_ref.dtype)

    fwd = lambda n: n
    row_v = pl.BlockSpec((CHUNK, DN_V), lambda n: (n, 0))
    return pl.pallas_call(
        body, grid=(nc,), in_specs=_dn_in_specs(fwd),
        out_specs=[row_v, row_v,
                   pl.BlockSpec((1, DN_HEADS, DN_DK, DN_DV), lambda n: (n, 0, 0, 0)),
                   pl.BlockSpec((1, DN_HEADS, CHUNK, CHUNK), lambda n: (n, 0, 0, 0)),
                   pl.BlockSpec((CHUNK, DN_CONV_CH), lambda n: (n, 0))],
        out_shape=[jax.ShapeDtypeStruct((rows, DN_V), F32), jax.ShapeDtypeStruct((rows, DN_V), BF16),
                   jax.ShapeDtypeStruct((nc, DN_HEADS, DN_DK, DN_DV), F32),
                   jax.ShapeDtypeStruct((nc, DN_HEADS, CHUNK, CHUNK), F32),
                   jax.ShapeDtypeStruct((rows, DN_CONV_CH), F32)],
        scratch_shapes=[pltpu.VMEM((DN_HEADS, DN_DK, DN_DV), F32)],
        name="dn_fwd", compiler_params=_params("arbitrary"))(
            proj, proj, proj, proj, conv_w, alog, dtb, norm_w.reshape(1, DN_DV))


def _dn_bwd(proj, conv_out, o, dy, states, tinv, conv_w, alog, dtb, norm_w, seq):
    rows = proj.shape[0]
    nc = rows // CHUNK

    def body(x_ref, yc_ref, gate_ref, ba_ref, cw_ref, al_ref, dt_ref, nw_ref,
             o_ref, dy_ref, ss_ref, t_ref,
             dp_ref, dcw_ref, dal_ref, ddt_ref, dnw_ref, ds_ref, nxt_ref):
        n = pl.program_id(0)

        @pl.when(n == 0)
        def _():
            ds_ref[...] = jnp.zeros_like(ds_ref)
            nxt_ref[...] = jnp.zeros_like(nxt_ref)
            dcw_ref[...] = jnp.zeros_like(dcw_ref)
            dal_ref[...] = jnp.zeros_like(dal_ref)
            ddt_ref[...] = jnp.zeros_like(ddt_ref)
            dnw_ref[...] = jnp.zeros_like(dnw_ref)

        f = _dn_front(nc - 1 - n, seq, x_ref, None, ba_ref, cw_ref, al_ref, dt_ref, yc_ref)
        ri, ci = f["ri"], f["ci"]
        strict = (ri > ci).astype(F32)
        nw = nw_ref[...]
        lane128 = _iota((1, LANES), 1)
        row128 = _iota((LANES, 1), 0)
        dgam_col = jnp.zeros((CHUNK, LANES), F32)
        dgam_row = jnp.zeros((LANES, CHUNK), F32)
        dbeta = jnp.zeros((CHUNK, LANES), F32)
        dnw = jnp.zeros((1, DN_DV), F32)
        hs = range(DN_HEADS)
        heads = [_dn_head(f, h) for h in hs]
        cols = [slice(h * DN_DV, (h + 1) * DN_DV) for h in hs]
        t_l = [t_ref[0, h] for h in hs]
        s_l = [ss_ref[0, h] for h in hs]
        ds_l = [ds_ref[h] for h in hs]
        kk_l = [_nt(hd["kn"], hd["kn"]) for hd in heads]
        p_l = [_nt(hd["qn"], hd["kn"]) for hd in heads]
        rhsw_l = [hd["kn"] * (hd["bc"] * hd["egam"]) for hd in heads]
        u_l = [_nn(t, hd["v"] * hd["bc"], hi=True) for t, hd in zip(t_l, heads)]
        w_l = [_nn(t, r, hi=True) for t, r in zip(t_l, rhsw_l)]
        vnew_l = [u - _nn(w, s) for u, w, s in zip(u_l, w_l, s_l)]
        gnb = [_gated_norm_bwd(dy_ref[:, c], o_ref[:, c], gate_ref[:, c], nw) for c in cols]
        do_l = [x[0] for x in gnb]
        for h in hs:
            dp_ref[:, DN_CONV_CH + h * DN_DV:DN_CONV_CH + (h + 1) * DN_DV] = gnb[h][1].astype(dp_ref.dtype)
            dnw = dnw + gnb[h][2]
        qg_l = [hd["qn"] * hd["egam"] for hd in heads]
        kd_l = [hd["kn"] * hd["ekd"] for hd in heads]
        dvnew_l = [_tn(p * hd["decay"], do) + _nn(kd, ds)
                   for p, hd, do, kd, ds in zip(p_l, heads, do_l, kd_l, ds_l)]
        m_l = [_nt(do, vn) for do, vn in zip(do_l, vnew_l)]
        dqg_l = [_nt(do, s) for do, s in zip(do_l, s_l)]
        dkd_l = [_nt(vn, ds) for vn, ds in zip(vnew_l, ds_l)]
        for h in hs:
            ds_ref[h] = (ds_l[h] * heads[h]["eglast"] + _tn(qg_l[h], do_l[h]) - _tn(w_l[h], dvnew_l[h]))
        dw_l = [-_nt(dvn, s) for dvn, s in zip(dvnew_l, s_l)]
        dru_l = [_tn(t, dvn, hi=True) for t, dvn in zip(t_l, dvnew_l)]
        drw_l = [_tn(t, dw_, hi=True) for t, dw_ in zip(t_l, dw_l)]
        da_l = [-(_nt(dru, u) + _nt(drw, w)) * strict for dru, u, drw, w in zip(dru_l, u_l, drw_l, w_l)]
        dp_l = [m * hd["decay"] for m, hd in zip(m_l, heads)]
        dkk_l = [da * (hd["bc"] * hd["decay"]) for da, hd in zip(da_l, heads)]
        dqn_l = [dqg * hd["egam"] + _nn(dp, hd["kn"]) for dqg, hd, dp in zip(dqg_l, heads, dp_l)]
        dkn_l = [_tn(dp, hd["qn"]) + dkd * hd["ekd"] + drw * (hd["bc"] * hd["egam"])
                 + _nn(dkk, hd["kn"]) + _tn(dkk, hd["kn"])
                 for dp, hd, dkd, drw, dkk in zip(dp_l, heads, dkd_l, drw_l, dkk_l)]
        dq_parts, dk_parts, dv_parts = [], [], []
        for h in hs:
            hd = heads[h]
            kn, v, bc, egam, decay = hd["kn"], hd["v"], hd["bc"], hd["egam"], hd["decay"]
            t1 = jnp.sum(dkd_l[h] * kd_l[h], axis=1, keepdims=True)
            dglast = (jnp.sum(t1, axis=0, keepdims=True)
                      + jnp.sum(jnp.sum(ds_l[h] * s_l[h], axis=1, keepdims=True), axis=0, keepdims=True)
                      * hd["eglast"])
            e = (m_l[h] * p_l[h] + da_l[h] * (bc * kk_l[h])) * decay
            dgc = (jnp.sum(dqg_l[h] * qg_l[h], axis=1, keepdims=True) - t1
                   + jnp.sum(drw_l[h] * rhsw_l[h], axis=1, keepdims=True)
                   + jnp.sum(e, axis=1, keepdims=True)
                   + jnp.where(_iota((CHUNK, 1), 0) == CHUNK - 1, dglast, 0.0))
            dgr = -jnp.sum(e, axis=0, keepdims=True)
            dbc = (jnp.sum(dru_l[h] * v, axis=1, keepdims=True)
                   + jnp.sum(drw_l[h] * kn, axis=1, keepdims=True) * egam
                   + jnp.sum(da_l[h] * kk_l[h] * decay, axis=1, keepdims=True))
            dv_parts.append(dru_l[h] * bc)
            qh, dqn, dkn = hd["qh"], dqn_l[h], dkn_l[h]
            dq_parts.append(((DN_DK ** -0.5) * hd["rq"])
                            * (dqn - qh * jnp.sum(dqn * qh, axis=1, keepdims=True)))
            dk_parts.append(hd["rk"] * (dkn - kn * jnp.sum(dkn * kn, axis=1, keepdims=True)))
            dgam_col = dgam_col + dgc * (lane128 == DECAY_LANE + h).astype(F32)
            dbeta = dbeta + dbc * (lane128 == BETA_LANE + h).astype(F32)
            dgam_row = dgam_row + (row128 == DECAY_LANE + h).astype(F32) * dgr
        dnw_ref[...] += dnw
        dgam = dgam_col + _nt(f["eye"], dgam_row, hi=True)
        dg = _nn(f["upper"], dgam, hi=True)
        d_a = dg * (-f["eal"]) * jax.nn.sigmoid(f["z"]) * f["valid"]
        dal_ref[...] += jnp.sum(dg * f["g"], axis=0, keepdims=True)
        ddt_ref[...] += jnp.sum(d_a, axis=0, keepdims=True)
        d_b = dbeta * f["valid"] * f["sig"] * (1.0 - f["sig"])
        dp_ref[:, DN_CONV_CH + DN_V:DN_CONV_CH + DN_V + LANES] = (d_a + d_b).astype(dp_ref.dtype)
        dp_ref[:, DN_CONV_CH + DN_V + LANES:] = jnp.zeros((CHUNK, DN_IN_PAD - DN_IN_USED), dp_ref.dtype)
        dact = jnp.concatenate(dq_parts + dk_parts + dv_parts, axis=1)
        yc, sgc = f["yc"], f["sgc"]
        dyc = dact * (sgc * (1.0 + yc * (1.0 - sgc)))
        nxt = nxt_ref[...]
        ups = [dyc] + [_shift_up(dyc, nxt, j) for j in range(1, CONV_K)]
        dx = ups[0] * cw_ref[CONV_K - 1]
        for j in range(1, CONV_K):
            dx = dx + ups[j] * cw_ref[CONV_K - 1 - j]
        for j in range(CONV_K):
            dcw_ref[CONV_K - 1 - j] += jnp.sum(f["xin"] * ups[j], axis=0, keepdims=True)
        nxt_ref[...] = dyc[0:8]
        dp_ref[:, :DN_CONV_CH] = (dx * f["valid"]).astype(dp_ref.dtype)

    rev = lambda n: nc - 1 - n
    row_v = pl.BlockSpec((CHUNK, DN_V), lambda n: (rev(n), 0))
    vec = pl.BlockSpec((1, LANES), lambda n: (0, 0))
    return pl.pallas_call(
        body, grid=(nc,),
        in_specs=_dn_in_specs(rev, conv_saved=True) + [
            row_v, row_v,
            pl.BlockSpec((1, DN_HEADS, DN_DK, DN_DV), lambda n: (rev(n), 0, 0, 0)),
            pl.BlockSpec((1, DN_HEADS, CHUNK, CHUNK), lambda n: (rev(n), 0, 0, 0))],
        out_specs=[pl.BlockSpec((CHUNK, DN_IN_PAD), lambda n: (rev(n), 0)),
                   pl.BlockSpec((CONV_K, 1, DN_CONV_CH), lambda n: (0, 0, 0)), vec, vec,
                   pl.BlockSpec((1, DN_DV), lambda n: (0, 0))],
        out_shape=[jax.ShapeDtypeStruct((rows, DN_IN_PAD), BF16),
                   jax.ShapeDtypeStruct((CONV_K, 1, DN_CONV_CH), F32),
                   jax.ShapeDtypeStruct((1, LANES), F32), jax.ShapeDtypeStruct((1, LANES), F32),
                   jax.ShapeDtypeStruct((1, DN_DV), F32)],
        scratch_shapes=[pltpu.VMEM((DN_HEADS, DN_DK, DN_DV), F32), pltpu.VMEM((8, DN_CONV_CH), F32)],
        name="dn_bwd", compiler_params=_params("arbitrary"))(
            proj, conv_out, proj, proj, conv_w, alog, dtb, norm_w.reshape(1, DN_DV), o, dy, states, tinv)


def _train_step(x, tgt, wts, sh, idx):
    seq = x.shape[0]
    rows = -(-(seq + CHUNK) // ROW_ALIGN) * ROW_ALIGN
    tail = rows - seq - CHUNK
    h0 = jnp.concatenate([jnp.zeros((PAD, D_MODEL), F32), wts["meta_tokens"].astype(F32), x,
                          jnp.zeros((tail, D_MODEL), F32)], axis=0)
    tgt_p = jnp.concatenate([jnp.zeros((CHUNK, D_MODEL), F32), tgt, jnp.zeros((tail, D_MODEL), F32)],
                            axis=0)
    cos, sin = _rope_tables(rows // CHUNK)
    consts = _ret_consts()
    conv_w = wts["dn_conv_w"].reshape(CONV_K, 1, DN_CONV_CH)
    lane_pad = LANES - 2 * DN_HEADS
    alog = jnp.pad(wts["dn_a_log"].reshape(1, DN_HEADS), ((0, 0), (DECAY_LANE, lane_pad)))
    dtb = jnp.pad(wts["dn_dt_bias"].reshape(1, DN_HEADS), ((0, 0), (DECAY_LANE, lane_pad)))
    g = {}

    wts = dict(wts)
    hn0, (got,) = _rms_fwd(h0, wts["mix_norm_w"][0], "rms_mix0", ride=_Ride("gather", [sh["ret_w_in"]]))
    wts["ret_w_in"] = got.reshape(N_SHARD, D_MODEL, -1)
    proj0, got = _mm_cols(hn0, wts["ret_w_in"], "ret_in",
                          ride=_Ride("gather", [sh["ret_w_out"], sh["ffn_w_gate"], sh["dn_w_out"]]))
    wts["ret_w_out"] = got[0].reshape(-1, D_MODEL)
    wts["ffn_w_gate"] = got[1]
    wts["dn_w_out"] = got[2].reshape(-1, D_MODEL)
    (o0, y0, st0), got = _ret_fwd(proj0, cos, sin, consts, wts["ret_gn_w"], seq,
                                  ride=_Ride("gather", [sh["ffn_w_up"], sh["ffn_w_down"], sh["dn_w_in"]]))
    wts["ffn_w_up"], wts["ffn_w_down"] = got[0], got[1]
    n_dn = sh["dn_w_in"].shape[-1]
    dn_shards = got[2].reshape(N_SHARD, D_MODEL, n_dn)
    wts["dn_w_in"] = jnp.concatenate(
        [dn_shards[j] for j in range(N_SHARD)]
        + [jnp.zeros((D_MODEL, DN_IN_PAD - N_SHARD * n_dn), dn_shards.dtype)], axis=-1)
    h1 = _mm(y0, wts["ret_w_out"], mode="nn", name="ret_out", resid=h0)
    hn1 = _rms_fwd(h1, wts["ffn_norm_w"][0], "rms_ffn0")
    g0, u0, act0 = _ffn_up(hn1, wts["ffn_w_gate"], wts["ffn_w_up"], 0, "ffn_up0")
    h2 = _ffn_down(act0, wts["ffn_w_down"], h1, 0, "ffn_down0")
    hn2 = _rms_fwd(h2, wts["mix_norm_w"][1], "rms_mix1")
    proj1 = _mm(hn2, wts["dn_w_in"], mode="nn", name="dn_in")
    o1, y1, st1, tinv, conv1 = _dn_fwd(proj1, conv_w, alog, dtb, wts["dn_norm_w"], seq)
    h3 = _mm(y1, wts["dn_w_out"], mode="nn", name="dn_out", resid=h2)
    hn3 = _rms_fwd(h3, wts["ffn_norm_w"][1], "rms_ffn1")
    g1, u1, act1 = _ffn_up(hn3, wts["ffn_w_gate"], wts["ffn_w_up"], 1, "ffn_up1")
    h4 = _ffn_down(act1, wts["ffn_w_down"], h3, 1, "ffn_down1")

    dh4, g["final_norm_w"], loss, dh4b = _final_loss(h4, wts["final_norm_w"], tgt_p, seq, "final_loss")

    layers = wts["ffn_w_gate"].shape[1]

    ffn_names = ["ffn_w_down", "ffn_w_gate", "ffn_w_up"]

    def ffn_bwd(dh_out, dhb_out, h_mid, hn, gg, uu, act, layer, prev, ride=None, last=False):
        tag = str(layer)
        res = _ffn_down_bwd(dhb_out, wts["ffn_w_down"], gg, uu, layer, "ffn_down_bwd" + tag, ride=ride)
        (dg, du), rode = res if ride is not None else (res, None)
        d_down = _ffn_wgrad(act, [dhb_out], layer, layers, prev and prev[:1], True, "ffn_dwd" + tag)
        d_gu = _ffn_wgrad(hn, [dg, du], layer, layers, prev and prev[1:], False, "ffn_dwgu" + tag)
        grads = list(d_down) + list(d_gu)
        gs = rs_grads(ffn_names, grads) if last else None
        res = _ffn_up_bwd(dg, du, wts["ffn_w_gate"], wts["ffn_w_up"], layer, h_mid, wts["ffn_norm_w"][layer],
                          dh_out, "ffn_up_bwd" + tag, ride=_Ride("pair", gs) if last else None)
        (dh_mid, d_norm, dhb_mid), sib = res if last else (res, None)
        return dh_mid, dhb_mid, grads, d_norm, rode, gs, sib

    red = {}

    def rs_grads(names, grads):
        return [gr.reshape((N_SHARD,) + sh[n].shape) for n, gr in zip(names, grads)]

    def rs_partials(names, gs, sib):
        return [_rs_pair_add(gs[t], sib[t], idx, "rs_pair_add_" + n) for t, n in enumerate(names)]

    def rs_end(names, gs, sib, others, tag):
        mine = [_rs_final_add(gs[t], sib[t], others[t], idx, "rs_final_add_" + n) for t, n in enumerate(names)]
        red.update(zip(names, _rs_share(mine, "rs_share" + tag)))

    dh3, dh3b, ffn_grads, dfn1 = ffn_bwd(dh4, dh4b, h3, hn3, g1, u1, act1, 1, None)[:4]
    dy1 = _mm(dh3b, wts["dn_w_out"], mode="nt", name="dn_out_bwd")
    d_dn_out = _mm(y1, dh3b, mode="tn", name="dn_dwo")
    dproj1, dcw, dal, ddt, g["dn_norm_w"] = _dn_bwd(proj1, conv1, o1, dy1, st1, tinv, conv_w, alog, dtb,
                                                    wts["dn_norm_w"], seq)
    d_dn_in = _mm(hn2, dproj1, mode="tn", name="dn_dwi")
    d_dn_in = jnp.stack([d_dn_in[:, j * n_dn:(j + 1) * n_dn] for j in range(N_SHARD)])
    group1 = ["dn_w_out", "dn_w_in"]
    gs1 = rs_grads(group1, [d_dn_out, d_dn_in])
    (dh2, dmn1, dh2b), sib1 = _mm_nt_rms(dproj1, wts["dn_w_in"], h2, wts["mix_norm_w"][1], dh3, "dn_in_bwd",
                                         ride=_Ride("pair", gs1))
    g["dn_conv_w"] = dcw.reshape(CONV_K, DN_CONV_CH)
    g["dn_a_log"] = dal[0, DECAY_LANE:DECAY_LANE + DN_HEADS]
    g["dn_dt_bias"] = ddt[0, DECAY_LANE:DECAY_LANE + DN_HEADS]

    dh1, dh1b, _, dfn0, others1, gs2, sib2 = ffn_bwd(
        dh2, dh2b, h1, hn1, g0, u0, act0, 0, ffn_grads,
        ride=_Ride("chips", rs_partials(group1, gs1, sib1)), last=True)
    rs_end(group1, gs1, sib1, others1, "1")
    d_ret_out = _mm(y0, dh1b, mode="tn", name="ret_dwo")
    gs2b = rs_grads(["ret_w_out"], [d_ret_out])
    dy0, sib2b = _mm(dh1b, wts["ret_w_out"], mode="nt", name="ret_out_bwd", ride=_Ride("pair", gs2b))
    group2 = ffn_names + ["ret_w_out"]
    gs2, sib2 = gs2 + gs2b, list(sib2) + list(sib2b)
    (dproj0, g["ret_gn_w"]), others2 = _ret_bwd(proj0, o0, dy0, st0, cos, sin, consts, wts["ret_gn_w"], seq,
                                                ride=_Ride("chips", rs_partials(group2, gs2, sib2)))
    rs_end(group2, gs2, sib2, others2, "2")
    d_ret_in = _mm_cols_grad(hn0, dproj0, "ret_dwi")
    gs3 = rs_grads(["ret_w_in"], [d_ret_in])
    sib3 = _rs_pair(gs3, "rs_pair3")
    (dh0, dmn0, _), others3 = _mm_cols_t_rms(dproj0, wts["ret_w_in"], h0, wts["mix_norm_w"][0], dh1, "ret_in_bwd",
                                             ride=_Ride("chips", rs_partials(["ret_w_in"], gs3, sib3)))
    rs_end(["ret_w_in"], gs3, sib3, others3, "3")

    g["ffn_norm_w"] = jnp.concatenate([dfn0, dfn1], axis=0)
    g["mix_norm_w"] = jnp.concatenate([dmn0, dmn1], axis=0)
    g["meta_tokens"] = dh0[PAD:CHUNK]
    g["final_norm_w"] = g["final_norm_w"].reshape(D_MODEL)
    g["ret_gn_w"] = g["ret_gn_w"].reshape(RET_DV)
    g["dn_norm_w"] = g["dn_norm_w"].reshape(DN_DV)
    return loss, dh0, g, red


def _mesh_pos():
    return lax.axis_index("x"), lax.axis_index("y"), lax.axis_index("c")


def _other_chips(x, y):
    return [(1 - x, y), (x, 1 - y), (1 - x, 1 - y)]


def _remote(src, dst, send_sem, recv_sem, to):
    return pltpu.make_async_remote_copy(src_ref=src, dst_ref=dst, send_sem=send_sem, recv_sem=recv_sem,
                                        device_id=to, device_id_type=MESH)


GATHER_COPIES = 7


def _gather_phase(phase, ins, outs, send_sems, recv_sems):
    x, y, c = _mesh_pos()
    me = 2 * x + y
    chips = _other_chips(x, y)
    sibling = (x, y, 1 - c)

    def cp(t, k, src, dst, to):
        i = GATHER_COPIES * t + k
        return _remote(src, dst, send_sems.at[i], recv_sems.at[i], to)

    for t in range(len(ins)):
        own = cp(t, 0, ins[t], outs[t].at[me], sibling)
        if phase == 0:
            own.start()
        if phase == 2:
            own.wait()
        for k, (px, py) in enumerate(chips):
            landed = outs[t].at[2 * px + py, c]
            theirs = outs[t].at[2 * px + py, 1 - c]
            to_chip = cp(t, 1 + k, ins[t].at[c], outs[t].at[me, c], (px, py, c))
            if phase == 0:
                to_chip.start()
            if phase == 1:
                cp(t, 1 + k, ins[t].at[c], landed, (px, py, c)).wait_recv()
                cp(t, 4 + k, landed, landed, sibling).start()
            if phase == 2:
                to_chip.wait_send()
                cp(t, 4 + k, landed, landed, sibling).wait_send()
                cp(t, 4 + k, theirs, theirs, sibling).wait_recv()


def _chips_phase(phase, ins, outs, send_sems, recv_sems):
    x, y, c = _mesh_pos()
    for t in range(len(ins)):
        for k, (px, py) in enumerate(_other_chips(x, y)):
            cp = _remote(ins[t].at[2 * px + py], outs[t].at[k], send_sems.at[3 * t + k], recv_sems.at[3 * t + k],
                         (px, py, c))
            if phase == 0:
                cp.start()
            if phase == 2:
                cp.wait()


class _Ride:
    def __init__(self, kind, arrays):
        self.kind, self.arrays = kind, list(arrays)
        nt = len(self.arrays)
        if kind == "gather":
            self.phase_fn, n_sem = _gather_phase, GATHER_COPIES * nt
            self.out_shape = [jax.ShapeDtypeStruct((N_SHARD,) + a.shape, a.dtype) for a in self.arrays]
        elif kind == "pair":
            self.phase_fn, n_sem = _pair_phase, nt
            self.out_shape = [jax.ShapeDtypeStruct(a.shape[:1] + a.shape[2:], a.dtype) for a in self.arrays]
        else:
            self.phase_fn, n_sem = _chips_phase, 3 * nt
            self.out_shape = [jax.ShapeDtypeStruct((3,) + a.shape[1:], a.dtype) for a in self.arrays]
        self.in_specs, self.out_specs = [ANY] * nt, [ANY] * nt
        self.scratch = [pltpu.SemaphoreType.DMA((n_sem,)), pltpu.SemaphoreType.DMA((n_sem,))]

    def emit(self, step, nsteps, ins, outs, send_sems, recv_sems, before):
        mid = max(0, min((7 * nsteps) // 8, nsteps - 2))
        todo = [(0, 0), (1, mid)] if before else [(2, nsteps - 1)]
        for phase, at in todo:
            if phase == 1 and self.kind != "gather":
                continue

            @pl.when(step == at)
            def _(phase=phase):
                self.phase_fn(phase, ins, outs, send_sems, recv_sems)


def _gather_small(blk):
    r, wd = blk.shape

    def body(b_ref, out_ref, send_sems, recv_sems):
        x, y, c = _mesh_pos()
        chips = _other_chips(x, y)
        out_ref[2 * x + y] = b_ref[...]
        sends = [_remote(b_ref, out_ref.at[2 * x + y], send_sems.at[k], recv_sems.at[k], (px, py, c))
                 for k, (px, py) in enumerate(chips)]
        for cp in sends:
            cp.start()
        for k, (px, py) in enumerate(chips):
            _remote(b_ref, out_ref.at[2 * px + py], send_sems.at[k], recv_sems.at[k], (px, py, c)).wait_recv()
        for cp in sends:
            cp.wait_send()

    return pl.pallas_call(
        body, out_shape=jax.ShapeDtypeStruct((4, r, wd), blk.dtype), in_specs=[VMEM_SPEC], out_specs=VMEM_SPEC,
        scratch_shapes=[pltpu.SemaphoreType.DMA((3,)), pltpu.SemaphoreType.DMA((3,))],
        name="gather_small")(blk)


def _allreduce_small(blk):
    r, wd = blk.shape
    rels = [(dx, dy, dc) for dx in (0, 1) for dy in (0, 1) for dc in (0, 1) if dx or dy or dc]

    def body(b_ref, out_ref, buf_ref, send_sems, recv_sems):
        x, y, c = _mesh_pos()

        def peer(rel):
            dx, dy, dc = rel
            return (1 - x if dx else x, 1 - y if dy else y, 1 - c if dc else c)

        me = 4 * x + 2 * y + c
        buf_ref[me] = b_ref[...]
        sends = [_remote(b_ref, buf_ref.at[me], send_sems.at[k], recv_sems.at[k], peer(rel))
                 for k, rel in enumerate(rels)]
        for cp in sends:
            cp.start()
        for k, rel in enumerate(rels):
            px, py, pc = peer(rel)
            _remote(b_ref, buf_ref.at[4 * px + 2 * py + pc], send_sems.at[k], recv_sems.at[k],
                    (px, py, pc)).wait_recv()
        for cp in sends:
            cp.wait_send()
        acc = buf_ref[0]
        for d in range(1, 8):
            acc = acc + buf_ref[d]
        out_ref[...] = acc

    return pl.pallas_call(
        body, out_shape=jax.ShapeDtypeStruct((r, wd), blk.dtype), in_specs=[VMEM_SPEC], out_specs=VMEM_SPEC,
        scratch_shapes=[pltpu.VMEM((8, r, wd), blk.dtype), pltpu.SemaphoreType.DMA((7,)),
                        pltpu.SemaphoreType.DMA((7,))],
        name="allreduce_small")(blk)


def _rs_pair(gs, name):
    ride = _Ride("pair", gs)

    def body(*refs):
        nt = len(gs)
        for phase in (0, 2):
            _pair_phase(phase, refs[:nt], refs[nt:2 * nt], *refs[2 * nt:])

    return pl.pallas_call(body, out_shape=ride.out_shape, in_specs=ride.in_specs, out_specs=ride.out_specs,
                          scratch_shapes=ride.scratch, name=name)(*gs)


def _pair_phase(phase, ins, outs, send_sems, recv_sems):
    x, y, c = _mesh_pos()
    for t in range(len(ins)):
        cp = _remote(ins[t].at[:, 1 - c], outs[t], send_sems.at[t], recv_sems.at[t], (x, y, 1 - c))
        if phase == 0:
            cp.start()
        if phase == 2:
            cp.wait()


def _rs_tile(a, b):
    return _div_tile(a, 512 if b <= 1024 else 256, 16)


def _rs_pair_add(g, a, idx, name):
    _, _, rows, cols = g.shape
    tr = _rs_tile(rows, cols)

    def body(s_ref, g_ref, a_ref, p_ref):
        p_ref[...] = (g_ref[...] + a_ref[...]).astype(p_ref.dtype)

    blk = pl.BlockSpec((None, tr, cols), lambda j, i, s: (j, i, 0))
    spec = pltpu.PrefetchScalarGridSpec(
        num_scalar_prefetch=1, grid=(N_SHARD, rows // tr),
        in_specs=[pl.BlockSpec((None, None, tr, cols), lambda j, i, s: (j, s[0], i, 0)), blk], out_specs=blk)
    return pl.pallas_call(
        body, grid_spec=spec, out_shape=jax.ShapeDtypeStruct((N_SHARD, rows, cols), BF16), name=name,
        compiler_params=_params("parallel", "parallel"))(idx, g, a)


def _rs_final_add(g, a, b, idx, name):
    _, _, rows, cols = g.shape
    tr = _rs_tile(rows, cols)

    def body(s_ref, g_ref, a_ref, b0_ref, b1_ref, b2_ref, f_ref):
        own = g_ref[...] + a_ref[...]
        f_ref[...] = ((own + b0_ref[...].astype(F32)) + b1_ref[...].astype(F32)) + b2_ref[...].astype(F32)

    def b_spec(k):
        return pl.BlockSpec((None, tr, cols), lambda i, s: (k, i, 0))

    spec = pltpu.PrefetchScalarGridSpec(
        num_scalar_prefetch=1, grid=(rows // tr,),
        in_specs=[pl.BlockSpec((None, None, tr, cols), lambda i, s: (s[1], s[0], i, 0)),
                  pl.BlockSpec((None, tr, cols), lambda i, s: (s[1], i, 0)), b_spec(0), b_spec(1), b_spec(2)],
        out_specs=pl.BlockSpec((None, tr, cols), lambda i, s: (s[0], i, 0)))
    return pl.pallas_call(
        body, grid_spec=spec, out_shape=jax.ShapeDtypeStruct((2, rows, cols), F32), name=name,
        compiler_params=_params("parallel"))(idx, g, a, b, b, b)


def _rs_share(fs, name):
    nt = len(fs)

    def body(*refs):
        outs = refs[nt:2 * nt]
        send_sems, recv_sems = refs[2 * nt:]
        x, y, c = _mesh_pos()
        cps = [_remote(outs[t].at[c], outs[t].at[c], send_sems.at[t], recv_sems.at[t], (x, y, 1 - c))
               for t in range(nt)]
        for cp in cps:
            cp.start()
        for cp in cps:
            cp.wait()

    return pl.pallas_call(
        body, out_shape=[jax.ShapeDtypeStruct(f.shape, f.dtype) for f in fs],
        in_specs=[ANY] * nt, out_specs=[ANY] * nt, input_output_aliases={t: t for t in range(nt)},
        scratch_shapes=[pltpu.SemaphoreType.DMA((nt,)), pltpu.SemaphoreType.DMA((nt,))], name=name)(*fs)


def _adamw(w, g, m, v, name):
    lead, rows, cols = w.shape
    tr = rows // 4 if rows % 32 == 0 else rows

    def body(w_ref, g_ref, m_ref, v_ref, go_ref, d_ref, mo_ref, vo_ref):
        gv = g_ref[...]
        go_ref[...] = gv
        mn = ADAM_B1 * m_ref[...] + (1.0 - ADAM_B1) * gv
        vn = ADAM_B2 * v_ref[...] + (1.0 - ADAM_B2) * (gv * gv)
        m_hat = mn / (1.0 - ADAM_B1 ** ADAM_STEP)
        v_hat = vn / (1.0 - ADAM_B2 ** ADAM_STEP)
        d_ref[...] = -ADAM_LR * (m_hat / (jnp.sqrt(v_hat) + ADAM_EPS) + ADAM_WD * w_ref[...])
        mo_ref[...] = mn
        vo_ref[...] = vn

    blk = pl.BlockSpec((None, tr, cols), lambda l, i: (l, i, 0))
    out = jax.ShapeDtypeStruct((lead, rows, cols), F32)
    return pl.pallas_call(
        body, grid=(lead, rows // tr), in_specs=[blk] * 4, out_specs=[blk] * 4, out_shape=[out] * 4, name=name,
        compiler_params=_params("parallel", "parallel"))(w, g, m, v)


BIG = ["ret_w_in", "ret_w_out", "dn_w_in", "dn_w_out", "ffn_w_gate", "ffn_w_up", "ffn_w_down"]
TRANSPOSED_AT_BOUNDARY = {"dn_w_in": True, "ffn_w_gate": False, "ffn_w_up": False}
SMALL =["meta_tokens", "mix_norm_w", "ffn_norm_w", "ret_gn_w", "dn_conv_w", "dn_a_log", "dn_dt_bias",
         "dn_norm_w", "final_norm_w"]
SMALL_SHARDED = {"meta_tokens", "dn_conv_w", "dn_norm_w"}
ORDER = ["meta_tokens", "mix_norm_w", "ffn_norm_w", "ret_w_in", "ret_gn_w", "ret_w_out", "dn_w_in",
         "dn_conv_w", "dn_a_log", "dn_dt_bias", "dn_norm_w", "dn_w_out", "ffn_w_gate", "ffn_w_up",
         "ffn_w_down", "final_norm_w"]


def _halves(a):
    return a.reshape(2, -1, a.shape[-1])


def _pack_lanes(parts, align=8):
    flat = jnp.concatenate([p.reshape(-1) for p in parts])
    flat = jnp.pad(flat, (0, -flat.shape[0] % (align * LANES)))
    return flat.reshape(-1, LANES)


def _unpack(buf, shapes):
    lead = buf.shape[:-2]
    flat = buf.reshape(lead + (-1,))
    out, off = [], 0
    for shp in shapes:
        size = math.prod(shp)
        out.append(flat[..., off:off + size].reshape(lead + tuple(shp)))
        off += size
    return out


def _join_cols(shards):
    return jnp.concatenate([shards[j] for j in range(N_SHARD)], axis=-1)


def kernel(x, meta_tokens, mix_norm_w, ffn_norm_w, ret_w_in, ret_gn_w, ret_w_out, dn_w_in, dn_conv_w, dn_a_log, dn_dt_bias, dn_norm_w, dn_w_out, ffn_w_gate, ffn_w_up, ffn_w_down, final_norm_w, loss_target, m_meta_tokens, m_mix_norm_w, m_ffn_norm_w, m_ret_w_in, m_ret_gn_w, m_ret_w_out, m_dn_w_in, m_dn_conv_w, m_dn_a_log, m_dn_dt_bias, m_dn_norm_w, m_dn_w_out, m_ffn_w_gate, m_ffn_w_up, m_ffn_w_down, m_final_norm_w, v_meta_tokens, v_mix_norm_w, v_ffn_norm_w, v_ret_w_in, v_ret_gn_w, v_ret_w_out, v_dn_w_in, v_dn_conv_w, v_dn_a_log, v_dn_dt_bias, v_dn_norm_w, v_dn_w_out, v_ffn_w_gate, v_ffn_w_up, v_ffn_w_down, v_final_norm_w):
    w = dict(meta_tokens=meta_tokens, mix_norm_w=mix_norm_w, ffn_norm_w=ffn_norm_w, ret_w_in=ret_w_in,
             ret_gn_w=ret_gn_w, ret_w_out=ret_w_out, dn_w_in=dn_w_in, dn_conv_w=dn_conv_w, dn_a_log=dn_a_log,
             dn_dt_bias=dn_dt_bias, dn_norm_w=dn_norm_w, dn_w_out=dn_w_out, ffn_w_gate=ffn_w_gate,
             ffn_w_up=ffn_w_up, ffn_w_down=ffn_w_down, final_norm_w=final_norm_w)
    m = dict(meta_tokens=m_meta_tokens, mix_norm_w=m_mix_norm_w, ffn_norm_w=m_ffn_norm_w, ret_w_in=m_ret_w_in,
             ret_gn_w=m_ret_gn_w, ret_w_out=m_ret_w_out, dn_w_in=m_dn_w_in, dn_conv_w=m_dn_conv_w,
             dn_a_log=m_dn_a_log, dn_dt_bias=m_dn_dt_bias, dn_norm_w=m_dn_norm_w, dn_w_out=m_dn_w_out,
             ffn_w_gate=m_ffn_w_gate, ffn_w_up=m_ffn_w_up, ffn_w_down=m_ffn_w_down, final_norm_w=m_final_norm_w)
    v = dict(meta_tokens=v_meta_tokens, mix_norm_w=v_mix_norm_w, ffn_norm_w=v_ffn_norm_w, ret_w_in=v_ret_w_in,
             ret_gn_w=v_ret_gn_w, ret_w_out=v_ret_w_out, dn_w_in=v_dn_w_in, dn_conv_w=v_dn_conv_w,
             dn_a_log=v_dn_a_log, dn_dt_bias=v_dn_dt_bias, dn_norm_w=v_dn_norm_w, dn_w_out=v_dn_w_out,
             ffn_w_gate=v_ffn_w_gate, ffn_w_up=v_ffn_w_up, ffn_w_down=v_ffn_w_down, final_norm_w=v_final_norm_w)
    mx, my, mc = _mesh_pos()
    chip = 2 * mx + my

    sm_names = [n for n in SMALL if n in SMALL_SHARDED]
    sm_gathered = _unpack(_gather_small(_pack_lanes([w[n] for n in sm_names])), [w[n].shape for n in sm_names])
    full = {n: _join_cols(sm_gathered[i]) for i, n in enumerate(sm_names)}
    wts = {
        "meta_tokens": full["meta_tokens"], "mix_norm_w": mix_norm_w, "ffn_norm_w": ffn_norm_w,
        "ret_gn_w": ret_gn_w[0], "final_norm_w": final_norm_w, "dn_conv_w": full["dn_conv_w"][0],
        "dn_a_log": dn_a_log[0], "dn_dt_bias": dn_dt_bias[0], "dn_norm_w": full["dn_norm_w"][0],
    }
    idx = jnp.stack([mc, chip]).astype(jnp.int32)
    shards = {n: _halves(w[n].astype(MXU_DTYPE)) for n in BIG}
    loss_part, dh0, g, reduced = _train_step(x[0], loss_target[0], wts, shards, idx)
    seq = x.shape[1]
    grad_x = dh0[CHUNK:CHUNK + seq].reshape(x.shape)
    gsh = {}

    small_full_shapes = [g[n].shape for n in SMALL] + [(1,)]
    red = _unpack(_allreduce_small(_pack_lanes([g[n] for n in SMALL] + [loss_part[0, :1]])), small_full_shapes)
    loss = red[-1][0]
    for i, n in enumerate(SMALL):
        gn = red[i]
        if n in SMALL_SHARDED:
            width = w[n].shape[-1]
            gn = lax.dynamic_slice_in_dim(gn, chip * width, width, axis=gn.ndim - 1)
        gsh[n] = gn.reshape(w[n].shape)

    delta, new_m, new_v = {}, {}, {}
    for n in BIG:
        shp = w[n].shape
        if n in TRANSPOSED_AT_BOUNDARY and TRANSPOSED_AT_BOUNDARY[n]:
            view = lambda a: jnp.swapaxes(a, 1, 2).reshape(1, -1, LANES)
            back = lambda a: jnp.swapaxes(a.reshape(shp[0], shp[2], shp[1]), 1, 2)
        elif n in TRANSPOSED_AT_BOUNDARY:
            view = back = lambda a: jnp.swapaxes(a, 1, 2)
        else:
            view = back = lambda a: a
        res = _adamw(view(w[n]), view(reduced[n].reshape(shp)), view(m[n]), view(v[n]), "adamw_" + n)
        gsh[n], delta[n], new_m[n], new_v[n] = [back(r) for r in res]
    sm_local_shapes = [w[n].shape for n in SMALL]
    _, d_, m_, v_ = _adamw(*[_pack_lanes([t[n] for n in SMALL])[None] for t in (w, gsh, m, v)], "adamw_small")
    d_, m_, v_ = d_[0], m_[0], v_[0]
    for n, dd, mm, vv in zip(SMALL, _unpack(d_, sm_local_shapes), _unpack(m_, sm_local_shapes),
                             _unpack(v_, sm_local_shapes)):
        delta[n], new_m[n], new_v[n] = dd, mm, vv

    return (loss, grad_x, *[gsh[n] for n in ORDER], *[delta[n] for n in ORDER],
            *[new_m[n] for n in ORDER], *[new_v[n] for n in ORDER])
```

```python
import functools
import math

import jax
import jax.numpy as jnp
from jax import lax
from jax.experimental import pallas as pl
from jax.experimental.pallas import tpu as pltpu

F32 = jnp.float32
BF16 = jnp.bfloat16
MXU_DTYPE = BF16

D_MODEL = 1024
N_META = 16
CHUNK = 64
PAD = CHUNK - N_META
RMS_EPS = 1e-6
RET_HEADS, RET_DK, RET_DV = 4, 256, 512
RET_QK, RET_V = RET_HEADS * RET_DK, RET_HEADS * RET_DV
RET_IN = 2 * RET_QK + 2 * RET_V
ROPE_BASE = 10000.0
DN_HEADS, DN_DK, DN_DV = 8, 128, 256
DN_QK, DN_V = DN_HEADS * DN_DK, DN_HEADS * DN_DV
DN_CONV_CH = 2 * DN_QK + DN_V
DN_IN = DN_CONV_CH + DN_V + 2 * DN_HEADS
LANES = 128
DN_IN_USED = DN_CONV_CH + DN_V + LANES
DN_IN_PAD = DN_IN_USED + LANES
CONV_K = 4
FFN_HIDDEN = 2816
ADAM_LR, ADAM_B1, ADAM_B2, ADAM_EPS, ADAM_WD, ADAM_STEP = 0.001, 0.9, 0.999, 1e-08, 0.01, 10

ROW_ALIGN = 256
VMEM_LIMIT = 56 * 1024 * 1024
MESH = pl.DeviceIdType.MESH
ANY = pl.BlockSpec(memory_space=pl.ANY)
VMEM_SPEC = pl.BlockSpec(memory_space=pltpu.VMEM)
_HI = lax.Precision.HIGHEST


def _params(*sem):
    return pltpu.CompilerParams(dimension_semantics=sem, vmem_limit_bytes=VMEM_LIMIT)


def _dg(a, b, ca, cb, hi):
    dims = (((ca,), (cb,)), ((), ()))

    def dot(p, q):
        return lax.dot_general(p, q, dims, preferred_element_type=F32)

    if not hi:
        return dot(a.astype(MXU_DTYPE), b.astype(MXU_DTYPE))
    if MXU_DTYPE == F32:
        return lax.dot_general(a, b, dims, precision=_HI, preferred_element_type=F32)
    a_hi, b_hi = a.astype(MXU_DTYPE), b.astype(MXU_DTYPE)
    a_lo = (a - a_hi.astype(F32)).astype(MXU_DTYPE)
    b_lo = (b - b_hi.astype(F32)).astype(MXU_DTYPE)
    return dot(a_hi, b_hi) + (dot(a_hi, b_lo) + dot(a_lo, b_hi))


def _nn(a, b, hi=False):
    return _dg(a, b, 1, 0, hi)


def _nt(a, b, hi=False):
    return _dg(a, b, 1, 1, hi)


def _tn(a, b, hi=False):
    return _dg(a, b, 0, 0, hi)


def _iota(shape, dim):
    return lax.broadcasted_iota(jnp.int32, shape, dim)


def _valid_rows(first_row, rows, seq):
    r = first_row + _iota((rows, 1), 0)
    return ((r >= PAD) & (r < CHUNK + seq)).astype(F32)


def _rope(t, cs, sn):
    half = t.shape[-1] // 2
    t1, t2 = t[:, :half], t[:, half:]
    return jnp.concatenate([t1 * cs - t2 * sn, t1 * sn + t2 * cs], axis=1)


def _rope_bwd(d, cs, sn):
    half = d.shape[-1] // 2
    d1, d2 = d[:, :half], d[:, half:]
    return jnp.concatenate([d1 * cs + d2 * sn, d2 * cs - d1 * sn], axis=1)


def _col(x, idx):
    oh = (_iota((1, x.shape[1]), 1) == idx).astype(F32)
    return jnp.sum(x * oh, axis=1, keepdims=True)


def _row(x, idx):
    oh = (_iota((x.shape[0], 1), 0) == idx).astype(F32)
    return jnp.sum(x * oh, axis=0, keepdims=True)


def _shift_down(x, halo8, k):
    xr = pltpu.roll(x, k, 0)
    hr = pltpu.roll(halo8, k, 0)
    first = jnp.where(_iota((8, 1), 0) < k, hr, xr[0:8])
    return jnp.concatenate([first, xr[8:]], axis=0)


def _shift_up(x, next8, j):
    rows = x.shape[0]
    xr = pltpu.roll(x, rows - j, 0)
    nr = pltpu.roll(next8, 8 - j, 0)
    last = jnp.where(_iota((8, 1), 0) >= 8 - j, nr, xr[rows - 8:])
    return jnp.concatenate([xr[:rows - 8], last], axis=0)


def _gated_norm(o, gate, w):
    r = lax.rsqrt(jnp.mean(o * o, axis=-1, keepdims=True) + RMS_EPS)
    return o * r * w * (gate * jax.nn.sigmoid(gate))


def _gated_norm_bwd(dy, o, gate, w):
    r = lax.rsqrt(jnp.mean(o * o, axis=-1, keepdims=True) + RMS_EPS)
    nrm = o * r
    sg = jax.nn.sigmoid(gate)
    sl = gate * sg
    dgate = dy * nrm * w * (sg * (1.0 + gate * (1.0 - sg)))
    dn = dy * w * sl
    dw = jnp.sum(dy * nrm * sl, axis=0, keepdims=True)
    do = r * (dn - nrm * jnp.mean(dn * nrm, axis=-1, keepdims=True))
    return do, dgate, dw


def _softplus(z):
    return jnp.maximum(z, 0.0) + jnp.log(1.0 + jnp.exp(-jnp.abs(z)))


def _row_tile(rows, cap=768):
    for t in (768, 512, 256, 128, 64, 32, 16, 8):
        if t <= cap and rows % t == 0:
            return t
    return rows


TILE_BUDGET = 44 * 1024 * 1024


def _fit_rows(rows, row_bytes, fixed_bytes, value_row_bytes):
    best = None
    for t in range(LANES, rows + 1, LANES):
        if rows % t == 0 and 2 * (row_bytes * t + fixed_bytes) + value_row_bytes * t <= TILE_BUDGET:
            best = t
    return best or _row_tile(rows, 256)


def _div_tile(n, cap, mult):
    best = None
    for t in range(mult, min(cap, n) + 1, mult):
        if n % t == 0:
            best = t
    return best or n


def _col_tile(cols, cap=1536):
    best = None
    for t in range(LANES, min(cap, cols) + 1, LANES):
        if cols % t == 0:
            best = t
    return best or cols


def _rms_fwd(h, w, name, ride=None):
    rows, d = h.shape
    tm = _row_tile(rows)

    def body(h_ref, w_ref, o_ref):
        x = h_ref[...]
        r = lax.rsqrt(jnp.mean(x * x, axis=-1, keepdims=True) + RMS_EPS)
        o_ref[...] = (x * r * w_ref[...]).astype(o_ref.dtype)

    res, rode = _pcall(body, [h, w.reshape(1, d)], grid=(rows // tm,),
                       in_specs=[pl.BlockSpec((tm, d), lambda i: (i, 0)), pl.BlockSpec((1, d), lambda i: (0, 0))],
                       out_specs=[pl.BlockSpec((tm, d), lambda i: (i, 0))],
                       out_shape=[jax.ShapeDtypeStruct((rows, d), BF16)], name=name, sem=("parallel",), ride=ride)
    return res[0] if ride is None else (res[0], rode)


def _gmm_rms(name, grid, args, in_specs, row_spec, fn, h, w, resid, row_axis, red_axis=None, ride=None):
    m, d = h.shape
    n_in = len(args)
    vec = pl.BlockSpec((1, d), lambda *g: (0, 0))

    def body(*refs):
        ins = refs[:n_in]
        h_ref, w_ref, r_ref, dh_ref, dw_ref, dh16_ref = refs[n_in:]
        part = fn(*ins)
        row = pl.program_id(row_axis)

        def finish(dy):
            x = h_ref[...]
            r = lax.rsqrt(jnp.mean(x * x, axis=-1, keepdims=True) + RMS_EPS)
            xh = x * r
            dxh = dy * w_ref[...]
            dh = r_ref[...] + r * (dxh - xh * jnp.mean(dxh * xh, axis=-1, keepdims=True))
            dh_ref[...] = dh
            dh16_ref[...] = dh.astype(dh16_ref.dtype)
            dwp = jnp.sum(dy * xh, axis=0, keepdims=True)

            @pl.when(row == 0)
            def _():
                dw_ref[...] = dwp

            @pl.when(row > 0)
            def _():
                dw_ref[...] += dwp

        if red_axis is None:
            finish(part)
            return
        k = pl.program_id(red_axis)

        @pl.when(k == 0)
        def _():
            dh_ref[...] = part

        @pl.when(k > 0)
        def _():
            dh_ref[...] += part

        @pl.when(k == grid[red_axis] - 1)
        def _():
            finish(dh_ref[...])

    res, rode = _pcall(body, list(args) + [h, w.reshape(1, d), resid], grid=grid,
                       in_specs=list(in_specs) + [row_spec, vec, row_spec], out_specs=[row_spec, vec, row_spec],
                       out_shape=[jax.ShapeDtypeStruct((m, d), F32), jax.ShapeDtypeStruct((1, d), F32),
                                  jax.ShapeDtypeStruct((m, d), BF16)],
                       name=name, sem=("arbitrary",) * len(grid), ride=ride)
    return res if ride is None else (res, rode)


def _final_loss(h, w, tgt, seq, name):
    rows, d = h.shape
    tm = _row_tile(rows)

    def body(h_ref, w_ref, t_ref, dh_ref, dw_ref, loss_ref, dh16_ref):
        i = pl.program_id(0)
        r_idx = i * tm + _iota((tm, 1), 0)
        m = ((r_idx >= CHUNK) & (r_idx < CHUNK + seq)).astype(F32)
        x = h_ref[...]
        wv = w_ref[...]
        r = lax.rsqrt(jnp.mean(x * x, axis=-1, keepdims=True) + RMS_EPS)
        xh = x * r
        err = (xh * wv - t_ref[...]) * m
        lpart = 0.5 * jnp.sum(jnp.mean(err * err, axis=-1, keepdims=True), axis=0, keepdims=True)
        dyv = err * (1.0 / d)
        dxh = dyv * wv
        dh = r * (dxh - xh * jnp.mean(dxh * xh, axis=-1, keepdims=True))
        dh_ref[...] = dh
        dh16_ref[...] = dh.astype(dh16_ref.dtype)
        part = jnp.sum(dyv * xh, axis=0, keepdims=True)

        @pl.when(i == 0)
        def _():
            dw_ref[...] = part
            loss_ref[...] = jnp.broadcast_to(lpart, loss_ref.shape)

        @pl.when(i > 0)
        def _():
            dw_ref[...] += part
            loss_ref[...] += jnp.broadcast_to(lpart, loss_ref.shape)

    blk = pl.BlockSpec((tm, d), lambda i: (i, 0))
    vec = pl.BlockSpec((1, d), lambda i: (0, 0))
    return pl.pallas_call(
        body, grid=(rows // tm,), in_specs=[blk, vec, blk],
        out_specs=[blk, vec, pl.BlockSpec((1, LANES), lambda i: (0, 0)), blk],
        out_shape=[jax.ShapeDtypeStruct((rows, d), F32), jax.ShapeDtypeStruct((1, d), F32),
                   jax.ShapeDtypeStruct((1, LANES), F32), jax.ShapeDtypeStruct((rows, d), BF16)],
        name=name, compiler_params=_params("arbitrary"))(h, w.reshape(1, d), tgt)


def _isz(x):
    return jnp.dtype(x.dtype).itemsize


def _mm(a, b, *, mode, name, out_dtype=F32, resid=None, col_cap=1536, ride=None):
    if mode == "tn":
        m, k = a.shape
        n = b.shape[1]
        tn = _col_tile(n, col_cap)
        tm = _fit_rows(m, k * _isz(a) + tn * _isz(b), (3 * k * tn * 4) // 2, 2 * (k + tn))

        def body_tn(a_ref, b_ref, o_ref):
            i = pl.program_id(1)
            part = _tn(a_ref[...], b_ref[...])

            @pl.when(i == 0)
            def _():
                o_ref[...] = part

            @pl.when(i > 0)
            def _():
                o_ref[...] += part

        return pl.pallas_call(
            body_tn, grid=(n // tn, m // tm),
            in_specs=[pl.BlockSpec((tm, k), lambda j, i: (i, 0)),
                      pl.BlockSpec((tm, tn), lambda j, i: (i, j))],
            out_specs=pl.BlockSpec((k, tn), lambda j, i: (0, j)),
            out_shape=jax.ShapeDtypeStruct((k, n), F32), name=name,
            compiler_params=_params("parallel", "arbitrary"))(a, b)

    m, ka = a.shape
    n = b.shape[1] if mode == "nn" else b.shape[0]
    has_resid = resid is not None
    tn = _col_tile(n, col_cap)
    tm = _fit_rows(m, ka * _isz(a) + tn * (jnp.dtype(out_dtype).itemsize + (4 if has_resid else 0)),
                   ka * tn * _isz(b), 2 * ka + 8 * tn)

    def body(*refs):
        if has_resid:
            a_ref, b_ref, r_ref, o_ref = refs
        else:
            a_ref, b_ref, o_ref = refs
        acc = _nn(a_ref[...], b_ref[...]) if mode == "nn" else _nt(a_ref[...], b_ref[...])
        if has_resid:
            acc = acc + r_ref[...]
        o_ref[...] = acc.astype(o_ref.dtype)

    b_spec = (pl.BlockSpec((b.shape[0], tn), lambda j, i: (0, j)) if mode == "nn"
              else pl.BlockSpec((tn, b.shape[1]), lambda j, i: (j, 0)))
    o_spec = pl.BlockSpec((tm, tn), lambda j, i: (i, j))
    in_specs = [pl.BlockSpec((tm, ka), lambda j, i: (i, 0)), b_spec]
    args = [a, b]
    if has_resid:
        in_specs.append(o_spec)
        args.append(resid)
    res, rode = _pcall(body, args, grid=(n // tn, m // tm), in_specs=in_specs, out_specs=[o_spec],
                       out_shape=[jax.ShapeDtypeStruct((m, n), out_dtype)], name=name,
                       sem=("parallel", "parallel"), ride=ride)
    return res[0] if ride is None else (res[0], rode)


N_SHARD = 4


def _gmm(name, grid, args, in_specs, out_specs, out_shape, fn, red_axis=None, init_arg=None, aliases=None,
         ride=None):
    n_in = len(args)
    single = not isinstance(out_shape, (list, tuple))
    out_specs = [out_specs] if single else list(out_specs)
    out_shape = [out_shape] if single else list(out_shape)

    def body(*refs):
        _gmm_step(fn, refs[:n_in], refs[n_in:], red_axis, init_arg)

    sem = tuple("arbitrary" if ax == red_axis else "parallel" for ax in range(len(grid)))
    res, rode = _pcall(body, args, grid=grid, in_specs=in_specs, out_specs=out_specs, out_shape=out_shape,
                       name=name, sem=sem, aliases=aliases, ride=ride)
    ours = res[0] if single else res
    return ours if ride is None else (ours, rode)


def _gmm_step(fn, ins, outs, red_axis, init_arg):
    parts = fn(*ins)
    if red_axis is None:
        for o_ref, p in zip(outs, parts):
            o_ref[...] = p.astype(o_ref.dtype)
        return
    k = pl.program_id(red_axis)

    @pl.when(k == 0)
    def _():
        for idx, (o_ref, p) in enumerate(zip(outs, parts)):
            o_ref[...] = p + ins[init_arg][...] if (idx == 0 and init_arg is not None) else p

    @pl.when(k > 0)
    def _():
        for o_ref, p in zip(outs, parts):
            o_ref[...] += p


def _ride_body(ride, grid, n_in, n_out, n_scratch, body):
    n_rin, n_rout = len(ride.arrays), len(ride.out_shape)
    nsteps = math.prod(grid)

    def wrapped(*refs):
        ins = refs[:n_in]
        r_ins = refs[n_in:n_in + n_rin]
        o0 = n_in + n_rin
        outs = refs[o0:o0 + n_out]
        r_outs = refs[o0 + n_out:o0 + n_out + n_rout]
        s0 = o0 + n_out + n_rout
        scratch = refs[s0:s0 + n_scratch]
        send_sems, recv_sems = refs[-2:]
        step = pl.program_id(0)
        for ax in range(1, len(grid)):
            step = step * grid[ax] + pl.program_id(ax)
        ride.emit(step, nsteps, r_ins, r_outs, send_sems, recv_sems, before=True)
        body(*ins, *outs, *scratch)
        ride.emit(step, nsteps, r_ins, r_outs, send_sems, recv_sems, before=False)

    return wrapped


def _pcall(body, args, *, grid, in_specs, out_specs, out_shape, name, sem, scratch=(), aliases=None, ride=None):
    if ride is None:
        res = pl.pallas_call(body, grid=grid, in_specs=list(in_specs), out_specs=list(out_specs),
                             out_shape=list(out_shape), scratch_shapes=list(scratch), name=name,
                             input_output_aliases=aliases or {}, compiler_params=_params(*sem))(*args)
        return res, None
    n_in, n_out = len(args), len(out_shape)
    res = pl.pallas_call(
        _ride_body(ride, grid, n_in, n_out, len(scratch), body), grid=grid,
        in_specs=list(in_specs) + ride.in_specs, out_specs=list(out_specs) + ride.out_specs,
        out_shape=list(out_shape) + ride.out_shape, scratch_shapes=list(scratch) + ride.scratch, name=name,
        input_output_aliases=aliases or {},
        compiler_params=_params(*(("arbitrary",) * len(grid))))(*args, *ride.arrays)
    return res[:n_out], res[n_out:]


def _mm_cols(a, ws, name, ride=None):
    m, k = a.shape
    n = ws.shape[2]
    tm = _fit_rows(m, k * _isz(a) + n * 4, k * n * _isz(ws), 4 * n)
    return _gmm(name, (N_SHARD, m // tm), [a, ws],
                [pl.BlockSpec((tm, k), lambda j, i: (i, 0)), pl.BlockSpec((None, k, n), lambda j, i: (j, 0, 0))],
                pl.BlockSpec((tm, n), lambda j, i: (i, j)), jax.ShapeDtypeStruct((m, N_SHARD * n), F32),
                lambda a_ref, w_ref: (_nn(a_ref[...], w_ref[...]),), ride=ride)


def _mm_cols_t_rms(d, ws, h, w, resid, name, ride=None):
    m = d.shape[0]
    _, k, n = ws.shape
    tm = _fit_rows(m, n * _isz(d) + 3 * k * 4, k * n * _isz(ws), 16 * k)
    return _gmm_rms(name, (m // tm, N_SHARD), [d, ws],
                    [pl.BlockSpec((tm, n), lambda i, j: (i, j)), pl.BlockSpec((None, k, n), lambda i, j: (j, 0, 0))],
                    pl.BlockSpec((tm, k), lambda i, j: (i, 0)),
                    lambda d_ref, w_ref: _nt(d_ref[...], w_ref[...]), h, w, resid, 0, red_axis=1, ride=ride)


def _mm_nt_rms(a, b, h, w, resid, name, ride=None):
    m, n = a.shape
    k = b.shape[0]
    tm = _fit_rows(m, n * _isz(a) + 3 * k * 4, k * n * _isz(b), 16 * k)
    return _gmm_rms(name, (m // tm,), [a, b],
                    [pl.BlockSpec((tm, n), lambda i: (i, 0)), pl.BlockSpec((k, n), lambda i: (0, 0))],
                    pl.BlockSpec((tm, k), lambda i: (i, 0)),
                    lambda a_ref, b_ref: _nt(a_ref[...], b_ref[...]), h, w, resid, 0, ride=ride)


def _mm_cols_grad(a, d, name):
    m, k = a.shape
    n = d.shape[1] // N_SHARD
    tm = _fit_rows(m, k * _isz(a) + n * _isz(d), (3 * k * n * 4) // 2, 2 * (k + n))
    return _gmm(name, (N_SHARD, m // tm), [a, d],
                [pl.BlockSpec((tm, k), lambda j, i: (i, 0)), pl.BlockSpec((tm, n), lambda j, i: (i, j))],
                pl.BlockSpec((None, k, n), lambda j, i: (j, 0, 0)), jax.ShapeDtypeStruct((N_SHARD, k, n), F32),
                lambda a_ref, d_ref: (_tn(a_ref[...], d_ref[...]),), red_axis=1)


def _ffn_up(hn, wg, wu, layer, name):
    m, k = hn.shape
    n = wg.shape[3]
    tm = _fit_rows(m, k * _isz(hn) + 3 * n * jnp.dtype(BF16).itemsize, 2 * k * n * _isz(wg), 16 * n)

    def fn(a_ref, wg_ref, wu_ref):
        a = a_ref[...]
        g = _nn(a, wg_ref[...])
        u = _nn(a, wu_ref[...])
        return g, u, g * jax.nn.sigmoid(g) * u

    w_spec = pl.BlockSpec((None, None, k, n), lambda j, i: (j, layer, 0, 0))
    o_spec = pl.BlockSpec((None, tm, n), lambda j, i: (j, i, 0))
    out = jax.ShapeDtypeStruct((N_SHARD, m, n), BF16)
    return _gmm(name, (N_SHARD, m // tm), [hn, wg, wu],
                [pl.BlockSpec((tm, k), lambda j, i: (i, 0)), w_spec, w_spec],
                [o_spec, o_spec, o_spec], [out, out, out], fn)


def _ffn_down(act, wd, resid, layer, name):
    _, m, n = act.shape
    d = wd.shape[3]
    tm = _fit_rows(m, N_SHARD * n * _isz(act) + 2 * d * 4, N_SHARD * n * d * _isz(wd), 8 * d)

    def fn(a_ref, w_ref, r_ref):
        acc = r_ref[...]
        for j in range(N_SHARD):
            acc = acc + _nn(a_ref[j], w_ref[j])
        return (acc,)

    row = pl.BlockSpec((tm, d), lambda i: (i, 0))
    return _gmm(name, (m // tm,), [act, wd, resid],
                [pl.BlockSpec((N_SHARD, tm, n), lambda i: (0, i, 0)),
                 pl.BlockSpec((N_SHARD, None, n, d), lambda i: (0, layer, 0, 0)), row],
                row, jax.ShapeDtypeStruct((m, d), F32), fn)


def _ffn_down_bwd(dh, wd, g, u, layer, name, ride=None):
    m, d = dh.shape
    n = wd.shape[2]
    tm = _fit_rows(m, d * _isz(dh) + 4 * N_SHARD * n * jnp.dtype(BF16).itemsize, N_SHARD * n * d * _isz(wd),
                   2 * d + 24 * n)

    def body(dh_ref, wd_ref, g_ref, u_ref, dg_ref, du_ref):
        dhv = dh_ref[...].astype(MXU_DTYPE)
        for j in range(N_SHARD):
            dact = _nt(dhv, wd_ref[j])
            gv = g_ref[j].astype(F32)
            sg = jax.nn.sigmoid(gv)
            gs = gv * sg
            dg_ref[j] = (dact * u_ref[j].astype(F32) * (sg + gs * (1.0 - sg))).astype(dg_ref.dtype)
            du_ref[j] = (dact * gs).astype(du_ref.dtype)

    sh_spec = pl.BlockSpec((N_SHARD, tm, n), lambda i: (0, i, 0))
    out = jax.ShapeDtypeStruct((N_SHARD, m, n), BF16)
    res, rode = _pcall(body, [dh, wd, g, u], grid=(m // tm,),
                       in_specs=[pl.BlockSpec((tm, d), lambda i: (i, 0)),
                                 pl.BlockSpec((N_SHARD, None, n, d), lambda i: (0, layer, 0, 0)), sh_spec, sh_spec],
                       out_specs=[sh_spec, sh_spec], out_shape=[out, out], name=name, sem=("parallel",), ride=ride)
    return res if ride is None else (res, rode)


def _ffn_up_bwd(dg, du, wg, wu, layer, h, w, resid, name, ride=None):
    _, m, n = dg.shape
    k = wg.shape[2]
    tm = _fit_rows(m, 2 * N_SHARD * n * _isz(dg) + 3 * k * 4, 2 * N_SHARD * k * n * _isz(wg), 16 * k)

    def fn(dg_ref, du_ref, wg_ref, wu_ref):
        acc = _nt(dg_ref[0], wg_ref[0]) + _nt(du_ref[0], wu_ref[0])
        for j in range(1, N_SHARD):
            acc = acc + _nt(dg_ref[j], wg_ref[j]) + _nt(du_ref[j], wu_ref[j])
        return acc

    d_spec = pl.BlockSpec((N_SHARD, tm, n), lambda i: (0, i, 0))
    w_spec = pl.BlockSpec((N_SHARD, None, k, n), lambda i: (0, layer, 0, 0))
    return _gmm_rms(name, (m // tm,), [dg, du, wg, wu], [d_spec, d_spec, w_spec, w_spec],
                    pl.BlockSpec((tm, k), lambda i: (i, 0)), fn, h, w, resid, 0, ride=ride)


def _ffn_wgrad(lhs, rhs_list, layer, layers, prev, lhs_sharded, name):
    if lhs_sharded:
        _, m, k = lhs.shape
        n = rhs_list[0].shape[1]
    else:
        m, k = lhs.shape
        n = rhs_list[0].shape[2]
    n_out = len(rhs_list)
    tm = _fit_rows(m, k * _isz(lhs) + n_out * n * _isz(rhs_list[0]), (3 * n_out * k * n * 4) // 2,
                   2 * (k + n_out * n))
    sh = pl.BlockSpec((None, tm, k if lhs_sharded else n), lambda j, i: (j, i, 0))
    fl = pl.BlockSpec((tm, n if lhs_sharded else k), lambda j, i: (i, 0))
    n_out = len(rhs_list)
    args = [lhs] + list(rhs_list)
    in_specs = [sh if lhs_sharded else fl] + [fl if lhs_sharded else sh] * n_out
    aliases = None
    if prev is not None:
        aliases = {len(args) + t: t for t in range(n_out)}
        args = args + list(prev)
        in_specs = in_specs + [ANY] * n_out

    def fn(l_ref, *rest):
        lv = l_ref[...]
        return tuple(_tn(lv, r_ref[...]) for r_ref in rest[:n_out])

    o_spec = pl.BlockSpec((None, None, k, n), lambda j, i: (j, layer, 0, 0))
    out = jax.ShapeDtypeStruct((N_SHARD, layers, k, n), F32)
    return _gmm(name, (N_SHARD, m // tm), args, in_specs, [o_spec] * n_out, [out] * n_out, fn,
                red_axis=1, aliases=aliases)


def _ret_consts():
    log_gamma = jnp.log1p(-jnp.exp2(-5.0 - jnp.arange(RET_HEADS, dtype=F32)))
    idx = jnp.arange(CHUNK, dtype=F32)
    rel = idx[:, None] - idx[None, :]
    dmask = jnp.where((rel >= 0)[None], jnp.exp(log_gamma[:, None, None] * jnp.maximum(rel, 0.0)), 0.0)
    xi = jnp.exp(log_gamma[:, None] * (idx[None, :] + 1.0))[:, :, None]
    zeta = jnp.exp(log_gamma[:, None] * (CHUNK - 1.0 - idx[None, :]))[:, :, None]
    gamma_c = jnp.exp(log_gamma * CHUNK)
    wide = (RET_HEADS, CHUNK, RET_DK)
    return dmask, jnp.broadcast_to(xi, wide), jnp.broadcast_to(zeta, wide), gamma_c


def _rope_tables(nc):
    half = RET_DK // 2
    inv_freq = ROPE_BASE ** (-jnp.arange(half, dtype=F32) / half)
    a_chunk = (jnp.arange(nc) * CHUNK - PAD).astype(F32)[:, None] * inv_freq[None, :]
    a_row = jnp.arange(CHUNK).astype(F32)[:, None] * inv_freq[None, :]
    return (jnp.stack([jnp.cos(a_chunk), jnp.sin(a_chunk)], axis=1),
            jnp.stack([jnp.cos(a_row), jnp.sin(a_row)], axis=0))


def _rope_chunk(rc_ref, rr_ref):
    cc, sc = rc_ref[0:1, :], rc_ref[1:2, :]
    cr, sr = rr_ref[0], rr_ref[1]
    return cc * cr - sc * sr, sc * cr + cc * sr


def _rope_specs(order):
    half = RET_DK // 2
    return [pl.BlockSpec((None, 2, half), lambda n: (order(n), 0, 0)),
            pl.BlockSpec((2, CHUNK, half), lambda n: (0, 0, 0))]


def _ret_specs(order):
    return [pl.BlockSpec((CHUNK, RET_QK), lambda n: (order(n), 0)),
            pl.BlockSpec((CHUNK, RET_QK), lambda n: (order(n), 1)),
            pl.BlockSpec((CHUNK, RET_V), lambda n: (order(n), 1)),
            pl.BlockSpec((CHUNK, RET_V), lambda n: (order(n), 2))]


def _ret_const_specs():
    return [pl.BlockSpec((RET_HEADS, CHUNK, CHUNK), lambda n: (0, 0, 0)),
            pl.BlockSpec((RET_HEADS, CHUNK, RET_DK), lambda n: (0, 0, 0)),
            pl.BlockSpec((RET_HEADS, CHUNK, RET_DK), lambda n: (0, 0, 0)),
            pl.BlockSpec((1, RET_DV), lambda n: (0, 0))]


def _ret_fwd(proj, cos, sin, consts, gn_w, seq, ride=None):
    rows = proj.shape[0]
    nc = rows // CHUNK
    dmask, xi, zeta, gamma_c = consts

    def body(gam_ref, q_ref, k_ref, v_ref, g_ref, cos_ref, sin_ref, dm_ref, xi_ref, ze_ref, gn_ref,
             o_ref, y_ref, ss_ref, s_ref):
        n = pl.program_id(0)

        @pl.when(n == 0)
        def _():
            s_ref[...] = jnp.zeros_like(s_ref)

        cs, sn = _rope_chunk(cos_ref, sin_ref)
        kscale = _valid_rows(n * CHUNK, CHUNK, seq) * (RET_DK ** -0.5)
        gn = gn_ref[...]
        hs = range(RET_HEADS)
        qk_cols = [slice(h * RET_DK, (h + 1) * RET_DK) for h in hs]
        v_cols = [slice(h * RET_DV, (h + 1) * RET_DV) for h in hs]
        qr_l = [_rope(q_ref[:, c], cs, sn) for c in qk_cols]
        kr_l = [_rope(k_ref[:, c], cs, sn) * kscale for c in qk_cols]
        v_l = [v_ref[:, c] for c in v_cols]
        s_l = [s_ref[h] for h in hs]
        sc_l = [_nt(qr, kr) * dm_ref[h] for h, (qr, kr) in enumerate(zip(qr_l, kr_l))]
        o_l = [_nn(sc_l[h], v_l[h]) + _nn(qr_l[h] * xi_ref[h], s_l[h]) for h in hs]
        for h in hs:
            ss_ref[0, h] = s_l[h].astype(ss_ref.dtype)
            s_ref[h] = gam_ref[h] * s_l[h] + _tn(kr_l[h] * ze_ref[h], v_l[h])
            o_ref[:, v_cols[h]] = o_l[h]
            y_ref[:, v_cols[h]] = _gated_norm(o_l[h], g_ref[:, v_cols[h]], gn).astype(y_ref.dtype)

    fwd = lambda n: n
    row_v = pl.BlockSpec((CHUNK, RET_V), lambda n: (n, 0))
    res, rode = _pcall(
        body, [gamma_c, proj, proj, proj, proj, cos, sin, dmask, xi, zeta, gn_w.reshape(1, RET_DV)],
        grid=(nc,),
        in_specs=[pl.BlockSpec(memory_space=pltpu.SMEM)] + _ret_specs(fwd) + _rope_specs(fwd)
        + _ret_const_specs(),
        out_specs=[row_v, row_v,
                   pl.BlockSpec((1, RET_HEADS, RET_DK, RET_DV), lambda n: (n, 0, 0, 0))],
        out_shape=[jax.ShapeDtypeStruct((rows, RET_V), F32), jax.ShapeDtypeStruct((rows, RET_V), BF16),
                   jax.ShapeDtypeStruct((nc, RET_HEADS, RET_DK, RET_DV), BF16)],
        scratch=[pltpu.VMEM((RET_HEADS, RET_DK, RET_DV), F32)], name="ret_fwd", sem=("arbitrary",), ride=ride)
    return res if ride is None else (res, rode)


def _ret_bwd(proj, o, dy, states, cos, sin, consts, gn_w, seq, ride=None):
    rows = proj.shape[0]
    nc = rows // CHUNK
    dmask, xi, zeta, gamma_c = consts

    def body(gam_ref, q_ref, k_ref, v_ref, g_ref, o_ref, dy_ref, ss_ref, cos_ref, sin_ref,
             dm_ref, xi_ref, ze_ref, gn_ref, dp_ref, dgn_ref, ds_ref):
        n = pl.program_id(0)

        @pl.when(n == 0)
        def _():
            ds_ref[...] = jnp.zeros_like(ds_ref)
            dgn_ref[...] = jnp.zeros_like(dgn_ref)

        cs, sn = _rope_chunk(cos_ref, sin_ref)
        kscale = _valid_rows((nc - 1 - n) * CHUNK, CHUNK, seq) * (RET_DK ** -0.5)
        gn = gn_ref[...]
        dgn = jnp.zeros((1, RET_DV), F32)
        hs = range(RET_HEADS)
        qk_cols = [slice(h * RET_DK, (h + 1) * RET_DK) for h in hs]
        v_cols = [slice(h * RET_DV, (h + 1) * RET_DV) for h in hs]
        qr_l = [_rope(q_ref[:, c], cs, sn) for c in qk_cols]
        kr_l = [_rope(k_ref[:, c], cs, sn) * kscale for c in qk_cols]
        v_l = [v_ref[:, c] for c in v_cols]
        s_l = [ss_ref[0, h] for h in hs]
        ds_l = [ds_ref[h] for h in hs]
        sc_l = [_nt(qr_l[h], kr_l[h]) * dm_ref[h] for h in hs]
        gnb = [_gated_norm_bwd(dy_ref[:, c], o_ref[:, c], g_ref[:, c], gn) for c in v_cols]
        do_l = [x[0] for x in gnb]
        dsc_l = [_nt(do_l[h], v_l[h]) * dm_ref[h] for h in hs]
        dv_l = [_tn(sc_l[h], do_l[h]) + _nn(kr_l[h] * ze_ref[h], ds_l[h]) for h in hs]
        dqr_l = [_nn(dsc_l[h], kr_l[h]) + _nt(do_l[h], s_l[h]) * xi_ref[h] for h in hs]
        dkr_l = [_tn(dsc_l[h], qr_l[h]) + _nt(v_l[h], ds_l[h]) * ze_ref[h] for h in hs]
        for h in hs:
            dgn = dgn + gnb[h][2]
            ds_ref[h] = gam_ref[h] * ds_l[h] + _tn(qr_l[h] * xi_ref[h], do_l[h])
            dp_ref[:, qk_cols[h]] = _rope_bwd(dqr_l[h], cs, sn).astype(dp_ref.dtype)
            dp_ref[:, RET_QK + h * RET_DK:RET_QK + (h + 1) * RET_DK] = (
                _rope_bwd(dkr_l[h] * kscale, cs, sn).astype(dp_ref.dtype))
            dp_ref[:, 2 * RET_QK + h * RET_DV:2 * RET_QK + (h + 1) * RET_DV] = dv_l[h].astype(dp_ref.dtype)
            dp_ref[:, 2 * RET_QK + RET_V + h * RET_DV:2 * RET_QK + RET_V + (h + 1) * RET_DV] = (
                gnb[h][1].astype(dp_ref.dtype))
        dgn_ref[...] += dgn

    rev = lambda n: nc - 1 - n
    row_v = pl.BlockSpec((CHUNK, RET_V), lambda n: (rev(n), 0))
    res, rode = _pcall(
        body, [gamma_c, proj, proj, proj, proj, o, dy, states, cos, sin, dmask, xi, zeta,
               gn_w.reshape(1, RET_DV)],
        grid=(nc,),
        in_specs=[pl.BlockSpec(memory_space=pltpu.SMEM)] + _ret_specs(rev) + [
            row_v, row_v, pl.BlockSpec((1, RET_HEADS, RET_DK, RET_DV), lambda n: (rev(n), 0, 0, 0))]
        + _rope_specs(rev) + _ret_const_specs(),
        out_specs=[pl.BlockSpec((CHUNK, RET_IN), lambda n: (rev(n), 0)),
                   pl.BlockSpec((1, RET_DV), lambda n: (0, 0))],
        out_shape=[jax.ShapeDtypeStruct((rows, RET_IN), BF16), jax.ShapeDtypeStruct((1, RET_DV), F32)],
        scratch=[pltpu.VMEM((RET_HEADS, RET_DK, RET_DV), F32)], name="ret_bwd", sem=("arbitrary",), ride=ride)
    return res if ride is None else (res, rode)


GATE_COL = DN_CONV_CH // DN_V
BA_COL = (DN_CONV_CH + DN_V) // LANES
BETA_LANE, DECAY_LANE = 0, DN_HEADS
INV_SHIFT = 4
INV_SQUARINGS = INV_SHIFT - 1
assert CHUNK == 4 << INV_SHIFT


def _dn_in_specs(order, conv_saved=False):
    return [pl.BlockSpec((CHUNK, DN_CONV_CH), lambda n: (order(n), 0)),
            pl.BlockSpec((CHUNK, DN_CONV_CH), lambda n: (order(n), 0)) if conv_saved else
            pl.BlockSpec((8, DN_CONV_CH), lambda n: (jnp.maximum(order(n) * (CHUNK // 8) - 1, 0), 0)),
            pl.BlockSpec((CHUNK, DN_V), lambda n: (order(n), GATE_COL)),
            pl.BlockSpec((CHUNK, LANES), lambda n: (order(n), BA_COL)),
            pl.BlockSpec((CONV_K, 1, DN_CONV_CH), lambda n: (0, 0, 0)),
            pl.BlockSpec((1, LANES), lambda n: (0, 0)),
            pl.BlockSpec((1, LANES), lambda n: (0, 0)),
            pl.BlockSpec((1, DN_DV), lambda n: (0, 0))]


def _dn_front(c, seq, x_ref, halo_ref, ba_ref, cw_ref, al_ref, dt_ref, yc_ref=None):
    valid = _valid_rows(c * CHUNK, CHUNK, seq)
    xin = x_ref[...] * valid
    if yc_ref is None:
        halo = halo_ref[...] * _valid_rows(c * CHUNK - 8, 8, seq)
        yc = xin * cw_ref[CONV_K - 1]
        for k in range(1, CONV_K):
            yc = yc + _shift_down(xin, halo, k) * cw_ref[CONV_K - 1 - k]
    else:
        yc = yc_ref[...]
    sgc = jax.nn.sigmoid(yc)
    ba = ba_ref[...]
    sig = jax.nn.sigmoid(ba)
    beta = sig * valid
    z = ba + dt_ref[...]
    eal = jnp.exp(al_ref[...])
    g = -eal * _softplus(z) * valid
    ri, ci = _iota((CHUNK, CHUNK), 0), _iota((CHUNK, CHUNK), 1)
    lower = (ri >= ci).astype(F32)
    upper = (ri <= ci).astype(F32)
    eye = (ri == ci).astype(F32)
    gam = _nn(lower, g, hi=True)
    gam_t = _tn(g, upper, hi=True)
    return dict(valid=valid, xin=xin, yc=yc, sgc=sgc, act=yc * sgc, sig=sig, beta=beta, z=z,
                eal=eal, g=g, gam=gam, gam_t=gam_t, ri=ri, ci=ci, upper=upper, eye=eye)


def _dn_head(f, h):
    act = f["act"]
    q_raw = act[:, h * DN_DK:(h + 1) * DN_DK]
    k_raw = act[:, DN_QK + h * DN_DK:DN_QK + (h + 1) * DN_DK]
    v = act[:, 2 * DN_QK + h * DN_DV:2 * DN_QK + (h + 1) * DN_DV]
    rq = lax.rsqrt(jnp.sum(q_raw * q_raw, axis=-1, keepdims=True) + RMS_EPS)
    rk = lax.rsqrt(jnp.sum(k_raw * k_raw, axis=-1, keepdims=True) + RMS_EPS)
    qh = q_raw * rq
    kn = k_raw * rk
    gam_c = _col(f["gam"], DECAY_LANE + h)
    gam_r = _row(f["gam_t"], DECAY_LANE + h)
    bc = _col(f["beta"], BETA_LANE + h)
    diff = gam_c - gam_r
    decay = jnp.where(f["ri"] >= f["ci"], jnp.exp(jnp.minimum(diff, 0.0)), 0.0)
    glast = jnp.sum(gam_r * (_iota((1, CHUNK), 1) == CHUNK - 1).astype(F32), axis=1, keepdims=True)
    return dict(rq=rq, rk=rk, qh=qh, qn=qh * (DN_DK ** -0.5), kn=kn, v=v, gam_c=gam_c, gam_r=gam_r,
                bc=bc, diff=diff, decay=decay, egam=jnp.exp(gam_c), glast=glast,
                eglast=jnp.exp(glast), ekd=jnp.exp(glast - gam_c))


def _dn_fwd(proj, conv_w, alog, dtb, norm_w, seq):
    rows = proj.shape[0]
    nc = rows // CHUNK

    def body(x_ref, halo_ref, gate_ref, ba_ref, cw_ref, al_ref, dt_ref, nw_ref,
             o_ref, y_ref, ss_ref, t_ref, yc_ref, s_ref):
        n = pl.program_id(0)

        @pl.when(n == 0)
        def _():
            s_ref[...] = jnp.zeros_like(s_ref)

        f = _dn_front(n, seq, x_ref, halo_ref, ba_ref, cw_ref, al_ref, dt_ref)
        yc_ref[...] = f["yc"]
        ri, ci = f["ri"], f["ci"]
        eye = f["eye"]
        diag_m = (jnp.right_shift(ri, INV_SHIFT) == jnp.right_shift(ci, INV_SHIFT)).astype(F32)
        half_m = (jnp.right_shift(ri, INV_SHIFT + 1) == jnp.right_shift(ci, INV_SHIFT + 1)).astype(F32)
        nw = nw_ref[...]
        heads = [_dn_head(f, h) for h in range(DN_HEADS)]
        a_all = [jnp.where(ri > ci, hd["bc"] * _nt(hd["kn"], hd["kn"]) * hd["decay"], 0.0) for hd in heads]
        b_all = [a * diag_m for a in a_all]
        t_all = [eye - b for b in b_all]
        for _ in range(INV_SQUARINGS):
            b_all = [_nn(b, b, hi=True) for b in b_all]
            t_all = [t + _nn(t, b, hi=True) for t, b in zip(t_all, b_all)]
        for off_m in (half_m - diag_m, 1.0 - half_m):
            x_all = [_nn(a * off_m, t, hi=True) for a, t in zip(a_all, t_all)]
            t_all = [t - _nn(t, x, hi=True) for t, x in zip(t_all, x_all)]
        u_all = [_nn(t, hd["v"] * hd["bc"], hi=True) for t, hd in zip(t_all, heads)]
        w_all = [_nn(t, hd["kn"] * (hd["bc"] * hd["egam"]), hi=True) for t, hd in zip(t_all, heads)]
        s_all = [s_ref[h] for h in range(DN_HEADS)]
        qk_all = [_nt(hd["qn"], hd["kn"]) * hd["decay"] for hd in heads]
        os_all = [_nn(hd["qn"] * hd["egam"], s) for hd, s in zip(heads, s_all)]
        vnew_all = [u - _nn(w, s) for u, w, s in zip(u_all, w_all, s_all)]
        o_all = [os + _nn(qk, vn) for os, qk, vn in zip(os_all, qk_all, vnew_all)]
        snew_all = [s * hd["eglast"] + _tn(hd["kn"] * hd["ekd"], vn)
                    for s, hd, vn in zip(s_all, heads, vnew_all)]
        for h in range(DN_HEADS):
            v_cols = slice(h * DN_DV, (h + 1) * DN_DV)
            t_ref[0, h] = t_all[h]
            ss_ref[0, h] = s_all[h]
            s_ref[h] = snew_all[h]
            o_ref[:, v_cols] = o_all[h]
            y_ref[:, v_cols] = _gated_norm(o_all[h], gate_ref[:, v_cols], nw).astype(y_ref.dtype)

    fwd = lambda n: n
    row_v = pl.BlockSpec((CHUNK, DN_V), lambda n: (n, 0))
    return pl.pallas_call(
        body, grid=(nc,), in_specs=_dn_in_specs(fwd),
        out_specs=[row_v, row_v,
                   pl.BlockSpec((1, DN_HEADS, DN_DK, DN_DV), lambda n: (n, 0, 0, 0)),
                   pl.BlockSpec((1, DN_HEADS, CHUNK, CHUNK), lambda n: (n, 0, 0, 0)),
                   pl.BlockSpec((CHUNK, DN_CONV_CH), lambda n: (n, 0))],
        out_shape=[jax.ShapeDtypeStruct((rows, DN_V), F32), jax.ShapeDtypeStruct((rows, DN_V), BF16),
                   jax.ShapeDtypeStruct((nc, DN_HEADS, DN_DK, DN_DV), F32),
                   jax.ShapeDtypeStruct((nc, DN_HEADS, CHUNK, CHUNK), F32),
                   jax.ShapeDtypeStruct((rows, DN_CONV_CH), F32)],
        scratch_shapes=[pltpu.VMEM((DN_HEADS, DN_DK, DN_DV), F32)],
        name="dn_fwd", compiler_params=_params("arbitrary"))(
            proj, proj, proj, proj, conv_w, alog, dtb, norm_w.reshape(1, DN_DV))


def _dn_bwd(proj, conv_out, o, dy, states, tinv, conv_w, alog, dtb, norm_w, seq):
    rows = proj.shape[0]
    nc = rows // CHUNK

    def body(x_ref, yc_ref, gate_ref, ba_ref, cw_ref, al_ref, dt_ref, nw_ref,
             o_ref, dy_ref, ss_ref, t_ref,
             dp_ref, dcw_ref, dal_ref, ddt_ref, dnw_ref, ds_ref, nxt_ref):
        n = pl.program_id(0)

        @pl.when(n == 0)
        def _():
            ds_ref[...] = jnp.zeros_like(ds_ref)
            nxt_ref[...] = jnp.zeros_like(nxt_ref)
            dcw_ref[...] = jnp.zeros_like(dcw_ref)
            dal_ref[...] = jnp.zeros_like(dal_ref)
            ddt_ref[...] = jnp.zeros_like(ddt_ref)
            dnw_ref[...] = jnp.zeros_like(dnw_ref)

        f = _dn_front(nc - 1 - n, seq, x_ref, None, ba_ref, cw_ref, al_ref, dt_ref, yc_ref)
        ri, ci = f["ri"], f["ci"]
        strict = (ri > ci).astype(F32)
        nw = nw_ref[...]
        lane128 = _iota((1, LANES), 1)
        row128 = _iota((LANES, 1), 0)
        dgam_col = jnp.zeros((CHUNK, LANES), F32)
        dgam_row = jnp.zeros((LANES, CHUNK), F32)
        dbeta = jnp.zeros((CHUNK, LANES), F32)
        dnw = jnp.zeros((1, DN_DV), F32)
        hs = range(DN_HEADS)
        heads = [_dn_head(f, h) for h in hs]
        cols = [slice(h * DN_DV, (h + 1) * DN_DV) for h in hs]
        t_l = [t_ref[0, h] for h in hs]
        s_l = [ss_ref[0, h] for h in hs]
        ds_l = [ds_ref[h] for h in hs]
        kk_l = [_nt(hd["kn"], hd["kn"]) for hd in heads]
        p_l = [_nt(hd["qn"], hd["kn"]) for hd in heads]
        rhsw_l = [hd["kn"] * (hd["bc"] * hd["egam"]) for hd in heads]
        u_l = [_nn(t, hd["v"] * hd["bc"], hi=True) for t, hd in zip(t_l, heads)]
        w_l = [_nn(t, r, hi=True) for t, r in zip(t_l, rhsw_l)]
        vnew_l = [u - _nn(w, s) for u, w, s in zip(u_l, w_l, s_l)]
        gnb = [_gated_norm_bwd(dy_ref[:, c], o_ref[:, c], gate_ref[:, c], nw) for c in cols]
        do_l = [x[0] for x in gnb]
        for h in hs:
            dp_ref[:, DN_CONV_CH + h * DN_DV:DN_CONV_CH + (h + 1) * DN_DV] = gnb[h][1].astype(dp_ref.dtype)
            dnw = dnw + gnb[h][2]
        qg_l = [hd["qn"] * hd["egam"] for hd in heads]
        kd_l = [hd["kn"] * hd["ekd"] for hd in heads]
        dvnew_l = [_tn(p * hd["decay"], do) + _nn(kd, ds)
                   for p, hd, do, kd, ds in zip(p_l, heads, do_l, kd_l, ds_l)]
        m_l = [_nt(do, vn) for do, vn in zip(do_l, vnew_l)]
        dqg_l = [_nt(do, s) for do, s in zip(do_l, s_l)]
        dkd_l = [_nt(vn, ds) for vn, ds in zip(vnew_l, ds_l)]
        for h in hs:
            ds_ref[h] = (ds_l[h] * heads[h]["eglast"] + _tn(qg_l[h], do_l[h]) - _tn(w_l[h], dvnew_l[h]))
        dw_l = [-_nt(dvn, s) for dvn, s in zip(dvnew_l, s_l)]
        dru_l = [_tn(t, dvn, hi=True) for t, dvn in zip(t_l, dvnew_l)]
        drw_l = [_tn(t, dw_, hi=True) for t, dw_ in zip(t_l, dw_l)]
        da_l = [-(_nt(dru, u) + _nt(drw, w)) * strict for dru, u, drw, w in zip(dru_l, u_l, drw_l, w_l)]
        dp_l = [m * hd["decay"] for m, hd in zip(m_l, heads)]
        dkk_l = [da * (hd["bc"] * hd["decay"]) for da, hd in zip(da_l, heads)]
        dqn_l = [dqg * hd["egam"] + _nn(dp, hd["kn"]) for dqg, hd, dp in zip(dqg_l, heads, dp_l)]
        dkn_l = [_tn(dp, hd["qn"]) + dkd * hd["ekd"] + drw * (hd["bc"] * hd["egam"])
                 + _nn(dkk, hd["kn"]) + _tn(dkk, hd["kn"])
                 for dp, hd, dkd, drw, dkk in zip(dp_l, heads, dkd_l, drw_l, dkk_l)]
        dq_parts, dk_parts, dv_parts = [], [], []
        for h in hs:
            hd = heads[h]
            kn, v, bc, egam, decay = hd["kn"], hd["v"], hd["bc"], hd["egam"], hd["decay"]
            t1 = jnp.sum(dkd_l[h] * kd_l[h], axis=1, keepdims=True)
            dglast = (jnp.sum(t1, axis=0, keepdims=True)
                      + jnp.sum(jnp.sum(ds_l[h] * s_l[h], axis=1, keepdims=True), axis=0, keepdims=True)
                      * hd["eglast"])
            e = (m_l[h] * p_l[h] + da_l[h] * (bc * kk_l[h])) * decay
            dgc = (jnp.sum(dqg_l[h] * qg_l[h], axis=1, keepdims=True) - t1
                   + jnp.sum(drw_l[h] * rhsw_l[h], axis=1, keepdims=True)
                   + jnp.sum(e, axis=1, keepdims=True)
                   + jnp.where(_iota((CHUNK, 1), 0) == CHUNK - 1, dglast, 0.0))
            dgr = -jnp.sum(e, axis=0, keepdims=True)
            dbc = (jnp.sum(dru_l[h] * v, axis=1, keepdims=True)
                   + jnp.sum(drw_l[h] * kn, axis=1, keepdims=True) * egam
                   + jnp.sum(da_l[h] * kk_l[h] * decay, axis=1, keepdims=True))
            dv_parts.append(dru_l[h] * bc)
            qh, dqn, dkn = hd["qh"], dqn_l[h], dkn_l[h]
            dq_parts.append(((DN_DK ** -0.5) * hd["rq"])
                            * (dqn - qh * jnp.sum(dqn * qh, axis=1, keepdims=True)))
            dk_parts.append(hd["rk"] * (dkn - kn * jnp.sum(dkn * kn, axis=1, keepdims=True)))
            dgam_col = dgam_col + dgc * (lane128 == DECAY_LANE + h).astype(F32)
            dbeta = dbeta + dbc * (lane128 == BETA_LANE + h).astype(F32)
            dgam_row = dgam_row + (row128 == DECAY_LANE + h).astype(F32) * dgr
        dnw_ref[...] += dnw
        dgam = dgam_col + _nt(f["eye"], dgam_row, hi=True)
        dg = _nn(f["upper"], dgam, hi=True)
        d_a = dg * (-f["eal"]) * jax.nn.sigmoid(f["z"]) * f["valid"]
        dal_ref[...] += jnp.sum(dg * f["g"], axis=0, keepdims=True)
        ddt_ref[...] += jnp.sum(d_a, axis=0, keepdims=True)
        d_b = dbeta * f["valid"] * f["sig"] * (1.0 - f["sig"])
        dp_ref[:, DN_CONV_CH + DN_V:DN_CONV_CH + DN_V + LANES] = (d_a + d_b).astype(dp_ref.dtype)
        dp_ref[:, DN_CONV_CH + DN_V + LANES:] = jnp.zeros((CHUNK, DN_IN_PAD - DN_IN_USED), dp_ref.dtype)
        dact = jnp.concatenate(dq_parts + dk_parts + dv_parts, axis=1)
        yc, sgc = f["yc"], f["sgc"]
        dyc = dact * (sgc * (1.0 + yc * (1.0 - sgc)))
        nxt = nxt_ref[...]
        ups = [dyc] + [_shift_up(dyc, nxt, j) for j in range(1, CONV_K)]
        dx = ups[0] * cw_ref[CONV_K - 1]
        for j in range(1, CONV_K):
            dx = dx + ups[j] * cw_ref[CONV_K - 1 - j]
        for j in range(CONV_K):
            dcw_ref[CONV_K - 1 - j] += jnp.sum(f["xin"] * ups[j], axis=0, keepdims=True)
        nxt_ref[...] = dyc[0:8]
        dp_ref[:, :DN_CONV_CH] = (dx * f["valid"]).astype(dp_ref.dtype)

    rev = lambda n: nc - 1 - n
    row_v = pl.BlockSpec((CHUNK, DN_V), lambda n: (rev(n), 0))
    vec = pl.BlockSpec((1, LANES), lambda n: (0, 0))
    return pl.pallas_call(
        body, grid=(nc,),
        in_specs=_dn_in_specs(rev, conv_saved=True) + [
            row_v, row_v,
            pl.BlockSpec((1, DN_HEADS, DN_DK, DN_DV), lambda n: (rev(n), 0, 0, 0)),
            pl.BlockSpec((1, DN_HEADS, CHUNK, CHUNK), lambda n: (rev(n), 0, 0, 0))],
        out_specs=[pl.BlockSpec((CHUNK, DN_IN_PAD), lambda n: (rev(n), 0)),
                   pl.BlockSpec((CONV_K, 1, DN_CONV_CH), lambda n: (0, 0, 0)), vec, vec,
                   pl.BlockSpec((1, DN_DV), lambda n: (0, 0))],
        out_shape=[jax.ShapeDtypeStruct((rows, DN_IN_PAD), BF16),
                   jax.ShapeDtypeStruct((CONV_K, 1, DN_CONV_CH), F32),
                   jax.ShapeDtypeStruct((1, LANES), F32), jax.ShapeDtypeStruct((1, LANES), F32),
                   jax.ShapeDtypeStruct((1, DN_DV), F32)],
        scratch_shapes=[pltpu.VMEM((DN_HEADS, DN_DK, DN_DV), F32), pltpu.VMEM((8, DN_CONV_CH), F32)],
        name="dn_bwd", compiler_params=_params("arbitrary"))(
            proj, conv_out, proj, proj, conv_w, alog, dtb, norm_w.reshape(1, DN_DV), o, dy, states, tinv)


def _train_step(x, tgt, wts, sh, idx):
    seq = x.shape[0]
    rows = -(-(seq + CHUNK) // ROW_ALIGN) * ROW_ALIGN
    tail = rows - seq - CHUNK
    h0 = jnp.concatenate([jnp.zeros((PAD, D_MODEL), F32), wts["meta_tokens"].astype(F32), x,
                          jnp.zeros((tail, D_MODEL), F32)], axis=0)
    tgt_p = jnp.concatenate([jnp.zeros((CHUNK, D_MODEL), F32), tgt, jnp.zeros((tail, D_MODEL), F32)],
                            axis=0)
    cos, sin = _rope_tables(rows // CHUNK)
    consts = _ret_consts()
    conv_w = wts["dn_conv_w"].reshape(CONV_K, 1, DN_CONV_CH)
    lane_pad = LANES - 2 * DN_HEADS
    alog = jnp.pad(wts["dn_a_log"].reshape(1, DN_HEADS), ((0, 0), (DECAY_LANE, lane_pad)))
    dtb = jnp.pad(wts["dn_dt_bias"].reshape(1, DN_HEADS), ((0, 0), (DECAY_LANE, lane_pad)))
    g = {}

    wts = dict(wts)
    hn0, (got,) = _rms_fwd(h0, wts["mix_norm_w"][0], "rms_mix0", ride=_Ride("gather", [sh["ret_w_in"]]))
    wts["ret_w_in"] = got.reshape(N_SHARD, D_MODEL, -1)
    proj0, got = _mm_cols(hn0, wts["ret_w_in"], "ret_in",
                          ride=_Ride("gather", [sh["ret_w_out"], sh["ffn_w_gate"], sh["dn_w_out"]]))
    wts["ret_w_out"] = got[0].reshape(-1, D_MODEL)
    wts["ffn_w_gate"] = got[1]
    wts["dn_w_out"] = got[2].reshape(-1, D_MODEL)
    (o0, y0, st0), got = _ret_fwd(proj0, cos, sin, consts, wts["ret_gn_w"], seq,
                                  ride=_Ride("gather", [sh["ffn_w_up"], sh["ffn_w_down"], sh["dn_w_in"]]))
    wts["ffn_w_up"], wts["ffn_w_down"] = got[0], got[1]
    n_dn = sh["dn_w_in"].shape[-1]
    dn_shards = got[2].reshape(N_SHARD, D_MODEL, n_dn)
    wts["dn_w_in"] = jnp.concatenate(
        [dn_shards[j] for j in range(N_SHARD)]
        + [jnp.zeros((D_MODEL, DN_IN_PAD - N_SHARD * n_dn), dn_shards.dtype)], axis=-1)
    h1 = _mm(y0, wts["ret_w_out"], mode="nn", name="ret_out", resid=h0)
    hn1 = _rms_fwd(h1, wts["ffn_norm_w"][0], "rms_ffn0")
    g0, u0, act0 = _ffn_up(hn1, wts["ffn_w_gate"], wts["ffn_w_up"], 0, "ffn_up0")
    h2 = _ffn_down(act0, wts["ffn_w_down"], h1, 0, "ffn_down0")
    hn2 = _rms_fwd(h2, wts["mix_norm_w"][1], "rms_mix1")
    proj1 = _mm(hn2, wts["dn_w_in"], mode="nn", name="dn_in")
    o1, y1, st1, tinv, conv1 = _dn_fwd(proj1, conv_w, alog, dtb, wts["dn_norm_w"], seq)
    h3 = _mm(y1, wts["dn_w_out"], mode="nn", name="dn_out", resid=h2)
    hn3 = _rms_fwd(h3, wts["ffn_norm_w"][1], "rms_ffn1")
    g1, u1, act1 = _ffn_up(hn3, wts["ffn_w_gate"], wts["ffn_w_up"], 1, "ffn_up1")
    h4 = _ffn_down(act1, wts["ffn_w_down"], h3, 1, "ffn_down1")

    dh4, g["final_norm_w"], loss, dh4b = _final_loss(h4, wts["final_norm_w"], tgt_p, seq, "final_loss")

    layers = wts["ffn_w_gate"].shape[1]

    ffn_names = ["ffn_w_down", "ffn_w_gate", "ffn_w_up"]

    def ffn_bwd(dh_out, dhb_out, h_mid, hn, gg, uu, act, layer, prev, ride=None, last=False):
        tag = str(layer)
        res = _ffn_down_bwd(dhb_out, wts["ffn_w_down"], gg, uu, layer, "ffn_down_bwd" + tag, ride=ride)
        (dg, du), rode = res if ride is not None else (res, None)
        d_down = _ffn_wgrad(act, [dhb_out], layer, layers, prev and prev[:1], True, "ffn_dwd" + tag)
        d_gu = _ffn_wgrad(hn, [dg, du], layer, layers, prev and prev[1:], False, "ffn_dwgu" + tag)
        grads = list(d_down) + list(d_gu)
        gs = rs_grads(ffn_names, grads) if last else None
        res = _ffn_up_bwd(dg, du, wts["ffn_w_gate"], wts["ffn_w_up"], layer, h_mid, wts["ffn_norm_w"][layer],
                          dh_out, "ffn_up_bwd" + tag, ride=_Ride("pair", gs) if last else None)
        (dh_mid, d_norm, dhb_mid), sib = res if last else (res, None)
        return dh_mid, dhb_mid, grads, d_norm, rode, gs, sib

    red = {}

    def rs_grads(names, grads):
        return [gr.reshape((N_SHARD,) + sh[n].shape) for n, gr in zip(names, grads)]

    def rs_partials(names, gs, sib):
        return [_rs_pair_add(gs[t], sib[t], idx, "rs_pair_add_" + n) for t, n in enumerate(names)]

    def rs_end(names, gs, sib, others, tag):
        mine = [_rs_final_add(gs[t], sib[t], others[t], idx, "rs_final_add_" + n) for t, n in enumerate(names)]
        red.update(zip(names, _rs_share(mine, "rs_share" + tag)))

    dh3, dh3b, ffn_grads, dfn1 = ffn_bwd(dh4, dh4b, h3, hn3, g1, u1, act1, 1, None)[:4]
    dy1 = _mm(dh3b, wts["dn_w_out"], mode="nt", name="dn_out_bwd")
    d_dn_out = _mm(y1, dh3b, mode="tn", name="dn_dwo")
    dproj1, dcw, dal, ddt, g["dn_norm_w"] = _dn_bwd(proj1, conv1, o1, dy1, st1, tinv, conv_w, alog, dtb,
                                                    wts["dn_norm_w"], seq)
    d_dn_in = _mm(hn2, dproj1, mode="tn", name="dn_dwi")
    d_dn_in = jnp.stack([d_dn_in[:, j * n_dn:(j + 1) * n_dn] for j in range(N_SHARD)])
    group1 = ["dn_w_out", "dn_w_in"]
    gs1 = rs_grads(group1, [d_dn_out, d_dn_in])
    (dh2, dmn1, dh2b), sib1 = _mm_nt_rms(dproj1, wts["dn_w_in"], h2, wts["mix_norm_w"][1], dh3, "dn_in_bwd",
                                         ride=_Ride("pair", gs1))
    g["dn_conv_w"] = dcw.reshape(CONV_K, DN_CONV_CH)
    g["dn_a_log"] = dal[0, DECAY_LANE:DECAY_LANE + DN_HEADS]
    g["dn_dt_bias"] = ddt[0, DECAY_LANE:DECAY_LANE + DN_HEADS]

    dh1, dh1b, _, dfn0, others1, gs2, sib2 = ffn_bwd(
        dh2, dh2b, h1, hn1, g0, u0, act0, 0, ffn_grads,
        ride=_Ride("chips", rs_partials(group1, gs1, sib1)), last=True)
    rs_end(group1, gs1, sib1, others1, "1")
    d_ret_out = _mm(y0, dh1b, mode="tn", name="ret_dwo")
    gs2b = rs_grads(["ret_w_out"], [d_ret_out])
    dy0, sib2b = _mm(dh1b, wts["ret_w_out"], mode="nt", name="ret_out_bwd", ride=_Ride("pair", gs2b))
    group2 = ffn_names + ["ret_w_out"]
    gs2, sib2 = gs2 + gs2b, list(sib2) + list(sib2b)
    (dproj0, g["ret_gn_w"]), others2 = _ret_bwd(proj0, o0, dy0, st0, cos, sin, consts, wts["ret_gn_w"], seq,
                                                ride=_Ride("chips", rs_partials(group2, gs2, sib2)))
    rs_end(group2, gs2, sib2, others2, "2")
    d_ret_in = _mm_cols_grad(hn0, dproj0, "ret_dwi")
    gs3 = rs_grads(["ret_w_in"], [d_ret_in])
    sib3 = _rs_pair(gs3, "rs_pair3")
    (dh0, dmn0, _), others3 = _mm_cols_t_rms(dproj0, wts["ret_w_in"], h0, wts["mix_norm_w"][0], dh1, "ret_in_bwd",
                                             ride=_Ride("chips", rs_partials(["ret_w_in"], gs3, sib3)))
    rs_end(["ret_w_in"], gs3, sib3, others3, "3")

    g["ffn_norm_w"] = jnp.concatenate([dfn0, dfn1], axis=0)
    g["mix_norm_w"] = jnp.concatenate([dmn0, dmn1], axis=0)
    g["meta_tokens"] = dh0[PAD:CHUNK]
    g["final_norm_w"] = g["final_norm_w"].reshape(D_MODEL)
    g["ret_gn_w"] = g["ret_gn_w"].reshape(RET_DV)
    g["dn_norm_w"] = g["dn_norm_w"].reshape(DN_DV)
    return loss, dh0, g, red


def _mesh_pos():
    return lax.axis_index("x"), lax.axis_index("y"), lax.axis_index("c")


def _other_chips(x, y):
    return [(1 - x, y), (x, 1 - y), (1 - x, 1 - y)]


def _remote(src, dst, send_sem, recv_sem, to):
    return pltpu.make_async_remote_copy(src_ref=src, dst_ref=dst, send_sem=send_sem, recv_sem=recv_sem,
                                        device_id=to, device_id_type=MESH)


GATHER_COPIES = 7


def _gather_phase(phase, ins, outs, send_sems, recv_sems):
    x, y, c = _mesh_pos()
    me = 2 * x + y
    chips = _other_chips(x, y)
    sibling = (x, y, 1 - c)

    def cp(t, k, src, dst, to):
        i = GATHER_COPIES * t + k
        return _remote(src, dst, send_sems.at[i], recv_sems.at[i], to)

    for t in range(len(ins)):
        own = cp(t, 0, ins[t], outs[t].at[me], sibling)
        if phase == 0:
            own.start()
        if phase == 2:
            own.wait()
        for k, (px, py) in enumerate(chips):
            landed = outs[t].at[2 * px + py, c]
            theirs = outs[t].at[2 * px + py, 1 - c]
            to_chip = cp(t, 1 + k, ins[t].at[c], outs[t].at[me, c], (px, py, c))
            if phase == 0:
                to_chip.start()
            if phase == 1:
                cp(t, 1 + k, ins[t].at[c], landed, (px, py, c)).wait_recv()
                cp(t, 4 + k, landed, landed, sibling).start()
            if phase == 2:
                to_chip.wait_send()
                cp(t, 4 + k, landed, landed, sibling).wait_send()
                cp(t, 4 + k, theirs, theirs, sibling).wait_recv()


def _chips_phase(phase, ins, outs, send_sems, recv_sems):
    x, y, c = _mesh_pos()
    for t in range(len(ins)):
        for k, (px, py) in enumerate(_other_chips(x, y)):
            cp = _remote(ins[t].at[2 * px + py], outs[t].at[k], send_sems.at[3 * t + k], recv_sems.at[3 * t + k],
                         (px, py, c))
            if phase == 0:
                cp.start()
            if phase == 2:
                cp.wait()


class _Ride:
    def __init__(self, kind, arrays):
        self.kind, self.arrays = kind, list(arrays)
        nt = len(self.arrays)
        if kind == "gather":
            self.phase_fn, n_sem = _gather_phase, GATHER_COPIES * nt
            self.out_shape = [jax.ShapeDtypeStruct((N_SHARD,) + a.shape, a.dtype) for a in self.arrays]
        elif kind == "pair":
            self.phase_fn, n_sem = _pair_phase, nt
            self.out_shape = [jax.ShapeDtypeStruct(a.shape[:1] + a.shape[2:], a.dtype) for a in self.arrays]
        else:
            self.phase_fn, n_sem = _chips_phase, 3 * nt
            self.out_shape = [jax.ShapeDtypeStruct((3,) + a.shape[1:], a.dtype) for a in self.arrays]
        self.in_specs, self.out_specs = [ANY] * nt, [ANY] * nt
        self.scratch = [pltpu.SemaphoreType.DMA((n_sem,)), pltpu.SemaphoreType.DMA((n_sem,))]

    def emit(self, step, nsteps, ins, outs, send_sems, recv_sems, before):
        mid = max(0, min((7 * nsteps) // 8, nsteps - 2))
        todo = [(0, 0), (1, mid)] if before else [(2, nsteps - 1)]
        for phase, at in todo:
            if phase == 1 and self.kind != "gather":
                continue

            @pl.when(step == at)
            def _(phase=phase):
                self.phase_fn(phase, ins, outs, send_sems, recv_sems)


def _gather_small(blk):
    r, wd = blk.shape

    def body(b_ref, out_ref, send_sems, recv_sems):
        x, y, c = _mesh_pos()
        chips = _other_chips(x, y)
        out_ref[2 * x + y] = b_ref[...]
        sends = [_remote(b_ref, out_ref.at[2 * x + y], send_sems.at[k], recv_sems.at[k], (px, py, c))
                 for k, (px, py) in enumerate(chips)]
        for cp in sends:
            cp.start()
        for k, (px, py) in enumerate(chips):
            _remote(b_ref, out_ref.at[2 * px + py], send_sems.at[k], recv_sems.at[k], (px, py, c)).wait_recv()
        for cp in sends:
            cp.wait_send()

    return pl.pallas_call(
        body, out_shape=jax.ShapeDtypeStruct((4, r, wd), blk.dtype), in_specs=[VMEM_SPEC], out_specs=VMEM_SPEC,
        scratch_shapes=[pltpu.SemaphoreType.DMA((3,)), pltpu.SemaphoreType.DMA((3,))],
        name="gather_small")(blk)


def _allreduce_small(blk):
    r, wd = blk.shape
    rels = [(dx, dy, dc) for dx in (0, 1) for dy in (0, 1) for dc in (0, 1) if dx or dy or dc]

    def body(b_ref, out_ref, buf_ref, send_sems, recv_sems):
        x, y, c = _mesh_pos()

        def peer(rel):
            dx, dy, dc = rel
            return (1 - x if dx else x, 1 - y if dy else y, 1 - c if dc else c)

        me = 4 * x + 2 * y + c
        buf_ref[me] = b_ref[...]
        sends = [_remote(b_ref, buf_ref.at[me], send_sems.at[k], recv_sems.at[k], peer(rel))
                 for k, rel in enumerate(rels)]
        for cp in sends:
            cp.start()
        for k, rel in enumerate(rels):
            px, py, pc = peer(rel)
            _remote(b_ref, buf_ref.at[4 * px + 2 * py + pc], send_sems.at[k], recv_sems.at[k],
                    (px, py, pc)).wait_recv()
        for cp in sends:
            cp.wait_send()
        acc = buf_ref[0]
        for d in range(1, 8):
            acc = acc + buf_ref[d]
        out_ref[...] = acc

    return pl.pallas_call(
        body, out_shape=jax.ShapeDtypeStruct((r, wd), blk.dtype), in_specs=[VMEM_SPEC], out_specs=VMEM_SPEC,
        scratch_shapes=[pltpu.VMEM((8, r, wd), blk.dtype), pltpu.SemaphoreType.DMA((7,)),
                        pltpu.SemaphoreType.DMA((7,))],
        name="allreduce_small")(blk)


def _rs_pair(gs, name):
    ride = _Ride("pair", gs)

    def body(*refs):
        nt = len(gs)
        for phase in (0, 2):
            _pair_phase(phase, refs[:nt], refs[nt:2 * nt], *refs[2 * nt:])

    return pl.pallas_call(body, out_shape=ride.out_shape, in_specs=ride.in_specs, out_specs=ride.out_specs,
                          scratch_shapes=ride.scratch, name=name)(*gs)


def _pair_phase(phase, ins, outs, send_sems, recv_sems):
    x, y, c = _mesh_pos()
    for t in range(len(ins)):
        cp = _remote(ins[t].at[:, 1 - c], outs[t], send_sems.at[t], recv_sems.at[t], (x, y, 1 - c))
        if phase == 0:
            cp.start()
        if phase == 2:
            cp.wait()


def _rs_tile(a, b):
    return _div_tile(a, 512 if b <= 1024 else 256, 16)


def _rs_pair_add(g, a, idx, name):
    _, _, rows, cols = g.shape
    tr = _rs_tile(rows, cols)

    def body(s_ref, g_ref, a_ref, p_ref):
        p_ref[...] = (g_ref[...] + a_ref[...]).astype(p_ref.dtype)

    blk = pl.BlockSpec((None, tr, cols), lambda j, i, s: (j, i, 0))
    spec = pltpu.PrefetchScalarGridSpec(
        num_scalar_prefetch=1, grid=(N_SHARD, rows // tr),
        in_specs=[pl.BlockSpec((None, None, tr, cols), lambda j, i, s: (j, s[0], i, 0)), blk], out_specs=blk)
    return pl.pallas_call(
        body, grid_spec=spec, out_shape=jax.ShapeDtypeStruct((N_SHARD, rows, cols), BF16), name=name,
        compiler_params=_params("parallel", "parallel"))(idx, g, a)


def _rs_final_add(g, a, b, idx, name):
    _, _, rows, cols = g.shape
    tr = _rs_tile(rows, cols)

    def body(s_ref, g_ref, a_ref, b0_ref, b1_ref, b2_ref, f_ref):
        own = g_ref[...] + a_ref[...]
        f_ref[...] = ((own + b0_ref[...].astype(F32)) + b1_ref[...].astype(F32)) + b2_ref[...].astype(F32)

    def b_spec(k):
        return pl.BlockSpec((None, tr, cols), lambda i, s: (k, i, 0))

    spec = pltpu.PrefetchScalarGridSpec(
        num_scalar_prefetch=1, grid=(rows // tr,),
        in_specs=[pl.BlockSpec((None, None, tr, cols), lambda i, s: (s[1], s[0], i, 0)),
                  pl.BlockSpec((None, tr, cols), lambda i, s: (s[1], i, 0)), b_spec(0), b_spec(1), b_spec(2)],
        out_specs=pl.BlockSpec((None, tr, cols), lambda i, s: (s[0], i, 0)))
    return pl.pallas_call(
        body, grid_spec=spec, out_shape=jax.ShapeDtypeStruct((2, rows, cols), F32), name=name,
        compiler_params=_params("parallel"))(idx, g, a, b, b, b)


def _rs_share(fs, name):
    nt = len(fs)

    def body(*refs):
        outs = refs[nt:2 * nt]
        send_sems, recv_sems = refs[2 * nt:]
        x, y, c = _mesh_pos()
        cps = [_remote(outs[t].at[c], outs[t].at[c], send_sems.at[t], recv_sems.at[t], (x, y, 1 - c))
               for t in range(nt)]
        for cp in cps:
            cp.start()
        for cp in cps:
            cp.wait()

    return pl.pallas_call(
        body, out_shape=[jax.ShapeDtypeStruct(f.shape, f.dtype) for f in fs],
        in_specs=[ANY] * nt, out_specs=[ANY] * nt, input_output_aliases={t: t for t in range(nt)},
        scratch_shapes=[pltpu.SemaphoreType.DMA((nt,)), pltpu.SemaphoreType.DMA((nt,))], name=name)(*fs)


def _adamw(w, g, m, v, name):
    lead, rows, cols = w.shape
    tr = rows // 4 if rows % 32 == 0 else rows

    def body(w_ref, g_ref, m_ref, v_ref, go_ref, d_ref, mo_ref, vo_ref):
        gv = g_ref[...]
        go_ref[...] = gv
        mn = ADAM_B1 * m_ref[...] + (1.0 - ADAM_B1) * gv
        vn = ADAM_B2 * v_ref[...] + (1.0 - ADAM_B2) * (gv * gv)
        m_hat = mn / (1.0 - ADAM_B1 ** ADAM_STEP)
        v_hat = vn / (1.0 - ADAM_B2 ** ADAM_STEP)
        d_ref[...] = -ADAM_LR * (m_hat / (jnp.sqrt(v_hat) + ADAM_EPS) + ADAM_WD * w_ref[...])
        mo_ref[...] = mn
        vo_ref[...] = vn

    blk = pl.BlockSpec((None, tr, cols), lambda l, i: (l, i, 0))
    out = jax.ShapeDtypeStruct((lead, rows, cols), F32)
    return pl.pallas_call(
        body, grid=(lead, rows // tr), in_specs=[blk] * 4, out_specs=[blk] * 4, out_shape=[out] * 4, name=name,
        compiler_params=_params("parallel", "parallel"))(w, g, m, v)


BIG = ["ret_w_in", "ret_w_out", "dn_w_in", "dn_w_out", "ffn_w_gate", "ffn_w_up", "ffn_w_down"]
TRANSPOSED_AT_BOUNDARY = {"dn_w_in": True, "ffn_w_gate": False, "ffn_w_up": False}
SMALL =["meta_tokens", "mix_norm_w", "ffn_norm_w", "ret_gn_w", "dn_conv_w", "dn_a_log", "dn_dt_bias",
         "dn_norm_w", "final_norm_w"]
SMALL_SHARDED = {"meta_tokens", "dn_conv_w", "dn_norm_w"}
ORDER = ["meta_tokens", "mix_norm_w", "ffn_norm_w", "ret_w_in", "ret_gn_w", "ret_w_out", "dn_w_in",
         "dn_conv_w", "dn_a_log", "dn_dt_bias", "dn_norm_w", "dn_w_out", "ffn_w_gate", "ffn_w_up",
         "ffn_w_down", "final_norm_w"]


def _halves(a):
    return a.reshape(2, -1, a.shape[-1])


def _pack_lanes(parts, align=8):
    flat = jnp.concatenate([p.reshape(-1) for p in parts])
    flat = jnp.pad(flat, (0, -flat.shape[0] % (align * LANES)))
    return flat.reshape(-1, LANES)


def _unpack(buf, shapes):
    lead = buf.shape[:-2]
    flat = buf.reshape(lead + (-1,))
    out, off = [], 0
    for shp in shapes:
        size = math.prod(shp)
        out.append(flat[..., off:off + size].reshape(lead + tuple(shp)))
        off += size
    return out


def _join_cols(shards):
    return jnp.concatenate([shards[j] for j in range(N_SHARD)], axis=-1)


def kernel(x, meta_tokens, mix_norm_w, ffn_norm_w, ret_w_in, ret_gn_w, ret_w_out, dn_w_in, dn_conv_w, dn_a_log, dn_dt_bias, dn_norm_w, dn_w_out, ffn_w_gate, ffn_w_up, ffn_w_down, final_norm_w, loss_target, m_meta_tokens, m_mix_norm_w, m_ffn_norm_w, m_ret_w_in, m_ret_gn_w, m_ret_w_out, m_dn_w_in, m_dn_conv_w, m_dn_a_log, m_dn_dt_bias, m_dn_norm_w, m_dn_w_out, m_ffn_w_gate, m_ffn_w_up, m_ffn_w_down, m_final_norm_w, v_meta_tokens, v_mix_norm_w, v_ffn_norm_w, v_ret_w_in, v_ret_gn_w, v_ret_w_out, v_dn_w_in, v_dn_conv_w, v_dn_a_log, v_dn_dt_bias, v_dn_norm_w, v_dn_w_out, v_ffn_w_gate, v_ffn_w_up, v_ffn_w_down, v_final_norm_w):
    w = dict(meta_tokens=meta_tokens, mix_norm_w=mix_norm_w, ffn_norm_w=ffn_norm_w, ret_w_in=ret_w_in,
             ret_gn_w=ret_gn_w, ret_w_out=ret_w_out, dn_w_in=dn_w_in, dn_conv_w=dn_conv_w, dn_a_log=dn_a_log,
             dn_dt_bias=dn_dt_bias, dn_norm_w=dn_norm_w, dn_w_out=dn_w_out, ffn_w_gate=ffn_w_gate,
             ffn_w_up=ffn_w_up, ffn_w_down=ffn_w_down, final_norm_w=final_norm_w)
    m = dict(meta_tokens=m_meta_tokens, mix_norm_w=m_mix_norm_w, ffn_norm_w=m_ffn_norm_w, ret_w_in=m_ret_w_in,
             ret_gn_w=m_ret_gn_w, ret_w_out=m_ret_w_out, dn_w_in=m_dn_w_in, dn_conv_w=m_dn_conv_w,
             dn_a_log=m_dn_a_log, dn_dt_bias=m_dn_dt_bias, dn_norm_w=m_dn_norm_w, dn_w_out=m_dn_w_out,
             ffn_w_gate=m_ffn_w_gate, ffn_w_up=m_ffn_w_up, ffn_w_down=m_ffn_w_down, final_norm_w=m_final_norm_w)
    v = dict(meta_tokens=v_meta_tokens, mix_norm_w=v_mix_norm_w, ffn_norm_w=v_ffn_norm_w, ret_w_in=v_ret_w_in,
             ret_gn_w=v_ret_gn_w, ret_w_out=v_ret_w_out, dn_w_in=v_dn_w_in, dn_conv_w=v_dn_conv_w,
             dn_a_log=v_dn_a_log, dn_dt_bias=v_dn_dt_bias, dn_norm_w=v_dn_norm_w, dn_w_out=v_dn_w_out,
             ffn_w_gate=v_ffn_w_gate, ffn_w_up=v_ffn_w_up, ffn_w_down=v_ffn_w_down, final_norm_w=v_final_norm_w)
    mx, my, mc = _mesh_pos()
    chip = 2 * mx + my

    sm_names = [n for n in SMALL if n in SMALL_SHARDED]
    sm_gathered = _unpack(_gather_small(_pack_lanes([w[n] for n in sm_names])), [w[n].shape for n in sm_names])
    full = {n: _join_cols(sm_gathered[i]) for i, n in enumerate(sm_names)}
    wts = {
        "meta_tokens": full["meta_tokens"], "mix_norm_w": mix_norm_w, "ffn_norm_w": ffn_norm_w,
        "ret_gn_w": ret_gn_w[0], "final_norm_w": final_norm_w, "dn_conv_w": full["dn_conv_w"][0],
        "dn_a_log": dn_a_log[0], "dn_dt_bias": dn_dt_bias[0], "dn_norm_w": full["dn_norm_w"][0],
    }
    idx = jnp.stack([mc, chip]).astype(jnp.int32)
    shards = {n: _halves(w[n].astype(MXU_DTYPE)) for n in BIG}
    loss_part, dh0, g, reduced = _train_step(x[0], loss_target[0], wts, shards, idx)
    seq = x.shape[1]
    grad_x = dh0[CHUNK:CHUNK + seq].reshape(x.shape)
    gsh = {}

    small_full_shapes = [g[n].shape for n in SMALL] + [(1,)]
    red = _unpack(_allreduce_small(_pack_lanes([g[n] for n in SMALL] + [loss_part[0, :1]])), small_full_shapes)
    loss = red[-1][0]
    for i, n in enumerate(SMALL):
        gn = red[i]
        if n in SMALL_SHARDED:
            width = w[n].shape[-1]
            gn = lax.dynamic_slice_in_dim(gn, chip * width, width, axis=gn.ndim - 1)
        gsh[n] = gn.reshape(w[n].shape)

    delta, new_m, new_v = {}, {}, {}
    for n in BIG:
        shp = w[n].shape
        if n in TRANSPOSED_AT_BOUNDARY and TRANSPOSED_AT_BOUNDARY[n]:
            view = lambda a: jnp.swapaxes(a, 1, 2).reshape(1, -1, LANES)
            back = lambda a: jnp.swapaxes(a.reshape(shp[0], shp[2], shp[1]), 1, 2)
        elif n in TRANSPOSED_AT_BOUNDARY:
            view = back = lambda a: jnp.swapaxes(a, 1, 2)
        else:
            view = back = lambda a: a
        res = _adamw(view(w[n]), view(reduced[n].reshape(shp)), view(m[n]), view(v[n]), "adamw_" + n)
        gsh[n], delta[n], new_m[n], new_v[n] = [back(r) for r in res]
    sm_local_shapes = [w[n].shape for n in SMALL]
    _, d_, m_, v_ = _adamw(*[_pack_lanes([t[n] for n in SMALL])[None] for t in (w, gsh, m, v)], "adamw_small")
    d_, m_, v_ = d_[0], m_[0], v_[0]
    for n, dd, mm, vv in zip(SMALL, _unpack(d_, sm_local_shapes), _unpack(m_, sm_local_shapes),
                             _unpack(v_, sm_local_shapes)):
        delta[n], new_m[n], new_v[n] = dd, mm, vv

    return (loss, grad_x, *[gsh[n] for n in ORDER], *[delta[n] for n in ORDER],
            *[new_m[n] for n in ORDER], *[new_v[n] for n in ORDER])
```

```python
import functools
import math

import jax
import jax.numpy as jnp
from jax import lax
from jax.experimental import pallas as pl
from jax.experimental.pallas import tpu as pltpu

F32 = jnp.float32
BF16 = jnp.bfloat16
MXU_DTYPE = BF16

D_MODEL = 1024
N_META = 16
CHUNK = 64
PAD = CHUNK - N_META
RMS_EPS = 1e-6
RET_HEADS, RET_DK, RET_DV = 4, 256, 512
RET_QK, RET_V = RET_HEADS * RET_DK, RET_HEADS * RET_DV
RET_IN = 2 * RET_QK + 2 * RET_V
ROPE_BASE = 10000.0
DN_HEADS, DN_DK, DN_DV = 8, 128, 256
DN_QK, DN_V = DN_HEADS * DN_DK, DN_HEADS * DN_DV
DN_CONV_CH = 2 * DN_QK + DN_V
DN_IN = DN_CONV_CH + DN_V + 2 * DN_HEADS
LANES = 128
DN_IN_USED = DN_CONV_CH + DN_V + LANES
DN_IN_PAD = DN_IN_USED + LANES
CONV_K = 4
FFN_HIDDEN = 2816
ADAM_LR, ADAM_B1, ADAM_B2, ADAM_EPS, ADAM_WD, ADAM_STEP = 0.001, 0.9, 0.999, 1e-08, 0.01, 10

ROW_ALIGN = 256
VMEM_LIMIT = 56 * 1024 * 1024
MESH = pl.DeviceIdType.MESH
ANY = pl.BlockSpec(memory_space=pl.ANY)
VMEM_SPEC = pl.BlockSpec(memory_space=pltpu.VMEM)
_HI = lax.Precision.HIGHEST


def _params(*sem):
    return pltpu.CompilerParams(dimension_semantics=sem, vmem_limit_bytes=VMEM_LIMIT)


def _dg(a, b, ca, cb, hi):
    dims = (((ca,), (cb,)), ((), ()))

    def dot(p, q):
        return lax.dot_general(p, q, dims, preferred_element_type=F32)

    if not hi:
        return dot(a.astype(MXU_DTYPE), b.astype(MXU_DTYPE))
    if MXU_DTYPE == F32:
        return lax.dot_general(a, b, dims, precision=_HI, preferred_element_type=F32)
    a_hi, b_hi = a.astype(MXU_DTYPE), b.astype(MXU_DTYPE)
    a_lo = (a - a_hi.astype(F32)).astype(MXU_DTYPE)
    b_lo = (b - b_hi.astype(F32)).astype(MXU_DTYPE)
    return dot(a_hi, b_hi) + (dot(a_hi, b_lo) + dot(a_lo, b_hi))


def _nn(a, b, hi=False):
    return _dg(a, b, 1, 0, hi)


def _nt(a, b, hi=False):
    return _dg(a, b, 1, 1, hi)


def _tn(a, b, hi=False):
    return _dg(a, b, 0, 0, hi)


def _iota(shape, dim):
    return lax.broadcasted_iota(jnp.int32, shape, dim)


def _valid_rows(first_row, rows, seq):
    r = first_row + _iota((rows, 1), 0)
    return ((r >= PAD) & (r < CHUNK + seq)).astype(F32)


def _rope(t, cs, sn):
    half = t.shape[-1] // 2
    t1, t2 = t[:, :half], t[:, half:]
    return jnp.concatenate([t1 * cs - t2 * sn, t1 * sn + t2 * cs], axis=1)


def _rope_bwd(d, cs, sn):
    half = d.shape[-1] // 2
    d1, d2 = d[:, :half], d[:, half:]
    return jnp.concatenate([d1 * cs + d2 * sn, d2 * cs - d1 * sn], axis=1)


def _col(x, idx):
    oh = (_iota((1, x.shape[1]), 1) == idx).astype(F32)
    return jnp.sum(x * oh, axis=1, keepdims=True)


def _row(x, idx):
    oh = (_iota((x.shape[0], 1), 0) == idx).astype(F32)
    return jnp.sum(x * oh, axis=0, keepdims=True)


def _shift_down(x, halo8, k):
    xr = pltpu.roll(x, k, 0)
    hr = pltpu.roll(halo8, k, 0)
    first = jnp.where(_iota((8, 1), 0) < k, hr, xr[0:8])
    return jnp.concatenate([first, xr[8:]], axis=0)


def _shift_up(x, next8, j):
    rows = x.shape[0]
    xr = pltpu.roll(x, rows - j, 0)
    nr = pltpu.roll(next8, 8 - j, 0)
    last = jnp.where(_iota((8, 1), 0) >= 8 - j, nr, xr[rows - 8:])
    return jnp.concatenate([xr[:rows - 8], last], axis=0)


def _gated_norm(o, gate, w):
    r = lax.rsqrt(jnp.mean(o * o, axis=-1, keepdims=True) + RMS_EPS)
    return o * r * w * (gate * jax.nn.sigmoid(gate))


def _gated_norm_bwd(dy, o, gate, w):
    r = lax.rsqrt(jnp.mean(o * o, axis=-1, keepdims=True) + RMS_EPS)
    nrm = o * r
    sg = jax.nn.sigmoid(gate)
    sl = gate * sg
    dgate = dy * nrm * w * (sg * (1.0 + gate * (1.0 - sg)))
    dn = dy * w * sl
    dw = jnp.sum(dy * nrm * sl, axis=0, keepdims=True)
    do = r * (dn - nrm * jnp.mean(dn * nrm, axis=-1, keepdims=True))
    return do, dgate, dw


def _softplus(z):
    return jnp.maximum(z, 0.0) + jnp.log(1.0 + jnp.exp(-jnp.abs(z)))


def _row_tile(rows, cap=768):
    for t in (768, 512, 256, 128, 64, 32, 16, 8):
        if t <= cap and rows % t == 0:
            return t
    return rows


TILE_BUDGET = 44 * 1024 * 1024


def _fit_rows(rows, row_bytes, fixed_bytes, value_row_bytes):
    best = None
    for t in range(LANES, rows + 1, LANES):
        if rows % t == 0 and 2 * (row_bytes * t + fixed_bytes) + value_row_bytes * t <= TILE_BUDGET:
            best = t
    return best or _row_tile(rows, 256)


def _div_tile(n, cap, mult):
    best = None
    for t in range(mult, min(cap, n) + 1, mult):
        if n % t == 0:
            best = t
    return best or n


def _col_tile(cols, cap=1536):
    best = None
    for t in range(LANES, min(cap, cols) + 1, LANES):
        if cols % t == 0:
            best = t
    return best or cols


def _rms_fwd(h, w, name, ride=None):
    rows, d = h.shape
    tm = _row_tile(rows)

    def body(h_ref, w_ref, o_ref):
        x = h_ref[...]
        r = lax.rsqrt(jnp.mean(x * x, axis=-1, keepdims=True) + RMS_EPS)
        o_ref[...] = (x * r * w_ref[...]).astype(o_ref.dtype)

    res, rode = _pcall(body, [h, w.reshape(1, d)], grid=(rows // tm,),
                       in_specs=[pl.BlockSpec((tm, d), lambda i: (i, 0)), pl.BlockSpec((1, d), lambda i: (0, 0))],
                       out_specs=[pl.BlockSpec((tm, d), lambda i: (i, 0))],
                       out_shape=[jax.ShapeDtypeStruct((rows, d), BF16)], name=name, sem=("parallel",), ride=ride)
    return res[0] if ride is None else (res[0], rode)


def _gmm_rms(name, grid, args, in_specs, row_spec, fn, h, w, resid, row_axis, red_axis=None, ride=None):
    m, d = h.shape
    n_in = len(args)
    vec = pl.BlockSpec((1, d), lambda *g: (0, 0))

    def body(*refs):
        ins = refs[:n_in]
        h_ref, w_ref, r_ref, dh_ref, dw_ref, dh16_ref = refs[n_in:]
        part = fn(*ins)
        row = pl.program_id(row_axis)

        def finish(dy):
            x = h_ref[...]
            r = lax.rsqrt(jnp.mean(x * x, axis=-1, keepdims=True) + RMS_EPS)
            xh = x * r
            dxh = dy * w_ref[...]
            dh = r_ref[...] + r * (dxh - xh * jnp.mean(dxh * xh, axis=-1, keepdims=True))
            dh_ref[...] = dh
            dh16_ref[...] = dh.astype(dh16_ref.dtype)
            dwp = jnp.sum(dy * xh, axis=0, keepdims=True)

            @pl.when(row == 0)
            def _():
                dw_ref[...] = dwp

            @pl.when(row > 0)
            def _():
                dw_ref[...] += dwp

        if red_axis is None:
            finish(part)
            return
        k = pl.program_id(red_axis)

        @pl.when(k == 0)
        def _():
            dh_ref[...] = part

        @pl.when(k > 0)
        def _():
            dh_ref[...] += part

        @pl.when(k == grid[red_axis] - 1)
        def _():
            finish(dh_ref[...])

    res, rode = _pcall(body, list(args) + [h, w.reshape(1, d), resid], grid=grid,
                       in_specs=list(in_specs) + [row_spec, vec, row_spec], out_specs=[row_spec, vec, row_spec],
                       out_shape=[jax.ShapeDtypeStruct((m, d), F32), jax.ShapeDtypeStruct((1, d), F32),
                                  jax.ShapeDtypeStruct((m, d), BF16)],
                       name=name, sem=("arbitrary",) * len(grid), ride=ride)
    return res if ride is None else (res, rode)


def _final_loss(h, w, tgt, seq, name):
    rows, d = h.shape
    tm = _row_tile(rows)

    def body(h_ref, w_ref, t_ref, dh_ref, dw_ref, loss_ref, dh16_ref):
        i = pl.program_id(0)
        r_idx = i * tm + _iota((tm, 1), 0)
        m = ((r_idx >= CHUNK) & (r_idx < CHUNK + seq)).astype(F32)
        x = h_ref[...]
        wv = w_ref[...]
        r = lax.rsqrt(jnp.mean(x * x, axis=-1, keepdims=True) + RMS_EPS)
        xh = x * r
        err = (xh * wv - t_ref[...]) * m
        lpart = 0.5 * jnp.sum(jnp.mean(err * err, axis=-1, keepdims=True), axis=0, keepdims=True)
        dyv = err * (1.0 / d)
        dxh = dyv * wv
        dh = r * (dxh - xh * jnp.mean(dxh * xh, axis=-1, keepdims=True))
        dh_ref[...] = dh
        dh16_ref[...] = dh.astype(dh16_ref.dtype)
        part = jnp.sum(dyv * xh, axis=0, keepdims=True)

        @pl.when(i == 0)
        def _():
            dw_ref[...] = part
            loss_ref[...] = jnp.broadcast_to(lpart, loss_ref.shape)

        @pl.when(i > 0)
        def _():
            dw_ref[...] += part
            loss_ref[...] += jnp.broadcast_to(lpart, loss_ref.shape)

    blk = pl.BlockSpec((tm, d), lambda i: (i, 0))
    vec = pl.BlockSpec((1, d), lambda i: (0, 0))
    return pl.pallas_call(
        body, grid=(rows // tm,), in_specs=[blk, vec, blk],
        out_specs=[blk, vec, pl.BlockSpec((1, LANES), lambda i: (0, 0)), blk],
        out_shape=[jax.ShapeDtypeStruct((rows, d), F32), jax.ShapeDtypeStruct((1, d), F32),
                   jax.ShapeDtypeStruct((1, LANES), F32), jax.ShapeDtypeStruct((rows, d), BF16)],
        name=name, compiler_params=_params("arbitrary"))(h, w.reshape(1, d), tgt)


def _isz(x):
    return jnp.dtype(x.dtype).itemsize


def _mm(a, b, *, mode, name, out_dtype=F32, resid=None, col_cap=1536, ride=None):
    if mode == "tn":
        m, k = a.shape
        n = b.shape[1]
        tn = _col_tile(n, col_cap)
        tm = _fit_rows(m, k * _isz(a) + tn * _isz(b), (3 * k * tn * 4) // 2, 2 * (k + tn))

        def body_tn(a_ref, b_ref, o_ref):
            i = pl.program_id(1)
            part = _tn(a_ref[...], b_ref[...])

            @pl.when(i == 0)
            def _():
                o_ref[...] = part

            @pl.when(i > 0)
            def _():
                o_ref[...] += part

        return pl.pallas_call(
            body_tn, grid=(n // tn, m // tm),
            in_specs=[pl.BlockSpec((tm, k), lambda j, i: (i, 0)),
                      pl.BlockSpec((tm, tn), lambda j, i: (i, j))],
            out_specs=pl.BlockSpec((k, tn), lambda j, i: (0, j)),
            out_shape=jax.ShapeDtypeStruct((k, n), F32), name=name,
            compiler_params=_params("parallel", "arbitrary"))(a, b)

    m, ka = a.shape
    n = b.shape[1] if mode == "nn" else b.shape[0]
    has_resid = resid is not None
    tn = _col_tile(n, col_cap)
    tm = _fit_rows(m, ka * _isz(a) + tn * (jnp.dtype(out_dtype).itemsize + (4 if has_resid else 0)),
                   ka * tn * _isz(b), 2 * ka + 8 * tn)

    def body(*refs):
        if has_resid:
            a_ref, b_ref, r_ref, o_ref = refs
        else:
            a_ref, b_ref, o_ref = refs
        acc = _nn(a_ref[...], b_ref[...]) if mode == "nn" else _nt(a_ref[...], b_ref[...])
        if has_resid:
            acc = acc + r_ref[...]
        o_ref[...] = acc.astype(o_ref.dtype)

    b_spec = (pl.BlockSpec((b.shape[0], tn), lambda j, i: (0, j)) if mode == "nn"
              else pl.BlockSpec((tn, b.shape[1]), lambda j, i: (j, 0)))
    o_spec = pl.BlockSpec((tm, tn), lambda j, i: (i, j))
    in_specs = [pl.BlockSpec((tm, ka), lambda j, i: (i, 0)), b_spec]
    args = [a, b]
    if has_resid:
        in_specs.append(o_spec)
        args.append(resid)
    res, rode = _pcall(body, args, grid=(n // tn, m // tm), in_specs=in_specs, out_specs=[o_spec],
                       out_shape=[jax.ShapeDtypeStruct((m, n), out_dtype)], name=name,
                       sem=("parallel", "parallel"), ride=ride)
    return res[0] if ride is None else (res[0], rode)


N_SHARD = 4


def _gmm(name, grid, args, in_specs, out_specs, out_shape, fn, red_axis=None, init_arg=None, aliases=None,
         ride=None):
    n_in = len(args)
    single = not isinstance(out_shape, (list, tuple))
    out_specs = [out_specs] if single else list(out_specs)
    out_shape = [out_shape] if single else list(out_shape)

    def body(*refs):
        _gmm_step(fn, refs[:n_in], refs[n_in:], red_axis, init_arg)

    sem = tuple("arbitrary" if ax == red_axis else "parallel" for ax in range(len(grid)))
    res, rode = _pcall(body, args, grid=grid, in_specs=in_specs, out_specs=out_specs, out_shape=out_shape,
                       name=name, sem=sem, aliases=aliases, ride=ride)
    ours = res[0] if single else res
    return ours if ride is None else (ours, rode)


def _gmm_step(fn, ins, outs, red_axis, init_arg):
    parts = fn(*ins)
    if red_axis is None:
        for o_ref, p in zip(outs, parts):
            o_ref[...] = p.astype(o_ref.dtype)
        return
    k = pl.program_id(red_axis)

    @pl.when(k == 0)
    def _():
        for idx, (o_ref, p) in enumerate(zip(outs, parts)):
            o_ref[...] = p + ins[init_arg][...] if (idx == 0 and init_arg is not None) else p

    @pl.when(k > 0)
    def _():
        for o_ref, p in zip(outs, parts):
            o_ref[...] += p


def _ride_body(ride, grid, n_in, n_out, n_scratch, body):
    n_rin, n_rout = len(ride.arrays), len(ride.out_shape)
    nsteps = math.prod(grid)

    def wrapped(*refs):
        ins = refs[:n_in]
        r_ins = refs[n_in:n_in + n_rin]
        o0 = n_in + n_rin
        outs = refs[o0:o0 + n_out]
        r_outs = refs[o0 + n_out:o0 + n_out + n_rout]
        s0 = o0 + n_out + n_rout
        scratch = refs[s0:s0 + n_scratch]
        send_sems, recv_sems = refs[-2:]
        step = pl.program_id(0)
        for ax in range(1, len(grid)):
            step = step * grid[ax] + pl.program_id(ax)
        ride.emit(step, nsteps, r_ins, r_outs, send_sems, recv_sems, before=True)
        body(*ins, *outs, *scratch)
        ride.emit(step, nsteps, r_ins, r_outs, send_sems, recv_sems, before=False)

    return wrapped


def _pcall(body, args, *, grid, in_specs, out_specs, out_shape, name, sem, scratch=(), aliases=None, ride=None):
    if ride is None:
        res = pl.pallas_call(body, grid=grid, in_specs=list(in_specs), out_specs=list(out_specs),
                             out_shape=list(out_shape), scratch_shapes=list(scratch), name=name,
                             input_output_aliases=aliases or {}, compiler_params=_params(*sem))(*args)
        return res, None
    n_in, n_out = len(args), len(out_shape)
    res = pl.pallas_call(
        _ride_body(ride, grid, n_in, n_out, len(scratch), body), grid=grid,
        in_specs=list(in_specs) + ride.in_specs, out_specs=list(out_specs) + ride.out_specs,
        out_shape=list(out_shape) + ride.out_shape, scratch_shapes=list(scratch) + ride.scratch, name=name,
        input_output_aliases=aliases or {},
        compiler_params=_params(*(("arbitrary",) * len(grid))))(*args, *ride.arrays)
    return res[:n_out], res[n_out:]


def _mm_cols(a, ws, name, ride=None):
    m, k = a.shape
    n = ws.shape[2]
    tm = _fit_rows(m, k * _isz(a) + n * 4, k * n * _isz(ws), 4 * n)
    return _gmm(name, (N_SHARD, m // tm), [a, ws],
                [pl.BlockSpec((tm, k), lambda j, i: (i, 0)), pl.BlockSpec((None, k, n), lambda j, i: (j, 0, 0))],
                pl.BlockSpec((tm, n), lambda j, i: (i, j)), jax.ShapeDtypeStruct((m, N_SHARD * n), F32),
                lambda a_ref, w_ref: (_nn(a_ref[...], w_ref[...]),), ride=ride)


def _mm_cols_t_rms(d, ws, h, w, resid, name, ride=None):
    m = d.shape[0]
    _, k, n = ws.shape
    tm = _fit_rows(m, n * _isz(d) + 3 * k * 4, k * n * _isz(ws), 16 * k)
    return _gmm_rms(name, (m // tm, N_SHARD), [d, ws],
                    [pl.BlockSpec((tm, n), lambda i, j: (i, j)), pl.BlockSpec((None, k, n), lambda i, j: (j, 0, 0))],
                    pl.BlockSpec((tm, k), lambda i, j: (i, 0)),
                    lambda d_ref, w_ref: _nt(d_ref[...], w_ref[...]), h, w, resid, 0, red_axis=1, ride=ride)


def _mm_nt_rms(a, b, h, w, resid, name, ride=None):
    m, n = a.shape
    k = b.shape[0]
    tm = _fit_rows(m, n * _isz(a) + 3 * k * 4, k * n * _isz(b), 16 * k)
    return _gmm_rms(name, (m // tm,), [a, b],
                    [pl.BlockSpec((tm, n), lambda i: (i, 0)), pl.BlockSpec((k, n), lambda i: (0, 0))],
                    pl.BlockSpec((tm, k), lambda i: (i, 0)),
                    lambda a_ref, b_ref: _nt(a_ref[...], b_ref[...]), h, w, resid, 0, ride=ride)


def _mm_cols_grad(a, d, name):
    m, k = a.shape
    n = d.shape[1] // N_SHARD
    tm = _fit_rows(m, k * _isz(a) + n * _isz(d), (3 * k * n * 4) // 2, 2 * (k + n))
    return _gmm(name, (N_SHARD, m // tm), [a, d],
                [pl.BlockSpec((tm, k), lambda j, i: (i, 0)), pl.BlockSpec((tm, n), lambda j, i: (i, j))],
                pl.BlockSpec((None, k, n), lambda j, i: (j, 0, 0)), jax.ShapeDtypeStruct((N_SHARD, k, n), F32),
                lambda a_ref, d_ref: (_tn(a_ref[...], d_ref[...]),), red_axis=1)


def _ffn_up(hn, wg, wu, layer, name):
    m, k = hn.shape
    n = wg.shape[3]
    tm = _fit_rows(m, k * _isz(hn) + 3 * n * jnp.dtype(BF16).itemsize, 2 * k * n * _isz(wg), 16 * n)

    def fn(a_ref, wg_ref, wu_ref):
        a = a_ref[...]
        g = _nn(a, wg_ref[...])
        u = _nn(a, wu_ref[...])
        return g, u, g * jax.nn.sigmoid(g) * u

    w_spec = pl.BlockSpec((None, None, k, n), lambda j, i: (j, layer, 0, 0))
    o_spec = pl.BlockSpec((None, tm, n), lambda j, i: (j, i, 0))
    out = jax.ShapeDtypeStruct((N_SHARD, m, n), BF16)
    return _gmm(name, (N_SHARD, m // tm), [hn, wg, wu],
                [pl.BlockSpec((tm, k), lambda j, i: (i, 0)), w_spec, w_spec],
                [o_spec, o_spec, o_spec], [out, out, out], fn)


def _ffn_down(act, wd, resid, layer, name):
    _, m, n = act.shape
    d = wd.shape[3]
    tm = _fit_rows(m, N_SHARD * n * _isz(act) + 2 * d * 4, N_SHARD * n * d * _isz(wd), 8 * d)

    def fn(a_ref, w_ref, r_ref):
        acc = r_ref[...]
        for j in range(N_SHARD):
            acc = acc + _nn(a_ref[j], w_ref[j])
        return (acc,)

    row = pl.BlockSpec((tm, d), lambda i: (i, 0))
    return _gmm(name, (m // tm,), [act, wd, resid],
                [pl.BlockSpec((N_SHARD, tm, n), lambda i: (0, i, 0)),
                 pl.BlockSpec((N_SHARD, None, n, d), lambda i: (0, layer, 0, 0)), row],
                row, jax.ShapeDtypeStruct((m, d), F32), fn)


def _ffn_down_bwd(dh, wd, g, u, layer, name, ride=None):
    m, d = dh.shape
    n = wd.shape[2]
    tm = _fit_rows(m, d * _isz(dh) + 4 * N_SHARD * n * jnp.dtype(BF16).itemsize, N_SHARD * n * d * _isz(wd),
                   2 * d + 24 * n)

    def body(dh_ref, wd_ref, g_ref, u_ref, dg_ref, du_ref):
        dhv = dh_ref[...].astype(MXU_DTYPE)
        for j in range(N_SHARD):
            dact = _nt(dhv, wd_ref[j])
            gv = g_ref[j].astype(F32)
            sg = jax.nn.sigmoid(gv)
            gs = gv * sg
            dg_ref[j] = (dact * u_ref[j].astype(F32) * (sg + gs * (1.0 - sg))).astype(dg_ref.dtype)
            du_ref[j] = (dact * gs).astype(du_ref.dtype)

    sh_spec = pl.BlockSpec((N_SHARD, tm, n), lambda i: (0, i, 0))
    out = jax.ShapeDtypeStruct((N_SHARD, m, n), BF16)
    res, rode = _pcall(body, [dh, wd, g, u], grid=(m // tm,),
                       in_specs=[pl.BlockSpec((tm, d), lambda i: (i, 0)),
                                 pl.BlockSpec((N_SHARD, None, n, d), lambda i: (0, layer, 0, 0)), sh_spec, sh_spec],
                       out_specs=[sh_spec, sh_spec], out_shape=[out, out], name=name, sem=("parallel",), ride=ride)
    return res if ride is None else (res, rode)


def _ffn_up_bwd(dg, du, wg, wu, layer, h, w, resid, name, ride=None):
    _, m, n = dg.shape
    k = wg.shape[2]
    tm = _fit_rows(m, 2 * N_SHARD * n * _isz(dg) + 3 * k * 4, 2 * N_SHARD * k * n * _isz(wg), 16 * k)

    def fn(dg_ref, du_ref, wg_ref, wu_ref):
        acc = _nt(dg_ref[0], wg_ref[0]) + _nt(du_ref[0], wu_ref[0])
        for j in range(1, N_SHARD):
            acc = acc + _nt(dg_ref[j], wg_ref[j]) + _nt(du_ref[j], wu_ref[j])
        return acc

    d_spec = pl.BlockSpec((N_SHARD, tm, n), lambda i: (0, i, 0))
    w_spec = pl.BlockSpec((N_SHARD, None, k, n), lambda i: (0, layer, 0, 0))
    return _gmm_rms(name, (m // tm,), [dg, du, wg, wu], [d_spec, d_spec, w_spec, w_spec],
                    pl.BlockSpec((tm, k), lambda i: (i, 0)), fn, h, w, resid, 0, ride=ride)


def _ffn_wgrad(lhs, rhs_list, layer, layers, prev, lhs_sharded, name):
    if lhs_sharded:
        _, m, k = lhs.shape
        n = rhs_list[0].shape[1]
    else:
        m, k = lhs.shape
        n = rhs_list[0].shape[2]
    n_out = len(rhs_list)
    tm = _fit_rows(m, k * _isz(lhs) + n_out * n * _isz(rhs_list[0]), (3 * n_out * k * n * 4) // 2,
                   2 * (k + n_out * n))
    sh = pl.BlockSpec((None, tm, k if lhs_sharded else n), lambda j, i: (j, i, 0))
    fl = pl.BlockSpec((tm, n if lhs_sharded else k), lambda j, i: (i, 0))
    n_out = len(rhs_list)
    args = [lhs] + list(rhs_list)
    in_specs = [sh if lhs_sharded else fl] + [fl if lhs_sharded else sh] * n_out
    aliases = None
    if prev is not None:
        aliases = {len(args) + t: t for t in range(n_out)}
        args = args + list(prev)
        in_specs = in_specs + [ANY] * n_out

    def fn(l_ref, *rest):
        lv = l_ref[...]
        return tuple(_tn(lv, r_ref[...]) for r_ref in rest[:n_out])

    o_spec = pl.BlockSpec((None, None, k, n), lambda j, i: (j, layer, 0, 0))
    out = jax.ShapeDtypeStruct((N_SHARD, layers, k, n), F32)
    return _gmm(name, (N_SHARD, m // tm), args, in_specs, [o_spec] * n_out, [out] * n_out, fn,
                red_axis=1, aliases=aliases)


def _ret_consts():
    log_gamma = jnp.log1p(-jnp.exp2(-5.0 - jnp.arange(RET_HEADS, dtype=F32)))
    idx = jnp.arange(CHUNK, dtype=F32)
    rel = idx[:, None] - idx[None, :]
    dmask = jnp.where((rel >= 0)[None], jnp.exp(log_gamma[:, None, None] * jnp.maximum(rel, 0.0)), 0.0)
    xi = jnp.exp(log_gamma[:, None] * (idx[None, :] + 1.0))[:, :, None]
    zeta = jnp.exp(log_gamma[:, None] * (CHUNK - 1.0 - idx[None, :]))[:, :, None]
    gamma_c = jnp.exp(log_gamma * CHUNK)
    wide = (RET_HEADS, CHUNK, RET_DK)
    return dmask, jnp.broadcast_to(xi, wide), jnp.broadcast_to(zeta, wide), gamma_c


def _rope_tables(nc):
    half = RET_DK // 2
    inv_freq = ROPE_BASE ** (-jnp.arange(half, dtype=F32) / half)
    a_chunk = (jnp.arange(nc) * CHUNK - PAD).astype(F32)[:, None] * inv_freq[None, :]
    a_row = jnp.arange(CHUNK).astype(F32)[:, None] * inv_freq[None, :]
    return (jnp.stack([jnp.cos(a_chunk), jnp.sin(a_chunk)], axis=1),
            jnp.stack([jnp.cos(a_row), jnp.sin(a_row)], axis=0))


RET_CPS = 2


def _rope_chunk(rc_ref, rr_ref, c):
    cc, sc = rc_ref[c, 0:1, :], rc_ref[c, 1:2, :]
    cr, sr = rr_ref[0], rr_ref[1]
    return cc * cr - sc * sr, sc * cr + cc * sr


def _rope_specs(order):
    half = RET_DK // 2
    return [pl.BlockSpec((RET_CPS, 2, half), lambda n: (order(n), 0, 0)),
            pl.BlockSpec((2, CHUNK, half), lambda n: (0, 0, 0))]


def _ret_specs(order):
    rows = RET_CPS * CHUNK
    return [pl.BlockSpec((rows, RET_QK), lambda n: (order(n), 0)),
            pl.BlockSpec((rows, RET_QK), lambda n: (order(n), 1)),
            pl.BlockSpec((rows, RET_V), lambda n: (order(n), 1)),
            pl.BlockSpec((rows, RET_V), lambda n: (order(n), 2))]


def _ret_const_specs():
    return [pl.BlockSpec((RET_HEADS, CHUNK, CHUNK), lambda n: (0, 0, 0)),
            pl.BlockSpec((RET_HEADS, CHUNK, RET_DK), lambda n: (0, 0, 0)),
            pl.BlockSpec((RET_HEADS, CHUNK, RET_DK), lambda n: (0, 0, 0)),
            pl.BlockSpec((1, RET_DV), lambda n: (0, 0))]


def _ret_fwd(proj, cos, sin, consts, gn_w, seq, ride=None):
    rows = proj.shape[0]
    nc = rows // CHUNK
    dmask, xi, zeta, gamma_c = consts

    def body(gam_ref, q_ref, k_ref, v_ref, g_ref, cos_ref, sin_ref, dm_ref, xi_ref, ze_ref, gn_ref,
             o_ref, y_ref, ss_ref, s_ref):
        n = pl.program_id(0)

        @pl.when(n == 0)
        def _():
            s_ref[...] = jnp.zeros_like(s_ref)

        gn = gn_ref[...]
        hs = range(RET_HEADS)
        qk_cols = [slice(h * RET_DK, (h + 1) * RET_DK) for h in hs]
        v_cols = [slice(h * RET_DV, (h + 1) * RET_DV) for h in hs]
        for c in range(RET_CPS):
            rs = slice(c * CHUNK, (c + 1) * CHUNK)
            cs, sn = _rope_chunk(cos_ref, sin_ref, c)
            kscale = _valid_rows((n * RET_CPS + c) * CHUNK, CHUNK, seq) * (RET_DK ** -0.5)
            qr_l = [_rope(q_ref[rs, col], cs, sn) for col in qk_cols]
            kr_l = [_rope(k_ref[rs, col], cs, sn) * kscale for col in qk_cols]
            v_l = [v_ref[rs, col] for col in v_cols]
            s_l = [s_ref[h] for h in hs]
            sc_l = [_nt(qr, kr) * dm_ref[h] for h, (qr, kr) in enumerate(zip(qr_l, kr_l))]
            o_l = [_nn(sc_l[h], v_l[h]) + _nn(qr_l[h] * xi_ref[h], s_l[h]) for h in hs]
            for h in hs:
                ss_ref[c, h] = s_l[h].astype(ss_ref.dtype)
                s_ref[h] = gam_ref[h] * s_l[h] + _tn(kr_l[h] * ze_ref[h], v_l[h])
                o_ref[rs, v_cols[h]] = o_l[h]
                y_ref[rs, v_cols[h]] = _gated_norm(o_l[h], g_ref[rs, v_cols[h]], gn).astype(y_ref.dtype)

    fwd = lambda n: n
    row_v = pl.BlockSpec((RET_CPS * CHUNK, RET_V), lambda n: (n, 0))
    res, rode = _pcall(
        body, [gamma_c, proj, proj, proj, proj, cos, sin, dmask, xi, zeta, gn_w.reshape(1, RET_DV)],
        grid=(nc // RET_CPS,),
        in_specs=[pl.BlockSpec(memory_space=pltpu.SMEM)] + _ret_specs(fwd) + _rope_specs(fwd)
        + _ret_const_specs(),
        out_specs=[row_v, row_v,
                   pl.BlockSpec((RET_CPS, RET_HEADS, RET_DK, RET_DV), lambda n: (n, 0, 0, 0))],
        out_shape=[jax.ShapeDtypeStruct((rows, RET_V), F32), jax.ShapeDtypeStruct((rows, RET_V), BF16),
                   jax.ShapeDtypeStruct((nc, RET_HEADS, RET_DK, RET_DV), BF16)],
        scratch=[pltpu.VMEM((RET_HEADS, RET_DK, RET_DV), F32)], name="ret_fwd", sem=("arbitrary",), ride=ride)
    return res if ride is None else (res, rode)


def _ret_bwd(proj, o, dy, states, cos, sin, consts, gn_w, seq, ride=None):
    rows = proj.shape[0]
    nc = rows // CHUNK
    dmask, xi, zeta, gamma_c = consts

    def body(gam_ref, q_ref, k_ref, v_ref, g_ref, o_ref, dy_ref, ss_ref, cos_ref, sin_ref,
             dm_ref, xi_ref, ze_ref, gn_ref, dp_ref, dgn_ref, ds_ref):
        n = pl.program_id(0)

        @pl.when(n == 0)
        def _():
            ds_ref[...] = jnp.zeros_like(ds_ref)
            dgn_ref[...] = jnp.zeros_like(dgn_ref)

        gn = gn_ref[...]
        dgn = jnp.zeros((1, RET_DV), F32)
        hs = range(RET_HEADS)
        qk_cols = [slice(h * RET_DK, (h + 1) * RET_DK) for h in hs]
        v_cols = [slice(h * RET_DV, (h + 1) * RET_DV) for h in hs]
        for c in reversed(range(RET_CPS)):
            rs = slice(c * CHUNK, (c + 1) * CHUNK)
            cs, sn = _rope_chunk(cos_ref, sin_ref, c)
            kscale = _valid_rows(((steps - 1 - n) * RET_CPS + c) * CHUNK, CHUNK, seq) * (RET_DK ** -0.5)
            qr_l = [_rope(q_ref[rs, col], cs, sn) for col in qk_cols]
            kr_l = [_rope(k_ref[rs, col], cs, sn) * kscale for col in qk_cols]
            v_l = [v_ref[rs, col] for col in v_cols]
            s_l = [ss_ref[c, h] for h in hs]
            ds_l = [ds_ref[h] for h in hs]
            sc_l = [_nt(qr_l[h], kr_l[h]) * dm_ref[h] for h in hs]
            gnb = [_gated_norm_bwd(dy_ref[rs, col], o_ref[rs, col], g_ref[rs, col], gn) for col in v_cols]
            do_l = [x[0] for x in gnb]
            dsc_l = [_nt(do_l[h], v_l[h]) * dm_ref[h] for h in hs]
            dv_l = [_tn(sc_l[h], do_l[h]) + _nn(kr_l[h] * ze_ref[h], ds_l[h]) for h in hs]
            dqr_l = [_nn(dsc_l[h], kr_l[h]) + _nt(do_l[h], s_l[h]) * xi_ref[h] for h in hs]
            dkr_l = [_tn(dsc_l[h], qr_l[h]) + _nt(v_l[h], ds_l[h]) * ze_ref[h] for h in hs]
            for h in hs:
                dgn = dgn + gnb[h][2]
                ds_ref[h] = gam_ref[h] * ds_l[h] + _tn(qr_l[h] * xi_ref[h], do_l[h])
                dp_ref[rs, qk_cols[h]] = _rope_bwd(dqr_l[h], cs, sn).astype(dp_ref.dtype)
                dp_ref[rs, RET_QK + h * RET_DK:RET_QK + (h + 1) * RET_DK] = (
                    _rope_bwd(dkr_l[h] * kscale, cs, sn).astype(dp_ref.dtype))
                dp_ref[rs, 2 * RET_QK + h * RET_DV:2 * RET_QK + (h + 1) * RET_DV] = dv_l[h].astype(dp_ref.dtype)
                dp_ref[rs, 2 * RET_QK + RET_V + h * RET_DV:2 * RET_QK + RET_V + (h + 1) * RET_DV] = (
                    gnb[h][1].astype(dp_ref.dtype))
        dgn_ref[...] += dgn

    steps = nc // RET_CPS
    rev = lambda n: steps - 1 - n
    row_v = pl.BlockSpec((RET_CPS * CHUNK, RET_V), lambda n: (rev(n), 0))
    res, rode = _pcall(
        body, [gamma_c, proj, proj, proj, proj, o, dy, states, cos, sin, dmask, xi, zeta,
               gn_w.reshape(1, RET_DV)],
        grid=(steps,),
        in_specs=[pl.BlockSpec(memory_space=pltpu.SMEM)] + _ret_specs(rev) + [
            row_v, row_v, pl.BlockSpec((RET_CPS, RET_HEADS, RET_DK, RET_DV), lambda n: (rev(n), 0, 0, 0))]
        + _rope_specs(rev) + _ret_const_specs(),
        out_specs=[pl.BlockSpec((RET_CPS * CHUNK, RET_IN), lambda n: (rev(n), 0)),
                   pl.BlockSpec((1, RET_DV), lambda n: (0, 0))],
        out_shape=[jax.ShapeDtypeStruct((rows, RET_IN), BF16), jax.ShapeDtypeStruct((1, RET_DV), F32)],
        scratch=[pltpu.VMEM((RET_HEADS, RET_DK, RET_DV), F32)], name="ret_bwd", sem=("arbitrary",), ride=ride)
    return res if ride is None else (res, rode)


GATE_COL = DN_CONV_CH // DN_V
BA_COL = (DN_CONV_CH + DN_V) // LANES
BETA_LANE, DECAY_LANE = 0, DN_HEADS
INV_SHIFT = 4
INV_SQUARINGS = INV_SHIFT - 1
assert CHUNK == 4 << INV_SHIFT


def _dn_in_specs(order, conv_saved=False):
    return [pl.BlockSpec((CHUNK, DN_CONV_CH), lambda n: (order(n), 0)),
            pl.BlockSpec((CHUNK, DN_CONV_CH), lambda n: (order(n), 0)) if conv_saved else
            pl.BlockSpec((8, DN_CONV_CH), lambda n: (jnp.maximum(order(n) * (CHUNK // 8) - 1, 0), 0)),
            pl.BlockSpec((CHUNK, DN_V), lambda n: (order(n), GATE_COL)),
            pl.BlockSpec((CHUNK, LANES), lambda n: (order(n), BA_COL)),
            pl.BlockSpec((CONV_K, 1, DN_CONV_CH), lambda n: (0, 0, 0)),
            pl.BlockSpec((1, LANES), lambda n: (0, 0)),
            pl.BlockSpec((1, LANES), lambda n: (0, 0)),
            pl.BlockSpec((1, DN_DV), lambda n: (0, 0))]


def _dn_front(c, seq, x_ref, halo_ref, ba_ref, cw_ref, al_ref, dt_ref, yc_ref=None):
    valid = _valid_rows(c * CHUNK, CHUNK, seq)
    xin = x_ref[...] * valid
    if yc_ref is None:
        halo = halo_ref[...] * _valid_rows(c * CHUNK - 8, 8, seq)
        yc = xin * cw_ref[CONV_K - 1]
        for k in range(1, CONV_K):
            yc = yc + _shift_down(xin, halo, k) * cw_ref[CONV_K - 1 - k]
    else:
        yc = yc_ref[...]
    sgc = jax.nn.sigmoid(yc)
    ba = ba_ref[...]
    sig = jax.nn.sigmoid(ba)
    beta = sig * valid
    z = ba + dt_ref[...]
    eal = jnp.exp(al_ref[...])
    g = -eal * _softplus(z) * valid
    ri, ci = _iota((CHUNK, CHUNK), 0), _iota((CHUNK, CHUNK), 1)
    lower = (ri >= ci).astype(F32)
    upper = (ri <= ci).astype(F32)
    eye = (ri == ci).astype(F32)
    gam = _nn(lower, g, hi=True)
    gam_t = _tn(g, upper, hi=True)
    return dict(valid=valid, xin=xin, yc=yc, sgc=sgc, act=yc * sgc, sig=sig, beta=beta, z=z,
                eal=eal, g=g, gam=gam, gam_t=gam_t, ri=ri, ci=ci, upper=upper, eye=eye)


def _dn_head(f, h):
    act = f["act"]
    q_raw = act[:, h * DN_DK:(h + 1) * DN_DK]
    k_raw = act[:, DN_QK + h * DN_DK:DN_QK + (h + 1) * DN_DK]
    v = act[:, 2 * DN_QK + h * DN_DV:2 * DN_QK + (h + 1) * DN_DV]
    rq = lax.rsqrt(jnp.sum(q_raw * q_raw, axis=-1, keepdims=True) + RMS_EPS)
    rk = lax.rsqrt(jnp.sum(k_raw * k_raw, axis=-1, keepdims=True) + RMS_EPS)
    qh = q_raw * rq
    kn = k_raw * rk
    gam_c = _col(f["gam"], DECAY_LANE + h)
    gam_r = _row(f["gam_t"], DECAY_LANE + h)
    bc = _col(f["beta"], BETA_LANE + h)
    diff = gam_c - gam_r
    decay = jnp.where(f["ri"] >= f["ci"], jnp.exp(jnp.minimum(diff, 0.0)), 0.0)
    glast = jnp.sum(gam_r * (_iota((1, CHUNK), 1) == CHUNK - 1).astype(F32), axis=1, keepdims=True)
    return dict(rq=rq, rk=rk, qh=qh, qn=qh * (DN_DK ** -0.5), kn=kn, v=v, gam_c=gam_c, gam_r=gam_r,
                bc=bc, diff=diff, decay=decay, egam=jnp.exp(gam_c), glast=glast,
                eglast=jnp.exp(glast), ekd=jnp.exp(glast - gam_c))


def _dn_fwd(proj, conv_w, alog, dtb, norm_w, seq):
    rows = proj.shape[0]
    nc = rows // CHUNK

    def body(x_ref, halo_ref, gate_ref, ba_ref, cw_ref, al_ref, dt_ref, nw_ref,
             o_ref, y_ref, ss_ref, t_ref, yc_ref, s_ref):
        n = pl.program_id(0)

        @pl.when(n == 0)
        def _():
            s_ref[...] = jnp.zeros_like(s_ref)

        f = _dn_front(n, seq, x_ref, halo_ref, ba_ref, cw_ref, al_ref, dt_ref)
        yc_ref[...] = f["yc"]
        ri, ci = f["ri"], f["ci"]
        eye = f["eye"]
        diag_m = (jnp.right_shift(ri, INV_SHIFT) == jnp.right_shift(ci, INV_SHIFT)).astype(F32)
        half_m = (jnp.right_shift(ri, INV_SHIFT + 1) == jnp.right_shift(ci, INV_SHIFT + 1)).astype(F32)
        nw = nw_ref[...]
        heads = [_dn_head(f, h) for h in range(DN_HEADS)]
        a_all = [jnp.where(ri > ci, hd["bc"] * _nt(hd["kn"], hd["kn"]) * hd["decay"], 0.0) for hd in heads]
        b_all = [a * diag_m for a in a_all]
        t_all = [eye - b for b in b_all]
        for _ in range(INV_SQUARINGS):
            b_all = [_nn(b, b, hi=True) for b in b_all]
            t_all = [t + _nn(t, b, hi=True) for t, b in zip(t_all, b_all)]
        for off_m in (half_m - diag_m, 1.0 - half_m):
            x_all = [_nn(a * off_m, t, hi=True) for a, t in zip(a_all, t_all)]
            t_all = [t - _nn(t, x, hi=True) for t, x in zip(t_all, x_all)]
        u_all = [_nn(t, hd["v"] * hd["bc"], hi=True) for t, hd in zip(t_all, heads)]
        w_all = [_nn(t, hd["kn"] * (hd["bc"] * hd["egam"]), hi=True) for t, hd in zip(t_all, heads)]
        s_all = [s_ref[h] for h in range(DN_HEADS)]
        qk_all = [_nt(hd["qn"], hd["kn"]) * hd["decay"] for hd in heads]
        os_all = [_nn(hd["qn"] * hd["egam"], s) for hd, s in zip(heads, s_all)]
        vnew_all = [u - _nn(w, s) for u, w, s in zip(u_all, w_all, s_all)]
        o_all = [os + _nn(qk, vn) for os, qk, vn in zip(os_all, qk_all, vnew_all)]
        snew_all = [s * hd["eglast"] + _tn(hd["kn"] * hd["ekd"], vn)
                    for s, hd, vn in zip(s_all, heads, vnew_all)]
        for h in range(DN_HEADS):
            v_cols = slice(h * DN_DV, (h + 1) * DN_DV)
            t_ref[0, h] = t_all[h]
            ss_ref[0, h] = s_all[h]
            s_ref[h] = snew_all[h]
            o_ref[:, v_cols] = o_all[h]
            y_ref[:, v_cols] = _gated_norm(o_all[h], gate_ref[:, v_cols], nw).astype(y_ref.dtype)

    fwd = lambda n: n
    row_v = pl.BlockSpec((CHUNK, DN_V), lambda n: (n, 0))
    return pl.pallas_call(
        body, grid=(nc,), in_specs=_dn_in_specs(fwd),
        out_specs=[row_v, row_v,
                   pl.BlockSpec((1, DN_HEADS, DN_DK, DN_DV), lambda n: (n, 0, 0, 0)),
                   pl.BlockSpec((1, DN_HEADS, CHUNK, CHUNK), lambda n: (n, 0, 0, 0)),
                   pl.BlockSpec((CHUNK, DN_CONV_CH), lambda n: (n, 0))],
        out_shape=[jax.ShapeDtypeStruct((rows, DN_V), F32), jax.ShapeDtypeStruct((rows, DN_V), BF16),
                   jax.ShapeDtypeStruct((nc, DN_HEADS, DN_DK, DN_DV), F32),
                   jax.ShapeDtypeStruct((nc, DN_HEADS, CHUNK, CHUNK), F32),
                   jax.ShapeDtypeStruct((rows, DN_CONV_CH), F32)],
        scratch_shapes=[pltpu.VMEM((DN_HEADS, DN_DK, DN_DV), F32)],
        name="dn_fwd", compiler_params=_params("arbitrary"))(
            proj, proj, proj, proj, conv_w, alog, dtb, norm_w.reshape(1, DN_DV))


def _dn_bwd(proj, conv_out, o, dy, states, tinv, conv_w, alog, dtb, norm_w, seq):
    rows = proj.shape[0]
    nc = rows // CHUNK

    def body(x_ref, yc_ref, gate_ref, ba_ref, cw_ref, al_ref, dt_ref, nw_ref,
             o_ref, dy_ref, ss_ref, t_ref,
             dp_ref, dcw_ref, dal_ref, ddt_ref, dnw_ref, ds_ref, nxt_ref):
        n = pl.program_id(0)

        @pl.when(n == 0)
        def _():
            ds_ref[...] = jnp.zeros_like(ds_ref)
            nxt_ref[...] = jnp.zeros_like(nxt_ref)
            dcw_ref[...] = jnp.zeros_like(dcw_ref)
            dal_ref[...] = jnp.zeros_like(dal_ref)
            ddt_ref[...] = jnp.zeros_like(ddt_ref)
            dnw_ref[...] = jnp.zeros_like(dnw_ref)

        f = _dn_front(nc - 1 - n, seq, x_ref, None, ba_ref, cw_ref, al_ref, dt_ref, yc_ref)
        ri, ci = f["ri"], f["ci"]
        strict = (ri > ci).astype(F32)
        nw = nw_ref[...]
        lane128 = _iota((1, LANES), 1)
        row128 = _iota((LANES, 1), 0)
        dgam_col = jnp.zeros((CHUNK, LANES), F32)
        dgam_row = jnp.zeros((LANES, CHUNK), F32)
        dbeta = jnp.zeros((CHUNK, LANES), F32)
        dnw = jnp.zeros((1, DN_DV), F32)
        hs = range(DN_HEADS)
        heads = [_dn_head(f, h) for h in hs]
        cols = [slice(h * DN_DV, (h + 1) * DN_DV) for h in hs]
        t_l = [t_ref[0, h] for h in hs]
        s_l = [ss_ref[0, h] for h in hs]
        ds_l = [ds_ref[h] for h in hs]
        kk_l = [_nt(hd["kn"], hd["kn"]) for hd in heads]
        p_l = [_nt(hd["qn"], hd["kn"]) for hd in heads]
        rhsw_l = [hd["kn"] * (hd["bc"] * hd["egam"]) for hd in heads]
        u_l = [_nn(t, hd["v"] * hd["bc"], hi=True) for t, hd in zip(t_l, heads)]
        w_l = [_nn(t, r, hi=True) for t, r in zip(t_l, rhsw_l)]
        vnew_l = [u - _nn(w, s) for u, w, s in zip(u_l, w_l, s_l)]
        gnb = [_gated_norm_bwd(dy_ref[:, c], o_ref[:, c], gate_ref[:, c], nw) for c in cols]
        do_l = [x[0] for x in gnb]
        for h in hs:
            dp_ref[:, DN_CONV_CH + h * DN_DV:DN_CONV_CH + (h + 1) * DN_DV] = gnb[h][1].astype(dp_ref.dtype)
            dnw = dnw + gnb[h][2]
        qg_l = [hd["qn"] * hd["egam"] for hd in heads]
        kd_l = [hd["kn"] * hd["ekd"] for hd in heads]
        dvnew_l = [_tn(p * hd["decay"], do) + _nn(kd, ds)
                   for p, hd, do, kd, ds in zip(p_l, heads, do_l, kd_l, ds_l)]
        m_l = [_nt(do, vn) for do, vn in zip(do_l, vnew_l)]
        dqg_l = [_nt(do, s) for do, s in zip(do_l, s_l)]
        dkd_l = [_nt(vn, ds) for vn, ds in zip(vnew_l, ds_l)]
        for h in hs:
            ds_ref[h] = (ds_l[h] * heads[h]["eglast"] + _tn(qg_l[h], do_l[h]) - _tn(w_l[h], dvnew_l[h]))
        dw_l = [-_nt(dvn, s) for dvn, s in zip(dvnew_l, s_l)]
        dru_l = [_tn(t, dvn, hi=True) for t, dvn in zip(t_l, dvnew_l)]
        drw_l = [_tn(t, dw_, hi=True) for t, dw_ in zip(t_l, dw_l)]
        da_l = [-(_nt(dru, u) + _nt(drw, w)) * strict for dru, u, drw, w in zip(dru_l, u_l, drw_l, w_l)]
        dp_l = [m * hd["decay"] for m, hd in zip(m_l, heads)]
        dkk_l = [da * (hd["bc"] * hd["decay"]) for da, hd in zip(da_l, heads)]
        dqn_l = [dqg * hd["egam"] + _nn(dp, hd["kn"]) for dqg, hd, dp in zip(dqg_l, heads, dp_l)]
        dkn_l = [_tn(dp, hd["qn"]) + dkd * hd["ekd"] + drw * (hd["bc"] * hd["egam"])
                 + _nn(dkk, hd["kn"]) + _tn(dkk, hd["kn"])
                 for dp, hd, dkd, drw, dkk in zip(dp_l, heads, dkd_l, drw_l, dkk_l)]
        dq_parts, dk_parts, dv_parts = [], [], []
        for h in hs:
            hd = heads[h]
            kn, v, bc, egam, decay = hd["kn"], hd["v"], hd["bc"], hd["egam"], hd["decay"]
            t1 = jnp.sum(dkd_l[h] * kd_l[h], axis=1, keepdims=True)
            dglast = (jnp.sum(t1, axis=0, keepdims=True)
                      + jnp.sum(jnp.sum(ds_l[h] * s_l[h], axis=1, keepdims=True), axis=0, keepdims=True)
                      * hd["eglast"])
            e = (m_l[h] * p_l[h] + da_l[h] * (bc * kk_l[h])) * decay
            dgc = (jnp.sum(dqg_l[h] * qg_l[h], axis=1, keepdims=True) - t1
                   + jnp.sum(drw_l[h] * rhsw_l[h], axis=1, keepdims=True)
                   + jnp.sum(e, axis=1, keepdims=True)
                   + jnp.where(_iota((CHUNK, 1), 0) == CHUNK - 1, dglast, 0.0))
            dgr = -jnp.sum(e, axis=0, keepdims=True)
            dbc = (jnp.sum(dru_l[h] * v, axis=1, keepdims=True)
                   + jnp.sum(drw_l[h] * kn, axis=1, keepdims=True) * egam
                   + jnp.sum(da_l[h] * kk_l[h] * decay, axis=1, keepdims=True))
            dv_parts.append(dru_l[h] * bc)
            qh, dqn, dkn = hd["qh"], dqn_l[h], dkn_l[h]
            dq_parts.append(((DN_DK ** -0.5) * hd["rq"])
                            * (dqn - qh * jnp.sum(dqn * qh, axis=1, keepdims=True)))
            dk_parts.append(hd["rk"] * (dkn - kn * jnp.sum(dkn * kn, axis=1, keepdims=True)))
            dgam_col = dgam_col + dgc * (lane128 == DECAY_LANE + h).astype(F32)
            dbeta = dbeta + dbc * (lane128 == BETA_LANE + h).astype(F32)
            dgam_row = dgam_row + (row128 == DECAY_LANE + h).astype(F32) * dgr
        dnw_ref[...] += dnw
        dgam = dgam_col + _nt(f["eye"], dgam_row, hi=True)
        dg = _nn(f["upper"], dgam, hi=True)
        d_a = dg * (-f["eal"]) * jax.nn.sigmoid(f["z"]) * f["valid"]
        dal_ref[...] += jnp.sum(dg * f["g"], axis=0, keepdims=True)
        ddt_ref[...] += jnp.sum(d_a, axis=0, keepdims=True)
        d_b = dbeta * f["valid"] * f["sig"] * (1.0 - f["sig"])
        dp_ref[:, DN_CONV_CH + DN_V:DN_CONV_CH + DN_V + LANES] = (d_a + d_b).astype(dp_ref.dtype)
        dp_ref[:, DN_CONV_CH + DN_V + LANES:] = jnp.zeros((CHUNK, DN_IN_PAD - DN_IN_USED), dp_ref.dtype)
        dact = jnp.concatenate(dq_parts + dk_parts + dv_parts, axis=1)
        yc, sgc = f["yc"], f["sgc"]
        dyc = dact * (sgc * (1.0 + yc * (1.0 - sgc)))
        nxt = nxt_ref[...]
        ups = [dyc] + [_shift_up(dyc, nxt, j) for j in range(1, CONV_K)]
        dx = ups[0] * cw_ref[CONV_K - 1]
        for j in range(1, CONV_K):
            dx = dx + ups[j] * cw_ref[CONV_K - 1 - j]
        for j in range(CONV_K):
            dcw_ref[CONV_K - 1 - j] += jnp.sum(f["xin"] * ups[j], axis=0, keepdims=True)
        nxt_ref[...] = dyc[0:8]
        dp_ref[:, :DN_CONV_CH] = (dx * f["valid"]).astype(dp_ref.dtype)

    rev = lambda n: nc - 1 - n
    row_v = pl.BlockSpec((CHUNK, DN_V), lambda n: (rev(n), 0))
    vec = pl.BlockSpec((1, LANES), lambda n: (0, 0))
    return pl.pallas_call(
        body, grid=(nc,),
        in_specs=_dn_in_specs(rev, conv_saved=True) + [
            row_v, row_v,
            pl.BlockSpec((1, DN_HEADS, DN_DK, DN_DV), lambda n: (rev(n), 0, 0, 0)),
            pl.BlockSpec((1, DN_HEADS, CHUNK, CHUNK), lambda n: (rev(n), 0, 0, 0))],
        out_specs=[pl.BlockSpec((CHUNK, DN_IN_PAD), lambda n: (rev(n), 0)),
                   pl.BlockSpec((CONV_K, 1, DN_CONV_CH), lambda n: (0, 0, 0)), vec, vec,
                   pl.BlockSpec((1, DN_DV), lambda n: (0, 0))],
        out_shape=[jax.ShapeDtypeStruct((rows, DN_IN_PAD), BF16),
                   jax.ShapeDtypeStruct((CONV_K, 1, DN_CONV_CH), F32),
                   jax.ShapeDtypeStruct((1, LANES), F32), jax.ShapeDtypeStruct((1, LANES), F32),
                   jax.ShapeDtypeStruct((1, DN_DV), F32)],
        scratch_shapes=[pltpu.VMEM((DN_HEADS, DN_DK, DN_DV), F32), pltpu.VMEM((8, DN_CONV_CH), F32)],
        name="dn_bwd", compiler_params=_params("arbitrary"))(
            proj, conv_out, proj, proj, conv_w, alog, dtb, norm_w.reshape(1, DN_DV), o, dy, states, tinv)


def _train_step(x, tgt, wts, sh, idx):
    seq = x.shape[0]
    rows = -(-(seq + CHUNK) // ROW_ALIGN) * ROW_ALIGN
    tail = rows - seq - CHUNK
    h0 = jnp.concatenate([jnp.zeros((PAD, D_MODEL), F32), wts["meta_tokens"].astype(F32), x,
                          jnp.zeros((tail, D_MODEL), F32)], axis=0)
    tgt_p = jnp.concatenate([jnp.zeros((CHUNK, D_MODEL), F32), tgt, jnp.zeros((tail, D_MODEL), F32)],
                            axis=0)
    cos, sin = _rope_tables(rows // CHUNK)
    consts = _ret_consts()
    conv_w = wts["dn_conv_w"].reshape(CONV_K, 1, DN_CONV_CH)
    lane_pad = LANES - 2 * DN_HEADS
    alog = jnp.pad(wts["dn_a_log"].reshape(1, DN_HEADS), ((0, 0), (DECAY_LANE, lane_pad)))
    dtb = jnp.pad(wts["dn_dt_bias"].reshape(1, DN_HEADS), ((0, 0), (DECAY_LANE, lane_pad)))
    g = {}

    wts = dict(wts)
    hn0, (got,) = _rms_fwd(h0, wts["mix_norm_w"][0], "rms_mix0", ride=_Ride("gather", [sh["ret_w_in"]]))
    wts["ret_w_in"] = got.reshape(N_SHARD, D_MODEL, -1)
    proj0, got = _mm_cols(hn0, wts["ret_w_in"], "ret_in",
                          ride=_Ride("gather", [sh["ret_w_out"], sh["ffn_w_gate"], sh["dn_w_out"]]))
    wts["ret_w_out"] = got[0].reshape(-1, D_MODEL)
    wts["ffn_w_gate"] = got[1]
    wts["dn_w_out"] = got[2].reshape(-1, D_MODEL)
    (o0, y0, st0), got = _ret_fwd(proj0, cos, sin, consts, wts["ret_gn_w"], seq,
                                  ride=_Ride("gather", [sh["ffn_w_up"], sh["ffn_w_down"], sh["dn_w_in"]]))
    wts["ffn_w_up"], wts["ffn_w_down"] = got[0], got[1]
    n_dn = sh["dn_w_in"].shape[-1]
    dn_shards = got[2].reshape(N_SHARD, D_MODEL, n_dn)
    wts["dn_w_in"] = jnp.concatenate(
        [dn_shards[j] for j in range(N_SHARD)]
        + [jnp.zeros((D_MODEL, DN_IN_PAD - N_SHARD * n_dn), dn_shards.dtype)], axis=-1)
    h1 = _mm(y0, wts["ret_w_out"], mode="nn", name="ret_out", resid=h0)
    hn1 = _rms_fwd(h1, wts["ffn_norm_w"][0], "rms_ffn0")
    g0, u0, act0 = _ffn_up(hn1, wts["ffn_w_gate"], wts["ffn_w_up"], 0, "ffn_up0")
    h2 = _ffn_down(act0, wts["ffn_w_down"], h1, 0, "ffn_down0")
    hn2 = _rms_fwd(h2, wts["mix_norm_w"][1], "rms_mix1")
    proj1 = _mm(hn2, wts["dn_w_in"], mode="nn", name="dn_in")
    o1, y1, st1, tinv, conv1 = _dn_fwd(proj1, conv_w, alog, dtb, wts["dn_norm_w"], seq)
    h3 = _mm(y1, wts["dn_w_out"], mode="nn", name="dn_out", resid=h2)
    hn3 = _rms_fwd(h3, wts["ffn_norm_w"][1], "rms_ffn1")
    g1, u1, act1 = _ffn_up(hn3, wts["ffn_w_gate"], wts["ffn_w_up"], 1, "ffn_up1")
    h4 = _ffn_down(act1, wts["ffn_w_down"], h3, 1, "ffn_down1")

    dh4, g["final_norm_w"], loss, dh4b = _final_loss(h4, wts["final_norm_w"], tgt_p, seq, "final_loss")

    layers = wts["ffn_w_gate"].shape[1]

    ffn_names = ["ffn_w_down", "ffn_w_gate", "ffn_w_up"]

    def ffn_bwd(dh_out, dhb_out, h_mid, hn, gg, uu, act, layer, prev, ride=None, last=False):
        tag = str(layer)
        res = _ffn_down_bwd(dhb_out, wts["ffn_w_down"], gg, uu, layer, "ffn_down_bwd" + tag, ride=ride)
        (dg, du), rode = res if ride is not None else (res, None)
        d_down = _ffn_wgrad(act, [dhb_out], layer, layers, prev and prev[:1], True, "ffn_dwd" + tag)
        d_gu = _ffn_wgrad(hn, [dg, du], layer, layers, prev and prev[1:], False, "ffn_dwgu" + tag)
        grads = list(d_down) + list(d_gu)
        gs = rs_grads(ffn_names, grads) if last else None
        res = _ffn_up_bwd(dg, du, wts["ffn_w_gate"], wts["ffn_w_up"], layer, h_mid, wts["ffn_norm_w"][layer],
                          dh_out, "ffn_up_bwd" + tag, ride=_Ride("pair", gs) if last else None)
        (dh_mid, d_norm, dhb_mid), sib = res if last else (res, None)
        return dh_mid, dhb_mid, grads, d_norm, rode, gs, sib

    red = {}

    def rs_grads(names, grads):
        return [gr.reshape((N_SHARD,) + sh[n].shape) for n, gr in zip(names, grads)]

    def rs_partials(names, gs, sib):
        return [_rs_pair_add(gs[t], sib[t], idx, "rs_pair_add_" + n) for t, n in enumerate(names)]

    def rs_end(names, gs, sib, others, tag):
        mine = [_rs_final_add(gs[t], sib[t], others[t], idx, "rs_final_add_" + n) for t, n in enumerate(names)]
        red.update(zip(names, _rs_share(mine, "rs_share" + tag)))

    dh3, dh3b, ffn_grads, dfn1 = ffn_bwd(dh4, dh4b, h3, hn3, g1, u1, act1, 1, None)[:4]
    dy1 = _mm(dh3b, wts["dn_w_out"], mode="nt", name="dn_out_bwd")
    d_dn_out = _mm(y1, dh3b, mode="tn", name="dn_dwo")
    dproj1, dcw, dal, ddt, g["dn_norm_w"] = _dn_bwd(proj1, conv1, o1, dy1, st1, tinv, conv_w, alog, dtb,
                                                    wts["dn_norm_w"], seq)
    d_dn_in = _mm(hn2, dproj1, mode="tn", name="dn_dwi")
    d_dn_in = jnp.stack([d_dn_in[:, j * n_dn:(j + 1) * n_dn] for j in range(N_SHARD)])
    group1 = ["dn_w_out", "dn_w_in"]
    gs1 = rs_grads(group1, [d_dn_out, d_dn_in])
    (dh2, dmn1, dh2b), sib1 = _mm_nt_rms(dproj1, wts["dn_w_in"], h2, wts["mix_norm_w"][1], dh3, "dn_in_bwd",
                                         ride=_Ride("pair", gs1))
    g["dn_conv_w"] = dcw.reshape(CONV_K, DN_CONV_CH)
    g["dn_a_log"] = dal[0, DECAY_LANE:DECAY_LANE + DN_HEADS]
    g["dn_dt_bias"] = ddt[0, DECAY_LANE:DECAY_LANE + DN_HEADS]

    dh1, dh1b, _, dfn0, others1, gs2, sib2 = ffn_bwd(
        dh2, dh2b, h1, hn1, g0, u0, act0, 0, ffn_grads,
        ride=_Ride("chips", rs_partials(group1, gs1, sib1)), last=True)
    rs_end(group1, gs1, sib1, others1, "1")
    d_ret_out = _mm(y0, dh1b, mode="tn", name="ret_dwo")
    gs2b = rs_grads(["ret_w_out"], [d_ret_out])
    dy0, sib2b = _mm(dh1b, wts["ret_w_out"], mode="nt", name="ret_out_bwd", ride=_Ride("pair", gs2b))
    group2 = ffn_names + ["ret_w_out"]
    gs2, sib2 = gs2 + gs2b, list(sib2) + list(sib2b)
    (dproj0, g["ret_gn_w"]), others2 = _ret_bwd(proj0, o0, dy0, st0, cos, sin, consts, wts["ret_gn_w"], seq,
                                                ride=_Ride("chips", rs_partials(group2, gs2, sib2)))
    rs_end(group2, gs2, sib2, others2, "2")
    d_ret_in = _mm_cols_grad(hn0, dproj0, "ret_dwi")
    gs3 = rs_grads(["ret_w_in"], [d_ret_in])
    sib3 = _rs_pair(gs3, "rs_pair3")
    (dh0, dmn0, _), others3 = _mm_cols_t_rms(dproj0, wts["ret_w_in"], h0, wts["mix_norm_w"][0], dh1, "ret_in_bwd",
                                             ride=_Ride("chips", rs_partials(["ret_w_in"], gs3, sib3)))
    rs_end(["ret_w_in"], gs3, sib3, others3, "3")

    g["ffn_norm_w"] = jnp.concatenate([dfn0, dfn1], axis=0)
    g["mix_norm_w"] = jnp.concatenate([dmn0, dmn1], axis=0)
    g["meta_tokens"] = dh0[PAD:CHUNK]
    g["final_norm_w"] = g["final_norm_w"].reshape(D_MODEL)
    g["ret_gn_w"] = g["ret_gn_w"].reshape(RET_DV)
    g["dn_norm_w"] = g["dn_norm_w"].reshape(DN_DV)
    return loss, dh0, g, red


def _mesh_pos():
    return lax.axis_index("x"), lax.axis_index("y"), lax.axis_index("c")


def _other_chips(x, y):
    return [(1 - x, y), (x, 1 - y), (1 - x, 1 - y)]


def _remote(src, dst, send_sem, recv_sem, to):
    return pltpu.make_async_remote_copy(src_ref=src, dst_ref=dst, send_sem=send_sem, recv_sem=recv_sem,
                                        device_id=to, device_id_type=MESH)


GATHER_COPIES = 7


def _gather_phase(phase, ins, outs, send_sems, recv_sems):
    x, y, c = _mesh_pos()
    me = 2 * x + y
    chips = _other_chips(x, y)
    sibling = (x, y, 1 - c)

    def cp(t, k, src, dst, to):
        i = GATHER_COPIES * t + k
        return _remote(src, dst, send_sems.at[i], recv_sems.at[i], to)

    for t in range(len(ins)):
        own = cp(t, 0, ins[t], outs[t].at[me], sibling)
        if phase == 0:
            own.start()
        if phase == 2:
            own.wait()
        for k, (px, py) in enumerate(chips):
            landed = outs[t].at[2 * px + py, c]
            theirs = outs[t].at[2 * px + py, 1 - c]
            to_chip = cp(t, 1 + k, ins[t].at[c], outs[t].at[me, c], (px, py, c))
            if phase == 0:
                to_chip.start()
            if phase == 1:
                cp(t, 1 + k, ins[t].at[c], landed, (px, py, c)).wait_recv()
                cp(t, 4 + k, landed, landed, sibling).start()
            if phase == 2:
                to_chip.wait_send()
                cp(t, 4 + k, landed, landed, sibling).wait_send()
                cp(t, 4 + k, theirs, theirs, sibling).wait_recv()


def _chips_phase(phase, ins, outs, send_sems, recv_sems):
    x, y, c = _mesh_pos()
    for t in range(len(ins)):
        for k, (px, py) in enumerate(_other_chips(x, y)):
            cp = _remote(ins[t].at[2 * px + py], outs[t].at[k], send_sems.at[3 * t + k], recv_sems.at[3 * t + k],
                         (px, py, c))
            if phase == 0:
                cp.start()
            if phase == 2:
                cp.wait()


class _Ride:
    def __init__(self, kind, arrays):
        self.kind, self.arrays = kind, list(arrays)
        nt = len(self.arrays)
        if kind == "gather":
            self.phase_fn, n_sem = _gather_phase, GATHER_COPIES * nt
            self.out_shape = [jax.ShapeDtypeStruct((N_SHARD,) + a.shape, a.dtype) for a in self.arrays]
        elif kind == "pair":
            self.phase_fn, n_sem = _pair_phase, nt
            self.out_shape = [jax.ShapeDtypeStruct(a.shape[:1] + a.shape[2:], a.dtype) for a in self.arrays]
        else:
            self.phase_fn, n_sem = _chips_phase, 3 * nt
            self.out_shape = [jax.ShapeDtypeStruct((3,) + a.shape[1:], a.dtype) for a in self.arrays]
        self.in_specs, self.out_specs = [ANY] * nt, [ANY] * nt
        self.scratch = [pltpu.SemaphoreType.DMA((n_sem,)), pltpu.SemaphoreType.DMA((n_sem,))]

    def emit(self, step, nsteps, ins, outs, send_sems, recv_sems, before):
        mid = max(0, min((7 * nsteps) // 8, nsteps - 2))
        todo = [(0, 0), (1, mid)] if before else [(2, nsteps - 1)]
        for phase, at in todo:
            if phase == 1 and self.kind != "gather":
                continue

            @pl.when(step == at)
            def _(phase=phase):
                self.phase_fn(phase, ins, outs, send_sems, recv_sems)


def _gather_small(blk):
    r, wd = blk.shape

    def body(b_ref, out_ref, send_sems, recv_sems):
        x, y, c = _mesh_pos()
        chips = _other_chips(x, y)
        out_ref[2 * x + y] = b_ref[...]
        sends = [_remote(b_ref, out_ref.at[2 * x + y], send_sems.at[k], recv_sems.at[k], (px, py, c))
                 for k, (px, py) in enumerate(chips)]
        for cp in sends:
            cp.start()
        for k, (px, py) in enumerate(chips):
            _remote(b_ref, out_ref.at[2 * px + py], send_sems.at[k], recv_sems.at[k], (px, py, c)).wait_recv()
        for cp in sends:
            cp.wait_send()

    return pl.pallas_call(
        body, out_shape=jax.ShapeDtypeStruct((4, r, wd), blk.dtype), in_specs=[VMEM_SPEC], out_specs=VMEM_SPEC,
        scratch_shapes=[pltpu.SemaphoreType.DMA((3,)), pltpu.SemaphoreType.DMA((3,))],
        name="gather_small")(blk)


def _allreduce_small(blk):
    r, wd = blk.shape
    rels = [(dx, dy, dc) for dx in (0, 1) for dy in (0, 1) for dc in (0, 1) if dx or dy or dc]

    def body(b_ref, out_ref, buf_ref, send_sems, recv_sems):
        x, y, c = _mesh_pos()

        def peer(rel):
            dx, dy, dc = rel
            return (1 - x if dx else x, 1 - y if dy else y, 1 - c if dc else c)

        me = 4 * x + 2 * y + c
        buf_ref[me] = b_ref[...]
        sends = [_remote(b_ref, buf_ref.at[me], send_sems.at[k], recv_sems.at[k], peer(rel))
                 for k, rel in enumerate(rels)]
        for cp in sends:
            cp.start()
        for k, rel in enumerate(rels):
            px, py, pc = peer(rel)
            _remote(b_ref, buf_ref.at[4 * px + 2 * py + pc], send_sems.at[k], recv_sems.at[k],
                    (px, py, pc)).wait_recv()
        for cp in sends:
            cp.wait_send()
        acc = buf_ref[0]
        for d in range(1, 8):
            acc = acc + buf_ref[d]
        out_ref[...] = acc

    return pl.pallas_call(
        body, out_shape=jax.ShapeDtypeStruct((r, wd), blk.dtype), in_specs=[VMEM_SPEC], out_specs=VMEM_SPEC,
        scratch_shapes=[pltpu.VMEM((8, r, wd), blk.dtype), pltpu.SemaphoreType.DMA((7,)),
                        pltpu.SemaphoreType.DMA((7,))],
        name="allreduce_small")(blk)


def _rs_pair(gs, name):
    ride = _Ride("pair", gs)

    def body(*refs):
        nt = len(gs)
        for phase in (0, 2):
            _pair_phase(phase, refs[:nt], refs[nt:2 * nt], *refs[2 * nt:])

    return pl.pallas_call(body, out_shape=ride.out_shape, in_specs=ride.in_specs, out_specs=ride.out_specs,
                          scratch_shapes=ride.scratch, name=name)(*gs)


def _pair_phase(phase, ins, outs, send_sems, recv_sems):
    x, y, c = _mesh_pos()
    for t in range(len(ins)):
        cp = _remote(ins[t].at[:, 1 - c], outs[t], send_sems.at[t], recv_sems.at[t], (x, y, 1 - c))
        if phase == 0:
            cp.start()
        if phase == 2:
            cp.wait()


def _rs_tile(a, b):
    return _div_tile(a, 512 if b <= 1024 else 256, 16)


def _rs_pair_add(g, a, idx, name):
    _, _, rows, cols = g.shape
    tr = _rs_tile(rows, cols)

    def body(s_ref, g_ref, a_ref, p_ref):
        p_ref[...] = (g_ref[...] + a_ref[...]).astype(p_ref.dtype)

    blk = pl.BlockSpec((None, tr, cols), lambda j, i, s: (j, i, 0))
    spec = pltpu.PrefetchScalarGridSpec(
        num_scalar_prefetch=1, grid=(N_SHARD, rows // tr),
        in_specs=[pl.BlockSpec((None, None, tr, cols), lambda j, i, s: (j, s[0], i, 0)), blk], out_specs=blk)
    return pl.pallas_call(
        body, grid_spec=spec, out_shape=jax.ShapeDtypeStruct((N_SHARD, rows, cols), BF16), name=name,
        compiler_params=_params("parallel", "parallel"))(idx, g, a)


def _rs_final_add(g, a, b, idx, name):
    _, _, rows, cols = g.shape
    tr = _rs_tile(rows, cols)

    def body(s_ref, g_ref, a_ref, b0_ref, b1_ref, b2_ref, f_ref):
        own = g_ref[...] + a_ref[...]
        f_ref[...] = ((own + b0_ref[...].astype(F32)) + b1_ref[...].astype(F32)) + b2_ref[...].astype(F32)

    def b_spec(k):
        return pl.BlockSpec((None, tr, cols), lambda i, s: (k, i, 0))

    spec = pltpu.PrefetchScalarGridSpec(
        num_scalar_prefetch=1, grid=(rows // tr,),
        in_specs=[pl.BlockSpec((None, None, tr, cols), lambda i, s: (s[1], s[0], i, 0)),
                  pl.BlockSpec((None, tr, cols), lambda i, s: (s[1], i, 0)), b_spec(0), b_spec(1), b_spec(2)],
        out_specs=pl.BlockSpec((None, tr, cols), lambda i, s: (s[0], i, 0)))
    return pl.pallas_call(
        body, grid_spec=spec, out_shape=jax.ShapeDtypeStruct((2, rows, cols), F32), name=name,
        compiler_params=_params("parallel"))(idx, g, a, b, b, b)


def _rs_share(fs, name):
    nt = len(fs)

    def body(*refs):
        outs = refs[nt:2 * nt]
        send_sems, recv_sems = refs[2 * nt:]
        x, y, c = _mesh_pos()
        cps = [_remote(outs[t].at[c], outs[t].at[c], send_sems.at[t], recv_sems.at[t], (x, y, 1 - c))
               for t in range(nt)]
        for cp in cps:
            cp.start()
        for cp in cps:
            cp.wait()

    return pl.pallas_call(
        body, out_shape=[jax.ShapeDtypeStruct(f.shape, f.dtype) for f in fs],
        in_specs=[ANY] * nt, out_specs=[ANY] * nt, input_output_aliases={t: t for t in range(nt)},
        scratch_shapes=[pltpu.SemaphoreType.DMA((nt,)), pltpu.SemaphoreType.DMA((nt,))], name=name)(*fs)


def _adamw(w, g, m, v, name):
    lead, rows, cols = w.shape
    tr = rows // 4 if rows % 32 == 0 else rows

    def body(w_ref, g_ref, m_ref, v_ref, go_ref, d_ref, mo_ref, vo_ref):
        gv = g_ref[...]
        go_ref[...] = gv
        mn = ADAM_B1 * m_ref[...] + (1.0 - ADAM_B1) * gv
        vn = ADAM_B2 * v_ref[...] + (1.0 - ADAM_B2) * (gv * gv)
        m_hat = mn / (1.0 - ADAM_B1 ** ADAM_STEP)
        v_hat = vn / (1.0 - ADAM_B2 ** ADAM_STEP)
        d_ref[...] = -ADAM_LR * (m_hat / (jnp.sqrt(v_hat) + ADAM_EPS) + ADAM_WD * w_ref[...])
        mo_ref[...] = mn
        vo_ref[...] = vn

    blk = pl.BlockSpec((None, tr, cols), lambda l, i: (l, i, 0))
    out = jax.ShapeDtypeStruct((lead, rows, cols), F32)
    return pl.pallas_call(
        body, grid=(lead, rows // tr), in_specs=[blk] * 4, out_specs=[blk] * 4, out_shape=[out] * 4, name=name,
        compiler_params=_params("parallel", "parallel"))(w, g, m, v)


BIG = ["ret_w_in", "ret_w_out", "dn_w_in", "dn_w_out", "ffn_w_gate", "ffn_w_up", "ffn_w_down"]
TRANSPOSED_AT_BOUNDARY = {"dn_w_in": True, "ffn_w_gate": False, "ffn_w_up": False}
SMALL =["meta_tokens", "mix_norm_w", "ffn_norm_w", "ret_gn_w", "dn_conv_w", "dn_a_log", "dn_dt_bias",
         "dn_norm_w", "final_norm_w"]
SMALL_SHARDED = {"meta_tokens", "dn_conv_w", "dn_norm_w"}
ORDER = ["meta_tokens", "mix_norm_w", "ffn_norm_w", "ret_w_in", "ret_gn_w", "ret_w_out", "dn_w_in",
         "dn_conv_w", "dn_a_log", "dn_dt_bias", "dn_norm_w", "dn_w_out", "ffn_w_gate", "ffn_w_up",
         "ffn_w_down", "final_norm_w"]


def _halves(a):
    return a.reshape(2, -1, a.shape[-1])


def _pack_lanes(parts, align=8):
    flat = jnp.concatenate([p.reshape(-1) for p in parts])
    flat = jnp.pad(flat, (0, -flat.shape[0] % (align * LANES)))
    return flat.reshape(-1, LANES)


def _unpack(buf, shapes):
    lead = buf.shape[:-2]
    flat = buf.reshape(lead + (-1,))
    out, off = [], 0
    for shp in shapes:
        size = math.prod(shp)
        out.append(flat[..., off:off + size].reshape(lead + tuple(shp)))
        off += size
    return out


def _join_cols(shards):
    return jnp.concatenate([shards[j] for j in range(N_SHARD)], axis=-1)


def kernel(x, meta_tokens, mix_norm_w, ffn_norm_w, ret_w_in, ret_gn_w, ret_w_out, dn_w_in, dn_conv_w, dn_a_log, dn_dt_bias, dn_norm_w, dn_w_out, ffn_w_gate, ffn_w_up, ffn_w_down, final_norm_w, loss_target, m_meta_tokens, m_mix_norm_w, m_ffn_norm_w, m_ret_w_in, m_ret_gn_w, m_ret_w_out, m_dn_w_in, m_dn_conv_w, m_dn_a_log, m_dn_dt_bias, m_dn_norm_w, m_dn_w_out, m_ffn_w_gate, m_ffn_w_up, m_ffn_w_down, m_final_norm_w, v_meta_tokens, v_mix_norm_w, v_ffn_norm_w, v_ret_w_in, v_ret_gn_w, v_ret_w_out, v_dn_w_in, v_dn_conv_w, v_dn_a_log, v_dn_dt_bias, v_dn_norm_w, v_dn_w_out, v_ffn_w_gate, v_ffn_w_up, v_ffn_w_down, v_final_norm_w):
    w = dict(meta_tokens=meta_tokens, mix_norm_w=mix_norm_w, ffn_norm_w=ffn_norm_w, ret_w_in=ret_w_in,
             ret_gn_w=ret_gn_w, ret_w_out=ret_w_out, dn_w_in=dn_w_in, dn_conv_w=dn_conv_w, dn_a_log=dn_a_log,
             dn_dt_bias=dn_dt_bias, dn_norm_w=dn_norm_w, dn_w_out=dn_w_out, ffn_w_gate=ffn_w_gate,
             ffn_w_up=ffn_w_up, ffn_w_down=ffn_w_down, final_norm_w=final_norm_w)
    m = dict(meta_tokens=m_meta_tokens, mix_norm_w=m_mix_norm_w, ffn_norm_w=m_ffn_norm_w, ret_w_in=m_ret_w_in,
             ret_gn_w=m_ret_gn_w, ret_w_out=m_ret_w_out, dn_w_in=m_dn_w_in, dn_conv_w=m_dn_conv_w,
             dn_a_log=m_dn_a_log, dn_dt_bias=m_dn_dt_bias, dn_norm_w=m_dn_norm_w, dn_w_out=m_dn_w_out,
             ffn_w_gate=m_ffn_w_gate, ffn_w_up=m_ffn_w_up, ffn_w_down=m_ffn_w_down, final_norm_w=m_final_norm_w)
    v = dict(meta_tokens=v_meta_tokens, mix_norm_w=v_mix_norm_w, ffn_norm_w=v_ffn_norm_w, ret_w_in=v_ret_w_in,
             ret_gn_w=v_ret_gn_w, ret_w_out=v_ret_w_out, dn_w_in=v_dn_w_in, dn_conv_w=v_dn_conv_w,
             dn_a_log=v_dn_a_log, dn_dt_bias=v_dn_dt_bias, dn_norm_w=v_dn_norm_w, dn_w_out=v_dn_w_out,
             ffn_w_gate=v_ffn_w_gate, ffn_w_up=v_ffn_w_up, ffn_w_down=v_ffn_w_down, final_norm_w=v_final_norm_w)
    mx, my, mc = _mesh_pos()
    chip = 2 * mx + my

    sm_names = [n for n in SMALL if n in SMALL_SHARDED]
    sm_gathered = _unpack(_gather_small(_pack_lanes([w[n] for n in sm_names])), [w[n].shape for n in sm_names])
    full = {n: _join_cols(sm_gathered[i]) for i, n in enumerate(sm_names)}
    wts = {
        "meta_tokens": full["meta_tokens"], "mix_norm_w": mix_norm_w, "ffn_norm_w": ffn_norm_w,
        "ret_gn_w": ret_gn_w[0], "final_norm_w": final_norm_w, "dn_conv_w": full["dn_conv_w"][0],
        "dn_a_log": dn_a_log[0], "dn_dt_bias": dn_dt_bias[0], "dn_norm_w": full["dn_norm_w"][0],
    }
    idx = jnp.stack([mc, chip]).astype(jnp.int32)
    shards = {n: _halves(w[n].astype(MXU_DTYPE)) for n in BIG}
    loss_part, dh0, g, reduced = _train_step(x[0], loss_target[0], wts, shards, idx)
    seq = x.shape[1]
    grad_x = dh0[CHUNK:CHUNK + seq].reshape(x.shape)
    gsh = {}

    small_full_shapes = [g[n].shape for n in SMALL] + [(1,)]
    red = _unpack(_allreduce_small(_pack_lanes([g[n] for n in SMALL] + [loss_part[0, :1]])), small_full_shapes)
    loss = red[-1][0]
    for i, n in enumerate(SMALL):
        gn = red[i]
        if n in SMALL_SHARDED:
            width = w[n].shape[-1]
            gn = lax.dynamic_slice_in_dim(gn, chip * width, width, axis=gn.ndim - 1)
        gsh[n] = gn.reshape(w[n].shape)

    delta, new_m, new_v = {}, {}, {}
    for n in BIG:
        shp = w[n].shape
        if n in TRANSPOSED_AT_BOUNDARY and TRANSPOSED_AT_BOUNDARY[n]:
            view = lambda a: jnp.swapaxes(a, 1, 2).reshape(1, -1, LANES)
            back = lambda a: jnp.swapaxes(a.reshape(shp[0], shp[2], shp[1]), 1, 2)
        elif n in TRANSPOSED_AT_BOUNDARY:
            view = back = lambda a: jnp.swapaxes(a, 1, 2)
        else:
            view = back = lambda a: a
        res = _adamw(view(w[n]), view(reduced[n].reshape(shp)), view(m[n]), view(v[n]), "adamw_" + n)
        gsh[n], delta[n], new_m[n], new_v[n] = [back(r) for r in res]
    sm_local_shapes = [w[n].shape for n in SMALL]
    _, d_, m_, v_ = _adamw(*[_pack_lanes([t[n] for n in SMALL])[None] for t in (w, gsh, m, v)], "adamw_small")
    d_, m_, v_ = d_[0], m_[0], v_[0]
    for n, dd, mm, vv in zip(SMALL, _unpack(d_, sm_local_shapes), _unpack(m_, sm_local_shapes),
                             _unpack(v_, sm_local_shapes)):
        delta[n], new_m[n], new_v[n] = dd, mm, vv

    return (loss, grad_x, *[gsh[n] for n in ORDER], *[delta[n] for n in ORDER],
            *[new_m[n] for n in ORDER], *[new_v[n] for n in ORDER])
```

```python
import functools
import math

import jax
import jax.numpy as jnp
from jax import lax
from jax.experimental import pallas as pl
from jax.experimental.pallas import tpu as pltpu

F32 = jnp.float32
BF16 = jnp.bfloat16
MXU_DTYPE = BF16

D_MODEL = 1024
N_META = 16
CHUNK = 64
PAD = CHUNK - N_META
RMS_EPS = 1e-6
RET_HEADS, RET_DK, RET_DV = 4, 256, 512
RET_QK, RET_V = RET_HEADS * RET_DK, RET_HEADS * RET_DV
RET_IN = 2 * RET_QK + 2 * RET_V
ROPE_BASE = 10000.0
DN_HEADS, DN_DK, DN_DV = 8, 128, 256
DN_QK, DN_V = DN_HEADS * DN_DK, DN_HEADS * DN_DV
DN_CONV_CH = 2 * DN_QK + DN_V
DN_IN = DN_CONV_CH + DN_V + 2 * DN_HEADS
LANES = 128
DN_IN_USED = DN_CONV_CH + DN_V + LANES
DN_IN_PAD = DN_IN_USED + LANES
CONV_K = 4
FFN_HIDDEN = 2816
ADAM_LR, ADAM_B1, ADAM_B2, ADAM_EPS, ADAM_WD, ADAM_STEP = 0.001, 0.9, 0.999, 1e-08, 0.01, 10

ROW_ALIGN = 256
VMEM_LIMIT = 56 * 1024 * 1024
MESH = pl.DeviceIdType.MESH
ANY = pl.BlockSpec(memory_space=pl.ANY)
VMEM_SPEC = pl.BlockSpec(memory_space=pltpu.VMEM)
_HI = lax.Precision.HIGHEST


def _params(*sem):
    return pltpu.CompilerParams(dimension_semantics=sem, vmem_limit_bytes=VMEM_LIMIT)


def _dg(a, b, ca, cb, hi):
    dims = (((ca,), (cb,)), ((), ()))

    def dot(p, q):
        return lax.dot_general(p, q, dims, preferred_element_type=F32)

    if not hi:
        return dot(a.astype(MXU_DTYPE), b.astype(MXU_DTYPE))
    if MXU_DTYPE == F32:
        return lax.dot_general(a, b, dims, precision=_HI, preferred_element_type=F32)
    a_hi, b_hi = a.astype(MXU_DTYPE), b.astype(MXU_DTYPE)
    a_lo = (a - a_hi.astype(F32)).astype(MXU_DTYPE)
    b_lo = (b - b_hi.astype(F32)).astype(MXU_DTYPE)
    return dot(a_hi, b_hi) + (dot(a_hi, b_lo) + dot(a_lo, b_hi))


def _nn(a, b, hi=False):
    return _dg(a, b, 1, 0, hi)


def _nt(a, b, hi=False):
    return _dg(a, b, 1, 1, hi)


def _tn(a, b, hi=False):
    return _dg(a, b, 0, 0, hi)


def _iota(shape, dim):
    return lax.broadcasted_iota(jnp.int32, shape, dim)


def _valid_rows(first_row, rows, seq):
    r = first_row + _iota((rows, 1), 0)
    return ((r >= PAD) & (r < CHUNK + seq)).astype(F32)


def _rope(t, cs, sn):
    half = t.shape[-1] // 2
    t1, t2 = t[:, :half], t[:, half:]
    return jnp.concatenate([t1 * cs - t2 * sn, t1 * sn + t2 * cs], axis=1)


def _rope_bwd(d, cs, sn):
    half = d.shape[-1] // 2
    d1, d2 = d[:, :half], d[:, half:]
    return jnp.concatenate([d1 * cs + d2 * sn, d2 * cs - d1 * sn], axis=1)


def _col(x, idx):
    oh = (_iota((1, x.shape[1]), 1) == idx).astype(F32)
    return jnp.sum(x * oh, axis=1, keepdims=True)


def _row(x, idx):
    oh = (_iota((x.shape[0], 1), 0) == idx).astype(F32)
    return jnp.sum(x * oh, axis=0, keepdims=True)


def _shift_down(x, halo8, k):
    xr = pltpu.roll(x, k, 0)
    hr = pltpu.roll(halo8, k, 0)
    first = jnp.where(_iota((8, 1), 0) < k, hr, xr[0:8])
    return jnp.concatenate([first, xr[8:]], axis=0)


def _shift_up(x, next8, j):
    rows = x.shape[0]
    xr = pltpu.roll(x, rows - j, 0)
    nr = pltpu.roll(next8, 8 - j, 0)
    last = jnp.where(_iota((8, 1), 0) >= 8 - j, nr, xr[rows - 8:])
    return jnp.concatenate([xr[:rows - 8], last], axis=0)


def _gated_norm(o, gate, w):
    r = lax.rsqrt(jnp.mean(o * o, axis=-1, keepdims=True) + RMS_EPS)
    return o * r * w * (gate * jax.nn.sigmoid(gate))


def _gated_norm_bwd(dy, o, gate, w):
    r = lax.rsqrt(jnp.mean(o * o, axis=-1, keepdims=True) + RMS_EPS)
    nrm = o * r
    sg = jax.nn.sigmoid(gate)
    sl = gate * sg
    dgate = dy * nrm * w * (sg * (1.0 + gate * (1.0 - sg)))
    dn = dy * w * sl
    dw = jnp.sum(dy * nrm * sl, axis=0, keepdims=True)
    do = r * (dn - nrm * jnp.mean(dn * nrm, axis=-1, keepdims=True))
    return do, dgate, dw


def _softplus(z):
    return jnp.maximum(z, 0.0) + jnp.log(1.0 + jnp.exp(-jnp.abs(z)))


def _row_tile(rows, cap=768):
    for t in (768, 512, 256, 128, 64, 32, 16, 8):
        if t <= cap and rows % t == 0:
            return t
    return rows


TILE_BUDGET = 44 * 1024 * 1024


def _fit_rows(rows, row_bytes, fixed_bytes, value_row_bytes):
    best = None
    for t in range(LANES, rows + 1, LANES):
        if rows % t == 0 and 2 * (row_bytes * t + fixed_bytes) + value_row_bytes * t <= TILE_BUDGET:
            best = t
    return best or _row_tile(rows, 256)


def _div_tile(n, cap, mult):
    best = None
    for t in range(mult, min(cap, n) + 1, mult):
        if n % t == 0:
            best = t
    return best or n


def _col_tile(cols, cap=1536):
    best = None
    for t in range(LANES, min(cap, cols) + 1, LANES):
        if cols % t == 0:
            best = t
    return best or cols


def _rms_fwd(h, w, name, ride=None):
    rows, d = h.shape
    tm = _row_tile(rows)

    def body(h_ref, w_ref, o_ref):
        x = h_ref[...]
        r = lax.rsqrt(jnp.mean(x * x, axis=-1, keepdims=True) + RMS_EPS)
        o_ref[...] = (x * r * w_ref[...]).astype(o_ref.dtype)

    res, rode = _pcall(body, [h, w.reshape(1, d)], grid=(rows // tm,),
                       in_specs=[pl.BlockSpec((tm, d), lambda i: (i, 0)), pl.BlockSpec((1, d), lambda i: (0, 0))],
                       out_specs=[pl.BlockSpec((tm, d), lambda i: (i, 0))],
                       out_shape=[jax.ShapeDtypeStruct((rows, d), BF16)], name=name, sem=("parallel",), ride=ride)
    return res[0] if ride is None else (res[0], rode)


def _gmm_rms(name, grid, args, in_specs, row_spec, fn, h, w, resid, row_axis, red_axis=None, ride=None):
    m, d = h.shape
    n_in = len(args)
    vec = pl.BlockSpec((1, d), lambda *g: (0, 0))

    def body(*refs):
        ins = refs[:n_in]
        h_ref, w_ref, r_ref, dh_ref, dw_ref, dh16_ref = refs[n_in:]
        part = fn(*ins)
        row = pl.program_id(row_axis)

        def finish(dy):
            x = h_ref[...]
            r = lax.rsqrt(jnp.mean(x * x, axis=-1, keepdims=True) + RMS_EPS)
            xh = x * r
            dxh = dy * w_ref[...]
            dh = r_ref[...] + r * (dxh - xh * jnp.mean(dxh * xh, axis=-1, keepdims=True))
            dh_ref[...] = dh
            dh16_ref[...] = dh.astype(dh16_ref.dtype)
            dwp = jnp.sum(dy * xh, axis=0, keepdims=True)

            @pl.when(row == 0)
            def _():
                dw_ref[...] = dwp

            @pl.when(row > 0)
            def _():
                dw_ref[...] += dwp

        if red_axis is None:
            finish(part)
            return
        k = pl.program_id(red_axis)

        @pl.when(k == 0)
        def _():
            dh_ref[...] = part

        @pl.when(k > 0)
        def _():
            dh_ref[...] += part

        @pl.when(k == grid[red_axis] - 1)
        def _():
            finish(dh_ref[...])

    res, rode = _pcall(body, list(args) + [h, w.reshape(1, d), resid], grid=grid,
                       in_specs=list(in_specs) + [row_spec, vec, row_spec], out_specs=[row_spec, vec, row_spec],
                       out_shape=[jax.ShapeDtypeStruct((m, d), F32), jax.ShapeDtypeStruct((1, d), F32),
                                  jax.ShapeDtypeStruct((m, d), BF16)],
                       name=name, sem=("arbitrary",) * len(grid), ride=ride)
    return res if ride is None else (res, rode)


def _final_loss(h, w, tgt, seq, name):
    rows, d = h.shape
    tm = _row_tile(rows)

    def body(h_ref, w_ref, t_ref, dh_ref, dw_ref, loss_ref, dh16_ref):
        i = pl.program_id(0)
        r_idx = i * tm + _iota((tm, 1), 0)
        m = ((r_idx >= CHUNK) & (r_idx < CHUNK + seq)).astype(F32)
        x = h_ref[...]
        wv = w_ref[...]
        r = lax.rsqrt(jnp.mean(x * x, axis=-1, keepdims=True) + RMS_EPS)
        xh = x * r
        err = (xh * wv - t_ref[...]) * m
        lpart = 0.5 * jnp.sum(jnp.mean(err * err, axis=-1, keepdims=True), axis=0, keepdims=True)
        dyv = err * (1.0 / d)
        dxh = dyv * wv
        dh = r * (dxh - xh * jnp.mean(dxh * xh, axis=-1, keepdims=True))
        dh_ref[...] = dh
        dh16_ref[...] = dh.astype(dh16_ref.dtype)
        part = jnp.sum(dyv * xh, axis=0, keepdims=True)

        @pl.when(i == 0)
        def _():
            dw_ref[...] = part
            loss_ref[...] = jnp.broadcast_to(lpart, loss_ref.shape)

        @pl.when(i > 0)
        def _():
            dw_ref[...] += part
            loss_ref[...] += jnp.broadcast_to(lpart, loss_ref.shape)

    blk = pl.BlockSpec((tm, d), lambda i: (i, 0))
    vec = pl.BlockSpec((1, d), lambda i: (0, 0))
    return pl.pallas_call(
        body, grid=(rows // tm,), in_specs=[blk, vec, blk],
        out_specs=[blk, vec, pl.BlockSpec((1, LANES), lambda i: (0, 0)), blk],
        out_shape=[jax.ShapeDtypeStruct((rows, d), F32), jax.ShapeDtypeStruct((1, d), F32),
                   jax.ShapeDtypeStruct((1, LANES), F32), jax.ShapeDtypeStruct((rows, d), BF16)],
        name=name, compiler_params=_params("arbitrary"))(h, w.reshape(1, d), tgt)


def _isz(x):
    return jnp.dtype(x.dtype).itemsize


def _mm(a, b, *, mode, name, out_dtype=F32, resid=None, col_cap=1536, ride=None):
    if mode == "tn":
        m, k = a.shape
        n = b.shape[1]
        tn = _col_tile(n, col_cap)
        tm = _fit_rows(m, k * _isz(a) + tn * _isz(b), (3 * k * tn * 4) // 2, 2 * (k + tn))

        def body_tn(a_ref, b_ref, o_ref):
            i = pl.program_id(1)
            part = _tn(a_ref[...], b_ref[...])

            @pl.when(i == 0)
            def _():
                o_ref[...] = part

            @pl.when(i > 0)
            def _():
                o_ref[...] += part

        return pl.pallas_call(
            body_tn, grid=(n // tn, m // tm),
            in_specs=[pl.BlockSpec((tm, k), lambda j, i: (i, 0)),
                      pl.BlockSpec((tm, tn), lambda j, i: (i, j))],
            out_specs=pl.BlockSpec((k, tn), lambda j, i: (0, j)),
            out_shape=jax.ShapeDtypeStruct((k, n), F32), name=name,
            compiler_params=_params("parallel", "arbitrary"))(a, b)

    m, ka = a.shape
    n = b.shape[1] if mode == "nn" else b.shape[0]
    has_resid = resid is not None
    tn = _col_tile(n, col_cap)
    tm = _fit_rows(m, ka * _isz(a) + tn * (jnp.dtype(out_dtype).itemsize + (4 if has_resid else 0)),
                   ka * tn * _isz(b), 2 * ka + 8 * tn)

    def body(*refs):
        if has_resid:
            a_ref, b_ref, r_ref, o_ref = refs
        else:
            a_ref, b_ref, o_ref = refs
        acc = _nn(a_ref[...], b_ref[...]) if mode == "nn" else _nt(a_ref[...], b_ref[...])
        if has_resid:
            acc = acc + r_ref[...]
        o_ref[...] = acc.astype(o_ref.dtype)

    b_spec = (pl.BlockSpec((b.shape[0], tn), lambda j, i: (0, j)) if mode == "nn"
              else pl.BlockSpec((tn, b.shape[1]), lambda j, i: (j, 0)))
    o_spec = pl.BlockSpec((tm, tn), lambda j, i: (i, j))
    in_specs = [pl.BlockSpec((tm, ka), lambda j, i: (i, 0)), b_spec]
    args = [a, b]
    if has_resid:
        in_specs.append(o_spec)
        args.append(resid)
    res, rode = _pcall(body, args, grid=(n // tn, m // tm), in_specs=in_specs, out_specs=[o_spec],
                       out_shape=[jax.ShapeDtypeStruct((m, n), out_dtype)], name=name,
                       sem=("parallel", "parallel"), ride=ride)
    return res[0] if ride is None else (res[0], rode)


N_SHARD = 4


def _gmm(name, grid, args, in_specs, out_specs, out_shape, fn, red_axis=None, init_arg=None, aliases=None,
         ride=None):
    n_in = len(args)
    single = not isinstance(out_shape, (list, tuple))
    out_specs = [out_specs] if single else list(out_specs)
    out_shape = [out_shape] if single else list(out_shape)

    def body(*refs):
        _gmm_step(fn, refs[:n_in], refs[n_in:], red_axis, init_arg)

    sem = tuple("arbitrary" if ax == red_axis else "parallel" for ax in range(len(grid)))
    res, rode = _pcall(body, args, grid=grid, in_specs=in_specs, out_specs=out_specs, out_shape=out_shape,
                       name=name, sem=sem, aliases=aliases, ride=ride)
    ours = res[0] if single else res
    return ours if ride is None else (ours, rode)


def _gmm_step(fn, ins, outs, red_axis, init_arg):
    parts = fn(*ins)
    if red_axis is None:
        for o_ref, p in zip(outs, parts):
            o_ref[...] = p.astype(o_ref.dtype)
        return
    k = pl.program_id(red_axis)

    @pl.when(k == 0)
    def _():
        for idx, (o_ref, p) in enumerate(zip(outs, parts)):
            o_ref[...] = p + ins[init_arg][...] if (idx == 0 and init_arg is not None) else p

    @pl.when(k > 0)
    def _():
        for o_ref, p in zip(outs, parts):
            o_ref[...] += p


def _ride_body(ride, grid, n_in, n_out, n_scratch, body):
    n_rin, n_rout = len(ride.arrays), len(ride.out_shape)
    nsteps = math.prod(grid)

    def wrapped(*refs):
        ins = refs[:n_in]
        r_ins = refs[n_in:n_in + n_rin]
        o0 = n_in + n_rin
        outs = refs[o0:o0 + n_out]
        r_outs = refs[o0 + n_out:o0 + n_out + n_rout]
        s0 = o0 + n_out + n_rout
        scratch = refs[s0:s0 + n_scratch]
        send_sems, recv_sems = refs[-2:]
        step = pl.program_id(0)
        for ax in range(1, len(grid)):
            step = step * grid[ax] + pl.program_id(ax)
        ride.emit(step, nsteps, r_ins, r_outs, send_sems, recv_sems, before=True)
        body(*ins, *outs, *scratch)
        ride.emit(step, nsteps, r_ins, r_outs, send_sems, recv_sems, before=False)

    return wrapped


def _pcall(body, args, *, grid, in_specs, out_specs, out_shape, name, sem, scratch=(), aliases=None, ride=None):
    if ride is None:
        res = pl.pallas_call(body, grid=grid, in_specs=list(in_specs), out_specs=list(out_specs),
                             out_shape=list(out_shape), scratch_shapes=list(scratch), name=name,
                             input_output_aliases=aliases or {}, compiler_params=_params(*sem))(*args)
        return res, None
    n_in, n_out = len(args), len(out_shape)
    res = pl.pallas_call(
        _ride_body(ride, grid, n_in, n_out, len(scratch), body), grid=grid,
        in_specs=list(in_specs) + ride.in_specs, out_specs=list(out_specs) + ride.out_specs,
        out_shape=list(out_shape) + ride.out_shape, scratch_shapes=list(scratch) + ride.scratch, name=name,
        input_output_aliases=aliases or {},
        compiler_params=_params(*(("arbitrary",) * len(grid))))(*args, *ride.arrays)
    return res[:n_out], res[n_out:]


def _mm_cols(a, ws, name, ride=None):
    m, k = a.shape
    n = ws.shape[2]
    tm = _fit_rows(m, k * _isz(a) + n * 4, k * n * _isz(ws), 4 * n)
    return _gmm(name, (N_SHARD, m // tm), [a, ws],
                [pl.BlockSpec((tm, k), lambda j, i: (i, 0)), pl.BlockSpec((None, k, n), lambda j, i: (j, 0, 0))],
                pl.BlockSpec((tm, n), lambda j, i: (i, j)), jax.ShapeDtypeStruct((m, N_SHARD * n), F32),
                lambda a_ref, w_ref: (_nn(a_ref[...], w_ref[...]),), ride=ride)


def _mm_cols_t_rms(d, ws, h, w, resid, name, ride=None):
    m = d.shape[0]
    _, k, n = ws.shape
    tm = _fit_rows(m, n * _isz(d) + 3 * k * 4, k * n * _isz(ws), 16 * k)
    return _gmm_rms(name, (m // tm, N_SHARD), [d, ws],
                    [pl.BlockSpec((tm, n), lambda i, j: (i, j)), pl.BlockSpec((None, k, n), lambda i, j: (j, 0, 0))],
                    pl.BlockSpec((tm, k), lambda i, j: (i, 0)),
                    lambda d_ref, w_ref: _nt(d_ref[...], w_ref[...]), h, w, resid, 0, red_axis=1, ride=ride)


def _mm_nt_rms(a, b, h, w, resid, name, ride=None):
    m, n = a.shape
    k = b.shape[0]
    tm = _fit_rows(m, n * _isz(a) + 3 * k * 4, k * n * _isz(b), 16 * k)
    return _gmm_rms(name, (m // tm,), [a, b],
                    [pl.BlockSpec((tm, n), lambda i: (i, 0)), pl.BlockSpec((k, n), lambda i: (0, 0))],
                    pl.BlockSpec((tm, k), lambda i: (i, 0)),
                    lambda a_ref, b_ref: _nt(a_ref[...], b_ref[...]), h, w, resid, 0, ride=ride)


def _mm_cols_grad(a, d, name):
    m, k = a.shape
    n = d.shape[1] // N_SHARD
    tm = _fit_rows(m, k * _isz(a) + n * _isz(d), (3 * k * n * 4) // 2, 2 * (k + n))
    return _gmm(name, (N_SHARD, m // tm), [a, d],
                [pl.BlockSpec((tm, k), lambda j, i: (i, 0)), pl.BlockSpec((tm, n), lambda j, i: (i, j))],
                pl.BlockSpec((None, k, n), lambda j, i: (j, 0, 0)), jax.ShapeDtypeStruct((N_SHARD, k, n), F32),
                lambda a_ref, d_ref: (_tn(a_ref[...], d_ref[...]),), red_axis=1)


def _ffn_up(hn, wg, wu, layer, name, ride=None):
    m, k = hn.shape
    n = wg.shape[3]
    tm = _fit_rows(m, k * _isz(hn) + 3 * n * jnp.dtype(BF16).itemsize, 2 * k * n * _isz(wg), 16 * n)

    def fn(a_ref, wg_ref, wu_ref):
        a = a_ref[...]
        g = _nn(a, wg_ref[...])
        u = _nn(a, wu_ref[...])
        return g, u, g * jax.nn.sigmoid(g) * u

    w_spec = pl.BlockSpec((None, None, k, n), lambda j, i: (j, layer, 0, 0))
    o_spec = pl.BlockSpec((None, tm, n), lambda j, i: (j, i, 0))
    out = jax.ShapeDtypeStruct((N_SHARD, m, n), BF16)
    return _gmm(name, (N_SHARD, m // tm), [hn, wg, wu],
                [pl.BlockSpec((tm, k), lambda j, i: (i, 0)), w_spec, w_spec],
                [o_spec, o_spec, o_spec], [out, out, out], fn, ride=ride)


def _ffn_down(act, wd, resid, layer, name):
    _, m, n = act.shape
    d = wd.shape[3]
    tm = _fit_rows(m, N_SHARD * n * _isz(act) + 2 * d * 4, N_SHARD * n * d * _isz(wd), 8 * d)

    def fn(a_ref, w_ref, r_ref):
        acc = r_ref[...]
        for j in range(N_SHARD):
            acc = acc + _nn(a_ref[j], w_ref[j])
        return (acc,)

    row = pl.BlockSpec((tm, d), lambda i: (i, 0))
    return _gmm(name, (m // tm,), [act, wd, resid],
                [pl.BlockSpec((N_SHARD, tm, n), lambda i: (0, i, 0)),
                 pl.BlockSpec((N_SHARD, None, n, d), lambda i: (0, layer, 0, 0)), row],
                row, jax.ShapeDtypeStruct((m, d), F32), fn)


def _ffn_down_bwd(dh, wd, g, u, layer, name, ride=None):
    m, d = dh.shape
    n = wd.shape[2]
    tm = _fit_rows(m, d * _isz(dh) + 4 * N_SHARD * n * jnp.dtype(BF16).itemsize, N_SHARD * n * d * _isz(wd),
                   2 * d + 24 * n)

    def body(dh_ref, wd_ref, g_ref, u_ref, dg_ref, du_ref):
        dhv = dh_ref[...].astype(MXU_DTYPE)
        for j in range(N_SHARD):
            dact = _nt(dhv, wd_ref[j])
            gv = g_ref[j].astype(F32)
            sg = jax.nn.sigmoid(gv)
            gs = gv * sg
            dg_ref[j] = (dact * u_ref[j].astype(F32) * (sg + gs * (1.0 - sg))).astype(dg_ref.dtype)
            du_ref[j] = (dact * gs).astype(du_ref.dtype)

    sh_spec = pl.BlockSpec((N_SHARD, tm, n), lambda i: (0, i, 0))
    out = jax.ShapeDtypeStruct((N_SHARD, m, n), BF16)
    res, rode = _pcall(body, [dh, wd, g, u], grid=(m // tm,),
                       in_specs=[pl.BlockSpec((tm, d), lambda i: (i, 0)),
                                 pl.BlockSpec((N_SHARD, None, n, d), lambda i: (0, layer, 0, 0)), sh_spec, sh_spec],
                       out_specs=[sh_spec, sh_spec], out_shape=[out, out], name=name, sem=("parallel",), ride=ride)
    return res if ride is None else (res, rode)


def _ffn_up_bwd(dg, du, wg, wu, layer, h, w, resid, name, ride=None):
    _, m, n = dg.shape
    k = wg.shape[2]
    tm = _fit_rows(m, 2 * N_SHARD * n * _isz(dg) + 3 * k * 4, 2 * N_SHARD * k * n * _isz(wg), 16 * k)

    def fn(dg_ref, du_ref, wg_ref, wu_ref):
        acc = _nt(dg_ref[0], wg_ref[0]) + _nt(du_ref[0], wu_ref[0])
        for j in range(1, N_SHARD):
            acc = acc + _nt(dg_ref[j], wg_ref[j]) + _nt(du_ref[j], wu_ref[j])
        return acc

    d_spec = pl.BlockSpec((N_SHARD, tm, n), lambda i: (0, i, 0))
    w_spec = pl.BlockSpec((N_SHARD, None, k, n), lambda i: (0, layer, 0, 0))
    return _gmm_rms(name, (m // tm,), [dg, du, wg, wu], [d_spec, d_spec, w_spec, w_spec],
                    pl.BlockSpec((tm, k), lambda i: (i, 0)), fn, h, w, resid, 0, ride=ride)


def _ffn_wgrad(lhs, rhs_list, layer, layers, prev, lhs_sharded, name):
    if lhs_sharded:
        _, m, k = lhs.shape
        n = rhs_list[0].shape[1]
    else:
        m, k = lhs.shape
        n = rhs_list[0].shape[2]
    n_out = len(rhs_list)
    tm = _fit_rows(m, k * _isz(lhs) + n_out * n * _isz(rhs_list[0]), (3 * n_out * k * n * 4) // 2,
                   2 * (k + n_out * n))
    sh = pl.BlockSpec((None, tm, k if lhs_sharded else n), lambda j, i: (j, i, 0))
    fl = pl.BlockSpec((tm, n if lhs_sharded else k), lambda j, i: (i, 0))
    n_out = len(rhs_list)
    args = [lhs] + list(rhs_list)
    in_specs = [sh if lhs_sharded else fl] + [fl if lhs_sharded else sh] * n_out
    aliases = None
    if prev is not None:
        aliases = {len(args) + t: t for t in range(n_out)}
        args = args + list(prev)
        in_specs = in_specs + [ANY] * n_out

    def fn(l_ref, *rest):
        lv = l_ref[...]
        return tuple(_tn(lv, r_ref[...]) for r_ref in rest[:n_out])

    o_spec = pl.BlockSpec((None, None, k, n), lambda j, i: (j, layer, 0, 0))
    out = jax.ShapeDtypeStruct((N_SHARD, layers, k, n), F32)
    return _gmm(name, (N_SHARD, m // tm), args, in_specs, [o_spec] * n_out, [out] * n_out, fn,
                red_axis=1, aliases=aliases)


def _ret_consts():
    log_gamma = jnp.log1p(-jnp.exp2(-5.0 - jnp.arange(RET_HEADS, dtype=F32)))
    idx = jnp.arange(CHUNK, dtype=F32)
    rel = idx[:, None] - idx[None, :]
    dmask = jnp.where((rel >= 0)[None], jnp.exp(log_gamma[:, None, None] * jnp.maximum(rel, 0.0)), 0.0)
    xi = jnp.exp(log_gamma[:, None] * (idx[None, :] + 1.0))[:, :, None]
    zeta = jnp.exp(log_gamma[:, None] * (CHUNK - 1.0 - idx[None, :]))[:, :, None]
    gamma_c = jnp.exp(log_gamma * CHUNK)
    wide = (RET_HEADS, CHUNK, RET_DK)
    return dmask, jnp.broadcast_to(xi, wide), jnp.broadcast_to(zeta, wide), gamma_c


def _rope_tables(nc):
    half = RET_DK // 2
    inv_freq = ROPE_BASE ** (-jnp.arange(half, dtype=F32) / half)
    a_chunk = (jnp.arange(nc) * CHUNK - PAD).astype(F32)[:, None] * inv_freq[None, :]
    a_row = jnp.arange(CHUNK).astype(F32)[:, None] * inv_freq[None, :]
    return (jnp.stack([jnp.cos(a_chunk), jnp.sin(a_chunk)], axis=1),
            jnp.stack([jnp.cos(a_row), jnp.sin(a_row)], axis=0))


RET_CPS = 2


def _rope_chunk(rc_ref, rr_ref, c):
    cc, sc = rc_ref[c, 0:1, :], rc_ref[c, 1:2, :]
    cr, sr = rr_ref[0], rr_ref[1]
    return cc * cr - sc * sr, sc * cr + cc * sr


def _rope_specs(order):
    half = RET_DK // 2
    return [pl.BlockSpec((RET_CPS, 2, half), lambda n: (order(n), 0, 0)),
            pl.BlockSpec((2, CHUNK, half), lambda n: (0, 0, 0))]


def _ret_specs(order):
    rows = RET_CPS * CHUNK
    return [pl.BlockSpec((rows, RET_QK), lambda n: (order(n), 0)),
            pl.BlockSpec((rows, RET_QK), lambda n: (order(n), 1)),
            pl.BlockSpec((rows, RET_V), lambda n: (order(n), 1)),
            pl.BlockSpec((rows, RET_V), lambda n: (order(n), 2))]


def _ret_const_specs():
    return [pl.BlockSpec((RET_HEADS, CHUNK, CHUNK), lambda n: (0, 0, 0)),
            pl.BlockSpec((RET_HEADS, CHUNK, RET_DK), lambda n: (0, 0, 0)),
            pl.BlockSpec((RET_HEADS, CHUNK, RET_DK), lambda n: (0, 0, 0)),
            pl.BlockSpec((1, RET_DV), lambda n: (0, 0))]


def _ret_fwd(proj, cos, sin, consts, gn_w, seq, ride=None):
    rows = proj.shape[0]
    nc = rows // CHUNK
    dmask, xi, zeta, gamma_c = consts

    def body(gam_ref, q_ref, k_ref, v_ref, g_ref, cos_ref, sin_ref, dm_ref, xi_ref, ze_ref, gn_ref,
             o_ref, y_ref, ss_ref, s_ref):
        n = pl.program_id(0)

        @pl.when(n == 0)
        def _():
            s_ref[...] = jnp.zeros_like(s_ref)

        gn = gn_ref[...]
        hs = range(RET_HEADS)
        qk_cols = [slice(h * RET_DK, (h + 1) * RET_DK) for h in hs]
        v_cols = [slice(h * RET_DV, (h + 1) * RET_DV) for h in hs]
        for c in range(RET_CPS):
            rs = slice(c * CHUNK, (c + 1) * CHUNK)
            cs, sn = _rope_chunk(cos_ref, sin_ref, c)
            kscale = _valid_rows((n * RET_CPS + c) * CHUNK, CHUNK, seq) * (RET_DK ** -0.5)
            qr_l = [_rope(q_ref[rs, col], cs, sn) for col in qk_cols]
            kr_l = [_rope(k_ref[rs, col], cs, sn) * kscale for col in qk_cols]
            v_l = [v_ref[rs, col] for col in v_cols]
            s_l = [s_ref[h] for h in hs]
            sc_l = [_nt(qr, kr) * dm_ref[h] for h, (qr, kr) in enumerate(zip(qr_l, kr_l))]
            o_l = [_nn(sc_l[h], v_l[h]) + _nn(qr_l[h] * xi_ref[h], s_l[h]) for h in hs]
            for h in hs:
                ss_ref[c, h] = s_l[h].astype(ss_ref.dtype)
                s_ref[h] = gam_ref[h] * s_l[h] + _tn(kr_l[h] * ze_ref[h], v_l[h])
                o_ref[rs, v_cols[h]] = o_l[h]
                y_ref[rs, v_cols[h]] = _gated_norm(o_l[h], g_ref[rs, v_cols[h]], gn).astype(y_ref.dtype)

    fwd = lambda n: n
    row_v = pl.BlockSpec((RET_CPS * CHUNK, RET_V), lambda n: (n, 0))
    res, rode = _pcall(
        body, [gamma_c, proj, proj, proj, proj, cos, sin, dmask, xi, zeta, gn_w.reshape(1, RET_DV)],
        grid=(nc // RET_CPS,),
        in_specs=[pl.BlockSpec(memory_space=pltpu.SMEM)] + _ret_specs(fwd) + _rope_specs(fwd)
        + _ret_const_specs(),
        out_specs=[row_v, row_v,
                   pl.BlockSpec((RET_CPS, RET_HEADS, RET_DK, RET_DV), lambda n: (n, 0, 0, 0))],
        out_shape=[jax.ShapeDtypeStruct((rows, RET_V), F32), jax.ShapeDtypeStruct((rows, RET_V), BF16),
                   jax.ShapeDtypeStruct((nc, RET_HEADS, RET_DK, RET_DV), BF16)],
        scratch=[pltpu.VMEM((RET_HEADS, RET_DK, RET_DV), F32)], name="ret_fwd", sem=("arbitrary",), ride=ride)
    return res if ride is None else (res, rode)


def _ret_bwd(proj, o, dy, states, cos, sin, consts, gn_w, seq, ride=None):
    rows = proj.shape[0]
    nc = rows // CHUNK
    dmask, xi, zeta, gamma_c = consts

    def body(gam_ref, q_ref, k_ref, v_ref, g_ref, o_ref, dy_ref, ss_ref, cos_ref, sin_ref,
             dm_ref, xi_ref, ze_ref, gn_ref, dp_ref, dgn_ref, ds_ref):
        n = pl.program_id(0)

        @pl.when(n == 0)
        def _():
            ds_ref[...] = jnp.zeros_like(ds_ref)
            dgn_ref[...] = jnp.zeros_like(dgn_ref)

        gn = gn_ref[...]
        dgn = jnp.zeros((1, RET_DV), F32)
        hs = range(RET_HEADS)
        qk_cols = [slice(h * RET_DK, (h + 1) * RET_DK) for h in hs]
        v_cols = [slice(h * RET_DV, (h + 1) * RET_DV) for h in hs]
        for c in reversed(range(RET_CPS)):
            rs = slice(c * CHUNK, (c + 1) * CHUNK)
            cs, sn = _rope_chunk(cos_ref, sin_ref, c)
            kscale = _valid_rows(((steps - 1 - n) * RET_CPS + c) * CHUNK, CHUNK, seq) * (RET_DK ** -0.5)
            qr_l = [_rope(q_ref[rs, col], cs, sn) for col in qk_cols]
            kr_l = [_rope(k_ref[rs, col], cs, sn) * kscale for col in qk_cols]
            v_l = [v_ref[rs, col] for col in v_cols]
            s_l = [ss_ref[c, h] for h in hs]
            ds_l = [ds_ref[h] for h in hs]
            sc_l = [_nt(qr_l[h], kr_l[h]) * dm_ref[h] for h in hs]
            gnb = [_gated_norm_bwd(dy_ref[rs, col], o_ref[rs, col], g_ref[rs, col], gn) for col in v_cols]
            do_l = [x[0] for x in gnb]
            dsc_l = [_nt(do_l[h], v_l[h]) * dm_ref[h] for h in hs]
            dv_l = [_tn(sc_l[h], do_l[h]) + _nn(kr_l[h] * ze_ref[h], ds_l[h]) for h in hs]
            dqr_l = [_nn(dsc_l[h], kr_l[h]) + _nt(do_l[h], s_l[h]) * xi_ref[h] for h in hs]
            dkr_l = [_tn(dsc_l[h], qr_l[h]) + _nt(v_l[h], ds_l[h]) * ze_ref[h] for h in hs]
            for h in hs:
                dgn = dgn + gnb[h][2]
                ds_ref[h] = gam_ref[h] * ds_l[h] + _tn(qr_l[h] * xi_ref[h], do_l[h])
                dp_ref[rs, qk_cols[h]] = _rope_bwd(dqr_l[h], cs, sn).astype(dp_ref.dtype)
                dp_ref[rs, RET_QK + h * RET_DK:RET_QK + (h + 1) * RET_DK] = (
                    _rope_bwd(dkr_l[h] * kscale, cs, sn).astype(dp_ref.dtype))
                dp_ref[rs, 2 * RET_QK + h * RET_DV:2 * RET_QK + (h + 1) * RET_DV] = dv_l[h].astype(dp_ref.dtype)
                dp_ref[rs, 2 * RET_QK + RET_V + h * RET_DV:2 * RET_QK + RET_V + (h + 1) * RET_DV] = (
                    gnb[h][1].astype(dp_ref.dtype))
        dgn_ref[...] += dgn

    steps = nc // RET_CPS
    rev = lambda n: steps - 1 - n
    row_v = pl.BlockSpec((RET_CPS * CHUNK, RET_V), lambda n: (rev(n), 0))
    res, rode = _pcall(
        body, [gamma_c, proj, proj, proj, proj, o, dy, states, cos, sin, dmask, xi, zeta,
               gn_w.reshape(1, RET_DV)],
        grid=(steps,),
        in_specs=[pl.BlockSpec(memory_space=pltpu.SMEM)] + _ret_specs(rev) + [
            row_v, row_v, pl.BlockSpec((RET_CPS, RET_HEADS, RET_DK, RET_DV), lambda n: (rev(n), 0, 0, 0))]
        + _rope_specs(rev) + _ret_const_specs(),
        out_specs=[pl.BlockSpec((RET_CPS * CHUNK, RET_IN), lambda n: (rev(n), 0)),
                   pl.BlockSpec((1, RET_DV), lambda n: (0, 0))],
        out_shape=[jax.ShapeDtypeStruct((rows, RET_IN), BF16), jax.ShapeDtypeStruct((1, RET_DV), F32)],
        scratch=[pltpu.VMEM((RET_HEADS, RET_DK, RET_DV), F32)], name="ret_bwd", sem=("arbitrary",), ride=ride)
    return res if ride is None else (res, rode)


GATE_COL = DN_CONV_CH // DN_V
BA_COL = (DN_CONV_CH + DN_V) // LANES
BETA_LANE, DECAY_LANE = 0, DN_HEADS
INV_SHIFT = 4
INV_SQUARINGS = INV_SHIFT - 1
assert CHUNK == 4 << INV_SHIFT


def _dn_in_specs(order, conv_saved=False):
    return [pl.BlockSpec((CHUNK, DN_CONV_CH), lambda n: (order(n), 0)),
            pl.BlockSpec((CHUNK, DN_CONV_CH), lambda n: (order(n), 0)) if conv_saved else
            pl.BlockSpec((8, DN_CONV_CH), lambda n: (jnp.maximum(order(n) * (CHUNK // 8) - 1, 0), 0)),
            pl.BlockSpec((CHUNK, DN_V), lambda n: (order(n), GATE_COL)),
            pl.BlockSpec((CHUNK, LANES), lambda n: (order(n), BA_COL)),
            pl.BlockSpec((CONV_K, 1, DN_CONV_CH), lambda n: (0, 0, 0)),
            pl.BlockSpec((1, LANES), lambda n: (0, 0)),
            pl.BlockSpec((1, LANES), lambda n: (0, 0)),
            pl.BlockSpec((1, DN_DV), lambda n: (0, 0))]


def _dn_front(c, seq, x_ref, halo_ref, ba_ref, cw_ref, al_ref, dt_ref, yc_ref=None):
    valid = _valid_rows(c * CHUNK, CHUNK, seq)
    xin = x_ref[...] * valid
    if yc_ref is None:
        halo = halo_ref[...] * _valid_rows(c * CHUNK - 8, 8, seq)
        yc = xin * cw_ref[CONV_K - 1]
        for k in range(1, CONV_K):
            yc = yc + _shift_down(xin, halo, k) * cw_ref[CONV_K - 1 - k]
    else:
        yc = yc_ref[...]
    sgc = jax.nn.sigmoid(yc)
    ba = ba_ref[...]
    sig = jax.nn.sigmoid(ba)
    beta = sig * valid
    z = ba + dt_ref[...]
    eal = jnp.exp(al_ref[...])
    g = -eal * _softplus(z) * valid
    ri, ci = _iota((CHUNK, CHUNK), 0), _iota((CHUNK, CHUNK), 1)
    lower = (ri >= ci).astype(F32)
    upper = (ri <= ci).astype(F32)
    eye = (ri == ci).astype(F32)
    gam = _nn(lower, g, hi=True)
    gam_t = _tn(g, upper, hi=True)
    return dict(valid=valid, xin=xin, yc=yc, sgc=sgc, act=yc * sgc, sig=sig, beta=beta, z=z,
                eal=eal, g=g, gam=gam, gam_t=gam_t, ri=ri, ci=ci, upper=upper, eye=eye)


def _dn_head(f, h):
    act = f["act"]
    q_raw = act[:, h * DN_DK:(h + 1) * DN_DK]
    k_raw = act[:, DN_QK + h * DN_DK:DN_QK + (h + 1) * DN_DK]
    v = act[:, 2 * DN_QK + h * DN_DV:2 * DN_QK + (h + 1) * DN_DV]
    rq = lax.rsqrt(jnp.sum(q_raw * q_raw, axis=-1, keepdims=True) + RMS_EPS)
    rk = lax.rsqrt(jnp.sum(k_raw * k_raw, axis=-1, keepdims=True) + RMS_EPS)
    qh = q_raw * rq
    kn = k_raw * rk
    gam_c = _col(f["gam"], DECAY_LANE + h)
    gam_r = _row(f["gam_t"], DECAY_LANE + h)
    bc = _col(f["beta"], BETA_LANE + h)
    diff = gam_c - gam_r
    decay = jnp.where(f["ri"] >= f["ci"], jnp.exp(jnp.minimum(diff, 0.0)), 0.0)
    glast = jnp.sum(gam_r * (_iota((1, CHUNK), 1) == CHUNK - 1).astype(F32), axis=1, keepdims=True)
    return dict(rq=rq, rk=rk, qh=qh, qn=qh * (DN_DK ** -0.5), kn=kn, v=v, gam_c=gam_c, gam_r=gam_r,
                bc=bc, diff=diff, decay=decay, egam=jnp.exp(gam_c), glast=glast,
                eglast=jnp.exp(glast), ekd=jnp.exp(glast - gam_c))


def _dn_fwd(proj, conv_w, alog, dtb, norm_w, seq):
    rows = proj.shape[0]
    nc = rows // CHUNK

    def body(x_ref, halo_ref, gate_ref, ba_ref, cw_ref, al_ref, dt_ref, nw_ref,
             o_ref, y_ref, ss_ref, t_ref, yc_ref, s_ref):
        n = pl.program_id(0)

        @pl.when(n == 0)
        def _():
            s_ref[...] = jnp.zeros_like(s_ref)

        f = _dn_front(n, seq, x_ref, halo_ref, ba_ref, cw_ref, al_ref, dt_ref)
        yc_ref[...] = f["yc"]
        ri, ci = f["ri"], f["ci"]
        eye = f["eye"]
        diag_m = (jnp.right_shift(ri, INV_SHIFT) == jnp.right_shift(ci, INV_SHIFT)).astype(F32)
        half_m = (jnp.right_shift(ri, INV_SHIFT + 1) == jnp.right_shift(ci, INV_SHIFT + 1)).astype(F32)
        nw = nw_ref[...]
        heads = [_dn_head(f, h) for h in range(DN_HEADS)]
        a_all = [jnp.where(ri > ci, hd["bc"] * _nt(hd["kn"], hd["kn"]) * hd["decay"], 0.0) for hd in heads]
        b_all = [a * diag_m for a in a_all]
        t_all = [eye - b for b in b_all]
        for _ in range(INV_SQUARINGS):
            b_all = [_nn(b, b, hi=True) for b in b_all]
            t_all = [t + _nn(t, b, hi=True) for t, b in zip(t_all, b_all)]
        for off_m in (half_m - diag_m, 1.0 - half_m):
            x_all = [_nn(a * off_m, t, hi=True) for a, t in zip(a_all, t_all)]
            t_all = [t - _nn(t, x, hi=True) for t, x in zip(t_all, x_all)]
        u_all = [_nn(t, hd["v"] * hd["bc"], hi=True) for t, hd in zip(t_all, heads)]
        w_all = [_nn(t, hd["kn"] * (hd["bc"] * hd["egam"]), hi=True) for t, hd in zip(t_all, heads)]
        s_all = [s_ref[h] for h in range(DN_HEADS)]
        qk_all = [_nt(hd["qn"], hd["kn"]) * hd["decay"] for hd in heads]
        os_all = [_nn(hd["qn"] * hd["egam"], s) for hd, s in zip(heads, s_all)]
        vnew_all = [u - _nn(w, s) for u, w, s in zip(u_all, w_all, s_all)]
        o_all = [os + _nn(qk, vn) for os, qk, vn in zip(os_all, qk_all, vnew_all)]
        snew_all = [s * hd["eglast"] + _tn(hd["kn"] * hd["ekd"], vn)
                    for s, hd, vn in zip(s_all, heads, vnew_all)]
        for h in range(DN_HEADS):
            v_cols = slice(h * DN_DV, (h + 1) * DN_DV)
            t_ref[0, h] = t_all[h]
            ss_ref[0, h] = s_all[h]
            s_ref[h] = snew_all[h]
            o_ref[:, v_cols] = o_all[h]
            y_ref[:, v_cols] = _gated_norm(o_all[h], gate_ref[:, v_cols], nw).astype(y_ref.dtype)

    fwd = lambda n: n
    row_v = pl.BlockSpec((CHUNK, DN_V), lambda n: (n, 0))
    return pl.pallas_call(
        body, grid=(nc,), in_specs=_dn_in_specs(fwd),
        out_specs=[row_v, row_v,
                   pl.BlockSpec((1, DN_HEADS, DN_DK, DN_DV), lambda n: (n, 0, 0, 0)),
                   pl.BlockSpec((1, DN_HEADS, CHUNK, CHUNK), lambda n: (n, 0, 0, 0)),
                   pl.BlockSpec((CHUNK, DN_CONV_CH), lambda n: (n, 0))],
        out_shape=[jax.ShapeDtypeStruct((rows, DN_V), F32), jax.ShapeDtypeStruct((rows, DN_V), BF16),
                   jax.ShapeDtypeStruct((nc, DN_HEADS, DN_DK, DN_DV), F32),
                   jax.ShapeDtypeStruct((nc, DN_HEADS, CHUNK, CHUNK), F32),
                   jax.ShapeDtypeStruct((rows, DN_CONV_CH), F32)],
        scratch_shapes=[pltpu.VMEM((DN_HEADS, DN_DK, DN_DV), F32)],
        name="dn_fwd", compiler_params=_params("arbitrary"))(
            proj, proj, proj, proj, conv_w, alog, dtb, norm_w.reshape(1, DN_DV))


def _dn_bwd(proj, conv_out, o, dy, states, tinv, conv_w, alog, dtb, norm_w, seq):
    rows = proj.shape[0]
    nc = rows // CHUNK

    def body(x_ref, yc_ref, gate_ref, ba_ref, cw_ref, al_ref, dt_ref, nw_ref,
             o_ref, dy_ref, ss_ref, t_ref,
             dp_ref, dcw_ref, dal_ref, ddt_ref, dnw_ref, ds_ref, nxt_ref):
        n = pl.program_id(0)

        @pl.when(n == 0)
        def _():
            ds_ref[...] = jnp.zeros_like(ds_ref)
            nxt_ref[...] = jnp.zeros_like(nxt_ref)
            dcw_ref[...] = jnp.zeros_like(dcw_ref)
            dal_ref[...] = jnp.zeros_like(dal_ref)
            ddt_ref[...] = jnp.zeros_like(ddt_ref)
            dnw_ref[...] = jnp.zeros_like(dnw_ref)

        f = _dn_front(nc - 1 - n, seq, x_ref, None, ba_ref, cw_ref, al_ref, dt_ref, yc_ref)
        ri, ci = f["ri"], f["ci"]
        strict = (ri > ci).astype(F32)
        nw = nw_ref[...]
        lane128 = _iota((1, LANES), 1)
        row128 = _iota((LANES, 1), 0)
        dgam_col = jnp.zeros((CHUNK, LANES), F32)
        dgam_row = jnp.zeros((LANES, CHUNK), F32)
        dbeta = jnp.zeros((CHUNK, LANES), F32)
        dnw = jnp.zeros((1, DN_DV), F32)
        hs = range(DN_HEADS)
        heads = [_dn_head(f, h) for h in hs]
        cols = [slice(h * DN_DV, (h + 1) * DN_DV) for h in hs]
        t_l = [t_ref[0, h] for h in hs]
        s_l = [ss_ref[0, h] for h in hs]
        ds_l = [ds_ref[h] for h in hs]
        kk_l = [_nt(hd["kn"], hd["kn"]) for hd in heads]
        p_l = [_nt(hd["qn"], hd["kn"]) for hd in heads]
        rhsw_l = [hd["kn"] * (hd["bc"] * hd["egam"]) for hd in heads]
        u_l = [_nn(t, hd["v"] * hd["bc"], hi=True) for t, hd in zip(t_l, heads)]
        w_l = [_nn(t, r, hi=True) for t, r in zip(t_l, rhsw_l)]
        vnew_l = [u - _nn(w, s) for u, w, s in zip(u_l, w_l, s_l)]
        gnb = [_gated_norm_bwd(dy_ref[:, c], o_ref[:, c], gate_ref[:, c], nw) for c in cols]
        do_l = [x[0] for x in gnb]
        for h in hs:
            dp_ref[:, DN_CONV_CH + h * DN_DV:DN_CONV_CH + (h + 1) * DN_DV] = gnb[h][1].astype(dp_ref.dtype)
            dnw = dnw + gnb[h][2]
        qg_l = [hd["qn"] * hd["egam"] for hd in heads]
        kd_l = [hd["kn"] * hd["ekd"] for hd in heads]
        dvnew_l = [_tn(p * hd["decay"], do) + _nn(kd, ds)
                   for p, hd, do, kd, ds in zip(p_l, heads, do_l, kd_l, ds_l)]
        m_l = [_nt(do, vn) for do, vn in zip(do_l, vnew_l)]
        dqg_l = [_nt(do, s) for do, s in zip(do_l, s_l)]
        dkd_l = [_nt(vn, ds) for vn, ds in zip(vnew_l, ds_l)]
        for h in hs:
            ds_ref[h] = (ds_l[h] * heads[h]["eglast"] + _tn(qg_l[h], do_l[h]) - _tn(w_l[h], dvnew_l[h]))
        dw_l = [-_nt(dvn, s) for dvn, s in zip(dvnew_l, s_l)]
        dru_l = [_tn(t, dvn, hi=True) for t, dvn in zip(t_l, dvnew_l)]
        drw_l = [_tn(t, dw_, hi=True) for t, dw_ in zip(t_l, dw_l)]
        da_l = [-(_nt(dru, u) + _nt(drw, w)) * strict for dru, u, drw, w in zip(dru_l, u_l, drw_l, w_l)]
        dp_l = [m * hd["decay"] for m, hd in zip(m_l, heads)]
        dkk_l = [da * (hd["bc"] * hd["decay"]) for da, hd in zip(da_l, heads)]
        dqn_l = [dqg * hd["egam"] + _nn(dp, hd["kn"]) for dqg, hd, dp in zip(dqg_l, heads, dp_l)]
        dkn_l = [_tn(dp, hd["qn"]) + dkd * hd["ekd"] + drw * (hd["bc"] * hd["egam"])
                 + _nn(dkk, hd["kn"]) + _tn(dkk, hd["kn"])
                 for dp, hd, dkd, drw, dkk in zip(dp_l, heads, dkd_l, drw_l, dkk_l)]
        dq_parts, dk_parts, dv_parts = [], [], []
        for h in hs:
            hd = heads[h]
            kn, v, bc, egam, decay = hd["kn"], hd["v"], hd["bc"], hd["egam"], hd["decay"]
            t1 = jnp.sum(dkd_l[h] * kd_l[h], axis=1, keepdims=True)
            dglast = (jnp.sum(t1, axis=0, keepdims=True)
                      + jnp.sum(jnp.sum(ds_l[h] * s_l[h], axis=1, keepdims=True), axis=0, keepdims=True)
                      * hd["eglast"])
            e = (m_l[h] * p_l[h] + da_l[h] * (bc * kk_l[h])) * decay
            dgc = (jnp.sum(dqg_l[h] * qg_l[h], axis=1, keepdims=True) - t1
                   + jnp.sum(drw_l[h] * rhsw_l[h], axis=1, keepdims=True)
                   + jnp.sum(e, axis=1, keepdims=True)
                   + jnp.where(_iota((CHUNK, 1), 0) == CHUNK - 1, dglast, 0.0))
            dgr = -jnp.sum(e, axis=0, keepdims=True)
            dbc = (jnp.sum(dru_l[h] * v, axis=1, keepdims=True)
                   + jnp.sum(drw_l[h] * kn, axis=1, keepdims=True) * egam
                   + jnp.sum(da_l[h] * kk_l[h] * decay, axis=1, keepdims=True))
            dv_parts.append(dru_l[h] * bc)
            qh, dqn, dkn = hd["qh"], dqn_l[h], dkn_l[h]
            dq_parts.append(((DN_DK ** -0.5) * hd["rq"])
                            * (dqn - qh * jnp.sum(dqn * qh, axis=1, keepdims=True)))
            dk_parts.append(hd["rk"] * (dkn - kn * jnp.sum(dkn * kn, axis=1, keepdims=True)))
            dgam_col = dgam_col + dgc * (lane128 == DECAY_LANE + h).astype(F32)
            dbeta = dbeta + dbc * (lane128 == BETA_LANE + h).astype(F32)
            dgam_row = dgam_row + (row128 == DECAY_LANE + h).astype(F32) * dgr
        dnw_ref[...] += dnw
        dgam = dgam_col + _nt(f["eye"], dgam_row, hi=True)
        dg = _nn(f["upper"], dgam, hi=True)
        d_a = dg * (-f["eal"]) * jax.nn.sigmoid(f["z"]) * f["valid"]
        dal_ref[...] += jnp.sum(dg * f["g"], axis=0, keepdims=True)
        ddt_ref[...] += jnp.sum(d_a, axis=0, keepdims=True)
        d_b = dbeta * f["valid"] * f["sig"] * (1.0 - f["sig"])
        dp_ref[:, DN_CONV_CH + DN_V:DN_CONV_CH + DN_V + LANES] = (d_a + d_b).astype(dp_ref.dtype)
        dp_ref[:, DN_CONV_CH + DN_V + LANES:] = jnp.zeros((CHUNK, DN_IN_PAD - DN_IN_USED), dp_ref.dtype)
        dact = jnp.concatenate(dq_parts + dk_parts + dv_parts, axis=1)
        yc, sgc = f["yc"], f["sgc"]
        dyc = dact * (sgc * (1.0 + yc * (1.0 - sgc)))
        nxt = nxt_ref[...]
        ups = [dyc] + [_shift_up(dyc, nxt, j) for j in range(1, CONV_K)]
        dx = ups[0] * cw_ref[CONV_K - 1]
        for j in range(1, CONV_K):
            dx = dx + ups[j] * cw_ref[CONV_K - 1 - j]
        for j in range(CONV_K):
            dcw_ref[CONV_K - 1 - j] += jnp.sum(f["xin"] * ups[j], axis=0, keepdims=True)
        nxt_ref[...] = dyc[0:8]
        dp_ref[:, :DN_CONV_CH] = (dx * f["valid"]).astype(dp_ref.dtype)

    rev = lambda n: nc - 1 - n
    row_v = pl.BlockSpec((CHUNK, DN_V), lambda n: (rev(n), 0))
    vec = pl.BlockSpec((1, LANES), lambda n: (0, 0))
    return pl.pallas_call(
        body, grid=(nc,),
        in_specs=_dn_in_specs(rev, conv_saved=True) + [
            row_v, row_v,
            pl.BlockSpec((1, DN_HEADS, DN_DK, DN_DV), lambda n: (rev(n), 0, 0, 0)),
            pl.BlockSpec((1, DN_HEADS, CHUNK, CHUNK), lambda n: (rev(n), 0, 0, 0))],
        out_specs=[pl.BlockSpec((CHUNK, DN_IN_PAD), lambda n: (rev(n), 0)),
                   pl.BlockSpec((CONV_K, 1, DN_CONV_CH), lambda n: (0, 0, 0)), vec, vec,
                   pl.BlockSpec((1, DN_DV), lambda n: (0, 0))],
        out_shape=[jax.ShapeDtypeStruct((rows, DN_IN_PAD), BF16),
                   jax.ShapeDtypeStruct((CONV_K, 1, DN_CONV_CH), F32),
                   jax.ShapeDtypeStruct((1, LANES), F32), jax.ShapeDtypeStruct((1, LANES), F32),
                   jax.ShapeDtypeStruct((1, DN_DV), F32)],
        scratch_shapes=[pltpu.VMEM((DN_HEADS, DN_DK, DN_DV), F32), pltpu.VMEM((8, DN_CONV_CH), F32)],
        name="dn_bwd", compiler_params=_params("arbitrary"))(
            proj, conv_out, proj, proj, conv_w, alog, dtb, norm_w.reshape(1, DN_DV), o, dy, states, tinv)


def _train_step(x, tgt, wts, sh, idx):
    seq = x.shape[0]
    rows = -(-(seq + CHUNK) // ROW_ALIGN) * ROW_ALIGN
    tail = rows - seq - CHUNK
    h0 = jnp.concatenate([jnp.zeros((PAD, D_MODEL), F32), wts["meta_tokens"].astype(F32), x,
                          jnp.zeros((tail, D_MODEL), F32)], axis=0)
    tgt_p = jnp.concatenate([jnp.zeros((CHUNK, D_MODEL), F32), tgt, jnp.zeros((tail, D_MODEL), F32)],
                            axis=0)
    cos, sin = _rope_tables(rows // CHUNK)
    consts = _ret_consts()
    conv_w = wts["dn_conv_w"].reshape(CONV_K, 1, DN_CONV_CH)
    lane_pad = LANES - 2 * DN_HEADS
    alog = jnp.pad(wts["dn_a_log"].reshape(1, DN_HEADS), ((0, 0), (DECAY_LANE, lane_pad)))
    dtb = jnp.pad(wts["dn_dt_bias"].reshape(1, DN_HEADS), ((0, 0), (DECAY_LANE, lane_pad)))
    g = {}

    wts = dict(wts)
    hn0, (got,) = _rms_fwd(h0, wts["mix_norm_w"][0], "rms_mix0", ride=_Ride("gather", [sh["ret_w_in"]]))
    wts["ret_w_in"] = got.reshape(N_SHARD, D_MODEL, -1)
    proj0, got = _mm_cols(hn0, wts["ret_w_in"], "ret_in",
                          ride=_Ride("gather", [sh["ret_w_out"], sh["ffn_w_gate"]]))
    wts["ret_w_out"] = got[0].reshape(-1, D_MODEL)
    wts["ffn_w_gate"] = got[1]
    (o0, y0, st0), got = _ret_fwd(proj0, cos, sin, consts, wts["ret_gn_w"], seq,
                                  ride=_Ride("gather", [sh["ffn_w_up"], sh["ffn_w_down"]]))
    wts["ffn_w_up"], wts["ffn_w_down"] = got
    h1 = _mm(y0, wts["ret_w_out"], mode="nn", name="ret_out", resid=h0)
    hn1 = _rms_fwd(h1, wts["ffn_norm_w"][0], "rms_ffn0")
    (g0, u0, act0), got = _ffn_up(hn1, wts["ffn_w_gate"], wts["ffn_w_up"], 0, "ffn_up0",
                                  ride=_Ride("gather", [sh["dn_w_in"], sh["dn_w_out"]]))
    n_dn = sh["dn_w_in"].shape[-1]
    dn_shards = got[0].reshape(N_SHARD, D_MODEL, n_dn)
    wts["dn_w_in"] = jnp.concatenate(
        [dn_shards[j] for j in range(N_SHARD)]
        + [jnp.zeros((D_MODEL, DN_IN_PAD - N_SHARD * n_dn), dn_shards.dtype)], axis=-1)
    wts["dn_w_out"] = got[1].reshape(-1, D_MODEL)
    h2 = _ffn_down(act0, wts["ffn_w_down"], h1, 0, "ffn_down0")
    hn2 = _rms_fwd(h2, wts["mix_norm_w"][1], "rms_mix1")
    proj1 = _mm(hn2, wts["dn_w_in"], mode="nn", name="dn_in")
    o1, y1, st1, tinv, conv1 = _dn_fwd(proj1, conv_w, alog, dtb, wts["dn_norm_w"], seq)
    h3 = _mm(y1, wts["dn_w_out"], mode="nn", name="dn_out", resid=h2)
    hn3 = _rms_fwd(h3, wts["ffn_norm_w"][1], "rms_ffn1")
    g1, u1, act1 = _ffn_up(hn3, wts["ffn_w_gate"], wts["ffn_w_up"], 1, "ffn_up1")
    h4 = _ffn_down(act1, wts["ffn_w_down"], h3, 1, "ffn_down1")

    dh4, g["final_norm_w"], loss, dh4b = _final_loss(h4, wts["final_norm_w"], tgt_p, seq, "final_loss")

    layers = wts["ffn_w_gate"].shape[1]

    ffn_names = ["ffn_w_down", "ffn_w_gate", "ffn_w_up"]

    def ffn_bwd(dh_out, dhb_out, h_mid, hn, gg, uu, act, layer, prev, ride=None, last=False):
        tag = str(layer)
        res = _ffn_down_bwd(dhb_out, wts["ffn_w_down"], gg, uu, layer, "ffn_down_bwd" + tag, ride=ride)
        (dg, du), rode = res if ride is not None else (res, None)
        d_down = _ffn_wgrad(act, [dhb_out], layer, layers, prev and prev[:1], True, "ffn_dwd" + tag)
        d_gu = _ffn_wgrad(hn, [dg, du], layer, layers, prev and prev[1:], False, "ffn_dwgu" + tag)
        grads = list(d_down) + list(d_gu)
        gs = rs_grads(ffn_names, grads) if last else None
        res = _ffn_up_bwd(dg, du, wts["ffn_w_gate"], wts["ffn_w_up"], layer, h_mid, wts["ffn_norm_w"][layer],
                          dh_out, "ffn_up_bwd" + tag, ride=_Ride("pair", gs) if last else None)
        (dh_mid, d_norm, dhb_mid), sib = res if last else (res, None)
        return dh_mid, dhb_mid, grads, d_norm, rode, gs, sib

    red = {}

    def rs_grads(names, grads):
        return [gr.reshape((N_SHARD,) + sh[n].shape) for n, gr in zip(names, grads)]

    def rs_partials(names, gs, sib):
        return [_rs_pair_add(gs[t], sib[t], idx, "rs_pair_add_" + n) for t, n in enumerate(names)]

    def rs_end(names, gs, sib, others, tag):
        mine = [_rs_final_add(gs[t], sib[t], others[t], idx, "rs_final_add_" + n) for t, n in enumerate(names)]
        red.update(zip(names, _rs_share(mine, "rs_share" + tag)))

    dh3, dh3b, ffn_grads, dfn1 = ffn_bwd(dh4, dh4b, h3, hn3, g1, u1, act1, 1, None)[:4]
    dy1 = _mm(dh3b, wts["dn_w_out"], mode="nt", name="dn_out_bwd")
    d_dn_out = _mm(y1, dh3b, mode="tn", name="dn_dwo")
    dproj1, dcw, dal, ddt, g["dn_norm_w"] = _dn_bwd(proj1, conv1, o1, dy1, st1, tinv, conv_w, alog, dtb,
                                                    wts["dn_norm_w"], seq)
    d_dn_in = _mm(hn2, dproj1, mode="tn", name="dn_dwi")
    d_dn_in = jnp.stack([d_dn_in[:, j * n_dn:(j + 1) * n_dn] for j in range(N_SHARD)])
    group1 = ["dn_w_out", "dn_w_in"]
    gs1 = rs_grads(group1, [d_dn_out, d_dn_in])
    (dh2, dmn1, dh2b), sib1 = _mm_nt_rms(dproj1, wts["dn_w_in"], h2, wts["mix_norm_w"][1], dh3, "dn_in_bwd",
                                         ride=_Ride("pair", gs1))
    g["dn_conv_w"] = dcw.reshape(CONV_K, DN_CONV_CH)
    g["dn_a_log"] = dal[0, DECAY_LANE:DECAY_LANE + DN_HEADS]
    g["dn_dt_bias"] = ddt[0, DECAY_LANE:DECAY_LANE + DN_HEADS]

    dh1, dh1b, _, dfn0, others1, gs2, sib2 = ffn_bwd(
        dh2, dh2b, h1, hn1, g0, u0, act0, 0, ffn_grads,
        ride=_Ride("chips", rs_partials(group1, gs1, sib1)), last=True)
    rs_end(group1, gs1, sib1, others1, "1")
    d_ret_out = _mm(y0, dh1b, mode="tn", name="ret_dwo")
    gs2b = rs_grads(["ret_w_out"], [d_ret_out])
    dy0, sib2b = _mm(dh1b, wts["ret_w_out"], mode="nt", name="ret_out_bwd", ride=_Ride("pair", gs2b))
    group2 = ffn_names + ["ret_w_out"]
    gs2, sib2 = gs2 + gs2b, list(sib2) + list(sib2b)
    (dproj0, g["ret_gn_w"]), others2 = _ret_bwd(proj0, o0, dy0, st0, cos, sin, consts, wts["ret_gn_w"], seq,
                                                ride=_Ride("chips", rs_partials(group2, gs2, sib2)))
    rs_end(group2, gs2, sib2, others2, "2")
    d_ret_in = _mm_cols_grad(hn0, dproj0, "ret_dwi")
    gs3 = rs_grads(["ret_w_in"], [d_ret_in])
    sib3 = _rs_pair(gs3, "rs_pair3")
    (dh0, dmn0, _), others3 = _mm_cols_t_rms(dproj0, wts["ret_w_in"], h0, wts["mix_norm_w"][0], dh1, "ret_in_bwd",
                                             ride=_Ride("chips", rs_partials(["ret_w_in"], gs3, sib3)))
    rs_end(["ret_w_in"], gs3, sib3, others3, "3")

    g["ffn_norm_w"] = jnp.concatenate([dfn0, dfn1], axis=0)
    g["mix_norm_w"] = jnp.concatenate([dmn0, dmn1], axis=0)
    g["meta_tokens"] = dh0[PAD:CHUNK]
    g["final_norm_w"] = g["final_norm_w"].reshape(D_MODEL)
    g["ret_gn_w"] = g["ret_gn_w"].reshape(RET_DV)
    g["dn_norm_w"] = g["dn_norm_w"].reshape(DN_DV)
    return loss, dh0, g, red


def _mesh_pos():
    return lax.axis_index("x"), lax.axis_index("y"), lax.axis_index("c")


def _other_chips(x, y):
    return [(1 - x, y), (x, 1 - y), (1 - x, 1 - y)]


def _remote(src, dst, send_sem, recv_sem, to):
    return pltpu.make_async_remote_copy(src_ref=src, dst_ref=dst, send_sem=send_sem, recv_sem=recv_sem,
                                        device_id=to, device_id_type=MESH)


GATHER_COPIES = 7


def _gather_phase(phase, ins, outs, send_sems, recv_sems):
    x, y, c = _mesh_pos()
    me = 2 * x + y
    chips = _other_chips(x, y)
    sibling = (x, y, 1 - c)

    def cp(t, k, src, dst, to):
        i = GATHER_COPIES * t + k
        return _remote(src, dst, send_sems.at[i], recv_sems.at[i], to)

    for t in range(len(ins)):
        own = cp(t, 0, ins[t], outs[t].at[me], sibling)
        if phase == 0:
            own.start()
        if phase == 2:
            own.wait()
        for k, (px, py) in enumerate(chips):
            landed = outs[t].at[2 * px + py, c]
            theirs = outs[t].at[2 * px + py, 1 - c]
            to_chip = cp(t, 1 + k, ins[t].at[c], outs[t].at[me, c], (px, py, c))
            if phase == 0:
                to_chip.start()
            if phase == 1:
                cp(t, 1 + k, ins[t].at[c], landed, (px, py, c)).wait_recv()
                cp(t, 4 + k, landed, landed, sibling).start()
            if phase == 2:
                to_chip.wait_send()
                cp(t, 4 + k, landed, landed, sibling).wait_send()
                cp(t, 4 + k, theirs, theirs, sibling).wait_recv()


def _chips_phase(phase, ins, outs, send_sems, recv_sems):
    x, y, c = _mesh_pos()
    for t in range(len(ins)):
        for k, (px, py) in enumerate(_other_chips(x, y)):
            cp = _remote(ins[t].at[2 * px + py], outs[t].at[k], send_sems.at[3 * t + k], recv_sems.at[3 * t + k],
                         (px, py, c))
            if phase == 0:
                cp.start()
            if phase == 2:
                cp.wait()


class _Ride:
    def __init__(self, kind, arrays):
        self.kind, self.arrays = kind, list(arrays)
        nt = len(self.arrays)
        if kind == "gather":
            self.phase_fn, n_sem = _gather_phase, GATHER_COPIES * nt
            self.out_shape = [jax.ShapeDtypeStruct((N_SHARD,) + a.shape, a.dtype) for a in self.arrays]
        elif kind == "pair":
            self.phase_fn, n_sem = _pair_phase, nt
            self.out_shape = [jax.ShapeDtypeStruct(a.shape[:1] + a.shape[2:], a.dtype) for a in self.arrays]
        else:
            self.phase_fn, n_sem = _chips_phase, 3 * nt
            self.out_shape = [jax.ShapeDtypeStruct((3,) + a.shape[1:], a.dtype) for a in self.arrays]
        self.in_specs, self.out_specs = [ANY] * nt, [ANY] * nt
        self.scratch = [pltpu.SemaphoreType.DMA((n_sem,)), pltpu.SemaphoreType.DMA((n_sem,))]

    def emit(self, step, nsteps, ins, outs, send_sems, recv_sems, before):
        mid = max(0, min((7 * nsteps) // 8, nsteps - 2))
        todo = [(0, 0), (1, mid)] if before else [(2, nsteps - 1)]
        for phase, at in todo:
            if phase == 1 and self.kind != "gather":
                continue

            @pl.when(step == at)
            def _(phase=phase):
                self.phase_fn(phase, ins, outs, send_sems, recv_sems)


def _gather_small(blk):
    r, wd = blk.shape

    def body(b_ref, out_ref, send_sems, recv_sems):
        x, y, c = _mesh_pos()
        chips = _other_chips(x, y)
        out_ref[2 * x + y] = b_ref[...]
        sends = [_remote(b_ref, out_ref.at[2 * x + y], send_sems.at[k], recv_sems.at[k], (px, py, c))
                 for k, (px, py) in enumerate(chips)]
        for cp in sends:
            cp.start()
        for k, (px, py) in enumerate(chips):
            _remote(b_ref, out_ref.at[2 * px + py], send_sems.at[k], recv_sems.at[k], (px, py, c)).wait_recv()
        for cp in sends:
            cp.wait_send()

    return pl.pallas_call(
        body, out_shape=jax.ShapeDtypeStruct((4, r, wd), blk.dtype), in_specs=[VMEM_SPEC], out_specs=VMEM_SPEC,
        scratch_shapes=[pltpu.SemaphoreType.DMA((3,)), pltpu.SemaphoreType.DMA((3,))],
        name="gather_small")(blk)


def _allreduce_small(blk):
    r, wd = blk.shape
    rels = [(dx, dy, dc) for dx in (0, 1) for dy in (0, 1) for dc in (0, 1) if dx or dy or dc]

    def body(b_ref, out_ref, buf_ref, send_sems, recv_sems):
        x, y, c = _mesh_pos()

        def peer(rel):
            dx, dy, dc = rel
            return (1 - x if dx else x, 1 - y if dy else y, 1 - c if dc else c)

        me = 4 * x + 2 * y + c
        buf_ref[me] = b_ref[...]
        sends = [_remote(b_ref, buf_ref.at[me], send_sems.at[k], recv_sems.at[k], peer(rel))
                 for k, rel in enumerate(rels)]
        for cp in sends:
            cp.start()
        for k, rel in enumerate(rels):
            px, py, pc = peer(rel)
            _remote(b_ref, buf_ref.at[4 * px + 2 * py + pc], send_sems.at[k], recv_sems.at[k],
                    (px, py, pc)).wait_recv()
        for cp in sends:
            cp.wait_send()
        acc = buf_ref[0]
        for d in range(1, 8):
            acc = acc + buf_ref[d]
        out_ref[...] = acc

    return pl.pallas_call(
        body, out_shape=jax.ShapeDtypeStruct((r, wd), blk.dtype), in_specs=[VMEM_SPEC], out_specs=VMEM_SPEC,
        scratch_shapes=[pltpu.VMEM((8, r, wd), blk.dtype), pltpu.SemaphoreType.DMA((7,)),
                        pltpu.SemaphoreType.DMA((7,))],
        name="allreduce_small")(blk)


def _rs_pair(gs, name):
    ride = _Ride("pair", gs)

    def body(*refs):
        nt = len(gs)
        for phase in (0, 2):
            _pair_phase(phase, refs[:nt], refs[nt:2 * nt], *refs[2 * nt:])

    return pl.pallas_call(body, out_shape=ride.out_shape, in_specs=ride.in_specs, out_specs=ride.out_specs,
                          scratch_shapes=ride.scratch, name=name)(*gs)


def _pair_phase(phase, ins, outs, send_sems, recv_sems):
    x, y, c = _mesh_pos()
    for t in range(len(ins)):
        cp = _remote(ins[t].at[:, 1 - c], outs[t], send_sems.at[t], recv_sems.at[t], (x, y, 1 - c))
        if phase == 0:
            cp.start()
        if phase == 2:
            cp.wait()


def _rs_tile(a, b):
    return _div_tile(a, 512 if b <= 1024 else 256, 16)


def _rs_pair_add(g, a, idx, name):
    _, _, rows, cols = g.shape
    tr = _rs_tile(rows, cols)

    def body(s_ref, g_ref, a_ref, p_ref):
        p_ref[...] = (g_ref[...] + a_ref[...]).astype(p_ref.dtype)

    blk = pl.BlockSpec((None, tr, cols), lambda j, i, s: (j, i, 0))
    spec = pltpu.PrefetchScalarGridSpec(
        num_scalar_prefetch=1, grid=(N_SHARD, rows // tr),
        in_specs=[pl.BlockSpec((None, None, tr, cols), lambda j, i, s: (j, s[0], i, 0)), blk], out_specs=blk)
    return pl.pallas_call(
        body, grid_spec=spec, out_shape=jax.ShapeDtypeStruct((N_SHARD, rows, cols), BF16), name=name,
        compiler_params=_params("parallel", "parallel"))(idx, g, a)


def _rs_final_add(g, a, b, idx, name):
    _, _, rows, cols = g.shape
    tr = _rs_tile(rows, cols)

    def body(s_ref, g_ref, a_ref, b0_ref, b1_ref, b2_ref, f_ref):
        own = g_ref[...] + a_ref[...]
        f_ref[...] = ((own + b0_ref[...].astype(F32)) + b1_ref[...].astype(F32)) + b2_ref[...].astype(F32)

    def b_spec(k):
        return pl.BlockSpec((None, tr, cols), lambda i, s: (k, i, 0))

    spec = pltpu.PrefetchScalarGridSpec(
        num_scalar_prefetch=1, grid=(rows // tr,),
        in_specs=[pl.BlockSpec((None, None, tr, cols), lambda i, s: (s[1], s[0], i, 0)),
                  pl.BlockSpec((None, tr, cols), lambda i, s: (s[1], i, 0)), b_spec(0), b_spec(1), b_spec(2)],
        out_specs=pl.BlockSpec((None, tr, cols), lambda i, s: (s[0], i, 0)))
    return pl.pallas_call(
        body, grid_spec=spec, out_shape=jax.ShapeDtypeStruct((2, rows, cols), F32), name=name,
        compiler_params=_params("parallel"))(idx, g, a, b, b, b)


def _rs_share(fs, name):
    nt = len(fs)

    def body(*refs):
        outs = refs[nt:2 * nt]
        send_sems, recv_sems = refs[2 * nt:]
        x, y, c = _mesh_pos()
        cps = [_remote(outs[t].at[c], outs[t].at[c], send_sems.at[t], recv_sems.at[t], (x, y, 1 - c))
               for t in range(nt)]
        for cp in cps:
            cp.start()
        for cp in cps:
            cp.wait()

    return pl.pallas_call(
        body, out_shape=[jax.ShapeDtypeStruct(f.shape, f.dtype) for f in fs],
        in_specs=[ANY] * nt, out_specs=[ANY] * nt, input_output_aliases={t: t for t in range(nt)},
        scratch_shapes=[pltpu.SemaphoreType.DMA((nt,)), pltpu.SemaphoreType.DMA((nt,))], name=name)(*fs)


def _adamw(w, g, m, v, name):
    lead, rows, cols = w.shape
    tr = rows // 4 if rows % 32 == 0 else rows

    def body(w_ref, g_ref, m_ref, v_ref, go_ref, d_ref, mo_ref, vo_ref):
        gv = g_ref[...]
        go_ref[...] = gv
        mn = ADAM_B1 * m_ref[...] + (1.0 - ADAM_B1) * gv
        vn = ADAM_B2 * v_ref[...] + (1.0 - ADAM_B2) * (gv * gv)
        m_hat = mn / (1.0 - ADAM_B1 ** ADAM_STEP)
        v_hat = vn / (1.0 - ADAM_B2 ** ADAM_STEP)
        d_ref[...] = -ADAM_LR * (m_hat / (jnp.sqrt(v_hat) + ADAM_EPS) + ADAM_WD * w_ref[...])
        mo_ref[...] = mn
        vo_ref[...] = vn

    blk = pl.BlockSpec((None, tr, cols), lambda l, i: (l, i, 0))
    out = jax.ShapeDtypeStruct((lead, rows, cols), F32)
    return pl.pallas_call(
        body, grid=(lead, rows // tr), in_specs=[blk] * 4, out_specs=[blk] * 4, out_shape=[out] * 4, name=name,
        compiler_params=_params("parallel", "parallel"))(w, g, m, v)


BIG = ["ret_w_in", "ret_w_out", "dn_w_in", "dn_w_out", "ffn_w_gate", "ffn_w_up", "ffn_w_down"]
TRANSPOSED_AT_BOUNDARY = {"dn_w_in": True, "ffn_w_gate": False, "ffn_w_up": False}
SMALL =["meta_tokens", "mix_norm_w", "ffn_norm_w", "ret_gn_w", "dn_conv_w", "dn_a_log", "dn_dt_bias",
         "dn_norm_w", "final_norm_w"]
SMALL_SHARDED = {"meta_tokens", "dn_conv_w", "dn_norm_w"}
ORDER = ["meta_tokens", "mix_norm_w", "ffn_norm_w", "ret_w_in", "ret_gn_w", "ret_w_out", "dn_w_in",
         "dn_conv_w", "dn_a_log", "dn_dt_bias", "dn_norm_w", "dn_w_out", "ffn_w_gate", "ffn_w_up",
         "ffn_w_down", "final_norm_w"]


def _halves(a):
    return a.reshape(2, -1, a.shape[-1])


def _pack_lanes(parts, align=8):
    flat = jnp.concatenate([p.reshape(-1) for p in parts])
    flat = jnp.pad(flat, (0, -flat.shape[0] % (align * LANES)))
    return flat.reshape(-1, LANES)


def _unpack(buf, shapes):
    lead = buf.shape[:-2]
    flat = buf.reshape(lead + (-1,))
    out, off = [], 0
    for shp in shapes:
        size = math.prod(shp)
        out.append(flat[..., off:off + size].reshape(lead + tuple(shp)))
        off += size
    return out


def _join_cols(shards):
    return jnp.concatenate([shards[j] for j in range(N_SHARD)], axis=-1)


def kernel(x, meta_tokens, mix_norm_w, ffn_norm_w, ret_w_in, ret_gn_w, ret_w_out, dn_w_in, dn_conv_w, dn_a_log, dn_dt_bias, dn_norm_w, dn_w_out, ffn_w_gate, ffn_w_up, ffn_w_down, final_norm_w, loss_target, m_meta_tokens, m_mix_norm_w, m_ffn_norm_w, m_ret_w_in, m_ret_gn_w, m_ret_w_out, m_dn_w_in, m_dn_conv_w, m_dn_a_log, m_dn_dt_bias, m_dn_norm_w, m_dn_w_out, m_ffn_w_gate, m_ffn_w_up, m_ffn_w_down, m_final_norm_w, v_meta_tokens, v_mix_norm_w, v_ffn_norm_w, v_ret_w_in, v_ret_gn_w, v_ret_w_out, v_dn_w_in, v_dn_conv_w, v_dn_a_log, v_dn_dt_bias, v_dn_norm_w, v_dn_w_out, v_ffn_w_gate, v_ffn_w_up, v_ffn_w_down, v_final_norm_w):
    w = dict(meta_tokens=meta_tokens, mix_norm_w=mix_norm_w, ffn_norm_w=ffn_norm_w, ret_w_in=ret_w_in,
             ret_gn_w=ret_gn_w, ret_w_out=ret_w_out, dn_w_in=dn_w_in, dn_conv_w=dn_conv_w, dn_a_log=dn_a_log,
             dn_dt_bias=dn_dt_bias, dn_norm_w=dn_norm_w, dn_w_out=dn_w_out, ffn_w_gate=ffn_w_gate,
             ffn_w_up=ffn_w_up, ffn_w_down=ffn_w_down, final_norm_w=final_norm_w)
    m = dict(meta_tokens=m_meta_tokens, mix_norm_w=m_mix_norm_w, ffn_norm_w=m_ffn_norm_w, ret_w_in=m_ret_w_in,
             ret_gn_w=m_ret_gn_w, ret_w_out=m_ret_w_out, dn_w_in=m_dn_w_in, dn_conv_w=m_dn_conv_w,
             dn_a_log=m_dn_a_log, dn_dt_bias=m_dn_dt_bias, dn_norm_w=m_dn_norm_w, dn_w_out=m_dn_w_out,
             ffn_w_gate=m_ffn_w_gate, ffn_w_up=m_ffn_w_up, ffn_w_down=m_ffn_w_down, final_norm_w=m_final_norm_w)
    v = dict(meta_tokens=v_meta_tokens, mix_norm_w=v_mix_norm_w, ffn_norm_w=v_ffn_norm_w, ret_w_in=v_ret_w_in,
             ret_gn_w=v_ret_gn_w, ret_w_out=v_ret_w_out, dn_w_in=v_dn_w_in, dn_conv_w=v_dn_conv_w,
             dn_a_log=v_dn_a_log, dn_dt_bias=v_dn_dt_bias, dn_norm_w=v_dn_norm_w, dn_w_out=v_dn_w_out,
             ffn_w_gate=v_ffn_w_gate, ffn_w_up=v_ffn_w_up, ffn_w_down=v_ffn_w_down, final_norm_w=v_final_norm_w)
    mx, my, mc = _mesh_pos()
    chip = 2 * mx + my

    sm_names = [n for n in SMALL if n in SMALL_SHARDED]
    sm_gathered = _unpack(_gather_small(_pack_lanes([w[n] for n in sm_names])), [w[n].shape for n in sm_names])
    full = {n: _join_cols(sm_gathered[i]) for i, n in enumerate(sm_names)}
    wts = {
        "meta_tokens": full["meta_tokens"], "mix_norm_w": mix_norm_w, "ffn_norm_w": ffn_norm_w,
        "ret_gn_w": ret_gn_w[0], "final_norm_w": final_norm_w, "dn_conv_w": full["dn_conv_w"][0],
        "dn_a_log": dn_a_log[0], "dn_dt_bias": dn_dt_bias[0], "dn_norm_w": full["dn_norm_w"][0],
    }
    idx = jnp.stack([mc, chip]).astype(jnp.int32)
    shards = {n: _halves(w[n].astype(MXU_DTYPE)) for n in BIG}
    loss_part, dh0, g, reduced = _train_step(x[0], loss_target[0], wts, shards, idx)
    seq = x.shape[1]
    grad_x = dh0[CHUNK:CHUNK + seq].reshape(x.shape)
    gsh = {}

    small_full_shapes = [g[n].shape for n in SMALL] + [(1,)]
    red = _unpack(_allreduce_small(_pack_lanes([g[n] for n in SMALL] + [loss_part[0, :1]])), small_full_shapes)
    loss = red[-1][0]
    for i, n in enumerate(SMALL):
        gn = red[i]
        if n in SMALL_SHARDED:
            width = w[n].shape[-1]
            gn = lax.dynamic_slice_in_dim(gn, chip * width, width, axis=gn.ndim - 1)
        gsh[n] = gn.reshape(w[n].shape)

    delta, new_m, new_v = {}, {}, {}
    for n in BIG:
        shp = w[n].shape
        if n in TRANSPOSED_AT_BOUNDARY and TRANSPOSED_AT_BOUNDARY[n]:
            view = lambda a: jnp.swapaxes(a, 1, 2).reshape(1, -1, LANES)
            back = lambda a: jnp.swapaxes(a.reshape(shp[0], shp[2], shp[1]), 1, 2)
        elif n in TRANSPOSED_AT_BOUNDARY:
            view = back = lambda a: jnp.swapaxes(a, 1, 2)
        else:
            view = back = lambda a: a
        res = _adamw(view(w[n]), view(reduced[n].reshape(shp)), view(m[n]), view(v[n]), "adamw_" + n)
        gsh[n], delta[n], new_m[n], new_v[n] = [back(r) for r in res]
    sm_local_shapes = [w[n].shape for n in SMALL]
    _, d_, m_, v_ = _adamw(*[_pack_lanes([t[n] for n in SMALL])[None] for t in (w, gsh, m, v)], "adamw_small")
    d_, m_, v_ = d_[0], m_[0], v_[0]
    for n, dd, mm, vv in zip(SMALL, _unpack(d_, sm_local_shapes), _unpack(m_, sm_local_shapes),
                             _unpack(v_, sm_local_shapes)):
        delta[n], new_m[n], new_v[n] = dd, mm, vv

    return (loss, grad_x, *[gsh[n] for n in ORDER], *[delta[n] for n in ORDER],
            *[new_m[n] for n in ORDER], *[new_v[n] for n in ORDER])
```

```python
import functools
import math

import jax
import jax.numpy as jnp
from jax import lax
from jax.experimental import pallas as pl
from jax.experimental.pallas import tpu as pltpu

F32 = jnp.float32
BF16 = jnp.bfloat16
MXU_DTYPE = BF16

D_MODEL = 1024
N_META = 16
CHUNK = 64
PAD = CHUNK - N_META
RMS_EPS = 1e-6
RET_HEADS, RET_DK, RET_DV = 4, 256, 512
RET_QK, RET_V = RET_HEADS * RET_DK, RET_HEADS * RET_DV
RET_IN = 2 * RET_QK + 2 * RET_V
ROPE_BASE = 10000.0
DN_HEADS, DN_DK, DN_DV = 8, 128, 256
DN_QK, DN_V = DN_HEADS * DN_DK, DN_HEADS * DN_DV
DN_CONV_CH = 2 * DN_QK + DN_V
DN_IN = DN_CONV_CH + DN_V + 2 * DN_HEADS
LANES = 128
DN_IN_USED = DN_CONV_CH + DN_V + LANES
DN_IN_PAD = DN_IN_USED + LANES
CONV_K = 4
FFN_HIDDEN = 2816
ADAM_LR, ADAM_B1, ADAM_B2, ADAM_EPS, ADAM_WD, ADAM_STEP = 0.001, 0.9, 0.999, 1e-08, 0.01, 10

ROW_ALIGN = 256
VMEM_LIMIT = 56 * 1024 * 1024
MESH = pl.DeviceIdType.MESH
ANY = pl.BlockSpec(memory_space=pl.ANY)
VMEM_SPEC = pl.BlockSpec(memory_space=pltpu.VMEM)
_HI = lax.Precision.HIGHEST


def _params(*sem):
    return pltpu.CompilerParams(dimension_semantics=sem, vmem_limit_bytes=VMEM_LIMIT)


def _dg(a, b, ca, cb, hi):
    dims = (((ca,), (cb,)), ((), ()))

    def dot(p, q):
        return lax.dot_general(p, q, dims, preferred_element_type=F32)

    if not hi:
        return dot(a.astype(MXU_DTYPE), b.astype(MXU_DTYPE))
    if MXU_DTYPE == F32:
        return lax.dot_general(a, b, dims, precision=_HI, preferred_element_type=F32)
    a_hi, b_hi = a.astype(MXU_DTYPE), b.astype(MXU_DTYPE)
    a_lo = (a - a_hi.astype(F32)).astype(MXU_DTYPE)
    b_lo = (b - b_hi.astype(F32)).astype(MXU_DTYPE)
    return dot(a_hi, b_hi) + (dot(a_hi, b_lo) + dot(a_lo, b_hi))


def _nn(a, b, hi=False):
    return _dg(a, b, 1, 0, hi)


def _nt(a, b, hi=False):
    return _dg(a, b, 1, 1, hi)


def _tn(a, b, hi=False):
    return _dg(a, b, 0, 0, hi)


def _iota(shape, dim):
    return lax.broadcasted_iota(jnp.int32, shape, dim)


def _valid_rows(first_row, rows, seq):
    r = first_row + _iota((rows, 1), 0)
    return ((r >= PAD) & (r < CHUNK + seq)).astype(F32)


def _rope(t, cs, sn):
    half = t.shape[-1] // 2
    t1, t2 = t[:, :half], t[:, half:]
    return jnp.concatenate([t1 * cs - t2 * sn, t1 * sn + t2 * cs], axis=1)


def _rope_bwd(d, cs, sn):
    half = d.shape[-1] // 2
    d1, d2 = d[:, :half], d[:, half:]
    return jnp.concatenate([d1 * cs + d2 * sn, d2 * cs - d1 * sn], axis=1)


def _col(x, idx):
    oh = (_iota((1, x.shape[1]), 1) == idx).astype(F32)
    return jnp.sum(x * oh, axis=1, keepdims=True)


def _row(x, idx):
    oh = (_iota((x.shape[0], 1), 0) == idx).astype(F32)
    return jnp.sum(x * oh, axis=0, keepdims=True)


def _shift_down(x, halo8, k):
    xr = pltpu.roll(x, k, 0)
    hr = pltpu.roll(halo8, k, 0)
    first = jnp.where(_iota((8, 1), 0) < k, hr, xr[0:8])
    return jnp.concatenate([first, xr[8:]], axis=0)


def _shift_up(x, next8, j):
    rows = x.shape[0]
    xr = pltpu.roll(x, rows - j, 0)
    nr = pltpu.roll(next8, 8 - j, 0)
    last = jnp.where(_iota((8, 1), 0) >= 8 - j, nr, xr[rows - 8:])
    return jnp.concatenate([xr[:rows - 8], last], axis=0)


def _gated_norm(o, gate, w):
    r = lax.rsqrt(jnp.mean(o * o, axis=-1, keepdims=True) + RMS_EPS)
    return o * r * w * (gate * jax.nn.sigmoid(gate))


def _gated_norm_bwd(dy, o, gate, w):
    r = lax.rsqrt(jnp.mean(o * o, axis=-1, keepdims=True) + RMS_EPS)
    nrm = o * r
    sg = jax.nn.sigmoid(gate)
    sl = gate * sg
    dgate = dy * nrm * w * (sg * (1.0 + gate * (1.0 - sg)))
    dn = dy * w * sl
    dw = jnp.sum(dy * nrm * sl, axis=0, keepdims=True)
    do = r * (dn - nrm * jnp.mean(dn * nrm, axis=-1, keepdims=True))
    return do, dgate, dw


def _softplus(z):
    return jnp.maximum(z, 0.0) + jnp.log(1.0 + jnp.exp(-jnp.abs(z)))


def _row_tile(rows, cap=768):
    for t in (768, 512, 256, 128, 64, 32, 16, 8):
        if t <= cap and rows % t == 0:
            return t
    return rows


TILE_BUDGET = 44 * 1024 * 1024


def _fit_rows(rows, row_bytes, fixed_bytes, value_row_bytes):
    best = None
    for t in range(LANES, rows + 1, LANES):
        if rows % t == 0 and 2 * (row_bytes * t + fixed_bytes) + value_row_bytes * t <= TILE_BUDGET:
            best = t
    return best or _row_tile(rows, 256)


def _div_tile(n, cap, mult):
    best = None
    for t in range(mult, min(cap, n) + 1, mult):
        if n % t == 0:
            best = t
    return best or n


def _col_tile(cols, cap=1536):
    best = None
    for t in range(LANES, min(cap, cols) + 1, LANES):
        if cols % t == 0:
            best = t
    return best or cols


def _rms_fwd(h, w, name, ride=None):
    rows, d = h.shape
    tm = _row_tile(rows)

    def body(h_ref, w_ref, o_ref):
        x = h_ref[...]
        r = lax.rsqrt(jnp.mean(x * x, axis=-1, keepdims=True) + RMS_EPS)
        o_ref[...] = (x * r * w_ref[...]).astype(o_ref.dtype)

    res, rode = _pcall(body, [h, w.reshape(1, d)], grid=(rows // tm,),
                       in_specs=[pl.BlockSpec((tm, d), lambda i: (i, 0)), pl.BlockSpec((1, d), lambda i: (0, 0))],
                       out_specs=[pl.BlockSpec((tm, d), lambda i: (i, 0))],
                       out_shape=[jax.ShapeDtypeStruct((rows, d), BF16)], name=name, sem=("parallel",), ride=ride)
    return res[0] if ride is None else (res[0], rode)


def _gmm_rms(name, grid, args, in_specs, row_spec, fn, h, w, resid, row_axis, red_axis=None, ride=None):
    m, d = h.shape
    n_in = len(args)
    vec = pl.BlockSpec((1, d), lambda *g: (0, 0))

    def body(*refs):
        ins = refs[:n_in]
        h_ref, w_ref, r_ref, dh_ref, dw_ref, dh16_ref = refs[n_in:]
        part = fn(*ins)
        row = pl.program_id(row_axis)

        def finish(dy):
            x = h_ref[...]
            r = lax.rsqrt(jnp.mean(x * x, axis=-1, keepdims=True) + RMS_EPS)
            xh = x * r
            dxh = dy * w_ref[...]
            dh = r_ref[...] + r * (dxh - xh * jnp.mean(dxh * xh, axis=-1, keepdims=True))
            dh_ref[...] = dh
            dh16_ref[...] = dh.astype(dh16_ref.dtype)
            dwp = jnp.sum(dy * xh, axis=0, keepdims=True)

            @pl.when(row == 0)
            def _():
                dw_ref[...] = dwp

            @pl.when(row > 0)
            def _():
                dw_ref[...] += dwp

        if red_axis is None:
            finish(part)
            return
        k = pl.program_id(red_axis)

        @pl.when(k == 0)
        def _():
            dh_ref[...] = part

        @pl.when(k > 0)
        def _():
            dh_ref[...] += part

        @pl.when(k == grid[red_axis] - 1)
        def _():
            finish(dh_ref[...])

    res, rode = _pcall(body, list(args) + [h, w.reshape(1, d), resid], grid=grid,
                       in_specs=list(in_specs) + [row_spec, vec, row_spec], out_specs=[row_spec, vec, row_spec],
                       out_shape=[jax.ShapeDtypeStruct((m, d), F32), jax.ShapeDtypeStruct((1, d), F32),
                                  jax.ShapeDtypeStruct((m, d), BF16)],
                       name=name, sem=("arbitrary",) * len(grid), ride=ride)
    return res if ride is None else (res, rode)


def _final_loss(h, w, tgt, seq, name):
    rows, d = h.shape
    tm = _row_tile(rows)

    def body(h_ref, w_ref, t_ref, dh_ref, dw_ref, loss_ref, dh16_ref):
        i = pl.program_id(0)
        r_idx = i * tm + _iota((tm, 1), 0)
        m = ((r_idx >= CHUNK) & (r_idx < CHUNK + seq)).astype(F32)
        x = h_ref[...]
        wv = w_ref[...]
        r = lax.rsqrt(jnp.mean(x * x, axis=-1, keepdims=True) + RMS_EPS)
        xh = x * r
        err = (xh * wv - t_ref[...]) * m
        lpart = 0.5 * jnp.sum(jnp.mean(err * err, axis=-1, keepdims=True), axis=0, keepdims=True)
        dyv = err * (1.0 / d)
        dxh = dyv * wv
        dh = r * (dxh - xh * jnp.mean(dxh * xh, axis=-1, keepdims=True))
        dh_ref[...] = dh
        dh16_ref[...] = dh.astype(dh16_ref.dtype)
        part = jnp.sum(dyv * xh, axis=0, keepdims=True)

        @pl.when(i == 0)
        def _():
            dw_ref[...] = part
            loss_ref[...] = jnp.broadcast_to(lpart, loss_ref.shape)

        @pl.when(i > 0)
        def _():
            dw_ref[...] += part
            loss_ref[...] += jnp.broadcast_to(lpart, loss_ref.shape)

    blk = pl.BlockSpec((tm, d), lambda i: (i, 0))
    vec = pl.BlockSpec((1, d), lambda i: (0, 0))
    return pl.pallas_call(
        body, grid=(rows // tm,), in_specs=[blk, vec, blk],
        out_specs=[blk, vec, pl.BlockSpec((1, LANES), lambda i: (0, 0)), blk],
        out_shape=[jax.ShapeDtypeStruct((rows, d), F32), jax.ShapeDtypeStruct((1, d), F32),
                   jax.ShapeDtypeStruct((1, LANES), F32), jax.ShapeDtypeStruct((rows, d), BF16)],
        name=name, compiler_params=_params("arbitrary"))(h, w.reshape(1, d), tgt)


def _isz(x):
    return jnp.dtype(x.dtype).itemsize


def _mm(a, b, *, mode, name, out_dtype=F32, resid=None, col_cap=1536, ride=None):
    if mode == "tn":
        m, k = a.shape
        n = b.shape[1]
        tn = _col_tile(n, col_cap)
        tm = _fit_rows(m, k * _isz(a) + tn * _isz(b), (3 * k * tn * 4) // 2, 2 * (k + tn))

        def body_tn(a_ref, b_ref, o_ref):
            i = pl.program_id(1)
            part = _tn(a_ref[...], b_ref[...])

            @pl.when(i == 0)
            def _():
                o_ref[...] = part

            @pl.when(i > 0)
            def _():
                o_ref[...] += part

        return pl.pallas_call(
            body_tn, grid=(n // tn, m // tm),
            in_specs=[pl.BlockSpec((tm, k), lambda j, i: (i, 0)),
                      pl.BlockSpec((tm, tn), lambda j, i: (i, j))],
            out_specs=pl.BlockSpec((k, tn), lambda j, i: (0, j)),
            out_shape=jax.ShapeDtypeStruct((k, n), F32), name=name,
            compiler_params=_params("parallel", "arbitrary"))(a, b)

    m, ka = a.shape
    n = b.shape[1] if mode == "nn" else b.shape[0]
    has_resid = resid is not None
    tn = _col_tile(n, col_cap)
    tm = _fit_rows(m, ka * _isz(a) + tn * (jnp.dtype(out_dtype).itemsize + (4 if has_resid else 0)),
                   ka * tn * _isz(b), 2 * ka + 8 * tn)

    def body(*refs):
        if has_resid:
            a_ref, b_ref, r_ref, o_ref = refs
        else:
            a_ref, b_ref, o_ref = refs
        acc = _nn(a_ref[...], b_ref[...]) if mode == "nn" else _nt(a_ref[...], b_ref[...])
        if has_resid:
            acc = acc + r_ref[...]
        o_ref[...] = acc.astype(o_ref.dtype)

    b_spec = (pl.BlockSpec((b.shape[0], tn), lambda j, i: (0, j)) if mode == "nn"
              else pl.BlockSpec((tn, b.shape[1]), lambda j, i: (j, 0)))
    o_spec = pl.BlockSpec((tm, tn), lambda j, i: (i, j))
    in_specs = [pl.BlockSpec((tm, ka), lambda j, i: (i, 0)), b_spec]
    args = [a, b]
    if has_resid:
        in_specs.append(o_spec)
        args.append(resid)
    res, rode = _pcall(body, args, grid=(n // tn, m // tm), in_specs=in_specs, out_specs=[o_spec],
                       out_shape=[jax.ShapeDtypeStruct((m, n), out_dtype)], name=name,
                       sem=("parallel", "parallel"), ride=ride)
    return res[0] if ride is None else (res[0], rode)


N_SHARD = 4


def _gmm(name, grid, args, in_specs, out_specs, out_shape, fn, red_axis=None, init_arg=None, aliases=None,
         ride=None):
    n_in = len(args)
    single = not isinstance(out_shape, (list, tuple))
    out_specs = [out_specs] if single else list(out_specs)
    out_shape = [out_shape] if single else list(out_shape)

    def body(*refs):
        _gmm_step(fn, refs[:n_in], refs[n_in:], red_axis, init_arg)

    sem = tuple("arbitrary" if ax == red_axis else "parallel" for ax in range(len(grid)))
    res, rode = _pcall(body, args, grid=grid, in_specs=in_specs, out_specs=out_specs, out_shape=out_shape,
                       name=name, sem=sem, aliases=aliases, ride=ride)
    ours = res[0] if single else res
    return ours if ride is None else (ours, rode)


def _gmm_step(fn, ins, outs, red_axis, init_arg):
    parts = fn(*ins)
    if red_axis is None:
        for o_ref, p in zip(outs, parts):
            o_ref[...] = p.astype(o_ref.dtype)
        return
    k = pl.program_id(red_axis)

    @pl.when(k == 0)
    def _():
        for idx, (o_ref, p) in enumerate(zip(outs, parts)):
            o_ref[...] = p + ins[init_arg][...] if (idx == 0 and init_arg is not None) else p

    @pl.when(k > 0)
    def _():
        for o_ref, p in zip(outs, parts):
            o_ref[...] += p


def _ride_body(ride, grid, n_in, n_out, n_scratch, body):
    n_rin, n_rout = len(ride.arrays), len(ride.out_shape)
    nsteps = math.prod(grid)

    def wrapped(*refs):
        ins = refs[:n_in]
        r_ins = refs[n_in:n_in + n_rin]
        o0 = n_in + n_rin
        outs = refs[o0:o0 + n_out]
        r_outs = refs[o0 + n_out:o0 + n_out + n_rout]
        s0 = o0 + n_out + n_rout
        scratch = refs[s0:s0 + n_scratch]
        send_sems, recv_sems = refs[-2:]
        step = pl.program_id(0)
        for ax in range(1, len(grid)):
            step = step * grid[ax] + pl.program_id(ax)
        ride.emit(step, nsteps, r_ins, r_outs, send_sems, recv_sems, before=True)
        body(*ins, *outs, *scratch)
        ride.emit(step, nsteps, r_ins, r_outs, send_sems, recv_sems, before=False)

    return wrapped


def _pcall(body, args, *, grid, in_specs, out_specs, out_shape, name, sem, scratch=(), aliases=None, ride=None):
    if ride is None:
        res = pl.pallas_call(body, grid=grid, in_specs=list(in_specs), out_specs=list(out_specs),
                             out_shape=list(out_shape), scratch_shapes=list(scratch), name=name,
                             input_output_aliases=aliases or {}, compiler_params=_params(*sem))(*args)
        return res, None
    n_in, n_out = len(args), len(out_shape)
    res = pl.pallas_call(
        _ride_body(ride, grid, n_in, n_out, len(scratch), body), grid=grid,
        in_specs=list(in_specs) + ride.in_specs, out_specs=list(out_specs) + ride.out_specs,
        out_shape=list(out_shape) + ride.out_shape, scratch_shapes=list(scratch) + ride.scratch, name=name,
        input_output_aliases=aliases or {},
        compiler_params=_params(*(("arbitrary",) * len(grid))))(*args, *ride.arrays)
    return res[:n_out], res[n_out:]


def _mm_cols(a, ws, name, ride=None):
    m, k = a.shape
    n = ws.shape[2]
    tm = _fit_rows(m, k * _isz(a) + n * 4, k * n * _isz(ws), 4 * n)
    return _gmm(name, (N_SHARD, m // tm), [a, ws],
                [pl.BlockSpec((tm, k), lambda j, i: (i, 0)), pl.BlockSpec((None, k, n), lambda j, i: (j, 0, 0))],
                pl.BlockSpec((tm, n), lambda j, i: (i, j)), jax.ShapeDtypeStruct((m, N_SHARD * n), F32),
                lambda a_ref, w_ref: (_nn(a_ref[...], w_ref[...]),), ride=ride)


def _mm_cols_t_rms(d, ws, h, w, resid, name, ride=None):
    m = d.shape[0]
    _, k, n = ws.shape
    tm = _fit_rows(m, n * _isz(d) + 3 * k * 4, k * n * _isz(ws), 16 * k)
    return _gmm_rms(name, (m // tm, N_SHARD), [d, ws],
                    [pl.BlockSpec((tm, n), lambda i, j: (i, j)), pl.BlockSpec((None, k, n), lambda i, j: (j, 0, 0))],
                    pl.BlockSpec((tm, k), lambda i, j: (i, 0)),
                    lambda d_ref, w_ref: _nt(d_ref[...], w_ref[...]), h, w, resid, 0, red_axis=1, ride=ride)


def _mm_nt_rms(a, b, h, w, resid, name, ride=None):
    m, n = a.shape
    k = b.shape[0]
    tm = _fit_rows(m, n * _isz(a) + 3 * k * 4, k * n * _isz(b), 16 * k)
    return _gmm_rms(name, (m // tm,), [a, b],
                    [pl.BlockSpec((tm, n), lambda i: (i, 0)), pl.BlockSpec((k, n), lambda i: (0, 0))],
                    pl.BlockSpec((tm, k), lambda i: (i, 0)),
                    lambda a_ref, b_ref: _nt(a_ref[...], b_ref[...]), h, w, resid, 0, ride=ride)


def _mm_cols_grad(a, d, name):
    m, k = a.shape
    n = d.shape[1] // N_SHARD
    tm = _fit_rows(m, k * _isz(a) + n * _isz(d), (3 * k * n * 4) // 2, 2 * (k + n))
    return _gmm(name, (N_SHARD, m // tm), [a, d],
                [pl.BlockSpec((tm, k), lambda j, i: (i, 0)), pl.BlockSpec((tm, n), lambda j, i: (i, j))],
                pl.BlockSpec((None, k, n), lambda j, i: (j, 0, 0)), jax.ShapeDtypeStruct((N_SHARD, k, n), F32),
                lambda a_ref, d_ref: (_tn(a_ref[...], d_ref[...]),), red_axis=1)


def _ffn_up(hn, wg, wu, layer, name, ride=None):
    m, k = hn.shape
    n = wg.shape[3]
    tm = _fit_rows(m, k * _isz(hn) + 3 * n * jnp.dtype(BF16).itemsize, 2 * k * n * _isz(wg), 16 * n)

    def fn(a_ref, wg_ref, wu_ref):
        a = a_ref[...]
        g = _nn(a, wg_ref[...])
        u = _nn(a, wu_ref[...])
        return g, u, g * jax.nn.sigmoid(g) * u

    w_spec = pl.BlockSpec((None, None, k, n), lambda j, i: (j, layer, 0, 0))
    o_spec = pl.BlockSpec((None, tm, n), lambda j, i: (j, i, 0))
    out = jax.ShapeDtypeStruct((N_SHARD, m, n), BF16)
    return _gmm(name, (N_SHARD, m // tm), [hn, wg, wu],
                [pl.BlockSpec((tm, k), lambda j, i: (i, 0)), w_spec, w_spec],
                [o_spec, o_spec, o_spec], [out, out, out], fn, ride=ride)


def _ffn_down(act, wd, resid, layer, name):
    _, m, n = act.shape
    d = wd.shape[3]
    tm = _fit_rows(m, N_SHARD * n * _isz(act) + 2 * d * 4, N_SHARD * n * d * _isz(wd), 8 * d)

    def fn(a_ref, w_ref, r_ref):
        acc = r_ref[...]
        for j in range(N_SHARD):
            acc = acc + _nn(a_ref[j], w_ref[j])
        return (acc,)

    row = pl.BlockSpec((tm, d), lambda i: (i, 0))
    return _gmm(name, (m // tm,), [act, wd, resid],
                [pl.BlockSpec((N_SHARD, tm, n), lambda i: (0, i, 0)),
                 pl.BlockSpec((N_SHARD, None, n, d), lambda i: (0, layer, 0, 0)), row],
                row, jax.ShapeDtypeStruct((m, d), F32), fn)


def _ffn_down_bwd(dh, wd, g, u, layer, name, ride=None):
    m, d = dh.shape
    n = wd.shape[2]
    tm = _fit_rows(m, d * _isz(dh) + 4 * N_SHARD * n * jnp.dtype(BF16).itemsize, N_SHARD * n * d * _isz(wd),
                   2 * d + 24 * n)

    def body(dh_ref, wd_ref, g_ref, u_ref, dg_ref, du_ref):
        dhv = dh_ref[...].astype(MXU_DTYPE)
        for j in range(N_SHARD):
            dact = _nt(dhv, wd_ref[j])
            gv = g_ref[j].astype(F32)
            sg = jax.nn.sigmoid(gv)
            gs = gv * sg
            dg_ref[j] = (dact * u_ref[j].astype(F32) * (sg + gs * (1.0 - sg))).astype(dg_ref.dtype)
            du_ref[j] = (dact * gs).astype(du_ref.dtype)

    sh_spec = pl.BlockSpec((N_SHARD, tm, n), lambda i: (0, i, 0))
    out = jax.ShapeDtypeStruct((N_SHARD, m, n), BF16)
    res, rode = _pcall(body, [dh, wd, g, u], grid=(m // tm,),
                       in_specs=[pl.BlockSpec((tm, d), lambda i: (i, 0)),
                                 pl.BlockSpec((N_SHARD, None, n, d), lambda i: (0, layer, 0, 0)), sh_spec, sh_spec],
                       out_specs=[sh_spec, sh_spec], out_shape=[out, out], name=name, sem=("parallel",), ride=ride)
    return res if ride is None else (res, rode)


def _ffn_up_bwd(dg, du, wg, wu, layer, h, w, resid, name, ride=None):
    _, m, n = dg.shape
    k = wg.shape[2]
    tm = _fit_rows(m, 2 * N_SHARD * n * _isz(dg) + 3 * k * 4, 2 * N_SHARD * k * n * _isz(wg), 16 * k)

    def fn(dg_ref, du_ref, wg_ref, wu_ref):
        acc = _nt(dg_ref[0], wg_ref[0]) + _nt(du_ref[0], wu_ref[0])
        for j in range(1, N_SHARD):
            acc = acc + _nt(dg_ref[j], wg_ref[j]) + _nt(du_ref[j], wu_ref[j])
        return acc

    d_spec = pl.BlockSpec((N_SHARD, tm, n), lambda i: (0, i, 0))
    w_spec = pl.BlockSpec((N_SHARD, None, k, n), lambda i: (0, layer, 0, 0))
    return _gmm_rms(name, (m // tm,), [dg, du, wg, wu], [d_spec, d_spec, w_spec, w_spec],
                    pl.BlockSpec((tm, k), lambda i: (i, 0)), fn, h, w, resid, 0, ride=ride)


def _ffn_wgrad(lhs, rhs_list, layer, layers, prev, lhs_sharded, name):
    if lhs_sharded:
        _, m, k = lhs.shape
        n = rhs_list[0].shape[1]
    else:
        m, k = lhs.shape
        n = rhs_list[0].shape[2]
    n_out = len(rhs_list)
    tm = _fit_rows(m, k * _isz(lhs) + n_out * n * _isz(rhs_list[0]), (3 * n_out * k * n * 4) // 2,
                   2 * (k + n_out * n))
    sh = pl.BlockSpec((None, tm, k if lhs_sharded else n), lambda j, i: (j, i, 0))
    fl = pl.BlockSpec((tm, n if lhs_sharded else k), lambda j, i: (i, 0))
    n_out = len(rhs_list)
    args = [lhs] + list(rhs_list)
    in_specs = [sh if lhs_sharded else fl] + [fl if lhs_sharded else sh] * n_out
    aliases = None
    if prev is not None:
        aliases = {len(args) + t: t for t in range(n_out)}
        args = args + list(prev)
        in_specs = in_specs + [ANY] * n_out

    def fn(l_ref, *rest):
        lv = l_ref[...]
        return tuple(_tn(lv, r_ref[...]) for r_ref in rest[:n_out])

    o_spec = pl.BlockSpec((None, None, k, n), lambda j, i: (j, layer, 0, 0))
    out = jax.ShapeDtypeStruct((N_SHARD, layers, k, n), F32)
    return _gmm(name, (N_SHARD, m // tm), args, in_specs, [o_spec] * n_out, [out] * n_out, fn,
                red_axis=1, aliases=aliases)


def _ret_consts():
    log_gamma = jnp.log1p(-jnp.exp2(-5.0 - jnp.arange(RET_HEADS, dtype=F32)))
    idx = jnp.arange(CHUNK, dtype=F32)
    rel = idx[:, None] - idx[None, :]
    dmask = jnp.where((rel >= 0)[None], jnp.exp(log_gamma[:, None, None] * jnp.maximum(rel, 0.0)), 0.0)
    xi = jnp.exp(log_gamma[:, None] * (idx[None, :] + 1.0))[:, :, None]
    zeta = jnp.exp(log_gamma[:, None] * (CHUNK - 1.0 - idx[None, :]))[:, :, None]
    gamma_c = jnp.exp(log_gamma * CHUNK)
    wide = (RET_HEADS, CHUNK, RET_DK)
    return dmask, jnp.broadcast_to(xi, wide), jnp.broadcast_to(zeta, wide), gamma_c


def _rope_tables(nc):
    half = RET_DK // 2
    inv_freq = ROPE_BASE ** (-jnp.arange(half, dtype=F32) / half)
    a_chunk = (jnp.arange(nc) * CHUNK - PAD).astype(F32)[:, None] * inv_freq[None, :]
    a_row = jnp.arange(CHUNK).astype(F32)[:, None] * inv_freq[None, :]
    return (jnp.stack([jnp.cos(a_chunk), jnp.sin(a_chunk)], axis=1),
            jnp.stack([jnp.cos(a_row), jnp.sin(a_row)], axis=0))


RET_CPS = 2


def _rope_chunk(rc_ref, rr_ref, c):
    cc, sc = rc_ref[c, 0:1, :], rc_ref[c, 1:2, :]
    cr, sr = rr_ref[0], rr_ref[1]
    return cc * cr - sc * sr, sc * cr + cc * sr


def _rope_specs(order):
    half = RET_DK // 2
    return [pl.BlockSpec((RET_CPS, 2, half), lambda n: (order(n), 0, 0)),
            pl.BlockSpec((2, CHUNK, half), lambda n: (0, 0, 0))]


def _ret_specs(order):
    rows = RET_CPS * CHUNK
    return [pl.BlockSpec((rows, RET_QK), lambda n: (order(n), 0)),
            pl.BlockSpec((rows, RET_QK), lambda n: (order(n), 1)),
            pl.BlockSpec((rows, RET_V), lambda n: (order(n), 1)),
            pl.BlockSpec((rows, RET_V), lambda n: (order(n), 2))]


def _ret_const_specs():
    return [pl.BlockSpec((RET_HEADS, CHUNK, CHUNK), lambda n: (0, 0, 0)),
            pl.BlockSpec((RET_HEADS, CHUNK, RET_DK), lambda n: (0, 0, 0)),
            pl.BlockSpec((RET_HEADS, CHUNK, RET_DK), lambda n: (0, 0, 0)),
            pl.BlockSpec((1, RET_DV), lambda n: (0, 0))]


def _ret_fwd(proj, cos, sin, consts, gn_w, seq, ride=None):
    rows = proj.shape[0]
    nc = rows // CHUNK
    dmask, xi, zeta, gamma_c = consts

    def body(gam_ref, q_ref, k_ref, v_ref, g_ref, cos_ref, sin_ref, dm_ref, xi_ref, ze_ref, gn_ref,
             o_ref, y_ref, ss_ref, s_ref):
        n = pl.program_id(0)

        @pl.when(n == 0)
        def _():
            s_ref[...] = jnp.zeros_like(s_ref)

        gn = gn_ref[...]
        hs = range(RET_HEADS)
        qk_cols = [slice(h * RET_DK, (h + 1) * RET_DK) for h in hs]
        v_cols = [slice(h * RET_DV, (h + 1) * RET_DV) for h in hs]
        for c in range(RET_CPS):
            rs = slice(c * CHUNK, (c + 1) * CHUNK)
            cs, sn = _rope_chunk(cos_ref, sin_ref, c)
            kscale = _valid_rows((n * RET_CPS + c) * CHUNK, CHUNK, seq) * (RET_DK ** -0.5)
            qr_l = [_rope(q_ref[rs, col], cs, sn) for col in qk_cols]
            kr_l = [_rope(k_ref[rs, col], cs, sn) * kscale for col in qk_cols]
            v_l = [v_ref[rs, col] for col in v_cols]
            s_l = [s_ref[h] for h in hs]
            sc_l = [_nt(qr, kr) * dm_ref[h] for h, (qr, kr) in enumerate(zip(qr_l, kr_l))]
            o_l = [_nn(sc_l[h], v_l[h]) + _nn(qr_l[h] * xi_ref[h], s_l[h]) for h in hs]
            for h in hs:
                ss_ref[c, h] = s_l[h].astype(ss_ref.dtype)
                s_ref[h] = gam_ref[h] * s_l[h] + _tn(kr_l[h] * ze_ref[h], v_l[h])
                o_ref[rs, v_cols[h]] = o_l[h]
                y_ref[rs, v_cols[h]] = _gated_norm(o_l[h], g_ref[rs, v_cols[h]], gn).astype(y_ref.dtype)

    fwd = lambda n: n
    row_v = pl.BlockSpec((RET_CPS * CHUNK, RET_V), lambda n: (n, 0))
    res, rode = _pcall(
        body, [gamma_c, proj, proj, proj, proj, cos, sin, dmask, xi, zeta, gn_w.reshape(1, RET_DV)],
        grid=(nc // RET_CPS,),
        in_specs=[pl.BlockSpec(memory_space=pltpu.SMEM)] + _ret_specs(fwd) + _rope_specs(fwd)
        + _ret_const_specs(),
        out_specs=[row_v, row_v,
                   pl.BlockSpec((RET_CPS, RET_HEADS, RET_DK, RET_DV), lambda n: (n, 0, 0, 0))],
        out_shape=[jax.ShapeDtypeStruct((rows, RET_V), F32), jax.ShapeDtypeStruct((rows, RET_V), BF16),
                   jax.ShapeDtypeStruct((nc, RET_HEADS, RET_DK, RET_DV), BF16)],
        scratch=[pltpu.VMEM((RET_HEADS, RET_DK, RET_DV), F32)], name="ret_fwd", sem=("arbitrary",), ride=ride)
    return res if ride is None else (res, rode)


def _ret_bwd(proj, o, dy, states, cos, sin, consts, gn_w, seq, ride=None):
    rows = proj.shape[0]
    nc = rows // CHUNK
    dmask, xi, zeta, gamma_c = consts

    def body(gam_ref, q_ref, k_ref, v_ref, g_ref, o_ref, dy_ref, ss_ref, cos_ref, sin_ref,
             dm_ref, xi_ref, ze_ref, gn_ref, dp_ref, dgn_ref, ds_ref):
        n = pl.program_id(0)

        @pl.when(n == 0)
        def _():
            ds_ref[...] = jnp.zeros_like(ds_ref)
            dgn_ref[...] = jnp.zeros_like(dgn_ref)

        gn = gn_ref[...]
        dgn = jnp.zeros((1, RET_DV), F32)
        hs = range(RET_HEADS)
        qk_cols = [slice(h * RET_DK, (h + 1) * RET_DK) for h in hs]
        v_cols = [slice(h * RET_DV, (h + 1) * RET_DV) for h in hs]
        for c in reversed(range(RET_CPS)):
            rs = slice(c * CHUNK, (c + 1) * CHUNK)
            cs, sn = _rope_chunk(cos_ref, sin_ref, c)
            kscale = _valid_rows(((steps - 1 - n) * RET_CPS + c) * CHUNK, CHUNK, seq) * (RET_DK ** -0.5)
            qr_l = [_rope(q_ref[rs, col], cs, sn) for col in qk_cols]
            kr_l = [_rope(k_ref[rs, col], cs, sn) * kscale for col in qk_cols]
            v_l = [v_ref[rs, col] for col in v_cols]
            s_l = [ss_ref[c, h] for h in hs]
            ds_l = [ds_ref[h] for h in hs]
            sc_l = [_nt(qr_l[h], kr_l[h]) * dm_ref[h] for h in hs]
            gnb = [_gated_norm_bwd(dy_ref[rs, col], o_ref[rs, col], g_ref[rs, col], gn) for col in v_cols]
            do_l = [x[0] for x in gnb]
            dsc_l = [_nt(do_l[h], v_l[h]) * dm_ref[h] for h in hs]
            dv_l = [_tn(sc_l[h], do_l[h]) + _nn(kr_l[h] * ze_ref[h], ds_l[h]) for h in hs]
            dqr_l = [_nn(dsc_l[h], kr_l[h]) + _nt(do_l[h], s_l[h]) * xi_ref[h] for h in hs]
            dkr_l = [_tn(dsc_l[h], qr_l[h]) + _nt(v_l[h], ds_l[h]) * ze_ref[h] for h in hs]
            for h in hs:
                dgn = dgn + gnb[h][2]
                ds_ref[h] = gam_ref[h] * ds_l[h] + _tn(qr_l[h] * xi_ref[h], do_l[h])
                dp_ref[rs, qk_cols[h]] = _rope_bwd(dqr_l[h], cs, sn).astype(dp_ref.dtype)
                dp_ref[rs, RET_QK + h * RET_DK:RET_QK + (h + 1) * RET_DK] = (
                    _rope_bwd(dkr_l[h] * kscale, cs, sn).astype(dp_ref.dtype))
                dp_ref[rs, 2 * RET_QK + h * RET_DV:2 * RET_QK + (h + 1) * RET_DV] = dv_l[h].astype(dp_ref.dtype)
                dp_ref[rs, 2 * RET_QK + RET_V + h * RET_DV:2 * RET_QK + RET_V + (h + 1) * RET_DV] = (
                    gnb[h][1].astype(dp_ref.dtype))
        dgn_ref[...] += dgn

    steps = nc // RET_CPS
    rev = lambda n: steps - 1 - n
    row_v = pl.BlockSpec((RET_CPS * CHUNK, RET_V), lambda n: (rev(n), 0))
    res, rode = _pcall(
        body, [gamma_c, proj, proj, proj, proj, o, dy, states, cos, sin, dmask, xi, zeta,
               gn_w.reshape(1, RET_DV)],
        grid=(steps,),
        in_specs=[pl.BlockSpec(memory_space=pltpu.SMEM)] + _ret_specs(rev) + [
            row_v, row_v, pl.BlockSpec((RET_CPS, RET_HEADS, RET_DK, RET_DV), lambda n: (rev(n), 0, 0, 0))]
        + _rope_specs(rev) + _ret_const_specs(),
        out_specs=[pl.BlockSpec((RET_CPS * CHUNK, RET_IN), lambda n: (rev(n), 0)),
                   pl.BlockSpec((1, RET_DV), lambda n: (0, 0))],
        out_shape=[jax.ShapeDtypeStruct((rows, RET_IN), BF16), jax.ShapeDtypeStruct((1, RET_DV), F32)],
        scratch=[pltpu.VMEM((RET_HEADS, RET_DK, RET_DV), F32)], name="ret_bwd", sem=("arbitrary",), ride=ride)
    return res if ride is None else (res, rode)


GATE_COL = DN_CONV_CH // DN_V
BA_COL = (DN_CONV_CH + DN_V) // LANES
BETA_LANE, DECAY_LANE = 0, DN_HEADS
INV_SHIFT = 4
INV_SQUARINGS = INV_SHIFT - 1
assert CHUNK == 4 << INV_SHIFT


DN_CPS = 2


def _dn_in_specs(order, conv_saved=False):
    rows = DN_CPS * CHUNK
    return [pl.BlockSpec((rows, DN_CONV_CH), lambda n: (order(n), 0)),
            pl.BlockSpec((rows, DN_CONV_CH), lambda n: (order(n), 0)) if conv_saved else
            pl.BlockSpec((8, DN_CONV_CH), lambda n: (jnp.maximum(order(n) * (rows // 8) - 1, 0), 0)),
            pl.BlockSpec((rows, DN_V), lambda n: (order(n), GATE_COL)),
            pl.BlockSpec((rows, LANES), lambda n: (order(n), BA_COL)),
            pl.BlockSpec((CONV_K, 1, DN_CONV_CH), lambda n: (0, 0, 0)),
            pl.BlockSpec((1, LANES), lambda n: (0, 0)),
            pl.BlockSpec((1, LANES), lambda n: (0, 0)),
            pl.BlockSpec((1, DN_DV), lambda n: (0, 0))]


def _dn_front(c, seq, x, halo, ba, cw_ref, al_ref, dt_ref, yc=None):
    valid = _valid_rows(c * CHUNK, CHUNK, seq)
    xin = x * valid
    if yc is None:
        halo = halo * _valid_rows(c * CHUNK - 8, 8, seq)
        yc = xin * cw_ref[CONV_K - 1]
        for k in range(1, CONV_K):
            yc = yc + _shift_down(xin, halo, k) * cw_ref[CONV_K - 1 - k]
    sgc = jax.nn.sigmoid(yc)
    sig = jax.nn.sigmoid(ba)
    beta = sig * valid
    z = ba + dt_ref[...]
    eal = jnp.exp(al_ref[...])
    g = -eal * _softplus(z) * valid
    ri, ci = _iota((CHUNK, CHUNK), 0), _iota((CHUNK, CHUNK), 1)
    lower = (ri >= ci).astype(F32)
    upper = (ri <= ci).astype(F32)
    eye = (ri == ci).astype(F32)
    gam = _nn(lower, g, hi=True)
    gam_t = _tn(g, upper, hi=True)
    return dict(valid=valid, xin=xin, yc=yc, sgc=sgc, act=yc * sgc, sig=sig, beta=beta, z=z,
                eal=eal, g=g, gam=gam, gam_t=gam_t, ri=ri, ci=ci, upper=upper, eye=eye)


def _dn_head(f, h):
    act = f["act"]
    q_raw = act[:, h * DN_DK:(h + 1) * DN_DK]
    k_raw = act[:, DN_QK + h * DN_DK:DN_QK + (h + 1) * DN_DK]
    v = act[:, 2 * DN_QK + h * DN_DV:2 * DN_QK + (h + 1) * DN_DV]
    rq = lax.rsqrt(jnp.sum(q_raw * q_raw, axis=-1, keepdims=True) + RMS_EPS)
    rk = lax.rsqrt(jnp.sum(k_raw * k_raw, axis=-1, keepdims=True) + RMS_EPS)
    qh = q_raw * rq
    kn = k_raw * rk
    gam_c = _col(f["gam"], DECAY_LANE + h)
    gam_r = _row(f["gam_t"], DECAY_LANE + h)
    bc = _col(f["beta"], BETA_LANE + h)
    diff = gam_c - gam_r
    decay = jnp.where(f["ri"] >= f["ci"], jnp.exp(jnp.minimum(diff, 0.0)), 0.0)
    glast = jnp.sum(gam_r * (_iota((1, CHUNK), 1) == CHUNK - 1).astype(F32), axis=1, keepdims=True)
    return dict(rq=rq, rk=rk, qh=qh, qn=qh * (DN_DK ** -0.5), kn=kn, v=v, gam_c=gam_c, gam_r=gam_r,
                bc=bc, diff=diff, decay=decay, egam=jnp.exp(gam_c), glast=glast,
                eglast=jnp.exp(glast), ekd=jnp.exp(glast - gam_c))


def _dn_fwd(proj, conv_w, alog, dtb, norm_w, seq):
    rows = proj.shape[0]
    nc = rows // CHUNK

    def body(x_ref, halo_ref, gate_ref, ba_ref, cw_ref, al_ref, dt_ref, nw_ref,
             o_ref, y_ref, ss_ref, t_ref, yc_ref, s_ref):
        n = pl.program_id(0)

        @pl.when(n == 0)
        def _():
            s_ref[...] = jnp.zeros_like(s_ref)

        nw = nw_ref[...]
        pre = []
        for c in range(DN_CPS):
            rs = slice(c * CHUNK, (c + 1) * CHUNK)
            halo = halo_ref[...] if c == 0 else x_ref[c * CHUNK - 8:c * CHUNK, :]
            f = _dn_front(n * DN_CPS + c, seq, x_ref[rs, :], halo, ba_ref[rs, :], cw_ref, al_ref, dt_ref)
            yc_ref[rs, :] = f["yc"]
            ri, ci = f["ri"], f["ci"]
            eye = f["eye"]
            diag_m = (jnp.right_shift(ri, INV_SHIFT) == jnp.right_shift(ci, INV_SHIFT)).astype(F32)
            half_m = (jnp.right_shift(ri, INV_SHIFT + 1) == jnp.right_shift(ci, INV_SHIFT + 1)).astype(F32)
            heads = [_dn_head(f, h) for h in range(DN_HEADS)]
            a_all = [jnp.where(ri > ci, hd["bc"] * _nt(hd["kn"], hd["kn"]) * hd["decay"], 0.0) for hd in heads]
            b_all = [a * diag_m for a in a_all]
            t_all = [eye - b for b in b_all]
            for _ in range(INV_SQUARINGS):
                b_all = [_nn(b, b, hi=True) for b in b_all]
                t_all = [t + _nn(t, b, hi=True) for t, b in zip(t_all, b_all)]
            for off_m in (half_m - diag_m, 1.0 - half_m):
                x_all = [_nn(a * off_m, t, hi=True) for a, t in zip(a_all, t_all)]
                t_all = [t - _nn(t, x, hi=True) for t, x in zip(t_all, x_all)]
            u_all = [_nn(t, hd["v"] * hd["bc"], hi=True) for t, hd in zip(t_all, heads)]
            w_all = [_nn(t, hd["kn"] * (hd["bc"] * hd["egam"]), hi=True) for t, hd in zip(t_all, heads)]
            qk_all = [_nt(hd["qn"], hd["kn"]) * hd["decay"] for hd in heads]
            for h in range(DN_HEADS):
                t_ref[c, h] = t_all[h]
            pre.append((heads, u_all, w_all, qk_all))
        for c in range(DN_CPS):
            rs = slice(c * CHUNK, (c + 1) * CHUNK)
            heads, u_all, w_all, qk_all = pre[c]
            s_all = [s_ref[h] for h in range(DN_HEADS)]
            os_all = [_nn(hd["qn"] * hd["egam"], s) for hd, s in zip(heads, s_all)]
            vnew_all = [u - _nn(w, s) for u, w, s in zip(u_all, w_all, s_all)]
            o_all = [os + _nn(qk, vn) for os, qk, vn in zip(os_all, qk_all, vnew_all)]
            snew_all = [s * hd["eglast"] + _tn(hd["kn"] * hd["ekd"], vn)
                        for s, hd, vn in zip(s_all, heads, vnew_all)]
            for h in range(DN_HEADS):
                v_cols = slice(h * DN_DV, (h + 1) * DN_DV)
                ss_ref[c, h] = s_all[h]
                s_ref[h] = snew_all[h]
                o_ref[rs, v_cols] = o_all[h]
                y_ref[rs, v_cols] = _gated_norm(o_all[h], gate_ref[rs, v_cols], nw).astype(y_ref.dtype)

    fwd = lambda n: n
    row_v = pl.BlockSpec((DN_CPS * CHUNK, DN_V), lambda n: (n, 0))
    return pl.pallas_call(
        body, grid=(nc // DN_CPS,), in_specs=_dn_in_specs(fwd),
        out_specs=[row_v, row_v,
                   pl.BlockSpec((DN_CPS, DN_HEADS, DN_DK, DN_DV), lambda n: (n, 0, 0, 0)),
                   pl.BlockSpec((DN_CPS, DN_HEADS, CHUNK, CHUNK), lambda n: (n, 0, 0, 0)),
                   pl.BlockSpec((DN_CPS * CHUNK, DN_CONV_CH), lambda n: (n, 0))],
        out_shape=[jax.ShapeDtypeStruct((rows, DN_V), F32), jax.ShapeDtypeStruct((rows, DN_V), BF16),
                   jax.ShapeDtypeStruct((nc, DN_HEADS, DN_DK, DN_DV), F32),
                   jax.ShapeDtypeStruct((nc, DN_HEADS, CHUNK, CHUNK), F32),
                   jax.ShapeDtypeStruct((rows, DN_CONV_CH), F32)],
        scratch_shapes=[pltpu.VMEM((DN_HEADS, DN_DK, DN_DV), F32)],
        name="dn_fwd", compiler_params=_params("arbitrary"))(
            proj, proj, proj, proj, conv_w, alog, dtb, norm_w.reshape(1, DN_DV))


def _dn_bwd(proj, conv_out, o, dy, states, tinv, conv_w, alog, dtb, norm_w, seq):
    rows = proj.shape[0]
    nc = rows // CHUNK

    def body(x_ref, yc_ref, gate_ref, ba_ref, cw_ref, al_ref, dt_ref, nw_ref,
             o_ref, dy_ref, ss_ref, t_ref,
             dp_ref, dcw_ref, dal_ref, ddt_ref, dnw_ref, ds_ref, nxt_ref):
        n = pl.program_id(0)

        @pl.when(n == 0)
        def _():
            ds_ref[...] = jnp.zeros_like(ds_ref)
            nxt_ref[...] = jnp.zeros_like(nxt_ref)
            dcw_ref[...] = jnp.zeros_like(dcw_ref)
            dal_ref[...] = jnp.zeros_like(dal_ref)
            ddt_ref[...] = jnp.zeros_like(ddt_ref)
            dnw_ref[...] = jnp.zeros_like(dnw_ref)

        for c in reversed(range(DN_CPS)):
            rs = pl.ds(c * CHUNK, CHUNK)
            chunk((steps - 1 - n) * DN_CPS + c, x_ref.at[rs], yc_ref.at[rs], gate_ref.at[rs], ba_ref.at[rs],
                  cw_ref, al_ref, dt_ref, nw_ref, o_ref.at[rs], dy_ref.at[rs], ss_ref.at[c], t_ref.at[c],
                  dp_ref.at[rs], dcw_ref, dal_ref, ddt_ref, dnw_ref, ds_ref, nxt_ref)

    def chunk(ch, x_ref, yc_ref, gate_ref, ba_ref, cw_ref, al_ref, dt_ref, nw_ref,
              o_ref, dy_ref, ss_ref, t_ref,
              dp_ref, dcw_ref, dal_ref, ddt_ref, dnw_ref, ds_ref, nxt_ref):
        f = _dn_front(ch, seq, x_ref[...], None, ba_ref[...], cw_ref, al_ref, dt_ref, yc_ref[...])
        ri, ci = f["ri"], f["ci"]
        strict = (ri > ci).astype(F32)
        nw = nw_ref[...]
        lane128 = _iota((1, LANES), 1)
        row128 = _iota((LANES, 1), 0)
        dgam_col = jnp.zeros((CHUNK, LANES), F32)
        dgam_row = jnp.zeros((LANES, CHUNK), F32)
        dbeta = jnp.zeros((CHUNK, LANES), F32)
        dnw = jnp.zeros((1, DN_DV), F32)
        hs = range(DN_HEADS)
        heads = [_dn_head(f, h) for h in hs]
        cols = [slice(h * DN_DV, (h + 1) * DN_DV) for h in hs]
        t_l = [t_ref[h] for h in hs]
        s_l = [ss_ref[h] for h in hs]
        ds_l = [ds_ref[h] for h in hs]
        kk_l = [_nt(hd["kn"], hd["kn"]) for hd in heads]
        p_l = [_nt(hd["qn"], hd["kn"]) for hd in heads]
        rhsw_l = [hd["kn"] * (hd["bc"] * hd["egam"]) for hd in heads]
        u_l = [_nn(t, hd["v"] * hd["bc"], hi=True) for t, hd in zip(t_l, heads)]
        w_l = [_nn(t, r, hi=True) for t, r in zip(t_l, rhsw_l)]
        vnew_l = [u - _nn(w, s) for u, w, s in zip(u_l, w_l, s_l)]
        gnb = [_gated_norm_bwd(dy_ref[:, c], o_ref[:, c], gate_ref[:, c], nw) for c in cols]
        do_l = [x[0] for x in gnb]
        for h in hs:
            dp_ref[:, DN_CONV_CH + h * DN_DV:DN_CONV_CH + (h + 1) * DN_DV] = gnb[h][1].astype(dp_ref.dtype)
            dnw = dnw + gnb[h][2]
        qg_l = [hd["qn"] * hd["egam"] for hd in heads]
        kd_l = [hd["kn"] * hd["ekd"] for hd in heads]
        dvnew_l = [_tn(p * hd["decay"], do) + _nn(kd, ds)
                   for p, hd, do, kd, ds in zip(p_l, heads, do_l, kd_l, ds_l)]
        m_l = [_nt(do, vn) for do, vn in zip(do_l, vnew_l)]
        dqg_l = [_nt(do, s) for do, s in zip(do_l, s_l)]
        dkd_l = [_nt(vn, ds) for vn, ds in zip(vnew_l, ds_l)]
        for h in hs:
            ds_ref[h] = (ds_l[h] * heads[h]["eglast"] + _tn(qg_l[h], do_l[h]) - _tn(w_l[h], dvnew_l[h]))
        dw_l = [-_nt(dvn, s) for dvn, s in zip(dvnew_l, s_l)]
        dru_l = [_tn(t, dvn, hi=True) for t, dvn in zip(t_l, dvnew_l)]
        drw_l = [_tn(t, dw_, hi=True) for t, dw_ in zip(t_l, dw_l)]
        da_l = [-(_nt(dru, u) + _nt(drw, w)) * strict for dru, u, drw, w in zip(dru_l, u_l, drw_l, w_l)]
        dp_l = [m * hd["decay"] for m, hd in zip(m_l, heads)]
        dkk_l = [da * (hd["bc"] * hd["decay"]) for da, hd in zip(da_l, heads)]
        dqn_l = [dqg * hd["egam"] + _nn(dp, hd["kn"]) for dqg, hd, dp in zip(dqg_l, heads, dp_l)]
        dkn_l = [_tn(dp, hd["qn"]) + dkd * hd["ekd"] + drw * (hd["bc"] * hd["egam"])
                 + _nn(dkk, hd["kn"]) + _tn(dkk, hd["kn"])
                 for dp, hd, dkd, drw, dkk in zip(dp_l, heads, dkd_l, drw_l, dkk_l)]
        dq_parts, dk_parts, dv_parts = [], [], []
        for h in hs:
            hd = heads[h]
            kn, v, bc, egam, decay = hd["kn"], hd["v"], hd["bc"], hd["egam"], hd["decay"]
            t1 = jnp.sum(dkd_l[h] * kd_l[h], axis=1, keepdims=True)
            dglast = (jnp.sum(t1, axis=0, keepdims=True)
                      + jnp.sum(jnp.sum(ds_l[h] * s_l[h], axis=1, keepdims=True), axis=0, keepdims=True)
                      * hd["eglast"])
            e = (m_l[h] * p_l[h] + da_l[h] * (bc * kk_l[h])) * decay
            dgc = (jnp.sum(dqg_l[h] * qg_l[h], axis=1, keepdims=True) - t1
                   + jnp.sum(drw_l[h] * rhsw_l[h], axis=1, keepdims=True)
                   + jnp.sum(e, axis=1, keepdims=True)
                   + jnp.where(_iota((CHUNK, 1), 0) == CHUNK - 1, dglast, 0.0))
            dgr = -jnp.sum(e, axis=0, keepdims=True)
            dbc = (jnp.sum(dru_l[h] * v, axis=1, keepdims=True)
                   + jnp.sum(drw_l[h] * kn, axis=1, keepdims=True) * egam
                   + jnp.sum(da_l[h] * kk_l[h] * decay, axis=1, keepdims=True))
            dv_parts.append(dru_l[h] * bc)
            qh, dqn, dkn = hd["qh"], dqn_l[h], dkn_l[h]
            dq_parts.append(((DN_DK ** -0.5) * hd["rq"])
                            * (dqn - qh * jnp.sum(dqn * qh, axis=1, keepdims=True)))
            dk_parts.append(hd["rk"] * (dkn - kn * jnp.sum(dkn * kn, axis=1, keepdims=True)))
            dgam_col = dgam_col + dgc * (lane128 == DECAY_LANE + h).astype(F32)
            dbeta = dbeta + dbc * (lane128 == BETA_LANE + h).astype(F32)
            dgam_row = dgam_row + (row128 == DECAY_LANE + h).astype(F32) * dgr
        dnw_ref[...] += dnw
        dgam = dgam_col + _nt(f["eye"], dgam_row, hi=True)
        dg = _nn(f["upper"], dgam, hi=True)
        d_a = dg * (-f["eal"]) * jax.nn.sigmoid(f["z"]) * f["valid"]
        dal_ref[...] += jnp.sum(dg * f["g"], axis=0, keepdims=True)
        ddt_ref[...] += jnp.sum(d_a, axis=0, keepdims=True)
        d_b = dbeta * f["valid"] * f["sig"] * (1.0 - f["sig"])
        dp_ref[:, DN_CONV_CH + DN_V:DN_CONV_CH + DN_V + LANES] = (d_a + d_b).astype(dp_ref.dtype)
        dp_ref[:, DN_CONV_CH + DN_V + LANES:] = jnp.zeros((CHUNK, DN_IN_PAD - DN_IN_USED), dp_ref.dtype)
        dact = jnp.concatenate(dq_parts + dk_parts + dv_parts, axis=1)
        yc, sgc = f["yc"], f["sgc"]
        dyc = dact * (sgc * (1.0 + yc * (1.0 - sgc)))
        nxt = nxt_ref[...]
        ups = [dyc] + [_shift_up(dyc, nxt, j) for j in range(1, CONV_K)]
        dx = ups[0] * cw_ref[CONV_K - 1]
        for j in range(1, CONV_K):
            dx = dx + ups[j] * cw_ref[CONV_K - 1 - j]
        for j in range(CONV_K):
            dcw_ref[CONV_K - 1 - j] += jnp.sum(f["xin"] * ups[j], axis=0, keepdims=True)
        nxt_ref[...] = dyc[0:8]
        dp_ref[:, :DN_CONV_CH] = (dx * f["valid"]).astype(dp_ref.dtype)

    steps = nc // DN_CPS
    rev = lambda n: steps - 1 - n
    row_v = pl.BlockSpec((DN_CPS * CHUNK, DN_V), lambda n: (rev(n), 0))
    vec = pl.BlockSpec((1, LANES), lambda n: (0, 0))
    return pl.pallas_call(
        body, grid=(steps,),
        in_specs=_dn_in_specs(rev, conv_saved=True) + [
            row_v, row_v,
            pl.BlockSpec((DN_CPS, DN_HEADS, DN_DK, DN_DV), lambda n: (rev(n), 0, 0, 0)),
            pl.BlockSpec((DN_CPS, DN_HEADS, CHUNK, CHUNK), lambda n: (rev(n), 0, 0, 0))],
        out_specs=[pl.BlockSpec((DN_CPS * CHUNK, DN_IN_PAD), lambda n: (rev(n), 0)),
                   pl.BlockSpec((CONV_K, 1, DN_CONV_CH), lambda n: (0, 0, 0)), vec, vec,
                   pl.BlockSpec((1, DN_DV), lambda n: (0, 0))],
        out_shape=[jax.ShapeDtypeStruct((rows, DN_IN_PAD), BF16),
                   jax.ShapeDtypeStruct((CONV_K, 1, DN_CONV_CH), F32),
                   jax.ShapeDtypeStruct((1, LANES), F32), jax.ShapeDtypeStruct((1, LANES), F32),
                   jax.ShapeDtypeStruct((1, DN_DV), F32)],
        scratch_shapes=[pltpu.VMEM((DN_HEADS, DN_DK, DN_DV), F32), pltpu.VMEM((8, DN_CONV_CH), F32)],
        name="dn_bwd", compiler_params=_params("arbitrary"))(
            proj, conv_out, proj, proj, conv_w, alog, dtb, norm_w.reshape(1, DN_DV), o, dy, states, tinv)


def _train_step(x, tgt, wts, sh, idx):
    seq = x.shape[0]
    rows = -(-(seq + CHUNK) // ROW_ALIGN) * ROW_ALIGN
    tail = rows - seq - CHUNK
    h0 = jnp.concatenate([jnp.zeros((PAD, D_MODEL), F32), wts["meta_tokens"].astype(F32), x,
                          jnp.zeros((tail, D_MODEL), F32)], axis=0)
    tgt_p = jnp.concatenate([jnp.zeros((CHUNK, D_MODEL), F32), tgt, jnp.zeros((tail, D_MODEL), F32)],
                            axis=0)
    cos, sin = _rope_tables(rows // CHUNK)
    consts = _ret_consts()
    conv_w = wts["dn_conv_w"].reshape(CONV_K, 1, DN_CONV_CH)
    lane_pad = LANES - 2 * DN_HEADS
    alog = jnp.pad(wts["dn_a_log"].reshape(1, DN_HEADS), ((0, 0), (DECAY_LANE, lane_pad)))
    dtb = jnp.pad(wts["dn_dt_bias"].reshape(1, DN_HEADS), ((0, 0), (DECAY_LANE, lane_pad)))
    g = {}

    wts = dict(wts)
    hn0, (got,) = _rms_fwd(h0, wts["mix_norm_w"][0], "rms_mix0", ride=_Ride("gather", [sh["ret_w_in"]]))
    wts["ret_w_in"] = got.reshape(N_SHARD, D_MODEL, -1)
    proj0, got = _mm_cols(hn0, wts["ret_w_in"], "ret_in",
                          ride=_Ride("gather", [sh["ret_w_out"], sh["ffn_w_gate"]]))
    wts["ret_w_out"] = got[0].reshape(-1, D_MODEL)
    wts["ffn_w_gate"] = got[1]
    (o0, y0, st0), got = _ret_fwd(proj0, cos, sin, consts, wts["ret_gn_w"], seq,
                                  ride=_Ride("gather", [sh["ffn_w_up"], sh["ffn_w_down"]]))
    wts["ffn_w_up"], wts["ffn_w_down"] = got
    h1 = _mm(y0, wts["ret_w_out"], mode="nn", name="ret_out", resid=h0)
    hn1 = _rms_fwd(h1, wts["ffn_norm_w"][0], "rms_ffn0")
    (g0, u0, act0), got = _ffn_up(hn1, wts["ffn_w_gate"], wts["ffn_w_up"], 0, "ffn_up0",
                                  ride=_Ride("gather", [sh["dn_w_in"], sh["dn_w_out"]]))
    n_dn = sh["dn_w_in"].shape[-1]
    dn_shards = got[0].reshape(N_SHARD, D_MODEL, n_dn)
    wts["dn_w_in"] = jnp.concatenate(
        [dn_shards[j] for j in range(N_SHARD)]
        + [jnp.zeros((D_MODEL, DN_IN_PAD - N_SHARD * n_dn), dn_shards.dtype)], axis=-1)
    wts["dn_w_out"] = got[1].reshape(-1, D_MODEL)
    h2 = _ffn_down(act0, wts["ffn_w_down"], h1, 0, "ffn_down0")
    hn2 = _rms_fwd(h2, wts["mix_norm_w"][1], "rms_mix1")
    proj1 = _mm(hn2, wts["dn_w_in"], mode="nn", name="dn_in")
    o1, y1, st1, tinv, conv1 = _dn_fwd(proj1, conv_w, alog, dtb, wts["dn_norm_w"], seq)
    h3 = _mm(y1, wts["dn_w_out"], mode="nn", name="dn_out", resid=h2)
    hn3 = _rms_fwd(h3, wts["ffn_norm_w"][1], "rms_ffn1")
    g1, u1, act1 = _ffn_up(hn3, wts["ffn_w_gate"], wts["ffn_w_up"], 1, "ffn_up1")
    h4 = _ffn_down(act1, wts["ffn_w_down"], h3, 1, "ffn_down1")

    dh4, g["final_norm_w"], loss, dh4b = _final_loss(h4, wts["final_norm_w"], tgt_p, seq, "final_loss")

    layers = wts["ffn_w_gate"].shape[1]

    ffn_names = ["ffn_w_down", "ffn_w_gate", "ffn_w_up"]

    def ffn_bwd(dh_out, dhb_out, h_mid, hn, gg, uu, act, layer, prev, ride=None, last=False):
        tag = str(layer)
        res = _ffn_down_bwd(dhb_out, wts["ffn_w_down"], gg, uu, layer, "ffn_down_bwd" + tag, ride=ride)
        (dg, du), rode = res if ride is not None else (res, None)
        d_down = _ffn_wgrad(act, [dhb_out], layer, layers, prev and prev[:1], True, "ffn_dwd" + tag)
        d_gu = _ffn_wgrad(hn, [dg, du], layer, layers, prev and prev[1:], False, "ffn_dwgu" + tag)
        grads = list(d_down) + list(d_gu)
        gs = rs_grads(ffn_names, grads) if last else None
        res = _ffn_up_bwd(dg, du, wts["ffn_w_gate"], wts["ffn_w_up"], layer, h_mid, wts["ffn_norm_w"][layer],
                          dh_out, "ffn_up_bwd" + tag, ride=_Ride("pair", gs) if last else None)
        (dh_mid, d_norm, dhb_mid), sib = res if last else (res, None)
        return dh_mid, dhb_mid, grads, d_norm, rode, gs, sib

    red = {}

    def rs_grads(names, grads):
        return [gr.reshape((N_SHARD,) + sh[n].shape) for n, gr in zip(names, grads)]

    def rs_partials(names, gs, sib):
        return [_rs_pair_add(gs[t], sib[t], idx, "rs_pair_add_" + n) for t, n in enumerate(names)]

    def rs_end(names, gs, sib, others, tag):
        mine = [_rs_final_add(gs[t], sib[t], others[t], idx, "rs_final_add_" + n) for t, n in enumerate(names)]
        red.update(zip(names, _rs_share(mine, "rs_share" + tag)))

    dh3, dh3b, ffn_grads, dfn1 = ffn_bwd(dh4, dh4b, h3, hn3, g1, u1, act1, 1, None)[:4]
    dy1 = _mm(dh3b, wts["dn_w_out"], mode="nt", name="dn_out_bwd")
    d_dn_out = _mm(y1, dh3b, mode="tn", name="dn_dwo")
    dproj1, dcw, dal, ddt, g["dn_norm_w"] = _dn_bwd(proj1, conv1, o1, dy1, st1, tinv, conv_w, alog, dtb,
                                                    wts["dn_norm_w"], seq)
    d_dn_in = _mm(hn2, dproj1, mode="tn", name="dn_dwi")
    d_dn_in = jnp.stack([d_dn_in[:, j * n_dn:(j + 1) * n_dn] for j in range(N_SHARD)])
    group1 = ["dn_w_out", "dn_w_in"]
    gs1 = rs_grads(group1, [d_dn_out, d_dn_in])
    (dh2, dmn1, dh2b), sib1 = _mm_nt_rms(dproj1, wts["dn_w_in"], h2, wts["mix_norm_w"][1], dh3, "dn_in_bwd",
                                         ride=_Ride("pair", gs1))
    g["dn_conv_w"] = dcw.reshape(CONV_K, DN_CONV_CH)
    g["dn_a_log"] = dal[0, DECAY_LANE:DECAY_LANE + DN_HEADS]
    g["dn_dt_bias"] = ddt[0, DECAY_LANE:DECAY_LANE + DN_HEADS]

    dh1, dh1b, _, dfn0, others1, gs2, sib2 = ffn_bwd(
        dh2, dh2b, h1, hn1, g0, u0, act0, 0, ffn_grads,
        ride=_Ride("chips", rs_partials(group1, gs1, sib1)), last=True)
    rs_end(group1, gs1, sib1, others1, "1")
    d_ret_out = _mm(y0, dh1b, mode="tn", name="ret_dwo")
    gs2b = rs_grads(["ret_w_out"], [d_ret_out])
    dy0, sib2b = _mm(dh1b, wts["ret_w_out"], mode="nt", name="ret_out_bwd", ride=_Ride("pair", gs2b))
    group2 = ffn_names + ["ret_w_out"]
    gs2, sib2 = gs2 + gs2b, list(sib2) + list(sib2b)
    (dproj0, g["ret_gn_w"]), others2 = _ret_bwd(proj0, o0, dy0, st0, cos, sin, consts, wts["ret_gn_w"], seq,
                                                ride=_Ride("chips", rs_partials(group2, gs2, sib2)))
    rs_end(group2, gs2, sib2, others2, "2")
    d_ret_in = _mm_cols_grad(hn0, dproj0, "ret_dwi")
    gs3 = rs_grads(["ret_w_in"], [d_ret_in])
    sib3 = _rs_pair(gs3, "rs_pair3")
    (dh0, dmn0, _), others3 = _mm_cols_t_rms(dproj0, wts["ret_w_in"], h0, wts["mix_norm_w"][0], dh1, "ret_in_bwd",
                                             ride=_Ride("chips", rs_partials(["ret_w_in"], gs3, sib3)))
    rs_end(["ret_w_in"], gs3, sib3, others3, "3")

    g["ffn_norm_w"] = jnp.concatenate([dfn0, dfn1], axis=0)
    g["mix_norm_w"] = jnp.concatenate([dmn0, dmn1], axis=0)
    g["meta_tokens"] = dh0[PAD:CHUNK]
    g["final_norm_w"] = g["final_norm_w"].reshape(D_MODEL)
    g["ret_gn_w"] = g["ret_gn_w"].reshape(RET_DV)
    g["dn_norm_w"] = g["dn_norm_w"].reshape(DN_DV)
    return loss, dh0, g, red


def _mesh_pos():
    return lax.axis_index("x"), lax.axis_index("y"), lax.axis_index("c")


def _other_chips(x, y):
    return [(1 - x, y), (x, 1 - y), (1 - x, 1 - y)]


def _remote(src, dst, send_sem, recv_sem, to):
    return pltpu.make_async_remote_copy(src_ref=src, dst_ref=dst, send_sem=send_sem, recv_sem=recv_sem,
                                        device_id=to, device_id_type=MESH)


GATHER_COPIES = 7


def _gather_phase(phase, ins, outs, send_sems, recv_sems):
    x, y, c = _mesh_pos()
    me = 2 * x + y
    chips = _other_chips(x, y)
    sibling = (x, y, 1 - c)

    def cp(t, k, src, dst, to):
        i = GATHER_COPIES * t + k
        return _remote(src, dst, send_sems.at[i], recv_sems.at[i], to)

    for t in range(len(ins)):
        own = cp(t, 0, ins[t], outs[t].at[me], sibling)
        if phase == 0:
            own.start()
        if phase == 2:
            own.wait()
        for k, (px, py) in enumerate(chips):
            landed = outs[t].at[2 * px + py, c]
            theirs = outs[t].at[2 * px + py, 1 - c]
            to_chip = cp(t, 1 + k, ins[t].at[c], outs[t].at[me, c], (px, py, c))
            if phase == 0:
                to_chip.start()
            if phase == 1:
                cp(t, 1 + k, ins[t].at[c], landed, (px, py, c)).wait_recv()
                cp(t, 4 + k, landed, landed, sibling).start()
            if phase == 2:
                to_chip.wait_send()
                cp(t, 4 + k, landed, landed, sibling).wait_send()
                cp(t, 4 + k, theirs, theirs, sibling).wait_recv()


def _chips_phase(phase, ins, outs, send_sems, recv_sems):
    x, y, c = _mesh_pos()
    for t in range(len(ins)):
        for k, (px, py) in enumerate(_other_chips(x, y)):
            cp = _remote(ins[t].at[2 * px + py], outs[t].at[k], send_sems.at[3 * t + k], recv_sems.at[3 * t + k],
                         (px, py, c))
            if phase == 0:
                cp.start()
            if phase == 2:
                cp.wait()


class _Ride:
    def __init__(self, kind, arrays):
        self.kind, self.arrays = kind, list(arrays)
        nt = len(self.arrays)
        if kind == "gather":
            self.phase_fn, n_sem = _gather_phase, GATHER_COPIES * nt
            self.out_shape = [jax.ShapeDtypeStruct((N_SHARD,) + a.shape, a.dtype) for a in self.arrays]
        elif kind == "pair":
            self.phase_fn, n_sem = _pair_phase, nt
            self.out_shape = [jax.ShapeDtypeStruct(a.shape[:1] + a.shape[2:], a.dtype) for a in self.arrays]
        else:
            self.phase_fn, n_sem = _chips_phase, 3 * nt
            self.out_shape = [jax.ShapeDtypeStruct((3,) + a.shape[1:], a.dtype) for a in self.arrays]
        self.in_specs, self.out_specs = [ANY] * nt, [ANY] * nt
        self.scratch = [pltpu.SemaphoreType.DMA((n_sem,)), pltpu.SemaphoreType.DMA((n_sem,))]

    def emit(self, step, nsteps, ins, outs, send_sems, recv_sems, before):
        mid = max(0, min((7 * nsteps) // 8, nsteps - 2))
        todo = [(0, 0), (1, mid)] if before else [(2, nsteps - 1)]
        for phase, at in todo:
            if phase == 1 and self.kind != "gather":
                continue

            @pl.when(step == at)
            def _(phase=phase):
                self.phase_fn(phase, ins, outs, send_sems, recv_sems)


def _gather_small(blk):
    r, wd = blk.shape

    def body(b_ref, out_ref, send_sems, recv_sems):
        x, y, c = _mesh_pos()
        chips = _other_chips(x, y)
        out_ref[2 * x + y] = b_ref[...]
        sends = [_remote(b_ref, out_ref.at[2 * x + y], send_sems.at[k], recv_sems.at[k], (px, py, c))
                 for k, (px, py) in enumerate(chips)]
        for cp in sends:
            cp.start()
        for k, (px, py) in enumerate(chips):
            _remote(b_ref, out_ref.at[2 * px + py], send_sems.at[k], recv_sems.at[k], (px, py, c)).wait_recv()
        for cp in sends:
            cp.wait_send()

    return pl.pallas_call(
        body, out_shape=jax.ShapeDtypeStruct((4, r, wd), blk.dtype), in_specs=[VMEM_SPEC], out_specs=VMEM_SPEC,
        scratch_shapes=[pltpu.SemaphoreType.DMA((3,)), pltpu.SemaphoreType.DMA((3,))],
        name="gather_small")(blk)


def _allreduce_small(blk):
    r, wd = blk.shape
    rels = [(dx, dy, dc) for dx in (0, 1) for dy in (0, 1) for dc in (0, 1) if dx or dy or dc]

    def body(b_ref, out_ref, buf_ref, send_sems, recv_sems):
        x, y, c = _mesh_pos()

        def peer(rel):
            dx, dy, dc = rel
            return (1 - x if dx else x, 1 - y if dy else y, 1 - c if dc else c)

        me = 4 * x + 2 * y + c
        buf_ref[me] = b_ref[...]
        sends = [_remote(b_ref, buf_ref.at[me], send_sems.at[k], recv_sems.at[k], peer(rel))
                 for k, rel in enumerate(rels)]
        for cp in sends:
            cp.start()
        for k, rel in enumerate(rels):
            px, py, pc = peer(rel)
            _remote(b_ref, buf_ref.at[4 * px + 2 * py + pc], send_sems.at[k], recv_sems.at[k],
                    (px, py, pc)).wait_recv()
        for cp in sends:
            cp.wait_send()
        acc = buf_ref[0]
        for d in range(1, 8):
            acc = acc + buf_ref[d]
        out_ref[...] = acc

    return pl.pallas_call(
        body, out_shape=jax.ShapeDtypeStruct((r, wd), blk.dtype), in_specs=[VMEM_SPEC], out_specs=VMEM_SPEC,
        scratch_shapes=[pltpu.VMEM((8, r, wd), blk.dtype), pltpu.SemaphoreType.DMA((7,)),
                        pltpu.SemaphoreType.DMA((7,))],
        name="allreduce_small")(blk)


def _rs_pair(gs, name):
    ride = _Ride("pair", gs)

    def body(*refs):
        nt = len(gs)
        for phase in (0, 2):
            _pair_phase(phase, refs[:nt], refs[nt:2 * nt], *refs[2 * nt:])

    return pl.pallas_call(body, out_shape=ride.out_shape, in_specs=ride.in_specs, out_specs=ride.out_specs,
                          scratch_shapes=ride.scratch, name=name)(*gs)


def _pair_phase(phase, ins, outs, send_sems, recv_sems):
    x, y, c = _mesh_pos()
    for t in range(len(ins)):
        cp = _remote(ins[t].at[:, 1 - c], outs[t], send_sems.at[t], recv_sems.at[t], (x, y, 1 - c))
        if phase == 0:
            cp.start()
        if phase == 2:
            cp.wait()


def _rs_tile(a, b):
    return _div_tile(a, 512 if b <= 1024 else 256, 16)


def _rs_pair_add(g, a, idx, name):
    _, _, rows, cols = g.shape
    tr = _rs_tile(rows, cols)

    def body(s_ref, g_ref, a_ref, p_ref):
        p_ref[...] = (g_ref[...] + a_ref[...]).astype(p_ref.dtype)

    blk = pl.BlockSpec((None, tr, cols), lambda j, i, s: (j, i, 0))
    spec = pltpu.PrefetchScalarGridSpec(
        num_scalar_prefetch=1, grid=(N_SHARD, rows // tr),
        in_specs=[pl.BlockSpec((None, None, tr, cols), lambda j, i, s: (j, s[0], i, 0)), blk], out_specs=blk)
    return pl.pallas_call(
        body, grid_spec=spec, out_shape=jax.ShapeDtypeStruct((N_SHARD, rows, cols), BF16), name=name,
        compiler_params=_params("parallel", "parallel"))(idx, g, a)


def _rs_final_add(g, a, b, idx, name):
    _, _, rows, cols = g.shape
    tr = _rs_tile(rows, cols)

    def body(s_ref, g_ref, a_ref, b0_ref, b1_ref, b2_ref, f_ref):
        own = g_ref[...] + a_ref[...]
        f_ref[...] = ((own + b0_ref[...].astype(F32)) + b1_ref[...].astype(F32)) + b2_ref[...].astype(F32)

    def b_spec(k):
        return pl.BlockSpec((None, tr, cols), lambda i, s: (k, i, 0))

    spec = pltpu.PrefetchScalarGridSpec(
        num_scalar_prefetch=1, grid=(rows // tr,),
        in_specs=[pl.BlockSpec((None, None, tr, cols), lambda i, s: (s[1], s[0], i, 0)),
                  pl.BlockSpec((None, tr, cols), lambda i, s: (s[1], i, 0)), b_spec(0), b_spec(1), b_spec(2)],
        out_specs=pl.BlockSpec((None, tr, cols), lambda i, s: (s[0], i, 0)))
    return pl.pallas_call(
        body, grid_spec=spec, out_shape=jax.ShapeDtypeStruct((2, rows, cols), F32), name=name,
        compiler_params=_params("parallel"))(idx, g, a, b, b, b)


def _rs_share(fs, name):
    nt = len(fs)

    def body(*refs):
        outs = refs[nt:2 * nt]
        send_sems, recv_sems = refs[2 * nt:]
        x, y, c = _mesh_pos()
        cps = [_remote(outs[t].at[c], outs[t].at[c], send_sems.at[t], recv_sems.at[t], (x, y, 1 - c))
               for t in range(nt)]
        for cp in cps:
            cp.start()
        for cp in cps:
            cp.wait()

    return pl.pallas_call(
        body, out_shape=[jax.ShapeDtypeStruct(f.shape, f.dtype) for f in fs],
        in_specs=[ANY] * nt, out_specs=[ANY] * nt, input_output_aliases={t: t for t in range(nt)},
        scratch_shapes=[pltpu.SemaphoreType.DMA((nt,)), pltpu.SemaphoreType.DMA((nt,))], name=name)(*fs)


def _adamw(w, g, m, v, name):
    lead, rows, cols = w.shape
    tr = rows // 4 if rows % 32 == 0 else rows

    def body(w_ref, g_ref, m_ref, v_ref, go_ref, d_ref, mo_ref, vo_ref):
        gv = g_ref[...]
        go_ref[...] = gv
        mn = ADAM_B1 * m_ref[...] + (1.0 - ADAM_B1) * gv
        vn = ADAM_B2 * v_ref[...] + (1.0 - ADAM_B2) * (gv * gv)
        m_hat = mn / (1.0 - ADAM_B1 ** ADAM_STEP)
        v_hat = vn / (1.0 - ADAM_B2 ** ADAM_STEP)
        d_ref[...] = -ADAM_LR * (m_hat / (jnp.sqrt(v_hat) + ADAM_EPS) + ADAM_WD * w_ref[...])
        mo_ref[...] = mn
        vo_ref[...] = vn

    blk = pl.BlockSpec((None, tr, cols), lambda l, i: (l, i, 0))
    out = jax.ShapeDtypeStruct((lead, rows, cols), F32)
    return pl.pallas_call(
        body, grid=(lead, rows // tr), in_specs=[blk] * 4, out_specs=[blk] * 4, out_shape=[out] * 4, name=name,
        compiler_params=_params("parallel", "parallel"))(w, g, m, v)


BIG = ["ret_w_in", "ret_w_out", "dn_w_in", "dn_w_out", "ffn_w_gate", "ffn_w_up", "ffn_w_down"]
TRANSPOSED_AT_BOUNDARY = {"dn_w_in": True, "ffn_w_gate": False, "ffn_w_up": False}
SMALL =["meta_tokens", "mix_norm_w", "ffn_norm_w", "ret_gn_w", "dn_conv_w", "dn_a_log", "dn_dt_bias",
         "dn_norm_w", "final_norm_w"]
SMALL_SHARDED = {"meta_tokens", "dn_conv_w", "dn_norm_w"}
ORDER = ["meta_tokens", "mix_norm_w", "ffn_norm_w", "ret_w_in", "ret_gn_w", "ret_w_out", "dn_w_in",
         "dn_conv_w", "dn_a_log", "dn_dt_bias", "dn_norm_w", "dn_w_out", "ffn_w_gate", "ffn_w_up",
         "ffn_w_down", "final_norm_w"]


def _halves(a):
    return a.reshape(2, -1, a.shape[-1])


def _pack_lanes(parts, align=8):
    flat = jnp.concatenate([p.reshape(-1) for p in parts])
    flat = jnp.pad(flat, (0, -flat.shape[0] % (align * LANES)))
    return flat.reshape(-1, LANES)


def _unpack(buf, shapes):
    lead = buf.shape[:-2]
    flat = buf.reshape(lead + (-1,))
    out, off = [], 0
    for shp in shapes:
        size = math.prod(shp)
        out.append(flat[..., off:off + size].reshape(lead + tuple(shp)))
        off += size
    return out


def _join_cols(shards):
    return jnp.concatenate([shards[j] for j in range(N_SHARD)], axis=-1)


def kernel(x, meta_tokens, mix_norm_w, ffn_norm_w, ret_w_in, ret_gn_w, ret_w_out, dn_w_in, dn_conv_w, dn_a_log, dn_dt_bias, dn_norm_w, dn_w_out, ffn_w_gate, ffn_w_up, ffn_w_down, final_norm_w, loss_target, m_meta_tokens, m_mix_norm_w, m_ffn_norm_w, m_ret_w_in, m_ret_gn_w, m_ret_w_out, m_dn_w_in, m_dn_conv_w, m_dn_a_log, m_dn_dt_bias, m_dn_norm_w, m_dn_w_out, m_ffn_w_gate, m_ffn_w_up, m_ffn_w_down, m_final_norm_w, v_meta_tokens, v_mix_norm_w, v_ffn_norm_w, v_ret_w_in, v_ret_gn_w, v_ret_w_out, v_dn_w_in, v_dn_conv_w, v_dn_a_log, v_dn_dt_bias, v_dn_norm_w, v_dn_w_out, v_ffn_w_gate, v_ffn_w_up, v_ffn_w_down, v_final_norm_w):
    w = dict(meta_tokens=meta_tokens, mix_norm_w=mix_norm_w, ffn_norm_w=ffn_norm_w, ret_w_in=ret_w_in,
             ret_gn_w=ret_gn_w, ret_w_out=ret_w_out, dn_w_in=dn_w_in, dn_conv_w=dn_conv_w, dn_a_log=dn_a_log,
             dn_dt_bias=dn_dt_bias, dn_norm_w=dn_norm_w, dn_w_out=dn_w_out, ffn_w_gate=ffn_w_gate,
             ffn_w_up=ffn_w_up, ffn_w_down=ffn_w_down, final_norm_w=final_norm_w)
    m = dict(meta_tokens=m_meta_tokens, mix_norm_w=m_mix_norm_w, ffn_norm_w=m_ffn_norm_w, ret_w_in=m_ret_w_in,
             ret_gn_w=m_ret_gn_w, ret_w_out=m_ret_w_out, dn_w_in=m_dn_w_in, dn_conv_w=m_dn_conv_w,
             dn_a_log=m_dn_a_log, dn_dt_bias=m_dn_dt_bias, dn_norm_w=m_dn_norm_w, dn_w_out=m_dn_w_out,
             ffn_w_gate=m_ffn_w_gate, ffn_w_up=m_ffn_w_up, ffn_w_down=m_ffn_w_down, final_norm_w=m_final_norm_w)
    v = dict(meta_tokens=v_meta_tokens, mix_norm_w=v_mix_norm_w, ffn_norm_w=v_ffn_norm_w, ret_w_in=v_ret_w_in,
             ret_gn_w=v_ret_gn_w, ret_w_out=v_ret_w_out, dn_w_in=v_dn_w_in, dn_conv_w=v_dn_conv_w,
             dn_a_log=v_dn_a_log, dn_dt_bias=v_dn_dt_bias, dn_norm_w=v_dn_norm_w, dn_w_out=v_dn_w_out,
             ffn_w_gate=v_ffn_w_gate, ffn_w_up=v_ffn_w_up, ffn_w_down=v_ffn_w_down, final_norm_w=v_final_norm_w)
    mx, my, mc = _mesh_pos()
    chip = 2 * mx + my

    sm_names = [n for n in SMALL if n in SMALL_SHARDED]
    sm_gathered = _unpack(_gather_small(_pack_lanes([w[n] for n in sm_names])), [w[n].shape for n in sm_names])
    full = {n: _join_cols(sm_gathered[i]) for i, n in enumerate(sm_names)}
    wts = {
        "meta_tokens": full["meta_tokens"], "mix_norm_w": mix_norm_w, "ffn_norm_w": ffn_norm_w,
        "ret_gn_w": ret_gn_w[0], "final_norm_w": final_norm_w, "dn_conv_w": full["dn_conv_w"][0],
        "dn_a_log": dn_a_log[0], "dn_dt_bias": dn_dt_bias[0], "dn_norm_w": full["dn_norm_w"][0],
    }
    idx = jnp.stack([mc, chip]).astype(jnp.int32)
    shards = {n: _halves(w[n].astype(MXU_DTYPE)) for n in BIG}
    loss_part, dh0, g, reduced = _train_step(x[0], loss_target[0], wts, shards, idx)
    seq = x.shape[1]
    grad_x = dh0[CHUNK:CHUNK + seq].reshape(x.shape)
    gsh = {}

    small_full_shapes = [g[n].shape for n in SMALL] + [(1,)]
    red = _unpack(_allreduce_small(_pack_lanes([g[n] for n in SMALL] + [loss_part[0, :1]])), small_full_shapes)
    loss = red[-1][0]
    for i, n in enumerate(SMALL):
        gn = red[i]
        if n in SMALL_SHARDED:
            width = w[n].shape[-1]
            gn = lax.dynamic_slice_in_dim(gn, chip * width, width, axis=gn.ndim - 1)
        gsh[n] = gn.reshape(w[n].shape)

    delta, new_m, new_v = {}, {}, {}
    for n in BIG:
        shp = w[n].shape
        if n in TRANSPOSED_AT_BOUNDARY and TRANSPOSED_AT_BOUNDARY[n]:
            view = lambda a: jnp.swapaxes(a, 1, 2).reshape(1, -1, LANES)
            back = lambda a: jnp.swapaxes(a.reshape(shp[0], shp[2], shp[1]), 1, 2)
        elif n in TRANSPOSED_AT_BOUNDARY:
            view = back = lambda a: jnp.swapaxes(a, 1, 2)
        else:
            view = back = lambda a: a
        res = _adamw(view(w[n]), view(reduced[n].reshape(shp)), view(m[n]), view(v[n]), "adamw_" + n)
        gsh[n], delta[n], new_m[n], new_v[n] = [back(r) for r in res]
    sm_local_shapes = [w[n].shape for n in SMALL]
    _, d_, m_, v_ = _adamw(*[_pack_lanes([t[n] for n in SMALL])[None] for t in (w, gsh, m, v)], "adamw_small")
    d_, m_, v_ = d_[0], m_[0], v_[0]
    for n, dd, mm, vv in zip(SMALL, _unpack(d_, sm_local_shapes), _unpack(m_, sm_local_shapes),
                             _unpack(v_, sm_local_shapes)):
        delta[n], new_m[n], new_v[n] = dd, mm, vv

    return (loss, grad_x, *[gsh[n] for n in ORDER], *[delta[n] for n in ORDER],
            *[new_m[n] for n in ORDER], *[new_v[n] for n in ORDER])
```

```python
import functools
import math

import jax
import jax.numpy as jnp
from jax import lax
from jax.experimental import pallas as pl
from jax.experimental.pallas import tpu as pltpu

F32 = jnp.float32
BF16 = jnp.bfloat16
MXU_DTYPE = BF16

D_MODEL = 1024
N_META = 16
CHUNK = 64
PAD = CHUNK - N_META
RMS_EPS = 1e-6
RET_HEADS, RET_DK, RET_DV = 4, 256, 512
RET_QK, RET_V = RET_HEADS * RET_DK, RET_HEADS * RET_DV
RET_IN = 2 * RET_QK + 2 * RET_V
ROPE_BASE = 10000.0
DN_HEADS, DN_DK, DN_DV = 8, 128, 256
DN_QK, DN_V = DN_HEADS * DN_DK, DN_HEADS * DN_DV
DN_CONV_CH = 2 * DN_QK + DN_V
DN_IN = DN_CONV_CH + DN_V + 2 * DN_HEADS
LANES = 128
DN_IN_USED = DN_CONV_CH + DN_V + LANES
DN_IN_PAD = DN_IN_USED + LANES
CONV_K = 4
FFN_HIDDEN = 2816
ADAM_LR, ADAM_B1, ADAM_B2, ADAM_EPS, ADAM_WD, ADAM_STEP = 0.001, 0.9, 0.999, 1e-08, 0.01, 10

ROW_ALIGN = 256
VMEM_LIMIT = 56 * 1024 * 1024
MESH = pl.DeviceIdType.MESH
ANY = pl.BlockSpec(memory_space=pl.ANY)
VMEM_SPEC = pl.BlockSpec(memory_space=pltpu.VMEM)
_HI = lax.Precision.HIGHEST


def _params(*sem):
    return pltpu.CompilerParams(dimension_semantics=sem, vmem_limit_bytes=VMEM_LIMIT)


def _dg(a, b, ca, cb, hi):
    dims = (((ca,), (cb,)), ((), ()))

    def dot(p, q):
        return lax.dot_general(p, q, dims, preferred_element_type=F32)

    if not hi:
        return dot(a.astype(MXU_DTYPE), b.astype(MXU_DTYPE))
    if MXU_DTYPE == F32:
        return lax.dot_general(a, b, dims, precision=_HI, preferred_element_type=F32)
    a_hi, b_hi = a.astype(MXU_DTYPE), b.astype(MXU_DTYPE)
    a_lo = (a - a_hi.astype(F32)).astype(MXU_DTYPE)
    b_lo = (b - b_hi.astype(F32)).astype(MXU_DTYPE)
    return dot(a_hi, b_hi) + (dot(a_hi, b_lo) + dot(a_lo, b_hi))


def _nn(a, b, hi=False):
    return _dg(a, b, 1, 0, hi)


def _nt(a, b, hi=False):
    return _dg(a, b, 1, 1, hi)


def _tn(a, b, hi=False):
    return _dg(a, b, 0, 0, hi)


def _iota(shape, dim):
    return lax.broadcasted_iota(jnp.int32, shape, dim)


def _valid_rows(first_row, rows, seq):
    r = first_row + _iota((rows, 1), 0)
    return ((r >= PAD) & (r < CHUNK + seq)).astype(F32)


def _rope(t, cs, sn):
    half = t.shape[-1] // 2
    t1, t2 = t[:, :half], t[:, half:]
    return jnp.concatenate([t1 * cs - t2 * sn, t1 * sn + t2 * cs], axis=1)


def _rope_bwd(d, cs, sn):
    half = d.shape[-1] // 2
    d1, d2 = d[:, :half], d[:, half:]
    return jnp.concatenate([d1 * cs + d2 * sn, d2 * cs - d1 * sn], axis=1)


def _col(x, idx):
    oh = (_iota((1, x.shape[1]), 1) == idx).astype(F32)
    return jnp.sum(x * oh, axis=1, keepdims=True)


def _row(x, idx):
    oh = (_iota((x.shape[0], 1), 0) == idx).astype(F32)
    return jnp.sum(x * oh, axis=0, keepdims=True)


def _shift_down(x, halo8, k):
    xr = pltpu.roll(x, k, 0)
    hr = pltpu.roll(halo8, k, 0)
    first = jnp.where(_iota((8, 1), 0) < k, hr, xr[0:8])
    return jnp.concatenate([first, xr[8:]], axis=0)


def _shift_up(x, next8, j):
    rows = x.shape[0]
    xr = pltpu.roll(x, rows - j, 0)
    nr = pltpu.roll(next8, 8 - j, 0)
    last = jnp.where(_iota((8, 1), 0) >= 8 - j, nr, xr[rows - 8:])
    return jnp.concatenate([xr[:rows - 8], last], axis=0)


def _gated_norm(o, gate, w):
    r = lax.rsqrt(jnp.mean(o * o, axis=-1, keepdims=True) + RMS_EPS)
    return o * r * w * (gate * jax.nn.sigmoid(gate))


def _gated_norm_bwd(dy, o, gate, w):
    r = lax.rsqrt(jnp.mean(o * o, axis=-1, keepdims=True) + RMS_EPS)
    nrm = o * r
    sg = jax.nn.sigmoid(gate)
    sl = gate * sg
    dgate = dy * nrm * w * (sg * (1.0 + gate * (1.0 - sg)))
    dn = dy * w * sl
    dw = jnp.sum(dy * nrm * sl, axis=0, keepdims=True)
    do = r * (dn - nrm * jnp.mean(dn * nrm, axis=-1, keepdims=True))
    return do, dgate, dw


def _softplus(z):
    return jnp.maximum(z, 0.0) + jnp.log(1.0 + jnp.exp(-jnp.abs(z)))


def _row_tile(rows, cap=768):
    for t in (768, 512, 256, 128, 64, 32, 16, 8):
        if t <= cap and rows % t == 0:
            return t
    return rows


TILE_BUDGET = 44 * 1024 * 1024


def _fit_rows(rows, row_bytes, fixed_bytes, value_row_bytes):
    best = None
    for t in range(LANES, rows + 1, LANES):
        if rows % t == 0 and 2 * (row_bytes * t + fixed_bytes) + value_row_bytes * t <= TILE_BUDGET:
            best = t
    return best or _row_tile(rows, 256)


def _div_tile(n, cap, mult):
    best = None
    for t in range(mult, min(cap, n) + 1, mult):
        if n % t == 0:
            best = t
    return best or n


def _col_tile(cols, cap=1536):
    best = None
    for t in range(LANES, min(cap, cols) + 1, LANES):
        if cols % t == 0:
            best = t
    return best or cols


def _rms_fwd(h, w, name, ride=None):
    rows, d = h.shape
    tm = _row_tile(rows)

    def body(h_ref, w_ref, o_ref):
        x = h_ref[...]
        r = lax.rsqrt(jnp.mean(x * x, axis=-1, keepdims=True) + RMS_EPS)
        o_ref[...] = (x * r * w_ref[...]).astype(o_ref.dtype)

    res, rode = _pcall(body, [h, w.reshape(1, d)], grid=(rows // tm,),
                       in_specs=[pl.BlockSpec((tm, d), lambda i: (i, 0)), pl.BlockSpec((1, d), lambda i: (0, 0))],
                       out_specs=[pl.BlockSpec((tm, d), lambda i: (i, 0))],
                       out_shape=[jax.ShapeDtypeStruct((rows, d), BF16)], name=name, sem=("parallel",), ride=ride)
    return res[0] if ride is None else (res[0], rode)


def _gmm_rms(name, grid, args, in_specs, row_spec, fn, h, w, resid, row_axis, red_axis=None, ride=None):
    m, d = h.shape
    n_in = len(args)
    vec = pl.BlockSpec((1, d), lambda *g: (0, 0))

    def body(*refs):
        ins = refs[:n_in]
        h_ref, w_ref, r_ref, dh_ref, dw_ref, dh16_ref = refs[n_in:]
        part = fn(*ins)
        row = pl.program_id(row_axis)

        def finish(dy):
            x = h_ref[...]
            r = lax.rsqrt(jnp.mean(x * x, axis=-1, keepdims=True) + RMS_EPS)
            xh = x * r
            dxh = dy * w_ref[...]
            dh = r_ref[...] + r * (dxh - xh * jnp.mean(dxh * xh, axis=-1, keepdims=True))
            dh_ref[...] = dh
            dh16_ref[...] = dh.astype(dh16_ref.dtype)
            dwp = jnp.sum(dy * xh, axis=0, keepdims=True)

            @pl.when(row == 0)
            def _():
                dw_ref[...] = dwp

            @pl.when(row > 0)
            def _():
                dw_ref[...] += dwp

        if red_axis is None:
            finish(part)
            return
        k = pl.program_id(red_axis)

        @pl.when(k == 0)
        def _():
            dh_ref[...] = part

        @pl.when(k > 0)
        def _():
            dh_ref[...] += part

        @pl.when(k == grid[red_axis] - 1)
        def _():
            finish(dh_ref[...])

    res, rode = _pcall(body, list(args) + [h, w.reshape(1, d), resid], grid=grid,
                       in_specs=list(in_specs) + [row_spec, vec, row_spec], out_specs=[row_spec, vec, row_spec],
                       out_shape=[jax.ShapeDtypeStruct((m, d), F32), jax.ShapeDtypeStruct((1, d), F32),
                                  jax.ShapeDtypeStruct((m, d), BF16)],
                       name=name, sem=("arbitrary",) * len(grid), ride=ride)
    return res if ride is None else (res, rode)


def _final_loss(h, w, tgt, seq, name):
    rows, d = h.shape
    tm = _row_tile(rows)

    def body(h_ref, w_ref, t_ref, dh_ref, dw_ref, loss_ref, dh16_ref):
        i = pl.program_id(0)
        r_idx = i * tm + _iota((tm, 1), 0)
        m = ((r_idx >= CHUNK) & (r_idx < CHUNK + seq)).astype(F32)
        x = h_ref[...]
        wv = w_ref[...]
        r = lax.rsqrt(jnp.mean(x * x, axis=-1, keepdims=True) + RMS_EPS)
        xh = x * r
        err = (xh * wv - t_ref[...]) * m
        lpart = 0.5 * jnp.sum(jnp.mean(err * err, axis=-1, keepdims=True), axis=0, keepdims=True)
        dyv = err * (1.0 / d)
        dxh = dyv * wv
        dh = r * (dxh - xh * jnp.mean(dxh * xh, axis=-1, keepdims=True))
        dh_ref[...] = dh
        dh16_ref[...] = dh.astype(dh16_ref.dtype)
        part = jnp.sum(dyv * xh, axis=0, keepdims=True)

        @pl.when(i == 0)
        def _():
            dw_ref[...] = part
            loss_ref[...] = jnp.broadcast_to(lpart, loss_ref.shape)

        @pl.when(i > 0)
        def _():
            dw_ref[...] += part
            loss_ref[...] += jnp.broadcast_to(lpart, loss_ref.shape)

    blk = pl.BlockSpec((tm, d), lambda i: (i, 0))
    vec = pl.BlockSpec((1, d), lambda i: (0, 0))
    return pl.pallas_call(
        body, grid=(rows // tm,), in_specs=[blk, vec, blk],
        out_specs=[blk, vec, pl.BlockSpec((1, LANES), lambda i: (0, 0)), blk],
        out_shape=[jax.ShapeDtypeStruct((rows, d), F32), jax.ShapeDtypeStruct((1, d), F32),
                   jax.ShapeDtypeStruct((1, LANES), F32), jax.ShapeDtypeStruct((rows, d), BF16)],
        name=name, compiler_params=_params("arbitrary"))(h, w.reshape(1, d), tgt)


def _isz(x):
    return jnp.dtype(x.dtype).itemsize


def _mm(a, b, *, mode, name, out_dtype=F32, resid=None, col_cap=1536, ride=None):
    if mode == "tn":
        m, k = a.shape
        n = b.shape[1]
        tn = _col_tile(n, col_cap)
        tm = _fit_rows(m, k * _isz(a) + tn * _isz(b), (3 * k * tn * 4) // 2, 2 * (k + tn))

        def body_tn(a_ref, b_ref, o_ref):
            i = pl.program_id(1)
            part = _tn(a_ref[...], b_ref[...])

            @pl.when(i == 0)
            def _():
                o_ref[...] = part

            @pl.when(i > 0)
            def _():
                o_ref[...] += part

        return pl.pallas_call(
            body_tn, grid=(n // tn, m // tm),
            in_specs=[pl.BlockSpec((tm, k), lambda j, i: (i, 0)),
                      pl.BlockSpec((tm, tn), lambda j, i: (i, j))],
            out_specs=pl.BlockSpec((k, tn), lambda j, i: (0, j)),
            out_shape=jax.ShapeDtypeStruct((k, n), F32), name=name,
            compiler_params=_params("parallel", "arbitrary"))(a, b)

    m, ka = a.shape
    n = b.shape[1] if mode == "nn" else b.shape[0]
    has_resid = resid is not None
    tn = _col_tile(n, col_cap)
    tm = _fit_rows(m, ka * _isz(a) + tn * (jnp.dtype(out_dtype).itemsize + (4 if has_resid else 0)),
                   ka * tn * _isz(b), 2 * ka + 8 * tn)

    def body(*refs):
        if has_resid:
            a_ref, b_ref, r_ref, o_ref = refs
        else:
            a_ref, b_ref, o_ref = refs
        acc = _nn(a_ref[...], b_ref[...]) if mode == "nn" else _nt(a_ref[...], b_ref[...])
        if has_resid:
            acc = acc + r_ref[...]
        o_ref[...] = acc.astype(o_ref.dtype)

    b_spec = (pl.BlockSpec((b.shape[0], tn), lambda j, i: (0, j)) if mode == "nn"
              else pl.BlockSpec((tn, b.shape[1]), lambda j, i: (j, 0)))
    o_spec = pl.BlockSpec((tm, tn), lambda j, i: (i, j))
    in_specs = [pl.BlockSpec((tm, ka), lambda j, i: (i, 0)), b_spec]
    args = [a, b]
    if has_resid:
        in_specs.append(o_spec)
        args.append(resid)
    res, rode = _pcall(body, args, grid=(n // tn, m // tm), in_specs=in_specs, out_specs=[o_spec],
                       out_shape=[jax.ShapeDtypeStruct((m, n), out_dtype)], name=name,
                       sem=("parallel", "parallel"), ride=ride)
    return res[0] if ride is None else (res[0], rode)


N_SHARD = 4


def _gmm(name, grid, args, in_specs, out_specs, out_shape, fn, red_axis=None, init_arg=None, aliases=None,
         ride=None):
    n_in = len(args)
    single = not isinstance(out_shape, (list, tuple))
    out_specs = [out_specs] if single else list(out_specs)
    out_shape = [out_shape] if single else list(out_shape)

    def body(*refs):
        _gmm_step(fn, refs[:n_in], refs[n_in:], red_axis, init_arg)

    sem = tuple("arbitrary" if ax == red_axis else "parallel" for ax in range(len(grid)))
    res, rode = _pcall(body, args, grid=grid, in_specs=in_specs, out_specs=out_specs, out_shape=out_shape,
                       name=name, sem=sem, aliases=aliases, ride=ride)
    ours = res[0] if single else res
    return ours if ride is None else (ours, rode)


def _gmm_step(fn, ins, outs, red_axis, init_arg):
    parts = fn(*ins)
    if red_axis is None:
        for o_ref, p in zip(outs, parts):
            o_ref[...] = p.astype(o_ref.dtype)
        return
    k = pl.program_id(red_axis)

    @pl.when(k == 0)
    def _():
        for idx, (o_ref, p) in enumerate(zip(outs, parts)):
            o_ref[...] = p + ins[init_arg][...] if (idx == 0 and init_arg is not None) else p

    @pl.when(k > 0)
    def _():
        for o_ref, p in zip(outs, parts):
            o_ref[...] += p


def _ride_body(ride, grid, n_in, n_out, n_scratch, body):
    n_rin, n_rout = len(ride.arrays), len(ride.out_shape)
    nsteps = math.prod(grid)

    def wrapped(*refs):
        ins = refs[:n_in]
        r_ins = refs[n_in:n_in + n_rin]
        o0 = n_in + n_rin
        outs = refs[o0:o0 + n_out]
        r_outs = refs[o0 + n_out:o0 + n_out + n_rout]
        s0 = o0 + n_out + n_rout
        scratch = refs[s0:s0 + n_scratch]
        send_sems, recv_sems = refs[-2:]
        step = pl.program_id(0)
        for ax in range(1, len(grid)):
            step = step * grid[ax] + pl.program_id(ax)
        ride.emit(step, nsteps, r_ins, r_outs, send_sems, recv_sems, before=True)
        body(*ins, *outs, *scratch)
        ride.emit(step, nsteps, r_ins, r_outs, send_sems, recv_sems, before=False)

    return wrapped


def _pcall(body, args, *, grid, in_specs, out_specs, out_shape, name, sem, scratch=(), aliases=None, ride=None):
    if ride is None:
        res = pl.pallas_call(body, grid=grid, in_specs=list(in_specs), out_specs=list(out_specs),
                             out_shape=list(out_shape), scratch_shapes=list(scratch), name=name,
                             input_output_aliases=aliases or {}, compiler_params=_params(*sem))(*args)
        return res, None
    n_in, n_out = len(args), len(out_shape)
    res = pl.pallas_call(
        _ride_body(ride, grid, n_in, n_out, len(scratch), body), grid=grid,
        in_specs=list(in_specs) + ride.in_specs, out_specs=list(out_specs) + ride.out_specs,
        out_shape=list(out_shape) + ride.out_shape, scratch_shapes=list(scratch) + ride.scratch, name=name,
        input_output_aliases=aliases or {},
        compiler_params=_params(*(("arbitrary",) * len(grid))))(*args, *ride.arrays)
    return res[:n_out], res[n_out:]


def _mm_cols(a, ws, name, ride=None):
    m, k = a.shape
    n = ws.shape[2]
    tm = _fit_rows(m, k * _isz(a) + n * 4, k * n * _isz(ws), 4 * n)
    return _gmm(name, (N_SHARD, m // tm), [a, ws],
                [pl.BlockSpec((tm, k), lambda j, i: (i, 0)), pl.BlockSpec((None, k, n), lambda j, i: (j, 0, 0))],
                pl.BlockSpec((tm, n), lambda j, i: (i, j)), jax.ShapeDtypeStruct((m, N_SHARD * n), F32),
                lambda a_ref, w_ref: (_nn(a_ref[...], w_ref[...]),), ride=ride)


def _mm_cols_t_rms(d, ws, h, w, resid, name, ride=None):
    m = d.shape[0]
    _, k, n = ws.shape
    tm = _fit_rows(m, n * _isz(d) + 3 * k * 4, k * n * _isz(ws), 16 * k)
    return _gmm_rms(name, (m // tm, N_SHARD), [d, ws],
                    [pl.BlockSpec((tm, n), lambda i, j: (i, j)), pl.BlockSpec((None, k, n), lambda i, j: (j, 0, 0))],
                    pl.BlockSpec((tm, k), lambda i, j: (i, 0)),
                    lambda d_ref, w_ref: _nt(d_ref[...], w_ref[...]), h, w, resid, 0, red_axis=1, ride=ride)


def _mm_nt_rms(a, b, h, w, resid, name, ride=None):
    m, n = a.shape
    k = b.shape[0]
    tm = _fit_rows(m, n * _isz(a) + 3 * k * 4, k * n * _isz(b), 16 * k)
    return _gmm_rms(name, (m // tm,), [a, b],
                    [pl.BlockSpec((tm, n), lambda i: (i, 0)), pl.BlockSpec((k, n), lambda i: (0, 0))],
                    pl.BlockSpec((tm, k), lambda i: (i, 0)),
                    lambda a_ref, b_ref: _nt(a_ref[...], b_ref[...]), h, w, resid, 0, ride=ride)


def _mm_cols_grad(a, d, name):
    m, k = a.shape
    n = d.shape[1] // N_SHARD
    tm = _fit_rows(m, k * _isz(a) + n * _isz(d), (3 * k * n * 4) // 2, 2 * (k + n))
    return _gmm(name, (N_SHARD, m // tm), [a, d],
                [pl.BlockSpec((tm, k), lambda j, i: (i, 0)), pl.BlockSpec((tm, n), lambda j, i: (i, j))],
                pl.BlockSpec((None, k, n), lambda j, i: (j, 0, 0)), jax.ShapeDtypeStruct((N_SHARD, k, n), F32),
                lambda a_ref, d_ref: (_tn(a_ref[...], d_ref[...]),), red_axis=1)


def _ffn_up(hn, wg, wu, layer, name, ride=None):
    m, k = hn.shape
    n = wg.shape[3]
    tm = _fit_rows(m, k * _isz(hn) + 3 * n * jnp.dtype(BF16).itemsize, 2 * k * n * _isz(wg), 16 * n)

    def fn(a_ref, wg_ref, wu_ref):
        a = a_ref[...]
        g = _nn(a, wg_ref[...])
        u = _nn(a, wu_ref[...])
        return g, u, g * jax.nn.sigmoid(g) * u

    w_spec = pl.BlockSpec((None, None, k, n), lambda j, i: (j, layer, 0, 0))
    o_spec = pl.BlockSpec((None, tm, n), lambda j, i: (j, i, 0))
    out = jax.ShapeDtypeStruct((N_SHARD, m, n), BF16)
    return _gmm(name, (N_SHARD, m // tm), [hn, wg, wu],
                [pl.BlockSpec((tm, k), lambda j, i: (i, 0)), w_spec, w_spec],
                [o_spec, o_spec, o_spec], [out, out, out], fn, ride=ride)


def _ffn_down(act, wd, resid, layer, name, ride=None):
    _, m, n = act.shape
    d = wd.shape[3]
    tm = _fit_rows(m, N_SHARD * n * _isz(act) + 2 * d * 4, N_SHARD * n * d * _isz(wd), 8 * d)

    def fn(a_ref, w_ref, r_ref):
        acc = r_ref[...]
        for j in range(N_SHARD):
            acc = acc + _nn(a_ref[j], w_ref[j])
        return (acc,)

    row = pl.BlockSpec((tm, d), lambda i: (i, 0))
    return _gmm(name, (m // tm,), [act, wd, resid],
                [pl.BlockSpec((N_SHARD, tm, n), lambda i: (0, i, 0)),
                 pl.BlockSpec((N_SHARD, None, n, d), lambda i: (0, layer, 0, 0)), row],
                row, jax.ShapeDtypeStruct((m, d), F32), fn, ride=ride)


def _ffn_down_bwd(dh, wd, g, u, layer, name, ride=None):
    m, d = dh.shape
    n = wd.shape[2]
    tm = _fit_rows(m, d * _isz(dh) + 4 * N_SHARD * n * jnp.dtype(BF16).itemsize, N_SHARD * n * d * _isz(wd),
                   2 * d + 24 * n)

    def body(dh_ref, wd_ref, g_ref, u_ref, dg_ref, du_ref):
        dhv = dh_ref[...].astype(MXU_DTYPE)
        for j in range(N_SHARD):
            dact = _nt(dhv, wd_ref[j])
            gv = g_ref[j].astype(F32)
            sg = jax.nn.sigmoid(gv)
            gs = gv * sg
            dg_ref[j] = (dact * u_ref[j].astype(F32) * (sg + gs * (1.0 - sg))).astype(dg_ref.dtype)
            du_ref[j] = (dact * gs).astype(du_ref.dtype)

    sh_spec = pl.BlockSpec((N_SHARD, tm, n), lambda i: (0, i, 0))
    out = jax.ShapeDtypeStruct((N_SHARD, m, n), BF16)
    res, rode = _pcall(body, [dh, wd, g, u], grid=(m // tm,),
                       in_specs=[pl.BlockSpec((tm, d), lambda i: (i, 0)),
                                 pl.BlockSpec((N_SHARD, None, n, d), lambda i: (0, layer, 0, 0)), sh_spec, sh_spec],
                       out_specs=[sh_spec, sh_spec], out_shape=[out, out], name=name, sem=("parallel",), ride=ride)
    return res if ride is None else (res, rode)


def _ffn_up_bwd(dg, du, wg, wu, layer, h, w, resid, name, ride=None):
    _, m, n = dg.shape
    k = wg.shape[2]
    tm = _fit_rows(m, 2 * N_SHARD * n * _isz(dg) + 3 * k * 4, 2 * N_SHARD * k * n * _isz(wg), 16 * k)

    def fn(dg_ref, du_ref, wg_ref, wu_ref):
        acc = _nt(dg_ref[0], wg_ref[0]) + _nt(du_ref[0], wu_ref[0])
        for j in range(1, N_SHARD):
            acc = acc + _nt(dg_ref[j], wg_ref[j]) + _nt(du_ref[j], wu_ref[j])
        return acc

    d_spec = pl.BlockSpec((N_SHARD, tm, n), lambda i: (0, i, 0))
    w_spec = pl.BlockSpec((N_SHARD, None, k, n), lambda i: (0, layer, 0, 0))
    return _gmm_rms(name, (m // tm,), [dg, du, wg, wu], [d_spec, d_spec, w_spec, w_spec],
                    pl.BlockSpec((tm, k), lambda i: (i, 0)), fn, h, w, resid, 0, ride=ride)


def _ffn_wgrad(lhs, rhs_list, layer, layers, prev, lhs_sharded, name):
    if lhs_sharded:
        _, m, k = lhs.shape
        n = rhs_list[0].shape[1]
    else:
        m, k = lhs.shape
        n = rhs_list[0].shape[2]
    n_out = len(rhs_list)
    tm = _fit_rows(m, k * _isz(lhs) + n_out * n * _isz(rhs_list[0]), (3 * n_out * k * n * 4) // 2,
                   2 * (k + n_out * n))
    sh = pl.BlockSpec((None, tm, k if lhs_sharded else n), lambda j, i: (j, i, 0))
    fl = pl.BlockSpec((tm, n if lhs_sharded else k), lambda j, i: (i, 0))
    n_out = len(rhs_list)
    args = [lhs] + list(rhs_list)
    in_specs = [sh if lhs_sharded else fl] + [fl if lhs_sharded else sh] * n_out
    aliases = None
    if prev is not None:
        aliases = {len(args) + t: t for t in range(n_out)}
        args = args + list(prev)
        in_specs = in_specs + [ANY] * n_out

    def fn(l_ref, *rest):
        lv = l_ref[...]
        return tuple(_tn(lv, r_ref[...]) for r_ref in rest[:n_out])

    o_spec = pl.BlockSpec((None, None, k, n), lambda j, i: (j, layer, 0, 0))
    out = jax.ShapeDtypeStruct((N_SHARD, layers, k, n), F32)
    return _gmm(name, (N_SHARD, m // tm), args, in_specs, [o_spec] * n_out, [out] * n_out, fn,
                red_axis=1, aliases=aliases)


def _ret_consts():
    log_gamma = jnp.log1p(-jnp.exp2(-5.0 - jnp.arange(RET_HEADS, dtype=F32)))
    idx = jnp.arange(CHUNK, dtype=F32)
    rel = idx[:, None] - idx[None, :]
    dmask = jnp.where((rel >= 0)[None], jnp.exp(log_gamma[:, None, None] * jnp.maximum(rel, 0.0)), 0.0)
    xi = jnp.exp(log_gamma[:, None] * (idx[None, :] + 1.0))[:, :, None]
    zeta = jnp.exp(log_gamma[:, None] * (CHUNK - 1.0 - idx[None, :]))[:, :, None]
    gamma_c = jnp.exp(log_gamma * CHUNK)
    wide = (RET_HEADS, CHUNK, RET_DK)
    return dmask, jnp.broadcast_to(xi, wide), jnp.broadcast_to(zeta, wide), gamma_c


def _rope_tables(nc):
    half = RET_DK // 2
    inv_freq = ROPE_BASE ** (-jnp.arange(half, dtype=F32) / half)
    a_chunk = (jnp.arange(nc) * CHUNK - PAD).astype(F32)[:, None] * inv_freq[None, :]
    a_row = jnp.arange(CHUNK).astype(F32)[:, None] * inv_freq[None, :]
    return (jnp.stack([jnp.cos(a_chunk), jnp.sin(a_chunk)], axis=1),
            jnp.stack([jnp.cos(a_row), jnp.sin(a_row)], axis=0))


RET_CPS = 4


def _rope_chunk(rc_ref, rr_ref, c):
    cc, sc = rc_ref[c, 0:1, :], rc_ref[c, 1:2, :]
    cr, sr = rr_ref[0], rr_ref[1]
    return cc * cr - sc * sr, sc * cr + cc * sr


def _rope_specs(order):
    half = RET_DK // 2
    return [pl.BlockSpec((RET_CPS, 2, half), lambda n: (order(n), 0, 0)),
            pl.BlockSpec((2, CHUNK, half), lambda n: (0, 0, 0))]


def _ret_specs(order):
    rows = RET_CPS * CHUNK
    return [pl.BlockSpec((rows, RET_QK), lambda n: (order(n), 0)),
            pl.BlockSpec((rows, RET_QK), lambda n: (order(n), 1)),
            pl.BlockSpec((rows, RET_V), lambda n: (order(n), 1)),
            pl.BlockSpec((rows, RET_V), lambda n: (order(n), 2))]


def _ret_const_specs():
    return [pl.BlockSpec((RET_HEADS, CHUNK, CHUNK), lambda n: (0, 0, 0)),
            pl.BlockSpec((RET_HEADS, CHUNK, RET_DK), lambda n: (0, 0, 0)),
            pl.BlockSpec((RET_HEADS, CHUNK, RET_DK), lambda n: (0, 0, 0)),
            pl.BlockSpec((1, RET_DV), lambda n: (0, 0))]


def _ret_fwd(proj, cos, sin, consts, gn_w, seq, ride=None):
    rows = proj.shape[0]
    nc = rows // CHUNK
    dmask, xi, zeta, gamma_c = consts

    def body(gam_ref, q_ref, k_ref, v_ref, g_ref, cos_ref, sin_ref, dm_ref, xi_ref, ze_ref, gn_ref,
             o_ref, y_ref, ss_ref, s_ref):
        n = pl.program_id(0)

        @pl.when(n == 0)
        def _():
            s_ref[...] = jnp.zeros_like(s_ref)

        gn = gn_ref[...]
        hs = range(RET_HEADS)
        qk_cols = [slice(h * RET_DK, (h + 1) * RET_DK) for h in hs]
        v_cols = [slice(h * RET_DV, (h + 1) * RET_DV) for h in hs]
        for c in range(RET_CPS):
            rs = slice(c * CHUNK, (c + 1) * CHUNK)
            cs, sn = _rope_chunk(cos_ref, sin_ref, c)
            kscale = _valid_rows((n * RET_CPS + c) * CHUNK, CHUNK, seq) * (RET_DK ** -0.5)
            qr_l = [_rope(q_ref[rs, col], cs, sn) for col in qk_cols]
            kr_l = [_rope(k_ref[rs, col], cs, sn) * kscale for col in qk_cols]
            v_l = [v_ref[rs, col] for col in v_cols]
            s_l = [s_ref[h] for h in hs]
            sc_l = [_nt(qr, kr) * dm_ref[h] for h, (qr, kr) in enumerate(zip(qr_l, kr_l))]
            o_l = [_nn(sc_l[h], v_l[h]) + _nn(qr_l[h] * xi_ref[h], s_l[h]) for h in hs]
            for h in hs:
                ss_ref[c, h] = s_l[h].astype(ss_ref.dtype)
                s_ref[h] = gam_ref[h] * s_l[h] + _tn(kr_l[h] * ze_ref[h], v_l[h])
                o_ref[rs, v_cols[h]] = o_l[h]
                y_ref[rs, v_cols[h]] = _gated_norm(o_l[h], g_ref[rs, v_cols[h]], gn).astype(y_ref.dtype)

    fwd = lambda n: n
    row_v = pl.BlockSpec((RET_CPS * CHUNK, RET_V), lambda n: (n, 0))
    res, rode = _pcall(
        body, [gamma_c, proj, proj, proj, proj, cos, sin, dmask, xi, zeta, gn_w.reshape(1, RET_DV)],
        grid=(nc // RET_CPS,),
        in_specs=[pl.BlockSpec(memory_space=pltpu.SMEM)] + _ret_specs(fwd) + _rope_specs(fwd)
        + _ret_const_specs(),
        out_specs=[row_v, row_v,
                   pl.BlockSpec((RET_CPS, RET_HEADS, RET_DK, RET_DV), lambda n: (n, 0, 0, 0))],
        out_shape=[jax.ShapeDtypeStruct((rows, RET_V), F32), jax.ShapeDtypeStruct((rows, RET_V), BF16),
                   jax.ShapeDtypeStruct((nc, RET_HEADS, RET_DK, RET_DV), BF16)],
        scratch=[pltpu.VMEM((RET_HEADS, RET_DK, RET_DV), F32)], name="ret_fwd", sem=("arbitrary",), ride=ride)
    return res if ride is None else (res, rode)


def _ret_bwd(proj, o, dy, states, cos, sin, consts, gn_w, seq, ride=None):
    rows = proj.shape[0]
    nc = rows // CHUNK
    dmask, xi, zeta, gamma_c = consts

    def body(gam_ref, q_ref, k_ref, v_ref, g_ref, o_ref, dy_ref, ss_ref, cos_ref, sin_ref,
             dm_ref, xi_ref, ze_ref, gn_ref, dp_ref, dgn_ref, ds_ref):
        n = pl.program_id(0)

        @pl.when(n == 0)
        def _():
            ds_ref[...] = jnp.zeros_like(ds_ref)
            dgn_ref[...] = jnp.zeros_like(dgn_ref)

        gn = gn_ref[...]
        dgn = jnp.zeros((1, RET_DV), F32)
        hs = range(RET_HEADS)
        qk_cols = [slice(h * RET_DK, (h + 1) * RET_DK) for h in hs]
        v_cols = [slice(h * RET_DV, (h + 1) * RET_DV) for h in hs]
        for c in reversed(range(RET_CPS)):
            rs = slice(c * CHUNK, (c + 1) * CHUNK)
            cs, sn = _rope_chunk(cos_ref, sin_ref, c)
            kscale = _valid_rows(((steps - 1 - n) * RET_CPS + c) * CHUNK, CHUNK, seq) * (RET_DK ** -0.5)
            qr_l = [_rope(q_ref[rs, col], cs, sn) for col in qk_cols]
            kr_l = [_rope(k_ref[rs, col], cs, sn) * kscale for col in qk_cols]
            v_l = [v_ref[rs, col] for col in v_cols]
            s_l = [ss_ref[c, h] for h in hs]
            ds_l = [ds_ref[h] for h in hs]
            sc_l = [_nt(qr_l[h], kr_l[h]) * dm_ref[h] for h in hs]
            gnb = [_gated_norm_bwd(dy_ref[rs, col], o_ref[rs, col], g_ref[rs, col], gn) for col in v_cols]
            do_l = [x[0] for x in gnb]
            dsc_l = [_nt(do_l[h], v_l[h]) * dm_ref[h] for h in hs]
            dv_l = [_tn(sc_l[h], do_l[h]) + _nn(kr_l[h] * ze_ref[h], ds_l[h]) for h in hs]
            dqr_l = [_nn(dsc_l[h], kr_l[h]) + _nt(do_l[h], s_l[h]) * xi_ref[h] for h in hs]
            dkr_l = [_tn(dsc_l[h], qr_l[h]) + _nt(v_l[h], ds_l[h]) * ze_ref[h] for h in hs]
            for h in hs:
                dgn = dgn + gnb[h][2]
                ds_ref[h] = gam_ref[h] * ds_l[h] + _tn(qr_l[h] * xi_ref[h], do_l[h])
                dp_ref[rs, qk_cols[h]] = _rope_bwd(dqr_l[h], cs, sn).astype(dp_ref.dtype)
                dp_ref[rs, RET_QK + h * RET_DK:RET_QK + (h + 1) * RET_DK] = (
                    _rope_bwd(dkr_l[h] * kscale, cs, sn).astype(dp_ref.dtype))
                dp_ref[rs, 2 * RET_QK + h * RET_DV:2 * RET_QK + (h + 1) * RET_DV] = dv_l[h].astype(dp_ref.dtype)
                dp_ref[rs, 2 * RET_QK + RET_V + h * RET_DV:2 * RET_QK + RET_V + (h + 1) * RET_DV] = (
                    gnb[h][1].astype(dp_ref.dtype))
        dgn_ref[...] += dgn

    steps = nc // RET_CPS
    rev = lambda n: steps - 1 - n
    row_v = pl.BlockSpec((RET_CPS * CHUNK, RET_V), lambda n: (rev(n), 0))
    res, rode = _pcall(
        body, [gamma_c, proj, proj, proj, proj, o, dy, states, cos, sin, dmask, xi, zeta,
               gn_w.reshape(1, RET_DV)],
        grid=(steps,),
        in_specs=[pl.BlockSpec(memory_space=pltpu.SMEM)] + _ret_specs(rev) + [
            row_v, row_v, pl.BlockSpec((RET_CPS, RET_HEADS, RET_DK, RET_DV), lambda n: (rev(n), 0, 0, 0))]
        + _rope_specs(rev) + _ret_const_specs(),
        out_specs=[pl.BlockSpec((RET_CPS * CHUNK, RET_IN), lambda n: (rev(n), 0)),
                   pl.BlockSpec((1, RET_DV), lambda n: (0, 0))],
        out_shape=[jax.ShapeDtypeStruct((rows, RET_IN), BF16), jax.ShapeDtypeStruct((1, RET_DV), F32)],
        scratch=[pltpu.VMEM((RET_HEADS, RET_DK, RET_DV), F32)], name="ret_bwd", sem=("arbitrary",), ride=ride)
    return res if ride is None else (res, rode)


GATE_COL = DN_CONV_CH // DN_V
BA_COL = (DN_CONV_CH + DN_V) // LANES
BETA_LANE, DECAY_LANE = 0, DN_HEADS
INV_SHIFT = 4
INV_SQUARINGS = INV_SHIFT - 1
assert CHUNK == 4 << INV_SHIFT


DN_CPS = 2


def _dn_in_specs(order, conv_saved=False):
    rows = DN_CPS * CHUNK
    return [pl.BlockSpec((rows, DN_CONV_CH), lambda n: (order(n), 0)),
            pl.BlockSpec((rows, DN_CONV_CH), lambda n: (order(n), 0)) if conv_saved else
            pl.BlockSpec((8, DN_CONV_CH), lambda n: (jnp.maximum(order(n) * (rows // 8) - 1, 0), 0)),
            pl.BlockSpec((rows, DN_V), lambda n: (order(n), GATE_COL)),
            pl.BlockSpec((rows, LANES), lambda n: (order(n), BA_COL)),
            pl.BlockSpec((CONV_K, 1, DN_CONV_CH), lambda n: (0, 0, 0)),
            pl.BlockSpec((1, LANES), lambda n: (0, 0)),
            pl.BlockSpec((1, LANES), lambda n: (0, 0)),
            pl.BlockSpec((1, DN_DV), lambda n: (0, 0))]


def _dn_front(c, seq, x, halo, ba, cw_ref, al_ref, dt_ref, yc=None):
    valid = _valid_rows(c * CHUNK, CHUNK, seq)
    xin = x * valid
    if yc is None:
        halo = halo * _valid_rows(c * CHUNK - 8, 8, seq)
        yc = xin * cw_ref[CONV_K - 1]
        for k in range(1, CONV_K):
            yc = yc + _shift_down(xin, halo, k) * cw_ref[CONV_K - 1 - k]
    sgc = jax.nn.sigmoid(yc)
    sig = jax.nn.sigmoid(ba)
    beta = sig * valid
    z = ba + dt_ref[...]
    eal = jnp.exp(al_ref[...])
    g = -eal * _softplus(z) * valid
    ri, ci = _iota((CHUNK, CHUNK), 0), _iota((CHUNK, CHUNK), 1)
    lower = (ri >= ci).astype(F32)
    upper = (ri <= ci).astype(F32)
    eye = (ri == ci).astype(F32)
    gam = _nn(lower, g, hi=True)
    gam_t = _tn(g, upper, hi=True)
    return dict(valid=valid, xin=xin, yc=yc, sgc=sgc, act=yc * sgc, sig=sig, beta=beta, z=z,
                eal=eal, g=g, gam=gam, gam_t=gam_t, ri=ri, ci=ci, upper=upper, eye=eye)


def _dn_head(f, h):
    act = f["act"]
    q_raw = act[:, h * DN_DK:(h + 1) * DN_DK]
    k_raw = act[:, DN_QK + h * DN_DK:DN_QK + (h + 1) * DN_DK]
    v = act[:, 2 * DN_QK + h * DN_DV:2 * DN_QK + (h + 1) * DN_DV]
    rq = lax.rsqrt(jnp.sum(q_raw * q_raw, axis=-1, keepdims=True) + RMS_EPS)
    rk = lax.rsqrt(jnp.sum(k_raw * k_raw, axis=-1, keepdims=True) + RMS_EPS)
    qh = q_raw * rq
    kn = k_raw * rk
    gam_c = _col(f["gam"], DECAY_LANE + h)
    gam_r = _row(f["gam_t"], DECAY_LANE + h)
    bc = _col(f["beta"], BETA_LANE + h)
    diff = gam_c - gam_r
    decay = jnp.where(f["ri"] >= f["ci"], jnp.exp(jnp.minimum(diff, 0.0)), 0.0)
    glast = jnp.sum(gam_r * (_iota((1, CHUNK), 1) == CHUNK - 1).astype(F32), axis=1, keepdims=True)
    return dict(rq=rq, rk=rk, qh=qh, qn=qh * (DN_DK ** -0.5), kn=kn, v=v, gam_c=gam_c, gam_r=gam_r,
                bc=bc, diff=diff, decay=decay, egam=jnp.exp(gam_c), glast=glast,
                eglast=jnp.exp(glast), ekd=jnp.exp(glast - gam_c))


def _dn_fwd(proj, conv_w, alog, dtb, norm_w, seq):
    rows = proj.shape[0]
    nc = rows // CHUNK

    def body(x_ref, halo_ref, gate_ref, ba_ref, cw_ref, al_ref, dt_ref, nw_ref,
             o_ref, y_ref, ss_ref, t_ref, yc_ref, s_ref):
        n = pl.program_id(0)

        @pl.when(n == 0)
        def _():
            s_ref[...] = jnp.zeros_like(s_ref)

        nw = nw_ref[...]
        pre = []
        for c in range(DN_CPS):
            rs = slice(c * CHUNK, (c + 1) * CHUNK)
            halo = halo_ref[...] if c == 0 else x_ref[c * CHUNK - 8:c * CHUNK, :]
            f = _dn_front(n * DN_CPS + c, seq, x_ref[rs, :], halo, ba_ref[rs, :], cw_ref, al_ref, dt_ref)
            yc_ref[rs, :] = f["yc"]
            ri, ci = f["ri"], f["ci"]
            eye = f["eye"]
            diag_m = (jnp.right_shift(ri, INV_SHIFT) == jnp.right_shift(ci, INV_SHIFT)).astype(F32)
            half_m = (jnp.right_shift(ri, INV_SHIFT + 1) == jnp.right_shift(ci, INV_SHIFT + 1)).astype(F32)
            heads = [_dn_head(f, h) for h in range(DN_HEADS)]
            a_all = [jnp.where(ri > ci, hd["bc"] * _nt(hd["kn"], hd["kn"]) * hd["decay"], 0.0) for hd in heads]
            b_all = [a * diag_m for a in a_all]
            t_all = [eye - b for b in b_all]
            for _ in range(INV_SQUARINGS):
                b_all = [_nn(b, b, hi=True) for b in b_all]
                t_all = [t + _nn(t, b, hi=True) for t, b in zip(t_all, b_all)]
            for off_m in (half_m - diag_m, 1.0 - half_m):
                x_all = [_nn(a * off_m, t, hi=True) for a, t in zip(a_all, t_all)]
                t_all = [t - _nn(t, x, hi=True) for t, x in zip(t_all, x_all)]
            u_all = [_nn(t, hd["v"] * hd["bc"], hi=True) for t, hd in zip(t_all, heads)]
            w_all = [_nn(t, hd["kn"] * (hd["bc"] * hd["egam"]), hi=True) for t, hd in zip(t_all, heads)]
            qk_all = [_nt(hd["qn"], hd["kn"]) * hd["decay"] for hd in heads]
            for h in range(DN_HEADS):
                t_ref[c, h] = t_all[h]
            pre.append((heads, u_all, w_all, qk_all))
        for c in range(DN_CPS):
            rs = slice(c * CHUNK, (c + 1) * CHUNK)
            heads, u_all, w_all, qk_all = pre[c]
            s_all = [s_ref[h] for h in range(DN_HEADS)]
            os_all = [_nn(hd["qn"] * hd["egam"], s) for hd, s in zip(heads, s_all)]
            vnew_all = [u - _nn(w, s) for u, w, s in zip(u_all, w_all, s_all)]
            o_all = [os + _nn(qk, vn) for os, qk, vn in zip(os_all, qk_all, vnew_all)]
            snew_all = [s * hd["eglast"] + _tn(hd["kn"] * hd["ekd"], vn)
                        for s, hd, vn in zip(s_all, heads, vnew_all)]
            for h in range(DN_HEADS):
                v_cols = slice(h * DN_DV, (h + 1) * DN_DV)
                ss_ref[c, h] = s_all[h]
                s_ref[h] = snew_all[h]
                o_ref[rs, v_cols] = o_all[h]
                y_ref[rs, v_cols] = _gated_norm(o_all[h], gate_ref[rs, v_cols], nw).astype(y_ref.dtype)

    fwd = lambda n: n
    row_v = pl.BlockSpec((DN_CPS * CHUNK, DN_V), lambda n: (n, 0))
    return pl.pallas_call(
        body, grid=(nc // DN_CPS,), in_specs=_dn_in_specs(fwd),
        out_specs=[row_v, row_v,
                   pl.BlockSpec((DN_CPS, DN_HEADS, DN_DK, DN_DV), lambda n: (n, 0, 0, 0)),
                   pl.BlockSpec((DN_CPS, DN_HEADS, CHUNK, CHUNK), lambda n: (n, 0, 0, 0)),
                   pl.BlockSpec((DN_CPS * CHUNK, DN_CONV_CH), lambda n: (n, 0))],
        out_shape=[jax.ShapeDtypeStruct((rows, DN_V), F32), jax.ShapeDtypeStruct((rows, DN_V), BF16),
                   jax.ShapeDtypeStruct((nc, DN_HEADS, DN_DK, DN_DV), F32),
                   jax.ShapeDtypeStruct((nc, DN_HEADS, CHUNK, CHUNK), F32),
                   jax.ShapeDtypeStruct((rows, DN_CONV_CH), F32)],
        scratch_shapes=[pltpu.VMEM((DN_HEADS, DN_DK, DN_DV), F32)],
        name="dn_fwd", compiler_params=_params("arbitrary"))(
            proj, proj, proj, proj, conv_w, alog, dtb, norm_w.reshape(1, DN_DV))


def _dn_bwd(proj, conv_out, o, dy, states, tinv, conv_w, alog, dtb, norm_w, seq):
    rows = proj.shape[0]
    nc = rows // CHUNK

    def body(x_ref, yc_ref, gate_ref, ba_ref, cw_ref, al_ref, dt_ref, nw_ref,
             o_ref, dy_ref, ss_ref, t_ref,
             dp_ref, dcw_ref, dal_ref, ddt_ref, dnw_ref, ds_ref, nxt_ref):
        n = pl.program_id(0)

        @pl.when(n == 0)
        def _():
            ds_ref[...] = jnp.zeros_like(ds_ref)
            nxt_ref[...] = jnp.zeros_like(nxt_ref)
            dcw_ref[...] = jnp.zeros_like(dcw_ref)
            dal_ref[...] = jnp.zeros_like(dal_ref)
            ddt_ref[...] = jnp.zeros_like(ddt_ref)
            dnw_ref[...] = jnp.zeros_like(dnw_ref)

        for c in reversed(range(DN_CPS)):
            rs = pl.ds(c * CHUNK, CHUNK)
            chunk((steps - 1 - n) * DN_CPS + c, x_ref.at[rs], yc_ref.at[rs], gate_ref.at[rs], ba_ref.at[rs],
                  cw_ref, al_ref, dt_ref, nw_ref, o_ref.at[rs], dy_ref.at[rs], ss_ref.at[c], t_ref.at[c],
                  dp_ref.at[rs], dcw_ref, dal_ref, ddt_ref, dnw_ref, ds_ref, nxt_ref)

    def chunk(ch, x_ref, yc_ref, gate_ref, ba_ref, cw_ref, al_ref, dt_ref, nw_ref,
              o_ref, dy_ref, ss_ref, t_ref,
              dp_ref, dcw_ref, dal_ref, ddt_ref, dnw_ref, ds_ref, nxt_ref):
        f = _dn_front(ch, seq, x_ref[...], None, ba_ref[...], cw_ref, al_ref, dt_ref, yc_ref[...])
        ri, ci = f["ri"], f["ci"]
        strict = (ri > ci).astype(F32)
        nw = nw_ref[...]
        lane128 = _iota((1, LANES), 1)
        row128 = _iota((LANES, 1), 0)
        dgam_col = jnp.zeros((CHUNK, LANES), F32)
        dgam_row = jnp.zeros((LANES, CHUNK), F32)
        dbeta = jnp.zeros((CHUNK, LANES), F32)
        dnw = jnp.zeros((1, DN_DV), F32)
        hs = range(DN_HEADS)
        heads = [_dn_head(f, h) for h in hs]
        cols = [slice(h * DN_DV, (h + 1) * DN_DV) for h in hs]
        t_l = [t_ref[h] for h in hs]
        s_l = [ss_ref[h] for h in hs]
        ds_l = [ds_ref[h] for h in hs]
        kk_l = [_nt(hd["kn"], hd["kn"]) for hd in heads]
        p_l = [_nt(hd["qn"], hd["kn"]) for hd in heads]
        rhsw_l = [hd["kn"] * (hd["bc"] * hd["egam"]) for hd in heads]
        u_l = [_nn(t, hd["v"] * hd["bc"], hi=True) for t, hd in zip(t_l, heads)]
        w_l = [_nn(t, r, hi=True) for t, r in zip(t_l, rhsw_l)]
        vnew_l = [u - _nn(w, s) for u, w, s in zip(u_l, w_l, s_l)]
        gnb = [_gated_norm_bwd(dy_ref[:, c], o_ref[:, c], gate_ref[:, c], nw) for c in cols]
        do_l = [x[0] for x in gnb]
        for h in hs:
            dp_ref[:, DN_CONV_CH + h * DN_DV:DN_CONV_CH + (h + 1) * DN_DV] = gnb[h][1].astype(dp_ref.dtype)
            dnw = dnw + gnb[h][2]
        qg_l = [hd["qn"] * hd["egam"] for hd in heads]
        kd_l = [hd["kn"] * hd["ekd"] for hd in heads]
        dvnew_l = [_tn(p * hd["decay"], do) + _nn(kd, ds)
                   for p, hd, do, kd, ds in zip(p_l, heads, do_l, kd_l, ds_l)]
        m_l = [_nt(do, vn) for do, vn in zip(do_l, vnew_l)]
        dqg_l = [_nt(do, s) for do, s in zip(do_l, s_l)]
        dkd_l = [_nt(vn, ds) for vn, ds in zip(vnew_l, ds_l)]
        for h in hs:
            ds_ref[h] = (ds_l[h] * heads[h]["eglast"] + _tn(qg_l[h], do_l[h]) - _tn(w_l[h], dvnew_l[h]))
        dw_l = [-_nt(dvn, s) for dvn, s in zip(dvnew_l, s_l)]
        dru_l = [_tn(t, dvn, hi=True) for t, dvn in zip(t_l, dvnew_l)]
        drw_l = [_tn(t, dw_, hi=True) for t, dw_ in zip(t_l, dw_l)]
        da_l = [-(_nt(dru, u) + _nt(drw, w)) * strict for dru, u, drw, w in zip(dru_l, u_l, drw_l, w_l)]
        dp_l = [m * hd["decay"] for m, hd in zip(m_l, heads)]
        dkk_l = [da * (hd["bc"] * hd["decay"]) for da, hd in zip(da_l, heads)]
        dqn_l = [dqg * hd["egam"] + _nn(dp, hd["kn"]) for dqg, hd, dp in zip(dqg_l, heads, dp_l)]
        dkn_l = [_tn(dp, hd["qn"]) + dkd * hd["ekd"] + drw * (hd["bc"] * hd["egam"])
                 + _nn(dkk, hd["kn"]) + _tn(dkk, hd["kn"])
                 for dp, hd, dkd, drw, dkk in zip(dp_l, heads, dkd_l, drw_l, dkk_l)]
        dq_parts, dk_parts, dv_parts = [], [], []
        for h in hs:
            hd = heads[h]
            kn, v, bc, egam, decay = hd["kn"], hd["v"], hd["bc"], hd["egam"], hd["decay"]
            t1 = jnp.sum(dkd_l[h] * kd_l[h], axis=1, keepdims=True)
            dglast = (jnp.sum(t1, axis=0, keepdims=True)
                      + jnp.sum(jnp.sum(ds_l[h] * s_l[h], axis=1, keepdims=True), axis=0, keepdims=True)
                      * hd["eglast"])
            e = (m_l[h] * p_l[h] + da_l[h] * (bc * kk_l[h])) * decay
            dgc = (jnp.sum(dqg_l[h] * qg_l[h], axis=1, keepdims=True) - t1
                   + jnp.sum(drw_l[h] * rhsw_l[h], axis=1, keepdims=True)
                   + jnp.sum(e, axis=1, keepdims=True)
                   + jnp.where(_iota((CHUNK, 1), 0) == CHUNK - 1, dglast, 0.0))
            dgr = -jnp.sum(e, axis=0, keepdims=True)
            dbc = (jnp.sum(dru_l[h] * v, axis=1, keepdims=True)
                   + jnp.sum(drw_l[h] * kn, axis=1, keepdims=True) * egam
                   + jnp.sum(da_l[h] * kk_l[h] * decay, axis=1, keepdims=True))
            dv_parts.append(dru_l[h] * bc)
            qh, dqn, dkn = hd["qh"], dqn_l[h], dkn_l[h]
            dq_parts.append(((DN_DK ** -0.5) * hd["rq"])
                            * (dqn - qh * jnp.sum(dqn * qh, axis=1, keepdims=True)))
            dk_parts.append(hd["rk"] * (dkn - kn * jnp.sum(dkn * kn, axis=1, keepdims=True)))
            dgam_col = dgam_col + dgc * (lane128 == DECAY_LANE + h).astype(F32)
            dbeta = dbeta + dbc * (lane128 == BETA_LANE + h).astype(F32)
            dgam_row = dgam_row + (row128 == DECAY_LANE + h).astype(F32) * dgr
        dnw_ref[...] += dnw
        dgam = dgam_col + _nt(f["eye"], dgam_row, hi=True)
        dg = _nn(f["upper"], dgam, hi=True)
        d_a = dg * (-f["eal"]) * jax.nn.sigmoid(f["z"]) * f["valid"]
        dal_ref[...] += jnp.sum(dg * f["g"], axis=0, keepdims=True)
        ddt_ref[...] += jnp.sum(d_a, axis=0, keepdims=True)
        d_b = dbeta * f["valid"] * f["sig"] * (1.0 - f["sig"])
        dp_ref[:, DN_CONV_CH + DN_V:DN_CONV_CH + DN_V + LANES] = (d_a + d_b).astype(dp_ref.dtype)
        dp_ref[:, DN_CONV_CH + DN_V + LANES:] = jnp.zeros((CHUNK, DN_IN_PAD - DN_IN_USED), dp_ref.dtype)
        dact = jnp.concatenate(dq_parts + dk_parts + dv_parts, axis=1)
        yc, sgc = f["yc"], f["sgc"]
        dyc = dact * (sgc * (1.0 + yc * (1.0 - sgc)))
        nxt = nxt_ref[...]
        ups = [dyc] + [_shift_up(dyc, nxt, j) for j in range(1, CONV_K)]
        dx = ups[0] * cw_ref[CONV_K - 1]
        for j in range(1, CONV_K):
            dx = dx + ups[j] * cw_ref[CONV_K - 1 - j]
        for j in range(CONV_K):
            dcw_ref[CONV_K - 1 - j] += jnp.sum(f["xin"] * ups[j], axis=0, keepdims=True)
        nxt_ref[...] = dyc[0:8]
        dp_ref[:, :DN_CONV_CH] = (dx * f["valid"]).astype(dp_ref.dtype)

    steps = nc // DN_CPS
    rev = lambda n: steps - 1 - n
    row_v = pl.BlockSpec((DN_CPS * CHUNK, DN_V), lambda n: (rev(n), 0))
    vec = pl.BlockSpec((1, LANES), lambda n: (0, 0))
    return pl.pallas_call(
        body, grid=(steps,),
        in_specs=_dn_in_specs(rev, conv_saved=True) + [
            row_v, row_v,
            pl.BlockSpec((DN_CPS, DN_HEADS, DN_DK, DN_DV), lambda n: (rev(n), 0, 0, 0)),
            pl.BlockSpec((DN_CPS, DN_HEADS, CHUNK, CHUNK), lambda n: (rev(n), 0, 0, 0))],
        out_specs=[pl.BlockSpec((DN_CPS * CHUNK, DN_IN_PAD), lambda n: (rev(n), 0)),
                   pl.BlockSpec((CONV_K, 1, DN_CONV_CH), lambda n: (0, 0, 0)), vec, vec,
                   pl.BlockSpec((1, DN_DV), lambda n: (0, 0))],
        out_shape=[jax.ShapeDtypeStruct((rows, DN_IN_PAD), BF16),
                   jax.ShapeDtypeStruct((CONV_K, 1, DN_CONV_CH), F32),
                   jax.ShapeDtypeStruct((1, LANES), F32), jax.ShapeDtypeStruct((1, LANES), F32),
                   jax.ShapeDtypeStruct((1, DN_DV), F32)],
        scratch_shapes=[pltpu.VMEM((DN_HEADS, DN_DK, DN_DV), F32), pltpu.VMEM((8, DN_CONV_CH), F32)],
        name="dn_bwd", compiler_params=_params("arbitrary"))(
            proj, conv_out, proj, proj, conv_w, alog, dtb, norm_w.reshape(1, DN_DV), o, dy, states, tinv)


def _train_step(x, tgt, wts, sh, idx):
    seq = x.shape[0]
    rows = -(-(seq + CHUNK) // ROW_ALIGN) * ROW_ALIGN
    tail = rows - seq - CHUNK
    h0 = jnp.concatenate([jnp.zeros((PAD, D_MODEL), F32), wts["meta_tokens"].astype(F32), x,
                          jnp.zeros((tail, D_MODEL), F32)], axis=0)
    tgt_p = jnp.concatenate([jnp.zeros((CHUNK, D_MODEL), F32), tgt, jnp.zeros((tail, D_MODEL), F32)],
                            axis=0)
    cos, sin = _rope_tables(rows // CHUNK)
    consts = _ret_consts()
    conv_w = wts["dn_conv_w"].reshape(CONV_K, 1, DN_CONV_CH)
    lane_pad = LANES - 2 * DN_HEADS
    alog = jnp.pad(wts["dn_a_log"].reshape(1, DN_HEADS), ((0, 0), (DECAY_LANE, lane_pad)))
    dtb = jnp.pad(wts["dn_dt_bias"].reshape(1, DN_HEADS), ((0, 0), (DECAY_LANE, lane_pad)))
    g = {}

    wts = dict(wts)
    hn0, (got,) = _rms_fwd(h0, wts["mix_norm_w"][0], "rms_mix0", ride=_Ride("gather", [sh["ret_w_in"]]))
    wts["ret_w_in"] = got.reshape(N_SHARD, D_MODEL, -1)
    proj0, got = _mm_cols(hn0, wts["ret_w_in"], "ret_in",
                          ride=_Ride("gather", [sh["ret_w_out"], sh["ffn_w_gate"]]))
    wts["ret_w_out"] = got[0].reshape(-1, D_MODEL)
    wts["ffn_w_gate"] = got[1]
    (o0, y0, st0), got = _ret_fwd(proj0, cos, sin, consts, wts["ret_gn_w"], seq,
                                  ride=_Ride("gather", [sh["ffn_w_up"]]))
    wts["ffn_w_up"] = got[0]
    h1 = _mm(y0, wts["ret_w_out"], mode="nn", name="ret_out", resid=h0)
    hn1 = _rms_fwd(h1, wts["ffn_norm_w"][0], "rms_ffn0")
    (g0, u0, act0), got = _ffn_up(hn1, wts["ffn_w_gate"], wts["ffn_w_up"], 0, "ffn_up0",
                                  ride=_Ride("gather", [sh["ffn_w_down"], sh["dn_w_out"]]))
    wts["ffn_w_down"] = got[0]
    wts["dn_w_out"] = got[1].reshape(-1, D_MODEL)
    h2, got = _ffn_down(act0, wts["ffn_w_down"], h1, 0, "ffn_down0", ride=_Ride("gather", [sh["dn_w_in"]]))
    n_dn = sh["dn_w_in"].shape[-1]
    dn_shards = got[0].reshape(N_SHARD, D_MODEL, n_dn)
    wts["dn_w_in"] = jnp.concatenate(
        [dn_shards[j] for j in range(N_SHARD)]
        + [jnp.zeros((D_MODEL, DN_IN_PAD - N_SHARD * n_dn), dn_shards.dtype)], axis=-1)
    hn2 = _rms_fwd(h2, wts["mix_norm_w"][1], "rms_mix1")
    proj1 = _mm(hn2, wts["dn_w_in"], mode="nn", name="dn_in")
    o1, y1, st1, tinv, conv1 = _dn_fwd(proj1, conv_w, alog, dtb, wts["dn_norm_w"], seq)
    h3 = _mm(y1, wts["dn_w_out"], mode="nn", name="dn_out", resid=h2)
    hn3 = _rms_fwd(h3, wts["ffn_norm_w"][1], "rms_ffn1")
    g1, u1, act1 = _ffn_up(hn3, wts["ffn_w_gate"], wts["ffn_w_up"], 1, "ffn_up1")
    h4 = _ffn_down(act1, wts["ffn_w_down"], h3, 1, "ffn_down1")

    dh4, g["final_norm_w"], loss, dh4b = _final_loss(h4, wts["final_norm_w"], tgt_p, seq, "final_loss")

    layers = wts["ffn_w_gate"].shape[1]

    ffn_names = ["ffn_w_down", "ffn_w_gate", "ffn_w_up"]

    def ffn_bwd(dh_out, dhb_out, h_mid, hn, gg, uu, act, layer, prev, ride=None, last=False):
        tag = str(layer)
        res = _ffn_down_bwd(dhb_out, wts["ffn_w_down"], gg, uu, layer, "ffn_down_bwd" + tag, ride=ride)
        (dg, du), rode = res if ride is not None else (res, None)
        d_down = _ffn_wgrad(act, [dhb_out], layer, layers, prev and prev[:1], True, "ffn_dwd" + tag)
        d_gu = _ffn_wgrad(hn, [dg, du], layer, layers, prev and prev[1:], False, "ffn_dwgu" + tag)
        grads = list(d_down) + list(d_gu)
        gs = rs_grads(ffn_names, grads) if last else None
        res = _ffn_up_bwd(dg, du, wts["ffn_w_gate"], wts["ffn_w_up"], layer, h_mid, wts["ffn_norm_w"][layer],
                          dh_out, "ffn_up_bwd" + tag, ride=_Ride("pair", gs) if last else None)
        (dh_mid, d_norm, dhb_mid), sib = res if last else (res, None)
        return dh_mid, dhb_mid, grads, d_norm, rode, gs, sib

    red = {}

    def rs_grads(names, grads):
        return [gr.reshape((N_SHARD,) + sh[n].shape) for n, gr in zip(names, grads)]

    def rs_partials(names, gs, sib):
        return [_rs_pair_add(gs[t], sib[t], idx, "rs_pair_add_" + n) for t, n in enumerate(names)]

    def rs_end(names, gs, sib, others, tag):
        mine = [_rs_final_add(gs[t], sib[t], others[t], idx, "rs_final_add_" + n) for t, n in enumerate(names)]
        red.update(zip(names, _rs_share(mine, "rs_share" + tag)))

    dh3, dh3b, ffn_grads, dfn1 = ffn_bwd(dh4, dh4b, h3, hn3, g1, u1, act1, 1, None)[:4]
    dy1 = _mm(dh3b, wts["dn_w_out"], mode="nt", name="dn_out_bwd")
    d_dn_out = _mm(y1, dh3b, mode="tn", name="dn_dwo")
    dproj1, dcw, dal, ddt, g["dn_norm_w"] = _dn_bwd(proj1, conv1, o1, dy1, st1, tinv, conv_w, alog, dtb,
                                                    wts["dn_norm_w"], seq)
    d_dn_in = _mm(hn2, dproj1, mode="tn", name="dn_dwi")
    d_dn_in = jnp.stack([d_dn_in[:, j * n_dn:(j + 1) * n_dn] for j in range(N_SHARD)])
    group1 = ["dn_w_out", "dn_w_in"]
    gs1 = rs_grads(group1, [d_dn_out, d_dn_in])
    (dh2, dmn1, dh2b), sib1 = _mm_nt_rms(dproj1, wts["dn_w_in"], h2, wts["mix_norm_w"][1], dh3, "dn_in_bwd",
                                         ride=_Ride("pair", gs1))
    g["dn_conv_w"] = dcw.reshape(CONV_K, DN_CONV_CH)
    g["dn_a_log"] = dal[0, DECAY_LANE:DECAY_LANE + DN_HEADS]
    g["dn_dt_bias"] = ddt[0, DECAY_LANE:DECAY_LANE + DN_HEADS]

    dh1, dh1b, _, dfn0, others1, gs2, sib2 = ffn_bwd(
        dh2, dh2b, h1, hn1, g0, u0, act0, 0, ffn_grads,
        ride=_Ride("chips", rs_partials(group1, gs1, sib1)), last=True)
    rs_end(group1, gs1, sib1, others1, "1")
    d_ret_out = _mm(y0, dh1b, mode="tn", name="ret_dwo")
    gs2b = rs_grads(["ret_w_out"], [d_ret_out])
    dy0, sib2b = _mm(dh1b, wts["ret_w_out"], mode="nt", name="ret_out_bwd", ride=_Ride("pair", gs2b))
    group2 = ffn_names + ["ret_w_out"]
    gs2, sib2 = gs2 + gs2b, list(sib2) + list(sib2b)
    (dproj0, g["ret_gn_w"]), others2 = _ret_bwd(proj0, o0, dy0, st0, cos, sin, consts, wts["ret_gn_w"], seq,
                                                ride=_Ride("chips", rs_partials(group2, gs2, sib2)))
    rs_end(group2, gs2, sib2, others2, "2")
    d_ret_in = _mm_cols_grad(hn0, dproj0, "ret_dwi")
    gs3 = rs_grads(["ret_w_in"], [d_ret_in])
    sib3 = _rs_pair(gs3, "rs_pair3")
    (dh0, dmn0, _), others3 = _mm_cols_t_rms(dproj0, wts["ret_w_in"], h0, wts["mix_norm_w"][0], dh1, "ret_in_bwd",
                                             ride=_Ride("chips", rs_partials(["ret_w_in"], gs3, sib3)))
    rs_end(["ret_w_in"], gs3, sib3, others3, "3")

    g["ffn_norm_w"] = jnp.concatenate([dfn0, dfn1], axis=0)
    g["mix_norm_w"] = jnp.concatenate([dmn0, dmn1], axis=0)
    g["meta_tokens"] = dh0[PAD:CHUNK]
    g["final_norm_w"] = g["final_norm_w"].reshape(D_MODEL)
    g["ret_gn_w"] = g["ret_gn_w"].reshape(RET_DV)
    g["dn_norm_w"] = g["dn_norm_w"].reshape(DN_DV)
    return loss, dh0, g, red


def _mesh_pos():
    return lax.axis_index("x"), lax.axis_index("y"), lax.axis_index("c")


def _other_chips(x, y):
    return [(1 - x, y), (x, 1 - y), (1 - x, 1 - y)]


def _remote(src, dst, send_sem, recv_sem, to):
    return pltpu.make_async_remote_copy(src_ref=src, dst_ref=dst, send_sem=send_sem, recv_sem=recv_sem,
                                        device_id=to, device_id_type=MESH)


GATHER_COPIES = 7


def _gather_phase(phase, ins, outs, send_sems, recv_sems):
    x, y, c = _mesh_pos()
    me = 2 * x + y
    chips = _other_chips(x, y)
    sibling = (x, y, 1 - c)

    def cp(t, k, src, dst, to):
        i = GATHER_COPIES * t + k
        return _remote(src, dst, send_sems.at[i], recv_sems.at[i], to)

    for t in range(len(ins)):
        own = cp(t, 0, ins[t], outs[t].at[me], sibling)
        if phase == 0:
            own.start()
        if phase == 2:
            own.wait()
        for k, (px, py) in enumerate(chips):
            landed = outs[t].at[2 * px + py, c]
            theirs = outs[t].at[2 * px + py, 1 - c]
            to_chip = cp(t, 1 + k, ins[t].at[c], outs[t].at[me, c], (px, py, c))
            if phase == 0:
                to_chip.start()
            if phase == 1:
                cp(t, 1 + k, ins[t].at[c], landed, (px, py, c)).wait_recv()
                cp(t, 4 + k, landed, landed, sibling).start()
            if phase == 2:
                to_chip.wait_send()
                cp(t, 4 + k, landed, landed, sibling).wait_send()
                cp(t, 4 + k, theirs, theirs, sibling).wait_recv()


def _chips_phase(phase, ins, outs, send_sems, recv_sems):
    x, y, c = _mesh_pos()
    for t in range(len(ins)):
        for k, (px, py) in enumerate(_other_chips(x, y)):
            cp = _remote(ins[t].at[2 * px + py], outs[t].at[k], send_sems.at[3 * t + k], recv_sems.at[3 * t + k],
                         (px, py, c))
            if phase == 0:
                cp.start()
            if phase == 2:
                cp.wait()


class _Ride:
    def __init__(self, kind, arrays):
        self.kind, self.arrays = kind, list(arrays)
        nt = len(self.arrays)
        if kind == "gather":
            self.phase_fn, n_sem = _gather_phase, GATHER_COPIES * nt
            self.out_shape = [jax.ShapeDtypeStruct((N_SHARD,) + a.shape, a.dtype) for a in self.arrays]
        elif kind == "pair":
            self.phase_fn, n_sem = _pair_phase, nt
            self.out_shape = [jax.ShapeDtypeStruct(a.shape[:1] + a.shape[2:], a.dtype) for a in self.arrays]
        else:
            self.phase_fn, n_sem = _chips_phase, 3 * nt
            self.out_shape = [jax.ShapeDtypeStruct((3,) + a.shape[1:], a.dtype) for a in self.arrays]
        self.in_specs, self.out_specs = [ANY] * nt, [ANY] * nt
        self.scratch = [pltpu.SemaphoreType.DMA((n_sem,)), pltpu.SemaphoreType.DMA((n_sem,))]

    def emit(self, step, nsteps, ins, outs, send_sems, recv_sems, before):
        mid = max(0, min((7 * nsteps) // 8, nsteps - 2))
        todo = [(0, 0), (1, mid)] if before else [(2, nsteps - 1)]
        for phase, at in todo:
            if phase == 1 and self.kind != "gather":
                continue

            @pl.when(step == at)
            def _(phase=phase):
                self.phase_fn(phase, ins, outs, send_sems, recv_sems)


def _gather_small(blk):
    r, wd = blk.shape

    def body(b_ref, out_ref, send_sems, recv_sems):
        x, y, c = _mesh_pos()
        chips = _other_chips(x, y)
        out_ref[2 * x + y] = b_ref[...]
        sends = [_remote(b_ref, out_ref.at[2 * x + y], send_sems.at[k], recv_sems.at[k], (px, py, c))
                 for k, (px, py) in enumerate(chips)]
        for cp in sends:
            cp.start()
        for k, (px, py) in enumerate(chips):
            _remote(b_ref, out_ref.at[2 * px + py], send_sems.at[k], recv_sems.at[k], (px, py, c)).wait_recv()
        for cp in sends:
            cp.wait_send()

    return pl.pallas_call(
        body, out_shape=jax.ShapeDtypeStruct((4, r, wd), blk.dtype), in_specs=[VMEM_SPEC], out_specs=VMEM_SPEC,
        scratch_shapes=[pltpu.SemaphoreType.DMA((3,)), pltpu.SemaphoreType.DMA((3,))],
        name="gather_small")(blk)


def _allreduce_small(blk):
    r, wd = blk.shape
    rels = [(dx, dy, dc) for dx in (0, 1) for dy in (0, 1) for dc in (0, 1) if dx or dy or dc]

    def body(b_ref, out_ref, buf_ref, send_sems, recv_sems):
        x, y, c = _mesh_pos()

        def peer(rel):
            dx, dy, dc = rel
            return (1 - x if dx else x, 1 - y if dy else y, 1 - c if dc else c)

        me = 4 * x + 2 * y + c
        buf_ref[me] = b_ref[...]
        sends = [_remote(b_ref, buf_ref.at[me], send_sems.at[k], recv_sems.at[k], peer(rel))
                 for k, rel in enumerate(rels)]
        for cp in sends:
            cp.start()
        for k, rel in enumerate(rels):
            px, py, pc = peer(rel)
            _remote(b_ref, buf_ref.at[4 * px + 2 * py + pc], send_sems.at[k], recv_sems.at[k],
                    (px, py, pc)).wait_recv()
        for cp in sends:
            cp.wait_send()
        acc = buf_ref[0]
        for d in range(1, 8):
            acc = acc + buf_ref[d]
        out_ref[...] = acc

    return pl.pallas_call(
        body, out_shape=jax.ShapeDtypeStruct((r, wd), blk.dtype), in_specs=[VMEM_SPEC], out_specs=VMEM_SPEC,
        scratch_shapes=[pltpu.VMEM((8, r, wd), blk.dtype), pltpu.SemaphoreType.DMA((7,)),
                        pltpu.SemaphoreType.DMA((7,))],
        name="allreduce_small")(blk)


def _rs_pair(gs, name):
    ride = _Ride("pair", gs)

    def body(*refs):
        nt = len(gs)
        for phase in (0, 2):
            _pair_phase(phase, refs[:nt], refs[nt:2 * nt], *refs[2 * nt:])

    return pl.pallas_call(body, out_shape=ride.out_shape, in_specs=ride.in_specs, out_specs=ride.out_specs,
                          scratch_shapes=ride.scratch, name=name)(*gs)


def _pair_phase(phase, ins, outs, send_sems, recv_sems):
    x, y, c = _mesh_pos()
    for t in range(len(ins)):
        cp = _remote(ins[t].at[:, 1 - c], outs[t], send_sems.at[t], recv_sems.at[t], (x, y, 1 - c))
        if phase == 0:
            cp.start()
        if phase == 2:
            cp.wait()


def _rs_tile(a, b):
    return _div_tile(a, 512 if b <= 1024 else 256, 16)


def _rs_pair_add(g, a, idx, name):
    _, _, rows, cols = g.shape
    tr = _rs_tile(rows, cols)

    def body(s_ref, g_ref, a_ref, p_ref):
        p_ref[...] = (g_ref[...] + a_ref[...]).astype(p_ref.dtype)

    blk = pl.BlockSpec((None, tr, cols), lambda j, i, s: (j, i, 0))
    spec = pltpu.PrefetchScalarGridSpec(
        num_scalar_prefetch=1, grid=(N_SHARD, rows // tr),
        in_specs=[pl.BlockSpec((None, None, tr, cols), lambda j, i, s: (j, s[0], i, 0)), blk], out_specs=blk)
    return pl.pallas_call(
        body, grid_spec=spec, out_shape=jax.ShapeDtypeStruct((N_SHARD, rows, cols), BF16), name=name,
        compiler_params=_params("parallel", "parallel"))(idx, g, a)


def _rs_final_add(g, a, b, idx, name):
    _, _, rows, cols = g.shape
    tr = _rs_tile(rows, cols)

    def body(s_ref, g_ref, a_ref, b0_ref, b1_ref, b2_ref, f_ref):
        own = g_ref[...] + a_ref[...]
        f_ref[...] = ((own + b0_ref[...].astype(F32)) + b1_ref[...].astype(F32)) + b2_ref[...].astype(F32)

    def b_spec(k):
        return pl.BlockSpec((None, tr, cols), lambda i, s: (k, i, 0))

    spec = pltpu.PrefetchScalarGridSpec(
        num_scalar_prefetch=1, grid=(rows // tr,),
        in_specs=[pl.BlockSpec((None, None, tr, cols), lambda i, s: (s[1], s[0], i, 0)),
                  pl.BlockSpec((None, tr, cols), lambda i, s: (s[1], i, 0)), b_spec(0), b_spec(1), b_spec(2)],
        out_specs=pl.BlockSpec((None, tr, cols), lambda i, s: (s[0], i, 0)))
    return pl.pallas_call(
        body, grid_spec=spec, out_shape=jax.ShapeDtypeStruct((2, rows, cols), F32), name=name,
        compiler_params=_params("parallel"))(idx, g, a, b, b, b)


def _rs_share(fs, name):
    nt = len(fs)

    def body(*refs):
        outs = refs[nt:2 * nt]
        send_sems, recv_sems = refs[2 * nt:]
        x, y, c = _mesh_pos()
        cps = [_remote(outs[t].at[c], outs[t].at[c], send_sems.at[t], recv_sems.at[t], (x, y, 1 - c))
               for t in range(nt)]
        for cp in cps:
            cp.start()
        for cp in cps:
            cp.wait()

    return pl.pallas_call(
        body, out_shape=[jax.ShapeDtypeStruct(f.shape, f.dtype) for f in fs],
        in_specs=[ANY] * nt, out_specs=[ANY] * nt, input_output_aliases={t: t for t in range(nt)},
        scratch_shapes=[pltpu.SemaphoreType.DMA((nt,)), pltpu.SemaphoreType.DMA((nt,))], name=name)(*fs)


def _adamw(w, g, m, v, name):
    lead, rows, cols = w.shape
    tr = rows // 4 if rows % 32 == 0 else rows

    def body(w_ref, g_ref, m_ref, v_ref, go_ref, d_ref, mo_ref, vo_ref):
        gv = g_ref[...]
        go_ref[...] = gv
        mn = ADAM_B1 * m_ref[...] + (1.0 - ADAM_B1) * gv
        vn = ADAM_B2 * v_ref[...] + (1.0 - ADAM_B2) * (gv * gv)
        m_hat = mn / (1.0 - ADAM_B1 ** ADAM_STEP)
        v_hat = vn / (1.0 - ADAM_B2 ** ADAM_STEP)
        d_ref[...] = -ADAM_LR * (m_hat / (jnp.sqrt(v_hat) + ADAM_EPS) + ADAM_WD * w_ref[...])
        mo_ref[...] = mn
        vo_ref[...] = vn

    blk = pl.BlockSpec((None, tr, cols), lambda l, i: (l, i, 0))
    out = jax.ShapeDtypeStruct((lead, rows, cols), F32)
    return pl.pallas_call(
        body, grid=(lead, rows // tr), in_specs=[blk] * 4, out_specs=[blk] * 4, out_shape=[out] * 4, name=name,
        compiler_params=_params("parallel", "parallel"))(w, g, m, v)


BIG = ["ret_w_in", "ret_w_out", "dn_w_in", "dn_w_out", "ffn_w_gate", "ffn_w_up", "ffn_w_down"]
TRANSPOSED_AT_BOUNDARY = {"dn_w_in": True, "ffn_w_gate": False, "ffn_w_up": False}
SMALL =["meta_tokens", "mix_norm_w", "ffn_norm_w", "ret_gn_w", "dn_conv_w", "dn_a_log", "dn_dt_bias",
         "dn_norm_w", "final_norm_w"]
SMALL_SHARDED = {"meta_tokens", "dn_conv_w", "dn_norm_w"}
ORDER = ["meta_tokens", "mix_norm_w", "ffn_norm_w", "ret_w_in", "ret_gn_w", "ret_w_out", "dn_w_in",
         "dn_conv_w", "dn_a_log", "dn_dt_bias", "dn_norm_w", "dn_w_out", "ffn_w_gate", "ffn_w_up",
         "ffn_w_down", "final_norm_w"]


def _halves(a):
    return a.reshape(2, -1, a.shape[-1])


def _pack_lanes(parts, align=8):
    flat = jnp.concatenate([p.reshape(-1) for p in parts])
    flat = jnp.pad(flat, (0, -flat.shape[0] % (align * LANES)))
    return flat.reshape(-1, LANES)


def _unpack(buf, shapes):
    lead = buf.shape[:-2]
    flat = buf.reshape(lead + (-1,))
    out, off = [], 0
    for shp in shapes:
        size = math.prod(shp)
        out.append(flat[..., off:off + size].reshape(lead + tuple(shp)))
        off += size
    return out


def _join_cols(shards):
    return jnp.concatenate([shards[j] for j in range(N_SHARD)], axis=-1)


def kernel(x, meta_tokens, mix_norm_w, ffn_norm_w, ret_w_in, ret_gn_w, ret_w_out, dn_w_in, dn_conv_w, dn_a_log, dn_dt_bias, dn_norm_w, dn_w_out, ffn_w_gate, ffn_w_up, ffn_w_down, final_norm_w, loss_target, m_meta_tokens, m_mix_norm_w, m_ffn_norm_w, m_ret_w_in, m_ret_gn_w, m_ret_w_out, m_dn_w_in, m_dn_conv_w, m_dn_a_log, m_dn_dt_bias, m_dn_norm_w, m_dn_w_out, m_ffn_w_gate, m_ffn_w_up, m_ffn_w_down, m_final_norm_w, v_meta_tokens, v_mix_norm_w, v_ffn_norm_w, v_ret_w_in, v_ret_gn_w, v_ret_w_out, v_dn_w_in, v_dn_conv_w, v_dn_a_log, v_dn_dt_bias, v_dn_norm_w, v_dn_w_out, v_ffn_w_gate, v_ffn_w_up, v_ffn_w_down, v_final_norm_w):
    w = dict(meta_tokens=meta_tokens, mix_norm_w=mix_norm_w, ffn_norm_w=ffn_norm_w, ret_w_in=ret_w_in,
             ret_gn_w=ret_gn_w, ret_w_out=ret_w_out, dn_w_in=dn_w_in, dn_conv_w=dn_conv_w, dn_a_log=dn_a_log,
             dn_dt_bias=dn_dt_bias, dn_norm_w=dn_norm_w, dn_w_out=dn_w_out, ffn_w_gate=ffn_w_gate,
             ffn_w_up=ffn_w_up, ffn_w_down=ffn_w_down, final_norm_w=final_norm_w)
    m = dict(meta_tokens=m_meta_tokens, mix_norm_w=m_mix_norm_w, ffn_norm_w=m_ffn_norm_w, ret_w_in=m_ret_w_in,
             ret_gn_w=m_ret_gn_w, ret_w_out=m_ret_w_out, dn_w_in=m_dn_w_in, dn_conv_w=m_dn_conv_w,
             dn_a_log=m_dn_a_log, dn_dt_bias=m_dn_dt_bias, dn_norm_w=m_dn_norm_w, dn_w_out=m_dn_w_out,
             ffn_w_gate=m_ffn_w_gate, ffn_w_up=m_ffn_w_up, ffn_w_down=m_ffn_w_down, final_norm_w=m_final_norm_w)
    v = dict(meta_tokens=v_meta_tokens, mix_norm_w=v_mix_norm_w, ffn_norm_w=v_ffn_norm_w, ret_w_in=v_ret_w_in,
             ret_gn_w=v_ret_gn_w, ret_w_out=v_ret_w_out, dn_w_in=v_dn_w_in, dn_conv_w=v_dn_conv_w,
             dn_a_log=v_dn_a_log, dn_dt_bias=v_dn_dt_bias, dn_norm_w=v_dn_norm_w, dn_w_out=v_dn_w_out,
             ffn_w_gate=v_ffn_w_gate, ffn_w_up=v_ffn_w_up, ffn_w_down=v_ffn_w_down, final_norm_w=v_final_norm_w)
    mx, my, mc = _mesh_pos()
    chip = 2 * mx + my

    sm_names = [n for n in SMALL if n in SMALL_SHARDED]
    sm_gathered = _unpack(_gather_small(_pack_lanes([w[n] for n in sm_names])), [w[n].shape for n in sm_names])
    full = {n: _join_cols(sm_gathered[i]) for i, n in enumerate(sm_names)}
    wts = {
        "meta_tokens": full["meta_tokens"], "mix_norm_w": mix_norm_w, "ffn_norm_w": ffn_norm_w,
        "ret_gn_w": ret_gn_w[0], "final_norm_w": final_norm_w, "dn_conv_w": full["dn_conv_w"][0],
        "dn_a_log": dn_a_log[0], "dn_dt_bias": dn_dt_bias[0], "dn_norm_w": full["dn_norm_w"][0],
    }
    idx = jnp.stack([mc, chip]).astype(jnp.int32)
    shards = {n: _halves(w[n].astype(MXU_DTYPE)) for n in BIG}
    loss_part, dh0, g, reduced = _train_step(x[0], loss_target[0], wts, shards, idx)
    seq = x.shape[1]
    grad_x = dh0[CHUNK:CHUNK + seq].reshape(x.shape)
    gsh = {}

    small_full_shapes = [g[n].shape for n in SMALL] + [(1,)]
    red = _unpack(_allreduce_small(_pack_lanes([g[n] for n in SMALL] + [loss_part[0, :1]])), small_full_shapes)
    loss = red[-1][0]
    for i, n in enumerate(SMALL):
        gn = red[i]
        if n in SMALL_SHARDED:
            width = w[n].shape[-1]
            gn = lax.dynamic_slice_in_dim(gn, chip * width, width, axis=gn.ndim - 1)
        gsh[n] = gn.reshape(w[n].shape)

    delta, new_m, new_v = {}, {}, {}
    for n in BIG:
        shp = w[n].shape
        if n in TRANSPOSED_AT_BOUNDARY and TRANSPOSED_AT_BOUNDARY[n]:
            view = lambda a: jnp.swapaxes(a, 1, 2).reshape(1, -1, LANES)
            back = lambda a: jnp.swapaxes(a.reshape(shp[0], shp[2], shp[1]), 1, 2)
        elif n in TRANSPOSED_AT_BOUNDARY:
            view = back = lambda a: jnp.swapaxes(a, 1, 2)
        else:
            view = back = lambda a: a
        res = _adamw(view(w[n]), view(reduced[n].reshape(shp)), view(m[n]), view(v[n]), "adamw_" + n)
        gsh[n], delta[n], new_m[n], new_v[n] = [back(r) for r in res]
    sm_local_shapes = [w[n].shape for n in SMALL]
    _, d_, m_, v_ = _adamw(*[_pack_lanes([t[n] for n in SMALL])[None] for t in (w, gsh, m, v)], "adamw_small")
    d_, m_, v_ = d_[0], m_[0], v_[0]
    for n, dd, mm, vv in zip(SMALL, _unpack(d_, sm_local_shapes), _unpack(m_, sm_local_shapes),
                             _unpack(v_, sm_local_shapes)):
        delta[n], new_m[n], new_v[n] = dd, mm, vv

    return (loss, grad_x, *[gsh[n] for n in ORDER], *[delta[n] for n in ORDER],
            *[new_m[n] for n in ORDER], *[new_v[n] for n in ORDER])
```

```python
import math

import jax
import jax.numpy as jnp
from jax import lax
from jax.experimental import pallas as pl
from jax.experimental.pallas import tpu as pltpu

F32 = jnp.float32
BF16 = jnp.bfloat16
MXU_DTYPE = BF16

D_MODEL = 1024
N_META = 16
CHUNK = 64
PAD = CHUNK - N_META
RMS_EPS = 1e-6
RET_HEADS, RET_DK, RET_DV = 4, 256, 512
RET_QK, RET_V = RET_HEADS * RET_DK, RET_HEADS * RET_DV
RET_IN = 2 * RET_QK + 2 * RET_V
ROPE_BASE = 10000.0
DN_HEADS, DN_DK, DN_DV = 8, 128, 256
DN_QK, DN_V = DN_HEADS * DN_DK, DN_HEADS * DN_DV
DN_CONV_CH = 2 * DN_QK + DN_V
DN_IN = DN_CONV_CH + DN_V + 2 * DN_HEADS
LANES = 128
DN_IN_USED = DN_CONV_CH + DN_V + LANES
DN_IN_PAD = DN_IN_USED + LANES
CONV_K = 4
FFN_HIDDEN = 2816
ADAM_LR, ADAM_B1, ADAM_B2, ADAM_EPS, ADAM_WD, ADAM_STEP = 0.001, 0.9, 0.999, 1e-08, 0.01, 10

ROW_ALIGN = 256
VMEM_LIMIT = 56 * 1024 * 1024
MESH = pl.DeviceIdType.MESH
ANY = pl.BlockSpec(memory_space=pl.ANY)
VMEM_SPEC = pl.BlockSpec(memory_space=pltpu.VMEM)
_HI = lax.Precision.HIGHEST


def _params(*sem):
    return pltpu.CompilerParams(dimension_semantics=sem, vmem_limit_bytes=VMEM_LIMIT)


def _dg(a, b, ca, cb, hi):
    dims = (((ca,), (cb,)), ((), ()))

    def dot(p, q):
        return lax.dot_general(p, q, dims, preferred_element_type=F32)

    if not hi:
        return dot(a.astype(MXU_DTYPE), b.astype(MXU_DTYPE))
    if MXU_DTYPE == F32:
        return lax.dot_general(a, b, dims, precision=_HI, preferred_element_type=F32)
    a_hi, b_hi = a.astype(MXU_DTYPE), b.astype(MXU_DTYPE)
    a_lo = (a - a_hi.astype(F32)).astype(MXU_DTYPE)
    b_lo = (b - b_hi.astype(F32)).astype(MXU_DTYPE)
    return dot(a_hi, b_hi) + (dot(a_hi, b_lo) + dot(a_lo, b_hi))


def _nn(a, b, hi=False):
    return _dg(a, b, 1, 0, hi)


def _nt(a, b, hi=False):
    return _dg(a, b, 1, 1, hi)


def _tn(a, b, hi=False):
    return _dg(a, b, 0, 0, hi)


def _iota(shape, dim):
    return lax.broadcasted_iota(jnp.int32, shape, dim)


def _valid_rows(first_row, rows, seq):
    r = first_row + _iota((rows, 1), 0)
    return ((r >= PAD) & (r < CHUNK + seq)).astype(F32)


def _rope(t, cs, sn):
    half = t.shape[-1] // 2
    t1, t2 = t[:, :half], t[:, half:]
    return jnp.concatenate([t1 * cs - t2 * sn, t1 * sn + t2 * cs], axis=1)


def _rope_bwd(d, cs, sn):
    half = d.shape[-1] // 2
    d1, d2 = d[:, :half], d[:, half:]
    return jnp.concatenate([d1 * cs + d2 * sn, d2 * cs - d1 * sn], axis=1)


def _col(x, idx):
    oh = (_iota((1, x.shape[1]), 1) == idx).astype(F32)
    return jnp.sum(x * oh, axis=1, keepdims=True)


def _row(x, idx):
    oh = (_iota((x.shape[0], 1), 0) == idx).astype(F32)
    return jnp.sum(x * oh, axis=0, keepdims=True)


def _shift_down(x, halo8, k):
    xr = pltpu.roll(x, k, 0)
    hr = pltpu.roll(halo8, k, 0)
    first = jnp.where(_iota((8, 1), 0) < k, hr, xr[0:8])
    return jnp.concatenate([first, xr[8:]], axis=0)


def _shift_up(x, next8, j):
    rows = x.shape[0]
    xr = pltpu.roll(x, rows - j, 0)
    nr = pltpu.roll(next8, 8 - j, 0)
    last = jnp.where(_iota((8, 1), 0) >= 8 - j, nr, xr[rows - 8:])
    return jnp.concatenate([xr[:rows - 8], last], axis=0)


def _gated_norm(o, gate, w):
    r = lax.rsqrt(jnp.mean(o * o, axis=-1, keepdims=True) + RMS_EPS)
    return o * r * w * (gate * jax.nn.sigmoid(gate))


def _gated_norm_bwd(dy, o, gate, w):
    r = lax.rsqrt(jnp.mean(o * o, axis=-1, keepdims=True) + RMS_EPS)
    nrm = o * r
    sg = jax.nn.sigmoid(gate)
    sl = gate * sg
    dgate = dy * nrm * w * (sg * (1.0 + gate * (1.0 - sg)))
    dn = dy * w * sl
    dw = jnp.sum(dy * nrm * sl, axis=0, keepdims=True)
    do = r * (dn - nrm * jnp.mean(dn * nrm, axis=-1, keepdims=True))
    return do, dgate, dw


def _softplus(z):
    return jnp.maximum(z, 0.0) + jnp.log(1.0 + jnp.exp(-jnp.abs(z)))


def _row_tile(rows, cap=768):
    for t in (768, 512, 256, 128, 64, 32, 16, 8):
        if t <= cap and rows % t == 0:
            return t
    return rows


TILE_BUDGET = 44 * 1024 * 1024


def _fit_rows(rows, row_bytes, fixed_bytes, value_row_bytes):
    best = None
    for t in range(LANES, rows + 1, LANES):
        if rows % t == 0 and 2 * (row_bytes * t + fixed_bytes) + value_row_bytes * t <= TILE_BUDGET:
            best = t
    return best or _row_tile(rows, 256)


def _div_tile(n, cap, mult):
    best = None
    for t in range(mult, min(cap, n) + 1, mult):
        if n % t == 0:
            best = t
    return best or n


def _col_tile(cols, cap=1536):
    best = None
    for t in range(LANES, min(cap, cols) + 1, LANES):
        if cols % t == 0:
            best = t
    return best or cols


def _rms_fwd(h, w, name, ride=None):
    rows, d = h.shape
    tm = _row_tile(rows)

    def body(h_ref, w_ref, o_ref):
        x = h_ref[...]
        r = lax.rsqrt(jnp.mean(x * x, axis=-1, keepdims=True) + RMS_EPS)
        o_ref[...] = (x * r * w_ref[...]).astype(o_ref.dtype)

    res, rode = _pcall(body, [h, w.reshape(1, d)], grid=(rows // tm,),
                       in_specs=[pl.BlockSpec((tm, d), lambda i: (i, 0)), pl.BlockSpec((1, d), lambda i: (0, 0))],
                       out_specs=[pl.BlockSpec((tm, d), lambda i: (i, 0))],
                       out_shape=[jax.ShapeDtypeStruct((rows, d), BF16)], name=name, sem=("parallel",), ride=ride)
    return res[0] if ride is None else (res[0], rode)


def _gmm_rms(name, grid, args, in_specs, row_spec, fn, h, w, resid, row_axis, red_axis=None, ride=None):
    m, d = h.shape
    n_in = len(args)
    vec = pl.BlockSpec((1, d), lambda *g: (0, 0))

    def body(*refs):
        ins = refs[:n_in]
        h_ref, w_ref, r_ref, dh_ref, dw_ref, dh16_ref = refs[n_in:]
        part = fn(*ins)
        row = pl.program_id(row_axis)

        def finish(dy):
            x = h_ref[...]
            r = lax.rsqrt(jnp.mean(x * x, axis=-1, keepdims=True) + RMS_EPS)
            xh = x * r
            dxh = dy * w_ref[...]
            dh = r_ref[...] + r * (dxh - xh * jnp.mean(dxh * xh, axis=-1, keepdims=True))
            dh_ref[...] = dh
            dh16_ref[...] = dh.astype(dh16_ref.dtype)
            dwp = jnp.sum(dy * xh, axis=0, keepdims=True)

            @pl.when(row == 0)
            def _():
                dw_ref[...] = dwp

            @pl.when(row > 0)
            def _():
                dw_ref[...] += dwp

        if red_axis is None:
            finish(part)
            return
        k = pl.program_id(red_axis)

        @pl.when(k == 0)
        def _():
            dh_ref[...] = part

        @pl.when(k > 0)
        def _():
            dh_ref[...] += part

        @pl.when(k == grid[red_axis] - 1)
        def _():
            finish(dh_ref[...])

    res, rode = _pcall(body, list(args) + [h, w.reshape(1, d), resid], grid=grid,
                       in_specs=list(in_specs) + [row_spec, vec, row_spec], out_specs=[row_spec, vec, row_spec],
                       out_shape=[jax.ShapeDtypeStruct((m, d), F32), jax.ShapeDtypeStruct((1, d), F32),
                                  jax.ShapeDtypeStruct((m, d), BF16)],
                       name=name, sem=("arbitrary",) * len(grid), ride=ride)
    return res if ride is None else (res, rode)


def _final_loss(h, w, tgt, seq, name):
    rows, d = h.shape
    tm = _row_tile(rows)

    def body(h_ref, w_ref, t_ref, dh_ref, dw_ref, loss_ref, dh16_ref):
        i = pl.program_id(0)
        r_idx = i * tm + _iota((tm, 1), 0)
        m = ((r_idx >= CHUNK) & (r_idx < CHUNK + seq)).astype(F32)
        x = h_ref[...]
        wv = w_ref[...]
        r = lax.rsqrt(jnp.mean(x * x, axis=-1, keepdims=True) + RMS_EPS)
        xh = x * r
        err = (xh * wv - t_ref[...]) * m
        lpart = 0.5 * jnp.sum(jnp.mean(err * err, axis=-1, keepdims=True), axis=0, keepdims=True)
        dyv = err * (1.0 / d)
        dxh = dyv * wv
        dh = r * (dxh - xh * jnp.mean(dxh * xh, axis=-1, keepdims=True))
        dh_ref[...] = dh
        dh16_ref[...] = dh.astype(dh16_ref.dtype)
        part = jnp.sum(dyv * xh, axis=0, keepdims=True)

        @pl.when(i == 0)
        def _():
            dw_ref[...] = part
            loss_ref[...] = jnp.broadcast_to(lpart, loss_ref.shape)

        @pl.when(i > 0)
        def _():
            dw_ref[...] += part
            loss_ref[...] += jnp.broadcast_to(lpart, loss_ref.shape)

    blk = pl.BlockSpec((tm, d), lambda i: (i, 0))
    vec = pl.BlockSpec((1, d), lambda i: (0, 0))
    return pl.pallas_call(
        body, grid=(rows // tm,), in_specs=[blk, vec, blk],
        out_specs=[blk, vec, pl.BlockSpec((1, LANES), lambda i: (0, 0)), blk],
        out_shape=[jax.ShapeDtypeStruct((rows, d), F32), jax.ShapeDtypeStruct((1, d), F32),
                   jax.ShapeDtypeStruct((1, LANES), F32), jax.ShapeDtypeStruct((rows, d), BF16)],
        name=name, compiler_params=_params("arbitrary"))(h, w.reshape(1, d), tgt)


def _isz(x):
    return jnp.dtype(x.dtype).itemsize


def _mm(a, b, *, mode, name, out_dtype=F32, resid=None, col_cap=1536, ride=None):
    if mode == "tn":
        m, k = a.shape
        n = b.shape[1]
        tn = _col_tile(n, col_cap)
        tm = _fit_rows(m, k * _isz(a) + tn * _isz(b), (3 * k * tn * 4) // 2, 2 * (k + tn))

        def body_tn(a_ref, b_ref, o_ref):
            i = pl.program_id(1)
            part = _tn(a_ref[...], b_ref[...])

            @pl.when(i == 0)
            def _():
                o_ref[...] = part

            @pl.when(i > 0)
            def _():
                o_ref[...] += part

        return pl.pallas_call(
            body_tn, grid=(n // tn, m // tm),
            in_specs=[pl.BlockSpec((tm, k), lambda j, i: (i, 0)),
                      pl.BlockSpec((tm, tn), lambda j, i: (i, j))],
            out_specs=pl.BlockSpec((k, tn), lambda j, i: (0, j)),
            out_shape=jax.ShapeDtypeStruct((k, n), F32), name=name,
            compiler_params=_params("parallel", "arbitrary"))(a, b)

    m, ka = a.shape
    n = b.shape[1] if mode == "nn" else b.shape[0]
    has_resid = resid is not None
    tn = _col_tile(n, col_cap)
    tm = _fit_rows(m, ka * _isz(a) + tn * (jnp.dtype(out_dtype).itemsize + (4 if has_resid else 0)),
                   ka * tn * _isz(b), 2 * ka + 8 * tn)

    def body(*refs):
        if has_resid:
            a_ref, b_ref, r_ref, o_ref = refs
        else:
            a_ref, b_ref, o_ref = refs
        acc = _nn(a_ref[...], b_ref[...]) if mode == "nn" else _nt(a_ref[...], b_ref[...])
        if has_resid:
            acc = acc + r_ref[...]
        o_ref[...] = acc.astype(o_ref.dtype)

    b_spec = (pl.BlockSpec((b.shape[0], tn), lambda j, i: (0, j)) if mode == "nn"
              else pl.BlockSpec((tn, b.shape[1]), lambda j, i: (j, 0)))
    o_spec = pl.BlockSpec((tm, tn), lambda j, i: (i, j))
    in_specs = [pl.BlockSpec((tm, ka), lambda j, i: (i, 0)), b_spec]
    args = [a, b]
    if has_resid:
        in_specs.append(o_spec)
        args.append(resid)
    res, rode = _pcall(body, args, grid=(n // tn, m // tm), in_specs=in_specs, out_specs=[o_spec],
                       out_shape=[jax.ShapeDtypeStruct((m, n), out_dtype)], name=name,
                       sem=("parallel", "parallel"), ride=ride)
    return res[0] if ride is None else (res[0], rode)


N_SHARD = 4


def _gmm(name, grid, args, in_specs, out_specs, out_shape, fn, red_axis=None, init_arg=None, aliases=None,
         ride=None):
    n_in = len(args)
    single = not isinstance(out_shape, (list, tuple))
    out_specs = [out_specs] if single else list(out_specs)
    out_shape = [out_shape] if single else list(out_shape)

    def body(*refs):
        _gmm_step(fn, refs[:n_in], refs[n_in:], red_axis, init_arg)

    sem = tuple("arbitrary" if ax == red_axis else "parallel" for ax in range(len(grid)))
    res, rode = _pcall(body, args, grid=grid, in_specs=in_specs, out_specs=out_specs, out_shape=out_shape,
                       name=name, sem=sem, aliases=aliases, ride=ride)
    ours = res[0] if single else res
    return ours if ride is None else (ours, rode)


def _gmm_step(fn, ins, outs, red_axis, init_arg):
    parts = fn(*ins)
    if red_axis is None:
        for o_ref, p in zip(outs, parts):
            o_ref[...] = p.astype(o_ref.dtype)
        return
    k = pl.program_id(red_axis)

    @pl.when(k == 0)
    def _():
        for idx, (o_ref, p) in enumerate(zip(outs, parts)):
            o_ref[...] = p + ins[init_arg][...] if (idx == 0 and init_arg is not None) else p

    @pl.when(k > 0)
    def _():
        for o_ref, p in zip(outs, parts):
            o_ref[...] += p


def _ride_body(ride, grid, n_in, n_out, n_scratch, body):
    n_rin, n_rout = len(ride.arrays), len(ride.out_shape)
    nsteps = math.prod(grid)

    def wrapped(*refs):
        ins = refs[:n_in]
        r_ins = refs[n_in:n_in + n_rin]
        o0 = n_in + n_rin
        outs = refs[o0:o0 + n_out]
        r_outs = refs[o0 + n_out:o0 + n_out + n_rout]
        s0 = o0 + n_out + n_rout
        scratch = refs[s0:s0 + n_scratch]
        send_sems, recv_sems = refs[-2:]
        step = pl.program_id(0)
        for ax in range(1, len(grid)):
            step = step * grid[ax] + pl.program_id(ax)
        ride.emit(step, nsteps, r_ins, r_outs, send_sems, recv_sems, before=True)
        body(*ins, *outs, *scratch)
        ride.emit(step, nsteps, r_ins, r_outs, send_sems, recv_sems, before=False)

    return wrapped


def _pcall(body, args, *, grid, in_specs, out_specs, out_shape, name, sem, scratch=(), aliases=None, ride=None):
    if ride is None:
        res = pl.pallas_call(body, grid=grid, in_specs=list(in_specs), out_specs=list(out_specs),
                             out_shape=list(out_shape), scratch_shapes=list(scratch), name=name,
                             input_output_aliases=aliases or {}, compiler_params=_params(*sem))(*args)
        return res, None
    n_in, n_out = len(args), len(out_shape)
    res = pl.pallas_call(
        _ride_body(ride, grid, n_in, n_out, len(scratch), body), grid=grid,
        in_specs=list(in_specs) + ride.in_specs, out_specs=list(out_specs) + ride.out_specs,
        out_shape=list(out_shape) + ride.out_shape, scratch_shapes=list(scratch) + ride.scratch, name=name,
        input_output_aliases=aliases or {},
        compiler_params=_params(*(("arbitrary",) * len(grid))))(*args, *ride.arrays)
    return res[:n_out], res[n_out:]


def _mm_cols(a, ws, name, ride=None):
    m, k = a.shape
    n = ws.shape[2]
    tm = _fit_rows(m, k * _isz(a) + n * 4, k * n * _isz(ws), 4 * n)
    return _gmm(name, (N_SHARD, m // tm), [a, ws],
                [pl.BlockSpec((tm, k), lambda j, i: (i, 0)), pl.BlockSpec((None, k, n), lambda j, i: (j, 0, 0))],
                pl.BlockSpec((tm, n), lambda j, i: (i, j)), jax.ShapeDtypeStruct((m, N_SHARD * n), F32),
                lambda a_ref, w_ref: (_nn(a_ref[...], w_ref[...]),), ride=ride)


def _mm_cols_t_rms(d, ws, h, w, resid, name, ride=None):
    m = d.shape[0]
    _, k, n = ws.shape
    tm = _fit_rows(m, n * _isz(d) + 3 * k * 4, k * n * _isz(ws), 16 * k)
    return _gmm_rms(name, (m // tm, N_SHARD), [d, ws],
                    [pl.BlockSpec((tm, n), lambda i, j: (i, j)), pl.BlockSpec((None, k, n), lambda i, j: (j, 0, 0))],
                    pl.BlockSpec((tm, k), lambda i, j: (i, 0)),
                    lambda d_ref, w_ref: _nt(d_ref[...], w_ref[...]), h, w, resid, 0, red_axis=1, ride=ride)


def _mm_nt_rms(a, b, h, w, resid, name, ride=None):
    m, n = a.shape
    k = b.shape[0]
    tm = _fit_rows(m, n * _isz(a) + 3 * k * 4, k * n * _isz(b), 16 * k)
    return _gmm_rms(name, (m // tm,), [a, b],
                    [pl.BlockSpec((tm, n), lambda i: (i, 0)), pl.BlockSpec((k, n), lambda i: (0, 0))],
                    pl.BlockSpec((tm, k), lambda i: (i, 0)),
                    lambda a_ref, b_ref: _nt(a_ref[...], b_ref[...]), h, w, resid, 0, ride=ride)


def _mm_cols_grad(a, d, name):
    m, k = a.shape
    n = d.shape[1] // N_SHARD
    tm = _fit_rows(m, k * _isz(a) + n * _isz(d), (3 * k * n * 4) // 2, 2 * (k + n))
    return _gmm(name, (N_SHARD, m // tm), [a, d],
                [pl.BlockSpec((tm, k), lambda j, i: (i, 0)), pl.BlockSpec((tm, n), lambda j, i: (i, j))],
                pl.BlockSpec((None, k, n), lambda j, i: (j, 0, 0)), jax.ShapeDtypeStruct((N_SHARD, k, n), F32),
                lambda a_ref, d_ref: (_tn(a_ref[...], d_ref[...]),), red_axis=1)


def _ffn_up(hn, wg, wu, layer, name, ride=None):
    m, k = hn.shape
    n = wg.shape[3]
    tm = _fit_rows(m, k * _isz(hn) + 3 * n * jnp.dtype(BF16).itemsize, 2 * k * n * _isz(wg), 16 * n)

    def fn(a_ref, wg_ref, wu_ref):
        a = a_ref[...]
        g = _nn(a, wg_ref[...])
        u = _nn(a, wu_ref[...])
        return g, u, g * jax.nn.sigmoid(g) * u

    w_spec = pl.BlockSpec((None, None, k, n), lambda j, i: (j, layer, 0, 0))
    o_spec = pl.BlockSpec((None, tm, n), lambda j, i: (j, i, 0))
    out = jax.ShapeDtypeStruct((N_SHARD, m, n), BF16)
    return _gmm(name, (N_SHARD, m // tm), [hn, wg, wu],
                [pl.BlockSpec((tm, k), lambda j, i: (i, 0)), w_spec, w_spec],
                [o_spec, o_spec, o_spec], [out, out, out], fn, ride=ride)


def _ffn_down(act, wd, resid, layer, name, ride=None):
    _, m, n = act.shape
    d = wd.shape[3]
    tm = _fit_rows(m, N_SHARD * n * _isz(act) + 2 * d * 4, N_SHARD * n * d * _isz(wd), 8 * d)

    def fn(a_ref, w_ref, r_ref):
        acc = r_ref[...]
        for j in range(N_SHARD):
            acc = acc + _nn(a_ref[j], w_ref[j])
        return (acc,)

    row = pl.BlockSpec((tm, d), lambda i: (i, 0))
    return _gmm(name, (m // tm,), [act, wd, resid],
                [pl.BlockSpec((N_SHARD, tm, n), lambda i: (0, i, 0)),
                 pl.BlockSpec((N_SHARD, None, n, d), lambda i: (0, layer, 0, 0)), row],
                row, jax.ShapeDtypeStruct((m, d), F32), fn, ride=ride)


def _ffn_down_bwd(dh, wd, g, u, layer, name, ride=None):
    m, d = dh.shape
    n = wd.shape[2]
    tm = _fit_rows(m, d * _isz(dh) + 4 * N_SHARD * n * jnp.dtype(BF16).itemsize, N_SHARD * n * d * _isz(wd),
                   2 * d + 24 * n)

    def body(dh_ref, wd_ref, g_ref, u_ref, dg_ref, du_ref):
        dhv = dh_ref[...].astype(MXU_DTYPE)
        for j in range(N_SHARD):
            dact = _nt(dhv, wd_ref[j])
            gv = g_ref[j].astype(F32)
            sg = jax.nn.sigmoid(gv)
            gs = gv * sg
            dg_ref[j] = (dact * u_ref[j].astype(F32) * (sg + gs * (1.0 - sg))).astype(dg_ref.dtype)
            du_ref[j] = (dact * gs).astype(du_ref.dtype)

    sh_spec = pl.BlockSpec((N_SHARD, tm, n), lambda i: (0, i, 0))
    out = jax.ShapeDtypeStruct((N_SHARD, m, n), BF16)
    res, rode = _pcall(body, [dh, wd, g, u], grid=(m // tm,),
                       in_specs=[pl.BlockSpec((tm, d), lambda i: (i, 0)),
                                 pl.BlockSpec((N_SHARD, None, n, d), lambda i: (0, layer, 0, 0)), sh_spec, sh_spec],
                       out_specs=[sh_spec, sh_spec], out_shape=[out, out], name=name, sem=("parallel",), ride=ride)
    return res if ride is None else (res, rode)


def _ffn_up_bwd(dg, du, wg, wu, layer, h, w, resid, name, ride=None):
    _, m, n = dg.shape
    k = wg.shape[2]
    tm = _fit_rows(m, 2 * N_SHARD * n * _isz(dg) + 3 * k * 4, 2 * N_SHARD * k * n * _isz(wg), 16 * k)

    def fn(dg_ref, du_ref, wg_ref, wu_ref):
        acc = _nt(dg_ref[0], wg_ref[0]) + _nt(du_ref[0], wu_ref[0])
        for j in range(1, N_SHARD):
            acc = acc + _nt(dg_ref[j], wg_ref[j]) + _nt(du_ref[j], wu_ref[j])
        return acc

    d_spec = pl.BlockSpec((N_SHARD, tm, n), lambda i: (0, i, 0))
    w_spec = pl.BlockSpec((N_SHARD, None, k, n), lambda i: (0, layer, 0, 0))
    return _gmm_rms(name, (m // tm,), [dg, du, wg, wu], [d_spec, d_spec, w_spec, w_spec],
                    pl.BlockSpec((tm, k), lambda i: (i, 0)), fn, h, w, resid, 0, ride=ride)


def _ffn_wgrad(lhs, rhs_list, layer, layers, prev, lhs_sharded, name):
    if lhs_sharded:
        _, m, k = lhs.shape
        n = rhs_list[0].shape[1]
    else:
        m, k = lhs.shape
        n = rhs_list[0].shape[2]
    n_out = len(rhs_list)
    tm = _fit_rows(m, k * _isz(lhs) + n_out * n * _isz(rhs_list[0]), (3 * n_out * k * n * 4) // 2,
                   2 * (k + n_out * n))
    sh = pl.BlockSpec((None, tm, k if lhs_sharded else n), lambda j, i: (j, i, 0))
    fl = pl.BlockSpec((tm, n if lhs_sharded else k), lambda j, i: (i, 0))
    n_out = len(rhs_list)
    args = [lhs] + list(rhs_list)
    in_specs = [sh if lhs_sharded else fl] + [fl if lhs_sharded else sh] * n_out
    aliases = None
    if prev is not None:
        aliases = {len(args) + t: t for t in range(n_out)}
        args = args + list(prev)
        in_specs = in_specs + [ANY] * n_out

    def fn(l_ref, *rest):
        lv = l_ref[...]
        return tuple(_tn(lv, r_ref[...]) for r_ref in rest[:n_out])

    o_spec = pl.BlockSpec((None, None, k, n), lambda j, i: (j, layer, 0, 0))
    out = jax.ShapeDtypeStruct((N_SHARD, layers, k, n), F32)
    return _gmm(name, (N_SHARD, m // tm), args, in_specs, [o_spec] * n_out, [out] * n_out, fn,
                red_axis=1, aliases=aliases)


def _ret_consts():
    log_gamma = jnp.log1p(-jnp.exp2(-5.0 - jnp.arange(RET_HEADS, dtype=F32)))
    idx = jnp.arange(CHUNK, dtype=F32)
    rel = idx[:, None] - idx[None, :]
    dmask = jnp.where((rel >= 0)[None], jnp.exp(log_gamma[:, None, None] * jnp.maximum(rel, 0.0)), 0.0)
    xi = jnp.exp(log_gamma[:, None] * (idx[None, :] + 1.0))[:, :, None]
    zeta = jnp.exp(log_gamma[:, None] * (CHUNK - 1.0 - idx[None, :]))[:, :, None]
    gamma_c = jnp.exp(log_gamma * CHUNK)
    wide = (RET_HEADS, CHUNK, RET_DK)
    return dmask, jnp.broadcast_to(xi, wide), jnp.broadcast_to(zeta, wide), gamma_c


def _rope_tables(nc):
    half = RET_DK // 2
    inv_freq = ROPE_BASE ** (-jnp.arange(half, dtype=F32) / half)
    a_chunk = (jnp.arange(nc) * CHUNK - PAD).astype(F32)[:, None] * inv_freq[None, :]
    a_row = jnp.arange(CHUNK).astype(F32)[:, None] * inv_freq[None, :]
    return (jnp.stack([jnp.cos(a_chunk), jnp.sin(a_chunk)], axis=1),
            jnp.stack([jnp.cos(a_row), jnp.sin(a_row)], axis=0))


RET_CPS = 4


def _rope_chunk(rc_ref, rr_ref, c):
    cc, sc = rc_ref[c, 0:1, :], rc_ref[c, 1:2, :]
    cr, sr = rr_ref[0], rr_ref[1]
    return cc * cr - sc * sr, sc * cr + cc * sr


def _rope_specs(order):
    half = RET_DK // 2
    return [pl.BlockSpec((RET_CPS, 2, half), lambda n: (order(n), 0, 0)),
            pl.BlockSpec((2, CHUNK, half), lambda n: (0, 0, 0))]


def _ret_specs(order):
    rows = RET_CPS * CHUNK
    return [pl.BlockSpec((rows, RET_QK), lambda n: (order(n), 0)),
            pl.BlockSpec((rows, RET_QK), lambda n: (order(n), 1)),
            pl.BlockSpec((rows, RET_V), lambda n: (order(n), 1)),
            pl.BlockSpec((rows, RET_V), lambda n: (order(n), 2))]


def _ret_const_specs():
    return [pl.BlockSpec((RET_HEADS, CHUNK, CHUNK), lambda n: (0, 0, 0)),
            pl.BlockSpec((RET_HEADS, CHUNK, RET_DK), lambda n: (0, 0, 0)),
            pl.BlockSpec((RET_HEADS, CHUNK, RET_DK), lambda n: (0, 0, 0)),
            pl.BlockSpec((1, RET_DV), lambda n: (0, 0))]


def _ret_fwd(proj, cos, sin, consts, gn_w, seq, ride=None):
    rows = proj.shape[0]
    nc = rows // CHUNK
    dmask, xi, zeta, gamma_c = consts

    def body(gam_ref, q_ref, k_ref, v_ref, g_ref, cos_ref, sin_ref, dm_ref, xi_ref, ze_ref, gn_ref,
             o_ref, y_ref, ss_ref, s_ref):
        n = pl.program_id(0)

        @pl.when(n == 0)
        def _():
            s_ref[...] = jnp.zeros_like(s_ref)

        gn = gn_ref[...]
        hs = range(RET_HEADS)
        qk_cols = [slice(h * RET_DK, (h + 1) * RET_DK) for h in hs]
        v_cols = [slice(h * RET_DV, (h + 1) * RET_DV) for h in hs]
        for c in range(RET_CPS):
            rs = slice(c * CHUNK, (c + 1) * CHUNK)
            cs, sn = _rope_chunk(cos_ref, sin_ref, c)
            kscale = _valid_rows((n * RET_CPS + c) * CHUNK, CHUNK, seq) * (RET_DK ** -0.5)
            qr_l = [_rope(q_ref[rs, col], cs, sn) for col in qk_cols]
            kr_l = [_rope(k_ref[rs, col], cs, sn) * kscale for col in qk_cols]
            v_l = [v_ref[rs, col] for col in v_cols]
            s_l = [s_ref[h] for h in hs]
            sc_l = [_nt(qr, kr) * dm_ref[h] for h, (qr, kr) in enumerate(zip(qr_l, kr_l))]
            o_l = [_nn(sc_l[h], v_l[h]) + _nn(qr_l[h] * xi_ref[h], s_l[h]) for h in hs]
            for h in hs:
                ss_ref[c, h] = s_l[h].astype(ss_ref.dtype)
                s_ref[h] = gam_ref[h] * s_l[h] + _tn(kr_l[h] * ze_ref[h], v_l[h])
                o_ref[rs, v_cols[h]] = o_l[h]
                y_ref[rs, v_cols[h]] = _gated_norm(o_l[h], g_ref[rs, v_cols[h]], gn).astype(y_ref.dtype)

    fwd = lambda n: n
    row_v = pl.BlockSpec((RET_CPS * CHUNK, RET_V), lambda n: (n, 0))
    res, rode = _pcall(
        body, [gamma_c, proj, proj, proj, proj, cos, sin, dmask, xi, zeta, gn_w.reshape(1, RET_DV)],
        grid=(nc // RET_CPS,),
        in_specs=[pl.BlockSpec(memory_space=pltpu.SMEM)] + _ret_specs(fwd) + _rope_specs(fwd)
        + _ret_const_specs(),
        out_specs=[row_v, row_v,
                   pl.BlockSpec((RET_CPS, RET_HEADS, RET_DK, RET_DV), lambda n: (n, 0, 0, 0))],
        out_shape=[jax.ShapeDtypeStruct((rows, RET_V), F32), jax.ShapeDtypeStruct((rows, RET_V), BF16),
                   jax.ShapeDtypeStruct((nc, RET_HEADS, RET_DK, RET_DV), BF16)],
        scratch=[pltpu.VMEM((RET_HEADS, RET_DK, RET_DV), F32)], name="ret_fwd", sem=("arbitrary",), ride=ride)
    return res if ride is None else (res, rode)


def _ret_bwd(proj, o, dy, states, cos, sin, consts, gn_w, seq, ride=None):
    rows = proj.shape[0]
    nc = rows // CHUNK
    dmask, xi, zeta, gamma_c = consts

    def body(gam_ref, q_ref, k_ref, v_ref, g_ref, o_ref, dy_ref, ss_ref, cos_ref, sin_ref,
             dm_ref, xi_ref, ze_ref, gn_ref, dp_ref, dgn_ref, ds_ref):
        n = pl.program_id(0)

        @pl.when(n == 0)
        def _():
            ds_ref[...] = jnp.zeros_like(ds_ref)
            dgn_ref[...] = jnp.zeros_like(dgn_ref)

        gn = gn_ref[...]
        dgn = jnp.zeros((1, RET_DV), F32)
        hs = range(RET_HEADS)
        qk_cols = [slice(h * RET_DK, (h + 1) * RET_DK) for h in hs]
        v_cols = [slice(h * RET_DV, (h + 1) * RET_DV) for h in hs]
        for c in reversed(range(RET_CPS)):
            rs = slice(c * CHUNK, (c + 1) * CHUNK)
            cs, sn = _rope_chunk(cos_ref, sin_ref, c)
            kscale = _valid_rows(((steps - 1 - n) * RET_CPS + c) * CHUNK, CHUNK, seq) * (RET_DK ** -0.5)
            qr_l = [_rope(q_ref[rs, col], cs, sn) for col in qk_cols]
            kr_l = [_rope(k_ref[rs, col], cs, sn) * kscale for col in qk_cols]
            v_l = [v_ref[rs, col] for col in v_cols]
            s_l = [ss_ref[c, h] for h in hs]
            ds_l = [ds_ref[h] for h in hs]
            sc_l = [_nt(qr_l[h], kr_l[h]) * dm_ref[h] for h in hs]
            gnb = [_gated_norm_bwd(dy_ref[rs, col], o_ref[rs, col], g_ref[rs, col], gn) for col in v_cols]
            do_l = [x[0] for x in gnb]
            dsc_l = [_nt(do_l[h], v_l[h]) * dm_ref[h] for h in hs]
            dv_l = [_tn(sc_l[h], do_l[h]) + _nn(kr_l[h] * ze_ref[h], ds_l[h]) for h in hs]
            dqr_l = [_nn(dsc_l[h], kr_l[h]) + _nt(do_l[h], s_l[h]) * xi_ref[h] for h in hs]
            dkr_l = [_tn(dsc_l[h], qr_l[h]) + _nt(v_l[h], ds_l[h]) * ze_ref[h] for h in hs]
            for h in hs:
                dgn = dgn + gnb[h][2]
                ds_ref[h] = gam_ref[h] * ds_l[h] + _tn(qr_l[h] * xi_ref[h], do_l[h])
                dp_ref[rs, qk_cols[h]] = _rope_bwd(dqr_l[h], cs, sn).astype(dp_ref.dtype)
                dp_ref[rs, RET_QK + h * RET_DK:RET_QK + (h + 1) * RET_DK] = (
                    _rope_bwd(dkr_l[h] * kscale, cs, sn).astype(dp_ref.dtype))
                dp_ref[rs, 2 * RET_QK + h * RET_DV:2 * RET_QK + (h + 1) * RET_DV] = dv_l[h].astype(dp_ref.dtype)
                dp_ref[rs, 2 * RET_QK + RET_V + h * RET_DV:2 * RET_QK + RET_V + (h + 1) * RET_DV] = (
                    gnb[h][1].astype(dp_ref.dtype))
        dgn_ref[...] += dgn

    steps = nc // RET_CPS
    rev = lambda n: steps - 1 - n
    row_v = pl.BlockSpec((RET_CPS * CHUNK, RET_V), lambda n: (rev(n), 0))
    res, rode = _pcall(
        body, [gamma_c, proj, proj, proj, proj, o, dy, states, cos, sin, dmask, xi, zeta,
               gn_w.reshape(1, RET_DV)],
        grid=(steps,),
        in_specs=[pl.BlockSpec(memory_space=pltpu.SMEM)] + _ret_specs(rev) + [
            row_v, row_v, pl.BlockSpec((RET_CPS, RET_HEADS, RET_DK, RET_DV), lambda n: (rev(n), 0, 0, 0))]
        + _rope_specs(rev) + _ret_const_specs(),
        out_specs=[pl.BlockSpec((RET_CPS * CHUNK, RET_IN), lambda n: (rev(n), 0)),
                   pl.BlockSpec((1, RET_DV), lambda n: (0, 0))],
        out_shape=[jax.ShapeDtypeStruct((rows, RET_IN), BF16), jax.ShapeDtypeStruct((1, RET_DV), F32)],
        scratch=[pltpu.VMEM((RET_HEADS, RET_DK, RET_DV), F32)], name="ret_bwd", sem=("arbitrary",), ride=ride)
    return res if ride is None else (res, rode)


GATE_COL = DN_CONV_CH // DN_V
BA_COL = (DN_CONV_CH + DN_V) // LANES
BETA_LANE, DECAY_LANE = 0, DN_HEADS
INV_SHIFT = 4
INV_SQUARINGS = INV_SHIFT - 1
assert CHUNK == 4 << INV_SHIFT


DN_CPS = 2


def _dn_in_specs(order, conv_saved=False):
    rows = DN_CPS * CHUNK
    return [pl.BlockSpec((rows, DN_CONV_CH), lambda n: (order(n), 0)),
            pl.BlockSpec((rows, DN_CONV_CH), lambda n: (order(n), 0)) if conv_saved else
            pl.BlockSpec((8, DN_CONV_CH), lambda n: (jnp.maximum(order(n) * (rows // 8) - 1, 0), 0)),
            pl.BlockSpec((rows, DN_V), lambda n: (order(n), GATE_COL)),
            pl.BlockSpec((rows, LANES), lambda n: (order(n), BA_COL)),
            pl.BlockSpec((CONV_K, 1, DN_CONV_CH), lambda n: (0, 0, 0)),
            pl.BlockSpec((1, LANES), lambda n: (0, 0)),
            pl.BlockSpec((1, LANES), lambda n: (0, 0)),
            pl.BlockSpec((1, DN_DV), lambda n: (0, 0))]


def _dn_front(c, seq, x, halo, ba, cw_ref, al_ref, dt_ref, yc=None):
    valid = _valid_rows(c * CHUNK, CHUNK, seq)
    xin = x * valid
    if yc is None:
        halo = halo * _valid_rows(c * CHUNK - 8, 8, seq)
        yc = xin * cw_ref[CONV_K - 1]
        for k in range(1, CONV_K):
            yc = yc + _shift_down(xin, halo, k) * cw_ref[CONV_K - 1 - k]
    sgc = jax.nn.sigmoid(yc)
    sig = jax.nn.sigmoid(ba)
    beta = sig * valid
    z = ba + dt_ref[...]
    eal = jnp.exp(al_ref[...])
    g = -eal * _softplus(z) * valid
    ri, ci = _iota((CHUNK, CHUNK), 0), _iota((CHUNK, CHUNK), 1)
    lower = (ri >= ci).astype(F32)
    upper = (ri <= ci).astype(F32)
    eye = (ri == ci).astype(F32)
    gam = _nn(lower, g, hi=True)
    gam_t = _tn(g, upper, hi=True)
    return dict(valid=valid, xin=xin, yc=yc, sgc=sgc, act=yc * sgc, sig=sig, beta=beta, z=z,
                eal=eal, g=g, gam=gam, gam_t=gam_t, ri=ri, ci=ci, upper=upper, eye=eye)


def _dn_head(f, h):
    act = f["act"]
    q_raw = act[:, h * DN_DK:(h + 1) * DN_DK]
    k_raw = act[:, DN_QK + h * DN_DK:DN_QK + (h + 1) * DN_DK]
    v = act[:, 2 * DN_QK + h * DN_DV:2 * DN_QK + (h + 1) * DN_DV]
    rq = lax.rsqrt(jnp.sum(q_raw * q_raw, axis=-1, keepdims=True) + RMS_EPS)
    rk = lax.rsqrt(jnp.sum(k_raw * k_raw, axis=-1, keepdims=True) + RMS_EPS)
    qh = q_raw * rq
    kn = k_raw * rk
    gam_c = _col(f["gam"], DECAY_LANE + h)
    gam_r = _row(f["gam_t"], DECAY_LANE + h)
    bc = _col(f["beta"], BETA_LANE + h)
    diff = gam_c - gam_r
    decay = jnp.where(f["ri"] >= f["ci"], jnp.exp(jnp.minimum(diff, 0.0)), 0.0)
    glast = jnp.sum(gam_r * (_iota((1, CHUNK), 1) == CHUNK - 1).astype(F32), axis=1, keepdims=True)
    return dict(rq=rq, rk=rk, qh=qh, qn=qh * (DN_DK ** -0.5), kn=kn, v=v, gam_c=gam_c, gam_r=gam_r,
                bc=bc, diff=diff, decay=decay, egam=jnp.exp(gam_c), glast=glast,
                eglast=jnp.exp(glast), ekd=jnp.exp(glast - gam_c))


def _dn_fwd(proj, conv_w, alog, dtb, norm_w, seq):
    rows = proj.shape[0]
    nc = rows // CHUNK

    def body(x_ref, halo_ref, gate_ref, ba_ref, cw_ref, al_ref, dt_ref, nw_ref,
             o_ref, y_ref, ss_ref, t_ref, yc_ref, s_ref):
        n = pl.program_id(0)

        @pl.when(n == 0)
        def _():
            s_ref[...] = jnp.zeros_like(s_ref)

        nw = nw_ref[...]
        pre = []
        for c in range(DN_CPS):
            rs = slice(c * CHUNK, (c + 1) * CHUNK)
            halo = halo_ref[...] if c == 0 else x_ref[c * CHUNK - 8:c * CHUNK, :]
            f = _dn_front(n * DN_CPS + c, seq, x_ref[rs, :], halo, ba_ref[rs, :], cw_ref, al_ref, dt_ref)
            yc_ref[rs, :] = f["yc"]
            ri, ci = f["ri"], f["ci"]
            eye = f["eye"]
            diag_m = (jnp.right_shift(ri, INV_SHIFT) == jnp.right_shift(ci, INV_SHIFT)).astype(F32)
            half_m = (jnp.right_shift(ri, INV_SHIFT + 1) == jnp.right_shift(ci, INV_SHIFT + 1)).astype(F32)
            heads = [_dn_head(f, h) for h in range(DN_HEADS)]
            a_all = [jnp.where(ri > ci, hd["bc"] * _nt(hd["kn"], hd["kn"]) * hd["decay"], 0.0) for hd in heads]
            b_all = [a * diag_m for a in a_all]
            t_all = [eye - b for b in b_all]
            for _ in range(INV_SQUARINGS):
                b_all = [_nn(b, b, hi=True) for b in b_all]
                t_all = [t + _nn(t, b, hi=True) for t, b in zip(t_all, b_all)]
            for off_m in (half_m - diag_m, 1.0 - half_m):
                x_all = [_nn(a * off_m, t, hi=True) for a, t in zip(a_all, t_all)]
                t_all = [t - _nn(t, x, hi=True) for t, x in zip(t_all, x_all)]
            u_all = [_nn(t, hd["v"] * hd["bc"], hi=True) for t, hd in zip(t_all, heads)]
            w_all = [_nn(t, hd["kn"] * (hd["bc"] * hd["egam"]), hi=True) for t, hd in zip(t_all, heads)]
            qk_all = [_nt(hd["qn"], hd["kn"]) * hd["decay"] for hd in heads]
            for h in range(DN_HEADS):
                t_ref[c, h] = t_all[h]
            pre.append((heads, u_all, w_all, qk_all))
        for c in range(DN_CPS):
            rs = slice(c * CHUNK, (c + 1) * CHUNK)
            heads, u_all, w_all, qk_all = pre[c]
            s_all = [s_ref[h] for h in range(DN_HEADS)]
            os_all = [_nn(hd["qn"] * hd["egam"], s) for hd, s in zip(heads, s_all)]
            vnew_all = [u - _nn(w, s) for u, w, s in zip(u_all, w_all, s_all)]
            o_all = [os + _nn(qk, vn) for os, qk, vn in zip(os_all, qk_all, vnew_all)]
            snew_all = [s * hd["eglast"] + _tn(hd["kn"] * hd["ekd"], vn)
                        for s, hd, vn in zip(s_all, heads, vnew_all)]
            for h in range(DN_HEADS):
                v_cols = slice(h * DN_DV, (h + 1) * DN_DV)
                ss_ref[c, h] = s_all[h]
                s_ref[h] = snew_all[h]
                o_ref[rs, v_cols] = o_all[h]
                y_ref[rs, v_cols] = _gated_norm(o_all[h], gate_ref[rs, v_cols], nw).astype(y_ref.dtype)

    fwd = lambda n: n
    row_v = pl.BlockSpec((DN_CPS * CHUNK, DN_V), lambda n: (n, 0))
    return pl.pallas_call(
        body, grid=(nc // DN_CPS,), in_specs=_dn_in_specs(fwd),
        out_specs=[row_v, row_v,
                   pl.BlockSpec((DN_CPS, DN_HEADS, DN_DK, DN_DV), lambda n: (n, 0, 0, 0)),
                   pl.BlockSpec((DN_CPS, DN_HEADS, CHUNK, CHUNK), lambda n: (n, 0, 0, 0)),
                   pl.BlockSpec((DN_CPS * CHUNK, DN_CONV_CH), lambda n: (n, 0))],
        out_shape=[jax.ShapeDtypeStruct((rows, DN_V), F32), jax.ShapeDtypeStruct((rows, DN_V), BF16),
                   jax.ShapeDtypeStruct((nc, DN_HEADS, DN_DK, DN_DV), F32),
                   jax.ShapeDtypeStruct((nc, DN_HEADS, CHUNK, CHUNK), F32),
                   jax.ShapeDtypeStruct((rows, DN_CONV_CH), F32)],
        scratch_shapes=[pltpu.VMEM((DN_HEADS, DN_DK, DN_DV), F32)],
        name="dn_fwd", compiler_params=_params("arbitrary"))(
            proj, proj, proj, proj, conv_w, alog, dtb, norm_w.reshape(1, DN_DV))


def _dn_bwd(proj, conv_out, o, dy, states, tinv, conv_w, alog, dtb, norm_w, seq):
    rows = proj.shape[0]
    nc = rows // CHUNK

    def body(x_ref, yc_ref, gate_ref, ba_ref, cw_ref, al_ref, dt_ref, nw_ref,
             o_ref, dy_ref, ss_ref, t_ref,
             dp_ref, dcw_ref, dal_ref, ddt_ref, dnw_ref, ds_ref, nxt_ref):
        n = pl.program_id(0)

        @pl.when(n == 0)
        def _():
            ds_ref[...] = jnp.zeros_like(ds_ref)
            nxt_ref[...] = jnp.zeros_like(nxt_ref)
            dcw_ref[...] = jnp.zeros_like(dcw_ref)
            dal_ref[...] = jnp.zeros_like(dal_ref)
            ddt_ref[...] = jnp.zeros_like(ddt_ref)
            dnw_ref[...] = jnp.zeros_like(dnw_ref)

        for c in reversed(range(DN_CPS)):
            rs = pl.ds(c * CHUNK, CHUNK)
            chunk((steps - 1 - n) * DN_CPS + c, x_ref.at[rs], yc_ref.at[rs], gate_ref.at[rs], ba_ref.at[rs],
                  cw_ref, al_ref, dt_ref, nw_ref, o_ref.at[rs], dy_ref.at[rs], ss_ref.at[c], t_ref.at[c],
                  dp_ref.at[rs], dcw_ref, dal_ref, ddt_ref, dnw_ref, ds_ref, nxt_ref)

    def chunk(ch, x_ref, yc_ref, gate_ref, ba_ref, cw_ref, al_ref, dt_ref, nw_ref,
              o_ref, dy_ref, ss_ref, t_ref,
              dp_ref, dcw_ref, dal_ref, ddt_ref, dnw_ref, ds_ref, nxt_ref):
        f = _dn_front(ch, seq, x_ref[...], None, ba_ref[...], cw_ref, al_ref, dt_ref, yc_ref[...])
        ri, ci = f["ri"], f["ci"]
        strict = (ri > ci).astype(F32)
        nw = nw_ref[...]
        lane128 = _iota((1, LANES), 1)
        row128 = _iota((LANES, 1), 0)
        dgam_col = jnp.zeros((CHUNK, LANES), F32)
        dgam_row = jnp.zeros((LANES, CHUNK), F32)
        dbeta = jnp.zeros((CHUNK, LANES), F32)
        dnw = jnp.zeros((1, DN_DV), F32)
        hs = range(DN_HEADS)
        heads = [_dn_head(f, h) for h in hs]
        cols = [slice(h * DN_DV, (h + 1) * DN_DV) for h in hs]
        t_l = [t_ref[h] for h in hs]
        s_l = [ss_ref[h] for h in hs]
        ds_l = [ds_ref[h] for h in hs]
        kk_l = [_nt(hd["kn"], hd["kn"]) for hd in heads]
        p_l = [_nt(hd["qn"], hd["kn"]) for hd in heads]
        rhsw_l = [hd["kn"] * (hd["bc"] * hd["egam"]) for hd in heads]
        u_l = [_nn(t, hd["v"] * hd["bc"], hi=True) for t, hd in zip(t_l, heads)]
        w_l = [_nn(t, r, hi=True) for t, r in zip(t_l, rhsw_l)]
        vnew_l = [u - _nn(w, s) for u, w, s in zip(u_l, w_l, s_l)]
        gnb = [_gated_norm_bwd(dy_ref[:, c], o_ref[:, c], gate_ref[:, c], nw) for c in cols]
        do_l = [x[0] for x in gnb]
        for h in hs:
            dp_ref[:, DN_CONV_CH + h * DN_DV:DN_CONV_CH + (h + 1) * DN_DV] = gnb[h][1].astype(dp_ref.dtype)
            dnw = dnw + gnb[h][2]
        qg_l = [hd["qn"] * hd["egam"] for hd in heads]
        kd_l = [hd["kn"] * hd["ekd"] for hd in heads]
        dvnew_l = [_tn(p * hd["decay"], do) + _nn(kd, ds)
                   for p, hd, do, kd, ds in zip(p_l, heads, do_l, kd_l, ds_l)]
        m_l = [_nt(do, vn) for do, vn in zip(do_l, vnew_l)]
        dqg_l = [_nt(do, s) for do, s in zip(do_l, s_l)]
        dkd_l = [_nt(vn, ds) for vn, ds in zip(vnew_l, ds_l)]
        for h in hs:
            ds_ref[h] = (ds_l[h] * heads[h]["eglast"] + _tn(qg_l[h], do_l[h]) - _tn(w_l[h], dvnew_l[h]))
        dw_l = [-_nt(dvn, s) for dvn, s in zip(dvnew_l, s_l)]
        dru_l = [_tn(t, dvn, hi=True) for t, dvn in zip(t_l, dvnew_l)]
        drw_l = [_tn(t, dw_, hi=True) for t, dw_ in zip(t_l, dw_l)]
        da_l = [-(_nt(dru, u) + _nt(drw, w)) * strict for dru, u, drw, w in zip(dru_l, u_l, drw_l, w_l)]
        dp_l = [m * hd["decay"] for m, hd in zip(m_l, heads)]
        dkk_l = [da * (hd["bc"] * hd["decay"]) for da, hd in zip(da_l, heads)]
        dqn_l = [dqg * hd["egam"] + _nn(dp, hd["kn"]) for dqg, hd, dp in zip(dqg_l, heads, dp_l)]
        dkn_l = [_tn(dp, hd["qn"]) + dkd * hd["ekd"] + drw * (hd["bc"] * hd["egam"])
                 + _nn(dkk, hd["kn"]) + _tn(dkk, hd["kn"])
                 for dp, hd, dkd, drw, dkk in zip(dp_l, heads, dkd_l, drw_l, dkk_l)]
        dq_parts, dk_parts, dv_parts = [], [], []
        for h in hs:
            hd = heads[h]
            kn, v, bc, egam, decay = hd["kn"], hd["v"], hd["bc"], hd["egam"], hd["decay"]
            t1 = jnp.sum(dkd_l[h] * kd_l[h], axis=1, keepdims=True)
            dglast = (jnp.sum(t1, axis=0, keepdims=True)
                      + jnp.sum(jnp.sum(ds_l[h] * s_l[h], axis=1, keepdims=True), axis=0, keepdims=True)
                      * hd["eglast"])
            e = (m_l[h] * p_l[h] + da_l[h] * (bc * kk_l[h])) * decay
            dgc = (jnp.sum(dqg_l[h] * qg_l[h], axis=1, keepdims=True) - t1
                   + jnp.sum(drw_l[h] * rhsw_l[h], axis=1, keepdims=True)
                   + jnp.sum(e, axis=1, keepdims=True)
                   + jnp.where(_iota((CHUNK, 1), 0) == CHUNK - 1, dglast, 0.0))
            dgr = -jnp.sum(e, axis=0, keepdims=True)
            dbc = (jnp.sum(dru_l[h] * v, axis=1, keepdims=True)
                   + jnp.sum(drw_l[h] * kn, axis=1, keepdims=True) * egam
                   + jnp.sum(da_l[h] * kk_l[h] * decay, axis=1, keepdims=True))
            dv_parts.append(dru_l[h] * bc)
            qh, dqn, dkn = hd["qh"], dqn_l[h], dkn_l[h]
            dq_parts.append(((DN_DK ** -0.5) * hd["rq"])
                            * (dqn - qh * jnp.sum(dqn * qh, axis=1, keepdims=True)))
            dk_parts.append(hd["rk"] * (dkn - kn * jnp.sum(dkn * kn, axis=1, keepdims=True)))
            dgam_col = dgam_col + dgc * (lane128 == DECAY_LANE + h).astype(F32)
            dbeta = dbeta + dbc * (lane128 == BETA_LANE + h).astype(F32)
            dgam_row = dgam_row + (row128 == DECAY_LANE + h).astype(F32) * dgr
        dnw_ref[...] += dnw
        dgam = dgam_col + _nt(f["eye"], dgam_row, hi=True)
        dg = _nn(f["upper"], dgam, hi=True)
        d_a = dg * (-f["eal"]) * jax.nn.sigmoid(f["z"]) * f["valid"]
        dal_ref[...] += jnp.sum(dg * f["g"], axis=0, keepdims=True)
        ddt_ref[...] += jnp.sum(d_a, axis=0, keepdims=True)
        d_b = dbeta * f["valid"] * f["sig"] * (1.0 - f["sig"])
        dp_ref[:, DN_CONV_CH + DN_V:DN_CONV_CH + DN_V + LANES] = (d_a + d_b).astype(dp_ref.dtype)
        dp_ref[:, DN_CONV_CH + DN_V + LANES:] = jnp.zeros((CHUNK, DN_IN_PAD - DN_IN_USED), dp_ref.dtype)
        dact = jnp.concatenate(dq_parts + dk_parts + dv_parts, axis=1)
        yc, sgc = f["yc"], f["sgc"]
        dyc = dact * (sgc * (1.0 + yc * (1.0 - sgc)))
        nxt = nxt_ref[...]
        ups = [dyc] + [_shift_up(dyc, nxt, j) for j in range(1, CONV_K)]
        dx = ups[0] * cw_ref[CONV_K - 1]
        for j in range(1, CONV_K):
            dx = dx + ups[j] * cw_ref[CONV_K - 1 - j]
        for j in range(CONV_K):
            dcw_ref[CONV_K - 1 - j] += jnp.sum(f["xin"] * ups[j], axis=0, keepdims=True)
        nxt_ref[...] = dyc[0:8]
        dp_ref[:, :DN_CONV_CH] = (dx * f["valid"]).astype(dp_ref.dtype)

    steps = nc // DN_CPS
    rev = lambda n: steps - 1 - n
    row_v = pl.BlockSpec((DN_CPS * CHUNK, DN_V), lambda n: (rev(n), 0))
    vec = pl.BlockSpec((1, LANES), lambda n: (0, 0))
    return pl.pallas_call(
        body, grid=(steps,),
        in_specs=_dn_in_specs(rev, conv_saved=True) + [
            row_v, row_v,
            pl.BlockSpec((DN_CPS, DN_HEADS, DN_DK, DN_DV), lambda n: (rev(n), 0, 0, 0)),
            pl.BlockSpec((DN_CPS, DN_HEADS, CHUNK, CHUNK), lambda n: (rev(n), 0, 0, 0))],
        out_specs=[pl.BlockSpec((DN_CPS * CHUNK, DN_IN_PAD), lambda n: (rev(n), 0)),
                   pl.BlockSpec((CONV_K, 1, DN_CONV_CH), lambda n: (0, 0, 0)), vec, vec,
                   pl.BlockSpec((1, DN_DV), lambda n: (0, 0))],
        out_shape=[jax.ShapeDtypeStruct((rows, DN_IN_PAD), BF16),
                   jax.ShapeDtypeStruct((CONV_K, 1, DN_CONV_CH), F32),
                   jax.ShapeDtypeStruct((1, LANES), F32), jax.ShapeDtypeStruct((1, LANES), F32),
                   jax.ShapeDtypeStruct((1, DN_DV), F32)],
        scratch_shapes=[pltpu.VMEM((DN_HEADS, DN_DK, DN_DV), F32), pltpu.VMEM((8, DN_CONV_CH), F32)],
        name="dn_bwd", compiler_params=_params("arbitrary"))(
            proj, conv_out, proj, proj, conv_w, alog, dtb, norm_w.reshape(1, DN_DV), o, dy, states, tinv)


def _train_step(x, tgt, wts, sh, idx):
    seq = x.shape[0]
    rows = -(-(seq + CHUNK) // ROW_ALIGN) * ROW_ALIGN
    tail = rows - seq - CHUNK
    h0 = jnp.concatenate([jnp.zeros((PAD, D_MODEL), F32), wts["meta_tokens"].astype(F32), x,
                          jnp.zeros((tail, D_MODEL), F32)], axis=0)
    tgt_p = jnp.concatenate([jnp.zeros((CHUNK, D_MODEL), F32), tgt, jnp.zeros((tail, D_MODEL), F32)],
                            axis=0)
    cos, sin = _rope_tables(rows // CHUNK)
    consts = _ret_consts()
    conv_w = wts["dn_conv_w"].reshape(CONV_K, 1, DN_CONV_CH)
    lane_pad = LANES - 2 * DN_HEADS
    alog = jnp.pad(wts["dn_a_log"].reshape(1, DN_HEADS), ((0, 0), (DECAY_LANE, lane_pad)))
    dtb = jnp.pad(wts["dn_dt_bias"].reshape(1, DN_HEADS), ((0, 0), (DECAY_LANE, lane_pad)))
    g = {}

    wts = dict(wts)
    hn0, (got,) = _rms_fwd(h0, wts["mix_norm_w"][0], "rms_mix0", ride=_Ride("gather", [sh["ret_w_in"]]))
    wts["ret_w_in"] = got.reshape(N_SHARD, D_MODEL, -1)
    proj0, got = _mm_cols(hn0, wts["ret_w_in"], "ret_in",
                          ride=_Ride("gather", [sh["ret_w_out"], sh["ffn_w_gate"]]))
    wts["ret_w_out"] = got[0].reshape(-1, D_MODEL)
    wts["ffn_w_gate"] = got[1]
    (o0, y0, st0), got = _ret_fwd(proj0, cos, sin, consts, wts["ret_gn_w"], seq,
                                  ride=_Ride("gather", [sh["ffn_w_up"], sh["ffn_w_down"]]))
    wts["ffn_w_up"], wts["ffn_w_down"] = got
    h1 = _mm(y0, wts["ret_w_out"], mode="nn", name="ret_out", resid=h0)
    hn1 = _rms_fwd(h1, wts["ffn_norm_w"][0], "rms_ffn0")
    (g0, u0, act0), got = _ffn_up(hn1, wts["ffn_w_gate"], wts["ffn_w_up"], 0, "ffn_up0",
                                  ride=_Ride("gather", [sh["dn_w_in"], sh["dn_w_out"]]))
    n_dn = sh["dn_w_in"].shape[-1]
    dn_shards = got[0].reshape(N_SHARD, D_MODEL, n_dn)
    wts["dn_w_in"] = jnp.concatenate(
        [dn_shards[j] for j in range(N_SHARD)]
        + [jnp.zeros((D_MODEL, DN_IN_PAD - N_SHARD * n_dn), dn_shards.dtype)], axis=-1)
    wts["dn_w_out"] = got[1].reshape(-1, D_MODEL)
    h2 = _ffn_down(act0, wts["ffn_w_down"], h1, 0, "ffn_down0")
    hn2 = _rms_fwd(h2, wts["mix_norm_w"][1], "rms_mix1")
    proj1 = _mm(hn2, wts["dn_w_in"], mode="nn", name="dn_in")
    o1, y1, st1, tinv, conv1 = _dn_fwd(proj1, conv_w, alog, dtb, wts["dn_norm_w"], seq)
    h3 = _mm(y1, wts["dn_w_out"], mode="nn", name="dn_out", resid=h2)
    hn3 = _rms_fwd(h3, wts["ffn_norm_w"][1], "rms_ffn1")
    g1, u1, act1 = _ffn_up(hn3, wts["ffn_w_gate"], wts["ffn_w_up"], 1, "ffn_up1")
    h4 = _ffn_down(act1, wts["ffn_w_down"], h3, 1, "ffn_down1")

    dh4, g["final_norm_w"], loss, dh4b = _final_loss(h4, wts["final_norm_w"], tgt_p, seq, "final_loss")

    layers = wts["ffn_w_gate"].shape[1]

    ffn_names = ["ffn_w_down", "ffn_w_gate", "ffn_w_up"]

    def ffn_bwd(dh_out, dhb_out, h_mid, hn, gg, uu, act, layer, prev, ride=None, last=False):
        tag = str(layer)
        res = _ffn_down_bwd(dhb_out, wts["ffn_w_down"], gg, uu, layer, "ffn_down_bwd" + tag, ride=ride)
        (dg, du), rode = res if ride is not None else (res, None)
        d_down = _ffn_wgrad(act, [dhb_out], layer, layers, prev and prev[:1], True, "ffn_dwd" + tag)
        d_gu = _ffn_wgrad(hn, [dg, du], layer, layers, prev and prev[1:], False, "ffn_dwgu" + tag)
        grads = list(d_down) + list(d_gu)
        gs = rs_grads(ffn_names, grads) if last else None
        res = _ffn_up_bwd(dg, du, wts["ffn_w_gate"], wts["ffn_w_up"], layer, h_mid, wts["ffn_norm_w"][layer],
                          dh_out, "ffn_up_bwd" + tag, ride=_Ride("pair", gs) if last else None)
        (dh_mid, d_norm, dhb_mid), sib = res if last else (res, None)
        return dh_mid, dhb_mid, grads, d_norm, rode, gs, sib

    red = {}

    def rs_grads(names, grads):
        return [gr.reshape((N_SHARD,) + sh[n].shape) for n, gr in zip(names, grads)]

    def rs_partials(names, gs, sib):
        return [_rs_pair_add(gs[t], sib[t], idx, "rs_pair_add_" + n) for t, n in enumerate(names)]

    def rs_end(names, gs, sib, others, tag):
        mine = [_rs_final_add(gs[t], sib[t], others[t], idx, "rs_final_add_" + n) for t, n in enumerate(names)]
        red.update(zip(names, _rs_share(mine, "rs_share" + tag)))

    dh3, dh3b, ffn_grads, dfn1 = ffn_bwd(dh4, dh4b, h3, hn3, g1, u1, act1, 1, None)[:4]
    dy1 = _mm(dh3b, wts["dn_w_out"], mode="nt", name="dn_out_bwd")
    d_dn_out = _mm(y1, dh3b, mode="tn", name="dn_dwo")
    dproj1, dcw, dal, ddt, g["dn_norm_w"] = _dn_bwd(proj1, conv1, o1, dy1, st1, tinv, conv_w, alog, dtb,
                                                    wts["dn_norm_w"], seq)
    d_dn_in = _mm(hn2, dproj1, mode="tn", name="dn_dwi")
    d_dn_in = jnp.stack([d_dn_in[:, j * n_dn:(j + 1) * n_dn] for j in range(N_SHARD)])
    group1 = ["dn_w_out", "dn_w_in"]
    gs1 = rs_grads(group1, [d_dn_out, d_dn_in])
    (dh2, dmn1, dh2b), sib1 = _mm_nt_rms(dproj1, wts["dn_w_in"], h2, wts["mix_norm_w"][1], dh3, "dn_in_bwd",
                                         ride=_Ride("pair", gs1))
    g["dn_conv_w"] = dcw.reshape(CONV_K, DN_CONV_CH)
    g["dn_a_log"] = dal[0, DECAY_LANE:DECAY_LANE + DN_HEADS]
    g["dn_dt_bias"] = ddt[0, DECAY_LANE:DECAY_LANE + DN_HEADS]

    dh1, dh1b, _, dfn0, others1, gs2, sib2 = ffn_bwd(
        dh2, dh2b, h1, hn1, g0, u0, act0, 0, ffn_grads,
        ride=_Ride("chips", rs_partials(group1, gs1, sib1)), last=True)
    rs_end(group1, gs1, sib1, others1, "1")
    d_ret_out = _mm(y0, dh1b, mode="tn", name="ret_dwo")
    gs2b = rs_grads(["ret_w_out"], [d_ret_out])
    dy0, sib2b = _mm(dh1b, wts["ret_w_out"], mode="nt", name="ret_out_bwd", ride=_Ride("pair", gs2b))
    group2 = ffn_names + ["ret_w_out"]
    gs2, sib2 = gs2 + gs2b, list(sib2) + list(sib2b)
    (dproj0, g["ret_gn_w"]), others2 = _ret_bwd(proj0, o0, dy0, st0, cos, sin, consts, wts["ret_gn_w"], seq,
                                                ride=_Ride("chips", rs_partials(group2, gs2, sib2)))
    rs_end(group2, gs2, sib2, others2, "2")
    d_ret_in = _mm_cols_grad(hn0, dproj0, "ret_dwi")
    gs3 = rs_grads(["ret_w_in"], [d_ret_in])
    sib3 = _rs_pair(gs3, "rs_pair3")
    (dh0, dmn0, _), others3 = _mm_cols_t_rms(dproj0, wts["ret_w_in"], h0, wts["mix_norm_w"][0], dh1, "ret_in_bwd",
                                             ride=_Ride("chips", rs_partials(["ret_w_in"], gs3, sib3)))
    rs_end(["ret_w_in"], gs3, sib3, others3, "3")

    g["ffn_norm_w"] = jnp.concatenate([dfn0, dfn1], axis=0)
    g["mix_norm_w"] = jnp.concatenate([dmn0, dmn1], axis=0)
    g["meta_tokens"] = dh0[PAD:CHUNK]
    g["final_norm_w"] = g["final_norm_w"].reshape(D_MODEL)
    g["ret_gn_w"] = g["ret_gn_w"].reshape(RET_DV)
    g["dn_norm_w"] = g["dn_norm_w"].reshape(DN_DV)
    return loss, dh0, g, red


def _mesh_pos():
    return lax.axis_index("x"), lax.axis_index("y"), lax.axis_index("c")


def _other_chips(x, y):
    return [(1 - x, y), (x, 1 - y), (1 - x, 1 - y)]


def _remote(src, dst, send_sem, recv_sem, to):
    return pltpu.make_async_remote_copy(src_ref=src, dst_ref=dst, send_sem=send_sem, recv_sem=recv_sem,
                                        device_id=to, device_id_type=MESH)


GATHER_COPIES = 7


def _gather_phase(phase, ins, outs, send_sems, recv_sems):
    x, y, c = _mesh_pos()
    me = 2 * x + y
    chips = _other_chips(x, y)
    sibling = (x, y, 1 - c)

    def cp(t, k, src, dst, to):
        i = GATHER_COPIES * t + k
        return _remote(src, dst, send_sems.at[i], recv_sems.at[i], to)

    for t in range(len(ins)):
        own = cp(t, 0, ins[t], outs[t].at[me], sibling)
        if phase == 0:
            own.start()
        if phase == 2:
            own.wait()
        for k, (px, py) in enumerate(chips):
            landed = outs[t].at[2 * px + py, c]
            theirs = outs[t].at[2 * px + py, 1 - c]
            to_chip = cp(t, 1 + k, ins[t].at[c], outs[t].at[me, c], (px, py, c))
            if phase == 0:
                to_chip.start()
            if phase == 1:
                cp(t, 1 + k, ins[t].at[c], landed, (px, py, c)).wait_recv()
                cp(t, 4 + k, landed, landed, sibling).start()
            if phase == 2:
                to_chip.wait_send()
                cp(t, 4 + k, landed, landed, sibling).wait_send()
                cp(t, 4 + k, theirs, theirs, sibling).wait_recv()


def _chips_phase(phase, ins, outs, send_sems, recv_sems):
    x, y, c = _mesh_pos()
    for t in range(len(ins)):
        for k, (px, py) in enumerate(_other_chips(x, y)):
            cp = _remote(ins[t].at[2 * px + py], outs[t].at[k], send_sems.at[3 * t + k], recv_sems.at[3 * t + k],
                         (px, py, c))
            if phase == 0:
                cp.start()
            if phase == 2:
                cp.wait()


class _Ride:
    def __init__(self, kind, arrays):
        self.kind, self.arrays = kind, list(arrays)
        nt = len(self.arrays)
        if kind == "gather":
            self.phase_fn, n_sem = _gather_phase, GATHER_COPIES * nt
            self.out_shape = [jax.ShapeDtypeStruct((N_SHARD,) + a.shape, a.dtype) for a in self.arrays]
        elif kind == "pair":
            self.phase_fn, n_sem = _pair_phase, nt
            self.out_shape = [jax.ShapeDtypeStruct(a.shape[:1] + a.shape[2:], a.dtype) for a in self.arrays]
        else:
            self.phase_fn, n_sem = _chips_phase, 3 * nt
            self.out_shape = [jax.ShapeDtypeStruct((3,) + a.shape[1:], a.dtype) for a in self.arrays]
        self.in_specs, self.out_specs = [ANY] * nt, [ANY] * nt
        self.scratch = [pltpu.SemaphoreType.DMA((n_sem,)), pltpu.SemaphoreType.DMA((n_sem,))]

    def emit(self, step, nsteps, ins, outs, send_sems, recv_sems, before):
        mid = max(0, min((7 * nsteps) // 8, nsteps - 2))
        todo = [(0, 0), (1, mid)] if before else [(2, nsteps - 1)]
        for phase, at in todo:
            if phase == 1 and self.kind != "gather":
                continue

            @pl.when(step == at)
            def _(phase=phase):
                self.phase_fn(phase, ins, outs, send_sems, recv_sems)


def _gather_small(blk):
    r, wd = blk.shape

    def body(b_ref, out_ref, send_sems, recv_sems):
        x, y, c = _mesh_pos()
        chips = _other_chips(x, y)
        out_ref[2 * x + y] = b_ref[...]
        sends = [_remote(b_ref, out_ref.at[2 * x + y], send_sems.at[k], recv_sems.at[k], (px, py, c))
                 for k, (px, py) in enumerate(chips)]
        for cp in sends:
            cp.start()
        for k, (px, py) in enumerate(chips):
            _remote(b_ref, out_ref.at[2 * px + py], send_sems.at[k], recv_sems.at[k], (px, py, c)).wait_recv()
        for cp in sends:
            cp.wait_send()

    return pl.pallas_call(
        body, out_shape=jax.ShapeDtypeStruct((4, r, wd), blk.dtype), in_specs=[VMEM_SPEC], out_specs=VMEM_SPEC,
        scratch_shapes=[pltpu.SemaphoreType.DMA((3,)), pltpu.SemaphoreType.DMA((3,))],
        name="gather_small")(blk)


def _allreduce_small(blk):
    r, wd = blk.shape
    rels = [(dx, dy, dc) for dx in (0, 1) for dy in (0, 1) for dc in (0, 1) if dx or dy or dc]

    def body(b_ref, out_ref, buf_ref, send_sems, recv_sems):
        x, y, c = _mesh_pos()

        def peer(rel):
            dx, dy, dc = rel
            return (1 - x if dx else x, 1 - y if dy else y, 1 - c if dc else c)

        me = 4 * x + 2 * y + c
        buf_ref[me] = b_ref[...]
        sends = [_remote(b_ref, buf_ref.at[me], send_sems.at[k], recv_sems.at[k], peer(rel))
                 for k, rel in enumerate(rels)]
        for cp in sends:
            cp.start()
        for k, rel in enumerate(rels):
            px, py, pc = peer(rel)
            _remote(b_ref, buf_ref.at[4 * px + 2 * py + pc], send_sems.at[k], recv_sems.at[k],
                    (px, py, pc)).wait_recv()
        for cp in sends:
            cp.wait_send()
        acc = buf_ref[0]
        for d in range(1, 8):
            acc = acc + buf_ref[d]
        out_ref[...] = acc

    return pl.pallas_call(
        body, out_shape=jax.ShapeDtypeStruct((r, wd), blk.dtype), in_specs=[VMEM_SPEC], out_specs=VMEM_SPEC,
        scratch_shapes=[pltpu.VMEM((8, r, wd), blk.dtype), pltpu.SemaphoreType.DMA((7,)),
                        pltpu.SemaphoreType.DMA((7,))],
        name="allreduce_small")(blk)


def _rs_pair(gs, name):
    ride = _Ride("pair", gs)

    def body(*refs):
        nt = len(gs)
        for phase in (0, 2):
            _pair_phase(phase, refs[:nt], refs[nt:2 * nt], *refs[2 * nt:])

    return pl.pallas_call(body, out_shape=ride.out_shape, in_specs=ride.in_specs, out_specs=ride.out_specs,
                          scratch_shapes=ride.scratch, name=name)(*gs)


def _pair_phase(phase, ins, outs, send_sems, recv_sems):
    x, y, c = _mesh_pos()
    for t in range(len(ins)):
        cp = _remote(ins[t].at[:, 1 - c], outs[t], send_sems.at[t], recv_sems.at[t], (x, y, 1 - c))
        if phase == 0:
            cp.start()
        if phase == 2:
            cp.wait()


def _rs_tile(a, b):
    return _div_tile(a, 512 if b <= 1024 else 256, 16)


def _rs_pair_add(g, a, idx, name):
    _, _, rows, cols = g.shape
    tr = _rs_tile(rows, cols)

    def body(s_ref, g_ref, a_ref, p_ref):
        p_ref[...] = (g_ref[...] + a_ref[...]).astype(p_ref.dtype)

    blk = pl.BlockSpec((None, tr, cols), lambda j, i, s: (j, i, 0))
    spec = pltpu.PrefetchScalarGridSpec(
        num_scalar_prefetch=1, grid=(N_SHARD, rows // tr),
        in_specs=[pl.BlockSpec((None, None, tr, cols), lambda j, i, s: (j, s[0], i, 0)), blk], out_specs=blk)
    return pl.pallas_call(
        body, grid_spec=spec, out_shape=jax.ShapeDtypeStruct((N_SHARD, rows, cols), BF16), name=name,
        compiler_params=_params("parallel", "parallel"))(idx, g, a)


def _rs_final_add(g, a, b, idx, name):
    _, _, rows, cols = g.shape
    tr = _rs_tile(rows, cols)

    def body(s_ref, g_ref, a_ref, b0_ref, b1_ref, b2_ref, f_ref):
        own = g_ref[...] + a_ref[...]
        f_ref[...] = ((own + b0_ref[...].astype(F32)) + b1_ref[...].astype(F32)) + b2_ref[...].astype(F32)

    def b_spec(k):
        return pl.BlockSpec((None, tr, cols), lambda i, s: (k, i, 0))

    spec = pltpu.PrefetchScalarGridSpec(
        num_scalar_prefetch=1, grid=(rows // tr,),
        in_specs=[pl.BlockSpec((None, None, tr, cols), lambda i, s: (s[1], s[0], i, 0)),
                  pl.BlockSpec((None, tr, cols), lambda i, s: (s[1], i, 0)), b_spec(0), b_spec(1), b_spec(2)],
        out_specs=pl.BlockSpec((None, tr, cols), lambda i, s: (s[0], i, 0)))
    return pl.pallas_call(
        body, grid_spec=spec, out_shape=jax.ShapeDtypeStruct((2, rows, cols), F32), name=name,
        compiler_params=_params("parallel"))(idx, g, a, b, b, b)


def _rs_share(fs, name):
    nt = len(fs)

    def body(*refs):
        outs = refs[nt:2 * nt]
        send_sems, recv_sems = refs[2 * nt:]
        x, y, c = _mesh_pos()
        cps = [_remote(outs[t].at[c], outs[t].at[c], send_sems.at[t], recv_sems.at[t], (x, y, 1 - c))
               for t in range(nt)]
        for cp in cps:
            cp.start()
        for cp in cps:
            cp.wait()

    return pl.pallas_call(
        body, out_shape=[jax.ShapeDtypeStruct(f.shape, f.dtype) for f in fs],
        in_specs=[ANY] * nt, out_specs=[ANY] * nt, input_output_aliases={t: t for t in range(nt)},
        scratch_shapes=[pltpu.SemaphoreType.DMA((nt,)), pltpu.SemaphoreType.DMA((nt,))], name=name)(*fs)


def _adamw(w, g, m, v, name):
    lead, rows, cols = w.shape
    tr = rows // 4 if rows % 32 == 0 else rows

    def body(w_ref, g_ref, m_ref, v_ref, go_ref, d_ref, mo_ref, vo_ref):
        gv = g_ref[...]
        go_ref[...] = gv
        mn = ADAM_B1 * m_ref[...] + (1.0 - ADAM_B1) * gv
        vn = ADAM_B2 * v_ref[...] + (1.0 - ADAM_B2) * (gv * gv)
        m_hat = mn / (1.0 - ADAM_B1 ** ADAM_STEP)
        v_hat = vn / (1.0 - ADAM_B2 ** ADAM_STEP)
        d_ref[...] = -ADAM_LR * (m_hat / (jnp.sqrt(v_hat) + ADAM_EPS) + ADAM_WD * w_ref[...])
        mo_ref[...] = mn
        vo_ref[...] = vn

    blk = pl.BlockSpec((None, tr, cols), lambda l, i: (l, i, 0))
    out = jax.ShapeDtypeStruct((lead, rows, cols), F32)
    return pl.pallas_call(
        body, grid=(lead, rows // tr), in_specs=[blk] * 4, out_specs=[blk] * 4, out_shape=[out] * 4, name=name,
        compiler_params=_params("parallel", "parallel"))(w, g, m, v)


BIG = ["ret_w_in", "ret_w_out", "dn_w_in", "dn_w_out", "ffn_w_gate", "ffn_w_up", "ffn_w_down"]
TRANSPOSED_AT_BOUNDARY = {"dn_w_in": True, "ffn_w_gate": False, "ffn_w_up": False}
SMALL =["meta_tokens", "mix_norm_w", "ffn_norm_w", "ret_gn_w", "dn_conv_w", "dn_a_log", "dn_dt_bias",
         "dn_norm_w", "final_norm_w"]
SMALL_SHARDED = {"meta_tokens", "dn_conv_w", "dn_norm_w"}
ORDER = ["meta_tokens", "mix_norm_w", "ffn_norm_w", "ret_w_in", "ret_gn_w", "ret_w_out", "dn_w_in",
         "dn_conv_w", "dn_a_log", "dn_dt_bias", "dn_norm_w", "dn_w_out", "ffn_w_gate", "ffn_w_up",
         "ffn_w_down", "final_norm_w"]


def _halves(a):
    return a.reshape(2, -1, a.shape[-1])


def _pack_lanes(parts, align=8):
    flat = jnp.concatenate([p.reshape(-1) for p in parts])
    flat = jnp.pad(flat, (0, -flat.shape[0] % (align * LANES)))
    return flat.reshape(-1, LANES)


def _unpack(buf, shapes):
    lead = buf.shape[:-2]
    flat = buf.reshape(lead + (-1,))
    out, off = [], 0
    for shp in shapes:
        size = math.prod(shp)
        out.append(flat[..., off:off + size].reshape(lead + tuple(shp)))
        off += size
    return out


def _join_cols(shards):
    return jnp.concatenate([shards[j] for j in range(N_SHARD)], axis=-1)


def kernel(x, meta_tokens, mix_norm_w, ffn_norm_w, ret_w_in, ret_gn_w, ret_w_out, dn_w_in, dn_conv_w, dn_a_log, dn_dt_bias, dn_norm_w, dn_w_out, ffn_w_gate, ffn_w_up, ffn_w_down, final_norm_w, loss_target, m_meta_tokens, m_mix_norm_w, m_ffn_norm_w, m_ret_w_in, m_ret_gn_w, m_ret_w_out, m_dn_w_in, m_dn_conv_w, m_dn_a_log, m_dn_dt_bias, m_dn_norm_w, m_dn_w_out, m_ffn_w_gate, m_ffn_w_up, m_ffn_w_down, m_final_norm_w, v_meta_tokens, v_mix_norm_w, v_ffn_norm_w, v_ret_w_in, v_ret_gn_w, v_ret_w_out, v_dn_w_in, v_dn_conv_w, v_dn_a_log, v_dn_dt_bias, v_dn_norm_w, v_dn_w_out, v_ffn_w_gate, v_ffn_w_up, v_ffn_w_down, v_final_norm_w):
    w = dict(meta_tokens=meta_tokens, mix_norm_w=mix_norm_w, ffn_norm_w=ffn_norm_w, ret_w_in=ret_w_in,
             ret_gn_w=ret_gn_w, ret_w_out=ret_w_out, dn_w_in=dn_w_in, dn_conv_w=dn_conv_w, dn_a_log=dn_a_log,
             dn_dt_bias=dn_dt_bias, dn_norm_w=dn_norm_w, dn_w_out=dn_w_out, ffn_w_gate=ffn_w_gate,
             ffn_w_up=ffn_w_up, ffn_w_down=ffn_w_down, final_norm_w=final_norm_w)
    m = dict(meta_tokens=m_meta_tokens, mix_norm_w=m_mix_norm_w, ffn_norm_w=m_ffn_norm_w, ret_w_in=m_ret_w_in,
             ret_gn_w=m_ret_gn_w, ret_w_out=m_ret_w_out, dn_w_in=m_dn_w_in, dn_conv_w=m_dn_conv_w,
             dn_a_log=m_dn_a_log, dn_dt_bias=m_dn_dt_bias, dn_norm_w=m_dn_norm_w, dn_w_out=m_dn_w_out,
             ffn_w_gate=m_ffn_w_gate, ffn_w_up=m_ffn_w_up, ffn_w_down=m_ffn_w_down, final_norm_w=m_final_norm_w)
    v = dict(meta_tokens=v_meta_tokens, mix_norm_w=v_mix_norm_w, ffn_norm_w=v_ffn_norm_w, ret_w_in=v_ret_w_in,
             ret_gn_w=v_ret_gn_w, ret_w_out=v_ret_w_out, dn_w_in=v_dn_w_in, dn_conv_w=v_dn_conv_w,
             dn_a_log=v_dn_a_log, dn_dt_bias=v_dn_dt_bias, dn_norm_w=v_dn_norm_w, dn_w_out=v_dn_w_out,
             ffn_w_gate=v_ffn_w_gate, ffn_w_up=v_ffn_w_up, ffn_w_down=v_ffn_w_down, final_norm_w=v_final_norm_w)
    mx, my, mc = _mesh_pos()
    chip = 2 * mx + my

    sm_names = [n for n in SMALL if n in SMALL_SHARDED]
    sm_gathered = _unpack(_gather_small(_pack_lanes([w[n] for n in sm_names])), [w[n].shape for n in sm_names])
    full = {n: _join_cols(sm_gathered[i]) for i, n in enumerate(sm_names)}
    wts = {
        "meta_tokens": full["meta_tokens"], "mix_norm_w": mix_norm_w, "ffn_norm_w": ffn_norm_w,
        "ret_gn_w": ret_gn_w[0], "final_norm_w": final_norm_w, "dn_conv_w": full["dn_conv_w"][0],
        "dn_a_log": dn_a_log[0], "dn_dt_bias": dn_dt_bias[0], "dn_norm_w": full["dn_norm_w"][0],
    }
    idx = jnp.stack([mc, chip]).astype(jnp.int32)
    shards = {n: _halves(w[n].astype(MXU_DTYPE)) for n in BIG}
    loss_part, dh0, g, reduced = _train_step(x[0], loss_target[0], wts, shards, idx)
    seq = x.shape[1]
    grad_x = dh0[CHUNK:CHUNK + seq].reshape(x.shape)
    gsh = {}

    small_full_shapes = [g[n].shape for n in SMALL] + [(1,)]
    red = _unpack(_allreduce_small(_pack_lanes([g[n] for n in SMALL] + [loss_part[0, :1]])), small_full_shapes)
    loss = red[-1][0]
    for i, n in enumerate(SMALL):
        gn = red[i]
        if n in SMALL_SHARDED:
            width = w[n].shape[-1]
            gn = lax.dynamic_slice_in_dim(gn, chip * width, width, axis=gn.ndim - 1)
        gsh[n] = gn.reshape(w[n].shape)

    delta, new_m, new_v = {}, {}, {}
    for n in BIG:
        shp = w[n].shape
        if n in TRANSPOSED_AT_BOUNDARY and TRANSPOSED_AT_BOUNDARY[n]:
            view = lambda a: jnp.swapaxes(a, 1, 2).reshape(1, -1, LANES)
            back = lambda a: jnp.swapaxes(a.reshape(shp[0], shp[2], shp[1]), 1, 2)
        elif n in TRANSPOSED_AT_BOUNDARY:
            view = back = lambda a: jnp.swapaxes(a, 1, 2)
        else:
            view = back = lambda a: a
        res = _adamw(view(w[n]), view(reduced[n].reshape(shp)), view(m[n]), view(v[n]), "adamw_" + n)
        gsh[n], delta[n], new_m[n], new_v[n] = [back(r) for r in res]
    sm_local_shapes = [w[n].shape for n in SMALL]
    _, d_, m_, v_ = _adamw(*[_pack_lanes([t[n] for n in SMALL])[None] for t in (w, gsh, m, v)], "adamw_small")
    d_, m_, v_ = d_[0], m_[0], v_[0]
    for n, dd, mm, vv in zip(SMALL, _unpack(d_, sm_local_shapes), _unpack(m_, sm_local_shapes),
                             _unpack(v_, sm_local_shapes)):
        delta[n], new_m[n], new_v[n] = dd, mm, vv

    return (loss, grad_x, *[gsh[n] for n in ORDER], *[delta[n] for n in ORDER],
            *[new_m[n] for n in ORDER], *[new_v[n] for n in ORDER])
```

```python
import math

import jax
import jax.numpy as jnp
from jax import lax
from jax.experimental import pallas as pl
from jax.experimental.pallas import tpu as pltpu

F32 = jnp.float32
BF16 = jnp.bfloat16
MXU_DTYPE = BF16

D_MODEL = 1024
N_META = 16
CHUNK = 64
PAD = CHUNK - N_META
RMS_EPS = 1e-6
RET_HEADS, RET_DK, RET_DV = 4, 256, 512
RET_QK, RET_V = RET_HEADS * RET_DK, RET_HEADS * RET_DV
RET_IN = 2 * RET_QK + 2 * RET_V
ROPE_BASE = 10000.0
DN_HEADS, DN_DK, DN_DV = 8, 128, 256
DN_QK, DN_V = DN_HEADS * DN_DK, DN_HEADS * DN_DV
DN_CONV_CH = 2 * DN_QK + DN_V
DN_IN = DN_CONV_CH + DN_V + 2 * DN_HEADS
LANES = 128
DN_IN_USED = DN_CONV_CH + DN_V + LANES
DN_IN_PAD = DN_IN_USED + LANES
CONV_K = 4
FFN_HIDDEN = 2816
ADAM_LR, ADAM_B1, ADAM_B2, ADAM_EPS, ADAM_WD, ADAM_STEP = 0.001, 0.9, 0.999, 1e-08, 0.01, 10

ROW_ALIGN = 256
VMEM_LIMIT = 56 * 1024 * 1024
MESH = pl.DeviceIdType.MESH
ANY = pl.BlockSpec(memory_space=pl.ANY)
VMEM_SPEC = pl.BlockSpec(memory_space=pltpu.VMEM)
_HI = lax.Precision.HIGHEST


def _params(*sem):
    return pltpu.CompilerParams(dimension_semantics=sem, vmem_limit_bytes=VMEM_LIMIT)


def _dg(a, b, ca, cb, hi):
    dims = (((ca,), (cb,)), ((), ()))

    def dot(p, q):
        return lax.dot_general(p, q, dims, preferred_element_type=F32)

    if not hi:
        return dot(a.astype(MXU_DTYPE), b.astype(MXU_DTYPE))
    if MXU_DTYPE == F32:
        return lax.dot_general(a, b, dims, precision=_HI, preferred_element_type=F32)
    a_hi, b_hi = a.astype(MXU_DTYPE), b.astype(MXU_DTYPE)
    a_lo = (a - a_hi.astype(F32)).astype(MXU_DTYPE)
    b_lo = (b - b_hi.astype(F32)).astype(MXU_DTYPE)
    return dot(a_hi, b_hi) + (dot(a_hi, b_lo) + dot(a_lo, b_hi))


def _nn(a, b, hi=False):
    return _dg(a, b, 1, 0, hi)


def _nt(a, b, hi=False):
    return _dg(a, b, 1, 1, hi)


def _tn(a, b, hi=False):
    return _dg(a, b, 0, 0, hi)


def _iota(shape, dim):
    return lax.broadcasted_iota(jnp.int32, shape, dim)


def _valid_rows(first_row, rows, seq):
    r = first_row + _iota((rows, 1), 0)
    return ((r >= PAD) & (r < CHUNK + seq)).astype(F32)


def _rope(t, cs, sn):
    half = t.shape[-1] // 2
    t1, t2 = t[:, :half], t[:, half:]
    return jnp.concatenate([t1 * cs - t2 * sn, t1 * sn + t2 * cs], axis=1)


def _rope_bwd(d, cs, sn):
    half = d.shape[-1] // 2
    d1, d2 = d[:, :half], d[:, half:]
    return jnp.concatenate([d1 * cs + d2 * sn, d2 * cs - d1 * sn], axis=1)


def _col(x, idx):
    oh = (_iota((1, x.shape[1]), 1) == idx).astype(F32)
    return jnp.sum(x * oh, axis=1, keepdims=True)


def _row(x, idx):
    oh = (_iota((x.shape[0], 1), 0) == idx).astype(F32)
    return jnp.sum(x * oh, axis=0, keepdims=True)


def _shift_down(x, halo8, k):
    xr = pltpu.roll(x, k, 0)
    hr = pltpu.roll(halo8, k, 0)
    first = jnp.where(_iota((8, 1), 0) < k, hr, xr[0:8])
    return jnp.concatenate([first, xr[8:]], axis=0)


def _shift_up(x, next8, j):
    rows = x.shape[0]
    xr = pltpu.roll(x, rows - j, 0)
    nr = pltpu.roll(next8, 8 - j, 0)
    last = jnp.where(_iota((8, 1), 0) >= 8 - j, nr, xr[rows - 8:])
    return jnp.concatenate([xr[:rows - 8], last], axis=0)


def _gated_norm(o, gate, w):
    r = lax.rsqrt(jnp.mean(o * o, axis=-1, keepdims=True) + RMS_EPS)
    return o * r * w * (gate * jax.nn.sigmoid(gate))


def _gated_norm_bwd(dy, o, gate, w):
    r = lax.rsqrt(jnp.mean(o * o, axis=-1, keepdims=True) + RMS_EPS)
    nrm = o * r
    sg = jax.nn.sigmoid(gate)
    sl = gate * sg
    dgate = dy * nrm * w * (sg * (1.0 + gate * (1.0 - sg)))
    dn = dy * w * sl
    dw = jnp.sum(dy * nrm * sl, axis=0, keepdims=True)
    do = r * (dn - nrm * jnp.mean(dn * nrm, axis=-1, keepdims=True))
    return do, dgate, dw


def _softplus(z):
    return jnp.maximum(z, 0.0) + jnp.log(1.0 + jnp.exp(-jnp.abs(z)))


def _row_tile(rows, cap=768):
    for t in (768, 512, 256, 128, 64, 32, 16, 8):
        if t <= cap and rows % t == 0:
            return t
    return rows


TILE_BUDGET = 44 * 1024 * 1024


def _fit_rows(rows, row_bytes, fixed_bytes, value_row_bytes):
    best = None
    for t in range(LANES, rows + 1, LANES):
        if rows % t == 0 and 2 * (row_bytes * t + fixed_bytes) + value_row_bytes * t <= TILE_BUDGET:
            best = t
    return best or _row_tile(rows, 256)


def _div_tile(n, cap, mult):
    best = None
    for t in range(mult, min(cap, n) + 1, mult):
        if n % t == 0:
            best = t
    return best or n


def _col_tile(cols, cap=1536):
    best = None
    for t in range(LANES, min(cap, cols) + 1, LANES):
        if cols % t == 0:
            best = t
    return best or cols


def _embed(x, tgt, meta, rows, ride=None):
    seq, d = x.shape
    n_tok = seq // CHUNK

    def body(x_ref, t_ref, m_ref, h_ref, tp_ref):
        i = pl.program_id(0)

        @pl.when(i == 0)
        def _():
            h_ref[...] = jnp.concatenate([jnp.zeros((PAD, d), F32), m_ref[...]], axis=0)
            tp_ref[...] = jnp.zeros_like(tp_ref)

        @pl.when((i >= 1) & (i <= n_tok))
        def _():
            h_ref[...] = x_ref[...]
            tp_ref[...] = t_ref[...]

        @pl.when(i > n_tok)
        def _():
            h_ref[...] = jnp.zeros_like(h_ref)
            tp_ref[...] = jnp.zeros_like(tp_ref)

    tok = pl.BlockSpec((CHUNK, d), lambda i: (jnp.clip(i - 1, 0, n_tok - 1), 0))
    out = pl.BlockSpec((CHUNK, d), lambda i: (i, 0))
    res, rode = _pcall(body, [x, tgt, meta], grid=(rows // CHUNK,),
                       in_specs=[tok, tok, pl.BlockSpec((N_META, d), lambda i: (0, 0))], out_specs=[out, out],
                       out_shape=[jax.ShapeDtypeStruct((rows, d), F32)] * 2, name="embed", sem=("parallel",),
                       ride=ride)
    return res if ride is None else (res, rode)


def _rms_fwd(h, w, name, ride=None):
    rows, d = h.shape
    tm = _row_tile(rows)

    def body(h_ref, w_ref, o_ref):
        x = h_ref[...]
        r = lax.rsqrt(jnp.mean(x * x, axis=-1, keepdims=True) + RMS_EPS)
        o_ref[...] = (x * r * w_ref[...]).astype(o_ref.dtype)

    res, rode = _pcall(body, [h, w.reshape(1, d)], grid=(rows // tm,),
                       in_specs=[pl.BlockSpec((tm, d), lambda i: (i, 0)), pl.BlockSpec((1, d), lambda i: (0, 0))],
                       out_specs=[pl.BlockSpec((tm, d), lambda i: (i, 0))],
                       out_shape=[jax.ShapeDtypeStruct((rows, d), BF16)], name=name, sem=("parallel",), ride=ride)
    return res[0] if ride is None else (res[0], rode)


def _gmm_rms(name, grid, args, in_specs, row_spec, fn, h, w, resid, row_axis, red_axis=None, ride=None):
    m, d = h.shape
    n_in = len(args)
    vec = pl.BlockSpec((1, d), lambda *g: (0, 0))

    def body(*refs):
        ins = refs[:n_in]
        h_ref, w_ref, r_ref, dh_ref, dw_ref, dh16_ref = refs[n_in:]
        part = fn(*ins)
        row = pl.program_id(row_axis)

        def finish(dy):
            x = h_ref[...]
            r = lax.rsqrt(jnp.mean(x * x, axis=-1, keepdims=True) + RMS_EPS)
            xh = x * r
            dxh = dy * w_ref[...]
            dh = r_ref[...] + r * (dxh - xh * jnp.mean(dxh * xh, axis=-1, keepdims=True))
            dh_ref[...] = dh
            dh16_ref[...] = dh.astype(dh16_ref.dtype)
            dwp = jnp.sum(dy * xh, axis=0, keepdims=True)

            @pl.when(row == 0)
            def _():
                dw_ref[...] = dwp

            @pl.when(row > 0)
            def _():
                dw_ref[...] += dwp

        if red_axis is None:
            finish(part)
            return
        k = pl.program_id(red_axis)

        @pl.when(k == 0)
        def _():
            dh_ref[...] = part

        @pl.when(k > 0)
        def _():
            dh_ref[...] += part

        @pl.when(k == grid[red_axis] - 1)
        def _():
            finish(dh_ref[...])

    res, rode = _pcall(body, list(args) + [h, w.reshape(1, d), resid], grid=grid,
                       in_specs=list(in_specs) + [row_spec, vec, row_spec], out_specs=[row_spec, vec, row_spec],
                       out_shape=[jax.ShapeDtypeStruct((m, d), F32), jax.ShapeDtypeStruct((1, d), F32),
                                  jax.ShapeDtypeStruct((m, d), BF16)],
                       name=name, sem=("arbitrary",) * len(grid), ride=ride)
    return res if ride is None else (res, rode)


def _final_loss(h, w, tgt, seq, name):
    rows, d = h.shape
    tm = _row_tile(rows)

    def body(h_ref, w_ref, t_ref, dh_ref, dw_ref, loss_ref, dh16_ref):
        i = pl.program_id(0)
        r_idx = i * tm + _iota((tm, 1), 0)
        m = ((r_idx >= CHUNK) & (r_idx < CHUNK + seq)).astype(F32)
        x = h_ref[...]
        wv = w_ref[...]
        r = lax.rsqrt(jnp.mean(x * x, axis=-1, keepdims=True) + RMS_EPS)
        xh = x * r
        err = (xh * wv - t_ref[...]) * m
        lpart = 0.5 * jnp.sum(jnp.mean(err * err, axis=-1, keepdims=True), axis=0, keepdims=True)
        dyv = err * (1.0 / d)
        dxh = dyv * wv
        dh = r * (dxh - xh * jnp.mean(dxh * xh, axis=-1, keepdims=True))
        dh_ref[...] = dh
        dh16_ref[...] = dh.astype(dh16_ref.dtype)
        part = jnp.sum(dyv * xh, axis=0, keepdims=True)

        @pl.when(i == 0)
        def _():
            dw_ref[...] = part
            loss_ref[...] = jnp.broadcast_to(lpart, loss_ref.shape)

        @pl.when(i > 0)
        def _():
            dw_ref[...] += part
            loss_ref[...] += jnp.broadcast_to(lpart, loss_ref.shape)

    blk = pl.BlockSpec((tm, d), lambda i: (i, 0))
    vec = pl.BlockSpec((1, d), lambda i: (0, 0))
    return pl.pallas_call(
        body, grid=(rows // tm,), in_specs=[blk, vec, blk],
        out_specs=[blk, vec, pl.BlockSpec((1, LANES), lambda i: (0, 0)), blk],
        out_shape=[jax.ShapeDtypeStruct((rows, d), F32), jax.ShapeDtypeStruct((1, d), F32),
                   jax.ShapeDtypeStruct((1, LANES), F32), jax.ShapeDtypeStruct((rows, d), BF16)],
        name=name, compiler_params=_params("arbitrary"))(h, w.reshape(1, d), tgt)


def _isz(x):
    return jnp.dtype(x.dtype).itemsize


def _mm(a, b, *, mode, name, out_dtype=F32, resid=None, col_cap=1536, ride=None):
    if mode == "tn":
        m, k = a.shape
        n = b.shape[1]
        tn = _col_tile(n, col_cap)
        tm = _fit_rows(m, k * _isz(a) + tn * _isz(b), (3 * k * tn * 4) // 2, 2 * (k + tn))

        def body_tn(a_ref, b_ref, o_ref):
            i = pl.program_id(1)
            part = _tn(a_ref[...], b_ref[...])

            @pl.when(i == 0)
            def _():
                o_ref[...] = part

            @pl.when(i > 0)
            def _():
                o_ref[...] += part

        return pl.pallas_call(
            body_tn, grid=(n // tn, m // tm),
            in_specs=[pl.BlockSpec((tm, k), lambda j, i: (i, 0)),
                      pl.BlockSpec((tm, tn), lambda j, i: (i, j))],
            out_specs=pl.BlockSpec((k, tn), lambda j, i: (0, j)),
            out_shape=jax.ShapeDtypeStruct((k, n), F32), name=name,
            compiler_params=_params("parallel", "arbitrary"))(a, b)

    m, ka = a.shape
    n = b.shape[1] if mode == "nn" else b.shape[0]
    has_resid = resid is not None
    tn = _col_tile(n, col_cap)
    tm = _fit_rows(m, ka * _isz(a) + tn * (jnp.dtype(out_dtype).itemsize + (4 if has_resid else 0)),
                   ka * tn * _isz(b), 2 * ka + 8 * tn)

    def body(*refs):
        if has_resid:
            a_ref, b_ref, r_ref, o_ref = refs
        else:
            a_ref, b_ref, o_ref = refs
        acc = _nn(a_ref[...], b_ref[...]) if mode == "nn" else _nt(a_ref[...], b_ref[...])
        if has_resid:
            acc = acc + r_ref[...]
        o_ref[...] = acc.astype(o_ref.dtype)

    b_spec = (pl.BlockSpec((b.shape[0], tn), lambda j, i: (0, j)) if mode == "nn"
              else pl.BlockSpec((tn, b.shape[1]), lambda j, i: (j, 0)))
    o_spec = pl.BlockSpec((tm, tn), lambda j, i: (i, j))
    in_specs = [pl.BlockSpec((tm, ka), lambda j, i: (i, 0)), b_spec]
    args = [a, b]
    if has_resid:
        in_specs.append(o_spec)
        args.append(resid)
    res, rode = _pcall(body, args, grid=(n // tn, m // tm), in_specs=in_specs, out_specs=[o_spec],
                       out_shape=[jax.ShapeDtypeStruct((m, n), out_dtype)], name=name,
                       sem=("parallel", "parallel"), ride=ride)
    return res[0] if ride is None else (res[0], rode)


N_SHARD = 4


def _gmm(name, grid, args, in_specs, out_specs, out_shape, fn, red_axis=None, init_arg=None, aliases=None,
         ride=None):
    n_in = len(args)
    single = not isinstance(out_shape, (list, tuple))
    out_specs = [out_specs] if single else list(out_specs)
    out_shape = [out_shape] if single else list(out_shape)

    def body(*refs):
        _gmm_step(fn, refs[:n_in], refs[n_in:], red_axis, init_arg)

    sem = tuple("arbitrary" if ax == red_axis else "parallel" for ax in range(len(grid)))
    res, rode = _pcall(body, args, grid=grid, in_specs=in_specs, out_specs=out_specs, out_shape=out_shape,
                       name=name, sem=sem, aliases=aliases, ride=ride)
    ours = res[0] if single else res
    return ours if ride is None else (ours, rode)


def _gmm_step(fn, ins, outs, red_axis, init_arg):
    parts = fn(*ins)
    if red_axis is None:
        for o_ref, p in zip(outs, parts):
            o_ref[...] = p.astype(o_ref.dtype)
        return
    k = pl.program_id(red_axis)

    @pl.when(k == 0)
    def _():
        for idx, (o_ref, p) in enumerate(zip(outs, parts)):
            o_ref[...] = p + ins[init_arg][...] if (idx == 0 and init_arg is not None) else p

    @pl.when(k > 0)
    def _():
        for o_ref, p in zip(outs, parts):
            o_ref[...] += p


def _ride_body(ride, grid, n_in, n_out, n_scratch, body):
    n_rin, n_rout = len(ride.arrays), len(ride.out_shape)
    nsteps = math.prod(grid)

    def wrapped(*refs):
        ins = refs[:n_in]
        r_ins = refs[n_in:n_in + n_rin]
        o0 = n_in + n_rin
        outs = refs[o0:o0 + n_out]
        r_outs = refs[o0 + n_out:o0 + n_out + n_rout]
        s0 = o0 + n_out + n_rout
        scratch = refs[s0:s0 + n_scratch]
        send_sems, recv_sems = refs[-2:]
        step = pl.program_id(0)
        for ax in range(1, len(grid)):
            step = step * grid[ax] + pl.program_id(ax)
        ride.emit(step, nsteps, r_ins, r_outs, send_sems, recv_sems, before=True)
        body(*ins, *outs, *scratch)
        ride.emit(step, nsteps, r_ins, r_outs, send_sems, recv_sems, before=False)

    return wrapped


def _pcall(body, args, *, grid, in_specs, out_specs, out_shape, name, sem, scratch=(), aliases=None, ride=None):
    if ride is None:
        res = pl.pallas_call(body, grid=grid, in_specs=list(in_specs), out_specs=list(out_specs),
                             out_shape=list(out_shape), scratch_shapes=list(scratch), name=name,
                             input_output_aliases=aliases or {}, compiler_params=_params(*sem))(*args)
        return res, None
    n_in, n_out = len(args), len(out_shape)
    res = pl.pallas_call(
        _ride_body(ride, grid, n_in, n_out, len(scratch), body), grid=grid,
        in_specs=list(in_specs) + ride.in_specs, out_specs=list(out_specs) + ride.out_specs,
        out_shape=list(out_shape) + ride.out_shape, scratch_shapes=list(scratch) + ride.scratch, name=name,
        input_output_aliases=aliases or {},
        compiler_params=_params(*(("arbitrary",) * len(grid))))(*args, *ride.arrays)
    return res[:n_out], res[n_out:]


def _mm_cols(a, ws, name, ride=None):
    m, k = a.shape
    n = ws.shape[2]
    tm = _fit_rows(m, k * _isz(a) + n * 4, k * n * _isz(ws), 4 * n)
    return _gmm(name, (N_SHARD, m // tm), [a, ws],
                [pl.BlockSpec((tm, k), lambda j, i: (i, 0)), pl.BlockSpec((None, k, n), lambda j, i: (j, 0, 0))],
                pl.BlockSpec((tm, n), lambda j, i: (i, j)), jax.ShapeDtypeStruct((m, N_SHARD * n), F32),
                lambda a_ref, w_ref: (_nn(a_ref[...], w_ref[...]),), ride=ride)


def _mm_cols_t_rms(d, ws, h, w, resid, name, ride=None):
    m = d.shape[0]
    _, k, n = ws.shape
    tm = _fit_rows(m, n * _isz(d) + 3 * k * 4, k * n * _isz(ws), 16 * k)
    return _gmm_rms(name, (m // tm, N_SHARD), [d, ws],
                    [pl.BlockSpec((tm, n), lambda i, j: (i, j)), pl.BlockSpec((None, k, n), lambda i, j: (j, 0, 0))],
                    pl.BlockSpec((tm, k), lambda i, j: (i, 0)),
                    lambda d_ref, w_ref: _nt(d_ref[...], w_ref[...]), h, w, resid, 0, red_axis=1, ride=ride)


def _mm_nt_rms(a, b, h, w, resid, name, ride=None):
    m, n = a.shape
    k = b.shape[0]
    tm = _fit_rows(m, n * _isz(a) + 3 * k * 4, k * n * _isz(b), 16 * k)
    return _gmm_rms(name, (m // tm,), [a, b],
                    [pl.BlockSpec((tm, n), lambda i: (i, 0)), pl.BlockSpec((k, n), lambda i: (0, 0))],
                    pl.BlockSpec((tm, k), lambda i: (i, 0)),
                    lambda a_ref, b_ref: _nt(a_ref[...], b_ref[...]), h, w, resid, 0, ride=ride)


def _mm_cols_grad(a, d, name):
    m, k = a.shape
    n = d.shape[1] // N_SHARD
    tm = _fit_rows(m, k * _isz(a) + n * _isz(d), (3 * k * n * 4) // 2, 2 * (k + n))
    return _gmm(name, (N_SHARD, m // tm), [a, d],
                [pl.BlockSpec((tm, k), lambda j, i: (i, 0)), pl.BlockSpec((tm, n), lambda j, i: (i, j))],
                pl.BlockSpec((None, k, n), lambda j, i: (j, 0, 0)), jax.ShapeDtypeStruct((N_SHARD, k, n), F32),
                lambda a_ref, d_ref: (_tn(a_ref[...], d_ref[...]),), red_axis=1)


def _ffn_up(hn, wg, wu, layer, name, ride=None):
    m, k = hn.shape
    n = wg.shape[3]
    tm = _fit_rows(m, k * _isz(hn) + 3 * n * jnp.dtype(BF16).itemsize, 2 * k * n * _isz(wg), 16 * n)

    def fn(a_ref, wg_ref, wu_ref):
        a = a_ref[...]
        g = _nn(a, wg_ref[...])
        u = _nn(a, wu_ref[...])
        return g, u, g * jax.nn.sigmoid(g) * u

    w_spec = pl.BlockSpec((None, None, k, n), lambda j, i: (j, layer, 0, 0))
    o_spec = pl.BlockSpec((None, tm, n), lambda j, i: (j, i, 0))
    out = jax.ShapeDtypeStruct((N_SHARD, m, n), BF16)
    return _gmm(name, (N_SHARD, m // tm), [hn, wg, wu],
                [pl.BlockSpec((tm, k), lambda j, i: (i, 0)), w_spec, w_spec],
                [o_spec, o_spec, o_spec], [out, out, out], fn, ride=ride)


def _ffn_down(act, wd, resid, layer, name, ride=None):
    _, m, n = act.shape
    d = wd.shape[3]
    tm = _fit_rows(m, N_SHARD * n * _isz(act) + 2 * d * 4, N_SHARD * n * d * _isz(wd), 8 * d)

    def fn(a_ref, w_ref, r_ref):
        acc = r_ref[...]
        for j in range(N_SHARD):
            acc = acc + _nn(a_ref[j], w_ref[j])
        return (acc,)

    row = pl.BlockSpec((tm, d), lambda i: (i, 0))
    return _gmm(name, (m // tm,), [act, wd, resid],
                [pl.BlockSpec((N_SHARD, tm, n), lambda i: (0, i, 0)),
                 pl.BlockSpec((N_SHARD, None, n, d), lambda i: (0, layer, 0, 0)), row],
                row, jax.ShapeDtypeStruct((m, d), F32), fn, ride=ride)


def _ffn_down_bwd(dh, wd, g, u, layer, name, ride=None):
    m, d = dh.shape
    n = wd.shape[2]
    tm = _fit_rows(m, d * _isz(dh) + 4 * N_SHARD * n * jnp.dtype(BF16).itemsize, N_SHARD * n * d * _isz(wd),
                   2 * d + 24 * n)

    def body(dh_ref, wd_ref, g_ref, u_ref, dg_ref, du_ref):
        dhv = dh_ref[...].astype(MXU_DTYPE)
        for j in range(N_SHARD):
            dact = _nt(dhv, wd_ref[j])
            gv = g_ref[j].astype(F32)
            sg = jax.nn.sigmoid(gv)
            gs = gv * sg
            dg_ref[j] = (dact * u_ref[j].astype(F32) * (sg + gs * (1.0 - sg))).astype(dg_ref.dtype)
            du_ref[j] = (dact * gs).astype(du_ref.dtype)

    sh_spec = pl.BlockSpec((N_SHARD, tm, n), lambda i: (0, i, 0))
    out = jax.ShapeDtypeStruct((N_SHARD, m, n), BF16)
    res, rode = _pcall(body, [dh, wd, g, u], grid=(m // tm,),
                       in_specs=[pl.BlockSpec((tm, d), lambda i: (i, 0)),
                                 pl.BlockSpec((N_SHARD, None, n, d), lambda i: (0, layer, 0, 0)), sh_spec, sh_spec],
                       out_specs=[sh_spec, sh_spec], out_shape=[out, out], name=name, sem=("parallel",), ride=ride)
    return res if ride is None else (res, rode)


def _ffn_up_bwd(dg, du, wg, wu, layer, h, w, resid, name, ride=None):
    _, m, n = dg.shape
    k = wg.shape[2]
    tm = _fit_rows(m, 2 * N_SHARD * n * _isz(dg) + 3 * k * 4, 2 * N_SHARD * k * n * _isz(wg), 16 * k)

    def fn(dg_ref, du_ref, wg_ref, wu_ref):
        acc = _nt(dg_ref[0], wg_ref[0]) + _nt(du_ref[0], wu_ref[0])
        for j in range(1, N_SHARD):
            acc = acc + _nt(dg_ref[j], wg_ref[j]) + _nt(du_ref[j], wu_ref[j])
        return acc

    d_spec = pl.BlockSpec((N_SHARD, tm, n), lambda i: (0, i, 0))
    w_spec = pl.BlockSpec((N_SHARD, None, k, n), lambda i: (0, layer, 0, 0))
    return _gmm_rms(name, (m // tm,), [dg, du, wg, wu], [d_spec, d_spec, w_spec, w_spec],
                    pl.BlockSpec((tm, k), lambda i: (i, 0)), fn, h, w, resid, 0, ride=ride)


def _ffn_wgrad(lhs, rhs_list, layer, layers, prev, lhs_sharded, name):
    if lhs_sharded:
        _, m, k = lhs.shape
        n = rhs_list[0].shape[1]
    else:
        m, k = lhs.shape
        n = rhs_list[0].shape[2]
    n_out = len(rhs_list)
    tm = _fit_rows(m, k * _isz(lhs) + n_out * n * _isz(rhs_list[0]), (3 * n_out * k * n * 4) // 2,
                   2 * (k + n_out * n))
    sh = pl.BlockSpec((None, tm, k if lhs_sharded else n), lambda j, i: (j, i, 0))
    fl = pl.BlockSpec((tm, n if lhs_sharded else k), lambda j, i: (i, 0))
    n_out = len(rhs_list)
    args = [lhs] + list(rhs_list)
    in_specs = [sh if lhs_sharded else fl] + [fl if lhs_sharded else sh] * n_out
    aliases = None
    if prev is not None:
        aliases = {len(args) + t: t for t in range(n_out)}
        args = args + list(prev)
        in_specs = in_specs + [ANY] * n_out

    def fn(l_ref, *rest):
        lv = l_ref[...]
        return tuple(_tn(lv, r_ref[...]) for r_ref in rest[:n_out])

    o_spec = pl.BlockSpec((None, None, k, n), lambda j, i: (j, layer, 0, 0))
    out = jax.ShapeDtypeStruct((N_SHARD, layers, k, n), F32)
    return _gmm(name, (N_SHARD, m // tm), args, in_specs, [o_spec] * n_out, [out] * n_out, fn,
                red_axis=1, aliases=aliases)


def _ret_consts():
    log_gamma = jnp.log1p(-jnp.exp2(-5.0 - jnp.arange(RET_HEADS, dtype=F32)))
    idx = jnp.arange(CHUNK, dtype=F32)
    rel = idx[:, None] - idx[None, :]
    dmask = jnp.where((rel >= 0)[None], jnp.exp(log_gamma[:, None, None] * jnp.maximum(rel, 0.0)), 0.0)
    xi = jnp.exp(log_gamma[:, None] * (idx[None, :] + 1.0))[:, :, None]
    zeta = jnp.exp(log_gamma[:, None] * (CHUNK - 1.0 - idx[None, :]))[:, :, None]
    gamma_c = jnp.exp(log_gamma * CHUNK)
    wide = (RET_HEADS, CHUNK, RET_DK)
    return dmask, jnp.broadcast_to(xi, wide), jnp.broadcast_to(zeta, wide), gamma_c


def _rope_tables(nc):
    half = RET_DK // 2
    inv_freq = ROPE_BASE ** (-jnp.arange(half, dtype=F32) / half)
    a_chunk = (jnp.arange(nc) * CHUNK - PAD).astype(F32)[:, None] * inv_freq[None, :]
    a_row = jnp.arange(CHUNK).astype(F32)[:, None] * inv_freq[None, :]
    return (jnp.stack([jnp.cos(a_chunk), jnp.sin(a_chunk)], axis=1),
            jnp.stack([jnp.cos(a_row), jnp.sin(a_row)], axis=0))


RET_CPS = 4


def _rope_chunk(rc_ref, rr_ref, c):
    cc, sc = rc_ref[c, 0:1, :], rc_ref[c, 1:2, :]
    cr, sr = rr_ref[0], rr_ref[1]
    return cc * cr - sc * sr, sc * cr + cc * sr


def _rope_specs(order):
    half = RET_DK // 2
    return [pl.BlockSpec((RET_CPS, 2, half), lambda n: (order(n), 0, 0)),
            pl.BlockSpec((2, CHUNK, half), lambda n: (0, 0, 0))]


def _ret_specs(order):
    rows = RET_CPS * CHUNK
    return [pl.BlockSpec((rows, RET_QK), lambda n: (order(n), 0)),
            pl.BlockSpec((rows, RET_QK), lambda n: (order(n), 1)),
            pl.BlockSpec((rows, RET_V), lambda n: (order(n), 1)),
            pl.BlockSpec((rows, RET_V), lambda n: (order(n), 2))]


def _ret_const_specs():
    return [pl.BlockSpec((RET_HEADS, CHUNK, CHUNK), lambda n: (0, 0, 0)),
            pl.BlockSpec((RET_HEADS, CHUNK, RET_DK), lambda n: (0, 0, 0)),
            pl.BlockSpec((RET_HEADS, CHUNK, RET_DK), lambda n: (0, 0, 0)),
            pl.BlockSpec((1, RET_DV), lambda n: (0, 0))]


def _ret_fwd(proj, cos, sin, consts, gn_w, seq, ride=None):
    rows = proj.shape[0]
    nc = rows // CHUNK
    dmask, xi, zeta, gamma_c = consts

    def body(gam_ref, q_ref, k_ref, v_ref, g_ref, cos_ref, sin_ref, dm_ref, xi_ref, ze_ref, gn_ref,
             o_ref, y_ref, ss_ref, s_ref):
        n = pl.program_id(0)

        @pl.when(n == 0)
        def _():
            s_ref[...] = jnp.zeros_like(s_ref)

        gn = gn_ref[...]
        hs = range(RET_HEADS)
        qk_cols = [slice(h * RET_DK, (h + 1) * RET_DK) for h in hs]
        v_cols = [slice(h * RET_DV, (h + 1) * RET_DV) for h in hs]
        for c in range(RET_CPS):
            rs = slice(c * CHUNK, (c + 1) * CHUNK)
            cs, sn = _rope_chunk(cos_ref, sin_ref, c)
            kscale = _valid_rows((n * RET_CPS + c) * CHUNK, CHUNK, seq) * (RET_DK ** -0.5)
            qr_l = [_rope(q_ref[rs, col], cs, sn) for col in qk_cols]
            kr_l = [_rope(k_ref[rs, col], cs, sn) * kscale for col in qk_cols]
            v_l = [v_ref[rs, col] for col in v_cols]
            s_l = [s_ref[h] for h in hs]
            sc_l = [_nt(qr, kr) * dm_ref[h] for h, (qr, kr) in enumerate(zip(qr_l, kr_l))]
            o_l = [_nn(sc_l[h], v_l[h]) + _nn(qr_l[h] * xi_ref[h], s_l[h]) for h in hs]
            for h in hs:
                ss_ref[c, h] = s_l[h].astype(ss_ref.dtype)
                s_ref[h] = gam_ref[h] * s_l[h] + _tn(kr_l[h] * ze_ref[h], v_l[h])
                o_ref[rs, v_cols[h]] = o_l[h]
                y_ref[rs, v_cols[h]] = _gated_norm(o_l[h], g_ref[rs, v_cols[h]], gn).astype(y_ref.dtype)

    fwd = lambda n: n
    row_v = pl.BlockSpec((RET_CPS * CHUNK, RET_V), lambda n: (n, 0))
    res, rode = _pcall(
        body, [gamma_c, proj, proj, proj, proj, cos, sin, dmask, xi, zeta, gn_w.reshape(1, RET_DV)],
        grid=(nc // RET_CPS,),
        in_specs=[pl.BlockSpec(memory_space=pltpu.SMEM)] + _ret_specs(fwd) + _rope_specs(fwd)
        + _ret_const_specs(),
        out_specs=[row_v, row_v,
                   pl.BlockSpec((RET_CPS, RET_HEADS, RET_DK, RET_DV), lambda n: (n, 0, 0, 0))],
        out_shape=[jax.ShapeDtypeStruct((rows, RET_V), F32), jax.ShapeDtypeStruct((rows, RET_V), BF16),
                   jax.ShapeDtypeStruct((nc, RET_HEADS, RET_DK, RET_DV), BF16)],
        scratch=[pltpu.VMEM((RET_HEADS, RET_DK, RET_DV), F32)], name="ret_fwd", sem=("arbitrary",), ride=ride)
    return res if ride is None else (res, rode)


def _ret_bwd(proj, o, dy, states, cos, sin, consts, gn_w, seq, ride=None):
    rows = proj.shape[0]
    nc = rows // CHUNK
    dmask, xi, zeta, gamma_c = consts

    def body(gam_ref, q_ref, k_ref, v_ref, g_ref, o_ref, dy_ref, ss_ref, cos_ref, sin_ref,
             dm_ref, xi_ref, ze_ref, gn_ref, dp_ref, dgn_ref, ds_ref):
        n = pl.program_id(0)

        @pl.when(n == 0)
        def _():
            ds_ref[...] = jnp.zeros_like(ds_ref)
            dgn_ref[...] = jnp.zeros_like(dgn_ref)

        gn = gn_ref[...]
        dgn = jnp.zeros((1, RET_DV), F32)
        hs = range(RET_HEADS)
        qk_cols = [slice(h * RET_DK, (h + 1) * RET_DK) for h in hs]
        v_cols = [slice(h * RET_DV, (h + 1) * RET_DV) for h in hs]
        for c in reversed(range(RET_CPS)):
            rs = slice(c * CHUNK, (c + 1) * CHUNK)
            cs, sn = _rope_chunk(cos_ref, sin_ref, c)
            kscale = _valid_rows(((steps - 1 - n) * RET_CPS + c) * CHUNK, CHUNK, seq) * (RET_DK ** -0.5)
            qr_l = [_rope(q_ref[rs, col], cs, sn) for col in qk_cols]
            kr_l = [_rope(k_ref[rs, col], cs, sn) * kscale for col in qk_cols]
            v_l = [v_ref[rs, col] for col in v_cols]
            s_l = [ss_ref[c, h] for h in hs]
            ds_l = [ds_ref[h] for h in hs]
            sc_l = [_nt(qr_l[h], kr_l[h]) * dm_ref[h] for h in hs]
            gnb = [_gated_norm_bwd(dy_ref[rs, col], o_ref[rs, col], g_ref[rs, col], gn) for col in v_cols]
            do_l = [x[0] for x in gnb]
            dsc_l = [_nt(do_l[h], v_l[h]) * dm_ref[h] for h in hs]
            dv_l = [_tn(sc_l[h], do_l[h]) + _nn(kr_l[h] * ze_ref[h], ds_l[h]) for h in hs]
            dqr_l = [_nn(dsc_l[h], kr_l[h]) + _nt(do_l[h], s_l[h]) * xi_ref[h] for h in hs]
            dkr_l = [_tn(dsc_l[h], qr_l[h]) + _nt(v_l[h], ds_l[h]) * ze_ref[h] for h in hs]
            for h in hs:
                dgn = dgn + gnb[h][2]
                ds_ref[h] = gam_ref[h] * ds_l[h] + _tn(qr_l[h] * xi_ref[h], do_l[h])
                dp_ref[rs, qk_cols[h]] = _rope_bwd(dqr_l[h], cs, sn).astype(dp_ref.dtype)
                dp_ref[rs, RET_QK + h * RET_DK:RET_QK + (h + 1) * RET_DK] = (
                    _rope_bwd(dkr_l[h] * kscale, cs, sn).astype(dp_ref.dtype))
                dp_ref[rs, 2 * RET_QK + h * RET_DV:2 * RET_QK + (h + 1) * RET_DV] = dv_l[h].astype(dp_ref.dtype)
                dp_ref[rs, 2 * RET_QK + RET_V + h * RET_DV:2 * RET_QK + RET_V + (h + 1) * RET_DV] = (
                    gnb[h][1].astype(dp_ref.dtype))
        dgn_ref[...] += dgn

    steps = nc // RET_CPS
    rev = lambda n: steps - 1 - n
    row_v = pl.BlockSpec((RET_CPS * CHUNK, RET_V), lambda n: (rev(n), 0))
    res, rode = _pcall(
        body, [gamma_c, proj, proj, proj, proj, o, dy, states, cos, sin, dmask, xi, zeta,
               gn_w.reshape(1, RET_DV)],
        grid=(steps,),
        in_specs=[pl.BlockSpec(memory_space=pltpu.SMEM)] + _ret_specs(rev) + [
            row_v, row_v, pl.BlockSpec((RET_CPS, RET_HEADS, RET_DK, RET_DV), lambda n: (rev(n), 0, 0, 0))]
        + _rope_specs(rev) + _ret_const_specs(),
        out_specs=[pl.BlockSpec((RET_CPS * CHUNK, RET_IN), lambda n: (rev(n), 0)),
                   pl.BlockSpec((1, RET_DV), lambda n: (0, 0))],
        out_shape=[jax.ShapeDtypeStruct((rows, RET_IN), BF16), jax.ShapeDtypeStruct((1, RET_DV), F32)],
        scratch=[pltpu.VMEM((RET_HEADS, RET_DK, RET_DV), F32)], name="ret_bwd", sem=("arbitrary",), ride=ride)
    return res if ride is None else (res, rode)


GATE_COL = DN_CONV_CH // DN_V
BA_COL = (DN_CONV_CH + DN_V) // LANES
BETA_LANE, DECAY_LANE = 0, DN_HEADS
INV_SHIFT = 4
INV_SQUARINGS = INV_SHIFT - 1
assert CHUNK == 4 << INV_SHIFT


DN_CPS = 2


def _dn_in_specs(order, conv_saved=False):
    rows = DN_CPS * CHUNK
    return [pl.BlockSpec((rows, DN_CONV_CH), lambda n: (order(n), 0)),
            pl.BlockSpec((rows, DN_CONV_CH), lambda n: (order(n), 0)) if conv_saved else
            pl.BlockSpec((8, DN_CONV_CH), lambda n: (jnp.maximum(order(n) * (rows // 8) - 1, 0), 0)),
            pl.BlockSpec((rows, DN_V), lambda n: (order(n), GATE_COL)),
            pl.BlockSpec((rows, LANES), lambda n: (order(n), BA_COL)),
            pl.BlockSpec((CONV_K, 1, DN_CONV_CH), lambda n: (0, 0, 0)),
            pl.BlockSpec((1, LANES), lambda n: (0, 0)),
            pl.BlockSpec((1, LANES), lambda n: (0, 0)),
            pl.BlockSpec((1, DN_DV), lambda n: (0, 0))]


def _dn_front(c, seq, x, halo, ba, cw_ref, al_ref, dt_ref, yc=None):
    valid = _valid_rows(c * CHUNK, CHUNK, seq)
    xin = x * valid
    if yc is None:
        halo = halo * _valid_rows(c * CHUNK - 8, 8, seq)
        yc = xin * cw_ref[CONV_K - 1]
        for k in range(1, CONV_K):
            yc = yc + _shift_down(xin, halo, k) * cw_ref[CONV_K - 1 - k]
    sgc = jax.nn.sigmoid(yc)
    sig = jax.nn.sigmoid(ba)
    beta = sig * valid
    z = ba + dt_ref[...]
    eal = jnp.exp(al_ref[...])
    g = -eal * _softplus(z) * valid
    ri, ci = _iota((CHUNK, CHUNK), 0), _iota((CHUNK, CHUNK), 1)
    lower = (ri >= ci).astype(F32)
    upper = (ri <= ci).astype(F32)
    eye = (ri == ci).astype(F32)
    gam = _nn(lower, g, hi=True)
    gam_t = _tn(g, upper, hi=True)
    return dict(valid=valid, xin=xin, yc=yc, sgc=sgc, act=yc * sgc, sig=sig, beta=beta, z=z,
                eal=eal, g=g, gam=gam, gam_t=gam_t, ri=ri, ci=ci, upper=upper, eye=eye)


def _dn_head(f, h):
    act = f["act"]
    q_raw = act[:, h * DN_DK:(h + 1) * DN_DK]
    k_raw = act[:, DN_QK + h * DN_DK:DN_QK + (h + 1) * DN_DK]
    v = act[:, 2 * DN_QK + h * DN_DV:2 * DN_QK + (h + 1) * DN_DV]
    rq = lax.rsqrt(jnp.sum(q_raw * q_raw, axis=-1, keepdims=True) + RMS_EPS)
    rk = lax.rsqrt(jnp.sum(k_raw * k_raw, axis=-1, keepdims=True) + RMS_EPS)
    qh = q_raw * rq
    kn = k_raw * rk
    gam_c = _col(f["gam"], DECAY_LANE + h)
    gam_r = _row(f["gam_t"], DECAY_LANE + h)
    bc = _col(f["beta"], BETA_LANE + h)
    diff = gam_c - gam_r
    decay = jnp.where(f["ri"] >= f["ci"], jnp.exp(jnp.minimum(diff, 0.0)), 0.0)
    glast = jnp.sum(gam_r * (_iota((1, CHUNK), 1) == CHUNK - 1).astype(F32), axis=1, keepdims=True)
    return dict(rq=rq, rk=rk, qh=qh, qn=qh * (DN_DK ** -0.5), kn=kn, v=v, gam_c=gam_c, gam_r=gam_r,
                bc=bc, diff=diff, decay=decay, egam=jnp.exp(gam_c), glast=glast,
                eglast=jnp.exp(glast), ekd=jnp.exp(glast - gam_c))


def _dn_fwd(proj, conv_w, alog, dtb, norm_w, seq):
    rows = proj.shape[0]
    nc = rows // CHUNK

    def body(x_ref, halo_ref, gate_ref, ba_ref, cw_ref, al_ref, dt_ref, nw_ref,
             o_ref, y_ref, ss_ref, t_ref, yc_ref, s_ref):
        n = pl.program_id(0)

        @pl.when(n == 0)
        def _():
            s_ref[...] = jnp.zeros_like(s_ref)

        nw = nw_ref[...]
        pre = []
        for c in range(DN_CPS):
            rs = slice(c * CHUNK, (c + 1) * CHUNK)
            halo = halo_ref[...] if c == 0 else x_ref[c * CHUNK - 8:c * CHUNK, :]
            f = _dn_front(n * DN_CPS + c, seq, x_ref[rs, :], halo, ba_ref[rs, :], cw_ref, al_ref, dt_ref)
            yc_ref[rs, :] = f["yc"]
            ri, ci = f["ri"], f["ci"]
            eye = f["eye"]
            diag_m = (jnp.right_shift(ri, INV_SHIFT) == jnp.right_shift(ci, INV_SHIFT)).astype(F32)
            half_m = (jnp.right_shift(ri, INV_SHIFT + 1) == jnp.right_shift(ci, INV_SHIFT + 1)).astype(F32)
            heads = [_dn_head(f, h) for h in range(DN_HEADS)]
            a_all = [jnp.where(ri > ci, hd["bc"] * _nt(hd["kn"], hd["kn"]) * hd["decay"], 0.0) for hd in heads]
            b_all = [a * diag_m for a in a_all]
            t_all = [eye - b for b in b_all]
            for _ in range(INV_SQUARINGS):
                b_all = [_nn(b, b, hi=True) for b in b_all]
                t_all = [t + _nn(t, b, hi=True) for t, b in zip(t_all, b_all)]
            for off_m in (half_m - diag_m, 1.0 - half_m):
                x_all = [_nn(a * off_m, t, hi=True) for a, t in zip(a_all, t_all)]
                t_all = [t - _nn(t, x, hi=True) for t, x in zip(t_all, x_all)]
            u_all = [_nn(t, hd["v"] * hd["bc"], hi=True) for t, hd in zip(t_all, heads)]
            w_all = [_nn(t, hd["kn"] * (hd["bc"] * hd["egam"]), hi=True) for t, hd in zip(t_all, heads)]
            qk_all = [_nt(hd["qn"], hd["kn"]) * hd["decay"] for hd in heads]
            for h in range(DN_HEADS):
                t_ref[c, h] = t_all[h]
            pre.append((heads, u_all, w_all, qk_all))
        for c in range(DN_CPS):
            rs = slice(c * CHUNK, (c + 1) * CHUNK)
            heads, u_all, w_all, qk_all = pre[c]
            s_all = [s_ref[h] for h in range(DN_HEADS)]
            os_all = [_nn(hd["qn"] * hd["egam"], s) for hd, s in zip(heads, s_all)]
            vnew_all = [u - _nn(w, s) for u, w, s in zip(u_all, w_all, s_all)]
            o_all = [os + _nn(qk, vn) for os, qk, vn in zip(os_all, qk_all, vnew_all)]
            snew_all = [s * hd["eglast"] + _tn(hd["kn"] * hd["ekd"], vn)
                        for s, hd, vn in zip(s_all, heads, vnew_all)]
            for h in range(DN_HEADS):
                v_cols = slice(h * DN_DV, (h + 1) * DN_DV)
                ss_ref[c, h] = s_all[h]
                s_ref[h] = snew_all[h]
                o_ref[rs, v_cols] = o_all[h]
                y_ref[rs, v_cols] = _gated_norm(o_all[h], gate_ref[rs, v_cols], nw).astype(y_ref.dtype)

    fwd = lambda n: n
    row_v = pl.BlockSpec((DN_CPS * CHUNK, DN_V), lambda n: (n, 0))
    return pl.pallas_call(
        body, grid=(nc // DN_CPS,), in_specs=_dn_in_specs(fwd),
        out_specs=[row_v, row_v,
                   pl.BlockSpec((DN_CPS, DN_HEADS, DN_DK, DN_DV), lambda n: (n, 0, 0, 0)),
                   pl.BlockSpec((DN_CPS, DN_HEADS, CHUNK, CHUNK), lambda n: (n, 0, 0, 0)),
                   pl.BlockSpec((DN_CPS * CHUNK, DN_CONV_CH), lambda n: (n, 0))],
        out_shape=[jax.ShapeDtypeStruct((rows, DN_V), F32), jax.ShapeDtypeStruct((rows, DN_V), BF16),
                   jax.ShapeDtypeStruct((nc, DN_HEADS, DN_DK, DN_DV), F32),
                   jax.ShapeDtypeStruct((nc, DN_HEADS, CHUNK, CHUNK), F32),
                   jax.ShapeDtypeStruct((rows, DN_CONV_CH), F32)],
        scratch_shapes=[pltpu.VMEM((DN_HEADS, DN_DK, DN_DV), F32)],
        name="dn_fwd", compiler_params=_params("arbitrary"))(
            proj, proj, proj, proj, conv_w, alog, dtb, norm_w.reshape(1, DN_DV))


def _dn_bwd(proj, conv_out, o, dy, states, tinv, conv_w, alog, dtb, norm_w, seq):
    rows = proj.shape[0]
    nc = rows // CHUNK

    def body(x_ref, yc_ref, gate_ref, ba_ref, cw_ref, al_ref, dt_ref, nw_ref,
             o_ref, dy_ref, ss_ref, t_ref,
             dp_ref, dcw_ref, dal_ref, ddt_ref, dnw_ref, ds_ref, nxt_ref):
        n = pl.program_id(0)

        @pl.when(n == 0)
        def _():
            ds_ref[...] = jnp.zeros_like(ds_ref)
            nxt_ref[...] = jnp.zeros_like(nxt_ref)
            dcw_ref[...] = jnp.zeros_like(dcw_ref)
            dal_ref[...] = jnp.zeros_like(dal_ref)
            ddt_ref[...] = jnp.zeros_like(ddt_ref)
            dnw_ref[...] = jnp.zeros_like(dnw_ref)

        for c in reversed(range(DN_CPS)):
            rs = pl.ds(c * CHUNK, CHUNK)
            chunk((steps - 1 - n) * DN_CPS + c, x_ref.at[rs], yc_ref.at[rs], gate_ref.at[rs], ba_ref.at[rs],
                  cw_ref, al_ref, dt_ref, nw_ref, o_ref.at[rs], dy_ref.at[rs], ss_ref.at[c], t_ref.at[c],
                  dp_ref.at[rs], dcw_ref, dal_ref, ddt_ref, dnw_ref, ds_ref, nxt_ref)

    def chunk(ch, x_ref, yc_ref, gate_ref, ba_ref, cw_ref, al_ref, dt_ref, nw_ref,
              o_ref, dy_ref, ss_ref, t_ref,
              dp_ref, dcw_ref, dal_ref, ddt_ref, dnw_ref, ds_ref, nxt_ref):
        f = _dn_front(ch, seq, x_ref[...], None, ba_ref[...], cw_ref, al_ref, dt_ref, yc_ref[...])
        ri, ci = f["ri"], f["ci"]
        strict = (ri > ci).astype(F32)
        nw = nw_ref[...]
        lane128 = _iota((1, LANES), 1)
        row128 = _iota((LANES, 1), 0)
        dgam_col = jnp.zeros((CHUNK, LANES), F32)
        dgam_row = jnp.zeros((LANES, CHUNK), F32)
        dbeta = jnp.zeros((CHUNK, LANES), F32)
        dnw = jnp.zeros((1, DN_DV), F32)
        hs = range(DN_HEADS)
        heads = [_dn_head(f, h) for h in hs]
        cols = [slice(h * DN_DV, (h + 1) * DN_DV) for h in hs]
        t_l = [t_ref[h] for h in hs]
        s_l = [ss_ref[h] for h in hs]
        ds_l = [ds_ref[h] for h in hs]
        kk_l = [_nt(hd["kn"], hd["kn"]) for hd in heads]
        p_l = [_nt(hd["qn"], hd["kn"]) for hd in heads]
        rhsw_l = [hd["kn"] * (hd["bc"] * hd["egam"]) for hd in heads]
        u_l = [_nn(t, hd["v"] * hd["bc"], hi=True) for t, hd in zip(t_l, heads)]
        w_l = [_nn(t, r, hi=True) for t, r in zip(t_l, rhsw_l)]
        vnew_l = [u - _nn(w, s) for u, w, s in zip(u_l, w_l, s_l)]
        gnb = [_gated_norm_bwd(dy_ref[:, c], o_ref[:, c], gate_ref[:, c], nw) for c in cols]
        do_l = [x[0] for x in gnb]
        for h in hs:
            dp_ref[:, DN_CONV_CH + h * DN_DV:DN_CONV_CH + (h + 1) * DN_DV] = gnb[h][1].astype(dp_ref.dtype)
            dnw = dnw + gnb[h][2]
        qg_l = [hd["qn"] * hd["egam"] for hd in heads]
        kd_l = [hd["kn"] * hd["ekd"] for hd in heads]
        dvnew_l = [_tn(p * hd["decay"], do) + _nn(kd, ds)
                   for p, hd, do, kd, ds in zip(p_l, heads, do_l, kd_l, ds_l)]
        m_l = [_nt(do, vn) for do, vn in zip(do_l, vnew_l)]
        dqg_l = [_nt(do, s) for do, s in zip(do_l, s_l)]
        dkd_l = [_nt(vn, ds) for vn, ds in zip(vnew_l, ds_l)]
        for h in hs:
            ds_ref[h] = (ds_l[h] * heads[h]["eglast"] + _tn(qg_l[h], do_l[h]) - _tn(w_l[h], dvnew_l[h]))
        dw_l = [-_nt(dvn, s) for dvn, s in zip(dvnew_l, s_l)]
        dru_l = [_tn(t, dvn, hi=True) for t, dvn in zip(t_l, dvnew_l)]
        drw_l = [_tn(t, dw_, hi=True) for t, dw_ in zip(t_l, dw_l)]
        da_l = [-(_nt(dru, u) + _nt(drw, w)) * strict for dru, u, drw, w in zip(dru_l, u_l, drw_l, w_l)]
        dp_l = [m * hd["decay"] for m, hd in zip(m_l, heads)]
        dkk_l = [da * (hd["bc"] * hd["decay"]) for da, hd in zip(da_l, heads)]
        dqn_l = [dqg * hd["egam"] + _nn(dp, hd["kn"]) for dqg, hd, dp in zip(dqg_l, heads, dp_l)]
        dkn_l = [_tn(dp, hd["qn"]) + dkd * hd["ekd"] + drw * (hd["bc"] * hd["egam"])
                 + _nn(dkk, hd["kn"]) + _tn(dkk, hd["kn"])
                 for dp, hd, dkd, drw, dkk in zip(dp_l, heads, dkd_l, drw_l, dkk_l)]
        dq_parts, dk_parts, dv_parts = [], [], []
        for h in hs:
            hd = heads[h]
            kn, v, bc, egam, decay = hd["kn"], hd["v"], hd["bc"], hd["egam"], hd["decay"]
            t1 = jnp.sum(dkd_l[h] * kd_l[h], axis=1, keepdims=True)
            dglast = (jnp.sum(t1, axis=0, keepdims=True)
                      + jnp.sum(jnp.sum(ds_l[h] * s_l[h], axis=1, keepdims=True), axis=0, keepdims=True)
                      * hd["eglast"])
            e = (m_l[h] * p_l[h] + da_l[h] * (bc * kk_l[h])) * decay
            dgc = (jnp.sum(dqg_l[h] * qg_l[h], axis=1, keepdims=True) - t1
                   + jnp.sum(drw_l[h] * rhsw_l[h], axis=1, keepdims=True)
                   + jnp.sum(e, axis=1, keepdims=True)
                   + jnp.where(_iota((CHUNK, 1), 0) == CHUNK - 1, dglast, 0.0))
            dgr = -jnp.sum(e, axis=0, keepdims=True)
            dbc = (jnp.sum(dru_l[h] * v, axis=1, keepdims=True)
                   + jnp.sum(drw_l[h] * kn, axis=1, keepdims=True) * egam
                   + jnp.sum(da_l[h] * kk_l[h] * decay, axis=1, keepdims=True))
            dv_parts.append(dru_l[h] * bc)
            qh, dqn, dkn = hd["qh"], dqn_l[h], dkn_l[h]
            dq_parts.append(((DN_DK ** -0.5) * hd["rq"])
                            * (dqn - qh * jnp.sum(dqn * qh, axis=1, keepdims=True)))
            dk_parts.append(hd["rk"] * (dkn - kn * jnp.sum(dkn * kn, axis=1, keepdims=True)))
            dgam_col = dgam_col + dgc * (lane128 == DECAY_LANE + h).astype(F32)
            dbeta = dbeta + dbc * (lane128 == BETA_LANE + h).astype(F32)
            dgam_row = dgam_row + (row128 == DECAY_LANE + h).astype(F32) * dgr
        dnw_ref[...] += dnw
        dgam = dgam_col + _nt(f["eye"], dgam_row, hi=True)
        dg = _nn(f["upper"], dgam, hi=True)
        d_a = dg * (-f["eal"]) * jax.nn.sigmoid(f["z"]) * f["valid"]
        dal_ref[...] += jnp.sum(dg * f["g"], axis=0, keepdims=True)
        ddt_ref[...] += jnp.sum(d_a, axis=0, keepdims=True)
        d_b = dbeta * f["valid"] * f["sig"] * (1.0 - f["sig"])
        dp_ref[:, DN_CONV_CH + DN_V:DN_CONV_CH + DN_V + LANES] = (d_a + d_b).astype(dp_ref.dtype)
        dp_ref[:, DN_CONV_CH + DN_V + LANES:] = jnp.zeros((CHUNK, DN_IN_PAD - DN_IN_USED), dp_ref.dtype)
        dact = jnp.concatenate(dq_parts + dk_parts + dv_parts, axis=1)
        yc, sgc = f["yc"], f["sgc"]
        dyc = dact * (sgc * (1.0 + yc * (1.0 - sgc)))
        nxt = nxt_ref[...]
        ups = [dyc] + [_shift_up(dyc, nxt, j) for j in range(1, CONV_K)]
        dx = ups[0] * cw_ref[CONV_K - 1]
        for j in range(1, CONV_K):
            dx = dx + ups[j] * cw_ref[CONV_K - 1 - j]
        for j in range(CONV_K):
            dcw_ref[CONV_K - 1 - j] += jnp.sum(f["xin"] * ups[j], axis=0, keepdims=True)
        nxt_ref[...] = dyc[0:8]
        dp_ref[:, :DN_CONV_CH] = (dx * f["valid"]).astype(dp_ref.dtype)

    steps = nc // DN_CPS
    rev = lambda n: steps - 1 - n
    row_v = pl.BlockSpec((DN_CPS * CHUNK, DN_V), lambda n: (rev(n), 0))
    vec = pl.BlockSpec((1, LANES), lambda n: (0, 0))
    return pl.pallas_call(
        body, grid=(steps,),
        in_specs=_dn_in_specs(rev, conv_saved=True) + [
            row_v, row_v,
            pl.BlockSpec((DN_CPS, DN_HEADS, DN_DK, DN_DV), lambda n: (rev(n), 0, 0, 0)),
            pl.BlockSpec((DN_CPS, DN_HEADS, CHUNK, CHUNK), lambda n: (rev(n), 0, 0, 0))],
        out_specs=[pl.BlockSpec((DN_CPS * CHUNK, DN_IN_PAD), lambda n: (rev(n), 0)),
                   pl.BlockSpec((CONV_K, 1, DN_CONV_CH), lambda n: (0, 0, 0)), vec, vec,
                   pl.BlockSpec((1, DN_DV), lambda n: (0, 0))],
        out_shape=[jax.ShapeDtypeStruct((rows, DN_IN_PAD), BF16),
                   jax.ShapeDtypeStruct((CONV_K, 1, DN_CONV_CH), F32),
                   jax.ShapeDtypeStruct((1, LANES), F32), jax.ShapeDtypeStruct((1, LANES), F32),
                   jax.ShapeDtypeStruct((1, DN_DV), F32)],
        scratch_shapes=[pltpu.VMEM((DN_HEADS, DN_DK, DN_DV), F32), pltpu.VMEM((8, DN_CONV_CH), F32)],
        name="dn_bwd", compiler_params=_params("arbitrary"))(
            proj, conv_out, proj, proj, conv_w, alog, dtb, norm_w.reshape(1, DN_DV), o, dy, states, tinv)


def _train_step(x, tgt, wts, sh, idx):
    seq = x.shape[0]
    rows = -(-(seq + CHUNK) // ROW_ALIGN) * ROW_ALIGN
    wts = dict(wts)
    (h0, tgt_p), (got,) = _embed(x, tgt, wts["meta_tokens"].astype(F32), rows,
                                 ride=_Ride("gather", [sh["ret_w_in"]]))
    wts["ret_w_in"] = got.reshape(N_SHARD, D_MODEL, -1)
    cos, sin = _rope_tables(rows // CHUNK)
    consts = _ret_consts()
    conv_w = wts["dn_conv_w"].reshape(CONV_K, 1, DN_CONV_CH)
    lane_pad = LANES - 2 * DN_HEADS
    alog = jnp.pad(wts["dn_a_log"].reshape(1, DN_HEADS), ((0, 0), (DECAY_LANE, lane_pad)))
    dtb = jnp.pad(wts["dn_dt_bias"].reshape(1, DN_HEADS), ((0, 0), (DECAY_LANE, lane_pad)))
    g = {}

    hn0 = _rms_fwd(h0, wts["mix_norm_w"][0], "rms_mix0")
    proj0, got = _mm_cols(hn0, wts["ret_w_in"], "ret_in",
                          ride=_Ride("gather", [sh["ret_w_out"], sh["ffn_w_gate"]]))
    wts["ret_w_out"] = got[0].reshape(-1, D_MODEL)
    wts["ffn_w_gate"] = got[1]
    (o0, y0, st0), got = _ret_fwd(proj0, cos, sin, consts, wts["ret_gn_w"], seq,
                                  ride=_Ride("gather", [sh["ffn_w_up"], sh["ffn_w_down"]]))
    wts["ffn_w_up"], wts["ffn_w_down"] = got
    h1 = _mm(y0, wts["ret_w_out"], mode="nn", name="ret_out", resid=h0)
    hn1 = _rms_fwd(h1, wts["ffn_norm_w"][0], "rms_ffn0")
    (g0, u0, act0), got = _ffn_up(hn1, wts["ffn_w_gate"], wts["ffn_w_up"], 0, "ffn_up0",
                                  ride=_Ride("gather", [sh["dn_w_in"], sh["dn_w_out"]]))
    n_dn = sh["dn_w_in"].shape[-1]
    dn_shards = got[0].reshape(N_SHARD, D_MODEL, n_dn)
    wts["dn_w_in"] = jnp.concatenate(
        [dn_shards[j] for j in range(N_SHARD)]
        + [jnp.zeros((D_MODEL, DN_IN_PAD - N_SHARD * n_dn), dn_shards.dtype)], axis=-1)
    wts["dn_w_out"] = got[1].reshape(-1, D_MODEL)
    h2 = _ffn_down(act0, wts["ffn_w_down"], h1, 0, "ffn_down0")
    hn2 = _rms_fwd(h2, wts["mix_norm_w"][1], "rms_mix1")
    proj1 = _mm(hn2, wts["dn_w_in"], mode="nn", name="dn_in")
    o1, y1, st1, tinv, conv1 = _dn_fwd(proj1, conv_w, alog, dtb, wts["dn_norm_w"], seq)
    h3 = _mm(y1, wts["dn_w_out"], mode="nn", name="dn_out", resid=h2)
    hn3 = _rms_fwd(h3, wts["ffn_norm_w"][1], "rms_ffn1")
    g1, u1, act1 = _ffn_up(hn3, wts["ffn_w_gate"], wts["ffn_w_up"], 1, "ffn_up1")
    h4 = _ffn_down(act1, wts["ffn_w_down"], h3, 1, "ffn_down1")

    dh4, g["final_norm_w"], loss, dh4b = _final_loss(h4, wts["final_norm_w"], tgt_p, seq, "final_loss")

    layers = wts["ffn_w_gate"].shape[1]

    ffn_names = ["ffn_w_down", "ffn_w_gate", "ffn_w_up"]

    def ffn_bwd(dh_out, dhb_out, h_mid, hn, gg, uu, act, layer, prev, ride=None, last=False):
        tag = str(layer)
        res = _ffn_down_bwd(dhb_out, wts["ffn_w_down"], gg, uu, layer, "ffn_down_bwd" + tag, ride=ride)
        (dg, du), rode = res if ride is not None else (res, None)
        d_down = _ffn_wgrad(act, [dhb_out], layer, layers, prev and prev[:1], True, "ffn_dwd" + tag)
        d_gu = _ffn_wgrad(hn, [dg, du], layer, layers, prev and prev[1:], False, "ffn_dwgu" + tag)
        grads = list(d_down) + list(d_gu)
        gs = rs_grads(ffn_names, grads) if last else None
        res = _ffn_up_bwd(dg, du, wts["ffn_w_gate"], wts["ffn_w_up"], layer, h_mid, wts["ffn_norm_w"][layer],
                          dh_out, "ffn_up_bwd" + tag, ride=_Ride("pair", gs) if last else None)
        (dh_mid, d_norm, dhb_mid), sib = res if last else (res, None)
        return dh_mid, dhb_mid, grads, d_norm, rode, gs, sib

    red = {}

    def rs_grads(names, grads):
        return [gr.reshape((N_SHARD,) + sh[n].shape) for n, gr in zip(names, grads)]

    def rs_partials(names, gs, sib):
        return [_rs_pair_add(gs[t], sib[t], idx, "rs_pair_add_" + n) for t, n in enumerate(names)]

    def rs_end(names, gs, sib, others, tag):
        mine = [_rs_final_add(gs[t], sib[t], others[t], idx, "rs_final_add_" + n) for t, n in enumerate(names)]
        red.update(zip(names, _rs_share(mine, "rs_share" + tag)))

    dh3, dh3b, ffn_grads, dfn1 = ffn_bwd(dh4, dh4b, h3, hn3, g1, u1, act1, 1, None)[:4]
    dy1 = _mm(dh3b, wts["dn_w_out"], mode="nt", name="dn_out_bwd")
    d_dn_out = _mm(y1, dh3b, mode="tn", name="dn_dwo")
    dproj1, dcw, dal, ddt, g["dn_norm_w"] = _dn_bwd(proj1, conv1, o1, dy1, st1, tinv, conv_w, alog, dtb,
                                                    wts["dn_norm_w"], seq)
    d_dn_in = _mm(hn2, dproj1, mode="tn", name="dn_dwi")
    d_dn_in = jnp.stack([d_dn_in[:, j * n_dn:(j + 1) * n_dn] for j in range(N_SHARD)])
    group1 = ["dn_w_out", "dn_w_in"]
    gs1 = rs_grads(group1, [d_dn_out, d_dn_in])
    (dh2, dmn1, dh2b), sib1 = _mm_nt_rms(dproj1, wts["dn_w_in"], h2, wts["mix_norm_w"][1], dh3, "dn_in_bwd",
                                         ride=_Ride("pair", gs1))
    g["dn_conv_w"] = dcw.reshape(CONV_K, DN_CONV_CH)
    g["dn_a_log"] = dal[0, DECAY_LANE:DECAY_LANE + DN_HEADS]
    g["dn_dt_bias"] = ddt[0, DECAY_LANE:DECAY_LANE + DN_HEADS]

    dh1, dh1b, _, dfn0, others1, gs2, sib2 = ffn_bwd(
        dh2, dh2b, h1, hn1, g0, u0, act0, 0, ffn_grads,
        ride=_Ride("chips", rs_partials(group1, gs1, sib1)), last=True)
    rs_end(group1, gs1, sib1, others1, "1")
    d_ret_out = _mm(y0, dh1b, mode="tn", name="ret_dwo")
    gs2b = rs_grads(["ret_w_out"], [d_ret_out])
    dy0, sib2b = _mm(dh1b, wts["ret_w_out"], mode="nt", name="ret_out_bwd", ride=_Ride("pair", gs2b))
    group2 = ffn_names + ["ret_w_out"]
    gs2, sib2 = gs2 + gs2b, list(sib2) + list(sib2b)
    (dproj0, g["ret_gn_w"]), others2 = _ret_bwd(proj0, o0, dy0, st0, cos, sin, consts, wts["ret_gn_w"], seq,
                                                ride=_Ride("chips", rs_partials(group2, gs2, sib2)))
    rs_end(group2, gs2, sib2, others2, "2")
    d_ret_in = _mm_cols_grad(hn0, dproj0, "ret_dwi")
    gs3 = rs_grads(["ret_w_in"], [d_ret_in])
    sib3 = _rs_pair(gs3, "rs_pair3")
    (dh0, dmn0, _), others3 = _mm_cols_t_rms(dproj0, wts["ret_w_in"], h0, wts["mix_norm_w"][0], dh1, "ret_in_bwd",
                                             ride=_Ride("chips", rs_partials(["ret_w_in"], gs3, sib3)))
    rs_end(["ret_w_in"], gs3, sib3, others3, "3")

    g["ffn_norm_w"] = jnp.concatenate([dfn0, dfn1], axis=0)
    g["mix_norm_w"] = jnp.concatenate([dmn0, dmn1], axis=0)
    g["meta_tokens"] = dh0[PAD:CHUNK]
    g["final_norm_w"] = g["final_norm_w"].reshape(D_MODEL)
    g["ret_gn_w"] = g["ret_gn_w"].reshape(RET_DV)
    g["dn_norm_w"] = g["dn_norm_w"].reshape(DN_DV)
    return loss, dh0, g, red


def _mesh_pos():
    return lax.axis_index("x"), lax.axis_index("y"), lax.axis_index("c")


def _other_chips(x, y):
    return [(1 - x, y), (x, 1 - y), (1 - x, 1 - y)]


def _remote(src, dst, send_sem, recv_sem, to):
    return pltpu.make_async_remote_copy(src_ref=src, dst_ref=dst, send_sem=send_sem, recv_sem=recv_sem,
                                        device_id=to, device_id_type=MESH)


GATHER_COPIES = 7


def _gather_phase(phase, ins, outs, send_sems, recv_sems):
    x, y, c = _mesh_pos()
    me = 2 * x + y
    chips = _other_chips(x, y)
    sibling = (x, y, 1 - c)

    def cp(t, k, src, dst, to):
        i = GATHER_COPIES * t + k
        return _remote(src, dst, send_sems.at[i], recv_sems.at[i], to)

    for t in range(len(ins)):
        own = cp(t, 0, ins[t], outs[t].at[me], sibling)
        if phase == 0:
            own.start()
        if phase == 2:
            own.wait()
        for k, (px, py) in enumerate(chips):
            landed = outs[t].at[2 * px + py, c]
            theirs = outs[t].at[2 * px + py, 1 - c]
            to_chip = cp(t, 1 + k, ins[t].at[c], outs[t].at[me, c], (px, py, c))
            if phase == 0:
                to_chip.start()
            if phase == 1:
                cp(t, 1 + k, ins[t].at[c], landed, (px, py, c)).wait_recv()
                cp(t, 4 + k, landed, landed, sibling).start()
            if phase == 2:
                to_chip.wait_send()
                cp(t, 4 + k, landed, landed, sibling).wait_send()
                cp(t, 4 + k, theirs, theirs, sibling).wait_recv()


def _chips_phase(phase, ins, outs, send_sems, recv_sems):
    x, y, c = _mesh_pos()
    for t in range(len(ins)):
        for k, (px, py) in enumerate(_other_chips(x, y)):
            cp = _remote(ins[t].at[2 * px + py], outs[t].at[k], send_sems.at[3 * t + k], recv_sems.at[3 * t + k],
                         (px, py, c))
            if phase == 0:
                cp.start()
            if phase == 2:
                cp.wait()


class _Ride:
    def __init__(self, kind, arrays):
        self.kind, self.arrays = kind, list(arrays)
        nt = len(self.arrays)
        if kind == "gather":
            self.phase_fn, n_sem = _gather_phase, GATHER_COPIES * nt
            self.out_shape = [jax.ShapeDtypeStruct((N_SHARD,) + a.shape, a.dtype) for a in self.arrays]
        elif kind == "pair":
            self.phase_fn, n_sem = _pair_phase, nt
            self.out_shape = [jax.ShapeDtypeStruct(a.shape[:1] + a.shape[2:], a.dtype) for a in self.arrays]
        else:
            self.phase_fn, n_sem = _chips_phase, 3 * nt
            self.out_shape = [jax.ShapeDtypeStruct((3,) + a.shape[1:], a.dtype) for a in self.arrays]
        self.in_specs, self.out_specs = [ANY] * nt, [ANY] * nt
        self.scratch = [pltpu.SemaphoreType.DMA((n_sem,)), pltpu.SemaphoreType.DMA((n_sem,))]

    def emit(self, step, nsteps, ins, outs, send_sems, recv_sems, before):
        mid = max(0, min((7 * nsteps) // 8, nsteps - 2))
        todo = [(0, 0), (1, mid)] if before else [(2, nsteps - 1)]
        for phase, at in todo:
            if phase == 1 and self.kind != "gather":
                continue

            @pl.when(step == at)
            def _(phase=phase):
                self.phase_fn(phase, ins, outs, send_sems, recv_sems)


def _gather_small(blk):
    r, wd = blk.shape

    def body(b_ref, out_ref, send_sems, recv_sems):
        x, y, c = _mesh_pos()
        chips = _other_chips(x, y)
        out_ref[2 * x + y] = b_ref[...]
        sends = [_remote(b_ref, out_ref.at[2 * x + y], send_sems.at[k], recv_sems.at[k], (px, py, c))
                 for k, (px, py) in enumerate(chips)]
        for cp in sends:
            cp.start()
        for k, (px, py) in enumerate(chips):
            _remote(b_ref, out_ref.at[2 * px + py], send_sems.at[k], recv_sems.at[k], (px, py, c)).wait_recv()
        for cp in sends:
            cp.wait_send()

    return pl.pallas_call(
        body, out_shape=jax.ShapeDtypeStruct((4, r, wd), blk.dtype), in_specs=[VMEM_SPEC], out_specs=VMEM_SPEC,
        scratch_shapes=[pltpu.SemaphoreType.DMA((3,)), pltpu.SemaphoreType.DMA((3,))],
        name="gather_small")(blk)


def _allreduce_small(blk):
    r, wd = blk.shape
    rels = [(dx, dy, dc) for dx in (0, 1) for dy in (0, 1) for dc in (0, 1) if dx or dy or dc]

    def body(b_ref, out_ref, buf_ref, send_sems, recv_sems):
        x, y, c = _mesh_pos()

        def peer(rel):
            dx, dy, dc = rel
            return (1 - x if dx else x, 1 - y if dy else y, 1 - c if dc else c)

        me = 4 * x + 2 * y + c
        buf_ref[me] = b_ref[...]
        sends = [_remote(b_ref, buf_ref.at[me], send_sems.at[k], recv_sems.at[k], peer(rel))
                 for k, rel in enumerate(rels)]
        for cp in sends:
            cp.start()
        for k, rel in enumerate(rels):
            px, py, pc = peer(rel)
            _remote(b_ref, buf_ref.at[4 * px + 2 * py + pc], send_sems.at[k], recv_sems.at[k],
                    (px, py, pc)).wait_recv()
        for cp in sends:
            cp.wait_send()
        acc = buf_ref[0]
        for d in range(1, 8):
            acc = acc + buf_ref[d]
        out_ref[...] = acc

    return pl.pallas_call(
        body, out_shape=jax.ShapeDtypeStruct((r, wd), blk.dtype), in_specs=[VMEM_SPEC], out_specs=VMEM_SPEC,
        scratch_shapes=[pltpu.VMEM((8, r, wd), blk.dtype), pltpu.SemaphoreType.DMA((7,)),
                        pltpu.SemaphoreType.DMA((7,))],
        name="allreduce_small")(blk)


def _rs_pair(gs, name):
    ride = _Ride("pair", gs)

    def body(*refs):
        nt = len(gs)
        for phase in (0, 2):
            _pair_phase(phase, refs[:nt], refs[nt:2 * nt], *refs[2 * nt:])

    return pl.pallas_call(body, out_shape=ride.out_shape, in_specs=ride.in_specs, out_specs=ride.out_specs,
                          scratch_shapes=ride.scratch, name=name)(*gs)


def _pair_phase(phase, ins, outs, send_sems, recv_sems):
    x, y, c = _mesh_pos()
    for t in range(len(ins)):
        cp = _remote(ins[t].at[:, 1 - c], outs[t], send_sems.at[t], recv_sems.at[t], (x, y, 1 - c))
        if phase == 0:
            cp.start()
        if phase == 2:
            cp.wait()


def _rs_tile(a, b):
    return _div_tile(a, 512 if b <= 1024 else 256, 16)


def _rs_pair_add(g, a, idx, name):
    _, _, rows, cols = g.shape
    tr = _rs_tile(rows, cols)

    def body(s_ref, g_ref, a_ref, p_ref):
        p_ref[...] = (g_ref[...] + a_ref[...]).astype(p_ref.dtype)

    blk = pl.BlockSpec((None, tr, cols), lambda j, i, s: (j, i, 0))
    spec = pltpu.PrefetchScalarGridSpec(
        num_scalar_prefetch=1, grid=(N_SHARD, rows // tr),
        in_specs=[pl.BlockSpec((None, None, tr, cols), lambda j, i, s: (j, s[0], i, 0)), blk], out_specs=blk)
    return pl.pallas_call(
        body, grid_spec=spec, out_shape=jax.ShapeDtypeStruct((N_SHARD, rows, cols), BF16), name=name,
        compiler_params=_params("parallel", "parallel"))(idx, g, a)


def _rs_final_add(g, a, b, idx, name):
    _, _, rows, cols = g.shape
    tr = _rs_tile(rows, cols)

    def body(s_ref, g_ref, a_ref, b0_ref, b1_ref, b2_ref, f_ref):
        own = g_ref[...] + a_ref[...]
        f_ref[...] = ((own + b0_ref[...].astype(F32)) + b1_ref[...].astype(F32)) + b2_ref[...].astype(F32)

    def b_spec(k):
        return pl.BlockSpec((None, tr, cols), lambda i, s: (k, i, 0))

    spec = pltpu.PrefetchScalarGridSpec(
        num_scalar_prefetch=1, grid=(rows // tr,),
        in_specs=[pl.BlockSpec((None, None, tr, cols), lambda i, s: (s[1], s[0], i, 0)),
                  pl.BlockSpec((None, tr, cols), lambda i, s: (s[1], i, 0)), b_spec(0), b_spec(1), b_spec(2)],
        out_specs=pl.BlockSpec((None, tr, cols), lambda i, s: (s[0], i, 0)))
    return pl.pallas_call(
        body, grid_spec=spec, out_shape=jax.ShapeDtypeStruct((2, rows, cols), F32), name=name,
        compiler_params=_params("parallel"))(idx, g, a, b, b, b)


def _rs_share(fs, name):
    nt = len(fs)

    def body(*refs):
        outs = refs[nt:2 * nt]
        send_sems, recv_sems = refs[2 * nt:]
        x, y, c = _mesh_pos()
        cps = [_remote(outs[t].at[c], outs[t].at[c], send_sems.at[t], recv_sems.at[t], (x, y, 1 - c))
               for t in range(nt)]
        for cp in cps:
            cp.start()
        for cp in cps:
            cp.wait()

    return pl.pallas_call(
        body, out_shape=[jax.ShapeDtypeStruct(f.shape, f.dtype) for f in fs],
        in_specs=[ANY] * nt, out_specs=[ANY] * nt, input_output_aliases={t: t for t in range(nt)},
        scratch_shapes=[pltpu.SemaphoreType.DMA((nt,)), pltpu.SemaphoreType.DMA((nt,))], name=name)(*fs)


def _adamw(w, g, m, v, name):
    lead, rows, cols = w.shape
    tr = rows // 4 if rows % 32 == 0 else rows

    def body(w_ref, g_ref, m_ref, v_ref, go_ref, d_ref, mo_ref, vo_ref):
        gv = g_ref[...]
        go_ref[...] = gv
        mn = ADAM_B1 * m_ref[...] + (1.0 - ADAM_B1) * gv
        vn = ADAM_B2 * v_ref[...] + (1.0 - ADAM_B2) * (gv * gv)
        m_hat = mn / (1.0 - ADAM_B1 ** ADAM_STEP)
        v_hat = vn / (1.0 - ADAM_B2 ** ADAM_STEP)
        d_ref[...] = -ADAM_LR * (m_hat / (jnp.sqrt(v_hat) + ADAM_EPS) + ADAM_WD * w_ref[...])
        mo_ref[...] = mn
        vo_ref[...] = vn

    blk = pl.BlockSpec((None, tr, cols), lambda l, i: (l, i, 0))
    out = jax.ShapeDtypeStruct((lead, rows, cols), F32)
    return pl.pallas_call(
        body, grid=(lead, rows // tr), in_specs=[blk] * 4, out_specs=[blk] * 4, out_shape=[out] * 4, name=name,
        compiler_params=_params("parallel", "parallel"))(w, g, m, v)


BIG = ["ret_w_in", "ret_w_out", "dn_w_in", "dn_w_out", "ffn_w_gate", "ffn_w_up", "ffn_w_down"]
TRANSPOSED_AT_BOUNDARY = {"dn_w_in": True, "ffn_w_gate": False, "ffn_w_up": False}
SMALL =["meta_tokens", "mix_norm_w", "ffn_norm_w", "ret_gn_w", "dn_conv_w", "dn_a_log", "dn_dt_bias",
         "dn_norm_w", "final_norm_w"]
SMALL_SHARDED = {"meta_tokens", "dn_conv_w", "dn_norm_w"}
ORDER = ["meta_tokens", "mix_norm_w", "ffn_norm_w", "ret_w_in", "ret_gn_w", "ret_w_out", "dn_w_in",
         "dn_conv_w", "dn_a_log", "dn_dt_bias", "dn_norm_w", "dn_w_out", "ffn_w_gate", "ffn_w_up",
         "ffn_w_down", "final_norm_w"]


def _halves(a):
    return a.reshape(2, -1, a.shape[-1])


def _pack_lanes(parts, align=8):
    flat = jnp.concatenate([p.reshape(-1) for p in parts])
    flat = jnp.pad(flat, (0, -flat.shape[0] % (align * LANES)))
    return flat.reshape(-1, LANES)


def _unpack(buf, shapes):
    lead = buf.shape[:-2]
    flat = buf.reshape(lead + (-1,))
    out, off = [], 0
    for shp in shapes:
        size = math.prod(shp)
        out.append(flat[..., off:off + size].reshape(lead + tuple(shp)))
        off += size
    return out


def _join_cols(shards):
    return jnp.concatenate([shards[j] for j in range(N_SHARD)], axis=-1)


def kernel(x, meta_tokens, mix_norm_w, ffn_norm_w, ret_w_in, ret_gn_w, ret_w_out, dn_w_in, dn_conv_w, dn_a_log, dn_dt_bias, dn_norm_w, dn_w_out, ffn_w_gate, ffn_w_up, ffn_w_down, final_norm_w, loss_target, m_meta_tokens, m_mix_norm_w, m_ffn_norm_w, m_ret_w_in, m_ret_gn_w, m_ret_w_out, m_dn_w_in, m_dn_conv_w, m_dn_a_log, m_dn_dt_bias, m_dn_norm_w, m_dn_w_out, m_ffn_w_gate, m_ffn_w_up, m_ffn_w_down, m_final_norm_w, v_meta_tokens, v_mix_norm_w, v_ffn_norm_w, v_ret_w_in, v_ret_gn_w, v_ret_w_out, v_dn_w_in, v_dn_conv_w, v_dn_a_log, v_dn_dt_bias, v_dn_norm_w, v_dn_w_out, v_ffn_w_gate, v_ffn_w_up, v_ffn_w_down, v_final_norm_w):
    w = dict(meta_tokens=meta_tokens, mix_norm_w=mix_norm_w, ffn_norm_w=ffn_norm_w, ret_w_in=ret_w_in,
             ret_gn_w=ret_gn_w, ret_w_out=ret_w_out, dn_w_in=dn_w_in, dn_conv_w=dn_conv_w, dn_a_log=dn_a_log,
             dn_dt_bias=dn_dt_bias, dn_norm_w=dn_norm_w, dn_w_out=dn_w_out, ffn_w_gate=ffn_w_gate,
             ffn_w_up=ffn_w_up, ffn_w_down=ffn_w_down, final_norm_w=final_norm_w)
    m = dict(meta_tokens=m_meta_tokens, mix_norm_w=m_mix_norm_w, ffn_norm_w=m_ffn_norm_w, ret_w_in=m_ret_w_in,
             ret_gn_w=m_ret_gn_w, ret_w_out=m_ret_w_out, dn_w_in=m_dn_w_in, dn_conv_w=m_dn_conv_w,
             dn_a_log=m_dn_a_log, dn_dt_bias=m_dn_dt_bias, dn_norm_w=m_dn_norm_w, dn_w_out=m_dn_w_out,
             ffn_w_gate=m_ffn_w_gate, ffn_w_up=m_ffn_w_up, ffn_w_down=m_ffn_w_down, final_norm_w=m_final_norm_w)
    v = dict(meta_tokens=v_meta_tokens, mix_norm_w=v_mix_norm_w, ffn_norm_w=v_ffn_norm_w, ret_w_in=v_ret_w_in,
             ret_gn_w=v_ret_gn_w, ret_w_out=v_ret_w_out, dn_w_in=v_dn_w_in, dn_conv_w=v_dn_conv_w,
             dn_a_log=v_dn_a_log, dn_dt_bias=v_dn_dt_bias, dn_norm_w=v_dn_norm_w, dn_w_out=v_dn_w_out,
             ffn_w_gate=v_ffn_w_gate, ffn_w_up=v_ffn_w_up, ffn_w_down=v_ffn_w_down, final_norm_w=v_final_norm_w)
    mx, my, mc = _mesh_pos()
    chip = 2 * mx + my

    sm_names = [n for n in SMALL if n in SMALL_SHARDED]
    sm_gathered = _unpack(_gather_small(_pack_lanes([w[n] for n in sm_names])), [w[n].shape for n in sm_names])
    full = {n: _join_cols(sm_gathered[i]) for i, n in enumerate(sm_names)}
    wts = {
        "meta_tokens": full["meta_tokens"], "mix_norm_w": mix_norm_w, "ffn_norm_w": ffn_norm_w,
        "ret_gn_w": ret_gn_w[0], "final_norm_w": final_norm_w, "dn_conv_w": full["dn_conv_w"][0],
        "dn_a_log": dn_a_log[0], "dn_dt_bias": dn_dt_bias[0], "dn_norm_w": full["dn_norm_w"][0],
    }
    idx = jnp.stack([mc, chip]).astype(jnp.int32)
    shards = {n: _halves(w[n].astype(MXU_DTYPE)) for n in BIG}
    loss_part, dh0, g, reduced = _train_step(x[0], loss_target[0], wts, shards, idx)
    seq = x.shape[1]
    grad_x = dh0[CHUNK:CHUNK + seq].reshape(x.shape)
    gsh = {}

    small_full_shapes = [g[n].shape for n in SMALL] + [(1,)]
    red = _unpack(_allreduce_small(_pack_lanes([g[n] for n in SMALL] + [loss_part[0, :1]])), small_full_shapes)
    loss = red[-1][0]
    for i, n in enumerate(SMALL):
        gn = red[i]
        if n in SMALL_SHARDED:
            width = w[n].shape[-1]
            gn = lax.dynamic_slice_in_dim(gn, chip * width, width, axis=gn.ndim - 1)
        gsh[n] = gn.reshape(w[n].shape)

    delta, new_m, new_v = {}, {}, {}
    for n in BIG:
        shp = w[n].shape
        if n in TRANSPOSED_AT_BOUNDARY and TRANSPOSED_AT_BOUNDARY[n]:
            view = lambda a: jnp.swapaxes(a, 1, 2).reshape(1, -1, LANES)
            back = lambda a: jnp.swapaxes(a.reshape(shp[0], shp[2], shp[1]), 1, 2)
        elif n in TRANSPOSED_AT_BOUNDARY:
            view = back = lambda a: jnp.swapaxes(a, 1, 2)
        else:
            view = back = lambda a: a
        res = _adamw(view(w[n]), view(reduced[n].reshape(shp)), view(m[n]), view(v[n]), "adamw_" + n)
        gsh[n], delta[n], new_m[n], new_v[n] = [back(r) for r in res]
    sm_local_shapes = [w[n].shape for n in SMALL]
    _, d_, m_, v_ = _adamw(*[_pack_lanes([t[n] for n in SMALL])[None] for t in (w, gsh, m, v)], "adamw_small")
    d_, m_, v_ = d_[0], m_[0], v_[0]
    for n, dd, mm, vv in zip(SMALL, _unpack(d_, sm_local_shapes), _unpack(m_, sm_local_shapes),
                             _unpack(v_, sm_local_shapes)):
        delta[n], new_m[n], new_v[n] = dd, mm, vv

    return (loss, grad_x, *[gsh[n] for n in ORDER], *[delta[n] for n in ORDER],
            *[new_m[n] for n in ORDER], *[new_v[n] for n in ORDER])
```

```python
import math

import jax
import jax.numpy as jnp
from jax import lax
from jax.experimental import pallas as pl
from jax.experimental.pallas import tpu as pltpu

F32 = jnp.float32
BF16 = jnp.bfloat16
MXU_DTYPE = BF16

D_MODEL = 1024
N_META = 16
CHUNK = 64
PAD = CHUNK - N_META
RMS_EPS = 1e-6
RET_HEADS, RET_DK, RET_DV = 4, 256, 512
RET_QK, RET_V = RET_HEADS * RET_DK, RET_HEADS * RET_DV
RET_IN = 2 * RET_QK + 2 * RET_V
ROPE_BASE = 10000.0
DN_HEADS, DN_DK, DN_DV = 8, 128, 256
DN_QK, DN_V = DN_HEADS * DN_DK, DN_HEADS * DN_DV
DN_CONV_CH = 2 * DN_QK + DN_V
DN_IN = DN_CONV_CH + DN_V + 2 * DN_HEADS
LANES = 128
DN_IN_USED = DN_CONV_CH + DN_V + LANES
DN_IN_PAD = DN_IN_USED + LANES
CONV_K = 4
FFN_HIDDEN = 2816
ADAM_LR, ADAM_B1, ADAM_B2, ADAM_EPS, ADAM_WD, ADAM_STEP = 0.001, 0.9, 0.999, 1e-08, 0.01, 10

ROW_ALIGN = 256
VMEM_LIMIT = 56 * 1024 * 1024
MESH = pl.DeviceIdType.MESH
ANY = pl.BlockSpec(memory_space=pl.ANY)
VMEM_SPEC = pl.BlockSpec(memory_space=pltpu.VMEM)
_HI = lax.Precision.HIGHEST


def _params(*sem):
    return pltpu.CompilerParams(dimension_semantics=sem, vmem_limit_bytes=VMEM_LIMIT)


def _dg(a, b, ca, cb, hi):
    dims = (((ca,), (cb,)), ((), ()))

    def dot(p, q):
        return lax.dot_general(p, q, dims, preferred_element_type=F32)

    if not hi:
        return dot(a.astype(MXU_DTYPE), b.astype(MXU_DTYPE))
    if MXU_DTYPE == F32:
        return lax.dot_general(a, b, dims, precision=_HI, preferred_element_type=F32)
    a_hi, b_hi = a.astype(MXU_DTYPE), b.astype(MXU_DTYPE)
    a_lo = (a - a_hi.astype(F32)).astype(MXU_DTYPE)
    b_lo = (b - b_hi.astype(F32)).astype(MXU_DTYPE)
    return dot(a_hi, b_hi) + (dot(a_hi, b_lo) + dot(a_lo, b_hi))


def _nn(a, b, hi=False):
    return _dg(a, b, 1, 0, hi)


def _nt(a, b, hi=False):
    return _dg(a, b, 1, 1, hi)


def _tn(a, b, hi=False):
    return _dg(a, b, 0, 0, hi)


def _iota(shape, dim):
    return lax.broadcasted_iota(jnp.int32, shape, dim)


def _valid_rows(first_row, rows, seq):
    r = first_row + _iota((rows, 1), 0)
    return ((r >= PAD) & (r < CHUNK + seq)).astype(F32)


def _rope(t, cs, sn):
    half = t.shape[-1] // 2
    t1, t2 = t[:, :half], t[:, half:]
    return jnp.concatenate([t1 * cs - t2 * sn, t1 * sn + t2 * cs], axis=1)


def _rope_bwd(d, cs, sn):
    half = d.shape[-1] // 2
    d1, d2 = d[:, :half], d[:, half:]
    return jnp.concatenate([d1 * cs + d2 * sn, d2 * cs - d1 * sn], axis=1)


def _col(x, idx):
    oh = (_iota((1, x.shape[1]), 1) == idx).astype(F32)
    return jnp.sum(x * oh, axis=1, keepdims=True)


def _row(x, idx):
    oh = (_iota((x.shape[0], 1), 0) == idx).astype(F32)
    return jnp.sum(x * oh, axis=0, keepdims=True)


def _shift_down(x, halo8, k):
    xr = pltpu.roll(x, k, 0)
    hr = pltpu.roll(halo8, k, 0)
    first = jnp.where(_iota((8, 1), 0) < k, hr, xr[0:8])
    return jnp.concatenate([first, xr[8:]], axis=0)


def _shift_up(x, next8, j):
    rows = x.shape[0]
    xr = pltpu.roll(x, rows - j, 0)
    nr = pltpu.roll(next8, 8 - j, 0)
    last = jnp.where(_iota((8, 1), 0) >= 8 - j, nr, xr[rows - 8:])
    return jnp.concatenate([xr[:rows - 8], last], axis=0)


def _gated_norm(o, gate, w):
    r = lax.rsqrt(jnp.mean(o * o, axis=-1, keepdims=True) + RMS_EPS)
    return o * r * w * (gate * jax.nn.sigmoid(gate))


def _gated_norm_bwd(dy, o, gate, w):
    r = lax.rsqrt(jnp.mean(o * o, axis=-1, keepdims=True) + RMS_EPS)
    nrm = o * r
    sg = jax.nn.sigmoid(gate)
    sl = gate * sg
    dgate = dy * nrm * w * (sg * (1.0 + gate * (1.0 - sg)))
    dn = dy * w * sl
    dw = jnp.sum(dy * nrm * sl, axis=0, keepdims=True)
    do = r * (dn - nrm * jnp.mean(dn * nrm, axis=-1, keepdims=True))
    return do, dgate, dw


def _softplus(z):
    return jnp.maximum(z, 0.0) + jnp.log(1.0 + jnp.exp(-jnp.abs(z)))


def _row_tile(rows, cap=768):
    for t in (768, 512, 256, 128, 64, 32, 16, 8):
        if t <= cap and rows % t == 0:
            return t
    return rows


TILE_BUDGET = 44 * 1024 * 1024


def _fit_rows(rows, row_bytes, fixed_bytes, value_row_bytes):
    best = None
    for t in range(LANES, rows + 1, LANES):
        if rows % t == 0 and 2 * (row_bytes * t + fixed_bytes) + value_row_bytes * t <= TILE_BUDGET:
            best = t
    return best or _row_tile(rows, 256)


def _div_tile(n, cap, mult):
    best = None
    for t in range(mult, min(cap, n) + 1, mult):
        if n % t == 0:
            best = t
    return best or n


def _col_tile(cols, cap=1536):
    best = None
    for t in range(LANES, min(cap, cols) + 1, LANES):
        if cols % t == 0:
            best = t
    return best or cols


def _embed(x, tgt, meta, rows, ride=None):
    seq, d = x.shape
    n_tok = seq // CHUNK

    def body(xa_ref, xb_ref, ta_ref, tb_ref, m_ref, h_ref, tp_ref):
        i = pl.program_id(0)
        first = jnp.concatenate([jnp.zeros((PAD, d), F32), m_ref[...]], axis=0)
        for half, (x_ref, t_ref) in enumerate(((xa_ref, ta_ref), (xb_ref, tb_ref))):
            k = 2 * i + half
            tokens = (k >= 1) & (k <= n_tok)
            rs = slice(half * CHUNK, (half + 1) * CHUNK)
            h_ref[rs, :] = jnp.where(k == 0, first, jnp.where(tokens, x_ref[...], 0.0))
            tp_ref[rs, :] = jnp.where(tokens, t_ref[...], 0.0)

    def tok(half):
        return pl.BlockSpec((CHUNK, d), lambda i: (jnp.clip(2 * i + half - 1, 0, n_tok - 1), 0))

    out = pl.BlockSpec((2 * CHUNK, d), lambda i: (i, 0))
    res, rode = _pcall(body, [x, x, tgt, tgt, meta], grid=(rows // (2 * CHUNK),),
                       in_specs=[tok(0), tok(1), tok(0), tok(1), pl.BlockSpec((N_META, d), lambda i: (0, 0))],
                       out_specs=[out, out], out_shape=[jax.ShapeDtypeStruct((rows, d), F32)] * 2, name="embed",
                       sem=("parallel",), ride=ride)
    return res if ride is None else (res, rode)


def _rms_fwd(h, w, name, ride=None):
    rows, d = h.shape
    tm = _row_tile(rows)

    def body(h_ref, w_ref, o_ref):
        x = h_ref[...]
        r = lax.rsqrt(jnp.mean(x * x, axis=-1, keepdims=True) + RMS_EPS)
        o_ref[...] = (x * r * w_ref[...]).astype(o_ref.dtype)

    res, rode = _pcall(body, [h, w.reshape(1, d)], grid=(rows // tm,),
                       in_specs=[pl.BlockSpec((tm, d), lambda i: (i, 0)), pl.BlockSpec((1, d), lambda i: (0, 0))],
                       out_specs=[pl.BlockSpec((tm, d), lambda i: (i, 0))],
                       out_shape=[jax.ShapeDtypeStruct((rows, d), BF16)], name=name, sem=("parallel",), ride=ride)
    return res[0] if ride is None else (res[0], rode)


def _gmm_rms(name, grid, args, in_specs, row_spec, fn, h, w, resid, row_axis, red_axis=None, ride=None):
    m, d = h.shape
    n_in = len(args)
    vec = pl.BlockSpec((1, d), lambda *g: (0, 0))

    def body(*refs):
        ins = refs[:n_in]
        h_ref, w_ref, r_ref, dh_ref, dw_ref, dh16_ref = refs[n_in:]
        part = fn(*ins)
        row = pl.program_id(row_axis)

        def finish(dy):
            x = h_ref[...]
            r = lax.rsqrt(jnp.mean(x * x, axis=-1, keepdims=True) + RMS_EPS)
            xh = x * r
            dxh = dy * w_ref[...]
            dh = r_ref[...] + r * (dxh - xh * jnp.mean(dxh * xh, axis=-1, keepdims=True))
            dh_ref[...] = dh
            dh16_ref[...] = dh.astype(dh16_ref.dtype)
            dwp = jnp.sum(dy * xh, axis=0, keepdims=True)

            @pl.when(row == 0)
            def _():
                dw_ref[...] = dwp

            @pl.when(row > 0)
            def _():
                dw_ref[...] += dwp

        if red_axis is None:
            finish(part)
            return
        k = pl.program_id(red_axis)

        @pl.when(k == 0)
        def _():
            dh_ref[...] = part

        @pl.when(k > 0)
        def _():
            dh_ref[...] += part

        @pl.when(k == grid[red_axis] - 1)
        def _():
            finish(dh_ref[...])

    res, rode = _pcall(body, list(args) + [h, w.reshape(1, d), resid], grid=grid,
                       in_specs=list(in_specs) + [row_spec, vec, row_spec], out_specs=[row_spec, vec, row_spec],
                       out_shape=[jax.ShapeDtypeStruct((m, d), F32), jax.ShapeDtypeStruct((1, d), F32),
                                  jax.ShapeDtypeStruct((m, d), BF16)],
                       name=name, sem=("arbitrary",) * len(grid), ride=ride)
    return res if ride is None else (res, rode)


def _final_loss(h, w, tgt, seq, name):
    rows, d = h.shape
    tm = _row_tile(rows)

    def body(h_ref, w_ref, t_ref, dh_ref, dw_ref, loss_ref, dh16_ref):
        i = pl.program_id(0)
        r_idx = i * tm + _iota((tm, 1), 0)
        m = ((r_idx >= CHUNK) & (r_idx < CHUNK + seq)).astype(F32)
        x = h_ref[...]
        wv = w_ref[...]
        r = lax.rsqrt(jnp.mean(x * x, axis=-1, keepdims=True) + RMS_EPS)
        xh = x * r
        err = (xh * wv - t_ref[...]) * m
        lpart = 0.5 * jnp.sum(jnp.mean(err * err, axis=-1, keepdims=True), axis=0, keepdims=True)
        dyv = err * (1.0 / d)
        dxh = dyv * wv
        dh = r * (dxh - xh * jnp.mean(dxh * xh, axis=-1, keepdims=True))
        dh_ref[...] = dh
        dh16_ref[...] = dh.astype(dh16_ref.dtype)
        part = jnp.sum(dyv * xh, axis=0, keepdims=True)

        @pl.when(i == 0)
        def _():
            dw_ref[...] = part
            loss_ref[...] = jnp.broadcast_to(lpart, loss_ref.shape)

        @pl.when(i > 0)
        def _():
            dw_ref[...] += part
            loss_ref[...] += jnp.broadcast_to(lpart, loss_ref.shape)

    blk = pl.BlockSpec((tm, d), lambda i: (i, 0))
    vec = pl.BlockSpec((1, d), lambda i: (0, 0))
    return pl.pallas_call(
        body, grid=(rows // tm,), in_specs=[blk, vec, blk],
        out_specs=[blk, vec, pl.BlockSpec((1, LANES), lambda i: (0, 0)), blk],
        out_shape=[jax.ShapeDtypeStruct((rows, d), F32), jax.ShapeDtypeStruct((1, d), F32),
                   jax.ShapeDtypeStruct((1, LANES), F32), jax.ShapeDtypeStruct((rows, d), BF16)],
        name=name, compiler_params=_params("arbitrary"))(h, w.reshape(1, d), tgt)


def _isz(x):
    return jnp.dtype(x.dtype).itemsize


def _mm(a, b, *, mode, name, out_dtype=F32, resid=None, col_cap=1536, ride=None):
    if mode == "tn":
        m, k = a.shape
        n = b.shape[1]
        tn = _col_tile(n, col_cap)
        tm = _fit_rows(m, k * _isz(a) + tn * _isz(b), (3 * k * tn * 4) // 2, 2 * (k + tn))

        def body_tn(a_ref, b_ref, o_ref):
            i = pl.program_id(1)
            part = _tn(a_ref[...], b_ref[...])

            @pl.when(i == 0)
            def _():
                o_ref[...] = part

            @pl.when(i > 0)
            def _():
                o_ref[...] += part

        return pl.pallas_call(
            body_tn, grid=(n // tn, m // tm),
            in_specs=[pl.BlockSpec((tm, k), lambda j, i: (i, 0)),
                      pl.BlockSpec((tm, tn), lambda j, i: (i, j))],
            out_specs=pl.BlockSpec((k, tn), lambda j, i: (0, j)),
            out_shape=jax.ShapeDtypeStruct((k, n), F32), name=name,
            compiler_params=_params("parallel", "arbitrary"))(a, b)

    m, ka = a.shape
    n = b.shape[1] if mode == "nn" else b.shape[0]
    has_resid = resid is not None
    tn = _col_tile(n, col_cap)
    tm = _fit_rows(m, ka * _isz(a) + tn * (jnp.dtype(out_dtype).itemsize + (4 if has_resid else 0)),
                   ka * tn * _isz(b), 2 * ka + 8 * tn)

    def body(*refs):
        if has_resid:
            a_ref, b_ref, r_ref, o_ref = refs
        else:
            a_ref, b_ref, o_ref = refs
        acc = _nn(a_ref[...], b_ref[...]) if mode == "nn" else _nt(a_ref[...], b_ref[...])
        if has_resid:
            acc = acc + r_ref[...]
        o_ref[...] = acc.astype(o_ref.dtype)

    b_spec = (pl.BlockSpec((b.shape[0], tn), lambda j, i: (0, j)) if mode == "nn"
              else pl.BlockSpec((tn, b.shape[1]), lambda j, i: (j, 0)))
    o_spec = pl.BlockSpec((tm, tn), lambda j, i: (i, j))
    in_specs = [pl.BlockSpec((tm, ka), lambda j, i: (i, 0)), b_spec]
    args = [a, b]
    if has_resid:
        in_specs.append(o_spec)
        args.append(resid)
    res, rode = _pcall(body, args, grid=(n // tn, m // tm), in_specs=in_specs, out_specs=[o_spec],
                       out_shape=[jax.ShapeDtypeStruct((m, n), out_dtype)], name=name,
                       sem=("parallel", "parallel"), ride=ride)
    return res[0] if ride is None else (res[0], rode)


N_SHARD = 4


def _gmm(name, grid, args, in_specs, out_specs, out_shape, fn, red_axis=None, init_arg=None, aliases=None,
         ride=None):
    n_in = len(args)
    single = not isinstance(out_shape, (list, tuple))
    out_specs = [out_specs] if single else list(out_specs)
    out_shape = [out_shape] if single else list(out_shape)

    def body(*refs):
        _gmm_step(fn, refs[:n_in], refs[n_in:], red_axis, init_arg)

    sem = tuple("arbitrary" if ax == red_axis else "parallel" for ax in range(len(grid)))
    res, rode = _pcall(body, args, grid=grid, in_specs=in_specs, out_specs=out_specs, out_shape=out_shape,
                       name=name, sem=sem, aliases=aliases, ride=ride)
    ours = res[0] if single else res
    return ours if ride is None else (ours, rode)


def _gmm_step(fn, ins, outs, red_axis, init_arg):
    parts = fn(*ins)
    if red_axis is None:
        for o_ref, p in zip(outs, parts):
            o_ref[...] = p.astype(o_ref.dtype)
        return
    k = pl.program_id(red_axis)

    @pl.when(k == 0)
    def _():
        for idx, (o_ref, p) in enumerate(zip(outs, parts)):
            o_ref[...] = p + ins[init_arg][...] if (idx == 0 and init_arg is not None) else p

    @pl.when(k > 0)
    def _():
        for o_ref, p in zip(outs, parts):
            o_ref[...] += p


def _ride_body(ride, grid, n_in, n_out, n_scratch, body):
    n_rin, n_rout = len(ride.arrays), len(ride.out_shape)
    nsteps = math.prod(grid)

    def wrapped(*refs):
        ins = refs[:n_in]
        r_ins = refs[n_in:n_in + n_rin]
        o0 = n_in + n_rin
        outs = refs[o0:o0 + n_out]
        r_outs = refs[o0 + n_out:o0 + n_out + n_rout]
        s0 = o0 + n_out + n_rout
        scratch = refs[s0:s0 + n_scratch]
        send_sems, recv_sems = refs[-2:]
        step = pl.program_id(0)
        for ax in range(1, len(grid)):
            step = step * grid[ax] + pl.program_id(ax)
        ride.emit(step, nsteps, r_ins, r_outs, send_sems, recv_sems, before=True)
        body(*ins, *outs, *scratch)
        ride.emit(step, nsteps, r_ins, r_outs, send_sems, recv_sems, before=False)

    return wrapped


def _pcall(body, args, *, grid, in_specs, out_specs, out_shape, name, sem, scratch=(), aliases=None, ride=None):
    if ride is None:
        res = pl.pallas_call(body, grid=grid, in_specs=list(in_specs), out_specs=list(out_specs),
                             out_shape=list(out_shape), scratch_shapes=list(scratch), name=name,
                             input_output_aliases=aliases or {}, compiler_params=_params(*sem))(*args)
        return res, None
    n_in, n_out = len(args), len(out_shape)
    res = pl.pallas_call(
        _ride_body(ride, grid, n_in, n_out, len(scratch), body), grid=grid,
        in_specs=list(in_specs) + ride.in_specs, out_specs=list(out_specs) + ride.out_specs,
        out_shape=list(out_shape) + ride.out_shape, scratch_shapes=list(scratch) + ride.scratch, name=name,
        input_output_aliases=aliases or {},
        compiler_params=_params(*(("arbitrary",) * len(grid))))(*args, *ride.arrays)
    return res[:n_out], res[n_out:]


def _mm_cols(a, ws, name, ride=None):
    m, k = a.shape
    n = ws.shape[2]
    tm = _fit_rows(m, k * _isz(a) + n * 4, k * n * _isz(ws), 4 * n)
    return _gmm(name, (N_SHARD, m // tm), [a, ws],
                [pl.BlockSpec((tm, k), lambda j, i: (i, 0)), pl.BlockSpec((None, k, n), lambda j, i: (j, 0, 0))],
                pl.BlockSpec((tm, n), lambda j, i: (i, j)), jax.ShapeDtypeStruct((m, N_SHARD * n), F32),
                lambda a_ref, w_ref: (_nn(a_ref[...], w_ref[...]),), ride=ride)


def _mm_cols_t_rms(d, ws, h, w, resid, name, ride=None):
    m = d.shape[0]
    _, k, n = ws.shape
    tm = _fit_rows(m, n * _isz(d) + 3 * k * 4, k * n * _isz(ws), 16 * k)
    return _gmm_rms(name, (m // tm, N_SHARD), [d, ws],
                    [pl.BlockSpec((tm, n), lambda i, j: (i, j)), pl.BlockSpec((None, k, n), lambda i, j: (j, 0, 0))],
                    pl.BlockSpec((tm, k), lambda i, j: (i, 0)),
                    lambda d_ref, w_ref: _nt(d_ref[...], w_ref[...]), h, w, resid, 0, red_axis=1, ride=ride)


def _mm_nt_rms(a, b, h, w, resid, name, ride=None):
    m, n = a.shape
    k = b.shape[0]
    tm = _fit_rows(m, n * _isz(a) + 3 * k * 4, k * n * _isz(b), 16 * k)
    return _gmm_rms(name, (m // tm,), [a, b],
                    [pl.BlockSpec((tm, n), lambda i: (i, 0)), pl.BlockSpec((k, n), lambda i: (0, 0))],
                    pl.BlockSpec((tm, k), lambda i: (i, 0)),
                    lambda a_ref, b_ref: _nt(a_ref[...], b_ref[...]), h, w, resid, 0, ride=ride)


def _mm_cols_grad(a, d, name):
    m, k = a.shape
    n = d.shape[1] // N_SHARD
    tm = _fit_rows(m, k * _isz(a) + n * _isz(d), (3 * k * n * 4) // 2, 2 * (k + n))
    return _gmm(name, (N_SHARD, m // tm), [a, d],
                [pl.BlockSpec((tm, k), lambda j, i: (i, 0)), pl.BlockSpec((tm, n), lambda j, i: (i, j))],
                pl.BlockSpec((None, k, n), lambda j, i: (j, 0, 0)), jax.ShapeDtypeStruct((N_SHARD, k, n), F32),
                lambda a_ref, d_ref: (_tn(a_ref[...], d_ref[...]),), red_axis=1)


def _ffn_up(hn, wg, wu, layer, name, ride=None):
    m, k = hn.shape
    n = wg.shape[3]
    tm = _fit_rows(m, k * _isz(hn) + 3 * n * jnp.dtype(BF16).itemsize, 2 * k * n * _isz(wg), 16 * n)

    def fn(a_ref, wg_ref, wu_ref):
        a = a_ref[...]
        g = _nn(a, wg_ref[...])
        u = _nn(a, wu_ref[...])
        return g, u, g * jax.nn.sigmoid(g) * u

    w_spec = pl.BlockSpec((None, None, k, n), lambda j, i: (j, layer, 0, 0))
    o_spec = pl.BlockSpec((None, tm, n), lambda j, i: (j, i, 0))
    out = jax.ShapeDtypeStruct((N_SHARD, m, n), BF16)
    return _gmm(name, (N_SHARD, m // tm), [hn, wg, wu],
                [pl.BlockSpec((tm, k), lambda j, i: (i, 0)), w_spec, w_spec],
                [o_spec, o_spec, o_spec], [out, out, out], fn, ride=ride)


def _ffn_down(act, wd, resid, layer, name, ride=None):
    _, m, n = act.shape
    d = wd.shape[3]
    tm = _fit_rows(m, N_SHARD * n * _isz(act) + 2 * d * 4, N_SHARD * n * d * _isz(wd), 8 * d)

    def fn(a_ref, w_ref, r_ref):
        acc = r_ref[...]
        for j in range(N_SHARD):
            acc = acc + _nn(a_ref[j], w_ref[j])
        return (acc,)

    row = pl.BlockSpec((tm, d), lambda i: (i, 0))
    return _gmm(name, (m // tm,), [act, wd, resid],
                [pl.BlockSpec((N_SHARD, tm, n), lambda i: (0, i, 0)),
                 pl.BlockSpec((N_SHARD, None, n, d), lambda i: (0, layer, 0, 0)), row],
                row, jax.ShapeDtypeStruct((m, d), F32), fn, ride=ride)


def _ffn_down_bwd(dh, wd, g, u, layer, name, ride=None):
    m, d = dh.shape
    n = wd.shape[2]
    tm = _fit_rows(m, d * _isz(dh) + 4 * N_SHARD * n * jnp.dtype(BF16).itemsize, N_SHARD * n * d * _isz(wd),
                   2 * d + 24 * n)

    def body(dh_ref, wd_ref, g_ref, u_ref, dg_ref, du_ref):
        dhv = dh_ref[...].astype(MXU_DTYPE)
        for j in range(N_SHARD):
            dact = _nt(dhv, wd_ref[j])
            gv = g_ref[j].astype(F32)
            sg = jax.nn.sigmoid(gv)
            gs = gv * sg
            dg_ref[j] = (dact * u_ref[j].astype(F32) * (sg + gs * (1.0 - sg))).astype(dg_ref.dtype)
            du_ref[j] = (dact * gs).astype(du_ref.dtype)

    sh_spec = pl.BlockSpec((N_SHARD, tm, n), lambda i: (0, i, 0))
    out = jax.ShapeDtypeStruct((N_SHARD, m, n), BF16)
    res, rode = _pcall(body, [dh, wd, g, u], grid=(m // tm,),
                       in_specs=[pl.BlockSpec((tm, d), lambda i: (i, 0)),
                                 pl.BlockSpec((N_SHARD, None, n, d), lambda i: (0, layer, 0, 0)), sh_spec, sh_spec],
                       out_specs=[sh_spec, sh_spec], out_shape=[out, out], name=name, sem=("parallel",), ride=ride)
    return res if ride is None else (res, rode)


def _ffn_up_bwd(dg, du, wg, wu, layer, h, w, resid, name, ride=None):
    _, m, n = dg.shape
    k = wg.shape[2]
    tm = _fit_rows(m, 2 * N_SHARD * n * _isz(dg) + 3 * k * 4, 2 * N_SHARD * k * n * _isz(wg), 16 * k)

    def fn(dg_ref, du_ref, wg_ref, wu_ref):
        acc = _nt(dg_ref[0], wg_ref[0]) + _nt(du_ref[0], wu_ref[0])
        for j in range(1, N_SHARD):
            acc = acc + _nt(dg_ref[j], wg_ref[j]) + _nt(du_ref[j], wu_ref[j])
        return acc

    d_spec = pl.BlockSpec((N_SHARD, tm, n), lambda i: (0, i, 0))
    w_spec = pl.BlockSpec((N_SHARD, None, k, n), lambda i: (0, layer, 0, 0))
    return _gmm_rms(name, (m // tm,), [dg, du, wg, wu], [d_spec, d_spec, w_spec, w_spec],
                    pl.BlockSpec((tm, k), lambda i: (i, 0)), fn, h, w, resid, 0, ride=ride)


def _ffn_wgrad(lhs, rhs_list, layer, layers, prev, lhs_sharded, name):
    if lhs_sharded:
        _, m, k = lhs.shape
        n = rhs_list[0].shape[1]
    else:
        m, k = lhs.shape
        n = rhs_list[0].shape[2]
    n_out = len(rhs_list)
    tm = _fit_rows(m, k * _isz(lhs) + n_out * n * _isz(rhs_list[0]), (3 * n_out * k * n * 4) // 2,
                   2 * (k + n_out * n))
    sh = pl.BlockSpec((None, tm, k if lhs_sharded else n), lambda j, i: (j, i, 0))
    fl = pl.BlockSpec((tm, n if lhs_sharded else k), lambda j, i: (i, 0))
    n_out = len(rhs_list)
    args = [lhs] + list(rhs_list)
    in_specs = [sh if lhs_sharded else fl] + [fl if lhs_sharded else sh] * n_out
    aliases = None
    if prev is not None:
        aliases = {len(args) + t: t for t in range(n_out)}
        args = args + list(prev)
        in_specs = in_specs + [ANY] * n_out

    def fn(l_ref, *rest):
        lv = l_ref[...]
        return tuple(_tn(lv, r_ref[...]) for r_ref in rest[:n_out])

    o_spec = pl.BlockSpec((None, None, k, n), lambda j, i: (j, layer, 0, 0))
    out = jax.ShapeDtypeStruct((N_SHARD, layers, k, n), F32)
    return _gmm(name, (N_SHARD, m // tm), args, in_specs, [o_spec] * n_out, [out] * n_out, fn,
                red_axis=1, aliases=aliases)


def _ret_consts():
    log_gamma = jnp.log1p(-jnp.exp2(-5.0 - jnp.arange(RET_HEADS, dtype=F32)))
    idx = jnp.arange(CHUNK, dtype=F32)
    rel = idx[:, None] - idx[None, :]
    dmask = jnp.where((rel >= 0)[None], jnp.exp(log_gamma[:, None, None] * jnp.maximum(rel, 0.0)), 0.0)
    xi = jnp.exp(log_gamma[:, None] * (idx[None, :] + 1.0))[:, :, None]
    zeta = jnp.exp(log_gamma[:, None] * (CHUNK - 1.0 - idx[None, :]))[:, :, None]
    gamma_c = jnp.exp(log_gamma * CHUNK)
    wide = (RET_HEADS, CHUNK, RET_DK)
    return dmask, jnp.broadcast_to(xi, wide), jnp.broadcast_to(zeta, wide), gamma_c


def _rope_tables(nc):
    half = RET_DK // 2
    inv_freq = ROPE_BASE ** (-jnp.arange(half, dtype=F32) / half)
    a_chunk = (jnp.arange(nc) * CHUNK - PAD).astype(F32)[:, None] * inv_freq[None, :]
    a_row = jnp.arange(CHUNK).astype(F32)[:, None] * inv_freq[None, :]
    return (jnp.stack([jnp.cos(a_chunk), jnp.sin(a_chunk)], axis=1),
            jnp.stack([jnp.cos(a_row), jnp.sin(a_row)], axis=0))


RET_CPS = 4


def _rope_chunk(rc_ref, rr_ref, c):
    cc, sc = rc_ref[c, 0:1, :], rc_ref[c, 1:2, :]
    cr, sr = rr_ref[0], rr_ref[1]
    return cc * cr - sc * sr, sc * cr + cc * sr


def _rope_specs(order):
    half = RET_DK // 2
    return [pl.BlockSpec((RET_CPS, 2, half), lambda n: (order(n), 0, 0)),
            pl.BlockSpec((2, CHUNK, half), lambda n: (0, 0, 0))]


def _ret_specs(order):
    rows = RET_CPS * CHUNK
    return [pl.BlockSpec((rows, RET_QK), lambda n: (order(n), 0)),
            pl.BlockSpec((rows, RET_QK), lambda n: (order(n), 1)),
            pl.BlockSpec((rows, RET_V), lambda n: (order(n), 1)),
            pl.BlockSpec((rows, RET_V), lambda n: (order(n), 2))]


def _ret_const_specs():
    return [pl.BlockSpec((RET_HEADS, CHUNK, CHUNK), lambda n: (0, 0, 0)),
            pl.BlockSpec((RET_HEADS, CHUNK, RET_DK), lambda n: (0, 0, 0)),
            pl.BlockSpec((RET_HEADS, CHUNK, RET_DK), lambda n: (0, 0, 0)),
            pl.BlockSpec((1, RET_DV), lambda n: (0, 0))]


def _ret_fwd(proj, cos, sin, consts, gn_w, seq, ride=None):
    rows = proj.shape[0]
    nc = rows // CHUNK
    dmask, xi, zeta, gamma_c = consts

    def body(gam_ref, q_ref, k_ref, v_ref, g_ref, cos_ref, sin_ref, dm_ref, xi_ref, ze_ref, gn_ref,
             o_ref, y_ref, ss_ref, s_ref):
        n = pl.program_id(0)

        @pl.when(n == 0)
        def _():
            s_ref[...] = jnp.zeros_like(s_ref)

        gn = gn_ref[...]
        hs = range(RET_HEADS)
        qk_cols = [slice(h * RET_DK, (h + 1) * RET_DK) for h in hs]
        v_cols = [slice(h * RET_DV, (h + 1) * RET_DV) for h in hs]
        for c in range(RET_CPS):
            rs = slice(c * CHUNK, (c + 1) * CHUNK)
            cs, sn = _rope_chunk(cos_ref, sin_ref, c)
            kscale = _valid_rows((n * RET_CPS + c) * CHUNK, CHUNK, seq) * (RET_DK ** -0.5)
            qr_l = [_rope(q_ref[rs, col], cs, sn) for col in qk_cols]
            kr_l = [_rope(k_ref[rs, col], cs, sn) * kscale for col in qk_cols]
            v_l = [v_ref[rs, col] for col in v_cols]
            s_l = [s_ref[h] for h in hs]
            sc_l = [_nt(qr, kr) * dm_ref[h] for h, (qr, kr) in enumerate(zip(qr_l, kr_l))]
            o_l = [_nn(sc_l[h], v_l[h]) + _nn(qr_l[h] * xi_ref[h], s_l[h]) for h in hs]
            for h in hs:
                ss_ref[c, h] = s_l[h].astype(ss_ref.dtype)
                s_ref[h] = gam_ref[h] * s_l[h] + _tn(kr_l[h] * ze_ref[h], v_l[h])
                o_ref[rs, v_cols[h]] = o_l[h]
                y_ref[rs, v_cols[h]] = _gated_norm(o_l[h], g_ref[rs, v_cols[h]], gn).astype(y_ref.dtype)

    fwd = lambda n: n
    row_v = pl.BlockSpec((RET_CPS * CHUNK, RET_V), lambda n: (n, 0))
    res, rode = _pcall(
        body, [gamma_c, proj, proj, proj, proj, cos, sin, dmask, xi, zeta, gn_w.reshape(1, RET_DV)],
        grid=(nc // RET_CPS,),
        in_specs=[pl.BlockSpec(memory_space=pltpu.SMEM)] + _ret_specs(fwd) + _rope_specs(fwd)
        + _ret_const_specs(),
        out_specs=[row_v, row_v,
                   pl.BlockSpec((RET_CPS, RET_HEADS, RET_DK, RET_DV), lambda n: (n, 0, 0, 0))],
        out_shape=[jax.ShapeDtypeStruct((rows, RET_V), F32), jax.ShapeDtypeStruct((rows, RET_V), BF16),
                   jax.ShapeDtypeStruct((nc, RET_HEADS, RET_DK, RET_DV), BF16)],
        scratch=[pltpu.VMEM((RET_HEADS, RET_DK, RET_DV), F32)], name="ret_fwd", sem=("arbitrary",), ride=ride)
    return res if ride is None else (res, rode)


def _ret_bwd(proj, o, dy, states, cos, sin, consts, gn_w, seq, ride=None):
    rows = proj.shape[0]
    nc = rows // CHUNK
    dmask, xi, zeta, gamma_c = consts

    def body(gam_ref, q_ref, k_ref, v_ref, g_ref, o_ref, dy_ref, ss_ref, cos_ref, sin_ref,
             dm_ref, xi_ref, ze_ref, gn_ref, dp_ref, dgn_ref, ds_ref):
        n = pl.program_id(0)

        @pl.when(n == 0)
        def _():
            ds_ref[...] = jnp.zeros_like(ds_ref)
            dgn_ref[...] = jnp.zeros_like(dgn_ref)

        gn = gn_ref[...]
        dgn = jnp.zeros((1, RET_DV), F32)
        hs = range(RET_HEADS)
        qk_cols = [slice(h * RET_DK, (h + 1) * RET_DK) for h in hs]
        v_cols = [slice(h * RET_DV, (h + 1) * RET_DV) for h in hs]
        for c in reversed(range(RET_CPS)):
            rs = slice(c * CHUNK, (c + 1) * CHUNK)
            cs, sn = _rope_chunk(cos_ref, sin_ref, c)
            kscale = _valid_rows(((steps - 1 - n) * RET_CPS + c) * CHUNK, CHUNK, seq) * (RET_DK ** -0.5)
            qr_l = [_rope(q_ref[rs, col], cs, sn) for col in qk_cols]
            kr_l = [_rope(k_ref[rs, col], cs, sn) * kscale for col in qk_cols]
            v_l = [v_ref[rs, col] for col in v_cols]
            s_l = [ss_ref[c, h] for h in hs]
            ds_l = [ds_ref[h] for h in hs]
            sc_l = [_nt(qr_l[h], kr_l[h]) * dm_ref[h] for h in hs]
            gnb = [_gated_norm_bwd(dy_ref[rs, col], o_ref[rs, col], g_ref[rs, col], gn) for col in v_cols]
            do_l = [x[0] for x in gnb]
            dsc_l = [_nt(do_l[h], v_l[h]) * dm_ref[h] for h in hs]
            dv_l = [_tn(sc_l[h], do_l[h]) + _nn(kr_l[h] * ze_ref[h], ds_l[h]) for h in hs]
            dqr_l = [_nn(dsc_l[h], kr_l[h]) + _nt(do_l[h], s_l[h]) * xi_ref[h] for h in hs]
            dkr_l = [_tn(dsc_l[h], qr_l[h]) + _nt(v_l[h], ds_l[h]) * ze_ref[h] for h in hs]
            for h in hs:
                dgn = dgn + gnb[h][2]
                ds_ref[h] = gam_ref[h] * ds_l[h] + _tn(qr_l[h] * xi_ref[h], do_l[h])
                dp_ref[rs, qk_cols[h]] = _rope_bwd(dqr_l[h], cs, sn).astype(dp_ref.dtype)
                dp_ref[rs, RET_QK + h * RET_DK:RET_QK + (h + 1) * RET_DK] = (
                    _rope_bwd(dkr_l[h] * kscale, cs, sn).astype(dp_ref.dtype))
                dp_ref[rs, 2 * RET_QK + h * RET_DV:2 * RET_QK + (h + 1) * RET_DV] = dv_l[h].astype(dp_ref.dtype)
                dp_ref[rs, 2 * RET_QK + RET_V + h * RET_DV:2 * RET_QK + RET_V + (h + 1) * RET_DV] = (
                    gnb[h][1].astype(dp_ref.dtype))
        dgn_ref[...] += dgn

    steps = nc // RET_CPS
    rev = lambda n: steps - 1 - n
    row_v = pl.BlockSpec((RET_CPS * CHUNK, RET_V), lambda n: (rev(n), 0))
    res, rode = _pcall(
        body, [gamma_c, proj, proj, proj, proj, o, dy, states, cos, sin, dmask, xi, zeta,
               gn_w.reshape(1, RET_DV)],
        grid=(steps,),
        in_specs=[pl.BlockSpec(memory_space=pltpu.SMEM)] + _ret_specs(rev) + [
            row_v, row_v, pl.BlockSpec((RET_CPS, RET_HEADS, RET_DK, RET_DV), lambda n: (rev(n), 0, 0, 0))]
        + _rope_specs(rev) + _ret_const_specs(),
        out_specs=[pl.BlockSpec((RET_CPS * CHUNK, RET_IN), lambda n: (rev(n), 0)),
                   pl.BlockSpec((1, RET_DV), lambda n: (0, 0))],
        out_shape=[jax.ShapeDtypeStruct((rows, RET_IN), BF16), jax.ShapeDtypeStruct((1, RET_DV), F32)],
        scratch=[pltpu.VMEM((RET_HEADS, RET_DK, RET_DV), F32)], name="ret_bwd", sem=("arbitrary",), ride=ride)
    return res if ride is None else (res, rode)


GATE_COL = DN_CONV_CH // DN_V
BA_COL = (DN_CONV_CH + DN_V) // LANES
BETA_LANE, DECAY_LANE = 0, DN_HEADS
INV_SHIFT = 4
INV_SQUARINGS = INV_SHIFT - 1
assert CHUNK == 4 << INV_SHIFT


DN_CPS = 2


def _dn_in_specs(order, conv_saved=False):
    rows = DN_CPS * CHUNK
    return [pl.BlockSpec((rows, DN_CONV_CH), lambda n: (order(n), 0)),
            pl.BlockSpec((rows, DN_CONV_CH), lambda n: (order(n), 0)) if conv_saved else
            pl.BlockSpec((8, DN_CONV_CH), lambda n: (jnp.maximum(order(n) * (rows // 8) - 1, 0), 0)),
            pl.BlockSpec((rows, DN_V), lambda n: (order(n), GATE_COL)),
            pl.BlockSpec((rows, LANES), lambda n: (order(n), BA_COL)),
            pl.BlockSpec((CONV_K, 1, DN_CONV_CH), lambda n: (0, 0, 0)),
            pl.BlockSpec((1, LANES), lambda n: (0, 0)),
            pl.BlockSpec((1, LANES), lambda n: (0, 0)),
            pl.BlockSpec((1, DN_DV), lambda n: (0, 0))]


def _dn_front(c, seq, x, halo, ba, cw_ref, al_ref, dt_ref, yc=None):
    valid = _valid_rows(c * CHUNK, CHUNK, seq)
    xin = x * valid
    if yc is None:
        halo = halo * _valid_rows(c * CHUNK - 8, 8, seq)
        yc = xin * cw_ref[CONV_K - 1]
        for k in range(1, CONV_K):
            yc = yc + _shift_down(xin, halo, k) * cw_ref[CONV_K - 1 - k]
    sgc = jax.nn.sigmoid(yc)
    sig = jax.nn.sigmoid(ba)
    beta = sig * valid
    z = ba + dt_ref[...]
    eal = jnp.exp(al_ref[...])
    g = -eal * _softplus(z) * valid
    ri, ci = _iota((CHUNK, CHUNK), 0), _iota((CHUNK, CHUNK), 1)
    lower = (ri >= ci).astype(F32)
    upper = (ri <= ci).astype(F32)
    eye = (ri == ci).astype(F32)
    gam = _nn(lower, g, hi=True)
    gam_t = _tn(g, upper, hi=True)
    return dict(valid=valid, xin=xin, yc=yc, sgc=sgc, act=yc * sgc, sig=sig, beta=beta, z=z,
                eal=eal, g=g, gam=gam, gam_t=gam_t, ri=ri, ci=ci, upper=upper, eye=eye)


def _dn_head(f, h):
    act = f["act"]
    q_raw = act[:, h * DN_DK:(h + 1) * DN_DK]
    k_raw = act[:, DN_QK + h * DN_DK:DN_QK + (h + 1) * DN_DK]
    v = act[:, 2 * DN_QK + h * DN_DV:2 * DN_QK + (h + 1) * DN_DV]
    rq = lax.rsqrt(jnp.sum(q_raw * q_raw, axis=-1, keepdims=True) + RMS_EPS)
    rk = lax.rsqrt(jnp.sum(k_raw * k_raw, axis=-1, keepdims=True) + RMS_EPS)
    qh = q_raw * rq
    kn = k_raw * rk
    gam_c = _col(f["gam"], DECAY_LANE + h)
    gam_r = _row(f["gam_t"], DECAY_LANE + h)
    bc = _col(f["beta"], BETA_LANE + h)
    diff = gam_c - gam_r
    decay = jnp.where(f["ri"] >= f["ci"], jnp.exp(jnp.minimum(diff, 0.0)), 0.0)
    glast = jnp.sum(gam_r * (_iota((1, CHUNK), 1) == CHUNK - 1).astype(F32), axis=1, keepdims=True)
    return dict(rq=rq, rk=rk, qh=qh, qn=qh * (DN_DK ** -0.5), kn=kn, v=v, gam_c=gam_c, gam_r=gam_r,
                bc=bc, diff=diff, decay=decay, egam=jnp.exp(gam_c), glast=glast,
                eglast=jnp.exp(glast), ekd=jnp.exp(glast - gam_c))


def _dn_fwd(proj, conv_w, alog, dtb, norm_w, seq):
    rows = proj.shape[0]
    nc = rows // CHUNK

    def body(x_ref, halo_ref, gate_ref, ba_ref, cw_ref, al_ref, dt_ref, nw_ref,
             o_ref, y_ref, ss_ref, t_ref, yc_ref, s_ref):
        n = pl.program_id(0)

        @pl.when(n == 0)
        def _():
            s_ref[...] = jnp.zeros_like(s_ref)

        nw = nw_ref[...]
        pre = []
        for c in range(DN_CPS):
            rs = slice(c * CHUNK, (c + 1) * CHUNK)
            halo = halo_ref[...] if c == 0 else x_ref[c * CHUNK - 8:c * CHUNK, :]
            f = _dn_front(n * DN_CPS + c, seq, x_ref[rs, :], halo, ba_ref[rs, :], cw_ref, al_ref, dt_ref)
            yc_ref[rs, :] = f["yc"]
            ri, ci = f["ri"], f["ci"]
            eye = f["eye"]
            diag_m = (jnp.right_shift(ri, INV_SHIFT) == jnp.right_shift(ci, INV_SHIFT)).astype(F32)
            half_m = (jnp.right_shift(ri, INV_SHIFT + 1) == jnp.right_shift(ci, INV_SHIFT + 1)).astype(F32)
            heads = [_dn_head(f, h) for h in range(DN_HEADS)]
            a_all = [jnp.where(ri > ci, hd["bc"] * _nt(hd["kn"], hd["kn"]) * hd["decay"], 0.0) for hd in heads]
            b_all = [a * diag_m for a in a_all]
            t_all = [eye - b for b in b_all]
            for _ in range(INV_SQUARINGS):
                b_all = [_nn(b, b, hi=True) for b in b_all]
                t_all = [t + _nn(t, b, hi=True) for t, b in zip(t_all, b_all)]
            for off_m in (half_m - diag_m, 1.0 - half_m):
                x_all = [_nn(a * off_m, t, hi=True) for a, t in zip(a_all, t_all)]
                t_all = [t - _nn(t, x, hi=True) for t, x in zip(t_all, x_all)]
            u_all = [_nn(t, hd["v"] * hd["bc"], hi=True) for t, hd in zip(t_all, heads)]
            w_all = [_nn(t, hd["kn"] * (hd["bc"] * hd["egam"]), hi=True) for t, hd in zip(t_all, heads)]
            qk_all = [_nt(hd["qn"], hd["kn"]) * hd["decay"] for hd in heads]
            for h in range(DN_HEADS):
                t_ref[c, h] = t_all[h]
            pre.append((heads, u_all, w_all, qk_all))
        for c in range(DN_CPS):
            rs = slice(c * CHUNK, (c + 1) * CHUNK)
            heads, u_all, w_all, qk_all = pre[c]
            s_all = [s_ref[h] for h in range(DN_HEADS)]
            os_all = [_nn(hd["qn"] * hd["egam"], s) for hd, s in zip(heads, s_all)]
            vnew_all = [u - _nn(w, s) for u, w, s in zip(u_all, w_all, s_all)]
            o_all = [os + _nn(qk, vn) for os, qk, vn in zip(os_all, qk_all, vnew_all)]
            snew_all = [s * hd["eglast"] + _tn(hd["kn"] * hd["ekd"], vn)
                        for s, hd, vn in zip(s_all, heads, vnew_all)]
            for h in range(DN_HEADS):
                v_cols = slice(h * DN_DV, (h + 1) * DN_DV)
                ss_ref[c, h] = s_all[h]
                s_ref[h] = snew_all[h]
                o_ref[rs, v_cols] = o_all[h]
                y_ref[rs, v_cols] = _gated_norm(o_all[h], gate_ref[rs, v_cols], nw).astype(y_ref.dtype)

    fwd = lambda n: n
    row_v = pl.BlockSpec((DN_CPS * CHUNK, DN_V), lambda n: (n, 0))
    return pl.pallas_call(
        body, grid=(nc // DN_CPS,), in_specs=_dn_in_specs(fwd),
        out_specs=[row_v, row_v,
                   pl.BlockSpec((DN_CPS, DN_HEADS, DN_DK, DN_DV), lambda n: (n, 0, 0, 0)),
                   pl.BlockSpec((DN_CPS, DN_HEADS, CHUNK, CHUNK), lambda n: (n, 0, 0, 0)),
                   pl.BlockSpec((DN_CPS * CHUNK, DN_CONV_CH), lambda n: (n, 0))],
        out_shape=[jax.ShapeDtypeStruct((rows, DN_V), F32), jax.ShapeDtypeStruct((rows, DN_V), BF16),
                   jax.ShapeDtypeStruct((nc, DN_HEADS, DN_DK, DN_DV), F32),
                   jax.ShapeDtypeStruct((nc, DN_HEADS, CHUNK, CHUNK), F32),
                   jax.ShapeDtypeStruct((rows, DN_CONV_CH), F32)],
        scratch_shapes=[pltpu.VMEM((DN_HEADS, DN_DK, DN_DV), F32)],
        name="dn_fwd", compiler_params=_params("arbitrary"))(
            proj, proj, proj, proj, conv_w, alog, dtb, norm_w.reshape(1, DN_DV))


def _dn_bwd(proj, conv_out, o, dy, states, tinv, conv_w, alog, dtb, norm_w, seq):
    rows = proj.shape[0]
    nc = rows // CHUNK

    def body(x_ref, yc_ref, gate_ref, ba_ref, cw_ref, al_ref, dt_ref, nw_ref,
             o_ref, dy_ref, ss_ref, t_ref,
             dp_ref, dcw_ref, dal_ref, ddt_ref, dnw_ref, ds_ref, nxt_ref):
        n = pl.program_id(0)

        @pl.when(n == 0)
        def _():
            ds_ref[...] = jnp.zeros_like(ds_ref)
            nxt_ref[...] = jnp.zeros_like(nxt_ref)
            dcw_ref[...] = jnp.zeros_like(dcw_ref)
            dal_ref[...] = jnp.zeros_like(dal_ref)
            ddt_ref[...] = jnp.zeros_like(ddt_ref)
            dnw_ref[...] = jnp.zeros_like(dnw_ref)

        for c in reversed(range(DN_CPS)):
            rs = pl.ds(c * CHUNK, CHUNK)
            chunk((steps - 1 - n) * DN_CPS + c, x_ref.at[rs], yc_ref.at[rs], gate_ref.at[rs], ba_ref.at[rs],
                  cw_ref, al_ref, dt_ref, nw_ref, o_ref.at[rs], dy_ref.at[rs], ss_ref.at[c], t_ref.at[c],
                  dp_ref.at[rs], dcw_ref, dal_ref, ddt_ref, dnw_ref, ds_ref, nxt_ref)

    def chunk(ch, x_ref, yc_ref, gate_ref, ba_ref, cw_ref, al_ref, dt_ref, nw_ref,
              o_ref, dy_ref, ss_ref, t_ref,
              dp_ref, dcw_ref, dal_ref, ddt_ref, dnw_ref, ds_ref, nxt_ref):
        f = _dn_front(ch, seq, x_ref[...], None, ba_ref[...], cw_ref, al_ref, dt_ref, yc_ref[...])
        ri, ci = f["ri"], f["ci"]
        strict = (ri > ci).astype(F32)
        nw = nw_ref[...]
        lane128 = _iota((1, LANES), 1)
        row128 = _iota((LANES, 1), 0)
        dgam_col = jnp.zeros((CHUNK, LANES), F32)
        dgam_row = jnp.zeros((LANES, CHUNK), F32)
        dbeta = jnp.zeros((CHUNK, LANES), F32)
        dnw = jnp.zeros((1, DN_DV), F32)
        hs = range(DN_HEADS)
        heads = [_dn_head(f, h) for h in hs]
        cols = [slice(h * DN_DV, (h + 1) * DN_DV) for h in hs]
        t_l = [t_ref[h] for h in hs]
        s_l = [ss_ref[h] for h in hs]
        ds_l = [ds_ref[h] for h in hs]
        kk_l = [_nt(hd["kn"], hd["kn"]) for hd in heads]
        p_l = [_nt(hd["qn"], hd["kn"]) for hd in heads]
        rhsw_l = [hd["kn"] * (hd["bc"] * hd["egam"]) for hd in heads]
        u_l = [_nn(t, hd["v"] * hd["bc"], hi=True) for t, hd in zip(t_l, heads)]
        w_l = [_nn(t, r, hi=True) for t, r in zip(t_l, rhsw_l)]
        vnew_l = [u - _nn(w, s) for u, w, s in zip(u_l, w_l, s_l)]
        gnb = [_gated_norm_bwd(dy_ref[:, c], o_ref[:, c], gate_ref[:, c], nw) for c in cols]
        do_l = [x[0] for x in gnb]
        for h in hs:
            dp_ref[:, DN_CONV_CH + h * DN_DV:DN_CONV_CH + (h + 1) * DN_DV] = gnb[h][1].astype(dp_ref.dtype)
            dnw = dnw + gnb[h][2]
        qg_l = [hd["qn"] * hd["egam"] for hd in heads]
        kd_l = [hd["kn"] * hd["ekd"] for hd in heads]
        dvnew_l = [_tn(p * hd["decay"], do) + _nn(kd, ds)
                   for p, hd, do, kd, ds in zip(p_l, heads, do_l, kd_l, ds_l)]
        m_l = [_nt(do, vn) for do, vn in zip(do_l, vnew_l)]
        dqg_l = [_nt(do, s) for do, s in zip(do_l, s_l)]
        dkd_l = [_nt(vn, ds) for vn, ds in zip(vnew_l, ds_l)]
        for h in hs:
            ds_ref[h] = (ds_l[h] * heads[h]["eglast"] + _tn(qg_l[h], do_l[h]) - _tn(w_l[h], dvnew_l[h]))
        dw_l = [-_nt(dvn, s) for dvn, s in zip(dvnew_l, s_l)]
        dru_l = [_tn(t, dvn, hi=True) for t, dvn in zip(t_l, dvnew_l)]
        drw_l = [_tn(t, dw_, hi=True) for t, dw_ in zip(t_l, dw_l)]
        da_l = [-(_nt(dru, u) + _nt(drw, w)) * strict for dru, u, drw, w in zip(dru_l, u_l, drw_l, w_l)]
        dp_l = [m * hd["decay"] for m, hd in zip(m_l, heads)]
        dkk_l = [da * (hd["bc"] * hd["decay"]) for da, hd in zip(da_l, heads)]
        dqn_l = [dqg * hd["egam"] + _nn(dp, hd["kn"]) for dqg, hd, dp in zip(dqg_l, heads, dp_l)]
        dkn_l = [_tn(dp, hd["qn"]) + dkd * hd["ekd"] + drw * (hd["bc"] * hd["egam"])
                 + _nn(dkk, hd["kn"]) + _tn(dkk, hd["kn"])
                 for dp, hd, dkd, drw, dkk in zip(dp_l, heads, dkd_l, drw_l, dkk_l)]
        dq_parts, dk_parts, dv_parts = [], [], []
        for h in hs:
            hd = heads[h]
            kn, v, bc, egam, decay = hd["kn"], hd["v"], hd["bc"], hd["egam"], hd["decay"]
            t1 = jnp.sum(dkd_l[h] * kd_l[h], axis=1, keepdims=True)
            dglast = (jnp.sum(t1, axis=0, keepdims=True)
                      + jnp.sum(jnp.sum(ds_l[h] * s_l[h], axis=1, keepdims=True), axis=0, keepdims=True)
                      * hd["eglast"])
            e = (m_l[h] * p_l[h] + da_l[h] * (bc * kk_l[h])) * decay
            dgc = (jnp.sum(dqg_l[h] * qg_l[h], axis=1, keepdims=True) - t1
                   + jnp.sum(drw_l[h] * rhsw_l[h], axis=1, keepdims=True)
                   + jnp.sum(e, axis=1, keepdims=True)
                   + jnp.where(_iota((CHUNK, 1), 0) == CHUNK - 1, dglast, 0.0))
            dgr = -jnp.sum(e, axis=0, keepdims=True)
            dbc = (jnp.sum(dru_l[h] * v, axis=1, keepdims=True)
                   + jnp.sum(drw_l[h] * kn, axis=1, keepdims=True) * egam
                   + jnp.sum(da_l[h] * kk_l[h] * decay, axis=1, keepdims=True))
            dv_parts.append(dru_l[h] * bc)
            qh, dqn, dkn = hd["qh"], dqn_l[h], dkn_l[h]
            dq_parts.append(((DN_DK ** -0.5) * hd["rq"])
                            * (dqn - qh * jnp.sum(dqn * qh, axis=1, keepdims=True)))
            dk_parts.append(hd["rk"] * (dkn - kn * jnp.sum(dkn * kn, axis=1, keepdims=True)))
            dgam_col = dgam_col + dgc * (lane128 == DECAY_LANE + h).astype(F32)
            dbeta = dbeta + dbc * (lane128 == BETA_LANE + h).astype(F32)
            dgam_row = dgam_row + (row128 == DECAY_LANE + h).astype(F32) * dgr
        dnw_ref[...] += dnw
        dgam = dgam_col + _nt(f["eye"], dgam_row, hi=True)
        dg = _nn(f["upper"], dgam, hi=True)
        d_a = dg * (-f["eal"]) * jax.nn.sigmoid(f["z"]) * f["valid"]
        dal_ref[...] += jnp.sum(dg * f["g"], axis=0, keepdims=True)
        ddt_ref[...] += jnp.sum(d_a, axis=0, keepdims=True)
        d_b = dbeta * f["valid"] * f["sig"] * (1.0 - f["sig"])
        dp_ref[:, DN_CONV_CH + DN_V:DN_CONV_CH + DN_V + LANES] = (d_a + d_b).astype(dp_ref.dtype)
        dp_ref[:, DN_CONV_CH + DN_V + LANES:] = jnp.zeros((CHUNK, DN_IN_PAD - DN_IN_USED), dp_ref.dtype)
        dact = jnp.concatenate(dq_parts + dk_parts + dv_parts, axis=1)
        yc, sgc = f["yc"], f["sgc"]
        dyc = dact * (sgc * (1.0 + yc * (1.0 - sgc)))
        nxt = nxt_ref[...]
        ups = [dyc] + [_shift_up(dyc, nxt, j) for j in range(1, CONV_K)]
        dx = ups[0] * cw_ref[CONV_K - 1]
        for j in range(1, CONV_K):
            dx = dx + ups[j] * cw_ref[CONV_K - 1 - j]
        for j in range(CONV_K):
            dcw_ref[CONV_K - 1 - j] += jnp.sum(f["xin"] * ups[j], axis=0, keepdims=True)
        nxt_ref[...] = dyc[0:8]
        dp_ref[:, :DN_CONV_CH] = (dx * f["valid"]).astype(dp_ref.dtype)

    steps = nc // DN_CPS
    rev = lambda n: steps - 1 - n
    row_v = pl.BlockSpec((DN_CPS * CHUNK, DN_V), lambda n: (rev(n), 0))
    vec = pl.BlockSpec((1, LANES), lambda n: (0, 0))
    return pl.pallas_call(
        body, grid=(steps,),
        in_specs=_dn_in_specs(rev, conv_saved=True) + [
            row_v, row_v,
            pl.BlockSpec((DN_CPS, DN_HEADS, DN_DK, DN_DV), lambda n: (rev(n), 0, 0, 0)),
            pl.BlockSpec((DN_CPS, DN_HEADS, CHUNK, CHUNK), lambda n: (rev(n), 0, 0, 0))],
        out_specs=[pl.BlockSpec((DN_CPS * CHUNK, DN_IN_PAD), lambda n: (rev(n), 0)),
                   pl.BlockSpec((CONV_K, 1, DN_CONV_CH), lambda n: (0, 0, 0)), vec, vec,
                   pl.BlockSpec((1, DN_DV), lambda n: (0, 0))],
        out_shape=[jax.ShapeDtypeStruct((rows, DN_IN_PAD), BF16),
                   jax.ShapeDtypeStruct((CONV_K, 1, DN_CONV_CH), F32),
                   jax.ShapeDtypeStruct((1, LANES), F32), jax.ShapeDtypeStruct((1, LANES), F32),
                   jax.ShapeDtypeStruct((1, DN_DV), F32)],
        scratch_shapes=[pltpu.VMEM((DN_HEADS, DN_DK, DN_DV), F32), pltpu.VMEM((8, DN_CONV_CH), F32)],
        name="dn_bwd", compiler_params=_params("arbitrary"))(
            proj, conv_out, proj, proj, conv_w, alog, dtb, norm_w.reshape(1, DN_DV), o, dy, states, tinv)


def _train_step(x, tgt, wts, sh, idx):
    seq = x.shape[0]
    rows = -(-(seq + CHUNK) // ROW_ALIGN) * ROW_ALIGN
    wts = dict(wts)
    (h0, tgt_p), (got,) = _embed(x, tgt, wts["meta_tokens"].astype(F32), rows,
                                 ride=_Ride("gather", [sh["ret_w_in"]]))
    wts["ret_w_in"] = got.reshape(N_SHARD, D_MODEL, -1)
    cos, sin = _rope_tables(rows // CHUNK)
    consts = _ret_consts()
    conv_w = wts["dn_conv_w"].reshape(CONV_K, 1, DN_CONV_CH)
    lane_pad = LANES - 2 * DN_HEADS
    alog = jnp.pad(wts["dn_a_log"].reshape(1, DN_HEADS), ((0, 0), (DECAY_LANE, lane_pad)))
    dtb = jnp.pad(wts["dn_dt_bias"].reshape(1, DN_HEADS), ((0, 0), (DECAY_LANE, lane_pad)))
    g = {}

    hn0 = _rms_fwd(h0, wts["mix_norm_w"][0], "rms_mix0")
    proj0, got = _mm_cols(hn0, wts["ret_w_in"], "ret_in",
                          ride=_Ride("gather", [sh["ret_w_out"], sh["ffn_w_gate"]]))
    wts["ret_w_out"] = got[0].reshape(-1, D_MODEL)
    wts["ffn_w_gate"] = got[1]
    (o0, y0, st0), got = _ret_fwd(proj0, cos, sin, consts, wts["ret_gn_w"], seq,
                                  ride=_Ride("gather", [sh["ffn_w_up"], sh["ffn_w_down"]]))
    wts["ffn_w_up"], wts["ffn_w_down"] = got
    h1 = _mm(y0, wts["ret_w_out"], mode="nn", name="ret_out", resid=h0)
    hn1 = _rms_fwd(h1, wts["ffn_norm_w"][0], "rms_ffn0")
    (g0, u0, act0), got = _ffn_up(hn1, wts["ffn_w_gate"], wts["ffn_w_up"], 0, "ffn_up0",
                                  ride=_Ride("gather", [sh["dn_w_in"], sh["dn_w_out"]]))
    n_dn = sh["dn_w_in"].shape[-1]
    dn_shards = got[0].reshape(N_SHARD, D_MODEL, n_dn)
    wts["dn_w_in"] = jnp.concatenate(
        [dn_shards[j] for j in range(N_SHARD)]
        + [jnp.zeros((D_MODEL, DN_IN_PAD - N_SHARD * n_dn), dn_shards.dtype)], axis=-1)
    wts["dn_w_out"] = got[1].reshape(-1, D_MODEL)
    h2 = _ffn_down(act0, wts["ffn_w_down"], h1, 0, "ffn_down0")
    hn2 = _rms_fwd(h2, wts["mix_norm_w"][1], "rms_mix1")
    proj1 = _mm(hn2, wts["dn_w_in"], mode="nn", name="dn_in")
    o1, y1, st1, tinv, conv1 = _dn_fwd(proj1, conv_w, alog, dtb, wts["dn_norm_w"], seq)
    h3 = _mm(y1, wts["dn_w_out"], mode="nn", name="dn_out", resid=h2)
    hn3 = _rms_fwd(h3, wts["ffn_norm_w"][1], "rms_ffn1")
    g1, u1, act1 = _ffn_up(hn3, wts["ffn_w_gate"], wts["ffn_w_up"], 1, "ffn_up1")
    h4 = _ffn_down(act1, wts["ffn_w_down"], h3, 1, "ffn_down1")

    dh4, g["final_norm_w"], loss, dh4b = _final_loss(h4, wts["final_norm_w"], tgt_p, seq, "final_loss")

    layers = wts["ffn_w_gate"].shape[1]

    ffn_names = ["ffn_w_down", "ffn_w_gate", "ffn_w_up"]

    def ffn_bwd(dh_out, dhb_out, h_mid, hn, gg, uu, act, layer, prev, ride=None, last=False):
        tag = str(layer)
        res = _ffn_down_bwd(dhb_out, wts["ffn_w_down"], gg, uu, layer, "ffn_down_bwd" + tag, ride=ride)
        (dg, du), rode = res if ride is not None else (res, None)
        d_down = _ffn_wgrad(act, [dhb_out], layer, layers, prev and prev[:1], True, "ffn_dwd" + tag)
        d_gu = _ffn_wgrad(hn, [dg, du], layer, layers, prev and prev[1:], False, "ffn_dwgu" + tag)
        grads = list(d_down) + list(d_gu)
        gs = rs_grads(ffn_names, grads) if last else None
        res = _ffn_up_bwd(dg, du, wts["ffn_w_gate"], wts["ffn_w_up"], layer, h_mid, wts["ffn_norm_w"][layer],
                          dh_out, "ffn_up_bwd" + tag, ride=_Ride("pair", gs) if last else None)
        (dh_mid, d_norm, dhb_mid), sib = res if last else (res, None)
        return dh_mid, dhb_mid, grads, d_norm, rode, gs, sib

    red = {}

    def rs_grads(names, grads):
        return [gr.reshape((N_SHARD,) + sh[n].shape) for n, gr in zip(names, grads)]

    def rs_partials(names, gs, sib):
        return [_rs_pair_add(gs[t], sib[t], idx, "rs_pair_add_" + n) for t, n in enumerate(names)]

    def rs_end(names, gs, sib, others, tag):
        mine = [_rs_final_add(gs[t], sib[t], others[t], idx, "rs_final_add_" + n) for t, n in enumerate(names)]
        red.update(zip(names, _rs_share(mine, "rs_share" + tag)))

    dh3, dh3b, ffn_grads, dfn1 = ffn_bwd(dh4, dh4b, h3, hn3, g1, u1, act1, 1, None)[:4]
    dy1 = _mm(dh3b, wts["dn_w_out"], mode="nt", name="dn_out_bwd")
    d_dn_out = _mm(y1, dh3b, mode="tn", name="dn_dwo")
    dproj1, dcw, dal, ddt, g["dn_norm_w"] = _dn_bwd(proj1, conv1, o1, dy1, st1, tinv, conv_w, alog, dtb,
                                                    wts["dn_norm_w"], seq)
    d_dn_in = _mm(hn2, dproj1, mode="tn", name="dn_dwi")
    d_dn_in = jnp.stack([d_dn_in[:, j * n_dn:(j + 1) * n_dn] for j in range(N_SHARD)])
    group1 = ["dn_w_out", "dn_w_in"]
    gs1 = rs_grads(group1, [d_dn_out, d_dn_in])
    (dh2, dmn1, dh2b), sib1 = _mm_nt_rms(dproj1, wts["dn_w_in"], h2, wts["mix_norm_w"][1], dh3, "dn_in_bwd",
                                         ride=_Ride("pair", gs1))
    g["dn_conv_w"] = dcw.reshape(CONV_K, DN_CONV_CH)
    g["dn_a_log"] = dal[0, DECAY_LANE:DECAY_LANE + DN_HEADS]
    g["dn_dt_bias"] = ddt[0, DECAY_LANE:DECAY_LANE + DN_HEADS]

    dh1, dh1b, _, dfn0, others1, gs2, sib2 = ffn_bwd(
        dh2, dh2b, h1, hn1, g0, u0, act0, 0, ffn_grads,
        ride=_Ride("chips", rs_partials(group1, gs1, sib1)), last=True)
    rs_end(group1, gs1, sib1, others1, "1")
    d_ret_out = _mm(y0, dh1b, mode="tn", name="ret_dwo")
    gs2b = rs_grads(["ret_w_out"], [d_ret_out])
    dy0, sib2b = _mm(dh1b, wts["ret_w_out"], mode="nt", name="ret_out_bwd", ride=_Ride("pair", gs2b))
    group2 = ffn_names + ["ret_w_out"]
    gs2, sib2 = gs2 + gs2b, list(sib2) + list(sib2b)
    (dproj0, g["ret_gn_w"]), others2 = _ret_bwd(proj0, o0, dy0, st0, cos, sin, consts, wts["ret_gn_w"], seq,
                                                ride=_Ride("chips", rs_partials(group2, gs2, sib2)))
    rs_end(group2, gs2, sib2, others2, "2")
    d_ret_in = _mm_cols_grad(hn0, dproj0, "ret_dwi")
    gs3 = rs_grads(["ret_w_in"], [d_ret_in])
    sib3 = _rs_pair(gs3, "rs_pair3")
    (dh0, dmn0, _), others3 = _mm_cols_t_rms(dproj0, wts["ret_w_in"], h0, wts["mix_norm_w"][0], dh1, "ret_in_bwd",
                                             ride=_Ride("chips", rs_partials(["ret_w_in"], gs3, sib3)))
    rs_end(["ret_w_in"], gs3, sib3, others3, "3")

    g["ffn_norm_w"] = jnp.concatenate([dfn0, dfn1], axis=0)
    g["mix_norm_w"] = jnp.concatenate([dmn0, dmn1], axis=0)
    g["meta_tokens"] = dh0[PAD:CHUNK]
    g["final_norm_w"] = g["final_norm_w"].reshape(D_MODEL)
    g["ret_gn_w"] = g["ret_gn_w"].reshape(RET_DV)
    g["dn_norm_w"] = g["dn_norm_w"].reshape(DN_DV)
    return loss, dh0, g, red


def _mesh_pos():
    return lax.axis_index("x"), lax.axis_index("y"), lax.axis_index("c")


def _other_chips(x, y):
    return [(1 - x, y), (x, 1 - y), (1 - x, 1 - y)]


def _remote(src, dst, send_sem, recv_sem, to):
    return pltpu.make_async_remote_copy(src_ref=src, dst_ref=dst, send_sem=send_sem, recv_sem=recv_sem,
                                        device_id=to, device_id_type=MESH)


GATHER_COPIES = 7


def _gather_phase(phase, ins, outs, send_sems, recv_sems):
    x, y, c = _mesh_pos()
    me = 2 * x + y
    chips = _other_chips(x, y)
    sibling = (x, y, 1 - c)

    def cp(t, k, src, dst, to):
        i = GATHER_COPIES * t + k
        return _remote(src, dst, send_sems.at[i], recv_sems.at[i], to)

    for t in range(len(ins)):
        own = cp(t, 0, ins[t], outs[t].at[me], sibling)
        if phase == 0:
            own.start()
        if phase == 2:
            own.wait()
        for k, (px, py) in enumerate(chips):
            landed = outs[t].at[2 * px + py, c]
            theirs = outs[t].at[2 * px + py, 1 - c]
            to_chip = cp(t, 1 + k, ins[t].at[c], outs[t].at[me, c], (px, py, c))
            if phase == 0:
                to_chip.start()
            if phase == 1:
                cp(t, 1 + k, ins[t].at[c], landed, (px, py, c)).wait_recv()
                cp(t, 4 + k, landed, landed, sibling).start()
            if phase == 2:
                to_chip.wait_send()
                cp(t, 4 + k, landed, landed, sibling).wait_send()
                cp(t, 4 + k, theirs, theirs, sibling).wait_recv()


def _chips_phase(phase, ins, outs, send_sems, recv_sems):
    x, y, c = _mesh_pos()
    for t in range(len(ins)):
        for k, (px, py) in enumerate(_other_chips(x, y)):
            cp = _remote(ins[t].at[2 * px + py], outs[t].at[k], send_sems.at[3 * t + k], recv_sems.at[3 * t + k],
                         (px, py, c))
            if phase == 0:
                cp.start()
            if phase == 2:
                cp.wait()


class _Ride:
    def __init__(self, kind, arrays):
        self.kind, self.arrays = kind, list(arrays)
        nt = len(self.arrays)
        if kind == "gather":
            self.phase_fn, n_sem = _gather_phase, GATHER_COPIES * nt
            self.out_shape = [jax.ShapeDtypeStruct((N_SHARD,) + a.shape, a.dtype) for a in self.arrays]
        elif kind == "pair":
            self.phase_fn, n_sem = _pair_phase, nt
            self.out_shape = [jax.ShapeDtypeStruct(a.shape[:1] + a.shape[2:], a.dtype) for a in self.arrays]
        else:
            self.phase_fn, n_sem = _chips_phase, 3 * nt
            self.out_shape = [jax.ShapeDtypeStruct((3,) + a.shape[1:], a.dtype) for a in self.arrays]
        self.in_specs, self.out_specs = [ANY] * nt, [ANY] * nt
        self.scratch = [pltpu.SemaphoreType.DMA((n_sem,)), pltpu.SemaphoreType.DMA((n_sem,))]

    def emit(self, step, nsteps, ins, outs, send_sems, recv_sems, before):
        mid = max(0, min((7 * nsteps) // 8, nsteps - 2))
        todo = [(0, 0), (1, mid)] if before else [(2, nsteps - 1)]
        for phase, at in todo:
            if phase == 1 and self.kind != "gather":
                continue

            @pl.when(step == at)
            def _(phase=phase):
                self.phase_fn(phase, ins, outs, send_sems, recv_sems)


def _gather_small(blk):
    r, wd = blk.shape

    def body(b_ref, out_ref, send_sems, recv_sems):
        x, y, c = _mesh_pos()
        chips = _other_chips(x, y)
        out_ref[2 * x + y] = b_ref[...]
        sends = [_remote(b_ref, out_ref.at[2 * x + y], send_sems.at[k], recv_sems.at[k], (px, py, c))
                 for k, (px, py) in enumerate(chips)]
        for cp in sends:
            cp.start()
        for k, (px, py) in enumerate(chips):
            _remote(b_ref, out_ref.at[2 * px + py], send_sems.at[k], recv_sems.at[k], (px, py, c)).wait_recv()
        for cp in sends:
            cp.wait_send()

    return pl.pallas_call(
        body, out_shape=jax.ShapeDtypeStruct((4, r, wd), blk.dtype), in_specs=[VMEM_SPEC], out_specs=VMEM_SPEC,
        scratch_shapes=[pltpu.SemaphoreType.DMA((3,)), pltpu.SemaphoreType.DMA((3,))],
        name="gather_small")(blk)


def _allreduce_small(blk):
    r, wd = blk.shape
    rels = [(dx, dy, dc) for dx in (0, 1) for dy in (0, 1) for dc in (0, 1) if dx or dy or dc]

    def body(b_ref, out_ref, buf_ref, send_sems, recv_sems):
        x, y, c = _mesh_pos()

        def peer(rel):
            dx, dy, dc = rel
            return (1 - x if dx else x, 1 - y if dy else y, 1 - c if dc else c)

        me = 4 * x + 2 * y + c
        buf_ref[me] = b_ref[...]
        sends = [_remote(b_ref, buf_ref.at[me], send_sems.at[k], recv_sems.at[k], peer(rel))
                 for k, rel in enumerate(rels)]
        for cp in sends:
            cp.start()
        for k, rel in enumerate(rels):
            px, py, pc = peer(rel)
            _remote(b_ref, buf_ref.at[4 * px + 2 * py + pc], send_sems.at[k], recv_sems.at[k],
                    (px, py, pc)).wait_recv()
        for cp in sends:
            cp.wait_send()
        acc = buf_ref[0]
        for d in range(1, 8):
            acc = acc + buf_ref[d]
        out_ref[...] = acc

    return pl.pallas_call(
        body, out_shape=jax.ShapeDtypeStruct((r, wd), blk.dtype), in_specs=[VMEM_SPEC], out_specs=VMEM_SPEC,
        scratch_shapes=[pltpu.VMEM((8, r, wd), blk.dtype), pltpu.SemaphoreType.DMA((7,)),
                        pltpu.SemaphoreType.DMA((7,))],
        name="allreduce_small")(blk)


def _rs_pair(gs, name):
    ride = _Ride("pair", gs)

    def body(*refs):
        nt = len(gs)
        for phase in (0, 2):
            _pair_phase(phase, refs[:nt], refs[nt:2 * nt], *refs[2 * nt:])

    return pl.pallas_call(body, out_shape=ride.out_shape, in_specs=ride.in_specs, out_specs=ride.out_specs,
                          scratch_shapes=ride.scratch, name=name)(*gs)


def _pair_phase(phase, ins, outs, send_sems, recv_sems):
    x, y, c = _mesh_pos()
    for t in range(len(ins)):
        cp = _remote(ins[t].at[:, 1 - c], outs[t], send_sems.at[t], recv_sems.at[t], (x, y, 1 - c))
        if phase == 0:
            cp.start()
        if phase == 2:
            cp.wait()


def _rs_tile(a, b):
    return _div_tile(a, 512 if b <= 1024 else 256, 16)


def _rs_pair_add(g, a, idx, name):
    _, _, rows, cols = g.shape
    tr = _rs_tile(rows, cols)

    def body(s_ref, g_ref, a_ref, p_ref):
        p_ref[...] = (g_ref[...] + a_ref[...]).astype(p_ref.dtype)

    blk = pl.BlockSpec((None, tr, cols), lambda j, i, s: (j, i, 0))
    spec = pltpu.PrefetchScalarGridSpec(
        num_scalar_prefetch=1, grid=(N_SHARD, rows // tr),
        in_specs=[pl.BlockSpec((None, None, tr, cols), lambda j, i, s: (j, s[0], i, 0)), blk], out_specs=blk)
    return pl.pallas_call(
        body, grid_spec=spec, out_shape=jax.ShapeDtypeStruct((N_SHARD, rows, cols), BF16), name=name,
        compiler_params=_params("parallel", "parallel"))(idx, g, a)


def _rs_final_add(g, a, b, idx, name):
    _, _, rows, cols = g.shape
    tr = _rs_tile(rows, cols)

    def body(s_ref, g_ref, a_ref, b0_ref, b1_ref, b2_ref, f_ref):
        own = g_ref[...] + a_ref[...]
        f_ref[...] = ((own + b0_ref[...].astype(F32)) + b1_ref[...].astype(F32)) + b2_ref[...].astype(F32)

    def b_spec(k):
        return pl.BlockSpec((None, tr, cols), lambda i, s: (k, i, 0))

    spec = pltpu.PrefetchScalarGridSpec(
        num_scalar_prefetch=1, grid=(rows // tr,),
        in_specs=[pl.BlockSpec((None, None, tr, cols), lambda i, s: (s[1], s[0], i, 0)),
                  pl.BlockSpec((None, tr, cols), lambda i, s: (s[1], i, 0)), b_spec(0), b_spec(1), b_spec(2)],
        out_specs=pl.BlockSpec((None, tr, cols), lambda i, s: (s[0], i, 0)))
    return pl.pallas_call(
        body, grid_spec=spec, out_shape=jax.ShapeDtypeStruct((2, rows, cols), F32), name=name,
        compiler_params=_params("parallel"))(idx, g, a, b, b, b)


def _rs_share(fs, name):
    nt = len(fs)

    def body(*refs):
        outs = refs[nt:2 * nt]
        send_sems, recv_sems = refs[2 * nt:]
        x, y, c = _mesh_pos()
        cps = [_remote(outs[t].at[c], outs[t].at[c], send_sems.at[t], recv_sems.at[t], (x, y, 1 - c))
               for t in range(nt)]
        for cp in cps:
            cp.start()
        for cp in cps:
            cp.wait()

    return pl.pallas_call(
        body, out_shape=[jax.ShapeDtypeStruct(f.shape, f.dtype) for f in fs],
        in_specs=[ANY] * nt, out_specs=[ANY] * nt, input_output_aliases={t: t for t in range(nt)},
        scratch_shapes=[pltpu.SemaphoreType.DMA((nt,)), pltpu.SemaphoreType.DMA((nt,))], name=name)(*fs)


def _adamw(w, g, m, v, name):
    lead, rows, cols = w.shape
    tr = rows // 4 if rows % 32 == 0 else rows

    def body(w_ref, g_ref, m_ref, v_ref, go_ref, d_ref, mo_ref, vo_ref):
        gv = g_ref[...]
        go_ref[...] = gv
        mn = ADAM_B1 * m_ref[...] + (1.0 - ADAM_B1) * gv
        vn = ADAM_B2 * v_ref[...] + (1.0 - ADAM_B2) * (gv * gv)
        m_hat = mn / (1.0 - ADAM_B1 ** ADAM_STEP)
        v_hat = vn / (1.0 - ADAM_B2 ** ADAM_STEP)
        d_ref[...] = -ADAM_LR * (m_hat / (jnp.sqrt(v_hat) + ADAM_EPS) + ADAM_WD * w_ref[...])
        mo_ref[...] = mn
        vo_ref[...] = vn

    blk = pl.BlockSpec((None, tr, cols), lambda l, i: (l, i, 0))
    out = jax.ShapeDtypeStruct((lead, rows, cols), F32)
    return pl.pallas_call(
        body, grid=(lead, rows // tr), in_specs=[blk] * 4, out_specs=[blk] * 4, out_shape=[out] * 4, name=name,
        compiler_params=_params("parallel", "parallel"))(w, g, m, v)


BIG = ["ret_w_in", "ret_w_out", "dn_w_in", "dn_w_out", "ffn_w_gate", "ffn_w_up", "ffn_w_down"]
TRANSPOSED_AT_BOUNDARY = {"dn_w_in": True, "ffn_w_gate": False, "ffn_w_up": False}
SMALL =["meta_tokens", "mix_norm_w", "ffn_norm_w", "ret_gn_w", "dn_conv_w", "dn_a_log", "dn_dt_bias",
         "dn_norm_w", "final_norm_w"]
SMALL_SHARDED = {"meta_tokens", "dn_conv_w", "dn_norm_w"}
ORDER = ["meta_tokens", "mix_norm_w", "ffn_norm_w", "ret_w_in", "ret_gn_w", "ret_w_out", "dn_w_in",
         "dn_conv_w", "dn_a_log", "dn_dt_bias", "dn_norm_w", "dn_w_out", "ffn_w_gate", "ffn_w_up",
         "ffn_w_down", "final_norm_w"]


def _halves(a):
    return a.reshape(2, -1, a.shape[-1])


def _pack_lanes(parts, align=8):
    flat = jnp.concatenate([p.reshape(-1) for p in parts])
    flat = jnp.pad(flat, (0, -flat.shape[0] % (align * LANES)))
    return flat.reshape(-1, LANES)


def _unpack(buf, shapes):
    lead = buf.shape[:-2]
    flat = buf.reshape(lead + (-1,))
    out, off = [], 0
    for shp in shapes:
        size = math.prod(shp)
        out.append(flat[..., off:off + size].reshape(lead + tuple(shp)))
        off += size
    return out


def _join_cols(shards):
    return jnp.concatenate([shards[j] for j in range(N_SHARD)], axis=-1)


def kernel(x, meta_tokens, mix_norm_w, ffn_norm_w, ret_w_in, ret_gn_w, ret_w_out, dn_w_in, dn_conv_w, dn_a_log, dn_dt_bias, dn_norm_w, dn_w_out, ffn_w_gate, ffn_w_up, ffn_w_down, final_norm_w, loss_target, m_meta_tokens, m_mix_norm_w, m_ffn_norm_w, m_ret_w_in, m_ret_gn_w, m_ret_w_out, m_dn_w_in, m_dn_conv_w, m_dn_a_log, m_dn_dt_bias, m_dn_norm_w, m_dn_w_out, m_ffn_w_gate, m_ffn_w_up, m_ffn_w_down, m_final_norm_w, v_meta_tokens, v_mix_norm_w, v_ffn_norm_w, v_ret_w_in, v_ret_gn_w, v_ret_w_out, v_dn_w_in, v_dn_conv_w, v_dn_a_log, v_dn_dt_bias, v_dn_norm_w, v_dn_w_out, v_ffn_w_gate, v_ffn_w_up, v_ffn_w_down, v_final_norm_w):
    w = dict(meta_tokens=meta_tokens, mix_norm_w=mix_norm_w, ffn_norm_w=ffn_norm_w, ret_w_in=ret_w_in,
             ret_gn_w=ret_gn_w, ret_w_out=ret_w_out, dn_w_in=dn_w_in, dn_conv_w=dn_conv_w, dn_a_log=dn_a_log,
             dn_dt_bias=dn_dt_bias, dn_norm_w=dn_norm_w, dn_w_out=dn_w_out, ffn_w_gate=ffn_w_gate,
             ffn_w_up=ffn_w_up, ffn_w_down=ffn_w_down, final_norm_w=final_norm_w)
    m = dict(meta_tokens=m_meta_tokens, mix_norm_w=m_mix_norm_w, ffn_norm_w=m_ffn_norm_w, ret_w_in=m_ret_w_in,
             ret_gn_w=m_ret_gn_w, ret_w_out=m_ret_w_out, dn_w_in=m_dn_w_in, dn_conv_w=m_dn_conv_w,
             dn_a_log=m_dn_a_log, dn_dt_bias=m_dn_dt_bias, dn_norm_w=m_dn_norm_w, dn_w_out=m_dn_w_out,
             ffn_w_gate=m_ffn_w_gate, ffn_w_up=m_ffn_w_up, ffn_w_down=m_ffn_w_down, final_norm_w=m_final_norm_w)
    v = dict(meta_tokens=v_meta_tokens, mix_norm_w=v_mix_norm_w, ffn_norm_w=v_ffn_norm_w, ret_w_in=v_ret_w_in,
             ret_gn_w=v_ret_gn_w, ret_w_out=v_ret_w_out, dn_w_in=v_dn_w_in, dn_conv_w=v_dn_conv_w,
             dn_a_log=v_dn_a_log, dn_dt_bias=v_dn_dt_bias, dn_norm_w=v_dn_norm_w, dn_w_out=v_dn_w_out,
             ffn_w_gate=v_ffn_w_gate, ffn_w_up=v_ffn_w_up, ffn_w_down=v_ffn_w_down, final_norm_w=v_final_norm_w)
    mx, my, mc = _mesh_pos()
    chip = 2 * mx + my

    sm_names = [n for n in SMALL if n in SMALL_SHARDED]
    sm_gathered = _unpack(_gather_small(_pack_lanes([w[n] for n in sm_names])), [w[n].shape for n in sm_names])
    full = {n: _join_cols(sm_gathered[i]) for i, n in enumerate(sm_names)}
    wts = {
        "meta_tokens": full["meta_tokens"], "mix_norm_w": mix_norm_w, "ffn_norm_w": ffn_norm_w,
        "ret_gn_w": ret_gn_w[0], "final_norm_w": final_norm_w, "dn_conv_w": full["dn_conv_w"][0],
        "dn_a_log": dn_a_log[0], "dn_dt_bias": dn_dt_bias[0], "dn_norm_w": full["dn_norm_w"][0],
    }
    idx = jnp.stack([mc, chip]).astype(jnp.int32)
    shards = {n: _halves(w[n].astype(MXU_DTYPE)) for n in BIG}
    loss_part, dh0, g, reduced = _train_step(x[0], loss_target[0], wts, shards, idx)
    seq = x.shape[1]
    grad_x = dh0[CHUNK:CHUNK + seq].reshape(x.shape)
    gsh = {}

    small_full_shapes = [g[n].shape for n in SMALL] + [(1,)]
    red = _unpack(_allreduce_small(_pack_lanes([g[n] for n in SMALL] + [loss_part[0, :1]])), small_full_shapes)
    loss = red[-1][0]
    for i, n in enumerate(SMALL):
        gn = red[i]
        if n in SMALL_SHARDED:
            width = w[n].shape[-1]
            gn = lax.dynamic_slice_in_dim(gn, chip * width, width, axis=gn.ndim - 1)
        gsh[n] = gn.reshape(w[n].shape)

    delta, new_m, new_v = {}, {}, {}
    for n in BIG:
        shp = w[n].shape
        if n in TRANSPOSED_AT_BOUNDARY and TRANSPOSED_AT_BOUNDARY[n]:
            view = lambda a: jnp.swapaxes(a, 1, 2).reshape(1, -1, LANES)
            back = lambda a: jnp.swapaxes(a.reshape(shp[0], shp[2], shp[1]), 1, 2)
        elif n in TRANSPOSED_AT_BOUNDARY:
            view = back = lambda a: jnp.swapaxes(a, 1, 2)
        else:
            view = back = lambda a: a
        res = _adamw(view(w[n]), view(reduced[n].reshape(shp)), view(m[n]), view(v[n]), "adamw_" + n)
        gsh[n], delta[n], new_m[n], new_v[n] = [back(r) for r in res]
    sm_local_shapes = [w[n].shape for n in SMALL]
    _, d_, m_, v_ = _adamw(*[_pack_lanes([t[n] for n in SMALL])[None] for t in (w, gsh, m, v)], "adamw_small")
    d_, m_, v_ = d_[0], m_[0], v_[0]
    for n, dd, mm, vv in zip(SMALL, _unpack(d_, sm_local_shapes), _unpack(m_, sm_local_shapes),
                             _unpack(v_, sm_local_shapes)):
        delta[n], new_m[n], new_v[n] = dd, mm, vv

    return (loss, grad_x, *[gsh[n] for n in ORDER], *[delta[n] for n in ORDER],
            *[new_m[n] for n in ORDER], *[new_v[n] for n in ORDER])
```

```python
import math

import jax
import jax.numpy as jnp
from jax import lax
from jax.experimental import pallas as pl
from jax.experimental.pallas import tpu as pltpu

F32 = jnp.float32
BF16 = jnp.bfloat16
MXU_DTYPE = BF16

D_MODEL = 1024
N_META = 16
CHUNK = 64
PAD = CHUNK - N_META
RMS_EPS = 1e-6
RET_HEADS, RET_DK, RET_DV = 4, 256, 512
RET_QK, RET_V = RET_HEADS * RET_DK, RET_HEADS * RET_DV
RET_IN = 2 * RET_QK + 2 * RET_V
ROPE_BASE = 10000.0
DN_HEADS, DN_DK, DN_DV = 8, 128, 256
DN_QK, DN_V = DN_HEADS * DN_DK, DN_HEADS * DN_DV
DN_CONV_CH = 2 * DN_QK + DN_V
DN_IN = DN_CONV_CH + DN_V + 2 * DN_HEADS
LANES = 128
DN_IN_USED = DN_CONV_CH + DN_V + LANES
DN_IN_PAD = DN_IN_USED + LANES
CONV_K = 4
FFN_HIDDEN = 2816
ADAM_LR, ADAM_B1, ADAM_B2, ADAM_EPS, ADAM_WD, ADAM_STEP = 0.001, 0.9, 0.999, 1e-08, 0.01, 10

ROW_ALIGN = 256
VMEM_LIMIT = 56 * 1024 * 1024
MESH = pl.DeviceIdType.MESH
ANY = pl.BlockSpec(memory_space=pl.ANY)
VMEM_SPEC = pl.BlockSpec(memory_space=pltpu.VMEM)
_HI = lax.Precision.HIGHEST


def _params(*sem):
    return pltpu.CompilerParams(dimension_semantics=sem, vmem_limit_bytes=VMEM_LIMIT)


def _dg(a, b, ca, cb, hi):
    dims = (((ca,), (cb,)), ((), ()))

    def dot(p, q):
        return lax.dot_general(p, q, dims, preferred_element_type=F32)

    if not hi:
        return dot(a.astype(MXU_DTYPE), b.astype(MXU_DTYPE))
    if MXU_DTYPE == F32:
        return lax.dot_general(a, b, dims, precision=_HI, preferred_element_type=F32)
    a_hi, b_hi = a.astype(MXU_DTYPE), b.astype(MXU_DTYPE)
    a_lo = (a - a_hi.astype(F32)).astype(MXU_DTYPE)
    b_lo = (b - b_hi.astype(F32)).astype(MXU_DTYPE)
    return dot(a_hi, b_hi) + (dot(a_hi, b_lo) + dot(a_lo, b_hi))


def _nn(a, b, hi=False):
    return _dg(a, b, 1, 0, hi)


def _nt(a, b, hi=False):
    return _dg(a, b, 1, 1, hi)


def _tn(a, b, hi=False):
    return _dg(a, b, 0, 0, hi)


def _iota(shape, dim):
    return lax.broadcasted_iota(jnp.int32, shape, dim)


def _valid_rows(first_row, rows, seq):
    r = first_row + _iota((rows, 1), 0)
    return ((r >= PAD) & (r < CHUNK + seq)).astype(F32)


def _rope(t, cs, sn):
    half = t.shape[-1] // 2
    t1, t2 = t[:, :half], t[:, half:]
    return jnp.concatenate([t1 * cs - t2 * sn, t1 * sn + t2 * cs], axis=1)


def _rope_bwd(d, cs, sn):
    half = d.shape[-1] // 2
    d1, d2 = d[:, :half], d[:, half:]
    return jnp.concatenate([d1 * cs + d2 * sn, d2 * cs - d1 * sn], axis=1)


def _col(x, idx):
    oh = (_iota((1, x.shape[1]), 1) == idx).astype(F32)
    return jnp.sum(x * oh, axis=1, keepdims=True)


def _row(x, idx):
    oh = (_iota((x.shape[0], 1), 0) == idx).astype(F32)
    return jnp.sum(x * oh, axis=0, keepdims=True)


def _shift_down(x, halo8, k):
    xr = pltpu.roll(x, k, 0)
    hr = pltpu.roll(halo8, k, 0)
    first = jnp.where(_iota((8, 1), 0) < k, hr, xr[0:8])
    return jnp.concatenate([first, xr[8:]], axis=0)


def _shift_up(x, next8, j):
    rows = x.shape[0]
    xr = pltpu.roll(x, rows - j, 0)
    nr = pltpu.roll(next8, 8 - j, 0)
    last = jnp.where(_iota((8, 1), 0) >= 8 - j, nr, xr[rows - 8:])
    return jnp.concatenate([xr[:rows - 8], last], axis=0)


def _gated_norm(o, gate, w):
    r = lax.rsqrt(jnp.mean(o * o, axis=-1, keepdims=True) + RMS_EPS)
    return o * r * w * (gate * jax.nn.sigmoid(gate))


def _gated_norm_bwd(dy, o, gate, w):
    r = lax.rsqrt(jnp.mean(o * o, axis=-1, keepdims=True) + RMS_EPS)
    nrm = o * r
    sg = jax.nn.sigmoid(gate)
    sl = gate * sg
    dgate = dy * nrm * w * (sg * (1.0 + gate * (1.0 - sg)))
    dn = dy * w * sl
    dw = jnp.sum(dy * nrm * sl, axis=0, keepdims=True)
    do = r * (dn - nrm * jnp.mean(dn * nrm, axis=-1, keepdims=True))
    return do, dgate, dw


def _softplus(z):
    return jnp.maximum(z, 0.0) + jnp.log(1.0 + jnp.exp(-jnp.abs(z)))


def _row_tile(rows, cap=768):
    for t in (768, 512, 256, 128, 64, 32, 16, 8):
        if t <= cap and rows % t == 0:
            return t
    return rows


TILE_BUDGET = 50 * 1024 * 1024


def _fit_rows(rows, row_bytes, fixed_bytes, value_row_bytes):
    best = None
    for t in range(LANES, rows + 1, LANES):
        if rows % t == 0 and 2 * (row_bytes * t + fixed_bytes) + value_row_bytes * t <= TILE_BUDGET:
            best = t
    return best or _row_tile(rows, 256)


def _div_tile(n, cap, mult):
    best = None
    for t in range(mult, min(cap, n) + 1, mult):
        if n % t == 0:
            best = t
    return best or n


def _col_tile(cols, cap=1536):
    best = None
    for t in range(LANES, min(cap, cols) + 1, LANES):
        if cols % t == 0:
            best = t
    return best or cols


def _embed(x, tgt, meta, rows, ride=None):
    seq, d = x.shape
    n_tok = seq // CHUNK

    def body(xa_ref, xb_ref, ta_ref, tb_ref, m_ref, h_ref, tp_ref):
        i = pl.program_id(0)
        first = jnp.concatenate([jnp.zeros((PAD, d), F32), m_ref[...]], axis=0)
        for half, (x_ref, t_ref) in enumerate(((xa_ref, ta_ref), (xb_ref, tb_ref))):
            k = 2 * i + half
            tokens = (k >= 1) & (k <= n_tok)
            rs = slice(half * CHUNK, (half + 1) * CHUNK)
            h_ref[rs, :] = jnp.where(k == 0, first, jnp.where(tokens, x_ref[...], 0.0))
            tp_ref[rs, :] = jnp.where(tokens, t_ref[...], 0.0)

    def tok(half):
        return pl.BlockSpec((CHUNK, d), lambda i: (jnp.clip(2 * i + half - 1, 0, n_tok - 1), 0))

    out = pl.BlockSpec((2 * CHUNK, d), lambda i: (i, 0))
    res, rode = _pcall(body, [x, x, tgt, tgt, meta], grid=(rows // (2 * CHUNK),),
                       in_specs=[tok(0), tok(1), tok(0), tok(1), pl.BlockSpec((N_META, d), lambda i: (0, 0))],
                       out_specs=[out, out], out_shape=[jax.ShapeDtypeStruct((rows, d), F32)] * 2, name="embed",
                       sem=("parallel",), ride=ride)
    return res if ride is None else (res, rode)


def _rms_fwd(h, w, name, ride=None):
    rows, d = h.shape
    tm = _row_tile(rows)

    def body(h_ref, w_ref, o_ref):
        x = h_ref[...]
        r = lax.rsqrt(jnp.mean(x * x, axis=-1, keepdims=True) + RMS_EPS)
        o_ref[...] = (x * r * w_ref[...]).astype(o_ref.dtype)

    res, rode = _pcall(body, [h, w.reshape(1, d)], grid=(rows // tm,),
                       in_specs=[pl.BlockSpec((tm, d), lambda i: (i, 0)), pl.BlockSpec((1, d), lambda i: (0, 0))],
                       out_specs=[pl.BlockSpec((tm, d), lambda i: (i, 0))],
                       out_shape=[jax.ShapeDtypeStruct((rows, d), BF16)], name=name, sem=("parallel",), ride=ride)
    return res[0] if ride is None else (res[0], rode)


def _gmm_rms(name, grid, args, in_specs, row_spec, fn, h, w, resid, row_axis, red_axis=None, ride=None):
    m, d = h.shape
    n_in = len(args)
    vec = pl.BlockSpec((1, d), lambda *g: (0, 0))

    def body(*refs):
        ins = refs[:n_in]
        h_ref, w_ref, r_ref, dh_ref, dw_ref, dh16_ref = refs[n_in:]
        part = fn(*ins)
        row = pl.program_id(row_axis)

        def finish(dy):
            x = h_ref[...]
            r = lax.rsqrt(jnp.mean(x * x, axis=-1, keepdims=True) + RMS_EPS)
            xh = x * r
            dxh = dy * w_ref[...]
            dh = r_ref[...] + r * (dxh - xh * jnp.mean(dxh * xh, axis=-1, keepdims=True))
            dh_ref[...] = dh
            dh16_ref[...] = dh.astype(dh16_ref.dtype)
            dwp = jnp.sum(dy * xh, axis=0, keepdims=True)

            @pl.when(row == 0)
            def _():
                dw_ref[...] = dwp

            @pl.when(row > 0)
            def _():
                dw_ref[...] += dwp

        if red_axis is None:
            finish(part)
            return
        k = pl.program_id(red_axis)

        @pl.when(k == 0)
        def _():
            dh_ref[...] = part

        @pl.when(k > 0)
        def _():
            dh_ref[...] += part

        @pl.when(k == grid[red_axis] - 1)
        def _():
            finish(dh_ref[...])

    res, rode = _pcall(body, list(args) + [h, w.reshape(1, d), resid], grid=grid,
                       in_specs=list(in_specs) + [row_spec, vec, row_spec], out_specs=[row_spec, vec, row_spec],
                       out_shape=[jax.ShapeDtypeStruct((m, d), F32), jax.ShapeDtypeStruct((1, d), F32),
                                  jax.ShapeDtypeStruct((m, d), BF16)],
                       name=name, sem=("arbitrary",) * len(grid), ride=ride)
    return res if ride is None else (res, rode)


def _final_loss(h, w, tgt, seq, name):
    rows, d = h.shape
    tm = _row_tile(rows)

    def body(h_ref, w_ref, t_ref, dh_ref, dw_ref, loss_ref, dh16_ref):
        i = pl.program_id(0)
        r_idx = i * tm + _iota((tm, 1), 0)
        m = ((r_idx >= CHUNK) & (r_idx < CHUNK + seq)).astype(F32)
        x = h_ref[...]
        wv = w_ref[...]
        r = lax.rsqrt(jnp.mean(x * x, axis=-1, keepdims=True) + RMS_EPS)
        xh = x * r
        err = (xh * wv - t_ref[...]) * m
        lpart = 0.5 * jnp.sum(jnp.mean(err * err, axis=-1, keepdims=True), axis=0, keepdims=True)
        dyv = err * (1.0 / d)
        dxh = dyv * wv
        dh = r * (dxh - xh * jnp.mean(dxh * xh, axis=-1, keepdims=True))
        dh_ref[...] = dh
        dh16_ref[...] = dh.astype(dh16_ref.dtype)
        part = jnp.sum(dyv * xh, axis=0, keepdims=True)

        @pl.when(i == 0)
        def _():
            dw_ref[...] = part
            loss_ref[...] = jnp.broadcast_to(lpart, loss_ref.shape)

        @pl.when(i > 0)
        def _():
            dw_ref[...] += part
            loss_ref[...] += jnp.broadcast_to(lpart, loss_ref.shape)

    blk = pl.BlockSpec((tm, d), lambda i: (i, 0))
    vec = pl.BlockSpec((1, d), lambda i: (0, 0))
    return pl.pallas_call(
        body, grid=(rows // tm,), in_specs=[blk, vec, blk],
        out_specs=[blk, vec, pl.BlockSpec((1, LANES), lambda i: (0, 0)), blk],
        out_shape=[jax.ShapeDtypeStruct((rows, d), F32), jax.ShapeDtypeStruct((1, d), F32),
                   jax.ShapeDtypeStruct((1, LANES), F32), jax.ShapeDtypeStruct((rows, d), BF16)],
        name=name, compiler_params=_params("arbitrary"))(h, w.reshape(1, d), tgt)


def _isz(x):
    return jnp.dtype(x.dtype).itemsize


def _mm(a, b, *, mode, name, out_dtype=F32, resid=None, col_cap=1536, ride=None):
    if mode == "tn":
        m, k = a.shape
        n = b.shape[1]
        tn = _col_tile(n, col_cap)
        tm = _fit_rows(m, k * _isz(a) + tn * _isz(b), (3 * k * tn * 4) // 2, 2 * (k + tn))

        def body_tn(a_ref, b_ref, o_ref):
            i = pl.program_id(1)
            part = _tn(a_ref[...], b_ref[...])

            @pl.when(i == 0)
            def _():
                o_ref[...] = part

            @pl.when(i > 0)
            def _():
                o_ref[...] += part

        return pl.pallas_call(
            body_tn, grid=(n // tn, m // tm),
            in_specs=[pl.BlockSpec((tm, k), lambda j, i: (i, 0)),
                      pl.BlockSpec((tm, tn), lambda j, i: (i, j))],
            out_specs=pl.BlockSpec((k, tn), lambda j, i: (0, j)),
            out_shape=jax.ShapeDtypeStruct((k, n), F32), name=name,
            compiler_params=_params("parallel", "arbitrary"))(a, b)

    m, ka = a.shape
    n = b.shape[1] if mode == "nn" else b.shape[0]
    has_resid = resid is not None
    tn = _col_tile(n, col_cap)
    tm = _fit_rows(m, ka * _isz(a) + tn * (jnp.dtype(out_dtype).itemsize + (4 if has_resid else 0)),
                   ka * tn * _isz(b), 2 * ka + 8 * tn)

    def body(*refs):
        if has_resid:
            a_ref, b_ref, r_ref, o_ref = refs
        else:
            a_ref, b_ref, o_ref = refs
        acc = _nn(a_ref[...], b_ref[...]) if mode == "nn" else _nt(a_ref[...], b_ref[...])
        if has_resid:
            acc = acc + r_ref[...]
        o_ref[...] = acc.astype(o_ref.dtype)

    b_spec = (pl.BlockSpec((b.shape[0], tn), lambda j, i: (0, j)) if mode == "nn"
              else pl.BlockSpec((tn, b.shape[1]), lambda j, i: (j, 0)))
    o_spec = pl.BlockSpec((tm, tn), lambda j, i: (i, j))
    in_specs = [pl.BlockSpec((tm, ka), lambda j, i: (i, 0)), b_spec]
    args = [a, b]
    if has_resid:
        in_specs.append(o_spec)
        args.append(resid)
    res, rode = _pcall(body, args, grid=(n // tn, m // tm), in_specs=in_specs, out_specs=[o_spec],
                       out_shape=[jax.ShapeDtypeStruct((m, n), out_dtype)], name=name,
                       sem=("parallel", "parallel"), ride=ride)
    return res[0] if ride is None else (res[0], rode)


N_SHARD = 4


def _gmm(name, grid, args, in_specs, out_specs, out_shape, fn, red_axis=None, init_arg=None, aliases=None,
         ride=None):
    n_in = len(args)
    single = not isinstance(out_shape, (list, tuple))
    out_specs = [out_specs] if single else list(out_specs)
    out_shape = [out_shape] if single else list(out_shape)

    def body(*refs):
        _gmm_step(fn, refs[:n_in], refs[n_in:], red_axis, init_arg)

    sem = tuple("arbitrary" if ax == red_axis else "parallel" for ax in range(len(grid)))
    res, rode = _pcall(body, args, grid=grid, in_specs=in_specs, out_specs=out_specs, out_shape=out_shape,
                       name=name, sem=sem, aliases=aliases, ride=ride)
    ours = res[0] if single else res
    return ours if ride is None else (ours, rode)


def _gmm_step(fn, ins, outs, red_axis, init_arg):
    parts = fn(*ins)
    if red_axis is None:
        for o_ref, p in zip(outs, parts):
            o_ref[...] = p.astype(o_ref.dtype)
        return
    k = pl.program_id(red_axis)

    @pl.when(k == 0)
    def _():
        for idx, (o_ref, p) in enumerate(zip(outs, parts)):
            o_ref[...] = p + ins[init_arg][...] if (idx == 0 and init_arg is not None) else p

    @pl.when(k > 0)
    def _():
        for o_ref, p in zip(outs, parts):
            o_ref[...] += p


def _ride_body(ride, grid, n_in, n_out, n_scratch, body):
    n_rin, n_rout = len(ride.arrays), len(ride.out_shape)
    nsteps = math.prod(grid)

    def wrapped(*refs):
        ins = refs[:n_in]
        r_ins = refs[n_in:n_in + n_rin]
        o0 = n_in + n_rin
        outs = refs[o0:o0 + n_out]
        r_outs = refs[o0 + n_out:o0 + n_out + n_rout]
        s0 = o0 + n_out + n_rout
        scratch = refs[s0:s0 + n_scratch]
        send_sems, recv_sems = refs[-2:]
        step = pl.program_id(0)
        for ax in range(1, len(grid)):
            step = step * grid[ax] + pl.program_id(ax)
        ride.emit(step, nsteps, r_ins, r_outs, send_sems, recv_sems, before=True)
        body(*ins, *outs, *scratch)
        ride.emit(step, nsteps, r_ins, r_outs, send_sems, recv_sems, before=False)

    return wrapped


def _pcall(body, args, *, grid, in_specs, out_specs, out_shape, name, sem, scratch=(), aliases=None, ride=None):
    if ride is None:
        res = pl.pallas_call(body, grid=grid, in_specs=list(in_specs), out_specs=list(out_specs),
                             out_shape=list(out_shape), scratch_shapes=list(scratch), name=name,
                             input_output_aliases=aliases or {}, compiler_params=_params(*sem))(*args)
        return res, None
    n_in, n_out = len(args), len(out_shape)
    res = pl.pallas_call(
        _ride_body(ride, grid, n_in, n_out, len(scratch), body), grid=grid,
        in_specs=list(in_specs) + ride.in_specs, out_specs=list(out_specs) + ride.out_specs,
        out_shape=list(out_shape) + ride.out_shape, scratch_shapes=list(scratch) + ride.scratch, name=name,
        input_output_aliases=aliases or {},
        compiler_params=_params(*(("arbitrary",) * len(grid))))(*args, *ride.arrays)
    return res[:n_out], res[n_out:]


def _mm_cols(a, ws, name, ride=None):
    m, k = a.shape
    n = ws.shape[2]
    tm = _fit_rows(m, k * _isz(a) + n * 4, k * n * _isz(ws), 4 * n)
    return _gmm(name, (N_SHARD, m // tm), [a, ws],
                [pl.BlockSpec((tm, k), lambda j, i: (i, 0)), pl.BlockSpec((None, k, n), lambda j, i: (j, 0, 0))],
                pl.BlockSpec((tm, n), lambda j, i: (i, j)), jax.ShapeDtypeStruct((m, N_SHARD * n), F32),
                lambda a_ref, w_ref: (_nn(a_ref[...], w_ref[...]),), ride=ride)


def _mm_cols_t_rms(d, ws, h, w, resid, name, ride=None):
    m = d.shape[0]
    _, k, n = ws.shape
    tm = _fit_rows(m, n * _isz(d) + 3 * k * 4, k * n * _isz(ws), 16 * k)
    return _gmm_rms(name, (m // tm, N_SHARD), [d, ws],
                    [pl.BlockSpec((tm, n), lambda i, j: (i, j)), pl.BlockSpec((None, k, n), lambda i, j: (j, 0, 0))],
                    pl.BlockSpec((tm, k), lambda i, j: (i, 0)),
                    lambda d_ref, w_ref: _nt(d_ref[...], w_ref[...]), h, w, resid, 0, red_axis=1, ride=ride)


def _mm_nt_rms(a, b, h, w, resid, name, ride=None):
    m, n = a.shape
    k = b.shape[0]
    tm = _fit_rows(m, n * _isz(a) + 3 * k * 4, k * n * _isz(b), 16 * k)
    return _gmm_rms(name, (m // tm,), [a, b],
                    [pl.BlockSpec((tm, n), lambda i: (i, 0)), pl.BlockSpec((k, n), lambda i: (0, 0))],
                    pl.BlockSpec((tm, k), lambda i: (i, 0)),
                    lambda a_ref, b_ref: _nt(a_ref[...], b_ref[...]), h, w, resid, 0, ride=ride)


def _mm_cols_grad(a, d, name):
    m, k = a.shape
    n = d.shape[1] // N_SHARD
    tm = _fit_rows(m, k * _isz(a) + n * _isz(d), (3 * k * n * 4) // 2, 2 * (k + n))
    return _gmm(name, (N_SHARD, m // tm), [a, d],
                [pl.BlockSpec((tm, k), lambda j, i: (i, 0)), pl.BlockSpec((tm, n), lambda j, i: (i, j))],
                pl.BlockSpec((None, k, n), lambda j, i: (j, 0, 0)), jax.ShapeDtypeStruct((N_SHARD, k, n), F32),
                lambda a_ref, d_ref: (_tn(a_ref[...], d_ref[...]),), red_axis=1)


def _ffn_up(hn, wg, wu, layer, name, ride=None):
    m, k = hn.shape
    n = wg.shape[3]
    tm = _fit_rows(m, k * _isz(hn) + 3 * n * jnp.dtype(BF16).itemsize, 2 * k * n * _isz(wg), 16 * n)

    def fn(a_ref, wg_ref, wu_ref):
        a = a_ref[...]
        g = _nn(a, wg_ref[...])
        u = _nn(a, wu_ref[...])
        return g, u, g * jax.nn.sigmoid(g) * u

    w_spec = pl.BlockSpec((None, None, k, n), lambda j, i: (j, layer, 0, 0))
    o_spec = pl.BlockSpec((None, tm, n), lambda j, i: (j, i, 0))
    out = jax.ShapeDtypeStruct((N_SHARD, m, n), BF16)
    return _gmm(name, (N_SHARD, m // tm), [hn, wg, wu],
                [pl.BlockSpec((tm, k), lambda j, i: (i, 0)), w_spec, w_spec],
                [o_spec, o_spec, o_spec], [out, out, out], fn, ride=ride)


def _ffn_down(act, wd, resid, layer, name, ride=None):
    _, m, n = act.shape
    d = wd.shape[3]
    tm = _fit_rows(m, N_SHARD * n * _isz(act) + 2 * d * 4, N_SHARD * n * d * _isz(wd), 8 * d)

    def fn(a_ref, w_ref, r_ref):
        acc = r_ref[...]
        for j in range(N_SHARD):
            acc = acc + _nn(a_ref[j], w_ref[j])
        return (acc,)

    row = pl.BlockSpec((tm, d), lambda i: (i, 0))
    return _gmm(name, (m // tm,), [act, wd, resid],
                [pl.BlockSpec((N_SHARD, tm, n), lambda i: (0, i, 0)),
                 pl.BlockSpec((N_SHARD, None, n, d), lambda i: (0, layer, 0, 0)), row],
                row, jax.ShapeDtypeStruct((m, d), F32), fn, ride=ride)


def _ffn_down_bwd(dh, wd, g, u, layer, name, ride=None):
    m, d = dh.shape
    n = wd.shape[2]
    tm = _fit_rows(m, d * _isz(dh) + 4 * N_SHARD * n * jnp.dtype(BF16).itemsize, N_SHARD * n * d * _isz(wd),
                   2 * d + 24 * n)

    def body(dh_ref, wd_ref, g_ref, u_ref, dg_ref, du_ref):
        dhv = dh_ref[...].astype(MXU_DTYPE)
        for j in range(N_SHARD):
            dact = _nt(dhv, wd_ref[j])
            gv = g_ref[j].astype(F32)
            sg = jax.nn.sigmoid(gv)
            gs = gv * sg
            dg_ref[j] = (dact * u_ref[j].astype(F32) * (sg + gs * (1.0 - sg))).astype(dg_ref.dtype)
            du_ref[j] = (dact * gs).astype(du_ref.dtype)

    sh_spec = pl.BlockSpec((N_SHARD, tm, n), lambda i: (0, i, 0))
    out = jax.ShapeDtypeStruct((N_SHARD, m, n), BF16)
    res, rode = _pcall(body, [dh, wd, g, u], grid=(m // tm,),
                       in_specs=[pl.BlockSpec((tm, d), lambda i: (i, 0)),
                                 pl.BlockSpec((N_SHARD, None, n, d), lambda i: (0, layer, 0, 0)), sh_spec, sh_spec],
                       out_specs=[sh_spec, sh_spec], out_shape=[out, out], name=name, sem=("parallel",), ride=ride)
    return res if ride is None else (res, rode)


def _ffn_up_bwd(dg, du, wg, wu, layer, h, w, resid, name, ride=None):
    _, m, n = dg.shape
    k = wg.shape[2]
    tm = _fit_rows(m, 2 * N_SHARD * n * _isz(dg) + 3 * k * 4, 2 * N_SHARD * k * n * _isz(wg), 16 * k)

    def fn(dg_ref, du_ref, wg_ref, wu_ref):
        acc = _nt(dg_ref[0], wg_ref[0]) + _nt(du_ref[0], wu_ref[0])
        for j in range(1, N_SHARD):
            acc = acc + _nt(dg_ref[j], wg_ref[j]) + _nt(du_ref[j], wu_ref[j])
        return acc

    d_spec = pl.BlockSpec((N_SHARD, tm, n), lambda i: (0, i, 0))
    w_spec = pl.BlockSpec((N_SHARD, None, k, n), lambda i: (0, layer, 0, 0))
    return _gmm_rms(name, (m // tm,), [dg, du, wg, wu], [d_spec, d_spec, w_spec, w_spec],
                    pl.BlockSpec((tm, k), lambda i: (i, 0)), fn, h, w, resid, 0, ride=ride)


def _ffn_wgrad(lhs, rhs_list, layer, layers, prev, lhs_sharded, name):
    if lhs_sharded:
        _, m, k = lhs.shape
        n = rhs_list[0].shape[1]
    else:
        m, k = lhs.shape
        n = rhs_list[0].shape[2]
    n_out = len(rhs_list)
    tm = _fit_rows(m, k * _isz(lhs) + n_out * n * _isz(rhs_list[0]), (3 * n_out * k * n * 4) // 2,
                   2 * (k + n_out * n))
    sh = pl.BlockSpec((None, tm, k if lhs_sharded else n), lambda j, i: (j, i, 0))
    fl = pl.BlockSpec((tm, n if lhs_sharded else k), lambda j, i: (i, 0))
    n_out = len(rhs_list)
    args = [lhs] + list(rhs_list)
    in_specs = [sh if lhs_sharded else fl] + [fl if lhs_sharded else sh] * n_out
    aliases = None
    if prev is not None:
        aliases = {len(args) + t: t for t in range(n_out)}
        args = args + list(prev)
        in_specs = in_specs + [ANY] * n_out

    def fn(l_ref, *rest):
        lv = l_ref[...]
        return tuple(_tn(lv, r_ref[...]) for r_ref in rest[:n_out])

    o_spec = pl.BlockSpec((None, None, k, n), lambda j, i: (j, layer, 0, 0))
    out = jax.ShapeDtypeStruct((N_SHARD, layers, k, n), F32)
    return _gmm(name, (N_SHARD, m // tm), args, in_specs, [o_spec] * n_out, [out] * n_out, fn,
                red_axis=1, aliases=aliases)


def _ret_consts():
    log_gamma = jnp.log1p(-jnp.exp2(-5.0 - jnp.arange(RET_HEADS, dtype=F32)))
    idx = jnp.arange(CHUNK, dtype=F32)
    rel = idx[:, None] - idx[None, :]
    dmask = jnp.where((rel >= 0)[None], jnp.exp(log_gamma[:, None, None] * jnp.maximum(rel, 0.0)), 0.0)
    xi = jnp.exp(log_gamma[:, None] * (idx[None, :] + 1.0))[:, :, None]
    zeta = jnp.exp(log_gamma[:, None] * (CHUNK - 1.0 - idx[None, :]))[:, :, None]
    gamma_c = jnp.exp(log_gamma * CHUNK)
    wide = (RET_HEADS, CHUNK, RET_DK)
    return dmask, jnp.broadcast_to(xi, wide), jnp.broadcast_to(zeta, wide), gamma_c


def _rope_tables(nc):
    half = RET_DK // 2
    inv_freq = ROPE_BASE ** (-jnp.arange(half, dtype=F32) / half)
    a_chunk = (jnp.arange(nc) * CHUNK - PAD).astype(F32)[:, None] * inv_freq[None, :]
    a_row = jnp.arange(CHUNK).astype(F32)[:, None] * inv_freq[None, :]
    return (jnp.stack([jnp.cos(a_chunk), jnp.sin(a_chunk)], axis=1),
            jnp.stack([jnp.cos(a_row), jnp.sin(a_row)], axis=0))


RET_CPS = 4


def _rope_chunk(rc_ref, rr_ref, c):
    cc, sc = rc_ref[c, 0:1, :], rc_ref[c, 1:2, :]
    cr, sr = rr_ref[0], rr_ref[1]
    return cc * cr - sc * sr, sc * cr + cc * sr


def _rope_specs(order):
    half = RET_DK // 2
    return [pl.BlockSpec((RET_CPS, 2, half), lambda n: (order(n), 0, 0)),
            pl.BlockSpec((2, CHUNK, half), lambda n: (0, 0, 0))]


def _ret_specs(order):
    rows = RET_CPS * CHUNK
    return [pl.BlockSpec((rows, RET_QK), lambda n: (order(n), 0)),
            pl.BlockSpec((rows, RET_QK), lambda n: (order(n), 1)),
            pl.BlockSpec((rows, RET_V), lambda n: (order(n), 1)),
            pl.BlockSpec((rows, RET_V), lambda n: (order(n), 2))]


def _ret_const_specs():
    return [pl.BlockSpec((RET_HEADS, CHUNK, CHUNK), lambda n: (0, 0, 0)),
            pl.BlockSpec((RET_HEADS, CHUNK, RET_DK), lambda n: (0, 0, 0)),
            pl.BlockSpec((RET_HEADS, CHUNK, RET_DK), lambda n: (0, 0, 0)),
            pl.BlockSpec((1, RET_DV), lambda n: (0, 0))]


def _ret_fwd(proj, cos, sin, consts, gn_w, seq, ride=None):
    rows = proj.shape[0]
    nc = rows // CHUNK
    dmask, xi, zeta, gamma_c = consts

    def body(gam_ref, q_ref, k_ref, v_ref, g_ref, cos_ref, sin_ref, dm_ref, xi_ref, ze_ref, gn_ref,
             o_ref, y_ref, ss_ref, s_ref):
        n = pl.program_id(0)

        @pl.when(n == 0)
        def _():
            s_ref[...] = jnp.zeros_like(s_ref)

        gn = gn_ref[...]
        hs = range(RET_HEADS)
        qk_cols = [slice(h * RET_DK, (h + 1) * RET_DK) for h in hs]
        v_cols = [slice(h * RET_DV, (h + 1) * RET_DV) for h in hs]
        for c in range(RET_CPS):
            rs = slice(c * CHUNK, (c + 1) * CHUNK)
            cs, sn = _rope_chunk(cos_ref, sin_ref, c)
            kscale = _valid_rows((n * RET_CPS + c) * CHUNK, CHUNK, seq) * (RET_DK ** -0.5)
            qr_l = [_rope(q_ref[rs, col], cs, sn) for col in qk_cols]
            kr_l = [_rope(k_ref[rs, col], cs, sn) * kscale for col in qk_cols]
            v_l = [v_ref[rs, col] for col in v_cols]
            s_l = [s_ref[h] for h in hs]
            sc_l = [_nt(qr, kr) * dm_ref[h] for h, (qr, kr) in enumerate(zip(qr_l, kr_l))]
            o_l = [_nn(sc_l[h], v_l[h]) + _nn(qr_l[h] * xi_ref[h], s_l[h]) for h in hs]
            for h in hs:
                ss_ref[c, h] = s_l[h].astype(ss_ref.dtype)
                s_ref[h] = gam_ref[h] * s_l[h] + _tn(kr_l[h] * ze_ref[h], v_l[h])
                o_ref[rs, v_cols[h]] = o_l[h]
                y_ref[rs, v_cols[h]] = _gated_norm(o_l[h], g_ref[rs, v_cols[h]], gn).astype(y_ref.dtype)

    fwd = lambda n: n
    row_v = pl.BlockSpec((RET_CPS * CHUNK, RET_V), lambda n: (n, 0))
    res, rode = _pcall(
        body, [gamma_c, proj, proj, proj, proj, cos, sin, dmask, xi, zeta, gn_w.reshape(1, RET_DV)],
        grid=(nc // RET_CPS,),
        in_specs=[pl.BlockSpec(memory_space=pltpu.SMEM)] + _ret_specs(fwd) + _rope_specs(fwd)
        + _ret_const_specs(),
        out_specs=[row_v, row_v,
                   pl.BlockSpec((RET_CPS, RET_HEADS, RET_DK, RET_DV), lambda n: (n, 0, 0, 0))],
        out_shape=[jax.ShapeDtypeStruct((rows, RET_V), F32), jax.ShapeDtypeStruct((rows, RET_V), BF16),
                   jax.ShapeDtypeStruct((nc, RET_HEADS, RET_DK, RET_DV), BF16)],
        scratch=[pltpu.VMEM((RET_HEADS, RET_DK, RET_DV), F32)], name="ret_fwd", sem=("arbitrary",), ride=ride)
    return res if ride is None else (res, rode)


def _ret_bwd(proj, o, dy, states, cos, sin, consts, gn_w, seq, ride=None):
    rows = proj.shape[0]
    nc = rows // CHUNK
    dmask, xi, zeta, gamma_c = consts

    def body(gam_ref, q_ref, k_ref, v_ref, g_ref, o_ref, dy_ref, ss_ref, cos_ref, sin_ref,
             dm_ref, xi_ref, ze_ref, gn_ref, dp_ref, dgn_ref, ds_ref):
        n = pl.program_id(0)

        @pl.when(n == 0)
        def _():
            ds_ref[...] = jnp.zeros_like(ds_ref)
            dgn_ref[...] = jnp.zeros_like(dgn_ref)

        gn = gn_ref[...]
        dgn = jnp.zeros((1, RET_DV), F32)
        hs = range(RET_HEADS)
        qk_cols = [slice(h * RET_DK, (h + 1) * RET_DK) for h in hs]
        v_cols = [slice(h * RET_DV, (h + 1) * RET_DV) for h in hs]
        for c in reversed(range(RET_CPS)):
            rs = slice(c * CHUNK, (c + 1) * CHUNK)
            cs, sn = _rope_chunk(cos_ref, sin_ref, c)
            kscale = _valid_rows(((steps - 1 - n) * RET_CPS + c) * CHUNK, CHUNK, seq) * (RET_DK ** -0.5)
            qr_l = [_rope(q_ref[rs, col], cs, sn) for col in qk_cols]
            kr_l = [_rope(k_ref[rs, col], cs, sn) * kscale for col in qk_cols]
            v_l = [v_ref[rs, col] for col in v_cols]
            s_l = [ss_ref[c, h] for h in hs]
            ds_l = [ds_ref[h] for h in hs]
            sc_l = [_nt(qr_l[h], kr_l[h]) * dm_ref[h] for h in hs]
            gnb = [_gated_norm_bwd(dy_ref[rs, col], o_ref[rs, col], g_ref[rs, col], gn) for col in v_cols]
            do_l = [x[0] for x in gnb]
            dsc_l = [_nt(do_l[h], v_l[h]) * dm_ref[h] for h in hs]
            dv_l = [_tn(sc_l[h], do_l[h]) + _nn(kr_l[h] * ze_ref[h], ds_l[h]) for h in hs]
            dqr_l = [_nn(dsc_l[h], kr_l[h]) + _nt(do_l[h], s_l[h]) * xi_ref[h] for h in hs]
            dkr_l = [_tn(dsc_l[h], qr_l[h]) + _nt(v_l[h], ds_l[h]) * ze_ref[h] for h in hs]
            for h in hs:
                dgn = dgn + gnb[h][2]
                ds_ref[h] = gam_ref[h] * ds_l[h] + _tn(qr_l[h] * xi_ref[h], do_l[h])
                dp_ref[rs, qk_cols[h]] = _rope_bwd(dqr_l[h], cs, sn).astype(dp_ref.dtype)
                dp_ref[rs, RET_QK + h * RET_DK:RET_QK + (h + 1) * RET_DK] = (
                    _rope_bwd(dkr_l[h] * kscale, cs, sn).astype(dp_ref.dtype))
                dp_ref[rs, 2 * RET_QK + h * RET_DV:2 * RET_QK + (h + 1) * RET_DV] = dv_l[h].astype(dp_ref.dtype)
                dp_ref[rs, 2 * RET_QK + RET_V + h * RET_DV:2 * RET_QK + RET_V + (h + 1) * RET_DV] = (
                    gnb[h][1].astype(dp_ref.dtype))
        dgn_ref[...] += dgn

    steps = nc // RET_CPS
    rev = lambda n: steps - 1 - n
    row_v = pl.BlockSpec((RET_CPS * CHUNK, RET_V), lambda n: (rev(n), 0))
    res, rode = _pcall(
        body, [gamma_c, proj, proj, proj, proj, o, dy, states, cos, sin, dmask, xi, zeta,
               gn_w.reshape(1, RET_DV)],
        grid=(steps,),
        in_specs=[pl.BlockSpec(memory_space=pltpu.SMEM)] + _ret_specs(rev) + [
            row_v, row_v, pl.BlockSpec((RET_CPS, RET_HEADS, RET_DK, RET_DV), lambda n: (rev(n), 0, 0, 0))]
        + _rope_specs(rev) + _ret_const_specs(),
        out_specs=[pl.BlockSpec((RET_CPS * CHUNK, RET_IN), lambda n: (rev(n), 0)),
                   pl.BlockSpec((1, RET_DV), lambda n: (0, 0))],
        out_shape=[jax.ShapeDtypeStruct((rows, RET_IN), BF16), jax.ShapeDtypeStruct((1, RET_DV), F32)],
        scratch=[pltpu.VMEM((RET_HEADS, RET_DK, RET_DV), F32)], name="ret_bwd", sem=("arbitrary",), ride=ride)
    return res if ride is None else (res, rode)


GATE_COL = DN_CONV_CH // DN_V
BA_COL = (DN_CONV_CH + DN_V) // LANES
BETA_LANE, DECAY_LANE = 0, DN_HEADS
INV_SHIFT = 4
INV_SQUARINGS = INV_SHIFT - 1
assert CHUNK == 4 << INV_SHIFT


DN_CPS = 2


def _dn_in_specs(order, conv_saved=False):
    rows = DN_CPS * CHUNK
    return [pl.BlockSpec((rows, DN_CONV_CH), lambda n: (order(n), 0)),
            pl.BlockSpec((rows, DN_CONV_CH), lambda n: (order(n), 0)) if conv_saved else
            pl.BlockSpec((8, DN_CONV_CH), lambda n: (jnp.maximum(order(n) * (rows // 8) - 1, 0), 0)),
            pl.BlockSpec((rows, DN_V), lambda n: (order(n), GATE_COL)),
            pl.BlockSpec((rows, LANES), lambda n: (order(n), BA_COL)),
            pl.BlockSpec((CONV_K, 1, DN_CONV_CH), lambda n: (0, 0, 0)),
            pl.BlockSpec((1, LANES), lambda n: (0, 0)),
            pl.BlockSpec((1, LANES), lambda n: (0, 0)),
            pl.BlockSpec((1, DN_DV), lambda n: (0, 0))]


def _dn_front(c, seq, x, halo, ba, cw_ref, al_ref, dt_ref, yc=None):
    valid = _valid_rows(c * CHUNK, CHUNK, seq)
    xin = x * valid
    if yc is None:
        halo = halo * _valid_rows(c * CHUNK - 8, 8, seq)
        yc = xin * cw_ref[CONV_K - 1]
        for k in range(1, CONV_K):
            yc = yc + _shift_down(xin, halo, k) * cw_ref[CONV_K - 1 - k]
    sgc = jax.nn.sigmoid(yc)
    sig = jax.nn.sigmoid(ba)
    beta = sig * valid
    z = ba + dt_ref[...]
    eal = jnp.exp(al_ref[...])
    g = -eal * _softplus(z) * valid
    ri, ci = _iota((CHUNK, CHUNK), 0), _iota((CHUNK, CHUNK), 1)
    lower = (ri >= ci).astype(F32)
    upper = (ri <= ci).astype(F32)
    eye = (ri == ci).astype(F32)
    gam = _nn(lower, g, hi=True)
    gam_t = _tn(g, upper, hi=True)
    return dict(valid=valid, xin=xin, yc=yc, sgc=sgc, act=yc * sgc, sig=sig, beta=beta, z=z,
                eal=eal, g=g, gam=gam, gam_t=gam_t, ri=ri, ci=ci, upper=upper, eye=eye)


def _dn_head(f, h):
    act = f["act"]
    q_raw = act[:, h * DN_DK:(h + 1) * DN_DK]
    k_raw = act[:, DN_QK + h * DN_DK:DN_QK + (h + 1) * DN_DK]
    v = act[:, 2 * DN_QK + h * DN_DV:2 * DN_QK + (h + 1) * DN_DV]
    rq = lax.rsqrt(jnp.sum(q_raw * q_raw, axis=-1, keepdims=True) + RMS_EPS)
    rk = lax.rsqrt(jnp.sum(k_raw * k_raw, axis=-1, keepdims=True) + RMS_EPS)
    qh = q_raw * rq
    kn = k_raw * rk
    gam_c = _col(f["gam"], DECAY_LANE + h)
    gam_r = _row(f["gam_t"], DECAY_LANE + h)
    bc = _col(f["beta"], BETA_LANE + h)
    diff = gam_c - gam_r
    decay = jnp.where(f["ri"] >= f["ci"], jnp.exp(jnp.minimum(diff, 0.0)), 0.0)
    glast = jnp.sum(gam_r * (_iota((1, CHUNK), 1) == CHUNK - 1).astype(F32), axis=1, keepdims=True)
    return dict(rq=rq, rk=rk, qh=qh, qn=qh * (DN_DK ** -0.5), kn=kn, v=v, gam_c=gam_c, gam_r=gam_r,
                bc=bc, diff=diff, decay=decay, egam=jnp.exp(gam_c), glast=glast,
                eglast=jnp.exp(glast), ekd=jnp.exp(glast - gam_c))


def _dn_fwd(proj, conv_w, alog, dtb, norm_w, seq):
    rows = proj.shape[0]
    nc = rows // CHUNK

    def body(x_ref, halo_ref, gate_ref, ba_ref, cw_ref, al_ref, dt_ref, nw_ref,
             o_ref, y_ref, ss_ref, t_ref, yc_ref, s_ref):
        n = pl.program_id(0)

        @pl.when(n == 0)
        def _():
            s_ref[...] = jnp.zeros_like(s_ref)

        nw = nw_ref[...]
        pre = []
        for c in range(DN_CPS):
            rs = slice(c * CHUNK, (c + 1) * CHUNK)
            halo = halo_ref[...] if c == 0 else x_ref[c * CHUNK - 8:c * CHUNK, :]
            f = _dn_front(n * DN_CPS + c, seq, x_ref[rs, :], halo, ba_ref[rs, :], cw_ref, al_ref, dt_ref)
            yc_ref[rs, :] = f["yc"]
            ri, ci = f["ri"], f["ci"]
            eye = f["eye"]
            diag_m = (jnp.right_shift(ri, INV_SHIFT) == jnp.right_shift(ci, INV_SHIFT)).astype(F32)
            half_m = (jnp.right_shift(ri, INV_SHIFT + 1) == jnp.right_shift(ci, INV_SHIFT + 1)).astype(F32)
            heads = [_dn_head(f, h) for h in range(DN_HEADS)]
            a_all = [jnp.where(ri > ci, hd["bc"] * _nt(hd["kn"], hd["kn"]) * hd["decay"], 0.0) for hd in heads]
            b_all = [a * diag_m for a in a_all]
            t_all = [eye - b for b in b_all]
            for _ in range(INV_SQUARINGS):
                b_all = [_nn(b, b, hi=True) for b in b_all]
                t_all = [t + _nn(t, b, hi=True) for t, b in zip(t_all, b_all)]
            for off_m in (half_m - diag_m, 1.0 - half_m):
                x_all = [_nn(a * off_m, t, hi=True) for a, t in zip(a_all, t_all)]
                t_all = [t - _nn(t, x, hi=True) for t, x in zip(t_all, x_all)]
            u_all = [_nn(t, hd["v"] * hd["bc"], hi=True) for t, hd in zip(t_all, heads)]
            w_all = [_nn(t, hd["kn"] * (hd["bc"] * hd["egam"]), hi=True) for t, hd in zip(t_all, heads)]
            qk_all = [_nt(hd["qn"], hd["kn"]) * hd["decay"] for hd in heads]
            for h in range(DN_HEADS):
                t_ref[c, h] = t_all[h]
            pre.append((heads, u_all, w_all, qk_all))
        for c in range(DN_CPS):
            rs = slice(c * CHUNK, (c + 1) * CHUNK)
            heads, u_all, w_all, qk_all = pre[c]
            s_all = [s_ref[h] for h in range(DN_HEADS)]
            os_all = [_nn(hd["qn"] * hd["egam"], s) for hd, s in zip(heads, s_all)]
            vnew_all = [u - _nn(w, s) for u, w, s in zip(u_all, w_all, s_all)]
            o_all = [os + _nn(qk, vn) for os, qk, vn in zip(os_all, qk_all, vnew_all)]
            snew_all = [s * hd["eglast"] + _tn(hd["kn"] * hd["ekd"], vn)
                        for s, hd, vn in zip(s_all, heads, vnew_all)]
            for h in range(DN_HEADS):
                v_cols = slice(h * DN_DV, (h + 1) * DN_DV)
                ss_ref[c, h] = s_all[h]
                s_ref[h] = snew_all[h]
                o_ref[rs, v_cols] = o_all[h]
                y_ref[rs, v_cols] = _gated_norm(o_all[h], gate_ref[rs, v_cols], nw).astype(y_ref.dtype)

    fwd = lambda n: n
    row_v = pl.BlockSpec((DN_CPS * CHUNK, DN_V), lambda n: (n, 0))
    return pl.pallas_call(
        body, grid=(nc // DN_CPS,), in_specs=_dn_in_specs(fwd),
        out_specs=[row_v, row_v,
                   pl.BlockSpec((DN_CPS, DN_HEADS, DN_DK, DN_DV), lambda n: (n, 0, 0, 0)),
                   pl.BlockSpec((DN_CPS, DN_HEADS, CHUNK, CHUNK), lambda n: (n, 0, 0, 0)),
                   pl.BlockSpec((DN_CPS * CHUNK, DN_CONV_CH), lambda n: (n, 0))],
        out_shape=[jax.ShapeDtypeStruct((rows, DN_V), F32), jax.ShapeDtypeStruct((rows, DN_V), BF16),
                   jax.ShapeDtypeStruct((nc, DN_HEADS, DN_DK, DN_DV), F32),
                   jax.ShapeDtypeStruct((nc, DN_HEADS, CHUNK, CHUNK), F32),
                   jax.ShapeDtypeStruct((rows, DN_CONV_CH), F32)],
        scratch_shapes=[pltpu.VMEM((DN_HEADS, DN_DK, DN_DV), F32)],
        name="dn_fwd", compiler_params=_params("arbitrary"))(
            proj, proj, proj, proj, conv_w, alog, dtb, norm_w.reshape(1, DN_DV))


def _dn_bwd(proj, conv_out, o, dy, states, tinv, conv_w, alog, dtb, norm_w, seq):
    rows = proj.shape[0]
    nc = rows // CHUNK

    def body(x_ref, yc_ref, gate_ref, ba_ref, cw_ref, al_ref, dt_ref, nw_ref,
             o_ref, dy_ref, ss_ref, t_ref,
             dp_ref, dcw_ref, dal_ref, ddt_ref, dnw_ref, ds_ref, nxt_ref):
        n = pl.program_id(0)

        @pl.when(n == 0)
        def _():
            ds_ref[...] = jnp.zeros_like(ds_ref)
            nxt_ref[...] = jnp.zeros_like(nxt_ref)
            dcw_ref[...] = jnp.zeros_like(dcw_ref)
            dal_ref[...] = jnp.zeros_like(dal_ref)
            ddt_ref[...] = jnp.zeros_like(ddt_ref)
            dnw_ref[...] = jnp.zeros_like(dnw_ref)

        for c in reversed(range(DN_CPS)):
            rs = pl.ds(c * CHUNK, CHUNK)
            chunk((steps - 1 - n) * DN_CPS + c, x_ref.at[rs], yc_ref.at[rs], gate_ref.at[rs], ba_ref.at[rs],
                  cw_ref, al_ref, dt_ref, nw_ref, o_ref.at[rs], dy_ref.at[rs], ss_ref.at[c], t_ref.at[c],
                  dp_ref.at[rs], dcw_ref, dal_ref, ddt_ref, dnw_ref, ds_ref, nxt_ref)

    def chunk(ch, x_ref, yc_ref, gate_ref, ba_ref, cw_ref, al_ref, dt_ref, nw_ref,
              o_ref, dy_ref, ss_ref, t_ref,
              dp_ref, dcw_ref, dal_ref, ddt_ref, dnw_ref, ds_ref, nxt_ref):
        f = _dn_front(ch, seq, x_ref[...], None, ba_ref[...], cw_ref, al_ref, dt_ref, yc_ref[...])
        ri, ci = f["ri"], f["ci"]
        strict = (ri > ci).astype(F32)
        nw = nw_ref[...]
        lane128 = _iota((1, LANES), 1)
        row128 = _iota((LANES, 1), 0)
        dgam_col = jnp.zeros((CHUNK, LANES), F32)
        dgam_row = jnp.zeros((LANES, CHUNK), F32)
        dbeta = jnp.zeros((CHUNK, LANES), F32)
        dnw = jnp.zeros((1, DN_DV), F32)
        hs = range(DN_HEADS)
        heads = [_dn_head(f, h) for h in hs]
        cols = [slice(h * DN_DV, (h + 1) * DN_DV) for h in hs]
        t_l = [t_ref[h] for h in hs]
        s_l = [ss_ref[h] for h in hs]
        ds_l = [ds_ref[h] for h in hs]
        kk_l = [_nt(hd["kn"], hd["kn"]) for hd in heads]
        p_l = [_nt(hd["qn"], hd["kn"]) for hd in heads]
        rhsw_l = [hd["kn"] * (hd["bc"] * hd["egam"]) for hd in heads]
        u_l = [_nn(t, hd["v"] * hd["bc"], hi=True) for t, hd in zip(t_l, heads)]
        w_l = [_nn(t, r, hi=True) for t, r in zip(t_l, rhsw_l)]
        vnew_l = [u - _nn(w, s) for u, w, s in zip(u_l, w_l, s_l)]
        gnb = [_gated_norm_bwd(dy_ref[:, c], o_ref[:, c], gate_ref[:, c], nw) for c in cols]
        do_l = [x[0] for x in gnb]
        for h in hs:
            dp_ref[:, DN_CONV_CH + h * DN_DV:DN_CONV_CH + (h + 1) * DN_DV] = gnb[h][1].astype(dp_ref.dtype)
            dnw = dnw + gnb[h][2]
        qg_l = [hd["qn"] * hd["egam"] for hd in heads]
        kd_l = [hd["kn"] * hd["ekd"] for hd in heads]
        dvnew_l = [_tn(p * hd["decay"], do) + _nn(kd, ds)
                   for p, hd, do, kd, ds in zip(p_l, heads, do_l, kd_l, ds_l)]
        m_l = [_nt(do, vn) for do, vn in zip(do_l, vnew_l)]
        dqg_l = [_nt(do, s) for do, s in zip(do_l, s_l)]
        dkd_l = [_nt(vn, ds) for vn, ds in zip(vnew_l, ds_l)]
        for h in hs:
            ds_ref[h] = (ds_l[h] * heads[h]["eglast"] + _tn(qg_l[h], do_l[h]) - _tn(w_l[h], dvnew_l[h]))
        dw_l = [-_nt(dvn, s) for dvn, s in zip(dvnew_l, s_l)]
        dru_l = [_tn(t, dvn, hi=True) for t, dvn in zip(t_l, dvnew_l)]
        drw_l = [_tn(t, dw_, hi=True) for t, dw_ in zip(t_l, dw_l)]
        da_l = [-(_nt(dru, u) + _nt(drw, w)) * strict for dru, u, drw, w in zip(dru_l, u_l, drw_l, w_l)]
        dp_l = [m * hd["decay"] for m, hd in zip(m_l, heads)]
        dkk_l = [da * (hd["bc"] * hd["decay"]) for da, hd in zip(da_l, heads)]
        dqn_l = [dqg * hd["egam"] + _nn(dp, hd["kn"]) for dqg, hd, dp in zip(dqg_l, heads, dp_l)]
        dkn_l = [_tn(dp, hd["qn"]) + dkd * hd["ekd"] + drw * (hd["bc"] * hd["egam"])
                 + _nn(dkk, hd["kn"]) + _tn(dkk, hd["kn"])
                 for dp, hd, dkd, drw, dkk in zip(dp_l, heads, dkd_l, drw_l, dkk_l)]
        dq_parts, dk_parts, dv_parts = [], [], []
        for h in hs:
            hd = heads[h]
            kn, v, bc, egam, decay = hd["kn"], hd["v"], hd["bc"], hd["egam"], hd["decay"]
            t1 = jnp.sum(dkd_l[h] * kd_l[h], axis=1, keepdims=True)
            dglast = (jnp.sum(t1, axis=0, keepdims=True)
                      + jnp.sum(jnp.sum(ds_l[h] * s_l[h], axis=1, keepdims=True), axis=0, keepdims=True)
                      * hd["eglast"])
            e = (m_l[h] * p_l[h] + da_l[h] * (bc * kk_l[h])) * decay
            dgc = (jnp.sum(dqg_l[h] * qg_l[h], axis=1, keepdims=True) - t1
                   + jnp.sum(drw_l[h] * rhsw_l[h], axis=1, keepdims=True)
                   + jnp.sum(e, axis=1, keepdims=True)
                   + jnp.where(_iota((CHUNK, 1), 0) == CHUNK - 1, dglast, 0.0))
            dgr = -jnp.sum(e, axis=0, keepdims=True)
            dbc = (jnp.sum(dru_l[h] * v, axis=1, keepdims=True)
                   + jnp.sum(drw_l[h] * kn, axis=1, keepdims=True) * egam
                   + jnp.sum(da_l[h] * kk_l[h] * decay, axis=1, keepdims=True))
            dv_parts.append(dru_l[h] * bc)
            qh, dqn, dkn = hd["qh"], dqn_l[h], dkn_l[h]
            dq_parts.append(((DN_DK ** -0.5) * hd["rq"])
                            * (dqn - qh * jnp.sum(dqn * qh, axis=1, keepdims=True)))
            dk_parts.append(hd["rk"] * (dkn - kn * jnp.sum(dkn * kn, axis=1, keepdims=True)))
            dgam_col = dgam_col + dgc * (lane128 == DECAY_LANE + h).astype(F32)
            dbeta = dbeta + dbc * (lane128 == BETA_LANE + h).astype(F32)
            dgam_row = dgam_row + (row128 == DECAY_LANE + h).astype(F32) * dgr
        dnw_ref[...] += dnw
        dgam = dgam_col + _nt(f["eye"], dgam_row, hi=True)
        dg = _nn(f["upper"], dgam, hi=True)
        d_a = dg * (-f["eal"]) * jax.nn.sigmoid(f["z"]) * f["valid"]
        dal_ref[...] += jnp.sum(dg * f["g"], axis=0, keepdims=True)
        ddt_ref[...] += jnp.sum(d_a, axis=0, keepdims=True)
        d_b = dbeta * f["valid"] * f["sig"] * (1.0 - f["sig"])
        dp_ref[:, DN_CONV_CH + DN_V:DN_CONV_CH + DN_V + LANES] = (d_a + d_b).astype(dp_ref.dtype)
        dp_ref[:, DN_CONV_CH + DN_V + LANES:] = jnp.zeros((CHUNK, DN_IN_PAD - DN_IN_USED), dp_ref.dtype)
        dact = jnp.concatenate(dq_parts + dk_parts + dv_parts, axis=1)
        yc, sgc = f["yc"], f["sgc"]
        dyc = dact * (sgc * (1.0 + yc * (1.0 - sgc)))
        nxt = nxt_ref[...]
        ups = [dyc] + [_shift_up(dyc, nxt, j) for j in range(1, CONV_K)]
        dx = ups[0] * cw_ref[CONV_K - 1]
        for j in range(1, CONV_K):
            dx = dx + ups[j] * cw_ref[CONV_K - 1 - j]
        for j in range(CONV_K):
            dcw_ref[CONV_K - 1 - j] += jnp.sum(f["xin"] * ups[j], axis=0, keepdims=True)
        nxt_ref[...] = dyc[0:8]
        dp_ref[:, :DN_CONV_CH] = (dx * f["valid"]).astype(dp_ref.dtype)

    steps = nc // DN_CPS
    rev = lambda n: steps - 1 - n
    row_v = pl.BlockSpec((DN_CPS * CHUNK, DN_V), lambda n: (rev(n), 0))
    vec = pl.BlockSpec((1, LANES), lambda n: (0, 0))
    return pl.pallas_call(
        body, grid=(steps,),
        in_specs=_dn_in_specs(rev, conv_saved=True) + [
            row_v, row_v,
            pl.BlockSpec((DN_CPS, DN_HEADS, DN_DK, DN_DV), lambda n: (rev(n), 0, 0, 0)),
            pl.BlockSpec((DN_CPS, DN_HEADS, CHUNK, CHUNK), lambda n: (rev(n), 0, 0, 0))],
        out_specs=[pl.BlockSpec((DN_CPS * CHUNK, DN_IN_PAD), lambda n: (rev(n), 0)),
                   pl.BlockSpec((CONV_K, 1, DN_CONV_CH), lambda n: (0, 0, 0)), vec, vec,
                   pl.BlockSpec((1, DN_DV), lambda n: (0, 0))],
        out_shape=[jax.ShapeDtypeStruct((rows, DN_IN_PAD), BF16),
                   jax.ShapeDtypeStruct((CONV_K, 1, DN_CONV_CH), F32),
                   jax.ShapeDtypeStruct((1, LANES), F32), jax.ShapeDtypeStruct((1, LANES), F32),
                   jax.ShapeDtypeStruct((1, DN_DV), F32)],
        scratch_shapes=[pltpu.VMEM((DN_HEADS, DN_DK, DN_DV), F32), pltpu.VMEM((8, DN_CONV_CH), F32)],
        name="dn_bwd", compiler_params=_params("arbitrary"))(
            proj, conv_out, proj, proj, conv_w, alog, dtb, norm_w.reshape(1, DN_DV), o, dy, states, tinv)


def _train_step(x, tgt, wts, sh, idx):
    seq = x.shape[0]
    rows = -(-(seq + CHUNK) // ROW_ALIGN) * ROW_ALIGN
    wts = dict(wts)
    (h0, tgt_p), (got,) = _embed(x, tgt, wts["meta_tokens"].astype(F32), rows,
                                 ride=_Ride("gather", [sh["ret_w_in"]]))
    wts["ret_w_in"] = got.reshape(N_SHARD, D_MODEL, -1)
    cos, sin = _rope_tables(rows // CHUNK)
    consts = _ret_consts()
    conv_w = wts["dn_conv_w"].reshape(CONV_K, 1, DN_CONV_CH)
    lane_pad = LANES - 2 * DN_HEADS
    alog = jnp.pad(wts["dn_a_log"].reshape(1, DN_HEADS), ((0, 0), (DECAY_LANE, lane_pad)))
    dtb = jnp.pad(wts["dn_dt_bias"].reshape(1, DN_HEADS), ((0, 0), (DECAY_LANE, lane_pad)))
    g = {}

    hn0 = _rms_fwd(h0, wts["mix_norm_w"][0], "rms_mix0")
    proj0, got = _mm_cols(hn0, wts["ret_w_in"], "ret_in",
                          ride=_Ride("gather", [sh["ret_w_out"], sh["ffn_w_gate"]]))
    wts["ret_w_out"] = got[0].reshape(-1, D_MODEL)
    wts["ffn_w_gate"] = got[1]
    (o0, y0, st0), got = _ret_fwd(proj0, cos, sin, consts, wts["ret_gn_w"], seq,
                                  ride=_Ride("gather", [sh["ffn_w_up"], sh["ffn_w_down"]]))
    wts["ffn_w_up"], wts["ffn_w_down"] = got
    h1 = _mm(y0, wts["ret_w_out"], mode="nn", name="ret_out", resid=h0)
    hn1 = _rms_fwd(h1, wts["ffn_norm_w"][0], "rms_ffn0")
    (g0, u0, act0), got = _ffn_up(hn1, wts["ffn_w_gate"], wts["ffn_w_up"], 0, "ffn_up0",
                                  ride=_Ride("gather", [sh["dn_w_in"], sh["dn_w_out"]]))
    n_dn = sh["dn_w_in"].shape[-1]
    dn_shards = got[0].reshape(N_SHARD, D_MODEL, n_dn)
    wts["dn_w_in"] = jnp.concatenate(
        [dn_shards[j] for j in range(N_SHARD)]
        + [jnp.zeros((D_MODEL, DN_IN_PAD - N_SHARD * n_dn), dn_shards.dtype)], axis=-1)
    wts["dn_w_out"] = got[1].reshape(-1, D_MODEL)
    h2 = _ffn_down(act0, wts["ffn_w_down"], h1, 0, "ffn_down0")
    hn2 = _rms_fwd(h2, wts["mix_norm_w"][1], "rms_mix1")
    proj1 = _mm(hn2, wts["dn_w_in"], mode="nn", name="dn_in")
    o1, y1, st1, tinv, conv1 = _dn_fwd(proj1, conv_w, alog, dtb, wts["dn_norm_w"], seq)
    h3 = _mm(y1, wts["dn_w_out"], mode="nn", name="dn_out", resid=h2)
    hn3 = _rms_fwd(h3, wts["ffn_norm_w"][1], "rms_ffn1")
    g1, u1, act1 = _ffn_up(hn3, wts["ffn_w_gate"], wts["ffn_w_up"], 1, "ffn_up1")
    h4 = _ffn_down(act1, wts["ffn_w_down"], h3, 1, "ffn_down1")

    dh4, g["final_norm_w"], loss, dh4b = _final_loss(h4, wts["final_norm_w"], tgt_p, seq, "final_loss")

    layers = wts["ffn_w_gate"].shape[1]

    ffn_names = ["ffn_w_down", "ffn_w_gate", "ffn_w_up"]

    def ffn_bwd(dh_out, dhb_out, h_mid, hn, gg, uu, act, layer, prev, ride=None, last=False):
        tag = str(layer)
        res = _ffn_down_bwd(dhb_out, wts["ffn_w_down"], gg, uu, layer, "ffn_down_bwd" + tag, ride=ride)
        (dg, du), rode = res if ride is not None else (res, None)
        d_down = _ffn_wgrad(act, [dhb_out], layer, layers, prev and prev[:1], True, "ffn_dwd" + tag)
        d_gu = _ffn_wgrad(hn, [dg, du], layer, layers, prev and prev[1:], False, "ffn_dwgu" + tag)
        grads = list(d_down) + list(d_gu)
        gs = rs_grads(ffn_names, grads) if last else None
        res = _ffn_up_bwd(dg, du, wts["ffn_w_gate"], wts["ffn_w_up"], layer, h_mid, wts["ffn_norm_w"][layer],
                          dh_out, "ffn_up_bwd" + tag, ride=_Ride("pair", gs) if last else None)
        (dh_mid, d_norm, dhb_mid), sib = res if last else (res, None)
        return dh_mid, dhb_mid, grads, d_norm, rode, gs, sib

    red = {}

    def rs_grads(names, grads):
        return [gr.reshape((N_SHARD,) + sh[n].shape) for n, gr in zip(names, grads)]

    def rs_partials(names, gs, sib):
        return [_rs_pair_add(gs[t], sib[t], idx, "rs_pair_add_" + n) for t, n in enumerate(names)]

    def rs_end(names, gs, sib, others, tag):
        mine = [_rs_final_add(gs[t], sib[t], others[t], idx, "rs_final_add_" + n) for t, n in enumerate(names)]
        red.update(zip(names, _rs_share(mine, "rs_share" + tag)))

    dh3, dh3b, ffn_grads, dfn1 = ffn_bwd(dh4, dh4b, h3, hn3, g1, u1, act1, 1, None)[:4]
    dy1 = _mm(dh3b, wts["dn_w_out"], mode="nt", name="dn_out_bwd")
    d_dn_out = _mm(y1, dh3b, mode="tn", name="dn_dwo")
    dproj1, dcw, dal, ddt, g["dn_norm_w"] = _dn_bwd(proj1, conv1, o1, dy1, st1, tinv, conv_w, alog, dtb,
                                                    wts["dn_norm_w"], seq)
    d_dn_in = _mm(hn2, dproj1, mode="tn", name="dn_dwi")
    d_dn_in = jnp.stack([d_dn_in[:, j * n_dn:(j + 1) * n_dn] for j in range(N_SHARD)])
    group1 = ["dn_w_out", "dn_w_in"]
    gs1 = rs_grads(group1, [d_dn_out, d_dn_in])
    (dh2, dmn1, dh2b), sib1 = _mm_nt_rms(dproj1, wts["dn_w_in"], h2, wts["mix_norm_w"][1], dh3, "dn_in_bwd",
                                         ride=_Ride("pair", gs1))
    g["dn_conv_w"] = dcw.reshape(CONV_K, DN_CONV_CH)
    g["dn_a_log"] = dal[0, DECAY_LANE:DECAY_LANE + DN_HEADS]
    g["dn_dt_bias"] = ddt[0, DECAY_LANE:DECAY_LANE + DN_HEADS]

    dh1, dh1b, _, dfn0, others1, gs2, sib2 = ffn_bwd(
        dh2, dh2b, h1, hn1, g0, u0, act0, 0, ffn_grads,
        ride=_Ride("chips", rs_partials(group1, gs1, sib1)), last=True)
    rs_end(group1, gs1, sib1, others1, "1")
    d_ret_out = _mm(y0, dh1b, mode="tn", name="ret_dwo")
    gs2b = rs_grads(["ret_w_out"], [d_ret_out])
    dy0, sib2b = _mm(dh1b, wts["ret_w_out"], mode="nt", name="ret_out_bwd", ride=_Ride("pair", gs2b))
    group2 = ffn_names + ["ret_w_out"]
    gs2, sib2 = gs2 + gs2b, list(sib2) + list(sib2b)
    (dproj0, g["ret_gn_w"]), others2 = _ret_bwd(proj0, o0, dy0, st0, cos, sin, consts, wts["ret_gn_w"], seq,
                                                ride=_Ride("chips", rs_partials(group2, gs2, sib2)))
    rs_end(group2, gs2, sib2, others2, "2")
    d_ret_in = _mm_cols_grad(hn0, dproj0, "ret_dwi")
    gs3 = rs_grads(["ret_w_in"], [d_ret_in])
    sib3 = _rs_pair(gs3, "rs_pair3")
    (dh0, dmn0, _), others3 = _mm_cols_t_rms(dproj0, wts["ret_w_in"], h0, wts["mix_norm_w"][0], dh1, "ret_in_bwd",
                                             ride=_Ride("chips", rs_partials(["ret_w_in"], gs3, sib3)))
    rs_end(["ret_w_in"], gs3, sib3, others3, "3")

    g["ffn_norm_w"] = jnp.concatenate([dfn0, dfn1], axis=0)
    g["mix_norm_w"] = jnp.concatenate([dmn0, dmn1], axis=0)
    g["meta_tokens"] = dh0[PAD:CHUNK]
    g["final_norm_w"] = g["final_norm_w"].reshape(D_MODEL)
    g["ret_gn_w"] = g["ret_gn_w"].reshape(RET_DV)
    g["dn_norm_w"] = g["dn_norm_w"].reshape(DN_DV)
    return loss, dh0, g, red


def _mesh_pos():
    return lax.axis_index("x"), lax.axis_index("y"), lax.axis_index("c")


def _other_chips(x, y):
    return [(1 - x, y), (x, 1 - y), (1 - x, 1 - y)]


def _remote(src, dst, send_sem, recv_sem, to):
    return pltpu.make_async_remote_copy(src_ref=src, dst_ref=dst, send_sem=send_sem, recv_sem=recv_sem,
                                        device_id=to, device_id_type=MESH)


GATHER_COPIES = 7


def _gather_phase(phase, ins, outs, send_sems, recv_sems):
    x, y, c = _mesh_pos()
    me = 2 * x + y
    chips = _other_chips(x, y)
    sibling = (x, y, 1 - c)

    def cp(t, k, src, dst, to):
        i = GATHER_COPIES * t + k
        return _remote(src, dst, send_sems.at[i], recv_sems.at[i], to)

    for t in range(len(ins)):
        own = cp(t, 0, ins[t], outs[t].at[me], sibling)
        if phase == 0:
            own.start()
        if phase == 2:
            own.wait()
        for k, (px, py) in enumerate(chips):
            landed = outs[t].at[2 * px + py, c]
            theirs = outs[t].at[2 * px + py, 1 - c]
            to_chip = cp(t, 1 + k, ins[t].at[c], outs[t].at[me, c], (px, py, c))
            if phase == 0:
                to_chip.start()
            if phase == 1:
                cp(t, 1 + k, ins[t].at[c], landed, (px, py, c)).wait_recv()
                cp(t, 4 + k, landed, landed, sibling).start()
            if phase == 2:
                to_chip.wait_send()
                cp(t, 4 + k, landed, landed, sibling).wait_send()
                cp(t, 4 + k, theirs, theirs, sibling).wait_recv()


def _chips_phase(phase, ins, outs, send_sems, recv_sems):
    x, y, c = _mesh_pos()
    for t in range(len(ins)):
        for k, (px, py) in enumerate(_other_chips(x, y)):
            cp = _remote(ins[t].at[2 * px + py], outs[t].at[k], send_sems.at[3 * t + k], recv_sems.at[3 * t + k],
                         (px, py, c))
            if phase == 0:
                cp.start()
            if phase == 2:
                cp.wait()


class _Ride:
    def __init__(self, kind, arrays):
        self.kind, self.arrays = kind, list(arrays)
        nt = len(self.arrays)
        if kind == "gather":
            self.phase_fn, n_sem = _gather_phase, GATHER_COPIES * nt
            self.out_shape = [jax.ShapeDtypeStruct((N_SHARD,) + a.shape, a.dtype) for a in self.arrays]
        elif kind == "pair":
            self.phase_fn, n_sem = _pair_phase, nt
            self.out_shape = [jax.ShapeDtypeStruct(a.shape[:1] + a.shape[2:], a.dtype) for a in self.arrays]
        else:
            self.phase_fn, n_sem = _chips_phase, 3 * nt
            self.out_shape = [jax.ShapeDtypeStruct((3,) + a.shape[1:], a.dtype) for a in self.arrays]
        self.in_specs, self.out_specs = [ANY] * nt, [ANY] * nt
        self.scratch = [pltpu.SemaphoreType.DMA((n_sem,)), pltpu.SemaphoreType.DMA((n_sem,))]

    def emit(self, step, nsteps, ins, outs, send_sems, recv_sems, before):
        mid = max(0, min((7 * nsteps) // 8, nsteps - 2))
        todo = [(0, 0), (1, mid)] if before else [(2, nsteps - 1)]
        for phase, at in todo:
            if phase == 1 and self.kind != "gather":
                continue

            @pl.when(step == at)
            def _(phase=phase):
                self.phase_fn(phase, ins, outs, send_sems, recv_sems)


def _gather_small(blk):
    r, wd = blk.shape

    def body(b_ref, out_ref, send_sems, recv_sems):
        x, y, c = _mesh_pos()
        chips = _other_chips(x, y)
        out_ref[2 * x + y] = b_ref[...]
        sends = [_remote(b_ref, out_ref.at[2 * x + y], send_sems.at[k], recv_sems.at[k], (px, py, c))
                 for k, (px, py) in enumerate(chips)]
        for cp in sends:
            cp.start()
        for k, (px, py) in enumerate(chips):
            _remote(b_ref, out_ref.at[2 * px + py], send_sems.at[k], recv_sems.at[k], (px, py, c)).wait_recv()
        for cp in sends:
            cp.wait_send()

    return pl.pallas_call(
        body, out_shape=jax.ShapeDtypeStruct((4, r, wd), blk.dtype), in_specs=[VMEM_SPEC], out_specs=VMEM_SPEC,
        scratch_shapes=[pltpu.SemaphoreType.DMA((3,)), pltpu.SemaphoreType.DMA((3,))],
        name="gather_small")(blk)


def _allreduce_small(blk):
    r, wd = blk.shape
    rels = [(dx, dy, dc) for dx in (0, 1) for dy in (0, 1) for dc in (0, 1) if dx or dy or dc]

    def body(b_ref, out_ref, buf_ref, send_sems, recv_sems):
        x, y, c = _mesh_pos()

        def peer(rel):
            dx, dy, dc = rel
            return (1 - x if dx else x, 1 - y if dy else y, 1 - c if dc else c)

        me = 4 * x + 2 * y + c
        buf_ref[me] = b_ref[...]
        sends = [_remote(b_ref, buf_ref.at[me], send_sems.at[k], recv_sems.at[k], peer(rel))
                 for k, rel in enumerate(rels)]
        for cp in sends:
            cp.start()
        for k, rel in enumerate(rels):
            px, py, pc = peer(rel)
            _remote(b_ref, buf_ref.at[4 * px + 2 * py + pc], send_sems.at[k], recv_sems.at[k],
                    (px, py, pc)).wait_recv()
        for cp in sends:
            cp.wait_send()
        acc = buf_ref[0]
        for d in range(1, 8):
            acc = acc + buf_ref[d]
        out_ref[...] = acc

    return pl.pallas_call(
        body, out_shape=jax.ShapeDtypeStruct((r, wd), blk.dtype), in_specs=[VMEM_SPEC], out_specs=VMEM_SPEC,
        scratch_shapes=[pltpu.VMEM((8, r, wd), blk.dtype), pltpu.SemaphoreType.DMA((7,)),
                        pltpu.SemaphoreType.DMA((7,))],
        name="allreduce_small")(blk)


def _rs_pair(gs, name):
    ride = _Ride("pair", gs)

    def body(*refs):
        nt = len(gs)
        for phase in (0, 2):
            _pair_phase(phase, refs[:nt], refs[nt:2 * nt], *refs[2 * nt:])

    return pl.pallas_call(body, out_shape=ride.out_shape, in_specs=ride.in_specs, out_specs=ride.out_specs,
                          scratch_shapes=ride.scratch, name=name)(*gs)


def _pair_phase(phase, ins, outs, send_sems, recv_sems):
    x, y, c = _mesh_pos()
    for t in range(len(ins)):
        cp = _remote(ins[t].at[:, 1 - c], outs[t], send_sems.at[t], recv_sems.at[t], (x, y, 1 - c))
        if phase == 0:
            cp.start()
        if phase == 2:
            cp.wait()


def _rs_tile(a, b):
    return _div_tile(a, 512 if b <= 1024 else 256, 16)


def _rs_pair_add(g, a, idx, name):
    _, _, rows, cols = g.shape
    tr = _rs_tile(rows, cols)

    def body(s_ref, g_ref, a_ref, p_ref):
        p_ref[...] = (g_ref[...] + a_ref[...]).astype(p_ref.dtype)

    blk = pl.BlockSpec((None, tr, cols), lambda j, i, s: (j, i, 0))
    spec = pltpu.PrefetchScalarGridSpec(
        num_scalar_prefetch=1, grid=(N_SHARD, rows // tr),
        in_specs=[pl.BlockSpec((None, None, tr, cols), lambda j, i, s: (j, s[0], i, 0)), blk], out_specs=blk)
    return pl.pallas_call(
        body, grid_spec=spec, out_shape=jax.ShapeDtypeStruct((N_SHARD, rows, cols), BF16), name=name,
        compiler_params=_params("parallel", "parallel"))(idx, g, a)


def _rs_final_add(g, a, b, idx, name):
    _, _, rows, cols = g.shape
    tr = _rs_tile(rows, cols)

    def body(s_ref, g_ref, a_ref, b0_ref, b1_ref, b2_ref, f_ref):
        own = g_ref[...] + a_ref[...]
        f_ref[...] = ((own + b0_ref[...].astype(F32)) + b1_ref[...].astype(F32)) + b2_ref[...].astype(F32)

    def b_spec(k):
        return pl.BlockSpec((None, tr, cols), lambda i, s: (k, i, 0))

    spec = pltpu.PrefetchScalarGridSpec(
        num_scalar_prefetch=1, grid=(rows // tr,),
        in_specs=[pl.BlockSpec((None, None, tr, cols), lambda i, s: (s[1], s[0], i, 0)),
                  pl.BlockSpec((None, tr, cols), lambda i, s: (s[1], i, 0)), b_spec(0), b_spec(1), b_spec(2)],
        out_specs=pl.BlockSpec((None, tr, cols), lambda i, s: (s[0], i, 0)))
    return pl.pallas_call(
        body, grid_spec=spec, out_shape=jax.ShapeDtypeStruct((2, rows, cols), F32), name=name,
        compiler_params=_params("parallel"))(idx, g, a, b, b, b)


def _rs_share(fs, name):
    nt = len(fs)

    def body(*refs):
        outs = refs[nt:2 * nt]
        send_sems, recv_sems = refs[2 * nt:]
        x, y, c = _mesh_pos()
        cps = [_remote(outs[t].at[c], outs[t].at[c], send_sems.at[t], recv_sems.at[t], (x, y, 1 - c))
               for t in range(nt)]
        for cp in cps:
            cp.start()
        for cp in cps:
            cp.wait()

    return pl.pallas_call(
        body, out_shape=[jax.ShapeDtypeStruct(f.shape, f.dtype) for f in fs],
        in_specs=[ANY] * nt, out_specs=[ANY] * nt, input_output_aliases={t: t for t in range(nt)},
        scratch_shapes=[pltpu.SemaphoreType.DMA((nt,)), pltpu.SemaphoreType.DMA((nt,))], name=name)(*fs)


def _adamw(w, g, m, v, name):
    lead, rows, cols = w.shape
    tr = rows // 4 if rows % 32 == 0 else rows

    def body(w_ref, g_ref, m_ref, v_ref, go_ref, d_ref, mo_ref, vo_ref):
        gv = g_ref[...]
        go_ref[...] = gv
        mn = ADAM_B1 * m_ref[...] + (1.0 - ADAM_B1) * gv
        vn = ADAM_B2 * v_ref[...] + (1.0 - ADAM_B2) * (gv * gv)
        m_hat = mn / (1.0 - ADAM_B1 ** ADAM_STEP)
        v_hat = vn / (1.0 - ADAM_B2 ** ADAM_STEP)
        d_ref[...] = -ADAM_LR * (m_hat / (jnp.sqrt(v_hat) + ADAM_EPS) + ADAM_WD * w_ref[...])
        mo_ref[...] = mn
        vo_ref[...] = vn

    blk = pl.BlockSpec((None, tr, cols), lambda l, i: (l, i, 0))
    out = jax.ShapeDtypeStruct((lead, rows, cols), F32)
    return pl.pallas_call(
        body, grid=(lead, rows // tr), in_specs=[blk] * 4, out_specs=[blk] * 4, out_shape=[out] * 4, name=name,
        compiler_params=_params("parallel", "parallel"))(w, g, m, v)


BIG = ["ret_w_in", "ret_w_out", "dn_w_in", "dn_w_out", "ffn_w_gate", "ffn_w_up", "ffn_w_down"]
TRANSPOSED_AT_BOUNDARY = {"dn_w_in": True, "ffn_w_gate": False, "ffn_w_up": False}
SMALL =["meta_tokens", "mix_norm_w", "ffn_norm_w", "ret_gn_w", "dn_conv_w", "dn_a_log", "dn_dt_bias",
         "dn_norm_w", "final_norm_w"]
SMALL_SHARDED = {"meta_tokens", "dn_conv_w", "dn_norm_w"}
ORDER = ["meta_tokens", "mix_norm_w", "ffn_norm_w", "ret_w_in", "ret_gn_w", "ret_w_out", "dn_w_in",
         "dn_conv_w", "dn_a_log", "dn_dt_bias", "dn_norm_w", "dn_w_out", "ffn_w_gate", "ffn_w_up",
         "ffn_w_down", "final_norm_w"]


def _halves(a):
    return a.reshape(2, -1, a.shape[-1])


def _pack_lanes(parts, align=8):
    flat = jnp.concatenate([p.reshape(-1) for p in parts])
    flat = jnp.pad(flat, (0, -flat.shape[0] % (align * LANES)))
    return flat.reshape(-1, LANES)


def _unpack(buf, shapes):
    lead = buf.shape[:-2]
    flat = buf.reshape(lead + (-1,))
    out, off = [], 0
    for shp in shapes:
        size = math.prod(shp)
        out.append(flat[..., off:off + size].reshape(lead + tuple(shp)))
        off += size
    return out


def _join_cols(shards):
    return jnp.concatenate([shards[j] for j in range(N_SHARD)], axis=-1)


def kernel(x, meta_tokens, mix_norm_w, ffn_norm_w, ret_w_in, ret_gn_w, ret_w_out, dn_w_in, dn_conv_w, dn_a_log, dn_dt_bias, dn_norm_w, dn_w_out, ffn_w_gate, ffn_w_up, ffn_w_down, final_norm_w, loss_target, m_meta_tokens, m_mix_norm_w, m_ffn_norm_w, m_ret_w_in, m_ret_gn_w, m_ret_w_out, m_dn_w_in, m_dn_conv_w, m_dn_a_log, m_dn_dt_bias, m_dn_norm_w, m_dn_w_out, m_ffn_w_gate, m_ffn_w_up, m_ffn_w_down, m_final_norm_w, v_meta_tokens, v_mix_norm_w, v_ffn_norm_w, v_ret_w_in, v_ret_gn_w, v_ret_w_out, v_dn_w_in, v_dn_conv_w, v_dn_a_log, v_dn_dt_bias, v_dn_norm_w, v_dn_w_out, v_ffn_w_gate, v_ffn_w_up, v_ffn_w_down, v_final_norm_w):
    w = dict(meta_tokens=meta_tokens, mix_norm_w=mix_norm_w, ffn_norm_w=ffn_norm_w, ret_w_in=ret_w_in,
             ret_gn_w=ret_gn_w, ret_w_out=ret_w_out, dn_w_in=dn_w_in, dn_conv_w=dn_conv_w, dn_a_log=dn_a_log,
             dn_dt_bias=dn_dt_bias, dn_norm_w=dn_norm_w, dn_w_out=dn_w_out, ffn_w_gate=ffn_w_gate,
             ffn_w_up=ffn_w_up, ffn_w_down=ffn_w_down, final_norm_w=final_norm_w)
    m = dict(meta_tokens=m_meta_tokens, mix_norm_w=m_mix_norm_w, ffn_norm_w=m_ffn_norm_w, ret_w_in=m_ret_w_in,
             ret_gn_w=m_ret_gn_w, ret_w_out=m_ret_w_out, dn_w_in=m_dn_w_in, dn_conv_w=m_dn_conv_w,
             dn_a_log=m_dn_a_log, dn_dt_bias=m_dn_dt_bias, dn_norm_w=m_dn_norm_w, dn_w_out=m_dn_w_out,
             ffn_w_gate=m_ffn_w_gate, ffn_w_up=m_ffn_w_up, ffn_w_down=m_ffn_w_down, final_norm_w=m_final_norm_w)
    v = dict(meta_tokens=v_meta_tokens, mix_norm_w=v_mix_norm_w, ffn_norm_w=v_ffn_norm_w, ret_w_in=v_ret_w_in,
             ret_gn_w=v_ret_gn_w, ret_w_out=v_ret_w_out, dn_w_in=v_dn_w_in, dn_conv_w=v_dn_conv_w,
             dn_a_log=v_dn_a_log, dn_dt_bias=v_dn_dt_bias, dn_norm_w=v_dn_norm_w, dn_w_out=v_dn_w_out,
             ffn_w_gate=v_ffn_w_gate, ffn_w_up=v_ffn_w_up, ffn_w_down=v_ffn_w_down, final_norm_w=v_final_norm_w)
    mx, my, mc = _mesh_pos()
    chip = 2 * mx + my

    sm_names = [n for n in SMALL if n in SMALL_SHARDED]
    sm_gathered = _unpack(_gather_small(_pack_lanes([w[n] for n in sm_names])), [w[n].shape for n in sm_names])
    full = {n: _join_cols(sm_gathered[i]) for i, n in enumerate(sm_names)}
    wts = {
        "meta_tokens": full["meta_tokens"], "mix_norm_w": mix_norm_w, "ffn_norm_w": ffn_norm_w,
        "ret_gn_w": ret_gn_w[0], "final_norm_w": final_norm_w, "dn_conv_w": full["dn_conv_w"][0],
        "dn_a_log": dn_a_log[0], "dn_dt_bias": dn_dt_bias[0], "dn_norm_w": full["dn_norm_w"][0],
    }
    idx = jnp.stack([mc, chip]).astype(jnp.int32)
    shards = {n: _halves(w[n].astype(MXU_DTYPE)) for n in BIG}
    loss_part, dh0, g, reduced = _train_step(x[0], loss_target[0], wts, shards, idx)
    seq = x.shape[1]
    grad_x = dh0[CHUNK:CHUNK + seq].reshape(x.shape)
    gsh = {}

    small_full_shapes = [g[n].shape for n in SMALL] + [(1,)]
    red = _unpack(_allreduce_small(_pack_lanes([g[n] for n in SMALL] + [loss_part[0, :1]])), small_full_shapes)
    loss = red[-1][0]
    for i, n in enumerate(SMALL):
        gn = red[i]
        if n in SMALL_SHARDED:
            width = w[n].shape[-1]
            gn = lax.dynamic_slice_in_dim(gn, chip * width, width, axis=gn.ndim - 1)
        gsh[n] = gn.reshape(w[n].shape)

    delta, new_m, new_v = {}, {}, {}
    for n in BIG:
        shp = w[n].shape
        if n in TRANSPOSED_AT_BOUNDARY and TRANSPOSED_AT_BOUNDARY[n]:
            view = lambda a: jnp.swapaxes(a, 1, 2).reshape(1, -1, LANES)
            back = lambda a: jnp.swapaxes(a.reshape(shp[0], shp[2], shp[1]), 1, 2)
        elif n in TRANSPOSED_AT_BOUNDARY:
            view = back = lambda a: jnp.swapaxes(a, 1, 2)
        else:
            view = back = lambda a: a
        res = _adamw(view(w[n]), view(reduced[n].reshape(shp)), view(m[n]), view(v[n]), "adamw_" + n)
        gsh[n], delta[n], new_m[n], new_v[n] = [back(r) for r in res]
    sm_local_shapes = [w[n].shape for n in SMALL]
    _, d_, m_, v_ = _adamw(*[_pack_lanes([t[n] for n in SMALL])[None] for t in (w, gsh, m, v)], "adamw_small")
    d_, m_, v_ = d_[0], m_[0], v_[0]
    for n, dd, mm, vv in zip(SMALL, _unpack(d_, sm_local_shapes), _unpack(m_, sm_local_shapes),
                             _unpack(v_, sm_local_shapes)):
        delta[n], new_m[n], new_v[n] = dd, mm, vv

    return (loss, grad_x, *[gsh[n] for n in ORDER], *[delta[n] for n in ORDER],
            *[new_m[n] for n in ORDER], *[new_v[n] for n in ORDER])
```

```python
import math

import jax
import jax.numpy as jnp
from jax import lax
from jax.experimental import pallas as pl
from jax.experimental.pallas import tpu as pltpu

F32 = jnp.float32
BF16 = jnp.bfloat16
MXU_DTYPE = BF16

D_MODEL = 1024
N_META = 16
CHUNK = 64
PAD = CHUNK - N_META
RMS_EPS = 1e-6
RET_HEADS, RET_DK, RET_DV = 4, 256, 512
RET_QK, RET_V = RET_HEADS * RET_DK, RET_HEADS * RET_DV
RET_IN = 2 * RET_QK + 2 * RET_V
ROPE_BASE = 10000.0
DN_HEADS, DN_DK, DN_DV = 8, 128, 256
DN_QK, DN_V = DN_HEADS * DN_DK, DN_HEADS * DN_DV
DN_CONV_CH = 2 * DN_QK + DN_V
DN_IN = DN_CONV_CH + DN_V + 2 * DN_HEADS
LANES = 128
DN_IN_USED = DN_CONV_CH + DN_V + LANES
DN_IN_PAD = DN_IN_USED + LANES
CONV_K = 4
FFN_HIDDEN = 2816
ADAM_LR, ADAM_B1, ADAM_B2, ADAM_EPS, ADAM_WD, ADAM_STEP = 0.001, 0.9, 0.999, 1e-08, 0.01, 10

ROW_ALIGN = 256
VMEM_LIMIT = 60 * 1024 * 1024
MESH = pl.DeviceIdType.MESH
ANY = pl.BlockSpec(memory_space=pl.ANY)
VMEM_SPEC = pl.BlockSpec(memory_space=pltpu.VMEM)
_HI = lax.Precision.HIGHEST


def _params(*sem):
    return pltpu.CompilerParams(dimension_semantics=sem, vmem_limit_bytes=VMEM_LIMIT)


def _dg(a, b, ca, cb, hi):
    dims = (((ca,), (cb,)), ((), ()))

    def dot(p, q):
        return lax.dot_general(p, q, dims, preferred_element_type=F32)

    if not hi:
        return dot(a.astype(MXU_DTYPE), b.astype(MXU_DTYPE))
    if MXU_DTYPE == F32:
        return lax.dot_general(a, b, dims, precision=_HI, preferred_element_type=F32)
    a_hi, b_hi = a.astype(MXU_DTYPE), b.astype(MXU_DTYPE)
    a_lo = (a - a_hi.astype(F32)).astype(MXU_DTYPE)
    b_lo = (b - b_hi.astype(F32)).astype(MXU_DTYPE)
    return dot(a_hi, b_hi) + (dot(a_hi, b_lo) + dot(a_lo, b_hi))


def _nn(a, b, hi=False):
    return _dg(a, b, 1, 0, hi)


def _nt(a, b, hi=False):
    return _dg(a, b, 1, 1, hi)


def _tn(a, b, hi=False):
    return _dg(a, b, 0, 0, hi)


def _iota(shape, dim):
    return lax.broadcasted_iota(jnp.int32, shape, dim)


def _valid_rows(first_row, rows, seq):
    r = first_row + _iota((rows, 1), 0)
    return ((r >= PAD) & (r < CHUNK + seq)).astype(F32)


def _rope(t, cs, sn):
    half = t.shape[-1] // 2
    t1, t2 = t[:, :half], t[:, half:]
    return jnp.concatenate([t1 * cs - t2 * sn, t1 * sn + t2 * cs], axis=1)


def _rope_bwd(d, cs, sn):
    half = d.shape[-1] // 2
    d1, d2 = d[:, :half], d[:, half:]
    return jnp.concatenate([d1 * cs + d2 * sn, d2 * cs - d1 * sn], axis=1)


def _col(x, idx):
    oh = (_iota((1, x.shape[1]), 1) == idx).astype(F32)
    return jnp.sum(x * oh, axis=1, keepdims=True)


def _row(x, idx):
    oh = (_iota((x.shape[0], 1), 0) == idx).astype(F32)
    return jnp.sum(x * oh, axis=0, keepdims=True)


def _shift_down(x, halo8, k):
    xr = pltpu.roll(x, k, 0)
    hr = pltpu.roll(halo8, k, 0)
    first = jnp.where(_iota((8, 1), 0) < k, hr, xr[0:8])
    return jnp.concatenate([first, xr[8:]], axis=0)


def _shift_up(x, next8, j):
    rows = x.shape[0]
    xr = pltpu.roll(x, rows - j, 0)
    nr = pltpu.roll(next8, 8 - j, 0)
    last = jnp.where(_iota((8, 1), 0) >= 8 - j, nr, xr[rows - 8:])
    return jnp.concatenate([xr[:rows - 8], last], axis=0)


def _gated_norm(o, gate, w):
    r = lax.rsqrt(jnp.mean(o * o, axis=-1, keepdims=True) + RMS_EPS)
    return o * r * w * (gate * jax.nn.sigmoid(gate))


def _gated_norm_bwd(dy, o, gate, w):
    r = lax.rsqrt(jnp.mean(o * o, axis=-1, keepdims=True) + RMS_EPS)
    nrm = o * r
    sg = jax.nn.sigmoid(gate)
    sl = gate * sg
    dgate = dy * nrm * w * (sg * (1.0 + gate * (1.0 - sg)))
    dn = dy * w * sl
    dw = jnp.sum(dy * nrm * sl, axis=0, keepdims=True)
    do = r * (dn - nrm * jnp.mean(dn * nrm, axis=-1, keepdims=True))
    return do, dgate, dw


def _softplus(z):
    return jnp.maximum(z, 0.0) + jnp.log(1.0 + jnp.exp(-jnp.abs(z)))


def _row_tile(rows, cap=768):
    for t in (768, 512, 256, 128, 64, 32, 16, 8):
        if t <= cap and rows % t == 0:
            return t
    return rows


TILE_BUDGET = 56 * 1024 * 1024


def _fit_rows(rows, row_bytes, fixed_bytes, value_row_bytes):
    best = None
    for t in range(LANES, rows + 1, LANES):
        if rows % t == 0 and 2 * (row_bytes * t + fixed_bytes) + value_row_bytes * t <= TILE_BUDGET:
            best = t
    return best or _row_tile(rows, 256)


def _div_tile(n, cap, mult):
    best = None
    for t in range(mult, min(cap, n) + 1, mult):
        if n % t == 0:
            best = t
    return best or n


def _col_tile(cols, cap=1536):
    best = None
    for t in range(LANES, min(cap, cols) + 1, LANES):
        if cols % t == 0:
            best = t
    return best or cols


def _embed(x, tgt, meta, rows, ride=None):
    seq, d = x.shape
    n_tok = seq // CHUNK

    def body(xa_ref, xb_ref, ta_ref, tb_ref, m_ref, h_ref, tp_ref):
        i = pl.program_id(0)
        first = jnp.concatenate([jnp.zeros((PAD, d), F32), m_ref[...]], axis=0)
        for half, (x_ref, t_ref) in enumerate(((xa_ref, ta_ref), (xb_ref, tb_ref))):
            k = 2 * i + half
            tokens = (k >= 1) & (k <= n_tok)
            rs = slice(half * CHUNK, (half + 1) * CHUNK)
            h_ref[rs, :] = jnp.where(k == 0, first, jnp.where(tokens, x_ref[...], 0.0))
            tp_ref[rs, :] = jnp.where(tokens, t_ref[...], 0.0)

    def tok(half):
        return pl.BlockSpec((CHUNK, d), lambda i: (jnp.clip(2 * i + half - 1, 0, n_tok - 1), 0))

    out = pl.BlockSpec((2 * CHUNK, d), lambda i: (i, 0))
    res, rode = _pcall(body, [x, x, tgt, tgt, meta], grid=(rows // (2 * CHUNK),),
                       in_specs=[tok(0), tok(1), tok(0), tok(1), pl.BlockSpec((N_META, d), lambda i: (0, 0))],
                       out_specs=[out, out], out_shape=[jax.ShapeDtypeStruct((rows, d), F32)] * 2, name="embed",
                       sem=("parallel",), ride=ride)
    return res if ride is None else (res, rode)


def _rms_fwd(h, w, name, ride=None):
    rows, d = h.shape
    tm = _row_tile(rows)

    def body(h_ref, w_ref, o_ref):
        x = h_ref[...]
        r = lax.rsqrt(jnp.mean(x * x, axis=-1, keepdims=True) + RMS_EPS)
        o_ref[...] = (x * r * w_ref[...]).astype(o_ref.dtype)

    res, rode = _pcall(body, [h, w.reshape(1, d)], grid=(rows // tm,),
                       in_specs=[pl.BlockSpec((tm, d), lambda i: (i, 0)), pl.BlockSpec((1, d), lambda i: (0, 0))],
                       out_specs=[pl.BlockSpec((tm, d), lambda i: (i, 0))],
                       out_shape=[jax.ShapeDtypeStruct((rows, d), BF16)], name=name, sem=("parallel",), ride=ride)
    return res[0] if ride is None else (res[0], rode)


def _gmm_rms(name, grid, args, in_specs, row_spec, fn, h, w, resid, row_axis, red_axis=None, ride=None):
    m, d = h.shape
    n_in = len(args)
    vec = pl.BlockSpec((1, d), lambda *g: (0, 0))

    def body(*refs):
        ins = refs[:n_in]
        h_ref, w_ref, r_ref, dh_ref, dw_ref, dh16_ref = refs[n_in:]
        part = fn(*ins)
        row = pl.program_id(row_axis)

        def finish(dy):
            x = h_ref[...]
            r = lax.rsqrt(jnp.mean(x * x, axis=-1, keepdims=True) + RMS_EPS)
            xh = x * r
            dxh = dy * w_ref[...]
            dh = r_ref[...] + r * (dxh - xh * jnp.mean(dxh * xh, axis=-1, keepdims=True))
            dh_ref[...] = dh
            dh16_ref[...] = dh.astype(dh16_ref.dtype)
            dwp = jnp.sum(dy * xh, axis=0, keepdims=True)

            @pl.when(row == 0)
            def _():
                dw_ref[...] = dwp

            @pl.when(row > 0)
            def _():
                dw_ref[...] += dwp

        if red_axis is None:
            finish(part)
            return
        k = pl.program_id(red_axis)

        @pl.when(k == 0)
        def _():
            dh_ref[...] = part

        @pl.when(k > 0)
        def _():
            dh_ref[...] += part

        @pl.when(k == grid[red_axis] - 1)
        def _():
            finish(dh_ref[...])

    res, rode = _pcall(body, list(args) + [h, w.reshape(1, d), resid], grid=grid,
                       in_specs=list(in_specs) + [row_spec, vec, row_spec], out_specs=[row_spec, vec, row_spec],
                       out_shape=[jax.ShapeDtypeStruct((m, d), F32), jax.ShapeDtypeStruct((1, d), F32),
                                  jax.ShapeDtypeStruct((m, d), BF16)],
                       name=name, sem=("arbitrary",) * len(grid), ride=ride)
    return res if ride is None else (res, rode)


def _final_loss(h, w, tgt, seq, name):
    rows, d = h.shape
    tm = _row_tile(rows)

    def body(h_ref, w_ref, t_ref, dh_ref, dw_ref, loss_ref, dh16_ref):
        i = pl.program_id(0)
        r_idx = i * tm + _iota((tm, 1), 0)
        m = ((r_idx >= CHUNK) & (r_idx < CHUNK + seq)).astype(F32)
        x = h_ref[...]
        wv = w_ref[...]
        r = lax.rsqrt(jnp.mean(x * x, axis=-1, keepdims=True) + RMS_EPS)
        xh = x * r
        err = (xh * wv - t_ref[...]) * m
        lpart = 0.5 * jnp.sum(jnp.mean(err * err, axis=-1, keepdims=True), axis=0, keepdims=True)
        dyv = err * (1.0 / d)
        dxh = dyv * wv
        dh = r * (dxh - xh * jnp.mean(dxh * xh, axis=-1, keepdims=True))
        dh_ref[...] = dh
        dh16_ref[...] = dh.astype(dh16_ref.dtype)
        part = jnp.sum(dyv * xh, axis=0, keepdims=True)

        @pl.when(i == 0)
        def _():
            dw_ref[...] = part
            loss_ref[...] = jnp.broadcast_to(lpart, loss_ref.shape)

        @pl.when(i > 0)
        def _():
            dw_ref[...] += part
            loss_ref[...] += jnp.broadcast_to(lpart, loss_ref.shape)

    blk = pl.BlockSpec((tm, d), lambda i: (i, 0))
    vec = pl.BlockSpec((1, d), lambda i: (0, 0))
    return pl.pallas_call(
        body, grid=(rows // tm,), in_specs=[blk, vec, blk],
        out_specs=[blk, vec, pl.BlockSpec((1, LANES), lambda i: (0, 0)), blk],
        out_shape=[jax.ShapeDtypeStruct((rows, d), F32), jax.ShapeDtypeStruct((1, d), F32),
                   jax.ShapeDtypeStruct((1, LANES), F32), jax.ShapeDtypeStruct((rows, d), BF16)],
        name=name, compiler_params=_params("arbitrary"))(h, w.reshape(1, d), tgt)


def _isz(x):
    return jnp.dtype(x.dtype).itemsize


def _mm(a, b, *, mode, name, out_dtype=F32, resid=None, col_cap=1536, ride=None):
    if mode == "tn":
        m, k = a.shape
        n = b.shape[1]
        tn = _col_tile(n, col_cap)
        tm = _fit_rows(m, k * _isz(a) + tn * _isz(b), (3 * k * tn * 4) // 2, 2 * (k + tn))

        def body_tn(a_ref, b_ref, o_ref):
            i = pl.program_id(1)
            part = _tn(a_ref[...], b_ref[...])

            @pl.when(i == 0)
            def _():
                o_ref[...] = part

            @pl.when(i > 0)
            def _():
                o_ref[...] += part

        return pl.pallas_call(
            body_tn, grid=(n // tn, m // tm),
            in_specs=[pl.BlockSpec((tm, k), lambda j, i: (i, 0)),
                      pl.BlockSpec((tm, tn), lambda j, i: (i, j))],
            out_specs=pl.BlockSpec((k, tn), lambda j, i: (0, j)),
            out_shape=jax.ShapeDtypeStruct((k, n), F32), name=name,
            compiler_params=_params("parallel", "arbitrary"))(a, b)

    m, ka = a.shape
    n = b.shape[1] if mode == "nn" else b.shape[0]
    has_resid = resid is not None
    tn = _col_tile(n, col_cap)
    tm = _fit_rows(m, ka * _isz(a) + tn * (jnp.dtype(out_dtype).itemsize + (4 if has_resid else 0)),
                   ka * tn * _isz(b), 2 * ka + 8 * tn)

    def body(*refs):
        if has_resid:
            a_ref, b_ref, r_ref, o_ref = refs
        else:
            a_ref, b_ref, o_ref = refs
        acc = _nn(a_ref[...], b_ref[...]) if mode == "nn" else _nt(a_ref[...], b_ref[...])
        if has_resid:
            acc = acc + r_ref[...]
        o_ref[...] = acc.astype(o_ref.dtype)

    b_spec = (pl.BlockSpec((b.shape[0], tn), lambda j, i: (0, j)) if mode == "nn"
              else pl.BlockSpec((tn, b.shape[1]), lambda j, i: (j, 0)))
    o_spec = pl.BlockSpec((tm, tn), lambda j, i: (i, j))
    in_specs = [pl.BlockSpec((tm, ka), lambda j, i: (i, 0)), b_spec]
    args = [a, b]
    if has_resid:
        in_specs.append(o_spec)
        args.append(resid)
    res, rode = _pcall(body, args, grid=(n // tn, m // tm), in_specs=in_specs, out_specs=[o_spec],
                       out_shape=[jax.ShapeDtypeStruct((m, n), out_dtype)], name=name,
                       sem=("parallel", "parallel"), ride=ride)
    return res[0] if ride is None else (res[0], rode)


N_SHARD = 4


def _gmm(name, grid, args, in_specs, out_specs, out_shape, fn, red_axis=None, init_arg=None, aliases=None,
         ride=None):
    n_in = len(args)
    single = not isinstance(out_shape, (list, tuple))
    out_specs = [out_specs] if single else list(out_specs)
    out_shape = [out_shape] if single else list(out_shape)

    def body(*refs):
        _gmm_step(fn, refs[:n_in], refs[n_in:], red_axis, init_arg)

    sem = tuple("arbitrary" if ax == red_axis else "parallel" for ax in range(len(grid)))
    res, rode = _pcall(body, args, grid=grid, in_specs=in_specs, out_specs=out_specs, out_shape=out_shape,
                       name=name, sem=sem, aliases=aliases, ride=ride)
    ours = res[0] if single else res
    return ours if ride is None else (ours, rode)


def _gmm_step(fn, ins, outs, red_axis, init_arg):
    parts = fn(*ins)
    if red_axis is None:
        for o_ref, p in zip(outs, parts):
            o_ref[...] = p.astype(o_ref.dtype)
        return
    k = pl.program_id(red_axis)

    @pl.when(k == 0)
    def _():
        for idx, (o_ref, p) in enumerate(zip(outs, parts)):
            o_ref[...] = p + ins[init_arg][...] if (idx == 0 and init_arg is not None) else p

    @pl.when(k > 0)
    def _():
        for o_ref, p in zip(outs, parts):
            o_ref[...] += p


def _ride_body(ride, grid, n_in, n_out, n_scratch, body):
    n_rin, n_rout = len(ride.arrays), len(ride.out_shape)
    nsteps = math.prod(grid)

    def wrapped(*refs):
        ins = refs[:n_in]
        r_ins = refs[n_in:n_in + n_rin]
        o0 = n_in + n_rin
        outs = refs[o0:o0 + n_out]
        r_outs = refs[o0 + n_out:o0 + n_out + n_rout]
        s0 = o0 + n_out + n_rout
        scratch = refs[s0:s0 + n_scratch]
        send_sems, recv_sems = refs[-2:]
        step = pl.program_id(0)
        for ax in range(1, len(grid)):
            step = step * grid[ax] + pl.program_id(ax)
        ride.emit(step, nsteps, r_ins, r_outs, send_sems, recv_sems, before=True)
        body(*ins, *outs, *scratch)
        ride.emit(step, nsteps, r_ins, r_outs, send_sems, recv_sems, before=False)

    return wrapped


def _pcall(body, args, *, grid, in_specs, out_specs, out_shape, name, sem, scratch=(), aliases=None, ride=None):
    if ride is None:
        res = pl.pallas_call(body, grid=grid, in_specs=list(in_specs), out_specs=list(out_specs),
                             out_shape=list(out_shape), scratch_shapes=list(scratch), name=name,
                             input_output_aliases=aliases or {}, compiler_params=_params(*sem))(*args)
        return res, None
    n_in, n_out = len(args), len(out_shape)
    res = pl.pallas_call(
        _ride_body(ride, grid, n_in, n_out, len(scratch), body), grid=grid,
        in_specs=list(in_specs) + ride.in_specs, out_specs=list(out_specs) + ride.out_specs,
        out_shape=list(out_shape) + ride.out_shape, scratch_shapes=list(scratch) + ride.scratch, name=name,
        input_output_aliases=aliases or {},
        compiler_params=_params(*(("arbitrary",) * len(grid))))(*args, *ride.arrays)
    return res[:n_out], res[n_out:]


def _mm_cols(a, ws, name, ride=None):
    m, k = a.shape
    n = ws.shape[2]
    tm = _fit_rows(m, k * _isz(a) + n * 4, k * n * _isz(ws), 4 * n)
    return _gmm(name, (N_SHARD, m // tm), [a, ws],
                [pl.BlockSpec((tm, k), lambda j, i: (i, 0)), pl.BlockSpec((None, k, n), lambda j, i: (j, 0, 0))],
                pl.BlockSpec((tm, n), lambda j, i: (i, j)), jax.ShapeDtypeStruct((m, N_SHARD * n), F32),
                lambda a_ref, w_ref: (_nn(a_ref[...], w_ref[...]),), ride=ride)


def _mm_cols_t_rms(d, ws, h, w, resid, name, ride=None):
    m = d.shape[0]
    _, k, n = ws.shape
    tm = _fit_rows(m, n * _isz(d) + 3 * k * 4, k * n * _isz(ws), 16 * k)
    return _gmm_rms(name, (m // tm, N_SHARD), [d, ws],
                    [pl.BlockSpec((tm, n), lambda i, j: (i, j)), pl.BlockSpec((None, k, n), lambda i, j: (j, 0, 0))],
                    pl.BlockSpec((tm, k), lambda i, j: (i, 0)),
                    lambda d_ref, w_ref: _nt(d_ref[...], w_ref[...]), h, w, resid, 0, red_axis=1, ride=ride)


def _mm_nt_rms(a, b, h, w, resid, name, ride=None):
    m, n = a.shape
    k = b.shape[0]
    tm = _fit_rows(m, n * _isz(a) + 3 * k * 4, k * n * _isz(b), 16 * k)
    return _gmm_rms(name, (m // tm,), [a, b],
                    [pl.BlockSpec((tm, n), lambda i: (i, 0)), pl.BlockSpec((k, n), lambda i: (0, 0))],
                    pl.BlockSpec((tm, k), lambda i: (i, 0)),
                    lambda a_ref, b_ref: _nt(a_ref[...], b_ref[...]), h, w, resid, 0, ride=ride)


def _mm_cols_grad(a, d, name):
    m, k = a.shape
    n = d.shape[1] // N_SHARD
    tm = _fit_rows(m, k * _isz(a) + n * _isz(d), (3 * k * n * 4) // 2, 2 * (k + n))
    return _gmm(name, (N_SHARD, m // tm), [a, d],
                [pl.BlockSpec((tm, k), lambda j, i: (i, 0)), pl.BlockSpec((tm, n), lambda j, i: (i, j))],
                pl.BlockSpec((None, k, n), lambda j, i: (j, 0, 0)), jax.ShapeDtypeStruct((N_SHARD, k, n), F32),
                lambda a_ref, d_ref: (_tn(a_ref[...], d_ref[...]),), red_axis=1)


def _ffn_up(hn, wg, wu, layer, name, ride=None):
    m, k = hn.shape
    n = wg.shape[3]
    tm = _fit_rows(m, k * _isz(hn) + 3 * n * jnp.dtype(BF16).itemsize, 2 * k * n * _isz(wg), 16 * n)

    def fn(a_ref, wg_ref, wu_ref):
        a = a_ref[...]
        g = _nn(a, wg_ref[...])
        u = _nn(a, wu_ref[...])
        return g, u, g * jax.nn.sigmoid(g) * u

    w_spec = pl.BlockSpec((None, None, k, n), lambda j, i: (j, layer, 0, 0))
    o_spec = pl.BlockSpec((None, tm, n), lambda j, i: (j, i, 0))
    out = jax.ShapeDtypeStruct((N_SHARD, m, n), BF16)
    return _gmm(name, (N_SHARD, m // tm), [hn, wg, wu],
                [pl.BlockSpec((tm, k), lambda j, i: (i, 0)), w_spec, w_spec],
                [o_spec, o_spec, o_spec], [out, out, out], fn, ride=ride)


def _ffn_down(act, wd, resid, layer, name, ride=None):
    _, m, n = act.shape
    d = wd.shape[3]
    tm = _fit_rows(m, N_SHARD * n * _isz(act) + 2 * d * 4, N_SHARD * n * d * _isz(wd), 8 * d)

    def fn(a_ref, w_ref, r_ref):
        acc = r_ref[...]
        for j in range(N_SHARD):
            acc = acc + _nn(a_ref[j], w_ref[j])
        return (acc,)

    row = pl.BlockSpec((tm, d), lambda i: (i, 0))
    return _gmm(name, (m // tm,), [act, wd, resid],
                [pl.BlockSpec((N_SHARD, tm, n), lambda i: (0, i, 0)),
                 pl.BlockSpec((N_SHARD, None, n, d), lambda i: (0, layer, 0, 0)), row],
                row, jax.ShapeDtypeStruct((m, d), F32), fn, ride=ride)


def _ffn_down_bwd(dh, wd, g, u, layer, name, ride=None):
    m, d = dh.shape
    n = wd.shape[2]
    tm = _fit_rows(m, d * _isz(dh) + 4 * N_SHARD * n * jnp.dtype(BF16).itemsize, N_SHARD * n * d * _isz(wd),
                   2 * d + 24 * n)

    def body(dh_ref, wd_ref, g_ref, u_ref, dg_ref, du_ref):
        dhv = dh_ref[...].astype(MXU_DTYPE)
        for j in range(N_SHARD):
            dact = _nt(dhv, wd_ref[j])
            gv = g_ref[j].astype(F32)
            sg = jax.nn.sigmoid(gv)
            gs = gv * sg
            dg_ref[j] = (dact * u_ref[j].astype(F32) * (sg + gs * (1.0 - sg))).astype(dg_ref.dtype)
            du_ref[j] = (dact * gs).astype(du_ref.dtype)

    sh_spec = pl.BlockSpec((N_SHARD, tm, n), lambda i: (0, i, 0))
    out = jax.ShapeDtypeStruct((N_SHARD, m, n), BF16)
    res, rode = _pcall(body, [dh, wd, g, u], grid=(m // tm,),
                       in_specs=[pl.BlockSpec((tm, d), lambda i: (i, 0)),
                                 pl.BlockSpec((N_SHARD, None, n, d), lambda i: (0, layer, 0, 0)), sh_spec, sh_spec],
                       out_specs=[sh_spec, sh_spec], out_shape=[out, out], name=name, sem=("parallel",), ride=ride)
    return res if ride is None else (res, rode)


def _ffn_up_bwd(dg, du, wg, wu, layer, h, w, resid, name, ride=None):
    _, m, n = dg.shape
    k = wg.shape[2]
    tm = _fit_rows(m, 2 * N_SHARD * n * _isz(dg) + 3 * k * 4, 2 * N_SHARD * k * n * _isz(wg), 16 * k)

    def fn(dg_ref, du_ref, wg_ref, wu_ref):
        acc = _nt(dg_ref[0], wg_ref[0]) + _nt(du_ref[0], wu_ref[0])
        for j in range(1, N_SHARD):
            acc = acc + _nt(dg_ref[j], wg_ref[j]) + _nt(du_ref[j], wu_ref[j])
        return acc

    d_spec = pl.BlockSpec((N_SHARD, tm, n), lambda i: (0, i, 0))
    w_spec = pl.BlockSpec((N_SHARD, None, k, n), lambda i: (0, layer, 0, 0))
    return _gmm_rms(name, (m // tm,), [dg, du, wg, wu], [d_spec, d_spec, w_spec, w_spec],
                    pl.BlockSpec((tm, k), lambda i: (i, 0)), fn, h, w, resid, 0, ride=ride)


def _ffn_wgrad(lhs, rhs_list, layer, layers, prev, lhs_sharded, name):
    if lhs_sharded:
        _, m, k = lhs.shape
        n = rhs_list[0].shape[1]
    else:
        m, k = lhs.shape
        n = rhs_list[0].shape[2]
    n_out = len(rhs_list)
    tm = _fit_rows(m, k * _isz(lhs) + n_out * n * _isz(rhs_list[0]), (3 * n_out * k * n * 4) // 2,
                   2 * (k + n_out * n))
    sh = pl.BlockSpec((None, tm, k if lhs_sharded else n), lambda j, i: (j, i, 0))
    fl = pl.BlockSpec((tm, n if lhs_sharded else k), lambda j, i: (i, 0))
    n_out = len(rhs_list)
    args = [lhs] + list(rhs_list)
    in_specs = [sh if lhs_sharded else fl] + [fl if lhs_sharded else sh] * n_out
    aliases = None
    if prev is not None:
        aliases = {len(args) + t: t for t in range(n_out)}
        args = args + list(prev)
        in_specs = in_specs + [ANY] * n_out

    def fn(l_ref, *rest):
        lv = l_ref[...]
        return tuple(_tn(lv, r_ref[...]) for r_ref in rest[:n_out])

    o_spec = pl.BlockSpec((None, None, k, n), lambda j, i: (j, layer, 0, 0))
    out = jax.ShapeDtypeStruct((N_SHARD, layers, k, n), F32)
    return _gmm(name, (N_SHARD, m // tm), args, in_specs, [o_spec] * n_out, [out] * n_out, fn,
                red_axis=1, aliases=aliases)


def _ret_consts():
    log_gamma = jnp.log1p(-jnp.exp2(-5.0 - jnp.arange(RET_HEADS, dtype=F32)))
    idx = jnp.arange(CHUNK, dtype=F32)
    rel = idx[:, None] - idx[None, :]
    dmask = jnp.where((rel >= 0)[None], jnp.exp(log_gamma[:, None, None] * jnp.maximum(rel, 0.0)), 0.0)
    xi = jnp.exp(log_gamma[:, None] * (idx[None, :] + 1.0))[:, :, None]
    zeta = jnp.exp(log_gamma[:, None] * (CHUNK - 1.0 - idx[None, :]))[:, :, None]
    gamma_c = jnp.exp(log_gamma * CHUNK)
    wide = (RET_HEADS, CHUNK, RET_DK)
    return dmask, jnp.broadcast_to(xi, wide), jnp.broadcast_to(zeta, wide), gamma_c


def _rope_tables(nc):
    half = RET_DK // 2
    inv_freq = ROPE_BASE ** (-jnp.arange(half, dtype=F32) / half)
    a_chunk = (jnp.arange(nc) * CHUNK - PAD).astype(F32)[:, None] * inv_freq[None, :]
    a_row = jnp.arange(CHUNK).astype(F32)[:, None] * inv_freq[None, :]
    return (jnp.stack([jnp.cos(a_chunk), jnp.sin(a_chunk)], axis=1),
            jnp.stack([jnp.cos(a_row), jnp.sin(a_row)], axis=0))


RET_CPS = 4


def _rope_chunk(rc_ref, rr_ref, c):
    cc, sc = rc_ref[c, 0:1, :], rc_ref[c, 1:2, :]
    cr, sr = rr_ref[0], rr_ref[1]
    return cc * cr - sc * sr, sc * cr + cc * sr


def _rope_specs(order):
    half = RET_DK // 2
    return [pl.BlockSpec((RET_CPS, 2, half), lambda n: (order(n), 0, 0)),
            pl.BlockSpec((2, CHUNK, half), lambda n: (0, 0, 0))]


def _ret_specs(order):
    rows = RET_CPS * CHUNK
    return [pl.BlockSpec((rows, RET_QK), lambda n: (order(n), 0)),
            pl.BlockSpec((rows, RET_QK), lambda n: (order(n), 1)),
            pl.BlockSpec((rows, RET_V), lambda n: (order(n), 1)),
            pl.BlockSpec((rows, RET_V), lambda n: (order(n), 2))]


def _ret_const_specs():
    return [pl.BlockSpec((RET_HEADS, CHUNK, CHUNK), lambda n: (0, 0, 0)),
            pl.BlockSpec((RET_HEADS, CHUNK, RET_DK), lambda n: (0, 0, 0)),
            pl.BlockSpec((RET_HEADS, CHUNK, RET_DK), lambda n: (0, 0, 0)),
            pl.BlockSpec((1, RET_DV), lambda n: (0, 0))]


def _ret_fwd(proj, cos, sin, consts, gn_w, seq, ride=None):
    rows = proj.shape[0]
    nc = rows // CHUNK
    dmask, xi, zeta, gamma_c = consts

    def body(gam_ref, q_ref, k_ref, v_ref, g_ref, cos_ref, sin_ref, dm_ref, xi_ref, ze_ref, gn_ref,
             o_ref, y_ref, ss_ref, s_ref):
        n = pl.program_id(0)

        @pl.when(n == 0)
        def _():
            s_ref[...] = jnp.zeros_like(s_ref)

        gn = gn_ref[...]
        hs = range(RET_HEADS)
        qk_cols = [slice(h * RET_DK, (h + 1) * RET_DK) for h in hs]
        v_cols = [slice(h * RET_DV, (h + 1) * RET_DV) for h in hs]
        for c in range(RET_CPS):
            rs = slice(c * CHUNK, (c + 1) * CHUNK)
            cs, sn = _rope_chunk(cos_ref, sin_ref, c)
            kscale = _valid_rows((n * RET_CPS + c) * CHUNK, CHUNK, seq) * (RET_DK ** -0.5)
            qr_l = [_rope(q_ref[rs, col], cs, sn) for col in qk_cols]
            kr_l = [_rope(k_ref[rs, col], cs, sn) * kscale for col in qk_cols]
            v_l = [v_ref[rs, col] for col in v_cols]
            s_l = [s_ref[h] for h in hs]
            sc_l = [_nt(qr, kr) * dm_ref[h] for h, (qr, kr) in enumerate(zip(qr_l, kr_l))]
            o_l = [_nn(sc_l[h], v_l[h]) + _nn(qr_l[h] * xi_ref[h], s_l[h]) for h in hs]
            for h in hs:
                ss_ref[c, h] = s_l[h].astype(ss_ref.dtype)
                s_ref[h] = gam_ref[h] * s_l[h] + _tn(kr_l[h] * ze_ref[h], v_l[h])
                o_ref[rs, v_cols[h]] = o_l[h]
                y_ref[rs, v_cols[h]] = _gated_norm(o_l[h], g_ref[rs, v_cols[h]], gn).astype(y_ref.dtype)

    fwd = lambda n: n
    row_v = pl.BlockSpec((RET_CPS * CHUNK, RET_V), lambda n: (n, 0))
    res, rode = _pcall(
        body, [gamma_c, proj, proj, proj, proj, cos, sin, dmask, xi, zeta, gn_w.reshape(1, RET_DV)],
        grid=(nc // RET_CPS,),
        in_specs=[pl.BlockSpec(memory_space=pltpu.SMEM)] + _ret_specs(fwd) + _rope_specs(fwd)
        + _ret_const_specs(),
        out_specs=[row_v, row_v,
                   pl.BlockSpec((RET_CPS, RET_HEADS, RET_DK, RET_DV), lambda n: (n, 0, 0, 0))],
        out_shape=[jax.ShapeDtypeStruct((rows, RET_V), F32), jax.ShapeDtypeStruct((rows, RET_V), BF16),
                   jax.ShapeDtypeStruct((nc, RET_HEADS, RET_DK, RET_DV), BF16)],
        scratch=[pltpu.VMEM((RET_HEADS, RET_DK, RET_DV), F32)], name="ret_fwd", sem=("arbitrary",), ride=ride)
    return res if ride is None else (res, rode)


def _ret_bwd(proj, o, dy, states, cos, sin, consts, gn_w, seq, ride=None):
    rows = proj.shape[0]
    nc = rows // CHUNK
    dmask, xi, zeta, gamma_c = consts

    def body(gam_ref, q_ref, k_ref, v_ref, g_ref, o_ref, dy_ref, ss_ref, cos_ref, sin_ref,
             dm_ref, xi_ref, ze_ref, gn_ref, dp_ref, dgn_ref, ds_ref):
        n = pl.program_id(0)

        @pl.when(n == 0)
        def _():
            ds_ref[...] = jnp.zeros_like(ds_ref)
            dgn_ref[...] = jnp.zeros_like(dgn_ref)

        gn = gn_ref[...]
        dgn = jnp.zeros((1, RET_DV), F32)
        hs = range(RET_HEADS)
        qk_cols = [slice(h * RET_DK, (h + 1) * RET_DK) for h in hs]
        v_cols = [slice(h * RET_DV, (h + 1) * RET_DV) for h in hs]
        for c in reversed(range(RET_CPS)):
            rs = slice(c * CHUNK, (c + 1) * CHUNK)
            cs, sn = _rope_chunk(cos_ref, sin_ref, c)
            kscale = _valid_rows(((steps - 1 - n) * RET_CPS + c) * CHUNK, CHUNK, seq) * (RET_DK ** -0.5)
            qr_l = [_rope(q_ref[rs, col], cs, sn) for col in qk_cols]
            kr_l = [_rope(k_ref[rs, col], cs, sn) * kscale for col in qk_cols]
            v_l = [v_ref[rs, col] for col in v_cols]
            s_l = [ss_ref[c, h] for h in hs]
            ds_l = [ds_ref[h] for h in hs]
            sc_l = [_nt(qr_l[h], kr_l[h]) * dm_ref[h] for h in hs]
            gnb = [_gated_norm_bwd(dy_ref[rs, col], o_ref[rs, col], g_ref[rs, col], gn) for col in v_cols]
            do_l = [x[0] for x in gnb]
            dsc_l = [_nt(do_l[h], v_l[h]) * dm_ref[h] for h in hs]
            dv_l = [_tn(sc_l[h], do_l[h]) + _nn(kr_l[h] * ze_ref[h], ds_l[h]) for h in hs]
            dqr_l = [_nn(dsc_l[h], kr_l[h]) + _nt(do_l[h], s_l[h]) * xi_ref[h] for h in hs]
            dkr_l = [_tn(dsc_l[h], qr_l[h]) + _nt(v_l[h], ds_l[h]) * ze_ref[h] for h in hs]
            for h in hs:
                dgn = dgn + gnb[h][2]
                ds_ref[h] = gam_ref[h] * ds_l[h] + _tn(qr_l[h] * xi_ref[h], do_l[h])
                dp_ref[rs, qk_cols[h]] = _rope_bwd(dqr_l[h], cs, sn).astype(dp_ref.dtype)
                dp_ref[rs, RET_QK + h * RET_DK:RET_QK + (h + 1) * RET_DK] = (
                    _rope_bwd(dkr_l[h] * kscale, cs, sn).astype(dp_ref.dtype))
                dp_ref[rs, 2 * RET_QK + h * RET_DV:2 * RET_QK + (h + 1) * RET_DV] = dv_l[h].astype(dp_ref.dtype)
                dp_ref[rs, 2 * RET_QK + RET_V + h * RET_DV:2 * RET_QK + RET_V + (h + 1) * RET_DV] = (
                    gnb[h][1].astype(dp_ref.dtype))
        dgn_ref[...] += dgn

    steps = nc // RET_CPS
    rev = lambda n: steps - 1 - n
    row_v = pl.BlockSpec((RET_CPS * CHUNK, RET_V), lambda n: (rev(n), 0))
    res, rode = _pcall(
        body, [gamma_c, proj, proj, proj, proj, o, dy, states, cos, sin, dmask, xi, zeta,
               gn_w.reshape(1, RET_DV)],
        grid=(steps,),
        in_specs=[pl.BlockSpec(memory_space=pltpu.SMEM)] + _ret_specs(rev) + [
            row_v, row_v, pl.BlockSpec((RET_CPS, RET_HEADS, RET_DK, RET_DV), lambda n: (rev(n), 0, 0, 0))]
        + _rope_specs(rev) + _ret_const_specs(),
        out_specs=[pl.BlockSpec((RET_CPS * CHUNK, RET_IN), lambda n: (rev(n), 0)),
                   pl.BlockSpec((1, RET_DV), lambda n: (0, 0))],
        out_shape=[jax.ShapeDtypeStruct((rows, RET_IN), BF16), jax.ShapeDtypeStruct((1, RET_DV), F32)],
        scratch=[pltpu.VMEM((RET_HEADS, RET_DK, RET_DV), F32)], name="ret_bwd", sem=("arbitrary",), ride=ride)
    return res if ride is None else (res, rode)


GATE_COL = DN_CONV_CH // DN_V
BA_COL = (DN_CONV_CH + DN_V) // LANES
BETA_LANE, DECAY_LANE = 0, DN_HEADS
INV_SHIFT = 4
INV_SQUARINGS = INV_SHIFT - 1
assert CHUNK == 4 << INV_SHIFT


DN_CPS = 2


def _dn_in_specs(order, conv_saved=False):
    rows = DN_CPS * CHUNK
    return [pl.BlockSpec((rows, DN_CONV_CH), lambda n: (order(n), 0)),
            pl.BlockSpec((rows, DN_CONV_CH), lambda n: (order(n), 0)) if conv_saved else
            pl.BlockSpec((8, DN_CONV_CH), lambda n: (jnp.maximum(order(n) * (rows // 8) - 1, 0), 0)),
            pl.BlockSpec((rows, DN_V), lambda n: (order(n), GATE_COL)),
            pl.BlockSpec((rows, LANES), lambda n: (order(n), BA_COL)),
            pl.BlockSpec((CONV_K, 1, DN_CONV_CH), lambda n: (0, 0, 0)),
            pl.BlockSpec((1, LANES), lambda n: (0, 0)),
            pl.BlockSpec((1, LANES), lambda n: (0, 0)),
            pl.BlockSpec((1, DN_DV), lambda n: (0, 0))]


def _dn_front(c, seq, x, halo, ba, cw_ref, al_ref, dt_ref, yc=None):
    valid = _valid_rows(c * CHUNK, CHUNK, seq)
    xin = x * valid
    if yc is None:
        halo = halo * _valid_rows(c * CHUNK - 8, 8, seq)
        yc = xin * cw_ref[CONV_K - 1]
        for k in range(1, CONV_K):
            yc = yc + _shift_down(xin, halo, k) * cw_ref[CONV_K - 1 - k]
    sgc = jax.nn.sigmoid(yc)
    sig = jax.nn.sigmoid(ba)
    beta = sig * valid
    z = ba + dt_ref[...]
    eal = jnp.exp(al_ref[...])
    g = -eal * _softplus(z) * valid
    ri, ci = _iota((CHUNK, CHUNK), 0), _iota((CHUNK, CHUNK), 1)
    lower = (ri >= ci).astype(F32)
    upper = (ri <= ci).astype(F32)
    eye = (ri == ci).astype(F32)
    gam = _nn(lower, g, hi=True)
    gam_t = _tn(g, upper, hi=True)
    return dict(valid=valid, xin=xin, yc=yc, sgc=sgc, act=yc * sgc, sig=sig, beta=beta, z=z,
                eal=eal, g=g, gam=gam, gam_t=gam_t, ri=ri, ci=ci, upper=upper, eye=eye)


def _dn_head(f, h):
    act = f["act"]
    q_raw = act[:, h * DN_DK:(h + 1) * DN_DK]
    k_raw = act[:, DN_QK + h * DN_DK:DN_QK + (h + 1) * DN_DK]
    v = act[:, 2 * DN_QK + h * DN_DV:2 * DN_QK + (h + 1) * DN_DV]
    rq = lax.rsqrt(jnp.sum(q_raw * q_raw, axis=-1, keepdims=True) + RMS_EPS)
    rk = lax.rsqrt(jnp.sum(k_raw * k_raw, axis=-1, keepdims=True) + RMS_EPS)
    qh = q_raw * rq
    kn = k_raw * rk
    gam_c = _col(f["gam"], DECAY_LANE + h)
    gam_r = _row(f["gam_t"], DECAY_LANE + h)
    bc = _col(f["beta"], BETA_LANE + h)
    diff = gam_c - gam_r
    decay = jnp.where(f["ri"] >= f["ci"], jnp.exp(jnp.minimum(diff, 0.0)), 0.0)
    glast = jnp.sum(gam_r * (_iota((1, CHUNK), 1) == CHUNK - 1).astype(F32), axis=1, keepdims=True)
    return dict(rq=rq, rk=rk, qh=qh, qn=qh * (DN_DK ** -0.5), kn=kn, v=v, gam_c=gam_c, gam_r=gam_r,
                bc=bc, diff=diff, decay=decay, egam=jnp.exp(gam_c), glast=glast,
                eglast=jnp.exp(glast), ekd=jnp.exp(glast - gam_c))


def _dn_fwd(proj, conv_w, alog, dtb, norm_w, seq):
    rows = proj.shape[0]
    nc = rows // CHUNK

    def body(x_ref, halo_ref, gate_ref, ba_ref, cw_ref, al_ref, dt_ref, nw_ref,
             o_ref, y_ref, ss_ref, t_ref, yc_ref, s_ref):
        n = pl.program_id(0)

        @pl.when(n == 0)
        def _():
            s_ref[...] = jnp.zeros_like(s_ref)

        nw = nw_ref[...]
        pre = []
        for c in range(DN_CPS):
            rs = slice(c * CHUNK, (c + 1) * CHUNK)
            halo = halo_ref[...] if c == 0 else x_ref[c * CHUNK - 8:c * CHUNK, :]
            f = _dn_front(n * DN_CPS + c, seq, x_ref[rs, :], halo, ba_ref[rs, :], cw_ref, al_ref, dt_ref)
            yc_ref[rs, :] = f["yc"]
            ri, ci = f["ri"], f["ci"]
            eye = f["eye"]
            diag_m = (jnp.right_shift(ri, INV_SHIFT) == jnp.right_shift(ci, INV_SHIFT)).astype(F32)
            half_m = (jnp.right_shift(ri, INV_SHIFT + 1) == jnp.right_shift(ci, INV_SHIFT + 1)).astype(F32)
            heads = [_dn_head(f, h) for h in range(DN_HEADS)]
            a_all = [jnp.where(ri > ci, hd["bc"] * _nt(hd["kn"], hd["kn"]) * hd["decay"], 0.0) for hd in heads]
            b_all = [a * diag_m for a in a_all]
            t_all = [eye - b for b in b_all]
            for _ in range(INV_SQUARINGS):
                b_all = [_nn(b, b, hi=True) for b in b_all]
                t_all = [t + _nn(t, b, hi=True) for t, b in zip(t_all, b_all)]
            for off_m in (half_m - diag_m, 1.0 - half_m):
                x_all = [_nn(a * off_m, t, hi=True) for a, t in zip(a_all, t_all)]
                t_all = [t - _nn(t, x, hi=True) for t, x in zip(t_all, x_all)]
            u_all = [_nn(t, hd["v"] * hd["bc"], hi=True) for t, hd in zip(t_all, heads)]
            w_all = [_nn(t, hd["kn"] * (hd["bc"] * hd["egam"]), hi=True) for t, hd in zip(t_all, heads)]
            qk_all = [_nt(hd["qn"], hd["kn"]) * hd["decay"] for hd in heads]
            for h in range(DN_HEADS):
                t_ref[c, h] = t_all[h]
            pre.append((heads, u_all, w_all, qk_all))
        for c in range(DN_CPS):
            rs = slice(c * CHUNK, (c + 1) * CHUNK)
            heads, u_all, w_all, qk_all = pre[c]
            s_all = [s_ref[h] for h in range(DN_HEADS)]
            os_all = [_nn(hd["qn"] * hd["egam"], s) for hd, s in zip(heads, s_all)]
            vnew_all = [u - _nn(w, s) for u, w, s in zip(u_all, w_all, s_all)]
            o_all = [os + _nn(qk, vn) for os, qk, vn in zip(os_all, qk_all, vnew_all)]
            snew_all = [s * hd["eglast"] + _tn(hd["kn"] * hd["ekd"], vn)
                        for s, hd, vn in zip(s_all, heads, vnew_all)]
            for h in range(DN_HEADS):
                v_cols = slice(h * DN_DV, (h + 1) * DN_DV)
                ss_ref[c, h] = s_all[h]
                s_ref[h] = snew_all[h]
                o_ref[rs, v_cols] = o_all[h]
                y_ref[rs, v_cols] = _gated_norm(o_all[h], gate_ref[rs, v_cols], nw).astype(y_ref.dtype)

    fwd = lambda n: n
    row_v = pl.BlockSpec((DN_CPS * CHUNK, DN_V), lambda n: (n, 0))
    return pl.pallas_call(
        body, grid=(nc // DN_CPS,), in_specs=_dn_in_specs(fwd),
        out_specs=[row_v, row_v,
                   pl.BlockSpec((DN_CPS, DN_HEADS, DN_DK, DN_DV), lambda n: (n, 0, 0, 0)),
                   pl.BlockSpec((DN_CPS, DN_HEADS, CHUNK, CHUNK), lambda n: (n, 0, 0, 0)),
                   pl.BlockSpec((DN_CPS * CHUNK, DN_CONV_CH), lambda n: (n, 0))],
        out_shape=[jax.ShapeDtypeStruct((rows, DN_V), F32), jax.ShapeDtypeStruct((rows, DN_V), BF16),
                   jax.ShapeDtypeStruct((nc, DN_HEADS, DN_DK, DN_DV), F32),
                   jax.ShapeDtypeStruct((nc, DN_HEADS, CHUNK, CHUNK), F32),
                   jax.ShapeDtypeStruct((rows, DN_CONV_CH), F32)],
        scratch_shapes=[pltpu.VMEM((DN_HEADS, DN_DK, DN_DV), F32)],
        name="dn_fwd", compiler_params=_params("arbitrary"))(
            proj, proj, proj, proj, conv_w, alog, dtb, norm_w.reshape(1, DN_DV))


def _dn_bwd(proj, conv_out, o, dy, states, tinv, conv_w, alog, dtb, norm_w, seq):
    rows = proj.shape[0]
    nc = rows // CHUNK

    def body(x_ref, yc_ref, gate_ref, ba_ref, cw_ref, al_ref, dt_ref, nw_ref,
             o_ref, dy_ref, ss_ref, t_ref,
             dp_ref, dcw_ref, dal_ref, ddt_ref, dnw_ref, ds_ref, nxt_ref):
        n = pl.program_id(0)

        @pl.when(n == 0)
        def _():
            ds_ref[...] = jnp.zeros_like(ds_ref)
            nxt_ref[...] = jnp.zeros_like(nxt_ref)
            dcw_ref[...] = jnp.zeros_like(dcw_ref)
            dal_ref[...] = jnp.zeros_like(dal_ref)
            ddt_ref[...] = jnp.zeros_like(ddt_ref)
            dnw_ref[...] = jnp.zeros_like(dnw_ref)

        for c in reversed(range(DN_CPS)):
            rs = pl.ds(c * CHUNK, CHUNK)
            chunk((steps - 1 - n) * DN_CPS + c, x_ref.at[rs], yc_ref.at[rs], gate_ref.at[rs], ba_ref.at[rs],
                  cw_ref, al_ref, dt_ref, nw_ref, o_ref.at[rs], dy_ref.at[rs], ss_ref.at[c], t_ref.at[c],
                  dp_ref.at[rs], dcw_ref, dal_ref, ddt_ref, dnw_ref, ds_ref, nxt_ref)

    def chunk(ch, x_ref, yc_ref, gate_ref, ba_ref, cw_ref, al_ref, dt_ref, nw_ref,
              o_ref, dy_ref, ss_ref, t_ref,
              dp_ref, dcw_ref, dal_ref, ddt_ref, dnw_ref, ds_ref, nxt_ref):
        f = _dn_front(ch, seq, x_ref[...], None, ba_ref[...], cw_ref, al_ref, dt_ref, yc_ref[...])
        ri, ci = f["ri"], f["ci"]
        strict = (ri > ci).astype(F32)
        nw = nw_ref[...]
        lane128 = _iota((1, LANES), 1)
        row128 = _iota((LANES, 1), 0)
        dgam_col = jnp.zeros((CHUNK, LANES), F32)
        dgam_row = jnp.zeros((LANES, CHUNK), F32)
        dbeta = jnp.zeros((CHUNK, LANES), F32)
        dnw = jnp.zeros((1, DN_DV), F32)
        hs = range(DN_HEADS)
        heads = [_dn_head(f, h) for h in hs]
        cols = [slice(h * DN_DV, (h + 1) * DN_DV) for h in hs]
        t_l = [t_ref[h] for h in hs]
        s_l = [ss_ref[h] for h in hs]
        ds_l = [ds_ref[h] for h in hs]
        kk_l = [_nt(hd["kn"], hd["kn"]) for hd in heads]
        p_l = [_nt(hd["qn"], hd["kn"]) for hd in heads]
        rhsw_l = [hd["kn"] * (hd["bc"] * hd["egam"]) for hd in heads]
        u_l = [_nn(t, hd["v"] * hd["bc"], hi=True) for t, hd in zip(t_l, heads)]
        w_l = [_nn(t, r, hi=True) for t, r in zip(t_l, rhsw_l)]
        vnew_l = [u - _nn(w, s) for u, w, s in zip(u_l, w_l, s_l)]
        gnb = [_gated_norm_bwd(dy_ref[:, c], o_ref[:, c], gate_ref[:, c], nw) for c in cols]
        do_l = [x[0] for x in gnb]
        for h in hs:
            dp_ref[:, DN_CONV_CH + h * DN_DV:DN_CONV_CH + (h + 1) * DN_DV] = gnb[h][1].astype(dp_ref.dtype)
            dnw = dnw + gnb[h][2]
        qg_l = [hd["qn"] * hd["egam"] for hd in heads]
        kd_l = [hd["kn"] * hd["ekd"] for hd in heads]
        dvnew_l = [_tn(p * hd["decay"], do) + _nn(kd, ds)
                   for p, hd, do, kd, ds in zip(p_l, heads, do_l, kd_l, ds_l)]
        m_l = [_nt(do, vn) for do, vn in zip(do_l, vnew_l)]
        dqg_l = [_nt(do, s) for do, s in zip(do_l, s_l)]
        dkd_l = [_nt(vn, ds) for vn, ds in zip(vnew_l, ds_l)]
        for h in hs:
            ds_ref[h] = (ds_l[h] * heads[h]["eglast"] + _tn(qg_l[h], do_l[h]) - _tn(w_l[h], dvnew_l[h]))
        dw_l = [-_nt(dvn, s) for dvn, s in zip(dvnew_l, s_l)]
        dru_l = [_tn(t, dvn, hi=True) for t, dvn in zip(t_l, dvnew_l)]
        drw_l = [_tn(t, dw_, hi=True) for t, dw_ in zip(t_l, dw_l)]
        da_l = [-(_nt(dru, u) + _nt(drw, w)) * strict for dru, u, drw, w in zip(dru_l, u_l, drw_l, w_l)]
        dp_l = [m * hd["decay"] for m, hd in zip(m_l, heads)]
        dkk_l = [da * (hd["bc"] * hd["decay"]) for da, hd in zip(da_l, heads)]
        dqn_l = [dqg * hd["egam"] + _nn(dp, hd["kn"]) for dqg, hd, dp in zip(dqg_l, heads, dp_l)]
        dkn_l = [_tn(dp, hd["qn"]) + dkd * hd["ekd"] + drw * (hd["bc"] * hd["egam"])
                 + _nn(dkk, hd["kn"]) + _tn(dkk, hd["kn"])
                 for dp, hd, dkd, drw, dkk in zip(dp_l, heads, dkd_l, drw_l, dkk_l)]
        dq_parts, dk_parts, dv_parts = [], [], []
        for h in hs:
            hd = heads[h]
            kn, v, bc, egam, decay = hd["kn"], hd["v"], hd["bc"], hd["egam"], hd["decay"]
            t1 = jnp.sum(dkd_l[h] * kd_l[h], axis=1, keepdims=True)
            dglast = (jnp.sum(t1, axis=0, keepdims=True)
                      + jnp.sum(jnp.sum(ds_l[h] * s_l[h], axis=1, keepdims=True), axis=0, keepdims=True)
                      * hd["eglast"])
            e = (m_l[h] * p_l[h] + da_l[h] * (bc * kk_l[h])) * decay
            dgc = (jnp.sum(dqg_l[h] * qg_l[h], axis=1, keepdims=True) - t1
                   + jnp.sum(drw_l[h] * rhsw_l[h], axis=1, keepdims=True)
                   + jnp.sum(e, axis=1, keepdims=True)
                   + jnp.where(_iota((CHUNK, 1), 0) == CHUNK - 1, dglast, 0.0))
            dgr = -jnp.sum(e, axis=0, keepdims=True)
            dbc = (jnp.sum(dru_l[h] * v, axis=1, keepdims=True)
                   + jnp.sum(drw_l[h] * kn, axis=1, keepdims=True) * egam
                   + jnp.sum(da_l[h] * kk_l[h] * decay, axis=1, keepdims=True))
            dv_parts.append(dru_l[h] * bc)
            qh, dqn, dkn = hd["qh"], dqn_l[h], dkn_l[h]
            dq_parts.append(((DN_DK ** -0.5) * hd["rq"])
                            * (dqn - qh * jnp.sum(dqn * qh, axis=1, keepdims=True)))
            dk_parts.append(hd["rk"] * (dkn - kn * jnp.sum(dkn * kn, axis=1, keepdims=True)))
            dgam_col = dgam_col + dgc * (lane128 == DECAY_LANE + h).astype(F32)
            dbeta = dbeta + dbc * (lane128 == BETA_LANE + h).astype(F32)
            dgam_row = dgam_row + (row128 == DECAY_LANE + h).astype(F32) * dgr
        dnw_ref[...] += dnw
        dgam = dgam_col + _nt(f["eye"], dgam_row, hi=True)
        dg = _nn(f["upper"], dgam, hi=True)
        d_a = dg * (-f["eal"]) * jax.nn.sigmoid(f["z"]) * f["valid"]
        dal_ref[...] += jnp.sum(dg * f["g"], axis=0, keepdims=True)
        ddt_ref[...] += jnp.sum(d_a, axis=0, keepdims=True)
        d_b = dbeta * f["valid"] * f["sig"] * (1.0 - f["sig"])
        dp_ref[:, DN_CONV_CH + DN_V:DN_CONV_CH + DN_V + LANES] = (d_a + d_b).astype(dp_ref.dtype)
        dp_ref[:, DN_CONV_CH + DN_V + LANES:] = jnp.zeros((CHUNK, DN_IN_PAD - DN_IN_USED), dp_ref.dtype)
        dact = jnp.concatenate(dq_parts + dk_parts + dv_parts, axis=1)
        yc, sgc = f["yc"], f["sgc"]
        dyc = dact * (sgc * (1.0 + yc * (1.0 - sgc)))
        nxt = nxt_ref[...]
        ups = [dyc] + [_shift_up(dyc, nxt, j) for j in range(1, CONV_K)]
        dx = ups[0] * cw_ref[CONV_K - 1]
        for j in range(1, CONV_K):
            dx = dx + ups[j] * cw_ref[CONV_K - 1 - j]
        for j in range(CONV_K):
            dcw_ref[CONV_K - 1 - j] += jnp.sum(f["xin"] * ups[j], axis=0, keepdims=True)
        nxt_ref[...] = dyc[0:8]
        dp_ref[:, :DN_CONV_CH] = (dx * f["valid"]).astype(dp_ref.dtype)

    steps = nc // DN_CPS
    rev = lambda n: steps - 1 - n
    row_v = pl.BlockSpec((DN_CPS * CHUNK, DN_V), lambda n: (rev(n), 0))
    vec = pl.BlockSpec((1, LANES), lambda n: (0, 0))
    return pl.pallas_call(
        body, grid=(steps,),
        in_specs=_dn_in_specs(rev, conv_saved=True) + [
            row_v, row_v,
            pl.BlockSpec((DN_CPS, DN_HEADS, DN_DK, DN_DV), lambda n: (rev(n), 0, 0, 0)),
            pl.BlockSpec((DN_CPS, DN_HEADS, CHUNK, CHUNK), lambda n: (rev(n), 0, 0, 0))],
        out_specs=[pl.BlockSpec((DN_CPS * CHUNK, DN_IN_PAD), lambda n: (rev(n), 0)),
                   pl.BlockSpec((CONV_K, 1, DN_CONV_CH), lambda n: (0, 0, 0)), vec, vec,
                   pl.BlockSpec((1, DN_DV), lambda n: (0, 0))],
        out_shape=[jax.ShapeDtypeStruct((rows, DN_IN_PAD), BF16),
                   jax.ShapeDtypeStruct((CONV_K, 1, DN_CONV_CH), F32),
                   jax.ShapeDtypeStruct((1, LANES), F32), jax.ShapeDtypeStruct((1, LANES), F32),
                   jax.ShapeDtypeStruct((1, DN_DV), F32)],
        scratch_shapes=[pltpu.VMEM((DN_HEADS, DN_DK, DN_DV), F32), pltpu.VMEM((8, DN_CONV_CH), F32)],
        name="dn_bwd", compiler_params=_params("arbitrary"))(
            proj, conv_out, proj, proj, conv_w, alog, dtb, norm_w.reshape(1, DN_DV), o, dy, states, tinv)


def _train_step(x, tgt, wts, sh, idx):
    seq = x.shape[0]
    rows = -(-(seq + CHUNK) // ROW_ALIGN) * ROW_ALIGN
    wts = dict(wts)
    (h0, tgt_p), (got,) = _embed(x, tgt, wts["meta_tokens"].astype(F32), rows,
                                 ride=_Ride("gather", [sh["ret_w_in"]]))
    wts["ret_w_in"] = got.reshape(N_SHARD, D_MODEL, -1)
    cos, sin = _rope_tables(rows // CHUNK)
    consts = _ret_consts()
    conv_w = wts["dn_conv_w"].reshape(CONV_K, 1, DN_CONV_CH)
    lane_pad = LANES - 2 * DN_HEADS
    alog = jnp.pad(wts["dn_a_log"].reshape(1, DN_HEADS), ((0, 0), (DECAY_LANE, lane_pad)))
    dtb = jnp.pad(wts["dn_dt_bias"].reshape(1, DN_HEADS), ((0, 0), (DECAY_LANE, lane_pad)))
    g = {}

    hn0 = _rms_fwd(h0, wts["mix_norm_w"][0], "rms_mix0")
    proj0, got = _mm_cols(hn0, wts["ret_w_in"], "ret_in",
                          ride=_Ride("gather", [sh["ret_w_out"], sh["ffn_w_gate"]]))
    wts["ret_w_out"] = got[0].reshape(-1, D_MODEL)
    wts["ffn_w_gate"] = got[1]
    (o0, y0, st0), got = _ret_fwd(proj0, cos, sin, consts, wts["ret_gn_w"], seq,
                                  ride=_Ride("gather", [sh["ffn_w_up"], sh["ffn_w_down"]]))
    wts["ffn_w_up"], wts["ffn_w_down"] = got
    h1 = _mm(y0, wts["ret_w_out"], mode="nn", name="ret_out", resid=h0)
    hn1 = _rms_fwd(h1, wts["ffn_norm_w"][0], "rms_ffn0")
    (g0, u0, act0), got = _ffn_up(hn1, wts["ffn_w_gate"], wts["ffn_w_up"], 0, "ffn_up0",
                                  ride=_Ride("gather", [sh["dn_w_in"], sh["dn_w_out"]]))
    n_dn = sh["dn_w_in"].shape[-1]
    dn_shards = got[0].reshape(N_SHARD, D_MODEL, n_dn)
    wts["dn_w_in"] = jnp.concatenate(
        [dn_shards[j] for j in range(N_SHARD)]
        + [jnp.zeros((D_MODEL, DN_IN_PAD - N_SHARD * n_dn), dn_shards.dtype)], axis=-1)
    wts["dn_w_out"] = got[1].reshape(-1, D_MODEL)
    h2 = _ffn_down(act0, wts["ffn_w_down"], h1, 0, "ffn_down0")
    hn2 = _rms_fwd(h2, wts["mix_norm_w"][1], "rms_mix1")
    proj1 = _mm(hn2, wts["dn_w_in"], mode="nn", name="dn_in")
    o1, y1, st1, tinv, conv1 = _dn_fwd(proj1, conv_w, alog, dtb, wts["dn_norm_w"], seq)
    h3 = _mm(y1, wts["dn_w_out"], mode="nn", name="dn_out", resid=h2)
    hn3 = _rms_fwd(h3, wts["ffn_norm_w"][1], "rms_ffn1")
    g1, u1, act1 = _ffn_up(hn3, wts["ffn_w_gate"], wts["ffn_w_up"], 1, "ffn_up1")
    h4 = _ffn_down(act1, wts["ffn_w_down"], h3, 1, "ffn_down1")

    dh4, g["final_norm_w"], loss, dh4b = _final_loss(h4, wts["final_norm_w"], tgt_p, seq, "final_loss")

    layers = wts["ffn_w_gate"].shape[1]

    ffn_names = ["ffn_w_down", "ffn_w_gate", "ffn_w_up"]

    def ffn_bwd(dh_out, dhb_out, h_mid, hn, gg, uu, act, layer, prev, ride=None, last=False):
        tag = str(layer)
        res = _ffn_down_bwd(dhb_out, wts["ffn_w_down"], gg, uu, layer, "ffn_down_bwd" + tag, ride=ride)
        (dg, du), rode = res if ride is not None else (res, None)
        d_down = _ffn_wgrad(act, [dhb_out], layer, layers, prev and prev[:1], True, "ffn_dwd" + tag)
        d_gu = _ffn_wgrad(hn, [dg, du], layer, layers, prev and prev[1:], False, "ffn_dwgu" + tag)
        grads = list(d_down) + list(d_gu)
        gs = rs_grads(ffn_names, grads) if last else None
        res = _ffn_up_bwd(dg, du, wts["ffn_w_gate"], wts["ffn_w_up"], layer, h_mid, wts["ffn_norm_w"][layer],
                          dh_out, "ffn_up_bwd" + tag, ride=_Ride("pair", gs) if last else None)
        (dh_mid, d_norm, dhb_mid), sib = res if last else (res, None)
        return dh_mid, dhb_mid, grads, d_norm, rode, gs, sib

    red = {}

    def rs_grads(names, grads):
        return [gr.reshape((N_SHARD,) + sh[n].shape) for n, gr in zip(names, grads)]

    def rs_partials(names, gs, sib):
        return [_rs_pair_add(gs[t], sib[t], idx, "rs_pair_add_" + n) for t, n in enumerate(names)]

    def rs_end(names, gs, sib, others, tag):
        mine = [_rs_final_add(gs[t], sib[t], others[t], idx, "rs_final_add_" + n) for t, n in enumerate(names)]
        red.update(zip(names, _rs_share(mine, "rs_share" + tag)))

    dh3, dh3b, ffn_grads, dfn1 = ffn_bwd(dh4, dh4b, h3, hn3, g1, u1, act1, 1, None)[:4]
    dy1 = _mm(dh3b, wts["dn_w_out"], mode="nt", name="dn_out_bwd")
    d_dn_out = _mm(y1, dh3b, mode="tn", name="dn_dwo")
    dproj1, dcw, dal, ddt, g["dn_norm_w"] = _dn_bwd(proj1, conv1, o1, dy1, st1, tinv, conv_w, alog, dtb,
                                                    wts["dn_norm_w"], seq)
    d_dn_in = _mm(hn2, dproj1, mode="tn", name="dn_dwi")
    d_dn_in = jnp.stack([d_dn_in[:, j * n_dn:(j + 1) * n_dn] for j in range(N_SHARD)])
    group1 = ["dn_w_out", "dn_w_in"]
    gs1 = rs_grads(group1, [d_dn_out, d_dn_in])
    (dh2, dmn1, dh2b), sib1 = _mm_nt_rms(dproj1, wts["dn_w_in"], h2, wts["mix_norm_w"][1], dh3, "dn_in_bwd",
                                         ride=_Ride("pair", gs1))
    g["dn_conv_w"] = dcw.reshape(CONV_K, DN_CONV_CH)
    g["dn_a_log"] = dal[0, DECAY_LANE:DECAY_LANE + DN_HEADS]
    g["dn_dt_bias"] = ddt[0, DECAY_LANE:DECAY_LANE + DN_HEADS]

    dh1, dh1b, _, dfn0, others1, gs2, sib2 = ffn_bwd(
        dh2, dh2b, h1, hn1, g0, u0, act0, 0, ffn_grads,
        ride=_Ride("chips", rs_partials(group1, gs1, sib1)), last=True)
    rs_end(group1, gs1, sib1, others1, "1")
    d_ret_out = _mm(y0, dh1b, mode="tn", name="ret_dwo")
    gs2b = rs_grads(["ret_w_out"], [d_ret_out])
    dy0, sib2b = _mm(dh1b, wts["ret_w_out"], mode="nt", name="ret_out_bwd", ride=_Ride("pair", gs2b))
    group2 = ffn_names + ["ret_w_out"]
    gs2, sib2 = gs2 + gs2b, list(sib2) + list(sib2b)
    (dproj0, g["ret_gn_w"]), others2 = _ret_bwd(proj0, o0, dy0, st0, cos, sin, consts, wts["ret_gn_w"], seq,
                                                ride=_Ride("chips", rs_partials(group2, gs2, sib2)))
    rs_end(group2, gs2, sib2, others2, "2")
    d_ret_in = _mm_cols_grad(hn0, dproj0, "ret_dwi")
    gs3 = rs_grads(["ret_w_in"], [d_ret_in])
    sib3 = _rs_pair(gs3, "rs_pair3")
    (dh0, dmn0, _), others3 = _mm_cols_t_rms(dproj0, wts["ret_w_in"], h0, wts["mix_norm_w"][0], dh1, "ret_in_bwd",
                                             ride=_Ride("chips", rs_partials(["ret_w_in"], gs3, sib3)))
    rs_end(["ret_w_in"], gs3, sib3, others3, "3")

    g["ffn_norm_w"] = jnp.concatenate([dfn0, dfn1], axis=0)
    g["mix_norm_w"] = jnp.concatenate([dmn0, dmn1], axis=0)
    g["meta_tokens"] = dh0[PAD:CHUNK]
    g["final_norm_w"] = g["final_norm_w"].reshape(D_MODEL)
    g["ret_gn_w"] = g["ret_gn_w"].reshape(RET_DV)
    g["dn_norm_w"] = g["dn_norm_w"].reshape(DN_DV)
    return loss, dh0, g, red


def _mesh_pos():
    return lax.axis_index("x"), lax.axis_index("y"), lax.axis_index("c")


def _other_chips(x, y):
    return [(1 - x, y), (x, 1 - y), (1 - x, 1 - y)]


def _remote(src, dst, send_sem, recv_sem, to):
    return pltpu.make_async_remote_copy(src_ref=src, dst_ref=dst, send_sem=send_sem, recv_sem=recv_sem,
                                        device_id=to, device_id_type=MESH)


GATHER_COPIES = 7


def _gather_phase(phase, ins, outs, send_sems, recv_sems):
    x, y, c = _mesh_pos()
    me = 2 * x + y
    chips = _other_chips(x, y)
    sibling = (x, y, 1 - c)

    def cp(t, k, src, dst, to):
        i = GATHER_COPIES * t + k
        return _remote(src, dst, send_sems.at[i], recv_sems.at[i], to)

    for t in range(len(ins)):
        own = cp(t, 0, ins[t], outs[t].at[me], sibling)
        if phase == 0:
            own.start()
        if phase == 2:
            own.wait()
        for k, (px, py) in enumerate(chips):
            landed = outs[t].at[2 * px + py, c]
            theirs = outs[t].at[2 * px + py, 1 - c]
            to_chip = cp(t, 1 + k, ins[t].at[c], outs[t].at[me, c], (px, py, c))
            if phase == 0:
                to_chip.start()
            if phase == 1:
                cp(t, 1 + k, ins[t].at[c], landed, (px, py, c)).wait_recv()
                cp(t, 4 + k, landed, landed, sibling).start()
            if phase == 2:
                to_chip.wait_send()
                cp(t, 4 + k, landed, landed, sibling).wait_send()
                cp(t, 4 + k, theirs, theirs, sibling).wait_recv()


def _chips_phase(phase, ins, outs, send_sems, recv_sems):
    x, y, c = _mesh_pos()
    for t in range(len(ins)):
        for k, (px, py) in enumerate(_other_chips(x, y)):
            cp = _remote(ins[t].at[2 * px + py], outs[t].at[k], send_sems.at[3 * t + k], recv_sems.at[3 * t + k],
                         (px, py, c))
            if phase == 0:
                cp.start()
            if phase == 2:
                cp.wait()


class _Ride:
    def __init__(self, kind, arrays):
        self.kind, self.arrays = kind, list(arrays)
        nt = len(self.arrays)
        if kind == "gather":
            self.phase_fn, n_sem = _gather_phase, GATHER_COPIES * nt
            self.out_shape = [jax.ShapeDtypeStruct((N_SHARD,) + a.shape, a.dtype) for a in self.arrays]
        elif kind == "pair":
            self.phase_fn, n_sem = _pair_phase, nt
            self.out_shape = [jax.ShapeDtypeStruct(a.shape[:1] + a.shape[2:], a.dtype) for a in self.arrays]
        else:
            self.phase_fn, n_sem = _chips_phase, 3 * nt
            self.out_shape = [jax.ShapeDtypeStruct((3,) + a.shape[1:], a.dtype) for a in self.arrays]
        self.in_specs, self.out_specs = [ANY] * nt, [ANY] * nt
        self.scratch = [pltpu.SemaphoreType.DMA((n_sem,)), pltpu.SemaphoreType.DMA((n_sem,))]

    def emit(self, step, nsteps, ins, outs, send_sems, recv_sems, before):
        mid = max(0, min((7 * nsteps) // 8, nsteps - 2))
        todo = [(0, 0), (1, mid)] if before else [(2, nsteps - 1)]
        for phase, at in todo:
            if phase == 1 and self.kind != "gather":
                continue

            @pl.when(step == at)
            def _(phase=phase):
                self.phase_fn(phase, ins, outs, send_sems, recv_sems)


def _gather_small(blk):
    r, wd = blk.shape

    def body(b_ref, out_ref, send_sems, recv_sems):
        x, y, c = _mesh_pos()
        chips = _other_chips(x, y)
        out_ref[2 * x + y] = b_ref[...]
        sends = [_remote(b_ref, out_ref.at[2 * x + y], send_sems.at[k], recv_sems.at[k], (px, py, c))
                 for k, (px, py) in enumerate(chips)]
        for cp in sends:
            cp.start()
        for k, (px, py) in enumerate(chips):
            _remote(b_ref, out_ref.at[2 * px + py], send_sems.at[k], recv_sems.at[k], (px, py, c)).wait_recv()
        for cp in sends:
            cp.wait_send()

    return pl.pallas_call(
        body, out_shape=jax.ShapeDtypeStruct((4, r, wd), blk.dtype), in_specs=[VMEM_SPEC], out_specs=VMEM_SPEC,
        scratch_shapes=[pltpu.SemaphoreType.DMA((3,)), pltpu.SemaphoreType.DMA((3,))],
        name="gather_small")(blk)


def _allreduce_small(blk):
    r, wd = blk.shape
    rels = [(dx, dy, dc) for dx in (0, 1) for dy in (0, 1) for dc in (0, 1) if dx or dy or dc]

    def body(b_ref, out_ref, buf_ref, send_sems, recv_sems):
        x, y, c = _mesh_pos()

        def peer(rel):
            dx, dy, dc = rel
            return (1 - x if dx else x, 1 - y if dy else y, 1 - c if dc else c)

        me = 4 * x + 2 * y + c
        buf_ref[me] = b_ref[...]
        sends = [_remote(b_ref, buf_ref.at[me], send_sems.at[k], recv_sems.at[k], peer(rel))
                 for k, rel in enumerate(rels)]
        for cp in sends:
            cp.start()
        for k, rel in enumerate(rels):
            px, py, pc = peer(rel)
            _remote(b_ref, buf_ref.at[4 * px + 2 * py + pc], send_sems.at[k], recv_sems.at[k],
                    (px, py, pc)).wait_recv()
        for cp in sends:
            cp.wait_send()
        acc = buf_ref[0]
        for d in range(1, 8):
            acc = acc + buf_ref[d]
        out_ref[...] = acc

    return pl.pallas_call(
        body, out_shape=jax.ShapeDtypeStruct((r, wd), blk.dtype), in_specs=[VMEM_SPEC], out_specs=VMEM_SPEC,
        scratch_shapes=[pltpu.VMEM((8, r, wd), blk.dtype), pltpu.SemaphoreType.DMA((7,)),
                        pltpu.SemaphoreType.DMA((7,))],
        name="allreduce_small")(blk)


def _rs_pair(gs, name):
    ride = _Ride("pair", gs)

    def body(*refs):
        nt = len(gs)
        for phase in (0, 2):
            _pair_phase(phase, refs[:nt], refs[nt:2 * nt], *refs[2 * nt:])

    return pl.pallas_call(body, out_shape=ride.out_shape, in_specs=ride.in_specs, out_specs=ride.out_specs,
                          scratch_shapes=ride.scratch, name=name)(*gs)


def _pair_phase(phase, ins, outs, send_sems, recv_sems):
    x, y, c = _mesh_pos()
    for t in range(len(ins)):
        cp = _remote(ins[t].at[:, 1 - c], outs[t], send_sems.at[t], recv_sems.at[t], (x, y, 1 - c))
        if phase == 0:
            cp.start()
        if phase == 2:
            cp.wait()


def _rs_tile(a, b):
    return _div_tile(a, 512 if b <= 1024 else 256, 16)


def _rs_pair_add(g, a, idx, name):
    _, _, rows, cols = g.shape
    tr = _rs_tile(rows, cols)

    def body(s_ref, g_ref, a_ref, p_ref):
        p_ref[...] = (g_ref[...] + a_ref[...]).astype(p_ref.dtype)

    blk = pl.BlockSpec((None, tr, cols), lambda j, i, s: (j, i, 0))
    spec = pltpu.PrefetchScalarGridSpec(
        num_scalar_prefetch=1, grid=(N_SHARD, rows // tr),
        in_specs=[pl.BlockSpec((None, None, tr, cols), lambda j, i, s: (j, s[0], i, 0)), blk], out_specs=blk)
    return pl.pallas_call(
        body, grid_spec=spec, out_shape=jax.ShapeDtypeStruct((N_SHARD, rows, cols), BF16), name=name,
        compiler_params=_params("parallel", "parallel"))(idx, g, a)


def _rs_final_add(g, a, b, idx, name):
    _, _, rows, cols = g.shape
    tr = _rs_tile(rows, cols)

    def body(s_ref, g_ref, a_ref, b0_ref, b1_ref, b2_ref, f_ref):
        own = g_ref[...] + a_ref[...]
        f_ref[...] = ((own + b0_ref[...].astype(F32)) + b1_ref[...].astype(F32)) + b2_ref[...].astype(F32)

    def b_spec(k):
        return pl.BlockSpec((None, tr, cols), lambda i, s: (k, i, 0))

    spec = pltpu.PrefetchScalarGridSpec(
        num_scalar_prefetch=1, grid=(rows // tr,),
        in_specs=[pl.BlockSpec((None, None, tr, cols), lambda i, s: (s[1], s[0], i, 0)),
                  pl.BlockSpec((None, tr, cols), lambda i, s: (s[1], i, 0)), b_spec(0), b_spec(1), b_spec(2)],
        out_specs=pl.BlockSpec((None, tr, cols), lambda i, s: (s[0], i, 0)))
    return pl.pallas_call(
        body, grid_spec=spec, out_shape=jax.ShapeDtypeStruct((2, rows, cols), F32), name=name,
        compiler_params=_params("parallel"))(idx, g, a, b, b, b)


def _rs_share(fs, name):
    nt = len(fs)

    def body(*refs):
        outs = refs[nt:2 * nt]
        send_sems, recv_sems = refs[2 * nt:]
        x, y, c = _mesh_pos()
        cps = [_remote(outs[t].at[c], outs[t].at[c], send_sems.at[t], recv_sems.at[t], (x, y, 1 - c))
               for t in range(nt)]
        for cp in cps:
            cp.start()
        for cp in cps:
            cp.wait()

    return pl.pallas_call(
        body, out_shape=[jax.ShapeDtypeStruct(f.shape, f.dtype) for f in fs],
        in_specs=[ANY] * nt, out_specs=[ANY] * nt, input_output_aliases={t: t for t in range(nt)},
        scratch_shapes=[pltpu.SemaphoreType.DMA((nt,)), pltpu.SemaphoreType.DMA((nt,))], name=name)(*fs)


def _adamw(w, g, m, v, name):
    lead, rows, cols = w.shape
    tr = rows // 4 if rows % 32 == 0 else rows

    def body(w_ref, g_ref, m_ref, v_ref, go_ref, d_ref, mo_ref, vo_ref):
        gv = g_ref[...]
        go_ref[...] = gv
        mn = ADAM_B1 * m_ref[...] + (1.0 - ADAM_B1) * gv
        vn = ADAM_B2 * v_ref[...] + (1.0 - ADAM_B2) * (gv * gv)
        m_hat = mn / (1.0 - ADAM_B1 ** ADAM_STEP)
        v_hat = vn / (1.0 - ADAM_B2 ** ADAM_STEP)
        d_ref[...] = -ADAM_LR * (m_hat / (jnp.sqrt(v_hat) + ADAM_EPS) + ADAM_WD * w_ref[...])
        mo_ref[...] = mn
        vo_ref[...] = vn

    blk = pl.BlockSpec((None, tr, cols), lambda l, i: (l, i, 0))
    out = jax.ShapeDtypeStruct((lead, rows, cols), F32)
    return pl.pallas_call(
        body, grid=(lead, rows // tr), in_specs=[blk] * 4, out_specs=[blk] * 4, out_shape=[out] * 4, name=name,
        compiler_params=_params("parallel", "parallel"))(w, g, m, v)


BIG = ["ret_w_in", "ret_w_out", "dn_w_in", "dn_w_out", "ffn_w_gate", "ffn_w_up", "ffn_w_down"]
TRANSPOSED_AT_BOUNDARY = {"dn_w_in": True, "ffn_w_gate": False, "ffn_w_up": False}
SMALL =["meta_tokens", "mix_norm_w", "ffn_norm_w", "ret_gn_w", "dn_conv_w", "dn_a_log", "dn_dt_bias",
         "dn_norm_w", "final_norm_w"]
SMALL_SHARDED = {"meta_tokens", "dn_conv_w", "dn_norm_w"}
ORDER = ["meta_tokens", "mix_norm_w", "ffn_norm_w", "ret_w_in", "ret_gn_w", "ret_w_out", "dn_w_in",
         "dn_conv_w", "dn_a_log", "dn_dt_bias", "dn_norm_w", "dn_w_out", "ffn_w_gate", "ffn_w_up",
         "ffn_w_down", "final_norm_w"]


def _halves(a):
    return a.reshape(2, -1, a.shape[-1])


def _pack_lanes(parts, align=8):
    flat = jnp.concatenate([p.reshape(-1) for p in parts])
    flat = jnp.pad(flat, (0, -flat.shape[0] % (align * LANES)))
    return flat.reshape(-1, LANES)


def _unpack(buf, shapes):
    lead = buf.shape[:-2]
    flat = buf.reshape(lead + (-1,))
    out, off = [], 0
    for shp in shapes:
        size = math.prod(shp)
        out.append(flat[..., off:off + size].reshape(lead + tuple(shp)))
        off += size
    return out


def _join_cols(shards):
    return jnp.concatenate([shards[j] for j in range(N_SHARD)], axis=-1)


def kernel(x, meta_tokens, mix_norm_w, ffn_norm_w, ret_w_in, ret_gn_w, ret_w_out, dn_w_in, dn_conv_w, dn_a_log, dn_dt_bias, dn_norm_w, dn_w_out, ffn_w_gate, ffn_w_up, ffn_w_down, final_norm_w, loss_target, m_meta_tokens, m_mix_norm_w, m_ffn_norm_w, m_ret_w_in, m_ret_gn_w, m_ret_w_out, m_dn_w_in, m_dn_conv_w, m_dn_a_log, m_dn_dt_bias, m_dn_norm_w, m_dn_w_out, m_ffn_w_gate, m_ffn_w_up, m_ffn_w_down, m_final_norm_w, v_meta_tokens, v_mix_norm_w, v_ffn_norm_w, v_ret_w_in, v_ret_gn_w, v_ret_w_out, v_dn_w_in, v_dn_conv_w, v_dn_a_log, v_dn_dt_bias, v_dn_norm_w, v_dn_w_out, v_ffn_w_gate, v_ffn_w_up, v_ffn_w_down, v_final_norm_w):
    w = dict(meta_tokens=meta_tokens, mix_norm_w=mix_norm_w, ffn_norm_w=ffn_norm_w, ret_w_in=ret_w_in,
             ret_gn_w=ret_gn_w, ret_w_out=ret_w_out, dn_w_in=dn_w_in, dn_conv_w=dn_conv_w, dn_a_log=dn_a_log,
             dn_dt_bias=dn_dt_bias, dn_norm_w=dn_norm_w, dn_w_out=dn_w_out, ffn_w_gate=ffn_w_gate,
             ffn_w_up=ffn_w_up, ffn_w_down=ffn_w_down, final_norm_w=final_norm_w)
    m = dict(meta_tokens=m_meta_tokens, mix_norm_w=m_mix_norm_w, ffn_norm_w=m_ffn_norm_w, ret_w_in=m_ret_w_in,
             ret_gn_w=m_ret_gn_w, ret_w_out=m_ret_w_out, dn_w_in=m_dn_w_in, dn_conv_w=m_dn_conv_w,
             dn_a_log=m_dn_a_log, dn_dt_bias=m_dn_dt_bias, dn_norm_w=m_dn_norm_w, dn_w_out=m_dn_w_out,
             ffn_w_gate=m_ffn_w_gate, ffn_w_up=m_ffn_w_up, ffn_w_down=m_ffn_w_down, final_norm_w=m_final_norm_w)
    v = dict(meta_tokens=v_meta_tokens, mix_norm_w=v_mix_norm_w, ffn_norm_w=v_ffn_norm_w, ret_w_in=v_ret_w_in,
             ret_gn_w=v_ret_gn_w, ret_w_out=v_ret_w_out, dn_w_in=v_dn_w_in, dn_conv_w=v_dn_conv_w,
             dn_a_log=v_dn_a_log, dn_dt_bias=v_dn_dt_bias, dn_norm_w=v_dn_norm_w, dn_w_out=v_dn_w_out,
             ffn_w_gate=v_ffn_w_gate, ffn_w_up=v_ffn_w_up, ffn_w_down=v_ffn_w_down, final_norm_w=v_final_norm_w)
    mx, my, mc = _mesh_pos()
    chip = 2 * mx + my

    sm_names = [n for n in SMALL if n in SMALL_SHARDED]
    sm_gathered = _unpack(_gather_small(_pack_lanes([w[n] for n in sm_names])), [w[n].shape for n in sm_names])
    full = {n: _join_cols(sm_gathered[i]) for i, n in enumerate(sm_names)}
    wts = {
        "meta_tokens": full["meta_tokens"], "mix_norm_w": mix_norm_w, "ffn_norm_w": ffn_norm_w,
        "ret_gn_w": ret_gn_w[0], "final_norm_w": final_norm_w, "dn_conv_w": full["dn_conv_w"][0],
        "dn_a_log": dn_a_log[0], "dn_dt_bias": dn_dt_bias[0], "dn_norm_w": full["dn_norm_w"][0],
    }
    idx = jnp.stack([mc, chip]).astype(jnp.int32)
    shards = {n: _halves(w[n].astype(MXU_DTYPE)) for n in BIG}
    loss_part, dh0, g, reduced = _train_step(x[0], loss_target[0], wts, shards, idx)
    seq = x.shape[1]
    grad_x = dh0[CHUNK:CHUNK + seq].reshape(x.shape)
    gsh = {}

    small_full_shapes = [g[n].shape for n in SMALL] + [(1,)]
    red = _unpack(_allreduce_small(_pack_lanes([g[n] for n in SMALL] + [loss_part[0, :1]])), small_full_shapes)
    loss = red[-1][0]
    for i, n in enumerate(SMALL):
        gn = red[i]
        if n in SMALL_SHARDED:
            width = w[n].shape[-1]
            gn = lax.dynamic_slice_in_dim(gn, chip * width, width, axis=gn.ndim - 1)
        gsh[n] = gn.reshape(w[n].shape)

    delta, new_m, new_v = {}, {}, {}
    for n in BIG:
        shp = w[n].shape
        if n in TRANSPOSED_AT_BOUNDARY and TRANSPOSED_AT_BOUNDARY[n]:
            view = lambda a: jnp.swapaxes(a, 1, 2).reshape(1, -1, LANES)
            back = lambda a: jnp.swapaxes(a.reshape(shp[0], shp[2], shp[1]), 1, 2)
        elif n in TRANSPOSED_AT_BOUNDARY:
            view = back = lambda a: jnp.swapaxes(a, 1, 2)
        else:
            view = back = lambda a: a
        res = _adamw(view(w[n]), view(reduced[n].reshape(shp)), view(m[n]), view(v[n]), "adamw_" + n)
        gsh[n], delta[n], new_m[n], new_v[n] = [back(r) for r in res]
    sm_local_shapes = [w[n].shape for n in SMALL]
    _, d_, m_, v_ = _adamw(*[_pack_lanes([t[n] for n in SMALL])[None] for t in (w, gsh, m, v)], "adamw_small")
    d_, m_, v_ = d_[0], m_[0], v_[0]
    for n, dd, mm, vv in zip(SMALL, _unpack(d_, sm_local_shapes), _unpack(m_, sm_local_shapes),
                             _unpack(v_, sm_local_shapes)):
        delta[n], new_m[n], new_v[n] = dd, mm, vv

    return (loss, grad_x, *[gsh[n] for n in ORDER], *[delta[n] for n in ORDER],
            *[new_m[n] for n in ORDER], *[new_v[n] for n in ORDER])
```

```python
import math

import jax
import jax.numpy as jnp
from jax import lax
from jax.experimental import pallas as pl
from jax.experimental.pallas import tpu as pltpu

F32 = jnp.float32
BF16 = jnp.bfloat16
MXU_DTYPE = BF16

D_MODEL = 1024
N_META = 16
CHUNK = 64
PAD = CHUNK - N_META
RMS_EPS = 1e-6
RET_HEADS, RET_DK, RET_DV = 4, 256, 512
RET_QK, RET_V = RET_HEADS * RET_DK, RET_HEADS * RET_DV
RET_IN = 2 * RET_QK + 2 * RET_V
ROPE_BASE = 10000.0
DN_HEADS, DN_DK, DN_DV = 8, 128, 256
DN_QK, DN_V = DN_HEADS * DN_DK, DN_HEADS * DN_DV
DN_CONV_CH = 2 * DN_QK + DN_V
DN_IN = DN_CONV_CH + DN_V + 2 * DN_HEADS
LANES = 128
DN_IN_USED = DN_CONV_CH + DN_V + LANES
DN_IN_PAD = DN_IN_USED + LANES
CONV_K = 4
FFN_HIDDEN = 2816
ADAM_LR, ADAM_B1, ADAM_B2, ADAM_EPS, ADAM_WD, ADAM_STEP = 0.001, 0.9, 0.999, 1e-08, 0.01, 10

ROW_ALIGN = 256
VMEM_LIMIT = 62 * 1024 * 1024
MESH = pl.DeviceIdType.MESH
ANY = pl.BlockSpec(memory_space=pl.ANY)
VMEM_SPEC = pl.BlockSpec(memory_space=pltpu.VMEM)
_HI = lax.Precision.HIGHEST


def _params(*sem):
    return pltpu.CompilerParams(dimension_semantics=sem, vmem_limit_bytes=VMEM_LIMIT)


def _dg(a, b, ca, cb, hi):
    dims = (((ca,), (cb,)), ((), ()))

    def dot(p, q):
        return lax.dot_general(p, q, dims, preferred_element_type=F32)

    if not hi:
        return dot(a.astype(MXU_DTYPE), b.astype(MXU_DTYPE))
    if MXU_DTYPE == F32:
        return lax.dot_general(a, b, dims, precision=_HI, preferred_element_type=F32)
    a_hi, b_hi = a.astype(MXU_DTYPE), b.astype(MXU_DTYPE)
    a_lo = (a - a_hi.astype(F32)).astype(MXU_DTYPE)
    b_lo = (b - b_hi.astype(F32)).astype(MXU_DTYPE)
    return dot(a_hi, b_hi) + (dot(a_hi, b_lo) + dot(a_lo, b_hi))


def _nn(a, b, hi=False):
    return _dg(a, b, 1, 0, hi)


def _nt(a, b, hi=False):
    return _dg(a, b, 1, 1, hi)


def _tn(a, b, hi=False):
    return _dg(a, b, 0, 0, hi)


def _iota(shape, dim):
    return lax.broadcasted_iota(jnp.int32, shape, dim)


def _valid_rows(first_row, rows, seq):
    r = first_row + _iota((rows, 1), 0)
    return ((r >= PAD) & (r < CHUNK + seq)).astype(F32)


def _rope(t, cs, sn):
    half = t.shape[-1] // 2
    t1, t2 = t[:, :half], t[:, half:]
    return jnp.concatenate([t1 * cs - t2 * sn, t1 * sn + t2 * cs], axis=1)


def _rope_bwd(d, cs, sn):
    half = d.shape[-1] // 2
    d1, d2 = d[:, :half], d[:, half:]
    return jnp.concatenate([d1 * cs + d2 * sn, d2 * cs - d1 * sn], axis=1)


def _col(x, idx):
    oh = (_iota((1, x.shape[1]), 1) == idx).astype(F32)
    return jnp.sum(x * oh, axis=1, keepdims=True)


def _row(x, idx):
    oh = (_iota((x.shape[0], 1), 0) == idx).astype(F32)
    return jnp.sum(x * oh, axis=0, keepdims=True)


def _shift_down(x, halo8, k):
    xr = pltpu.roll(x, k, 0)
    hr = pltpu.roll(halo8, k, 0)
    first = jnp.where(_iota((8, 1), 0) < k, hr, xr[0:8])
    return jnp.concatenate([first, xr[8:]], axis=0)


def _shift_up(x, next8, j):
    rows = x.shape[0]
    xr = pltpu.roll(x, rows - j, 0)
    nr = pltpu.roll(next8, 8 - j, 0)
    last = jnp.where(_iota((8, 1), 0) >= 8 - j, nr, xr[rows - 8:])
    return jnp.concatenate([xr[:rows - 8], last], axis=0)


def _gated_norm(o, gate, w):
    r = lax.rsqrt(jnp.mean(o * o, axis=-1, keepdims=True) + RMS_EPS)
    return o * r * w * (gate * jax.nn.sigmoid(gate))


def _gated_norm_bwd(dy, o, gate, w):
    r = lax.rsqrt(jnp.mean(o * o, axis=-1, keepdims=True) + RMS_EPS)
    nrm = o * r
    sg = jax.nn.sigmoid(gate)
    sl = gate * sg
    dgate = dy * nrm * w * (sg * (1.0 + gate * (1.0 - sg)))
    dn = dy * w * sl
    dw = jnp.sum(dy * nrm * sl, axis=0, keepdims=True)
    do = r * (dn - nrm * jnp.mean(dn * nrm, axis=-1, keepdims=True))
    return do, dgate, dw


def _softplus(z):
    return jnp.maximum(z, 0.0) + jnp.log(1.0 + jnp.exp(-jnp.abs(z)))


def _row_tile(rows, cap=768):
    for t in (768, 512, 256, 128, 64, 32, 16, 8):
        if t <= cap and rows % t == 0:
            return t
    return rows


TILE_BUDGET = 60 * 1024 * 1024


def _fit_rows(rows, row_bytes, fixed_bytes, value_row_bytes):
    best = None
    for t in range(LANES, rows + 1, LANES):
        if rows % t == 0 and 2 * (row_bytes * t + fixed_bytes) + value_row_bytes * t <= TILE_BUDGET:
            best = t
    return best or _row_tile(rows, 256)


def _div_tile(n, cap, mult):
    best = None
    for t in range(mult, min(cap, n) + 1, mult):
        if n % t == 0:
            best = t
    return best or n


def _col_tile(cols, cap=1536):
    best = None
    for t in range(LANES, min(cap, cols) + 1, LANES):
        if cols % t == 0:
            best = t
    return best or cols


def _embed(x, tgt, meta, rows, ride=None):
    seq, d = x.shape
    n_tok = seq // CHUNK

    def body(xa_ref, xb_ref, ta_ref, tb_ref, m_ref, h_ref, tp_ref):
        i = pl.program_id(0)
        first = jnp.concatenate([jnp.zeros((PAD, d), F32), m_ref[...]], axis=0)
        for half, (x_ref, t_ref) in enumerate(((xa_ref, ta_ref), (xb_ref, tb_ref))):
            k = 2 * i + half
            tokens = (k >= 1) & (k <= n_tok)
            rs = slice(half * CHUNK, (half + 1) * CHUNK)
            h_ref[rs, :] = jnp.where(k == 0, first, jnp.where(tokens, x_ref[...], 0.0))
            tp_ref[rs, :] = jnp.where(tokens, t_ref[...], 0.0)

    def tok(half):
        return pl.BlockSpec((CHUNK, d), lambda i: (jnp.clip(2 * i + half - 1, 0, n_tok - 1), 0))

    out = pl.BlockSpec((2 * CHUNK, d), lambda i: (i, 0))
    res, rode = _pcall(body, [x, x, tgt, tgt, meta], grid=(rows // (2 * CHUNK),),
                       in_specs=[tok(0), tok(1), tok(0), tok(1), pl.BlockSpec((N_META, d), lambda i: (0, 0))],
                       out_specs=[out, out], out_shape=[jax.ShapeDtypeStruct((rows, d), F32)] * 2, name="embed",
                       sem=("parallel",), ride=ride)
    return res if ride is None else (res, rode)


def _rms_fwd(h, w, name, ride=None):
    rows, d = h.shape
    tm = _row_tile(rows)

    def body(h_ref, w_ref, o_ref):
        x = h_ref[...]
        r = lax.rsqrt(jnp.mean(x * x, axis=-1, keepdims=True) + RMS_EPS)
        o_ref[...] = (x * r * w_ref[...]).astype(o_ref.dtype)

    res, rode = _pcall(body, [h, w.reshape(1, d)], grid=(rows // tm,),
                       in_specs=[pl.BlockSpec((tm, d), lambda i: (i, 0)), pl.BlockSpec((1, d), lambda i: (0, 0))],
                       out_specs=[pl.BlockSpec((tm, d), lambda i: (i, 0))],
                       out_shape=[jax.ShapeDtypeStruct((rows, d), BF16)], name=name, sem=("parallel",), ride=ride)
    return res[0] if ride is None else (res[0], rode)


def _gmm_rms(name, grid, args, in_specs, row_spec, fn, h, w, resid, row_axis, red_axis=None, ride=None):
    m, d = h.shape
    n_in = len(args)
    vec = pl.BlockSpec((1, d), lambda *g: (0, 0))

    def body(*refs):
        ins = refs[:n_in]
        h_ref, w_ref, r_ref, dh_ref, dw_ref, dh16_ref = refs[n_in:]
        part = fn(*ins)
        row = pl.program_id(row_axis)

        def finish(dy):
            x = h_ref[...]
            r = lax.rsqrt(jnp.mean(x * x, axis=-1, keepdims=True) + RMS_EPS)
            xh = x * r
            dxh = dy * w_ref[...]
            dh = r_ref[...] + r * (dxh - xh * jnp.mean(dxh * xh, axis=-1, keepdims=True))
            dh_ref[...] = dh
            dh16_ref[...] = dh.astype(dh16_ref.dtype)
            dwp = jnp.sum(dy * xh, axis=0, keepdims=True)

            @pl.when(row == 0)
            def _():
                dw_ref[...] = dwp

            @pl.when(row > 0)
            def _():
                dw_ref[...] += dwp

        if red_axis is None:
            finish(part)
            return
        k = pl.program_id(red_axis)

        @pl.when(k == 0)
        def _():
            dh_ref[...] = part

        @pl.when(k > 0)
        def _():
            dh_ref[...] += part

        @pl.when(k == grid[red_axis] - 1)
        def _():
            finish(dh_ref[...])

    res, rode = _pcall(body, list(args) + [h, w.reshape(1, d), resid], grid=grid,
                       in_specs=list(in_specs) + [row_spec, vec, row_spec], out_specs=[row_spec, vec, row_spec],
                       out_shape=[jax.ShapeDtypeStruct((m, d), F32), jax.ShapeDtypeStruct((1, d), F32),
                                  jax.ShapeDtypeStruct((m, d), BF16)],
                       name=name, sem=("arbitrary",) * len(grid), ride=ride)
    return res if ride is None else (res, rode)


def _final_loss(h, w, tgt, seq, name):
    rows, d = h.shape
    tm = _row_tile(rows)

    def body(h_ref, w_ref, t_ref, dh_ref, dw_ref, loss_ref, dh16_ref):
        i = pl.program_id(0)
        r_idx = i * tm + _iota((tm, 1), 0)
        m = ((r_idx >= CHUNK) & (r_idx < CHUNK + seq)).astype(F32)
        x = h_ref[...]
        wv = w_ref[...]
        r = lax.rsqrt(jnp.mean(x * x, axis=-1, keepdims=True) + RMS_EPS)
        xh = x * r
        err = (xh * wv - t_ref[...]) * m
        lpart = 0.5 * jnp.sum(jnp.mean(err * err, axis=-1, keepdims=True), axis=0, keepdims=True)
        dyv = err * (1.0 / d)
        dxh = dyv * wv
        dh = r * (dxh - xh * jnp.mean(dxh * xh, axis=-1, keepdims=True))
        dh_ref[...] = dh
        dh16_ref[...] = dh.astype(dh16_ref.dtype)
        part = jnp.sum(dyv * xh, axis=0, keepdims=True)

        @pl.when(i == 0)
        def _():
            dw_ref[...] = part
            loss_ref[...] = jnp.broadcast_to(lpart, loss_ref.shape)

        @pl.when(i > 0)
        def _():
            dw_ref[...] += part
            loss_ref[...] += jnp.broadcast_to(lpart, loss_ref.shape)

    blk = pl.BlockSpec((tm, d), lambda i: (i, 0))
    vec = pl.BlockSpec((1, d), lambda i: (0, 0))
    return pl.pallas_call(
        body, grid=(rows // tm,), in_specs=[blk, vec, blk],
        out_specs=[blk, vec, pl.BlockSpec((1, LANES), lambda i: (0, 0)), blk],
        out_shape=[jax.ShapeDtypeStruct((rows, d), F32), jax.ShapeDtypeStruct((1, d), F32),
                   jax.ShapeDtypeStruct((1, LANES), F32), jax.ShapeDtypeStruct((rows, d), BF16)],
        name=name, compiler_params=_params("arbitrary"))(h, w.reshape(1, d), tgt)


def _isz(x):
    return jnp.dtype(x.dtype).itemsize


def _mm(a, b, *, mode, name, out_dtype=F32, resid=None, col_cap=1536, ride=None):
    if mode == "tn":
        m, k = a.shape
        n = b.shape[1]
        tn = _col_tile(n, col_cap)
        tm = _fit_rows(m, k * _isz(a) + tn * _isz(b), (3 * k * tn * 4) // 2, 2 * (k + tn))

        def body_tn(a_ref, b_ref, o_ref):
            i = pl.program_id(1)
            part = _tn(a_ref[...], b_ref[...])

            @pl.when(i == 0)
            def _():
                o_ref[...] = part

            @pl.when(i > 0)
            def _():
                o_ref[...] += part

        return pl.pallas_call(
            body_tn, grid=(n // tn, m // tm),
            in_specs=[pl.BlockSpec((tm, k), lambda j, i: (i, 0)),
                      pl.BlockSpec((tm, tn), lambda j, i: (i, j))],
            out_specs=pl.BlockSpec((k, tn), lambda j, i: (0, j)),
            out_shape=jax.ShapeDtypeStruct((k, n), F32), name=name,
            compiler_params=_params("parallel", "arbitrary"))(a, b)

    m, ka = a.shape
    n = b.shape[1] if mode == "nn" else b.shape[0]
    has_resid = resid is not None
    tn = _col_tile(n, col_cap)
    tm = _fit_rows(m, ka * _isz(a) + tn * (jnp.dtype(out_dtype).itemsize + (4 if has_resid else 0)),
                   ka * tn * _isz(b), 2 * ka + 8 * tn)

    def body(*refs):
        if has_resid:
            a_ref, b_ref, r_ref, o_ref = refs
        else:
            a_ref, b_ref, o_ref = refs
        acc = _nn(a_ref[...], b_ref[...]) if mode == "nn" else _nt(a_ref[...], b_ref[...])
        if has_resid:
            acc = acc + r_ref[...]
        o_ref[...] = acc.astype(o_ref.dtype)

    b_spec = (pl.BlockSpec((b.shape[0], tn), lambda j, i: (0, j)) if mode == "nn"
              else pl.BlockSpec((tn, b.shape[1]), lambda j, i: (j, 0)))
    o_spec = pl.BlockSpec((tm, tn), lambda j, i: (i, j))
    in_specs = [pl.BlockSpec((tm, ka), lambda j, i: (i, 0)), b_spec]
    args = [a, b]
    if has_resid:
        in_specs.append(o_spec)
        args.append(resid)
    res, rode = _pcall(body, args, grid=(n // tn, m // tm), in_specs=in_specs, out_specs=[o_spec],
                       out_shape=[jax.ShapeDtypeStruct((m, n), out_dtype)], name=name,
                       sem=("parallel", "parallel"), ride=ride)
    return res[0] if ride is None else (res[0], rode)


N_SHARD = 4


def _gmm(name, grid, args, in_specs, out_specs, out_shape, fn, red_axis=None, init_arg=None, aliases=None,
         ride=None):
    n_in = len(args)
    single = not isinstance(out_shape, (list, tuple))
    out_specs = [out_specs] if single else list(out_specs)
    out_shape = [out_shape] if single else list(out_shape)

    def body(*refs):
        _gmm_step(fn, refs[:n_in], refs[n_in:], red_axis, init_arg)

    sem = tuple("arbitrary" if ax == red_axis else "parallel" for ax in range(len(grid)))
    res, rode = _pcall(body, args, grid=grid, in_specs=in_specs, out_specs=out_specs, out_shape=out_shape,
                       name=name, sem=sem, aliases=aliases, ride=ride)
    ours = res[0] if single else res
    return ours if ride is None else (ours, rode)


def _gmm_step(fn, ins, outs, red_axis, init_arg):
    parts = fn(*ins)
    if red_axis is None:
        for o_ref, p in zip(outs, parts):
            o_ref[...] = p.astype(o_ref.dtype)
        return
    k = pl.program_id(red_axis)

    @pl.when(k == 0)
    def _():
        for idx, (o_ref, p) in enumerate(zip(outs, parts)):
            o_ref[...] = p + ins[init_arg][...] if (idx == 0 and init_arg is not None) else p

    @pl.when(k > 0)
    def _():
        for o_ref, p in zip(outs, parts):
            o_ref[...] += p


def _ride_body(ride, grid, n_in, n_out, n_scratch, body):
    n_rin, n_rout = len(ride.arrays), len(ride.out_shape)
    nsteps = math.prod(grid)

    def wrapped(*refs):
        ins = refs[:n_in]
        r_ins = refs[n_in:n_in + n_rin]
        o0 = n_in + n_rin
        outs = refs[o0:o0 + n_out]
        r_outs = refs[o0 + n_out:o0 + n_out + n_rout]
        s0 = o0 + n_out + n_rout
        scratch = refs[s0:s0 + n_scratch]
        send_sems, recv_sems = refs[-2:]
        step = pl.program_id(0)
        for ax in range(1, len(grid)):
            step = step * grid[ax] + pl.program_id(ax)
        ride.emit(step, nsteps, r_ins, r_outs, send_sems, recv_sems, before=True)
        body(*ins, *outs, *scratch)
        ride.emit(step, nsteps, r_ins, r_outs, send_sems, recv_sems, before=False)

    return wrapped


def _pcall(body, args, *, grid, in_specs, out_specs, out_shape, name, sem, scratch=(), aliases=None, ride=None):
    if ride is None:
        res = pl.pallas_call(body, grid=grid, in_specs=list(in_specs), out_specs=list(out_specs),
                             out_shape=list(out_shape), scratch_shapes=list(scratch), name=name,
                             input_output_aliases=aliases or {}, compiler_params=_params(*sem))(*args)
        return res, None
    n_in, n_out = len(args), len(out_shape)
    res = pl.pallas_call(
        _ride_body(ride, grid, n_in, n_out, len(scratch), body), grid=grid,
        in_specs=list(in_specs) + ride.in_specs, out_specs=list(out_specs) + ride.out_specs,
        out_shape=list(out_shape) + ride.out_shape, scratch_shapes=list(scratch) + ride.scratch, name=name,
        input_output_aliases=aliases or {},
        compiler_params=_params(*(("arbitrary",) * len(grid))))(*args, *ride.arrays)
    return res[:n_out], res[n_out:]


def _mm_cols(a, ws, name, ride=None):
    m, k = a.shape
    n = ws.shape[2]
    tm = _fit_rows(m, k * _isz(a) + n * 4, k * n * _isz(ws), 4 * n)
    return _gmm(name, (N_SHARD, m // tm), [a, ws],
                [pl.BlockSpec((tm, k), lambda j, i: (i, 0)), pl.BlockSpec((None, k, n), lambda j, i: (j, 0, 0))],
                pl.BlockSpec((tm, n), lambda j, i: (i, j)), jax.ShapeDtypeStruct((m, N_SHARD * n), F32),
                lambda a_ref, w_ref: (_nn(a_ref[...], w_ref[...]),), ride=ride)


def _mm_cols_t_rms(d, ws, h, w, resid, name, ride=None):
    m = d.shape[0]
    _, k, n = ws.shape
    tm = _fit_rows(m, n * _isz(d) + 3 * k * 4, k * n * _isz(ws), 16 * k)
    return _gmm_rms(name, (m // tm, N_SHARD), [d, ws],
                    [pl.BlockSpec((tm, n), lambda i, j: (i, j)), pl.BlockSpec((None, k, n), lambda i, j: (j, 0, 0))],
                    pl.BlockSpec((tm, k), lambda i, j: (i, 0)),
                    lambda d_ref, w_ref: _nt(d_ref[...], w_ref[...]), h, w, resid, 0, red_axis=1, ride=ride)


def _mm_nt_rms(a, b, h, w, resid, name, ride=None):
    m, n = a.shape
    k = b.shape[0]
    tm = _fit_rows(m, n * _isz(a) + 3 * k * 4, k * n * _isz(b), 16 * k)
    return _gmm_rms(name, (m // tm,), [a, b],
                    [pl.BlockSpec((tm, n), lambda i: (i, 0)), pl.BlockSpec((k, n), lambda i: (0, 0))],
                    pl.BlockSpec((tm, k), lambda i: (i, 0)),
                    lambda a_ref, b_ref: _nt(a_ref[...], b_ref[...]), h, w, resid, 0, ride=ride)


def _mm_cols_grad(a, d, name):
    m, k = a.shape
    n = d.shape[1] // N_SHARD
    tm = _fit_rows(m, k * _isz(a) + n * _isz(d), (3 * k * n * 4) // 2, 2 * (k + n))
    return _gmm(name, (N_SHARD, m // tm), [a, d],
                [pl.BlockSpec((tm, k), lambda j, i: (i, 0)), pl.BlockSpec((tm, n), lambda j, i: (i, j))],
                pl.BlockSpec((None, k, n), lambda j, i: (j, 0, 0)), jax.ShapeDtypeStruct((N_SHARD, k, n), F32),
                lambda a_ref, d_ref: (_tn(a_ref[...], d_ref[...]),), red_axis=1)


def _ffn_up(hn, wg, wu, layer, name, ride=None):
    m, k = hn.shape
    n = wg.shape[3]
    tm = _fit_rows(m, k * _isz(hn) + 3 * n * jnp.dtype(BF16).itemsize, 2 * k * n * _isz(wg), 16 * n)

    def fn(a_ref, wg_ref, wu_ref):
        a = a_ref[...]
        g = _nn(a, wg_ref[...])
        u = _nn(a, wu_ref[...])
        return g, u, g * jax.nn.sigmoid(g) * u

    w_spec = pl.BlockSpec((None, None, k, n), lambda j, i: (j, layer, 0, 0))
    o_spec = pl.BlockSpec((None, tm, n), lambda j, i: (j, i, 0))
    out = jax.ShapeDtypeStruct((N_SHARD, m, n), BF16)
    return _gmm(name, (N_SHARD, m // tm), [hn, wg, wu],
                [pl.BlockSpec((tm, k), lambda j, i: (i, 0)), w_spec, w_spec],
                [o_spec, o_spec, o_spec], [out, out, out], fn, ride=ride)


def _ffn_down(act, wd, resid, layer, name, ride=None):
    _, m, n = act.shape
    d = wd.shape[3]
    tm = _fit_rows(m, N_SHARD * n * _isz(act) + 2 * d * 4, N_SHARD * n * d * _isz(wd), 8 * d)

    def fn(a_ref, w_ref, r_ref):
        acc = r_ref[...]
        for j in range(N_SHARD):
            acc = acc + _nn(a_ref[j], w_ref[j])
        return (acc,)

    row = pl.BlockSpec((tm, d), lambda i: (i, 0))
    return _gmm(name, (m // tm,), [act, wd, resid],
                [pl.BlockSpec((N_SHARD, tm, n), lambda i: (0, i, 0)),
                 pl.BlockSpec((N_SHARD, None, n, d), lambda i: (0, layer, 0, 0)), row],
                row, jax.ShapeDtypeStruct((m, d), F32), fn, ride=ride)


def _ffn_down_bwd(dh, wd, g, u, layer, name, ride=None):
    m, d = dh.shape
    n = wd.shape[2]
    tm = _fit_rows(m, d * _isz(dh) + 4 * N_SHARD * n * jnp.dtype(BF16).itemsize, N_SHARD * n * d * _isz(wd),
                   2 * d + 24 * n)

    def body(dh_ref, wd_ref, g_ref, u_ref, dg_ref, du_ref):
        dhv = dh_ref[...].astype(MXU_DTYPE)
        for j in range(N_SHARD):
            dact = _nt(dhv, wd_ref[j])
            gv = g_ref[j].astype(F32)
            sg = jax.nn.sigmoid(gv)
            gs = gv * sg
            dg_ref[j] = (dact * u_ref[j].astype(F32) * (sg + gs * (1.0 - sg))).astype(dg_ref.dtype)
            du_ref[j] = (dact * gs).astype(du_ref.dtype)

    sh_spec = pl.BlockSpec((N_SHARD, tm, n), lambda i: (0, i, 0))
    out = jax.ShapeDtypeStruct((N_SHARD, m, n), BF16)
    res, rode = _pcall(body, [dh, wd, g, u], grid=(m // tm,),
                       in_specs=[pl.BlockSpec((tm, d), lambda i: (i, 0)),
                                 pl.BlockSpec((N_SHARD, None, n, d), lambda i: (0, layer, 0, 0)), sh_spec, sh_spec],
                       out_specs=[sh_spec, sh_spec], out_shape=[out, out], name=name, sem=("parallel",), ride=ride)
    return res if ride is None else (res, rode)


def _ffn_up_bwd(dg, du, wg, wu, layer, h, w, resid, name, ride=None):
    _, m, n = dg.shape
    k = wg.shape[2]
    tm = _fit_rows(m, 2 * N_SHARD * n * _isz(dg) + 3 * k * 4, 2 * N_SHARD * k * n * _isz(wg), 16 * k)

    def fn(dg_ref, du_ref, wg_ref, wu_ref):
        acc = _nt(dg_ref[0], wg_ref[0]) + _nt(du_ref[0], wu_ref[0])
        for j in range(1, N_SHARD):
            acc = acc + _nt(dg_ref[j], wg_ref[j]) + _nt(du_ref[j], wu_ref[j])
        return acc

    d_spec = pl.BlockSpec((N_SHARD, tm, n), lambda i: (0, i, 0))
    w_spec = pl.BlockSpec((N_SHARD, None, k, n), lambda i: (0, layer, 0, 0))
    return _gmm_rms(name, (m // tm,), [dg, du, wg, wu], [d_spec, d_spec, w_spec, w_spec],
                    pl.BlockSpec((tm, k), lambda i: (i, 0)), fn, h, w, resid, 0, ride=ride)


def _ffn_wgrad(lhs, rhs_list, layer, layers, prev, lhs_sharded, name):
    if lhs_sharded:
        _, m, k = lhs.shape
        n = rhs_list[0].shape[1]
    else:
        m, k = lhs.shape
        n = rhs_list[0].shape[2]
    n_out = len(rhs_list)
    tm = _fit_rows(m, k * _isz(lhs) + n_out * n * _isz(rhs_list[0]), (3 * n_out * k * n * 4) // 2,
                   2 * (k + n_out * n))
    sh = pl.BlockSpec((None, tm, k if lhs_sharded else n), lambda j, i: (j, i, 0))
    fl = pl.BlockSpec((tm, n if lhs_sharded else k), lambda j, i: (i, 0))
    n_out = len(rhs_list)
    args = [lhs] + list(rhs_list)
    in_specs = [sh if lhs_sharded else fl] + [fl if lhs_sharded else sh] * n_out
    aliases = None
    if prev is not None:
        aliases = {len(args) + t: t for t in range(n_out)}
        args = args + list(prev)
        in_specs = in_specs + [ANY] * n_out

    def fn(l_ref, *rest):
        lv = l_ref[...]
        return tuple(_tn(lv, r_ref[...]) for r_ref in rest[:n_out])

    o_spec = pl.BlockSpec((None, None, k, n), lambda j, i: (j, layer, 0, 0))
    out = jax.ShapeDtypeStruct((N_SHARD, layers, k, n), F32)
    return _gmm(name, (N_SHARD, m // tm), args, in_specs, [o_spec] * n_out, [out] * n_out, fn,
                red_axis=1, aliases=aliases)


def _ret_consts():
    log_gamma = jnp.log1p(-jnp.exp2(-5.0 - jnp.arange(RET_HEADS, dtype=F32)))
    idx = jnp.arange(CHUNK, dtype=F32)
    rel = idx[:, None] - idx[None, :]
    dmask = jnp.where((rel >= 0)[None], jnp.exp(log_gamma[:, None, None] * jnp.maximum(rel, 0.0)), 0.0)
    xi = jnp.exp(log_gamma[:, None] * (idx[None, :] + 1.0))[:, :, None]
    zeta = jnp.exp(log_gamma[:, None] * (CHUNK - 1.0 - idx[None, :]))[:, :, None]
    gamma_c = jnp.exp(log_gamma * CHUNK)
    wide = (RET_HEADS, CHUNK, RET_DK)
    return dmask, jnp.broadcast_to(xi, wide), jnp.broadcast_to(zeta, wide), gamma_c


def _rope_tables(nc):
    half = RET_DK // 2
    inv_freq = ROPE_BASE ** (-jnp.arange(half, dtype=F32) / half)
    a_chunk = (jnp.arange(nc) * CHUNK - PAD).astype(F32)[:, None] * inv_freq[None, :]
    a_row = jnp.arange(CHUNK).astype(F32)[:, None] * inv_freq[None, :]
    return (jnp.stack([jnp.cos(a_chunk), jnp.sin(a_chunk)], axis=1),
            jnp.stack([jnp.cos(a_row), jnp.sin(a_row)], axis=0))


RET_CPS = 4


def _rope_chunk(rc_ref, rr_ref, c):
    cc, sc = rc_ref[c, 0:1, :], rc_ref[c, 1:2, :]
    cr, sr = rr_ref[0], rr_ref[1]
    return cc * cr - sc * sr, sc * cr + cc * sr


def _rope_specs(order):
    half = RET_DK // 2
    return [pl.BlockSpec((RET_CPS, 2, half), lambda n: (order(n), 0, 0)),
            pl.BlockSpec((2, CHUNK, half), lambda n: (0, 0, 0))]


def _ret_specs(order):
    rows = RET_CPS * CHUNK
    return [pl.BlockSpec((rows, RET_QK), lambda n: (order(n), 0)),
            pl.BlockSpec((rows, RET_QK), lambda n: (order(n), 1)),
            pl.BlockSpec((rows, RET_V), lambda n: (order(n), 1)),
            pl.BlockSpec((rows, RET_V), lambda n: (order(n), 2))]


def _ret_const_specs():
    return [pl.BlockSpec((RET_HEADS, CHUNK, CHUNK), lambda n: (0, 0, 0)),
            pl.BlockSpec((RET_HEADS, CHUNK, RET_DK), lambda n: (0, 0, 0)),
            pl.BlockSpec((RET_HEADS, CHUNK, RET_DK), lambda n: (0, 0, 0)),
            pl.BlockSpec((1, RET_DV), lambda n: (0, 0))]


def _ret_fwd(proj, cos, sin, consts, gn_w, seq, ride=None):
    rows = proj.shape[0]
    nc = rows // CHUNK
    dmask, xi, zeta, gamma_c = consts

    def body(gam_ref, q_ref, k_ref, v_ref, g_ref, cos_ref, sin_ref, dm_ref, xi_ref, ze_ref, gn_ref,
             o_ref, y_ref, ss_ref, s_ref):
        n = pl.program_id(0)

        @pl.when(n == 0)
        def _():
            s_ref[...] = jnp.zeros_like(s_ref)

        gn = gn_ref[...]
        hs = range(RET_HEADS)
        qk_cols = [slice(h * RET_DK, (h + 1) * RET_DK) for h in hs]
        v_cols = [slice(h * RET_DV, (h + 1) * RET_DV) for h in hs]
        for c in range(RET_CPS):
            rs = slice(c * CHUNK, (c + 1) * CHUNK)
            cs, sn = _rope_chunk(cos_ref, sin_ref, c)
            kscale = _valid_rows((n * RET_CPS + c) * CHUNK, CHUNK, seq) * (RET_DK ** -0.5)
            qr_l = [_rope(q_ref[rs, col], cs, sn) for col in qk_cols]
            kr_l = [_rope(k_ref[rs, col], cs, sn) * kscale for col in qk_cols]
            v_l = [v_ref[rs, col] for col in v_cols]
            s_l = [s_ref[h] for h in hs]
            sc_l = [_nt(qr, kr) * dm_ref[h] for h, (qr, kr) in enumerate(zip(qr_l, kr_l))]
            o_l = [_nn(sc_l[h], v_l[h]) + _nn(qr_l[h] * xi_ref[h], s_l[h]) for h in hs]
            for h in hs:
                ss_ref[c, h] = s_l[h].astype(ss_ref.dtype)
                s_ref[h] = gam_ref[h] * s_l[h] + _tn(kr_l[h] * ze_ref[h], v_l[h])
                o_ref[rs, v_cols[h]] = o_l[h]
                y_ref[rs, v_cols[h]] = _gated_norm(o_l[h], g_ref[rs, v_cols[h]], gn).astype(y_ref.dtype)

    fwd = lambda n: n
    row_v = pl.BlockSpec((RET_CPS * CHUNK, RET_V), lambda n: (n, 0))
    res, rode = _pcall(
        body, [gamma_c, proj, proj, proj, proj, cos, sin, dmask, xi, zeta, gn_w.reshape(1, RET_DV)],
        grid=(nc // RET_CPS,),
        in_specs=[pl.BlockSpec(memory_space=pltpu.SMEM)] + _ret_specs(fwd) + _rope_specs(fwd)
        + _ret_const_specs(),
        out_specs=[row_v, row_v,
                   pl.BlockSpec((RET_CPS, RET_HEADS, RET_DK, RET_DV), lambda n: (n, 0, 0, 0))],
        out_shape=[jax.ShapeDtypeStruct((rows, RET_V), F32), jax.ShapeDtypeStruct((rows, RET_V), BF16),
                   jax.ShapeDtypeStruct((nc, RET_HEADS, RET_DK, RET_DV), BF16)],
        scratch=[pltpu.VMEM((RET_HEADS, RET_DK, RET_DV), F32)], name="ret_fwd", sem=("arbitrary",), ride=ride)
    return res if ride is None else (res, rode)


def _ret_bwd(proj, o, dy, states, cos, sin, consts, gn_w, seq, ride=None):
    rows = proj.shape[0]
    nc = rows // CHUNK
    dmask, xi, zeta, gamma_c = consts

    def body(gam_ref, q_ref, k_ref, v_ref, g_ref, o_ref, dy_ref, ss_ref, cos_ref, sin_ref,
             dm_ref, xi_ref, ze_ref, gn_ref, dp_ref, dgn_ref, ds_ref):
        n = pl.program_id(0)

        @pl.when(n == 0)
        def _():
            ds_ref[...] = jnp.zeros_like(ds_ref)
            dgn_ref[...] = jnp.zeros_like(dgn_ref)

        gn = gn_ref[...]
        dgn = jnp.zeros((1, RET_DV), F32)
        hs = range(RET_HEADS)
        qk_cols = [slice(h * RET_DK, (h + 1) * RET_DK) for h in hs]
        v_cols = [slice(h * RET_DV, (h + 1) * RET_DV) for h in hs]
        for c in reversed(range(RET_CPS)):
            rs = slice(c * CHUNK, (c + 1) * CHUNK)
            cs, sn = _rope_chunk(cos_ref, sin_ref, c)
            kscale = _valid_rows(((steps - 1 - n) * RET_CPS + c) * CHUNK, CHUNK, seq) * (RET_DK ** -0.5)
            qr_l = [_rope(q_ref[rs, col], cs, sn) for col in qk_cols]
            kr_l = [_rope(k_ref[rs, col], cs, sn) * kscale for col in qk_cols]
            v_l = [v_ref[rs, col] for col in v_cols]
            s_l = [ss_ref[c, h] for h in hs]
            ds_l = [ds_ref[h] for h in hs]
            sc_l = [_nt(qr_l[h], kr_l[h]) * dm_ref[h] for h in hs]
            gnb = [_gated_norm_bwd(dy_ref[rs, col], o_ref[rs, col], g_ref[rs, col], gn) for col in v_cols]
            do_l = [x[0] for x in gnb]
            dsc_l = [_nt(do_l[h], v_l[h]) * dm_ref[h] for h in hs]
            dv_l = [_tn(sc_l[h], do_l[h]) + _nn(kr_l[h] * ze_ref[h], ds_l[h]) for h in hs]
            dqr_l = [_nn(dsc_l[h], kr_l[h]) + _nt(do_l[h], s_l[h]) * xi_ref[h] for h in hs]
            dkr_l = [_tn(dsc_l[h], qr_l[h]) + _nt(v_l[h], ds_l[h]) * ze_ref[h] for h in hs]
            for h in hs:
                dgn = dgn + gnb[h][2]
                ds_ref[h] = gam_ref[h] * ds_l[h] + _tn(qr_l[h] * xi_ref[h], do_l[h])
                dp_ref[rs, qk_cols[h]] = _rope_bwd(dqr_l[h], cs, sn).astype(dp_ref.dtype)
                dp_ref[rs, RET_QK + h * RET_DK:RET_QK + (h + 1) * RET_DK] = (
                    _rope_bwd(dkr_l[h] * kscale, cs, sn).astype(dp_ref.dtype))
                dp_ref[rs, 2 * RET_QK + h * RET_DV:2 * RET_QK + (h + 1) * RET_DV] = dv_l[h].astype(dp_ref.dtype)
                dp_ref[rs, 2 * RET_QK + RET_V + h * RET_DV:2 * RET_QK + RET_V + (h + 1) * RET_DV] = (
                    gnb[h][1].astype(dp_ref.dtype))
        dgn_ref[...] += dgn

    steps = nc // RET_CPS
    rev = lambda n: steps - 1 - n
    row_v = pl.BlockSpec((RET_CPS * CHUNK, RET_V), lambda n: (rev(n), 0))
    res, rode = _pcall(
        body, [gamma_c, proj, proj, proj, proj, o, dy, states, cos, sin, dmask, xi, zeta,
               gn_w.reshape(1, RET_DV)],
        grid=(steps,),
        in_specs=[pl.BlockSpec(memory_space=pltpu.SMEM)] + _ret_specs(rev) + [
            row_v, row_v, pl.BlockSpec((RET_CPS, RET_HEADS, RET_DK, RET_DV), lambda n: (rev(n), 0, 0, 0))]
        + _rope_specs(rev) + _ret_const_specs(),
        out_specs=[pl.BlockSpec((RET_CPS * CHUNK, RET_IN), lambda n: (rev(n), 0)),
                   pl.BlockSpec((1, RET_DV), lambda n: (0, 0))],
        out_shape=[jax.ShapeDtypeStruct((rows, RET_IN), BF16), jax.ShapeDtypeStruct((1, RET_DV), F32)],
        scratch=[pltpu.VMEM((RET_HEADS, RET_DK, RET_DV), F32)], name="ret_bwd", sem=("arbitrary",), ride=ride)
    return res if ride is None else (res, rode)


GATE_COL = DN_CONV_CH // DN_V
BA_COL = (DN_CONV_CH + DN_V) // LANES
BETA_LANE, DECAY_LANE = 0, DN_HEADS
INV_SHIFT = 4
INV_SQUARINGS = INV_SHIFT - 1
assert CHUNK == 4 << INV_SHIFT


DN_CPS = 2


def _dn_in_specs(order, conv_saved=False):
    rows = DN_CPS * CHUNK
    return [pl.BlockSpec((rows, DN_CONV_CH), lambda n: (order(n), 0)),
            pl.BlockSpec((rows, DN_CONV_CH), lambda n: (order(n), 0)) if conv_saved else
            pl.BlockSpec((8, DN_CONV_CH), lambda n: (jnp.maximum(order(n) * (rows // 8) - 1, 0), 0)),
            pl.BlockSpec((rows, DN_V), lambda n: (order(n), GATE_COL)),
            pl.BlockSpec((rows, LANES), lambda n: (order(n), BA_COL)),
            pl.BlockSpec((CONV_K, 1, DN_CONV_CH), lambda n: (0, 0, 0)),
            pl.BlockSpec((1, LANES), lambda n: (0, 0)),
            pl.BlockSpec((1, LANES), lambda n: (0, 0)),
            pl.BlockSpec((1, DN_DV), lambda n: (0, 0))]


def _dn_front(c, seq, x, halo, ba, cw_ref, al_ref, dt_ref, yc=None):
    valid = _valid_rows(c * CHUNK, CHUNK, seq)
    xin = x * valid
    if yc is None:
        halo = halo * _valid_rows(c * CHUNK - 8, 8, seq)
        yc = xin * cw_ref[CONV_K - 1]
        for k in range(1, CONV_K):
            yc = yc + _shift_down(xin, halo, k) * cw_ref[CONV_K - 1 - k]
    sgc = jax.nn.sigmoid(yc)
    sig = jax.nn.sigmoid(ba)
    beta = sig * valid
    z = ba + dt_ref[...]
    eal = jnp.exp(al_ref[...])
    g = -eal * _softplus(z) * valid
    ri, ci = _iota((CHUNK, CHUNK), 0), _iota((CHUNK, CHUNK), 1)
    lower = (ri >= ci).astype(F32)
    upper = (ri <= ci).astype(F32)
    eye = (ri == ci).astype(F32)
    gam = _nn(lower, g, hi=True)
    gam_t = _tn(g, upper, hi=True)
    return dict(valid=valid, xin=xin, yc=yc, sgc=sgc, act=yc * sgc, sig=sig, beta=beta, z=z,
                eal=eal, g=g, gam=gam, gam_t=gam_t, ri=ri, ci=ci, upper=upper, eye=eye)


def _dn_head(f, h):
    act = f["act"]
    q_raw = act[:, h * DN_DK:(h + 1) * DN_DK]
    k_raw = act[:, DN_QK + h * DN_DK:DN_QK + (h + 1) * DN_DK]
    v = act[:, 2 * DN_QK + h * DN_DV:2 * DN_QK + (h + 1) * DN_DV]
    rq = lax.rsqrt(jnp.sum(q_raw * q_raw, axis=-1, keepdims=True) + RMS_EPS)
    rk = lax.rsqrt(jnp.sum(k_raw * k_raw, axis=-1, keepdims=True) + RMS_EPS)
    qh = q_raw * rq
    kn = k_raw * rk
    gam_c = _col(f["gam"], DECAY_LANE + h)
    gam_r = _row(f["gam_t"], DECAY_LANE + h)
    bc = _col(f["beta"], BETA_LANE + h)
    diff = gam_c - gam_r
    decay = jnp.where(f["ri"] >= f["ci"], jnp.exp(jnp.minimum(diff, 0.0)), 0.0)
    glast = jnp.sum(gam_r * (_iota((1, CHUNK), 1) == CHUNK - 1).astype(F32), axis=1, keepdims=True)
    return dict(rq=rq, rk=rk, qh=qh, qn=qh * (DN_DK ** -0.5), kn=kn, v=v, gam_c=gam_c, gam_r=gam_r,
                bc=bc, diff=diff, decay=decay, egam=jnp.exp(gam_c), glast=glast,
                eglast=jnp.exp(glast), ekd=jnp.exp(glast - gam_c))


def _dn_fwd(proj, conv_w, alog, dtb, norm_w, seq):
    rows = proj.shape[0]
    nc = rows // CHUNK

    def body(x_ref, halo_ref, gate_ref, ba_ref, cw_ref, al_ref, dt_ref, nw_ref,
             o_ref, y_ref, ss_ref, t_ref, yc_ref, s_ref):
        n = pl.program_id(0)

        @pl.when(n == 0)
        def _():
            s_ref[...] = jnp.zeros_like(s_ref)

        nw = nw_ref[...]
        pre = []
        for c in range(DN_CPS):
            rs = slice(c * CHUNK, (c + 1) * CHUNK)
            halo = halo_ref[...] if c == 0 else x_ref[c * CHUNK - 8:c * CHUNK, :]
            f = _dn_front(n * DN_CPS + c, seq, x_ref[rs, :], halo, ba_ref[rs, :], cw_ref, al_ref, dt_ref)
            yc_ref[rs, :] = f["yc"]
            ri, ci = f["ri"], f["ci"]
            eye = f["eye"]
            diag_m = (jnp.right_shift(ri, INV_SHIFT) == jnp.right_shift(ci, INV_SHIFT)).astype(F32)
            half_m = (jnp.right_shift(ri, INV_SHIFT + 1) == jnp.right_shift(ci, INV_SHIFT + 1)).astype(F32)
            heads = [_dn_head(f, h) for h in range(DN_HEADS)]
            a_all = [jnp.where(ri > ci, hd["bc"] * _nt(hd["kn"], hd["kn"]) * hd["decay"], 0.0) for hd in heads]
            b_all = [a * diag_m for a in a_all]
            t_all = [eye - b for b in b_all]
            for _ in range(INV_SQUARINGS):
                b_all = [_nn(b, b, hi=True) for b in b_all]
                t_all = [t + _nn(t, b, hi=True) for t, b in zip(t_all, b_all)]
            for off_m in (half_m - diag_m, 1.0 - half_m):
                x_all = [_nn(a * off_m, t, hi=True) for a, t in zip(a_all, t_all)]
                t_all = [t - _nn(t, x, hi=True) for t, x in zip(t_all, x_all)]
            u_all = [_nn(t, hd["v"] * hd["bc"], hi=True) for t, hd in zip(t_all, heads)]
            w_all = [_nn(t, hd["kn"] * (hd["bc"] * hd["egam"]), hi=True) for t, hd in zip(t_all, heads)]
            qk_all = [_nt(hd["qn"], hd["kn"]) * hd["decay"] for hd in heads]
            for h in range(DN_HEADS):
                t_ref[c, h] = t_all[h]
            pre.append((heads, u_all, w_all, qk_all))
        for c in range(DN_CPS):
            rs = slice(c * CHUNK, (c + 1) * CHUNK)
            heads, u_all, w_all, qk_all = pre[c]
            s_all = [s_ref[h] for h in range(DN_HEADS)]
            os_all = [_nn(hd["qn"] * hd["egam"], s) for hd, s in zip(heads, s_all)]
            vnew_all = [u - _nn(w, s) for u, w, s in zip(u_all, w_all, s_all)]
            o_all = [os + _nn(qk, vn) for os, qk, vn in zip(os_all, qk_all, vnew_all)]
            snew_all = [s * hd["eglast"] + _tn(hd["kn"] * hd["ekd"], vn)
                        for s, hd, vn in zip(s_all, heads, vnew_all)]
            for h in range(DN_HEADS):
                v_cols = slice(h * DN_DV, (h + 1) * DN_DV)
                ss_ref[c, h] = s_all[h]
                s_ref[h] = snew_all[h]
                o_ref[rs, v_cols] = o_all[h]
                y_ref[rs, v_cols] = _gated_norm(o_all[h], gate_ref[rs, v_cols], nw).astype(y_ref.dtype)

    fwd = lambda n: n
    row_v = pl.BlockSpec((DN_CPS * CHUNK, DN_V), lambda n: (n, 0))
    return pl.pallas_call(
        body, grid=(nc // DN_CPS,), in_specs=_dn_in_specs(fwd),
        out_specs=[row_v, row_v,
                   pl.BlockSpec((DN_CPS, DN_HEADS, DN_DK, DN_DV), lambda n: (n, 0, 0, 0)),
                   pl.BlockSpec((DN_CPS, DN_HEADS, CHUNK, CHUNK), lambda n: (n, 0, 0, 0)),
                   pl.BlockSpec((DN_CPS * CHUNK, DN_CONV_CH), lambda n: (n, 0))],
        out_shape=[jax.ShapeDtypeStruct((rows, DN_V), F32), jax.ShapeDtypeStruct((rows, DN_V), BF16),
                   jax.ShapeDtypeStruct((nc, DN_HEADS, DN_DK, DN_DV), F32),
                   jax.ShapeDtypeStruct((nc, DN_HEADS, CHUNK, CHUNK), F32),
                   jax.ShapeDtypeStruct((rows, DN_CONV_CH), F32)],
        scratch_shapes=[pltpu.VMEM((DN_HEADS, DN_DK, DN_DV), F32)],
        name="dn_fwd", compiler_params=_params("arbitrary"))(
            proj, proj, proj, proj, conv_w, alog, dtb, norm_w.reshape(1, DN_DV))


def _dn_bwd(proj, conv_out, o, dy, states, tinv, conv_w, alog, dtb, norm_w, seq):
    rows = proj.shape[0]
    nc = rows // CHUNK

    def body(x_ref, yc_ref, gate_ref, ba_ref, cw_ref, al_ref, dt_ref, nw_ref,
             o_ref, dy_ref, ss_ref, t_ref,
             dp_ref, dcw_ref, dal_ref, ddt_ref, dnw_ref, ds_ref, nxt_ref):
        n = pl.program_id(0)

        @pl.when(n == 0)
        def _():
            ds_ref[...] = jnp.zeros_like(ds_ref)
            nxt_ref[...] = jnp.zeros_like(nxt_ref)
            dcw_ref[...] = jnp.zeros_like(dcw_ref)
            dal_ref[...] = jnp.zeros_like(dal_ref)
            ddt_ref[...] = jnp.zeros_like(ddt_ref)
            dnw_ref[...] = jnp.zeros_like(dnw_ref)

        for c in reversed(range(DN_CPS)):
            rs = pl.ds(c * CHUNK, CHUNK)
            chunk((steps - 1 - n) * DN_CPS + c, x_ref.at[rs], yc_ref.at[rs], gate_ref.at[rs], ba_ref.at[rs],
                  cw_ref, al_ref, dt_ref, nw_ref, o_ref.at[rs], dy_ref.at[rs], ss_ref.at[c], t_ref.at[c],
                  dp_ref.at[rs], dcw_ref, dal_ref, ddt_ref, dnw_ref, ds_ref, nxt_ref)

    def chunk(ch, x_ref, yc_ref, gate_ref, ba_ref, cw_ref, al_ref, dt_ref, nw_ref,
              o_ref, dy_ref, ss_ref, t_ref,
              dp_ref, dcw_ref, dal_ref, ddt_ref, dnw_ref, ds_ref, nxt_ref):
        f = _dn_front(ch, seq, x_ref[...], None, ba_ref[...], cw_ref, al_ref, dt_ref, yc_ref[...])
        ri, ci = f["ri"], f["ci"]
        strict = (ri > ci).astype(F32)
        nw = nw_ref[...]
        lane128 = _iota((1, LANES), 1)
        row128 = _iota((LANES, 1), 0)
        dgam_col = jnp.zeros((CHUNK, LANES), F32)
        dgam_row = jnp.zeros((LANES, CHUNK), F32)
        dbeta = jnp.zeros((CHUNK, LANES), F32)
        dnw = jnp.zeros((1, DN_DV), F32)
        hs = range(DN_HEADS)
        heads = [_dn_head(f, h) for h in hs]
        cols = [slice(h * DN_DV, (h + 1) * DN_DV) for h in hs]
        t_l = [t_ref[h] for h in hs]
        s_l = [ss_ref[h] for h in hs]
        ds_l = [ds_ref[h] for h in hs]
        kk_l = [_nt(hd["kn"], hd["kn"]) for hd in heads]
        p_l = [_nt(hd["qn"], hd["kn"]) for hd in heads]
        rhsw_l = [hd["kn"] * (hd["bc"] * hd["egam"]) for hd in heads]
        u_l = [_nn(t, hd["v"] * hd["bc"], hi=True) for t, hd in zip(t_l, heads)]
        w_l = [_nn(t, r, hi=True) for t, r in zip(t_l, rhsw_l)]
        vnew_l = [u - _nn(w, s) for u, w, s in zip(u_l, w_l, s_l)]
        gnb = [_gated_norm_bwd(dy_ref[:, c], o_ref[:, c], gate_ref[:, c], nw) for c in cols]
        do_l = [x[0] for x in gnb]
        for h in hs:
            dp_ref[:, DN_CONV_CH + h * DN_DV:DN_CONV_CH + (h + 1) * DN_DV] = gnb[h][1].astype(dp_ref.dtype)
            dnw = dnw + gnb[h][2]
        qg_l = [hd["qn"] * hd["egam"] for hd in heads]
        kd_l = [hd["kn"] * hd["ekd"] for hd in heads]
        dvnew_l = [_tn(p * hd["decay"], do) + _nn(kd, ds)
                   for p, hd, do, kd, ds in zip(p_l, heads, do_l, kd_l, ds_l)]
        m_l = [_nt(do, vn) for do, vn in zip(do_l, vnew_l)]
        dqg_l = [_nt(do, s) for do, s in zip(do_l, s_l)]
        dkd_l = [_nt(vn, ds) for vn, ds in zip(vnew_l, ds_l)]
        for h in hs:
            ds_ref[h] = (ds_l[h] * heads[h]["eglast"] + _tn(qg_l[h], do_l[h]) - _tn(w_l[h], dvnew_l[h]))
        dw_l = [-_nt(dvn, s) for dvn, s in zip(dvnew_l, s_l)]
        dru_l = [_tn(t, dvn, hi=True) for t, dvn in zip(t_l, dvnew_l)]
        drw_l = [_tn(t, dw_, hi=True) for t, dw_ in zip(t_l, dw_l)]
        da_l = [-(_nt(dru, u) + _nt(drw, w)) * strict for dru, u, drw, w in zip(dru_l, u_l, drw_l, w_l)]
        dp_l = [m * hd["decay"] for m, hd in zip(m_l, heads)]
        dkk_l = [da * (hd["bc"] * hd["decay"]) for da, hd in zip(da_l, heads)]
        dqn_l = [dqg * hd["egam"] + _nn(dp, hd["kn"]) for dqg, hd, dp in zip(dqg_l, heads, dp_l)]
        dkn_l = [_tn(dp, hd["qn"]) + dkd * hd["ekd"] + drw * (hd["bc"] * hd["egam"])
                 + _nn(dkk, hd["kn"]) + _tn(dkk, hd["kn"])
                 for dp, hd, dkd, drw, dkk in zip(dp_l, heads, dkd_l, drw_l, dkk_l)]
        dq_parts, dk_parts, dv_parts = [], [], []
        for h in hs:
            hd = heads[h]
            kn, v, bc, egam, decay = hd["kn"], hd["v"], hd["bc"], hd["egam"], hd["decay"]
            t1 = jnp.sum(dkd_l[h] * kd_l[h], axis=1, keepdims=True)
            dglast = (jnp.sum(t1, axis=0, keepdims=True)
                      + jnp.sum(jnp.sum(ds_l[h] * s_l[h], axis=1, keepdims=True), axis=0, keepdims=True)
                      * hd["eglast"])
            e = (m_l[h] * p_l[h] + da_l[h] * (bc * kk_l[h])) * decay
            dgc = (jnp.sum(dqg_l[h] * qg_l[h], axis=1, keepdims=True) - t1
                   + jnp.sum(drw_l[h] * rhsw_l[h], axis=1, keepdims=True)
                   + jnp.sum(e, axis=1, keepdims=True)
                   + jnp.where(_iota((CHUNK, 1), 0) == CHUNK - 1, dglast, 0.0))
            dgr = -jnp.sum(e, axis=0, keepdims=True)
            dbc = (jnp.sum(dru_l[h] * v, axis=1, keepdims=True)
                   + jnp.sum(drw_l[h] * kn, axis=1, keepdims=True) * egam
                   + jnp.sum(da_l[h] * kk_l[h] * decay, axis=1, keepdims=True))
            dv_parts.append(dru_l[h] * bc)
            qh, dqn, dkn = hd["qh"], dqn_l[h], dkn_l[h]
            dq_parts.append(((DN_DK ** -0.5) * hd["rq"])
                            * (dqn - qh * jnp.sum(dqn * qh, axis=1, keepdims=True)))
            dk_parts.append(hd["rk"] * (dkn - kn * jnp.sum(dkn * kn, axis=1, keepdims=True)))
            dgam_col = dgam_col + dgc * (lane128 == DECAY_LANE + h).astype(F32)
            dbeta = dbeta + dbc * (lane128 == BETA_LANE + h).astype(F32)
            dgam_row = dgam_row + (row128 == DECAY_LANE + h).astype(F32) * dgr
        dnw_ref[...] += dnw
        dgam = dgam_col + _nt(f["eye"], dgam_row, hi=True)
        dg = _nn(f["upper"], dgam, hi=True)
        d_a = dg * (-f["eal"]) * jax.nn.sigmoid(f["z"]) * f["valid"]
        dal_ref[...] += jnp.sum(dg * f["g"], axis=0, keepdims=True)
        ddt_ref[...] += jnp.sum(d_a, axis=0, keepdims=True)
        d_b = dbeta * f["valid"] * f["sig"] * (1.0 - f["sig"])
        dp_ref[:, DN_CONV_CH + DN_V:DN_CONV_CH + DN_V + LANES] = (d_a + d_b).astype(dp_ref.dtype)
        dp_ref[:, DN_CONV_CH + DN_V + LANES:] = jnp.zeros((CHUNK, DN_IN_PAD - DN_IN_USED), dp_ref.dtype)
        dact = jnp.concatenate(dq_parts + dk_parts + dv_parts, axis=1)
        yc, sgc = f["yc"], f["sgc"]
        dyc = dact * (sgc * (1.0 + yc * (1.0 - sgc)))
        nxt = nxt_ref[...]
        ups = [dyc] + [_shift_up(dyc, nxt, j) for j in range(1, CONV_K)]
        dx = ups[0] * cw_ref[CONV_K - 1]
        for j in range(1, CONV_K):
            dx = dx + ups[j] * cw_ref[CONV_K - 1 - j]
        for j in range(CONV_K):
            dcw_ref[CONV_K - 1 - j] += jnp.sum(f["xin"] * ups[j], axis=0, keepdims=True)
        nxt_ref[...] = dyc[0:8]
        dp_ref[:, :DN_CONV_CH] = (dx * f["valid"]).astype(dp_ref.dtype)

    steps = nc // DN_CPS
    rev = lambda n: steps - 1 - n
    row_v = pl.BlockSpec((DN_CPS * CHUNK, DN_V), lambda n: (rev(n), 0))
    vec = pl.BlockSpec((1, LANES), lambda n: (0, 0))
    return pl.pallas_call(
        body, grid=(steps,),
        in_specs=_dn_in_specs(rev, conv_saved=True) + [
            row_v, row_v,
            pl.BlockSpec((DN_CPS, DN_HEADS, DN_DK, DN_DV), lambda n: (rev(n), 0, 0, 0)),
            pl.BlockSpec((DN_CPS, DN_HEADS, CHUNK, CHUNK), lambda n: (rev(n), 0, 0, 0))],
        out_specs=[pl.BlockSpec((DN_CPS * CHUNK, DN_IN_PAD), lambda n: (rev(n), 0)),
                   pl.BlockSpec((CONV_K, 1, DN_CONV_CH), lambda n: (0, 0, 0)), vec, vec,
                   pl.BlockSpec((1, DN_DV), lambda n: (0, 0))],
        out_shape=[jax.ShapeDtypeStruct((rows, DN_IN_PAD), BF16),
                   jax.ShapeDtypeStruct((CONV_K, 1, DN_CONV_CH), F32),
                   jax.ShapeDtypeStruct((1, LANES), F32), jax.ShapeDtypeStruct((1, LANES), F32),
                   jax.ShapeDtypeStruct((1, DN_DV), F32)],
        scratch_shapes=[pltpu.VMEM((DN_HEADS, DN_DK, DN_DV), F32), pltpu.VMEM((8, DN_CONV_CH), F32)],
        name="dn_bwd", compiler_params=_params("arbitrary"))(
            proj, conv_out, proj, proj, conv_w, alog, dtb, norm_w.reshape(1, DN_DV), o, dy, states, tinv)


def _train_step(x, tgt, wts, sh, idx):
    seq = x.shape[0]
    rows = -(-(seq + CHUNK) // ROW_ALIGN) * ROW_ALIGN
    wts = dict(wts)
    (h0, tgt_p), (got,) = _embed(x, tgt, wts["meta_tokens"].astype(F32), rows,
                                 ride=_Ride("gather", [sh["ret_w_in"]]))
    wts["ret_w_in"] = got.reshape(N_SHARD, D_MODEL, -1)
    cos, sin = _rope_tables(rows // CHUNK)
    consts = _ret_consts()
    conv_w = wts["dn_conv_w"].reshape(CONV_K, 1, DN_CONV_CH)
    lane_pad = LANES - 2 * DN_HEADS
    alog = jnp.pad(wts["dn_a_log"].reshape(1, DN_HEADS), ((0, 0), (DECAY_LANE, lane_pad)))
    dtb = jnp.pad(wts["dn_dt_bias"].reshape(1, DN_HEADS), ((0, 0), (DECAY_LANE, lane_pad)))
    g = {}

    hn0 = _rms_fwd(h0, wts["mix_norm_w"][0], "rms_mix0")
    proj0, got = _mm_cols(hn0, wts["ret_w_in"], "ret_in",
                          ride=_Ride("gather", [sh["ret_w_out"], sh["ffn_w_gate"]]))
    wts["ret_w_out"] = got[0].reshape(-1, D_MODEL)
    wts["ffn_w_gate"] = got[1]
    (o0, y0, st0), got = _ret_fwd(proj0, cos, sin, consts, wts["ret_gn_w"], seq,
                                  ride=_Ride("gather", [sh["ffn_w_up"], sh["ffn_w_down"]]))
    wts["ffn_w_up"], wts["ffn_w_down"] = got
    h1 = _mm(y0, wts["ret_w_out"], mode="nn", name="ret_out", resid=h0)
    hn1 = _rms_fwd(h1, wts["ffn_norm_w"][0], "rms_ffn0")
    (g0, u0, act0), got = _ffn_up(hn1, wts["ffn_w_gate"], wts["ffn_w_up"], 0, "ffn_up0",
                                  ride=_Ride("gather", [sh["dn_w_in"], sh["dn_w_out"]]))
    n_dn = sh["dn_w_in"].shape[-1]
    dn_shards = got[0].reshape(N_SHARD, D_MODEL, n_dn)
    wts["dn_w_in"] = jnp.concatenate(
        [dn_shards[j] for j in range(N_SHARD)]
        + [jnp.zeros((D_MODEL, DN_IN_PAD - N_SHARD * n_dn), dn_shards.dtype)], axis=-1)
    wts["dn_w_out"] = got[1].reshape(-1, D_MODEL)
    h2 = _ffn_down(act0, wts["ffn_w_down"], h1, 0, "ffn_down0")
    hn2 = _rms_fwd(h2, wts["mix_norm_w"][1], "rms_mix1")
    proj1 = _mm(hn2, wts["dn_w_in"], mode="nn", name="dn_in")
    o1, y1, st1, tinv, conv1 = _dn_fwd(proj1, conv_w, alog, dtb, wts["dn_norm_w"], seq)
    h3 = _mm(y1, wts["dn_w_out"], mode="nn", name="dn_out", resid=h2)
    hn3 = _rms_fwd(h3, wts["ffn_norm_w"][1], "rms_ffn1")
    g1, u1, act1 = _ffn_up(hn3, wts["ffn_w_gate"], wts["ffn_w_up"], 1, "ffn_up1")
    h4 = _ffn_down(act1, wts["ffn_w_down"], h3, 1, "ffn_down1")

    dh4, g["final_norm_w"], loss, dh4b = _final_loss(h4, wts["final_norm_w"], tgt_p, seq, "final_loss")

    layers = wts["ffn_w_gate"].shape[1]

    ffn_names = ["ffn_w_down", "ffn_w_gate", "ffn_w_up"]

    def ffn_bwd(dh_out, dhb_out, h_mid, hn, gg, uu, act, layer, prev, ride=None, last=False):
        tag = str(layer)
        res = _ffn_down_bwd(dhb_out, wts["ffn_w_down"], gg, uu, layer, "ffn_down_bwd" + tag, ride=ride)
        (dg, du), rode = res if ride is not None else (res, None)
        d_down = _ffn_wgrad(act, [dhb_out], layer, layers, prev and prev[:1], True, "ffn_dwd" + tag)
        d_gu = _ffn_wgrad(hn, [dg, du], layer, layers, prev and prev[1:], False, "ffn_dwgu" + tag)
        grads = list(d_down) + list(d_gu)
        gs = rs_grads(ffn_names, grads) if last else None
        res = _ffn_up_bwd(dg, du, wts["ffn_w_gate"], wts["ffn_w_up"], layer, h_mid, wts["ffn_norm_w"][layer],
                          dh_out, "ffn_up_bwd" + tag, ride=_Ride("pair", gs) if last else None)
        (dh_mid, d_norm, dhb_mid), sib = res if last else (res, None)
        return dh_mid, dhb_mid, grads, d_norm, rode, gs, sib

    red = {}

    def rs_grads(names, grads):
        return [gr.reshape((N_SHARD,) + sh[n].shape) for n, gr in zip(names, grads)]

    def rs_partials(names, gs, sib):
        return [_rs_pair_add(gs[t], sib[t], idx, "rs_pair_add_" + n) for t, n in enumerate(names)]

    def rs_end(names, gs, sib, others, tag):
        mine = [_rs_final_add(gs[t], sib[t], others[t], idx, "rs_final_add_" + n) for t, n in enumerate(names)]
        red.update(zip(names, _rs_share(mine, "rs_share" + tag)))

    dh3, dh3b, ffn_grads, dfn1 = ffn_bwd(dh4, dh4b, h3, hn3, g1, u1, act1, 1, None)[:4]
    dy1 = _mm(dh3b, wts["dn_w_out"], mode="nt", name="dn_out_bwd")
    d_dn_out = _mm(y1, dh3b, mode="tn", name="dn_dwo")
    dproj1, dcw, dal, ddt, g["dn_norm_w"] = _dn_bwd(proj1, conv1, o1, dy1, st1, tinv, conv_w, alog, dtb,
                                                    wts["dn_norm_w"], seq)
    d_dn_in = _mm(hn2, dproj1, mode="tn", name="dn_dwi")
    d_dn_in = jnp.stack([d_dn_in[:, j * n_dn:(j + 1) * n_dn] for j in range(N_SHARD)])
    group1 = ["dn_w_out", "dn_w_in"]
    gs1 = rs_grads(group1, [d_dn_out, d_dn_in])
    (dh2, dmn1, dh2b), sib1 = _mm_nt_rms(dproj1, wts["dn_w_in"], h2, wts["mix_norm_w"][1], dh3, "dn_in_bwd",
                                         ride=_Ride("pair", gs1))
    g["dn_conv_w"] = dcw.reshape(CONV_K, DN_CONV_CH)
    g["dn_a_log"] = dal[0, DECAY_LANE:DECAY_LANE + DN_HEADS]
    g["dn_dt_bias"] = ddt[0, DECAY_LANE:DECAY_LANE + DN_HEADS]

    dh1, dh1b, _, dfn0, others1, gs2, sib2 = ffn_bwd(
        dh2, dh2b, h1, hn1, g0, u0, act0, 0, ffn_grads,
        ride=_Ride("chips", rs_partials(group1, gs1, sib1)), last=True)
    rs_end(group1, gs1, sib1, others1, "1")
    d_ret_out = _mm(y0, dh1b, mode="tn", name="ret_dwo")
    gs2b = rs_grads(["ret_w_out"], [d_ret_out])
    dy0, sib2b = _mm(dh1b, wts["ret_w_out"], mode="nt", name="ret_out_bwd", ride=_Ride("pair", gs2b))
    group2 = ffn_names + ["ret_w_out"]
    gs2, sib2 = gs2 + gs2b, list(sib2) + list(sib2b)
    (dproj0, g["ret_gn_w"]), others2 = _ret_bwd(proj0, o0, dy0, st0, cos, sin, consts, wts["ret_gn_w"], seq,
                                                ride=_Ride("chips", rs_partials(group2, gs2, sib2)))
    rs_end(group2, gs2, sib2, others2, "2")
    d_ret_in = _mm_cols_grad(hn0, dproj0, "ret_dwi")
    gs3 = rs_grads(["ret_w_in"], [d_ret_in])
    sib3 = _rs_pair(gs3, "rs_pair3")
    (dh0, dmn0, _), others3 = _mm_cols_t_rms(dproj0, wts["ret_w_in"], h0, wts["mix_norm_w"][0], dh1, "ret_in_bwd",
                                             ride=_Ride("chips", rs_partials(["ret_w_in"], gs3, sib3)))
    rs_end(["ret_w_in"], gs3, sib3, others3, "3")

    g["ffn_norm_w"] = jnp.concatenate([dfn0, dfn1], axis=0)
    g["mix_norm_w"] = jnp.concatenate([dmn0, dmn1], axis=0)
    g["meta_tokens"] = dh0[PAD:CHUNK]
    g["final_norm_w"] = g["final_norm_w"].reshape(D_MODEL)
    g["ret_gn_w"] = g["ret_gn_w"].reshape(RET_DV)
    g["dn_norm_w"] = g["dn_norm_w"].reshape(DN_DV)
    return loss, dh0, g, red


def _mesh_pos():
    return lax.axis_index("x"), lax.axis_index("y"), lax.axis_index("c")


def _other_chips(x, y):
    return [(1 - x, y), (x, 1 - y), (1 - x, 1 - y)]


def _remote(src, dst, send_sem, recv_sem, to):
    return pltpu.make_async_remote_copy(src_ref=src, dst_ref=dst, send_sem=send_sem, recv_sem=recv_sem,
                                        device_id=to, device_id_type=MESH)


GATHER_COPIES = 7


def _gather_phase(phase, ins, outs, send_sems, recv_sems):
    x, y, c = _mesh_pos()
    me = 2 * x + y
    chips = _other_chips(x, y)
    sibling = (x, y, 1 - c)

    def cp(t, k, src, dst, to):
        i = GATHER_COPIES * t + k
        return _remote(src, dst, send_sems.at[i], recv_sems.at[i], to)

    for t in range(len(ins)):
        own = cp(t, 0, ins[t], outs[t].at[me], sibling)
        if phase == 0:
            own.start()
        if phase == 2:
            own.wait()
        for k, (px, py) in enumerate(chips):
            landed = outs[t].at[2 * px + py, c]
            theirs = outs[t].at[2 * px + py, 1 - c]
            to_chip = cp(t, 1 + k, ins[t].at[c], outs[t].at[me, c], (px, py, c))
            if phase == 0:
                to_chip.start()
            if phase == 1:
                cp(t, 1 + k, ins[t].at[c], landed, (px, py, c)).wait_recv()
                cp(t, 4 + k, landed, landed, sibling).start()
            if phase == 2:
                to_chip.wait_send()
                cp(t, 4 + k, landed, landed, sibling).wait_send()
                cp(t, 4 + k, theirs, theirs, sibling).wait_recv()


def _chips_phase(phase, ins, outs, send_sems, recv_sems):
    x, y, c = _mesh_pos()
    for t in range(len(ins)):
        for k, (px, py) in enumerate(_other_chips(x, y)):
            cp = _remote(ins[t].at[2 * px + py], outs[t].at[k], send_sems.at[3 * t + k], recv_sems.at[3 * t + k],
                         (px, py, c))
            if phase == 0:
                cp.start()
            if phase == 2:
                cp.wait()


class _Ride:
    def __init__(self, kind, arrays):
        self.kind, self.arrays = kind, list(arrays)
        nt = len(self.arrays)
        if kind == "gather":
            self.phase_fn, n_sem = _gather_phase, GATHER_COPIES * nt
            self.out_shape = [jax.ShapeDtypeStruct((N_SHARD,) + a.shape, a.dtype) for a in self.arrays]
        elif kind == "pair":
            self.phase_fn, n_sem = _pair_phase, nt
            self.out_shape = [jax.ShapeDtypeStruct(a.shape[:1] + a.shape[2:], a.dtype) for a in self.arrays]
        else:
            self.phase_fn, n_sem = _chips_phase, 3 * nt
            self.out_shape = [jax.ShapeDtypeStruct((3,) + a.shape[1:], a.dtype) for a in self.arrays]
        self.in_specs, self.out_specs = [ANY] * nt, [ANY] * nt
        self.scratch = [pltpu.SemaphoreType.DMA((n_sem,)), pltpu.SemaphoreType.DMA((n_sem,))]

    def emit(self, step, nsteps, ins, outs, send_sems, recv_sems, before):
        mid = max(0, min((7 * nsteps) // 8, nsteps - 2))
        todo = [(0, 0), (1, mid)] if before else [(2, nsteps - 1)]
        for phase, at in todo:
            if phase == 1 and self.kind != "gather":
                continue

            @pl.when(step == at)
            def _(phase=phase):
                self.phase_fn(phase, ins, outs, send_sems, recv_sems)


def _gather_small(blk):
    r, wd = blk.shape

    def body(b_ref, out_ref, send_sems, recv_sems):
        x, y, c = _mesh_pos()
        chips = _other_chips(x, y)
        out_ref[2 * x + y] = b_ref[...]
        sends = [_remote(b_ref, out_ref.at[2 * x + y], send_sems.at[k], recv_sems.at[k], (px, py, c))
                 for k, (px, py) in enumerate(chips)]
        for cp in sends:
            cp.start()
        for k, (px, py) in enumerate(chips):
            _remote(b_ref, out_ref.at[2 * px + py], send_sems.at[k], recv_sems.at[k], (px, py, c)).wait_recv()
        for cp in sends:
            cp.wait_send()

    return pl.pallas_call(
        body, out_shape=jax.ShapeDtypeStruct((4, r, wd), blk.dtype), in_specs=[VMEM_SPEC], out_specs=VMEM_SPEC,
        scratch_shapes=[pltpu.SemaphoreType.DMA((3,)), pltpu.SemaphoreType.DMA((3,))],
        name="gather_small")(blk)


def _allreduce_small(blk):
    r, wd = blk.shape
    rels = [(dx, dy, dc) for dx in (0, 1) for dy in (0, 1) for dc in (0, 1) if dx or dy or dc]

    def body(b_ref, out_ref, buf_ref, send_sems, recv_sems):
        x, y, c = _mesh_pos()

        def peer(rel):
            dx, dy, dc = rel
            return (1 - x if dx else x, 1 - y if dy else y, 1 - c if dc else c)

        me = 4 * x + 2 * y + c
        buf_ref[me] = b_ref[...]
        sends = [_remote(b_ref, buf_ref.at[me], send_sems.at[k], recv_sems.at[k], peer(rel))
                 for k, rel in enumerate(rels)]
        for cp in sends:
            cp.start()
        for k, rel in enumerate(rels):
            px, py, pc = peer(rel)
            _remote(b_ref, buf_ref.at[4 * px + 2 * py + pc], send_sems.at[k], recv_sems.at[k],
                    (px, py, pc)).wait_recv()
        for cp in sends:
            cp.wait_send()
        acc = buf_ref[0]
        for d in range(1, 8):
            acc = acc + buf_ref[d]
        out_ref[...] = acc

    return pl.pallas_call(
        body, out_shape=jax.ShapeDtypeStruct((r, wd), blk.dtype), in_specs=[VMEM_SPEC], out_specs=VMEM_SPEC,
        scratch_shapes=[pltpu.VMEM((8, r, wd), blk.dtype), pltpu.SemaphoreType.DMA((7,)),
                        pltpu.SemaphoreType.DMA((7,))],
        name="allreduce_small")(blk)


def _rs_pair(gs, name):
    ride = _Ride("pair", gs)

    def body(*refs):
        nt = len(gs)
        for phase in (0, 2):
            _pair_phase(phase, refs[:nt], refs[nt:2 * nt], *refs[2 * nt:])

    return pl.pallas_call(body, out_shape=ride.out_shape, in_specs=ride.in_specs, out_specs=ride.out_specs,
                          scratch_shapes=ride.scratch, name=name)(*gs)


def _pair_phase(phase, ins, outs, send_sems, recv_sems):
    x, y, c = _mesh_pos()
    for t in range(len(ins)):
        cp = _remote(ins[t].at[:, 1 - c], outs[t], send_sems.at[t], recv_sems.at[t], (x, y, 1 - c))
        if phase == 0:
            cp.start()
        if phase == 2:
            cp.wait()


def _rs_tile(a, b):
    return _div_tile(a, 512 if b <= 1024 else 256, 16)


def _rs_pair_add(g, a, idx, name):
    _, _, rows, cols = g.shape
    tr = _rs_tile(rows, cols)

    def body(s_ref, g_ref, a_ref, p_ref):
        p_ref[...] = (g_ref[...] + a_ref[...]).astype(p_ref.dtype)

    blk = pl.BlockSpec((None, tr, cols), lambda j, i, s: (j, i, 0))
    spec = pltpu.PrefetchScalarGridSpec(
        num_scalar_prefetch=1, grid=(N_SHARD, rows // tr),
        in_specs=[pl.BlockSpec((None, None, tr, cols), lambda j, i, s: (j, s[0], i, 0)), blk], out_specs=blk)
    return pl.pallas_call(
        body, grid_spec=spec, out_shape=jax.ShapeDtypeStruct((N_SHARD, rows, cols), BF16), name=name,
        compiler_params=_params("parallel", "parallel"))(idx, g, a)


def _rs_final_add(g, a, b, idx, name):
    _, _, rows, cols = g.shape
    tr = _rs_tile(rows, cols)

    def body(s_ref, g_ref, a_ref, b0_ref, b1_ref, b2_ref, f_ref):
        own = g_ref[...] + a_ref[...]
        f_ref[...] = ((own + b0_ref[...].astype(F32)) + b1_ref[...].astype(F32)) + b2_ref[...].astype(F32)

    def b_spec(k):
        return pl.BlockSpec((None, tr, cols), lambda i, s: (k, i, 0))

    spec = pltpu.PrefetchScalarGridSpec(
        num_scalar_prefetch=1, grid=(rows // tr,),
        in_specs=[pl.BlockSpec((None, None, tr, cols), lambda i, s: (s[1], s[0], i, 0)),
                  pl.BlockSpec((None, tr, cols), lambda i, s: (s[1], i, 0)), b_spec(0), b_spec(1), b_spec(2)],
        out_specs=pl.BlockSpec((None, tr, cols), lambda i, s: (s[0], i, 0)))
    return pl.pallas_call(
        body, grid_spec=spec, out_shape=jax.ShapeDtypeStruct((2, rows, cols), F32), name=name,
        compiler_params=_params("parallel"))(idx, g, a, b, b, b)


def _rs_share(fs, name):
    nt = len(fs)

    def body(*refs):
        outs = refs[nt:2 * nt]
        send_sems, recv_sems = refs[2 * nt:]
        x, y, c = _mesh_pos()
        cps = [_remote(outs[t].at[c], outs[t].at[c], send_sems.at[t], recv_sems.at[t], (x, y, 1 - c))
               for t in range(nt)]
        for cp in cps:
            cp.start()
        for cp in cps:
            cp.wait()

    return pl.pallas_call(
        body, out_shape=[jax.ShapeDtypeStruct(f.shape, f.dtype) for f in fs],
        in_specs=[ANY] * nt, out_specs=[ANY] * nt, input_output_aliases={t: t for t in range(nt)},
        scratch_shapes=[pltpu.SemaphoreType.DMA((nt,)), pltpu.SemaphoreType.DMA((nt,))], name=name)(*fs)


def _adamw(w, g, m, v, name):
    lead, rows, cols = w.shape
    tr = rows // 4 if rows % 32 == 0 else rows

    def body(w_ref, g_ref, m_ref, v_ref, go_ref, d_ref, mo_ref, vo_ref):
        gv = g_ref[...]
        go_ref[...] = gv
        mn = ADAM_B1 * m_ref[...] + (1.0 - ADAM_B1) * gv
        vn = ADAM_B2 * v_ref[...] + (1.0 - ADAM_B2) * (gv * gv)
        m_hat = mn / (1.0 - ADAM_B1 ** ADAM_STEP)
        v_hat = vn / (1.0 - ADAM_B2 ** ADAM_STEP)
        d_ref[...] = -ADAM_LR * (m_hat / (jnp.sqrt(v_hat) + ADAM_EPS) + ADAM_WD * w_ref[...])
        mo_ref[...] = mn
        vo_ref[...] = vn

    blk = pl.BlockSpec((None, tr, cols), lambda l, i: (l, i, 0))
    out = jax.ShapeDtypeStruct((lead, rows, cols), F32)
    return pl.pallas_call(
        body, grid=(lead, rows // tr), in_specs=[blk] * 4, out_specs=[blk] * 4, out_shape=[out] * 4, name=name,
        compiler_params=_params("parallel", "parallel"))(w, g, m, v)


BIG = ["ret_w_in", "ret_w_out", "dn_w_in", "dn_w_out", "ffn_w_gate", "ffn_w_up", "ffn_w_down"]
TRANSPOSED_AT_BOUNDARY = {"dn_w_in": True, "ffn_w_gate": False, "ffn_w_up": False}
SMALL =["meta_tokens", "mix_norm_w", "ffn_norm_w", "ret_gn_w", "dn_conv_w", "dn_a_log", "dn_dt_bias",
         "dn_norm_w", "final_norm_w"]
SMALL_SHARDED = {"meta_tokens", "dn_conv_w", "dn_norm_w"}
ORDER = ["meta_tokens", "mix_norm_w", "ffn_norm_w", "ret_w_in", "ret_gn_w", "ret_w_out", "dn_w_in",
         "dn_conv_w", "dn_a_log", "dn_dt_bias", "dn_norm_w", "dn_w_out", "ffn_w_gate", "ffn_w_up",
         "ffn_w_down", "final_norm_w"]


def _halves(a):
    return a.reshape(2, -1, a.shape[-1])


def _pack_lanes(parts, align=8):
    flat = jnp.concatenate([p.reshape(-1) for p in parts])
    flat = jnp.pad(flat, (0, -flat.shape[0] % (align * LANES)))
    return flat.reshape(-1, LANES)


def _unpack(buf, shapes):
    lead = buf.shape[:-2]
    flat = buf.reshape(lead + (-1,))
    out, off = [], 0
    for shp in shapes:
        size = math.prod(shp)
        out.append(flat[..., off:off + size].reshape(lead + tuple(shp)))
        off += size
    return out


def _join_cols(shards):
    return jnp.concatenate([shards[j] for j in range(N_SHARD)], axis=-1)


def kernel(x, meta_tokens, mix_norm_w, ffn_norm_w, ret_w_in, ret_gn_w, ret_w_out, dn_w_in, dn_conv_w, dn_a_log, dn_dt_bias, dn_norm_w, dn_w_out, ffn_w_gate, ffn_w_up, ffn_w_down, final_norm_w, loss_target, m_meta_tokens, m_mix_norm_w, m_ffn_norm_w, m_ret_w_in, m_ret_gn_w, m_ret_w_out, m_dn_w_in, m_dn_conv_w, m_dn_a_log, m_dn_dt_bias, m_dn_norm_w, m_dn_w_out, m_ffn_w_gate, m_ffn_w_up, m_ffn_w_down, m_final_norm_w, v_meta_tokens, v_mix_norm_w, v_ffn_norm_w, v_ret_w_in, v_ret_gn_w, v_ret_w_out, v_dn_w_in, v_dn_conv_w, v_dn_a_log, v_dn_dt_bias, v_dn_norm_w, v_dn_w_out, v_ffn_w_gate, v_ffn_w_up, v_ffn_w_down, v_final_norm_w):
    w = dict(meta_tokens=meta_tokens, mix_norm_w=mix_norm_w, ffn_norm_w=ffn_norm_w, ret_w_in=ret_w_in,
             ret_gn_w=ret_gn_w, ret_w_out=ret_w_out, dn_w_in=dn_w_in, dn_conv_w=dn_conv_w, dn_a_log=dn_a_log,
             dn_dt_bias=dn_dt_bias, dn_norm_w=dn_norm_w, dn_w_out=dn_w_out, ffn_w_gate=ffn_w_gate,
             ffn_w_up=ffn_w_up, ffn_w_down=ffn_w_down, final_norm_w=final_norm_w)
    m = dict(meta_tokens=m_meta_tokens, mix_norm_w=m_mix_norm_w, ffn_norm_w=m_ffn_norm_w, ret_w_in=m_ret_w_in,
             ret_gn_w=m_ret_gn_w, ret_w_out=m_ret_w_out, dn_w_in=m_dn_w_in, dn_conv_w=m_dn_conv_w,
             dn_a_log=m_dn_a_log, dn_dt_bias=m_dn_dt_bias, dn_norm_w=m_dn_norm_w, dn_w_out=m_dn_w_out,
             ffn_w_gate=m_ffn_w_gate, ffn_w_up=m_ffn_w_up, ffn_w_down=m_ffn_w_down, final_norm_w=m_final_norm_w)
    v = dict(meta_tokens=v_meta_tokens, mix_norm_w=v_mix_norm_w, ffn_norm_w=v_ffn_norm_w, ret_w_in=v_ret_w_in,
             ret_gn_w=v_ret_gn_w, ret_w_out=v_ret_w_out, dn_w_in=v_dn_w_in, dn_conv_w=v_dn_conv_w,
             dn_a_log=v_dn_a_log, dn_dt_bias=v_dn_dt_bias, dn_norm_w=v_dn_norm_w, dn_w_out=v_dn_w_out,
             ffn_w_gate=v_ffn_w_gate, ffn_w_up=v_ffn_w_up, ffn_w_down=v_ffn_w_down, final_norm_w=v_final_norm_w)
    mx, my, mc = _mesh_pos()
    chip = 2 * mx + my

    sm_names = [n for n in SMALL if n in SMALL_SHARDED]
    sm_gathered = _unpack(_gather_small(_pack_lanes([w[n] for n in sm_names])), [w[n].shape for n in sm_names])
    full = {n: _join_cols(sm_gathered[i]) for i, n in enumerate(sm_names)}
    wts = {
        "meta_tokens": full["meta_tokens"], "mix_norm_w": mix_norm_w, "ffn_norm_w": ffn_norm_w,
        "ret_gn_w": ret_gn_w[0], "final_norm_w": final_norm_w, "dn_conv_w": full["dn_conv_w"][0],
        "dn_a_log": dn_a_log[0], "dn_dt_bias": dn_dt_bias[0], "dn_norm_w": full["dn_norm_w"][0],
    }
    idx = jnp.stack([mc, chip]).astype(jnp.int32)
    shards = {n: _halves(w[n].astype(MXU_DTYPE)) for n in BIG}
    loss_part, dh0, g, reduced = _train_step(x[0], loss_target[0], wts, shards, idx)
    seq = x.shape[1]
    grad_x = dh0[CHUNK:CHUNK + seq].reshape(x.shape)
    gsh = {}

    small_full_shapes = [g[n].shape for n in SMALL] + [(1,)]
    red = _unpack(_allreduce_small(_pack_lanes([g[n] for n in SMALL] + [loss_part[0, :1]])), small_full_shapes)
    loss = red[-1][0]
    for i, n in enumerate(SMALL):
        gn = red[i]
        if n in SMALL_SHARDED:
            width = w[n].shape[-1]
            gn = lax.dynamic_slice_in_dim(gn, chip * width, width, axis=gn.ndim - 1)
        gsh[n] = gn.reshape(w[n].shape)

    delta, new_m, new_v = {}, {}, {}
    for n in BIG:
        shp = w[n].shape
        if n in TRANSPOSED_AT_BOUNDARY and TRANSPOSED_AT_BOUNDARY[n]:
            view = lambda a: jnp.swapaxes(a, 1, 2).reshape(1, -1, LANES)
            back = lambda a: jnp.swapaxes(a.reshape(shp[0], shp[2], shp[1]), 1, 2)
        elif n in TRANSPOSED_AT_BOUNDARY:
            view = back = lambda a: jnp.swapaxes(a, 1, 2)
        else:
            view = back = lambda a: a
        res = _adamw(view(w[n]), view(reduced[n].reshape(shp)), view(m[n]), view(v[n]), "adamw_" + n)
        gsh[n], delta[n], new_m[n], new_v[n] = [back(r) for r in res]
    sm_local_shapes = [w[n].shape for n in SMALL]
    _, d_, m_, v_ = _adamw(*[_pack_lanes([t[n] for n in SMALL])[None] for t in (w, gsh, m, v)], "adamw_small")
    d_, m_, v_ = d_[0], m_[0], v_[0]
    for n, dd, mm, vv in zip(SMALL, _unpack(d_, sm_local_shapes), _unpack(m_, sm_local_shapes),
                             _unpack(v_, sm_local_shapes)):
        delta[n], new_m[n], new_v[n] = dd, mm, vv

    return (loss, grad_x, *[gsh[n] for n in ORDER], *[delta[n] for n in ORDER],
            *[new_m[n] for n in ORDER], *[new_v[n] for n in ORDER])
```

```python
import math

import jax
import jax.numpy as jnp
from jax import lax
from jax.experimental import pallas as pl
from jax.experimental.pallas import tpu as pltpu

F32 = jnp.float32
BF16 = jnp.bfloat16
MXU_DTYPE = BF16

D_MODEL = 1024
N_META = 16
CHUNK = 64
PAD = CHUNK - N_META
RMS_EPS = 1e-6
RET_HEADS, RET_DK, RET_DV = 4, 256, 512
RET_QK, RET_V = RET_HEADS * RET_DK, RET_HEADS * RET_DV
RET_IN = 2 * RET_QK + 2 * RET_V
ROPE_BASE = 10000.0
DN_HEADS, DN_DK, DN_DV = 8, 128, 256
DN_QK, DN_V = DN_HEADS * DN_DK, DN_HEADS * DN_DV
DN_CONV_CH = 2 * DN_QK + DN_V
DN_IN = DN_CONV_CH + DN_V + 2 * DN_HEADS
LANES = 128
DN_IN_USED = DN_CONV_CH + DN_V + LANES
DN_IN_PAD = DN_IN_USED + LANES
CONV_K = 4
FFN_HIDDEN = 2816
ADAM_LR, ADAM_B1, ADAM_B2, ADAM_EPS, ADAM_WD, ADAM_STEP = 0.001, 0.9, 0.999, 1e-08, 0.01, 10

ROW_ALIGN = 256
VMEM_LIMIT = 62 * 1024 * 1024
MESH = pl.DeviceIdType.MESH
ANY = pl.BlockSpec(memory_space=pl.ANY)
VMEM_SPEC = pl.BlockSpec(memory_space=pltpu.VMEM)
_HI = lax.Precision.HIGHEST


def _params(*sem):
    return pltpu.CompilerParams(dimension_semantics=sem, vmem_limit_bytes=VMEM_LIMIT)


def _dg(a, b, ca, cb, hi):
    dims = (((ca,), (cb,)), ((), ()))

    def dot(p, q):
        return lax.dot_general(p, q, dims, preferred_element_type=F32)

    if not hi:
        return dot(a.astype(MXU_DTYPE), b.astype(MXU_DTYPE))
    if MXU_DTYPE == F32:
        return lax.dot_general(a, b, dims, precision=_HI, preferred_element_type=F32)
    a_hi, b_hi = a.astype(MXU_DTYPE), b.astype(MXU_DTYPE)
    a_lo = (a - a_hi.astype(F32)).astype(MXU_DTYPE)
    b_lo = (b - b_hi.astype(F32)).astype(MXU_DTYPE)
    return dot(a_hi, b_hi) + (dot(a_hi, b_lo) + dot(a_lo, b_hi))


def _nn(a, b, hi=False):
    return _dg(a, b, 1, 0, hi)


def _nt(a, b, hi=False):
    return _dg(a, b, 1, 1, hi)


def _tn(a, b, hi=False):
    return _dg(a, b, 0, 0, hi)


def _iota(shape, dim):
    return lax.broadcasted_iota(jnp.int32, shape, dim)


def _valid_rows(first_row, rows, seq):
    r = first_row + _iota((rows, 1), 0)
    return ((r >= PAD) & (r < CHUNK + seq)).astype(F32)


def _rope(t, cs, sn):
    half = t.shape[-1] // 2
    t1, t2 = t[:, :half], t[:, half:]
    return jnp.concatenate([t1 * cs - t2 * sn, t1 * sn + t2 * cs], axis=1)


def _rope_bwd(d, cs, sn):
    half = d.shape[-1] // 2
    d1, d2 = d[:, :half], d[:, half:]
    return jnp.concatenate([d1 * cs + d2 * sn, d2 * cs - d1 * sn], axis=1)


def _col(x, idx):
    oh = (_iota((1, x.shape[1]), 1) == idx).astype(F32)
    return jnp.sum(x * oh, axis=1, keepdims=True)


def _row(x, idx):
    oh = (_iota((x.shape[0], 1), 0) == idx).astype(F32)
    return jnp.sum(x * oh, axis=0, keepdims=True)


def _shift_down(x, halo8, k):
    xr = pltpu.roll(x, k, 0)
    hr = pltpu.roll(halo8, k, 0)
    first = jnp.where(_iota((8, 1), 0) < k, hr, xr[0:8])
    return jnp.concatenate([first, xr[8:]], axis=0)


def _shift_up(x, next8, j):
    rows = x.shape[0]
    xr = pltpu.roll(x, rows - j, 0)
    nr = pltpu.roll(next8, 8 - j, 0)
    last = jnp.where(_iota((8, 1), 0) >= 8 - j, nr, xr[rows - 8:])
    return jnp.concatenate([xr[:rows - 8], last], axis=0)


def _gated_norm(o, gate, w):
    r = lax.rsqrt(jnp.mean(o * o, axis=-1, keepdims=True) + RMS_EPS)
    return o * r * w * (gate * jax.nn.sigmoid(gate))


def _gated_norm_bwd(dy, o, gate, w):
    r = lax.rsqrt(jnp.mean(o * o, axis=-1, keepdims=True) + RMS_EPS)
    nrm = o * r
    sg = jax.nn.sigmoid(gate)
    sl = gate * sg
    dgate = dy * nrm * w * (sg * (1.0 + gate * (1.0 - sg)))
    dn = dy * w * sl
    dw = jnp.sum(dy * nrm * sl, axis=0, keepdims=True)
    do = r * (dn - nrm * jnp.mean(dn * nrm, axis=-1, keepdims=True))
    return do, dgate, dw


def _softplus(z):
    return jnp.maximum(z, 0.0) + jnp.log(1.0 + jnp.exp(-jnp.abs(z)))


def _row_tile(rows, cap=768):
    for t in (768, 512, 256, 128, 64, 32, 16, 8):
        if t <= cap and rows % t == 0:
            return t
    return rows


TILE_BUDGET = 60 * 1024 * 1024


def _fit_rows(rows, row_bytes, fixed_bytes, value_row_bytes):
    best = None
    for t in range(LANES, rows + 1, LANES):
        if rows % t == 0 and 2 * (row_bytes * t + fixed_bytes) + value_row_bytes * t <= TILE_BUDGET:
            best = t
    return best or _row_tile(rows, 256)


def _div_tile(n, cap, mult):
    best = None
    for t in range(mult, min(cap, n) + 1, mult):
        if n % t == 0:
            best = t
    return best or n


def _col_tile(cols, cap=1536):
    best = None
    for t in range(LANES, min(cap, cols) + 1, LANES):
        if cols % t == 0:
            best = t
    return best or cols


def _embed(x, tgt, meta, rows, ride=None):
    seq, d = x.shape
    n_tok = seq // CHUNK

    def body(xa_ref, xb_ref, ta_ref, tb_ref, m_ref, h_ref, tp_ref):
        i = pl.program_id(0)
        first = jnp.concatenate([jnp.zeros((PAD, d), F32), m_ref[...]], axis=0)
        for half, (x_ref, t_ref) in enumerate(((xa_ref, ta_ref), (xb_ref, tb_ref))):
            k = 2 * i + half
            tokens = (k >= 1) & (k <= n_tok)
            rs = slice(half * CHUNK, (half + 1) * CHUNK)
            h_ref[rs, :] = jnp.where(k == 0, first, jnp.where(tokens, x_ref[...], 0.0))
            tp_ref[rs, :] = jnp.where(tokens, t_ref[...], 0.0)

    def tok(half):
        return pl.BlockSpec((CHUNK, d), lambda i: (jnp.clip(2 * i + half - 1, 0, n_tok - 1), 0))

    out = pl.BlockSpec((2 * CHUNK, d), lambda i: (i, 0))
    res, rode = _pcall(body, [x, x, tgt, tgt, meta], grid=(rows // (2 * CHUNK),),
                       in_specs=[tok(0), tok(1), tok(0), tok(1), pl.BlockSpec((N_META, d), lambda i: (0, 0))],
                       out_specs=[out, out], out_shape=[jax.ShapeDtypeStruct((rows, d), F32)] * 2, name="embed",
                       sem=("parallel",), ride=ride)
    return res if ride is None else (res, rode)


def _rms_fwd(h, w, name, ride=None):
    rows, d = h.shape
    tm = _row_tile(rows)

    def body(h_ref, w_ref, o_ref):
        x = h_ref[...]
        r = lax.rsqrt(jnp.mean(x * x, axis=-1, keepdims=True) + RMS_EPS)
        o_ref[...] = (x * r * w_ref[...]).astype(o_ref.dtype)

    res, rode = _pcall(body, [h, w.reshape(1, d)], grid=(rows // tm,),
                       in_specs=[pl.BlockSpec((tm, d), lambda i: (i, 0)), pl.BlockSpec((1, d), lambda i: (0, 0))],
                       out_specs=[pl.BlockSpec((tm, d), lambda i: (i, 0))],
                       out_shape=[jax.ShapeDtypeStruct((rows, d), BF16)], name=name, sem=("parallel",), ride=ride)
    return res[0] if ride is None else (res[0], rode)


def _gmm_rms(name, grid, args, in_specs, row_spec, fn, h, w, resid, row_axis, red_axis=None, ride=None):
    m, d = h.shape
    n_in = len(args)
    vec = pl.BlockSpec((1, d), lambda *g: (0, 0))

    def body(*refs):
        ins = refs[:n_in]
        h_ref, w_ref, r_ref, dh_ref, dw_ref, dh16_ref = refs[n_in:]
        part = fn(*ins)
        row = pl.program_id(row_axis)

        def finish(dy):
            x = h_ref[...]
            r = lax.rsqrt(jnp.mean(x * x, axis=-1, keepdims=True) + RMS_EPS)
            xh = x * r
            dxh = dy * w_ref[...]
            dh = r_ref[...] + r * (dxh - xh * jnp.mean(dxh * xh, axis=-1, keepdims=True))
            dh_ref[...] = dh
            dh16_ref[...] = dh.astype(dh16_ref.dtype)
            dwp = jnp.sum(dy * xh, axis=0, keepdims=True)

            @pl.when(row == 0)
            def _():
                dw_ref[...] = dwp

            @pl.when(row > 0)
            def _():
                dw_ref[...] += dwp

        if red_axis is None:
            finish(part)
            return
        k = pl.program_id(red_axis)

        @pl.when(k == 0)
        def _():
            dh_ref[...] = part

        @pl.when(k > 0)
        def _():
            dh_ref[...] += part

        @pl.when(k == grid[red_axis] - 1)
        def _():
            finish(dh_ref[...])

    res, rode = _pcall(body, list(args) + [h, w.reshape(1, d), resid], grid=grid,
                       in_specs=list(in_specs) + [row_spec, vec, row_spec], out_specs=[row_spec, vec, row_spec],
                       out_shape=[jax.ShapeDtypeStruct((m, d), F32), jax.ShapeDtypeStruct((1, d), F32),
                                  jax.ShapeDtypeStruct((m, d), BF16)],
                       name=name, sem=("arbitrary",) * len(grid), ride=ride)
    return res if ride is None else (res, rode)


def _final_loss(h, w, tgt, seq, name):
    rows, d = h.shape
    tm = _row_tile(rows)

    def body(h_ref, w_ref, t_ref, dh_ref, dw_ref, loss_ref, dh16_ref):
        i = pl.program_id(0)
        r_idx = i * tm + _iota((tm, 1), 0)
        m = ((r_idx >= CHUNK) & (r_idx < CHUNK + seq)).astype(F32)
        x = h_ref[...]
        wv = w_ref[...]
        r = lax.rsqrt(jnp.mean(x * x, axis=-1, keepdims=True) + RMS_EPS)
        xh = x * r
        err = (xh * wv - t_ref[...]) * m
        lpart = 0.5 * jnp.sum(jnp.mean(err * err, axis=-1, keepdims=True), axis=0, keepdims=True)
        dyv = err * (1.0 / d)
        dxh = dyv * wv
        dh = r * (dxh - xh * jnp.mean(dxh * xh, axis=-1, keepdims=True))
        dh_ref[...] = dh
        dh16_ref[...] = dh.astype(dh16_ref.dtype)
        part = jnp.sum(dyv * xh, axis=0, keepdims=True)

        @pl.when(i == 0)
        def _():
            dw_ref[...] = part
            loss_ref[...] = jnp.broadcast_to(lpart, loss_ref.shape)

        @pl.when(i > 0)
        def _():
            dw_ref[...] += part
            loss_ref[...] += jnp.broadcast_to(lpart, loss_ref.shape)

    blk = pl.BlockSpec((tm, d), lambda i: (i, 0))
    vec = pl.BlockSpec((1, d), lambda i: (0, 0))
    return pl.pallas_call(
        body, grid=(rows // tm,), in_specs=[blk, vec, blk],
        out_specs=[blk, vec, pl.BlockSpec((1, LANES), lambda i: (0, 0)), blk],
        out_shape=[jax.ShapeDtypeStruct((rows, d), F32), jax.ShapeDtypeStruct((1, d), F32),
                   jax.ShapeDtypeStruct((1, LANES), F32), jax.ShapeDtypeStruct((rows, d), BF16)],
        name=name, compiler_params=_params("arbitrary"))(h, w.reshape(1, d), tgt)


def _isz(x):
    return jnp.dtype(x.dtype).itemsize


def _mm(a, b, *, mode, name, out_dtype=F32, resid=None, col_cap=1536, ride=None):
    if mode == "tn":
        m, k = a.shape
        n = b.shape[1]
        tn = _col_tile(n, col_cap)
        tm = _fit_rows(m, k * _isz(a) + tn * _isz(b), (3 * k * tn * 4) // 2, 2 * (k + tn))

        def body_tn(a_ref, b_ref, o_ref):
            i = pl.program_id(1)
            part = _tn(a_ref[...], b_ref[...])

            @pl.when(i == 0)
            def _():
                o_ref[...] = part

            @pl.when(i > 0)
            def _():
                o_ref[...] += part

        return pl.pallas_call(
            body_tn, grid=(n // tn, m // tm),
            in_specs=[pl.BlockSpec((tm, k), lambda j, i: (i, 0)),
                      pl.BlockSpec((tm, tn), lambda j, i: (i, j))],
            out_specs=pl.BlockSpec((k, tn), lambda j, i: (0, j)),
            out_shape=jax.ShapeDtypeStruct((k, n), F32), name=name,
            compiler_params=_params("parallel", "arbitrary"))(a, b)

    m, ka = a.shape
    n = b.shape[1] if mode == "nn" else b.shape[0]
    has_resid = resid is not None
    tn = _col_tile(n, col_cap)
    tm = _fit_rows(m, ka * _isz(a) + tn * (jnp.dtype(out_dtype).itemsize + (4 if has_resid else 0)),
                   ka * tn * _isz(b), 2 * ka + 8 * tn)

    def body(*refs):
        if has_resid:
            a_ref, b_ref, r_ref, o_ref = refs
        else:
            a_ref, b_ref, o_ref = refs
        acc = _nn(a_ref[...], b_ref[...]) if mode == "nn" else _nt(a_ref[...], b_ref[...])
        if has_resid:
            acc = acc + r_ref[...]
        o_ref[...] = acc.astype(o_ref.dtype)

    b_spec = (pl.BlockSpec((b.shape[0], tn), lambda j, i: (0, j)) if mode == "nn"
              else pl.BlockSpec((tn, b.shape[1]), lambda j, i: (j, 0)))
    o_spec = pl.BlockSpec((tm, tn), lambda j, i: (i, j))
    in_specs = [pl.BlockSpec((tm, ka), lambda j, i: (i, 0)), b_spec]
    args = [a, b]
    if has_resid:
        in_specs.append(o_spec)
        args.append(resid)
    res, rode = _pcall(body, args, grid=(n // tn, m // tm), in_specs=in_specs, out_specs=[o_spec],
                       out_shape=[jax.ShapeDtypeStruct((m, n), out_dtype)], name=name,
                       sem=("parallel", "parallel"), ride=ride)
    return res[0] if ride is None else (res[0], rode)


N_SHARD = 4


def _gmm(name, grid, args, in_specs, out_specs, out_shape, fn, red_axis=None, init_arg=None, aliases=None,
         ride=None):
    n_in = len(args)
    single = not isinstance(out_shape, (list, tuple))
    out_specs = [out_specs] if single else list(out_specs)
    out_shape = [out_shape] if single else list(out_shape)

    def body(*refs):
        _gmm_step(fn, refs[:n_in], refs[n_in:], red_axis, init_arg)

    sem = tuple("arbitrary" if ax == red_axis else "parallel" for ax in range(len(grid)))
    res, rode = _pcall(body, args, grid=grid, in_specs=in_specs, out_specs=out_specs, out_shape=out_shape,
                       name=name, sem=sem, aliases=aliases, ride=ride)
    ours = res[0] if single else res
    return ours if ride is None else (ours, rode)


def _gmm_step(fn, ins, outs, red_axis, init_arg):
    parts = fn(*ins)
    if red_axis is None:
        for o_ref, p in zip(outs, parts):
            o_ref[...] = p.astype(o_ref.dtype)
        return
    k = pl.program_id(red_axis)

    @pl.when(k == 0)
    def _():
        for idx, (o_ref, p) in enumerate(zip(outs, parts)):
            o_ref[...] = p + ins[init_arg][...] if (idx == 0 and init_arg is not None) else p

    @pl.when(k > 0)
    def _():
        for o_ref, p in zip(outs, parts):
            o_ref[...] += p


def _ride_body(ride, grid, n_in, n_out, n_scratch, body):
    n_rin, n_rout = len(ride.arrays), len(ride.out_shape)
    nsteps = math.prod(grid)

    def wrapped(*refs):
        ins = refs[:n_in]
        r_ins = refs[n_in:n_in + n_rin]
        o0 = n_in + n_rin
        outs = refs[o0:o0 + n_out]
        r_outs = refs[o0 + n_out:o0 + n_out + n_rout]
        s0 = o0 + n_out + n_rout
        scratch = refs[s0:s0 + n_scratch]
        send_sems, recv_sems = refs[-2:]
        step = pl.program_id(0)
        for ax in range(1, len(grid)):
            step = step * grid[ax] + pl.program_id(ax)
        ride.emit(step, nsteps, r_ins, r_outs, send_sems, recv_sems, before=True)
        body(*ins, *outs, *scratch)
        ride.emit(step, nsteps, r_ins, r_outs, send_sems, recv_sems, before=False)

    return wrapped


def _pcall(body, args, *, grid, in_specs, out_specs, out_shape, name, sem, scratch=(), aliases=None, ride=None):
    if ride is None:
        res = pl.pallas_call(body, grid=grid, in_specs=list(in_specs), out_specs=list(out_specs),
                             out_shape=list(out_shape), scratch_shapes=list(scratch), name=name,
                             input_output_aliases=aliases or {}, compiler_params=_params(*sem))(*args)
        return res, None
    n_in, n_out = len(args), len(out_shape)
    res = pl.pallas_call(
        _ride_body(ride, grid, n_in, n_out, len(scratch), body), grid=grid,
        in_specs=list(in_specs) + ride.in_specs, out_specs=list(out_specs) + ride.out_specs,
        out_shape=list(out_shape) + ride.out_shape, scratch_shapes=list(scratch) + ride.scratch, name=name,
        input_output_aliases=aliases or {},
        compiler_params=_params(*(("arbitrary",) * len(grid))))(*args, *ride.arrays)
    return res[:n_out], res[n_out:]


def _mm_cols(a, ws, name, ride=None):
    m, k = a.shape
    n = ws.shape[2]
    tm = _fit_rows(m, k * _isz(a) + n * 4, k * n * _isz(ws), 4 * n)
    return _gmm(name, (N_SHARD, m // tm), [a, ws],
                [pl.BlockSpec((tm, k), lambda j, i: (i, 0)), pl.BlockSpec((None, k, n), lambda j, i: (j, 0, 0))],
                pl.BlockSpec((tm, n), lambda j, i: (i, j)), jax.ShapeDtypeStruct((m, N_SHARD * n), F32),
                lambda a_ref, w_ref: (_nn(a_ref[...], w_ref[...]),), ride=ride)


def _mm_cols_t_rms(d, ws, h, w, resid, name, ride=None):
    m = d.shape[0]
    _, k, n = ws.shape
    tm = _fit_rows(m, n * _isz(d) + 3 * k * 4, k * n * _isz(ws), 16 * k)
    return _gmm_rms(name, (m // tm, N_SHARD), [d, ws],
                    [pl.BlockSpec((tm, n), lambda i, j: (i, j)), pl.BlockSpec((None, k, n), lambda i, j: (j, 0, 0))],
                    pl.BlockSpec((tm, k), lambda i, j: (i, 0)),
                    lambda d_ref, w_ref: _nt(d_ref[...], w_ref[...]), h, w, resid, 0, red_axis=1, ride=ride)


def _mm_nt_rms(a, b, h, w, resid, name, ride=None):
    m, n = a.shape
    k = b.shape[0]
    tm = _fit_rows(m, n * _isz(a) + 3 * k * 4, k * n * _isz(b), 16 * k)
    return _gmm_rms(name, (m // tm,), [a, b],
                    [pl.BlockSpec((tm, n), lambda i: (i, 0)), pl.BlockSpec((k, n), lambda i: (0, 0))],
                    pl.BlockSpec((tm, k), lambda i: (i, 0)),
                    lambda a_ref, b_ref: _nt(a_ref[...], b_ref[...]), h, w, resid, 0, ride=ride)


def _mm_cols_grad(a, d, name):
    m, k = a.shape
    n = d.shape[1] // N_SHARD
    tm = _fit_rows(m, k * _isz(a) + n * _isz(d), (3 * k * n * 4) // 2, 2 * (k + n))
    return _gmm(name, (N_SHARD, m // tm), [a, d],
                [pl.BlockSpec((tm, k), lambda j, i: (i, 0)), pl.BlockSpec((tm, n), lambda j, i: (i, j))],
                pl.BlockSpec((None, k, n), lambda j, i: (j, 0, 0)), jax.ShapeDtypeStruct((N_SHARD, k, n), F32),
                lambda a_ref, d_ref: (_tn(a_ref[...], d_ref[...]),), red_axis=1)


def _ffn_up(hn, wg, wu, layer, name, ride=None):
    m, k = hn.shape
    n = wg.shape[3]
    tm = _fit_rows(m, k * _isz(hn) + 3 * n * jnp.dtype(BF16).itemsize, 2 * k * n * _isz(wg), 16 * n)

    def fn(a_ref, wg_ref, wu_ref):
        a = a_ref[...]
        g = _nn(a, wg_ref[...])
        u = _nn(a, wu_ref[...])
        return g, u, g * jax.nn.sigmoid(g) * u

    w_spec = pl.BlockSpec((None, None, k, n), lambda j, i: (j, layer, 0, 0))
    o_spec = pl.BlockSpec((None, tm, n), lambda j, i: (j, i, 0))
    out = jax.ShapeDtypeStruct((N_SHARD, m, n), BF16)
    return _gmm(name, (N_SHARD, m // tm), [hn, wg, wu],
                [pl.BlockSpec((tm, k), lambda j, i: (i, 0)), w_spec, w_spec],
                [o_spec, o_spec, o_spec], [out, out, out], fn, ride=ride)


def _ffn_down(act, wd, resid, layer, name, ride=None):
    _, m, n = act.shape
    d = wd.shape[3]
    tm = _fit_rows(m, N_SHARD * n * _isz(act) + 2 * d * 4, N_SHARD * n * d * _isz(wd), 8 * d)

    def fn(a_ref, w_ref, r_ref):
        acc = r_ref[...]
        for j in range(N_SHARD):
            acc = acc + _nn(a_ref[j], w_ref[j])
        return (acc,)

    row = pl.BlockSpec((tm, d), lambda i: (i, 0))
    return _gmm(name, (m // tm,), [act, wd, resid],
                [pl.BlockSpec((N_SHARD, tm, n), lambda i: (0, i, 0)),
                 pl.BlockSpec((N_SHARD, None, n, d), lambda i: (0, layer, 0, 0)), row],
                row, jax.ShapeDtypeStruct((m, d), F32), fn, ride=ride)


def _ffn_down_bwd(dh, wd, g, u, layer, name, ride=None):
    m, d = dh.shape
    n = wd.shape[2]
    tm = _fit_rows(m, d * _isz(dh) + 4 * N_SHARD * n * jnp.dtype(BF16).itemsize, N_SHARD * n * d * _isz(wd),
                   2 * d + 24 * n)

    def body(dh_ref, wd_ref, g_ref, u_ref, dg_ref, du_ref):
        dhv = dh_ref[...].astype(MXU_DTYPE)
        for j in range(N_SHARD):
            dact = _nt(dhv, wd_ref[j])
            gv = g_ref[j].astype(F32)
            sg = jax.nn.sigmoid(gv)
            gs = gv * sg
            dg_ref[j] = (dact * u_ref[j].astype(F32) * (sg + gs * (1.0 - sg))).astype(dg_ref.dtype)
            du_ref[j] = (dact * gs).astype(du_ref.dtype)

    sh_spec = pl.BlockSpec((N_SHARD, tm, n), lambda i: (0, i, 0))
    out = jax.ShapeDtypeStruct((N_SHARD, m, n), BF16)
    res, rode = _pcall(body, [dh, wd, g, u], grid=(m // tm,),
                       in_specs=[pl.BlockSpec((tm, d), lambda i: (i, 0)),
                                 pl.BlockSpec((N_SHARD, None, n, d), lambda i: (0, layer, 0, 0)), sh_spec, sh_spec],
                       out_specs=[sh_spec, sh_spec], out_shape=[out, out], name=name, sem=("parallel",), ride=ride)
    return res if ride is None else (res, rode)


def _ffn_up_bwd(dg, du, wg, wu, layer, h, w, resid, name, ride=None):
    _, m, n = dg.shape
    k = wg.shape[2]
    tm = _fit_rows(m, 2 * N_SHARD * n * _isz(dg) + 3 * k * 4, 2 * N_SHARD * k * n * _isz(wg), 16 * k)

    def fn(dg_ref, du_ref, wg_ref, wu_ref):
        acc = _nt(dg_ref[0], wg_ref[0]) + _nt(du_ref[0], wu_ref[0])
        for j in range(1, N_SHARD):
            acc = acc + _nt(dg_ref[j], wg_ref[j]) + _nt(du_ref[j], wu_ref[j])
        return acc

    d_spec = pl.BlockSpec((N_SHARD, tm, n), lambda i: (0, i, 0))
    w_spec = pl.BlockSpec((N_SHARD, None, k, n), lambda i: (0, layer, 0, 0))
    return _gmm_rms(name, (m // tm,), [dg, du, wg, wu], [d_spec, d_spec, w_spec, w_spec],
                    pl.BlockSpec((tm, k), lambda i: (i, 0)), fn, h, w, resid, 0, ride=ride)


def _ffn_wgrad(lhs, rhs_list, layer, layers, prev, lhs_sharded, name):
    if lhs_sharded:
        _, m, k = lhs.shape
        n = rhs_list[0].shape[1]
    else:
        m, k = lhs.shape
        n = rhs_list[0].shape[2]
    n_out = len(rhs_list)
    tm = _fit_rows(m, k * _isz(lhs) + n_out * n * _isz(rhs_list[0]), (3 * n_out * k * n * 4) // 2,
                   2 * (k + n_out * n))
    sh = pl.BlockSpec((None, tm, k if lhs_sharded else n), lambda j, i: (j, i, 0))
    fl = pl.BlockSpec((tm, n if lhs_sharded else k), lambda j, i: (i, 0))
    n_out = len(rhs_list)
    args = [lhs] + list(rhs_list)
    in_specs = [sh if lhs_sharded else fl] + [fl if lhs_sharded else sh] * n_out
    aliases = None
    if prev is not None:
        aliases = {len(args) + t: t for t in range(n_out)}
        args = args + list(prev)
        in_specs = in_specs + [ANY] * n_out

    def fn(l_ref, *rest):
        lv = l_ref[...]
        return tuple(_tn(lv, r_ref[...]) for r_ref in rest[:n_out])

    o_spec = pl.BlockSpec((None, None, k, n), lambda j, i: (j, layer, 0, 0))
    out = jax.ShapeDtypeStruct((N_SHARD, layers, k, n), F32)
    return _gmm(name, (N_SHARD, m // tm), args, in_specs, [o_spec] * n_out, [out] * n_out, fn,
                red_axis=1, aliases=aliases)


def _ret_consts():
    log_gamma = jnp.log1p(-jnp.exp2(-5.0 - jnp.arange(RET_HEADS, dtype=F32)))
    idx = jnp.arange(CHUNK, dtype=F32)
    rel = idx[:, None] - idx[None, :]
    dmask = jnp.where((rel >= 0)[None], jnp.exp(log_gamma[:, None, None] * jnp.maximum(rel, 0.0)), 0.0)
    xi = jnp.exp(log_gamma[:, None] * (idx[None, :] + 1.0))[:, :, None]
    zeta = jnp.exp(log_gamma[:, None] * (CHUNK - 1.0 - idx[None, :]))[:, :, None]
    gamma_c = jnp.exp(log_gamma * CHUNK)
    wide = (RET_HEADS, CHUNK, RET_DK)
    return dmask, jnp.broadcast_to(xi, wide), jnp.broadcast_to(zeta, wide), gamma_c


def _rope_tables(nc):
    half = RET_DK // 2
    inv_freq = ROPE_BASE ** (-jnp.arange(half, dtype=F32) / half)
    a_chunk = (jnp.arange(nc) * CHUNK - PAD).astype(F32)[:, None] * inv_freq[None, :]
    a_row = jnp.arange(CHUNK).astype(F32)[:, None] * inv_freq[None, :]
    return (jnp.stack([jnp.cos(a_chunk), jnp.sin(a_chunk)], axis=1),
            jnp.stack([jnp.cos(a_row), jnp.sin(a_row)], axis=0))


RET_CPS = 4


def _rope_chunk(rc_ref, rr_ref, c):
    cc, sc = rc_ref[c, 0:1, :], rc_ref[c, 1:2, :]
    cr, sr = rr_ref[0], rr_ref[1]
    return cc * cr - sc * sr, sc * cr + cc * sr


def _rope_specs(order):
    half = RET_DK // 2
    return [pl.BlockSpec((RET_CPS, 2, half), lambda n: (order(n), 0, 0)),
            pl.BlockSpec((2, CHUNK, half), lambda n: (0, 0, 0))]


def _ret_specs(order):
    rows = RET_CPS * CHUNK
    return [pl.BlockSpec((rows, RET_QK), lambda n: (order(n), 0)),
            pl.BlockSpec((rows, RET_QK), lambda n: (order(n), 1)),
            pl.BlockSpec((rows, RET_V), lambda n: (order(n), 1)),
            pl.BlockSpec((rows, RET_V), lambda n: (order(n), 2))]


def _ret_const_specs():
    return [pl.BlockSpec((RET_HEADS, CHUNK, CHUNK), lambda n: (0, 0, 0)),
            pl.BlockSpec((RET_HEADS, CHUNK, RET_DK), lambda n: (0, 0, 0)),
            pl.BlockSpec((RET_HEADS, CHUNK, RET_DK), lambda n: (0, 0, 0)),
            pl.BlockSpec((1, RET_DV), lambda n: (0, 0))]


def _ret_fwd(proj, cos, sin, consts, gn_w, seq, ride=None):
    rows = proj.shape[0]
    nc = rows // CHUNK
    dmask, xi, zeta, gamma_c = consts

    def body(gam_ref, q_ref, k_ref, v_ref, g_ref, cos_ref, sin_ref, dm_ref, xi_ref, ze_ref, gn_ref,
             o_ref, y_ref, ss_ref, s_ref):
        n = pl.program_id(0)

        @pl.when(n == 0)
        def _():
            s_ref[...] = jnp.zeros_like(s_ref)

        gn = gn_ref[...]
        hs = range(RET_HEADS)
        qk_cols = [slice(h * RET_DK, (h + 1) * RET_DK) for h in hs]
        v_cols = [slice(h * RET_DV, (h + 1) * RET_DV) for h in hs]
        for c in range(RET_CPS):
            rs = slice(c * CHUNK, (c + 1) * CHUNK)
            cs, sn = _rope_chunk(cos_ref, sin_ref, c)
            kscale = _valid_rows((n * RET_CPS + c) * CHUNK, CHUNK, seq) * (RET_DK ** -0.5)
            qr_l = [_rope(q_ref[rs, col], cs, sn) for col in qk_cols]
            kr_l = [_rope(k_ref[rs, col], cs, sn) * kscale for col in qk_cols]
            v_l = [v_ref[rs, col] for col in v_cols]
            s_l = [s_ref[h] for h in hs]
            sc_l = [_nt(qr, kr) * dm_ref[h] for h, (qr, kr) in enumerate(zip(qr_l, kr_l))]
            o_l = [_nn(sc_l[h], v_l[h]) + _nn(qr_l[h] * xi_ref[h], s_l[h]) for h in hs]
            for h in hs:
                ss_ref[c, h] = s_l[h].astype(ss_ref.dtype)
                s_ref[h] = gam_ref[h] * s_l[h] + _tn(kr_l[h] * ze_ref[h], v_l[h])
                o_ref[rs, v_cols[h]] = o_l[h]
                y_ref[rs, v_cols[h]] = _gated_norm(o_l[h], g_ref[rs, v_cols[h]], gn).astype(y_ref.dtype)

    fwd = lambda n: n
    row_v = pl.BlockSpec((RET_CPS * CHUNK, RET_V), lambda n: (n, 0))
    res, rode = _pcall(
        body, [gamma_c, proj, proj, proj, proj, cos, sin, dmask, xi, zeta, gn_w.reshape(1, RET_DV)],
        grid=(nc // RET_CPS,),
        in_specs=[pl.BlockSpec(memory_space=pltpu.SMEM)] + _ret_specs(fwd) + _rope_specs(fwd)
        + _ret_const_specs(),
        out_specs=[row_v, row_v,
                   pl.BlockSpec((RET_CPS, RET_HEADS, RET_DK, RET_DV), lambda n: (n, 0, 0, 0))],
        out_shape=[jax.ShapeDtypeStruct((rows, RET_V), F32), jax.ShapeDtypeStruct((rows, RET_V), BF16),
                   jax.ShapeDtypeStruct((nc, RET_HEADS, RET_DK, RET_DV), BF16)],
        scratch=[pltpu.VMEM((RET_HEADS, RET_DK, RET_DV), F32)], name="ret_fwd", sem=("arbitrary",), ride=ride)
    return res if ride is None else (res, rode)


def _ret_bwd(proj, o, dy, states, cos, sin, consts, gn_w, seq, ride=None):
    rows = proj.shape[0]
    nc = rows // CHUNK
    dmask, xi, zeta, gamma_c = consts

    def body(gam_ref, q_ref, k_ref, v_ref, g_ref, o_ref, dy_ref, ss_ref, cos_ref, sin_ref,
             dm_ref, xi_ref, ze_ref, gn_ref, dp_ref, dgn_ref, ds_ref):
        n = pl.program_id(0)

        @pl.when(n == 0)
        def _():
            ds_ref[...] = jnp.zeros_like(ds_ref)
            dgn_ref[...] = jnp.zeros_like(dgn_ref)

        gn = gn_ref[...]
        dgn = jnp.zeros((1, RET_DV), F32)
        hs = range(RET_HEADS)
        qk_cols = [slice(h * RET_DK, (h + 1) * RET_DK) for h in hs]
        v_cols = [slice(h * RET_DV, (h + 1) * RET_DV) for h in hs]
        for c in reversed(range(RET_CPS)):
            rs = slice(c * CHUNK, (c + 1) * CHUNK)
            cs, sn = _rope_chunk(cos_ref, sin_ref, c)
            kscale = _valid_rows(((steps - 1 - n) * RET_CPS + c) * CHUNK, CHUNK, seq) * (RET_DK ** -0.5)
            qr_l = [_rope(q_ref[rs, col], cs, sn) for col in qk_cols]
            kr_l = [_rope(k_ref[rs, col], cs, sn) * kscale for col in qk_cols]
            v_l = [v_ref[rs, col] for col in v_cols]
            s_l = [ss_ref[c, h] for h in hs]
            ds_l = [ds_ref[h] for h in hs]
            sc_l = [_nt(qr_l[h], kr_l[h]) * dm_ref[h] for h in hs]
            gnb = [_gated_norm_bwd(dy_ref[rs, col], o_ref[rs, col], g_ref[rs, col], gn) for col in v_cols]
            do_l = [x[0] for x in gnb]
            dsc_l = [_nt(do_l[h], v_l[h]) * dm_ref[h] for h in hs]
            dv_l = [_tn(sc_l[h], do_l[h]) + _nn(kr_l[h] * ze_ref[h], ds_l[h]) for h in hs]
            dqr_l = [_nn(dsc_l[h], kr_l[h]) + _nt(do_l[h], s_l[h]) * xi_ref[h] for h in hs]
            dkr_l = [_tn(dsc_l[h], qr_l[h]) + _nt(v_l[h], ds_l[h]) * ze_ref[h] for h in hs]
            for h in hs:
                dgn = dgn + gnb[h][2]
                ds_ref[h] = gam_ref[h] * ds_l[h] + _tn(qr_l[h] * xi_ref[h], do_l[h])
                dp_ref[rs, qk_cols[h]] = _rope_bwd(dqr_l[h], cs, sn).astype(dp_ref.dtype)
                dp_ref[rs, RET_QK + h * RET_DK:RET_QK + (h + 1) * RET_DK] = (
                    _rope_bwd(dkr_l[h] * kscale, cs, sn).astype(dp_ref.dtype))
                dp_ref[rs, 2 * RET_QK + h * RET_DV:2 * RET_QK + (h + 1) * RET_DV] = dv_l[h].astype(dp_ref.dtype)
                dp_ref[rs, 2 * RET_QK + RET_V + h * RET_DV:2 * RET_QK + RET_V + (h + 1) * RET_DV] = (
                    gnb[h][1].astype(dp_ref.dtype))
        dgn_ref[...] += dgn

    steps = nc // RET_CPS
    rev = lambda n: steps - 1 - n
    row_v = pl.BlockSpec((RET_CPS * CHUNK, RET_V), lambda n: (rev(n), 0))
    res, rode = _pcall(
        body, [gamma_c, proj, proj, proj, proj, o, dy, states, cos, sin, dmask, xi, zeta,
               gn_w.reshape(1, RET_DV)],
        grid=(steps,),
        in_specs=[pl.BlockSpec(memory_space=pltpu.SMEM)] + _ret_specs(rev) + [
            row_v, row_v, pl.BlockSpec((RET_CPS, RET_HEADS, RET_DK, RET_DV), lambda n: (rev(n), 0, 0, 0))]
        + _rope_specs(rev) + _ret_const_specs(),
        out_specs=[pl.BlockSpec((RET_CPS * CHUNK, RET_IN), lambda n: (rev(n), 0)),
                   pl.BlockSpec((1, RET_DV), lambda n: (0, 0))],
        out_shape=[jax.ShapeDtypeStruct((rows, RET_IN), BF16), jax.ShapeDtypeStruct((1, RET_DV), F32)],
        scratch=[pltpu.VMEM((RET_HEADS, RET_DK, RET_DV), F32)], name="ret_bwd", sem=("arbitrary",), ride=ride)
    return res if ride is None else (res, rode)


GATE_COL = DN_CONV_CH // DN_V
BA_COL = (DN_CONV_CH + DN_V) // LANES
BETA_LANE, DECAY_LANE = 0, DN_HEADS
INV_SHIFT = 4
INV_SQUARINGS = INV_SHIFT - 1
assert CHUNK == 4 << INV_SHIFT


DN_CPS = 2


def _dn_in_specs(order, conv_saved=False):
    rows = DN_CPS * CHUNK
    return [pl.BlockSpec((rows, DN_CONV_CH), lambda n: (order(n), 0)),
            pl.BlockSpec((rows, DN_CONV_CH), lambda n: (order(n), 0)) if conv_saved else
            pl.BlockSpec((8, DN_CONV_CH), lambda n: (jnp.maximum(order(n) * (rows // 8) - 1, 0), 0)),
            pl.BlockSpec((rows, DN_V), lambda n: (order(n), GATE_COL)),
            pl.BlockSpec((rows, LANES), lambda n: (order(n), BA_COL)),
            pl.BlockSpec((CONV_K, 1, DN_CONV_CH), lambda n: (0, 0, 0)),
            pl.BlockSpec((1, LANES), lambda n: (0, 0)),
            pl.BlockSpec((1, LANES), lambda n: (0, 0)),
            pl.BlockSpec((1, DN_DV), lambda n: (0, 0))]


def _dn_front(c, seq, x, halo, ba, cw_ref, al_ref, dt_ref, yc=None):
    valid = _valid_rows(c * CHUNK, CHUNK, seq)
    xin = x * valid
    if yc is None:
        halo = halo * _valid_rows(c * CHUNK - 8, 8, seq)
        yc = xin * cw_ref[CONV_K - 1]
        for k in range(1, CONV_K):
            yc = yc + _shift_down(xin, halo, k) * cw_ref[CONV_K - 1 - k]
    sgc = jax.nn.sigmoid(yc)
    sig = jax.nn.sigmoid(ba)
    beta = sig * valid
    z = ba + dt_ref[...]
    eal = jnp.exp(al_ref[...])
    g = -eal * _softplus(z) * valid
    ri, ci = _iota((CHUNK, CHUNK), 0), _iota((CHUNK, CHUNK), 1)
    lower = (ri >= ci).astype(F32)
    upper = (ri <= ci).astype(F32)
    eye = (ri == ci).astype(F32)
    gam = _nn(lower, g, hi=True)
    gam_t = _tn(g, upper, hi=True)
    return dict(valid=valid, xin=xin, yc=yc, sgc=sgc, act=yc * sgc, sig=sig, beta=beta, z=z,
                eal=eal, g=g, gam=gam, gam_t=gam_t, ri=ri, ci=ci, upper=upper, eye=eye)


def _dn_head(f, h):
    act = f["act"]
    q_raw = act[:, h * DN_DK:(h + 1) * DN_DK]
    k_raw = act[:, DN_QK + h * DN_DK:DN_QK + (h + 1) * DN_DK]
    v = act[:, 2 * DN_QK + h * DN_DV:2 * DN_QK + (h + 1) * DN_DV]
    rq = lax.rsqrt(jnp.sum(q_raw * q_raw, axis=-1, keepdims=True) + RMS_EPS)
    rk = lax.rsqrt(jnp.sum(k_raw * k_raw, axis=-1, keepdims=True) + RMS_EPS)
    qh = q_raw * rq
    kn = k_raw * rk
    gam_c = _col(f["gam"], DECAY_LANE + h)
    gam_r = _row(f["gam_t"], DECAY_LANE + h)
    bc = _col(f["beta"], BETA_LANE + h)
    diff = gam_c - gam_r
    decay = jnp.where(f["ri"] >= f["ci"], jnp.exp(jnp.minimum(diff, 0.0)), 0.0)
    glast = jnp.sum(gam_r * (_iota((1, CHUNK), 1) == CHUNK - 1).astype(F32), axis=1, keepdims=True)
    return dict(rq=rq, rk=rk, qh=qh, qn=qh * (DN_DK ** -0.5), kn=kn, v=v, gam_c=gam_c, gam_r=gam_r,
                bc=bc, diff=diff, decay=decay, egam=jnp.exp(gam_c), glast=glast,
                eglast=jnp.exp(glast), ekd=jnp.exp(glast - gam_c))


def _dn_fwd(proj, conv_w, alog, dtb, norm_w, seq):
    rows = proj.shape[0]
    nc = rows // CHUNK

    def body(x_ref, halo_ref, gate_ref, ba_ref, cw_ref, al_ref, dt_ref, nw_ref,
             o_ref, y_ref, ss_ref, t_ref, yc_ref, s_ref):
        n = pl.program_id(0)

        @pl.when(n == 0)
        def _():
            s_ref[...] = jnp.zeros_like(s_ref)

        nw = nw_ref[...]
        pre = []
        for c in range(DN_CPS):
            rs = slice(c * CHUNK, (c + 1) * CHUNK)
            halo = halo_ref[...] if c == 0 else x_ref[c * CHUNK - 8:c * CHUNK, :]
            f = _dn_front(n * DN_CPS + c, seq, x_ref[rs, :], halo, ba_ref[rs, :], cw_ref, al_ref, dt_ref)
            yc_ref[rs, :] = f["yc"]
            ri, ci = f["ri"], f["ci"]
            eye = f["eye"]
            diag_m = (jnp.right_shift(ri, INV_SHIFT) == jnp.right_shift(ci, INV_SHIFT)).astype(F32)
            half_m = (jnp.right_shift(ri, INV_SHIFT + 1) == jnp.right_shift(ci, INV_SHIFT + 1)).astype(F32)
            heads = [_dn_head(f, h) for h in range(DN_HEADS)]
            a_all = [jnp.where(ri > ci, hd["bc"] * _nt(hd["kn"], hd["kn"]) * hd["decay"], 0.0) for hd in heads]
            b_all = [a * diag_m for a in a_all]
            t_all = [eye - b for b in b_all]
            for _ in range(INV_SQUARINGS):
                b_all = [_nn(b, b, hi=True) for b in b_all]
                t_all = [t + _nn(t, b, hi=True) for t, b in zip(t_all, b_all)]
            for off_m in (half_m - diag_m, 1.0 - half_m):
                x_all = [_nn(a * off_m, t, hi=True) for a, t in zip(a_all, t_all)]
                t_all = [t - _nn(t, x, hi=True) for t, x in zip(t_all, x_all)]
            u_all = [_nn(t, hd["v"] * hd["bc"], hi=True) for t, hd in zip(t_all, heads)]
            w_all = [_nn(t, hd["kn"] * (hd["bc"] * hd["egam"]), hi=True) for t, hd in zip(t_all, heads)]
            qk_all = [_nt(hd["qn"], hd["kn"]) * hd["decay"] for hd in heads]
            for h in range(DN_HEADS):
                t_ref[c, h] = t_all[h]
            pre.append((heads, u_all, w_all, qk_all))
        for c in range(DN_CPS):
            rs = slice(c * CHUNK, (c + 1) * CHUNK)
            heads, u_all, w_all, qk_all = pre[c]
            s_all = [s_ref[h] for h in range(DN_HEADS)]
            os_all = [_nn(hd["qn"] * hd["egam"], s) for hd, s in zip(heads, s_all)]
            vnew_all = [u - _nn(w, s) for u, w, s in zip(u_all, w_all, s_all)]
            o_all = [os + _nn(qk, vn) for os, qk, vn in zip(os_all, qk_all, vnew_all)]
            snew_all = [s * hd["eglast"] + _tn(hd["kn"] * hd["ekd"], vn)
                        for s, hd, vn in zip(s_all, heads, vnew_all)]
            for h in range(DN_HEADS):
                v_cols = slice(h * DN_DV, (h + 1) * DN_DV)
                ss_ref[c, h] = s_all[h]
                s_ref[h] = snew_all[h]
                o_ref[rs, v_cols] = o_all[h]
                y_ref[rs, v_cols] = _gated_norm(o_all[h], gate_ref[rs, v_cols], nw).astype(y_ref.dtype)

    fwd = lambda n: n
    row_v = pl.BlockSpec((DN_CPS * CHUNK, DN_V), lambda n: (n, 0))
    return pl.pallas_call(
        body, grid=(nc // DN_CPS,), in_specs=_dn_in_specs(fwd),
        out_specs=[row_v, row_v,
                   pl.BlockSpec((DN_CPS, DN_HEADS, DN_DK, DN_DV), lambda n: (n, 0, 0, 0)),
                   pl.BlockSpec((DN_CPS, DN_HEADS, CHUNK, CHUNK), lambda n: (n, 0, 0, 0)),
                   pl.BlockSpec((DN_CPS * CHUNK, DN_CONV_CH), lambda n: (n, 0))],
        out_shape=[jax.ShapeDtypeStruct((rows, DN_V), F32), jax.ShapeDtypeStruct((rows, DN_V), BF16),
                   jax.ShapeDtypeStruct((nc, DN_HEADS, DN_DK, DN_DV), F32),
                   jax.ShapeDtypeStruct((nc, DN_HEADS, CHUNK, CHUNK), F32),
                   jax.ShapeDtypeStruct((rows, DN_CONV_CH), F32)],
        scratch_shapes=[pltpu.VMEM((DN_HEADS, DN_DK, DN_DV), F32)],
        name="dn_fwd", compiler_params=_params("arbitrary"))(
            proj, proj, proj, proj, conv_w, alog, dtb, norm_w.reshape(1, DN_DV))


def _dn_bwd(proj, conv_out, o, dy, states, tinv, conv_w, alog, dtb, norm_w, seq):
    rows = proj.shape[0]
    nc = rows // CHUNK

    def body(x_ref, yc_ref, gate_ref, ba_ref, cw_ref, al_ref, dt_ref, nw_ref,
             o_ref, dy_ref, ss_ref, t_ref,
             dp_ref, dcw_ref, dal_ref, ddt_ref, dnw_ref, ds_ref, nxt_ref):
        n = pl.program_id(0)

        @pl.when(n == 0)
        def _():
            ds_ref[...] = jnp.zeros_like(ds_ref)
            nxt_ref[...] = jnp.zeros_like(nxt_ref)
            dcw_ref[...] = jnp.zeros_like(dcw_ref)
            dal_ref[...] = jnp.zeros_like(dal_ref)
            ddt_ref[...] = jnp.zeros_like(ddt_ref)
            dnw_ref[...] = jnp.zeros_like(dnw_ref)

        for c in reversed(range(DN_CPS)):
            rs = pl.ds(c * CHUNK, CHUNK)
            chunk((steps - 1 - n) * DN_CPS + c, x_ref.at[rs], yc_ref.at[rs], gate_ref.at[rs], ba_ref.at[rs],
                  cw_ref, al_ref, dt_ref, nw_ref, o_ref.at[rs], dy_ref.at[rs], ss_ref.at[c], t_ref.at[c],
                  dp_ref.at[rs], dcw_ref, dal_ref, ddt_ref, dnw_ref, ds_ref, nxt_ref)

    def chunk(ch, x_ref, yc_ref, gate_ref, ba_ref, cw_ref, al_ref, dt_ref, nw_ref,
              o_ref, dy_ref, ss_ref, t_ref,
              dp_ref, dcw_ref, dal_ref, ddt_ref, dnw_ref, ds_ref, nxt_ref):
        f = _dn_front(ch, seq, x_ref[...], None, ba_ref[...], cw_ref, al_ref, dt_ref, yc_ref[...])
        ri, ci = f["ri"], f["ci"]
        strict = (ri > ci).astype(F32)
        nw = nw_ref[...]
        lane128 = _iota((1, LANES), 1)
        row128 = _iota((LANES, 1), 0)
        dgam_col = jnp.zeros((CHUNK, LANES), F32)
        dgam_row = jnp.zeros((LANES, CHUNK), F32)
        dbeta = jnp.zeros((CHUNK, LANES), F32)
        dnw = jnp.zeros((1, DN_DV), F32)
        hs = range(DN_HEADS)
        heads = [_dn_head(f, h) for h in hs]
        cols = [slice(h * DN_DV, (h + 1) * DN_DV) for h in hs]
        t_l = [t_ref[h] for h in hs]
        s_l = [ss_ref[h] for h in hs]
        ds_l = [ds_ref[h] for h in hs]
        kk_l = [_nt(hd["kn"], hd["kn"]) for hd in heads]
        p_l = [_nt(hd["qn"], hd["kn"]) for hd in heads]
        rhsw_l = [hd["kn"] * (hd["bc"] * hd["egam"]) for hd in heads]
        u_l = [_nn(t, hd["v"] * hd["bc"], hi=True) for t, hd in zip(t_l, heads)]
        w_l = [_nn(t, r, hi=True) for t, r in zip(t_l, rhsw_l)]
        vnew_l = [u - _nn(w, s) for u, w, s in zip(u_l, w_l, s_l)]
        gnb = [_gated_norm_bwd(dy_ref[:, c], o_ref[:, c], gate_ref[:, c], nw) for c in cols]
        do_l = [x[0] for x in gnb]
        for h in hs:
            dp_ref[:, DN_CONV_CH + h * DN_DV:DN_CONV_CH + (h + 1) * DN_DV] = gnb[h][1].astype(dp_ref.dtype)
            dnw = dnw + gnb[h][2]
        qg_l = [hd["qn"] * hd["egam"] for hd in heads]
        kd_l = [hd["kn"] * hd["ekd"] for hd in heads]
        dvnew_l = [_tn(p * hd["decay"], do) + _nn(kd, ds)
                   for p, hd, do, kd, ds in zip(p_l, heads, do_l, kd_l, ds_l)]
        m_l = [_nt(do, vn) for do, vn in zip(do_l, vnew_l)]
        dqg_l = [_nt(do, s) for do, s in zip(do_l, s_l)]
        dkd_l = [_nt(vn, ds) for vn, ds in zip(vnew_l, ds_l)]
        for h in hs:
            ds_ref[h] = (ds_l[h] * heads[h]["eglast"] + _tn(qg_l[h], do_l[h]) - _tn(w_l[h], dvnew_l[h]))
        dw_l = [-_nt(dvn, s) for dvn, s in zip(dvnew_l, s_l)]
        dru_l = [_tn(t, dvn, hi=True) for t, dvn in zip(t_l, dvnew_l)]
        drw_l = [_tn(t, dw_, hi=True) for t, dw_ in zip(t_l, dw_l)]
        da_l = [-(_nt(dru, u) + _nt(drw, w)) * strict for dru, u, drw, w in zip(dru_l, u_l, drw_l, w_l)]
        dp_l = [m * hd["decay"] for m, hd in zip(m_l, heads)]
        dkk_l = [da * (hd["bc"] * hd["decay"]) for da, hd in zip(da_l, heads)]
        dqn_l = [dqg * hd["egam"] + _nn(dp, hd["kn"]) for dqg, hd, dp in zip(dqg_l, heads, dp_l)]
        dkn_l = [_tn(dp, hd["qn"]) + dkd * hd["ekd"] + drw * (hd["bc"] * hd["egam"])
                 + _nn(dkk, hd["kn"]) + _tn(dkk, hd["kn"])
                 for dp, hd, dkd, drw, dkk in zip(dp_l, heads, dkd_l, drw_l, dkk_l)]
        dq_parts, dk_parts, dv_parts = [], [], []
        for h in hs:
            hd = heads[h]
            kn, v, bc, egam, decay = hd["kn"], hd["v"], hd["bc"], hd["egam"], hd["decay"]
            t1 = jnp.sum(dkd_l[h] * kd_l[h], axis=1, keepdims=True)
            dglast = (jnp.sum(t1, axis=0, keepdims=True)
                      + jnp.sum(jnp.sum(ds_l[h] * s_l[h], axis=1, keepdims=True), axis=0, keepdims=True)
                      * hd["eglast"])
            e = (m_l[h] * p_l[h] + da_l[h] * (bc * kk_l[h])) * decay
            dgc = (jnp.sum(dqg_l[h] * qg_l[h], axis=1, keepdims=True) - t1
                   + jnp.sum(drw_l[h] * rhsw_l[h], axis=1, keepdims=True)
                   + jnp.sum(e, axis=1, keepdims=True)
                   + jnp.where(_iota((CHUNK, 1), 0) == CHUNK - 1, dglast, 0.0))
            dgr = -jnp.sum(e, axis=0, keepdims=True)
            dbc = (jnp.sum(dru_l[h] * v, axis=1, keepdims=True)
                   + jnp.sum(drw_l[h] * kn, axis=1, keepdims=True) * egam
                   + jnp.sum(da_l[h] * kk_l[h] * decay, axis=1, keepdims=True))
            dv_parts.append(dru_l[h] * bc)
            qh, dqn, dkn = hd["qh"], dqn_l[h], dkn_l[h]
            dq_parts.append(((DN_DK ** -0.5) * hd["rq"])
                            * (dqn - qh * jnp.sum(dqn * qh, axis=1, keepdims=True)))
            dk_parts.append(hd["rk"] * (dkn - kn * jnp.sum(dkn * kn, axis=1, keepdims=True)))
            dgam_col = dgam_col + dgc * (lane128 == DECAY_LANE + h).astype(F32)
            dbeta = dbeta + dbc * (lane128 == BETA_LANE + h).astype(F32)
            dgam_row = dgam_row + (row128 == DECAY_LANE + h).astype(F32) * dgr
        dnw_ref[...] += dnw
        dgam = dgam_col + _nt(f["eye"], dgam_row, hi=True)
        dg = _nn(f["upper"], dgam, hi=True)
        d_a = dg * (-f["eal"]) * jax.nn.sigmoid(f["z"]) * f["valid"]
        dal_ref[...] += jnp.sum(dg * f["g"], axis=0, keepdims=True)
        ddt_ref[...] += jnp.sum(d_a, axis=0, keepdims=True)
        d_b = dbeta * f["valid"] * f["sig"] * (1.0 - f["sig"])
        dp_ref[:, DN_CONV_CH + DN_V:DN_CONV_CH + DN_V + LANES] = (d_a + d_b).astype(dp_ref.dtype)
        dp_ref[:, DN_CONV_CH + DN_V + LANES:] = jnp.zeros((CHUNK, DN_IN_PAD - DN_IN_USED), dp_ref.dtype)
        dact = jnp.concatenate(dq_parts + dk_parts + dv_parts, axis=1)
        yc, sgc = f["yc"], f["sgc"]
        dyc = dact * (sgc * (1.0 + yc * (1.0 - sgc)))
        nxt = nxt_ref[...]
        ups = [dyc] + [_shift_up(dyc, nxt, j) for j in range(1, CONV_K)]
        dx = ups[0] * cw_ref[CONV_K - 1]
        for j in range(1, CONV_K):
            dx = dx + ups[j] * cw_ref[CONV_K - 1 - j]
        for j in range(CONV_K):
            dcw_ref[CONV_K - 1 - j] += jnp.sum(f["xin"] * ups[j], axis=0, keepdims=True)
        nxt_ref[...] = dyc[0:8]
        dp_ref[:, :DN_CONV_CH] = (dx * f["valid"]).astype(dp_ref.dtype)

    steps = nc // DN_CPS
    rev = lambda n: steps - 1 - n
    row_v = pl.BlockSpec((DN_CPS * CHUNK, DN_V), lambda n: (rev(n), 0))
    vec = pl.BlockSpec((1, LANES), lambda n: (0, 0))
    return pl.pallas_call(
        body, grid=(steps,),
        in_specs=_dn_in_specs(rev, conv_saved=True) + [
            row_v, row_v,
            pl.BlockSpec((DN_CPS, DN_HEADS, DN_DK, DN_DV), lambda n: (rev(n), 0, 0, 0)),
            pl.BlockSpec((DN_CPS, DN_HEADS, CHUNK, CHUNK), lambda n: (rev(n), 0, 0, 0))],
        out_specs=[pl.BlockSpec((DN_CPS * CHUNK, DN_IN_PAD), lambda n: (rev(n), 0)),
                   pl.BlockSpec((CONV_K, 1, DN_CONV_CH), lambda n: (0, 0, 0)), vec, vec,
                   pl.BlockSpec((1, DN_DV), lambda n: (0, 0))],
        out_shape=[jax.ShapeDtypeStruct((rows, DN_IN_PAD), BF16),
                   jax.ShapeDtypeStruct((CONV_K, 1, DN_CONV_CH), F32),
                   jax.ShapeDtypeStruct((1, LANES), F32), jax.ShapeDtypeStruct((1, LANES), F32),
                   jax.ShapeDtypeStruct((1, DN_DV), F32)],
        scratch_shapes=[pltpu.VMEM((DN_HEADS, DN_DK, DN_DV), F32), pltpu.VMEM((8, DN_CONV_CH), F32)],
        name="dn_bwd", compiler_params=_params("arbitrary"))(
            proj, conv_out, proj, proj, conv_w, alog, dtb, norm_w.reshape(1, DN_DV), o, dy, states, tinv)


def _train_step(x, tgt, wts, sh, idx):
    seq = x.shape[0]
    rows = -(-(seq + CHUNK) // ROW_ALIGN) * ROW_ALIGN
    wts = dict(wts)
    (h0, tgt_p), (got,) = _embed(x, tgt, wts["meta_tokens"].astype(F32), rows,
                                 ride=_Ride("gather", [sh["ret_w_in"]]))
    wts["ret_w_in"] = got.reshape(N_SHARD, D_MODEL, -1)
    cos, sin = _rope_tables(rows // CHUNK)
    consts = _ret_consts()
    conv_w = wts["dn_conv_w"].reshape(CONV_K, 1, DN_CONV_CH)
    lane_pad = LANES - 2 * DN_HEADS
    alog = jnp.pad(wts["dn_a_log"].reshape(1, DN_HEADS), ((0, 0), (DECAY_LANE, lane_pad)))
    dtb = jnp.pad(wts["dn_dt_bias"].reshape(1, DN_HEADS), ((0, 0), (DECAY_LANE, lane_pad)))
    g = {}

    hn0 = _rms_fwd(h0, wts["mix_norm_w"][0], "rms_mix0")
    proj0, got = _mm_cols(hn0, wts["ret_w_in"], "ret_in",
                          ride=_Ride("gather", [sh["ret_w_out"], sh["ffn_w_gate"]]))
    wts["ret_w_out"] = got[0].reshape(-1, D_MODEL)
    wts["ffn_w_gate"] = got[1]
    (o0, y0, st0), got = _ret_fwd(proj0, cos, sin, consts, wts["ret_gn_w"], seq,
                                  ride=_Ride("gather", [sh["ffn_w_up"], sh["ffn_w_down"]]))
    wts["ffn_w_up"], wts["ffn_w_down"] = got
    h1 = _mm(y0, wts["ret_w_out"], mode="nn", name="ret_out", resid=h0)
    hn1 = _rms_fwd(h1, wts["ffn_norm_w"][0], "rms_ffn0")
    (g0, u0, act0), got = _ffn_up(hn1, wts["ffn_w_gate"], wts["ffn_w_up"], 0, "ffn_up0",
                                  ride=_Ride("gather", [sh["dn_w_in"], sh["dn_w_out"]]))
    n_dn = sh["dn_w_in"].shape[-1]
    dn_shards = got[0].reshape(N_SHARD, D_MODEL, n_dn)
    wts["dn_w_in"] = jnp.concatenate(
        [dn_shards[j] for j in range(N_SHARD)]
        + [jnp.zeros((D_MODEL, DN_IN_PAD - N_SHARD * n_dn), dn_shards.dtype)], axis=-1)
    wts["dn_w_out"] = got[1].reshape(-1, D_MODEL)
    h2 = _ffn_down(act0, wts["ffn_w_down"], h1, 0, "ffn_down0")
    hn2 = _rms_fwd(h2, wts["mix_norm_w"][1], "rms_mix1")
    proj1 = _mm(hn2, wts["dn_w_in"], mode="nn", name="dn_in")
    o1, y1, st1, tinv, conv1 = _dn_fwd(proj1, conv_w, alog, dtb, wts["dn_norm_w"], seq)
    h3 = _mm(y1, wts["dn_w_out"], mode="nn", name="dn_out", resid=h2)
    hn3 = _rms_fwd(h3, wts["ffn_norm_w"][1], "rms_ffn1")
    g1, u1, act1 = _ffn_up(hn3, wts["ffn_w_gate"], wts["ffn_w_up"], 1, "ffn_up1")
    h4 = _ffn_down(act1, wts["ffn_w_down"], h3, 1, "ffn_down1")

    dh4, g["final_norm_w"], loss, dh4b = _final_loss(h4, wts["final_norm_w"], tgt_p, seq, "final_loss")

    layers = wts["ffn_w_gate"].shape[1]

    ffn_names = ["ffn_w_down", "ffn_w_gate", "ffn_w_up"]

    def ffn_bwd(dh_out, dhb_out, h_mid, hn, gg, uu, act, layer, prev, ride=None, last=False):
        tag = str(layer)
        res = _ffn_down_bwd(dhb_out, wts["ffn_w_down"], gg, uu, layer, "ffn_down_bwd" + tag, ride=ride)
        (dg, du), rode = res if ride is not None else (res, None)
        d_down = _ffn_wgrad(act, [dhb_out], layer, layers, prev and prev[:1], True, "ffn_dwd" + tag)
        d_gu = _ffn_wgrad(hn, [dg, du], layer, layers, prev and prev[1:], False, "ffn_dwgu" + tag)
        grads = list(d_down) + list(d_gu)
        gs = rs_grads(ffn_names, grads) if last else None
        res = _ffn_up_bwd(dg, du, wts["ffn_w_gate"], wts["ffn_w_up"], layer, h_mid, wts["ffn_norm_w"][layer],
                          dh_out, "ffn_up_bwd" + tag, ride=_Ride("pair", gs) if last else None)
        (dh_mid, d_norm, dhb_mid), sib = res if last else (res, None)
        return dh_mid, dhb_mid, grads, d_norm, rode, gs, sib

    red = {}

    def rs_grads(names, grads):
        return [gr.reshape((N_SHARD,) + sh[n].shape) for n, gr in zip(names, grads)]

    def rs_partials(names, gs, sib):
        return [_rs_pair_add(gs[t], sib[t], idx, "rs_pair_add_" + n) for t, n in enumerate(names)]

    def rs_end(names, gs, sib, others, tag):
        mine = [_rs_final_add(gs[t], sib[t], others[t], idx, "rs_final_add_" + n) for t, n in enumerate(names)]
        red.update(zip(names, _rs_share(mine, "rs_share" + tag)))

    dh3, dh3b, ffn_grads, dfn1 = ffn_bwd(dh4, dh4b, h3, hn3, g1, u1, act1, 1, None)[:4]
    dy1 = _mm(dh3b, wts["dn_w_out"], mode="nt", name="dn_out_bwd")
    d_dn_out = _mm(y1, dh3b, mode="tn", name="dn_dwo")
    dproj1, dcw, dal, ddt, g["dn_norm_w"] = _dn_bwd(proj1, conv1, o1, dy1, st1, tinv, conv_w, alog, dtb,
                                                    wts["dn_norm_w"], seq)
    d_dn_in = _mm(hn2, dproj1, mode="tn", name="dn_dwi")
    d_dn_in = jnp.stack([d_dn_in[:, j * n_dn:(j + 1) * n_dn] for j in range(N_SHARD)])
    group1 = ["dn_w_out", "dn_w_in"]
    gs1 = rs_grads(group1, [d_dn_out, d_dn_in])
    (dh2, dmn1, dh2b), sib1 = _mm_nt_rms(dproj1, wts["dn_w_in"], h2, wts["mix_norm_w"][1], dh3, "dn_in_bwd",
                                         ride=_Ride("pair", gs1))
    g["dn_conv_w"] = dcw.reshape(CONV_K, DN_CONV_CH)
    g["dn_a_log"] = dal[0, DECAY_LANE:DECAY_LANE + DN_HEADS]
    g["dn_dt_bias"] = ddt[0, DECAY_LANE:DECAY_LANE + DN_HEADS]

    dh1, dh1b, _, dfn0, others1, gs2, sib2 = ffn_bwd(
        dh2, dh2b, h1, hn1, g0, u0, act0, 0, ffn_grads,
        ride=_Ride("chips", rs_partials(group1, gs1, sib1)), last=True)
    rs_end(group1, gs1, sib1, others1, "1")
    d_ret_out = _mm(y0, dh1b, mode="tn", name="ret_dwo")
    gs2b = rs_grads(["ret_w_out"], [d_ret_out])
    dy0, sib2b = _mm(dh1b, wts["ret_w_out"], mode="nt", name="ret_out_bwd", ride=_Ride("pair", gs2b))
    group2 = ffn_names + ["ret_w_out"]
    gs2, sib2 = gs2 + gs2b, list(sib2) + list(sib2b)
    (dproj0, g["ret_gn_w"]), others2 = _ret_bwd(proj0, o0, dy0, st0, cos, sin, consts, wts["ret_gn_w"], seq,
                                                ride=_Ride("chips", rs_partials(group2, gs2, sib2)))
    rs_end(group2, gs2, sib2, others2, "2")
    d_ret_in = _mm_cols_grad(hn0, dproj0, "ret_dwi")
    gs3 = rs_grads(["ret_w_in"], [d_ret_in])
    sib3 = _rs_pair(gs3, "rs_pair3")
    (dh0, dmn0, _), others3 = _mm_cols_t_rms(dproj0, wts["ret_w_in"], h0, wts["mix_norm_w"][0], dh1, "ret_in_bwd",
                                             ride=_Ride("chips", rs_partials(["ret_w_in"], gs3, sib3)))
    rs_end(["ret_w_in"], gs3, sib3, others3, "3")

    g["ffn_norm_w"] = jnp.concatenate([dfn0, dfn1], axis=0)
    g["mix_norm_w"] = jnp.concatenate([dmn0, dmn1], axis=0)
    g["meta_tokens"] = dh0[PAD:CHUNK]
    g["final_norm_w"] = g["final_norm_w"].reshape(D_MODEL)
    g["ret_gn_w"] = g["ret_gn_w"].reshape(RET_DV)
    g["dn_norm_w"] = g["dn_norm_w"].reshape(DN_DV)
    return loss, dh0, g, red


def _mesh_pos():
    return lax.axis_index("x"), lax.axis_index("y"), lax.axis_index("c")


def _other_chips(x, y):
    return [(1 - x, y), (x, 1 - y), (1 - x, 1 - y)]


def _remote(src, dst, send_sem, recv_sem, to):
    return pltpu.make_async_remote_copy(src_ref=src, dst_ref=dst, send_sem=send_sem, recv_sem=recv_sem,
                                        device_id=to, device_id_type=MESH)


GATHER_COPIES = 7


def _gather_phase(phase, ins, outs, send_sems, recv_sems):
    x, y, c = _mesh_pos()
    me = 2 * x + y
    chips = _other_chips(x, y)
    sibling = (x, y, 1 - c)

    def cp(t, k, src, dst, to):
        i = GATHER_COPIES * t + k
        return _remote(src, dst, send_sems.at[i], recv_sems.at[i], to)

    for t in range(len(ins)):
        own = cp(t, 0, ins[t], outs[t].at[me], sibling)
        if phase == 0:
            own.start()
        if phase == 2:
            own.wait()
        for k, (px, py) in enumerate(chips):
            landed = outs[t].at[2 * px + py, c]
            theirs = outs[t].at[2 * px + py, 1 - c]
            to_chip = cp(t, 1 + k, ins[t].at[c], outs[t].at[me, c], (px, py, c))
            if phase == 0:
                to_chip.start()
            if phase == 1:
                cp(t, 1 + k, ins[t].at[c], landed, (px, py, c)).wait_recv()
                cp(t, 4 + k, landed, landed, sibling).start()
            if phase == 2:
                to_chip.wait_send()
                cp(t, 4 + k, landed, landed, sibling).wait_send()
                cp(t, 4 + k, theirs, theirs, sibling).wait_recv()


def _chips_phase(phase, ins, outs, send_sems, recv_sems):
    x, y, c = _mesh_pos()
    for t in range(len(ins)):
        for k, (px, py) in enumerate(_other_chips(x, y)):
            cp = _remote(ins[t].at[2 * px + py], outs[t].at[k], send_sems.at[3 * t + k], recv_sems.at[3 * t + k],
                         (px, py, c))
            if phase == 0:
                cp.start()
            if phase == 2:
                cp.wait()


class _Ride:
    def __init__(self, kind, arrays):
        self.kind, self.arrays = kind, list(arrays)
        nt = len(self.arrays)
        if kind == "gather":
            self.phase_fn, n_sem = _gather_phase, GATHER_COPIES * nt
            self.out_shape = [jax.ShapeDtypeStruct((N_SHARD,) + a.shape, a.dtype) for a in self.arrays]
        elif kind == "pair":
            self.phase_fn, n_sem = _pair_phase, nt
            self.out_shape = [jax.ShapeDtypeStruct(a.shape[:1] + a.shape[2:], a.dtype) for a in self.arrays]
        else:
            self.phase_fn, n_sem = _chips_phase, 3 * nt
            self.out_shape = [jax.ShapeDtypeStruct((3,) + a.shape[1:], a.dtype) for a in self.arrays]
        self.in_specs, self.out_specs = [ANY] * nt, [ANY] * nt
        self.scratch = [pltpu.SemaphoreType.DMA((n_sem,)), pltpu.SemaphoreType.DMA((n_sem,))]

    def emit(self, step, nsteps, ins, outs, send_sems, recv_sems, before):
        mid = max(0, min((7 * nsteps) // 8, nsteps - 2))
        todo = [(0, 0), (1, mid)] if before else [(2, nsteps - 1)]
        for phase, at in todo:
            if phase == 1 and self.kind != "gather":
                continue

            @pl.when(step == at)
            def _(phase=phase):
                self.phase_fn(phase, ins, outs, send_sems, recv_sems)


def _gather_small(blk):
    r, wd = blk.shape

    def body(b_ref, out_ref, send_sems, recv_sems):
        x, y, c = _mesh_pos()
        chips = _other_chips(x, y)
        out_ref[2 * x + y] = b_ref[...]
        sends = [_remote(b_ref, out_ref.at[2 * x + y], send_sems.at[k], recv_sems.at[k], (px, py, c))
                 for k, (px, py) in enumerate(chips)]
        for cp in sends:
            cp.start()
        for k, (px, py) in enumerate(chips):
            _remote(b_ref, out_ref.at[2 * px + py], send_sems.at[k], recv_sems.at[k], (px, py, c)).wait_recv()
        for cp in sends:
            cp.wait_send()

    return pl.pallas_call(
        body, out_shape=jax.ShapeDtypeStruct((4, r, wd), blk.dtype), in_specs=[VMEM_SPEC], out_specs=VMEM_SPEC,
        scratch_shapes=[pltpu.SemaphoreType.DMA((3,)), pltpu.SemaphoreType.DMA((3,))],
        name="gather_small")(blk)


def _allreduce_small(blk):
    r, wd = blk.shape
    rels = [(dx, dy, dc) for dx in (0, 1) for dy in (0, 1) for dc in (0, 1) if dx or dy or dc]

    def body(b_ref, out_ref, buf_ref, send_sems, recv_sems):
        x, y, c = _mesh_pos()

        def peer(rel):
            dx, dy, dc = rel
            return (1 - x if dx else x, 1 - y if dy else y, 1 - c if dc else c)

        me = 4 * x + 2 * y + c
        buf_ref[me] = b_ref[...]
        sends = [_remote(b_ref, buf_ref.at[me], send_sems.at[k], recv_sems.at[k], peer(rel))
                 for k, rel in enumerate(rels)]
        for cp in sends:
            cp.start()
        for k, rel in enumerate(rels):
            px, py, pc = peer(rel)
            _remote(b_ref, buf_ref.at[4 * px + 2 * py + pc], send_sems.at[k], recv_sems.at[k],
                    (px, py, pc)).wait_recv()
        for cp in sends:
            cp.wait_send()
        acc = buf_ref[0]
        for d in range(1, 8):
            acc = acc + buf_ref[d]
        out_ref[...] = acc

    return pl.pallas_call(
        body, out_shape=jax.ShapeDtypeStruct((r, wd), blk.dtype), in_specs=[VMEM_SPEC], out_specs=VMEM_SPEC,
        scratch_shapes=[pltpu.VMEM((8, r, wd), blk.dtype), pltpu.SemaphoreType.DMA((7,)),
                        pltpu.SemaphoreType.DMA((7,))],
        name="allreduce_small")(blk)


def _rs_pair(gs, name):
    ride = _Ride("pair", gs)

    def body(*refs):
        nt = len(gs)
        for phase in (0, 2):
            _pair_phase(phase, refs[:nt], refs[nt:2 * nt], *refs[2 * nt:])

    return pl.pallas_call(body, out_shape=ride.out_shape, in_specs=ride.in_specs, out_specs=ride.out_specs,
                          scratch_shapes=ride.scratch, name=name)(*gs)


def _pair_phase(phase, ins, outs, send_sems, recv_sems):
    x, y, c = _mesh_pos()
    for t in range(len(ins)):
        cp = _remote(ins[t].at[:, 1 - c], outs[t], send_sems.at[t], recv_sems.at[t], (x, y, 1 - c))
        if phase == 0:
            cp.start()
        if phase == 2:
            cp.wait()


def _rs_tile(a, b):
    return _div_tile(a, 1024 if b <= 1024 else 512, 16)


def _rs_pair_add(g, a, idx, name):
    _, _, rows, cols = g.shape
    tr = _rs_tile(rows, cols)

    def body(s_ref, g_ref, a_ref, p_ref):
        p_ref[...] = (g_ref[...] + a_ref[...]).astype(p_ref.dtype)

    blk = pl.BlockSpec((None, tr, cols), lambda j, i, s: (j, i, 0))
    spec = pltpu.PrefetchScalarGridSpec(
        num_scalar_prefetch=1, grid=(N_SHARD, rows // tr),
        in_specs=[pl.BlockSpec((None, None, tr, cols), lambda j, i, s: (j, s[0], i, 0)), blk], out_specs=blk)
    return pl.pallas_call(
        body, grid_spec=spec, out_shape=jax.ShapeDtypeStruct((N_SHARD, rows, cols), BF16), name=name,
        compiler_params=_params("parallel", "parallel"))(idx, g, a)


def _rs_final_add(g, a, b, idx, name):
    _, _, rows, cols = g.shape
    tr = _rs_tile(rows, cols)

    def body(s_ref, g_ref, a_ref, b0_ref, b1_ref, b2_ref, f_ref):
        own = g_ref[...] + a_ref[...]
        f_ref[...] = ((own + b0_ref[...].astype(F32)) + b1_ref[...].astype(F32)) + b2_ref[...].astype(F32)

    def b_spec(k):
        return pl.BlockSpec((None, tr, cols), lambda i, s: (k, i, 0))

    spec = pltpu.PrefetchScalarGridSpec(
        num_scalar_prefetch=1, grid=(rows // tr,),
        in_specs=[pl.BlockSpec((None, None, tr, cols), lambda i, s: (s[1], s[0], i, 0)),
                  pl.BlockSpec((None, tr, cols), lambda i, s: (s[1], i, 0)), b_spec(0), b_spec(1), b_spec(2)],
        out_specs=pl.BlockSpec((None, tr, cols), lambda i, s: (s[0], i, 0)))
    return pl.pallas_call(
        body, grid_spec=spec, out_shape=jax.ShapeDtypeStruct((2, rows, cols), F32), name=name,
        compiler_params=_params("parallel"))(idx, g, a, b, b, b)


def _rs_share(fs, name):
    nt = len(fs)

    def body(*refs):
        outs = refs[nt:2 * nt]
        send_sems, recv_sems = refs[2 * nt:]
        x, y, c = _mesh_pos()
        cps = [_remote(outs[t].at[c], outs[t].at[c], send_sems.at[t], recv_sems.at[t], (x, y, 1 - c))
               for t in range(nt)]
        for cp in cps:
            cp.start()
        for cp in cps:
            cp.wait()

    return pl.pallas_call(
        body, out_shape=[jax.ShapeDtypeStruct(f.shape, f.dtype) for f in fs],
        in_specs=[ANY] * nt, out_specs=[ANY] * nt, input_output_aliases={t: t for t in range(nt)},
        scratch_shapes=[pltpu.SemaphoreType.DMA((nt,)), pltpu.SemaphoreType.DMA((nt,))], name=name)(*fs)


def _adamw(w, g, m, v, name):
    lead, rows, cols = w.shape
    tr = rows // 4 if rows % 32 == 0 else rows

    def body(w_ref, g_ref, m_ref, v_ref, go_ref, d_ref, mo_ref, vo_ref):
        gv = g_ref[...]
        go_ref[...] = gv
        mn = ADAM_B1 * m_ref[...] + (1.0 - ADAM_B1) * gv
        vn = ADAM_B2 * v_ref[...] + (1.0 - ADAM_B2) * (gv * gv)
        m_hat = mn / (1.0 - ADAM_B1 ** ADAM_STEP)
        v_hat = vn / (1.0 - ADAM_B2 ** ADAM_STEP)
        d_ref[...] = -ADAM_LR * (m_hat / (jnp.sqrt(v_hat) + ADAM_EPS) + ADAM_WD * w_ref[...])
        mo_ref[...] = mn
        vo_ref[...] = vn

    blk = pl.BlockSpec((None, tr, cols), lambda l, i: (l, i, 0))
    out = jax.ShapeDtypeStruct((lead, rows, cols), F32)
    return pl.pallas_call(
        body, grid=(lead, rows // tr), in_specs=[blk] * 4, out_specs=[blk] * 4, out_shape=[out] * 4, name=name,
        compiler_params=_params("parallel", "parallel"))(w, g, m, v)


BIG = ["ret_w_in", "ret_w_out", "dn_w_in", "dn_w_out", "ffn_w_gate", "ffn_w_up", "ffn_w_down"]
TRANSPOSED_AT_BOUNDARY = {"dn_w_in": True, "ffn_w_gate": False, "ffn_w_up": False}
SMALL =["meta_tokens", "mix_norm_w", "ffn_norm_w", "ret_gn_w", "dn_conv_w", "dn_a_log", "dn_dt_bias",
         "dn_norm_w", "final_norm_w"]
SMALL_SHARDED = {"meta_tokens", "dn_conv_w", "dn_norm_w"}
ORDER = ["meta_tokens", "mix_norm_w", "ffn_norm_w", "ret_w_in", "ret_gn_w", "ret_w_out", "dn_w_in",
         "dn_conv_w", "dn_a_log", "dn_dt_bias", "dn_norm_w", "dn_w_out", "ffn_w_gate", "ffn_w_up",
         "ffn_w_down", "final_norm_w"]


def _halves(a):
    return a.reshape(2, -1, a.shape[-1])


def _pack_lanes(parts, align=8):
    flat = jnp.concatenate([p.reshape(-1) for p in parts])
    flat = jnp.pad(flat, (0, -flat.shape[0] % (align * LANES)))
    return flat.reshape(-1, LANES)


def _unpack(buf, shapes):
    lead = buf.shape[:-2]
    flat = buf.reshape(lead + (-1,))
    out, off = [], 0
    for shp in shapes:
        size = math.prod(shp)
        out.append(flat[..., off:off + size].reshape(lead + tuple(shp)))
        off += size
    return out


def _join_cols(shards):
    return jnp.concatenate([shards[j] for j in range(N_SHARD)], axis=-1)


def kernel(x, meta_tokens, mix_norm_w, ffn_norm_w, ret_w_in, ret_gn_w, ret_w_out, dn_w_in, dn_conv_w, dn_a_log, dn_dt_bias, dn_norm_w, dn_w_out, ffn_w_gate, ffn_w_up, ffn_w_down, final_norm_w, loss_target, m_meta_tokens, m_mix_norm_w, m_ffn_norm_w, m_ret_w_in, m_ret_gn_w, m_ret_w_out, m_dn_w_in, m_dn_conv_w, m_dn_a_log, m_dn_dt_bias, m_dn_norm_w, m_dn_w_out, m_ffn_w_gate, m_ffn_w_up, m_ffn_w_down, m_final_norm_w, v_meta_tokens, v_mix_norm_w, v_ffn_norm_w, v_ret_w_in, v_ret_gn_w, v_ret_w_out, v_dn_w_in, v_dn_conv_w, v_dn_a_log, v_dn_dt_bias, v_dn_norm_w, v_dn_w_out, v_ffn_w_gate, v_ffn_w_up, v_ffn_w_down, v_final_norm_w):
    w = dict(meta_tokens=meta_tokens, mix_norm_w=mix_norm_w, ffn_norm_w=ffn_norm_w, ret_w_in=ret_w_in,
             ret_gn_w=ret_gn_w, ret_w_out=ret_w_out, dn_w_in=dn_w_in, dn_conv_w=dn_conv_w, dn_a_log=dn_a_log,
             dn_dt_bias=dn_dt_bias, dn_norm_w=dn_norm_w, dn_w_out=dn_w_out, ffn_w_gate=ffn_w_gate,
             ffn_w_up=ffn_w_up, ffn_w_down=ffn_w_down, final_norm_w=final_norm_w)
    m = dict(meta_tokens=m_meta_tokens, mix_norm_w=m_mix_norm_w, ffn_norm_w=m_ffn_norm_w, ret_w_in=m_ret_w_in,
             ret_gn_w=m_ret_gn_w, ret_w_out=m_ret_w_out, dn_w_in=m_dn_w_in, dn_conv_w=m_dn_conv_w,
             dn_a_log=m_dn_a_log, dn_dt_bias=m_dn_dt_bias, dn_norm_w=m_dn_norm_w, dn_w_out=m_dn_w_out,
             ffn_w_gate=m_ffn_w_gate, ffn_w_up=m_ffn_w_up, ffn_w_down=m_ffn_w_down, final_norm_w=m_final_norm_w)
    v = dict(meta_tokens=v_meta_tokens, mix_norm_w=v_mix_norm_w, ffn_norm_w=v_ffn_norm_w, ret_w_in=v_ret_w_in,
             ret_gn_w=v_ret_gn_w, ret_w_out=v_ret_w_out, dn_w_in=v_dn_w_in, dn_conv_w=v_dn_conv_w,
             dn_a_log=v_dn_a_log, dn_dt_bias=v_dn_dt_bias, dn_norm_w=v_dn_norm_w, dn_w_out=v_dn_w_out,
             ffn_w_gate=v_ffn_w_gate, ffn_w_up=v_ffn_w_up, ffn_w_down=v_ffn_w_down, final_norm_w=v_final_norm_w)
    mx, my, mc = _mesh_pos()
    chip = 2 * mx + my

    sm_names = [n for n in SMALL if n in SMALL_SHARDED]
    sm_gathered = _unpack(_gather_small(_pack_lanes([w[n] for n in sm_names])), [w[n].shape for n in sm_names])
    full = {n: _join_cols(sm_gathered[i]) for i, n in enumerate(sm_names)}
    wts = {
        "meta_tokens": full["meta_tokens"], "mix_norm_w": mix_norm_w, "ffn_norm_w": ffn_norm_w,
        "ret_gn_w": ret_gn_w[0], "final_norm_w": final_norm_w, "dn_conv_w": full["dn_conv_w"][0],
        "dn_a_log": dn_a_log[0], "dn_dt_bias": dn_dt_bias[0], "dn_norm_w": full["dn_norm_w"][0],
    }
    idx = jnp.stack([mc, chip]).astype(jnp.int32)
    shards = {n: _halves(w[n].astype(MXU_DTYPE)) for n in BIG}
    loss_part, dh0, g, reduced = _train_step(x[0], loss_target[0], wts, shards, idx)
    seq = x.shape[1]
    grad_x = dh0[CHUNK:CHUNK + seq].reshape(x.shape)
    gsh = {}

    small_full_shapes = [g[n].shape for n in SMALL] + [(1,)]
    red = _unpack(_allreduce_small(_pack_lanes([g[n] for n in SMALL] + [loss_part[0, :1]])), small_full_shapes)
    loss = red[-1][0]
    for i, n in enumerate(SMALL):
        gn = red[i]
        if n in SMALL_SHARDED:
            width = w[n].shape[-1]
            gn = lax.dynamic_slice_in_dim(gn, chip * width, width, axis=gn.ndim - 1)
        gsh[n] = gn.reshape(w[n].shape)

    delta, new_m, new_v = {}, {}, {}
    for n in BIG:
        shp = w[n].shape
        if n in TRANSPOSED_AT_BOUNDARY and TRANSPOSED_AT_BOUNDARY[n]:
            view = lambda a: jnp.swapaxes(a, 1, 2).reshape(1, -1, LANES)
            back = lambda a: jnp.swapaxes(a.reshape(shp[0], shp[2], shp[1]), 1, 2)
        elif n in TRANSPOSED_AT_BOUNDARY:
            view = back = lambda a: jnp.swapaxes(a, 1, 2)
        else:
            view = back = lambda a: a
        res = _adamw(view(w[n]), view(reduced[n].reshape(shp)), view(m[n]), view(v[n]), "adamw_" + n)
        gsh[n], delta[n], new_m[n], new_v[n] = [back(r) for r in res]
    sm_local_shapes = [w[n].shape for n in SMALL]
    _, d_, m_, v_ = _adamw(*[_pack_lanes([t[n] for n in SMALL])[None] for t in (w, gsh, m, v)], "adamw_small")
    d_, m_, v_ = d_[0], m_[0], v_[0]
    for n, dd, mm, vv in zip(SMALL, _unpack(d_, sm_local_shapes), _unpack(m_, sm_local_shapes),
                             _unpack(v_, sm_local_shapes)):
        delta[n], new_m[n], new_v[n] = dd, mm, vv

    return (loss, grad_x, *[gsh[n] for n in ORDER], *[delta[n] for n in ORDER],
            *[new_m[n] for n in ORDER], *[new_v[n] for n in ORDER])
```

```python
import math

import jax
import jax.numpy as jnp
from jax import lax
from jax.experimental import pallas as pl
from jax.experimental.pallas import tpu as pltpu

F32 = jnp.float32
BF16 = jnp.bfloat16
MXU_DTYPE = BF16

D_MODEL = 1024
N_META = 16
CHUNK = 64
PAD = CHUNK - N_META
RMS_EPS = 1e-6
RET_HEADS, RET_DK, RET_DV = 4, 256, 512
RET_QK, RET_V = RET_HEADS * RET_DK, RET_HEADS * RET_DV
RET_IN = 2 * RET_QK + 2 * RET_V
ROPE_BASE = 10000.0
DN_HEADS, DN_DK, DN_DV = 8, 128, 256
DN_QK, DN_V = DN_HEADS * DN_DK, DN_HEADS * DN_DV
DN_CONV_CH = 2 * DN_QK + DN_V
DN_IN = DN_CONV_CH + DN_V + 2 * DN_HEADS
LANES = 128
DN_IN_USED = DN_CONV_CH + DN_V + LANES
DN_IN_PAD = DN_IN_USED + LANES
CONV_K = 4
FFN_HIDDEN = 2816
ADAM_LR, ADAM_B1, ADAM_B2, ADAM_EPS, ADAM_WD, ADAM_STEP = 0.001, 0.9, 0.999, 1e-08, 0.01, 10

ROW_ALIGN = 256
VMEM_LIMIT = 62 * 1024 * 1024
MESH = pl.DeviceIdType.MESH
ANY = pl.BlockSpec(memory_space=pl.ANY)
VMEM_SPEC = pl.BlockSpec(memory_space=pltpu.VMEM)
_HI = lax.Precision.HIGHEST


def _params(*sem):
    return pltpu.CompilerParams(dimension_semantics=sem, vmem_limit_bytes=VMEM_LIMIT)


def _dg(a, b, ca, cb, hi):
    dims = (((ca,), (cb,)), ((), ()))

    def dot(p, q):
        return lax.dot_general(p, q, dims, preferred_element_type=F32)

    if not hi:
        return dot(a.astype(MXU_DTYPE), b.astype(MXU_DTYPE))
    if MXU_DTYPE == F32:
        return lax.dot_general(a, b, dims, precision=_HI, preferred_element_type=F32)
    a_hi, b_hi = a.astype(MXU_DTYPE), b.astype(MXU_DTYPE)
    a_lo = (a - a_hi.astype(F32)).astype(MXU_DTYPE)
    b_lo = (b - b_hi.astype(F32)).astype(MXU_DTYPE)
    return dot(a_hi, b_hi) + (dot(a_hi, b_lo) + dot(a_lo, b_hi))


def _nn(a, b, hi=False):
    return _dg(a, b, 1, 0, hi)


def _nt(a, b, hi=False):
    return _dg(a, b, 1, 1, hi)


def _tn(a, b, hi=False):
    return _dg(a, b, 0, 0, hi)


def _iota(shape, dim):
    return lax.broadcasted_iota(jnp.int32, shape, dim)


def _valid_rows(first_row, rows, seq):
    r = first_row + _iota((rows, 1), 0)
    return ((r >= PAD) & (r < CHUNK + seq)).astype(F32)


def _rope(t, cs, sn):
    half = t.shape[-1] // 2
    t1, t2 = t[:, :half], t[:, half:]
    return jnp.concatenate([t1 * cs - t2 * sn, t1 * sn + t2 * cs], axis=1)


def _rope_bwd(d, cs, sn):
    half = d.shape[-1] // 2
    d1, d2 = d[:, :half], d[:, half:]
    return jnp.concatenate([d1 * cs + d2 * sn, d2 * cs - d1 * sn], axis=1)


def _col(x, idx):
    oh = (_iota((1, x.shape[1]), 1) == idx).astype(F32)
    return jnp.sum(x * oh, axis=1, keepdims=True)


def _row(x, idx):
    oh = (_iota((x.shape[0], 1), 0) == idx).astype(F32)
    return jnp.sum(x * oh, axis=0, keepdims=True)


def _shift_down(x, halo8, k):
    xr = pltpu.roll(x, k, 0)
    hr = pltpu.roll(halo8, k, 0)
    first = jnp.where(_iota((8, 1), 0) < k, hr, xr[0:8])
    return jnp.concatenate([first, xr[8:]], axis=0)


def _shift_up(x, next8, j):
    rows = x.shape[0]
    xr = pltpu.roll(x, rows - j, 0)
    nr = pltpu.roll(next8, 8 - j, 0)
    last = jnp.where(_iota((8, 1), 0) >= 8 - j, nr, xr[rows - 8:])
    return jnp.concatenate([xr[:rows - 8], last], axis=0)


def _gated_norm(o, gate, w):
    r = lax.rsqrt(jnp.mean(o * o, axis=-1, keepdims=True) + RMS_EPS)
    return o * r * w * (gate * jax.nn.sigmoid(gate))


def _gated_norm_bwd(dy, o, gate, w):
    r = lax.rsqrt(jnp.mean(o * o, axis=-1, keepdims=True) + RMS_EPS)
    nrm = o * r
    sg = jax.nn.sigmoid(gate)
    sl = gate * sg
    dgate = dy * nrm * w * (sg * (1.0 + gate * (1.0 - sg)))
    dn = dy * w * sl
    dw = jnp.sum(dy * nrm * sl, axis=0, keepdims=True)
    do = r * (dn - nrm * jnp.mean(dn * nrm, axis=-1, keepdims=True))
    return do, dgate, dw


def _softplus(z):
    return jnp.maximum(z, 0.0) + jnp.log(1.0 + jnp.exp(-jnp.abs(z)))


def _row_tile(rows, cap=1408):
    for t in (1408, 768, 512, 256, 128, 64, 32, 16, 8):
        if t <= cap and rows % t == 0:
            return t
    return rows


TILE_BUDGET = 60 * 1024 * 1024


def _fit_rows(rows, row_bytes, fixed_bytes, value_row_bytes):
    best = None
    for t in range(LANES, rows + 1, LANES):
        if rows % t == 0 and 2 * (row_bytes * t + fixed_bytes) + value_row_bytes * t <= TILE_BUDGET:
            best = t
    return best or _row_tile(rows, 256)


def _div_tile(n, cap, mult):
    best = None
    for t in range(mult, min(cap, n) + 1, mult):
        if n % t == 0:
            best = t
    return best or n


def _col_tile(cols, cap=1536):
    best = None
    for t in range(LANES, min(cap, cols) + 1, LANES):
        if cols % t == 0:
            best = t
    return best or cols


def _embed(x, tgt, meta, rows, ride=None):
    seq, d = x.shape
    n_tok = seq // CHUNK

    def body(xa_ref, xb_ref, ta_ref, tb_ref, m_ref, h_ref, tp_ref):
        i = pl.program_id(0)
        first = jnp.concatenate([jnp.zeros((PAD, d), F32), m_ref[...]], axis=0)
        for half, (x_ref, t_ref) in enumerate(((xa_ref, ta_ref), (xb_ref, tb_ref))):
            k = 2 * i + half
            tokens = (k >= 1) & (k <= n_tok)
            rs = slice(half * CHUNK, (half + 1) * CHUNK)
            h_ref[rs, :] = jnp.where(k == 0, first, jnp.where(tokens, x_ref[...], 0.0))
            tp_ref[rs, :] = jnp.where(tokens, t_ref[...], 0.0)

    def tok(half):
        return pl.BlockSpec((CHUNK, d), lambda i: (jnp.clip(2 * i + half - 1, 0, n_tok - 1), 0))

    out = pl.BlockSpec((2 * CHUNK, d), lambda i: (i, 0))
    res, rode = _pcall(body, [x, x, tgt, tgt, meta], grid=(rows // (2 * CHUNK),),
                       in_specs=[tok(0), tok(1), tok(0), tok(1), pl.BlockSpec((N_META, d), lambda i: (0, 0))],
                       out_specs=[out, out], out_shape=[jax.ShapeDtypeStruct((rows, d), F32)] * 2, name="embed",
                       sem=("parallel",), ride=ride)
    return res if ride is None else (res, rode)


def _rms_fwd(h, w, name, ride=None):
    rows, d = h.shape
    tm = _row_tile(rows)

    def body(h_ref, w_ref, o_ref):
        x = h_ref[...]
        r = lax.rsqrt(jnp.mean(x * x, axis=-1, keepdims=True) + RMS_EPS)
        o_ref[...] = (x * r * w_ref[...]).astype(o_ref.dtype)

    res, rode = _pcall(body, [h, w.reshape(1, d)], grid=(rows // tm,),
                       in_specs=[pl.BlockSpec((tm, d), lambda i: (i, 0)), pl.BlockSpec((1, d), lambda i: (0, 0))],
                       out_specs=[pl.BlockSpec((tm, d), lambda i: (i, 0))],
                       out_shape=[jax.ShapeDtypeStruct((rows, d), BF16)], name=name, sem=("parallel",), ride=ride)
    return res[0] if ride is None else (res[0], rode)


def _gmm_rms(name, grid, args, in_specs, row_spec, fn, h, w, resid, row_axis, red_axis=None, ride=None):
    m, d = h.shape
    n_in = len(args)
    vec = pl.BlockSpec((1, d), lambda *g: (0, 0))

    def body(*refs):
        ins = refs[:n_in]
        h_ref, w_ref, r_ref, dh_ref, dw_ref, dh16_ref = refs[n_in:]
        part = fn(*ins)
        row = pl.program_id(row_axis)

        def finish(dy):
            x = h_ref[...]
            r = lax.rsqrt(jnp.mean(x * x, axis=-1, keepdims=True) + RMS_EPS)
            xh = x * r
            dxh = dy * w_ref[...]
            dh = r_ref[...] + r * (dxh - xh * jnp.mean(dxh * xh, axis=-1, keepdims=True))
            dh_ref[...] = dh
            dh16_ref[...] = dh.astype(dh16_ref.dtype)
            dwp = jnp.sum(dy * xh, axis=0, keepdims=True)

            @pl.when(row == 0)
            def _():
                dw_ref[...] = dwp

            @pl.when(row > 0)
            def _():
                dw_ref[...] += dwp

        if red_axis is None:
            finish(part)
            return
        k = pl.program_id(red_axis)

        @pl.when(k == 0)
        def _():
            dh_ref[...] = part

        @pl.when(k > 0)
        def _():
            dh_ref[...] += part

        @pl.when(k == grid[red_axis] - 1)
        def _():
            finish(dh_ref[...])

    res, rode = _pcall(body, list(args) + [h, w.reshape(1, d), resid], grid=grid,
                       in_specs=list(in_specs) + [row_spec, vec, row_spec], out_specs=[row_spec, vec, row_spec],
                       out_shape=[jax.ShapeDtypeStruct((m, d), F32), jax.ShapeDtypeStruct((1, d), F32),
                                  jax.ShapeDtypeStruct((m, d), BF16)],
                       name=name, sem=("arbitrary",) * len(grid), ride=ride)
    return res if ride is None else (res, rode)


def _final_loss(h, w, tgt, seq, name):
    rows, d = h.shape
    tm = _row_tile(rows)

    def body(h_ref, w_ref, t_ref, dh_ref, dw_ref, loss_ref, dh16_ref):
        i = pl.program_id(0)
        r_idx = i * tm + _iota((tm, 1), 0)
        m = ((r_idx >= CHUNK) & (r_idx < CHUNK + seq)).astype(F32)
        x = h_ref[...]
        wv = w_ref[...]
        r = lax.rsqrt(jnp.mean(x * x, axis=-1, keepdims=True) + RMS_EPS)
        xh = x * r
        err = (xh * wv - t_ref[...]) * m
        lpart = 0.5 * jnp.sum(jnp.mean(err * err, axis=-1, keepdims=True), axis=0, keepdims=True)
        dyv = err * (1.0 / d)
        dxh = dyv * wv
        dh = r * (dxh - xh * jnp.mean(dxh * xh, axis=-1, keepdims=True))
        dh_ref[...] = dh
        dh16_ref[...] = dh.astype(dh16_ref.dtype)
        part = jnp.sum(dyv * xh, axis=0, keepdims=True)

        @pl.when(i == 0)
        def _():
            dw_ref[...] = part
            loss_ref[...] = jnp.broadcast_to(lpart, loss_ref.shape)

        @pl.when(i > 0)
        def _():
            dw_ref[...] += part
            loss_ref[...] += jnp.broadcast_to(lpart, loss_ref.shape)

    blk = pl.BlockSpec((tm, d), lambda i: (i, 0))
    vec = pl.BlockSpec((1, d), lambda i: (0, 0))
    return pl.pallas_call(
        body, grid=(rows // tm,), in_specs=[blk, vec, blk],
        out_specs=[blk, vec, pl.BlockSpec((1, LANES), lambda i: (0, 0)), blk],
        out_shape=[jax.ShapeDtypeStruct((rows, d), F32), jax.ShapeDtypeStruct((1, d), F32),
                   jax.ShapeDtypeStruct((1, LANES), F32), jax.ShapeDtypeStruct((rows, d), BF16)],
        name=name, compiler_params=_params("arbitrary"))(h, w.reshape(1, d), tgt)


def _isz(x):
    return jnp.dtype(x.dtype).itemsize


def _mm(a, b, *, mode, name, out_dtype=F32, resid=None, col_cap=1536, ride=None):
    if mode == "tn":
        m, k = a.shape
        n = b.shape[1]
        tn = _col_tile(n, col_cap)
        tm = _fit_rows(m, k * _isz(a) + tn * _isz(b), (3 * k * tn * 4) // 2, 2 * (k + tn))

        def body_tn(a_ref, b_ref, o_ref):
            i = pl.program_id(1)
            part = _tn(a_ref[...], b_ref[...])

            @pl.when(i == 0)
            def _():
                o_ref[...] = part

            @pl.when(i > 0)
            def _():
                o_ref[...] += part

        return pl.pallas_call(
            body_tn, grid=(n // tn, m // tm),
            in_specs=[pl.BlockSpec((tm, k), lambda j, i: (i, 0)),
                      pl.BlockSpec((tm, tn), lambda j, i: (i, j))],
            out_specs=pl.BlockSpec((k, tn), lambda j, i: (0, j)),
            out_shape=jax.ShapeDtypeStruct((k, n), F32), name=name,
            compiler_params=_params("parallel", "arbitrary"))(a, b)

    m, ka = a.shape
    n = b.shape[1] if mode == "nn" else b.shape[0]
    has_resid = resid is not None
    tn = _col_tile(n, col_cap)
    tm = _fit_rows(m, ka * _isz(a) + tn * (jnp.dtype(out_dtype).itemsize + (4 if has_resid else 0)),
                   ka * tn * _isz(b), 2 * ka + 8 * tn)

    def body(*refs):
        if has_resid:
            a_ref, b_ref, r_ref, o_ref = refs
        else:
            a_ref, b_ref, o_ref = refs
        acc = _nn(a_ref[...], b_ref[...]) if mode == "nn" else _nt(a_ref[...], b_ref[...])
        if has_resid:
            acc = acc + r_ref[...]
        o_ref[...] = acc.astype(o_ref.dtype)

    b_spec = (pl.BlockSpec((b.shape[0], tn), lambda j, i: (0, j)) if mode == "nn"
              else pl.BlockSpec((tn, b.shape[1]), lambda j, i: (j, 0)))
    o_spec = pl.BlockSpec((tm, tn), lambda j, i: (i, j))
    in_specs = [pl.BlockSpec((tm, ka), lambda j, i: (i, 0)), b_spec]
    args = [a, b]
    if has_resid:
        in_specs.append(o_spec)
        args.append(resid)
    res, rode = _pcall(body, args, grid=(n // tn, m // tm), in_specs=in_specs, out_specs=[o_spec],
                       out_shape=[jax.ShapeDtypeStruct((m, n), out_dtype)], name=name,
                       sem=("parallel", "parallel"), ride=ride)
    return res[0] if ride is None else (res[0], rode)


N_SHARD = 4


def _gmm(name, grid, args, in_specs, out_specs, out_shape, fn, red_axis=None, init_arg=None, aliases=None,
         ride=None):
    n_in = len(args)
    single = not isinstance(out_shape, (list, tuple))
    out_specs = [out_specs] if single else list(out_specs)
    out_shape = [out_shape] if single else list(out_shape)

    def body(*refs):
        _gmm_step(fn, refs[:n_in], refs[n_in:], red_axis, init_arg)

    sem = tuple("arbitrary" if ax == red_axis else "parallel" for ax in range(len(grid)))
    res, rode = _pcall(body, args, grid=grid, in_specs=in_specs, out_specs=out_specs, out_shape=out_shape,
                       name=name, sem=sem, aliases=aliases, ride=ride)
    ours = res[0] if single else res
    return ours if ride is None else (ours, rode)


def _gmm_step(fn, ins, outs, red_axis, init_arg):
    parts = fn(*ins)
    if red_axis is None:
        for o_ref, p in zip(outs, parts):
            o_ref[...] = p.astype(o_ref.dtype)
        return
    k = pl.program_id(red_axis)

    @pl.when(k == 0)
    def _():
        for idx, (o_ref, p) in enumerate(zip(outs, parts)):
            o_ref[...] = p + ins[init_arg][...] if (idx == 0 and init_arg is not None) else p

    @pl.when(k > 0)
    def _():
        for o_ref, p in zip(outs, parts):
            o_ref[...] += p


def _ride_body(ride, grid, n_in, n_out, n_scratch, body):
    n_rin, n_rout = len(ride.arrays), len(ride.out_shape)
    nsteps = math.prod(grid)

    def wrapped(*refs):
        ins = refs[:n_in]
        r_ins = refs[n_in:n_in + n_rin]
        o0 = n_in + n_rin
        outs = refs[o0:o0 + n_out]
        r_outs = refs[o0 + n_out:o0 + n_out + n_rout]
        s0 = o0 + n_out + n_rout
        scratch = refs[s0:s0 + n_scratch]
        send_sems, recv_sems = refs[-2:]
        step = pl.program_id(0)
        for ax in range(1, len(grid)):
            step = step * grid[ax] + pl.program_id(ax)
        ride.emit(step, nsteps, r_ins, r_outs, send_sems, recv_sems, before=True)
        body(*ins, *outs, *scratch)
        ride.emit(step, nsteps, r_ins, r_outs, send_sems, recv_sems, before=False)

    return wrapped


def _pcall(body, args, *, grid, in_specs, out_specs, out_shape, name, sem, scratch=(), aliases=None, ride=None):
    if ride is None:
        res = pl.pallas_call(body, grid=grid, in_specs=list(in_specs), out_specs=list(out_specs),
                             out_shape=list(out_shape), scratch_shapes=list(scratch), name=name,
                             input_output_aliases=aliases or {}, compiler_params=_params(*sem))(*args)
        return res, None
    n_in, n_out = len(args), len(out_shape)
    res = pl.pallas_call(
        _ride_body(ride, grid, n_in, n_out, len(scratch), body), grid=grid,
        in_specs=list(in_specs) + ride.in_specs, out_specs=list(out_specs) + ride.out_specs,
        out_shape=list(out_shape) + ride.out_shape, scratch_shapes=list(scratch) + ride.scratch, name=name,
        input_output_aliases=aliases or {},
        compiler_params=_params(*(("arbitrary",) * len(grid))))(*args, *ride.arrays)
    return res[:n_out], res[n_out:]


def _mm_cols(a, ws, name, ride=None):
    m, k = a.shape
    n = ws.shape[2]
    tm = _fit_rows(m, k * _isz(a) + n * 4, k * n * _isz(ws), 4 * n)
    return _gmm(name, (N_SHARD, m // tm), [a, ws],
                [pl.BlockSpec((tm, k), lambda j, i: (i, 0)), pl.BlockSpec((None, k, n), lambda j, i: (j, 0, 0))],
                pl.BlockSpec((tm, n), lambda j, i: (i, j)), jax.ShapeDtypeStruct((m, N_SHARD * n), F32),
                lambda a_ref, w_ref: (_nn(a_ref[...], w_ref[...]),), ride=ride)


def _mm_cols_t_rms(d, ws, h, w, resid, name, ride=None):
    m = d.shape[0]
    _, k, n = ws.shape
    tm = _fit_rows(m, n * _isz(d) + 3 * k * 4, k * n * _isz(ws), 16 * k)
    return _gmm_rms(name, (m // tm, N_SHARD), [d, ws],
                    [pl.BlockSpec((tm, n), lambda i, j: (i, j)), pl.BlockSpec((None, k, n), lambda i, j: (j, 0, 0))],
                    pl.BlockSpec((tm, k), lambda i, j: (i, 0)),
                    lambda d_ref, w_ref: _nt(d_ref[...], w_ref[...]), h, w, resid, 0, red_axis=1, ride=ride)


def _mm_nt_rms(a, b, h, w, resid, name, ride=None):
    m, n = a.shape
    k = b.shape[0]
    tm = _fit_rows(m, n * _isz(a) + 3 * k * 4, k * n * _isz(b), 16 * k)
    return _gmm_rms(name, (m // tm,), [a, b],
                    [pl.BlockSpec((tm, n), lambda i: (i, 0)), pl.BlockSpec((k, n), lambda i: (0, 0))],
                    pl.BlockSpec((tm, k), lambda i: (i, 0)),
                    lambda a_ref, b_ref: _nt(a_ref[...], b_ref[...]), h, w, resid, 0, ride=ride)


def _mm_cols_grad(a, d, name):
    m, k = a.shape
    n = d.shape[1] // N_SHARD
    tm = _fit_rows(m, k * _isz(a) + n * _isz(d), (3 * k * n * 4) // 2, 2 * (k + n))
    return _gmm(name, (N_SHARD, m // tm), [a, d],
                [pl.BlockSpec((tm, k), lambda j, i: (i, 0)), pl.BlockSpec((tm, n), lambda j, i: (i, j))],
                pl.BlockSpec((None, k, n), lambda j, i: (j, 0, 0)), jax.ShapeDtypeStruct((N_SHARD, k, n), F32),
                lambda a_ref, d_ref: (_tn(a_ref[...], d_ref[...]),), red_axis=1)


def _ffn_up(hn, wg, wu, layer, name, ride=None):
    m, k = hn.shape
    n = wg.shape[3]
    tm = _fit_rows(m, k * _isz(hn) + 3 * n * jnp.dtype(BF16).itemsize, 2 * k * n * _isz(wg), 16 * n)

    def fn(a_ref, wg_ref, wu_ref):
        a = a_ref[...]
        g = _nn(a, wg_ref[...])
        u = _nn(a, wu_ref[...])
        return g, u, g * jax.nn.sigmoid(g) * u

    w_spec = pl.BlockSpec((None, None, k, n), lambda j, i: (j, layer, 0, 0))
    o_spec = pl.BlockSpec((None, tm, n), lambda j, i: (j, i, 0))
    out = jax.ShapeDtypeStruct((N_SHARD, m, n), BF16)
    return _gmm(name, (N_SHARD, m // tm), [hn, wg, wu],
                [pl.BlockSpec((tm, k), lambda j, i: (i, 0)), w_spec, w_spec],
                [o_spec, o_spec, o_spec], [out, out, out], fn, ride=ride)


def _ffn_down(act, wd, resid, layer, name, ride=None):
    _, m, n = act.shape
    d = wd.shape[3]
    tm = _fit_rows(m, N_SHARD * n * _isz(act) + 2 * d * 4, N_SHARD * n * d * _isz(wd), 8 * d)

    def fn(a_ref, w_ref, r_ref):
        acc = r_ref[...]
        for j in range(N_SHARD):
            acc = acc + _nn(a_ref[j], w_ref[j])
        return (acc,)

    row = pl.BlockSpec((tm, d), lambda i: (i, 0))
    return _gmm(name, (m // tm,), [act, wd, resid],
                [pl.BlockSpec((N_SHARD, tm, n), lambda i: (0, i, 0)),
                 pl.BlockSpec((N_SHARD, None, n, d), lambda i: (0, layer, 0, 0)), row],
                row, jax.ShapeDtypeStruct((m, d), F32), fn, ride=ride)


def _ffn_down_bwd(dh, wd, g, u, layer, name, ride=None):
    m, d = dh.shape
    n = wd.shape[2]
    tm = _fit_rows(m, d * _isz(dh) + 4 * N_SHARD * n * jnp.dtype(BF16).itemsize, N_SHARD * n * d * _isz(wd),
                   2 * d + 24 * n)

    def body(dh_ref, wd_ref, g_ref, u_ref, dg_ref, du_ref):
        dhv = dh_ref[...].astype(MXU_DTYPE)
        for j in range(N_SHARD):
            dact = _nt(dhv, wd_ref[j])
            gv = g_ref[j].astype(F32)
            sg = jax.nn.sigmoid(gv)
            gs = gv * sg
            dg_ref[j] = (dact * u_ref[j].astype(F32) * (sg + gs * (1.0 - sg))).astype(dg_ref.dtype)
            du_ref[j] = (dact * gs).astype(du_ref.dtype)

    sh_spec = pl.BlockSpec((N_SHARD, tm, n), lambda i: (0, i, 0))
    out = jax.ShapeDtypeStruct((N_SHARD, m, n), BF16)
    res, rode = _pcall(body, [dh, wd, g, u], grid=(m // tm,),
                       in_specs=[pl.BlockSpec((tm, d), lambda i: (i, 0)),
                                 pl.BlockSpec((N_SHARD, None, n, d), lambda i: (0, layer, 0, 0)), sh_spec, sh_spec],
                       out_specs=[sh_spec, sh_spec], out_shape=[out, out], name=name, sem=("parallel",), ride=ride)
    return res if ride is None else (res, rode)


def _ffn_up_bwd(dg, du, wg, wu, layer, h, w, resid, name, ride=None):
    _, m, n = dg.shape
    k = wg.shape[2]
    tm = _fit_rows(m, 2 * N_SHARD * n * _isz(dg) + 3 * k * 4, 2 * N_SHARD * k * n * _isz(wg), 16 * k)

    def fn(dg_ref, du_ref, wg_ref, wu_ref):
        acc = _nt(dg_ref[0], wg_ref[0]) + _nt(du_ref[0], wu_ref[0])
        for j in range(1, N_SHARD):
            acc = acc + _nt(dg_ref[j], wg_ref[j]) + _nt(du_ref[j], wu_ref[j])
        return acc

    d_spec = pl.BlockSpec((N_SHARD, tm, n), lambda i: (0, i, 0))
    w_spec = pl.BlockSpec((N_SHARD, None, k, n), lambda i: (0, layer, 0, 0))
    return _gmm_rms(name, (m // tm,), [dg, du, wg, wu], [d_spec, d_spec, w_spec, w_spec],
                    pl.BlockSpec((tm, k), lambda i: (i, 0)), fn, h, w, resid, 0, ride=ride)


def _ffn_wgrad(lhs, rhs_list, layer, layers, prev, lhs_sharded, name):
    if lhs_sharded:
        _, m, k = lhs.shape
        n = rhs_list[0].shape[1]
    else:
        m, k = lhs.shape
        n = rhs_list[0].shape[2]
    n_out = len(rhs_list)
    tm = _fit_rows(m, k * _isz(lhs) + n_out * n * _isz(rhs_list[0]), (3 * n_out * k * n * 4) // 2,
                   2 * (k + n_out * n))
    sh = pl.BlockSpec((None, tm, k if lhs_sharded else n), lambda j, i: (j, i, 0))
    fl = pl.BlockSpec((tm, n if lhs_sharded else k), lambda j, i: (i, 0))
    n_out = len(rhs_list)
    args = [lhs] + list(rhs_list)
    in_specs = [sh if lhs_sharded else fl] + [fl if lhs_sharded else sh] * n_out
    aliases = None
    if prev is not None:
        aliases = {len(args) + t: t for t in range(n_out)}
        args = args + list(prev)
        in_specs = in_specs + [ANY] * n_out

    def fn(l_ref, *rest):
        lv = l_ref[...]
        return tuple(_tn(lv, r_ref[...]) for r_ref in rest[:n_out])

    o_spec = pl.BlockSpec((None, None, k, n), lambda j, i: (j, layer, 0, 0))
    out = jax.ShapeDtypeStruct((N_SHARD, layers, k, n), F32)
    return _gmm(name, (N_SHARD, m // tm), args, in_specs, [o_spec] * n_out, [out] * n_out, fn,
                red_axis=1, aliases=aliases)


def _ret_consts():
    log_gamma = jnp.log1p(-jnp.exp2(-5.0 - jnp.arange(RET_HEADS, dtype=F32)))
    idx = jnp.arange(CHUNK, dtype=F32)
    rel = idx[:, None] - idx[None, :]
    dmask = jnp.where((rel >= 0)[None], jnp.exp(log_gamma[:, None, None] * jnp.maximum(rel, 0.0)), 0.0)
    xi = jnp.exp(log_gamma[:, None] * (idx[None, :] + 1.0))[:, :, None]
    zeta = jnp.exp(log_gamma[:, None] * (CHUNK - 1.0 - idx[None, :]))[:, :, None]
    gamma_c = jnp.exp(log_gamma * CHUNK)
    wide = (RET_HEADS, CHUNK, RET_DK)
    return dmask, jnp.broadcast_to(xi, wide), jnp.broadcast_to(zeta, wide), gamma_c


def _rope_tables(nc):
    half = RET_DK // 2
    inv_freq = ROPE_BASE ** (-jnp.arange(half, dtype=F32) / half)
    a_chunk = (jnp.arange(nc) * CHUNK - PAD).astype(F32)[:, None] * inv_freq[None, :]
    a_row = jnp.arange(CHUNK).astype(F32)[:, None] * inv_freq[None, :]
    return (jnp.stack([jnp.cos(a_chunk), jnp.sin(a_chunk)], axis=1),
            jnp.stack([jnp.cos(a_row), jnp.sin(a_row)], axis=0))


RET_CPS = 4


def _rope_chunk(rc_ref, rr_ref, c):
    cc, sc = rc_ref[c, 0:1, :], rc_ref[c, 1:2, :]
    cr, sr = rr_ref[0], rr_ref[1]
    return cc * cr - sc * sr, sc * cr + cc * sr


def _rope_specs(order):
    half = RET_DK // 2
    return [pl.BlockSpec((RET_CPS, 2, half), lambda n: (order(n), 0, 0)),
            pl.BlockSpec((2, CHUNK, half), lambda n: (0, 0, 0))]


def _ret_specs(order):
    rows = RET_CPS * CHUNK
    return [pl.BlockSpec((rows, RET_QK), lambda n: (order(n), 0)),
            pl.BlockSpec((rows, RET_QK), lambda n: (order(n), 1)),
            pl.BlockSpec((rows, RET_V), lambda n: (order(n), 1)),
            pl.BlockSpec((rows, RET_V), lambda n: (order(n), 2))]


def _ret_const_specs():
    return [pl.BlockSpec((RET_HEADS, CHUNK, CHUNK), lambda n: (0, 0, 0)),
            pl.BlockSpec((RET_HEADS, CHUNK, RET_DK), lambda n: (0, 0, 0)),
            pl.BlockSpec((RET_HEADS, CHUNK, RET_DK), lambda n: (0, 0, 0)),
            pl.BlockSpec((1, RET_DV), lambda n: (0, 0))]


def _ret_fwd(proj, cos, sin, consts, gn_w, seq, ride=None):
    rows = proj.shape[0]
    nc = rows // CHUNK
    dmask, xi, zeta, gamma_c = consts

    def body(gam_ref, q_ref, k_ref, v_ref, g_ref, cos_ref, sin_ref, dm_ref, xi_ref, ze_ref, gn_ref,
             o_ref, y_ref, ss_ref, s_ref):
        n = pl.program_id(0)

        @pl.when(n == 0)
        def _():
            s_ref[...] = jnp.zeros_like(s_ref)

        gn = gn_ref[...]
        hs = range(RET_HEADS)
        qk_cols = [slice(h * RET_DK, (h + 1) * RET_DK) for h in hs]
        v_cols = [slice(h * RET_DV, (h + 1) * RET_DV) for h in hs]
        for c in range(RET_CPS):
            rs = slice(c * CHUNK, (c + 1) * CHUNK)
            cs, sn = _rope_chunk(cos_ref, sin_ref, c)
            kscale = _valid_rows((n * RET_CPS + c) * CHUNK, CHUNK, seq) * (RET_DK ** -0.5)
            qr_l = [_rope(q_ref[rs, col], cs, sn) for col in qk_cols]
            kr_l = [_rope(k_ref[rs, col], cs, sn) * kscale for col in qk_cols]
            v_l = [v_ref[rs, col] for col in v_cols]
            s_l = [s_ref[h] for h in hs]
            sc_l = [_nt(qr, kr) * dm_ref[h] for h, (qr, kr) in enumerate(zip(qr_l, kr_l))]
            o_l = [_nn(sc_l[h], v_l[h]) + _nn(qr_l[h] * xi_ref[h], s_l[h]) for h in hs]
            for h in hs:
                ss_ref[c, h] = s_l[h].astype(ss_ref.dtype)
                s_ref[h] = gam_ref[h] * s_l[h] + _tn(kr_l[h] * ze_ref[h], v_l[h])
                o_ref[rs, v_cols[h]] = o_l[h]
                y_ref[rs, v_cols[h]] = _gated_norm(o_l[h], g_ref[rs, v_cols[h]], gn).astype(y_ref.dtype)

    fwd = lambda n: n
    row_v = pl.BlockSpec((RET_CPS * CHUNK, RET_V), lambda n: (n, 0))
    res, rode = _pcall(
        body, [gamma_c, proj, proj, proj, proj, cos, sin, dmask, xi, zeta, gn_w.reshape(1, RET_DV)],
        grid=(nc // RET_CPS,),
        in_specs=[pl.BlockSpec(memory_space=pltpu.SMEM)] + _ret_specs(fwd) + _rope_specs(fwd)
        + _ret_const_specs(),
        out_specs=[row_v, row_v,
                   pl.BlockSpec((RET_CPS, RET_HEADS, RET_DK, RET_DV), lambda n: (n, 0, 0, 0))],
        out_shape=[jax.ShapeDtypeStruct((rows, RET_V), F32), jax.ShapeDtypeStruct((rows, RET_V), BF16),
                   jax.ShapeDtypeStruct((nc, RET_HEADS, RET_DK, RET_DV), BF16)],
        scratch=[pltpu.VMEM((RET_HEADS, RET_DK, RET_DV), F32)], name="ret_fwd", sem=("arbitrary",), ride=ride)
    return res if ride is None else (res, rode)


def _ret_bwd(proj, o, dy, states, cos, sin, consts, gn_w, seq, ride=None):
    rows = proj.shape[0]
    nc = rows // CHUNK
    dmask, xi, zeta, gamma_c = consts

    def body(gam_ref, q_ref, k_ref, v_ref, g_ref, o_ref, dy_ref, ss_ref, cos_ref, sin_ref,
             dm_ref, xi_ref, ze_ref, gn_ref, dp_ref, dgn_ref, ds_ref):
        n = pl.program_id(0)

        @pl.when(n == 0)
        def _():
            ds_ref[...] = jnp.zeros_like(ds_ref)
            dgn_ref[...] = jnp.zeros_like(dgn_ref)

        gn = gn_ref[...]
        dgn = jnp.zeros((1, RET_DV), F32)
        hs = range(RET_HEADS)
        qk_cols = [slice(h * RET_DK, (h + 1) * RET_DK) for h in hs]
        v_cols = [slice(h * RET_DV, (h + 1) * RET_DV) for h in hs]
        for c in reversed(range(RET_CPS)):
            rs = slice(c * CHUNK, (c + 1) * CHUNK)
            cs, sn = _rope_chunk(cos_ref, sin_ref, c)
            kscale = _valid_rows(((steps - 1 - n) * RET_CPS + c) * CHUNK, CHUNK, seq) * (RET_DK ** -0.5)
            qr_l = [_rope(q_ref[rs, col], cs, sn) for col in qk_cols]
            kr_l = [_rope(k_ref[rs, col], cs, sn) * kscale for col in qk_cols]
            v_l = [v_ref[rs, col] for col in v_cols]
            s_l = [ss_ref[c, h] for h in hs]
            ds_l = [ds_ref[h] for h in hs]
            sc_l = [_nt(qr_l[h], kr_l[h]) * dm_ref[h] for h in hs]
            gnb = [_gated_norm_bwd(dy_ref[rs, col], o_ref[rs, col], g_ref[rs, col], gn) for col in v_cols]
            do_l = [x[0] for x in gnb]
            dsc_l = [_nt(do_l[h], v_l[h]) * dm_ref[h] for h in hs]
            dv_l = [_tn(sc_l[h], do_l[h]) + _nn(kr_l[h] * ze_ref[h], ds_l[h]) for h in hs]
            dqr_l = [_nn(dsc_l[h], kr_l[h]) + _nt(do_l[h], s_l[h]) * xi_ref[h] for h in hs]
            dkr_l = [_tn(dsc_l[h], qr_l[h]) + _nt(v_l[h], ds_l[h]) * ze_ref[h] for h in hs]
            for h in hs:
                dgn = dgn + gnb[h][2]
                ds_ref[h] = gam_ref[h] * ds_l[h] + _tn(qr_l[h] * xi_ref[h], do_l[h])
                dp_ref[rs, qk_cols[h]] = _rope_bwd(dqr_l[h], cs, sn).astype(dp_ref.dtype)
                dp_ref[rs, RET_QK + h * RET_DK:RET_QK + (h + 1) * RET_DK] = (
                    _rope_bwd(dkr_l[h] * kscale, cs, sn).astype(dp_ref.dtype))
                dp_ref[rs, 2 * RET_QK + h * RET_DV:2 * RET_QK + (h + 1) * RET_DV] = dv_l[h].astype(dp_ref.dtype)
                dp_ref[rs, 2 * RET_QK + RET_V + h * RET_DV:2 * RET_QK + RET_V + (h + 1) * RET_DV] = (
                    gnb[h][1].astype(dp_ref.dtype))
        dgn_ref[...] += dgn

    steps = nc // RET_CPS
    rev = lambda n: steps - 1 - n
    row_v = pl.BlockSpec((RET_CPS * CHUNK, RET_V), lambda n: (rev(n), 0))
    res, rode = _pcall(
        body, [gamma_c, proj, proj, proj, proj, o, dy, states, cos, sin, dmask, xi, zeta,
               gn_w.reshape(1, RET_DV)],
        grid=(steps,),
        in_specs=[pl.BlockSpec(memory_space=pltpu.SMEM)] + _ret_specs(rev) + [
            row_v, row_v, pl.BlockSpec((RET_CPS, RET_HEADS, RET_DK, RET_DV), lambda n: (rev(n), 0, 0, 0))]
        + _rope_specs(rev) + _ret_const_specs(),
        out_specs=[pl.BlockSpec((RET_CPS * CHUNK, RET_IN), lambda n: (rev(n), 0)),
                   pl.BlockSpec((1, RET_DV), lambda n: (0, 0))],
        out_shape=[jax.ShapeDtypeStruct((rows, RET_IN), BF16), jax.ShapeDtypeStruct((1, RET_DV), F32)],
        scratch=[pltpu.VMEM((RET_HEADS, RET_DK, RET_DV), F32)], name="ret_bwd", sem=("arbitrary",), ride=ride)
    return res if ride is None else (res, rode)


GATE_COL = DN_CONV_CH // DN_V
BA_COL = (DN_CONV_CH + DN_V) // LANES
BETA_LANE, DECAY_LANE = 0, DN_HEADS
INV_SHIFT = 4
INV_SQUARINGS = INV_SHIFT - 1
assert CHUNK == 4 << INV_SHIFT


DN_CPS = 2


def _dn_in_specs(order, conv_saved=False):
    rows = DN_CPS * CHUNK
    return [pl.BlockSpec((rows, DN_CONV_CH), lambda n: (order(n), 0)),
            pl.BlockSpec((rows, DN_CONV_CH), lambda n: (order(n), 0)) if conv_saved else
            pl.BlockSpec((8, DN_CONV_CH), lambda n: (jnp.maximum(order(n) * (rows // 8) - 1, 0), 0)),
            pl.BlockSpec((rows, DN_V), lambda n: (order(n), GATE_COL)),
            pl.BlockSpec((rows, LANES), lambda n: (order(n), BA_COL)),
            pl.BlockSpec((CONV_K, 1, DN_CONV_CH), lambda n: (0, 0, 0)),
            pl.BlockSpec((1, LANES), lambda n: (0, 0)),
            pl.BlockSpec((1, LANES), lambda n: (0, 0)),
            pl.BlockSpec((1, DN_DV), lambda n: (0, 0))]


def _dn_front(c, seq, x, halo, ba, cw_ref, al_ref, dt_ref, yc=None):
    valid = _valid_rows(c * CHUNK, CHUNK, seq)
    xin = x * valid
    if yc is None:
        halo = halo * _valid_rows(c * CHUNK - 8, 8, seq)
        yc = xin * cw_ref[CONV_K - 1]
        for k in range(1, CONV_K):
            yc = yc + _shift_down(xin, halo, k) * cw_ref[CONV_K - 1 - k]
    sgc = jax.nn.sigmoid(yc)
    sig = jax.nn.sigmoid(ba)
    beta = sig * valid
    z = ba + dt_ref[...]
    eal = jnp.exp(al_ref[...])
    g = -eal * _softplus(z) * valid
    ri, ci = _iota((CHUNK, CHUNK), 0), _iota((CHUNK, CHUNK), 1)
    lower = (ri >= ci).astype(F32)
    upper = (ri <= ci).astype(F32)
    eye = (ri == ci).astype(F32)
    gam = _nn(lower, g, hi=True)
    gam_t = _tn(g, upper, hi=True)
    return dict(valid=valid, xin=xin, yc=yc, sgc=sgc, act=yc * sgc, sig=sig, beta=beta, z=z,
                eal=eal, g=g, gam=gam, gam_t=gam_t, ri=ri, ci=ci, upper=upper, eye=eye)


def _dn_head(f, h):
    act = f["act"]
    q_raw = act[:, h * DN_DK:(h + 1) * DN_DK]
    k_raw = act[:, DN_QK + h * DN_DK:DN_QK + (h + 1) * DN_DK]
    v = act[:, 2 * DN_QK + h * DN_DV:2 * DN_QK + (h + 1) * DN_DV]
    rq = lax.rsqrt(jnp.sum(q_raw * q_raw, axis=-1, keepdims=True) + RMS_EPS)
    rk = lax.rsqrt(jnp.sum(k_raw * k_raw, axis=-1, keepdims=True) + RMS_EPS)
    qh = q_raw * rq
    kn = k_raw * rk
    gam_c = _col(f["gam"], DECAY_LANE + h)
    gam_r = _row(f["gam_t"], DECAY_LANE + h)
    bc = _col(f["beta"], BETA_LANE + h)
    diff = gam_c - gam_r
    decay = jnp.where(f["ri"] >= f["ci"], jnp.exp(jnp.minimum(diff, 0.0)), 0.0)
    glast = jnp.sum(gam_r * (_iota((1, CHUNK), 1) == CHUNK - 1).astype(F32), axis=1, keepdims=True)
    return dict(rq=rq, rk=rk, qh=qh, qn=qh * (DN_DK ** -0.5), kn=kn, v=v, gam_c=gam_c, gam_r=gam_r,
                bc=bc, diff=diff, decay=decay, egam=jnp.exp(gam_c), glast=glast,
                eglast=jnp.exp(glast), ekd=jnp.exp(glast - gam_c))


def _dn_fwd(proj, conv_w, alog, dtb, norm_w, seq):
    rows = proj.shape[0]
    nc = rows // CHUNK

    def body(x_ref, halo_ref, gate_ref, ba_ref, cw_ref, al_ref, dt_ref, nw_ref,
             o_ref, y_ref, ss_ref, t_ref, yc_ref, s_ref):
        n = pl.program_id(0)

        @pl.when(n == 0)
        def _():
            s_ref[...] = jnp.zeros_like(s_ref)

        nw = nw_ref[...]
        pre = []
        for c in range(DN_CPS):
            rs = slice(c * CHUNK, (c + 1) * CHUNK)
            halo = halo_ref[...] if c == 0 else x_ref[c * CHUNK - 8:c * CHUNK, :]
            f = _dn_front(n * DN_CPS + c, seq, x_ref[rs, :], halo, ba_ref[rs, :], cw_ref, al_ref, dt_ref)
            yc_ref[rs, :] = f["yc"]
            ri, ci = f["ri"], f["ci"]
            eye = f["eye"]
            diag_m = (jnp.right_shift(ri, INV_SHIFT) == jnp.right_shift(ci, INV_SHIFT)).astype(F32)
            half_m = (jnp.right_shift(ri, INV_SHIFT + 1) == jnp.right_shift(ci, INV_SHIFT + 1)).astype(F32)
            heads = [_dn_head(f, h) for h in range(DN_HEADS)]
            a_all = [jnp.where(ri > ci, hd["bc"] * _nt(hd["kn"], hd["kn"]) * hd["decay"], 0.0) for hd in heads]
            b_all = [a * diag_m for a in a_all]
            t_all = [eye - b for b in b_all]
            for _ in range(INV_SQUARINGS):
                b_all = [_nn(b, b, hi=True) for b in b_all]
                t_all = [t + _nn(t, b, hi=True) for t, b in zip(t_all, b_all)]
            for off_m in (half_m - diag_m, 1.0 - half_m):
                x_all = [_nn(a * off_m, t, hi=True) for a, t in zip(a_all, t_all)]
                t_all = [t - _nn(t, x, hi=True) for t, x in zip(t_all, x_all)]
            u_all = [_nn(t, hd["v"] * hd["bc"], hi=True) for t, hd in zip(t_all, heads)]
            w_all = [_nn(t, hd["kn"] * (hd["bc"] * hd["egam"]), hi=True) for t, hd in zip(t_all, heads)]
            qk_all = [_nt(hd["qn"], hd["kn"]) * hd["decay"] for hd in heads]
            for h in range(DN_HEADS):
                t_ref[c, h] = t_all[h]
            pre.append((heads, u_all, w_all, qk_all))
        for c in range(DN_CPS):
            rs = slice(c * CHUNK, (c + 1) * CHUNK)
            heads, u_all, w_all, qk_all = pre[c]
            s_all = [s_ref[h] for h in range(DN_HEADS)]
            os_all = [_nn(hd["qn"] * hd["egam"], s) for hd, s in zip(heads, s_all)]
            vnew_all = [u - _nn(w, s) for u, w, s in zip(u_all, w_all, s_all)]
            o_all = [os + _nn(qk, vn) for os, qk, vn in zip(os_all, qk_all, vnew_all)]
            snew_all = [s * hd["eglast"] + _tn(hd["kn"] * hd["ekd"], vn)
                        for s, hd, vn in zip(s_all, heads, vnew_all)]
            for h in range(DN_HEADS):
                v_cols = slice(h * DN_DV, (h + 1) * DN_DV)
                ss_ref[c, h] = s_all[h]
                s_ref[h] = snew_all[h]
                o_ref[rs, v_cols] = o_all[h]
                y_ref[rs, v_cols] = _gated_norm(o_all[h], gate_ref[rs, v_cols], nw).astype(y_ref.dtype)

    fwd = lambda n: n
    row_v = pl.BlockSpec((DN_CPS * CHUNK, DN_V), lambda n: (n, 0))
    return pl.pallas_call(
        body, grid=(nc // DN_CPS,), in_specs=_dn_in_specs(fwd),
        out_specs=[row_v, row_v,
                   pl.BlockSpec((DN_CPS, DN_HEADS, DN_DK, DN_DV), lambda n: (n, 0, 0, 0)),
                   pl.BlockSpec((DN_CPS, DN_HEADS, CHUNK, CHUNK), lambda n: (n, 0, 0, 0)),
                   pl.BlockSpec((DN_CPS * CHUNK, DN_CONV_CH), lambda n: (n, 0))],
        out_shape=[jax.ShapeDtypeStruct((rows, DN_V), F32), jax.ShapeDtypeStruct((rows, DN_V), BF16),
                   jax.ShapeDtypeStruct((nc, DN_HEADS, DN_DK, DN_DV), F32),
                   jax.ShapeDtypeStruct((nc, DN_HEADS, CHUNK, CHUNK), F32),
                   jax.ShapeDtypeStruct((rows, DN_CONV_CH), F32)],
        scratch_shapes=[pltpu.VMEM((DN_HEADS, DN_DK, DN_DV), F32)],
        name="dn_fwd", compiler_params=_params("arbitrary"))(
            proj, proj, proj, proj, conv_w, alog, dtb, norm_w.reshape(1, DN_DV))


def _dn_bwd(proj, conv_out, o, dy, states, tinv, conv_w, alog, dtb, norm_w, seq):
    rows = proj.shape[0]
    nc = rows // CHUNK

    def body(x_ref, yc_ref, gate_ref, ba_ref, cw_ref, al_ref, dt_ref, nw_ref,
             o_ref, dy_ref, ss_ref, t_ref,
             dp_ref, dcw_ref, dal_ref, ddt_ref, dnw_ref, ds_ref, nxt_ref):
        n = pl.program_id(0)

        @pl.when(n == 0)
        def _():
            ds_ref[...] = jnp.zeros_like(ds_ref)
            nxt_ref[...] = jnp.zeros_like(nxt_ref)
            dcw_ref[...] = jnp.zeros_like(dcw_ref)
            dal_ref[...] = jnp.zeros_like(dal_ref)
            ddt_ref[...] = jnp.zeros_like(ddt_ref)
            dnw_ref[...] = jnp.zeros_like(dnw_ref)

        for c in reversed(range(DN_CPS)):
            rs = pl.ds(c * CHUNK, CHUNK)
            chunk((steps - 1 - n) * DN_CPS + c, x_ref.at[rs], yc_ref.at[rs], gate_ref.at[rs], ba_ref.at[rs],
                  cw_ref, al_ref, dt_ref, nw_ref, o_ref.at[rs], dy_ref.at[rs], ss_ref.at[c], t_ref.at[c],
                  dp_ref.at[rs], dcw_ref, dal_ref, ddt_ref, dnw_ref, ds_ref, nxt_ref)

    def chunk(ch, x_ref, yc_ref, gate_ref, ba_ref, cw_ref, al_ref, dt_ref, nw_ref,
              o_ref, dy_ref, ss_ref, t_ref,
              dp_ref, dcw_ref, dal_ref, ddt_ref, dnw_ref, ds_ref, nxt_ref):
        f = _dn_front(ch, seq, x_ref[...], None, ba_ref[...], cw_ref, al_ref, dt_ref, yc_ref[...])
        ri, ci = f["ri"], f["ci"]
        strict = (ri > ci).astype(F32)
        nw = nw_ref[...]
        lane128 = _iota((1, LANES), 1)
        row128 = _iota((LANES, 1), 0)
        dgam_col = jnp.zeros((CHUNK, LANES), F32)
        dgam_row = jnp.zeros((LANES, CHUNK), F32)
        dbeta = jnp.zeros((CHUNK, LANES), F32)
        dnw = jnp.zeros((1, DN_DV), F32)
        hs = range(DN_HEADS)
        heads = [_dn_head(f, h) for h in hs]
        cols = [slice(h * DN_DV, (h + 1) * DN_DV) for h in hs]
        t_l = [t_ref[h] for h in hs]
        s_l = [ss_ref[h] for h in hs]
        ds_l = [ds_ref[h] for h in hs]
        kk_l = [_nt(hd["kn"], hd["kn"]) for hd in heads]
        p_l = [_nt(hd["qn"], hd["kn"]) for hd in heads]
        rhsw_l = [hd["kn"] * (hd["bc"] * hd["egam"]) for hd in heads]
        u_l = [_nn(t, hd["v"] * hd["bc"], hi=True) for t, hd in zip(t_l, heads)]
        w_l = [_nn(t, r, hi=True) for t, r in zip(t_l, rhsw_l)]
        vnew_l = [u - _nn(w, s) for u, w, s in zip(u_l, w_l, s_l)]
        gnb = [_gated_norm_bwd(dy_ref[:, c], o_ref[:, c], gate_ref[:, c], nw) for c in cols]
        do_l = [x[0] for x in gnb]
        for h in hs:
            dp_ref[:, DN_CONV_CH + h * DN_DV:DN_CONV_CH + (h + 1) * DN_DV] = gnb[h][1].astype(dp_ref.dtype)
            dnw = dnw + gnb[h][2]
        qg_l = [hd["qn"] * hd["egam"] for hd in heads]
        kd_l = [hd["kn"] * hd["ekd"] for hd in heads]
        dvnew_l = [_tn(p * hd["decay"], do) + _nn(kd, ds)
                   for p, hd, do, kd, ds in zip(p_l, heads, do_l, kd_l, ds_l)]
        m_l = [_nt(do, vn) for do, vn in zip(do_l, vnew_l)]
        dqg_l = [_nt(do, s) for do, s in zip(do_l, s_l)]
        dkd_l = [_nt(vn, ds) for vn, ds in zip(vnew_l, ds_l)]
        for h in hs:
            ds_ref[h] = (ds_l[h] * heads[h]["eglast"] + _tn(qg_l[h], do_l[h]) - _tn(w_l[h], dvnew_l[h]))
        dw_l = [-_nt(dvn, s) for dvn, s in zip(dvnew_l, s_l)]
        dru_l = [_tn(t, dvn, hi=True) for t, dvn in zip(t_l, dvnew_l)]
        drw_l = [_tn(t, dw_, hi=True) for t, dw_ in zip(t_l, dw_l)]
        da_l = [-(_nt(dru, u) + _nt(drw, w)) * strict for dru, u, drw, w in zip(dru_l, u_l, drw_l, w_l)]
        dp_l = [m * hd["decay"] for m, hd in zip(m_l, heads)]
        dkk_l = [da * (hd["bc"] * hd["decay"]) for da, hd in zip(da_l, heads)]
        dqn_l = [dqg * hd["egam"] + _nn(dp, hd["kn"]) for dqg, hd, dp in zip(dqg_l, heads, dp_l)]
        dkn_l = [_tn(dp, hd["qn"]) + dkd * hd["ekd"] + drw * (hd["bc"] * hd["egam"])
                 + _nn(dkk, hd["kn"]) + _tn(dkk, hd["kn"])
                 for dp, hd, dkd, drw, dkk in zip(dp_l, heads, dkd_l, drw_l, dkk_l)]
        dq_parts, dk_parts, dv_parts = [], [], []
        for h in hs:
            hd = heads[h]
            kn, v, bc, egam, decay = hd["kn"], hd["v"], hd["bc"], hd["egam"], hd["decay"]
            t1 = jnp.sum(dkd_l[h] * kd_l[h], axis=1, keepdims=True)
            dglast = (jnp.sum(t1, axis=0, keepdims=True)
                      + jnp.sum(jnp.sum(ds_l[h] * s_l[h], axis=1, keepdims=True), axis=0, keepdims=True)
                      * hd["eglast"])
            e = (m_l[h] * p_l[h] + da_l[h] * (bc * kk_l[h])) * decay
            dgc = (jnp.sum(dqg_l[h] * qg_l[h], axis=1, keepdims=True) - t1
                   + jnp.sum(drw_l[h] * rhsw_l[h], axis=1, keepdims=True)
                   + jnp.sum(e, axis=1, keepdims=True)
                   + jnp.where(_iota((CHUNK, 1), 0) == CHUNK - 1, dglast, 0.0))
            dgr = -jnp.sum(e, axis=0, keepdims=True)
            dbc = (jnp.sum(dru_l[h] * v, axis=1, keepdims=True)
                   + jnp.sum(drw_l[h] * kn, axis=1, keepdims=True) * egam
                   + jnp.sum(da_l[h] * kk_l[h] * decay, axis=1, keepdims=True))
            dv_parts.append(dru_l[h] * bc)
            qh, dqn, dkn = hd["qh"], dqn_l[h], dkn_l[h]
            dq_parts.append(((DN_DK ** -0.5) * hd["rq"])
                            * (dqn - qh * jnp.sum(dqn * qh, axis=1, keepdims=True)))
            dk_parts.append(hd["rk"] * (dkn - kn * jnp.sum(dkn * kn, axis=1, keepdims=True)))
            dgam_col = dgam_col + dgc * (lane128 == DECAY_LANE + h).astype(F32)
            dbeta = dbeta + dbc * (lane128 == BETA_LANE + h).astype(F32)
            dgam_row = dgam_row + (row128 == DECAY_LANE + h).astype(F32) * dgr
        dnw_ref[...] += dnw
        dgam = dgam_col + _nt(f["eye"], dgam_row, hi=True)
        dg = _nn(f["upper"], dgam, hi=True)
        d_a = dg * (-f["eal"]) * jax.nn.sigmoid(f["z"]) * f["valid"]
        dal_ref[...] += jnp.sum(dg * f["g"], axis=0, keepdims=True)
        ddt_ref[...] += jnp.sum(d_a, axis=0, keepdims=True)
        d_b = dbeta * f["valid"] * f["sig"] * (1.0 - f["sig"])
        dp_ref[:, DN_CONV_CH + DN_V:DN_CONV_CH + DN_V + LANES] = (d_a + d_b).astype(dp_ref.dtype)
        dp_ref[:, DN_CONV_CH + DN_V + LANES:] = jnp.zeros((CHUNK, DN_IN_PAD - DN_IN_USED), dp_ref.dtype)
        dact = jnp.concatenate(dq_parts + dk_parts + dv_parts, axis=1)
        yc, sgc = f["yc"], f["sgc"]
        dyc = dact * (sgc * (1.0 + yc * (1.0 - sgc)))
        nxt = nxt_ref[...]
        ups = [dyc] + [_shift_up(dyc, nxt, j) for j in range(1, CONV_K)]
        dx = ups[0] * cw_ref[CONV_K - 1]
        for j in range(1, CONV_K):
            dx = dx + ups[j] * cw_ref[CONV_K - 1 - j]
        for j in range(CONV_K):
            dcw_ref[CONV_K - 1 - j] += jnp.sum(f["xin"] * ups[j], axis=0, keepdims=True)
        nxt_ref[...] = dyc[0:8]
        dp_ref[:, :DN_CONV_CH] = (dx * f["valid"]).astype(dp_ref.dtype)

    steps = nc // DN_CPS
    rev = lambda n: steps - 1 - n
    row_v = pl.BlockSpec((DN_CPS * CHUNK, DN_V), lambda n: (rev(n), 0))
    vec = pl.BlockSpec((1, LANES), lambda n: (0, 0))
    return pl.pallas_call(
        body, grid=(steps,),
        in_specs=_dn_in_specs(rev, conv_saved=True) + [
            row_v, row_v,
            pl.BlockSpec((DN_CPS, DN_HEADS, DN_DK, DN_DV), lambda n: (rev(n), 0, 0, 0)),
            pl.BlockSpec((DN_CPS, DN_HEADS, CHUNK, CHUNK), lambda n: (rev(n), 0, 0, 0))],
        out_specs=[pl.BlockSpec((DN_CPS * CHUNK, DN_IN_PAD), lambda n: (rev(n), 0)),
                   pl.BlockSpec((CONV_K, 1, DN_CONV_CH), lambda n: (0, 0, 0)), vec, vec,
                   pl.BlockSpec((1, DN_DV), lambda n: (0, 0))],
        out_shape=[jax.ShapeDtypeStruct((rows, DN_IN_PAD), BF16),
                   jax.ShapeDtypeStruct((CONV_K, 1, DN_CONV_CH), F32),
                   jax.ShapeDtypeStruct((1, LANES), F32), jax.ShapeDtypeStruct((1, LANES), F32),
                   jax.ShapeDtypeStruct((1, DN_DV), F32)],
        scratch_shapes=[pltpu.VMEM((DN_HEADS, DN_DK, DN_DV), F32), pltpu.VMEM((8, DN_CONV_CH), F32)],
        name="dn_bwd", compiler_params=_params("arbitrary"))(
            proj, conv_out, proj, proj, conv_w, alog, dtb, norm_w.reshape(1, DN_DV), o, dy, states, tinv)


def _train_step(x, tgt, wts, sh, idx):
    seq = x.shape[0]
    rows = -(-(seq + CHUNK) // ROW_ALIGN) * ROW_ALIGN
    wts = dict(wts)
    (h0, tgt_p), (got,) = _embed(x, tgt, wts["meta_tokens"].astype(F32), rows,
                                 ride=_Ride("gather", [sh["ret_w_in"]]))
    wts["ret_w_in"] = got.reshape(N_SHARD, D_MODEL, -1)
    cos, sin = _rope_tables(rows // CHUNK)
    consts = _ret_consts()
    conv_w = wts["dn_conv_w"].reshape(CONV_K, 1, DN_CONV_CH)
    lane_pad = LANES - 2 * DN_HEADS
    alog = jnp.pad(wts["dn_a_log"].reshape(1, DN_HEADS), ((0, 0), (DECAY_LANE, lane_pad)))
    dtb = jnp.pad(wts["dn_dt_bias"].reshape(1, DN_HEADS), ((0, 0), (DECAY_LANE, lane_pad)))
    g = {}

    hn0 = _rms_fwd(h0, wts["mix_norm_w"][0], "rms_mix0")
    proj0, got = _mm_cols(hn0, wts["ret_w_in"], "ret_in",
                          ride=_Ride("gather", [sh["ret_w_out"], sh["ffn_w_gate"]]))
    wts["ret_w_out"] = got[0].reshape(-1, D_MODEL)
    wts["ffn_w_gate"] = got[1]
    (o0, y0, st0), got = _ret_fwd(proj0, cos, sin, consts, wts["ret_gn_w"], seq,
                                  ride=_Ride("gather", [sh["ffn_w_up"], sh["ffn_w_down"]]))
    wts["ffn_w_up"], wts["ffn_w_down"] = got
    h1 = _mm(y0, wts["ret_w_out"], mode="nn", name="ret_out", resid=h0)
    hn1 = _rms_fwd(h1, wts["ffn_norm_w"][0], "rms_ffn0")
    (g0, u0, act0), got = _ffn_up(hn1, wts["ffn_w_gate"], wts["ffn_w_up"], 0, "ffn_up0",
                                  ride=_Ride("gather", [sh["dn_w_in"], sh["dn_w_out"]]))
    n_dn = sh["dn_w_in"].shape[-1]
    dn_shards = got[0].reshape(N_SHARD, D_MODEL, n_dn)
    wts["dn_w_in"] = jnp.concatenate(
        [dn_shards[j] for j in range(N_SHARD)]
        + [jnp.zeros((D_MODEL, DN_IN_PAD - N_SHARD * n_dn), dn_shards.dtype)], axis=-1)
    wts["dn_w_out"] = got[1].reshape(-1, D_MODEL)
    h2 = _ffn_down(act0, wts["ffn_w_down"], h1, 0, "ffn_down0")
    hn2 = _rms_fwd(h2, wts["mix_norm_w"][1], "rms_mix1")
    proj1 = _mm(hn2, wts["dn_w_in"], mode="nn", name="dn_in")
    o1, y1, st1, tinv, conv1 = _dn_fwd(proj1, conv_w, alog, dtb, wts["dn_norm_w"], seq)
    h3 = _mm(y1, wts["dn_w_out"], mode="nn", name="dn_out", resid=h2)
    hn3 = _rms_fwd(h3, wts["ffn_norm_w"][1], "rms_ffn1")
    g1, u1, act1 = _ffn_up(hn3, wts["ffn_w_gate"], wts["ffn_w_up"], 1, "ffn_up1")
    h4 = _ffn_down(act1, wts["ffn_w_down"], h3, 1, "ffn_down1")

    dh4, g["final_norm_w"], loss, dh4b = _final_loss(h4, wts["final_norm_w"], tgt_p, seq, "final_loss")

    layers = wts["ffn_w_gate"].shape[1]

    ffn_names = ["ffn_w_down", "ffn_w_gate", "ffn_w_up"]

    def ffn_bwd(dh_out, dhb_out, h_mid, hn, gg, uu, act, layer, prev, ride=None, last=False):
        tag = str(layer)
        res = _ffn_down_bwd(dhb_out, wts["ffn_w_down"], gg, uu, layer, "ffn_down_bwd" + tag, ride=ride)
        (dg, du), rode = res if ride is not None else (res, None)
        d_down = _ffn_wgrad(act, [dhb_out], layer, layers, prev and prev[:1], True, "ffn_dwd" + tag)
        d_gu = _ffn_wgrad(hn, [dg, du], layer, layers, prev and prev[1:], False, "ffn_dwgu" + tag)
        grads = list(d_down) + list(d_gu)
        gs = rs_grads(ffn_names, grads) if last else None
        res = _ffn_up_bwd(dg, du, wts["ffn_w_gate"], wts["ffn_w_up"], layer, h_mid, wts["ffn_norm_w"][layer],
                          dh_out, "ffn_up_bwd" + tag, ride=_Ride("pair", gs) if last else None)
        (dh_mid, d_norm, dhb_mid), sib = res if last else (res, None)
        return dh_mid, dhb_mid, grads, d_norm, rode, gs, sib

    red = {}

    def rs_grads(names, grads):
        return [gr.reshape((N_SHARD,) + sh[n].shape) for n, gr in zip(names, grads)]

    def rs_partials(names, gs, sib):
        return [_rs_pair_add(gs[t], sib[t], idx, "rs_pair_add_" + n) for t, n in enumerate(names)]

    def rs_end(names, gs, sib, others, tag):
        mine = [_rs_final_add(gs[t], sib[t], others[t], idx, "rs_final_add_" + n) for t, n in enumerate(names)]
        red.update(zip(names, _rs_share(mine, "rs_share" + tag)))

    dh3, dh3b, ffn_grads, dfn1 = ffn_bwd(dh4, dh4b, h3, hn3, g1, u1, act1, 1, None)[:4]
    dy1 = _mm(dh3b, wts["dn_w_out"], mode="nt", name="dn_out_bwd")
    d_dn_out = _mm(y1, dh3b, mode="tn", name="dn_dwo")
    dproj1, dcw, dal, ddt, g["dn_norm_w"] = _dn_bwd(proj1, conv1, o1, dy1, st1, tinv, conv_w, alog, dtb,
                                                    wts["dn_norm_w"], seq)
    d_dn_in = _mm(hn2, dproj1, mode="tn", name="dn_dwi")
    d_dn_in = jnp.stack([d_dn_in[:, j * n_dn:(j + 1) * n_dn] for j in range(N_SHARD)])
    group1 = ["dn_w_out", "dn_w_in"]
    gs1 = rs_grads(group1, [d_dn_out, d_dn_in])
    (dh2, dmn1, dh2b), sib1 = _mm_nt_rms(dproj1, wts["dn_w_in"], h2, wts["mix_norm_w"][1], dh3, "dn_in_bwd",
                                         ride=_Ride("pair", gs1))
    g["dn_conv_w"] = dcw.reshape(CONV_K, DN_CONV_CH)
    g["dn_a_log"] = dal[0, DECAY_LANE:DECAY_LANE + DN_HEADS]
    g["dn_dt_bias"] = ddt[0, DECAY_LANE:DECAY_LANE + DN_HEADS]

    dh1, dh1b, _, dfn0, others1, gs2, sib2 = ffn_bwd(
        dh2, dh2b, h1, hn1, g0, u0, act0, 0, ffn_grads,
        ride=_Ride("chips", rs_partials(group1, gs1, sib1)), last=True)
    rs_end(group1, gs1, sib1, others1, "1")
    d_ret_out = _mm(y0, dh1b, mode="tn", name="ret_dwo")
    gs2b = rs_grads(["ret_w_out"], [d_ret_out])
    dy0, sib2b = _mm(dh1b, wts["ret_w_out"], mode="nt", name="ret_out_bwd", ride=_Ride("pair", gs2b))
    group2 = ffn_names + ["ret_w_out"]
    gs2, sib2 = gs2 + gs2b, list(sib2) + list(sib2b)
    (dproj0, g["ret_gn_w"]), others2 = _ret_bwd(proj0, o0, dy0, st0, cos, sin, consts, wts["ret_gn_w"], seq,
                                                ride=_Ride("chips", rs_partials(group2, gs2, sib2)))
    rs_end(group2, gs2, sib2, others2, "2")
    d_ret_in = _mm_cols_grad(hn0, dproj0, "ret_dwi")
    gs3 = rs_grads(["ret_w_in"], [d_ret_in])
    sib3 = _rs_pair(gs3, "rs_pair3")
    (dh0, dmn0, _), others3 = _mm_cols_t_rms(dproj0, wts["ret_w_in"], h0, wts["mix_norm_w"][0], dh1, "ret_in_bwd",
                                             ride=_Ride("chips", rs_partials(["ret_w_in"], gs3, sib3)))
    rs_end(["ret_w_in"], gs3, sib3, others3, "3")

    g["ffn_norm_w"] = jnp.concatenate([dfn0, dfn1], axis=0)
    g["mix_norm_w"] = jnp.concatenate([dmn0, dmn1], axis=0)
    g["meta_tokens"] = dh0[PAD:CHUNK]
    g["final_norm_w"] = g["final_norm_w"].reshape(D_MODEL)
    g["ret_gn_w"] = g["ret_gn_w"].reshape(RET_DV)
    g["dn_norm_w"] = g["dn_norm_w"].reshape(DN_DV)
    return loss, dh0, g, red


def _mesh_pos():
    return lax.axis_index("x"), lax.axis_index("y"), lax.axis_index("c")


def _other_chips(x, y):
    return [(1 - x, y), (x, 1 - y), (1 - x, 1 - y)]


def _remote(src, dst, send_sem, recv_sem, to):
    return pltpu.make_async_remote_copy(src_ref=src, dst_ref=dst, send_sem=send_sem, recv_sem=recv_sem,
                                        device_id=to, device_id_type=MESH)


GATHER_COPIES = 7


def _gather_phase(phase, ins, outs, send_sems, recv_sems):
    x, y, c = _mesh_pos()
    me = 2 * x + y
    chips = _other_chips(x, y)
    sibling = (x, y, 1 - c)

    def cp(t, k, src, dst, to):
        i = GATHER_COPIES * t + k
        return _remote(src, dst, send_sems.at[i], recv_sems.at[i], to)

    for t in range(len(ins)):
        own = cp(t, 0, ins[t], outs[t].at[me], sibling)
        if phase == 0:
            own.start()
        if phase == 2:
            own.wait()
        for k, (px, py) in enumerate(chips):
            landed = outs[t].at[2 * px + py, c]
            theirs = outs[t].at[2 * px + py, 1 - c]
            to_chip = cp(t, 1 + k, ins[t].at[c], outs[t].at[me, c], (px, py, c))
            if phase == 0:
                to_chip.start()
            if phase == 1:
                cp(t, 1 + k, ins[t].at[c], landed, (px, py, c)).wait_recv()
                cp(t, 4 + k, landed, landed, sibling).start()
            if phase == 2:
                to_chip.wait_send()
                cp(t, 4 + k, landed, landed, sibling).wait_send()
                cp(t, 4 + k, theirs, theirs, sibling).wait_recv()


def _chips_phase(phase, ins, outs, send_sems, recv_sems):
    x, y, c = _mesh_pos()
    for t in range(len(ins)):
        for k, (px, py) in enumerate(_other_chips(x, y)):
            cp = _remote(ins[t].at[2 * px + py], outs[t].at[k], send_sems.at[3 * t + k], recv_sems.at[3 * t + k],
                         (px, py, c))
            if phase == 0:
                cp.start()
            if phase == 2:
                cp.wait()


class _Ride:
    def __init__(self, kind, arrays):
        self.kind, self.arrays = kind, list(arrays)
        nt = len(self.arrays)
        if kind == "gather":
            self.phase_fn, n_sem = _gather_phase, GATHER_COPIES * nt
            self.out_shape = [jax.ShapeDtypeStruct((N_SHARD,) + a.shape, a.dtype) for a in self.arrays]
        elif kind == "pair":
            self.phase_fn, n_sem = _pair_phase, nt
            self.out_shape = [jax.ShapeDtypeStruct(a.shape[:1] + a.shape[2:], a.dtype) for a in self.arrays]
        else:
            self.phase_fn, n_sem = _chips_phase, 3 * nt
            self.out_shape = [jax.ShapeDtypeStruct((3,) + a.shape[1:], a.dtype) for a in self.arrays]
        self.in_specs, self.out_specs = [ANY] * nt, [ANY] * nt
        self.scratch = [pltpu.SemaphoreType.DMA((n_sem,)), pltpu.SemaphoreType.DMA((n_sem,))]

    def emit(self, step, nsteps, ins, outs, send_sems, recv_sems, before):
        mid = max(0, min((7 * nsteps) // 8, nsteps - 2))
        todo = [(0, 0), (1, mid)] if before else [(2, nsteps - 1)]
        for phase, at in todo:
            if phase == 1 and self.kind != "gather":
                continue

            @pl.when(step == at)
            def _(phase=phase):
                self.phase_fn(phase, ins, outs, send_sems, recv_sems)


def _gather_small(blk):
    r, wd = blk.shape

    def body(b_ref, out_ref, send_sems, recv_sems):
        x, y, c = _mesh_pos()
        chips = _other_chips(x, y)
        out_ref[2 * x + y] = b_ref[...]
        sends = [_remote(b_ref, out_ref.at[2 * x + y], send_sems.at[k], recv_sems.at[k], (px, py, c))
                 for k, (px, py) in enumerate(chips)]
        for cp in sends:
            cp.start()
        for k, (px, py) in enumerate(chips):
            _remote(b_ref, out_ref.at[2 * px + py], send_sems.at[k], recv_sems.at[k], (px, py, c)).wait_recv()
        for cp in sends:
            cp.wait_send()

    return pl.pallas_call(
        body, out_shape=jax.ShapeDtypeStruct((4, r, wd), blk.dtype), in_specs=[VMEM_SPEC], out_specs=VMEM_SPEC,
        scratch_shapes=[pltpu.SemaphoreType.DMA((3,)), pltpu.SemaphoreType.DMA((3,))],
        name="gather_small")(blk)


def _allreduce_small(blk):
    r, wd = blk.shape
    rels = [(dx, dy, dc) for dx in (0, 1) for dy in (0, 1) for dc in (0, 1) if dx or dy or dc]

    def body(b_ref, out_ref, buf_ref, send_sems, recv_sems):
        x, y, c = _mesh_pos()

        def peer(rel):
            dx, dy, dc = rel
            return (1 - x if dx else x, 1 - y if dy else y, 1 - c if dc else c)

        me = 4 * x + 2 * y + c
        buf_ref[me] = b_ref[...]
        sends = [_remote(b_ref, buf_ref.at[me], send_sems.at[k], recv_sems.at[k], peer(rel))
                 for k, rel in enumerate(rels)]
        for cp in sends:
            cp.start()
        for k, rel in enumerate(rels):
            px, py, pc = peer(rel)
            _remote(b_ref, buf_ref.at[4 * px + 2 * py + pc], send_sems.at[k], recv_sems.at[k],
                    (px, py, pc)).wait_recv()
        for cp in sends:
            cp.wait_send()
        acc = buf_ref[0]
        for d in range(1, 8):
            acc = acc + buf_ref[d]
        out_ref[...] = acc

    return pl.pallas_call(
        body, out_shape=jax.ShapeDtypeStruct((r, wd), blk.dtype), in_specs=[VMEM_SPEC], out_specs=VMEM_SPEC,
        scratch_shapes=[pltpu.VMEM((8, r, wd), blk.dtype), pltpu.SemaphoreType.DMA((7,)),
                        pltpu.SemaphoreType.DMA((7,))],
        name="allreduce_small")(blk)


def _rs_pair(gs, name):
    ride = _Ride("pair", gs)

    def body(*refs):
        nt = len(gs)
        for phase in (0, 2):
            _pair_phase(phase, refs[:nt], refs[nt:2 * nt], *refs[2 * nt:])

    return pl.pallas_call(body, out_shape=ride.out_shape, in_specs=ride.in_specs, out_specs=ride.out_specs,
                          scratch_shapes=ride.scratch, name=name)(*gs)


def _pair_phase(phase, ins, outs, send_sems, recv_sems):
    x, y, c = _mesh_pos()
    for t in range(len(ins)):
        cp = _remote(ins[t].at[:, 1 - c], outs[t], send_sems.at[t], recv_sems.at[t], (x, y, 1 - c))
        if phase == 0:
            cp.start()
        if phase == 2:
            cp.wait()


def _rs_tile(a, b):
    return _div_tile(a, 1024 if b <= 1024 else 512, 16)


def _rs_pair_add(g, a, idx, name):
    _, _, rows, cols = g.shape
    tr = _rs_tile(rows, cols)

    def body(s_ref, g_ref, a_ref, p_ref):
        p_ref[...] = (g_ref[...] + a_ref[...]).astype(p_ref.dtype)

    blk = pl.BlockSpec((None, tr, cols), lambda j, i, s: (j, i, 0))
    spec = pltpu.PrefetchScalarGridSpec(
        num_scalar_prefetch=1, grid=(N_SHARD, rows // tr),
        in_specs=[pl.BlockSpec((None, None, tr, cols), lambda j, i, s: (j, s[0], i, 0)), blk], out_specs=blk)
    return pl.pallas_call(
        body, grid_spec=spec, out_shape=jax.ShapeDtypeStruct((N_SHARD, rows, cols), BF16), name=name,
        compiler_params=_params("parallel", "parallel"))(idx, g, a)


def _rs_final_add(g, a, b, idx, name):
    _, _, rows, cols = g.shape
    tr = _rs_tile(rows, cols)

    def body(s_ref, g_ref, a_ref, b0_ref, b1_ref, b2_ref, f_ref):
        own = g_ref[...] + a_ref[...]
        f_ref[...] = ((own + b0_ref[...].astype(F32)) + b1_ref[...].astype(F32)) + b2_ref[...].astype(F32)

    def b_spec(k):
        return pl.BlockSpec((None, tr, cols), lambda i, s: (k, i, 0))

    spec = pltpu.PrefetchScalarGridSpec(
        num_scalar_prefetch=1, grid=(rows // tr,),
        in_specs=[pl.BlockSpec((None, None, tr, cols), lambda i, s: (s[1], s[0], i, 0)),
                  pl.BlockSpec((None, tr, cols), lambda i, s: (s[1], i, 0)), b_spec(0), b_spec(1), b_spec(2)],
        out_specs=pl.BlockSpec((None, tr, cols), lambda i, s: (s[0], i, 0)))
    return pl.pallas_call(
        body, grid_spec=spec, out_shape=jax.ShapeDtypeStruct((2, rows, cols), F32), name=name,
        compiler_params=_params("parallel"))(idx, g, a, b, b, b)


def _rs_share(fs, name):
    nt = len(fs)

    def body(*refs):
        outs = refs[nt:2 * nt]
        send_sems, recv_sems = refs[2 * nt:]
        x, y, c = _mesh_pos()
        cps = [_remote(outs[t].at[c], outs[t].at[c], send_sems.at[t], recv_sems.at[t], (x, y, 1 - c))
               for t in range(nt)]
        for cp in cps:
            cp.start()
        for cp in cps:
            cp.wait()

    return pl.pallas_call(
        body, out_shape=[jax.ShapeDtypeStruct(f.shape, f.dtype) for f in fs],
        in_specs=[ANY] * nt, out_specs=[ANY] * nt, input_output_aliases={t: t for t in range(nt)},
        scratch_shapes=[pltpu.SemaphoreType.DMA((nt,)), pltpu.SemaphoreType.DMA((nt,))], name=name)(*fs)


def _adamw(w, g, m, v, name):
    lead, rows, cols = w.shape
    tr = rows // 4 if rows % 32 == 0 else rows

    def body(w_ref, g_ref, m_ref, v_ref, go_ref, d_ref, mo_ref, vo_ref):
        gv = g_ref[...]
        go_ref[...] = gv
        mn = ADAM_B1 * m_ref[...] + (1.0 - ADAM_B1) * gv
        vn = ADAM_B2 * v_ref[...] + (1.0 - ADAM_B2) * (gv * gv)
        m_hat = mn / (1.0 - ADAM_B1 ** ADAM_STEP)
        v_hat = vn / (1.0 - ADAM_B2 ** ADAM_STEP)
        d_ref[...] = -ADAM_LR * (m_hat / (jnp.sqrt(v_hat) + ADAM_EPS) + ADAM_WD * w_ref[...])
        mo_ref[...] = mn
        vo_ref[...] = vn

    blk = pl.BlockSpec((None, tr, cols), lambda l, i: (l, i, 0))
    out = jax.ShapeDtypeStruct((lead, rows, cols), F32)
    return pl.pallas_call(
        body, grid=(lead, rows // tr), in_specs=[blk] * 4, out_specs=[blk] * 4, out_shape=[out] * 4, name=name,
        compiler_params=_params("parallel", "parallel"))(w, g, m, v)


BIG = ["ret_w_in", "ret_w_out", "dn_w_in", "dn_w_out", "ffn_w_gate", "ffn_w_up", "ffn_w_down"]
TRANSPOSED_AT_BOUNDARY = {"dn_w_in": True, "ffn_w_gate": False, "ffn_w_up": False}
SMALL =["meta_tokens", "mix_norm_w", "ffn_norm_w", "ret_gn_w", "dn_conv_w", "dn_a_log", "dn_dt_bias",
         "dn_norm_w", "final_norm_w"]
SMALL_SHARDED = {"meta_tokens", "dn_conv_w", "dn_norm_w"}
ORDER = ["meta_tokens", "mix_norm_w", "ffn_norm_w", "ret_w_in", "ret_gn_w", "ret_w_out", "dn_w_in",
         "dn_conv_w", "dn_a_log", "dn_dt_bias", "dn_norm_w", "dn_w_out", "ffn_w_gate", "ffn_w_up",
         "ffn_w_down", "final_norm_w"]


def _halves(a):
    return a.reshape(2, -1, a.shape[-1])


def _pack_lanes(parts, align=8):
    flat = jnp.concatenate([p.reshape(-1) for p in parts])
    flat = jnp.pad(flat, (0, -flat.shape[0] % (align * LANES)))
    return flat.reshape(-1, LANES)


def _unpack(buf, shapes):
    lead = buf.shape[:-2]
    flat = buf.reshape(lead + (-1,))
    out, off = [], 0
    for shp in shapes:
        size = math.prod(shp)
        out.append(flat[..., off:off + size].reshape(lead + tuple(shp)))
        off += size
    return out


def _join_cols(shards):
    return jnp.concatenate([shards[j] for j in range(N_SHARD)], axis=-1)


def kernel(x, meta_tokens, mix_norm_w, ffn_norm_w, ret_w_in, ret_gn_w, ret_w_out, dn_w_in, dn_conv_w, dn_a_log, dn_dt_bias, dn_norm_w, dn_w_out, ffn_w_gate, ffn_w_up, ffn_w_down, final_norm_w, loss_target, m_meta_tokens, m_mix_norm_w, m_ffn_norm_w, m_ret_w_in, m_ret_gn_w, m_ret_w_out, m_dn_w_in, m_dn_conv_w, m_dn_a_log, m_dn_dt_bias, m_dn_norm_w, m_dn_w_out, m_ffn_w_gate, m_ffn_w_up, m_ffn_w_down, m_final_norm_w, v_meta_tokens, v_mix_norm_w, v_ffn_norm_w, v_ret_w_in, v_ret_gn_w, v_ret_w_out, v_dn_w_in, v_dn_conv_w, v_dn_a_log, v_dn_dt_bias, v_dn_norm_w, v_dn_w_out, v_ffn_w_gate, v_ffn_w_up, v_ffn_w_down, v_final_norm_w):
    w = dict(meta_tokens=meta_tokens, mix_norm_w=mix_norm_w, ffn_norm_w=ffn_norm_w, ret_w_in=ret_w_in,
             ret_gn_w=ret_gn_w, ret_w_out=ret_w_out, dn_w_in=dn_w_in, dn_conv_w=dn_conv_w, dn_a_log=dn_a_log,
             dn_dt_bias=dn_dt_bias, dn_norm_w=dn_norm_w, dn_w_out=dn_w_out, ffn_w_gate=ffn_w_gate,
             ffn_w_up=ffn_w_up, ffn_w_down=ffn_w_down, final_norm_w=final_norm_w)
    m = dict(meta_tokens=m_meta_tokens, mix_norm_w=m_mix_norm_w, ffn_norm_w=m_ffn_norm_w, ret_w_in=m_ret_w_in,
             ret_gn_w=m_ret_gn_w, ret_w_out=m_ret_w_out, dn_w_in=m_dn_w_in, dn_conv_w=m_dn_conv_w,
             dn_a_log=m_dn_a_log, dn_dt_bias=m_dn_dt_bias, dn_norm_w=m_dn_norm_w, dn_w_out=m_dn_w_out,
             ffn_w_gate=m_ffn_w_gate, ffn_w_up=m_ffn_w_up, ffn_w_down=m_ffn_w_down, final_norm_w=m_final_norm_w)
    v = dict(meta_tokens=v_meta_tokens, mix_norm_w=v_mix_norm_w, ffn_norm_w=v_ffn_norm_w, ret_w_in=v_ret_w_in,
             ret_gn_w=v_ret_gn_w, ret_w_out=v_ret_w_out, dn_w_in=v_dn_w_in, dn_conv_w=v_dn_conv_w,
             dn_a_log=v_dn_a_log, dn_dt_bias=v_dn_dt_bias, dn_norm_w=v_dn_norm_w, dn_w_out=v_dn_w_out,
             ffn_w_gate=v_ffn_w_gate, ffn_w_up=v_ffn_w_up, ffn_w_down=v_ffn_w_down, final_norm_w=v_final_norm_w)
    mx, my, mc = _mesh_pos()
    chip = 2 * mx + my

    sm_names = [n for n in SMALL if n in SMALL_SHARDED]
    sm_gathered = _unpack(_gather_small(_pack_lanes([w[n] for n in sm_names])), [w[n].shape for n in sm_names])
    full = {n: _join_cols(sm_gathered[i]) for i, n in enumerate(sm_names)}
    wts = {
        "meta_tokens": full["meta_tokens"], "mix_norm_w": mix_norm_w, "ffn_norm_w": ffn_norm_w,
        "ret_gn_w": ret_gn_w[0], "final_norm_w": final_norm_w, "dn_conv_w": full["dn_conv_w"][0],
        "dn_a_log": dn_a_log[0], "dn_dt_bias": dn_dt_bias[0], "dn_norm_w": full["dn_norm_w"][0],
    }
    idx = jnp.stack([mc, chip]).astype(jnp.int32)
    shards = {n: _halves(w[n].astype(MXU_DTYPE)) for n in BIG}
    loss_part, dh0, g, reduced = _train_step(x[0], loss_target[0], wts, shards, idx)
    seq = x.shape[1]
    grad_x = dh0[CHUNK:CHUNK + seq].reshape(x.shape)
    gsh = {}

    small_full_shapes = [g[n].shape for n in SMALL] + [(1,)]
    red = _unpack(_allreduce_small(_pack_lanes([g[n] for n in SMALL] + [loss_part[0, :1]])), small_full_shapes)
    loss = red[-1][0]
    for i, n in enumerate(SMALL):
        gn = red[i]
        if n in SMALL_SHARDED:
            width = w[n].shape[-1]
            gn = lax.dynamic_slice_in_dim(gn, chip * width, width, axis=gn.ndim - 1)
        gsh[n] = gn.reshape(w[n].shape)

    delta, new_m, new_v = {}, {}, {}
    for n in BIG:
        shp = w[n].shape
        if n in TRANSPOSED_AT_BOUNDARY and TRANSPOSED_AT_BOUNDARY[n]:
            view = lambda a: jnp.swapaxes(a, 1, 2).reshape(1, -1, LANES)
            back = lambda a: jnp.swapaxes(a.reshape(shp[0], shp[2], shp[1]), 1, 2)
        elif n in TRANSPOSED_AT_BOUNDARY:
            view = back = lambda a: jnp.swapaxes(a, 1, 2)
        else:
            view = back = lambda a: a
        res = _adamw(view(w[n]), view(reduced[n].reshape(shp)), view(m[n]), view(v[n]), "adamw_" + n)
        gsh[n], delta[n], new_m[n], new_v[n] = [back(r) for r in res]
    sm_local_shapes = [w[n].shape for n in SMALL]
    _, d_, m_, v_ = _adamw(*[_pack_lanes([t[n] for n in SMALL])[None] for t in (w, gsh, m, v)], "adamw_small")
    d_, m_, v_ = d_[0], m_[0], v_[0]
    for n, dd, mm, vv in zip(SMALL, _unpack(d_, sm_local_shapes), _unpack(m_, sm_local_shapes),
                             _unpack(v_, sm_local_shapes)):
        delta[n], new_m[n], new_v[n] = dd, mm, vv

    return (loss, grad_x, *[gsh[n] for n in ORDER], *[delta[n] for n in ORDER],
            *[new_m[n] for n in ORDER], *[new_v[n] for n in ORDER])
```
